```python
import jax, jax.numpy as jnp
from jax import lax
import numpy as np

D_MODEL = 1024
BATCH = 16
SEQ = 2048
DEPTH = 2

CHUNK = 64
Q_BLOCK = 128
N_MIXERS = 2
N_RET_LAYERS = (DEPTH + 1) // 2
N_MLA_LAYERS = DEPTH // 2
RMS_EPS = 1e-6
ROPE_THETA = 10000.0

RET_HEADS = D_MODEL // 256
RET_QK_DIM = 256
RET_V_DIM = 2 * D_MODEL // RET_HEADS
RET_GAMMA_BASE = -5.0

MLA_HEADS = D_MODEL // 128
MLA_Q_RANK = 384
MLA_KV_RANK = D_MODEL // 4
MLA_NOPE_DIM = 128
MLA_ROPE_DIM = 64
MLA_V_DIM = 128
MLA_QK_DIM = MLA_NOPE_DIM + MLA_ROPE_DIM
MASK_VALUE = -1e30

FFN_DIM = 2816
CONV_WIDTH = 3

kernel_name = "hybrid_retention_mla_convffn_trunk"


def rms_norm(x, gain):
    xf = x.astype(jnp.float32)
    y = xf * lax.rsqrt(jnp.mean(xf * xf, axis=-1, keepdims=True) + RMS_EPS)
    return (y * gain.astype(jnp.float32)).astype(x.dtype)


def rope(x, pos):
    half = x.shape[-1] // 2
    inv_freq = ROPE_THETA ** (-jnp.arange(half, dtype=jnp.float32) / half)
    ang = pos.astype(jnp.float32)[:, None] * inv_freq[None, :]
    cos = jnp.cos(ang)[None, :, None, :]
    sin = jnp.sin(ang)[None, :, None, :]
    xf = x.astype(jnp.float32)
    x1, x2 = xf[..., :half], xf[..., half:]
    return jnp.concatenate([x1 * cos - x2 * sin, x2 * cos + x1 * sin], axis=-1).astype(x.dtype)


def retention_mixer(h, w_in, gn_gain, w_out):
    B, S, _ = h.shape
    H, dk, dv = RET_HEADS, RET_QK_DIM, RET_V_DIM
    proj = h @ w_in
    q, k, v, g = jnp.split(proj, [H * dk, 2 * H * dk, 2 * H * dk + H * dv], axis=-1)
    pos = jnp.arange(S)
    q = rope(q.reshape(B, S, H, dk), pos)
    k = rope(k.reshape(B, S, H, dk), pos) * (dk ** -0.5)
    v = v.reshape(B, S, H, dv)
    n_chunks = S // CHUNK

    def to_chunks(t):
        return t.reshape(B, n_chunks, CHUNK, H, t.shape[-1]).transpose(1, 0, 3, 2, 4)

    log_gamma = jnp.log1p(-jnp.exp2(RET_GAMMA_BASE - jnp.arange(H, dtype=jnp.float32)))
    idx = jnp.arange(CHUNK, dtype=jnp.float32)
    intra_decay = jnp.exp(log_gamma[:, None, None] * jnp.abs(idx[:, None] - idx[None, :]))
    q_decay = jnp.exp(log_gamma[:, None] * (idx + 1.0))[:, :, None]
    k_decay = jnp.exp(log_gamma[:, None] * (CHUNK - 1.0 - idx))[:, :, None]
    chunk_decay = jnp.exp(log_gamma * CHUNK)[:, None, None]

    def step(state, qkv):
        qc, kc, vc = qkv
        scores = jnp.einsum('bhid,bhjd->bhij', qc, kc) * intra_decay
        inner = jnp.einsum('bhij,bhjv->bhiv', scores, vc)
        cross = jnp.einsum('bhid,bhdv->bhiv', qc * q_decay, state)
        state = state * chunk_decay + jnp.einsum('bhjd,bhjv->bhdv', kc * k_decay, vc)
        return state, inner + cross

    state0 = jnp.zeros((B, H, dk, dv), jnp.float32)
    _, out = lax.scan(step, state0, (to_chunks(q), to_chunks(k), to_chunks(v)))
    out = out.transpose(1, 0, 3, 2, 4).reshape(B, S, H, dv)
    out = rms_norm(out, gn_gain).astype(h.dtype)
    out = out.reshape(B, S, H * dv) * jax.nn.silu(g)
    return out @ w_out


def mla_mixer(h, w_in, q_norm_g, w_qb, kv_norm_g, w_kvb, q_head_g, k_head_g, w_out):
    B, S, _ = h.shape
    H = MLA_HEADS
    proj = h @ w_in
    c_q, c_kv, k_rope = jnp.split(proj, [MLA_Q_RANK, MLA_Q_RANK + MLA_KV_RANK], axis=-1)
    q = (rms_norm(c_q, q_norm_g) @ w_qb).reshape(B, S, H, MLA_QK_DIM)
    kv = (rms_norm(c_kv, kv_norm_g) @ w_kvb).reshape(B, S, H, MLA_NOPE_DIM + MLA_V_DIM)
    k_nope, v = kv[..., :MLA_NOPE_DIM], kv[..., MLA_NOPE_DIM:]
    k_rope = jnp.broadcast_to(k_rope[:, :, None, :], (B, S, H, MLA_ROPE_DIM))
    k = jnp.concatenate([k_nope, k_rope], axis=-1)
    q = rms_norm(q, q_head_g)
    k = rms_norm(k, k_head_g)
    pos = jnp.arange(S)
    q = jnp.concatenate([q[..., :MLA_NOPE_DIM], rope(q[..., MLA_NOPE_DIM:], pos)], axis=-1)
    k = jnp.concatenate([k[..., :MLA_NOPE_DIM], rope(k[..., MLA_NOPE_DIM:], pos)], axis=-1)
    q = q * (MLA_QK_DIM ** -0.5)
    chunk_id = jnp.arange(S) // CHUNK
    outs = []
    for blk in range(S // Q_BLOCK):
        start, stop = blk * Q_BLOCK, (blk + 1) * Q_BLOCK
        logits = jnp.einsum('bqhd,bkhd->bhqk', q[:, start:stop], k[:, :stop]).astype(jnp.float32)
        mask = chunk_id[None, :stop] <= chunk_id[start:stop, None]
        logits = jnp.where(mask, logits, MASK_VALUE)
        p = jax.nn.softmax(logits, axis=-1).astype(v.dtype)
        outs.append(jnp.einsum('bhqk,bkhd->bqhd', p, v[:, :stop]))
    o = jnp.concatenate(outs, axis=1).reshape(B, S, H * MLA_V_DIM)
    return o @ w_out


def conv_ffn(h, w_in, conv_w, conv_b, w_out):
    a, g = jnp.split(h @ w_in, 2, axis=-1)
    g = lax.conv_general_dilated(
        g, conv_w[:, None, :], window_strides=(1,), padding=[(CONV_WIDTH - 1, 0)],
        dimension_numbers=('NWC', 'WIO', 'NWC'), feature_group_count=FFN_DIM) + conv_b
    return (jax.nn.silu(g) * a) @ w_out


def _fwd_setup_inputs(seed: int = 0) -> dict:
    key = jax.random.key(seed)
    ks = jax.random.split(key, 20)

    def dense(k, lead, fan_in, fan_out):
        return jax.random.normal(k, (lead, fan_in, fan_out), jnp.float32) * (fan_in ** -0.5)

    def gain(k, shape):
        return 1.0 + 0.01 * jax.random.normal(k, shape, jnp.float32)

    R, M, L = N_RET_LAYERS, N_MLA_LAYERS, DEPTH
    ret_in_width = 2 * RET_HEADS * RET_QK_DIM + 2 * RET_HEADS * RET_V_DIM
    mla_in_width = MLA_Q_RANK + MLA_KV_RANK + MLA_ROPE_DIM
    return {
        "x": jax.random.normal(ks[0], (BATCH, SEQ, D_MODEL), jnp.float32),
        "ret_norm": gain(ks[1], (R, D_MODEL)),
        "ret_w_in": dense(ks[2], R, D_MODEL, ret_in_width),
        "ret_gn": gain(ks[3], (R, RET_HEADS, RET_V_DIM)),
        "ret_w_out": dense(ks[4], R, RET_HEADS * RET_V_DIM, D_MODEL),
        "mla_norm": gain(ks[5], (M, D_MODEL)),
        "mla_w_in": dense(ks[6], M, D_MODEL, mla_in_width),
        "mla_q_norm": gain(ks[7], (M, MLA_Q_RANK)),
        "mla_w_qb": dense(ks[8], M, MLA_Q_RANK, MLA_HEADS * MLA_QK_DIM),
        "mla_kv_norm": gain(ks[9], (M, MLA_KV_RANK)),
        "mla_w_kvb": dense(ks[10], M, MLA_KV_RANK, MLA_HEADS * (MLA_NOPE_DIM + MLA_V_DIM)),
        "mla_q_head_norm": gain(ks[11], (M, MLA_QK_DIM)),
        "mla_k_head_norm": gain(ks[12], (M, MLA_QK_DIM)),
        "mla_w_out": dense(ks[13], M, MLA_HEADS * MLA_V_DIM, D_MODEL),
        "ffn_norm": gain(ks[14], (L, D_MODEL)),
        "ffn_w_in": dense(ks[15], L, D_MODEL, 2 * FFN_DIM),
        "ffn_conv_w": jax.random.normal(ks[16], (L, CONV_WIDTH, FFN_DIM), jnp.float32) * (CONV_WIDTH ** -0.5),
        "ffn_conv_b": 0.01 * jax.random.normal(ks[17], (L, FFN_DIM), jnp.float32),
        "ffn_w_out": dense(ks[18], L, FFN_DIM, D_MODEL),
    }


def _fwd_reference(x, ret_norm, ret_w_in, ret_gn, ret_w_out, mla_norm, mla_w_in, mla_q_norm, mla_w_qb,
              mla_kv_norm, mla_w_kvb, mla_q_head_norm, mla_k_head_norm, mla_w_out,
              ffn_norm, ffn_w_in, ffn_conv_w, ffn_conv_b, ffn_w_out):
    for i in range(DEPTH):
        j = i // N_MIXERS
        if i % N_MIXERS == 0:
            x = x + retention_mixer(rms_norm(x, ret_norm[j]), ret_w_in[j], ret_gn[j], ret_w_out[j])
        else:
            x = x + mla_mixer(rms_norm(x, mla_norm[j]), mla_w_in[j], mla_q_norm[j], mla_w_qb[j],
                              mla_kv_norm[j], mla_w_kvb[j], mla_q_head_norm[j], mla_k_head_norm[j],
                              mla_w_out[j])
        x = x + conv_ffn(rms_norm(x, ffn_norm[i]), ffn_w_in[i], ffn_conv_w[i], ffn_conv_b[i], ffn_w_out[i])
    return x


import jax as _jax
import jax.numpy as _jnp

TWIN_FORMAT = 'train_step'
FWD_PARAMS = ['x', 'ret_norm', 'ret_w_in', 'ret_gn', 'ret_w_out', 'mla_norm', 'mla_w_in', 'mla_q_norm', 'mla_w_qb', 'mla_kv_norm', 'mla_w_kvb', 'mla_q_head_norm', 'mla_k_head_norm', 'mla_w_out', 'ffn_norm', 'ffn_w_in', 'ffn_conv_w', 'ffn_conv_b', 'ffn_w_out']
TWIN_WEIGHTS = ['ret_norm', 'ret_w_in', 'ret_gn', 'ret_w_out', 'mla_norm', 'mla_w_in', 'mla_q_norm', 'mla_w_qb', 'mla_kv_norm', 'mla_w_kvb', 'mla_q_head_norm', 'mla_k_head_norm', 'mla_w_out', 'ffn_norm', 'ffn_w_in', 'ffn_conv_w', 'ffn_conv_b', 'ffn_w_out']
TWIN_DIFF_INPUT = 'x'
TWIN_INPUTS = ['x', 'ret_norm', 'ret_w_in', 'ret_gn', 'ret_w_out', 'mla_norm', 'mla_w_in', 'mla_q_norm', 'mla_w_qb', 'mla_kv_norm', 'mla_w_kvb', 'mla_q_head_norm', 'mla_k_head_norm', 'mla_w_out', 'ffn_norm', 'ffn_w_in', 'ffn_conv_w', 'ffn_conv_b', 'ffn_w_out', 'loss_target', 'm_ret_norm', 'm_ret_w_in', 'm_ret_gn', 'm_ret_w_out', 'm_mla_norm', 'm_mla_w_in', 'm_mla_q_norm', 'm_mla_w_qb', 'm_mla_kv_norm', 'm_mla_w_kvb', 'm_mla_q_head_norm', 'm_mla_k_head_norm', 'm_mla_w_out', 'm_ffn_norm', 'm_ffn_w_in', 'm_ffn_conv_w', 'm_ffn_conv_b', 'm_ffn_w_out', 'v_ret_norm', 'v_ret_w_in', 'v_ret_gn', 'v_ret_w_out', 'v_mla_norm', 'v_mla_w_in', 'v_mla_q_norm', 'v_mla_w_qb', 'v_mla_kv_norm', 'v_mla_w_kvb', 'v_mla_q_head_norm', 'v_mla_k_head_norm', 'v_mla_w_out', 'v_ffn_norm', 'v_ffn_w_in', 'v_ffn_conv_w', 'v_ffn_conv_b', 'v_ffn_w_out']
TWIN_OUTPUTS = ['loss', 'grad_x', 'grad_ret_norm', 'grad_ret_w_in', 'grad_ret_gn', 'grad_ret_w_out', 'grad_mla_norm', 'grad_mla_w_in', 'grad_mla_q_norm', 'grad_mla_w_qb', 'grad_mla_kv_norm', 'grad_mla_w_kvb', 'grad_mla_q_head_norm', 'grad_mla_k_head_norm', 'grad_mla_w_out', 'grad_ffn_norm', 'grad_ffn_w_in', 'grad_ffn_conv_w', 'grad_ffn_conv_b', 'grad_ffn_w_out', 'delta_ret_norm', 'delta_ret_w_in', 'delta_ret_gn', 'delta_ret_w_out', 'delta_mla_norm', 'delta_mla_w_in', 'delta_mla_q_norm', 'delta_mla_w_qb', 'delta_mla_kv_norm', 'delta_mla_w_kvb', 'delta_mla_q_head_norm', 'delta_mla_k_head_norm', 'delta_mla_w_out', 'delta_ffn_norm', 'delta_ffn_w_in', 'delta_ffn_conv_w', 'delta_ffn_conv_b', 'delta_ffn_w_out', 'new_m_ret_norm', 'new_m_ret_w_in', 'new_m_ret_gn', 'new_m_ret_w_out', 'new_m_mla_norm', 'new_m_mla_w_in', 'new_m_mla_q_norm', 'new_m_mla_w_qb', 'new_m_mla_kv_norm', 'new_m_mla_w_kvb', 'new_m_mla_q_head_norm', 'new_m_mla_k_head_norm', 'new_m_mla_w_out', 'new_m_ffn_norm', 'new_m_ffn_w_in', 'new_m_ffn_conv_w', 'new_m_ffn_conv_b', 'new_m_ffn_w_out', 'new_v_ret_norm', 'new_v_ret_w_in', 'new_v_ret_gn', 'new_v_ret_w_out', 'new_v_mla_norm', 'new_v_mla_w_in', 'new_v_mla_q_norm', 'new_v_mla_w_qb', 'new_v_mla_kv_norm', 'new_v_mla_w_kvb', 'new_v_mla_q_head_norm', 'new_v_mla_k_head_norm', 'new_v_mla_w_out', 'new_v_ffn_norm', 'new_v_ffn_w_in', 'new_v_ffn_conv_w', 'new_v_ffn_conv_b', 'new_v_ffn_w_out']
TWIN_LEAF_KINDS = {'loss': 'loss', 'grad_x': 'grad_x', 'grad_ret_norm': 'grad_w', 'grad_ret_w_in': 'grad_w', 'grad_ret_gn': 'grad_w', 'grad_ret_w_out': 'grad_w', 'grad_mla_norm': 'grad_w', 'grad_mla_w_in': 'grad_w', 'grad_mla_q_norm': 'grad_w', 'grad_mla_w_qb': 'grad_w', 'grad_mla_kv_norm': 'grad_w', 'grad_mla_w_kvb': 'grad_w', 'grad_mla_q_head_norm': 'grad_w', 'grad_mla_k_head_norm': 'grad_w', 'grad_mla_w_out': 'grad_w', 'grad_ffn_norm': 'grad_w', 'grad_ffn_w_in': 'grad_w', 'grad_ffn_conv_w': 'grad_w', 'grad_ffn_conv_b': 'grad_w', 'grad_ffn_w_out': 'grad_w', 'delta_ret_norm': 'delta_w', 'delta_ret_w_in': 'delta_w', 'delta_ret_gn': 'delta_w', 'delta_ret_w_out': 'delta_w', 'delta_mla_norm': 'delta_w', 'delta_mla_w_in': 'delta_w', 'delta_mla_q_norm': 'delta_w', 'delta_mla_w_qb': 'delta_w', 'delta_mla_kv_norm': 'delta_w', 'delta_mla_w_kvb': 'delta_w', 'delta_mla_q_head_norm': 'delta_w', 'delta_mla_k_head_norm': 'delta_w', 'delta_mla_w_out': 'delta_w', 'delta_ffn_norm': 'delta_w', 'delta_ffn_w_in': 'delta_w', 'delta_ffn_conv_w': 'delta_w', 'delta_ffn_conv_b': 'delta_w', 'delta_ffn_w_out': 'delta_w', 'new_m_ret_norm': 'new_m', 'new_m_ret_w_in': 'new_m', 'new_m_ret_gn': 'new_m', 'new_m_ret_w_out': 'new_m', 'new_m_mla_norm': 'new_m', 'new_m_mla_w_in': 'new_m', 'new_m_mla_q_norm': 'new_m', 'new_m_mla_w_qb': 'new_m', 'new_m_mla_kv_norm': 'new_m', 'new_m_mla_w_kvb': 'new_m', 'new_m_mla_q_head_norm': 'new_m', 'new_m_mla_k_head_norm': 'new_m', 'new_m_mla_w_out': 'new_m', 'new_m_ffn_norm': 'new_m', 'new_m_ffn_w_in': 'new_m', 'new_m_ffn_conv_w': 'new_m', 'new_m_ffn_conv_b': 'new_m', 'new_m_ffn_w_out': 'new_m', 'new_v_ret_norm': 'new_v', 'new_v_ret_w_in': 'new_v', 'new_v_ret_gn': 'new_v', 'new_v_ret_w_out': 'new_v', 'new_v_mla_norm': 'new_v', 'new_v_mla_w_in': 'new_v', 'new_v_mla_q_norm': 'new_v', 'new_v_mla_w_qb': 'new_v', 'new_v_mla_kv_norm': 'new_v', 'new_v_mla_w_kvb': 'new_v', 'new_v_mla_q_head_norm': 'new_v', 'new_v_mla_k_head_norm': 'new_v', 'new_v_mla_w_out': 'new_v', 'new_v_ffn_norm': 'new_v', 'new_v_ffn_w_in': 'new_v', 'new_v_ffn_conv_w': 'new_v', 'new_v_ffn_conv_b': 'new_v', 'new_v_ffn_w_out': 'new_v'}


def _forward(args):
    return _fwd_reference(*[args[k] for k in FWD_PARAMS])


def _output_shape():
    out = _jax.eval_shape(lambda: _forward(_fwd_setup_inputs(0)))
    return out.shape, out.dtype

N_MICROBATCH = 1
ADAM_LR = 0.001
ADAM_B1 = 0.9
ADAM_B2 = 0.999
ADAM_EPS = 1e-08
ADAM_WD = 0.01
ADAM_STEP = 10
PER_EXAMPLE_BATCH_AXIS = {'x': 0, 'loss_target': 0}
SHARED_INPUTS = []
_WEIGHT_DTYPES = {'ret_norm': _jnp.float32, 'ret_w_in': _jnp.float32, 'ret_gn': _jnp.float32, 'ret_w_out': _jnp.float32, 'mla_norm': _jnp.float32, 'mla_w_in': _jnp.float32, 'mla_q_norm': _jnp.float32, 'mla_w_qb': _jnp.float32, 'mla_kv_norm': _jnp.float32, 'mla_w_kvb': _jnp.float32, 'mla_q_head_norm': _jnp.float32, 'mla_k_head_norm': _jnp.float32, 'mla_w_out': _jnp.float32, 'ffn_norm': _jnp.float32, 'ffn_w_in': _jnp.float32, 'ffn_conv_w': _jnp.float32, 'ffn_conv_b': _jnp.float32, 'ffn_w_out': _jnp.float32}
MOMENT_SCALE = {'ret_norm': 1.284926e+01, 'ret_w_in': 2.938788e-01, 'ret_gn': 5.541192e+00, 'ret_w_out': 3.641178e-01, 'mla_norm': 1.262099e-01, 'mla_w_in': 1.482710e-01, 'mla_q_norm': 9.936192e-02, 'mla_w_qb': 5.322180e-02, 'mla_kv_norm': 6.471180e-01, 'mla_w_kvb': 6.837166e-02, 'mla_q_head_norm': 7.245298e-01, 'mla_k_head_norm': 7.241481e-01, 'mla_w_out': 7.781665e-02, 'ffn_norm': 2.571697e+01, 'ffn_w_in': 2.161220e-01, 'ffn_conv_w': 2.802393e+00, 'ffn_conv_b': 3.417571e+00, 'ffn_w_out': 3.415071e-01}


def _to_microbatches(a, axis):
    t = _jnp.moveaxis(a, axis, 0)
    t = t.reshape((N_MICROBATCH, t.shape[0] // N_MICROBATCH) + t.shape[1:])
    return _jnp.moveaxis(t, 1, axis + 1)


def setup_inputs(seed: int = 0) -> dict:
    inp = _fwd_setup_inputs(seed)
    key = _jax.random.fold_in(_jax.random.key(seed), 7919)
    shape, _ = _output_shape()
    out = dict(inp)
    out["loss_target"] = _jax.random.normal(_jax.random.fold_in(key, 0), shape, _jnp.float32)
    for i, name in enumerate(TWIN_WEIGHTS):
        w = inp[name].astype(_jnp.float32)
        if MOMENT_SCALE is None:
            s = _jnp.sqrt(_jnp.mean(_jnp.square(w)) + 1e-30)
        else:
            s = MOMENT_SCALE[name]
        km, kv = _jax.random.split(_jax.random.fold_in(key, i + 1))
        out[name] = w
        out["m_" + name] = s * _jax.random.normal(km, w.shape, _jnp.float32)
        out["v_" + name] = (s * s) * _jax.random.uniform(kv, w.shape, _jnp.float32, 0.5, 1.5)
    if N_MICROBATCH > 1:
        for name, axis in PER_EXAMPLE_BATCH_AXIS.items():
            out[name] = _to_microbatches(out[name], axis)
    return {'x': out['x'], 'ret_norm': out['ret_norm'], 'ret_w_in': out['ret_w_in'], 'ret_gn': out['ret_gn'], 'ret_w_out': out['ret_w_out'], 'mla_norm': out['mla_norm'], 'mla_w_in': out['mla_w_in'], 'mla_q_norm': out['mla_q_norm'], 'mla_w_qb': out['mla_w_qb'], 'mla_kv_norm': out['mla_kv_norm'], 'mla_w_kvb': out['mla_w_kvb'], 'mla_q_head_norm': out['mla_q_head_norm'], 'mla_k_head_norm': out['mla_k_head_norm'], 'mla_w_out': out['mla_w_out'], 'ffn_norm': out['ffn_norm'], 'ffn_w_in': out['ffn_w_in'], 'ffn_conv_w': out['ffn_conv_w'], 'ffn_conv_b': out['ffn_conv_b'], 'ffn_w_out': out['ffn_w_out'], 'loss_target': out['loss_target'], 'm_ret_norm': out['m_ret_norm'], 'm_ret_w_in': out['m_ret_w_in'], 'm_ret_gn': out['m_ret_gn'], 'm_ret_w_out': out['m_ret_w_out'], 'm_mla_norm': out['m_mla_norm'], 'm_mla_w_in': out['m_mla_w_in'], 'm_mla_q_norm': out['m_mla_q_norm'], 'm_mla_w_qb': out['m_mla_w_qb'], 'm_mla_kv_norm': out['m_mla_kv_norm'], 'm_mla_w_kvb': out['m_mla_w_kvb'], 'm_mla_q_head_norm': out['m_mla_q_head_norm'], 'm_mla_k_head_norm': out['m_mla_k_head_norm'], 'm_mla_w_out': out['m_mla_w_out'], 'm_ffn_norm': out['m_ffn_norm'], 'm_ffn_w_in': out['m_ffn_w_in'], 'm_ffn_conv_w': out['m_ffn_conv_w'], 'm_ffn_conv_b': out['m_ffn_conv_b'], 'm_ffn_w_out': out['m_ffn_w_out'], 'v_ret_norm': out['v_ret_norm'], 'v_ret_w_in': out['v_ret_w_in'], 'v_ret_gn': out['v_ret_gn'], 'v_ret_w_out': out['v_ret_w_out'], 'v_mla_norm': out['v_mla_norm'], 'v_mla_w_in': out['v_mla_w_in'], 'v_mla_q_norm': out['v_mla_q_norm'], 'v_mla_w_qb': out['v_mla_w_qb'], 'v_mla_kv_norm': out['v_mla_kv_norm'], 'v_mla_w_kvb': out['v_mla_w_kvb'], 'v_mla_q_head_norm': out['v_mla_q_head_norm'], 'v_mla_k_head_norm': out['v_mla_k_head_norm'], 'v_mla_w_out': out['v_mla_w_out'], 'v_ffn_norm': out['v_ffn_norm'], 'v_ffn_w_in': out['v_ffn_w_in'], 'v_ffn_conv_w': out['v_ffn_conv_w'], 'v_ffn_conv_b': out['v_ffn_conv_b'], 'v_ffn_w_out': out['v_ffn_w_out']}


def _loss(weights, diff, rest, loss_target):
    with _jax.named_scope("forward"):
        args = {**rest, TWIN_DIFF_INPUT: diff, **{k: w.astype(_WEIGHT_DTYPES[k]) for k, w in weights.items()}}
        y = _forward(args)
    with _jax.named_scope("loss_head"):
        err = _jnp.square(y.astype(_jnp.float32) - loss_target)
        return 0.5 * _jnp.sum(_jnp.mean(err, axis=-1)) if err.ndim else 0.5 * err


def _adamw(w, g, m, v):
    m = ADAM_B1 * m + (1.0 - ADAM_B1) * g
    v = ADAM_B2 * v + (1.0 - ADAM_B2) * _jnp.square(g)
    m_hat = m / (1.0 - ADAM_B1 ** ADAM_STEP)
    v_hat = v / (1.0 - ADAM_B2 ** ADAM_STEP)
    delta = -ADAM_LR * (m_hat / (_jnp.sqrt(v_hat) + ADAM_EPS) + ADAM_WD * w)
    return delta, m, v


def reference(x, ret_norm, ret_w_in, ret_gn, ret_w_out, mla_norm, mla_w_in, mla_q_norm, mla_w_qb, mla_kv_norm, mla_w_kvb, mla_q_head_norm, mla_k_head_norm, mla_w_out, ffn_norm, ffn_w_in, ffn_conv_w, ffn_conv_b, ffn_w_out, loss_target, m_ret_norm, m_ret_w_in, m_ret_gn, m_ret_w_out, m_mla_norm, m_mla_w_in, m_mla_q_norm, m_mla_w_qb, m_mla_kv_norm, m_mla_w_kvb, m_mla_q_head_norm, m_mla_k_head_norm, m_mla_w_out, m_ffn_norm, m_ffn_w_in, m_ffn_conv_w, m_ffn_conv_b, m_ffn_w_out, v_ret_norm, v_ret_w_in, v_ret_gn, v_ret_w_out, v_mla_norm, v_mla_w_in, v_mla_q_norm, v_mla_w_qb, v_mla_kv_norm, v_mla_w_kvb, v_mla_q_head_norm, v_mla_k_head_norm, v_mla_w_out, v_ffn_norm, v_ffn_w_in, v_ffn_conv_w, v_ffn_conv_b, v_ffn_w_out):
    given = dict(x=x, ret_norm=ret_norm, ret_w_in=ret_w_in, ret_gn=ret_gn, ret_w_out=ret_w_out, mla_norm=mla_norm, mla_w_in=mla_w_in, mla_q_norm=mla_q_norm, mla_w_qb=mla_w_qb, mla_kv_norm=mla_kv_norm, mla_w_kvb=mla_w_kvb, mla_q_head_norm=mla_q_head_norm, mla_k_head_norm=mla_k_head_norm, mla_w_out=mla_w_out, ffn_norm=ffn_norm, ffn_w_in=ffn_w_in, ffn_conv_w=ffn_conv_w, ffn_conv_b=ffn_conv_b, ffn_w_out=ffn_w_out, loss_target=loss_target, m_ret_norm=m_ret_norm, m_ret_w_in=m_ret_w_in, m_ret_gn=m_ret_gn, m_ret_w_out=m_ret_w_out, m_mla_norm=m_mla_norm, m_mla_w_in=m_mla_w_in, m_mla_q_norm=m_mla_q_norm, m_mla_w_qb=m_mla_w_qb, m_mla_kv_norm=m_mla_kv_norm, m_mla_w_kvb=m_mla_w_kvb, m_mla_q_head_norm=m_mla_q_head_norm, m_mla_k_head_norm=m_mla_k_head_norm, m_mla_w_out=m_mla_w_out, m_ffn_norm=m_ffn_norm, m_ffn_w_in=m_ffn_w_in, m_ffn_conv_w=m_ffn_conv_w, m_ffn_conv_b=m_ffn_conv_b, m_ffn_w_out=m_ffn_w_out, v_ret_norm=v_ret_norm, v_ret_w_in=v_ret_w_in, v_ret_gn=v_ret_gn, v_ret_w_out=v_ret_w_out, v_mla_norm=v_mla_norm, v_mla_w_in=v_mla_w_in, v_mla_q_norm=v_mla_q_norm, v_mla_w_qb=v_mla_w_qb, v_mla_kv_norm=v_mla_kv_norm, v_mla_w_kvb=v_mla_w_kvb, v_mla_q_head_norm=v_mla_q_head_norm, v_mla_k_head_norm=v_mla_k_head_norm, v_mla_w_out=v_mla_w_out, v_ffn_norm=v_ffn_norm, v_ffn_w_in=v_ffn_w_in, v_ffn_conv_w=v_ffn_conv_w, v_ffn_conv_b=v_ffn_conv_b, v_ffn_w_out=v_ffn_w_out)
    weights = {n: given[n] for n in TWIN_WEIGHTS}
    shared = {n: given[n] for n in SHARED_INPUTS}
    per_example = {n: given[n] for n in ['x']}
    grad_fn = _jax.value_and_grad(_loss, argnums=(0, 1))

    def one_microbatch(ex, loss_target):
        ex = dict(ex)
        diff = ex.pop(TWIN_DIFF_INPUT)
        return grad_fn(weights, diff, {**shared, **ex}, loss_target)

    if N_MICROBATCH == 1:
        loss, (grad_w, grad_x) = one_microbatch(per_example, given["loss_target"])
    else:
        def body(carry, xs):
            loss_sum, grad_sum = carry
            l_k, (gw_k, gx_k) = one_microbatch(xs[0], xs[1])
            with _jax.named_scope("update"):
                return (loss_sum + l_k, _jax.tree.map(_jnp.add, grad_sum, gw_k)), gx_k

        init = (_jnp.zeros((), _jnp.float32), _jax.tree.map(_jnp.zeros_like, weights))
        (loss, grad_w), grad_x = _jax.lax.scan(body, init, (per_example, given["loss_target"]))
    with _jax.named_scope("update"):
        delta_w, new_m, new_v = {}, {}, {}
        for n in TWIN_WEIGHTS:
            delta_w[n], new_m[n], new_v[n] = _adamw(weights[n], grad_w[n], given["m_" + n], given["v_" + n])
    return (loss, grad_x, *[grad_w[n] for n in TWIN_WEIGHTS], *[delta_w[n] for n in TWIN_WEIGHTS],
            *[new_m[n] for n in TWIN_WEIGHTS], *[new_v[n] for n in TWIN_WEIGHTS])
```

```python
import functools
import math

import numpy as np
import jax
import jax.numpy as jnp
from jax import lax
from jax.experimental import pallas as pl
from jax.experimental.pallas import tpu as pltpu

F32 = jnp.float32
BF16 = jnp.bfloat16
MXU_DTYPE = jnp.bfloat16

CHUNK = 64
RMS_EPS = 1e-6
ROPE_THETA = 10000.0
D_MODEL = 1024
RET_HEADS = 4
RET_QK = 256
RET_V = 512
RET_GAMMA_BASE = -5.0
MLA_HEADS = 8
MLA_Q_RANK = 384
MLA_KV_RANK = 256
MLA_NOPE = 128
MLA_ROPE = 64
MLA_V = 128
MLA_QK = MLA_NOPE + MLA_ROPE
MLA_PAD = 256
MLA_IN = MLA_Q_RANK + MLA_KV_RANK + MLA_ROPE
MLA_IN_PAD = MLA_IN + 64
MASK_VALUE = -1e30
FFN_DIM = 2816
ADAM_LR = 0.001
ADAM_B1 = 0.9
ADAM_B2 = 0.999
ADAM_EPS = 1e-08
ADAM_WD = 0.01
ADAM_STEP = 10

LANES = 128
ATT_BLOCK = 256
VMEM_LIMIT = 56 * 2 ** 20
N_SHARD = 4
N_DEV = 8

MESH = pl.DeviceIdType.MESH


def _params(sem=None, **kw):
    return pltpu.CompilerParams(dimension_semantics=sem, vmem_limit_bytes=VMEM_LIMIT, **kw)


def _pick(dim, target):
    if dim <= target:
        return dim
    best = None
    for d in range(LANES, target + 1, LANES):
        if dim % d == 0:
            best = d
    assert best is not None, (dim, target)
    return best


def _mm(a, b, dims, out_dtype, name, residual=None, bm=512, bn=1024, bk=2048):
    if dims == "nn":
        (M, K), (K2, N) = a.shape, b.shape
    elif dims == "nt":
        (M, K), (N, K2) = a.shape, b.shape
    else:
        (K, M), (K2, N) = a.shape, b.shape
    assert K == K2, (name, a.shape, b.shape)
    bm, bn, bk = _pick(M, bm), _pick(N, bn), _pick(K, min(bk, 1024) if dims == "tn" else bk)
    nk = K // bk
    if dims == "tn":
        a_spec = pl.BlockSpec((bk, bm), lambda i, j, k: (k, i))
        dn = (((0,), (0,)), ((), ()))
    else:
        a_spec = pl.BlockSpec((bm, bk), lambda i, j, k: (i, k))
        dn = (((1,), (1 if dims == "nt" else 0,)), ((), ()))
    if dims == "nt":
        b_spec = pl.BlockSpec((bn, bk), lambda i, j, k: (j, k))
    else:
        b_spec = pl.BlockSpec((bk, bn), lambda i, j, k: (k, j))
    o_spec = pl.BlockSpec((bm, bn), lambda i, j, k: (i, j))
    has_res = residual is not None

    def body(*refs):
        a_ref, b_ref = refs[0], refs[1]
        r_ref = refs[2] if has_res else None
        o_ref = refs[2 + has_res]
        p = lax.dot_general(a_ref[...].astype(MXU_DTYPE), b_ref[...].astype(MXU_DTYPE), dn,
                            preferred_element_type=F32)

        def finish(acc):
            if has_res:
                acc = acc + r_ref[...].astype(F32)
            o_ref[...] = acc.astype(out_dtype)

        if nk == 1:
            finish(p)
        else:
            acc_ref = refs[3 + has_res]
            k = pl.program_id(2)

            @pl.when(k == 0)
            def _():
                acc_ref[...] = p

            @pl.when(jnp.logical_and(k > 0, k < nk - 1))
            def _():
                acc_ref[...] += p

            @pl.when(k == nk - 1)
            def _():
                finish(acc_ref[...] + p)

    in_specs = [a_spec, b_spec] + ([o_spec] if has_res else [])
    args = (a, b) + ((residual,) if has_res else ())
    return pl.pallas_call(
        body, name=name, grid=(M // bm, N // bn, nk),
        in_specs=in_specs, out_specs=o_spec,
        out_shape=jax.ShapeDtypeStruct((M, N), out_dtype),
        scratch_shapes=[pltpu.VMEM((bm, bn), F32)] if nk > 1 else [],
        compiler_params=_params(("parallel", "parallel", "arbitrary")),
    )(*args)


def _tiles(ref, width, tile):
    return [ref[:, t * tile:(t + 1) * tile].astype(F32) for t in range(width // tile)]


def _row_specs(rows, pos, consts, bm, S):
    npos_blocks = S // bm
    specs = [pl.BlockSpec((bm, w), functools.partial(lambda i, c: (i, c), c=cb)) for (_, w, cb, _) in rows]
    specs += [pl.BlockSpec((bm, p.shape[1]), lambda i: (i % npos_blocks, 0)) for p in pos]
    specs += [pl.BlockSpec(c.shape, lambda i: (0, 0)) for (c, _) in consts]
    return specs


def _rowwise_fwd(fn, name, rows, pos, consts, outs, bm, S):
    T = rows[0][0].shape[0]
    nr, npos, nc = len(rows), len(pos), len(consts)

    def body(*refs):
        row_v = [_tiles(r, w, t) for r, (_, w, _, t) in zip(refs[:nr], rows)]
        pos_v = [r[...] for r in refs[nr:nr + npos]]
        const_v = [_tiles(r, c.shape[1], t) for r, (c, t) in zip(refs[nr + npos:nr + npos + nc], consts)]
        res = fn(row_v, pos_v, const_v)
        for o_ref, tiles, (w, t, dt) in zip(refs[nr + npos + nc:], res, outs):
            for k, v in enumerate(tiles):
                o_ref[:, k * t:(k + 1) * t] = v.astype(dt)

    return pl.pallas_call(
        body, name=name, grid=(T // bm,),
        in_specs=_row_specs(rows, pos, consts, bm, S),
        out_specs=[pl.BlockSpec((bm, w), lambda i: (i, 0)) for (w, _, _) in outs],
        out_shape=[jax.ShapeDtypeStruct((T, w), dt) for (w, _, dt) in outs],
        compiler_params=_params(("parallel",)),
    )(*[r[0] for r in rows], *pos, *[c[0] for c in consts])


def _rowwise_bwd(fn, name, rows, pos, consts, cts, bm, S, adds=None):
    adds = adds or {}
    T = rows[0][0].shape[0]
    nr, npos, nc, nct = len(rows), len(pos), len(consts), len(cts)
    add_idx = sorted(adds)

    def body(*refs):
        it = iter(refs)
        row_refs = [next(it) for _ in range(nr)]
        pos_refs = [next(it) for _ in range(npos)]
        const_refs = [next(it) for _ in range(nc)]
        ct_refs = [next(it) for _ in range(nct)]
        add_refs = {k: next(it) for k in add_idx}
        drow_refs = [next(it) for _ in range(nr)]
        dconst_refs = [next(it) for _ in range(nc)]
        row_v = [_tiles(r, w, t) for r, (_, w, _, t) in zip(row_refs, rows)]
        pos_v = [r[...] for r in pos_refs]
        const_v = [_tiles(r, c.shape[1], t) for r, (c, t) in zip(const_refs, consts)]
        ct_v = [_tiles(r, c.shape[1], t) for r, (c, t) in zip(ct_refs, cts)]
        _, vjp = jax.vjp(lambda rv, cv: fn(rv, pos_v, cv), row_v, const_v)
        drows, dconsts = vjp(ct_v)
        for a, (d_ref, tiles, (_, w, _, t)) in enumerate(zip(drow_refs, drows, rows)):
            for k, v in enumerate(tiles):
                if a in add_refs:
                    v = v + add_refs[a][:, k * t:(k + 1) * t].astype(F32)
                d_ref[:, k * t:(k + 1) * t] = v
        first = pl.program_id(0) == 0
        for d_ref, tiles, (_, t) in zip(dconst_refs, dconsts, consts):
            for k, v in enumerate(tiles):
                @pl.when(first)
                def _(d_ref=d_ref, k=k, t=t, v=v):
                    d_ref[:, k * t:(k + 1) * t] = v

                @pl.when(jnp.logical_not(first))
                def _(d_ref=d_ref, k=k, t=t, v=v):
                    d_ref[:, k * t:(k + 1) * t] += v

    in_specs = _row_specs(rows, pos, consts, bm, S)
    in_specs += [pl.BlockSpec((bm, c.shape[1]), lambda i: (i, 0)) for (c, _) in cts]
    in_specs += [pl.BlockSpec((bm, adds[k].shape[1]), lambda i: (i, 0)) for k in add_idx]
    out_specs = [pl.BlockSpec((bm, w), lambda i: (i, 0)) for (_, w, _, _) in rows]
    out_specs += [pl.BlockSpec(c.shape, lambda i: (0, 0)) for (c, _) in consts]
    out_shape = [jax.ShapeDtypeStruct((T, w), F32) for (_, w, _, _) in rows]
    out_shape += [jax.ShapeDtypeStruct(c.shape, F32) for (c, _) in consts]
    res = pl.pallas_call(
        body, name=name, grid=(T // bm,),
        in_specs=in_specs, out_specs=out_specs, out_shape=out_shape,
        compiler_params=_params(("arbitrary",)),
    )(*[r[0] for r in rows], *pos, *[c[0] for c in consts], *[c[0] for c in cts], *[adds[k] for k in add_idx])
    return res[:nr], res[nr:]


def _ssq(tiles):
    s = jnp.sum(tiles[0] * tiles[0], axis=-1, keepdims=True)
    for t in tiles[1:]:
        s = s + jnp.sum(t * t, axis=-1, keepdims=True)
    return s


def _sigmoid(x):
    return 1.0 / (1.0 + jnp.exp(-x))


def _fn_rms(rows, pos, consts):
    (x,), (g,) = rows[0], consts[0]
    r = lax.rsqrt(jnp.mean(x * x, axis=-1, keepdims=True) + RMS_EPS)
    return [[x * r * g]]


def _fn_ret_rope(rows, pos, consts):
    q, k, v = rows
    cos, sin = pos

    def rot(t, scale):
        out = []
        for h in range(RET_HEADS):
            x1, x2 = t[2 * h], t[2 * h + 1]
            o1, o2 = x1 * cos - x2 * sin, x2 * cos + x1 * sin
            out += [o1, o2] if scale is None else [o1 * scale, o2 * scale]
        return out

    return [rot(q, None), rot(k, RET_QK ** -0.5), list(v)]


def _fn_ret_gate(rows, pos, consts):
    o, g = rows
    (gn,) = consts
    out = []
    for h in range(RET_HEADS):
        r = lax.rsqrt(jnp.mean(o[h] * o[h], axis=-1, keepdims=True) + RMS_EPS)
        out.append((o[h] * r * gn[h]) * (g[h] * _sigmoid(g[h])))
    return [out]


def _fn_mla_lat(rows, pos, consts):
    (p,) = rows
    gq, gkv = consts
    nq, nkv = MLA_Q_RANK // LANES, MLA_KV_RANK // LANES
    cq, ckv, kr = p[:nq], p[nq:nq + nkv], p[nq + nkv]
    rq = lax.rsqrt(_ssq(cq) / MLA_Q_RANK + RMS_EPS)
    rkv = lax.rsqrt(_ssq(ckv) / MLA_KV_RANK + RMS_EPS)
    return [[t * rq * g for t, g in zip(cq, gq)], [t * rkv * g for t, g in zip(ckv, gkv)], [kr]]


def _swap32_impl(x):
    lane = lax.broadcasted_iota(jnp.int32, x.shape, 1)
    up, down = pltpu.roll(x, LANES - 32, 1), pltpu.roll(x, 32, 1)
    return jnp.where(lane < 32, up, jnp.where(lane < 64, down, 0.0))


@jax.custom_vjp
def _swap32(x):
    return _swap32_impl(x)


_swap32.defvjp(lambda x: (_swap32_impl(x), None), lambda _, g: (_swap32_impl(g),))


def _fn_mla_heads(rows, pos, consts):
    qf, kvf, (kr,) = rows
    cos, sin = pos
    gq, gk = consts
    q_out, k_out, v_out = [], [], []
    for h in range(MLA_HEADS):
        q0, q1 = qf[2 * h], qf[2 * h + 1]
        r = lax.rsqrt(_ssq([q0, q1]) / MLA_QK + RMS_EPS)
        a0, a1 = q0 * r * gq[0], q1 * r * gq[1]
        a1 = a1 * cos + _swap32(a1) * sin
        q_out += [a0 * (MLA_QK ** -0.5), a1 * (MLA_QK ** -0.5)]
        k0 = kvf[2 * h]
        r = lax.rsqrt(_ssq([k0, kr]) / MLA_QK + RMS_EPS)
        b0, b1 = k0 * r * gk[0], kr * r * gk[1]
        k_out += [b0, b1 * cos + _swap32(b1) * sin]
        v_out.append(kvf[2 * h + 1])
    return [q_out, k_out, v_out]


def _shift_down(x, n):
    row = lax.broadcasted_iota(jnp.int32, x.shape, 0)
    return jnp.where(row >= n, pltpu.roll(x, n, 0), 0.0)


def _shift_up(x, n):
    rows = x.shape[0]
    row = lax.broadcasted_iota(jnp.int32, x.shape, 0)
    return jnp.where(row < rows - n, pltpu.roll(x, rows - n, 0), 0.0)


def _conv_blocks(S):
    cb = 256
    return cb, FFN_DIM // cb


def _conv_fwd(ag, w8, B, S, name):
    cb, ncb = _conv_blocks(S)

    def body(a_ref, g_ref, w_ref, u_ref):
        g = g_ref[...]
        w = w_ref[...]
        gc = w[0:1] * _shift_down(g, 2) + w[1:2] * _shift_down(g, 1) + w[2:3] * g + w[3:4]
        u_ref[...] = (a_ref[...] * (gc * _sigmoid(gc))).astype(u_ref.dtype)

    return pl.pallas_call(
        body, name=name, grid=(ncb, B),
        in_specs=[pl.BlockSpec((S, cb), lambda j, b: (b, j)),
                  pl.BlockSpec((S, cb), lambda j, b: (b, ncb + j)),
                  pl.BlockSpec((8, cb), lambda j, b: (0, j))],
        out_specs=pl.BlockSpec((S, cb), lambda j, b: (b, j)),
        out_shape=jax.ShapeDtypeStruct((B * S, FFN_DIM), BF16),
        compiler_params=_params(("parallel", "parallel")),
    )(ag, ag, w8)


def _conv_bwd(ag, w8, du, B, S, name):
    cb, ncb = _conv_blocks(S)

    def body(a_ref, g_ref, w_ref, du_ref, da_ref, dg_ref, dw_ref):
        g = g_ref[...]
        w = w_ref[...]
        g1, g2 = _shift_down(g, 1), _shift_down(g, 2)
        gc = w[0:1] * g2 + w[1:2] * g1 + w[2:3] * g + w[3:4]
        sg = _sigmoid(gc)
        du_v = du_ref[...]
        da_ref[...] = du_v * (gc * sg)
        dgc = du_v * a_ref[...] * (sg * (1.0 + gc * (1.0 - sg)))
        dg_ref[...] = w[2:3] * dgc + w[1:2] * _shift_up(dgc, 1) + w[0:1] * _shift_up(dgc, 2)
        part = jnp.concatenate([
            jnp.sum(dgc * g2, axis=0, keepdims=True), jnp.sum(dgc * g1, axis=0, keepdims=True),
            jnp.sum(dgc * g, axis=0, keepdims=True), jnp.sum(dgc, axis=0, keepdims=True),
            jnp.zeros((4, cb), F32)], axis=0)

        @pl.when(pl.program_id(1) == 0)
        def _():
            dw_ref[...] = part

        @pl.when(pl.program_id(1) > 0)
        def _():
            dw_ref[...] += part

    blk = lambda j, b: (b, j)
    return pl.pallas_call(
        body, name=name, grid=(ncb, B),
        in_specs=[pl.BlockSpec((S, cb), blk),
                  pl.BlockSpec((S, cb), lambda j, b: (b, ncb + j)),
                  pl.BlockSpec((8, cb), lambda j, b: (0, j)),
                  pl.BlockSpec((S, cb), blk)],
        out_specs=[pl.BlockSpec((S, cb), blk), pl.BlockSpec((S, cb), blk),
                   pl.BlockSpec((8, cb), lambda j, b: (0, j))],
        out_shape=[jax.ShapeDtypeStruct((B * S, FFN_DIM), F32), jax.ShapeDtypeStruct((B * S, FFN_DIM), F32),
                   jax.ShapeDtypeStruct((8, FFN_DIM), F32)],
        compiler_params=_params(("parallel", "arbitrary")),
    )(ag, ag, w8, du)


_NT = (((1,), (1,)), ((), ()))
_NN = (((1,), (0,)), ((), ()))
_TN = (((0,), (0,)), ((), ()))


def _dot(a, b, dn):
    return lax.dot_general(a.astype(MXU_DTYPE), b.astype(MXU_DTYPE), dn, preferred_element_type=F32)


def _rel_and_mask():
    il = lax.broadcasted_iota(jnp.int32, (ATT_BLOCK, ATT_BLOCK), 0)
    jl = lax.broadcasted_iota(jnp.int32, (ATT_BLOCK, ATT_BLOCK), 1)
    return (il - jl).astype(F32), (jl // CHUNK) <= (il // CHUNK)


def _rows(i):
    return pl.ds(pl.multiple_of(i * ATT_BLOCK, ATT_BLOCK), ATT_BLOCK)


def _mla_attn_fwd(q, k, v, B, S):
    H, nq = MLA_HEADS, S // ATT_BLOCK

    def body(q_ref, k_ref, v_ref, o_ref, lse_ref):
        _, mask = _rel_and_mask()

        def qblock(i, _):
            qi = q_ref[_rows(i), :]

            def kv(j, carry, diag):
                m, l, acc = carry
                s = _dot(qi, k_ref[_rows(j), :], _NT)
                if diag:
                    s = jnp.where(mask, s, MASK_VALUE)
                m2 = jnp.maximum(m, jnp.max(s, axis=-1, keepdims=True))
                alpha = jnp.exp(m - m2)
                p = jnp.exp(s - m2)
                l2 = alpha * l + jnp.sum(p, axis=-1, keepdims=True)
                return m2, l2, alpha * acc + _dot(p, v_ref[_rows(j), :], _NN)

            init = (jnp.full((ATT_BLOCK, 1), MASK_VALUE, F32), jnp.zeros((ATT_BLOCK, 1), F32),
                    jnp.zeros((ATT_BLOCK, MLA_V), F32))
            carry = lax.fori_loop(0, i, lambda j, c: kv(j, c, False), init)
            m, l, acc = kv(i, carry, True)
            o_ref[_rows(i), :] = acc / l
            lse_ref[0, _rows(i), :] = m + jnp.log(l)
            return 0

        lax.fori_loop(0, nq, qblock, 0)

    return pl.pallas_call(
        body, name="mla_attn_fwd", grid=(B, H),
        in_specs=[pl.BlockSpec((S, MLA_PAD), lambda b, h: (b, h)),
                  pl.BlockSpec((S, MLA_PAD), lambda b, h: (b, h)),
                  pl.BlockSpec((S, MLA_V), lambda b, h: (b, h))],
        out_specs=[pl.BlockSpec((S, MLA_V), lambda b, h: (b, h)),
                   pl.BlockSpec((1, S, 1), lambda b, h: (b * H + h, 0, 0))],
        out_shape=[jax.ShapeDtypeStruct((B * S, H * MLA_V), F32), jax.ShapeDtypeStruct((B * H, S, 1), F32)],
        compiler_params=_params(("parallel", "parallel")),
    )(q, k, v)


def _mla_attn_bwd(q, k, v, o, do, lse, B, S):
    H, nq = MLA_HEADS, S // ATT_BLOCK

    def body(q_ref, k_ref, v_ref, o_ref, do_ref, lse_ref, dq_ref, dk_ref, dv_ref, acc_ref):
        _, mask = _rel_and_mask()
        dk_ref[...] = jnp.zeros(dk_ref.shape, F32)
        dv_ref[...] = jnp.zeros(dv_ref.shape, F32)

        def qblock(i, _):
            qi = q_ref[_rows(i), :]
            doi = do_ref[_rows(i), :]
            delta = jnp.sum(doi * o_ref[_rows(i), :], axis=-1, keepdims=True)
            lse_i = lse_ref[0, _rows(i), :]
            doi = doi.astype(MXU_DTYPE)
            acc_ref[...] = jnp.zeros(acc_ref.shape, F32)

            def kv(j, diag):
                kj = k_ref[_rows(j), :]
                p = jnp.exp(_dot(qi, kj, _NT) - lse_i)
                if diag:
                    p = jnp.where(mask, p, 0.0)
                ds = (p * (_dot(doi, v_ref[_rows(j), :], _NT) - delta)).astype(MXU_DTYPE)
                acc_ref[...] += _dot(ds, kj, _NN)
                dk_ref[_rows(j), :] += _dot(ds, qi, _TN)
                dv_ref[_rows(j), :] += _dot(p, doi, _TN)

            def off(j, c):
                kv(j, False)
                return c

            lax.fori_loop(0, i, off, 0)
            kv(i, True)
            dq_ref[_rows(i), :] = acc_ref[...]
            return 0

        lax.fori_loop(0, nq, qblock, 0)

    qk_spec = pl.BlockSpec((S, MLA_PAD), lambda b, h: (b, h))
    v_spec = pl.BlockSpec((S, MLA_V), lambda b, h: (b, h))
    return pl.pallas_call(
        body, name="mla_attn_bwd", grid=(B, H),
        in_specs=[qk_spec, qk_spec, v_spec, v_spec, v_spec,
                  pl.BlockSpec((1, S, 1), lambda b, h: (b * H + h, 0, 0))],
        out_specs=[qk_spec, qk_spec, v_spec],
        out_shape=[jax.ShapeDtypeStruct((B * S, H * MLA_PAD), F32), jax.ShapeDtypeStruct((B * S, H * MLA_PAD), F32),
                   jax.ShapeDtypeStruct((B * S, H * MLA_V), F32)],
        scratch_shapes=[pltpu.VMEM((ATT_BLOCK, MLA_PAD), F32)],
        compiler_params=_params(("parallel", "parallel")),
    )(q, k, v, o, do, lse)


def _ret_log_gamma():
    lg = np.log1p(-np.exp2(RET_GAMMA_BASE - np.arange(RET_HEADS, dtype=np.float32))).astype(np.float32)
    return jnp.asarray(np.broadcast_to(lg[:, None, None], (RET_HEADS, 8, LANES)).copy())


def _ret_decay(lg, rel, mask, steps):
    if steps is None:
        return jnp.where(mask, jnp.exp(lg * jnp.abs(rel)), 0.0)
    return jnp.exp(lg * (rel + (steps * ATT_BLOCK).astype(F32)))


def _ret_attn_fwd(q, k, v, B, S):
    H, nq = RET_HEADS, S // ATT_BLOCK

    def body(lg_ref, q_ref, k_ref, v_ref, o_ref, acc_ref):
        rel, mask = _rel_and_mask()
        lg = lg_ref[0, 0:1, 0:1]

        def qblock(i, _):
            qi = q_ref[_rows(i), :]
            acc_ref[...] = jnp.zeros(acc_ref.shape, F32)

            def kv(j, steps):
                a = _dot(qi, k_ref[_rows(j), :], _NT) * _ret_decay(lg, rel, mask, steps)
                acc_ref[...] += _dot(a, v_ref[_rows(j), :], _NN)

            def off(j, c):
                kv(j, i - j)
                return c

            lax.fori_loop(0, i, off, 0)
            kv(i, None)
            o_ref[_rows(i), :] = acc_ref[...]
            return 0

        lax.fori_loop(0, nq, qblock, 0)

    qk_spec = pl.BlockSpec((S, RET_QK), lambda b, h: (b, h))
    v_spec = pl.BlockSpec((S, RET_V), lambda b, h: (b, h))
    return pl.pallas_call(
        body, name="ret_attn_fwd", grid=(B, H),
        in_specs=[pl.BlockSpec((1, 8, LANES), lambda b, h: (h, 0, 0)), qk_spec, qk_spec, v_spec],
        out_specs=v_spec,
        out_shape=jax.ShapeDtypeStruct((B * S, H * RET_V), F32),
        scratch_shapes=[pltpu.VMEM((ATT_BLOCK, RET_V), F32)],
        compiler_params=_params(("parallel", "parallel")),
    )(_ret_log_gamma(), q, k, v)


def _ret_attn_bwd(q, k, v, do, B, S):
    H, nq = RET_HEADS, S // ATT_BLOCK

    def body(lg_ref, q_ref, k_ref, v_ref, do_ref, dq_ref, dk_ref, dv_ref, acc_ref):
        rel, mask = _rel_and_mask()
        lg = lg_ref[0, 0:1, 0:1]
        dk_ref[...] = jnp.zeros(dk_ref.shape, F32)
        dv_ref[...] = jnp.zeros(dv_ref.shape, F32)

        def qblock(i, _):
            qi = q_ref[_rows(i), :]
            doi = do_ref[_rows(i), :].astype(MXU_DTYPE)
            acc_ref[...] = jnp.zeros(acc_ref.shape, F32)

            def kv(j, steps):
                kj = k_ref[_rows(j), :]
                dec = _ret_decay(lg, rel, mask, steps)
                a = _dot(qi, kj, _NT) * dec
                da = (_dot(doi, v_ref[_rows(j), :], _NT) * dec).astype(MXU_DTYPE)
                acc_ref[...] += _dot(da, kj, _NN)
                dk_ref[_rows(j), :] += _dot(da, qi, _TN)
                dv_ref[_rows(j), :] += _dot(a, doi, _TN)

            def off(j, c):
                kv(j, i - j)
                return c

            lax.fori_loop(0, i, off, 0)
            kv(i, None)
            dq_ref[_rows(i), :] = acc_ref[...]
            return 0

        lax.fori_loop(0, nq, qblock, 0)

    qk_spec = pl.BlockSpec((S, RET_QK), lambda b, h: (b, h))
    v_spec = pl.BlockSpec((S, RET_V), lambda b, h: (b, h))
    return pl.pallas_call(
        body, name="ret_attn_bwd", grid=(B, H),
        in_specs=[pl.BlockSpec((1, 8, LANES), lambda b, h: (h, 0, 0)), qk_spec, qk_spec, v_spec, v_spec],
        out_specs=[qk_spec, qk_spec, v_spec],
        out_shape=[jax.ShapeDtypeStruct((B * S, H * RET_QK), F32), jax.ShapeDtypeStruct((B * S, H * RET_QK), F32),
                   jax.ShapeDtypeStruct((B * S, H * RET_V), F32)],
        scratch_shapes=[pltpu.VMEM((ATT_BLOCK, RET_QK), F32)],
        compiler_params=_params(("parallel", "parallel")),
    )(_ret_log_gamma(), q, k, v, do)


def _loss_head(y, target, bm=512):
    T, D = y.shape
    bm = _pick(T, bm)

    def body(y_ref, t_ref, dy_ref, l_ref):
        err = y_ref[...] - t_ref[...]
        dy_ref[...] = err / D
        part = jnp.full((8, LANES), 0.5 * jnp.sum(jnp.mean(err * err, axis=-1)), F32)

        @pl.when(pl.program_id(0) == 0)
        def _():
            l_ref[...] = part

        @pl.when(pl.program_id(0) > 0)
        def _():
            l_ref[...] += part

    blk = pl.BlockSpec((bm, D), lambda i: (i, 0))
    dy, l = pl.pallas_call(
        body, name="loss_head", grid=(T // bm,),
        in_specs=[blk, blk], out_specs=[blk, pl.BlockSpec((8, LANES), lambda i: (0, 0))],
        out_shape=[jax.ShapeDtypeStruct((T, D), F32), jax.ShapeDtypeStruct((8, LANES), F32)],
        compiler_params=_params(("arbitrary",)),
    )(y, target)
    return dy, l[0, 0]


def _adamw(w, g, m, v, name):
    R, C = w.shape
    br = R if R * C * 4 <= 2 ** 21 else _pick_rows(R, max(8, (2 ** 21) // (C * 4)))

    def body(w_ref, g_ref, m_ref, v_ref, d_ref, mo_ref, vo_ref):
        g_v = g_ref[...]
        m_v = ADAM_B1 * m_ref[...] + (1.0 - ADAM_B1) * g_v
        v_v = ADAM_B2 * v_ref[...] + (1.0 - ADAM_B2) * (g_v * g_v)
        m_hat = m_v / (1.0 - ADAM_B1 ** ADAM_STEP)
        v_hat = v_v / (1.0 - ADAM_B2 ** ADAM_STEP)
        d_ref[...] = -ADAM_LR * (m_hat / (jnp.sqrt(v_hat) + ADAM_EPS) + ADAM_WD * w_ref[...])
        mo_ref[...] = m_v
        vo_ref[...] = v_v

    blk = pl.BlockSpec((br, C), lambda i: (i, 0))
    return pl.pallas_call(
        body, name=name, grid=(R // br,),
        in_specs=[blk] * 4, out_specs=[blk] * 3,
        out_shape=[jax.ShapeDtypeStruct((R, C), F32)] * 3,
        compiler_params=_params(("parallel",)),
    )(w, g, m, v)


def _pick_rows(R, target):
    best = None
    for d in range(8, min(R, target) + 1, 8):
        if R % d == 0:
            best = d
    assert best is not None, (R, target)
    return best


def _position():
    return lax.axis_index("x"), lax.axis_index("y"), lax.axis_index("c")


HBM_SPEC = pl.BlockSpec(memory_space=pltpu.HBM)


def _all_gather_xy(big, small):
    def body(big_ref, small_ref, obig_ref, osmall_ref, send_sems, recv_sems, local_sems):
        x, y, c = _position()
        me = 2 * x + y
        chips = [(1 - x, y), (x, 1 - y), (1 - x, 1 - y)]
        pairs = [(big_ref, obig_ref), (small_ref, osmall_ref)]
        local = [pltpu.make_async_copy(src, dst.at[me], local_sems.at[a]) for a, (src, dst) in enumerate(pairs)]
        for cp in local:
            cp.start()

        def copy(j, a, slot):
            src, dst = pairs[a]
            px, py = chips[j]
            return pltpu.make_async_remote_copy(
                src_ref=src, dst_ref=dst.at[slot], send_sem=send_sems.at[2 * j + a],
                recv_sem=recv_sems.at[2 * j + a], device_id=(px, py, c), device_id_type=MESH)

        sends = [copy(j, a, me) for j in range(3) for a in range(2)]
        for cp in sends:
            cp.start()
        for j, (px, py) in enumerate(chips):
            for a in range(2):
                copy(j, a, 2 * px + py).wait_recv()
        for cp in sends:
            cp.wait_send()
        for cp in local:
            cp.wait()

    return pl.pallas_call(
        body, name="weights_all_gather",
        in_specs=[HBM_SPEC, HBM_SPEC], out_specs=[HBM_SPEC, HBM_SPEC],
        out_shape=[jax.ShapeDtypeStruct((N_SHARD,) + big.shape, big.dtype),
                   jax.ShapeDtypeStruct((N_SHARD,) + small.shape, small.dtype)],
        scratch_shapes=[pltpu.SemaphoreType.DMA((6,)), pltpu.SemaphoreType.DMA((6,)), pltpu.SemaphoreType.DMA((2,))],
    )(big, small)


def _sibling_send_half(g):
    n, _, rh, cols = g.shape

    def body(g_ref, o_ref, send_sem, recv_sem):
        x, y, c = _position()
        cp = pltpu.make_async_remote_copy(
            src_ref=g_ref.at[:, 1 - c], dst_ref=o_ref, send_sem=send_sem, recv_sem=recv_sem,
            device_id=(x, y, 1 - c), device_id_type=MESH)
        cp.start()
        cp.wait()

    return pl.pallas_call(
        body, name="grads_sibling_halves",
        in_specs=[HBM_SPEC], out_specs=HBM_SPEC,
        out_shape=jax.ShapeDtypeStruct((n, rh, cols), g.dtype),
        scratch_shapes=[pltpu.SemaphoreType.DMA, pltpu.SemaphoreType.DMA],
    )(g)


def _chip_partial(g, got, br=512):
    n, _, rh, cols = g.shape
    br = _pick_rows(rh, br)
    c_idx = lax.axis_index("c").astype(jnp.int32).reshape((1,))

    def body(c_ref, g_ref, got_ref, o_ref):
        o_ref[...] = (g_ref[...] + got_ref[...]).astype(o_ref.dtype)

    return pl.pallas_call(
        body, name="grads_chip_partial",
        grid_spec=pltpu.PrefetchScalarGridSpec(
            num_scalar_prefetch=1, grid=(n, rh // br),
            in_specs=[pl.BlockSpec((None, None, br, cols), lambda s, r, c_ref: (s, c_ref[0], r, 0)),
                      pl.BlockSpec((None, br, cols), lambda s, r, c_ref: (s, r, 0))],
            out_specs=pl.BlockSpec((None, br, cols), lambda s, r, c_ref: (s, r, 0))),
        out_shape=jax.ShapeDtypeStruct((n, rh, cols), BF16),
        compiler_params=_params(("parallel", "parallel")),
    )(c_idx, g, got)


def _exchange_partials(p):
    _, rh, cols = p.shape

    def body(p_ref, o_ref, send_sems, recv_sems):
        x, y, c = _position()
        chips = [(1 - x, y), (x, 1 - y), (1 - x, 1 - y)]
        sends = [pltpu.make_async_remote_copy(
            src_ref=p_ref.at[2 * px + py], dst_ref=o_ref.at[j], send_sem=send_sems.at[j], recv_sem=recv_sems.at[j],
            device_id=(px, py, c), device_id_type=MESH) for j, (px, py) in enumerate(chips)]
        for cp in sends:
            cp.start()
        for cp in sends:
            cp.wait()

    return pl.pallas_call(
        body, name="grads_exchange_partials",
        in_specs=[HBM_SPEC], out_specs=HBM_SPEC,
        out_shape=jax.ShapeDtypeStruct((3, rh, cols), p.dtype),
        scratch_shapes=[pltpu.SemaphoreType.DMA((3,)), pltpu.SemaphoreType.DMA((3,))],
    )(p)


def _sum_partials(p, got, br=512):
    _, rh, cols = p.shape
    br = _pick_rows(rh, br)
    me = (2 * lax.axis_index("x") + lax.axis_index("y")).astype(jnp.int32).reshape((1,))

    def body(me_ref, p_ref, got_ref, o_ref):
        acc = p_ref[...].astype(F32)
        for j in range(3):
            acc = acc + got_ref[j].astype(F32)
        o_ref[...] = acc

    return pl.pallas_call(
        body, name="grads_sum_partials",
        grid_spec=pltpu.PrefetchScalarGridSpec(
            num_scalar_prefetch=1, grid=(rh // br,),
            in_specs=[pl.BlockSpec((None, br, cols), lambda r, me_ref: (me_ref[0], r, 0)),
                      pl.BlockSpec((3, br, cols), lambda r, me_ref: (0, r, 0))],
            out_specs=pl.BlockSpec((br, cols), lambda r, me_ref: (r, 0))),
        out_shape=jax.ShapeDtypeStruct((rh, cols), F32),
        compiler_params=_params(("parallel",)),
    )(me, p, got)


def _sibling_share(half):
    rh, cols = half.shape

    def body(h_ref, o_ref, send_sem, recv_sem, local_sem):
        x, y, c = _position()
        local = pltpu.make_async_copy(h_ref, o_ref.at[c], local_sem)
        local.start()
        cp = pltpu.make_async_remote_copy(
            src_ref=h_ref, dst_ref=o_ref.at[c], send_sem=send_sem, recv_sem=recv_sem,
            device_id=(x, y, 1 - c), device_id_type=MESH)
        cp.start()
        cp.wait_send()
        pltpu.make_async_remote_copy(
            src_ref=h_ref, dst_ref=o_ref.at[1 - c], send_sem=send_sem, recv_sem=recv_sem,
            device_id=(x, y, 1 - c), device_id_type=MESH).wait_recv()
        local.wait()

    return pl.pallas_call(
        body, name="grads_sibling_share",
        in_specs=[HBM_SPEC], out_specs=HBM_SPEC,
        out_shape=jax.ShapeDtypeStruct((2, rh, cols), half.dtype),
        scratch_shapes=[pltpu.SemaphoreType.DMA, pltpu.SemaphoreType.DMA, pltpu.SemaphoreType.DMA],
    )(half)


def _all_reduce_small(v):
    R, cols = v.shape

    def body(v_ref, o_ref, buf_ref, send_sems, recv_sems):
        x, y, c = _position()
        me = 4 * x + 2 * y + c
        buf_ref[me] = v_ref[...]
        sends = []
        for k in range(1, N_DEV):
            px = 1 - x if k & 4 else x
            py = 1 - y if k & 2 else y
            pc = 1 - c if k & 1 else c
            sends.append(pltpu.make_async_remote_copy(
                src_ref=v_ref, dst_ref=buf_ref.at[me], send_sem=send_sems.at[k - 1], recv_sem=recv_sems.at[k - 1],
                device_id=(px, py, pc), device_id_type=MESH))
        for cp in sends:
            cp.start()
        for k in range(1, N_DEV):
            px = 1 - x if k & 4 else x
            py = 1 - y if k & 2 else y
            pc = 1 - c if k & 1 else c
            pltpu.make_async_remote_copy(
                src_ref=v_ref, dst_ref=buf_ref.at[4 * px + 2 * py + pc], send_sem=send_sems.at[k - 1],
                recv_sem=recv_sems.at[k - 1], device_id=(px, py, pc), device_id_type=MESH).wait_recv()
        for cp in sends:
            cp.wait_send()
        acc = buf_ref[0]
        for d in range(1, N_DEV):
            acc = acc + buf_ref[d]
        o_ref[...] = acc

    return pl.pallas_call(
        body, name="small_grads_all_reduce",
        in_specs=[pl.BlockSpec(memory_space=pltpu.VMEM)], out_specs=pl.BlockSpec(memory_space=pltpu.VMEM),
        out_shape=jax.ShapeDtypeStruct((R, cols), F32),
        scratch_shapes=[pltpu.VMEM((N_DEV, R, cols), F32), pltpu.SemaphoreType.DMA((N_DEV - 1,)),
                        pltpu.SemaphoreType.DMA((N_DEV - 1,))],
    )(v)


def _rope_tables(S, half, width):
    inv_freq = ROPE_THETA ** (-jnp.arange(half, dtype=F32) / half)
    ang = jnp.arange(S).astype(F32)[:, None] * inv_freq[None, :]
    return jnp.cos(ang), jnp.sin(ang)


def _local_step(x, target, w, B, S):
    T = B * S
    D = D_MODEL
    bm = 256
    full = lambda a, wd, tile=None: (a, wd, 0, tile or wd)
    g = {}

    cos_r, sin_r = _rope_tables(S, RET_QK // 2, LANES)
    cos_m, sin_m = _rope_tables(S, MLA_ROPE // 2, LANES)
    zeros64 = jnp.zeros((S, 64), F32)
    cos_m = jnp.concatenate([cos_m, cos_m, zeros64], axis=1)
    sin_m = jnp.concatenate([-sin_m, sin_m, zeros64], axis=1)

    def ffn_fwd(xin, i):
        norm = w["ffn_norm"][i:i + 1]
        (h,) = _rowwise_fwd(_fn_rms, f"ffn{i}_norm", [full(xin, D)], [], [(norm, D)], [(D, D, BF16)], bm, S)
        ag = _mm(h, w["ffn_w_in"][i], "nn", F32, f"ffn{i}_in", bn=1408)
        u = _conv_fwd(ag, w["ffn_conv8"][i], B, S, f"ffn{i}_conv")
        xout = _mm(u, w["ffn_w_out"][i], "nn", F32, f"ffn{i}_out", residual=xin, bk=1408)
        return xout, (xin, norm, h, ag, u)

    def ffn_bwd(dxout, saved, i):
        xin, norm, h, ag, u = saved
        du = _mm(dxout, w["ffn_w_out"][i], "nt", F32, f"ffn{i}_out_dx", bn=1408)
        g_w_out = _mm(u, dxout, "tn", F32, f"ffn{i}_out_dw", bm=1408)
        da, dg, dw8 = _conv_bwd(ag, w["ffn_conv8"][i], du, B, S, f"ffn{i}_conv_bwd")
        dag = jnp.concatenate([da, dg], axis=1)
        dh = _mm(dag, w["ffn_w_in"][i], "nt", F32, f"ffn{i}_in_dx", bk=1408)
        g_w_in = _mm(h, dag, "tn", F32, f"ffn{i}_in_dw", bn=1408)
        (dxin,), (g_norm,) = _rowwise_bwd(_fn_rms, f"ffn{i}_norm_bwd", [full(xin, D)], [], [(norm, D)],
                                          [(dh, D)], bm, S, adds={0: dxout})
        return dxin, (g_norm, g_w_in, dw8, g_w_out)

    (h0,) = _rowwise_fwd(_fn_rms, "ret_norm", [full(x, D)], [], [(w["ret_norm"], D)], [(D, D, BF16)], bm, S)
    proj = _mm(h0, w["ret_w_in"], "nn", F32, "ret_in")
    HQ, HV = RET_HEADS * RET_QK, RET_HEADS * RET_V
    rope_rows = [(proj, HQ, 0, LANES), (proj, HQ, 1, LANES), (proj, HV, 1, LANES)]
    q_r, k_r, v_r = _rowwise_fwd(_fn_ret_rope, "ret_rope", rope_rows, [cos_r, sin_r], [],
                                 [(HQ, LANES, BF16), (HQ, LANES, BF16), (HV, LANES, BF16)], bm, S)
    ret_o = _ret_attn_fwd(q_r, k_r, v_r, B, S)
    gate_rows = [full(ret_o, HV, RET_V), (proj, HV, 2, RET_V)]
    (y0,) = _rowwise_fwd(_fn_ret_gate, "ret_gate", gate_rows, [], [(w["ret_gn"], RET_V)], [(HV, RET_V, BF16)], 128, S)
    x1 = _mm(y0, w["ret_w_out"], "nn", F32, "ret_out", residual=x)
    x2, ffn0_saved = ffn_fwd(x1, 0)

    (h2,) = _rowwise_fwd(_fn_rms, "mla_norm", [full(x2, D)], [], [(w["mla_norm"], D)], [(D, D, BF16)], bm, S)
    proj2 = _mm(h2, w["mla_w_in"], "nn", F32, "mla_in")
    lat_consts = [(w["mla_q_norm"], LANES), (w["mla_kv_norm"], LANES)]
    cqn, ckvn, kr = _rowwise_fwd(_fn_mla_lat, "mla_latent_norm", [full(proj2, MLA_IN_PAD, LANES)], [], lat_consts,
                                 [(MLA_Q_RANK, LANES, BF16), (MLA_KV_RANK, LANES, BF16), (LANES, LANES, F32)], bm, S)
    qf = _mm(cqn, w["mla_w_qb"], "nn", F32, "mla_qb")
    kvf = _mm(ckvn, w["mla_w_kvb"], "nn", F32, "mla_kvb")
    HP, HVm = MLA_HEADS * MLA_PAD, MLA_HEADS * MLA_V
    head_rows = [full(qf, HP, LANES), full(kvf, HP, LANES), full(kr, LANES)]
    head_consts = [(w["mla_q_head_norm"], LANES), (w["mla_k_head_norm"], LANES)]
    q_a, k_a, v_a = _rowwise_fwd(_fn_mla_heads, "mla_heads", head_rows, [cos_m, sin_m], head_consts,
                                 [(HP, LANES, BF16), (HP, LANES, BF16), (HVm, LANES, BF16)], bm, S)
    att_o, lse = _mla_attn_fwd(q_a, k_a, v_a, B, S)
    x3 = _mm(att_o, w["mla_w_out"], "nn", F32, "mla_out", residual=x2)
    x4, ffn1_saved = ffn_fwd(x3, 1)

    dy, loss = _loss_head(x4, target)

    dx3, (g_n1, g_in1, dw8_1, g_out1) = ffn_bwd(dy, ffn1_saved, 1)

    d_att_o = _mm(dx3, w["mla_w_out"], "nt", F32, "mla_out_dx")
    g["mla_w_out"] = _mm(att_o, dx3, "tn", F32, "mla_out_dw")
    dq_a, dk_a, dv_a = _mla_attn_bwd(q_a, k_a, v_a, att_o, d_att_o, lse, B, S)
    (dqf, dkvf, dkr), (g["mla_q_head_norm"], g["mla_k_head_norm"]) = _rowwise_bwd(
        _fn_mla_heads, "mla_heads_bwd", head_rows, [cos_m, sin_m], head_consts,
        [(dq_a, LANES), (dk_a, LANES), (dv_a, LANES)], 128, S)
    dcqn = _mm(dqf, w["mla_w_qb"], "nt", F32, "mla_qb_dx")
    g["mla_w_qb"] = _mm(cqn, dqf, "tn", F32, "mla_qb_dw")
    dckvn = _mm(dkvf, w["mla_w_kvb"], "nt", F32, "mla_kvb_dx")
    g["mla_w_kvb"] = _mm(ckvn, dkvf, "tn", F32, "mla_kvb_dw")
    (dproj2,), (g["mla_q_norm"], g["mla_kv_norm"]) = _rowwise_bwd(
        _fn_mla_lat, "mla_latent_norm_bwd", [full(proj2, MLA_IN_PAD, LANES)], [], lat_consts,
        [(dcqn, LANES), (dckvn, LANES), (dkr, LANES)], bm, S)
    dh2 = _mm(dproj2, w["mla_w_in"], "nt", F32, "mla_in_dx")
    g["mla_w_in"] = _mm(h2, dproj2, "tn", F32, "mla_in_dw")
    (dx2,), (g["mla_norm"],) = _rowwise_bwd(_fn_rms, "mla_norm_bwd", [full(x2, D)], [], [(w["mla_norm"], D)],
                                            [(dh2, D)], bm, S, adds={0: dx3})

    dx1, (g_n0, g_in0, dw8_0, g_out0) = ffn_bwd(dx2, ffn0_saved, 0)

    dy0 = _mm(dx1, w["ret_w_out"], "nt", F32, "ret_out_dx")
    g["ret_w_out"] = _mm(y0, dx1, "tn", F32, "ret_out_dw")
    (d_ret_o, dgate), (g["ret_gn"],) = _rowwise_bwd(_fn_ret_gate, "ret_gate_bwd", gate_rows, [], [(w["ret_gn"], RET_V)],
                                                    [(dy0, RET_V)], 128, S)
    dq_r, dk_r, dv_r = _ret_attn_bwd(q_r, k_r, v_r, d_ret_o, B, S)
    (dq, dk, dv), _ = _rowwise_bwd(_fn_ret_rope, "ret_rope_bwd", rope_rows, [cos_r, sin_r], [],
                                   [(dq_r, LANES), (dk_r, LANES), (dv_r, LANES)], bm, S)
    dproj = jnp.concatenate([dq, dk, dv, dgate], axis=1)
    dh0 = _mm(dproj, w["ret_w_in"], "nt", F32, "ret_in_dx", bk=1024)
    g["ret_w_in"] = _mm(h0, dproj, "tn", F32, "ret_in_dw")
    (dx,), (g["ret_norm"],) = _rowwise_bwd(_fn_rms, "ret_norm_bwd", [full(x, D)], [], [(w["ret_norm"], D)],
                                           [(dh0, D)], bm, S, adds={0: dx1})

    g["ffn_norm"] = jnp.concatenate([g_n0, g_n1], axis=0)
    g["ffn_w_in"] = jnp.stack([g_in0, g_in1])
    g["ffn_w_out"] = jnp.stack([g_out0, g_out1])
    g["ffn_conv_w"] = jnp.stack([dw8_0[0:3], dw8_1[0:3]])
    g["ffn_conv_b"] = jnp.stack([dw8_0[3], dw8_1[3]])
    return loss, dx, g


_BIG = [("ret_w_in", 2), ("ret_w_out", 1), ("mla_w_in", 1), ("mla_w_qb", 2), ("mla_w_kvb", 2), ("mla_w_out", 1),
        ("ffn_w_in", 2), ("ffn_w_out", 1)]
_SMALL_SHARDED = [("ret_gn", 2), ("mla_norm", 1), ("mla_q_norm", 1), ("mla_kv_norm", 1), ("ffn_conv_w", 2)]
_SMALL_REPLICATED = ["ret_norm", "mla_q_head_norm", "mla_k_head_norm", "ffn_norm", "ffn_conv_b"]
_SMALL_ALL = ["ret_norm", "ret_gn", "mla_norm", "mla_q_norm", "mla_kv_norm", "mla_q_head_norm", "mla_k_head_norm",
              "ffn_norm", "ffn_conv_w", "ffn_conv_b"]
PACK_COLS = 1024
PACK_ROW_UNIT = 1024


def _to_slots(full, axis):
    shape = full.shape
    split = shape[:axis] + (N_SHARD, shape[axis] // N_SHARD) + shape[axis + 1:]
    return jnp.moveaxis(full.reshape(split), axis, 0).reshape(N_SHARD, -1)


def _from_slots(slots, shard_shape, axis):
    parts = jnp.moveaxis(slots.reshape((N_SHARD,) + tuple(shard_shape)), 0, axis)
    full = shard_shape[:axis] + (N_SHARD * shard_shape[axis],) + shard_shape[axis + 1:]
    return parts.reshape(full)


def _pad_rows(flat, cols, row_unit):
    n, L = flat.shape
    unit = cols * row_unit
    Lp = -(-L // unit) * unit
    if Lp != L:
        flat = jnp.concatenate([flat, jnp.zeros((n, Lp - L), flat.dtype)], axis=1)
    return flat.reshape(n, Lp // cols, cols)


def _pad_heads(a, axis):
    shape = a.shape
    a = a.reshape(shape[:axis] + (MLA_HEADS, MLA_QK) + shape[axis + 1:])
    pad = [(0, 0)] * a.ndim
    pad[axis + 1] = (0, MLA_PAD - MLA_QK)
    return jnp.pad(a, pad).reshape(shape[:axis] + (MLA_HEADS * MLA_PAD,) + shape[axis + 1:])


def _unpad_heads(a, axis):
    shape = a.shape
    a = a.reshape(shape[:axis] + (MLA_HEADS, MLA_PAD) + shape[axis + 1:])
    a = lax.slice_in_dim(a, 0, MLA_QK, axis=axis + 1)
    return a.reshape(shape[:axis] + (MLA_HEADS * MLA_QK,) + shape[axis + 1:])


def kernel(x, ret_norm, ret_w_in, ret_gn, ret_w_out, mla_norm, mla_w_in, mla_q_norm, mla_w_qb, mla_kv_norm, mla_w_kvb, mla_q_head_norm, mla_k_head_norm, mla_w_out, ffn_norm, ffn_w_in, ffn_conv_w, ffn_conv_b, ffn_w_out, loss_target, m_ret_norm, m_ret_w_in, m_ret_gn, m_ret_w_out, m_mla_norm, m_mla_w_in, m_mla_q_norm, m_mla_w_qb, m_mla_kv_norm, m_mla_w_kvb, m_mla_q_head_norm, m_mla_k_head_norm, m_mla_w_out, m_ffn_norm, m_ffn_w_in, m_ffn_conv_w, m_ffn_conv_b, m_ffn_w_out, v_ret_norm, v_ret_w_in, v_ret_gn, v_ret_w_out, v_mla_norm, v_mla_w_in, v_mla_q_norm, v_mla_w_qb, v_mla_kv_norm, v_mla_w_kvb, v_mla_q_head_norm, v_mla_k_head_norm, v_mla_w_out, v_ffn_norm, v_ffn_w_in, v_ffn_conv_w, v_ffn_conv_b, v_ffn_w_out):
    names = ["ret_norm", "ret_w_in", "ret_gn", "ret_w_out", "mla_norm", "mla_w_in", "mla_q_norm", "mla_w_qb",
             "mla_kv_norm", "mla_w_kvb", "mla_q_head_norm", "mla_k_head_norm", "mla_w_out", "ffn_norm", "ffn_w_in",
             "ffn_conv_w", "ffn_conv_b", "ffn_w_out"]
    shard = dict(zip(names, [ret_norm, ret_w_in, ret_gn, ret_w_out, mla_norm, mla_w_in, mla_q_norm, mla_w_qb,
                             mla_kv_norm, mla_w_kvb, mla_q_head_norm, mla_k_head_norm, mla_w_out, ffn_norm, ffn_w_in,
                             ffn_conv_w, ffn_conv_b, ffn_w_out]))
    mom_m = dict(zip(names, [m_ret_norm, m_ret_w_in, m_ret_gn, m_ret_w_out, m_mla_norm, m_mla_w_in, m_mla_q_norm,
                             m_mla_w_qb, m_mla_kv_norm, m_mla_w_kvb, m_mla_q_head_norm, m_mla_k_head_norm, m_mla_w_out,
                             m_ffn_norm, m_ffn_w_in, m_ffn_conv_w, m_ffn_conv_b, m_ffn_w_out]))
    mom_v = dict(zip(names, [v_ret_norm, v_ret_w_in, v_ret_gn, v_ret_w_out, v_mla_norm, v_mla_w_in, v_mla_q_norm,
                             v_mla_w_qb, v_mla_kv_norm, v_mla_w_kvb, v_mla_q_head_norm, v_mla_k_head_norm, v_mla_w_out,
                             v_ffn_norm, v_ffn_w_in, v_ffn_conv_w, v_ffn_conv_b, v_ffn_w_out]))
    B, S, D = x.shape
    T = B * S
    sx, sy = lax.axis_index("x"), lax.axis_index("y")
    me = 2 * sx + sy

    big_sizes = [int(np.prod(shard[n].shape)) for n, _ in _BIG]
    small_sizes = [int(np.prod(shard[n].shape)) for n, _ in _SMALL_SHARDED]
    big = jnp.concatenate([shard[n].astype(BF16).reshape(1, -1) for n, _ in _BIG], axis=1)
    big = _pad_rows(big, PACK_COLS, PACK_ROW_UNIT)[0]
    small = jnp.concatenate([shard[n].reshape(1, -1) for n, _ in _SMALL_SHARDED], axis=1)
    small = _pad_rows(small, LANES, 8)[0]
    gbig, gsmall = _all_gather_xy(big, small)
    gbig = gbig.reshape(N_SHARD, -1)
    gsmall = gsmall.reshape(N_SHARD, -1)
    wfull = {}
    off = 0
    for (n, ax), sz in zip(_BIG, big_sizes):
        wfull[n] = _from_slots(gbig[:, off:off + sz], shard[n].shape, ax)
        off += sz
    off = 0
    for (n, ax), sz in zip(_SMALL_SHARDED, small_sizes):
        wfull[n] = _from_slots(gsmall[:, off:off + sz], shard[n].shape, ax)
        off += sz
    for n in _SMALL_REPLICATED:
        wfull[n] = shard[n]

    conv8 = jnp.concatenate([wfull["ffn_conv_w"], wfull["ffn_conv_b"][:, None, :],
                             jnp.zeros((2, 4, FFN_DIM), F32)], axis=1)
    mla_w_in_p = jnp.pad(wfull["mla_w_in"][0], ((0, 0), (0, MLA_IN_PAD - MLA_IN)))
    w = {
        "ret_norm": wfull["ret_norm"], "ret_w_in": wfull["ret_w_in"][0],
        "ret_gn": wfull["ret_gn"].reshape(1, RET_HEADS * RET_V), "ret_w_out": wfull["ret_w_out"][0],
        "mla_norm": wfull["mla_norm"], "mla_w_in": mla_w_in_p, "mla_q_norm": wfull["mla_q_norm"],
        "mla_w_qb": _pad_heads(wfull["mla_w_qb"][0], 1), "mla_kv_norm": wfull["mla_kv_norm"],
        "mla_w_kvb": wfull["mla_w_kvb"][0],
        "mla_q_head_norm": jnp.pad(wfull["mla_q_head_norm"], ((0, 0), (0, MLA_PAD - MLA_QK))),
        "mla_k_head_norm": jnp.pad(wfull["mla_k_head_norm"], ((0, 0), (0, MLA_PAD - MLA_QK))),
        "mla_w_out": wfull["mla_w_out"][0], "ffn_norm": wfull["ffn_norm"], "ffn_w_in": wfull["ffn_w_in"],
        "ffn_conv8": conv8, "ffn_w_out": wfull["ffn_w_out"],
    }

    loss_part, dx, gl = _local_step(x.reshape(T, D), loss_target.reshape(T, D), w, B, S)
    loss = lax.psum(loss_part, ("x", "y", "c"))
    gfull = {
        "ret_norm": gl["ret_norm"], "ret_w_in": gl["ret_w_in"][None],
        "ret_gn": gl["ret_gn"].reshape(1, RET_HEADS, RET_V), "ret_w_out": gl["ret_w_out"][None],
        "mla_norm": gl["mla_norm"], "mla_w_in": gl["mla_w_in"][None, :, :MLA_IN], "mla_q_norm": gl["mla_q_norm"],
        "mla_w_qb": _unpad_heads(gl["mla_w_qb"], 1)[None], "mla_kv_norm": gl["mla_kv_norm"],
        "mla_w_kvb": gl["mla_w_kvb"][None],
        "mla_q_head_norm": gl["mla_q_head_norm"][:, :MLA_QK], "mla_k_head_norm": gl["mla_k_head_norm"][:, :MLA_QK],
        "mla_w_out": gl["mla_w_out"][None], "ffn_norm": gl["ffn_norm"], "ffn_w_in": gl["ffn_w_in"],
        "ffn_conv_w": gl["ffn_conv_w"], "ffn_conv_b": gl["ffn_conv_b"], "ffn_w_out": gl["ffn_w_out"],
    }

    gslots = jnp.concatenate([_to_slots(gfull[n], ax) for n, ax in _BIG], axis=1)
    gslots = _pad_rows(gslots, PACK_COLS, PACK_ROW_UNIT)
    rh = gslots.shape[1] // 2
    gslots = gslots.reshape(N_SHARD, 2, rh, PACK_COLS)
    from_sibling = _sibling_send_half(gslots)
    partial = _chip_partial(gslots, from_sibling)
    from_chips = _exchange_partials(partial)
    half = _sum_partials(partial, from_chips)
    gred = _sibling_share(half).reshape(-1)

    small_sizes_all = [int(np.prod(gfull[n].shape)) for n in _SMALL_ALL]
    gsm = jnp.concatenate([gfull[n].reshape(1, -1) for n in _SMALL_ALL], axis=1)
    gsm = _all_reduce_small(_pad_rows(gsm, LANES, 8)[0]).reshape(-1)

    grads = {}
    off = 0
    for (n, ax), sz in zip(_BIG, big_sizes):
        grads[n] = gred[off:off + sz].reshape(shard[n].shape)
        off += sz
    sharded_axis = dict(_SMALL_SHARDED)
    off = 0
    for n, sz in zip(_SMALL_ALL, small_sizes_all):
        gn = gsm[off:off + sz].reshape(gfull[n].shape)
        off += sz
        if n in sharded_axis:
            ax = sharded_axis[n]
            width = shard[n].shape[ax]
            gn = lax.dynamic_slice_in_dim(gn, me * width, width, axis=ax)
        grads[n] = gn

    delta, new_m, new_v = {}, {}, {}
    for n, _ in _BIG:
        shp = shard[n].shape
        two_d = lambda a: a.reshape(-1, shp[-1])
        d_, m_, v_ = _adamw(two_d(shard[n]), two_d(grads[n]), two_d(mom_m[n]), two_d(mom_v[n]), f"adamw_{n}")
        delta[n], new_m[n], new_v[n] = d_.reshape(shp), m_.reshape(shp), v_.reshape(shp)
    pack_small = lambda d: _pad_rows(jnp.concatenate([d[n].reshape(1, -1) for n in _SMALL_ALL], axis=1), LANES, 8)[0]
    d_, m_, v_ = _adamw(pack_small(shard), pack_small(grads), pack_small(mom_m), pack_small(mom_v), "adamw_small")
    off = 0
    for n in _SMALL_ALL:
        sz = int(np.prod(shard[n].shape))
        for dst, src in ((delta, d_), (new_m, m_), (new_v, v_)):
            dst[n] = src.reshape(-1)[off:off + sz].reshape(shard[n].shape)
        off += sz

    return (loss, dx.reshape(B, S, D), *[grads[n] for n in names], *[delta[n] for n in names],
            *[new_m[n] for n in names], *[new_v[n] for n in names])
```

```python
import functools
import math

import numpy as np
import jax
import jax.numpy as jnp
from jax import lax
from jax.experimental import pallas as pl
from jax.experimental.pallas import tpu as pltpu

F32 = jnp.float32
BF16 = jnp.bfloat16
MXU_DTYPE = jnp.bfloat16

CHUNK = 64
RMS_EPS = 1e-6
ROPE_THETA = 10000.0
D_MODEL = 1024
RET_HEADS = 4
RET_QK = 256
RET_V = 512
RET_GAMMA_BASE = -5.0
MLA_HEADS = 8
MLA_Q_RANK = 384
MLA_KV_RANK = 256
MLA_NOPE = 128
MLA_ROPE = 64
MLA_V = 128
MLA_QK = MLA_NOPE + MLA_ROPE
MLA_PAD = 256
MLA_IN = MLA_Q_RANK + MLA_KV_RANK + MLA_ROPE
MLA_IN_PAD = MLA_IN + 64
MASK_VALUE = -1e30
FFN_DIM = 2816
ADAM_LR = 0.001
ADAM_B1 = 0.9
ADAM_B2 = 0.999
ADAM_EPS = 1e-08
ADAM_WD = 0.01
ADAM_STEP = 10

LANES = 128
ATT_BLOCK = 256
VMEM_LIMIT = 56 * 2 ** 20
N_SHARD = 4
N_DEV = 8

MESH = pl.DeviceIdType.MESH


def _params(sem=None, **kw):
    return pltpu.CompilerParams(dimension_semantics=sem, vmem_limit_bytes=VMEM_LIMIT, **kw)


def _pick(dim, target):
    if dim <= target:
        return dim
    best = None
    for d in range(LANES, target + 1, LANES):
        if dim % d == 0:
            best = d
    assert best is not None, (dim, target)
    return best


def _mm(a, b, dims, out_dtype, name, residual=None, bm=512, bn=1024, bk=2048, out_slots=None):
    a_parts = list(a) if isinstance(a, (list, tuple)) else [a]
    b_parts = list(b) if isinstance(b, (list, tuple)) else [b]
    if dims == "tn":
        assert len(a_parts) == 1
        K, M = a_parts[0].shape
        N = sum(p.shape[1] for p in b_parts)
        part_widths = [p.shape[1] for p in b_parts]
    else:
        assert len(b_parts) == 1
        M = a_parts[0].shape[0]
        K = sum(p.shape[1] for p in a_parts)
        N = b_parts[0].shape[1 if dims == "nn" else 0]
        part_widths = [p.shape[1] for p in a_parts]
    bm, bn, bk = _pick(M, bm), _pick(N, bn), _pick(K, min(bk, 1024) if dims == "tn" else bk)
    nk = K // bk
    unit = bn if dims == "tn" else bk
    assert all(wd % unit == 0 for wd in part_widths), (name, part_widths, unit)
    bounds = np.cumsum([0] + [wd // unit for wd in part_widths])
    ranges = [(int(lo), int(hi)) for lo, hi in zip(bounds[:-1], bounds[1:])]

    def part_index(idx, lo, hi):
        return jnp.clip(idx - lo, 0, hi - lo - 1)

    if dims == "tn":
        a_specs = [pl.BlockSpec((bk, bm), lambda i, j, k: (k, i))]
        b_specs = [pl.BlockSpec((bk, bn), functools.partial(lambda i, j, k, lo, hi: (k, part_index(j, lo, hi)), lo=lo, hi=hi))
                   for lo, hi in ranges]
        dn = (((0,), (0,)), ((), ()))
    else:
        a_specs = [pl.BlockSpec((bm, bk), functools.partial(lambda i, j, k, lo, hi: (i, part_index(k, lo, hi)), lo=lo, hi=hi))
                   for lo, hi in ranges]
        if dims == "nt":
            b_specs = [pl.BlockSpec((bn, bk), lambda i, j, k: (j, k))]
        else:
            b_specs = [pl.BlockSpec((bk, bn), lambda i, j, k: (k, j))]
        dn = (((1,), (1 if dims == "nt" else 0,)), ((), ()))
    r_spec = pl.BlockSpec((bm, bn), lambda i, j, k: (i, j))
    if out_slots is None:
        o_spec, o_shape = r_spec, (M, N)
    else:
        ns = N // out_slots
        assert ns % bn == 0, (name, ns, bn)
        nbs = ns // bn
        o_spec = pl.BlockSpec((None, bm, bn), lambda i, j, k: (j // nbs, i, j % nbs))
        o_shape = (out_slots, M, ns)
    has_res = residual is not None
    na, nb = len(a_parts), len(b_parts)

    def body(*refs):
        a_refs, b_refs = refs[:na], refs[na:na + nb]
        r_ref = refs[na + nb] if has_res else None
        o_ref = refs[na + nb + has_res]
        acc_ref = refs[na + nb + has_res + 1] if nk > 1 else None
        k = pl.program_id(2)

        def finish(acc):
            if has_res:
                acc = acc + r_ref[...].astype(F32)
            o_ref[...] = acc.astype(out_dtype)

        def compute(a_ref, b_ref):
            p = lax.dot_general(a_ref[...].astype(MXU_DTYPE), b_ref[...].astype(MXU_DTYPE), dn,
                                preferred_element_type=F32)
            if nk == 1:
                finish(p)
                return

            @pl.when(k == 0)
            def _():
                acc_ref[...] = p

            @pl.when(jnp.logical_and(k > 0, k < nk - 1))
            def _():
                acc_ref[...] += p

            @pl.when(k == nk - 1)
            def _():
                finish(acc_ref[...] + p)

        if len(ranges) == 1:
            compute(a_refs[0], b_refs[0])
        else:
            idx = pl.program_id(1) if dims == "tn" else k
            for p, (lo, hi) in enumerate(ranges):
                @pl.when(jnp.logical_and(idx >= lo, idx < hi))
                def _(p=p):
                    compute(a_refs[0 if dims == "tn" else p], b_refs[p if dims == "tn" else 0])

    return pl.pallas_call(
        body, name=name, grid=(M // bm, N // bn, nk),
        in_specs=a_specs + b_specs + ([r_spec] if has_res else []), out_specs=o_spec,
        out_shape=jax.ShapeDtypeStruct(o_shape, out_dtype),
        scratch_shapes=[pltpu.VMEM((bm, bn), F32)] if nk > 1 else [],
        compiler_params=_params(("parallel", "parallel", "arbitrary")),
    )(*a_parts, *b_parts, *((residual,) if has_res else ()))


def _tiles(ref, width, tile):
    return [ref[:, t * tile:(t + 1) * tile].astype(F32) for t in range(width // tile)]


def _row_specs(rows, pos, consts, bm, S):
    npos_blocks = S // bm
    specs = [pl.BlockSpec((bm, w), functools.partial(lambda i, c: (i, c), c=cb)) for (_, w, cb, _) in rows]
    specs += [pl.BlockSpec((bm, p.shape[1]), lambda i: (i % npos_blocks, 0)) for p in pos]
    specs += [pl.BlockSpec(c.shape, lambda i: (0, 0)) for (c, _) in consts]
    return specs


def _rowwise_fwd(fn, name, rows, pos, consts, outs, bm, S):
    T = rows[0][0].shape[0]
    nr, npos, nc = len(rows), len(pos), len(consts)

    def body(*refs):
        row_v = [_tiles(r, w, t) for r, (_, w, _, t) in zip(refs[:nr], rows)]
        pos_v = [r[...] for r in refs[nr:nr + npos]]
        const_v = [_tiles(r, c.shape[1], t) for r, (c, t) in zip(refs[nr + npos:nr + npos + nc], consts)]
        res = fn(row_v, pos_v, const_v)
        for o_ref, tiles, (w, t, dt) in zip(refs[nr + npos + nc:], res, outs):
            for k, v in enumerate(tiles):
                o_ref[:, k * t:(k + 1) * t] = v.astype(dt)

    return pl.pallas_call(
        body, name=name, grid=(T // bm,),
        in_specs=_row_specs(rows, pos, consts, bm, S),
        out_specs=[pl.BlockSpec((bm, w), lambda i: (i, 0)) for (w, _, _) in outs],
        out_shape=[jax.ShapeDtypeStruct((T, w), dt) for (w, _, dt) in outs],
        compiler_params=_params(("parallel",)),
    )(*[r[0] for r in rows], *pos, *[c[0] for c in consts])


def _rowwise_bwd(fn, name, rows, pos, consts, cts, bm, S, adds=None):
    adds = adds or {}
    T = rows[0][0].shape[0]
    nr, npos, nc, nct = len(rows), len(pos), len(consts), len(cts)
    add_idx = sorted(adds)

    def body(*refs):
        it = iter(refs)
        row_refs = [next(it) for _ in range(nr)]
        pos_refs = [next(it) for _ in range(npos)]
        const_refs = [next(it) for _ in range(nc)]
        ct_refs = [next(it) for _ in range(nct)]
        add_refs = {k: next(it) for k in add_idx}
        drow_refs = [next(it) for _ in range(nr)]
        dconst_refs = [next(it) for _ in range(nc)]
        row_v = [_tiles(r, w, t) for r, (_, w, _, t) in zip(row_refs, rows)]
        pos_v = [r[...] for r in pos_refs]
        const_v = [_tiles(r, c.shape[1], t) for r, (c, t) in zip(const_refs, consts)]
        ct_v = [_tiles(r, c.shape[1], t) for r, (c, t) in zip(ct_refs, cts)]
        _, vjp = jax.vjp(lambda rv, cv: fn(rv, pos_v, cv), row_v, const_v)
        drows, dconsts = vjp(ct_v)
        for a, (d_ref, tiles, (_, w, _, t)) in enumerate(zip(drow_refs, drows, rows)):
            for k, v in enumerate(tiles):
                if a in add_refs:
                    v = v + add_refs[a][:, k * t:(k + 1) * t].astype(F32)
                d_ref[:, k * t:(k + 1) * t] = v
        first = pl.program_id(0) == 0
        for d_ref, tiles, (_, t) in zip(dconst_refs, dconsts, consts):
            for k, v in enumerate(tiles):
                @pl.when(first)
                def _(d_ref=d_ref, k=k, t=t, v=v):
                    d_ref[:, k * t:(k + 1) * t] = v

                @pl.when(jnp.logical_not(first))
                def _(d_ref=d_ref, k=k, t=t, v=v):
                    d_ref[:, k * t:(k + 1) * t] += v

    in_specs = _row_specs(rows, pos, consts, bm, S)
    in_specs += [pl.BlockSpec((bm, c.shape[1]), lambda i: (i, 0)) for (c, _) in cts]
    in_specs += [pl.BlockSpec((bm, adds[k].shape[1]), lambda i: (i, 0)) for k in add_idx]
    out_specs = [pl.BlockSpec((bm, w), lambda i: (i, 0)) for (_, w, _, _) in rows]
    out_specs += [pl.BlockSpec(c.shape, lambda i: (0, 0)) for (c, _) in consts]
    out_shape = [jax.ShapeDtypeStruct((T, w), F32) for (_, w, _, _) in rows]
    out_shape += [jax.ShapeDtypeStruct(c.shape, F32) for (c, _) in consts]
    res = pl.pallas_call(
        body, name=name, grid=(T // bm,),
        in_specs=in_specs, out_specs=out_specs, out_shape=out_shape,
        compiler_params=_params(("arbitrary",)),
    )(*[r[0] for r in rows], *pos, *[c[0] for c in consts], *[c[0] for c in cts], *[adds[k] for k in add_idx])
    return res[:nr], res[nr:]


def _ssq(tiles):
    s = jnp.sum(tiles[0] * tiles[0], axis=-1, keepdims=True)
    for t in tiles[1:]:
        s = s + jnp.sum(t * t, axis=-1, keepdims=True)
    return s


def _sigmoid(x):
    return 1.0 / (1.0 + jnp.exp(-x))


def _fn_rms(rows, pos, consts):
    (x,), (g,) = rows[0], consts[0]
    r = lax.rsqrt(jnp.mean(x * x, axis=-1, keepdims=True) + RMS_EPS)
    return [[x * r * g]]


def _fn_ret_rope(rows, pos, consts):
    (qkv,) = rows
    nq = RET_HEADS * RET_QK // LANES
    q, k, v = qkv[:nq], qkv[nq:2 * nq], qkv[2 * nq:]
    cos, sin = pos

    def rot(t, scale):
        out = []
        for h in range(RET_HEADS):
            x1, x2 = t[2 * h], t[2 * h + 1]
            o1, o2 = x1 * cos - x2 * sin, x2 * cos + x1 * sin
            out += [o1, o2] if scale is None else [o1 * scale, o2 * scale]
        return out

    return [rot(q, None), rot(k, RET_QK ** -0.5), list(v)]


def _fn_ret_gate(rows, pos, consts):
    o, g = rows
    (gn,) = consts
    out = []
    for h in range(RET_HEADS):
        r = lax.rsqrt(jnp.mean(o[h] * o[h], axis=-1, keepdims=True) + RMS_EPS)
        out.append((o[h] * r * gn[h]) * (g[h] * _sigmoid(g[h])))
    return [out]


def _fn_mla_lat(rows, pos, consts):
    (p,) = rows
    gq, gkv = consts
    nq, nkv = MLA_Q_RANK // LANES, MLA_KV_RANK // LANES
    cq, ckv, kr = p[:nq], p[nq:nq + nkv], p[nq + nkv]
    rq = lax.rsqrt(_ssq(cq) / MLA_Q_RANK + RMS_EPS)
    rkv = lax.rsqrt(_ssq(ckv) / MLA_KV_RANK + RMS_EPS)
    return [[t * rq * g for t, g in zip(cq, gq)], [t * rkv * g for t, g in zip(ckv, gkv)], [kr]]


def _swap32_impl(x):
    lane = lax.broadcasted_iota(jnp.int32, x.shape, 1)
    up, down = pltpu.roll(x, LANES - 32, 1), pltpu.roll(x, 32, 1)
    return jnp.where(lane < 32, up, jnp.where(lane < 64, down, 0.0))


@jax.custom_vjp
def _swap32(x):
    return _swap32_impl(x)


_swap32.defvjp(lambda x: (_swap32_impl(x), None), lambda _, g: (_swap32_impl(g),))


def _fn_mla_heads(rows, pos, consts):
    qf, kvf, (kr,) = rows
    cos, sin = pos
    gq, gk = consts
    q_out, k_out, v_out = [], [], []
    for h in range(MLA_HEADS):
        q0, q1 = qf[2 * h], qf[2 * h + 1]
        r = lax.rsqrt(_ssq([q0, q1]) / MLA_QK + RMS_EPS)
        a0, a1 = q0 * r * gq[0], q1 * r * gq[1]
        a1 = a1 * cos + _swap32(a1) * sin
        q_out += [a0 * (MLA_QK ** -0.5), a1 * (MLA_QK ** -0.5)]
        k0 = kvf[2 * h]
        r = lax.rsqrt(_ssq([k0, kr]) / MLA_QK + RMS_EPS)
        b0, b1 = k0 * r * gk[0], kr * r * gk[1]
        k_out += [b0, b1 * cos + _swap32(b1) * sin]
        v_out.append(kvf[2 * h + 1])
    return [q_out, k_out, v_out]


def _shift_down(x, n):
    row = lax.broadcasted_iota(jnp.int32, x.shape, 0)
    return jnp.where(row >= n, pltpu.roll(x, n, 0), 0.0)


def _shift_up(x, n):
    rows = x.shape[0]
    row = lax.broadcasted_iota(jnp.int32, x.shape, 0)
    return jnp.where(row < rows - n, pltpu.roll(x, rows - n, 0), 0.0)


def _conv_blocks(S):
    cb = 256
    return cb, FFN_DIM // cb


def _conv_fwd(ag, w8, B, S, name):
    cb, ncb = _conv_blocks(S)

    def body(a_ref, g_ref, w_ref, u_ref):
        g = g_ref[...]
        w = w_ref[...]
        gc = w[0:1] * _shift_down(g, 2) + w[1:2] * _shift_down(g, 1) + w[2:3] * g + w[3:4]
        u_ref[...] = (a_ref[...] * (gc * _sigmoid(gc))).astype(u_ref.dtype)

    return pl.pallas_call(
        body, name=name, grid=(ncb, B),
        in_specs=[pl.BlockSpec((S, cb), lambda j, b: (b, j)),
                  pl.BlockSpec((S, cb), lambda j, b: (b, ncb + j)),
                  pl.BlockSpec((8, cb), lambda j, b: (0, j))],
        out_specs=pl.BlockSpec((S, cb), lambda j, b: (b, j)),
        out_shape=jax.ShapeDtypeStruct((B * S, FFN_DIM), BF16),
        compiler_params=_params(("parallel", "parallel")),
    )(ag, ag, w8)


def _conv_bwd(ag, w8, du, B, S, name):
    cb, ncb = _conv_blocks(S)

    def body(a_ref, g_ref, w_ref, du_ref, da_ref, dg_ref, dw_ref):
        g = g_ref[...]
        w = w_ref[...]
        g1, g2 = _shift_down(g, 1), _shift_down(g, 2)
        gc = w[0:1] * g2 + w[1:2] * g1 + w[2:3] * g + w[3:4]
        sg = _sigmoid(gc)
        du_v = du_ref[...]
        da_ref[...] = du_v * (gc * sg)
        dgc = du_v * a_ref[...] * (sg * (1.0 + gc * (1.0 - sg)))
        dg_ref[...] = w[2:3] * dgc + w[1:2] * _shift_up(dgc, 1) + w[0:1] * _shift_up(dgc, 2)
        part = jnp.concatenate([
            jnp.sum(dgc * g2, axis=0, keepdims=True), jnp.sum(dgc * g1, axis=0, keepdims=True),
            jnp.sum(dgc * g, axis=0, keepdims=True), jnp.sum(dgc, axis=0, keepdims=True),
            jnp.zeros((4, cb), F32)], axis=0)

        @pl.when(pl.program_id(1) == 0)
        def _():
            dw_ref[...] = part

        @pl.when(pl.program_id(1) > 0)
        def _():
            dw_ref[...] += part

    blk = lambda j, b: (b, j)
    return pl.pallas_call(
        body, name=name, grid=(ncb, B),
        in_specs=[pl.BlockSpec((S, cb), blk),
                  pl.BlockSpec((S, cb), lambda j, b: (b, ncb + j)),
                  pl.BlockSpec((8, cb), lambda j, b: (0, j)),
                  pl.BlockSpec((S, cb), blk)],
        out_specs=[pl.BlockSpec((S, cb), blk), pl.BlockSpec((S, cb), blk),
                   pl.BlockSpec((8, cb), lambda j, b: (0, j))],
        out_shape=[jax.ShapeDtypeStruct((B * S, FFN_DIM), F32), jax.ShapeDtypeStruct((B * S, FFN_DIM), F32),
                   jax.ShapeDtypeStruct((8, FFN_DIM), F32)],
        compiler_params=_params(("parallel", "arbitrary")),
    )(ag, ag, w8, du)


_NT = (((1,), (1,)), ((), ()))
_NN = (((1,), (0,)), ((), ()))
_TN = (((0,), (0,)), ((), ()))


def _dot(a, b, dn):
    return lax.dot_general(a.astype(MXU_DTYPE), b.astype(MXU_DTYPE), dn, preferred_element_type=F32)


def _rel_and_mask():
    il = lax.broadcasted_iota(jnp.int32, (ATT_BLOCK, ATT_BLOCK), 0)
    jl = lax.broadcasted_iota(jnp.int32, (ATT_BLOCK, ATT_BLOCK), 1)
    return (il - jl).astype(F32), (jl // CHUNK) <= (il // CHUNK)


def _rows(i):
    return pl.ds(pl.multiple_of(i * ATT_BLOCK, ATT_BLOCK), ATT_BLOCK)


def _mla_attn_fwd(q, k, v, B, S):
    H, nq = MLA_HEADS, S // ATT_BLOCK

    def body(q_ref, k_ref, v_ref, o_ref, lse_ref):
        _, mask = _rel_and_mask()

        def qblock(i, _):
            qi = q_ref[_rows(i), :]

            def kv(j, carry, diag):
                m, l, acc = carry
                s = _dot(qi, k_ref[_rows(j), :], _NT)
                if diag:
                    s = jnp.where(mask, s, MASK_VALUE)
                m2 = jnp.maximum(m, jnp.max(s, axis=-1, keepdims=True))
                alpha = jnp.exp(m - m2)
                p = jnp.exp(s - m2)
                l2 = alpha * l + jnp.sum(p, axis=-1, keepdims=True)
                return m2, l2, alpha * acc + _dot(p, v_ref[_rows(j), :], _NN)

            init = (jnp.full((ATT_BLOCK, 1), MASK_VALUE, F32), jnp.zeros((ATT_BLOCK, 1), F32),
                    jnp.zeros((ATT_BLOCK, MLA_V), F32))
            carry = lax.fori_loop(0, i, lambda j, c: kv(j, c, False), init)
            m, l, acc = kv(i, carry, True)
            o_ref[_rows(i), :] = acc / l
            lse_ref[0, _rows(i), :] = m + jnp.log(l)
            return 0

        lax.fori_loop(0, nq, qblock, 0)

    return pl.pallas_call(
        body, name="mla_attn_fwd", grid=(B, H),
        in_specs=[pl.BlockSpec((S, MLA_PAD), lambda b, h: (b, h)),
                  pl.BlockSpec((S, MLA_PAD), lambda b, h: (b, h)),
                  pl.BlockSpec((S, MLA_V), lambda b, h: (b, h))],
        out_specs=[pl.BlockSpec((S, MLA_V), lambda b, h: (b, h)),
                   pl.BlockSpec((1, S, 1), lambda b, h: (b * H + h, 0, 0))],
        out_shape=[jax.ShapeDtypeStruct((B * S, H * MLA_V), F32), jax.ShapeDtypeStruct((B * H, S, 1), F32)],
        compiler_params=_params(("parallel", "parallel")),
    )(q, k, v)


def _mla_attn_bwd(q, k, v, o, do, lse, B, S):
    H, nq = MLA_HEADS, S // ATT_BLOCK

    def body(q_ref, k_ref, v_ref, o_ref, do_ref, lse_ref, dq_ref, dk_ref, dv_ref, acc_ref):
        _, mask = _rel_and_mask()
        dk_ref[...] = jnp.zeros(dk_ref.shape, F32)
        dv_ref[...] = jnp.zeros(dv_ref.shape, F32)

        def qblock(i, _):
            qi = q_ref[_rows(i), :]
            doi = do_ref[_rows(i), :]
            delta = jnp.sum(doi * o_ref[_rows(i), :], axis=-1, keepdims=True)
            lse_i = lse_ref[0, _rows(i), :]
            doi = doi.astype(MXU_DTYPE)
            acc_ref[...] = jnp.zeros(acc_ref.shape, F32)

            def kv(j, diag):
                kj = k_ref[_rows(j), :]
                p = jnp.exp(_dot(qi, kj, _NT) - lse_i)
                if diag:
                    p = jnp.where(mask, p, 0.0)
                ds = (p * (_dot(doi, v_ref[_rows(j), :], _NT) - delta)).astype(MXU_DTYPE)
                acc_ref[...] += _dot(ds, kj, _NN)
                dk_ref[_rows(j), :] += _dot(ds, qi, _TN)
                dv_ref[_rows(j), :] += _dot(p, doi, _TN)

            def off(j, c):
                kv(j, False)
                return c

            lax.fori_loop(0, i, off, 0)
            kv(i, True)
            dq_ref[_rows(i), :] = acc_ref[...]
            return 0

        lax.fori_loop(0, nq, qblock, 0)

    qk_spec = pl.BlockSpec((S, MLA_PAD), lambda b, h: (b, h))
    v_spec = pl.BlockSpec((S, MLA_V), lambda b, h: (b, h))
    return pl.pallas_call(
        body, name="mla_attn_bwd", grid=(B, H),
        in_specs=[qk_spec, qk_spec, v_spec, v_spec, v_spec,
                  pl.BlockSpec((1, S, 1), lambda b, h: (b * H + h, 0, 0))],
        out_specs=[qk_spec, qk_spec, v_spec],
        out_shape=[jax.ShapeDtypeStruct((B * S, H * MLA_PAD), F32), jax.ShapeDtypeStruct((B * S, H * MLA_PAD), F32),
                   jax.ShapeDtypeStruct((B * S, H * MLA_V), F32)],
        scratch_shapes=[pltpu.VMEM((ATT_BLOCK, MLA_PAD), F32)],
        compiler_params=_params(("parallel", "parallel")),
    )(q, k, v, o, do, lse)


def _ret_log_gamma():
    lg = np.log1p(-np.exp2(RET_GAMMA_BASE - np.arange(RET_HEADS, dtype=np.float32))).astype(np.float32)
    return jnp.asarray(np.broadcast_to(lg[:, None, None], (RET_HEADS, 8, LANES)).copy())


def _ret_decay(lg, rel, mask, steps):
    if steps is None:
        return jnp.where(mask, jnp.exp(lg * jnp.abs(rel)), 0.0)
    return jnp.exp(lg * (rel + (steps * ATT_BLOCK).astype(F32)))


def _ret_attn_fwd(q, k, v, B, S):
    H, nq = RET_HEADS, S // ATT_BLOCK

    def body(lg_ref, q_ref, k_ref, v_ref, o_ref, acc_ref):
        rel, mask = _rel_and_mask()
        lg = lg_ref[0, 0:1, 0:1]

        def qblock(i, _):
            qi = q_ref[_rows(i), :]
            acc_ref[...] = jnp.zeros(acc_ref.shape, F32)

            def kv(j, steps):
                a = _dot(qi, k_ref[_rows(j), :], _NT) * _ret_decay(lg, rel, mask, steps)
                acc_ref[...] += _dot(a, v_ref[_rows(j), :], _NN)

            def off(j, c):
                kv(j, i - j)
                return c

            lax.fori_loop(0, i, off, 0)
            kv(i, None)
            o_ref[_rows(i), :] = acc_ref[...]
            return 0

        lax.fori_loop(0, nq, qblock, 0)

    qk_spec = pl.BlockSpec((S, RET_QK), lambda b, h: (b, h))
    v_spec = pl.BlockSpec((S, RET_V), lambda b, h: (b, h))
    return pl.pallas_call(
        body, name="ret_attn_fwd", grid=(B, H),
        in_specs=[pl.BlockSpec((1, 8, LANES), lambda b, h: (h, 0, 0)), qk_spec, qk_spec, v_spec],
        out_specs=v_spec,
        out_shape=jax.ShapeDtypeStruct((B * S, H * RET_V), F32),
        scratch_shapes=[pltpu.VMEM((ATT_BLOCK, RET_V), F32)],
        compiler_params=_params(("parallel", "parallel")),
    )(_ret_log_gamma(), q, k, v)


def _ret_attn_bwd(q, k, v, do, B, S):
    H, nq = RET_HEADS, S // ATT_BLOCK

    def body(lg_ref, q_ref, k_ref, v_ref, do_ref, dq_ref, dk_ref, dv_ref, acc_ref):
        rel, mask = _rel_and_mask()
        lg = lg_ref[0, 0:1, 0:1]
        dk_ref[...] = jnp.zeros(dk_ref.shape, F32)
        dv_ref[...] = jnp.zeros(dv_ref.shape, F32)

        def qblock(i, _):
            qi = q_ref[_rows(i), :]
            doi = do_ref[_rows(i), :].astype(MXU_DTYPE)
            acc_ref[...] = jnp.zeros(acc_ref.shape, F32)

            def kv(j, steps):
                kj = k_ref[_rows(j), :]
                dec = _ret_decay(lg, rel, mask, steps)
                a = _dot(qi, kj, _NT) * dec
                da = (_dot(doi, v_ref[_rows(j), :], _NT) * dec).astype(MXU_DTYPE)
                acc_ref[...] += _dot(da, kj, _NN)
                dk_ref[_rows(j), :] += _dot(da, qi, _TN)
                dv_ref[_rows(j), :] += _dot(a, doi, _TN)

            def off(j, c):
                kv(j, i - j)
                return c

            lax.fori_loop(0, i, off, 0)
            kv(i, None)
            dq_ref[_rows(i), :] = acc_ref[...]
            return 0

        lax.fori_loop(0, nq, qblock, 0)

    qk_spec = pl.BlockSpec((S, RET_QK), lambda b, h: (b, h))
    v_spec = pl.BlockSpec((S, RET_V), lambda b, h: (b, h))
    return pl.pallas_call(
        body, name="ret_attn_bwd", grid=(B, H),
        in_specs=[pl.BlockSpec((1, 8, LANES), lambda b, h: (h, 0, 0)), qk_spec, qk_spec, v_spec, v_spec],
        out_specs=[qk_spec, qk_spec, v_spec],
        out_shape=[jax.ShapeDtypeStruct((B * S, H * RET_QK), F32), jax.ShapeDtypeStruct((B * S, H * RET_QK), F32),
                   jax.ShapeDtypeStruct((B * S, H * RET_V), F32)],
        scratch_shapes=[pltpu.VMEM((ATT_BLOCK, RET_QK), F32)],
        compiler_params=_params(("parallel", "parallel")),
    )(_ret_log_gamma(), q, k, v, do)


def _loss_head(y, target, bm=512):
    T, D = y.shape
    bm = _pick(T, bm)

    def body(y_ref, t_ref, dy_ref, l_ref):
        err = y_ref[...] - t_ref[...]
        dy_ref[...] = err / D
        part = jnp.full((8, LANES), 0.5 * jnp.sum(jnp.mean(err * err, axis=-1)), F32)

        @pl.when(pl.program_id(0) == 0)
        def _():
            l_ref[...] = part

        @pl.when(pl.program_id(0) > 0)
        def _():
            l_ref[...] += part

    blk = pl.BlockSpec((bm, D), lambda i: (i, 0))
    dy, l = pl.pallas_call(
        body, name="loss_head", grid=(T // bm,),
        in_specs=[blk, blk], out_specs=[blk, pl.BlockSpec((8, LANES), lambda i: (0, 0))],
        out_shape=[jax.ShapeDtypeStruct((T, D), F32), jax.ShapeDtypeStruct((8, LANES), F32)],
        compiler_params=_params(("arbitrary",)),
    )(y, target)
    return dy, l[0, 0]


def _adamw(w, g, m, v, name):
    R, C = w.shape
    br = R if R * C * 4 <= 2 ** 21 else _pick_rows(R, max(8, (2 ** 21) // (C * 4)))

    def body(w_ref, g_ref, m_ref, v_ref, d_ref, mo_ref, vo_ref):
        g_v = g_ref[...]
        m_v = ADAM_B1 * m_ref[...] + (1.0 - ADAM_B1) * g_v
        v_v = ADAM_B2 * v_ref[...] + (1.0 - ADAM_B2) * (g_v * g_v)
        m_hat = m_v / (1.0 - ADAM_B1 ** ADAM_STEP)
        v_hat = v_v / (1.0 - ADAM_B2 ** ADAM_STEP)
        d_ref[...] = -ADAM_LR * (m_hat / (jnp.sqrt(v_hat) + ADAM_EPS) + ADAM_WD * w_ref[...])
        mo_ref[...] = m_v
        vo_ref[...] = v_v

    blk = pl.BlockSpec((br, C), lambda i: (i, 0))
    return pl.pallas_call(
        body, name=name, grid=(R // br,),
        in_specs=[blk] * 4, out_specs=[blk] * 3,
        out_shape=[jax.ShapeDtypeStruct((R, C), F32)] * 3,
        compiler_params=_params(("parallel",)),
    )(w, g, m, v)


def _pick_rows(R, target):
    best = None
    for d in range(8, min(R, target) + 1, 8):
        if R % d == 0:
            best = d
    assert best is not None, (R, target)
    return best


def _position():
    return lax.axis_index("x"), lax.axis_index("y"), lax.axis_index("c")


HBM_SPEC = pl.BlockSpec(memory_space=pltpu.HBM)


def _other_chips(x, y):
    return [(1 - x, y), (x, 1 - y), (1 - x, 1 - y)]


def _all_gather_weights(bigs, small):
    nb = len(bigs)

    def body(*refs):
        big_refs, small_ref = refs[:nb], refs[nb]
        obig, osmall = refs[nb + 1:2 * nb + 1], refs[2 * nb + 1]
        ici_send, ici_recv, d2d_send, d2d_recv, sm_send, sm_recv, local_sems = refs[2 * nb + 2:]
        x, y, c = _position()
        me = 2 * x + y
        chips = _other_chips(x, y)
        local = [pltpu.make_async_copy(big_refs[n], obig[n].at[me], local_sems.at[n]) for n in range(nb)]
        local.append(pltpu.make_async_copy(small_ref, osmall.at[me], local_sems.at[nb]))
        for cp in local:
            cp.start()

        def rows(n, half):
            rh = bigs[n].shape[0] // 2
            return pl.ds(half * rh, rh)

        def over_ici(n, j, slot, from_shard):
            px, py = chips[j]
            dst = obig[n].at[slot, rows(n, c)]
            return pltpu.make_async_remote_copy(
                src_ref=big_refs[n].at[rows(n, c)] if from_shard else dst, dst_ref=dst,
                send_sem=ici_send.at[3 * n + j], recv_sem=ici_recv.at[3 * n + j],
                device_id=(px, py, c), device_id_type=MESH)

        def over_d2d(n, j, half):
            px, py = chips[j]
            part = obig[n].at[2 * px + py, rows(n, half)]
            return pltpu.make_async_remote_copy(
                src_ref=part, dst_ref=part, send_sem=d2d_send.at[3 * n + j], recv_sem=d2d_recv.at[3 * n + j],
                device_id=(x, y, 1 - c), device_id_type=MESH)

        def small_copy(j, slot):
            px, py = chips[j]
            return pltpu.make_async_remote_copy(
                src_ref=small_ref, dst_ref=osmall.at[slot], send_sem=sm_send.at[j], recv_sem=sm_recv.at[j],
                device_id=(px, py, c), device_id_type=MESH)

        sends = [over_ici(n, j, me, True) for n in range(nb) for j in range(3)]
        sends += [small_copy(j, me) for j in range(3)]
        for cp in sends:
            cp.start()
        passed = []
        for n in range(nb):
            for j, (px, py) in enumerate(chips):
                over_ici(n, j, 2 * px + py, False).wait_recv()
                fwd = over_d2d(n, j, c)
                fwd.start()
                passed.append(fwd)
        for n in range(nb):
            for j in range(3):
                over_d2d(n, j, 1 - c).wait_recv()
        for j, (px, py) in enumerate(chips):
            small_copy(j, 2 * px + py).wait_recv()
        for cp in sends + passed:
            cp.wait_send()
        for cp in local:
            cp.wait()

    dma = pltpu.SemaphoreType.DMA
    return pl.pallas_call(
        body, name="weights_all_gather",
        in_specs=[HBM_SPEC] * (nb + 1), out_specs=[HBM_SPEC] * (nb + 1),
        out_shape=[jax.ShapeDtypeStruct((N_SHARD,) + b.shape, b.dtype) for b in bigs]
        + [jax.ShapeDtypeStruct((N_SHARD,) + small.shape, small.dtype)],
        scratch_shapes=[dma((3 * nb,)), dma((3 * nb,)), dma((3 * nb,)), dma((3 * nb,)), dma((3,)), dma((3,)),
                        dma((nb + 1,))],
    )(*bigs, small)


def _sibling_send_halves(gs):
    n = len(gs)

    def body(*refs):
        g_refs, o_refs = refs[:n], refs[n:2 * n]
        send_sems, recv_sems = refs[2 * n:]
        x, y, c = _position()
        copies = []
        for a in range(n):
            rh = gs[a].shape[1] // 2
            copies.append(pltpu.make_async_remote_copy(
                src_ref=g_refs[a].at[:, pl.ds((1 - c) * rh, rh)], dst_ref=o_refs[a], send_sem=send_sems.at[a],
                recv_sem=recv_sems.at[a], device_id=(x, y, 1 - c), device_id_type=MESH))
        for cp in copies:
            cp.start()
        for cp in copies:
            cp.wait()

    return pl.pallas_call(
        body, name="grads_sibling_halves",
        in_specs=[HBM_SPEC] * n, out_specs=[HBM_SPEC] * n,
        out_shape=[jax.ShapeDtypeStruct((g.shape[0], g.shape[1] // 2, g.shape[2]), g.dtype) for g in gs],
        scratch_shapes=[pltpu.SemaphoreType.DMA((n,)), pltpu.SemaphoreType.DMA((n,))],
    )(*gs)


def _chip_partial(g, got, name):
    n, rh, cols = got.shape
    br = _pick_rows(rh, 256)
    nrb = rh // br
    c_idx = lax.axis_index("c").astype(jnp.int32).reshape((1,))

    def body(c_ref, g_ref, got_ref, o_ref):
        o_ref[...] = (g_ref[...] + got_ref[...]).astype(o_ref.dtype)

    return pl.pallas_call(
        body, name=name,
        grid_spec=pltpu.PrefetchScalarGridSpec(
            num_scalar_prefetch=1, grid=(n, nrb),
            in_specs=[pl.BlockSpec((None, br, cols), lambda s, r, c_ref: (s, c_ref[0] * nrb + r, 0)),
                      pl.BlockSpec((None, br, cols), lambda s, r, c_ref: (s, r, 0))],
            out_specs=pl.BlockSpec((None, br, cols), lambda s, r, c_ref: (s, r, 0))),
        out_shape=jax.ShapeDtypeStruct((n, rh, cols), BF16),
        compiler_params=_params(("parallel", "parallel")),
    )(c_idx, g, got)


def _exchange_partials(ps):
    n = len(ps)

    def body(*refs):
        p_refs, o_refs = refs[:n], refs[n:2 * n]
        send_sems, recv_sems = refs[2 * n:]
        x, y, c = _position()
        copies = [pltpu.make_async_remote_copy(
            src_ref=p_refs[a].at[2 * px + py], dst_ref=o_refs[a].at[j], send_sem=send_sems.at[3 * a + j],
            recv_sem=recv_sems.at[3 * a + j], device_id=(px, py, c), device_id_type=MESH)
            for a in range(n) for j, (px, py) in enumerate(_other_chips(x, y))]
        for cp in copies:
            cp.start()
        for cp in copies:
            cp.wait()

    return pl.pallas_call(
        body, name="grads_exchange_partials",
        in_specs=[HBM_SPEC] * n, out_specs=[HBM_SPEC] * n,
        out_shape=[jax.ShapeDtypeStruct((3,) + p.shape[1:], p.dtype) for p in ps],
        scratch_shapes=[pltpu.SemaphoreType.DMA((3 * n,)), pltpu.SemaphoreType.DMA((3 * n,))],
    )(*ps)


def _sum_partials(p, got, name):
    _, rh, cols = p.shape
    br = _pick_rows(rh, 256)
    me = (2 * lax.axis_index("x") + lax.axis_index("y")).astype(jnp.int32).reshape((1,))

    def body(me_ref, p_ref, got_ref, o_ref):
        acc = p_ref[...].astype(F32)
        for j in range(3):
            acc = acc + got_ref[j].astype(F32)
        o_ref[...] = acc

    return pl.pallas_call(
        body, name=name,
        grid_spec=pltpu.PrefetchScalarGridSpec(
            num_scalar_prefetch=1, grid=(rh // br,),
            in_specs=[pl.BlockSpec((None, br, cols), lambda r, me_ref: (me_ref[0], r, 0)),
                      pl.BlockSpec((3, br, cols), lambda r, me_ref: (0, r, 0))],
            out_specs=pl.BlockSpec((br, cols), lambda r, me_ref: (r, 0))),
        out_shape=jax.ShapeDtypeStruct((rh, cols), F32),
        compiler_params=_params(("parallel",)),
    )(me, p, got)


def _sibling_share(halves):
    n = len(halves)

    def body(*refs):
        h_refs, o_refs = refs[:n], refs[n:2 * n]
        send_sems, recv_sems, local_sems = refs[2 * n:]
        x, y, c = _position()

        def copy(a, half):
            return pltpu.make_async_remote_copy(
                src_ref=h_refs[a], dst_ref=o_refs[a].at[half], send_sem=send_sems.at[a], recv_sem=recv_sems.at[a],
                device_id=(x, y, 1 - c), device_id_type=MESH)

        local = [pltpu.make_async_copy(h_refs[a], o_refs[a].at[c], local_sems.at[a]) for a in range(n)]
        sends = [copy(a, c) for a in range(n)]
        for cp in local + sends:
            cp.start()
        for a in range(n):
            copy(a, 1 - c).wait_recv()
        for cp in sends:
            cp.wait_send()
        for cp in local:
            cp.wait()

    dma = pltpu.SemaphoreType.DMA
    return pl.pallas_call(
        body, name="grads_sibling_share",
        in_specs=[HBM_SPEC] * n, out_specs=[HBM_SPEC] * n,
        out_shape=[jax.ShapeDtypeStruct((2,) + h.shape, h.dtype) for h in halves],
        scratch_shapes=[dma((n,)), dma((n,)), dma((n,))],
    )(*halves)


def _all_reduce_small(v):
    R, cols = v.shape

    def body(v_ref, o_ref, buf_ref, send_sems, recv_sems):
        x, y, c = _position()
        me = 4 * x + 2 * y + c
        buf_ref[me] = v_ref[...]
        sends = []
        for k in range(1, N_DEV):
            px = 1 - x if k & 4 else x
            py = 1 - y if k & 2 else y
            pc = 1 - c if k & 1 else c
            sends.append(pltpu.make_async_remote_copy(
                src_ref=v_ref, dst_ref=buf_ref.at[me], send_sem=send_sems.at[k - 1], recv_sem=recv_sems.at[k - 1],
                device_id=(px, py, pc), device_id_type=MESH))
        for cp in sends:
            cp.start()
        for k in range(1, N_DEV):
            px = 1 - x if k & 4 else x
            py = 1 - y if k & 2 else y
            pc = 1 - c if k & 1 else c
            pltpu.make_async_remote_copy(
                src_ref=v_ref, dst_ref=buf_ref.at[4 * px + 2 * py + pc], send_sem=send_sems.at[k - 1],
                recv_sem=recv_sems.at[k - 1], device_id=(px, py, pc), device_id_type=MESH).wait_recv()
        for cp in sends:
            cp.wait_send()
        acc = buf_ref[0]
        for d in range(1, N_DEV):
            acc = acc + buf_ref[d]
        o_ref[...] = acc

    return pl.pallas_call(
        body, name="small_grads_all_reduce",
        in_specs=[pl.BlockSpec(memory_space=pltpu.VMEM)], out_specs=pl.BlockSpec(memory_space=pltpu.VMEM),
        out_shape=jax.ShapeDtypeStruct((R, cols), F32),
        scratch_shapes=[pltpu.VMEM((N_DEV, R, cols), F32), pltpu.SemaphoreType.DMA((N_DEV - 1,)),
                        pltpu.SemaphoreType.DMA((N_DEV - 1,))],
    )(v)


def _rope_tables(S, half, width):
    inv_freq = ROPE_THETA ** (-jnp.arange(half, dtype=F32) / half)
    ang = jnp.arange(S).astype(F32)[:, None] * inv_freq[None, :]
    return jnp.cos(ang), jnp.sin(ang)


def _local_step(x, target, w, B, S):
    T = B * S
    D = D_MODEL
    bm = 256
    full = lambda a, wd, tile=None: (a, wd, 0, tile or wd)
    g = {}

    cos_r, sin_r = _rope_tables(S, RET_QK // 2, LANES)
    cos_m, sin_m = _rope_tables(S, MLA_ROPE // 2, LANES)
    zeros64 = jnp.zeros((S, 64), F32)
    cos_m = jnp.concatenate([cos_m, cos_m, zeros64], axis=1)
    sin_m = jnp.concatenate([-sin_m, sin_m, zeros64], axis=1)

    def ffn_fwd(xin, i):
        norm = w["ffn_norm"][i:i + 1]
        (h,) = _rowwise_fwd(_fn_rms, f"ffn{i}_norm", [full(xin, D)], [], [(norm, D)], [(D, D, BF16)], bm, S)
        ag = _mm(h, w["ffn_w_in"][i], "nn", F32, f"ffn{i}_in", bn=1408)
        u = _conv_fwd(ag, w["ffn_conv8"][i], B, S, f"ffn{i}_conv")
        xout = _mm(u, w["ffn_w_out"][i], "nn", F32, f"ffn{i}_out", residual=xin, bk=1408)
        return xout, (xin, norm, h, ag, u)

    def ffn_bwd(dxout, saved, i):
        xin, norm, h, ag, u = saved
        du = _mm(dxout, w["ffn_w_out"][i], "nt", F32, f"ffn{i}_out_dx", bn=1408)
        g_w_out = _mm(u, dxout, "tn", F32, f"ffn{i}_out_dw", bm=1408)
        da, dg, dw8 = _conv_bwd(ag, w["ffn_conv8"][i], du, B, S, f"ffn{i}_conv_bwd")
        dh = _mm([da, dg], w["ffn_w_in"][i], "nt", F32, f"ffn{i}_in_dx", bk=1408)
        g_w_in = _mm(h, [da, dg], "tn", F32, f"ffn{i}_in_dw", bn=1408, out_slots=N_SHARD)
        (dxin,), (g_norm,) = _rowwise_bwd(_fn_rms, f"ffn{i}_norm_bwd", [full(xin, D)], [], [(norm, D)],
                                          [(dh, D)], bm, S, adds={0: dxout})
        return dxin, (g_norm, g_w_in, dw8, g_w_out)

    (h0,) = _rowwise_fwd(_fn_rms, "ret_norm", [full(x, D)], [], [(w["ret_norm"], D)], [(D, D, BF16)], bm, S)
    proj = _mm(h0, w["ret_w_in"], "nn", F32, "ret_in")
    HQ, HV = RET_HEADS * RET_QK, RET_HEADS * RET_V
    rope_rows = [(proj, 2 * HQ + HV, 0, LANES)]
    q_r, k_r, v_r = _rowwise_fwd(_fn_ret_rope, "ret_rope", rope_rows, [cos_r, sin_r], [],
                                 [(HQ, LANES, BF16), (HQ, LANES, BF16), (HV, LANES, BF16)], bm, S)
    ret_o = _ret_attn_fwd(q_r, k_r, v_r, B, S)
    gate_rows = [full(ret_o, HV, RET_V), (proj, HV, 2, RET_V)]
    (y0,) = _rowwise_fwd(_fn_ret_gate, "ret_gate", gate_rows, [], [(w["ret_gn"], RET_V)], [(HV, RET_V, BF16)], 128, S)
    x1 = _mm(y0, w["ret_w_out"], "nn", F32, "ret_out", residual=x)
    x2, ffn0_saved = ffn_fwd(x1, 0)

    (h2,) = _rowwise_fwd(_fn_rms, "mla_norm", [full(x2, D)], [], [(w["mla_norm"], D)], [(D, D, BF16)], bm, S)
    proj2 = _mm(h2, w["mla_w_in"], "nn", F32, "mla_in")
    lat_consts = [(w["mla_q_norm"], LANES), (w["mla_kv_norm"], LANES)]
    cqn, ckvn, kr = _rowwise_fwd(_fn_mla_lat, "mla_latent_norm", [full(proj2, MLA_IN_PAD, LANES)], [], lat_consts,
                                 [(MLA_Q_RANK, LANES, BF16), (MLA_KV_RANK, LANES, BF16), (LANES, LANES, F32)], bm, S)
    qf = _mm(cqn, w["mla_w_qb"], "nn", F32, "mla_qb")
    kvf = _mm(ckvn, w["mla_w_kvb"], "nn", F32, "mla_kvb")
    HP, HVm = MLA_HEADS * MLA_PAD, MLA_HEADS * MLA_V
    head_rows = [full(qf, HP, LANES), full(kvf, HP, LANES), full(kr, LANES)]
    head_consts = [(w["mla_q_head_norm"], LANES), (w["mla_k_head_norm"], LANES)]
    q_a, k_a, v_a = _rowwise_fwd(_fn_mla_heads, "mla_heads", head_rows, [cos_m, sin_m], head_consts,
                                 [(HP, LANES, BF16), (HP, LANES, BF16), (HVm, LANES, BF16)], bm, S)
    att_o, lse = _mla_attn_fwd(q_a, k_a, v_a, B, S)
    x3 = _mm(att_o, w["mla_w_out"], "nn", F32, "mla_out", residual=x2)
    x4, ffn1_saved = ffn_fwd(x3, 1)

    dy, loss = _loss_head(x4, target)

    dx3, (g_n1, g_in1, dw8_1, g_out1) = ffn_bwd(dy, ffn1_saved, 1)

    d_att_o = _mm(dx3, w["mla_w_out"], "nt", F32, "mla_out_dx")
    g["mla_w_out"] = _mm(att_o, dx3, "tn", F32, "mla_out_dw")
    dq_a, dk_a, dv_a = _mla_attn_bwd(q_a, k_a, v_a, att_o, d_att_o, lse, B, S)
    (dqf, dkvf, dkr), (g["mla_q_head_norm"], g["mla_k_head_norm"]) = _rowwise_bwd(
        _fn_mla_heads, "mla_heads_bwd", head_rows, [cos_m, sin_m], head_consts,
        [(dq_a, LANES), (dk_a, LANES), (dv_a, LANES)], 128, S)
    dcqn = _mm(dqf, w["mla_w_qb"], "nt", F32, "mla_qb_dx")
    g["mla_w_qb"] = _mm(cqn, dqf, "tn", F32, "mla_qb_dw")
    dckvn = _mm(dkvf, w["mla_w_kvb"], "nt", F32, "mla_kvb_dx")
    g["mla_w_kvb"] = _mm(ckvn, dkvf, "tn", F32, "mla_kvb_dw", bn=512, out_slots=N_SHARD)
    (dproj2,), (g["mla_q_norm"], g["mla_kv_norm"]) = _rowwise_bwd(
        _fn_mla_lat, "mla_latent_norm_bwd", [full(proj2, MLA_IN_PAD, LANES)], [], lat_consts,
        [(dcqn, LANES), (dckvn, LANES), (dkr, LANES)], bm, S)
    dh2 = _mm(dproj2, w["mla_w_in"], "nt", F32, "mla_in_dx")
    g["mla_w_in"] = _mm(h2, dproj2, "tn", F32, "mla_in_dw")
    (dx2,), (g["mla_norm"],) = _rowwise_bwd(_fn_rms, "mla_norm_bwd", [full(x2, D)], [], [(w["mla_norm"], D)],
                                            [(dh2, D)], bm, S, adds={0: dx3})

    dx1, (g_n0, g_in0, dw8_0, g_out0) = ffn_bwd(dx2, ffn0_saved, 0)

    dy0 = _mm(dx1, w["ret_w_out"], "nt", F32, "ret_out_dx")
    g["ret_w_out"] = _mm(y0, dx1, "tn", F32, "ret_out_dw")
    (d_ret_o, dgate), (g["ret_gn"],) = _rowwise_bwd(_fn_ret_gate, "ret_gate_bwd", gate_rows, [], [(w["ret_gn"], RET_V)],
                                                    [(dy0, RET_V)], 128, S)
    dq_r, dk_r, dv_r = _ret_attn_bwd(q_r, k_r, v_r, d_ret_o, B, S)
    (dqkv,), _ = _rowwise_bwd(_fn_ret_rope, "ret_rope_bwd", rope_rows, [cos_r, sin_r], [],
                              [(dq_r, LANES), (dk_r, LANES), (dv_r, LANES)], bm, S)
    dh0 = _mm([dqkv, dgate], w["ret_w_in"], "nt", F32, "ret_in_dx", bk=1024)
    g["ret_w_in"] = _mm(h0, [dqkv, dgate], "tn", F32, "ret_in_dw", bn=512, out_slots=N_SHARD)
    (dx,), (g["ret_norm"],) = _rowwise_bwd(_fn_rms, "ret_norm_bwd", [full(x, D)], [], [(w["ret_norm"], D)],
                                           [(dh0, D)], bm, S, adds={0: dx1})

    g["ffn_norm"] = jnp.concatenate([g_n0, g_n1], axis=0)
    g["ffn_w_in"] = [g_in0, g_in1]
    g["ffn_w_out"] = [g_out0, g_out1]
    g["ffn_conv_w"] = jnp.stack([dw8_0[0:3], dw8_1[0:3]])
    g["ffn_conv_b"] = jnp.stack([dw8_0[3], dw8_1[3]])
    return loss, dx, g


_BIG = [("ret_w_in", 2), ("ret_w_out", 1), ("mla_w_in", 1), ("mla_w_qb", 2), ("mla_w_kvb", 2), ("mla_w_out", 1),
        ("ffn_w_in", 2), ("ffn_w_out", 1)]
_SMALL_SHARDED = [("ret_gn", 2), ("mla_norm", 1), ("mla_q_norm", 1), ("mla_kv_norm", 1), ("ffn_conv_w", 2)]
_SMALL_REPLICATED = ["ret_norm", "mla_q_head_norm", "mla_k_head_norm", "ffn_norm", "ffn_conv_b"]
_SMALL_ALL = ["ret_norm", "ret_gn", "mla_norm", "mla_q_norm", "mla_kv_norm", "mla_q_head_norm", "mla_k_head_norm",
              "ffn_norm", "ffn_conv_w", "ffn_conv_b"]


def _to_slots(full, axis):
    shape = full.shape
    split = shape[:axis] + (N_SHARD, shape[axis] // N_SHARD) + shape[axis + 1:]
    return jnp.moveaxis(full.reshape(split), axis, 0).reshape(N_SHARD, -1)


def _from_slots(slots, shard_shape, axis):
    parts = jnp.moveaxis(slots.reshape((N_SHARD,) + tuple(shard_shape)), 0, axis)
    full = shard_shape[:axis] + (N_SHARD * shard_shape[axis],) + shard_shape[axis + 1:]
    return parts.reshape(full)


def _pad_rows(flat, cols, row_unit):
    n, L = flat.shape
    unit = cols * row_unit
    Lp = -(-L // unit) * unit
    if Lp != L:
        flat = jnp.concatenate([flat, jnp.zeros((n, Lp - L), flat.dtype)], axis=1)
    return flat.reshape(n, Lp // cols, cols)


def _pad_heads(a, axis):
    shape = a.shape
    a = a.reshape(shape[:axis] + (MLA_HEADS, MLA_QK) + shape[axis + 1:])
    pad = [(0, 0)] * a.ndim
    pad[axis + 1] = (0, MLA_PAD - MLA_QK)
    return jnp.pad(a, pad).reshape(shape[:axis] + (MLA_HEADS * MLA_PAD,) + shape[axis + 1:])


def _unpad_heads(a, axis):
    shape = a.shape
    a = a.reshape(shape[:axis] + (MLA_HEADS, MLA_PAD) + shape[axis + 1:])
    a = lax.slice_in_dim(a, 0, MLA_QK, axis=axis + 1)
    return a.reshape(shape[:axis] + (MLA_HEADS * MLA_QK,) + shape[axis + 1:])


def kernel(x, ret_norm, ret_w_in, ret_gn, ret_w_out, mla_norm, mla_w_in, mla_q_norm, mla_w_qb, mla_kv_norm, mla_w_kvb, mla_q_head_norm, mla_k_head_norm, mla_w_out, ffn_norm, ffn_w_in, ffn_conv_w, ffn_conv_b, ffn_w_out, loss_target, m_ret_norm, m_ret_w_in, m_ret_gn, m_ret_w_out, m_mla_norm, m_mla_w_in, m_mla_q_norm, m_mla_w_qb, m_mla_kv_norm, m_mla_w_kvb, m_mla_q_head_norm, m_mla_k_head_norm, m_mla_w_out, m_ffn_norm, m_ffn_w_in, m_ffn_conv_w, m_ffn_conv_b, m_ffn_w_out, v_ret_norm, v_ret_w_in, v_ret_gn, v_ret_w_out, v_mla_norm, v_mla_w_in, v_mla_q_norm, v_mla_w_qb, v_mla_kv_norm, v_mla_w_kvb, v_mla_q_head_norm, v_mla_k_head_norm, v_mla_w_out, v_ffn_norm, v_ffn_w_in, v_ffn_conv_w, v_ffn_conv_b, v_ffn_w_out):
    names = ["ret_norm", "ret_w_in", "ret_gn", "ret_w_out", "mla_norm", "mla_w_in", "mla_q_norm", "mla_w_qb",
             "mla_kv_norm", "mla_w_kvb", "mla_q_head_norm", "mla_k_head_norm", "mla_w_out", "ffn_norm", "ffn_w_in",
             "ffn_conv_w", "ffn_conv_b", "ffn_w_out"]
    shard = dict(zip(names, [ret_norm, ret_w_in, ret_gn, ret_w_out, mla_norm, mla_w_in, mla_q_norm, mla_w_qb,
                             mla_kv_norm, mla_w_kvb, mla_q_head_norm, mla_k_head_norm, mla_w_out, ffn_norm, ffn_w_in,
                             ffn_conv_w, ffn_conv_b, ffn_w_out]))
    mom_m = dict(zip(names, [m_ret_norm, m_ret_w_in, m_ret_gn, m_ret_w_out, m_mla_norm, m_mla_w_in, m_mla_q_norm,
                             m_mla_w_qb, m_mla_kv_norm, m_mla_w_kvb, m_mla_q_head_norm, m_mla_k_head_norm, m_mla_w_out,
                             m_ffn_norm, m_ffn_w_in, m_ffn_conv_w, m_ffn_conv_b, m_ffn_w_out]))
    mom_v = dict(zip(names, [v_ret_norm, v_ret_w_in, v_ret_gn, v_ret_w_out, v_mla_norm, v_mla_w_in, v_mla_q_norm,
                             v_mla_w_qb, v_mla_kv_norm, v_mla_w_kvb, v_mla_q_head_norm, v_mla_k_head_norm, v_mla_w_out,
                             v_ffn_norm, v_ffn_w_in, v_ffn_conv_w, v_ffn_conv_b, v_ffn_w_out]))
    B, S, D = x.shape
    T = B * S
    sx, sy = lax.axis_index("x"), lax.axis_index("y")
    me = 2 * sx + sy

    big_names = [n for n, _ in _BIG]
    two_d = lambda a: a.reshape(-1, a.shape[-1])
    small_sizes = [int(np.prod(shard[n].shape)) for n, _ in _SMALL_SHARDED]
    small = jnp.concatenate([shard[n].reshape(1, -1) for n, _ in _SMALL_SHARDED], axis=1)
    small = _pad_rows(small, LANES, 8)[0]
    *gbig, gsmall = _all_gather_weights([two_d(shard[n]).astype(BF16) for n in big_names], small)
    gw = dict(zip(big_names, gbig))
    gsmall = gsmall.reshape(N_SHARD, -1)
    by_cols = lambda a: jnp.moveaxis(a, 0, 1).reshape(a.shape[1], -1)
    by_rows = lambda a: a.reshape(-1, a.shape[-1])
    wfull = {}
    off = 0
    for (n, ax), sz in zip(_SMALL_SHARDED, small_sizes):
        wfull[n] = _from_slots(gsmall[:, off:off + sz], shard[n].shape, ax)
        off += sz
    for n in _SMALL_REPLICATED:
        wfull[n] = shard[n]

    conv8 = jnp.concatenate([wfull["ffn_conv_w"], wfull["ffn_conv_b"][:, None, :],
                             jnp.zeros((2, 4, FFN_DIM), F32)], axis=1)
    ffn_in_rows, ffn_out_rows = shard["ffn_w_in"].shape[1], shard["ffn_w_out"].shape[1]
    w = {
        "ret_norm": wfull["ret_norm"], "ret_w_in": by_cols(gw["ret_w_in"]),
        "ret_gn": wfull["ret_gn"].reshape(1, RET_HEADS * RET_V), "ret_w_out": by_rows(gw["ret_w_out"]),
        "mla_norm": wfull["mla_norm"],
        "mla_w_in": jnp.pad(by_rows(gw["mla_w_in"]), ((0, 0), (0, MLA_IN_PAD - MLA_IN))),
        "mla_q_norm": wfull["mla_q_norm"], "mla_w_qb": _pad_heads(by_cols(gw["mla_w_qb"]), 1),
        "mla_kv_norm": wfull["mla_kv_norm"], "mla_w_kvb": by_cols(gw["mla_w_kvb"]),
        "mla_q_head_norm": jnp.pad(wfull["mla_q_head_norm"], ((0, 0), (0, MLA_PAD - MLA_QK))),
        "mla_k_head_norm": jnp.pad(wfull["mla_k_head_norm"], ((0, 0), (0, MLA_PAD - MLA_QK))),
        "mla_w_out": by_rows(gw["mla_w_out"]), "ffn_norm": wfull["ffn_norm"],
        "ffn_w_in": [by_cols(gw["ffn_w_in"][:, i * ffn_in_rows:(i + 1) * ffn_in_rows]) for i in range(2)],
        "ffn_conv8": conv8,
        "ffn_w_out": [by_rows(gw["ffn_w_out"][:, i * ffn_out_rows:(i + 1) * ffn_out_rows]) for i in range(2)],
    }

    loss_part, dx, gl = _local_step(x.reshape(T, D), loss_target.reshape(T, D), w, B, S)
    loss = lax.psum(loss_part, ("x", "y", "c"))
    gfull = {
        "ret_norm": gl["ret_norm"], "ret_gn": gl["ret_gn"].reshape(1, RET_HEADS, RET_V),
        "mla_norm": gl["mla_norm"], "mla_q_norm": gl["mla_q_norm"], "mla_kv_norm": gl["mla_kv_norm"],
        "mla_q_head_norm": gl["mla_q_head_norm"][:, :MLA_QK], "mla_k_head_norm": gl["mla_k_head_norm"][:, :MLA_QK],
        "ffn_norm": gl["ffn_norm"], "ffn_conv_w": gl["ffn_conv_w"], "ffn_conv_b": gl["ffn_conv_b"],
    }

    slot_rows = lambda a: a.reshape(N_SHARD, -1, a.shape[-1])
    qb_slots = _to_slots(_unpad_heads(gl["mla_w_qb"], 1), 1).reshape((N_SHARD,) + shard["mla_w_qb"].shape[1:])
    gs = [gl["ret_w_in"], slot_rows(gl["ret_w_out"]), slot_rows(gl["mla_w_in"][:, :MLA_IN]), qb_slots,
          gl["mla_w_kvb"], slot_rows(gl["mla_w_out"]), gl["ffn_w_in"][0], gl["ffn_w_in"][1],
          slot_rows(gl["ffn_w_out"][0]), slot_rows(gl["ffn_w_out"][1])]
    from_sibling = _sibling_send_halves(gs)
    partials = [_chip_partial(g_, o_, f"grads_chip_partial_{i}") for i, (g_, o_) in enumerate(zip(gs, from_sibling))]
    from_chips = _exchange_partials(partials)
    halves = [_sum_partials(p_, o_, f"grads_sum_partials_{i}") for i, (p_, o_) in enumerate(zip(partials, from_chips))]
    red = [two_d(r) for r in _sibling_share(halves)]
    grads = {n: red[i].reshape(shard[n].shape) for i, n in enumerate(big_names[:6])}
    grads["ffn_w_in"] = jnp.stack([red[6], red[7]])
    grads["ffn_w_out"] = jnp.stack([red[8], red[9]])

    small_sizes_all = [int(np.prod(gfull[n].shape)) for n in _SMALL_ALL]
    gsm = jnp.concatenate([gfull[n].reshape(1, -1) for n in _SMALL_ALL], axis=1)
    gsm = _all_reduce_small(_pad_rows(gsm, LANES, 8)[0]).reshape(-1)

    sharded_axis = dict(_SMALL_SHARDED)
    off = 0
    for n, sz in zip(_SMALL_ALL, small_sizes_all):
        gn = gsm[off:off + sz].reshape(gfull[n].shape)
        off += sz
        if n in sharded_axis:
            ax = sharded_axis[n]
            width = shard[n].shape[ax]
            gn = lax.dynamic_slice_in_dim(gn, me * width, width, axis=ax)
        grads[n] = gn

    delta, new_m, new_v = {}, {}, {}
    for n, _ in _BIG:
        shp = shard[n].shape
        two_d = lambda a: a.reshape(-1, shp[-1])
        d_, m_, v_ = _adamw(two_d(shard[n]), two_d(grads[n]), two_d(mom_m[n]), two_d(mom_v[n]), f"adamw_{n}")
        delta[n], new_m[n], new_v[n] = d_.reshape(shp), m_.reshape(shp), v_.reshape(shp)
    pack_small = lambda d: _pad_rows(jnp.concatenate([d[n].reshape(1, -1) for n in _SMALL_ALL], axis=1), LANES, 8)[0]
    d_, m_, v_ = _adamw(pack_small(shard), pack_small(grads), pack_small(mom_m), pack_small(mom_v), "adamw_small")
    off = 0
    for n in _SMALL_ALL:
        sz = int(np.prod(shard[n].shape))
        for dst, src in ((delta, d_), (new_m, m_), (new_v, v_)):
            dst[n] = src.reshape(-1)[off:off + sz].reshape(shard[n].shape)
        off += sz

    return (loss, dx.reshape(B, S, D), *[grads[n] for n in names], *[delta[n] for n in names],
            *[new_m[n] for n in names], *[new_v[n] for n in names])
```

```python
import functools
import math

import numpy as np
import jax
import jax.numpy as jnp
from jax import lax
from jax.experimental import pallas as pl
from jax.experimental.pallas import tpu as pltpu

F32 = jnp.float32
BF16 = jnp.bfloat16
MXU_DTYPE = jnp.bfloat16

CHUNK = 64
RMS_EPS = 1e-6
ROPE_THETA = 10000.0
D_MODEL = 1024
RET_HEADS = 4
RET_QK = 256
RET_V = 512
RET_GAMMA_BASE = -5.0
MLA_HEADS = 8
MLA_Q_RANK = 384
MLA_KV_RANK = 256
MLA_NOPE = 128
MLA_ROPE = 64
MLA_V = 128
MLA_QK = MLA_NOPE + MLA_ROPE
MLA_PAD = 256
MLA_IN = MLA_Q_RANK + MLA_KV_RANK + MLA_ROPE
MLA_IN_PAD = MLA_IN + 64
MASK_VALUE = -1e30
FFN_DIM = 2816
ADAM_LR = 0.001
ADAM_B1 = 0.9
ADAM_B2 = 0.999
ADAM_EPS = 1e-08
ADAM_WD = 0.01
ADAM_STEP = 10

LANES = 128
ATT_BLOCK = 256
VMEM_LIMIT = 56 * 2 ** 20
N_SHARD = 4
N_DEV = 8

MESH = pl.DeviceIdType.MESH


def _params(sem=None, **kw):
    return pltpu.CompilerParams(dimension_semantics=sem, vmem_limit_bytes=VMEM_LIMIT, **kw)


def _pick(dim, target):
    if dim <= target:
        return dim
    best = None
    for d in range(LANES, target + 1, LANES):
        if dim % d == 0:
            best = d
    assert best is not None, (dim, target)
    return best


def _mm(a, b, dims, out_dtype, name, residual=None, bm=512, bn=1024, bk=2048, out_slots=None):
    a_parts = list(a) if isinstance(a, (list, tuple)) else [a]
    b_parts = list(b) if isinstance(b, (list, tuple)) else [b]
    if dims == "tn":
        assert len(a_parts) == 1
        K, M = a_parts[0].shape
        N = sum(p.shape[1] for p in b_parts)
        part_widths = [p.shape[1] for p in b_parts]
    else:
        assert len(b_parts) == 1
        M = a_parts[0].shape[0]
        K = sum(p.shape[1] for p in a_parts)
        N = b_parts[0].shape[1 if dims == "nn" else 0]
        part_widths = [p.shape[1] for p in a_parts]
    bm, bn, bk = _pick(M, bm), _pick(N, bn), _pick(K, min(bk, 1024) if dims == "tn" else bk)
    nk = K // bk
    unit = bn if dims == "tn" else bk
    assert all(wd % unit == 0 for wd in part_widths), (name, part_widths, unit)
    bounds = np.cumsum([0] + [wd // unit for wd in part_widths])
    ranges = [(int(lo), int(hi)) for lo, hi in zip(bounds[:-1], bounds[1:])]

    def part_index(idx, lo, hi):
        return jnp.clip(idx - lo, 0, hi - lo - 1)

    if dims == "tn":
        a_specs = [pl.BlockSpec((bk, bm), lambda i, j, k: (k, i))]
        b_specs = [pl.BlockSpec((bk, bn), functools.partial(lambda i, j, k, lo, hi: (k, part_index(j, lo, hi)), lo=lo, hi=hi))
                   for lo, hi in ranges]
        dn = (((0,), (0,)), ((), ()))
    else:
        a_specs = [pl.BlockSpec((bm, bk), functools.partial(lambda i, j, k, lo, hi: (i, part_index(k, lo, hi)), lo=lo, hi=hi))
                   for lo, hi in ranges]
        if dims == "nt":
            b_specs = [pl.BlockSpec((bn, bk), lambda i, j, k: (j, k))]
        else:
            b_specs = [pl.BlockSpec((bk, bn), lambda i, j, k: (k, j))]
        dn = (((1,), (1 if dims == "nt" else 0,)), ((), ()))
    r_spec = pl.BlockSpec((bm, bn), lambda i, j, k: (i, j))
    if out_slots is None:
        o_spec, o_shape = r_spec, (M, N)
    else:
        ns = N // out_slots
        assert ns % bn == 0, (name, ns, bn)
        nbs = ns // bn
        o_spec = pl.BlockSpec((None, bm, bn), lambda i, j, k: (j // nbs, i, j % nbs))
        o_shape = (out_slots, M, ns)
    has_res = residual is not None
    na, nb = len(a_parts), len(b_parts)

    def body(*refs):
        a_refs, b_refs = refs[:na], refs[na:na + nb]
        r_ref = refs[na + nb] if has_res else None
        o_ref = refs[na + nb + has_res]
        acc_ref = refs[na + nb + has_res + 1] if nk > 1 else None
        k = pl.program_id(2)

        def finish(acc):
            if has_res:
                acc = acc + r_ref[...].astype(F32)
            o_ref[...] = acc.astype(out_dtype)

        def compute(a_ref, b_ref):
            p = lax.dot_general(a_ref[...].astype(MXU_DTYPE), b_ref[...].astype(MXU_DTYPE), dn,
                                preferred_element_type=F32)
            if nk == 1:
                finish(p)
                return

            @pl.when(k == 0)
            def _():
                acc_ref[...] = p

            @pl.when(jnp.logical_and(k > 0, k < nk - 1))
            def _():
                acc_ref[...] += p

            @pl.when(k == nk - 1)
            def _():
                finish(acc_ref[...] + p)

        if len(ranges) == 1:
            compute(a_refs[0], b_refs[0])
        else:
            idx = pl.program_id(1) if dims == "tn" else k
            for p, (lo, hi) in enumerate(ranges):
                @pl.when(jnp.logical_and(idx >= lo, idx < hi))
                def _(p=p):
                    compute(a_refs[0 if dims == "tn" else p], b_refs[p if dims == "tn" else 0])

    return pl.pallas_call(
        body, name=name, grid=(M // bm, N // bn, nk),
        in_specs=a_specs + b_specs + ([r_spec] if has_res else []), out_specs=o_spec,
        out_shape=jax.ShapeDtypeStruct(o_shape, out_dtype),
        scratch_shapes=[pltpu.VMEM((bm, bn), F32)] if nk > 1 else [],
        compiler_params=_params(("parallel", "parallel", "arbitrary")),
    )(*a_parts, *b_parts, *((residual,) if has_res else ()))


def _tiles(ref, width, tile):
    return [ref[:, t * tile:(t + 1) * tile].astype(F32) for t in range(width // tile)]


def _row_specs(rows, pos, consts, bm, S):
    npos_blocks = S // bm
    specs = [pl.BlockSpec((bm, w), functools.partial(lambda i, c: (i, c), c=cb)) for (_, w, cb, _) in rows]
    specs += [pl.BlockSpec((bm, p.shape[1]), lambda i: (i % npos_blocks, 0)) for p in pos]
    specs += [pl.BlockSpec(c.shape, lambda i: (0, 0)) for (c, _) in consts]
    return specs


def _rowwise_fwd(fn, name, rows, pos, consts, outs, bm, S):
    T = rows[0][0].shape[0]
    nr, npos, nc = len(rows), len(pos), len(consts)

    def body(*refs):
        row_v = [_tiles(r, w, t) for r, (_, w, _, t) in zip(refs[:nr], rows)]
        pos_v = [r[...] for r in refs[nr:nr + npos]]
        const_v = [_tiles(r, c.shape[1], t) for r, (c, t) in zip(refs[nr + npos:nr + npos + nc], consts)]
        res = fn(row_v, pos_v, const_v)
        for o_ref, tiles, (w, t, dt) in zip(refs[nr + npos + nc:], res, outs):
            for k, v in enumerate(tiles):
                o_ref[:, k * t:(k + 1) * t] = v.astype(dt)

    return pl.pallas_call(
        body, name=name, grid=(T // bm,),
        in_specs=_row_specs(rows, pos, consts, bm, S),
        out_specs=[pl.BlockSpec((bm, w), lambda i: (i, 0)) for (w, _, _) in outs],
        out_shape=[jax.ShapeDtypeStruct((T, w), dt) for (w, _, dt) in outs],
        compiler_params=_params(("parallel",)),
    )(*[r[0] for r in rows], *pos, *[c[0] for c in consts])


def _rowwise_bwd(fn, name, rows, pos, consts, cts, bm, S, adds=None):
    adds = adds or {}
    T = rows[0][0].shape[0]
    nr, npos, nc, nct = len(rows), len(pos), len(consts), len(cts)
    add_idx = sorted(adds)

    def body(*refs):
        it = iter(refs)
        row_refs = [next(it) for _ in range(nr)]
        pos_refs = [next(it) for _ in range(npos)]
        const_refs = [next(it) for _ in range(nc)]
        ct_refs = [next(it) for _ in range(nct)]
        add_refs = {k: next(it) for k in add_idx}
        drow_refs = [next(it) for _ in range(nr)]
        dconst_refs = [next(it) for _ in range(nc)]
        row_v = [_tiles(r, w, t) for r, (_, w, _, t) in zip(row_refs, rows)]
        pos_v = [r[...] for r in pos_refs]
        const_v = [_tiles(r, c.shape[1], t) for r, (c, t) in zip(const_refs, consts)]
        ct_v = [_tiles(r, c.shape[1], t) for r, (c, t) in zip(ct_refs, cts)]
        _, vjp = jax.vjp(lambda rv, cv: fn(rv, pos_v, cv), row_v, const_v)
        drows, dconsts = vjp(ct_v)
        for a, (d_ref, tiles, (_, w, _, t)) in enumerate(zip(drow_refs, drows, rows)):
            for k, v in enumerate(tiles):
                if a in add_refs:
                    v = v + add_refs[a][:, k * t:(k + 1) * t].astype(F32)
                d_ref[:, k * t:(k + 1) * t] = v
        first = pl.program_id(0) == 0
        for d_ref, tiles, (_, t) in zip(dconst_refs, dconsts, consts):
            for k, v in enumerate(tiles):
                @pl.when(first)
                def _(d_ref=d_ref, k=k, t=t, v=v):
                    d_ref[:, k * t:(k + 1) * t] = v

                @pl.when(jnp.logical_not(first))
                def _(d_ref=d_ref, k=k, t=t, v=v):
                    d_ref[:, k * t:(k + 1) * t] += v

    in_specs = _row_specs(rows, pos, consts, bm, S)
    in_specs += [pl.BlockSpec((bm, c.shape[1]), lambda i: (i, 0)) for (c, _) in cts]
    in_specs += [pl.BlockSpec((bm, adds[k].shape[1]), lambda i: (i, 0)) for k in add_idx]
    out_specs = [pl.BlockSpec((bm, w), lambda i: (i, 0)) for (_, w, _, _) in rows]
    out_specs += [pl.BlockSpec(c.shape, lambda i: (0, 0)) for (c, _) in consts]
    out_shape = [jax.ShapeDtypeStruct((T, w), F32) for (_, w, _, _) in rows]
    out_shape += [jax.ShapeDtypeStruct(c.shape, F32) for (c, _) in consts]
    res = pl.pallas_call(
        body, name=name, grid=(T // bm,),
        in_specs=in_specs, out_specs=out_specs, out_shape=out_shape,
        compiler_params=_params(("arbitrary",)),
    )(*[r[0] for r in rows], *pos, *[c[0] for c in consts], *[c[0] for c in cts], *[adds[k] for k in add_idx])
    return res[:nr], res[nr:]


def _ssq(tiles):
    s = jnp.sum(tiles[0] * tiles[0], axis=-1, keepdims=True)
    for t in tiles[1:]:
        s = s + jnp.sum(t * t, axis=-1, keepdims=True)
    return s


def _sigmoid(x):
    return 1.0 / (1.0 + jnp.exp(-x))


def _fn_rms(rows, pos, consts):
    (x,), (g,) = rows[0], consts[0]
    r = lax.rsqrt(jnp.mean(x * x, axis=-1, keepdims=True) + RMS_EPS)
    return [[x * r * g]]


def _fn_ret_rope(rows, pos, consts):
    (qkv,) = rows
    nq = RET_HEADS * RET_QK // LANES
    q, k, v = qkv[:nq], qkv[nq:2 * nq], qkv[2 * nq:]
    cos, sin = pos

    def rot(t, scale):
        out = []
        for h in range(RET_HEADS):
            x1, x2 = t[2 * h], t[2 * h + 1]
            o1, o2 = x1 * cos - x2 * sin, x2 * cos + x1 * sin
            out += [o1, o2] if scale is None else [o1 * scale, o2 * scale]
        return out

    return [rot(q, None), rot(k, RET_QK ** -0.5), list(v)]


def _fn_ret_gate(rows, pos, consts):
    o, g = rows
    (gn,) = consts
    out = []
    for h in range(RET_HEADS):
        r = lax.rsqrt(jnp.mean(o[h] * o[h], axis=-1, keepdims=True) + RMS_EPS)
        out.append((o[h] * r * gn[h]) * (g[h] * _sigmoid(g[h])))
    return [out]


def _fn_mla_lat(rows, pos, consts):
    (p,) = rows
    gq, gkv = consts
    nq, nkv = MLA_Q_RANK // LANES, MLA_KV_RANK // LANES
    cq, ckv, kr = p[:nq], p[nq:nq + nkv], p[nq + nkv]
    rq = lax.rsqrt(_ssq(cq) / MLA_Q_RANK + RMS_EPS)
    rkv = lax.rsqrt(_ssq(ckv) / MLA_KV_RANK + RMS_EPS)
    return [[t * rq * g for t, g in zip(cq, gq)], [t * rkv * g for t, g in zip(ckv, gkv)], [kr]]


def _swap32_impl(x):
    lane = lax.broadcasted_iota(jnp.int32, x.shape, 1)
    up, down = pltpu.roll(x, LANES - 32, 1), pltpu.roll(x, 32, 1)
    return jnp.where(lane < 32, up, jnp.where(lane < 64, down, 0.0))


@jax.custom_vjp
def _swap32(x):
    return _swap32_impl(x)


_swap32.defvjp(lambda x: (_swap32_impl(x), None), lambda _, g: (_swap32_impl(g),))


def _fn_mla_heads(rows, pos, consts):
    qf, kvf, (kr,) = rows
    cos, sin = pos
    gq, gk = consts
    q_out, k_out, v_out = [], [], []
    for h in range(MLA_HEADS):
        q0, q1 = qf[2 * h], qf[2 * h + 1]
        r = lax.rsqrt(_ssq([q0, q1]) / MLA_QK + RMS_EPS)
        a0, a1 = q0 * r * gq[0], q1 * r * gq[1]
        a1 = a1 * cos + _swap32(a1) * sin
        q_out += [a0 * (MLA_QK ** -0.5), a1 * (MLA_QK ** -0.5)]
        k0 = kvf[2 * h]
        r = lax.rsqrt(_ssq([k0, kr]) / MLA_QK + RMS_EPS)
        b0, b1 = k0 * r * gk[0], kr * r * gk[1]
        k_out += [b0, b1 * cos + _swap32(b1) * sin]
        v_out.append(kvf[2 * h + 1])
    return [q_out, k_out, v_out]


def _shift_down(x, n):
    row = lax.broadcasted_iota(jnp.int32, x.shape, 0)
    return jnp.where(row >= n, pltpu.roll(x, n, 0), 0.0)


def _shift_up(x, n):
    rows = x.shape[0]
    row = lax.broadcasted_iota(jnp.int32, x.shape, 0)
    return jnp.where(row < rows - n, pltpu.roll(x, rows - n, 0), 0.0)


def _conv_blocks(S):
    cb = 256
    return cb, FFN_DIM // cb


def _conv_fwd(ag, w8, B, S, name):
    cb, ncb = _conv_blocks(S)

    def body(a_ref, g_ref, w_ref, u_ref):
        g = g_ref[...]
        w = w_ref[...]
        gc = w[0:1] * _shift_down(g, 2) + w[1:2] * _shift_down(g, 1) + w[2:3] * g + w[3:4]
        u_ref[...] = (a_ref[...] * (gc * _sigmoid(gc))).astype(u_ref.dtype)

    return pl.pallas_call(
        body, name=name, grid=(ncb, B),
        in_specs=[pl.BlockSpec((S, cb), lambda j, b: (b, j)),
                  pl.BlockSpec((S, cb), lambda j, b: (b, ncb + j)),
                  pl.BlockSpec((8, cb), lambda j, b: (0, j))],
        out_specs=pl.BlockSpec((S, cb), lambda j, b: (b, j)),
        out_shape=jax.ShapeDtypeStruct((B * S, FFN_DIM), BF16),
        compiler_params=_params(("parallel", "parallel")),
    )(ag, ag, w8)


def _conv_bwd(ag, w8, du, B, S, name):
    cb, ncb = _conv_blocks(S)

    def body(a_ref, g_ref, w_ref, du_ref, da_ref, dg_ref, dw_ref):
        g = g_ref[...]
        w = w_ref[...]
        g1, g2 = _shift_down(g, 1), _shift_down(g, 2)
        gc = w[0:1] * g2 + w[1:2] * g1 + w[2:3] * g + w[3:4]
        sg = _sigmoid(gc)
        du_v = du_ref[...]
        da_ref[...] = du_v * (gc * sg)
        dgc = du_v * a_ref[...] * (sg * (1.0 + gc * (1.0 - sg)))
        dg_ref[...] = w[2:3] * dgc + w[1:2] * _shift_up(dgc, 1) + w[0:1] * _shift_up(dgc, 2)
        part = jnp.concatenate([
            jnp.sum(dgc * g2, axis=0, keepdims=True), jnp.sum(dgc * g1, axis=0, keepdims=True),
            jnp.sum(dgc * g, axis=0, keepdims=True), jnp.sum(dgc, axis=0, keepdims=True),
            jnp.zeros((4, cb), F32)], axis=0)

        @pl.when(pl.program_id(1) == 0)
        def _():
            dw_ref[...] = part

        @pl.when(pl.program_id(1) > 0)
        def _():
            dw_ref[...] += part

    blk = lambda j, b: (b, j)
    return pl.pallas_call(
        body, name=name, grid=(ncb, B),
        in_specs=[pl.BlockSpec((S, cb), blk),
                  pl.BlockSpec((S, cb), lambda j, b: (b, ncb + j)),
                  pl.BlockSpec((8, cb), lambda j, b: (0, j)),
                  pl.BlockSpec((S, cb), blk)],
        out_specs=[pl.BlockSpec((S, cb), blk), pl.BlockSpec((S, cb), blk),
                   pl.BlockSpec((8, cb), lambda j, b: (0, j))],
        out_shape=[jax.ShapeDtypeStruct((B * S, FFN_DIM), F32), jax.ShapeDtypeStruct((B * S, FFN_DIM), F32),
                   jax.ShapeDtypeStruct((8, FFN_DIM), F32)],
        compiler_params=_params(("parallel", "arbitrary")),
    )(ag, ag, w8, du)


_NT = (((1,), (1,)), ((), ()))
_NN = (((1,), (0,)), ((), ()))
_TN = (((0,), (0,)), ((), ()))


def _dot(a, b, dn):
    return lax.dot_general(a.astype(MXU_DTYPE), b.astype(MXU_DTYPE), dn, preferred_element_type=F32)


def _rel_and_mask():
    il = lax.broadcasted_iota(jnp.int32, (ATT_BLOCK, ATT_BLOCK), 0)
    jl = lax.broadcasted_iota(jnp.int32, (ATT_BLOCK, ATT_BLOCK), 1)
    return (il - jl).astype(F32), (jl // CHUNK) <= (il // CHUNK)


def _rows(i):
    return pl.ds(pl.multiple_of(i * ATT_BLOCK, ATT_BLOCK), ATT_BLOCK)


def _mla_attn_fwd(q, k, v, B, S):
    H, nq = MLA_HEADS, S // ATT_BLOCK

    def body(q_ref, k_ref, v_ref, o_ref, lse_ref):
        _, mask = _rel_and_mask()

        def qblock(i, _):
            qi = q_ref[_rows(i), :]

            def kv(j, carry, diag):
                m, l, acc = carry
                s = _dot(qi, k_ref[_rows(j), :], _NT)
                if diag:
                    s = jnp.where(mask, s, MASK_VALUE)
                m2 = jnp.maximum(m, jnp.max(s, axis=-1, keepdims=True))
                alpha = jnp.exp(m - m2)
                p = jnp.exp(s - m2)
                l2 = alpha * l + jnp.sum(p, axis=-1, keepdims=True)
                return m2, l2, alpha * acc + _dot(p, v_ref[_rows(j), :], _NN)

            init = (jnp.full((ATT_BLOCK, 1), MASK_VALUE, F32), jnp.zeros((ATT_BLOCK, 1), F32),
                    jnp.zeros((ATT_BLOCK, MLA_V), F32))
            carry = lax.fori_loop(0, i, lambda j, c: kv(j, c, False), init)
            m, l, acc = kv(i, carry, True)
            o_ref[_rows(i), :] = acc / l
            lse_ref[0, _rows(i), :] = m + jnp.log(l)
            return 0

        lax.fori_loop(0, nq, qblock, 0)

    return pl.pallas_call(
        body, name="mla_attn_fwd", grid=(B, H),
        in_specs=[pl.BlockSpec((S, MLA_PAD), lambda b, h: (b, h)),
                  pl.BlockSpec((S, MLA_PAD), lambda b, h: (b, h)),
                  pl.BlockSpec((S, MLA_V), lambda b, h: (b, h))],
        out_specs=[pl.BlockSpec((S, MLA_V), lambda b, h: (b, h)),
                   pl.BlockSpec((1, S, 1), lambda b, h: (b * H + h, 0, 0))],
        out_shape=[jax.ShapeDtypeStruct((B * S, H * MLA_V), F32), jax.ShapeDtypeStruct((B * H, S, 1), F32)],
        compiler_params=_params(("parallel", "parallel")),
    )(q, k, v)


def _mla_attn_bwd(q, k, v, o, do, lse, B, S):
    H, nq = MLA_HEADS, S // ATT_BLOCK

    def body(q_ref, k_ref, v_ref, o_ref, do_ref, lse_ref, dq_ref, dk_ref, dv_ref, acc_ref):
        _, mask = _rel_and_mask()
        dk_ref[...] = jnp.zeros(dk_ref.shape, F32)
        dv_ref[...] = jnp.zeros(dv_ref.shape, F32)

        def qblock(i, _):
            qi = q_ref[_rows(i), :]
            doi = do_ref[_rows(i), :]
            delta = jnp.sum(doi * o_ref[_rows(i), :], axis=-1, keepdims=True)
            lse_i = lse_ref[0, _rows(i), :]
            doi = doi.astype(MXU_DTYPE)
            acc_ref[...] = jnp.zeros(acc_ref.shape, F32)

            def kv(j, diag):
                kj = k_ref[_rows(j), :]
                p = jnp.exp(_dot(qi, kj, _NT) - lse_i)
                if diag:
                    p = jnp.where(mask, p, 0.0)
                ds = (p * (_dot(doi, v_ref[_rows(j), :], _NT) - delta)).astype(MXU_DTYPE)
                acc_ref[...] += _dot(ds, kj, _NN)
                dk_ref[_rows(j), :] += _dot(ds, qi, _TN)
                dv_ref[_rows(j), :] += _dot(p, doi, _TN)

            def off(j, c):
                kv(j, False)
                return c

            lax.fori_loop(0, i, off, 0)
            kv(i, True)
            dq_ref[_rows(i), :] = acc_ref[...]
            return 0

        lax.fori_loop(0, nq, qblock, 0)

    qk_spec = pl.BlockSpec((S, MLA_PAD), lambda b, h: (b, h))
    v_spec = pl.BlockSpec((S, MLA_V), lambda b, h: (b, h))
    return pl.pallas_call(
        body, name="mla_attn_bwd", grid=(B, H),
        in_specs=[qk_spec, qk_spec, v_spec, v_spec, v_spec,
                  pl.BlockSpec((1, S, 1), lambda b, h: (b * H + h, 0, 0))],
        out_specs=[qk_spec, qk_spec, v_spec],
        out_shape=[jax.ShapeDtypeStruct((B * S, H * MLA_PAD), F32), jax.ShapeDtypeStruct((B * S, H * MLA_PAD), F32),
                   jax.ShapeDtypeStruct((B * S, H * MLA_V), F32)],
        scratch_shapes=[pltpu.VMEM((ATT_BLOCK, MLA_PAD), F32)],
        compiler_params=_params(("parallel", "parallel")),
    )(q, k, v, o, do, lse)


def _ret_log_gamma():
    lg = np.log1p(-np.exp2(RET_GAMMA_BASE - np.arange(RET_HEADS, dtype=np.float32))).astype(np.float32)
    return jnp.asarray(np.broadcast_to(lg[:, None, None], (RET_HEADS, 8, LANES)).copy())


def _ret_decay(lg, rel, mask, steps):
    if steps is None:
        return jnp.where(mask, jnp.exp(lg * jnp.abs(rel)), 0.0)
    return jnp.exp(lg * (rel + (steps * ATT_BLOCK).astype(F32)))


def _ret_attn_fwd(q, k, v, B, S):
    H, nq = RET_HEADS, S // ATT_BLOCK

    def body(lg_ref, q_ref, k_ref, v_ref, o_ref, acc_ref):
        rel, mask = _rel_and_mask()
        lg = lg_ref[0, 0:1, 0:1]

        def qblock(i, _):
            qi = q_ref[_rows(i), :]
            acc_ref[...] = jnp.zeros(acc_ref.shape, F32)

            def kv(j, steps):
                a = _dot(qi, k_ref[_rows(j), :], _NT) * _ret_decay(lg, rel, mask, steps)
                acc_ref[...] += _dot(a, v_ref[_rows(j), :], _NN)

            def off(j, c):
                kv(j, i - j)
                return c

            lax.fori_loop(0, i, off, 0)
            kv(i, None)
            o_ref[_rows(i), :] = acc_ref[...]
            return 0

        lax.fori_loop(0, nq, qblock, 0)

    qk_spec = pl.BlockSpec((S, RET_QK), lambda b, h: (b, h))
    v_spec = pl.BlockSpec((S, RET_V), lambda b, h: (b, h))
    return pl.pallas_call(
        body, name="ret_attn_fwd", grid=(B, H),
        in_specs=[pl.BlockSpec((1, 8, LANES), lambda b, h: (h, 0, 0)), qk_spec, qk_spec, v_spec],
        out_specs=v_spec,
        out_shape=jax.ShapeDtypeStruct((B * S, H * RET_V), F32),
        scratch_shapes=[pltpu.VMEM((ATT_BLOCK, RET_V), F32)],
        compiler_params=_params(("parallel", "parallel")),
    )(_ret_log_gamma(), q, k, v)


def _ret_attn_bwd(q, k, v, do, B, S):
    H, nq = RET_HEADS, S // ATT_BLOCK

    def body(lg_ref, q_ref, k_ref, v_ref, do_ref, dq_ref, dk_ref, dv_ref, acc_ref):
        rel, mask = _rel_and_mask()
        lg = lg_ref[0, 0:1, 0:1]
        dk_ref[...] = jnp.zeros(dk_ref.shape, F32)
        dv_ref[...] = jnp.zeros(dv_ref.shape, F32)

        def qblock(i, _):
            qi = q_ref[_rows(i), :]
            doi = do_ref[_rows(i), :].astype(MXU_DTYPE)
            acc_ref[...] = jnp.zeros(acc_ref.shape, F32)

            def kv(j, steps):
                kj = k_ref[_rows(j), :]
                dec = _ret_decay(lg, rel, mask, steps)
                a = _dot(qi, kj, _NT) * dec
                da = (_dot(doi, v_ref[_rows(j), :], _NT) * dec).astype(MXU_DTYPE)
                acc_ref[...] += _dot(da, kj, _NN)
                dk_ref[_rows(j), :] += _dot(da, qi, _TN)
                dv_ref[_rows(j), :] += _dot(a, doi, _TN)

            def off(j, c):
                kv(j, i - j)
                return c

            lax.fori_loop(0, i, off, 0)
            kv(i, None)
            dq_ref[_rows(i), :] = acc_ref[...]
            return 0

        lax.fori_loop(0, nq, qblock, 0)

    qk_spec = pl.BlockSpec((S, RET_QK), lambda b, h: (b, h))
    v_spec = pl.BlockSpec((S, RET_V), lambda b, h: (b, h))
    return pl.pallas_call(
        body, name="ret_attn_bwd", grid=(B, H),
        in_specs=[pl.BlockSpec((1, 8, LANES), lambda b, h: (h, 0, 0)), qk_spec, qk_spec, v_spec, v_spec],
        out_specs=[qk_spec, qk_spec, v_spec],
        out_shape=[jax.ShapeDtypeStruct((B * S, H * RET_QK), F32), jax.ShapeDtypeStruct((B * S, H * RET_QK), F32),
                   jax.ShapeDtypeStruct((B * S, H * RET_V), F32)],
        scratch_shapes=[pltpu.VMEM((ATT_BLOCK, RET_QK), F32)],
        compiler_params=_params(("parallel", "parallel")),
    )(_ret_log_gamma(), q, k, v, do)


def _loss_head(y, target, bm=512):
    T, D = y.shape
    bm = _pick(T, bm)

    def body(y_ref, t_ref, dy_ref, l_ref):
        err = y_ref[...] - t_ref[...]
        dy_ref[...] = err / D
        part = jnp.full((8, LANES), 0.5 * jnp.sum(jnp.mean(err * err, axis=-1)), F32)

        @pl.when(pl.program_id(0) == 0)
        def _():
            l_ref[...] = part

        @pl.when(pl.program_id(0) > 0)
        def _():
            l_ref[...] += part

    blk = pl.BlockSpec((bm, D), lambda i: (i, 0))
    dy, l = pl.pallas_call(
        body, name="loss_head", grid=(T // bm,),
        in_specs=[blk, blk], out_specs=[blk, pl.BlockSpec((8, LANES), lambda i: (0, 0))],
        out_shape=[jax.ShapeDtypeStruct((T, D), F32), jax.ShapeDtypeStruct((8, LANES), F32)],
        compiler_params=_params(("arbitrary",)),
    )(y, target)
    return dy, l[0, 0]


def _adamw(w, g, m, v, name):
    R, C = w.shape
    br = R if R * C * 4 <= 2 ** 21 else _pick_rows(R, max(8, (2 ** 21) // (C * 4)))

    def body(w_ref, g_ref, m_ref, v_ref, d_ref, mo_ref, vo_ref):
        g_v = g_ref[...]
        m_v = ADAM_B1 * m_ref[...] + (1.0 - ADAM_B1) * g_v
        v_v = ADAM_B2 * v_ref[...] + (1.0 - ADAM_B2) * (g_v * g_v)
        m_hat = m_v / (1.0 - ADAM_B1 ** ADAM_STEP)
        v_hat = v_v / (1.0 - ADAM_B2 ** ADAM_STEP)
        d_ref[...] = -ADAM_LR * (m_hat / (jnp.sqrt(v_hat) + ADAM_EPS) + ADAM_WD * w_ref[...])
        mo_ref[...] = m_v
        vo_ref[...] = v_v

    blk = pl.BlockSpec((br, C), lambda i: (i, 0))
    return pl.pallas_call(
        body, name=name, grid=(R // br,),
        in_specs=[blk] * 4, out_specs=[blk] * 3,
        out_shape=[jax.ShapeDtypeStruct((R, C), F32)] * 3,
        compiler_params=_params(("parallel",)),
    )(w, g, m, v)


def _pick_rows(R, target):
    best = None
    for d in range(8, min(R, target) + 1, 8):
        if R % d == 0:
            best = d
    assert best is not None, (R, target)
    return best


def _position():
    return lax.axis_index("x"), lax.axis_index("y"), lax.axis_index("c")


HBM_SPEC = pl.BlockSpec(memory_space=pltpu.HBM)


def _other_chips(x, y):
    return [(1 - x, y), (x, 1 - y), (1 - x, 1 - y)]


def _all_gather_weights(bigs, small):
    nb = len(bigs)

    def body(*refs):
        big_refs, small_ref = refs[:nb], refs[nb]
        obig, osmall = refs[nb + 1:2 * nb + 1], refs[2 * nb + 1]
        ici_send, ici_recv, d2d_send, d2d_recv, sm_send, sm_recv = refs[2 * nb + 2:]
        x, y, c = _position()
        me = 2 * x + y
        chips = _other_chips(x, y)

        def rows(n, half):
            rh = bigs[n].shape[0] // 2
            return pl.ds(half * rh, rh)

        def over_ici(n, j, slot, from_shard):
            px, py = chips[j]
            dst = obig[n].at[slot, rows(n, c)]
            return pltpu.make_async_remote_copy(
                src_ref=big_refs[n].at[rows(n, c)] if from_shard else dst, dst_ref=dst,
                send_sem=ici_send.at[3 * n + j], recv_sem=ici_recv.at[3 * n + j],
                device_id=(px, py, c), device_id_type=MESH)

        def over_d2d(n, j, half):
            px, py = chips[j]
            part = obig[n].at[2 * px + py, rows(n, half)]
            return pltpu.make_async_remote_copy(
                src_ref=part, dst_ref=part, send_sem=d2d_send.at[3 * n + j], recv_sem=d2d_recv.at[3 * n + j],
                device_id=(x, y, 1 - c), device_id_type=MESH)

        def small_copy(j, slot):
            px, py = chips[j]
            return pltpu.make_async_remote_copy(
                src_ref=small_ref, dst_ref=osmall.at[slot], send_sem=sm_send.at[j], recv_sem=sm_recv.at[j],
                device_id=(px, py, c), device_id_type=MESH)

        sends = [over_ici(n, j, me, True) for n in range(nb) for j in range(3)]
        sends += [small_copy(j, me) for j in range(3)]
        for cp in sends:
            cp.start()
        passed = []
        for n in range(nb):
            for j, (px, py) in enumerate(chips):
                over_ici(n, j, 2 * px + py, False).wait_recv()
                fwd = over_d2d(n, j, c)
                fwd.start()
                passed.append(fwd)
        for n in range(nb):
            for j in range(3):
                over_d2d(n, j, 1 - c).wait_recv()
        for j, (px, py) in enumerate(chips):
            small_copy(j, 2 * px + py).wait_recv()
        for cp in sends + passed:
            cp.wait_send()

    dma = pltpu.SemaphoreType.DMA
    return pl.pallas_call(
        body, name="weights_all_gather",
        in_specs=[HBM_SPEC] * (nb + 1), out_specs=[HBM_SPEC] * (nb + 1),
        out_shape=[jax.ShapeDtypeStruct((N_SHARD,) + b.shape, b.dtype) for b in bigs]
        + [jax.ShapeDtypeStruct((N_SHARD,) + small.shape, small.dtype)],
        scratch_shapes=[dma((3 * nb,)), dma((3 * nb,)), dma((3 * nb,)), dma((3 * nb,)), dma((3,)), dma((3,))],
    )(*bigs, small)


def _sibling_send_halves(gs):
    n = len(gs)

    def body(*refs):
        g_refs, o_refs = refs[:n], refs[n:2 * n]
        send_sems, recv_sems = refs[2 * n:]
        x, y, c = _position()
        copies = []
        for a in range(n):
            rh = gs[a].shape[1] // 2
            copies.append(pltpu.make_async_remote_copy(
                src_ref=g_refs[a].at[:, pl.ds((1 - c) * rh, rh)], dst_ref=o_refs[a], send_sem=send_sems.at[a],
                recv_sem=recv_sems.at[a], device_id=(x, y, 1 - c), device_id_type=MESH))
        for cp in copies:
            cp.start()
        for cp in copies:
            cp.wait()

    return pl.pallas_call(
        body, name="grads_sibling_halves",
        in_specs=[HBM_SPEC] * n, out_specs=[HBM_SPEC] * n,
        out_shape=[jax.ShapeDtypeStruct((g.shape[0], g.shape[1] // 2, g.shape[2]), g.dtype) for g in gs],
        scratch_shapes=[pltpu.SemaphoreType.DMA((n,)), pltpu.SemaphoreType.DMA((n,))],
    )(*gs)


def _chip_partial(g, got, name):
    n, rh, cols = got.shape
    br = _pick_rows(rh, 256)
    nrb = rh // br
    c_idx = lax.axis_index("c").astype(jnp.int32).reshape((1,))

    def body(c_ref, g_ref, got_ref, o_ref):
        o_ref[...] = (g_ref[...] + got_ref[...]).astype(o_ref.dtype)

    return pl.pallas_call(
        body, name=name,
        grid_spec=pltpu.PrefetchScalarGridSpec(
            num_scalar_prefetch=1, grid=(n, nrb),
            in_specs=[pl.BlockSpec((None, br, cols), lambda s, r, c_ref: (s, c_ref[0] * nrb + r, 0)),
                      pl.BlockSpec((None, br, cols), lambda s, r, c_ref: (s, r, 0))],
            out_specs=pl.BlockSpec((None, br, cols), lambda s, r, c_ref: (s, r, 0))),
        out_shape=jax.ShapeDtypeStruct((n, rh, cols), BF16),
        compiler_params=_params(("parallel", "parallel")),
    )(c_idx, g, got)


def _exchange_partials(ps):
    n = len(ps)

    def body(*refs):
        p_refs, o_refs = refs[:n], refs[n:2 * n]
        send_sems, recv_sems = refs[2 * n:]
        x, y, c = _position()
        copies = [pltpu.make_async_remote_copy(
            src_ref=p_refs[a].at[2 * px + py], dst_ref=o_refs[a].at[j], send_sem=send_sems.at[3 * a + j],
            recv_sem=recv_sems.at[3 * a + j], device_id=(px, py, c), device_id_type=MESH)
            for a in range(n) for j, (px, py) in enumerate(_other_chips(x, y))]
        for cp in copies:
            cp.start()
        for cp in copies:
            cp.wait()

    return pl.pallas_call(
        body, name="grads_exchange_partials",
        in_specs=[HBM_SPEC] * n, out_specs=[HBM_SPEC] * n,
        out_shape=[jax.ShapeDtypeStruct((3,) + p.shape[1:], p.dtype) for p in ps],
        scratch_shapes=[pltpu.SemaphoreType.DMA((3 * n,)), pltpu.SemaphoreType.DMA((3 * n,))],
    )(*ps)


def _sum_partials(p, got, name):
    _, rh, cols = p.shape
    br = _pick_rows(rh, 256)
    x, y, c = _position()
    where = jnp.stack([2 * x + y, c]).astype(jnp.int32)

    def body(where_ref, p_ref, got_ref, o_ref):
        acc = p_ref[...].astype(F32)
        for j in range(3):
            acc = acc + got_ref[j].astype(F32)
        o_ref[...] = acc

    return pl.pallas_call(
        body, name=name,
        grid_spec=pltpu.PrefetchScalarGridSpec(
            num_scalar_prefetch=1, grid=(rh // br,),
            in_specs=[pl.BlockSpec((None, br, cols), lambda r, where_ref: (where_ref[0], r, 0)),
                      pl.BlockSpec((3, br, cols), lambda r, where_ref: (0, r, 0))],
            out_specs=pl.BlockSpec((None, br, cols), lambda r, where_ref: (where_ref[1], r, 0))),
        out_shape=jax.ShapeDtypeStruct((2, rh, cols), F32),
        compiler_params=_params(("parallel",)),
    )(where, p, got)


def _sibling_share(fulls):
    n = len(fulls)

    def body(*refs):
        o_refs = refs[n:2 * n]
        send_sems, recv_sems = refs[2 * n:]
        x, y, c = _position()

        def copy(a, half):
            return pltpu.make_async_remote_copy(
                src_ref=o_refs[a].at[half], dst_ref=o_refs[a].at[half], send_sem=send_sems.at[a],
                recv_sem=recv_sems.at[a], device_id=(x, y, 1 - c), device_id_type=MESH)

        sends = [copy(a, c) for a in range(n)]
        for cp in sends:
            cp.start()
        for a in range(n):
            copy(a, 1 - c).wait_recv()
        for cp in sends:
            cp.wait_send()

    dma = pltpu.SemaphoreType.DMA
    return pl.pallas_call(
        body, name="grads_sibling_share",
        in_specs=[HBM_SPEC] * n, out_specs=[HBM_SPEC] * n,
        out_shape=[jax.ShapeDtypeStruct(f.shape, f.dtype) for f in fulls],
        input_output_aliases={a: a for a in range(n)},
        scratch_shapes=[dma((n,)), dma((n,))],
    )(*fulls)


def _all_reduce_small(v):
    R, cols = v.shape

    def body(v_ref, o_ref, buf_ref, send_sems, recv_sems):
        x, y, c = _position()
        me = 4 * x + 2 * y + c
        buf_ref[me] = v_ref[...]
        sends = []
        for k in range(1, N_DEV):
            px = 1 - x if k & 4 else x
            py = 1 - y if k & 2 else y
            pc = 1 - c if k & 1 else c
            sends.append(pltpu.make_async_remote_copy(
                src_ref=v_ref, dst_ref=buf_ref.at[me], send_sem=send_sems.at[k - 1], recv_sem=recv_sems.at[k - 1],
                device_id=(px, py, pc), device_id_type=MESH))
        for cp in sends:
            cp.start()
        for k in range(1, N_DEV):
            px = 1 - x if k & 4 else x
            py = 1 - y if k & 2 else y
            pc = 1 - c if k & 1 else c
            pltpu.make_async_remote_copy(
                src_ref=v_ref, dst_ref=buf_ref.at[4 * px + 2 * py + pc], send_sem=send_sems.at[k - 1],
                recv_sem=recv_sems.at[k - 1], device_id=(px, py, pc), device_id_type=MESH).wait_recv()
        for cp in sends:
            cp.wait_send()
        acc = buf_ref[0]
        for d in range(1, N_DEV):
            acc = acc + buf_ref[d]
        o_ref[...] = acc

    return pl.pallas_call(
        body, name="small_grads_all_reduce",
        in_specs=[pl.BlockSpec(memory_space=pltpu.VMEM)], out_specs=pl.BlockSpec(memory_space=pltpu.VMEM),
        out_shape=jax.ShapeDtypeStruct((R, cols), F32),
        scratch_shapes=[pltpu.VMEM((N_DEV, R, cols), F32), pltpu.SemaphoreType.DMA((N_DEV - 1,)),
                        pltpu.SemaphoreType.DMA((N_DEV - 1,))],
    )(v)


def _rope_tables(S, half, width):
    inv_freq = ROPE_THETA ** (-jnp.arange(half, dtype=F32) / half)
    ang = jnp.arange(S).astype(F32)[:, None] * inv_freq[None, :]
    return jnp.cos(ang), jnp.sin(ang)


def _local_step(x, target, w, B, S):
    T = B * S
    D = D_MODEL
    bm = 256
    full = lambda a, wd, tile=None: (a, wd, 0, tile or wd)
    g = {}

    cos_r, sin_r = _rope_tables(S, RET_QK // 2, LANES)
    cos_m, sin_m = _rope_tables(S, MLA_ROPE // 2, LANES)
    zeros64 = jnp.zeros((S, 64), F32)
    cos_m = jnp.concatenate([cos_m, cos_m, zeros64], axis=1)
    sin_m = jnp.concatenate([-sin_m, sin_m, zeros64], axis=1)

    def ffn_fwd(xin, i):
        norm = w["ffn_norm"][i:i + 1]
        (h,) = _rowwise_fwd(_fn_rms, f"ffn{i}_norm", [full(xin, D)], [], [(norm, D)], [(D, D, BF16)], bm, S)
        ag = _mm(h, w["ffn_w_in"][i], "nn", F32, f"ffn{i}_in", bn=1408)
        u = _conv_fwd(ag, w["ffn_conv8"][i], B, S, f"ffn{i}_conv")
        xout = _mm(u, w["ffn_w_out"][i], "nn", F32, f"ffn{i}_out", residual=xin, bk=1408)
        return xout, (xin, norm, h, ag, u)

    def ffn_bwd(dxout, saved, i):
        xin, norm, h, ag, u = saved
        du = _mm(dxout, w["ffn_w_out"][i], "nt", F32, f"ffn{i}_out_dx", bn=1408)
        g_w_out = _mm(u, dxout, "tn", F32, f"ffn{i}_out_dw", bm=1408)
        da, dg, dw8 = _conv_bwd(ag, w["ffn_conv8"][i], du, B, S, f"ffn{i}_conv_bwd")
        dh = _mm([da, dg], w["ffn_w_in"][i], "nt", F32, f"ffn{i}_in_dx", bk=1408)
        g_w_in = _mm(h, [da, dg], "tn", F32, f"ffn{i}_in_dw", bn=1408, out_slots=N_SHARD)
        (dxin,), (g_norm,) = _rowwise_bwd(_fn_rms, f"ffn{i}_norm_bwd", [full(xin, D)], [], [(norm, D)],
                                          [(dh, D)], bm, S, adds={0: dxout})
        return dxin, (g_norm, g_w_in, dw8, g_w_out)

    (h0,) = _rowwise_fwd(_fn_rms, "ret_norm", [full(x, D)], [], [(w["ret_norm"], D)], [(D, D, BF16)], bm, S)
    proj = _mm(h0, w["ret_w_in"], "nn", F32, "ret_in")
    HQ, HV = RET_HEADS * RET_QK, RET_HEADS * RET_V
    rope_rows = [(proj, 2 * HQ + HV, 0, LANES)]
    q_r, k_r, v_r = _rowwise_fwd(_fn_ret_rope, "ret_rope", rope_rows, [cos_r, sin_r], [],
                                 [(HQ, LANES, BF16), (HQ, LANES, BF16), (HV, LANES, BF16)], bm, S)
    ret_o = _ret_attn_fwd(q_r, k_r, v_r, B, S)
    gate_rows = [full(ret_o, HV, RET_V), (proj, HV, 2, RET_V)]
    (y0,) = _rowwise_fwd(_fn_ret_gate, "ret_gate", gate_rows, [], [(w["ret_gn"], RET_V)], [(HV, RET_V, BF16)], 128, S)
    x1 = _mm(y0, w["ret_w_out"], "nn", F32, "ret_out", residual=x)
    x2, ffn0_saved = ffn_fwd(x1, 0)

    (h2,) = _rowwise_fwd(_fn_rms, "mla_norm", [full(x2, D)], [], [(w["mla_norm"], D)], [(D, D, BF16)], bm, S)
    proj2 = _mm(h2, w["mla_w_in"], "nn", F32, "mla_in")
    lat_consts = [(w["mla_q_norm"], LANES), (w["mla_kv_norm"], LANES)]
    cqn, ckvn, kr = _rowwise_fwd(_fn_mla_lat, "mla_latent_norm", [full(proj2, MLA_IN_PAD, LANES)], [], lat_consts,
                                 [(MLA_Q_RANK, LANES, BF16), (MLA_KV_RANK, LANES, BF16), (LANES, LANES, F32)], bm, S)
    qf = _mm(cqn, w["mla_w_qb"], "nn", F32, "mla_qb")
    kvf = _mm(ckvn, w["mla_w_kvb"], "nn", F32, "mla_kvb")
    HP, HVm = MLA_HEADS * MLA_PAD, MLA_HEADS * MLA_V
    head_rows = [full(qf, HP, LANES), full(kvf, HP, LANES), full(kr, LANES)]
    head_consts = [(w["mla_q_head_norm"], LANES), (w["mla_k_head_norm"], LANES)]
    q_a, k_a, v_a = _rowwise_fwd(_fn_mla_heads, "mla_heads", head_rows, [cos_m, sin_m], head_consts,
                                 [(HP, LANES, BF16), (HP, LANES, BF16), (HVm, LANES, BF16)], bm, S)
    att_o, lse = _mla_attn_fwd(q_a, k_a, v_a, B, S)
    x3 = _mm(att_o, w["mla_w_out"], "nn", F32, "mla_out", residual=x2)
    x4, ffn1_saved = ffn_fwd(x3, 1)

    dy, loss = _loss_head(x4, target)

    dx3, (g_n1, g_in1, dw8_1, g_out1) = ffn_bwd(dy, ffn1_saved, 1)

    d_att_o = _mm(dx3, w["mla_w_out"], "nt", F32, "mla_out_dx")
    g["mla_w_out"] = _mm(att_o, dx3, "tn", F32, "mla_out_dw")
    dq_a, dk_a, dv_a = _mla_attn_bwd(q_a, k_a, v_a, att_o, d_att_o, lse, B, S)
    (dqf, dkvf, dkr), (g["mla_q_head_norm"], g["mla_k_head_norm"]) = _rowwise_bwd(
        _fn_mla_heads, "mla_heads_bwd", head_rows, [cos_m, sin_m], head_consts,
        [(dq_a, LANES), (dk_a, LANES), (dv_a, LANES)], 128, S)
    dcqn = _mm(dqf, w["mla_w_qb"], "nt", F32, "mla_qb_dx")
    g["mla_w_qb"] = _mm(cqn, dqf, "tn", F32, "mla_qb_dw")
    dckvn = _mm(dkvf, w["mla_w_kvb"], "nt", F32, "mla_kvb_dx")
    g["mla_w_kvb"] = _mm(ckvn, dkvf, "tn", F32, "mla_kvb_dw", bn=512, out_slots=N_SHARD)
    (dproj2,), (g["mla_q_norm"], g["mla_kv_norm"]) = _rowwise_bwd(
        _fn_mla_lat, "mla_latent_norm_bwd", [full(proj2, MLA_IN_PAD, LANES)], [], lat_consts,
        [(dcqn, LANES), (dckvn, LANES), (dkr, LANES)], bm, S)
    dh2 = _mm(dproj2, w["mla_w_in"], "nt", F32, "mla_in_dx")
    g["mla_w_in"] = _mm(h2, dproj2, "tn", F32, "mla_in_dw")
    (dx2,), (g["mla_norm"],) = _rowwise_bwd(_fn_rms, "mla_norm_bwd", [full(x2, D)], [], [(w["mla_norm"], D)],
                                            [(dh2, D)], bm, S, adds={0: dx3})

    dx1, (g_n0, g_in0, dw8_0, g_out0) = ffn_bwd(dx2, ffn0_saved, 0)

    dy0 = _mm(dx1, w["ret_w_out"], "nt", F32, "ret_out_dx")
    g["ret_w_out"] = _mm(y0, dx1, "tn", F32, "ret_out_dw")
    (d_ret_o, dgate), (g["ret_gn"],) = _rowwise_bwd(_fn_ret_gate, "ret_gate_bwd", gate_rows, [], [(w["ret_gn"], RET_V)],
                                                    [(dy0, RET_V)], 128, S)
    dq_r, dk_r, dv_r = _ret_attn_bwd(q_r, k_r, v_r, d_ret_o, B, S)
    (dqkv,), _ = _rowwise_bwd(_fn_ret_rope, "ret_rope_bwd", rope_rows, [cos_r, sin_r], [],
                              [(dq_r, LANES), (dk_r, LANES), (dv_r, LANES)], bm, S)
    dh0 = _mm([dqkv, dgate], w["ret_w_in"], "nt", F32, "ret_in_dx", bk=1024)
    g["ret_w_in"] = _mm(h0, [dqkv, dgate], "tn", F32, "ret_in_dw", bn=512, out_slots=N_SHARD)
    (dx,), (g["ret_norm"],) = _rowwise_bwd(_fn_rms, "ret_norm_bwd", [full(x, D)], [], [(w["ret_norm"], D)],
                                           [(dh0, D)], bm, S, adds={0: dx1})

    g["ffn_norm"] = jnp.concatenate([g_n0, g_n1], axis=0)
    g["ffn_w_in"] = [g_in0, g_in1]
    g["ffn_w_out"] = [g_out0, g_out1]
    g["ffn_conv_w"] = jnp.stack([dw8_0[0:3], dw8_1[0:3]])
    g["ffn_conv_b"] = jnp.stack([dw8_0[3], dw8_1[3]])
    return loss, dx, g


_BIG = [("ret_w_in", 2), ("ret_w_out", 1), ("mla_w_in", 1), ("mla_w_qb", 2), ("mla_w_kvb", 2), ("mla_w_out", 1),
        ("ffn_w_in", 2), ("ffn_w_out", 1)]
_SMALL_SHARDED = [("ret_gn", 2), ("mla_norm", 1), ("mla_q_norm", 1), ("mla_kv_norm", 1), ("ffn_conv_w", 2)]
_SMALL_REPLICATED = ["ret_norm", "mla_q_head_norm", "mla_k_head_norm", "ffn_norm", "ffn_conv_b"]
_SMALL_ALL = ["ret_norm", "ret_gn", "mla_norm", "mla_q_norm", "mla_kv_norm", "mla_q_head_norm", "mla_k_head_norm",
              "ffn_norm", "ffn_conv_w", "ffn_conv_b"]


def _to_slots(full, axis):
    shape = full.shape
    split = shape[:axis] + (N_SHARD, shape[axis] // N_SHARD) + shape[axis + 1:]
    return jnp.moveaxis(full.reshape(split), axis, 0).reshape(N_SHARD, -1)


def _from_slots(slots, shard_shape, axis):
    parts = jnp.moveaxis(slots.reshape((N_SHARD,) + tuple(shard_shape)), 0, axis)
    full = shard_shape[:axis] + (N_SHARD * shard_shape[axis],) + shard_shape[axis + 1:]
    return parts.reshape(full)


def _pad_rows(flat, cols, row_unit):
    n, L = flat.shape
    unit = cols * row_unit
    Lp = -(-L // unit) * unit
    if Lp != L:
        flat = jnp.concatenate([flat, jnp.zeros((n, Lp - L), flat.dtype)], axis=1)
    return flat.reshape(n, Lp // cols, cols)


def _pad_heads(a, axis):
    shape = a.shape
    a = a.reshape(shape[:axis] + (MLA_HEADS, MLA_QK) + shape[axis + 1:])
    pad = [(0, 0)] * a.ndim
    pad[axis + 1] = (0, MLA_PAD - MLA_QK)
    return jnp.pad(a, pad).reshape(shape[:axis] + (MLA_HEADS * MLA_PAD,) + shape[axis + 1:])


def _unpad_heads(a, axis):
    shape = a.shape
    a = a.reshape(shape[:axis] + (MLA_HEADS, MLA_PAD) + shape[axis + 1:])
    a = lax.slice_in_dim(a, 0, MLA_QK, axis=axis + 1)
    return a.reshape(shape[:axis] + (MLA_HEADS * MLA_QK,) + shape[axis + 1:])


def kernel(x, ret_norm, ret_w_in, ret_gn, ret_w_out, mla_norm, mla_w_in, mla_q_norm, mla_w_qb, mla_kv_norm, mla_w_kvb, mla_q_head_norm, mla_k_head_norm, mla_w_out, ffn_norm, ffn_w_in, ffn_conv_w, ffn_conv_b, ffn_w_out, loss_target, m_ret_norm, m_ret_w_in, m_ret_gn, m_ret_w_out, m_mla_norm, m_mla_w_in, m_mla_q_norm, m_mla_w_qb, m_mla_kv_norm, m_mla_w_kvb, m_mla_q_head_norm, m_mla_k_head_norm, m_mla_w_out, m_ffn_norm, m_ffn_w_in, m_ffn_conv_w, m_ffn_conv_b, m_ffn_w_out, v_ret_norm, v_ret_w_in, v_ret_gn, v_ret_w_out, v_mla_norm, v_mla_w_in, v_mla_q_norm, v_mla_w_qb, v_mla_kv_norm, v_mla_w_kvb, v_mla_q_head_norm, v_mla_k_head_norm, v_mla_w_out, v_ffn_norm, v_ffn_w_in, v_ffn_conv_w, v_ffn_conv_b, v_ffn_w_out):
    names = ["ret_norm", "ret_w_in", "ret_gn", "ret_w_out", "mla_norm", "mla_w_in", "mla_q_norm", "mla_w_qb",
             "mla_kv_norm", "mla_w_kvb", "mla_q_head_norm", "mla_k_head_norm", "mla_w_out", "ffn_norm", "ffn_w_in",
             "ffn_conv_w", "ffn_conv_b", "ffn_w_out"]
    shard = dict(zip(names, [ret_norm, ret_w_in, ret_gn, ret_w_out, mla_norm, mla_w_in, mla_q_norm, mla_w_qb,
                             mla_kv_norm, mla_w_kvb, mla_q_head_norm, mla_k_head_norm, mla_w_out, ffn_norm, ffn_w_in,
                             ffn_conv_w, ffn_conv_b, ffn_w_out]))
    mom_m = dict(zip(names, [m_ret_norm, m_ret_w_in, m_ret_gn, m_ret_w_out, m_mla_norm, m_mla_w_in, m_mla_q_norm,
                             m_mla_w_qb, m_mla_kv_norm, m_mla_w_kvb, m_mla_q_head_norm, m_mla_k_head_norm, m_mla_w_out,
                             m_ffn_norm, m_ffn_w_in, m_ffn_conv_w, m_ffn_conv_b, m_ffn_w_out]))
    mom_v = dict(zip(names, [v_ret_norm, v_ret_w_in, v_ret_gn, v_ret_w_out, v_mla_norm, v_mla_w_in, v_mla_q_norm,
                             v_mla_w_qb, v_mla_kv_norm, v_mla_w_kvb, v_mla_q_head_norm, v_mla_k_head_norm, v_mla_w_out,
                             v_ffn_norm, v_ffn_w_in, v_ffn_conv_w, v_ffn_conv_b, v_ffn_w_out]))
    B, S, D = x.shape
    T = B * S
    sx, sy = lax.axis_index("x"), lax.axis_index("y")
    me = 2 * sx + sy

    big_names = [n for n, _ in _BIG]
    two_d = lambda a: a.reshape(-1, a.shape[-1])
    small_sizes = [int(np.prod(shard[n].shape)) for n, _ in _SMALL_SHARDED]
    small = jnp.concatenate([shard[n].reshape(1, -1) for n, _ in _SMALL_SHARDED], axis=1)
    small = _pad_rows(small, LANES, 8)[0]
    bigs = [two_d(shard[n]).astype(BF16) for n in big_names]
    *gbig, gsmall = _all_gather_weights(bigs, small)
    is_me = lax.broadcasted_iota(jnp.int32, (N_SHARD, 1, 1), 0) == me
    with_own = lambda gathered, own: jnp.where(is_me, own[None], gathered)
    gw = {n: with_own(g_, b_) for n, g_, b_ in zip(big_names, gbig, bigs)}
    gsmall = with_own(gsmall, small).reshape(N_SHARD, -1)
    by_cols = lambda a: jnp.moveaxis(a, 0, 1).reshape(a.shape[1], -1)
    by_rows = lambda a: a.reshape(-1, a.shape[-1])
    wfull = {}
    off = 0
    for (n, ax), sz in zip(_SMALL_SHARDED, small_sizes):
        wfull[n] = _from_slots(gsmall[:, off:off + sz], shard[n].shape, ax)
        off += sz
    for n in _SMALL_REPLICATED:
        wfull[n] = shard[n]

    conv8 = jnp.concatenate([wfull["ffn_conv_w"], wfull["ffn_conv_b"][:, None, :],
                             jnp.zeros((2, 4, FFN_DIM), F32)], axis=1)
    ffn_in_rows, ffn_out_rows = shard["ffn_w_in"].shape[1], shard["ffn_w_out"].shape[1]
    w = {
        "ret_norm": wfull["ret_norm"], "ret_w_in": by_cols(gw["ret_w_in"]),
        "ret_gn": wfull["ret_gn"].reshape(1, RET_HEADS * RET_V), "ret_w_out": by_rows(gw["ret_w_out"]),
        "mla_norm": wfull["mla_norm"],
        "mla_w_in": jnp.pad(by_rows(gw["mla_w_in"]), ((0, 0), (0, MLA_IN_PAD - MLA_IN))),
        "mla_q_norm": wfull["mla_q_norm"], "mla_w_qb": _pad_heads(by_cols(gw["mla_w_qb"]), 1),
        "mla_kv_norm": wfull["mla_kv_norm"], "mla_w_kvb": by_cols(gw["mla_w_kvb"]),
        "mla_q_head_norm": jnp.pad(wfull["mla_q_head_norm"], ((0, 0), (0, MLA_PAD - MLA_QK))),
        "mla_k_head_norm": jnp.pad(wfull["mla_k_head_norm"], ((0, 0), (0, MLA_PAD - MLA_QK))),
        "mla_w_out": by_rows(gw["mla_w_out"]), "ffn_norm": wfull["ffn_norm"],
        "ffn_w_in": [by_cols(gw["ffn_w_in"][:, i * ffn_in_rows:(i + 1) * ffn_in_rows]) for i in range(2)],
        "ffn_conv8": conv8,
        "ffn_w_out": [by_rows(gw["ffn_w_out"][:, i * ffn_out_rows:(i + 1) * ffn_out_rows]) for i in range(2)],
    }

    loss_part, dx, gl = _local_step(x.reshape(T, D), loss_target.reshape(T, D), w, B, S)
    loss = lax.psum(loss_part, ("x", "y", "c"))
    gfull = {
        "ret_norm": gl["ret_norm"], "ret_gn": gl["ret_gn"].reshape(1, RET_HEADS, RET_V),
        "mla_norm": gl["mla_norm"], "mla_q_norm": gl["mla_q_norm"], "mla_kv_norm": gl["mla_kv_norm"],
        "mla_q_head_norm": gl["mla_q_head_norm"][:, :MLA_QK], "mla_k_head_norm": gl["mla_k_head_norm"][:, :MLA_QK],
        "ffn_norm": gl["ffn_norm"], "ffn_conv_w": gl["ffn_conv_w"], "ffn_conv_b": gl["ffn_conv_b"],
    }

    slot_rows = lambda a: a.reshape(N_SHARD, -1, a.shape[-1])
    qb_slots = _to_slots(_unpad_heads(gl["mla_w_qb"], 1), 1).reshape((N_SHARD,) + shard["mla_w_qb"].shape[1:])
    gs = [gl["ret_w_in"], slot_rows(gl["ret_w_out"]), slot_rows(gl["mla_w_in"][:, :MLA_IN]), qb_slots,
          gl["mla_w_kvb"], slot_rows(gl["mla_w_out"]), gl["ffn_w_in"][0], gl["ffn_w_in"][1],
          slot_rows(gl["ffn_w_out"][0]), slot_rows(gl["ffn_w_out"][1])]
    from_sibling = _sibling_send_halves(gs)
    partials = [_chip_partial(g_, o_, f"grads_chip_partial_{i}") for i, (g_, o_) in enumerate(zip(gs, from_sibling))]
    from_chips = _exchange_partials(partials)
    halves = [_sum_partials(p_, o_, f"grads_sum_partials_{i}") for i, (p_, o_) in enumerate(zip(partials, from_chips))]
    red = [two_d(r) for r in _sibling_share(halves)]
    grads = {n: red[i].reshape(shard[n].shape) for i, n in enumerate(big_names[:6])}
    grads["ffn_w_in"] = jnp.stack([red[6], red[7]])
    grads["ffn_w_out"] = jnp.stack([red[8], red[9]])

    small_sizes_all = [int(np.prod(gfull[n].shape)) for n in _SMALL_ALL]
    gsm = jnp.concatenate([gfull[n].reshape(1, -1) for n in _SMALL_ALL], axis=1)
    gsm = _all_reduce_small(_pad_rows(gsm, LANES, 8)[0]).reshape(-1)

    sharded_axis = dict(_SMALL_SHARDED)
    off = 0
    for n, sz in zip(_SMALL_ALL, small_sizes_all):
        gn = gsm[off:off + sz].reshape(gfull[n].shape)
        off += sz
        if n in sharded_axis:
            ax = sharded_axis[n]
            width = shard[n].shape[ax]
            gn = lax.dynamic_slice_in_dim(gn, me * width, width, axis=ax)
        grads[n] = gn

    delta, new_m, new_v = {}, {}, {}
    for n, _ in _BIG:
        shp = shard[n].shape
        two_d = lambda a: a.reshape(-1, shp[-1])
        d_, m_, v_ = _adamw(two_d(shard[n]), two_d(grads[n]), two_d(mom_m[n]), two_d(mom_v[n]), f"adamw_{n}")
        delta[n], new_m[n], new_v[n] = d_.reshape(shp), m_.reshape(shp), v_.reshape(shp)
    pack_small = lambda d: _pad_rows(jnp.concatenate([d[n].reshape(1, -1) for n in _SMALL_ALL], axis=1), LANES, 8)[0]
    d_, m_, v_ = _adamw(pack_small(shard), pack_small(grads), pack_small(mom_m), pack_small(mom_v), "adamw_small")
    off = 0
    for n in _SMALL_ALL:
        sz = int(np.prod(shard[n].shape))
        for dst, src in ((delta, d_), (new_m, m_), (new_v, v_)):
            dst[n] = src.reshape(-1)[off:off + sz].reshape(shard[n].shape)
        off += sz

    return (loss, dx.reshape(B, S, D), *[grads[n] for n in names], *[delta[n] for n in names],
            *[new_m[n] for n in names], *[new_v[n] for n in names])
```

```python
import functools
import math

import numpy as np
import jax
import jax.numpy as jnp
from jax import lax
from jax.experimental import pallas as pl
from jax.experimental.pallas import tpu as pltpu

F32 = jnp.float32
BF16 = jnp.bfloat16
MXU_DTYPE = jnp.bfloat16

CHUNK = 64
RMS_EPS = 1e-6
ROPE_THETA = 10000.0
D_MODEL = 1024
RET_HEADS = 4
RET_QK = 256
RET_V = 512
RET_GAMMA_BASE = -5.0
MLA_HEADS = 8
MLA_Q_RANK = 384
MLA_KV_RANK = 256
MLA_NOPE = 128
MLA_ROPE = 64
MLA_V = 128
MLA_QK = MLA_NOPE + MLA_ROPE
MLA_PAD = 256
MLA_IN = MLA_Q_RANK + MLA_KV_RANK + MLA_ROPE
MLA_IN_PAD = MLA_IN + 64
MASK_VALUE = -1e30
FFN_DIM = 2816
ADAM_LR = 0.001
ADAM_B1 = 0.9
ADAM_B2 = 0.999
ADAM_EPS = 1e-08
ADAM_WD = 0.01
ADAM_STEP = 10

LANES = 128
ATT_BLOCK = 256
VMEM_LIMIT = 56 * 2 ** 20
N_SHARD = 4
N_DEV = 8

MESH = pl.DeviceIdType.MESH


def _params(sem=None, **kw):
    return pltpu.CompilerParams(dimension_semantics=sem, vmem_limit_bytes=VMEM_LIMIT, **kw)


def _pick(dim, target):
    if dim <= target:
        return dim
    best = None
    for d in range(LANES, target + 1, LANES):
        if dim % d == 0:
            best = d
    assert best is not None, (dim, target)
    return best


def _mm(a, b, dims, out_dtype, name, residual=None, bm=512, bn=1024, bk=2048, out_slots=None):
    a_parts = list(a) if isinstance(a, (list, tuple)) else [a]
    b_parts = list(b) if isinstance(b, (list, tuple)) else [b]
    if dims == "tn":
        assert len(a_parts) == 1
        K, M = a_parts[0].shape
        N = sum(p.shape[1] for p in b_parts)
        part_widths = [p.shape[1] for p in b_parts]
    else:
        assert len(b_parts) == 1
        M = a_parts[0].shape[0]
        K = sum(p.shape[1] for p in a_parts)
        N = b_parts[0].shape[1 if dims == "nn" else 0]
        part_widths = [p.shape[1] for p in a_parts]
    bm, bn, bk = _pick(M, bm), _pick(N, bn), _pick(K, min(bk, 1024) if dims == "tn" else bk)
    nk = K // bk
    unit = bn if dims == "tn" else bk
    assert all(wd % unit == 0 for wd in part_widths), (name, part_widths, unit)
    bounds = np.cumsum([0] + [wd // unit for wd in part_widths])
    ranges = [(int(lo), int(hi)) for lo, hi in zip(bounds[:-1], bounds[1:])]

    def part_index(idx, lo, hi):
        return jnp.clip(idx - lo, 0, hi - lo - 1)

    if dims == "tn":
        a_specs = [pl.BlockSpec((bk, bm), lambda i, j, k: (k, i))]
        b_specs = [pl.BlockSpec((bk, bn), functools.partial(lambda i, j, k, lo, hi: (k, part_index(j, lo, hi)), lo=lo, hi=hi))
                   for lo, hi in ranges]
        dn = (((0,), (0,)), ((), ()))
    else:
        a_specs = [pl.BlockSpec((bm, bk), functools.partial(lambda i, j, k, lo, hi: (i, part_index(k, lo, hi)), lo=lo, hi=hi))
                   for lo, hi in ranges]
        if dims == "nt":
            b_specs = [pl.BlockSpec((bn, bk), lambda i, j, k: (j, k))]
        else:
            b_specs = [pl.BlockSpec((bk, bn), lambda i, j, k: (k, j))]
        dn = (((1,), (1 if dims == "nt" else 0,)), ((), ()))
    r_spec = pl.BlockSpec((bm, bn), lambda i, j, k: (i, j))
    if out_slots is None:
        o_spec, o_shape = r_spec, (M, N)
    else:
        ns = N // out_slots
        assert ns % bn == 0, (name, ns, bn)
        nbs = ns // bn
        o_spec = pl.BlockSpec((None, bm, bn), lambda i, j, k: (j // nbs, i, j % nbs))
        o_shape = (out_slots, M, ns)
    has_res = residual is not None
    na, nb = len(a_parts), len(b_parts)

    def body(*refs):
        a_refs, b_refs = refs[:na], refs[na:na + nb]
        r_ref = refs[na + nb] if has_res else None
        o_ref = refs[na + nb + has_res]
        acc_ref = refs[na + nb + has_res + 1] if nk > 1 else None
        k = pl.program_id(2)

        def finish(acc):
            if has_res:
                acc = acc + r_ref[...].astype(F32)
            o_ref[...] = acc.astype(out_dtype)

        def compute(a_ref, b_ref):
            p = lax.dot_general(a_ref[...].astype(MXU_DTYPE), b_ref[...].astype(MXU_DTYPE), dn,
                                preferred_element_type=F32)
            if nk == 1:
                finish(p)
                return

            @pl.when(k == 0)
            def _():
                acc_ref[...] = p

            @pl.when(jnp.logical_and(k > 0, k < nk - 1))
            def _():
                acc_ref[...] += p

            @pl.when(k == nk - 1)
            def _():
                finish(acc_ref[...] + p)

        if len(ranges) == 1:
            compute(a_refs[0], b_refs[0])
        else:
            idx = pl.program_id(1) if dims == "tn" else k
            for p, (lo, hi) in enumerate(ranges):
                @pl.when(jnp.logical_and(idx >= lo, idx < hi))
                def _(p=p):
                    compute(a_refs[0 if dims == "tn" else p], b_refs[p if dims == "tn" else 0])

    return pl.pallas_call(
        body, name=name, grid=(M // bm, N // bn, nk),
        in_specs=a_specs + b_specs + ([r_spec] if has_res else []), out_specs=o_spec,
        out_shape=jax.ShapeDtypeStruct(o_shape, out_dtype),
        scratch_shapes=[pltpu.VMEM((bm, bn), F32)] if nk > 1 else [],
        compiler_params=_params(("parallel", "parallel", "arbitrary")),
    )(*a_parts, *b_parts, *((residual,) if has_res else ()))


def _tiles(ref, width, tile):
    return [ref[:, t * tile:(t + 1) * tile].astype(F32) for t in range(width // tile)]


def _row_specs(rows, pos, consts, bm, S):
    npos_blocks = S // bm
    specs = [pl.BlockSpec((bm, w), functools.partial(lambda i, c: (i, c), c=cb)) for (_, w, cb, _) in rows]
    specs += [pl.BlockSpec((bm, p.shape[1]), lambda i: (i % npos_blocks, 0)) for p in pos]
    specs += [pl.BlockSpec(c.shape, lambda i: (0, 0)) for (c, _) in consts]
    return specs


def _rowwise_fwd(fn, name, rows, pos, consts, outs, bm, S):
    T = rows[0][0].shape[0]
    nr, npos, nc = len(rows), len(pos), len(consts)

    def body(*refs):
        row_v = [_tiles(r, w, t) for r, (_, w, _, t) in zip(refs[:nr], rows)]
        pos_v = [r[...] for r in refs[nr:nr + npos]]
        const_v = [_tiles(r, c.shape[1], t) for r, (c, t) in zip(refs[nr + npos:nr + npos + nc], consts)]
        res = fn(row_v, pos_v, const_v)
        for o_ref, tiles, (w, t, dt) in zip(refs[nr + npos + nc:], res, outs):
            for k, v in enumerate(tiles):
                o_ref[:, k * t:(k + 1) * t] = v.astype(dt)

    return pl.pallas_call(
        body, name=name, grid=(T // bm,),
        in_specs=_row_specs(rows, pos, consts, bm, S),
        out_specs=[pl.BlockSpec((bm, w), lambda i: (i, 0)) for (w, _, _) in outs],
        out_shape=[jax.ShapeDtypeStruct((T, w), dt) for (w, _, dt) in outs],
        compiler_params=_params(("parallel",)),
    )(*[r[0] for r in rows], *pos, *[c[0] for c in consts])


def _rowwise_bwd(fn, name, rows, pos, consts, cts, bm, S, adds=None):
    adds = adds or {}
    T = rows[0][0].shape[0]
    nr, npos, nc, nct = len(rows), len(pos), len(consts), len(cts)
    add_idx = sorted(adds)

    def body(*refs):
        it = iter(refs)
        row_refs = [next(it) for _ in range(nr)]
        pos_refs = [next(it) for _ in range(npos)]
        const_refs = [next(it) for _ in range(nc)]
        ct_refs = [next(it) for _ in range(nct)]
        add_refs = {k: next(it) for k in add_idx}
        drow_refs = [next(it) for _ in range(nr)]
        dconst_refs = [next(it) for _ in range(nc)]
        row_v = [_tiles(r, w, t) for r, (_, w, _, t) in zip(row_refs, rows)]
        pos_v = [r[...] for r in pos_refs]
        const_v = [_tiles(r, c.shape[1], t) for r, (c, t) in zip(const_refs, consts)]
        ct_v = [_tiles(r, c.shape[1], t) for r, (c, t) in zip(ct_refs, cts)]
        _, vjp = jax.vjp(lambda rv, cv: fn(rv, pos_v, cv), row_v, const_v)
        drows, dconsts = vjp(ct_v)
        for a, (d_ref, tiles, (_, w, _, t)) in enumerate(zip(drow_refs, drows, rows)):
            for k, v in enumerate(tiles):
                if a in add_refs:
                    v = v + add_refs[a][:, k * t:(k + 1) * t].astype(F32)
                d_ref[:, k * t:(k + 1) * t] = v
        first = pl.program_id(0) == 0
        for d_ref, tiles, (_, t) in zip(dconst_refs, dconsts, consts):
            for k, v in enumerate(tiles):
                @pl.when(first)
                def _(d_ref=d_ref, k=k, t=t, v=v):
                    d_ref[:, k * t:(k + 1) * t] = v

                @pl.when(jnp.logical_not(first))
                def _(d_ref=d_ref, k=k, t=t, v=v):
                    d_ref[:, k * t:(k + 1) * t] += v

    in_specs = _row_specs(rows, pos, consts, bm, S)
    in_specs += [pl.BlockSpec((bm, c.shape[1]), lambda i: (i, 0)) for (c, _) in cts]
    in_specs += [pl.BlockSpec((bm, adds[k].shape[1]), lambda i: (i, 0)) for k in add_idx]
    out_specs = [pl.BlockSpec((bm, w), lambda i: (i, 0)) for (_, w, _, _) in rows]
    out_specs += [pl.BlockSpec(c.shape, lambda i: (0, 0)) for (c, _) in consts]
    out_shape = [jax.ShapeDtypeStruct((T, w), F32) for (_, w, _, _) in rows]
    out_shape += [jax.ShapeDtypeStruct(c.shape, F32) for (c, _) in consts]
    res = pl.pallas_call(
        body, name=name, grid=(T // bm,),
        in_specs=in_specs, out_specs=out_specs, out_shape=out_shape,
        compiler_params=_params(("arbitrary",)),
    )(*[r[0] for r in rows], *pos, *[c[0] for c in consts], *[c[0] for c in cts], *[adds[k] for k in add_idx])
    return res[:nr], res[nr:]


def _ssq(tiles):
    s = jnp.sum(tiles[0] * tiles[0], axis=-1, keepdims=True)
    for t in tiles[1:]:
        s = s + jnp.sum(t * t, axis=-1, keepdims=True)
    return s


def _sigmoid(x):
    return 1.0 / (1.0 + jnp.exp(-x))


def _fn_rms(rows, pos, consts):
    (x,), (g,) = rows[0], consts[0]
    r = lax.rsqrt(jnp.mean(x * x, axis=-1, keepdims=True) + RMS_EPS)
    return [[x * r * g]]


def _fn_ret_rope(rows, pos, consts):
    (qkv,) = rows
    nq = RET_HEADS * RET_QK // LANES
    q, k, v = qkv[:nq], qkv[nq:2 * nq], qkv[2 * nq:]
    cos, sin = pos

    def rot(t, scale):
        out = []
        for h in range(RET_HEADS):
            x1, x2 = t[2 * h], t[2 * h + 1]
            o1, o2 = x1 * cos - x2 * sin, x2 * cos + x1 * sin
            out += [o1, o2] if scale is None else [o1 * scale, o2 * scale]
        return out

    return [rot(q, None), rot(k, RET_QK ** -0.5), list(v)]


def _fn_ret_gate(rows, pos, consts):
    o, g = rows
    (gn,) = consts
    out = []
    for h in range(RET_HEADS):
        r = lax.rsqrt(jnp.mean(o[h] * o[h], axis=-1, keepdims=True) + RMS_EPS)
        out.append((o[h] * r * gn[h]) * (g[h] * _sigmoid(g[h])))
    return [out]


def _fn_mla_lat(rows, pos, consts):
    (p,) = rows
    gq, gkv = consts
    nq, nkv = MLA_Q_RANK // LANES, MLA_KV_RANK // LANES
    cq, ckv, kr = p[:nq], p[nq:nq + nkv], p[nq + nkv]
    rq = lax.rsqrt(_ssq(cq) / MLA_Q_RANK + RMS_EPS)
    rkv = lax.rsqrt(_ssq(ckv) / MLA_KV_RANK + RMS_EPS)
    return [[t * rq * g for t, g in zip(cq, gq)], [t * rkv * g for t, g in zip(ckv, gkv)], [kr]]


def _swap32_impl(x):
    lane = lax.broadcasted_iota(jnp.int32, x.shape, 1)
    up, down = pltpu.roll(x, LANES - 32, 1), pltpu.roll(x, 32, 1)
    return jnp.where(lane < 32, up, jnp.where(lane < 64, down, 0.0))


@jax.custom_vjp
def _swap32(x):
    return _swap32_impl(x)


_swap32.defvjp(lambda x: (_swap32_impl(x), None), lambda _, g: (_swap32_impl(g),))


def _fn_mla_heads(rows, pos, consts):
    qf, kvf, (kr,) = rows
    cos, sin = pos
    gq, gk = consts
    q_out, k_out, v_out = [], [], []
    for h in range(MLA_HEADS):
        q0, q1 = qf[2 * h], qf[2 * h + 1]
        r = lax.rsqrt(_ssq([q0, q1]) / MLA_QK + RMS_EPS)
        a0, a1 = q0 * r * gq[0], q1 * r * gq[1]
        a1 = a1 * cos + _swap32(a1) * sin
        q_out += [a0 * (MLA_QK ** -0.5), a1 * (MLA_QK ** -0.5)]
        k0 = kvf[2 * h]
        r = lax.rsqrt(_ssq([k0, kr]) / MLA_QK + RMS_EPS)
        b0, b1 = k0 * r * gk[0], kr * r * gk[1]
        k_out += [b0, b1 * cos + _swap32(b1) * sin]
        v_out.append(kvf[2 * h + 1])
    return [q_out, k_out, v_out]


def _shift_down(x, n):
    row = lax.broadcasted_iota(jnp.int32, x.shape, 0)
    return jnp.where(row >= n, pltpu.roll(x, n, 0), 0.0)


def _shift_up(x, n):
    rows = x.shape[0]
    row = lax.broadcasted_iota(jnp.int32, x.shape, 0)
    return jnp.where(row < rows - n, pltpu.roll(x, rows - n, 0), 0.0)


def _conv_blocks(S):
    cb = 256
    return cb, FFN_DIM // cb


def _conv_fwd(ag, w8, B, S, name):
    cb, ncb = _conv_blocks(S)

    def body(a_ref, g_ref, w_ref, u_ref):
        g = g_ref[...]
        w = w_ref[...]
        gc = w[0:1] * _shift_down(g, 2) + w[1:2] * _shift_down(g, 1) + w[2:3] * g + w[3:4]
        u_ref[...] = (a_ref[...] * (gc * _sigmoid(gc))).astype(u_ref.dtype)

    return pl.pallas_call(
        body, name=name, grid=(ncb, B),
        in_specs=[pl.BlockSpec((S, cb), lambda j, b: (b, j)),
                  pl.BlockSpec((S, cb), lambda j, b: (b, ncb + j)),
                  pl.BlockSpec((8, cb), lambda j, b: (0, j))],
        out_specs=pl.BlockSpec((S, cb), lambda j, b: (b, j)),
        out_shape=jax.ShapeDtypeStruct((B * S, FFN_DIM), BF16),
        compiler_params=_params(("parallel", "parallel")),
    )(ag, ag, w8)


def _conv_bwd(ag, w8, du, B, S, name):
    cb, ncb = _conv_blocks(S)

    def body(a_ref, g_ref, w_ref, du_ref, da_ref, dg_ref, dw_ref):
        g = g_ref[...]
        w = w_ref[...]
        g1, g2 = _shift_down(g, 1), _shift_down(g, 2)
        gc = w[0:1] * g2 + w[1:2] * g1 + w[2:3] * g + w[3:4]
        sg = _sigmoid(gc)
        du_v = du_ref[...]
        da_ref[...] = du_v * (gc * sg)
        dgc = du_v * a_ref[...] * (sg * (1.0 + gc * (1.0 - sg)))
        dg_ref[...] = w[2:3] * dgc + w[1:2] * _shift_up(dgc, 1) + w[0:1] * _shift_up(dgc, 2)
        part = jnp.concatenate([
            jnp.sum(dgc * g2, axis=0, keepdims=True), jnp.sum(dgc * g1, axis=0, keepdims=True),
            jnp.sum(dgc * g, axis=0, keepdims=True), jnp.sum(dgc, axis=0, keepdims=True),
            jnp.zeros((4, cb), F32)], axis=0)

        @pl.when(pl.program_id(1) == 0)
        def _():
            dw_ref[...] = part

        @pl.when(pl.program_id(1) > 0)
        def _():
            dw_ref[...] += part

    blk = lambda j, b: (b, j)
    return pl.pallas_call(
        body, name=name, grid=(ncb, B),
        in_specs=[pl.BlockSpec((S, cb), blk),
                  pl.BlockSpec((S, cb), lambda j, b: (b, ncb + j)),
                  pl.BlockSpec((8, cb), lambda j, b: (0, j)),
                  pl.BlockSpec((S, cb), blk)],
        out_specs=[pl.BlockSpec((S, cb), blk), pl.BlockSpec((S, cb), blk),
                   pl.BlockSpec((8, cb), lambda j, b: (0, j))],
        out_shape=[jax.ShapeDtypeStruct((B * S, FFN_DIM), F32), jax.ShapeDtypeStruct((B * S, FFN_DIM), F32),
                   jax.ShapeDtypeStruct((8, FFN_DIM), F32)],
        compiler_params=_params(("parallel", "arbitrary")),
    )(ag, ag, w8, du)


_NT = (((1,), (1,)), ((), ()))
_NN = (((1,), (0,)), ((), ()))
_TN = (((0,), (0,)), ((), ()))


def _dot(a, b, dn):
    return lax.dot_general(a.astype(MXU_DTYPE), b.astype(MXU_DTYPE), dn, preferred_element_type=F32)


def _rel_and_mask():
    il = lax.broadcasted_iota(jnp.int32, (ATT_BLOCK, ATT_BLOCK), 0)
    jl = lax.broadcasted_iota(jnp.int32, (ATT_BLOCK, ATT_BLOCK), 1)
    return (il - jl).astype(F32), (jl // CHUNK) <= (il // CHUNK)


def _rows(i):
    return pl.ds(pl.multiple_of(i * ATT_BLOCK, ATT_BLOCK), ATT_BLOCK)


KV_UNROLL = 2


def _kv_loop(n, body, carry):
    main = n // KV_UNROLL

    def chunk(t, c):
        for u in range(KV_UNROLL):
            c = body(t * KV_UNROLL + u, c)
        return c

    carry = lax.fori_loop(0, main, chunk, carry)
    return lax.fori_loop(main * KV_UNROLL, n, body, carry)


def _mla_attn_fwd(q, k, v, B, S):
    H, nq = MLA_HEADS, S // ATT_BLOCK

    def body(q_ref, k_ref, v_ref, o_ref, lse_ref):
        _, mask = _rel_and_mask()

        def qblock(i, _):
            qi = q_ref[_rows(i), :]

            def kv(j, carry, diag):
                m, l, acc = carry
                s = _dot(qi, k_ref[_rows(j), :], _NT)
                if diag:
                    s = jnp.where(mask, s, MASK_VALUE)
                m2 = jnp.maximum(m, jnp.max(s, axis=-1, keepdims=True))
                alpha = jnp.exp(m - m2)
                p = jnp.exp(s - m2)
                l2 = alpha * l + jnp.sum(p, axis=-1, keepdims=True)
                return m2, l2, alpha * acc + _dot(p, v_ref[_rows(j), :], _NN)

            init = (jnp.full((ATT_BLOCK, 1), MASK_VALUE, F32), jnp.zeros((ATT_BLOCK, 1), F32),
                    jnp.zeros((ATT_BLOCK, MLA_V), F32))
            carry = _kv_loop(i, lambda j, c: kv(j, c, False), init)
            m, l, acc = kv(i, carry, True)
            o_ref[_rows(i), :] = acc / l
            lse_ref[0, _rows(i), :] = m + jnp.log(l)
            return 0

        lax.fori_loop(0, nq, qblock, 0)

    return pl.pallas_call(
        body, name="mla_attn_fwd", grid=(B, H),
        in_specs=[pl.BlockSpec((S, MLA_PAD), lambda b, h: (b, h)),
                  pl.BlockSpec((S, MLA_PAD), lambda b, h: (b, h)),
                  pl.BlockSpec((S, MLA_V), lambda b, h: (b, h))],
        out_specs=[pl.BlockSpec((S, MLA_V), lambda b, h: (b, h)),
                   pl.BlockSpec((1, S, 1), lambda b, h: (b * H + h, 0, 0))],
        out_shape=[jax.ShapeDtypeStruct((B * S, H * MLA_V), F32), jax.ShapeDtypeStruct((B * H, S, 1), F32)],
        compiler_params=_params(("parallel", "parallel")),
    )(q, k, v)


def _mla_attn_bwd(q, k, v, o, do, lse, B, S):
    H, nq = MLA_HEADS, S // ATT_BLOCK

    def body(q_ref, k_ref, v_ref, o_ref, do_ref, lse_ref, dq_ref, dk_ref, dv_ref, acc_ref):
        _, mask = _rel_and_mask()
        dk_ref[...] = jnp.zeros(dk_ref.shape, F32)
        dv_ref[...] = jnp.zeros(dv_ref.shape, F32)

        def qblock(i, _):
            qi = q_ref[_rows(i), :]
            doi = do_ref[_rows(i), :]
            delta = jnp.sum(doi * o_ref[_rows(i), :], axis=-1, keepdims=True)
            lse_i = lse_ref[0, _rows(i), :]
            doi = doi.astype(MXU_DTYPE)
            acc_ref[...] = jnp.zeros(acc_ref.shape, F32)

            def kv(j, diag):
                kj = k_ref[_rows(j), :]
                p = jnp.exp(_dot(qi, kj, _NT) - lse_i)
                if diag:
                    p = jnp.where(mask, p, 0.0)
                ds = (p * (_dot(doi, v_ref[_rows(j), :], _NT) - delta)).astype(MXU_DTYPE)
                acc_ref[...] += _dot(ds, kj, _NN)
                dk_ref[_rows(j), :] += _dot(ds, qi, _TN)
                dv_ref[_rows(j), :] += _dot(p, doi, _TN)

            def off(j, c):
                kv(j, False)
                return c

            _kv_loop(i, off, 0)
            kv(i, True)
            dq_ref[_rows(i), :] = acc_ref[...]
            return 0

        lax.fori_loop(0, nq, qblock, 0)

    qk_spec = pl.BlockSpec((S, MLA_PAD), lambda b, h: (b, h))
    v_spec = pl.BlockSpec((S, MLA_V), lambda b, h: (b, h))
    return pl.pallas_call(
        body, name="mla_attn_bwd", grid=(B, H),
        in_specs=[qk_spec, qk_spec, v_spec, v_spec, v_spec,
                  pl.BlockSpec((1, S, 1), lambda b, h: (b * H + h, 0, 0))],
        out_specs=[qk_spec, qk_spec, v_spec],
        out_shape=[jax.ShapeDtypeStruct((B * S, H * MLA_PAD), F32), jax.ShapeDtypeStruct((B * S, H * MLA_PAD), F32),
                   jax.ShapeDtypeStruct((B * S, H * MLA_V), F32)],
        scratch_shapes=[pltpu.VMEM((ATT_BLOCK, MLA_PAD), F32)],
        compiler_params=_params(("parallel", "parallel")),
    )(q, k, v, o, do, lse)


def _ret_log_gamma():
    lg = np.log1p(-np.exp2(RET_GAMMA_BASE - np.arange(RET_HEADS, dtype=np.float32))).astype(np.float32)
    return jnp.asarray(np.broadcast_to(lg[:, None, None], (RET_HEADS, 8, LANES)).copy())


def _ret_decay(lg, rel, mask, steps):
    if steps is None:
        return jnp.where(mask, jnp.exp(lg * jnp.abs(rel)), 0.0)
    return jnp.exp(lg * (rel + (steps * ATT_BLOCK).astype(F32)))


def _ret_attn_fwd(q, k, v, B, S):
    H, nq = RET_HEADS, S // ATT_BLOCK

    def body(lg_ref, q_ref, k_ref, v_ref, o_ref, acc_ref):
        rel, mask = _rel_and_mask()
        lg = lg_ref[0, 0:1, 0:1]

        def qblock(i, _):
            qi = q_ref[_rows(i), :]
            acc_ref[...] = jnp.zeros(acc_ref.shape, F32)

            def kv(j, steps):
                a = _dot(qi, k_ref[_rows(j), :], _NT) * _ret_decay(lg, rel, mask, steps)
                acc_ref[...] += _dot(a, v_ref[_rows(j), :], _NN)

            def off(j, c):
                kv(j, i - j)
                return c

            _kv_loop(i, off, 0)
            kv(i, None)
            o_ref[_rows(i), :] = acc_ref[...]
            return 0

        lax.fori_loop(0, nq, qblock, 0)

    qk_spec = pl.BlockSpec((S, RET_QK), lambda b, h: (b, h))
    v_spec = pl.BlockSpec((S, RET_V), lambda b, h: (b, h))
    return pl.pallas_call(
        body, name="ret_attn_fwd", grid=(B, H),
        in_specs=[pl.BlockSpec((1, 8, LANES), lambda b, h: (h, 0, 0)), qk_spec, qk_spec, v_spec],
        out_specs=v_spec,
        out_shape=jax.ShapeDtypeStruct((B * S, H * RET_V), F32),
        scratch_shapes=[pltpu.VMEM((ATT_BLOCK, RET_V), F32)],
        compiler_params=_params(("parallel", "parallel")),
    )(_ret_log_gamma(), q, k, v)


def _ret_attn_bwd(q, k, v, do, B, S):
    H, nq = RET_HEADS, S // ATT_BLOCK

    def body(lg_ref, q_ref, k_ref, v_ref, do_ref, dq_ref, dk_ref, dv_ref, acc_ref):
        rel, mask = _rel_and_mask()
        lg = lg_ref[0, 0:1, 0:1]
        dk_ref[...] = jnp.zeros(dk_ref.shape, F32)
        dv_ref[...] = jnp.zeros(dv_ref.shape, F32)

        def qblock(i, _):
            qi = q_ref[_rows(i), :]
            doi = do_ref[_rows(i), :].astype(MXU_DTYPE)
            acc_ref[...] = jnp.zeros(acc_ref.shape, F32)

            def kv(j, steps):
                kj = k_ref[_rows(j), :]
                dec = _ret_decay(lg, rel, mask, steps)
                a = _dot(qi, kj, _NT) * dec
                da = (_dot(doi, v_ref[_rows(j), :], _NT) * dec).astype(MXU_DTYPE)
                acc_ref[...] += _dot(da, kj, _NN)
                dk_ref[_rows(j), :] += _dot(da, qi, _TN)
                dv_ref[_rows(j), :] += _dot(a, doi, _TN)

            def off(j, c):
                kv(j, i - j)
                return c

            _kv_loop(i, off, 0)
            kv(i, None)
            dq_ref[_rows(i), :] = acc_ref[...]
            return 0

        lax.fori_loop(0, nq, qblock, 0)

    qk_spec = pl.BlockSpec((S, RET_QK), lambda b, h: (b, h))
    v_spec = pl.BlockSpec((S, RET_V), lambda b, h: (b, h))
    return pl.pallas_call(
        body, name="ret_attn_bwd", grid=(B, H),
        in_specs=[pl.BlockSpec((1, 8, LANES), lambda b, h: (h, 0, 0)), qk_spec, qk_spec, v_spec, v_spec],
        out_specs=[qk_spec, qk_spec, v_spec],
        out_shape=[jax.ShapeDtypeStruct((B * S, H * RET_QK), F32), jax.ShapeDtypeStruct((B * S, H * RET_QK), F32),
                   jax.ShapeDtypeStruct((B * S, H * RET_V), F32)],
        scratch_shapes=[pltpu.VMEM((ATT_BLOCK, RET_QK), F32)],
        compiler_params=_params(("parallel", "parallel")),
    )(_ret_log_gamma(), q, k, v, do)


def _loss_head(y, target, bm=512):
    T, D = y.shape
    bm = _pick(T, bm)

    def body(y_ref, t_ref, dy_ref, l_ref):
        err = y_ref[...] - t_ref[...]
        dy_ref[...] = err / D
        part = jnp.full((8, LANES), 0.5 * jnp.sum(jnp.mean(err * err, axis=-1)), F32)

        @pl.when(pl.program_id(0) == 0)
        def _():
            l_ref[...] = part

        @pl.when(pl.program_id(0) > 0)
        def _():
            l_ref[...] += part

    blk = pl.BlockSpec((bm, D), lambda i: (i, 0))
    dy, l = pl.pallas_call(
        body, name="loss_head", grid=(T // bm,),
        in_specs=[blk, blk], out_specs=[blk, pl.BlockSpec((8, LANES), lambda i: (0, 0))],
        out_shape=[jax.ShapeDtypeStruct((T, D), F32), jax.ShapeDtypeStruct((8, LANES), F32)],
        compiler_params=_params(("arbitrary",)),
    )(y, target)
    return dy, l[0, 0]


def _adamw(w, g, m, v, name):
    R, C = w.shape
    br = R if R * C * 4 <= 2 ** 21 else _pick_rows(R, max(8, (2 ** 21) // (C * 4)))

    def body(w_ref, g_ref, m_ref, v_ref, d_ref, mo_ref, vo_ref):
        g_v = g_ref[...]
        m_v = ADAM_B1 * m_ref[...] + (1.0 - ADAM_B1) * g_v
        v_v = ADAM_B2 * v_ref[...] + (1.0 - ADAM_B2) * (g_v * g_v)
        m_hat = m_v / (1.0 - ADAM_B1 ** ADAM_STEP)
        v_hat = v_v / (1.0 - ADAM_B2 ** ADAM_STEP)
        d_ref[...] = -ADAM_LR * (m_hat / (jnp.sqrt(v_hat) + ADAM_EPS) + ADAM_WD * w_ref[...])
        mo_ref[...] = m_v
        vo_ref[...] = v_v

    blk = pl.BlockSpec((br, C), lambda i: (i, 0))
    return pl.pallas_call(
        body, name=name, grid=(R // br,),
        in_specs=[blk] * 4, out_specs=[blk] * 3,
        out_shape=[jax.ShapeDtypeStruct((R, C), F32)] * 3,
        compiler_params=_params(("parallel",)),
    )(w, g, m, v)


def _pick_rows(R, target):
    best = None
    for d in range(8, min(R, target) + 1, 8):
        if R % d == 0:
            best = d
    assert best is not None, (R, target)
    return best


def _position():
    return lax.axis_index("x"), lax.axis_index("y"), lax.axis_index("c")


HBM_SPEC = pl.BlockSpec(memory_space=pltpu.HBM)


def _other_chips(x, y):
    return [(1 - x, y), (x, 1 - y), (1 - x, 1 - y)]


def _all_gather_weights(bigs, small):
    nb = len(bigs)

    def body(*refs):
        big_refs, small_ref = refs[:nb], refs[nb]
        obig, osmall = refs[nb + 1:2 * nb + 1], refs[2 * nb + 1]
        ici_send, ici_recv, d2d_send, d2d_recv, sm_send, sm_recv = refs[2 * nb + 2:]
        x, y, c = _position()
        me = 2 * x + y
        chips = _other_chips(x, y)

        def rows(n, half):
            rh = bigs[n].shape[0] // 2
            return pl.ds(half * rh, rh)

        def over_ici(n, j, slot, from_shard):
            px, py = chips[j]
            dst = obig[n].at[slot, rows(n, c)]
            return pltpu.make_async_remote_copy(
                src_ref=big_refs[n].at[rows(n, c)] if from_shard else dst, dst_ref=dst,
                send_sem=ici_send.at[3 * n + j], recv_sem=ici_recv.at[3 * n + j],
                device_id=(px, py, c), device_id_type=MESH)

        def over_d2d(n, j, half):
            px, py = chips[j]
            part = obig[n].at[2 * px + py, rows(n, half)]
            return pltpu.make_async_remote_copy(
                src_ref=part, dst_ref=part, send_sem=d2d_send.at[3 * n + j], recv_sem=d2d_recv.at[3 * n + j],
                device_id=(x, y, 1 - c), device_id_type=MESH)

        def small_copy(j, slot):
            px, py = chips[j]
            return pltpu.make_async_remote_copy(
                src_ref=small_ref, dst_ref=osmall.at[slot], send_sem=sm_send.at[j], recv_sem=sm_recv.at[j],
                device_id=(px, py, c), device_id_type=MESH)

        sends = [over_ici(n, j, me, True) for n in range(nb) for j in range(3)]
        sends += [small_copy(j, me) for j in range(3)]
        for cp in sends:
            cp.start()
        passed = []
        for n in range(nb):
            for j, (px, py) in enumerate(chips):
                over_ici(n, j, 2 * px + py, False).wait_recv()
                fwd = over_d2d(n, j, c)
                fwd.start()
                passed.append(fwd)
        for n in range(nb):
            for j in range(3):
                over_d2d(n, j, 1 - c).wait_recv()
        for j, (px, py) in enumerate(chips):
            small_copy(j, 2 * px + py).wait_recv()
        for cp in sends + passed:
            cp.wait_send()

    dma = pltpu.SemaphoreType.DMA
    return pl.pallas_call(
        body, name="weights_all_gather",
        in_specs=[HBM_SPEC] * (nb + 1), out_specs=[HBM_SPEC] * (nb + 1),
        out_shape=[jax.ShapeDtypeStruct((N_SHARD,) + b.shape, b.dtype) for b in bigs]
        + [jax.ShapeDtypeStruct((N_SHARD,) + small.shape, small.dtype)],
        scratch_shapes=[dma((3 * nb,)), dma((3 * nb,)), dma((3 * nb,)), dma((3 * nb,)), dma((3,)), dma((3,))],
    )(*bigs, small)


SEM_SPEC = pl.BlockSpec(memory_space=pltpu.SEMAPHORE)
DATAFLOW_EFFECT = pltpu.SideEffectType.DATAFLOW_SIDE_EFFECTING
N_PEERS = N_DEV - 1


def _grad_copies(p_refs, land_refs, send_sems, recv_sems):
    x, y, c = _position()
    copies = []
    for a, (p_ref, land_ref) in enumerate(zip(p_refs, land_refs)):
        rh = p_ref.shape[1] // 2
        for k in range(1, N_DEV):
            px = 1 - x if k & 4 else x
            py = 1 - y if k & 2 else y
            pc = 1 - c if k & 1 else c
            copies.append(pltpu.make_async_remote_copy(
                src_ref=p_ref.at[2 * px + py, pl.ds(pc * rh, rh)], dst_ref=land_ref.at[k - 1],
                send_sem=send_sems.at[N_PEERS * a + k - 1], recv_sem=recv_sems.at[N_PEERS * a + k - 1],
                device_id=(px, py, pc), device_id_type=MESH))
    return copies


def _grads_exchange_start(ps, name):
    n = len(ps)
    lands = [lax.empty((N_PEERS, p.shape[1] // 2, p.shape[2]), p.dtype) for p in ps]

    def body(*refs):
        p_refs, land_refs = refs[:n], refs[n:2 * n]
        send_sems, recv_sems = refs[2 * n], refs[2 * n + 1]
        token = refs[-1]
        for cp in _grad_copies(p_refs, land_refs, send_sems, recv_sems):
            cp.start()
        token[...] = jnp.zeros(token.shape, token.dtype)

    hbm = lambda a: pltpu.with_memory_space_constraint(a, pltpu.HBM)
    dma = pltpu.SemaphoreType.DMA
    res = pl.pallas_call(
        body, name=name,
        in_specs=[HBM_SPEC] * (2 * n),
        out_specs=[SEM_SPEC, SEM_SPEC] + [HBM_SPEC] * (2 * n) + [pl.BlockSpec(memory_space=pltpu.VMEM)],
        out_shape=[dma((N_PEERS * n,)), dma((N_PEERS * n,))] + [pltpu.HBM(a.shape, a.dtype) for a in ps + lands]
        + [jax.ShapeDtypeStruct((8, LANES), F32)],
        input_output_aliases={i: 2 + i for i in range(2 * n)},
        compiler_params=pltpu.CompilerParams(has_side_effects=DATAFLOW_EFFECT),
    )(*[hbm(a) for a in ps], *[hbm(a) for a in lands])
    return res[0], res[1], list(res[2:2 + n]), list(res[2 + n:2 + 2 * n]), res[-1]


def _grads_exchange_wait(send_sems, recv_sems, ps, lands, after, name):
    n = len(ps)

    def body(*refs):
        p_refs, land_refs = refs[:n], refs[n:2 * n]
        send_ref, recv_ref = refs[2 * n], refs[2 * n + 1]
        for cp in _grad_copies(p_refs, land_refs, send_ref, recv_ref):
            cp.wait_send()
            cp.wait_recv()

    res = pl.pallas_call(
        body, name=name,
        in_specs=[HBM_SPEC] * (2 * n) + [SEM_SPEC, SEM_SPEC, pl.BlockSpec(memory_space=pl.ANY)],
        out_specs=[HBM_SPEC] * (2 * n),
        out_shape=[pltpu.HBM(a.shape, a.dtype) for a in ps + lands],
        input_output_aliases={i: i for i in range(2 * n)},
        compiler_params=pltpu.CompilerParams(has_side_effects=DATAFLOW_EFFECT),
    )(*ps, *lands, send_sems, recv_sems, after)
    return list(res[:n]), list(res[n:])


def _sum_partials(p, land, name):
    _, rh, cols = land.shape
    br = _pick_rows(rh, 256)
    nrb = rh // br
    x, y, c = _position()
    where = jnp.stack([2 * x + y, c]).astype(jnp.int32)

    def body(where_ref, p_ref, land_ref, o_ref):
        acc = p_ref[...].astype(F32)
        for k in range(N_PEERS):
            acc = acc + land_ref[k].astype(F32)
        o_ref[...] = acc

    return pl.pallas_call(
        body, name=name,
        grid_spec=pltpu.PrefetchScalarGridSpec(
            num_scalar_prefetch=1, grid=(nrb,),
            in_specs=[pl.BlockSpec((None, br, cols), lambda r, where_ref: (where_ref[0], where_ref[1] * nrb + r, 0)),
                      pl.BlockSpec((N_PEERS, br, cols), lambda r, where_ref: (0, r, 0))],
            out_specs=pl.BlockSpec((None, br, cols), lambda r, where_ref: (where_ref[1], r, 0))),
        out_shape=jax.ShapeDtypeStruct((2, rh, cols), F32),
        compiler_params=_params(("parallel",)),
    )(where, p, land)


def _sibling_share(fulls, name):
    n = len(fulls)

    def body(*refs):
        o_refs = refs[n:2 * n]
        send_sems, recv_sems = refs[2 * n:]
        x, y, c = _position()

        def copy(a, half):
            return pltpu.make_async_remote_copy(
                src_ref=o_refs[a].at[half], dst_ref=o_refs[a].at[half], send_sem=send_sems.at[a],
                recv_sem=recv_sems.at[a], device_id=(x, y, 1 - c), device_id_type=MESH)

        sends = [copy(a, c) for a in range(n)]
        for cp in sends:
            cp.start()
        for a in range(n):
            copy(a, 1 - c).wait_recv()
        for cp in sends:
            cp.wait_send()

    dma = pltpu.SemaphoreType.DMA
    return pl.pallas_call(
        body, name=name,
        in_specs=[HBM_SPEC] * n, out_specs=[HBM_SPEC] * n,
        out_shape=[jax.ShapeDtypeStruct(f.shape, f.dtype) for f in fulls],
        input_output_aliases={a: a for a in range(n)},
        scratch_shapes=[dma((n,)), dma((n,))],
    )(*fulls)


def _all_reduce_small(v):
    R, cols = v.shape

    def body(v_ref, o_ref, buf_ref, send_sems, recv_sems):
        x, y, c = _position()
        me = 4 * x + 2 * y + c
        buf_ref[me] = v_ref[...]
        sends = []
        for k in range(1, N_DEV):
            px = 1 - x if k & 4 else x
            py = 1 - y if k & 2 else y
            pc = 1 - c if k & 1 else c
            sends.append(pltpu.make_async_remote_copy(
                src_ref=v_ref, dst_ref=buf_ref.at[me], send_sem=send_sems.at[k - 1], recv_sem=recv_sems.at[k - 1],
                device_id=(px, py, pc), device_id_type=MESH))
        for cp in sends:
            cp.start()
        for k in range(1, N_DEV):
            px = 1 - x if k & 4 else x
            py = 1 - y if k & 2 else y
            pc = 1 - c if k & 1 else c
            pltpu.make_async_remote_copy(
                src_ref=v_ref, dst_ref=buf_ref.at[4 * px + 2 * py + pc], send_sem=send_sems.at[k - 1],
                recv_sem=recv_sems.at[k - 1], device_id=(px, py, pc), device_id_type=MESH).wait_recv()
        for cp in sends:
            cp.wait_send()
        acc = buf_ref[0]
        for d in range(1, N_DEV):
            acc = acc + buf_ref[d]
        o_ref[...] = acc

    return pl.pallas_call(
        body, name="small_grads_all_reduce",
        in_specs=[pl.BlockSpec(memory_space=pltpu.VMEM)], out_specs=pl.BlockSpec(memory_space=pltpu.VMEM),
        out_shape=jax.ShapeDtypeStruct((R, cols), F32),
        scratch_shapes=[pltpu.VMEM((N_DEV, R, cols), F32), pltpu.SemaphoreType.DMA((N_DEV - 1,)),
                        pltpu.SemaphoreType.DMA((N_DEV - 1,))],
    )(v)


def _rope_tables(S, half, width):
    inv_freq = ROPE_THETA ** (-jnp.arange(half, dtype=F32) / half)
    ang = jnp.arange(S).astype(F32)[:, None] * inv_freq[None, :]
    return jnp.cos(ang), jnp.sin(ang)


def _slot_rows(a):
    return a.reshape(N_SHARD, -1, a.shape[-1])


def _local_step(x, target, w, B, S, exchange):
    T = B * S
    D = D_MODEL
    bm = 256
    full = lambda a, wd, tile=None: (a, wd, 0, tile or wd)
    g = {}

    cos_r, sin_r = _rope_tables(S, RET_QK // 2, LANES)
    cos_m, sin_m = _rope_tables(S, MLA_ROPE // 2, LANES)
    zeros64 = jnp.zeros((S, 64), F32)
    cos_m = jnp.concatenate([cos_m, cos_m, zeros64], axis=1)
    sin_m = jnp.concatenate([-sin_m, sin_m, zeros64], axis=1)

    def ffn_fwd(xin, i):
        norm = w["ffn_norm"][i:i + 1]
        (h,) = _rowwise_fwd(_fn_rms, f"ffn{i}_norm", [full(xin, D)], [], [(norm, D)], [(D, D, BF16)], bm, S)
        ag = _mm(h, w["ffn_w_in"][i], "nn", F32, f"ffn{i}_in", bn=1408)
        u = _conv_fwd(ag, w["ffn_conv8"][i], B, S, f"ffn{i}_conv")
        xout = _mm(u, w["ffn_w_out"][i], "nn", F32, f"ffn{i}_out", residual=xin, bk=1408)
        return xout, (xin, norm, h, ag, u)

    def ffn_bwd(dxout, saved, i):
        xin, norm, h, ag, u = saved
        du = _mm(dxout, w["ffn_w_out"][i], "nt", F32, f"ffn{i}_out_dx", bn=1408)
        g_w_out = _mm(u, dxout, "tn", BF16, f"ffn{i}_out_dw", bm=1408)
        da, dg, dw8 = _conv_bwd(ag, w["ffn_conv8"][i], du, B, S, f"ffn{i}_conv_bwd")
        g_w_in = _mm(h, [da, dg], "tn", BF16, f"ffn{i}_in_dw", bn=1408, out_slots=N_SHARD)
        da = exchange(f"ffn{i}", [g_w_in, _slot_rows(g_w_out)], da)
        dh = _mm([da, dg], w["ffn_w_in"][i], "nt", F32, f"ffn{i}_in_dx", bk=1408)
        (dxin,), (g_norm,) = _rowwise_bwd(_fn_rms, f"ffn{i}_norm_bwd", [full(xin, D)], [], [(norm, D)],
                                          [(dh, D)], bm, S, adds={0: dxout})
        return dxin, (g_norm, dw8)

    (h0,) = _rowwise_fwd(_fn_rms, "ret_norm", [full(x, D)], [], [(w["ret_norm"], D)], [(D, D, BF16)], bm, S)
    proj = _mm(h0, w["ret_w_in"], "nn", F32, "ret_in")
    HQ, HV = RET_HEADS * RET_QK, RET_HEADS * RET_V
    rope_rows = [(proj, 2 * HQ + HV, 0, LANES)]
    q_r, k_r, v_r = _rowwise_fwd(_fn_ret_rope, "ret_rope", rope_rows, [cos_r, sin_r], [],
                                 [(HQ, LANES, BF16), (HQ, LANES, BF16), (HV, LANES, BF16)], bm, S)
    ret_o = _ret_attn_fwd(q_r, k_r, v_r, B, S)
    gate_rows = [full(ret_o, HV, RET_V), (proj, HV, 2, RET_V)]
    (y0,) = _rowwise_fwd(_fn_ret_gate, "ret_gate", gate_rows, [], [(w["ret_gn"], RET_V)], [(HV, RET_V, BF16)], 128, S)
    x1 = _mm(y0, w["ret_w_out"], "nn", F32, "ret_out", residual=x)
    x2, ffn0_saved = ffn_fwd(x1, 0)

    (h2,) = _rowwise_fwd(_fn_rms, "mla_norm", [full(x2, D)], [], [(w["mla_norm"], D)], [(D, D, BF16)], bm, S)
    proj2 = _mm(h2, w["mla_w_in"], "nn", F32, "mla_in")
    lat_consts = [(w["mla_q_norm"], LANES), (w["mla_kv_norm"], LANES)]
    cqn, ckvn, kr = _rowwise_fwd(_fn_mla_lat, "mla_latent_norm", [full(proj2, MLA_IN_PAD, LANES)], [], lat_consts,
                                 [(MLA_Q_RANK, LANES, BF16), (MLA_KV_RANK, LANES, BF16), (LANES, LANES, F32)], bm, S)
    qf = _mm(cqn, w["mla_w_qb"], "nn", F32, "mla_qb")
    kvf = _mm(ckvn, w["mla_w_kvb"], "nn", F32, "mla_kvb")
    HP, HVm = MLA_HEADS * MLA_PAD, MLA_HEADS * MLA_V
    head_rows = [full(qf, HP, LANES), full(kvf, HP, LANES), full(kr, LANES)]
    head_consts = [(w["mla_q_head_norm"], LANES), (w["mla_k_head_norm"], LANES)]
    q_a, k_a, v_a = _rowwise_fwd(_fn_mla_heads, "mla_heads", head_rows, [cos_m, sin_m], head_consts,
                                 [(HP, LANES, BF16), (HP, LANES, BF16), (HVm, LANES, BF16)], bm, S)
    att_o, lse = _mla_attn_fwd(q_a, k_a, v_a, B, S)
    x3 = _mm(att_o, w["mla_w_out"], "nn", F32, "mla_out", residual=x2)
    x4, ffn1_saved = ffn_fwd(x3, 1)

    dy, loss = _loss_head(x4, target)

    dx3, (g_n1, dw8_1) = ffn_bwd(dy, ffn1_saved, 1)

    d_att_o = _mm(dx3, w["mla_w_out"], "nt", F32, "mla_out_dx")
    g_mla_out = _mm(att_o, dx3, "tn", BF16, "mla_out_dw")
    dq_a, dk_a, dv_a = _mla_attn_bwd(q_a, k_a, v_a, att_o, d_att_o, lse, B, S)
    (dqf, dkvf, dkr), (g["mla_q_head_norm"], g["mla_k_head_norm"]) = _rowwise_bwd(
        _fn_mla_heads, "mla_heads_bwd", head_rows, [cos_m, sin_m], head_consts,
        [(dq_a, LANES), (dk_a, LANES), (dv_a, LANES)], 128, S)
    dcqn = _mm(dqf, w["mla_w_qb"], "nt", F32, "mla_qb_dx")
    g_qb = _mm(cqn, dqf, "tn", BF16, "mla_qb_dw")
    g_qb = _to_slots(_unpad_heads(g_qb, 1), 1).reshape(N_SHARD, MLA_Q_RANK, -1)
    dckvn = _mm(dkvf, w["mla_w_kvb"], "nt", F32, "mla_kvb_dx")
    g_kvb = _mm(ckvn, dkvf, "tn", BF16, "mla_kvb_dw", bn=512, out_slots=N_SHARD)
    (dproj2,), (g["mla_q_norm"], g["mla_kv_norm"]) = _rowwise_bwd(
        _fn_mla_lat, "mla_latent_norm_bwd", [full(proj2, MLA_IN_PAD, LANES)], [], lat_consts,
        [(dcqn, LANES), (dckvn, LANES), (dkr, LANES)], bm, S)
    g_mla_in = _mm(h2, dproj2, "tn", BF16, "mla_in_dw")
    dproj2 = exchange("mla", [_slot_rows(g_mla_in[:, :MLA_IN]), g_qb, g_kvb, _slot_rows(g_mla_out)], dproj2)
    dh2 = _mm(dproj2, w["mla_w_in"], "nt", F32, "mla_in_dx")
    (dx2,), (g["mla_norm"],) = _rowwise_bwd(_fn_rms, "mla_norm_bwd", [full(x2, D)], [], [(w["mla_norm"], D)],
                                            [(dh2, D)], bm, S, adds={0: dx3})

    dx1, (g_n0, dw8_0) = ffn_bwd(dx2, ffn0_saved, 0)

    dy0 = _mm(dx1, w["ret_w_out"], "nt", F32, "ret_out_dx")
    g_ret_out = _mm(y0, dx1, "tn", BF16, "ret_out_dw")
    (d_ret_o, dgate), (g["ret_gn"],) = _rowwise_bwd(_fn_ret_gate, "ret_gate_bwd", gate_rows, [], [(w["ret_gn"], RET_V)],
                                                    [(dy0, RET_V)], 128, S)
    dq_r, dk_r, dv_r = _ret_attn_bwd(q_r, k_r, v_r, d_ret_o, B, S)
    (dqkv,), _ = _rowwise_bwd(_fn_ret_rope, "ret_rope_bwd", rope_rows, [cos_r, sin_r], [],
                              [(dq_r, LANES), (dk_r, LANES), (dv_r, LANES)], bm, S)
    g_ret_in = _mm(h0, [dqkv, dgate], "tn", BF16, "ret_in_dw", bn=512, out_slots=N_SHARD)
    dgate = exchange("ret", [g_ret_in, _slot_rows(g_ret_out)], dgate)
    dh0 = _mm([dqkv, dgate], w["ret_w_in"], "nt", F32, "ret_in_dx", bk=1024)
    (dx,), (g["ret_norm"],) = _rowwise_bwd(_fn_rms, "ret_norm_bwd", [full(x, D)], [], [(w["ret_norm"], D)],
                                           [(dh0, D)], bm, S, adds={0: dx1})

    g["ffn_norm"] = jnp.concatenate([g_n0, g_n1], axis=0)
    g["ffn_conv_w"] = jnp.stack([dw8_0[0:3], dw8_1[0:3]])
    g["ffn_conv_b"] = jnp.stack([dw8_0[3], dw8_1[3]])
    return loss, dx, g


_BIG = [("ret_w_in", 2), ("ret_w_out", 1), ("mla_w_in", 1), ("mla_w_qb", 2), ("mla_w_kvb", 2), ("mla_w_out", 1),
        ("ffn_w_in", 2), ("ffn_w_out", 1)]
_SMALL_SHARDED = [("ret_gn", 2), ("mla_norm", 1), ("mla_q_norm", 1), ("mla_kv_norm", 1), ("ffn_conv_w", 2)]
_SMALL_REPLICATED = ["ret_norm", "mla_q_head_norm", "mla_k_head_norm", "ffn_norm", "ffn_conv_b"]
_SMALL_ALL = ["ret_norm", "ret_gn", "mla_norm", "mla_q_norm", "mla_kv_norm", "mla_q_head_norm", "mla_k_head_norm",
              "ffn_norm", "ffn_conv_w", "ffn_conv_b"]


def _to_slots(full, axis):
    shape = full.shape
    split = shape[:axis] + (N_SHARD, shape[axis] // N_SHARD) + shape[axis + 1:]
    return jnp.moveaxis(full.reshape(split), axis, 0).reshape(N_SHARD, -1)


def _from_slots(slots, shard_shape, axis):
    parts = jnp.moveaxis(slots.reshape((N_SHARD,) + tuple(shard_shape)), 0, axis)
    full = shard_shape[:axis] + (N_SHARD * shard_shape[axis],) + shard_shape[axis + 1:]
    return parts.reshape(full)


def _pad_rows(flat, cols, row_unit):
    n, L = flat.shape
    unit = cols * row_unit
    Lp = -(-L // unit) * unit
    if Lp != L:
        flat = jnp.concatenate([flat, jnp.zeros((n, Lp - L), flat.dtype)], axis=1)
    return flat.reshape(n, Lp // cols, cols)


def _pad_heads(a, axis):
    shape = a.shape
    a = a.reshape(shape[:axis] + (MLA_HEADS, MLA_QK) + shape[axis + 1:])
    pad = [(0, 0)] * a.ndim
    pad[axis + 1] = (0, MLA_PAD - MLA_QK)
    return jnp.pad(a, pad).reshape(shape[:axis] + (MLA_HEADS * MLA_PAD,) + shape[axis + 1:])


def _unpad_heads(a, axis):
    shape = a.shape
    a = a.reshape(shape[:axis] + (MLA_HEADS, MLA_PAD) + shape[axis + 1:])
    a = lax.slice_in_dim(a, 0, MLA_QK, axis=axis + 1)
    return a.reshape(shape[:axis] + (MLA_HEADS * MLA_QK,) + shape[axis + 1:])


def kernel(x, ret_norm, ret_w_in, ret_gn, ret_w_out, mla_norm, mla_w_in, mla_q_norm, mla_w_qb, mla_kv_norm, mla_w_kvb, mla_q_head_norm, mla_k_head_norm, mla_w_out, ffn_norm, ffn_w_in, ffn_conv_w, ffn_conv_b, ffn_w_out, loss_target, m_ret_norm, m_ret_w_in, m_ret_gn, m_ret_w_out, m_mla_norm, m_mla_w_in, m_mla_q_norm, m_mla_w_qb, m_mla_kv_norm, m_mla_w_kvb, m_mla_q_head_norm, m_mla_k_head_norm, m_mla_w_out, m_ffn_norm, m_ffn_w_in, m_ffn_conv_w, m_ffn_conv_b, m_ffn_w_out, v_ret_norm, v_ret_w_in, v_ret_gn, v_ret_w_out, v_mla_norm, v_mla_w_in, v_mla_q_norm, v_mla_w_qb, v_mla_kv_norm, v_mla_w_kvb, v_mla_q_head_norm, v_mla_k_head_norm, v_mla_w_out, v_ffn_norm, v_ffn_w_in, v_ffn_conv_w, v_ffn_conv_b, v_ffn_w_out):
    names = ["ret_norm", "ret_w_in", "ret_gn", "ret_w_out", "mla_norm", "mla_w_in", "mla_q_norm", "mla_w_qb",
             "mla_kv_norm", "mla_w_kvb", "mla_q_head_norm", "mla_k_head_norm", "mla_w_out", "ffn_norm", "ffn_w_in",
             "ffn_conv_w", "ffn_conv_b", "ffn_w_out"]
    shard = dict(zip(names, [ret_norm, ret_w_in, ret_gn, ret_w_out, mla_norm, mla_w_in, mla_q_norm, mla_w_qb,
                             mla_kv_norm, mla_w_kvb, mla_q_head_norm, mla_k_head_norm, mla_w_out, ffn_norm, ffn_w_in,
                             ffn_conv_w, ffn_conv_b, ffn_w_out]))
    mom_m = dict(zip(names, [m_ret_norm, m_ret_w_in, m_ret_gn, m_ret_w_out, m_mla_norm, m_mla_w_in, m_mla_q_norm,
                             m_mla_w_qb, m_mla_kv_norm, m_mla_w_kvb, m_mla_q_head_norm, m_mla_k_head_norm, m_mla_w_out,
                             m_ffn_norm, m_ffn_w_in, m_ffn_conv_w, m_ffn_conv_b, m_ffn_w_out]))
    mom_v = dict(zip(names, [v_ret_norm, v_ret_w_in, v_ret_gn, v_ret_w_out, v_mla_norm, v_mla_w_in, v_mla_q_norm,
                             v_mla_w_qb, v_mla_kv_norm, v_mla_w_kvb, v_mla_q_head_norm, v_mla_k_head_norm, v_mla_w_out,
                             v_ffn_norm, v_ffn_w_in, v_ffn_conv_w, v_ffn_conv_b, v_ffn_w_out]))
    B, S, D = x.shape
    T = B * S
    sx, sy = lax.axis_index("x"), lax.axis_index("y")
    me = 2 * sx + sy

    big_names = [n for n, _ in _BIG]
    two_d = lambda a: a.reshape(-1, a.shape[-1])
    small_sizes = [int(np.prod(shard[n].shape)) for n, _ in _SMALL_SHARDED]
    small = jnp.concatenate([shard[n].reshape(1, -1) for n, _ in _SMALL_SHARDED], axis=1)
    small = _pad_rows(small, LANES, 8)[0]
    bigs = [two_d(shard[n]).astype(BF16) for n in big_names]
    *gbig, gsmall = _all_gather_weights(bigs, small)
    is_me = lax.broadcasted_iota(jnp.int32, (N_SHARD, 1, 1), 0) == me
    with_own = lambda gathered, own: jnp.where(is_me, own[None], gathered)
    gw = {n: with_own(g_, b_) for n, g_, b_ in zip(big_names, gbig, bigs)}
    gsmall = with_own(gsmall, small).reshape(N_SHARD, -1)
    by_cols = lambda a: jnp.moveaxis(a, 0, 1).reshape(a.shape[1], -1)
    by_rows = lambda a: a.reshape(-1, a.shape[-1])
    wfull = {}
    off = 0
    for (n, ax), sz in zip(_SMALL_SHARDED, small_sizes):
        wfull[n] = _from_slots(gsmall[:, off:off + sz], shard[n].shape, ax)
        off += sz
    for n in _SMALL_REPLICATED:
        wfull[n] = shard[n]

    conv8 = jnp.concatenate([wfull["ffn_conv_w"], wfull["ffn_conv_b"][:, None, :],
                             jnp.zeros((2, 4, FFN_DIM), F32)], axis=1)
    ffn_in_rows, ffn_out_rows = shard["ffn_w_in"].shape[1], shard["ffn_w_out"].shape[1]
    w = {
        "ret_norm": wfull["ret_norm"], "ret_w_in": by_cols(gw["ret_w_in"]),
        "ret_gn": wfull["ret_gn"].reshape(1, RET_HEADS * RET_V), "ret_w_out": by_rows(gw["ret_w_out"]),
        "mla_norm": wfull["mla_norm"],
        "mla_w_in": jnp.pad(by_rows(gw["mla_w_in"]), ((0, 0), (0, MLA_IN_PAD - MLA_IN))),
        "mla_q_norm": wfull["mla_q_norm"], "mla_w_qb": _pad_heads(by_cols(gw["mla_w_qb"]), 1),
        "mla_kv_norm": wfull["mla_kv_norm"], "mla_w_kvb": by_cols(gw["mla_w_kvb"]),
        "mla_q_head_norm": jnp.pad(wfull["mla_q_head_norm"], ((0, 0), (0, MLA_PAD - MLA_QK))),
        "mla_k_head_norm": jnp.pad(wfull["mla_k_head_norm"], ((0, 0), (0, MLA_PAD - MLA_QK))),
        "mla_w_out": by_rows(gw["mla_w_out"]), "ffn_norm": wfull["ffn_norm"],
        "ffn_w_in": [by_cols(gw["ffn_w_in"][:, i * ffn_in_rows:(i + 1) * ffn_in_rows]) for i in range(2)],
        "ffn_conv8": conv8,
        "ffn_w_out": [by_rows(gw["ffn_w_out"][:, i * ffn_out_rows:(i + 1) * ffn_out_rows]) for i in range(2)],
    }

    started = {}

    def exchange(group, arrays, tie):
        send_sems, recv_sems, ps, lands, token = _grads_exchange_start(arrays, f"grads_start_{group}")
        started[group] = (send_sems, recv_sems, ps, lands)
        return lax.optimization_barrier((tie, token))[0]

    loss_part, dx, gl = _local_step(x.reshape(T, D), loss_target.reshape(T, D), w, B, S, exchange)
    loss = lax.psum(loss_part, ("x", "y", "c"))
    gfull = {
        "ret_norm": gl["ret_norm"], "ret_gn": gl["ret_gn"].reshape(1, RET_HEADS, RET_V),
        "mla_norm": gl["mla_norm"], "mla_q_norm": gl["mla_q_norm"], "mla_kv_norm": gl["mla_kv_norm"],
        "mla_q_head_norm": gl["mla_q_head_norm"][:, :MLA_QK], "mla_k_head_norm": gl["mla_k_head_norm"][:, :MLA_QK],
        "ffn_norm": gl["ffn_norm"], "ffn_conv_w": gl["ffn_conv_w"], "ffn_conv_b": gl["ffn_conv_b"],
    }

    red = {}
    after = dx
    for group in ("ffn1", "mla", "ffn0", "ret"):
        send_sems, recv_sems, ps, lands = started[group]
        ps, lands = _grads_exchange_wait(send_sems, recv_sems, ps, lands, after, f"grads_wait_{group}")
        halves = [_sum_partials(p_, l_, f"grads_sum_{group}_{i}") for i, (p_, l_) in enumerate(zip(ps, lands))]
        red[group] = [two_d(r) for r in _sibling_share(halves, f"grads_share_{group}")]
        after = red[group][0]
    grads = {"ret_w_in": red["ret"][0], "ret_w_out": red["ret"][1], "mla_w_in": red["mla"][0],
             "mla_w_qb": red["mla"][1], "mla_w_kvb": red["mla"][2], "mla_w_out": red["mla"][3]}
    grads = {n: a.reshape(shard[n].shape) for n, a in grads.items()}
    grads["ffn_w_in"] = jnp.stack([red["ffn0"][0], red["ffn1"][0]])
    grads["ffn_w_out"] = jnp.stack([red["ffn0"][1], red["ffn1"][1]])

    small_sizes_all = [int(np.prod(gfull[n].shape)) for n in _SMALL_ALL]
    gsm = jnp.concatenate([gfull[n].reshape(1, -1) for n in _SMALL_ALL], axis=1)
    gsm = _all_reduce_small(_pad_rows(gsm, LANES, 8)[0]).reshape(-1)

    sharded_axis = dict(_SMALL_SHARDED)
    off = 0
    for n, sz in zip(_SMALL_ALL, small_sizes_all):
        gn = gsm[off:off + sz].reshape(gfull[n].shape)
        off += sz
        if n in sharded_axis:
            ax = sharded_axis[n]
            width = shard[n].shape[ax]
            gn = lax.dynamic_slice_in_dim(gn, me * width, width, axis=ax)
        grads[n] = gn

    delta, new_m, new_v = {}, {}, {}
    for n, _ in _BIG:
        shp = shard[n].shape
        two_d = lambda a: a.reshape(-1, shp[-1])
        d_, m_, v_ = _adamw(two_d(shard[n]), two_d(grads[n]), two_d(mom_m[n]), two_d(mom_v[n]), f"adamw_{n}")
        delta[n], new_m[n], new_v[n] = d_.reshape(shp), m_.reshape(shp), v_.reshape(shp)
    pack_small = lambda d: _pad_rows(jnp.concatenate([d[n].reshape(1, -1) for n in _SMALL_ALL], axis=1), LANES, 8)[0]
    d_, m_, v_ = _adamw(pack_small(shard), pack_small(grads), pack_small(mom_m), pack_small(mom_v), "adamw_small")
    off = 0
    for n in _SMALL_ALL:
        sz = int(np.prod(shard[n].shape))
        for dst, src in ((delta, d_), (new_m, m_), (new_v, v_)):
            dst[n] = src.reshape(-1)[off:off + sz].reshape(shard[n].shape)
        off += sz

    return (loss, dx.reshape(B, S, D), *[grads[n] for n in names], *[delta[n] for n in names],
            *[new_m[n] for n in names], *[new_v[n] for n in names])
```

```python
import functools
import math

import numpy as np
import jax
import jax.numpy as jnp
from jax import lax
from jax.experimental import pallas as pl
from jax.experimental.pallas import tpu as pltpu

F32 = jnp.float32
BF16 = jnp.bfloat16
MXU_DTYPE = jnp.bfloat16

CHUNK = 64
RMS_EPS = 1e-6
ROPE_THETA = 10000.0
D_MODEL = 1024
RET_HEADS = 4
RET_QK = 256
RET_V = 512
RET_GAMMA_BASE = -5.0
MLA_HEADS = 8
MLA_Q_RANK = 384
MLA_KV_RANK = 256
MLA_NOPE = 128
MLA_ROPE = 64
MLA_V = 128
MLA_QK = MLA_NOPE + MLA_ROPE
MLA_PAD = 256
MLA_IN = MLA_Q_RANK + MLA_KV_RANK + MLA_ROPE
MLA_IN_PAD = MLA_IN + 64
MASK_VALUE = -1e30
FFN_DIM = 2816
ADAM_LR = 0.001
ADAM_B1 = 0.9
ADAM_B2 = 0.999
ADAM_EPS = 1e-08
ADAM_WD = 0.01
ADAM_STEP = 10

LANES = 128
ATT_BLOCK = 256
VMEM_LIMIT = 56 * 2 ** 20
N_SHARD = 4
N_DEV = 8

MESH = pl.DeviceIdType.MESH


def _params(sem=None, **kw):
    return pltpu.CompilerParams(dimension_semantics=sem, vmem_limit_bytes=VMEM_LIMIT, **kw)


def _pick(dim, target):
    if dim <= target:
        return dim
    best = None
    for d in range(LANES, target + 1, LANES):
        if dim % d == 0:
            best = d
    assert best is not None, (dim, target)
    return best


def _mm(a, b, dims, out_dtype, name, residual=None, bm=512, bn=1024, bk=2048, out_slots=None, after=None):
    a_parts = list(a) if isinstance(a, (list, tuple)) else [a]
    b_parts = list(b) if isinstance(b, (list, tuple)) else [b]
    if dims == "tn":
        assert len(a_parts) == 1
        K, M = a_parts[0].shape
        N = sum(p.shape[1] for p in b_parts)
        part_widths = [p.shape[1] for p in b_parts]
    else:
        assert len(b_parts) == 1
        M = a_parts[0].shape[0]
        K = sum(p.shape[1] for p in a_parts)
        N = b_parts[0].shape[1 if dims == "nn" else 0]
        part_widths = [p.shape[1] for p in a_parts]
    bm, bn, bk = _pick(M, bm), _pick(N, bn), _pick(K, min(bk, 1024) if dims == "tn" else bk)
    nk = K // bk
    unit = bn if dims == "tn" else bk
    assert all(wd % unit == 0 for wd in part_widths), (name, part_widths, unit)
    bounds = np.cumsum([0] + [wd // unit for wd in part_widths])
    ranges = [(int(lo), int(hi)) for lo, hi in zip(bounds[:-1], bounds[1:])]

    def part_index(idx, lo, hi):
        return jnp.clip(idx - lo, 0, hi - lo - 1)

    if dims == "tn":
        a_specs = [pl.BlockSpec((bk, bm), lambda i, j, k: (k, i))]
        b_specs = [pl.BlockSpec((bk, bn), functools.partial(lambda i, j, k, lo, hi: (k, part_index(j, lo, hi)), lo=lo, hi=hi))
                   for lo, hi in ranges]
        dn = (((0,), (0,)), ((), ()))
    else:
        a_specs = [pl.BlockSpec((bm, bk), functools.partial(lambda i, j, k, lo, hi: (i, part_index(k, lo, hi)), lo=lo, hi=hi))
                   for lo, hi in ranges]
        if dims == "nt":
            b_specs = [pl.BlockSpec((bn, bk), lambda i, j, k: (j, k))]
        else:
            b_specs = [pl.BlockSpec((bk, bn), lambda i, j, k: (k, j))]
        dn = (((1,), (1 if dims == "nt" else 0,)), ((), ()))
    r_spec = pl.BlockSpec((bm, bn), lambda i, j, k: (i, j))
    if out_slots is None:
        o_spec, o_shape = r_spec, (M, N)
    else:
        ns = N // out_slots
        assert ns % bn == 0, (name, ns, bn)
        nbs = ns // bn
        o_spec = pl.BlockSpec((None, bm, bn), lambda i, j, k: (j // nbs, i, j % nbs))
        o_shape = (out_slots, M, ns)
    has_res = residual is not None
    na, nb = len(a_parts), len(b_parts)

    def body(*refs):
        a_refs, b_refs = refs[:na], refs[na:na + nb]
        r_ref = refs[na + nb] if has_res else None
        n_in = na + nb + has_res + (after is not None)
        o_ref = refs[n_in]
        acc_ref = refs[n_in + 1] if nk > 1 else None
        k = pl.program_id(2)

        def finish(acc):
            if has_res:
                acc = acc + r_ref[...].astype(F32)
            o_ref[...] = acc.astype(out_dtype)

        def compute(a_ref, b_ref):
            p = lax.dot_general(a_ref[...].astype(MXU_DTYPE), b_ref[...].astype(MXU_DTYPE), dn,
                                preferred_element_type=F32)
            if nk == 1:
                finish(p)
                return

            @pl.when(k == 0)
            def _():
                acc_ref[...] = p

            @pl.when(jnp.logical_and(k > 0, k < nk - 1))
            def _():
                acc_ref[...] += p

            @pl.when(k == nk - 1)
            def _():
                finish(acc_ref[...] + p)

        if len(ranges) == 1:
            compute(a_refs[0], b_refs[0])
        else:
            idx = pl.program_id(1) if dims == "tn" else k
            for p, (lo, hi) in enumerate(ranges):
                @pl.when(jnp.logical_and(idx >= lo, idx < hi))
                def _(p=p):
                    compute(a_refs[0 if dims == "tn" else p], b_refs[p if dims == "tn" else 0])

    after_specs = [] if after is None else [pl.BlockSpec(after.shape, lambda i, j, k: (0, 0))]
    return pl.pallas_call(
        body, name=name, grid=(M // bm, N // bn, nk),
        in_specs=a_specs + b_specs + ([r_spec] if has_res else []) + after_specs, out_specs=o_spec,
        out_shape=jax.ShapeDtypeStruct(o_shape, out_dtype),
        scratch_shapes=[pltpu.VMEM((bm, bn), F32)] if nk > 1 else [],
        compiler_params=_params(("parallel", "parallel", "arbitrary")),
    )(*a_parts, *b_parts, *((residual,) if has_res else ()), *(() if after is None else (after,)))


def _tiles(ref, width, tile):
    return [ref[:, t * tile:(t + 1) * tile].astype(F32) for t in range(width // tile)]


def _row_specs(rows, pos, consts, bm, S):
    npos_blocks = S // bm
    specs = [pl.BlockSpec((bm, w), functools.partial(lambda i, c: (i, c), c=cb)) for (_, w, cb, _) in rows]
    specs += [pl.BlockSpec((bm, p.shape[1]), lambda i: (i % npos_blocks, 0)) for p in pos]
    specs += [pl.BlockSpec(c.shape, lambda i: (0, 0)) for (c, _) in consts]
    return specs


def _rowwise_fwd(fn, name, rows, pos, consts, outs, bm, S):
    T = rows[0][0].shape[0]
    nr, npos, nc = len(rows), len(pos), len(consts)

    def body(*refs):
        row_v = [_tiles(r, w, t) for r, (_, w, _, t) in zip(refs[:nr], rows)]
        pos_v = [r[...] for r in refs[nr:nr + npos]]
        const_v = [_tiles(r, c.shape[1], t) for r, (c, t) in zip(refs[nr + npos:nr + npos + nc], consts)]
        res = fn(row_v, pos_v, const_v)
        for o_ref, tiles, (w, t, dt) in zip(refs[nr + npos + nc:], res, outs):
            for k, v in enumerate(tiles):
                o_ref[:, k * t:(k + 1) * t] = v.astype(dt)

    return pl.pallas_call(
        body, name=name, grid=(T // bm,),
        in_specs=_row_specs(rows, pos, consts, bm, S),
        out_specs=[pl.BlockSpec((bm, w), lambda i: (i, 0)) for (w, _, _) in outs],
        out_shape=[jax.ShapeDtypeStruct((T, w), dt) for (w, _, dt) in outs],
        compiler_params=_params(("parallel",)),
    )(*[r[0] for r in rows], *pos, *[c[0] for c in consts])


def _rowwise_bwd(fn, name, rows, pos, consts, cts, bm, S, adds=None):
    adds = adds or {}
    T = rows[0][0].shape[0]
    nr, npos, nc, nct = len(rows), len(pos), len(consts), len(cts)
    add_idx = sorted(adds)

    def body(*refs):
        it = iter(refs)
        row_refs = [next(it) for _ in range(nr)]
        pos_refs = [next(it) for _ in range(npos)]
        const_refs = [next(it) for _ in range(nc)]
        ct_refs = [next(it) for _ in range(nct)]
        add_refs = {k: next(it) for k in add_idx}
        drow_refs = [next(it) for _ in range(nr)]
        dconst_refs = [next(it) for _ in range(nc)]
        row_v = [_tiles(r, w, t) for r, (_, w, _, t) in zip(row_refs, rows)]
        pos_v = [r[...] for r in pos_refs]
        const_v = [_tiles(r, c.shape[1], t) for r, (c, t) in zip(const_refs, consts)]
        ct_v = [_tiles(r, c.shape[1], t) for r, (c, t) in zip(ct_refs, cts)]
        _, vjp = jax.vjp(lambda rv, cv: fn(rv, pos_v, cv), row_v, const_v)
        drows, dconsts = vjp(ct_v)
        for a, (d_ref, tiles, (_, w, _, t)) in enumerate(zip(drow_refs, drows, rows)):
            for k, v in enumerate(tiles):
                if a in add_refs:
                    v = v + add_refs[a][:, k * t:(k + 1) * t].astype(F32)
                d_ref[:, k * t:(k + 1) * t] = v
        first = pl.program_id(0) == 0
        for d_ref, tiles, (_, t) in zip(dconst_refs, dconsts, consts):
            for k, v in enumerate(tiles):
                @pl.when(first)
                def _(d_ref=d_ref, k=k, t=t, v=v):
                    d_ref[:, k * t:(k + 1) * t] = v

                @pl.when(jnp.logical_not(first))
                def _(d_ref=d_ref, k=k, t=t, v=v):
                    d_ref[:, k * t:(k + 1) * t] += v

    in_specs = _row_specs(rows, pos, consts, bm, S)
    in_specs += [pl.BlockSpec((bm, c.shape[1]), lambda i: (i, 0)) for (c, _) in cts]
    in_specs += [pl.BlockSpec((bm, adds[k].shape[1]), lambda i: (i, 0)) for k in add_idx]
    out_specs = [pl.BlockSpec((bm, w), lambda i: (i, 0)) for (_, w, _, _) in rows]
    out_specs += [pl.BlockSpec(c.shape, lambda i: (0, 0)) for (c, _) in consts]
    out_shape = [jax.ShapeDtypeStruct((T, w), F32) for (_, w, _, _) in rows]
    out_shape += [jax.ShapeDtypeStruct(c.shape, F32) for (c, _) in consts]
    res = pl.pallas_call(
        body, name=name, grid=(T // bm,),
        in_specs=in_specs, out_specs=out_specs, out_shape=out_shape,
        compiler_params=_params(("arbitrary",)),
    )(*[r[0] for r in rows], *pos, *[c[0] for c in consts], *[c[0] for c in cts], *[adds[k] for k in add_idx])
    return res[:nr], res[nr:]


def _ssq(tiles):
    s = jnp.sum(tiles[0] * tiles[0], axis=-1, keepdims=True)
    for t in tiles[1:]:
        s = s + jnp.sum(t * t, axis=-1, keepdims=True)
    return s


def _sigmoid(x):
    return 1.0 / (1.0 + jnp.exp(-x))


def _fn_rms(rows, pos, consts):
    (x,), (g,) = rows[0], consts[0]
    r = lax.rsqrt(jnp.mean(x * x, axis=-1, keepdims=True) + RMS_EPS)
    return [[x * r * g]]


def _fn_ret_rope(rows, pos, consts):
    (qkv,) = rows
    nq = RET_HEADS * RET_QK // LANES
    q, k, v = qkv[:nq], qkv[nq:2 * nq], qkv[2 * nq:]
    cos, sin = pos

    def rot(t, scale):
        out = []
        for h in range(RET_HEADS):
            x1, x2 = t[2 * h], t[2 * h + 1]
            o1, o2 = x1 * cos - x2 * sin, x2 * cos + x1 * sin
            out += [o1, o2] if scale is None else [o1 * scale, o2 * scale]
        return out

    return [rot(q, None), rot(k, RET_QK ** -0.5), list(v)]


def _fn_ret_gate(rows, pos, consts):
    o, g = rows
    (gn,) = consts
    out = []
    for h in range(RET_HEADS):
        r = lax.rsqrt(jnp.mean(o[h] * o[h], axis=-1, keepdims=True) + RMS_EPS)
        out.append((o[h] * r * gn[h]) * (g[h] * _sigmoid(g[h])))
    return [out]


def _fn_mla_lat(rows, pos, consts):
    (p,) = rows
    gq, gkv = consts
    nq, nkv = MLA_Q_RANK // LANES, MLA_KV_RANK // LANES
    cq, ckv, kr = p[:nq], p[nq:nq + nkv], p[nq + nkv]
    rq = lax.rsqrt(_ssq(cq) / MLA_Q_RANK + RMS_EPS)
    rkv = lax.rsqrt(_ssq(ckv) / MLA_KV_RANK + RMS_EPS)
    return [[t * rq * g for t, g in zip(cq, gq)], [t * rkv * g for t, g in zip(ckv, gkv)], [kr]]


def _swap32_impl(x):
    lane = lax.broadcasted_iota(jnp.int32, x.shape, 1)
    up, down = pltpu.roll(x, LANES - 32, 1), pltpu.roll(x, 32, 1)
    return jnp.where(lane < 32, up, jnp.where(lane < 64, down, 0.0))


@jax.custom_vjp
def _swap32(x):
    return _swap32_impl(x)


_swap32.defvjp(lambda x: (_swap32_impl(x), None), lambda _, g: (_swap32_impl(g),))


def _fn_mla_heads(rows, pos, consts):
    qf, kvf, (kr,) = rows
    cos, sin = pos
    gq, gk = consts
    q_out, k_out, v_out = [], [], []
    for h in range(MLA_HEADS):
        q0, q1 = qf[2 * h], qf[2 * h + 1]
        r = lax.rsqrt(_ssq([q0, q1]) / MLA_QK + RMS_EPS)
        a0, a1 = q0 * r * gq[0], q1 * r * gq[1]
        a1 = a1 * cos + _swap32(a1) * sin
        q_out += [a0 * (MLA_QK ** -0.5), a1 * (MLA_QK ** -0.5)]
        k0 = kvf[2 * h]
        r = lax.rsqrt(_ssq([k0, kr]) / MLA_QK + RMS_EPS)
        b0, b1 = k0 * r * gk[0], kr * r * gk[1]
        k_out += [b0, b1 * cos + _swap32(b1) * sin]
        v_out.append(kvf[2 * h + 1])
    return [q_out, k_out, v_out]


def _shift_down(x, n):
    row = lax.broadcasted_iota(jnp.int32, x.shape, 0)
    return jnp.where(row >= n, pltpu.roll(x, n, 0), 0.0)


def _shift_up(x, n):
    rows = x.shape[0]
    row = lax.broadcasted_iota(jnp.int32, x.shape, 0)
    return jnp.where(row < rows - n, pltpu.roll(x, rows - n, 0), 0.0)


def _conv_blocks(S):
    cb = 256
    return cb, FFN_DIM // cb


def _conv_fwd(ag, w8, B, S, name):
    cb, ncb = _conv_blocks(S)

    def body(a_ref, g_ref, w_ref, u_ref):
        g = g_ref[...]
        w = w_ref[...]
        gc = w[0:1] * _shift_down(g, 2) + w[1:2] * _shift_down(g, 1) + w[2:3] * g + w[3:4]
        u_ref[...] = (a_ref[...] * (gc * _sigmoid(gc))).astype(u_ref.dtype)

    return pl.pallas_call(
        body, name=name, grid=(ncb, B),
        in_specs=[pl.BlockSpec((S, cb), lambda j, b: (b, j)),
                  pl.BlockSpec((S, cb), lambda j, b: (b, ncb + j)),
                  pl.BlockSpec((8, cb), lambda j, b: (0, j))],
        out_specs=pl.BlockSpec((S, cb), lambda j, b: (b, j)),
        out_shape=jax.ShapeDtypeStruct((B * S, FFN_DIM), BF16),
        compiler_params=_params(("parallel", "parallel")),
    )(ag, ag, w8)


def _conv_bwd(ag, w8, du, B, S, name):
    cb, ncb = _conv_blocks(S)

    def body(a_ref, g_ref, w_ref, du_ref, da_ref, dg_ref, dw_ref):
        g = g_ref[...]
        w = w_ref[...]
        g1, g2 = _shift_down(g, 1), _shift_down(g, 2)
        gc = w[0:1] * g2 + w[1:2] * g1 + w[2:3] * g + w[3:4]
        sg = _sigmoid(gc)
        du_v = du_ref[...]
        da_ref[...] = du_v * (gc * sg)
        dgc = du_v * a_ref[...] * (sg * (1.0 + gc * (1.0 - sg)))
        dg_ref[...] = w[2:3] * dgc + w[1:2] * _shift_up(dgc, 1) + w[0:1] * _shift_up(dgc, 2)
        part = jnp.concatenate([
            jnp.sum(dgc * g2, axis=0, keepdims=True), jnp.sum(dgc * g1, axis=0, keepdims=True),
            jnp.sum(dgc * g, axis=0, keepdims=True), jnp.sum(dgc, axis=0, keepdims=True),
            jnp.zeros((4, cb), F32)], axis=0)

        @pl.when(pl.program_id(1) == 0)
        def _():
            dw_ref[...] = part

        @pl.when(pl.program_id(1) > 0)
        def _():
            dw_ref[...] += part

    blk = lambda j, b: (b, j)
    return pl.pallas_call(
        body, name=name, grid=(ncb, B),
        in_specs=[pl.BlockSpec((S, cb), blk),
                  pl.BlockSpec((S, cb), lambda j, b: (b, ncb + j)),
                  pl.BlockSpec((8, cb), lambda j, b: (0, j)),
                  pl.BlockSpec((S, cb), blk)],
        out_specs=[pl.BlockSpec((S, cb), blk), pl.BlockSpec((S, cb), blk),
                   pl.BlockSpec((8, cb), lambda j, b: (0, j))],
        out_shape=[jax.ShapeDtypeStruct((B * S, FFN_DIM), F32), jax.ShapeDtypeStruct((B * S, FFN_DIM), F32),
                   jax.ShapeDtypeStruct((8, FFN_DIM), F32)],
        compiler_params=_params(("parallel", "arbitrary")),
    )(ag, ag, w8, du)


_NT = (((1,), (1,)), ((), ()))
_NN = (((1,), (0,)), ((), ()))
_TN = (((0,), (0,)), ((), ()))


def _dot(a, b, dn):
    return lax.dot_general(a.astype(MXU_DTYPE), b.astype(MXU_DTYPE), dn, preferred_element_type=F32)


def _rel_and_mask():
    il = lax.broadcasted_iota(jnp.int32, (ATT_BLOCK, ATT_BLOCK), 0)
    jl = lax.broadcasted_iota(jnp.int32, (ATT_BLOCK, ATT_BLOCK), 1)
    return (il - jl).astype(F32), (jl // CHUNK) <= (il // CHUNK)


def _rows(i):
    return pl.ds(pl.multiple_of(i * ATT_BLOCK, ATT_BLOCK), ATT_BLOCK)


KV_UNROLL = 2


def _kv_loop(n, body, carry):
    main = n // KV_UNROLL

    def chunk(t, c):
        for u in range(KV_UNROLL):
            c = body(t * KV_UNROLL + u, c)
        return c

    carry = lax.fori_loop(0, main, chunk, carry)
    return lax.fori_loop(main * KV_UNROLL, n, body, carry)


def _mla_attn_fwd(q, k, v, B, S):
    H, nq = MLA_HEADS, S // ATT_BLOCK

    def body(q_ref, k_ref, v_ref, o_ref, lse_ref):
        _, mask = _rel_and_mask()

        def qblock(i, _):
            qi = q_ref[_rows(i), :]

            def kv(j, carry, diag):
                m, l, acc = carry
                s = _dot(qi, k_ref[_rows(j), :], _NT)
                if diag:
                    s = jnp.where(mask, s, MASK_VALUE)
                m2 = jnp.maximum(m, jnp.max(s, axis=-1, keepdims=True))
                alpha = jnp.exp(m - m2)
                p = jnp.exp(s - m2)
                l2 = alpha * l + jnp.sum(p, axis=-1, keepdims=True)
                return m2, l2, alpha * acc + _dot(p, v_ref[_rows(j), :], _NN)

            init = (jnp.full((ATT_BLOCK, 1), MASK_VALUE, F32), jnp.zeros((ATT_BLOCK, 1), F32),
                    jnp.zeros((ATT_BLOCK, MLA_V), F32))
            carry = _kv_loop(i, lambda j, c: kv(j, c, False), init)
            m, l, acc = kv(i, carry, True)
            o_ref[_rows(i), :] = acc / l
            lse_ref[0, _rows(i), :] = m + jnp.log(l)
            return 0

        lax.fori_loop(0, nq, qblock, 0)

    return pl.pallas_call(
        body, name="mla_attn_fwd", grid=(B, H),
        in_specs=[pl.BlockSpec((S, MLA_PAD), lambda b, h: (b, h)),
                  pl.BlockSpec((S, MLA_PAD), lambda b, h: (b, h)),
                  pl.BlockSpec((S, MLA_V), lambda b, h: (b, h))],
        out_specs=[pl.BlockSpec((S, MLA_V), lambda b, h: (b, h)),
                   pl.BlockSpec((1, S, 1), lambda b, h: (b * H + h, 0, 0))],
        out_shape=[jax.ShapeDtypeStruct((B * S, H * MLA_V), F32), jax.ShapeDtypeStruct((B * H, S, 1), F32)],
        compiler_params=_params(("parallel", "parallel")),
    )(q, k, v)


def _mla_attn_bwd(q, k, v, o, do, lse, B, S):
    H, nq = MLA_HEADS, S // ATT_BLOCK

    def body(q_ref, k_ref, v_ref, o_ref, do_ref, lse_ref, dq_ref, dk_ref, dv_ref, acc_ref):
        _, mask = _rel_and_mask()
        dk_ref[...] = jnp.zeros(dk_ref.shape, F32)
        dv_ref[...] = jnp.zeros(dv_ref.shape, F32)

        def qblock(i, _):
            qi = q_ref[_rows(i), :]
            doi = do_ref[_rows(i), :]
            delta = jnp.sum(doi * o_ref[_rows(i), :], axis=-1, keepdims=True)
            lse_i = lse_ref[0, _rows(i), :]
            doi = doi.astype(MXU_DTYPE)
            acc_ref[...] = jnp.zeros(acc_ref.shape, F32)

            def kv(j, diag):
                kj = k_ref[_rows(j), :]
                p = jnp.exp(_dot(qi, kj, _NT) - lse_i)
                if diag:
                    p = jnp.where(mask, p, 0.0)
                ds = (p * (_dot(doi, v_ref[_rows(j), :], _NT) - delta)).astype(MXU_DTYPE)
                acc_ref[...] += _dot(ds, kj, _NN)
                dk_ref[_rows(j), :] += _dot(ds, qi, _TN)
                dv_ref[_rows(j), :] += _dot(p, doi, _TN)

            def off(j, c):
                kv(j, False)
                return c

            _kv_loop(i, off, 0)
            kv(i, True)
            dq_ref[_rows(i), :] = acc_ref[...]
            return 0

        lax.fori_loop(0, nq, qblock, 0)

    qk_spec = pl.BlockSpec((S, MLA_PAD), lambda b, h: (b, h))
    v_spec = pl.BlockSpec((S, MLA_V), lambda b, h: (b, h))
    return pl.pallas_call(
        body, name="mla_attn_bwd", grid=(B, H),
        in_specs=[qk_spec, qk_spec, v_spec, v_spec, v_spec,
                  pl.BlockSpec((1, S, 1), lambda b, h: (b * H + h, 0, 0))],
        out_specs=[qk_spec, qk_spec, v_spec],
        out_shape=[jax.ShapeDtypeStruct((B * S, H * MLA_PAD), F32), jax.ShapeDtypeStruct((B * S, H * MLA_PAD), F32),
                   jax.ShapeDtypeStruct((B * S, H * MLA_V), F32)],
        scratch_shapes=[pltpu.VMEM((ATT_BLOCK, MLA_PAD), F32)],
        compiler_params=_params(("parallel", "parallel")),
    )(q, k, v, o, do, lse)


def _ret_log_gamma():
    lg = np.log1p(-np.exp2(RET_GAMMA_BASE - np.arange(RET_HEADS, dtype=np.float32))).astype(np.float32)
    return jnp.asarray(np.broadcast_to(lg[:, None, None], (RET_HEADS, 8, LANES)).copy())


def _ret_decay(lg, rel, mask, steps):
    if steps is None:
        return jnp.where(mask, jnp.exp(lg * jnp.abs(rel)), 0.0)
    return jnp.exp(lg * (rel + (steps * ATT_BLOCK).astype(F32)))


def _ret_attn_fwd(q, k, v, B, S):
    H, nq = RET_HEADS, S // ATT_BLOCK

    def body(lg_ref, q_ref, k_ref, v_ref, o_ref, acc_ref):
        rel, mask = _rel_and_mask()
        lg = lg_ref[0, 0:1, 0:1]

        def qblock(i, _):
            qi = q_ref[_rows(i), :]
            acc_ref[...] = jnp.zeros(acc_ref.shape, F32)

            def kv(j, steps):
                a = _dot(qi, k_ref[_rows(j), :], _NT) * _ret_decay(lg, rel, mask, steps)
                acc_ref[...] += _dot(a, v_ref[_rows(j), :], _NN)

            def off(j, c):
                kv(j, i - j)
                return c

            _kv_loop(i, off, 0)
            kv(i, None)
            o_ref[_rows(i), :] = acc_ref[...]
            return 0

        lax.fori_loop(0, nq, qblock, 0)

    qk_spec = pl.BlockSpec((S, RET_QK), lambda b, h: (b, h))
    v_spec = pl.BlockSpec((S, RET_V), lambda b, h: (b, h))
    return pl.pallas_call(
        body, name="ret_attn_fwd", grid=(B, H),
        in_specs=[pl.BlockSpec((1, 8, LANES), lambda b, h: (h, 0, 0)), qk_spec, qk_spec, v_spec],
        out_specs=v_spec,
        out_shape=jax.ShapeDtypeStruct((B * S, H * RET_V), F32),
        scratch_shapes=[pltpu.VMEM((ATT_BLOCK, RET_V), F32)],
        compiler_params=_params(("parallel", "parallel")),
    )(_ret_log_gamma(), q, k, v)


def _ret_attn_bwd(q, k, v, do, B, S):
    H, nq = RET_HEADS, S // ATT_BLOCK

    def body(lg_ref, q_ref, k_ref, v_ref, do_ref, dq_ref, dk_ref, dv_ref, acc_ref):
        rel, mask = _rel_and_mask()
        lg = lg_ref[0, 0:1, 0:1]
        dk_ref[...] = jnp.zeros(dk_ref.shape, F32)
        dv_ref[...] = jnp.zeros(dv_ref.shape, F32)

        def qblock(i, _):
            qi = q_ref[_rows(i), :]
            doi = do_ref[_rows(i), :].astype(MXU_DTYPE)
            acc_ref[...] = jnp.zeros(acc_ref.shape, F32)

            def kv(j, steps):
                kj = k_ref[_rows(j), :]
                dec = _ret_decay(lg, rel, mask, steps)
                a = _dot(qi, kj, _NT) * dec
                da = (_dot(doi, v_ref[_rows(j), :], _NT) * dec).astype(MXU_DTYPE)
                acc_ref[...] += _dot(da, kj, _NN)
                dk_ref[_rows(j), :] += _dot(da, qi, _TN)
                dv_ref[_rows(j), :] += _dot(a, doi, _TN)

            def off(j, c):
                kv(j, i - j)
                return c

            _kv_loop(i, off, 0)
            kv(i, None)
            dq_ref[_rows(i), :] = acc_ref[...]
            return 0

        lax.fori_loop(0, nq, qblock, 0)

    qk_spec = pl.BlockSpec((S, RET_QK), lambda b, h: (b, h))
    v_spec = pl.BlockSpec((S, RET_V), lambda b, h: (b, h))
    return pl.pallas_call(
        body, name="ret_attn_bwd", grid=(B, H),
        in_specs=[pl.BlockSpec((1, 8, LANES), lambda b, h: (h, 0, 0)), qk_spec, qk_spec, v_spec, v_spec],
        out_specs=[qk_spec, qk_spec, v_spec],
        out_shape=[jax.ShapeDtypeStruct((B * S, H * RET_QK), F32), jax.ShapeDtypeStruct((B * S, H * RET_QK), F32),
                   jax.ShapeDtypeStruct((B * S, H * RET_V), F32)],
        scratch_shapes=[pltpu.VMEM((ATT_BLOCK, RET_QK), F32)],
        compiler_params=_params(("parallel", "parallel")),
    )(_ret_log_gamma(), q, k, v, do)


def _loss_head(y, target, bm=512):
    T, D = y.shape
    bm = _pick(T, bm)

    def body(y_ref, t_ref, dy_ref, l_ref):
        err = y_ref[...] - t_ref[...]
        dy_ref[...] = err / D
        part = jnp.full((8, LANES), 0.5 * jnp.sum(jnp.mean(err * err, axis=-1)), F32)

        @pl.when(pl.program_id(0) == 0)
        def _():
            l_ref[...] = part

        @pl.when(pl.program_id(0) > 0)
        def _():
            l_ref[...] += part

    blk = pl.BlockSpec((bm, D), lambda i: (i, 0))
    dy, l = pl.pallas_call(
        body, name="loss_head", grid=(T // bm,),
        in_specs=[blk, blk], out_specs=[blk, pl.BlockSpec((8, LANES), lambda i: (0, 0))],
        out_shape=[jax.ShapeDtypeStruct((T, D), F32), jax.ShapeDtypeStruct((8, LANES), F32)],
        compiler_params=_params(("arbitrary",)),
    )(y, target)
    return dy, l[0, 0]


def _adamw(w, g, m, v, name):
    R, C = w.shape
    br = R if R * C * 4 <= 2 ** 21 else _pick_rows(R, max(8, (2 ** 21) // (C * 4)))

    def body(w_ref, g_ref, m_ref, v_ref, d_ref, mo_ref, vo_ref):
        g_v = g_ref[...]
        m_v = ADAM_B1 * m_ref[...] + (1.0 - ADAM_B1) * g_v
        v_v = ADAM_B2 * v_ref[...] + (1.0 - ADAM_B2) * (g_v * g_v)
        m_hat = m_v / (1.0 - ADAM_B1 ** ADAM_STEP)
        v_hat = v_v / (1.0 - ADAM_B2 ** ADAM_STEP)
        d_ref[...] = -ADAM_LR * (m_hat / (jnp.sqrt(v_hat) + ADAM_EPS) + ADAM_WD * w_ref[...])
        mo_ref[...] = m_v
        vo_ref[...] = v_v

    blk = pl.BlockSpec((br, C), lambda i: (i, 0))
    return pl.pallas_call(
        body, name=name, grid=(R // br,),
        in_specs=[blk] * 4, out_specs=[blk] * 3,
        out_shape=[jax.ShapeDtypeStruct((R, C), F32)] * 3,
        compiler_params=_params(("parallel",)),
    )(w, g, m, v)


def _pick_rows(R, target):
    best = None
    for d in range(8, min(R, target) + 1, 8):
        if R % d == 0:
            best = d
    assert best is not None, (R, target)
    return best


def _position():
    return lax.axis_index("x"), lax.axis_index("y"), lax.axis_index("c")


HBM_SPEC = pl.BlockSpec(memory_space=pltpu.HBM)


def _other_chips(x, y):
    return [(1 - x, y), (x, 1 - y), (1 - x, 1 - y)]


def _all_gather_weights(bigs, small):
    nb = len(bigs)

    def body(*refs):
        big_refs, small_ref = refs[:nb], refs[nb]
        obig, osmall = refs[nb + 1:2 * nb + 1], refs[2 * nb + 1]
        ici_send, ici_recv, d2d_send, d2d_recv, sm_send, sm_recv = refs[2 * nb + 2:]
        x, y, c = _position()
        me = 2 * x + y
        chips = _other_chips(x, y)

        def rows(n, half):
            rh = bigs[n].shape[0] // 2
            return pl.ds(half * rh, rh)

        def over_ici(n, j, slot, from_shard):
            px, py = chips[j]
            dst = obig[n].at[slot, rows(n, c)]
            return pltpu.make_async_remote_copy(
                src_ref=big_refs[n].at[rows(n, c)] if from_shard else dst, dst_ref=dst,
                send_sem=ici_send.at[3 * n + j], recv_sem=ici_recv.at[3 * n + j],
                device_id=(px, py, c), device_id_type=MESH)

        def over_d2d(n, j, half):
            px, py = chips[j]
            part = obig[n].at[2 * px + py, rows(n, half)]
            return pltpu.make_async_remote_copy(
                src_ref=part, dst_ref=part, send_sem=d2d_send.at[3 * n + j], recv_sem=d2d_recv.at[3 * n + j],
                device_id=(x, y, 1 - c), device_id_type=MESH)

        def small_copy(j, slot):
            px, py = chips[j]
            return pltpu.make_async_remote_copy(
                src_ref=small_ref, dst_ref=osmall.at[slot], send_sem=sm_send.at[j], recv_sem=sm_recv.at[j],
                device_id=(px, py, c), device_id_type=MESH)

        sends = [over_ici(n, j, me, True) for n in range(nb) for j in range(3)]
        sends += [small_copy(j, me) for j in range(3)]
        for cp in sends:
            cp.start()
        passed = []
        for n in range(nb):
            for j, (px, py) in enumerate(chips):
                over_ici(n, j, 2 * px + py, False).wait_recv()
                fwd = over_d2d(n, j, c)
                fwd.start()
                passed.append(fwd)
        for n in range(nb):
            for j in range(3):
                over_d2d(n, j, 1 - c).wait_recv()
        for j, (px, py) in enumerate(chips):
            small_copy(j, 2 * px + py).wait_recv()
        for cp in sends + passed:
            cp.wait_send()

    dma = pltpu.SemaphoreType.DMA
    return pl.pallas_call(
        body, name="weights_all_gather",
        in_specs=[HBM_SPEC] * (nb + 1), out_specs=[HBM_SPEC] * (nb + 1),
        out_shape=[jax.ShapeDtypeStruct((N_SHARD,) + b.shape, b.dtype) for b in bigs]
        + [jax.ShapeDtypeStruct((N_SHARD,) + small.shape, small.dtype)],
        scratch_shapes=[dma((3 * nb,)), dma((3 * nb,)), dma((3 * nb,)), dma((3 * nb,)), dma((3,)), dma((3,))],
    )(*bigs, small)


SEM_SPEC = pl.BlockSpec(memory_space=pltpu.SEMAPHORE)
DATAFLOW_EFFECT = pltpu.SideEffectType.DATAFLOW_SIDE_EFFECTING
N_PEERS = N_DEV - 1


def _grad_copies(p_refs, land_refs, send_sems, recv_sems):
    x, y, c = _position()
    copies = []
    for a, (p_ref, land_ref) in enumerate(zip(p_refs, land_refs)):
        rh = p_ref.shape[1] // 2
        for k in range(1, N_DEV):
            px = 1 - x if k & 4 else x
            py = 1 - y if k & 2 else y
            pc = 1 - c if k & 1 else c
            copies.append(pltpu.make_async_remote_copy(
                src_ref=p_ref.at[2 * px + py, pl.ds(pc * rh, rh)], dst_ref=land_ref.at[k - 1],
                send_sem=send_sems.at[N_PEERS * a + k - 1], recv_sem=recv_sems.at[N_PEERS * a + k - 1],
                device_id=(px, py, pc), device_id_type=MESH))
    return copies


def _grads_exchange_start(ps, name):
    n = len(ps)
    lands = [lax.empty((N_PEERS, p.shape[1] // 2, p.shape[2]), p.dtype) for p in ps]

    def body(*refs):
        p_refs, land_refs = refs[:n], refs[n:2 * n]
        send_sems, recv_sems = refs[2 * n], refs[2 * n + 1]
        token = refs[-1]
        for cp in _grad_copies(p_refs, land_refs, send_sems, recv_sems):
            cp.start()
        token[...] = jnp.zeros(token.shape, token.dtype)

    hbm = lambda a: pltpu.with_memory_space_constraint(a, pltpu.HBM)
    dma = pltpu.SemaphoreType.DMA
    res = pl.pallas_call(
        body, name=name,
        in_specs=[HBM_SPEC] * (2 * n),
        out_specs=[SEM_SPEC, SEM_SPEC] + [HBM_SPEC] * (2 * n) + [pl.BlockSpec(memory_space=pltpu.VMEM)],
        out_shape=[dma((N_PEERS * n,)), dma((N_PEERS * n,))] + [pltpu.HBM(a.shape, a.dtype) for a in ps + lands]
        + [jax.ShapeDtypeStruct((8, LANES), F32)],
        input_output_aliases={i: 2 + i for i in range(2 * n)},
        compiler_params=pltpu.CompilerParams(has_side_effects=DATAFLOW_EFFECT),
    )(*[hbm(a) for a in ps], *[hbm(a) for a in lands])
    return res[0], res[1], list(res[2:2 + n]), list(res[2 + n:2 + 2 * n]), res[-1]


def _grads_exchange_wait(send_sems, recv_sems, ps, lands, after, name):
    n = len(ps)

    def body(*refs):
        p_refs, land_refs = refs[:n], refs[n:2 * n]
        send_ref, recv_ref = refs[2 * n], refs[2 * n + 1]
        for cp in _grad_copies(p_refs, land_refs, send_ref, recv_ref):
            cp.wait_send()
            cp.wait_recv()

    res = pl.pallas_call(
        body, name=name,
        in_specs=[HBM_SPEC] * (2 * n) + [SEM_SPEC, SEM_SPEC, pl.BlockSpec(memory_space=pl.ANY)],
        out_specs=[HBM_SPEC] * (2 * n),
        out_shape=[pltpu.HBM(a.shape, a.dtype) for a in ps + lands],
        input_output_aliases={i: i for i in range(2 * n)},
        compiler_params=pltpu.CompilerParams(has_side_effects=DATAFLOW_EFFECT),
    )(*ps, *lands, send_sems, recv_sems, after)
    return list(res[:n]), list(res[n:])


def _sum_partials(p, land, name):
    _, rh, cols = land.shape
    br = _pick_rows(rh, 256)
    nrb = rh // br
    x, y, c = _position()
    where = jnp.stack([2 * x + y, c]).astype(jnp.int32)

    def body(where_ref, p_ref, land_ref, o_ref):
        acc = p_ref[...].astype(F32)
        for k in range(N_PEERS):
            acc = acc + land_ref[k].astype(F32)
        o_ref[...] = acc

    return pl.pallas_call(
        body, name=name,
        grid_spec=pltpu.PrefetchScalarGridSpec(
            num_scalar_prefetch=1, grid=(nrb,),
            in_specs=[pl.BlockSpec((None, br, cols), lambda r, where_ref: (where_ref[0], where_ref[1] * nrb + r, 0)),
                      pl.BlockSpec((N_PEERS, br, cols), lambda r, where_ref: (0, r, 0))],
            out_specs=pl.BlockSpec((None, br, cols), lambda r, where_ref: (where_ref[1], r, 0))),
        out_shape=jax.ShapeDtypeStruct((2, rh, cols), F32),
        compiler_params=_params(("parallel",)),
    )(where, p, land)


def _sibling_share(fulls, name):
    n = len(fulls)

    def body(*refs):
        o_refs = refs[n:2 * n]
        send_sems, recv_sems = refs[2 * n:]
        x, y, c = _position()

        def copy(a, half):
            return pltpu.make_async_remote_copy(
                src_ref=o_refs[a].at[half], dst_ref=o_refs[a].at[half], send_sem=send_sems.at[a],
                recv_sem=recv_sems.at[a], device_id=(x, y, 1 - c), device_id_type=MESH)

        sends = [copy(a, c) for a in range(n)]
        for cp in sends:
            cp.start()
        for a in range(n):
            copy(a, 1 - c).wait_recv()
        for cp in sends:
            cp.wait_send()

    dma = pltpu.SemaphoreType.DMA
    return pl.pallas_call(
        body, name=name,
        in_specs=[HBM_SPEC] * n, out_specs=[HBM_SPEC] * n,
        out_shape=[jax.ShapeDtypeStruct(f.shape, f.dtype) for f in fulls],
        input_output_aliases={a: a for a in range(n)},
        scratch_shapes=[dma((n,)), dma((n,))],
    )(*fulls)


def _all_reduce_small(v):
    R, cols = v.shape

    def body(v_ref, o_ref, buf_ref, send_sems, recv_sems):
        x, y, c = _position()
        me = 4 * x + 2 * y + c
        buf_ref[me] = v_ref[...]
        sends = []
        for k in range(1, N_DEV):
            px = 1 - x if k & 4 else x
            py = 1 - y if k & 2 else y
            pc = 1 - c if k & 1 else c
            sends.append(pltpu.make_async_remote_copy(
                src_ref=v_ref, dst_ref=buf_ref.at[me], send_sem=send_sems.at[k - 1], recv_sem=recv_sems.at[k - 1],
                device_id=(px, py, pc), device_id_type=MESH))
        for cp in sends:
            cp.start()
        for k in range(1, N_DEV):
            px = 1 - x if k & 4 else x
            py = 1 - y if k & 2 else y
            pc = 1 - c if k & 1 else c
            pltpu.make_async_remote_copy(
                src_ref=v_ref, dst_ref=buf_ref.at[4 * px + 2 * py + pc], send_sem=send_sems.at[k - 1],
                recv_sem=recv_sems.at[k - 1], device_id=(px, py, pc), device_id_type=MESH).wait_recv()
        for cp in sends:
            cp.wait_send()
        acc = buf_ref[0]
        for d in range(1, N_DEV):
            acc = acc + buf_ref[d]
        o_ref[...] = acc

    return pl.pallas_call(
        body, name="small_grads_all_reduce",
        in_specs=[pl.BlockSpec(memory_space=pltpu.VMEM)], out_specs=pl.BlockSpec(memory_space=pltpu.VMEM),
        out_shape=jax.ShapeDtypeStruct((R, cols), F32),
        scratch_shapes=[pltpu.VMEM((N_DEV, R, cols), F32), pltpu.SemaphoreType.DMA((N_DEV - 1,)),
                        pltpu.SemaphoreType.DMA((N_DEV - 1,))],
    )(v)


def _rope_tables(S, half, width):
    inv_freq = ROPE_THETA ** (-jnp.arange(half, dtype=F32) / half)
    ang = jnp.arange(S).astype(F32)[:, None] * inv_freq[None, :]
    return jnp.cos(ang), jnp.sin(ang)


def _slot_rows(a):
    return a.reshape(N_SHARD, -1, a.shape[-1])


def _local_step(x, target, w, B, S, exchange):
    T = B * S
    D = D_MODEL
    bm = 256
    full = lambda a, wd, tile=None: (a, wd, 0, tile or wd)
    g = {}

    cos_r, sin_r = _rope_tables(S, RET_QK // 2, LANES)
    cos_m, sin_m = _rope_tables(S, MLA_ROPE // 2, LANES)
    zeros64 = jnp.zeros((S, 64), F32)
    cos_m = jnp.concatenate([cos_m, cos_m, zeros64], axis=1)
    sin_m = jnp.concatenate([-sin_m, sin_m, zeros64], axis=1)

    def ffn_fwd(xin, i):
        norm = w["ffn_norm"][i:i + 1]
        (h,) = _rowwise_fwd(_fn_rms, f"ffn{i}_norm", [full(xin, D)], [], [(norm, D)], [(D, D, BF16)], bm, S)
        ag = _mm(h, w["ffn_w_in"][i], "nn", F32, f"ffn{i}_in", bn=1408)
        u = _conv_fwd(ag, w["ffn_conv8"][i], B, S, f"ffn{i}_conv")
        xout = _mm(u, w["ffn_w_out"][i], "nn", F32, f"ffn{i}_out", residual=xin, bk=1408)
        return xout, (xin, norm, h, ag, u)

    def ffn_bwd(dxout, saved, i):
        xin, norm, h, ag, u = saved
        du = _mm(dxout, w["ffn_w_out"][i], "nt", F32, f"ffn{i}_out_dx", bn=1408)
        g_w_out = _mm(u, dxout, "tn", BF16, f"ffn{i}_out_dw", bm=1408)
        da, dg, dw8 = _conv_bwd(ag, w["ffn_conv8"][i], du, B, S, f"ffn{i}_conv_bwd")
        g_w_in = _mm(h, [da, dg], "tn", BF16, f"ffn{i}_in_dw", bn=1408, out_slots=N_SHARD)
        token = exchange(f"ffn{i}", [g_w_in, _slot_rows(g_w_out)])
        dh = _mm([da, dg], w["ffn_w_in"][i], "nt", F32, f"ffn{i}_in_dx", bk=1408, after=token)
        (dxin,), (g_norm,) = _rowwise_bwd(_fn_rms, f"ffn{i}_norm_bwd", [full(xin, D)], [], [(norm, D)],
                                          [(dh, D)], bm, S, adds={0: dxout})
        return dxin, (g_norm, dw8)

    (h0,) = _rowwise_fwd(_fn_rms, "ret_norm", [full(x, D)], [], [(w["ret_norm"], D)], [(D, D, BF16)], bm, S)
    proj = _mm(h0, w["ret_w_in"], "nn", F32, "ret_in")
    HQ, HV = RET_HEADS * RET_QK, RET_HEADS * RET_V
    rope_rows = [(proj, 2 * HQ + HV, 0, LANES)]
    q_r, k_r, v_r = _rowwise_fwd(_fn_ret_rope, "ret_rope", rope_rows, [cos_r, sin_r], [],
                                 [(HQ, LANES, BF16), (HQ, LANES, BF16), (HV, LANES, BF16)], bm, S)
    ret_o = _ret_attn_fwd(q_r, k_r, v_r, B, S)
    gate_rows = [full(ret_o, HV, RET_V), (proj, HV, 2, RET_V)]
    (y0,) = _rowwise_fwd(_fn_ret_gate, "ret_gate", gate_rows, [], [(w["ret_gn"], RET_V)], [(HV, RET_V, BF16)], 128, S)
    x1 = _mm(y0, w["ret_w_out"], "nn", F32, "ret_out", residual=x)
    x2, ffn0_saved = ffn_fwd(x1, 0)

    (h2,) = _rowwise_fwd(_fn_rms, "mla_norm", [full(x2, D)], [], [(w["mla_norm"], D)], [(D, D, BF16)], bm, S)
    proj2 = _mm(h2, w["mla_w_in"], "nn", F32, "mla_in")
    lat_consts = [(w["mla_q_norm"], LANES), (w["mla_kv_norm"], LANES)]
    cqn, ckvn, kr = _rowwise_fwd(_fn_mla_lat, "mla_latent_norm", [full(proj2, MLA_IN_PAD, LANES)], [], lat_consts,
                                 [(MLA_Q_RANK, LANES, BF16), (MLA_KV_RANK, LANES, BF16), (LANES, LANES, F32)], bm, S)
    qf = _mm(cqn, w["mla_w_qb"], "nn", F32, "mla_qb")
    kvf = _mm(ckvn, w["mla_w_kvb"], "nn", F32, "mla_kvb")
    HP, HVm = MLA_HEADS * MLA_PAD, MLA_HEADS * MLA_V
    head_rows = [full(qf, HP, LANES), full(kvf, HP, LANES), full(kr, LANES)]
    head_consts = [(w["mla_q_head_norm"], LANES), (w["mla_k_head_norm"], LANES)]
    q_a, k_a, v_a = _rowwise_fwd(_fn_mla_heads, "mla_heads", head_rows, [cos_m, sin_m], head_consts,
                                 [(HP, LANES, BF16), (HP, LANES, BF16), (HVm, LANES, BF16)], bm, S)
    att_o, lse = _mla_attn_fwd(q_a, k_a, v_a, B, S)
    x3 = _mm(att_o, w["mla_w_out"], "nn", F32, "mla_out", residual=x2)
    x4, ffn1_saved = ffn_fwd(x3, 1)

    dy, loss = _loss_head(x4, target)

    dx3, (g_n1, dw8_1) = ffn_bwd(dy, ffn1_saved, 1)

    d_att_o = _mm(dx3, w["mla_w_out"], "nt", F32, "mla_out_dx")
    g_mla_out = _mm(att_o, dx3, "tn", BF16, "mla_out_dw")
    dq_a, dk_a, dv_a = _mla_attn_bwd(q_a, k_a, v_a, att_o, d_att_o, lse, B, S)
    (dqf, dkvf, dkr), (g["mla_q_head_norm"], g["mla_k_head_norm"]) = _rowwise_bwd(
        _fn_mla_heads, "mla_heads_bwd", head_rows, [cos_m, sin_m], head_consts,
        [(dq_a, LANES), (dk_a, LANES), (dv_a, LANES)], 128, S)
    dcqn = _mm(dqf, w["mla_w_qb"], "nt", F32, "mla_qb_dx")
    g_qb = _mm(cqn, dqf, "tn", BF16, "mla_qb_dw")
    g_qb = _to_slots(_unpad_heads(g_qb, 1), 1).reshape(N_SHARD, MLA_Q_RANK, -1)
    dckvn = _mm(dkvf, w["mla_w_kvb"], "nt", F32, "mla_kvb_dx")
    g_kvb = _mm(ckvn, dkvf, "tn", BF16, "mla_kvb_dw", bn=512, out_slots=N_SHARD)
    (dproj2,), (g["mla_q_norm"], g["mla_kv_norm"]) = _rowwise_bwd(
        _fn_mla_lat, "mla_latent_norm_bwd", [full(proj2, MLA_IN_PAD, LANES)], [], lat_consts,
        [(dcqn, LANES), (dckvn, LANES), (dkr, LANES)], bm, S)
    g_mla_in = _mm(h2, dproj2, "tn", BF16, "mla_in_dw")
    token = exchange("mla", [_slot_rows(g_mla_in[:, :MLA_IN]), g_qb, g_kvb, _slot_rows(g_mla_out)])
    dh2 = _mm(dproj2, w["mla_w_in"], "nt", F32, "mla_in_dx", after=token)
    (dx2,), (g["mla_norm"],) = _rowwise_bwd(_fn_rms, "mla_norm_bwd", [full(x2, D)], [], [(w["mla_norm"], D)],
                                            [(dh2, D)], bm, S, adds={0: dx3})

    dx1, (g_n0, dw8_0) = ffn_bwd(dx2, ffn0_saved, 0)

    dy0 = _mm(dx1, w["ret_w_out"], "nt", F32, "ret_out_dx")
    g_ret_out = _mm(y0, dx1, "tn", BF16, "ret_out_dw")
    (d_ret_o, dgate), (g["ret_gn"],) = _rowwise_bwd(_fn_ret_gate, "ret_gate_bwd", gate_rows, [], [(w["ret_gn"], RET_V)],
                                                    [(dy0, RET_V)], 128, S)
    dq_r, dk_r, dv_r = _ret_attn_bwd(q_r, k_r, v_r, d_ret_o, B, S)
    (dqkv,), _ = _rowwise_bwd(_fn_ret_rope, "ret_rope_bwd", rope_rows, [cos_r, sin_r], [],
                              [(dq_r, LANES), (dk_r, LANES), (dv_r, LANES)], bm, S)
    g_ret_in = _mm(h0, [dqkv, dgate], "tn", BF16, "ret_in_dw", bn=512, out_slots=N_SHARD)
    token = exchange("ret", [g_ret_in, _slot_rows(g_ret_out)])
    dh0 = _mm([dqkv, dgate], w["ret_w_in"], "nt", F32, "ret_in_dx", bk=1024, after=token)
    (dx,), (g["ret_norm"],) = _rowwise_bwd(_fn_rms, "ret_norm_bwd", [full(x, D)], [], [(w["ret_norm"], D)],
                                           [(dh0, D)], bm, S, adds={0: dx1})

    g["ffn_norm"] = jnp.concatenate([g_n0, g_n1], axis=0)
    g["ffn_conv_w"] = jnp.stack([dw8_0[0:3], dw8_1[0:3]])
    g["ffn_conv_b"] = jnp.stack([dw8_0[3], dw8_1[3]])
    return loss, dx, g


_BIG = [("ret_w_in", 2), ("ret_w_out", 1), ("mla_w_in", 1), ("mla_w_qb", 2), ("mla_w_kvb", 2), ("mla_w_out", 1),
        ("ffn_w_in", 2), ("ffn_w_out", 1)]
_SMALL_SHARDED = [("ret_gn", 2), ("mla_norm", 1), ("mla_q_norm", 1), ("mla_kv_norm", 1), ("ffn_conv_w", 2)]
_SMALL_REPLICATED = ["ret_norm", "mla_q_head_norm", "mla_k_head_norm", "ffn_norm", "ffn_conv_b"]
_SMALL_ALL = ["ret_norm", "ret_gn", "mla_norm", "mla_q_norm", "mla_kv_norm", "mla_q_head_norm", "mla_k_head_norm",
              "ffn_norm", "ffn_conv_w", "ffn_conv_b"]


def _to_slots(full, axis):
    shape = full.shape
    split = shape[:axis] + (N_SHARD, shape[axis] // N_SHARD) + shape[axis + 1:]
    return jnp.moveaxis(full.reshape(split), axis, 0).reshape(N_SHARD, -1)


def _from_slots(slots, shard_shape, axis):
    parts = jnp.moveaxis(slots.reshape((N_SHARD,) + tuple(shard_shape)), 0, axis)
    full = shard_shape[:axis] + (N_SHARD * shard_shape[axis],) + shard_shape[axis + 1:]
    return parts.reshape(full)


def _pad_rows(flat, cols, row_unit):
    n, L = flat.shape
    unit = cols * row_unit
    Lp = -(-L // unit) * unit
    if Lp != L:
        flat = jnp.concatenate([flat, jnp.zeros((n, Lp - L), flat.dtype)], axis=1)
    return flat.reshape(n, Lp // cols, cols)


def _pad_heads(a, axis):
    shape = a.shape
    a = a.reshape(shape[:axis] + (MLA_HEADS, MLA_QK) + shape[axis + 1:])
    pad = [(0, 0)] * a.ndim
    pad[axis + 1] = (0, MLA_PAD - MLA_QK)
    return jnp.pad(a, pad).reshape(shape[:axis] + (MLA_HEADS * MLA_PAD,) + shape[axis + 1:])


def _unpad_heads(a, axis):
    shape = a.shape
    a = a.reshape(shape[:axis] + (MLA_HEADS, MLA_PAD) + shape[axis + 1:])
    a = lax.slice_in_dim(a, 0, MLA_QK, axis=axis + 1)
    return a.reshape(shape[:axis] + (MLA_HEADS * MLA_QK,) + shape[axis + 1:])


def kernel(x, ret_norm, ret_w_in, ret_gn, ret_w_out, mla_norm, mla_w_in, mla_q_norm, mla_w_qb, mla_kv_norm, mla_w_kvb, mla_q_head_norm, mla_k_head_norm, mla_w_out, ffn_norm, ffn_w_in, ffn_conv_w, ffn_conv_b, ffn_w_out, loss_target, m_ret_norm, m_ret_w_in, m_ret_gn, m_ret_w_out, m_mla_norm, m_mla_w_in, m_mla_q_norm, m_mla_w_qb, m_mla_kv_norm, m_mla_w_kvb, m_mla_q_head_norm, m_mla_k_head_norm, m_mla_w_out, m_ffn_norm, m_ffn_w_in, m_ffn_conv_w, m_ffn_conv_b, m_ffn_w_out, v_ret_norm, v_ret_w_in, v_ret_gn, v_ret_w_out, v_mla_norm, v_mla_w_in, v_mla_q_norm, v_mla_w_qb, v_mla_kv_norm, v_mla_w_kvb, v_mla_q_head_norm, v_mla_k_head_norm, v_mla_w_out, v_ffn_norm, v_ffn_w_in, v_ffn_conv_w, v_ffn_conv_b, v_ffn_w_out):
    names = ["ret_norm", "ret_w_in", "ret_gn", "ret_w_out", "mla_norm", "mla_w_in", "mla_q_norm", "mla_w_qb",
             "mla_kv_norm", "mla_w_kvb", "mla_q_head_norm", "mla_k_head_norm", "mla_w_out", "ffn_norm", "ffn_w_in",
             "ffn_conv_w", "ffn_conv_b", "ffn_w_out"]
    shard = dict(zip(names, [ret_norm, ret_w_in, ret_gn, ret_w_out, mla_norm, mla_w_in, mla_q_norm, mla_w_qb,
                             mla_kv_norm, mla_w_kvb, mla_q_head_norm, mla_k_head_norm, mla_w_out, ffn_norm, ffn_w_in,
                             ffn_conv_w, ffn_conv_b, ffn_w_out]))
    mom_m = dict(zip(names, [m_ret_norm, m_ret_w_in, m_ret_gn, m_ret_w_out, m_mla_norm, m_mla_w_in, m_mla_q_norm,
                             m_mla_w_qb, m_mla_kv_norm, m_mla_w_kvb, m_mla_q_head_norm, m_mla_k_head_norm, m_mla_w_out,
                             m_ffn_norm, m_ffn_w_in, m_ffn_conv_w, m_ffn_conv_b, m_ffn_w_out]))
    mom_v = dict(zip(names, [v_ret_norm, v_ret_w_in, v_ret_gn, v_ret_w_out, v_mla_norm, v_mla_w_in, v_mla_q_norm,
                             v_mla_w_qb, v_mla_kv_norm, v_mla_w_kvb, v_mla_q_head_norm, v_mla_k_head_norm, v_mla_w_out,
                             v_ffn_norm, v_ffn_w_in, v_ffn_conv_w, v_ffn_conv_b, v_ffn_w_out]))
    B, S, D = x.shape
    T = B * S
    sx, sy = lax.axis_index("x"), lax.axis_index("y")
    me = 2 * sx + sy

    big_names = [n for n, _ in _BIG]
    two_d = lambda a: a.reshape(-1, a.shape[-1])
    small_sizes = [int(np.prod(shard[n].shape)) for n, _ in _SMALL_SHARDED]
    small = jnp.concatenate([shard[n].reshape(1, -1) for n, _ in _SMALL_SHARDED], axis=1)
    small = _pad_rows(small, LANES, 8)[0]
    bigs = [two_d(shard[n]).astype(BF16) for n in big_names]
    *gbig, gsmall = _all_gather_weights(bigs, small)
    is_me = lax.broadcasted_iota(jnp.int32, (N_SHARD, 1, 1), 0) == me
    with_own = lambda gathered, own: jnp.where(is_me, own[None], gathered)
    gw = {n: with_own(g_, b_) for n, g_, b_ in zip(big_names, gbig, bigs)}
    gsmall = with_own(gsmall, small).reshape(N_SHARD, -1)
    by_cols = lambda a: jnp.moveaxis(a, 0, 1).reshape(a.shape[1], -1)
    by_rows = lambda a: a.reshape(-1, a.shape[-1])
    wfull = {}
    off = 0
    for (n, ax), sz in zip(_SMALL_SHARDED, small_sizes):
        wfull[n] = _from_slots(gsmall[:, off:off + sz], shard[n].shape, ax)
        off += sz
    for n in _SMALL_REPLICATED:
        wfull[n] = shard[n]

    conv8 = jnp.concatenate([wfull["ffn_conv_w"], wfull["ffn_conv_b"][:, None, :],
                             jnp.zeros((2, 4, FFN_DIM), F32)], axis=1)
    ffn_in_rows, ffn_out_rows = shard["ffn_w_in"].shape[1], shard["ffn_w_out"].shape[1]
    w = {
        "ret_norm": wfull["ret_norm"], "ret_w_in": by_cols(gw["ret_w_in"]),
        "ret_gn": wfull["ret_gn"].reshape(1, RET_HEADS * RET_V), "ret_w_out": by_rows(gw["ret_w_out"]),
        "mla_norm": wfull["mla_norm"],
        "mla_w_in": jnp.pad(by_rows(gw["mla_w_in"]), ((0, 0), (0, MLA_IN_PAD - MLA_IN))),
        "mla_q_norm": wfull["mla_q_norm"], "mla_w_qb": _pad_heads(by_cols(gw["mla_w_qb"]), 1),
        "mla_kv_norm": wfull["mla_kv_norm"], "mla_w_kvb": by_cols(gw["mla_w_kvb"]),
        "mla_q_head_norm": jnp.pad(wfull["mla_q_head_norm"], ((0, 0), (0, MLA_PAD - MLA_QK))),
        "mla_k_head_norm": jnp.pad(wfull["mla_k_head_norm"], ((0, 0), (0, MLA_PAD - MLA_QK))),
        "mla_w_out": by_rows(gw["mla_w_out"]), "ffn_norm": wfull["ffn_norm"],
        "ffn_w_in": [by_cols(gw["ffn_w_in"][:, i * ffn_in_rows:(i + 1) * ffn_in_rows]) for i in range(2)],
        "ffn_conv8": conv8,
        "ffn_w_out": [by_rows(gw["ffn_w_out"][:, i * ffn_out_rows:(i + 1) * ffn_out_rows]) for i in range(2)],
    }

    started = {}

    def exchange(group, arrays):
        send_sems, recv_sems, ps, lands, token = _grads_exchange_start(arrays, f"grads_start_{group}")
        started[group] = (send_sems, recv_sems, ps, lands)
        return token

    loss_part, dx, gl = _local_step(x.reshape(T, D), loss_target.reshape(T, D), w, B, S, exchange)
    loss = lax.psum(loss_part, ("x", "y", "c"))
    gfull = {
        "ret_norm": gl["ret_norm"], "ret_gn": gl["ret_gn"].reshape(1, RET_HEADS, RET_V),
        "mla_norm": gl["mla_norm"], "mla_q_norm": gl["mla_q_norm"], "mla_kv_norm": gl["mla_kv_norm"],
        "mla_q_head_norm": gl["mla_q_head_norm"][:, :MLA_QK], "mla_k_head_norm": gl["mla_k_head_norm"][:, :MLA_QK],
        "ffn_norm": gl["ffn_norm"], "ffn_conv_w": gl["ffn_conv_w"], "ffn_conv_b": gl["ffn_conv_b"],
    }

    red = {}
    after = dx
    for group in ("ffn1", "mla", "ffn0", "ret"):
        send_sems, recv_sems, ps, lands = started[group]
        ps, lands = _grads_exchange_wait(send_sems, recv_sems, ps, lands, after, f"grads_wait_{group}")
        halves = [_sum_partials(p_, l_, f"grads_sum_{group}_{i}") for i, (p_, l_) in enumerate(zip(ps, lands))]
        red[group] = [two_d(r) for r in _sibling_share(halves, f"grads_share_{group}")]
        after = red[group][0]
    grads = {"ret_w_in": red["ret"][0], "ret_w_out": red["ret"][1], "mla_w_in": red["mla"][0],
             "mla_w_qb": red["mla"][1], "mla_w_kvb": red["mla"][2], "mla_w_out": red["mla"][3]}
    grads = {n: a.reshape(shard[n].shape) for n, a in grads.items()}
    grads["ffn_w_in"] = jnp.stack([red["ffn0"][0], red["ffn1"][0]])
    grads["ffn_w_out"] = jnp.stack([red["ffn0"][1], red["ffn1"][1]])

    small_sizes_all = [int(np.prod(gfull[n].shape)) for n in _SMALL_ALL]
    gsm = jnp.concatenate([gfull[n].reshape(1, -1) for n in _SMALL_ALL], axis=1)
    gsm = _all_reduce_small(_pad_rows(gsm, LANES, 8)[0]).reshape(-1)

    sharded_axis = dict(_SMALL_SHARDED)
    off = 0
    for n, sz in zip(_SMALL_ALL, small_sizes_all):
        gn = gsm[off:off + sz].reshape(gfull[n].shape)
        off += sz
        if n in sharded_axis:
            ax = sharded_axis[n]
            width = shard[n].shape[ax]
            gn = lax.dynamic_slice_in_dim(gn, me * width, width, axis=ax)
        grads[n] = gn

    delta, new_m, new_v = {}, {}, {}
    for n, _ in _BIG:
        shp = shard[n].shape
        two_d = lambda a: a.reshape(-1, shp[-1])
        d_, m_, v_ = _adamw(two_d(shard[n]), two_d(grads[n]), two_d(mom_m[n]), two_d(mom_v[n]), f"adamw_{n}")
        delta[n], new_m[n], new_v[n] = d_.reshape(shp), m_.reshape(shp), v_.reshape(shp)
    pack_small = lambda d: _pad_rows(jnp.concatenate([d[n].reshape(1, -1) for n in _SMALL_ALL], axis=1), LANES, 8)[0]
    d_, m_, v_ = _adamw(pack_small(shard), pack_small(grads), pack_small(mom_m), pack_small(mom_v), "adamw_small")
    off = 0
    for n in _SMALL_ALL:
        sz = int(np.prod(shard[n].shape))
        for dst, src in ((delta, d_), (new_m, m_), (new_v, v_)):
            dst[n] = src.reshape(-1)[off:off + sz].reshape(shard[n].shape)
        off += sz

    return (loss, dx.reshape(B, S, D), *[grads[n] for n in names], *[delta[n] for n in names],
            *[new_m[n] for n in names], *[new_v[n] for n in names])
```

```python
import functools
import math

import numpy as np
import jax
import jax.numpy as jnp
from jax import lax
from jax.experimental import pallas as pl
from jax.experimental.pallas import tpu as pltpu

F32 = jnp.float32
BF16 = jnp.bfloat16
MXU_DTYPE = jnp.bfloat16

CHUNK = 64
RMS_EPS = 1e-6
ROPE_THETA = 10000.0
D_MODEL = 1024
RET_HEADS = 4
RET_QK = 256
RET_V = 512
RET_GAMMA_BASE = -5.0
MLA_HEADS = 8
MLA_Q_RANK = 384
MLA_KV_RANK = 256
MLA_NOPE = 128
MLA_ROPE = 64
MLA_V = 128
MLA_QK = MLA_NOPE + MLA_ROPE
MLA_PAD = 256
MLA_IN = MLA_Q_RANK + MLA_KV_RANK + MLA_ROPE
MLA_IN_PAD = MLA_IN + 64
MASK_VALUE = -1e30
FFN_DIM = 2816
ADAM_LR = 0.001
ADAM_B1 = 0.9
ADAM_B2 = 0.999
ADAM_EPS = 1e-08
ADAM_WD = 0.01
ADAM_STEP = 10

LANES = 128
ATT_BLOCK = 256
VMEM_LIMIT = 56 * 2 ** 20
N_SHARD = 4
N_DEV = 8

MESH = pl.DeviceIdType.MESH


def _params(sem=None, **kw):
    return pltpu.CompilerParams(dimension_semantics=sem, vmem_limit_bytes=VMEM_LIMIT, **kw)


def _pick(dim, target):
    if dim <= target:
        return dim
    best = None
    for d in range(LANES, target + 1, LANES):
        if dim % d == 0:
            best = d
    assert best is not None, (dim, target)
    return best


def _mm(a, b, dims, out_dtype, name, residual=None, bm=512, bn=1024, bk=2048, out_slots=None, after=None):
    a_parts = list(a) if isinstance(a, (list, tuple)) else [a]
    b_parts = list(b) if isinstance(b, (list, tuple)) else [b]
    if dims == "tn":
        assert len(a_parts) == 1
        K, M = a_parts[0].shape
        N = sum(p.shape[1] for p in b_parts)
        part_widths = [p.shape[1] for p in b_parts]
    else:
        assert len(b_parts) == 1
        M = a_parts[0].shape[0]
        K = sum(p.shape[1] for p in a_parts)
        N = b_parts[0].shape[1 if dims == "nn" else 0]
        part_widths = [p.shape[1] for p in a_parts]
    bm, bn, bk = _pick(M, bm), _pick(N, bn), _pick(K, min(bk, 1024) if dims == "tn" else bk)
    nk = K // bk
    unit = bn if dims == "tn" else bk
    assert all(wd % unit == 0 for wd in part_widths), (name, part_widths, unit)
    bounds = np.cumsum([0] + [wd // unit for wd in part_widths])
    ranges = [(int(lo), int(hi)) for lo, hi in zip(bounds[:-1], bounds[1:])]

    def part_index(idx, lo, hi):
        return jnp.clip(idx - lo, 0, hi - lo - 1)

    if dims == "tn":
        a_specs = [pl.BlockSpec((bk, bm), lambda i, j, k: (k, i))]
        b_specs = [pl.BlockSpec((bk, bn), functools.partial(lambda i, j, k, lo, hi: (k, part_index(j, lo, hi)), lo=lo, hi=hi))
                   for lo, hi in ranges]
        dn = (((0,), (0,)), ((), ()))
    else:
        a_specs = [pl.BlockSpec((bm, bk), functools.partial(lambda i, j, k, lo, hi: (i, part_index(k, lo, hi)), lo=lo, hi=hi))
                   for lo, hi in ranges]
        if dims == "nt":
            b_specs = [pl.BlockSpec((bn, bk), lambda i, j, k: (j, k))]
        else:
            b_specs = [pl.BlockSpec((bk, bn), lambda i, j, k: (k, j))]
        dn = (((1,), (1 if dims == "nt" else 0,)), ((), ()))
    r_spec = pl.BlockSpec((bm, bn), lambda i, j, k: (i, j))
    if out_slots is None:
        o_spec, o_shape = r_spec, (M, N)
    else:
        ns = N // out_slots
        assert ns % bn == 0, (name, ns, bn)
        nbs = ns // bn
        o_spec = pl.BlockSpec((None, bm, bn), lambda i, j, k: (j // nbs, i, j % nbs))
        o_shape = (out_slots, M, ns)
    has_res = residual is not None
    na, nb = len(a_parts), len(b_parts)

    def body(*refs):
        a_refs, b_refs = refs[:na], refs[na:na + nb]
        r_ref = refs[na + nb] if has_res else None
        n_in = na + nb + has_res + (after is not None)
        o_ref = refs[n_in]
        acc_ref = refs[n_in + 1] if nk > 1 else None
        k = pl.program_id(2)

        def finish(acc):
            if has_res:
                acc = acc + r_ref[...].astype(F32)
            o_ref[...] = acc.astype(out_dtype)

        def compute(a_ref, b_ref):
            p = lax.dot_general(a_ref[...].astype(MXU_DTYPE), b_ref[...].astype(MXU_DTYPE), dn,
                                preferred_element_type=F32)
            if nk == 1:
                finish(p)
                return

            @pl.when(k == 0)
            def _():
                acc_ref[...] = p

            @pl.when(jnp.logical_and(k > 0, k < nk - 1))
            def _():
                acc_ref[...] += p

            @pl.when(k == nk - 1)
            def _():
                finish(acc_ref[...] + p)

        if len(ranges) == 1:
            compute(a_refs[0], b_refs[0])
        else:
            idx = pl.program_id(1) if dims == "tn" else k
            for p, (lo, hi) in enumerate(ranges):
                @pl.when(jnp.logical_and(idx >= lo, idx < hi))
                def _(p=p):
                    compute(a_refs[0 if dims == "tn" else p], b_refs[p if dims == "tn" else 0])

    after_specs = [] if after is None else [pl.BlockSpec(after.shape, lambda i, j, k: (0, 0))]
    return pl.pallas_call(
        body, name=name, grid=(M // bm, N // bn, nk),
        in_specs=a_specs + b_specs + ([r_spec] if has_res else []) + after_specs, out_specs=o_spec,
        out_shape=jax.ShapeDtypeStruct(o_shape, out_dtype),
        scratch_shapes=[pltpu.VMEM((bm, bn), F32)] if nk > 1 else [],
        compiler_params=_params(("parallel", "parallel", "arbitrary")),
    )(*a_parts, *b_parts, *((residual,) if has_res else ()), *(() if after is None else (after,)))


def _tiles(ref, width, tile):
    return [ref[:, t * tile:(t + 1) * tile].astype(F32) for t in range(width // tile)]


def _row_specs(rows, pos, consts, bm, S):
    npos_blocks = S // bm
    specs = [pl.BlockSpec((bm, w), functools.partial(lambda i, c: (i, c), c=cb)) for (_, w, cb, _) in rows]
    specs += [pl.BlockSpec((bm, p.shape[1]), lambda i: (i % npos_blocks, 0)) for p in pos]
    specs += [pl.BlockSpec(c.shape, lambda i: (0, 0)) for (c, _) in consts]
    return specs


def _rowwise_fwd(fn, name, rows, pos, consts, outs, bm, S):
    T = rows[0][0].shape[0]
    nr, npos, nc = len(rows), len(pos), len(consts)

    def body(*refs):
        row_v = [_tiles(r, w, t) for r, (_, w, _, t) in zip(refs[:nr], rows)]
        pos_v = [r[...] for r in refs[nr:nr + npos]]
        const_v = [_tiles(r, c.shape[1], t) for r, (c, t) in zip(refs[nr + npos:nr + npos + nc], consts)]
        res = fn(row_v, pos_v, const_v)
        for o_ref, tiles, (w, t, dt) in zip(refs[nr + npos + nc:], res, outs):
            for k, v in enumerate(tiles):
                o_ref[:, k * t:(k + 1) * t] = v.astype(dt)

    return pl.pallas_call(
        body, name=name, grid=(T // bm,),
        in_specs=_row_specs(rows, pos, consts, bm, S),
        out_specs=[pl.BlockSpec((bm, w), lambda i: (i, 0)) for (w, _, _) in outs],
        out_shape=[jax.ShapeDtypeStruct((T, w), dt) for (w, _, dt) in outs],
        compiler_params=_params(("parallel",)),
    )(*[r[0] for r in rows], *pos, *[c[0] for c in consts])


def _rowwise_bwd(fn, name, rows, pos, consts, cts, bm, S, adds=None):
    adds = adds or {}
    T = rows[0][0].shape[0]
    nr, npos, nc, nct = len(rows), len(pos), len(consts), len(cts)
    add_idx = sorted(adds)

    def body(*refs):
        it = iter(refs)
        row_refs = [next(it) for _ in range(nr)]
        pos_refs = [next(it) for _ in range(npos)]
        const_refs = [next(it) for _ in range(nc)]
        ct_refs = [next(it) for _ in range(nct)]
        add_refs = {k: next(it) for k in add_idx}
        drow_refs = [next(it) for _ in range(nr)]
        dconst_refs = [next(it) for _ in range(nc)]
        row_v = [_tiles(r, w, t) for r, (_, w, _, t) in zip(row_refs, rows)]
        pos_v = [r[...] for r in pos_refs]
        const_v = [_tiles(r, c.shape[1], t) for r, (c, t) in zip(const_refs, consts)]
        ct_v = [_tiles(r, c.shape[1], t) for r, (c, t) in zip(ct_refs, cts)]
        _, vjp = jax.vjp(lambda rv, cv: fn(rv, pos_v, cv), row_v, const_v)
        drows, dconsts = vjp(ct_v)
        for a, (d_ref, tiles, (_, w, _, t)) in enumerate(zip(drow_refs, drows, rows)):
            for k, v in enumerate(tiles):
                if a in add_refs:
                    v = v + add_refs[a][:, k * t:(k + 1) * t].astype(F32)
                d_ref[:, k * t:(k + 1) * t] = v
        first = pl.program_id(0) == 0
        for d_ref, tiles, (_, t) in zip(dconst_refs, dconsts, consts):
            for k, v in enumerate(tiles):
                @pl.when(first)
                def _(d_ref=d_ref, k=k, t=t, v=v):
                    d_ref[:, k * t:(k + 1) * t] = v

                @pl.when(jnp.logical_not(first))
                def _(d_ref=d_ref, k=k, t=t, v=v):
                    d_ref[:, k * t:(k + 1) * t] += v

    in_specs = _row_specs(rows, pos, consts, bm, S)
    in_specs += [pl.BlockSpec((bm, c.shape[1]), lambda i: (i, 0)) for (c, _) in cts]
    in_specs += [pl.BlockSpec((bm, adds[k].shape[1]), lambda i: (i, 0)) for k in add_idx]
    out_specs = [pl.BlockSpec((bm, w), lambda i: (i, 0)) for (_, w, _, _) in rows]
    out_specs += [pl.BlockSpec(c.shape, lambda i: (0, 0)) for (c, _) in consts]
    out_shape = [jax.ShapeDtypeStruct((T, w), F32) for (_, w, _, _) in rows]
    out_shape += [jax.ShapeDtypeStruct(c.shape, F32) for (c, _) in consts]
    res = pl.pallas_call(
        body, name=name, grid=(T // bm,),
        in_specs=in_specs, out_specs=out_specs, out_shape=out_shape,
        compiler_params=_params(("arbitrary",)),
    )(*[r[0] for r in rows], *pos, *[c[0] for c in consts], *[c[0] for c in cts], *[adds[k] for k in add_idx])
    return res[:nr], res[nr:]


def _ssq(tiles):
    s = jnp.sum(tiles[0] * tiles[0], axis=-1, keepdims=True)
    for t in tiles[1:]:
        s = s + jnp.sum(t * t, axis=-1, keepdims=True)
    return s


def _sigmoid(x):
    return 1.0 / (1.0 + jnp.exp(-x))


def _fn_rms(rows, pos, consts):
    (x,), (g,) = rows[0], consts[0]
    r = lax.rsqrt(jnp.mean(x * x, axis=-1, keepdims=True) + RMS_EPS)
    return [[x * r * g]]


def _fn_ret_rope(rows, pos, consts):
    (qkv,) = rows
    nq = RET_HEADS * RET_QK // LANES
    q, k, v = qkv[:nq], qkv[nq:2 * nq], qkv[2 * nq:]
    cos, sin = pos

    def rot(t, scale):
        out = []
        for h in range(RET_HEADS):
            x1, x2 = t[2 * h], t[2 * h + 1]
            o1, o2 = x1 * cos - x2 * sin, x2 * cos + x1 * sin
            out += [o1, o2] if scale is None else [o1 * scale, o2 * scale]
        return out

    return [rot(q, None), rot(k, RET_QK ** -0.5), list(v)]


def _fn_ret_gate(rows, pos, consts):
    o, g = rows
    (gn,) = consts
    out = []
    for h in range(RET_HEADS):
        r = lax.rsqrt(jnp.mean(o[h] * o[h], axis=-1, keepdims=True) + RMS_EPS)
        out.append((o[h] * r * gn[h]) * (g[h] * _sigmoid(g[h])))
    return [out]


def _fn_mla_lat(rows, pos, consts):
    (p,) = rows
    gq, gkv = consts
    nq, nkv = MLA_Q_RANK // LANES, MLA_KV_RANK // LANES
    cq, ckv, kr = p[:nq], p[nq:nq + nkv], p[nq + nkv]
    rq = lax.rsqrt(_ssq(cq) / MLA_Q_RANK + RMS_EPS)
    rkv = lax.rsqrt(_ssq(ckv) / MLA_KV_RANK + RMS_EPS)
    return [[t * rq * g for t, g in zip(cq, gq)], [t * rkv * g for t, g in zip(ckv, gkv)], [kr]]


def _swap32_impl(x):
    lane = lax.broadcasted_iota(jnp.int32, x.shape, 1)
    up, down = pltpu.roll(x, LANES - 32, 1), pltpu.roll(x, 32, 1)
    return jnp.where(lane < 32, up, jnp.where(lane < 64, down, 0.0))


@jax.custom_vjp
def _swap32(x):
    return _swap32_impl(x)


_swap32.defvjp(lambda x: (_swap32_impl(x), None), lambda _, g: (_swap32_impl(g),))


def _fn_mla_heads(rows, pos, consts):
    qf, kvf, (kr,) = rows
    cos, sin = pos
    gq, gk = consts
    q_out, k_out, v_out = [], [], []
    for h in range(MLA_HEADS):
        q0, q1 = qf[2 * h], qf[2 * h + 1]
        r = lax.rsqrt(_ssq([q0, q1]) / MLA_QK + RMS_EPS)
        a0, a1 = q0 * r * gq[0], q1 * r * gq[1]
        a1 = a1 * cos + _swap32(a1) * sin
        q_out += [a0 * (MLA_QK ** -0.5), a1 * (MLA_QK ** -0.5)]
        k0 = kvf[2 * h]
        r = lax.rsqrt(_ssq([k0, kr]) / MLA_QK + RMS_EPS)
        b0, b1 = k0 * r * gk[0], kr * r * gk[1]
        k_out += [b0, b1 * cos + _swap32(b1) * sin]
        v_out.append(kvf[2 * h + 1])
    return [q_out, k_out, v_out]


def _shift_down(x, n):
    row = lax.broadcasted_iota(jnp.int32, x.shape, 0)
    return jnp.where(row >= n, pltpu.roll(x, n, 0), 0.0)


def _shift_up(x, n):
    rows = x.shape[0]
    row = lax.broadcasted_iota(jnp.int32, x.shape, 0)
    return jnp.where(row < rows - n, pltpu.roll(x, rows - n, 0), 0.0)


def _conv_blocks(S):
    cb = 256
    return cb, FFN_DIM // cb


def _conv_fwd(ag, w8, B, S, name):
    cb, ncb = _conv_blocks(S)

    def body(a_ref, g_ref, w_ref, u_ref):
        g = g_ref[...]
        w = w_ref[...]
        gc = w[0:1] * _shift_down(g, 2) + w[1:2] * _shift_down(g, 1) + w[2:3] * g + w[3:4]
        u_ref[...] = (a_ref[...] * (gc * _sigmoid(gc))).astype(u_ref.dtype)

    return pl.pallas_call(
        body, name=name, grid=(ncb, B),
        in_specs=[pl.BlockSpec((S, cb), lambda j, b: (b, j)),
                  pl.BlockSpec((S, cb), lambda j, b: (b, ncb + j)),
                  pl.BlockSpec((8, cb), lambda j, b: (0, j))],
        out_specs=pl.BlockSpec((S, cb), lambda j, b: (b, j)),
        out_shape=jax.ShapeDtypeStruct((B * S, FFN_DIM), BF16),
        compiler_params=_params(("parallel", "parallel")),
    )(ag, ag, w8)


def _conv_bwd(ag, w8, du, B, S, name):
    cb, ncb = _conv_blocks(S)

    def body(a_ref, g_ref, w_ref, du_ref, da_ref, dg_ref, dw_ref):
        g = g_ref[...]
        w = w_ref[...]
        g1, g2 = _shift_down(g, 1), _shift_down(g, 2)
        gc = w[0:1] * g2 + w[1:2] * g1 + w[2:3] * g + w[3:4]
        sg = _sigmoid(gc)
        du_v = du_ref[...]
        da_ref[...] = du_v * (gc * sg)
        dgc = du_v * a_ref[...] * (sg * (1.0 + gc * (1.0 - sg)))
        dg_ref[...] = w[2:3] * dgc + w[1:2] * _shift_up(dgc, 1) + w[0:1] * _shift_up(dgc, 2)
        part = jnp.concatenate([
            jnp.sum(dgc * g2, axis=0, keepdims=True), jnp.sum(dgc * g1, axis=0, keepdims=True),
            jnp.sum(dgc * g, axis=0, keepdims=True), jnp.sum(dgc, axis=0, keepdims=True),
            jnp.zeros((4, cb), F32)], axis=0)

        @pl.when(pl.program_id(1) == 0)
        def _():
            dw_ref[...] = part

        @pl.when(pl.program_id(1) > 0)
        def _():
            dw_ref[...] += part

    blk = lambda j, b: (b, j)
    return pl.pallas_call(
        body, name=name, grid=(ncb, B),
        in_specs=[pl.BlockSpec((S, cb), blk),
                  pl.BlockSpec((S, cb), lambda j, b: (b, ncb + j)),
                  pl.BlockSpec((8, cb), lambda j, b: (0, j)),
                  pl.BlockSpec((S, cb), blk)],
        out_specs=[pl.BlockSpec((S, cb), blk), pl.BlockSpec((S, cb), blk),
                   pl.BlockSpec((8, cb), lambda j, b: (0, j))],
        out_shape=[jax.ShapeDtypeStruct((B * S, FFN_DIM), F32), jax.ShapeDtypeStruct((B * S, FFN_DIM), F32),
                   jax.ShapeDtypeStruct((8, FFN_DIM), F32)],
        compiler_params=_params(("parallel", "arbitrary")),
    )(ag, ag, w8, du)


_NT = (((1,), (1,)), ((), ()))
_NN = (((1,), (0,)), ((), ()))
_TN = (((0,), (0,)), ((), ()))


def _dot(a, b, dn):
    return lax.dot_general(a.astype(MXU_DTYPE), b.astype(MXU_DTYPE), dn, preferred_element_type=F32)


def _rel_and_mask():
    il = lax.broadcasted_iota(jnp.int32, (ATT_BLOCK, ATT_BLOCK), 0)
    jl = lax.broadcasted_iota(jnp.int32, (ATT_BLOCK, ATT_BLOCK), 1)
    return (il - jl).astype(F32), (jl // CHUNK) <= (il // CHUNK)


def _rows(i):
    return pl.ds(pl.multiple_of(i * ATT_BLOCK, ATT_BLOCK), ATT_BLOCK)


KV_UNROLL = 2


def _kv_loop(n, body, carry):
    main = n // KV_UNROLL

    def chunk(t, c):
        for u in range(KV_UNROLL):
            c = body(t * KV_UNROLL + u, c)
        return c

    carry = lax.fori_loop(0, main, chunk, carry)
    return lax.fori_loop(main * KV_UNROLL, n, body, carry)


def _mla_attn_fwd(q, k, v, B, S):
    H, nq = MLA_HEADS, S // ATT_BLOCK

    def body(q_ref, k_ref, v_ref, o_ref, lse_ref):
        _, mask = _rel_and_mask()

        def qblock(i, _):
            qi = q_ref[_rows(i), :]

            def kv(j, carry, diag):
                m, l, acc = carry
                s = _dot(qi, k_ref[_rows(j), :], _NT)
                if diag:
                    s = jnp.where(mask, s, MASK_VALUE)
                m2 = jnp.maximum(m, jnp.max(s, axis=-1, keepdims=True))
                alpha = jnp.exp(m - m2)
                p = jnp.exp(s - m2)
                l2 = alpha * l + jnp.sum(p, axis=-1, keepdims=True)
                return m2, l2, alpha * acc + _dot(p, v_ref[_rows(j), :], _NN)

            init = (jnp.full((ATT_BLOCK, 1), MASK_VALUE, F32), jnp.zeros((ATT_BLOCK, 1), F32),
                    jnp.zeros((ATT_BLOCK, MLA_V), F32))
            carry = _kv_loop(i, lambda j, c: kv(j, c, False), init)
            m, l, acc = kv(i, carry, True)
            o_ref[_rows(i), :] = acc / l
            lse_ref[0, _rows(i), :] = m + jnp.log(l)
            return 0

        lax.fori_loop(0, nq, qblock, 0)

    return pl.pallas_call(
        body, name="mla_attn_fwd", grid=(B, H),
        in_specs=[pl.BlockSpec((S, MLA_PAD), lambda b, h: (b, h)),
                  pl.BlockSpec((S, MLA_PAD), lambda b, h: (b, h)),
                  pl.BlockSpec((S, MLA_V), lambda b, h: (b, h))],
        out_specs=[pl.BlockSpec((S, MLA_V), lambda b, h: (b, h)),
                   pl.BlockSpec((1, S, 1), lambda b, h: (b * H + h, 0, 0))],
        out_shape=[jax.ShapeDtypeStruct((B * S, H * MLA_V), F32), jax.ShapeDtypeStruct((B * H, S, 1), F32)],
        compiler_params=_params(("parallel", "parallel")),
    )(q, k, v)


def _mla_attn_bwd(q, k, v, o, do, lse, B, S):
    H, nq = MLA_HEADS, S // ATT_BLOCK

    def body(q_ref, k_ref, v_ref, o_ref, do_ref, lse_ref, dq_ref, dk_ref, dv_ref, acc_ref):
        _, mask = _rel_and_mask()
        dk_ref[...] = jnp.zeros(dk_ref.shape, F32)
        dv_ref[...] = jnp.zeros(dv_ref.shape, F32)

        def qblock(i, _):
            qi = q_ref[_rows(i), :]
            doi = do_ref[_rows(i), :]
            delta = jnp.sum(doi * o_ref[_rows(i), :], axis=-1, keepdims=True)
            lse_i = lse_ref[0, _rows(i), :]
            doi = doi.astype(MXU_DTYPE)
            acc_ref[...] = jnp.zeros(acc_ref.shape, F32)

            def kv(j, diag):
                kj = k_ref[_rows(j), :]
                p = jnp.exp(_dot(qi, kj, _NT) - lse_i)
                if diag:
                    p = jnp.where(mask, p, 0.0)
                ds = (p * (_dot(doi, v_ref[_rows(j), :], _NT) - delta)).astype(MXU_DTYPE)
                acc_ref[...] += _dot(ds, kj, _NN)
                dk_ref[_rows(j), :] += _dot(ds, qi, _TN)
                dv_ref[_rows(j), :] += _dot(p, doi, _TN)

            def off(j, c):
                kv(j, False)
                return c

            _kv_loop(i, off, 0)
            kv(i, True)
            dq_ref[_rows(i), :] = acc_ref[...]
            return 0

        lax.fori_loop(0, nq, qblock, 0)

    qk_spec = pl.BlockSpec((S, MLA_PAD), lambda b, h: (b, h))
    v_spec = pl.BlockSpec((S, MLA_V), lambda b, h: (b, h))
    return pl.pallas_call(
        body, name="mla_attn_bwd", grid=(B, H),
        in_specs=[qk_spec, qk_spec, v_spec, v_spec, v_spec,
                  pl.BlockSpec((1, S, 1), lambda b, h: (b * H + h, 0, 0))],
        out_specs=[qk_spec, qk_spec, v_spec],
        out_shape=[jax.ShapeDtypeStruct((B * S, H * MLA_PAD), F32), jax.ShapeDtypeStruct((B * S, H * MLA_PAD), F32),
                   jax.ShapeDtypeStruct((B * S, H * MLA_V), F32)],
        scratch_shapes=[pltpu.VMEM((ATT_BLOCK, MLA_PAD), F32)],
        compiler_params=_params(("parallel", "parallel")),
    )(q, k, v, o, do, lse)


def _ret_log_gamma():
    lg = np.log1p(-np.exp2(RET_GAMMA_BASE - np.arange(RET_HEADS, dtype=np.float32))).astype(np.float32)
    return jnp.asarray(np.broadcast_to(lg[:, None, None], (RET_HEADS, 8, LANES)).copy())


def _ret_decay(lg, rel, mask, steps):
    if steps is None:
        return jnp.where(mask, jnp.exp(lg * jnp.abs(rel)), 0.0)
    return jnp.exp(lg * (rel + (steps * ATT_BLOCK).astype(F32)))


def _ret_attn_fwd(q, k, v, B, S):
    H, nq = RET_HEADS, S // ATT_BLOCK

    def body(lg_ref, q_ref, k_ref, v_ref, o_ref, acc_ref):
        rel, mask = _rel_and_mask()
        lg = lg_ref[0, 0:1, 0:1]

        def qblock(i, _):
            qi = q_ref[_rows(i), :]
            acc_ref[...] = jnp.zeros(acc_ref.shape, F32)

            def kv(j, steps):
                a = _dot(qi, k_ref[_rows(j), :], _NT) * _ret_decay(lg, rel, mask, steps)
                acc_ref[...] += _dot(a, v_ref[_rows(j), :], _NN)

            def off(j, c):
                kv(j, i - j)
                return c

            _kv_loop(i, off, 0)
            kv(i, None)
            o_ref[_rows(i), :] = acc_ref[...]
            return 0

        lax.fori_loop(0, nq, qblock, 0)

    qk_spec = pl.BlockSpec((S, RET_QK), lambda b, h: (b, h))
    v_spec = pl.BlockSpec((S, RET_V), lambda b, h: (b, h))
    return pl.pallas_call(
        body, name="ret_attn_fwd", grid=(B, H),
        in_specs=[pl.BlockSpec((1, 8, LANES), lambda b, h: (h, 0, 0)), qk_spec, qk_spec, v_spec],
        out_specs=v_spec,
        out_shape=jax.ShapeDtypeStruct((B * S, H * RET_V), F32),
        scratch_shapes=[pltpu.VMEM((ATT_BLOCK, RET_V), F32)],
        compiler_params=_params(("parallel", "parallel")),
    )(_ret_log_gamma(), q, k, v)


def _ret_attn_bwd(q, k, v, do, B, S):
    H, nq = RET_HEADS, S // ATT_BLOCK

    def body(lg_ref, q_ref, k_ref, v_ref, do_ref, dq_ref, dk_ref, dv_ref, acc_ref):
        rel, mask = _rel_and_mask()
        lg = lg_ref[0, 0:1, 0:1]
        dk_ref[...] = jnp.zeros(dk_ref.shape, F32)
        dv_ref[...] = jnp.zeros(dv_ref.shape, F32)

        def qblock(i, _):
            qi = q_ref[_rows(i), :]
            doi = do_ref[_rows(i), :].astype(MXU_DTYPE)
            acc_ref[...] = jnp.zeros(acc_ref.shape, F32)

            def kv(j, steps):
                kj = k_ref[_rows(j), :]
                dec = _ret_decay(lg, rel, mask, steps)
                a = _dot(qi, kj, _NT) * dec
                da = (_dot(doi, v_ref[_rows(j), :], _NT) * dec).astype(MXU_DTYPE)
                acc_ref[...] += _dot(da, kj, _NN)
                dk_ref[_rows(j), :] += _dot(da, qi, _TN)
                dv_ref[_rows(j), :] += _dot(a, doi, _TN)

            def off(j, c):
                kv(j, i - j)
                return c

            _kv_loop(i, off, 0)
            kv(i, None)
            dq_ref[_rows(i), :] = acc_ref[...]
            return 0

        lax.fori_loop(0, nq, qblock, 0)

    qk_spec = pl.BlockSpec((S, RET_QK), lambda b, h: (b, h))
    v_spec = pl.BlockSpec((S, RET_V), lambda b, h: (b, h))
    return pl.pallas_call(
        body, name="ret_attn_bwd", grid=(B, H),
        in_specs=[pl.BlockSpec((1, 8, LANES), lambda b, h: (h, 0, 0)), qk_spec, qk_spec, v_spec, v_spec],
        out_specs=[qk_spec, qk_spec, v_spec],
        out_shape=[jax.ShapeDtypeStruct((B * S, H * RET_QK), F32), jax.ShapeDtypeStruct((B * S, H * RET_QK), F32),
                   jax.ShapeDtypeStruct((B * S, H * RET_V), F32)],
        scratch_shapes=[pltpu.VMEM((ATT_BLOCK, RET_QK), F32)],
        compiler_params=_params(("parallel", "parallel")),
    )(_ret_log_gamma(), q, k, v, do)


def _loss_head(y, target, bm=512):
    T, D = y.shape
    bm = _pick(T, bm)

    def body(y_ref, t_ref, dy_ref, l_ref):
        err = y_ref[...] - t_ref[...]
        dy_ref[...] = err / D
        part = jnp.full((8, LANES), 0.5 * jnp.sum(jnp.mean(err * err, axis=-1)), F32)

        @pl.when(pl.program_id(0) == 0)
        def _():
            l_ref[...] = part

        @pl.when(pl.program_id(0) > 0)
        def _():
            l_ref[...] += part

    blk = pl.BlockSpec((bm, D), lambda i: (i, 0))
    dy, l = pl.pallas_call(
        body, name="loss_head", grid=(T // bm,),
        in_specs=[blk, blk], out_specs=[blk, pl.BlockSpec((8, LANES), lambda i: (0, 0))],
        out_shape=[jax.ShapeDtypeStruct((T, D), F32), jax.ShapeDtypeStruct((8, LANES), F32)],
        compiler_params=_params(("arbitrary",)),
    )(y, target)
    return dy, l[0, 0]


def _adamw(w, g, m, v, name):
    R, C = w.shape
    br = R if R * C * 4 <= 2 ** 21 else _pick_rows(R, max(8, (2 ** 21) // (C * 4)))

    def body(w_ref, g_ref, m_ref, v_ref, d_ref, mo_ref, vo_ref):
        g_v = g_ref[...]
        m_v = ADAM_B1 * m_ref[...] + (1.0 - ADAM_B1) * g_v
        v_v = ADAM_B2 * v_ref[...] + (1.0 - ADAM_B2) * (g_v * g_v)
        m_hat = m_v / (1.0 - ADAM_B1 ** ADAM_STEP)
        v_hat = v_v / (1.0 - ADAM_B2 ** ADAM_STEP)
        d_ref[...] = -ADAM_LR * (m_hat / (jnp.sqrt(v_hat) + ADAM_EPS) + ADAM_WD * w_ref[...])
        mo_ref[...] = m_v
        vo_ref[...] = v_v

    blk = pl.BlockSpec((br, C), lambda i: (i, 0))
    return pl.pallas_call(
        body, name=name, grid=(R // br,),
        in_specs=[blk] * 4, out_specs=[blk] * 3,
        out_shape=[jax.ShapeDtypeStruct((R, C), F32)] * 3,
        compiler_params=_params(("parallel",)),
    )(w, g, m, v)


def _pick_rows(R, target):
    best = None
    for d in range(8, min(R, target) + 1, 8):
        if R % d == 0:
            best = d
    assert best is not None, (R, target)
    return best


def _position():
    return lax.axis_index("x"), lax.axis_index("y"), lax.axis_index("c")


HBM_SPEC = pl.BlockSpec(memory_space=pltpu.HBM)


def _other_chips(x, y):
    return [(1 - x, y), (x, 1 - y), (1 - x, 1 - y)]


def _all_gather_weights(bigs, small):
    nb = len(bigs)

    def body(*refs):
        big_refs, small_ref = refs[:nb], refs[nb]
        obig, osmall = refs[nb + 1:2 * nb + 1], refs[2 * nb + 1]
        ici_send, ici_recv, d2d_send, d2d_recv, sm_send, sm_recv = refs[2 * nb + 2:]
        x, y, c = _position()
        me = 2 * x + y
        chips = _other_chips(x, y)

        def rows(n, half):
            rh = bigs[n].shape[0] // 2
            return pl.ds(half * rh, rh)

        def over_ici(n, j, slot, from_shard):
            px, py = chips[j]
            dst = obig[n].at[slot, rows(n, c)]
            return pltpu.make_async_remote_copy(
                src_ref=big_refs[n].at[rows(n, c)] if from_shard else dst, dst_ref=dst,
                send_sem=ici_send.at[3 * n + j], recv_sem=ici_recv.at[3 * n + j],
                device_id=(px, py, c), device_id_type=MESH)

        def over_d2d(n, j, half):
            px, py = chips[j]
            part = obig[n].at[2 * px + py, rows(n, half)]
            return pltpu.make_async_remote_copy(
                src_ref=part, dst_ref=part, send_sem=d2d_send.at[3 * n + j], recv_sem=d2d_recv.at[3 * n + j],
                device_id=(x, y, 1 - c), device_id_type=MESH)

        def small_copy(j, slot):
            px, py = chips[j]
            return pltpu.make_async_remote_copy(
                src_ref=small_ref, dst_ref=osmall.at[slot], send_sem=sm_send.at[j], recv_sem=sm_recv.at[j],
                device_id=(px, py, c), device_id_type=MESH)

        sends = [over_ici(n, j, me, True) for n in range(nb) for j in range(3)]
        sends += [small_copy(j, me) for j in range(3)]
        for cp in sends:
            cp.start()
        passed = []
        for n in range(nb):
            for j, (px, py) in enumerate(chips):
                over_ici(n, j, 2 * px + py, False).wait_recv()
                fwd = over_d2d(n, j, c)
                fwd.start()
                passed.append(fwd)
        for n in range(nb):
            for j in range(3):
                over_d2d(n, j, 1 - c).wait_recv()
        for j, (px, py) in enumerate(chips):
            small_copy(j, 2 * px + py).wait_recv()
        for cp in sends + passed:
            cp.wait_send()

    dma = pltpu.SemaphoreType.DMA
    return pl.pallas_call(
        body, name="weights_all_gather",
        in_specs=[HBM_SPEC] * (nb + 1), out_specs=[HBM_SPEC] * (nb + 1),
        out_shape=[jax.ShapeDtypeStruct((N_SHARD,) + b.shape, b.dtype) for b in bigs]
        + [jax.ShapeDtypeStruct((N_SHARD,) + small.shape, small.dtype)],
        scratch_shapes=[dma((3 * nb,)), dma((3 * nb,)), dma((3 * nb,)), dma((3 * nb,)), dma((3,)), dma((3,))],
    )(*bigs, small)


SEM_SPEC = pl.BlockSpec(memory_space=pltpu.SEMAPHORE)
DATAFLOW_EFFECT = pltpu.SideEffectType.DATAFLOW_SIDE_EFFECTING
N_PEERS = N_DEV - 1


def _grad_copies(p_refs, land_refs, send_sems, recv_sems):
    x, y, c = _position()
    copies = []
    for a, (p_ref, land_ref) in enumerate(zip(p_refs, land_refs)):
        rh = p_ref.shape[1] // 2
        for k in range(1, N_DEV):
            px = 1 - x if k & 4 else x
            py = 1 - y if k & 2 else y
            pc = 1 - c if k & 1 else c
            copies.append(pltpu.make_async_remote_copy(
                src_ref=p_ref.at[2 * px + py, pl.ds(pc * rh, rh)], dst_ref=land_ref.at[k - 1],
                send_sem=send_sems.at[N_PEERS * a + k - 1], recv_sem=recv_sems.at[N_PEERS * a + k - 1],
                device_id=(px, py, pc), device_id_type=MESH))
    return copies


def _weight_copies(w_refs, land_refs, send_sems, recv_sems):
    x, y, c = _position()
    copies = []
    for a, (w_ref, land_ref) in enumerate(zip(w_refs, land_refs)):
        for j, (px, py) in enumerate(_other_chips(x, y)):
            copies.append(pltpu.make_async_remote_copy(
                src_ref=w_ref, dst_ref=land_ref.at[2 * x + y], send_sem=send_sems.at[3 * a + j],
                recv_sem=recv_sems.at[3 * a + j], device_id=(px, py, c), device_id_type=MESH))
    return copies


def _exchange_start(make_copies, srcs, lands, n_sems, name, after=None):
    n, m = len(srcs), len(lands)
    n_in = n + m + (after is not None)

    def body(*refs):
        send_sems, recv_sems, token = refs[n_in], refs[n_in + 1], refs[-1]
        for cp in make_copies(refs[:n], refs[n:n + m], send_sems, recv_sems):
            cp.start()
        token[...] = jnp.zeros(token.shape, token.dtype)

    hbm = lambda a: pltpu.with_memory_space_constraint(a, pltpu.HBM)
    dma = pltpu.SemaphoreType.DMA
    res = pl.pallas_call(
        body, name=name,
        in_specs=[HBM_SPEC] * (n + m) + ([] if after is None else [pl.BlockSpec(memory_space=pl.ANY)]),
        out_specs=[SEM_SPEC, SEM_SPEC] + [HBM_SPEC] * (n + m) + [pl.BlockSpec(memory_space=pltpu.VMEM)],
        out_shape=[dma((n_sems,)), dma((n_sems,))] + [pltpu.HBM(a.shape, a.dtype) for a in list(srcs) + list(lands)]
        + [jax.ShapeDtypeStruct((8, LANES), F32)],
        input_output_aliases={i: 2 + i for i in range(n + m)},
        compiler_params=pltpu.CompilerParams(has_side_effects=DATAFLOW_EFFECT),
    )(*[hbm(a) for a in srcs], *[hbm(a) for a in lands], *(() if after is None else (after,)))
    return res[0], res[1], list(res[2:2 + n]), list(res[2 + n:2 + n + m]), res[-1]


def _exchange_wait(make_copies, send_sems, recv_sems, srcs, lands, after, name):
    n, m = len(srcs), len(lands)

    def body(*refs):
        for cp in make_copies(refs[:n], refs[n:n + m], refs[n + m], refs[n + m + 1]):
            cp.wait_send()
            cp.wait_recv()

    res = pl.pallas_call(
        body, name=name,
        in_specs=[HBM_SPEC] * (n + m) + [SEM_SPEC, SEM_SPEC, pl.BlockSpec(memory_space=pl.ANY)],
        out_specs=[HBM_SPEC] * (n + m),
        out_shape=[pltpu.HBM(a.shape, a.dtype) for a in list(srcs) + list(lands)],
        input_output_aliases={i: i for i in range(n + m)},
        compiler_params=pltpu.CompilerParams(has_side_effects=DATAFLOW_EFFECT),
    )(*srcs, *lands, send_sems, recv_sems, after)
    return list(res[:n]), list(res[n:])


def _sum_partials(p, land, name):
    _, rh, cols = land.shape
    br = _pick_rows(rh, 256)
    nrb = rh // br
    x, y, c = _position()
    where = jnp.stack([2 * x + y, c]).astype(jnp.int32)

    def body(where_ref, p_ref, land_ref, o_ref):
        acc = p_ref[...].astype(F32)
        for k in range(N_PEERS):
            acc = acc + land_ref[k].astype(F32)
        o_ref[...] = acc

    return pl.pallas_call(
        body, name=name,
        grid_spec=pltpu.PrefetchScalarGridSpec(
            num_scalar_prefetch=1, grid=(nrb,),
            in_specs=[pl.BlockSpec((None, br, cols), lambda r, where_ref: (where_ref[0], where_ref[1] * nrb + r, 0)),
                      pl.BlockSpec((N_PEERS, br, cols), lambda r, where_ref: (0, r, 0))],
            out_specs=pl.BlockSpec((None, br, cols), lambda r, where_ref: (where_ref[1], r, 0))),
        out_shape=jax.ShapeDtypeStruct((2, rh, cols), F32),
        compiler_params=_params(("parallel",)),
    )(where, p, land)


def _sibling_share(fulls, name):
    n = len(fulls)

    def body(*refs):
        o_refs = refs[n:2 * n]
        send_sems, recv_sems = refs[2 * n:]
        x, y, c = _position()

        def copy(a, half):
            return pltpu.make_async_remote_copy(
                src_ref=o_refs[a].at[half], dst_ref=o_refs[a].at[half], send_sem=send_sems.at[a],
                recv_sem=recv_sems.at[a], device_id=(x, y, 1 - c), device_id_type=MESH)

        sends = [copy(a, c) for a in range(n)]
        for cp in sends:
            cp.start()
        for a in range(n):
            copy(a, 1 - c).wait_recv()
        for cp in sends:
            cp.wait_send()

    dma = pltpu.SemaphoreType.DMA
    return pl.pallas_call(
        body, name=name,
        in_specs=[HBM_SPEC] * n, out_specs=[HBM_SPEC] * n,
        out_shape=[jax.ShapeDtypeStruct(f.shape, f.dtype) for f in fulls],
        input_output_aliases={a: a for a in range(n)},
        scratch_shapes=[dma((n,)), dma((n,))],
    )(*fulls)


def _all_reduce_small(v):
    R, cols = v.shape

    def body(v_ref, o_ref, buf_ref, send_sems, recv_sems):
        x, y, c = _position()
        me = 4 * x + 2 * y + c
        buf_ref[me] = v_ref[...]
        sends = []
        for k in range(1, N_DEV):
            px = 1 - x if k & 4 else x
            py = 1 - y if k & 2 else y
            pc = 1 - c if k & 1 else c
            sends.append(pltpu.make_async_remote_copy(
                src_ref=v_ref, dst_ref=buf_ref.at[me], send_sem=send_sems.at[k - 1], recv_sem=recv_sems.at[k - 1],
                device_id=(px, py, pc), device_id_type=MESH))
        for cp in sends:
            cp.start()
        for k in range(1, N_DEV):
            px = 1 - x if k & 4 else x
            py = 1 - y if k & 2 else y
            pc = 1 - c if k & 1 else c
            pltpu.make_async_remote_copy(
                src_ref=v_ref, dst_ref=buf_ref.at[4 * px + 2 * py + pc], send_sem=send_sems.at[k - 1],
                recv_sem=recv_sems.at[k - 1], device_id=(px, py, pc), device_id_type=MESH).wait_recv()
        for cp in sends:
            cp.wait_send()
        acc = buf_ref[0]
        for d in range(1, N_DEV):
            acc = acc + buf_ref[d]
        o_ref[...] = acc

    return pl.pallas_call(
        body, name="small_grads_all_reduce",
        in_specs=[pl.BlockSpec(memory_space=pltpu.VMEM)], out_specs=pl.BlockSpec(memory_space=pltpu.VMEM),
        out_shape=jax.ShapeDtypeStruct((R, cols), F32),
        scratch_shapes=[pltpu.VMEM((N_DEV, R, cols), F32), pltpu.SemaphoreType.DMA((N_DEV - 1,)),
                        pltpu.SemaphoreType.DMA((N_DEV - 1,))],
    )(v)


def _rope_tables(S, half, width):
    inv_freq = ROPE_THETA ** (-jnp.arange(half, dtype=F32) / half)
    ang = jnp.arange(S).astype(F32)[:, None] * inv_freq[None, :]
    return jnp.cos(ang), jnp.sin(ang)


def _slot_rows(a):
    return a.reshape(N_SHARD, -1, a.shape[-1])


def _local_step(x, target, w, B, S, late, exchange):
    T = B * S
    D = D_MODEL
    bm = 256
    full = lambda a, wd, tile=None: (a, wd, 0, tile or wd)
    g = {}

    cos_r, sin_r = _rope_tables(S, RET_QK // 2, LANES)
    cos_m, sin_m = _rope_tables(S, MLA_ROPE // 2, LANES)
    zeros64 = jnp.zeros((S, 64), F32)
    cos_m = jnp.concatenate([cos_m, cos_m, zeros64], axis=1)
    sin_m = jnp.concatenate([-sin_m, sin_m, zeros64], axis=1)

    def ffn_fwd(xin, i):
        w.update(late(f"ffn{i}", xin))
        norm = w["ffn_norm"][i:i + 1]
        (h,) = _rowwise_fwd(_fn_rms, f"ffn{i}_norm", [full(xin, D)], [], [(norm, D)], [(D, D, BF16)], bm, S)
        ag = _mm(h, w[f"ffn_w_in{i}"], "nn", F32, f"ffn{i}_in", bn=1408)
        u = _conv_fwd(ag, w["ffn_conv8"][i], B, S, f"ffn{i}_conv")
        xout = _mm(u, w[f"ffn_w_out{i}"], "nn", F32, f"ffn{i}_out", residual=xin, bk=1408)
        return xout, (xin, norm, h, ag, u)

    def ffn_bwd(dxout, saved, i):
        xin, norm, h, ag, u = saved
        du = _mm(dxout, w[f"ffn_w_out{i}"], "nt", F32, f"ffn{i}_out_dx", bn=1408)
        g_w_out = _mm(u, dxout, "tn", BF16, f"ffn{i}_out_dw", bm=1408)
        da, dg, dw8 = _conv_bwd(ag, w["ffn_conv8"][i], du, B, S, f"ffn{i}_conv_bwd")
        g_w_in = _mm(h, [da, dg], "tn", BF16, f"ffn{i}_in_dw", bn=1408, out_slots=N_SHARD)
        token = exchange(f"ffn{i}", [g_w_in, _slot_rows(g_w_out)])
        dh = _mm([da, dg], w[f"ffn_w_in{i}"], "nt", F32, f"ffn{i}_in_dx", bk=1408, after=token)
        (dxin,), (g_norm,) = _rowwise_bwd(_fn_rms, f"ffn{i}_norm_bwd", [full(xin, D)], [], [(norm, D)],
                                          [(dh, D)], bm, S, adds={0: dxout})
        return dxin, (g_norm, dw8)

    (h0,) = _rowwise_fwd(_fn_rms, "ret_norm", [full(x, D)], [], [(w["ret_norm"], D)], [(D, D, BF16)], bm, S)
    proj = _mm(h0, w["ret_w_in"], "nn", F32, "ret_in", after=w["started"])
    HQ, HV = RET_HEADS * RET_QK, RET_HEADS * RET_V
    rope_rows = [(proj, 2 * HQ + HV, 0, LANES)]
    q_r, k_r, v_r = _rowwise_fwd(_fn_ret_rope, "ret_rope", rope_rows, [cos_r, sin_r], [],
                                 [(HQ, LANES, BF16), (HQ, LANES, BF16), (HV, LANES, BF16)], bm, S)
    ret_o = _ret_attn_fwd(q_r, k_r, v_r, B, S)
    gate_rows = [full(ret_o, HV, RET_V), (proj, HV, 2, RET_V)]
    (y0,) = _rowwise_fwd(_fn_ret_gate, "ret_gate", gate_rows, [], [(w["ret_gn"], RET_V)], [(HV, RET_V, BF16)], 128, S)
    w.update(late("ret_out", y0))
    x1 = _mm(y0, w["ret_w_out"], "nn", F32, "ret_out", residual=x)
    x2, ffn0_saved = ffn_fwd(x1, 0)

    w.update(late("mla", x2))
    (h2,) = _rowwise_fwd(_fn_rms, "mla_norm", [full(x2, D)], [], [(w["mla_norm"], D)], [(D, D, BF16)], bm, S)
    proj2 = _mm(h2, w["mla_w_in"], "nn", F32, "mla_in")
    lat_consts = [(w["mla_q_norm"], LANES), (w["mla_kv_norm"], LANES)]
    cqn, ckvn, kr = _rowwise_fwd(_fn_mla_lat, "mla_latent_norm", [full(proj2, MLA_IN_PAD, LANES)], [], lat_consts,
                                 [(MLA_Q_RANK, LANES, BF16), (MLA_KV_RANK, LANES, BF16), (LANES, LANES, F32)], bm, S)
    qf = _mm(cqn, w["mla_w_qb"], "nn", F32, "mla_qb")
    kvf = _mm(ckvn, w["mla_w_kvb"], "nn", F32, "mla_kvb")
    HP, HVm = MLA_HEADS * MLA_PAD, MLA_HEADS * MLA_V
    head_rows = [full(qf, HP, LANES), full(kvf, HP, LANES), full(kr, LANES)]
    head_consts = [(w["mla_q_head_norm"], LANES), (w["mla_k_head_norm"], LANES)]
    q_a, k_a, v_a = _rowwise_fwd(_fn_mla_heads, "mla_heads", head_rows, [cos_m, sin_m], head_consts,
                                 [(HP, LANES, BF16), (HP, LANES, BF16), (HVm, LANES, BF16)], bm, S)
    att_o, lse = _mla_attn_fwd(q_a, k_a, v_a, B, S)
    x3 = _mm(att_o, w["mla_w_out"], "nn", F32, "mla_out", residual=x2)
    x4, ffn1_saved = ffn_fwd(x3, 1)

    dy, loss = _loss_head(x4, target)

    dx3, (g_n1, dw8_1) = ffn_bwd(dy, ffn1_saved, 1)

    d_att_o = _mm(dx3, w["mla_w_out"], "nt", F32, "mla_out_dx")
    g_mla_out = _mm(att_o, dx3, "tn", BF16, "mla_out_dw")
    dq_a, dk_a, dv_a = _mla_attn_bwd(q_a, k_a, v_a, att_o, d_att_o, lse, B, S)
    (dqf, dkvf, dkr), (g["mla_q_head_norm"], g["mla_k_head_norm"]) = _rowwise_bwd(
        _fn_mla_heads, "mla_heads_bwd", head_rows, [cos_m, sin_m], head_consts,
        [(dq_a, LANES), (dk_a, LANES), (dv_a, LANES)], 128, S)
    dcqn = _mm(dqf, w["mla_w_qb"], "nt", F32, "mla_qb_dx")
    g_qb = _mm(cqn, dqf, "tn", BF16, "mla_qb_dw")
    g_qb = _to_slots(_unpad_heads(g_qb, 1), 1).reshape(N_SHARD, MLA_Q_RANK, -1)
    dckvn = _mm(dkvf, w["mla_w_kvb"], "nt", F32, "mla_kvb_dx")
    g_kvb = _mm(ckvn, dkvf, "tn", BF16, "mla_kvb_dw", bn=512, out_slots=N_SHARD)
    (dproj2,), (g["mla_q_norm"], g["mla_kv_norm"]) = _rowwise_bwd(
        _fn_mla_lat, "mla_latent_norm_bwd", [full(proj2, MLA_IN_PAD, LANES)], [], lat_consts,
        [(dcqn, LANES), (dckvn, LANES), (dkr, LANES)], bm, S)
    g_mla_in = _mm(h2, dproj2, "tn", BF16, "mla_in_dw")
    token = exchange("mla", [_slot_rows(g_mla_in[:, :MLA_IN]), g_qb, g_kvb, _slot_rows(g_mla_out)])
    dh2 = _mm(dproj2, w["mla_w_in"], "nt", F32, "mla_in_dx", after=token)
    (dx2,), (g["mla_norm"],) = _rowwise_bwd(_fn_rms, "mla_norm_bwd", [full(x2, D)], [], [(w["mla_norm"], D)],
                                            [(dh2, D)], bm, S, adds={0: dx3})

    dx1, (g_n0, dw8_0) = ffn_bwd(dx2, ffn0_saved, 0)

    dy0 = _mm(dx1, w["ret_w_out"], "nt", F32, "ret_out_dx")
    g_ret_out = _mm(y0, dx1, "tn", BF16, "ret_out_dw")
    (d_ret_o, dgate), (g["ret_gn"],) = _rowwise_bwd(_fn_ret_gate, "ret_gate_bwd", gate_rows, [], [(w["ret_gn"], RET_V)],
                                                    [(dy0, RET_V)], 128, S)
    dq_r, dk_r, dv_r = _ret_attn_bwd(q_r, k_r, v_r, d_ret_o, B, S)
    (dqkv,), _ = _rowwise_bwd(_fn_ret_rope, "ret_rope_bwd", rope_rows, [cos_r, sin_r], [],
                              [(dq_r, LANES), (dk_r, LANES), (dv_r, LANES)], bm, S)
    g_ret_in = _mm(h0, [dqkv, dgate], "tn", BF16, "ret_in_dw", bn=512, out_slots=N_SHARD)
    token = exchange("ret", [g_ret_in, _slot_rows(g_ret_out)])
    dh0 = _mm([dqkv, dgate], w["ret_w_in"], "nt", F32, "ret_in_dx", bk=1024, after=token)
    (dx,), (g["ret_norm"],) = _rowwise_bwd(_fn_rms, "ret_norm_bwd", [full(x, D)], [], [(w["ret_norm"], D)],
                                           [(dh0, D)], bm, S, adds={0: dx1})

    g["ffn_norm"] = jnp.concatenate([g_n0, g_n1], axis=0)
    g["ffn_conv_w"] = jnp.stack([dw8_0[0:3], dw8_1[0:3]])
    g["ffn_conv_b"] = jnp.stack([dw8_0[3], dw8_1[3]])
    return loss, dx, g


_BIG = [("ret_w_in", 2), ("ret_w_out", 1), ("mla_w_in", 1), ("mla_w_qb", 2), ("mla_w_kvb", 2), ("mla_w_out", 1),
        ("ffn_w_in", 2), ("ffn_w_out", 1)]
_SMALL_SHARDED = [("ret_gn", 2), ("mla_norm", 1), ("mla_q_norm", 1), ("mla_kv_norm", 1), ("ffn_conv_w", 2)]
_SMALL_REPLICATED = ["ret_norm", "mla_q_head_norm", "mla_k_head_norm", "ffn_norm", "ffn_conv_b"]
_SMALL_ALL = ["ret_norm", "ret_gn", "mla_norm", "mla_q_norm", "mla_kv_norm", "mla_q_head_norm", "mla_k_head_norm",
              "ffn_norm", "ffn_conv_w", "ffn_conv_b"]


def _to_slots(full, axis):
    shape = full.shape
    split = shape[:axis] + (N_SHARD, shape[axis] // N_SHARD) + shape[axis + 1:]
    return jnp.moveaxis(full.reshape(split), axis, 0).reshape(N_SHARD, -1)


def _from_slots(slots, shard_shape, axis):
    parts = jnp.moveaxis(slots.reshape((N_SHARD,) + tuple(shard_shape)), 0, axis)
    full = shard_shape[:axis] + (N_SHARD * shard_shape[axis],) + shard_shape[axis + 1:]
    return parts.reshape(full)


def _pad_rows(flat, cols, row_unit):
    n, L = flat.shape
    unit = cols * row_unit
    Lp = -(-L // unit) * unit
    if Lp != L:
        flat = jnp.concatenate([flat, jnp.zeros((n, Lp - L), flat.dtype)], axis=1)
    return flat.reshape(n, Lp // cols, cols)


def _pad_heads(a, axis):
    shape = a.shape
    a = a.reshape(shape[:axis] + (MLA_HEADS, MLA_QK) + shape[axis + 1:])
    pad = [(0, 0)] * a.ndim
    pad[axis + 1] = (0, MLA_PAD - MLA_QK)
    return jnp.pad(a, pad).reshape(shape[:axis] + (MLA_HEADS * MLA_PAD,) + shape[axis + 1:])


def _unpad_heads(a, axis):
    shape = a.shape
    a = a.reshape(shape[:axis] + (MLA_HEADS, MLA_PAD) + shape[axis + 1:])
    a = lax.slice_in_dim(a, 0, MLA_QK, axis=axis + 1)
    return a.reshape(shape[:axis] + (MLA_HEADS * MLA_QK,) + shape[axis + 1:])


def kernel(x, ret_norm, ret_w_in, ret_gn, ret_w_out, mla_norm, mla_w_in, mla_q_norm, mla_w_qb, mla_kv_norm, mla_w_kvb, mla_q_head_norm, mla_k_head_norm, mla_w_out, ffn_norm, ffn_w_in, ffn_conv_w, ffn_conv_b, ffn_w_out, loss_target, m_ret_norm, m_ret_w_in, m_ret_gn, m_ret_w_out, m_mla_norm, m_mla_w_in, m_mla_q_norm, m_mla_w_qb, m_mla_kv_norm, m_mla_w_kvb, m_mla_q_head_norm, m_mla_k_head_norm, m_mla_w_out, m_ffn_norm, m_ffn_w_in, m_ffn_conv_w, m_ffn_conv_b, m_ffn_w_out, v_ret_norm, v_ret_w_in, v_ret_gn, v_ret_w_out, v_mla_norm, v_mla_w_in, v_mla_q_norm, v_mla_w_qb, v_mla_kv_norm, v_mla_w_kvb, v_mla_q_head_norm, v_mla_k_head_norm, v_mla_w_out, v_ffn_norm, v_ffn_w_in, v_ffn_conv_w, v_ffn_conv_b, v_ffn_w_out):
    names = ["ret_norm", "ret_w_in", "ret_gn", "ret_w_out", "mla_norm", "mla_w_in", "mla_q_norm", "mla_w_qb",
             "mla_kv_norm", "mla_w_kvb", "mla_q_head_norm", "mla_k_head_norm", "mla_w_out", "ffn_norm", "ffn_w_in",
             "ffn_conv_w", "ffn_conv_b", "ffn_w_out"]
    shard = dict(zip(names, [ret_norm, ret_w_in, ret_gn, ret_w_out, mla_norm, mla_w_in, mla_q_norm, mla_w_qb,
                             mla_kv_norm, mla_w_kvb, mla_q_head_norm, mla_k_head_norm, mla_w_out, ffn_norm, ffn_w_in,
                             ffn_conv_w, ffn_conv_b, ffn_w_out]))
    mom_m = dict(zip(names, [m_ret_norm, m_ret_w_in, m_ret_gn, m_ret_w_out, m_mla_norm, m_mla_w_in, m_mla_q_norm,
                             m_mla_w_qb, m_mla_kv_norm, m_mla_w_kvb, m_mla_q_head_norm, m_mla_k_head_norm, m_mla_w_out,
                             m_ffn_norm, m_ffn_w_in, m_ffn_conv_w, m_ffn_conv_b, m_ffn_w_out]))
    mom_v = dict(zip(names, [v_ret_norm, v_ret_w_in, v_ret_gn, v_ret_w_out, v_mla_norm, v_mla_w_in, v_mla_q_norm,
                             v_mla_w_qb, v_mla_kv_norm, v_mla_w_kvb, v_mla_q_head_norm, v_mla_k_head_norm, v_mla_w_out,
                             v_ffn_norm, v_ffn_w_in, v_ffn_conv_w, v_ffn_conv_b, v_ffn_w_out]))
    B, S, D = x.shape
    T = B * S
    sx, sy = lax.axis_index("x"), lax.axis_index("y")
    me = 2 * sx + sy

    two_d = lambda a: a.reshape(-1, a.shape[-1])
    small_sizes = [int(np.prod(shard[n].shape)) for n, _ in _SMALL_SHARDED]
    small = jnp.concatenate([shard[n].reshape(1, -1) for n, _ in _SMALL_SHARDED], axis=1)
    small = _pad_rows(small, LANES, 8)[0]
    as_mxu = lambda a: two_d(a).astype(BF16)
    is_me = lax.broadcasted_iota(jnp.int32, (N_SHARD, 1, 1), 0) == me
    with_own = lambda gathered, own: jnp.where(is_me, own[None], gathered)
    by_cols = lambda a: jnp.moveaxis(a, 0, 1).reshape(a.shape[1], -1)
    by_rows = lambda a: a.reshape(-1, a.shape[-1])
    pad_in = lambda a: jnp.pad(by_rows(a), ((0, 0), (0, MLA_IN_PAD - MLA_IN)))
    pad_qb = lambda a: _pad_heads(by_cols(a), 1)
    ret_in_shard = as_mxu(shard["ret_w_in"])
    g_ret_in, gsmall = _all_gather_weights([ret_in_shard], small)
    later = [
        ("ret_out", [("ret_w_out", as_mxu(shard["ret_w_out"]), by_rows)]),
        ("ffn0", [("ffn_w_in0", as_mxu(shard["ffn_w_in"][0]), by_cols), ("ffn_w_out0", as_mxu(shard["ffn_w_out"][0]), by_rows)]),
        ("mla", [("mla_w_in", as_mxu(shard["mla_w_in"]), pad_in), ("mla_w_qb", as_mxu(shard["mla_w_qb"]), pad_qb),
                 ("mla_w_kvb", as_mxu(shard["mla_w_kvb"]), by_cols), ("mla_w_out", as_mxu(shard["mla_w_out"]), by_rows)]),
        ("ffn1", [("ffn_w_in1", as_mxu(shard["ffn_w_in"][1]), by_cols), ("ffn_w_out1", as_mxu(shard["ffn_w_out"][1]), by_rows)]),
    ]
    gathering = {}
    token = gsmall
    for group, items in later:
        shards = [s_ for _, s_, _ in items]
        lands = [lax.empty((N_SHARD,) + s_.shape, s_.dtype) for s_ in shards]
        send_sems, recv_sems, shards, lands, token = _exchange_start(
            _weight_copies, shards, lands, 3 * len(shards), f"weights_start_{group}", after=token)
        gathering[group] = (send_sems, recv_sems, shards, lands, items)

    def late(group, after):
        send_sems, recv_sems, shards, lands, items = gathering[group]
        shards, lands = _exchange_wait(_weight_copies, send_sems, recv_sems, shards, lands, after,
                                       f"weights_wait_{group}")
        return {key: full(with_own(l_, s_)) for (key, _, full), s_, l_ in zip(items, shards, lands)}

    gsmall = with_own(gsmall, small).reshape(N_SHARD, -1)
    wfull = {}
    off = 0
    for (n, ax), sz in zip(_SMALL_SHARDED, small_sizes):
        wfull[n] = _from_slots(gsmall[:, off:off + sz], shard[n].shape, ax)
        off += sz
    for n in _SMALL_REPLICATED:
        wfull[n] = shard[n]

    conv8 = jnp.concatenate([wfull["ffn_conv_w"], wfull["ffn_conv_b"][:, None, :],
                             jnp.zeros((2, 4, FFN_DIM), F32)], axis=1)
    w = {
        "started": token, "ret_norm": wfull["ret_norm"], "ret_w_in": by_cols(with_own(g_ret_in, ret_in_shard)),
        "ret_gn": wfull["ret_gn"].reshape(1, RET_HEADS * RET_V), "mla_norm": wfull["mla_norm"],
        "mla_q_norm": wfull["mla_q_norm"], "mla_kv_norm": wfull["mla_kv_norm"],
        "mla_q_head_norm": jnp.pad(wfull["mla_q_head_norm"], ((0, 0), (0, MLA_PAD - MLA_QK))),
        "mla_k_head_norm": jnp.pad(wfull["mla_k_head_norm"], ((0, 0), (0, MLA_PAD - MLA_QK))),
        "ffn_norm": wfull["ffn_norm"], "ffn_conv8": conv8,
    }

    started = {}

    def exchange(group, arrays):
        lands = [lax.empty((N_PEERS, p.shape[1] // 2, p.shape[2]), p.dtype) for p in arrays]
        send_sems, recv_sems, ps, lands, token = _exchange_start(
            _grad_copies, arrays, lands, N_PEERS * len(arrays), f"grads_start_{group}")
        started[group] = (send_sems, recv_sems, ps, lands)
        return token

    loss_part, dx, gl = _local_step(x.reshape(T, D), loss_target.reshape(T, D), w, B, S, late, exchange)
    loss = lax.psum(loss_part, ("x", "y", "c"))
    gfull = {
        "ret_norm": gl["ret_norm"], "ret_gn": gl["ret_gn"].reshape(1, RET_HEADS, RET_V),
        "mla_norm": gl["mla_norm"], "mla_q_norm": gl["mla_q_norm"], "mla_kv_norm": gl["mla_kv_norm"],
        "mla_q_head_norm": gl["mla_q_head_norm"][:, :MLA_QK], "mla_k_head_norm": gl["mla_k_head_norm"][:, :MLA_QK],
        "ffn_norm": gl["ffn_norm"], "ffn_conv_w": gl["ffn_conv_w"], "ffn_conv_b": gl["ffn_conv_b"],
    }

    red = {}
    after = dx
    for group in ("ffn1", "mla", "ffn0", "ret"):
        send_sems, recv_sems, ps, lands = started[group]
        ps, lands = _exchange_wait(_grad_copies, send_sems, recv_sems, ps, lands, after, f"grads_wait_{group}")
        halves = [_sum_partials(p_, l_, f"grads_sum_{group}_{i}") for i, (p_, l_) in enumerate(zip(ps, lands))]
        red[group] = [two_d(r) for r in _sibling_share(halves, f"grads_share_{group}")]
        after = red[group][0]
    grads = {"ret_w_in": red["ret"][0], "ret_w_out": red["ret"][1], "mla_w_in": red["mla"][0],
             "mla_w_qb": red["mla"][1], "mla_w_kvb": red["mla"][2], "mla_w_out": red["mla"][3]}
    grads = {n: a.reshape(shard[n].shape) for n, a in grads.items()}
    grads["ffn_w_in"] = jnp.stack([red["ffn0"][0], red["ffn1"][0]])
    grads["ffn_w_out"] = jnp.stack([red["ffn0"][1], red["ffn1"][1]])

    small_sizes_all = [int(np.prod(gfull[n].shape)) for n in _SMALL_ALL]
    gsm = jnp.concatenate([gfull[n].reshape(1, -1) for n in _SMALL_ALL], axis=1)
    gsm = _all_reduce_small(_pad_rows(gsm, LANES, 8)[0]).reshape(-1)

    sharded_axis = dict(_SMALL_SHARDED)
    off = 0
    for n, sz in zip(_SMALL_ALL, small_sizes_all):
        gn = gsm[off:off + sz].reshape(gfull[n].shape)
        off += sz
        if n in sharded_axis:
            ax = sharded_axis[n]
            width = shard[n].shape[ax]
            gn = lax.dynamic_slice_in_dim(gn, me * width, width, axis=ax)
        grads[n] = gn

    delta, new_m, new_v = {}, {}, {}
    for n, _ in _BIG:
        shp = shard[n].shape
        two_d = lambda a: a.reshape(-1, shp[-1])
        d_, m_, v_ = _adamw(two_d(shard[n]), two_d(grads[n]), two_d(mom_m[n]), two_d(mom_v[n]), f"adamw_{n}")
        delta[n], new_m[n], new_v[n] = d_.reshape(shp), m_.reshape(shp), v_.reshape(shp)
    pack_small = lambda d: _pad_rows(jnp.concatenate([d[n].reshape(1, -1) for n in _SMALL_ALL], axis=1), LANES, 8)[0]
    d_, m_, v_ = _adamw(pack_small(shard), pack_small(grads), pack_small(mom_m), pack_small(mom_v), "adamw_small")
    off = 0
    for n in _SMALL_ALL:
        sz = int(np.prod(shard[n].shape))
        for dst, src in ((delta, d_), (new_m, m_), (new_v, v_)):
            dst[n] = src.reshape(-1)[off:off + sz].reshape(shard[n].shape)
        off += sz

    return (loss, dx.reshape(B, S, D), *[grads[n] for n in names], *[delta[n] for n in names],
            *[new_m[n] for n in names], *[new_v[n] for n in names])
```

```python
import functools
import math

import numpy as np
import jax
import jax.numpy as jnp
from jax import lax
from jax.experimental import pallas as pl
from jax.experimental.pallas import tpu as pltpu

F32 = jnp.float32
BF16 = jnp.bfloat16
MXU_DTYPE = jnp.bfloat16

CHUNK = 64
RMS_EPS = 1e-6
ROPE_THETA = 10000.0
D_MODEL = 1024
RET_HEADS = 4
RET_QK = 256
RET_V = 512
RET_GAMMA_BASE = -5.0
MLA_HEADS = 8
MLA_Q_RANK = 384
MLA_KV_RANK = 256
MLA_NOPE = 128
MLA_ROPE = 64
MLA_V = 128
MLA_QK = MLA_NOPE + MLA_ROPE
MLA_PAD = 256
MLA_IN = MLA_Q_RANK + MLA_KV_RANK + MLA_ROPE
MLA_IN_PAD = MLA_IN + 64
MASK_VALUE = -1e30
FFN_DIM = 2816
ADAM_LR = 0.001
ADAM_B1 = 0.9
ADAM_B2 = 0.999
ADAM_EPS = 1e-08
ADAM_WD = 0.01
ADAM_STEP = 10

LANES = 128
ATT_BLOCK = 256
VMEM_LIMIT = 56 * 2 ** 20
N_SHARD = 4
N_DEV = 8

MESH = pl.DeviceIdType.MESH


def _params(sem=None, **kw):
    return pltpu.CompilerParams(dimension_semantics=sem, vmem_limit_bytes=VMEM_LIMIT, **kw)


def _pick(dim, target):
    if dim <= target:
        return dim
    best = None
    for d in range(LANES, target + 1, LANES):
        if dim % d == 0:
            best = d
    assert best is not None, (dim, target)
    return best


def _mm(a, b, dims, out_dtype, name, residual=None, bm=512, bn=1024, bk=2048, out_slots=None, after=None):
    a_parts = list(a) if isinstance(a, (list, tuple)) else [a]
    b_parts = list(b) if isinstance(b, (list, tuple)) else [b]
    parts_on_n = dims == "tn" or len(b_parts) > 1
    if parts_on_n:
        assert len(a_parts) == 1 and dims in ("tn", "nn")
        (K, M) = a_parts[0].shape if dims == "tn" else a_parts[0].shape[::-1]
        N = sum(p.shape[1] for p in b_parts)
        part_widths = [p.shape[1] for p in b_parts]
    else:
        assert len(b_parts) == 1
        M = a_parts[0].shape[0]
        K = sum(p.shape[1] for p in a_parts)
        N = b_parts[0].shape[1 if dims == "nn" else 0]
        part_widths = [p.shape[1] for p in a_parts]
    bm, bn, bk = _pick(M, bm), _pick(N, bn), _pick(K, min(bk, 1024) if dims == "tn" else bk)
    nk = K // bk
    unit = bn if parts_on_n else bk
    assert all(wd % unit == 0 for wd in part_widths), (name, part_widths, unit)
    bounds = np.cumsum([0] + [wd // unit for wd in part_widths])
    ranges = [(int(lo), int(hi)) for lo, hi in zip(bounds[:-1], bounds[1:])]

    def part_index(idx, lo, hi):
        return jnp.clip(idx - lo, 0, hi - lo - 1)

    if parts_on_n:
        if dims == "tn":
            a_specs = [pl.BlockSpec((bk, bm), lambda i, j, k: (k, i))]
            dn = (((0,), (0,)), ((), ()))
        else:
            a_specs = [pl.BlockSpec((bm, bk), lambda i, j, k: (i, k))]
            dn = (((1,), (0,)), ((), ()))
        b_specs = [pl.BlockSpec((bk, bn), functools.partial(lambda i, j, k, lo, hi: (k, part_index(j, lo, hi)), lo=lo, hi=hi))
                   for lo, hi in ranges]
    else:
        a_specs = [pl.BlockSpec((bm, bk), functools.partial(lambda i, j, k, lo, hi: (i, part_index(k, lo, hi)), lo=lo, hi=hi))
                   for lo, hi in ranges]
        if dims == "nt":
            b_specs = [pl.BlockSpec((bn, bk), lambda i, j, k: (j, k))]
        else:
            b_specs = [pl.BlockSpec((bk, bn), lambda i, j, k: (k, j))]
        dn = (((1,), (1 if dims == "nt" else 0,)), ((), ()))
    r_spec = pl.BlockSpec((bm, bn), lambda i, j, k: (i, j))
    if out_slots is None:
        o_spec, o_shape = r_spec, (M, N)
    else:
        ns = N // out_slots
        assert ns % bn == 0, (name, ns, bn)
        nbs = ns // bn
        o_spec = pl.BlockSpec((None, bm, bn), lambda i, j, k: (j // nbs, i, j % nbs))
        o_shape = (out_slots, M, ns)
    has_res = residual is not None
    na, nb = len(a_parts), len(b_parts)

    def body(*refs):
        a_refs, b_refs = refs[:na], refs[na:na + nb]
        r_ref = refs[na + nb] if has_res else None
        n_in = na + nb + has_res + (after is not None)
        o_ref = refs[n_in]
        acc_ref = refs[n_in + 1] if nk > 1 else None
        k = pl.program_id(2)

        def finish(acc):
            if has_res:
                acc = acc + r_ref[...].astype(F32)
            o_ref[...] = acc.astype(out_dtype)

        def compute(a_ref, b_ref):
            p = lax.dot_general(a_ref[...].astype(MXU_DTYPE), b_ref[...].astype(MXU_DTYPE), dn,
                                preferred_element_type=F32)
            if nk == 1:
                finish(p)
                return

            @pl.when(k == 0)
            def _():
                acc_ref[...] = p

            @pl.when(jnp.logical_and(k > 0, k < nk - 1))
            def _():
                acc_ref[...] += p

            @pl.when(k == nk - 1)
            def _():
                finish(acc_ref[...] + p)

        if len(ranges) == 1:
            compute(a_refs[0], b_refs[0])
        else:
            idx = pl.program_id(1) if parts_on_n else k
            for p, (lo, hi) in enumerate(ranges):
                @pl.when(jnp.logical_and(idx >= lo, idx < hi))
                def _(p=p):
                    compute(a_refs[0 if parts_on_n else p], b_refs[p if parts_on_n else 0])

    after_specs = [] if after is None else [pl.BlockSpec(after.shape, lambda i, j, k: (0, 0))]
    return pl.pallas_call(
        body, name=name, grid=(M // bm, N // bn, nk),
        in_specs=a_specs + b_specs + ([r_spec] if has_res else []) + after_specs, out_specs=o_spec,
        out_shape=jax.ShapeDtypeStruct(o_shape, out_dtype),
        scratch_shapes=[pltpu.VMEM((bm, bn), F32)] if nk > 1 else [],
        compiler_params=_params(("parallel", "parallel", "arbitrary")),
    )(*a_parts, *b_parts, *((residual,) if has_res else ()), *(() if after is None else (after,)))


def _tiles(ref, width, tile):
    return [ref[:, t * tile:(t + 1) * tile].astype(F32) for t in range(width // tile)]


def _row_specs(rows, pos, consts, bm, S):
    npos_blocks = S // bm
    specs = [pl.BlockSpec((bm, w), functools.partial(lambda i, c: (i, c), c=cb)) for (_, w, cb, _) in rows]
    specs += [pl.BlockSpec((bm, p.shape[1]), lambda i: (i % npos_blocks, 0)) for p in pos]
    specs += [pl.BlockSpec(c.shape, lambda i: (0, 0)) for (c, _) in consts]
    return specs


def _rowwise_fwd(fn, name, rows, pos, consts, outs, bm, S, transposed=()):
    T = rows[0][0].shape[0]
    nr, npos, nc, no = len(rows), len(pos), len(consts), len(outs)

    def body(*refs):
        row_v = [_tiles(r, w, t) for r, (_, w, _, t) in zip(refs[:nr], rows)]
        pos_v = [r[...] for r in refs[nr:nr + npos]]
        const_v = [_tiles(r, c.shape[1], t) for r, (c, t) in zip(refs[nr + npos:nr + npos + nc], consts)]
        res = fn(row_v, pos_v, const_v)
        out_refs = refs[nr + npos + nc:]
        for o_ref, tiles, (w, t, dt) in zip(out_refs, res, outs):
            for k, v in enumerate(tiles):
                o_ref[:, k * t:(k + 1) * t] = v.astype(dt)
        for t_ref, a in zip(out_refs[no:], transposed):
            t = outs[a][1]
            for k, v in enumerate(res[a]):
                t_ref[k * t:(k + 1) * t, :] = v.T.astype(t_ref.dtype)

    return pl.pallas_call(
        body, name=name, grid=(T // bm,),
        in_specs=_row_specs(rows, pos, consts, bm, S),
        out_specs=[pl.BlockSpec((bm, w), lambda i: (i, 0)) for (w, _, _) in outs]
        + [pl.BlockSpec((outs[a][0], bm), lambda i: (0, i)) for a in transposed],
        out_shape=[jax.ShapeDtypeStruct((T, w), dt) for (w, _, dt) in outs]
        + [jax.ShapeDtypeStruct((outs[a][0], T), BF16) for a in transposed],
        compiler_params=_params(("parallel",)),
    )(*[r[0] for r in rows], *pos, *[c[0] for c in consts])


def _rowwise_bwd(fn, name, rows, pos, consts, cts, bm, S, adds=None, grad_dtypes=None, mxu_copies=()):
    adds = adds or {}
    T = rows[0][0].shape[0]
    nr, npos, nc, nct = len(rows), len(pos), len(consts), len(cts)
    add_idx = sorted(adds)
    grad_dtypes = grad_dtypes or [F32] * nr

    def body(*refs):
        it = iter(refs)
        row_refs = [next(it) for _ in range(nr)]
        pos_refs = [next(it) for _ in range(npos)]
        const_refs = [next(it) for _ in range(nc)]
        ct_refs = [next(it) for _ in range(nct)]
        add_refs = {k: next(it) for k in add_idx}
        drow_refs = [next(it) for _ in range(nr)]
        copy_refs = {a: next(it) for a in mxu_copies}
        dconst_refs = [next(it) for _ in range(nc)]
        row_v = [_tiles(r, w, t) for r, (_, w, _, t) in zip(row_refs, rows)]
        pos_v = [r[...] for r in pos_refs]
        const_v = [_tiles(r, c.shape[1], t) for r, (c, t) in zip(const_refs, consts)]
        ct_v = [_tiles(r, c.shape[1], t) for r, (c, t) in zip(ct_refs, cts)]
        _, vjp = jax.vjp(lambda rv, cv: fn(rv, pos_v, cv), row_v, const_v)
        drows, dconsts = vjp(ct_v)
        for a, (d_ref, tiles, (_, w, _, t)) in enumerate(zip(drow_refs, drows, rows)):
            for k, v in enumerate(tiles):
                if a in add_refs:
                    v = v + add_refs[a][:, k * t:(k + 1) * t].astype(F32)
                d_ref[:, k * t:(k + 1) * t] = v.astype(d_ref.dtype)
                if a in copy_refs:
                    copy_refs[a][:, k * t:(k + 1) * t] = v.astype(BF16)
        first = pl.program_id(0) == 0
        for d_ref, tiles, (_, t) in zip(dconst_refs, dconsts, consts):
            for k, v in enumerate(tiles):
                @pl.when(first)
                def _(d_ref=d_ref, k=k, t=t, v=v):
                    d_ref[:, k * t:(k + 1) * t] = v

                @pl.when(jnp.logical_not(first))
                def _(d_ref=d_ref, k=k, t=t, v=v):
                    d_ref[:, k * t:(k + 1) * t] += v

    in_specs = _row_specs(rows, pos, consts, bm, S)
    in_specs += [pl.BlockSpec((bm, c.shape[1]), lambda i: (i, 0)) for (c, _) in cts]
    in_specs += [pl.BlockSpec((bm, adds[k].shape[1]), lambda i: (i, 0)) for k in add_idx]
    out_specs = [pl.BlockSpec((bm, w), lambda i: (i, 0)) for (_, w, _, _) in rows]
    out_specs += [pl.BlockSpec((bm, rows[a][1]), lambda i: (i, 0)) for a in mxu_copies]
    out_specs += [pl.BlockSpec(c.shape, lambda i: (0, 0)) for (c, _) in consts]
    out_shape = [jax.ShapeDtypeStruct((T, w), dt) for (_, w, _, _), dt in zip(rows, grad_dtypes)]
    out_shape += [jax.ShapeDtypeStruct((T, rows[a][1]), BF16) for a in mxu_copies]
    out_shape += [jax.ShapeDtypeStruct(c.shape, F32) for (c, _) in consts]
    res = pl.pallas_call(
        body, name=name, grid=(T // bm,),
        in_specs=in_specs, out_specs=out_specs, out_shape=out_shape,
        compiler_params=_params(("arbitrary",)),
    )(*[r[0] for r in rows], *pos, *[c[0] for c in consts], *[c[0] for c in cts], *[adds[k] for k in add_idx])
    n_rows = nr + len(mxu_copies)
    return res[:n_rows], res[n_rows:]


def _ssq(tiles):
    s = jnp.sum(tiles[0] * tiles[0], axis=-1, keepdims=True)
    for t in tiles[1:]:
        s = s + jnp.sum(t * t, axis=-1, keepdims=True)
    return s


def _sigmoid(x):
    return 1.0 / (1.0 + jnp.exp(-x))


def _fn_rms(rows, pos, consts):
    (x,), (g,) = rows[0], consts[0]
    r = lax.rsqrt(jnp.mean(x * x, axis=-1, keepdims=True) + RMS_EPS)
    return [[x * r * g]]


def _fn_ret_rope(rows, pos, consts):
    (qkv,) = rows
    nq = RET_HEADS * RET_QK // LANES
    q, k, v = qkv[:nq], qkv[nq:2 * nq], qkv[2 * nq:]
    cos, sin = pos

    def rot(t, scale):
        out = []
        for h in range(RET_HEADS):
            x1, x2 = t[2 * h], t[2 * h + 1]
            o1, o2 = x1 * cos - x2 * sin, x2 * cos + x1 * sin
            out += [o1, o2] if scale is None else [o1 * scale, o2 * scale]
        return out

    return [rot(q, None), rot(k, RET_QK ** -0.5), list(v)]


def _fn_ret_gate(rows, pos, consts):
    o, g = rows
    (gn,) = consts
    out = []
    for h in range(RET_HEADS):
        r = lax.rsqrt(jnp.mean(o[h] * o[h], axis=-1, keepdims=True) + RMS_EPS)
        out.append((o[h] * r * gn[h]) * (g[h] * _sigmoid(g[h])))
    return [out]


def _fn_mla_lat(rows, pos, consts):
    (p,) = rows
    gq, gkv = consts
    nq, nkv = MLA_Q_RANK // LANES, MLA_KV_RANK // LANES
    cq, ckv, kr = p[:nq], p[nq:nq + nkv], p[nq + nkv]
    rq = lax.rsqrt(_ssq(cq) / MLA_Q_RANK + RMS_EPS)
    rkv = lax.rsqrt(_ssq(ckv) / MLA_KV_RANK + RMS_EPS)
    return [[t * rq * g for t, g in zip(cq, gq)], [t * rkv * g for t, g in zip(ckv, gkv)], [kr]]


def _swap32_impl(x):
    lane = lax.broadcasted_iota(jnp.int32, x.shape, 1)
    up, down = pltpu.roll(x, LANES - 32, 1), pltpu.roll(x, 32, 1)
    return jnp.where(lane < 32, up, jnp.where(lane < 64, down, 0.0))


@jax.custom_vjp
def _swap32(x):
    return _swap32_impl(x)


_swap32.defvjp(lambda x: (_swap32_impl(x), None), lambda _, g: (_swap32_impl(g),))


def _fn_mla_heads(rows, pos, consts):
    qf, kvf, (kr,) = rows
    cos, sin = pos
    gq, gk = consts
    q_out, k_out, v_out = [], [], []
    for h in range(MLA_HEADS):
        q0, q1 = qf[2 * h], qf[2 * h + 1]
        r = lax.rsqrt(_ssq([q0, q1]) / MLA_QK + RMS_EPS)
        a0, a1 = q0 * r * gq[0], q1 * r * gq[1]
        a1 = a1 * cos + _swap32(a1) * sin
        q_out += [a0 * (MLA_QK ** -0.5), a1 * (MLA_QK ** -0.5)]
        k0 = kvf[2 * h]
        r = lax.rsqrt(_ssq([k0, kr]) / MLA_QK + RMS_EPS)
        b0, b1 = k0 * r * gk[0], kr * r * gk[1]
        k_out += [b0, b1 * cos + _swap32(b1) * sin]
        v_out.append(kvf[2 * h + 1])
    return [q_out, k_out, v_out]


def _shift_down(x, n):
    row = lax.broadcasted_iota(jnp.int32, x.shape, 0)
    return jnp.where(row >= n, pltpu.roll(x, n, 0), 0.0)


def _shift_up(x, n):
    rows = x.shape[0]
    row = lax.broadcasted_iota(jnp.int32, x.shape, 0)
    return jnp.where(row < rows - n, pltpu.roll(x, rows - n, 0), 0.0)


def _conv_blocks(S):
    cb = 256
    return cb, FFN_DIM // cb


def _conv_fwd(ag, w8, B, S, name):
    cb, ncb = _conv_blocks(S)

    def body(a_ref, g_ref, w_ref, u_ref, ut_ref):
        g = g_ref[...]
        w = w_ref[...]
        gc = w[0:1] * _shift_down(g, 2) + w[1:2] * _shift_down(g, 1) + w[2:3] * g + w[3:4]
        u = a_ref[...] * (gc * _sigmoid(gc))
        u_ref[...] = u.astype(u_ref.dtype)
        ut_ref[...] = u.T.astype(ut_ref.dtype)

    return pl.pallas_call(
        body, name=name, grid=(ncb, B),
        in_specs=[pl.BlockSpec((S, cb), lambda j, b: (b, j)),
                  pl.BlockSpec((S, cb), lambda j, b: (b, ncb + j)),
                  pl.BlockSpec((8, cb), lambda j, b: (0, j))],
        out_specs=[pl.BlockSpec((S, cb), lambda j, b: (b, j)), pl.BlockSpec((cb, S), lambda j, b: (j, b))],
        out_shape=[jax.ShapeDtypeStruct((B * S, FFN_DIM), BF16), jax.ShapeDtypeStruct((FFN_DIM, B * S), BF16)],
        compiler_params=_params(("parallel", "parallel")),
    )(ag, ag, w8)


def _conv_bwd(ag, w8, du, B, S, name):
    cb, ncb = _conv_blocks(S)

    def body(a_ref, g_ref, w_ref, du_ref, da_ref, dg_ref, dw_ref):
        g = g_ref[...]
        w = w_ref[...]
        g1, g2 = _shift_down(g, 1), _shift_down(g, 2)
        gc = w[0:1] * g2 + w[1:2] * g1 + w[2:3] * g + w[3:4]
        sg = _sigmoid(gc)
        du_v = du_ref[...]
        da_ref[...] = (du_v * (gc * sg)).astype(da_ref.dtype)
        dgc = du_v * a_ref[...] * (sg * (1.0 + gc * (1.0 - sg)))
        dg = w[2:3] * dgc + w[1:2] * _shift_up(dgc, 1) + w[0:1] * _shift_up(dgc, 2)
        dg_ref[...] = dg.astype(dg_ref.dtype)
        part = jnp.concatenate([
            jnp.sum(dgc * g2, axis=0, keepdims=True), jnp.sum(dgc * g1, axis=0, keepdims=True),
            jnp.sum(dgc * g, axis=0, keepdims=True), jnp.sum(dgc, axis=0, keepdims=True),
            jnp.zeros((4, cb), F32)], axis=0)

        @pl.when(pl.program_id(1) == 0)
        def _():
            dw_ref[...] = part

        @pl.when(pl.program_id(1) > 0)
        def _():
            dw_ref[...] += part

    blk = lambda j, b: (b, j)
    return pl.pallas_call(
        body, name=name, grid=(ncb, B),
        in_specs=[pl.BlockSpec((S, cb), blk),
                  pl.BlockSpec((S, cb), lambda j, b: (b, ncb + j)),
                  pl.BlockSpec((8, cb), lambda j, b: (0, j)),
                  pl.BlockSpec((S, cb), blk)],
        out_specs=[pl.BlockSpec((S, cb), blk), pl.BlockSpec((S, cb), blk),
                   pl.BlockSpec((8, cb), lambda j, b: (0, j))],
        out_shape=[jax.ShapeDtypeStruct((B * S, FFN_DIM), BF16), jax.ShapeDtypeStruct((B * S, FFN_DIM), BF16),
                   jax.ShapeDtypeStruct((8, FFN_DIM), F32)],
        compiler_params=_params(("parallel", "arbitrary")),
    )(ag, ag, w8, du)


_NT = (((1,), (1,)), ((), ()))
_NN = (((1,), (0,)), ((), ()))
_TN = (((0,), (0,)), ((), ()))


def _dot(a, b, dn):
    return lax.dot_general(a.astype(MXU_DTYPE), b.astype(MXU_DTYPE), dn, preferred_element_type=F32)


def _rel_and_mask():
    il = lax.broadcasted_iota(jnp.int32, (ATT_BLOCK, ATT_BLOCK), 0)
    jl = lax.broadcasted_iota(jnp.int32, (ATT_BLOCK, ATT_BLOCK), 1)
    return (il - jl).astype(F32), (jl // CHUNK) <= (il // CHUNK)


def _rows(i):
    return pl.ds(pl.multiple_of(i * ATT_BLOCK, ATT_BLOCK), ATT_BLOCK)


KV_UNROLL = 2


def _kv_loop(n, body, carry):
    main = n // KV_UNROLL

    def chunk(t, c):
        for u in range(KV_UNROLL):
            c = body(t * KV_UNROLL + u, c)
        return c

    carry = lax.fori_loop(0, main, chunk, carry)
    return lax.fori_loop(main * KV_UNROLL, n, body, carry)


def _mla_attn_fwd(q, k, v, B, S):
    H, nq = MLA_HEADS, S // ATT_BLOCK

    def body(q_ref, k_ref, v_ref, o_ref, lse_ref):
        _, mask = _rel_and_mask()

        def qblock(i, _):
            qi = q_ref[_rows(i), :]

            def kv(j, carry, diag):
                m, l, acc = carry
                s = _dot(qi, k_ref[_rows(j), :], _NT)
                if diag:
                    s = jnp.where(mask, s, MASK_VALUE)
                m2 = jnp.maximum(m, jnp.max(s, axis=-1, keepdims=True))
                alpha = jnp.exp(m - m2)
                p = jnp.exp(s - m2)
                l2 = alpha * l + jnp.sum(p, axis=-1, keepdims=True)
                return m2, l2, alpha * acc + _dot(p, v_ref[_rows(j), :], _NN)

            init = (jnp.full((ATT_BLOCK, 1), MASK_VALUE, F32), jnp.zeros((ATT_BLOCK, 1), F32),
                    jnp.zeros((ATT_BLOCK, MLA_V), F32))
            carry = _kv_loop(i, lambda j, c: kv(j, c, False), init)
            m, l, acc = kv(i, carry, True)
            o_ref[_rows(i), :] = acc / l
            lse_ref[0, _rows(i), :] = m + jnp.log(l)
            return 0

        lax.fori_loop(0, nq, qblock, 0)

    return pl.pallas_call(
        body, name="mla_attn_fwd", grid=(B, H),
        in_specs=[pl.BlockSpec((S, MLA_PAD), lambda b, h: (b, h)),
                  pl.BlockSpec((S, MLA_PAD), lambda b, h: (b, h)),
                  pl.BlockSpec((S, MLA_V), lambda b, h: (b, h))],
        out_specs=[pl.BlockSpec((S, MLA_V), lambda b, h: (b, h)),
                   pl.BlockSpec((1, S, 1), lambda b, h: (b * H + h, 0, 0))],
        out_shape=[jax.ShapeDtypeStruct((B * S, H * MLA_V), F32), jax.ShapeDtypeStruct((B * H, S, 1), F32)],
        compiler_params=_params(("parallel", "parallel")),
    )(q, k, v)


def _mla_attn_bwd(q, k, v, o, do, lse, B, S):
    H, nq = MLA_HEADS, S // ATT_BLOCK

    def body(q_ref, k_ref, v_ref, o_ref, do_ref, lse_ref, dq_ref, dk_ref, dv_ref, acc_ref):
        _, mask = _rel_and_mask()
        dk_ref[...] = jnp.zeros(dk_ref.shape, F32)
        dv_ref[...] = jnp.zeros(dv_ref.shape, F32)

        def qblock(i, _):
            qi = q_ref[_rows(i), :]
            doi = do_ref[_rows(i), :]
            delta = jnp.sum(doi * o_ref[_rows(i), :], axis=-1, keepdims=True)
            lse_i = lse_ref[0, _rows(i), :]
            doi = doi.astype(MXU_DTYPE)
            acc_ref[...] = jnp.zeros(acc_ref.shape, F32)

            def kv(j, diag):
                kj = k_ref[_rows(j), :]
                p = jnp.exp(_dot(qi, kj, _NT) - lse_i)
                if diag:
                    p = jnp.where(mask, p, 0.0)
                ds = (p * (_dot(doi, v_ref[_rows(j), :], _NT) - delta)).astype(MXU_DTYPE)
                acc_ref[...] += _dot(ds, kj, _NN)
                dk_ref[_rows(j), :] += _dot(ds, qi, _TN)
                dv_ref[_rows(j), :] += _dot(p, doi, _TN)

            def off(j, c):
                kv(j, False)
                return c

            _kv_loop(i, off, 0)
            kv(i, True)
            dq_ref[_rows(i), :] = acc_ref[...]
            return 0

        lax.fori_loop(0, nq, qblock, 0)

    qk_spec = pl.BlockSpec((S, MLA_PAD), lambda b, h: (b, h))
    v_spec = pl.BlockSpec((S, MLA_V), lambda b, h: (b, h))
    return pl.pallas_call(
        body, name="mla_attn_bwd", grid=(B, H),
        in_specs=[qk_spec, qk_spec, v_spec, v_spec, v_spec,
                  pl.BlockSpec((1, S, 1), lambda b, h: (b * H + h, 0, 0))],
        out_specs=[qk_spec, qk_spec, v_spec],
        out_shape=[jax.ShapeDtypeStruct((B * S, H * MLA_PAD), F32), jax.ShapeDtypeStruct((B * S, H * MLA_PAD), F32),
                   jax.ShapeDtypeStruct((B * S, H * MLA_V), F32)],
        scratch_shapes=[pltpu.VMEM((ATT_BLOCK, MLA_PAD), F32)],
        compiler_params=_params(("parallel", "parallel")),
    )(q, k, v, o, do, lse)


def _ret_log_gamma():
    lg = np.log1p(-np.exp2(RET_GAMMA_BASE - np.arange(RET_HEADS, dtype=np.float32))).astype(np.float32)
    return jnp.asarray(np.broadcast_to(lg[:, None, None], (RET_HEADS, 8, LANES)).copy())


def _ret_decay(lg, rel, mask, steps):
    if steps is None:
        return jnp.where(mask, jnp.exp(lg * jnp.abs(rel)), 0.0)
    return jnp.exp(lg * (rel + (steps * ATT_BLOCK).astype(F32)))


def _ret_attn_fwd(q, k, v, B, S):
    H, nq = RET_HEADS, S // ATT_BLOCK

    def body(lg_ref, q_ref, k_ref, v_ref, o_ref, acc_ref):
        rel, mask = _rel_and_mask()
        lg = lg_ref[0, 0:1, 0:1]

        def qblock(i, _):
            qi = q_ref[_rows(i), :]
            acc_ref[...] = jnp.zeros(acc_ref.shape, F32)

            def kv(j, steps):
                a = _dot(qi, k_ref[_rows(j), :], _NT) * _ret_decay(lg, rel, mask, steps)
                acc_ref[...] += _dot(a, v_ref[_rows(j), :], _NN)

            def off(j, c):
                kv(j, i - j)
                return c

            _kv_loop(i, off, 0)
            kv(i, None)
            o_ref[_rows(i), :] = acc_ref[...]
            return 0

        lax.fori_loop(0, nq, qblock, 0)

    qk_spec = pl.BlockSpec((S, RET_QK), lambda b, h: (b, h))
    v_spec = pl.BlockSpec((S, RET_V), lambda b, h: (b, h))
    return pl.pallas_call(
        body, name="ret_attn_fwd", grid=(B, H),
        in_specs=[pl.BlockSpec((1, 8, LANES), lambda b, h: (h, 0, 0)), qk_spec, qk_spec, v_spec],
        out_specs=v_spec,
        out_shape=jax.ShapeDtypeStruct((B * S, H * RET_V), F32),
        scratch_shapes=[pltpu.VMEM((ATT_BLOCK, RET_V), F32)],
        compiler_params=_params(("parallel", "parallel")),
    )(_ret_log_gamma(), q, k, v)


def _ret_attn_bwd(q, k, v, do, B, S):
    H, nq = RET_HEADS, S // ATT_BLOCK

    def body(lg_ref, q_ref, k_ref, v_ref, do_ref, dq_ref, dk_ref, dv_ref, acc_ref):
        rel, mask = _rel_and_mask()
        lg = lg_ref[0, 0:1, 0:1]
        dk_ref[...] = jnp.zeros(dk_ref.shape, F32)
        dv_ref[...] = jnp.zeros(dv_ref.shape, F32)

        def qblock(i, _):
            qi = q_ref[_rows(i), :]
            doi = do_ref[_rows(i), :].astype(MXU_DTYPE)
            acc_ref[...] = jnp.zeros(acc_ref.shape, F32)

            def kv(j, steps):
                kj = k_ref[_rows(j), :]
                dec = _ret_decay(lg, rel, mask, steps)
                a = _dot(qi, kj, _NT) * dec
                da = (_dot(doi, v_ref[_rows(j), :], _NT) * dec).astype(MXU_DTYPE)
                acc_ref[...] += _dot(da, kj, _NN)
                dk_ref[_rows(j), :] += _dot(da, qi, _TN)
                dv_ref[_rows(j), :] += _dot(a, doi, _TN)

            def off(j, c):
                kv(j, i - j)
                return c

            _kv_loop(i, off, 0)
            kv(i, None)
            dq_ref[_rows(i), :] = acc_ref[...]
            return 0

        lax.fori_loop(0, nq, qblock, 0)

    qk_spec = pl.BlockSpec((S, RET_QK), lambda b, h: (b, h))
    v_spec = pl.BlockSpec((S, RET_V), lambda b, h: (b, h))
    return pl.pallas_call(
        body, name="ret_attn_bwd", grid=(B, H),
        in_specs=[pl.BlockSpec((1, 8, LANES), lambda b, h: (h, 0, 0)), qk_spec, qk_spec, v_spec, v_spec],
        out_specs=[qk_spec, qk_spec, v_spec],
        out_shape=[jax.ShapeDtypeStruct((B * S, H * RET_QK), F32), jax.ShapeDtypeStruct((B * S, H * RET_QK), F32),
                   jax.ShapeDtypeStruct((B * S, H * RET_V), F32)],
        scratch_shapes=[pltpu.VMEM((ATT_BLOCK, RET_QK), F32)],
        compiler_params=_params(("parallel", "parallel")),
    )(_ret_log_gamma(), q, k, v, do)


def _loss_head(y, target, bm=512):
    T, D = y.shape
    bm = _pick(T, bm)

    def body(y_ref, t_ref, dy_ref, dyc_ref, l_ref):
        err = y_ref[...] - t_ref[...]
        dy_ref[...] = err / D
        dyc_ref[...] = (err / D).astype(dyc_ref.dtype)
        part = jnp.full((8, LANES), 0.5 * jnp.sum(jnp.mean(err * err, axis=-1)), F32)

        @pl.when(pl.program_id(0) == 0)
        def _():
            l_ref[...] = part

        @pl.when(pl.program_id(0) > 0)
        def _():
            l_ref[...] += part

    blk = pl.BlockSpec((bm, D), lambda i: (i, 0))
    dy, dyc, l = pl.pallas_call(
        body, name="loss_head", grid=(T // bm,),
        in_specs=[blk, blk], out_specs=[blk, blk, pl.BlockSpec((8, LANES), lambda i: (0, 0))],
        out_shape=[jax.ShapeDtypeStruct((T, D), F32), jax.ShapeDtypeStruct((T, D), BF16),
                   jax.ShapeDtypeStruct((8, LANES), F32)],
        compiler_params=_params(("arbitrary",)),
    )(y, target)
    return dy, dyc, l[0, 0]


def _adamw(w, g, m, v, name):
    R, C = w.shape
    br = R if R * C * 4 <= 2 ** 21 else _pick_rows(R, max(8, (2 ** 21) // (C * 4)))

    def body(w_ref, g_ref, m_ref, v_ref, d_ref, mo_ref, vo_ref):
        g_v = g_ref[...]
        m_v = ADAM_B1 * m_ref[...] + (1.0 - ADAM_B1) * g_v
        v_v = ADAM_B2 * v_ref[...] + (1.0 - ADAM_B2) * (g_v * g_v)
        m_hat = m_v / (1.0 - ADAM_B1 ** ADAM_STEP)
        v_hat = v_v / (1.0 - ADAM_B2 ** ADAM_STEP)
        d_ref[...] = -ADAM_LR * (m_hat / (jnp.sqrt(v_hat) + ADAM_EPS) + ADAM_WD * w_ref[...])
        mo_ref[...] = m_v
        vo_ref[...] = v_v

    blk = pl.BlockSpec((br, C), lambda i: (i, 0))
    return pl.pallas_call(
        body, name=name, grid=(R // br,),
        in_specs=[blk] * 4, out_specs=[blk] * 3,
        out_shape=[jax.ShapeDtypeStruct((R, C), F32)] * 3,
        compiler_params=_params(("parallel",)),
    )(w, g, m, v)


def _pick_rows(R, target):
    best = None
    for d in range(8, min(R, target) + 1, 8):
        if R % d == 0:
            best = d
    assert best is not None, (R, target)
    return best


def _position():
    return lax.axis_index("x"), lax.axis_index("y"), lax.axis_index("c")


HBM_SPEC = pl.BlockSpec(memory_space=pltpu.HBM)


def _other_chips(x, y):
    return [(1 - x, y), (x, 1 - y), (1 - x, 1 - y)]


def _all_gather_weights(bigs, small):
    nb = len(bigs)

    def body(*refs):
        big_refs, small_ref = refs[:nb], refs[nb]
        obig, osmall = refs[nb + 1:2 * nb + 1], refs[2 * nb + 1]
        ici_send, ici_recv, d2d_send, d2d_recv, sm_send, sm_recv = refs[2 * nb + 2:]
        x, y, c = _position()
        me = 2 * x + y
        chips = _other_chips(x, y)

        def rows(n, half):
            rh = bigs[n].shape[0] // 2
            return pl.ds(half * rh, rh)

        def over_ici(n, j, slot, from_shard):
            px, py = chips[j]
            dst = obig[n].at[slot, rows(n, c)]
            return pltpu.make_async_remote_copy(
                src_ref=big_refs[n].at[rows(n, c)] if from_shard else dst, dst_ref=dst,
                send_sem=ici_send.at[3 * n + j], recv_sem=ici_recv.at[3 * n + j],
                device_id=(px, py, c), device_id_type=MESH)

        def over_d2d(n, j, half):
            px, py = chips[j]
            part = obig[n].at[2 * px + py, rows(n, half)]
            return pltpu.make_async_remote_copy(
                src_ref=part, dst_ref=part, send_sem=d2d_send.at[3 * n + j], recv_sem=d2d_recv.at[3 * n + j],
                device_id=(x, y, 1 - c), device_id_type=MESH)

        def small_copy(j, slot):
            px, py = chips[j]
            return pltpu.make_async_remote_copy(
                src_ref=small_ref, dst_ref=osmall.at[slot], send_sem=sm_send.at[j], recv_sem=sm_recv.at[j],
                device_id=(px, py, c), device_id_type=MESH)

        sends = [over_ici(n, j, me, True) for n in range(nb) for j in range(3)]
        sends += [small_copy(j, me) for j in range(3)]
        for cp in sends:
            cp.start()
        passed = []
        for n in range(nb):
            for j, (px, py) in enumerate(chips):
                over_ici(n, j, 2 * px + py, False).wait_recv()
                fwd = over_d2d(n, j, c)
                fwd.start()
                passed.append(fwd)
        for n in range(nb):
            for j in range(3):
                over_d2d(n, j, 1 - c).wait_recv()
        for j, (px, py) in enumerate(chips):
            small_copy(j, 2 * px + py).wait_recv()
        for cp in sends + passed:
            cp.wait_send()

    dma = pltpu.SemaphoreType.DMA
    return pl.pallas_call(
        body, name="weights_all_gather",
        in_specs=[HBM_SPEC] * (nb + 1), out_specs=[HBM_SPEC] * (nb + 1),
        out_shape=[jax.ShapeDtypeStruct((N_SHARD,) + b.shape, b.dtype) for b in bigs]
        + [jax.ShapeDtypeStruct((N_SHARD,) + small.shape, small.dtype)],
        scratch_shapes=[dma((3 * nb,)), dma((3 * nb,)), dma((3 * nb,)), dma((3 * nb,)), dma((3,)), dma((3,))],
    )(*bigs, small)


SEM_SPEC = pl.BlockSpec(memory_space=pltpu.SEMAPHORE)
DATAFLOW_EFFECT = pltpu.SideEffectType.DATAFLOW_SIDE_EFFECTING
N_PEERS = N_DEV - 1


def _grad_copies(p_refs, land_refs, send_sems, recv_sems):
    x, y, c = _position()
    copies = []
    for a, (p_ref, land_ref) in enumerate(zip(p_refs, land_refs)):
        rh = p_ref.shape[1] // 2
        for k in range(1, N_DEV):
            px = 1 - x if k & 4 else x
            py = 1 - y if k & 2 else y
            pc = 1 - c if k & 1 else c
            copies.append(pltpu.make_async_remote_copy(
                src_ref=p_ref.at[2 * px + py, pl.ds(pc * rh, rh)], dst_ref=land_ref.at[k - 1],
                send_sem=send_sems.at[N_PEERS * a + k - 1], recv_sem=recv_sems.at[N_PEERS * a + k - 1],
                device_id=(px, py, pc), device_id_type=MESH))
    return copies


def _weight_copies(w_refs, land_refs, send_sems, recv_sems):
    x, y, c = _position()
    copies = []
    for a, (w_ref, land_ref) in enumerate(zip(w_refs, land_refs)):
        for j, (px, py) in enumerate(_other_chips(x, y)):
            copies.append(pltpu.make_async_remote_copy(
                src_ref=w_ref, dst_ref=land_ref.at[2 * x + y], send_sem=send_sems.at[3 * a + j],
                recv_sem=recv_sems.at[3 * a + j], device_id=(px, py, c), device_id_type=MESH))
    return copies


def _exchange_start(make_copies, srcs, lands, n_sems, name, after=None):
    n, m = len(srcs), len(lands)
    n_in = n + m + (after is not None)

    def body(*refs):
        send_sems, recv_sems, token = refs[n_in], refs[n_in + 1], refs[-1]
        for cp in make_copies(refs[:n], refs[n:n + m], send_sems, recv_sems):
            cp.start()
        token[...] = jnp.zeros(token.shape, token.dtype)

    hbm = lambda a: pltpu.with_memory_space_constraint(a, pltpu.HBM)
    dma = pltpu.SemaphoreType.DMA
    res = pl.pallas_call(
        body, name=name,
        in_specs=[HBM_SPEC] * (n + m) + ([] if after is None else [pl.BlockSpec(memory_space=pl.ANY)]),
        out_specs=[SEM_SPEC, SEM_SPEC] + [HBM_SPEC] * (n + m) + [pl.BlockSpec(memory_space=pltpu.VMEM)],
        out_shape=[dma((n_sems,)), dma((n_sems,))] + [pltpu.HBM(a.shape, a.dtype) for a in list(srcs) + list(lands)]
        + [jax.ShapeDtypeStruct((8, LANES), F32)],
        input_output_aliases={i: 2 + i for i in range(n + m)},
        compiler_params=pltpu.CompilerParams(has_side_effects=DATAFLOW_EFFECT),
    )(*[hbm(a) for a in srcs], *[hbm(a) for a in lands], *(() if after is None else (after,)))
    return res[0], res[1], list(res[2:2 + n]), list(res[2 + n:2 + n + m]), res[-1]


def _exchange_wait(make_copies, send_sems, recv_sems, srcs, lands, after, name):
    n, m = len(srcs), len(lands)

    def body(*refs):
        for cp in make_copies(refs[:n], refs[n:n + m], refs[n + m], refs[n + m + 1]):
            cp.wait_send()
            cp.wait_recv()

    res = pl.pallas_call(
        body, name=name,
        in_specs=[HBM_SPEC] * (n + m) + [SEM_SPEC, SEM_SPEC, pl.BlockSpec(memory_space=pl.ANY)],
        out_specs=[HBM_SPEC] * (n + m),
        out_shape=[pltpu.HBM(a.shape, a.dtype) for a in list(srcs) + list(lands)],
        input_output_aliases={i: i for i in range(n + m)},
        compiler_params=pltpu.CompilerParams(has_side_effects=DATAFLOW_EFFECT),
    )(*srcs, *lands, send_sems, recv_sems, after)
    return list(res[:n]), list(res[n:])


def _sum_partials(p, land, name):
    _, rh, cols = land.shape
    br = _pick_rows(rh, 256)
    nrb = rh // br
    x, y, c = _position()
    where = jnp.stack([2 * x + y, c]).astype(jnp.int32)

    def body(where_ref, p_ref, land_ref, o_ref):
        acc = p_ref[...].astype(F32)
        for k in range(N_PEERS):
            acc = acc + land_ref[k].astype(F32)
        o_ref[...] = acc

    return pl.pallas_call(
        body, name=name,
        grid_spec=pltpu.PrefetchScalarGridSpec(
            num_scalar_prefetch=1, grid=(nrb,),
            in_specs=[pl.BlockSpec((None, br, cols), lambda r, where_ref: (where_ref[0], where_ref[1] * nrb + r, 0)),
                      pl.BlockSpec((N_PEERS, br, cols), lambda r, where_ref: (0, r, 0))],
            out_specs=pl.BlockSpec((None, br, cols), lambda r, where_ref: (where_ref[1], r, 0))),
        out_shape=jax.ShapeDtypeStruct((2, rh, cols), F32),
        compiler_params=_params(("parallel",)),
    )(where, p, land)


def _sibling_share(fulls, name):
    n = len(fulls)

    def body(*refs):
        o_refs = refs[n:2 * n]
        send_sems, recv_sems = refs[2 * n:]
        x, y, c = _position()

        def copy(a, half):
            return pltpu.make_async_remote_copy(
                src_ref=o_refs[a].at[half], dst_ref=o_refs[a].at[half], send_sem=send_sems.at[a],
                recv_sem=recv_sems.at[a], device_id=(x, y, 1 - c), device_id_type=MESH)

        sends = [copy(a, c) for a in range(n)]
        for cp in sends:
            cp.start()
        for a in range(n):
            copy(a, 1 - c).wait_recv()
        for cp in sends:
            cp.wait_send()

    dma = pltpu.SemaphoreType.DMA
    return pl.pallas_call(
        body, name=name,
        in_specs=[HBM_SPEC] * n, out_specs=[HBM_SPEC] * n,
        out_shape=[jax.ShapeDtypeStruct(f.shape, f.dtype) for f in fulls],
        input_output_aliases={a: a for a in range(n)},
        scratch_shapes=[dma((n,)), dma((n,))],
    )(*fulls)


def _all_reduce_small(v):
    R, cols = v.shape

    def body(v_ref, o_ref, buf_ref, send_sems, recv_sems):
        x, y, c = _position()
        me = 4 * x + 2 * y + c
        buf_ref[me] = v_ref[...]
        sends = []
        for k in range(1, N_DEV):
            px = 1 - x if k & 4 else x
            py = 1 - y if k & 2 else y
            pc = 1 - c if k & 1 else c
            sends.append(pltpu.make_async_remote_copy(
                src_ref=v_ref, dst_ref=buf_ref.at[me], send_sem=send_sems.at[k - 1], recv_sem=recv_sems.at[k - 1],
                device_id=(px, py, pc), device_id_type=MESH))
        for cp in sends:
            cp.start()
        for k in range(1, N_DEV):
            px = 1 - x if k & 4 else x
            py = 1 - y if k & 2 else y
            pc = 1 - c if k & 1 else c
            pltpu.make_async_remote_copy(
                src_ref=v_ref, dst_ref=buf_ref.at[4 * px + 2 * py + pc], send_sem=send_sems.at[k - 1],
                recv_sem=recv_sems.at[k - 1], device_id=(px, py, pc), device_id_type=MESH).wait_recv()
        for cp in sends:
            cp.wait_send()
        acc = buf_ref[0]
        for d in range(1, N_DEV):
            acc = acc + buf_ref[d]
        o_ref[...] = acc

    return pl.pallas_call(
        body, name="small_grads_all_reduce",
        in_specs=[pl.BlockSpec(memory_space=pltpu.VMEM)], out_specs=pl.BlockSpec(memory_space=pltpu.VMEM),
        out_shape=jax.ShapeDtypeStruct((R, cols), F32),
        scratch_shapes=[pltpu.VMEM((N_DEV, R, cols), F32), pltpu.SemaphoreType.DMA((N_DEV - 1,)),
                        pltpu.SemaphoreType.DMA((N_DEV - 1,))],
    )(v)


def _rope_tables(S, half, width):
    inv_freq = ROPE_THETA ** (-jnp.arange(half, dtype=F32) / half)
    ang = jnp.arange(S).astype(F32)[:, None] * inv_freq[None, :]
    return jnp.cos(ang), jnp.sin(ang)


def _slot_rows(a):
    return a.reshape(N_SHARD, -1, a.shape[-1])


def _local_step(x, target, w, B, S, late, exchange):
    T = B * S
    D = D_MODEL
    bm = 256
    full = lambda a, wd, tile=None: (a, wd, 0, tile or wd)
    g = {}

    cos_r, sin_r = _rope_tables(S, RET_QK // 2, LANES)
    cos_m, sin_m = _rope_tables(S, MLA_ROPE // 2, LANES)
    zeros64 = jnp.zeros((S, 64), F32)
    cos_m = jnp.concatenate([cos_m, cos_m, zeros64], axis=1)
    sin_m = jnp.concatenate([-sin_m, sin_m, zeros64], axis=1)

    def ffn_fwd(xin, i):
        w.update(late(f"ffn{i}", xin))
        norm = w["ffn_norm"][i:i + 1]
        h, ht = _rowwise_fwd(_fn_rms, f"ffn{i}_norm", [full(xin, D)], [], [(norm, D)], [(D, D, BF16)], bm, S,
                             transposed=(0,))
        ag = _mm(h, w[f"ffn_w_in{i}"], "nn", F32, f"ffn{i}_in", bn=1408)
        u, ut = _conv_fwd(ag, w["ffn_conv8"][i], B, S, f"ffn{i}_conv")
        xout = _mm(u, w[f"ffn_w_out{i}"], "nn", F32, f"ffn{i}_out", residual=xin, bk=1408)
        return xout, (xin, norm, ht, ag, ut)

    def ffn_bwd(dxout, dxout_c, saved, i):
        xin, norm, ht, ag, ut = saved
        du = _mm(dxout_c, w[f"ffn_w_out{i}"], "nt", F32, f"ffn{i}_out_dx", bn=1408)
        g_w_out = _mm(ut, dxout_c, "nn", BF16, f"ffn{i}_out_dw", bm=1408, bn=512, bk=T)
        da, dg, dw8 = _conv_bwd(ag, w["ffn_conv8"][i], du, B, S, f"ffn{i}_conv_bwd")
        g_w_in = _mm(ht, [da, dg], "nn", BF16, f"ffn{i}_in_dw", bm=1024, bn=1408, bk=T // 2, out_slots=N_SHARD)
        token = exchange(f"ffn{i}", [g_w_in, _slot_rows(g_w_out)])
        dh = _mm([da, dg], w[f"ffn_w_in{i}"], "nt", F32, f"ffn{i}_in_dx", bk=1408, after=token)
        (dxin, dxin_c), (g_norm,) = _rowwise_bwd(_fn_rms, f"ffn{i}_norm_bwd", [full(xin, D)], [], [(norm, D)],
                                                 [(dh, D)], bm, S, adds={0: dxout}, mxu_copies=(0,))
        return dxin, dxin_c, (g_norm, dw8)

    h0, h0t = _rowwise_fwd(_fn_rms, "ret_norm", [full(x, D)], [], [(w["ret_norm"], D)], [(D, D, BF16)], bm, S,
                           transposed=(0,))
    proj = _mm(h0, w["ret_w_in"], "nn", F32, "ret_in", after=w["started"])
    HQ, HV = RET_HEADS * RET_QK, RET_HEADS * RET_V
    rope_rows = [(proj, 2 * HQ + HV, 0, LANES)]
    q_r, k_r, v_r = _rowwise_fwd(_fn_ret_rope, "ret_rope", rope_rows, [cos_r, sin_r], [],
                                 [(HQ, LANES, BF16), (HQ, LANES, BF16), (HV, LANES, BF16)], bm, S)
    ret_o = _ret_attn_fwd(q_r, k_r, v_r, B, S)
    gate_rows = [full(ret_o, HV, RET_V), (proj, HV, 2, RET_V)]
    y0, y0t = _rowwise_fwd(_fn_ret_gate, "ret_gate", gate_rows, [], [(w["ret_gn"], RET_V)], [(HV, RET_V, BF16)], 128, S,
                           transposed=(0,))
    w.update(late("ret_out", y0))
    x1 = _mm(y0, w["ret_w_out"], "nn", F32, "ret_out", residual=x)
    x2, ffn0_saved = ffn_fwd(x1, 0)

    w.update(late("mla", x2))
    (h2,) = _rowwise_fwd(_fn_rms, "mla_norm", [full(x2, D)], [], [(w["mla_norm"], D)], [(D, D, BF16)], bm, S)
    proj2 = _mm(h2, w["mla_w_in"], "nn", F32, "mla_in")
    lat_consts = [(w["mla_q_norm"], LANES), (w["mla_kv_norm"], LANES)]
    cqn, ckvn, kr = _rowwise_fwd(_fn_mla_lat, "mla_latent_norm", [full(proj2, MLA_IN_PAD, LANES)], [], lat_consts,
                                 [(MLA_Q_RANK, LANES, BF16), (MLA_KV_RANK, LANES, BF16), (LANES, LANES, F32)], bm, S)
    qf = _mm(cqn, w["mla_w_qb"], "nn", F32, "mla_qb")
    kvf = _mm(ckvn, w["mla_w_kvb"], "nn", F32, "mla_kvb")
    HP, HVm = MLA_HEADS * MLA_PAD, MLA_HEADS * MLA_V
    head_rows = [full(qf, HP, LANES), full(kvf, HP, LANES), full(kr, LANES)]
    head_consts = [(w["mla_q_head_norm"], LANES), (w["mla_k_head_norm"], LANES)]
    q_a, k_a, v_a = _rowwise_fwd(_fn_mla_heads, "mla_heads", head_rows, [cos_m, sin_m], head_consts,
                                 [(HP, LANES, BF16), (HP, LANES, BF16), (HVm, LANES, BF16)], bm, S)
    att_o, lse = _mla_attn_fwd(q_a, k_a, v_a, B, S)
    x3 = _mm(att_o, w["mla_w_out"], "nn", F32, "mla_out", residual=x2)
    x4, ffn1_saved = ffn_fwd(x3, 1)

    dy, dy_c, loss = _loss_head(x4, target)

    dx3, dx3_c, (g_n1, dw8_1) = ffn_bwd(dy, dy_c, ffn1_saved, 1)

    d_att_o = _mm(dx3_c, w["mla_w_out"], "nt", F32, "mla_out_dx")
    g_mla_out = _mm(att_o, dx3_c, "tn", BF16, "mla_out_dw")
    dq_a, dk_a, dv_a = _mla_attn_bwd(q_a, k_a, v_a, att_o, d_att_o, lse, B, S)
    (dqf, dkvf, dkr), (g["mla_q_head_norm"], g["mla_k_head_norm"]) = _rowwise_bwd(
        _fn_mla_heads, "mla_heads_bwd", head_rows, [cos_m, sin_m], head_consts,
        [(dq_a, LANES), (dk_a, LANES), (dv_a, LANES)], 128, S)
    dcqn = _mm(dqf, w["mla_w_qb"], "nt", F32, "mla_qb_dx")
    g_qb = _mm(cqn, dqf, "tn", BF16, "mla_qb_dw")
    g_qb = _to_slots(_unpad_heads(g_qb, 1), 1).reshape(N_SHARD, MLA_Q_RANK, -1)
    dckvn = _mm(dkvf, w["mla_w_kvb"], "nt", F32, "mla_kvb_dx")
    g_kvb = _mm(ckvn, dkvf, "tn", BF16, "mla_kvb_dw", bn=512, out_slots=N_SHARD)
    (dproj2,), (g["mla_q_norm"], g["mla_kv_norm"]) = _rowwise_bwd(
        _fn_mla_lat, "mla_latent_norm_bwd", [full(proj2, MLA_IN_PAD, LANES)], [], lat_consts,
        [(dcqn, LANES), (dckvn, LANES), (dkr, LANES)], bm, S)
    g_mla_in = _mm(h2, dproj2, "tn", BF16, "mla_in_dw")
    token = exchange("mla", [_slot_rows(g_mla_in[:, :MLA_IN]), g_qb, g_kvb, _slot_rows(g_mla_out)])
    dh2 = _mm(dproj2, w["mla_w_in"], "nt", F32, "mla_in_dx", after=token)
    (dx2, dx2_c), (g["mla_norm"],) = _rowwise_bwd(_fn_rms, "mla_norm_bwd", [full(x2, D)], [], [(w["mla_norm"], D)],
                                                  [(dh2, D)], bm, S, adds={0: dx3}, mxu_copies=(0,))

    dx1, dx1_c, (g_n0, dw8_0) = ffn_bwd(dx2, dx2_c, ffn0_saved, 0)

    dy0 = _mm(dx1_c, w["ret_w_out"], "nt", F32, "ret_out_dx")
    g_ret_out = _mm(y0t, dx1_c, "nn", BF16, "ret_out_dw", bm=1024, bn=512, bk=T)
    (d_ret_o, dgate), (g["ret_gn"],) = _rowwise_bwd(_fn_ret_gate, "ret_gate_bwd", gate_rows, [], [(w["ret_gn"], RET_V)],
                                                    [(dy0, RET_V)], 128, S, grad_dtypes=[F32, BF16])
    dq_r, dk_r, dv_r = _ret_attn_bwd(q_r, k_r, v_r, d_ret_o, B, S)
    (dqkv,), _ = _rowwise_bwd(_fn_ret_rope, "ret_rope_bwd", rope_rows, [cos_r, sin_r], [],
                              [(dq_r, LANES), (dk_r, LANES), (dv_r, LANES)], bm, S, grad_dtypes=[BF16])
    g_ret_in = _mm(h0t, [dqkv, dgate], "nn", BF16, "ret_in_dw", bn=512, bk=T, out_slots=N_SHARD)
    token = exchange("ret", [g_ret_in, _slot_rows(g_ret_out)])
    dh0 = _mm([dqkv, dgate], w["ret_w_in"], "nt", F32, "ret_in_dx", bk=1024, after=token)
    (dx,), (g["ret_norm"],) = _rowwise_bwd(_fn_rms, "ret_norm_bwd", [full(x, D)], [], [(w["ret_norm"], D)],
                                           [(dh0, D)], bm, S, adds={0: dx1})

    g["ffn_norm"] = jnp.concatenate([g_n0, g_n1], axis=0)
    g["ffn_conv_w"] = jnp.stack([dw8_0[0:3], dw8_1[0:3]])
    g["ffn_conv_b"] = jnp.stack([dw8_0[3], dw8_1[3]])
    return loss, dx, g


_BIG = [("ret_w_in", 2), ("ret_w_out", 1), ("mla_w_in", 1), ("mla_w_qb", 2), ("mla_w_kvb", 2), ("mla_w_out", 1),
        ("ffn_w_in", 2), ("ffn_w_out", 1)]
_SMALL_SHARDED = [("ret_gn", 2), ("mla_norm", 1), ("mla_q_norm", 1), ("mla_kv_norm", 1), ("ffn_conv_w", 2)]
_SMALL_REPLICATED = ["ret_norm", "mla_q_head_norm", "mla_k_head_norm", "ffn_norm", "ffn_conv_b"]
_SMALL_ALL = ["ret_norm", "ret_gn", "mla_norm", "mla_q_norm", "mla_kv_norm", "mla_q_head_norm", "mla_k_head_norm",
              "ffn_norm", "ffn_conv_w", "ffn_conv_b"]


def _to_slots(full, axis):
    shape = full.shape
    split = shape[:axis] + (N_SHARD, shape[axis] // N_SHARD) + shape[axis + 1:]
    return jnp.moveaxis(full.reshape(split), axis, 0).reshape(N_SHARD, -1)


def _from_slots(slots, shard_shape, axis):
    parts = jnp.moveaxis(slots.reshape((N_SHARD,) + tuple(shard_shape)), 0, axis)
    full = shard_shape[:axis] + (N_SHARD * shard_shape[axis],) + shard_shape[axis + 1:]
    return parts.reshape(full)


def _pad_rows(flat, cols, row_unit):
    n, L = flat.shape
    unit = cols * row_unit
    Lp = -(-L // unit) * unit
    if Lp != L:
        flat = jnp.concatenate([flat, jnp.zeros((n, Lp - L), flat.dtype)], axis=1)
    return flat.reshape(n, Lp // cols, cols)


def _pad_heads(a, axis):
    shape = a.shape
    a = a.reshape(shape[:axis] + (MLA_HEADS, MLA_QK) + shape[axis + 1:])
    pad = [(0, 0)] * a.ndim
    pad[axis + 1] = (0, MLA_PAD - MLA_QK)
    return jnp.pad(a, pad).reshape(shape[:axis] + (MLA_HEADS * MLA_PAD,) + shape[axis + 1:])


def _unpad_heads(a, axis):
    shape = a.shape
    a = a.reshape(shape[:axis] + (MLA_HEADS, MLA_PAD) + shape[axis + 1:])
    a = lax.slice_in_dim(a, 0, MLA_QK, axis=axis + 1)
    return a.reshape(shape[:axis] + (MLA_HEADS * MLA_QK,) + shape[axis + 1:])


def kernel(x, ret_norm, ret_w_in, ret_gn, ret_w_out, mla_norm, mla_w_in, mla_q_norm, mla_w_qb, mla_kv_norm, mla_w_kvb, mla_q_head_norm, mla_k_head_norm, mla_w_out, ffn_norm, ffn_w_in, ffn_conv_w, ffn_conv_b, ffn_w_out, loss_target, m_ret_norm, m_ret_w_in, m_ret_gn, m_ret_w_out, m_mla_norm, m_mla_w_in, m_mla_q_norm, m_mla_w_qb, m_mla_kv_norm, m_mla_w_kvb, m_mla_q_head_norm, m_mla_k_head_norm, m_mla_w_out, m_ffn_norm, m_ffn_w_in, m_ffn_conv_w, m_ffn_conv_b, m_ffn_w_out, v_ret_norm, v_ret_w_in, v_ret_gn, v_ret_w_out, v_mla_norm, v_mla_w_in, v_mla_q_norm, v_mla_w_qb, v_mla_kv_norm, v_mla_w_kvb, v_mla_q_head_norm, v_mla_k_head_norm, v_mla_w_out, v_ffn_norm, v_ffn_w_in, v_ffn_conv_w, v_ffn_conv_b, v_ffn_w_out):
    names = ["ret_norm", "ret_w_in", "ret_gn", "ret_w_out", "mla_norm", "mla_w_in", "mla_q_norm", "mla_w_qb",
             "mla_kv_norm", "mla_w_kvb", "mla_q_head_norm", "mla_k_head_norm", "mla_w_out", "ffn_norm", "ffn_w_in",
             "ffn_conv_w", "ffn_conv_b", "ffn_w_out"]
    shard = dict(zip(names, [ret_norm, ret_w_in, ret_gn, ret_w_out, mla_norm, mla_w_in, mla_q_norm, mla_w_qb,
                             mla_kv_norm, mla_w_kvb, mla_q_head_norm, mla_k_head_norm, mla_w_out, ffn_norm, ffn_w_in,
                             ffn_conv_w, ffn_conv_b, ffn_w_out]))
    mom_m = dict(zip(names, [m_ret_norm, m_ret_w_in, m_ret_gn, m_ret_w_out, m_mla_norm, m_mla_w_in, m_mla_q_norm,
                             m_mla_w_qb, m_mla_kv_norm, m_mla_w_kvb, m_mla_q_head_norm, m_mla_k_head_norm, m_mla_w_out,
                             m_ffn_norm, m_ffn_w_in, m_ffn_conv_w, m_ffn_conv_b, m_ffn_w_out]))
    mom_v = dict(zip(names, [v_ret_norm, v_ret_w_in, v_ret_gn, v_ret_w_out, v_mla_norm, v_mla_w_in, v_mla_q_norm,
                             v_mla_w_qb, v_mla_kv_norm, v_mla_w_kvb, v_mla_q_head_norm, v_mla_k_head_norm, v_mla_w_out,
                             v_ffn_norm, v_ffn_w_in, v_ffn_conv_w, v_ffn_conv_b, v_ffn_w_out]))
    B, S, D = x.shape
    T = B * S
    sx, sy = lax.axis_index("x"), lax.axis_index("y")
    me = 2 * sx + sy

    two_d = lambda a: a.reshape(-1, a.shape[-1])
    small_sizes = [int(np.prod(shard[n].shape)) for n, _ in _SMALL_SHARDED]
    small = jnp.concatenate([shard[n].reshape(1, -1) for n, _ in _SMALL_SHARDED], axis=1)
    small = _pad_rows(small, LANES, 8)[0]
    as_mxu = lambda a: two_d(a).astype(BF16)
    is_me = lax.broadcasted_iota(jnp.int32, (N_SHARD, 1, 1), 0) == me
    with_own = lambda gathered, own: jnp.where(is_me, own[None], gathered)
    by_cols = lambda a: jnp.moveaxis(a, 0, 1).reshape(a.shape[1], -1)
    by_rows = lambda a: a.reshape(-1, a.shape[-1])
    pad_in = lambda a: jnp.pad(by_rows(a), ((0, 0), (0, MLA_IN_PAD - MLA_IN)))
    pad_qb = lambda a: _pad_heads(by_cols(a), 1)
    ret_in_shard = as_mxu(shard["ret_w_in"])
    g_ret_in, gsmall = _all_gather_weights([ret_in_shard], small)
    later = [
        ("ret_out", [("ret_w_out", as_mxu(shard["ret_w_out"]), by_rows)]),
        ("ffn0", [("ffn_w_in0", as_mxu(shard["ffn_w_in"][0]), by_cols), ("ffn_w_out0", as_mxu(shard["ffn_w_out"][0]), by_rows)]),
        ("mla", [("mla_w_in", as_mxu(shard["mla_w_in"]), pad_in), ("mla_w_qb", as_mxu(shard["mla_w_qb"]), pad_qb),
                 ("mla_w_kvb", as_mxu(shard["mla_w_kvb"]), by_cols), ("mla_w_out", as_mxu(shard["mla_w_out"]), by_rows)]),
        ("ffn1", [("ffn_w_in1", as_mxu(shard["ffn_w_in"][1]), by_cols), ("ffn_w_out1", as_mxu(shard["ffn_w_out"][1]), by_rows)]),
    ]
    gathering = {}
    token = gsmall
    for group, items in later:
        shards = [s_ for _, s_, _ in items]
        lands = [lax.empty((N_SHARD,) + s_.shape, s_.dtype) for s_ in shards]
        send_sems, recv_sems, shards, lands, token = _exchange_start(
            _weight_copies, shards, lands, 3 * len(shards), f"weights_start_{group}", after=token)
        gathering[group] = (send_sems, recv_sems, shards, lands, items)

    def late(group, after):
        send_sems, recv_sems, shards, lands, items = gathering[group]
        shards, lands = _exchange_wait(_weight_copies, send_sems, recv_sems, shards, lands, after,
                                       f"weights_wait_{group}")
        return {key: full(with_own(l_, s_)) for (key, _, full), s_, l_ in zip(items, shards, lands)}

    gsmall = with_own(gsmall, small).reshape(N_SHARD, -1)
    wfull = {}
    off = 0
    for (n, ax), sz in zip(_SMALL_SHARDED, small_sizes):
        wfull[n] = _from_slots(gsmall[:, off:off + sz], shard[n].shape, ax)
        off += sz
    for n in _SMALL_REPLICATED:
        wfull[n] = shard[n]

    conv8 = jnp.concatenate([wfull["ffn_conv_w"], wfull["ffn_conv_b"][:, None, :],
                             jnp.zeros((2, 4, FFN_DIM), F32)], axis=1)
    w = {
        "started": token, "ret_norm": wfull["ret_norm"], "ret_w_in": by_cols(with_own(g_ret_in, ret_in_shard)),
        "ret_gn": wfull["ret_gn"].reshape(1, RET_HEADS * RET_V), "mla_norm": wfull["mla_norm"],
        "mla_q_norm": wfull["mla_q_norm"], "mla_kv_norm": wfull["mla_kv_norm"],
        "mla_q_head_norm": jnp.pad(wfull["mla_q_head_norm"], ((0, 0), (0, MLA_PAD - MLA_QK))),
        "mla_k_head_norm": jnp.pad(wfull["mla_k_head_norm"], ((0, 0), (0, MLA_PAD - MLA_QK))),
        "ffn_norm": wfull["ffn_norm"], "ffn_conv8": conv8,
    }

    started = {}

    def exchange(group, arrays):
        lands = [lax.empty((N_PEERS, p.shape[1] // 2, p.shape[2]), p.dtype) for p in arrays]
        send_sems, recv_sems, ps, lands, token = _exchange_start(
            _grad_copies, arrays, lands, N_PEERS * len(arrays), f"grads_start_{group}")
        started[group] = (send_sems, recv_sems, ps, lands)
        return token

    loss_part, dx, gl = _local_step(x.reshape(T, D), loss_target.reshape(T, D), w, B, S, late, exchange)
    loss = lax.psum(loss_part, ("x", "y", "c"))
    gfull = {
        "ret_norm": gl["ret_norm"], "ret_gn": gl["ret_gn"].reshape(1, RET_HEADS, RET_V),
        "mla_norm": gl["mla_norm"], "mla_q_norm": gl["mla_q_norm"], "mla_kv_norm": gl["mla_kv_norm"],
        "mla_q_head_norm": gl["mla_q_head_norm"][:, :MLA_QK], "mla_k_head_norm": gl["mla_k_head_norm"][:, :MLA_QK],
        "ffn_norm": gl["ffn_norm"], "ffn_conv_w": gl["ffn_conv_w"], "ffn_conv_b": gl["ffn_conv_b"],
    }

    red = {}
    after = dx
    for group in ("ffn1", "mla", "ffn0", "ret"):
        send_sems, recv_sems, ps, lands = started[group]
        ps, lands = _exchange_wait(_grad_copies, send_sems, recv_sems, ps, lands, after, f"grads_wait_{group}")
        halves = [_sum_partials(p_, l_, f"grads_sum_{group}_{i}") for i, (p_, l_) in enumerate(zip(ps, lands))]
        red[group] = [two_d(r) for r in _sibling_share(halves, f"grads_share_{group}")]
        after = red[group][0]
    grads = {"ret_w_in": red["ret"][0], "ret_w_out": red["ret"][1], "mla_w_in": red["mla"][0],
             "mla_w_qb": red["mla"][1], "mla_w_kvb": red["mla"][2], "mla_w_out": red["mla"][3]}
    grads = {n: a.reshape(shard[n].shape) for n, a in grads.items()}
    grads["ffn_w_in"] = jnp.stack([red["ffn0"][0], red["ffn1"][0]])
    grads["ffn_w_out"] = jnp.stack([red["ffn0"][1], red["ffn1"][1]])

    small_sizes_all = [int(np.prod(gfull[n].shape)) for n in _SMALL_ALL]
    gsm = jnp.concatenate([gfull[n].reshape(1, -1) for n in _SMALL_ALL], axis=1)
    gsm = _all_reduce_small(_pad_rows(gsm, LANES, 8)[0]).reshape(-1)

    sharded_axis = dict(_SMALL_SHARDED)
    off = 0
    for n, sz in zip(_SMALL_ALL, small_sizes_all):
        gn = gsm[off:off + sz].reshape(gfull[n].shape)
        off += sz
        if n in sharded_axis:
            ax = sharded_axis[n]
            width = shard[n].shape[ax]
            gn = lax.dynamic_slice_in_dim(gn, me * width, width, axis=ax)
        grads[n] = gn

    delta, new_m, new_v = {}, {}, {}
    for n, _ in _BIG:
        shp = shard[n].shape
        two_d = lambda a: a.reshape(-1, shp[-1])
        d_, m_, v_ = _adamw(two_d(shard[n]), two_d(grads[n]), two_d(mom_m[n]), two_d(mom_v[n]), f"adamw_{n}")
        delta[n], new_m[n], new_v[n] = d_.reshape(shp), m_.reshape(shp), v_.reshape(shp)
    pack_small = lambda d: _pad_rows(jnp.concatenate([d[n].reshape(1, -1) for n in _SMALL_ALL], axis=1), LANES, 8)[0]
    d_, m_, v_ = _adamw(pack_small(shard), pack_small(grads), pack_small(mom_m), pack_small(mom_v), "adamw_small")
    off = 0
    for n in _SMALL_ALL:
        sz = int(np.prod(shard[n].shape))
        for dst, src in ((delta, d_), (new_m, m_), (new_v, v_)):
            dst[n] = src.reshape(-1)[off:off + sz].reshape(shard[n].shape)
        off += sz

    return (loss, dx.reshape(B, S, D), *[grads[n] for n in names], *[delta[n] for n in names],
            *[new_m[n] for n in names], *[new_v[n] for n in names])
```

```python
import functools
import math

import numpy as np
import jax
import jax.numpy as jnp
from jax import lax
from jax.experimental import pallas as pl
from jax.experimental.pallas import tpu as pltpu

F32 = jnp.float32
BF16 = jnp.bfloat16
MXU_DTYPE = jnp.bfloat16

CHUNK = 64
RMS_EPS = 1e-6
ROPE_THETA = 10000.0
D_MODEL = 1024
RET_HEADS = 4
RET_QK = 256
RET_V = 512
RET_GAMMA_BASE = -5.0
MLA_HEADS = 8
MLA_Q_RANK = 384
MLA_KV_RANK = 256
MLA_NOPE = 128
MLA_ROPE = 64
MLA_V = 128
MLA_QK = MLA_NOPE + MLA_ROPE
MLA_PAD = 256
MLA_IN = MLA_Q_RANK + MLA_KV_RANK + MLA_ROPE
MLA_IN_PAD = MLA_IN + 64
MASK_VALUE = -1e30
FFN_DIM = 2816
ADAM_LR = 0.001
ADAM_B1 = 0.9
ADAM_B2 = 0.999
ADAM_EPS = 1e-08
ADAM_WD = 0.01
ADAM_STEP = 10

LANES = 128
ATT_BLOCK = 256
MLA_FWD_BLOCK = 512
VMEM_LIMIT = 56 * 2 ** 20
N_SHARD = 4
N_DEV = 8

MESH = pl.DeviceIdType.MESH


def _params(sem=None, **kw):
    return pltpu.CompilerParams(dimension_semantics=sem, vmem_limit_bytes=VMEM_LIMIT, **kw)


def _pick(dim, target):
    if dim <= target:
        return dim
    best = None
    for d in range(LANES, target + 1, LANES):
        if dim % d == 0:
            best = d
    assert best is not None, (dim, target)
    return best


def _mm(a, b, dims, out_dtype, name, residual=None, bm=512, bn=1024, bk=2048, out_slots=None, after=None):
    a_parts = list(a) if isinstance(a, (list, tuple)) else [a]
    b_parts = list(b) if isinstance(b, (list, tuple)) else [b]
    parts_on_n = dims == "tn" or len(b_parts) > 1
    if parts_on_n:
        assert len(a_parts) == 1 and dims in ("tn", "nn")
        (K, M) = a_parts[0].shape if dims == "tn" else a_parts[0].shape[::-1]
        N = sum(p.shape[1] for p in b_parts)
        part_widths = [p.shape[1] for p in b_parts]
    else:
        assert len(b_parts) == 1
        M = a_parts[0].shape[0]
        K = sum(p.shape[1] for p in a_parts)
        N = b_parts[0].shape[1 if dims == "nn" else 0]
        part_widths = [p.shape[1] for p in a_parts]
    bm, bn, bk = _pick(M, bm), _pick(N, bn), _pick(K, min(bk, 1024) if dims == "tn" else bk)
    nk = K // bk
    unit = bn if parts_on_n else bk
    assert all(wd % unit == 0 for wd in part_widths), (name, part_widths, unit)
    bounds = np.cumsum([0] + [wd // unit for wd in part_widths])
    ranges = [(int(lo), int(hi)) for lo, hi in zip(bounds[:-1], bounds[1:])]

    def part_index(idx, lo, hi):
        return jnp.clip(idx - lo, 0, hi - lo - 1)

    if parts_on_n:
        if dims == "tn":
            a_specs = [pl.BlockSpec((bk, bm), lambda i, j, k: (k, i))]
            dn = (((0,), (0,)), ((), ()))
        else:
            a_specs = [pl.BlockSpec((bm, bk), lambda i, j, k: (i, k))]
            dn = (((1,), (0,)), ((), ()))
        b_specs = [pl.BlockSpec((bk, bn), functools.partial(lambda i, j, k, lo, hi: (k, part_index(j, lo, hi)), lo=lo, hi=hi))
                   for lo, hi in ranges]
    else:
        a_specs = [pl.BlockSpec((bm, bk), functools.partial(lambda i, j, k, lo, hi: (i, part_index(k, lo, hi)), lo=lo, hi=hi))
                   for lo, hi in ranges]
        if dims == "nt":
            b_specs = [pl.BlockSpec((bn, bk), lambda i, j, k: (j, k))]
        else:
            b_specs = [pl.BlockSpec((bk, bn), lambda i, j, k: (k, j))]
        dn = (((1,), (1 if dims == "nt" else 0,)), ((), ()))
    r_spec = pl.BlockSpec((bm, bn), lambda i, j, k: (i, j))
    if out_slots is None:
        o_spec, o_shape = r_spec, (M, N)
    else:
        ns = N // out_slots
        assert ns % bn == 0, (name, ns, bn)
        nbs = ns // bn
        o_spec = pl.BlockSpec((None, bm, bn), lambda i, j, k: (j // nbs, i, j % nbs))
        o_shape = (out_slots, M, ns)
    has_res = residual is not None
    na, nb = len(a_parts), len(b_parts)

    def body(*refs):
        a_refs, b_refs = refs[:na], refs[na:na + nb]
        r_ref = refs[na + nb] if has_res else None
        n_in = na + nb + has_res + (after is not None)
        o_ref = refs[n_in]
        acc_ref = refs[n_in + 1] if nk > 1 else None
        k = pl.program_id(2)

        def finish(acc):
            if has_res:
                acc = acc + r_ref[...].astype(F32)
            o_ref[...] = acc.astype(out_dtype)

        def compute(a_ref, b_ref):
            p = lax.dot_general(a_ref[...].astype(MXU_DTYPE), b_ref[...].astype(MXU_DTYPE), dn,
                                preferred_element_type=F32)
            if nk == 1:
                finish(p)
                return

            @pl.when(k == 0)
            def _():
                acc_ref[...] = p

            @pl.when(jnp.logical_and(k > 0, k < nk - 1))
            def _():
                acc_ref[...] += p

            @pl.when(k == nk - 1)
            def _():
                finish(acc_ref[...] + p)

        if len(ranges) == 1:
            compute(a_refs[0], b_refs[0])
        else:
            idx = pl.program_id(1) if parts_on_n else k
            for p, (lo, hi) in enumerate(ranges):
                @pl.when(jnp.logical_and(idx >= lo, idx < hi))
                def _(p=p):
                    compute(a_refs[0 if parts_on_n else p], b_refs[p if parts_on_n else 0])

    after_specs = [] if after is None else [pl.BlockSpec(after.shape, lambda i, j, k: (0, 0))]
    return pl.pallas_call(
        body, name=name, grid=(M // bm, N // bn, nk),
        in_specs=a_specs + b_specs + ([r_spec] if has_res else []) + after_specs, out_specs=o_spec,
        out_shape=jax.ShapeDtypeStruct(o_shape, out_dtype),
        scratch_shapes=[pltpu.VMEM((bm, bn), F32)] if nk > 1 else [],
        compiler_params=_params(("parallel", "parallel", "arbitrary")),
    )(*a_parts, *b_parts, *((residual,) if has_res else ()), *(() if after is None else (after,)))


def _tiles(ref, width, tile):
    return [ref[:, t * tile:(t + 1) * tile].astype(F32) for t in range(width // tile)]


def _row_specs(rows, pos, consts, bm, S):
    npos_blocks = S // bm
    specs = [pl.BlockSpec((bm, w), functools.partial(lambda i, c: (i, c), c=cb)) for (_, w, cb, _) in rows]
    specs += [pl.BlockSpec((bm, p.shape[1]), lambda i: (i % npos_blocks, 0)) for p in pos]
    specs += [pl.BlockSpec(c.shape, lambda i: (0, 0)) for (c, _) in consts]
    return specs


def _rowwise_fwd(fn, name, rows, pos, consts, outs, bm, S, transposed=()):
    T = rows[0][0].shape[0]
    nr, npos, nc, no = len(rows), len(pos), len(consts), len(outs)

    def body(*refs):
        row_v = [_tiles(r, w, t) for r, (_, w, _, t) in zip(refs[:nr], rows)]
        pos_v = [r[...] for r in refs[nr:nr + npos]]
        const_v = [_tiles(r, c.shape[1], t) for r, (c, t) in zip(refs[nr + npos:nr + npos + nc], consts)]
        res = fn(row_v, pos_v, const_v)
        out_refs = refs[nr + npos + nc:]
        for o_ref, tiles, (w, t, dt) in zip(out_refs, res, outs):
            for k, v in enumerate(tiles):
                o_ref[:, k * t:(k + 1) * t] = v.astype(dt)
        for t_ref, a in zip(out_refs[no:], transposed):
            t = outs[a][1]
            for k, v in enumerate(res[a]):
                t_ref[k * t:(k + 1) * t, :] = v.T.astype(t_ref.dtype)

    return pl.pallas_call(
        body, name=name, grid=(T // bm,),
        in_specs=_row_specs(rows, pos, consts, bm, S),
        out_specs=[pl.BlockSpec((bm, w), lambda i: (i, 0)) for (w, _, _) in outs]
        + [pl.BlockSpec((outs[a][0], bm), lambda i: (0, i)) for a in transposed],
        out_shape=[jax.ShapeDtypeStruct((T, w), dt) for (w, _, dt) in outs]
        + [jax.ShapeDtypeStruct((outs[a][0], T), BF16) for a in transposed],
        compiler_params=_params(("parallel",)),
    )(*[r[0] for r in rows], *pos, *[c[0] for c in consts])


def _rowwise_bwd(fn, name, rows, pos, consts, cts, bm, S, adds=None, grad_dtypes=None, mxu_copies=()):
    adds = adds or {}
    T = rows[0][0].shape[0]
    nr, npos, nc, nct = len(rows), len(pos), len(consts), len(cts)
    add_idx = sorted(adds)
    grad_dtypes = grad_dtypes or [F32] * nr

    def body(*refs):
        it = iter(refs)
        row_refs = [next(it) for _ in range(nr)]
        pos_refs = [next(it) for _ in range(npos)]
        const_refs = [next(it) for _ in range(nc)]
        ct_refs = [next(it) for _ in range(nct)]
        add_refs = {k: next(it) for k in add_idx}
        drow_refs = [next(it) for _ in range(nr)]
        copy_refs = {a: next(it) for a in mxu_copies}
        dconst_refs = [next(it) for _ in range(nc)]
        row_v = [_tiles(r, w, t) for r, (_, w, _, t) in zip(row_refs, rows)]
        pos_v = [r[...] for r in pos_refs]
        const_v = [_tiles(r, c.shape[1], t) for r, (c, t) in zip(const_refs, consts)]
        ct_v = [_tiles(r, c.shape[1], t) for r, (c, t) in zip(ct_refs, cts)]
        _, vjp = jax.vjp(lambda rv, cv: fn(rv, pos_v, cv), row_v, const_v)
        drows, dconsts = vjp(ct_v)
        for a, (d_ref, tiles, (_, w, _, t)) in enumerate(zip(drow_refs, drows, rows)):
            for k, v in enumerate(tiles):
                if a in add_refs:
                    v = v + add_refs[a][:, k * t:(k + 1) * t].astype(F32)
                d_ref[:, k * t:(k + 1) * t] = v.astype(d_ref.dtype)
                if a in copy_refs:
                    copy_refs[a][:, k * t:(k + 1) * t] = v.astype(BF16)
        first = pl.program_id(0) == 0
        for d_ref, tiles, (_, t) in zip(dconst_refs, dconsts, consts):
            for k, v in enumerate(tiles):
                @pl.when(first)
                def _(d_ref=d_ref, k=k, t=t, v=v):
                    d_ref[:, k * t:(k + 1) * t] = v

                @pl.when(jnp.logical_not(first))
                def _(d_ref=d_ref, k=k, t=t, v=v):
                    d_ref[:, k * t:(k + 1) * t] += v

    in_specs = _row_specs(rows, pos, consts, bm, S)
    in_specs += [pl.BlockSpec((bm, c.shape[1]), lambda i: (i, 0)) for (c, _) in cts]
    in_specs += [pl.BlockSpec((bm, adds[k].shape[1]), lambda i: (i, 0)) for k in add_idx]
    out_specs = [pl.BlockSpec((bm, w), lambda i: (i, 0)) for (_, w, _, _) in rows]
    out_specs += [pl.BlockSpec((bm, rows[a][1]), lambda i: (i, 0)) for a in mxu_copies]
    out_specs += [pl.BlockSpec(c.shape, lambda i: (0, 0)) for (c, _) in consts]
    out_shape = [jax.ShapeDtypeStruct((T, w), dt) for (_, w, _, _), dt in zip(rows, grad_dtypes)]
    out_shape += [jax.ShapeDtypeStruct((T, rows[a][1]), BF16) for a in mxu_copies]
    out_shape += [jax.ShapeDtypeStruct(c.shape, F32) for (c, _) in consts]
    res = pl.pallas_call(
        body, name=name, grid=(T // bm,),
        in_specs=in_specs, out_specs=out_specs, out_shape=out_shape,
        compiler_params=_params(("arbitrary",)),
    )(*[r[0] for r in rows], *pos, *[c[0] for c in consts], *[c[0] for c in cts], *[adds[k] for k in add_idx])
    n_rows = nr + len(mxu_copies)
    return res[:n_rows], res[n_rows:]


def _ssq(tiles):
    s = jnp.sum(tiles[0] * tiles[0], axis=-1, keepdims=True)
    for t in tiles[1:]:
        s = s + jnp.sum(t * t, axis=-1, keepdims=True)
    return s


def _sigmoid(x):
    return 1.0 / (1.0 + jnp.exp(-x))


def _fn_rms(rows, pos, consts):
    (x,), (g,) = rows[0], consts[0]
    r = lax.rsqrt(jnp.mean(x * x, axis=-1, keepdims=True) + RMS_EPS)
    return [[x * r * g]]


def _fn_ret_rope(rows, pos, consts):
    (qkv,) = rows
    nq = RET_HEADS * RET_QK // LANES
    q, k, v = qkv[:nq], qkv[nq:2 * nq], qkv[2 * nq:]
    cos, sin = pos

    def rot(t, scale):
        out = []
        for h in range(RET_HEADS):
            x1, x2 = t[2 * h], t[2 * h + 1]
            o1, o2 = x1 * cos - x2 * sin, x2 * cos + x1 * sin
            out += [o1, o2] if scale is None else [o1 * scale, o2 * scale]
        return out

    return [rot(q, None), rot(k, RET_QK ** -0.5), list(v)]


def _fn_ret_gate(rows, pos, consts):
    o, g = rows
    (gn,) = consts
    out = []
    for h in range(RET_HEADS):
        r = lax.rsqrt(jnp.mean(o[h] * o[h], axis=-1, keepdims=True) + RMS_EPS)
        out.append((o[h] * r * gn[h]) * (g[h] * _sigmoid(g[h])))
    return [out]


def _fn_mla_lat(rows, pos, consts):
    (p,) = rows
    gq, gkv = consts
    nq, nkv = MLA_Q_RANK // LANES, MLA_KV_RANK // LANES
    cq, ckv, kr = p[:nq], p[nq:nq + nkv], p[nq + nkv]
    rq = lax.rsqrt(_ssq(cq) / MLA_Q_RANK + RMS_EPS)
    rkv = lax.rsqrt(_ssq(ckv) / MLA_KV_RANK + RMS_EPS)
    return [[t * rq * g for t, g in zip(cq, gq)], [t * rkv * g for t, g in zip(ckv, gkv)], [kr]]


def _swap32_impl(x):
    lane = lax.broadcasted_iota(jnp.int32, x.shape, 1)
    up, down = pltpu.roll(x, LANES - 32, 1), pltpu.roll(x, 32, 1)
    return jnp.where(lane < 32, up, jnp.where(lane < 64, down, 0.0))


@jax.custom_vjp
def _swap32(x):
    return _swap32_impl(x)


_swap32.defvjp(lambda x: (_swap32_impl(x), None), lambda _, g: (_swap32_impl(g),))


def _fn_mla_heads(rows, pos, consts):
    qf, kvf, (kr,) = rows
    cos, sin = pos
    gq, gk = consts
    q_out, k_out, v_out = [], [], []
    for h in range(MLA_HEADS):
        q0, q1 = qf[2 * h], qf[2 * h + 1]
        r = lax.rsqrt(_ssq([q0, q1]) / MLA_QK + RMS_EPS)
        a0, a1 = q0 * r * gq[0], q1 * r * gq[1]
        a1 = a1 * cos + _swap32(a1) * sin
        q_out += [a0 * (MLA_QK ** -0.5), a1 * (MLA_QK ** -0.5)]
        k0 = kvf[2 * h]
        r = lax.rsqrt(_ssq([k0, kr]) / MLA_QK + RMS_EPS)
        b0, b1 = k0 * r * gk[0], kr * r * gk[1]
        k_out += [b0, b1 * cos + _swap32(b1) * sin]
        v_out.append(kvf[2 * h + 1])
    return [q_out, k_out, v_out]


def _shift_down(x, n):
    row = lax.broadcasted_iota(jnp.int32, x.shape, 0)
    return jnp.where(row >= n, pltpu.roll(x, n, 0), 0.0)


def _shift_up(x, n):
    rows = x.shape[0]
    row = lax.broadcasted_iota(jnp.int32, x.shape, 0)
    return jnp.where(row < rows - n, pltpu.roll(x, rows - n, 0), 0.0)


def _conv_blocks(S):
    cb = 256
    return cb, FFN_DIM // cb


def _conv_fwd(ag, w8, B, S, name):
    cb, ncb = _conv_blocks(S)

    def body(a_ref, g_ref, w_ref, u_ref, ut_ref):
        g = g_ref[...]
        w = w_ref[...]
        gc = w[0:1] * _shift_down(g, 2) + w[1:2] * _shift_down(g, 1) + w[2:3] * g + w[3:4]
        u = a_ref[...] * (gc * _sigmoid(gc))
        u_ref[...] = u.astype(u_ref.dtype)
        ut_ref[...] = u.T.astype(ut_ref.dtype)

    return pl.pallas_call(
        body, name=name, grid=(ncb, B),
        in_specs=[pl.BlockSpec((S, cb), lambda j, b: (b, j)),
                  pl.BlockSpec((S, cb), lambda j, b: (b, ncb + j)),
                  pl.BlockSpec((8, cb), lambda j, b: (0, j))],
        out_specs=[pl.BlockSpec((S, cb), lambda j, b: (b, j)), pl.BlockSpec((cb, S), lambda j, b: (j, b))],
        out_shape=[jax.ShapeDtypeStruct((B * S, FFN_DIM), BF16), jax.ShapeDtypeStruct((FFN_DIM, B * S), BF16)],
        compiler_params=_params(("parallel", "parallel")),
    )(ag, ag, w8)


def _conv_bwd(ag, w8, du, B, S, name):
    cb, ncb = _conv_blocks(S)

    def body(a_ref, g_ref, w_ref, du_ref, da_ref, dg_ref, dw_ref):
        g = g_ref[...]
        w = w_ref[...]
        g1, g2 = _shift_down(g, 1), _shift_down(g, 2)
        gc = w[0:1] * g2 + w[1:2] * g1 + w[2:3] * g + w[3:4]
        sg = _sigmoid(gc)
        du_v = du_ref[...]
        da_ref[...] = (du_v * (gc * sg)).astype(da_ref.dtype)
        dgc = du_v * a_ref[...] * (sg * (1.0 + gc * (1.0 - sg)))
        dg = w[2:3] * dgc + w[1:2] * _shift_up(dgc, 1) + w[0:1] * _shift_up(dgc, 2)
        dg_ref[...] = dg.astype(dg_ref.dtype)
        part = jnp.concatenate([
            jnp.sum(dgc * g2, axis=0, keepdims=True), jnp.sum(dgc * g1, axis=0, keepdims=True),
            jnp.sum(dgc * g, axis=0, keepdims=True), jnp.sum(dgc, axis=0, keepdims=True),
            jnp.zeros((4, cb), F32)], axis=0)

        @pl.when(pl.program_id(1) == 0)
        def _():
            dw_ref[...] = part

        @pl.when(pl.program_id(1) > 0)
        def _():
            dw_ref[...] += part

    blk = lambda j, b: (b, j)
    return pl.pallas_call(
        body, name=name, grid=(ncb, B),
        in_specs=[pl.BlockSpec((S, cb), blk),
                  pl.BlockSpec((S, cb), lambda j, b: (b, ncb + j)),
                  pl.BlockSpec((8, cb), lambda j, b: (0, j)),
                  pl.BlockSpec((S, cb), blk)],
        out_specs=[pl.BlockSpec((S, cb), blk), pl.BlockSpec((S, cb), blk),
                   pl.BlockSpec((8, cb), lambda j, b: (0, j))],
        out_shape=[jax.ShapeDtypeStruct((B * S, FFN_DIM), BF16), jax.ShapeDtypeStruct((B * S, FFN_DIM), BF16),
                   jax.ShapeDtypeStruct((8, FFN_DIM), F32)],
        compiler_params=_params(("parallel", "arbitrary")),
    )(ag, ag, w8, du)


_NT = (((1,), (1,)), ((), ()))
_NN = (((1,), (0,)), ((), ()))
_TN = (((0,), (0,)), ((), ()))


def _dot(a, b, dn):
    return lax.dot_general(a.astype(MXU_DTYPE), b.astype(MXU_DTYPE), dn, preferred_element_type=F32)


def _rel_and_mask():
    il = lax.broadcasted_iota(jnp.int32, (ATT_BLOCK, ATT_BLOCK), 0)
    jl = lax.broadcasted_iota(jnp.int32, (ATT_BLOCK, ATT_BLOCK), 1)
    return (il - jl).astype(F32), (jl // CHUNK) <= (il // CHUNK)


def _rows(i):
    return pl.ds(pl.multiple_of(i * ATT_BLOCK, ATT_BLOCK), ATT_BLOCK)


def _run_bits(n):
    bits, b = [], 1
    while b < n:
        bits.append(b)
        b *= 2
    return bits[::-1]


def _key_runs(n, nq, update):
    for bit in _run_bits(nq + 1):
        @pl.when((n & bit) != 0)
        def _(bit=bit):
            update(n & ~(2 * bit - 1), bit, (n & (bit - 1)) == 0)


def _chunk_visible(shape, nblk, blk):
    key = lax.broadcasted_iota(jnp.int32, shape, 0) - (nblk - 1) * blk
    query = lax.broadcasted_iota(jnp.int32, shape, 1)
    return jnp.logical_or(key < 0, (key // CHUNK) <= (query // CHUNK))


KV_UNROLL = 2


def _kv_loop(n, body, carry):
    main = n // KV_UNROLL

    def chunk(t, c):
        for u in range(KV_UNROLL):
            c = body(t * KV_UNROLL + u, c)
        return c

    carry = lax.fori_loop(0, main, chunk, carry)
    return lax.fori_loop(main * KV_UNROLL, n, body, carry)


def _mla_attn_fwd(q, k, v, B, S):
    blk = min(MLA_FWD_BLOCK, S)
    H, nq = MLA_HEADS, S // blk

    def body(q_ref, k_ref, v_ref, o_ref, lse_ref, m_ref, l_ref, acc_ref):
        def qblock(i, _):
            q_rows = pl.ds(pl.multiple_of(i * blk, blk), blk)
            qi = q_ref[q_rows, :]
            m_ref[...] = jnp.full(m_ref.shape, MASK_VALUE, F32)
            l_ref[...] = jnp.zeros(l_ref.shape, F32)
            acc_ref[...] = jnp.zeros(acc_ref.shape, F32)

            def keys(first, nblk, last):
                rows = pl.ds(pl.multiple_of(first * blk, blk), nblk * blk)
                s = _dot(k_ref[rows, :], qi, _NT)
                s = jnp.where(jnp.logical_or(_chunk_visible(s.shape, nblk, blk), jnp.logical_not(last)), s, MASK_VALUE)
                m = m_ref[...]
                m2 = jnp.maximum(m, jnp.max(s, axis=0, keepdims=True))
                alpha = jnp.exp(m - m2)
                p = jnp.exp(s - m2)
                l_ref[...] = alpha * l_ref[...] + jnp.sum(p, axis=0, keepdims=True)
                acc_ref[...] = alpha * acc_ref[...] + _dot(v_ref[rows, :], p, _TN)
                m_ref[...] = m2

            _key_runs(i + 1, nq, keys)
            l = l_ref[...]
            o_ref[q_rows, :] = (acc_ref[...] / l).T
            lse_ref[0, :, q_rows] = m_ref[...] + jnp.log(l)
            return 0

        lax.fori_loop(0, nq, qblock, 0)

    return pl.pallas_call(
        body, name="mla_attn_fwd", grid=(B, H),
        in_specs=[pl.BlockSpec((S, MLA_PAD), lambda b, h: (b, h)),
                  pl.BlockSpec((S, MLA_PAD), lambda b, h: (b, h)),
                  pl.BlockSpec((S, MLA_V), lambda b, h: (b, h))],
        out_specs=[pl.BlockSpec((S, MLA_V), lambda b, h: (b, h)),
                   pl.BlockSpec((1, 1, S), lambda b, h: (b * H + h, 0, 0))],
        out_shape=[jax.ShapeDtypeStruct((B * S, H * MLA_V), F32), jax.ShapeDtypeStruct((B * H, 1, S), F32)],
        scratch_shapes=[pltpu.VMEM((1, blk), F32), pltpu.VMEM((1, blk), F32), pltpu.VMEM((MLA_V, blk), F32)],
        compiler_params=_params(("parallel", "parallel")),
    )(q, k, v)


def _mla_attn_bwd(q, k, v, o, do, lse, B, S):
    blk = min(MLA_FWD_BLOCK, S)
    H, nq = MLA_HEADS, S // blk

    def body(q_ref, k_ref, v_ref, o_ref, do_ref, lse_ref, dq_ref, dk_ref, dv_ref, kt_ref, dqt_ref):
        dk_ref[...] = jnp.zeros(dk_ref.shape, F32)
        dv_ref[...] = jnp.zeros(dv_ref.shape, F32)
        for g in range(nq):
            kt_ref[g] = k_ref[g * blk:(g + 1) * blk, :].T

        def qblock(i, _):
            q_rows = pl.ds(pl.multiple_of(i * blk, blk), blk)
            qi = q_ref[q_rows, :]
            doi = do_ref[q_rows, :]
            delta = jnp.sum((doi * o_ref[q_rows, :]).T, axis=0, keepdims=True)
            lse_i = lse_ref[0, :, q_rows]
            doi = doi.astype(MXU_DTYPE)
            dqt_ref[...] = jnp.zeros(dqt_ref.shape, F32)

            def keys(first, nblk, last):
                rows = pl.ds(pl.multiple_of(first * blk, blk), nblk * blk)
                k_run, v_run = k_ref[rows, :], v_ref[rows, :]
                p = jnp.exp(_dot(k_run, qi, _NT) - lse_i)
                p = jnp.where(jnp.logical_or(_chunk_visible(p.shape, nblk, blk), jnp.logical_not(last)), p, 0.0)
                ds = (p * (_dot(v_run, doi, _NT) - delta)).astype(MXU_DTYPE)
                dk_ref[rows, :] += _dot(ds, qi, _NN)
                dv_ref[rows, :] += _dot(p, doi, _NN)
                for r in range(nblk):
                    dqt_ref[...] += _dot(kt_ref[first + r], ds[r * blk:(r + 1) * blk, :], _NN)

            _key_runs(i + 1, nq, keys)
            dq_ref[q_rows, :] = dqt_ref[...].T
            return 0

        lax.fori_loop(0, nq, qblock, 0)

    qk_spec = pl.BlockSpec((S, MLA_PAD), lambda b, h: (b, h))
    v_spec = pl.BlockSpec((S, MLA_V), lambda b, h: (b, h))
    return pl.pallas_call(
        body, name="mla_attn_bwd", grid=(B, H),
        in_specs=[qk_spec, qk_spec, v_spec, v_spec, v_spec,
                  pl.BlockSpec((1, 1, S), lambda b, h: (b * H + h, 0, 0))],
        out_specs=[qk_spec, qk_spec, v_spec],
        out_shape=[jax.ShapeDtypeStruct((B * S, H * MLA_PAD), F32), jax.ShapeDtypeStruct((B * S, H * MLA_PAD), F32),
                   jax.ShapeDtypeStruct((B * S, H * MLA_V), F32)],
        scratch_shapes=[pltpu.VMEM((nq, MLA_PAD, blk), q.dtype), pltpu.VMEM((MLA_PAD, blk), F32)],
        compiler_params=_params(("parallel", "parallel")),
    )(q, k, v, o, do, lse)


def _ret_log_gamma():
    lg = np.log1p(-np.exp2(RET_GAMMA_BASE - np.arange(RET_HEADS, dtype=np.float32))).astype(np.float32)
    return jnp.asarray(np.broadcast_to(lg[:, None, None], (RET_HEADS, 8, LANES)).copy())


def _ret_decay(lg, rel, mask, steps):
    if steps is None:
        return jnp.where(mask, jnp.exp(lg * jnp.abs(rel)), 0.0)
    return jnp.exp(lg * (rel + (steps * ATT_BLOCK).astype(F32)))


def _ret_attn_fwd(q, k, v, B, S):
    H, nq = RET_HEADS, S // ATT_BLOCK

    def body(lg_ref, q_ref, k_ref, v_ref, o_ref, acc_ref):
        rel, mask = _rel_and_mask()
        lg = lg_ref[0, 0:1, 0:1]

        def qblock(i, _):
            qi = q_ref[_rows(i), :]
            acc_ref[...] = jnp.zeros(acc_ref.shape, F32)

            def kv(j, steps):
                a = _dot(qi, k_ref[_rows(j), :], _NT) * _ret_decay(lg, rel, mask, steps)
                acc_ref[...] += _dot(a, v_ref[_rows(j), :], _NN)

            def off(j, c):
                kv(j, i - j)
                return c

            _kv_loop(i, off, 0)
            kv(i, None)
            o_ref[_rows(i), :] = acc_ref[...]
            return 0

        lax.fori_loop(0, nq, qblock, 0)

    qk_spec = pl.BlockSpec((S, RET_QK), lambda b, h: (b, h))
    v_spec = pl.BlockSpec((S, RET_V), lambda b, h: (b, h))
    return pl.pallas_call(
        body, name="ret_attn_fwd", grid=(B, H),
        in_specs=[pl.BlockSpec((1, 8, LANES), lambda b, h: (h, 0, 0)), qk_spec, qk_spec, v_spec],
        out_specs=v_spec,
        out_shape=jax.ShapeDtypeStruct((B * S, H * RET_V), F32),
        scratch_shapes=[pltpu.VMEM((ATT_BLOCK, RET_V), F32)],
        compiler_params=_params(("parallel", "parallel")),
    )(_ret_log_gamma(), q, k, v)


def _ret_attn_bwd(q, k, v, do, B, S):
    H, nq = RET_HEADS, S // ATT_BLOCK

    def body(lg_ref, q_ref, k_ref, v_ref, do_ref, dq_ref, dk_ref, dv_ref, acc_ref):
        rel, mask = _rel_and_mask()
        lg = lg_ref[0, 0:1, 0:1]
        dk_ref[...] = jnp.zeros(dk_ref.shape, F32)
        dv_ref[...] = jnp.zeros(dv_ref.shape, F32)

        def qblock(i, _):
            qi = q_ref[_rows(i), :]
            doi = do_ref[_rows(i), :].astype(MXU_DTYPE)
            acc_ref[...] = jnp.zeros(acc_ref.shape, F32)

            def kv(j, steps):
                kj = k_ref[_rows(j), :]
                dec = _ret_decay(lg, rel, mask, steps)
                a = _dot(qi, kj, _NT) * dec
                da = (_dot(doi, v_ref[_rows(j), :], _NT) * dec).astype(MXU_DTYPE)
                acc_ref[...] += _dot(da, kj, _NN)
                dk_ref[_rows(j), :] += _dot(da, qi, _TN)
                dv_ref[_rows(j), :] += _dot(a, doi, _TN)

            def off(j, c):
                kv(j, i - j)
                return c

            _kv_loop(i, off, 0)
            kv(i, None)
            dq_ref[_rows(i), :] = acc_ref[...]
            return 0

        lax.fori_loop(0, nq, qblock, 0)

    qk_spec = pl.BlockSpec((S, RET_QK), lambda b, h: (b, h))
    v_spec = pl.BlockSpec((S, RET_V), lambda b, h: (b, h))
    return pl.pallas_call(
        body, name="ret_attn_bwd", grid=(B, H),
        in_specs=[pl.BlockSpec((1, 8, LANES), lambda b, h: (h, 0, 0)), qk_spec, qk_spec, v_spec, v_spec],
        out_specs=[qk_spec, qk_spec, v_spec],
        out_shape=[jax.ShapeDtypeStruct((B * S, H * RET_QK), F32), jax.ShapeDtypeStruct((B * S, H * RET_QK), F32),
                   jax.ShapeDtypeStruct((B * S, H * RET_V), F32)],
        scratch_shapes=[pltpu.VMEM((ATT_BLOCK, RET_QK), F32)],
        compiler_params=_params(("parallel", "parallel")),
    )(_ret_log_gamma(), q, k, v, do)


def _loss_head(y, target, bm=512):
    T, D = y.shape
    bm = _pick(T, bm)

    def body(y_ref, t_ref, dy_ref, dyc_ref, l_ref):
        err = y_ref[...] - t_ref[...]
        dy_ref[...] = err / D
        dyc_ref[...] = (err / D).astype(dyc_ref.dtype)
        part = jnp.full((8, LANES), 0.5 * jnp.sum(jnp.mean(err * err, axis=-1)), F32)

        @pl.when(pl.program_id(0) == 0)
        def _():
            l_ref[...] = part

        @pl.when(pl.program_id(0) > 0)
        def _():
            l_ref[...] += part

    blk = pl.BlockSpec((bm, D), lambda i: (i, 0))
    dy, dyc, l = pl.pallas_call(
        body, name="loss_head", grid=(T // bm,),
        in_specs=[blk, blk], out_specs=[blk, blk, pl.BlockSpec((8, LANES), lambda i: (0, 0))],
        out_shape=[jax.ShapeDtypeStruct((T, D), F32), jax.ShapeDtypeStruct((T, D), BF16),
                   jax.ShapeDtypeStruct((8, LANES), F32)],
        compiler_params=_params(("arbitrary",)),
    )(y, target)
    return dy, dyc, l[0, 0]


def _adamw(w, g, m, v, name):
    R, C = w.shape
    br = R if R * C * 4 <= 2 ** 21 else _pick_rows(R, max(8, (2 ** 21) // (C * 4)))

    def body(w_ref, g_ref, m_ref, v_ref, d_ref, mo_ref, vo_ref):
        g_v = g_ref[...]
        m_v = ADAM_B1 * m_ref[...] + (1.0 - ADAM_B1) * g_v
        v_v = ADAM_B2 * v_ref[...] + (1.0 - ADAM_B2) * (g_v * g_v)
        m_hat = m_v / (1.0 - ADAM_B1 ** ADAM_STEP)
        v_hat = v_v / (1.0 - ADAM_B2 ** ADAM_STEP)
        d_ref[...] = -ADAM_LR * (m_hat / (jnp.sqrt(v_hat) + ADAM_EPS) + ADAM_WD * w_ref[...])
        mo_ref[...] = m_v
        vo_ref[...] = v_v

    blk = pl.BlockSpec((br, C), lambda i: (i, 0))
    return pl.pallas_call(
        body, name=name, grid=(R // br,),
        in_specs=[blk] * 4, out_specs=[blk] * 3,
        out_shape=[jax.ShapeDtypeStruct((R, C), F32)] * 3,
        compiler_params=_params(("parallel",)),
    )(w, g, m, v)


def _pick_rows(R, target):
    best = None
    for d in range(8, min(R, target) + 1, 8):
        if R % d == 0:
            best = d
    assert best is not None, (R, target)
    return best


def _position():
    return lax.axis_index("x"), lax.axis_index("y"), lax.axis_index("c")


HBM_SPEC = pl.BlockSpec(memory_space=pltpu.HBM)


def _other_chips(x, y):
    return [(1 - x, y), (x, 1 - y), (1 - x, 1 - y)]


def _all_gather_weights(bigs, small):
    nb = len(bigs)

    def body(*refs):
        big_refs, small_ref = refs[:nb], refs[nb]
        obig, osmall = refs[nb + 1:2 * nb + 1], refs[2 * nb + 1]
        ici_send, ici_recv, d2d_send, d2d_recv, sm_send, sm_recv = refs[2 * nb + 2:]
        x, y, c = _position()
        me = 2 * x + y
        chips = _other_chips(x, y)

        def rows(n, half):
            rh = bigs[n].shape[0] // 2
            return pl.ds(half * rh, rh)

        def over_ici(n, j, slot, from_shard):
            px, py = chips[j]
            dst = obig[n].at[slot, rows(n, c)]
            return pltpu.make_async_remote_copy(
                src_ref=big_refs[n].at[rows(n, c)] if from_shard else dst, dst_ref=dst,
                send_sem=ici_send.at[3 * n + j], recv_sem=ici_recv.at[3 * n + j],
                device_id=(px, py, c), device_id_type=MESH)

        def over_d2d(n, j, half):
            px, py = chips[j]
            part = obig[n].at[2 * px + py, rows(n, half)]
            return pltpu.make_async_remote_copy(
                src_ref=part, dst_ref=part, send_sem=d2d_send.at[3 * n + j], recv_sem=d2d_recv.at[3 * n + j],
                device_id=(x, y, 1 - c), device_id_type=MESH)

        def small_copy(j, slot):
            px, py = chips[j]
            return pltpu.make_async_remote_copy(
                src_ref=small_ref, dst_ref=osmall.at[slot], send_sem=sm_send.at[j], recv_sem=sm_recv.at[j],
                device_id=(px, py, c), device_id_type=MESH)

        sends = [over_ici(n, j, me, True) for n in range(nb) for j in range(3)]
        sends += [small_copy(j, me) for j in range(3)]
        for cp in sends:
            cp.start()
        passed = []
        for n in range(nb):
            for j, (px, py) in enumerate(chips):
                over_ici(n, j, 2 * px + py, False).wait_recv()
                fwd = over_d2d(n, j, c)
                fwd.start()
                passed.append(fwd)
        for n in range(nb):
            for j in range(3):
                over_d2d(n, j, 1 - c).wait_recv()
        for j, (px, py) in enumerate(chips):
            small_copy(j, 2 * px + py).wait_recv()
        for cp in sends + passed:
            cp.wait_send()

    dma = pltpu.SemaphoreType.DMA
    return pl.pallas_call(
        body, name="weights_all_gather",
        in_specs=[HBM_SPEC] * (nb + 1), out_specs=[HBM_SPEC] * (nb + 1),
        out_shape=[jax.ShapeDtypeStruct((N_SHARD,) + b.shape, b.dtype) for b in bigs]
        + [jax.ShapeDtypeStruct((N_SHARD,) + small.shape, small.dtype)],
        scratch_shapes=[dma((3 * nb,)), dma((3 * nb,)), dma((3 * nb,)), dma((3 * nb,)), dma((3,)), dma((3,))],
    )(*bigs, small)


SEM_SPEC = pl.BlockSpec(memory_space=pltpu.SEMAPHORE)
DATAFLOW_EFFECT = pltpu.SideEffectType.DATAFLOW_SIDE_EFFECTING
N_PEERS = N_DEV - 1


def _grad_copies(p_refs, land_refs, send_sems, recv_sems):
    x, y, c = _position()
    copies = []
    for a, (p_ref, land_ref) in enumerate(zip(p_refs, land_refs)):
        rh = p_ref.shape[1] // 2
        for k in range(1, N_DEV):
            px = 1 - x if k & 4 else x
            py = 1 - y if k & 2 else y
            pc = 1 - c if k & 1 else c
            copies.append(pltpu.make_async_remote_copy(
                src_ref=p_ref.at[2 * px + py, pl.ds(pc * rh, rh)], dst_ref=land_ref.at[k - 1],
                send_sem=send_sems.at[N_PEERS * a + k - 1], recv_sem=recv_sems.at[N_PEERS * a + k - 1],
                device_id=(px, py, pc), device_id_type=MESH))
    return copies


def _weight_copies(w_refs, land_refs, send_sems, recv_sems):
    x, y, c = _position()
    copies = []
    for a, (w_ref, land_ref) in enumerate(zip(w_refs, land_refs)):
        for j, (px, py) in enumerate(_other_chips(x, y)):
            copies.append(pltpu.make_async_remote_copy(
                src_ref=w_ref, dst_ref=land_ref.at[2 * x + y], send_sem=send_sems.at[3 * a + j],
                recv_sem=recv_sems.at[3 * a + j], device_id=(px, py, c), device_id_type=MESH))
    return copies


def _exchange_start(make_copies, srcs, lands, n_sems, name, after=None):
    n, m = len(srcs), len(lands)
    n_in = n + m + (after is not None)

    def body(*refs):
        send_sems, recv_sems, token = refs[n_in], refs[n_in + 1], refs[-1]
        for cp in make_copies(refs[:n], refs[n:n + m], send_sems, recv_sems):
            cp.start()
        token[...] = jnp.zeros(token.shape, token.dtype)

    hbm = lambda a: pltpu.with_memory_space_constraint(a, pltpu.HBM)
    dma = pltpu.SemaphoreType.DMA
    res = pl.pallas_call(
        body, name=name,
        in_specs=[HBM_SPEC] * (n + m) + ([] if after is None else [pl.BlockSpec(memory_space=pl.ANY)]),
        out_specs=[SEM_SPEC, SEM_SPEC] + [HBM_SPEC] * (n + m) + [pl.BlockSpec(memory_space=pltpu.VMEM)],
        out_shape=[dma((n_sems,)), dma((n_sems,))] + [pltpu.HBM(a.shape, a.dtype) for a in list(srcs) + list(lands)]
        + [jax.ShapeDtypeStruct((8, LANES), F32)],
        input_output_aliases={i: 2 + i for i in range(n + m)},
        compiler_params=pltpu.CompilerParams(has_side_effects=DATAFLOW_EFFECT),
    )(*[hbm(a) for a in srcs], *[hbm(a) for a in lands], *(() if after is None else (after,)))
    return res[0], res[1], list(res[2:2 + n]), list(res[2 + n:2 + n + m]), res[-1]


def _exchange_wait(make_copies, send_sems, recv_sems, srcs, lands, after, name):
    n, m = len(srcs), len(lands)

    def body(*refs):
        for cp in make_copies(refs[:n], refs[n:n + m], refs[n + m], refs[n + m + 1]):
            cp.wait_send()
            cp.wait_recv()

    res = pl.pallas_call(
        body, name=name,
        in_specs=[HBM_SPEC] * (n + m) + [SEM_SPEC, SEM_SPEC, pl.BlockSpec(memory_space=pl.ANY)],
        out_specs=[HBM_SPEC] * (n + m),
        out_shape=[pltpu.HBM(a.shape, a.dtype) for a in list(srcs) + list(lands)],
        input_output_aliases={i: i for i in range(n + m)},
        compiler_params=pltpu.CompilerParams(has_side_effects=DATAFLOW_EFFECT),
    )(*srcs, *lands, send_sems, recv_sems, after)
    return list(res[:n]), list(res[n:])


def _sum_partials(p, land, name):
    _, rh, cols = land.shape
    br = _pick_rows(rh, 256)
    nrb = rh // br
    x, y, c = _position()
    where = jnp.stack([2 * x + y, c]).astype(jnp.int32)

    def body(where_ref, p_ref, land_ref, o_ref):
        acc = p_ref[...].astype(F32)
        for k in range(N_PEERS):
            acc = acc + land_ref[k].astype(F32)
        o_ref[...] = acc

    return pl.pallas_call(
        body, name=name,
        grid_spec=pltpu.PrefetchScalarGridSpec(
            num_scalar_prefetch=1, grid=(nrb,),
            in_specs=[pl.BlockSpec((None, br, cols), lambda r, where_ref: (where_ref[0], where_ref[1] * nrb + r, 0)),
                      pl.BlockSpec((N_PEERS, br, cols), lambda r, where_ref: (0, r, 0))],
            out_specs=pl.BlockSpec((None, br, cols), lambda r, where_ref: (where_ref[1], r, 0))),
        out_shape=jax.ShapeDtypeStruct((2, rh, cols), F32),
        compiler_params=_params(("parallel",)),
    )(where, p, land)


def _sibling_share(fulls, name):
    n = len(fulls)

    def body(*refs):
        o_refs = refs[n:2 * n]
        send_sems, recv_sems = refs[2 * n:]
        x, y, c = _position()

        def copy(a, half):
            return pltpu.make_async_remote_copy(
                src_ref=o_refs[a].at[half], dst_ref=o_refs[a].at[half], send_sem=send_sems.at[a],
                recv_sem=recv_sems.at[a], device_id=(x, y, 1 - c), device_id_type=MESH)

        sends = [copy(a, c) for a in range(n)]
        for cp in sends:
            cp.start()
        for a in range(n):
            copy(a, 1 - c).wait_recv()
        for cp in sends:
            cp.wait_send()

    dma = pltpu.SemaphoreType.DMA
    return pl.pallas_call(
        body, name=name,
        in_specs=[HBM_SPEC] * n, out_specs=[HBM_SPEC] * n,
        out_shape=[jax.ShapeDtypeStruct(f.shape, f.dtype) for f in fulls],
        input_output_aliases={a: a for a in range(n)},
        scratch_shapes=[dma((n,)), dma((n,))],
    )(*fulls)


def _all_reduce_small(v):
    R, cols = v.shape

    def body(v_ref, o_ref, buf_ref, send_sems, recv_sems):
        x, y, c = _position()
        me = 4 * x + 2 * y + c
        buf_ref[me] = v_ref[...]
        sends = []
        for k in range(1, N_DEV):
            px = 1 - x if k & 4 else x
            py = 1 - y if k & 2 else y
            pc = 1 - c if k & 1 else c
            sends.append(pltpu.make_async_remote_copy(
                src_ref=v_ref, dst_ref=buf_ref.at[me], send_sem=send_sems.at[k - 1], recv_sem=recv_sems.at[k - 1],
                device_id=(px, py, pc), device_id_type=MESH))
        for cp in sends:
            cp.start()
        for k in range(1, N_DEV):
            px = 1 - x if k & 4 else x
            py = 1 - y if k & 2 else y
            pc = 1 - c if k & 1 else c
            pltpu.make_async_remote_copy(
                src_ref=v_ref, dst_ref=buf_ref.at[4 * px + 2 * py + pc], send_sem=send_sems.at[k - 1],
                recv_sem=recv_sems.at[k - 1], device_id=(px, py, pc), device_id_type=MESH).wait_recv()
        for cp in sends:
            cp.wait_send()
        acc = buf_ref[0]
        for d in range(1, N_DEV):
            acc = acc + buf_ref[d]
        o_ref[...] = acc

    return pl.pallas_call(
        body, name="small_grads_all_reduce",
        in_specs=[pl.BlockSpec(memory_space=pltpu.VMEM)], out_specs=pl.BlockSpec(memory_space=pltpu.VMEM),
        out_shape=jax.ShapeDtypeStruct((R, cols), F32),
        scratch_shapes=[pltpu.VMEM((N_DEV, R, cols), F32), pltpu.SemaphoreType.DMA((N_DEV - 1,)),
                        pltpu.SemaphoreType.DMA((N_DEV - 1,))],
    )(v)


def _rope_tables(S, half, width):
    inv_freq = ROPE_THETA ** (-jnp.arange(half, dtype=F32) / half)
    ang = jnp.arange(S).astype(F32)[:, None] * inv_freq[None, :]
    return jnp.cos(ang), jnp.sin(ang)


def _slot_rows(a):
    return a.reshape(N_SHARD, -1, a.shape[-1])


def _local_step(x, target, w, B, S, late, exchange):
    T = B * S
    D = D_MODEL
    bm = 256
    full = lambda a, wd, tile=None: (a, wd, 0, tile or wd)
    g = {}

    cos_r, sin_r = _rope_tables(S, RET_QK // 2, LANES)
    cos_m, sin_m = _rope_tables(S, MLA_ROPE // 2, LANES)
    zeros64 = jnp.zeros((S, 64), F32)
    cos_m = jnp.concatenate([cos_m, cos_m, zeros64], axis=1)
    sin_m = jnp.concatenate([-sin_m, sin_m, zeros64], axis=1)

    def ffn_fwd(xin, i):
        w.update(late(f"ffn{i}", xin))
        norm = w["ffn_norm"][i:i + 1]
        h, ht = _rowwise_fwd(_fn_rms, f"ffn{i}_norm", [full(xin, D)], [], [(norm, D)], [(D, D, BF16)], bm, S,
                             transposed=(0,))
        ag = _mm(h, w[f"ffn_w_in{i}"], "nn", F32, f"ffn{i}_in", bn=1408)
        u, ut = _conv_fwd(ag, w["ffn_conv8"][i], B, S, f"ffn{i}_conv")
        xout = _mm(u, w[f"ffn_w_out{i}"], "nn", F32, f"ffn{i}_out", residual=xin, bk=1408)
        return xout, (xin, norm, ht, ag, ut)

    def ffn_bwd(dxout, dxout_c, saved, i):
        xin, norm, ht, ag, ut = saved
        du = _mm(dxout_c, w[f"ffn_w_out{i}"], "nt", F32, f"ffn{i}_out_dx", bn=1408)
        g_w_out = _mm(ut, dxout_c, "nn", BF16, f"ffn{i}_out_dw", bm=1408, bn=512, bk=T)
        da, dg, dw8 = _conv_bwd(ag, w["ffn_conv8"][i], du, B, S, f"ffn{i}_conv_bwd")
        g_w_in = _mm(ht, [da, dg], "nn", BF16, f"ffn{i}_in_dw", bm=1024, bn=1408, bk=T // 2, out_slots=N_SHARD)
        token = exchange(f"ffn{i}", [g_w_in, _slot_rows(g_w_out)])
        dh = _mm([da, dg], w[f"ffn_w_in{i}"], "nt", F32, f"ffn{i}_in_dx", bk=1408, after=token)
        (dxin, dxin_c), (g_norm,) = _rowwise_bwd(_fn_rms, f"ffn{i}_norm_bwd", [full(xin, D)], [], [(norm, D)],
                                                 [(dh, D)], bm, S, adds={0: dxout}, mxu_copies=(0,))
        return dxin, dxin_c, (g_norm, dw8)

    h0, h0t = _rowwise_fwd(_fn_rms, "ret_norm", [full(x, D)], [], [(w["ret_norm"], D)], [(D, D, BF16)], bm, S,
                           transposed=(0,))
    proj = _mm(h0, w["ret_w_in"], "nn", F32, "ret_in", after=w["started"])
    HQ, HV = RET_HEADS * RET_QK, RET_HEADS * RET_V
    rope_rows = [(proj, 2 * HQ + HV, 0, LANES)]
    q_r, k_r, v_r = _rowwise_fwd(_fn_ret_rope, "ret_rope", rope_rows, [cos_r, sin_r], [],
                                 [(HQ, LANES, BF16), (HQ, LANES, BF16), (HV, LANES, BF16)], bm, S)
    ret_o = _ret_attn_fwd(q_r, k_r, v_r, B, S)
    gate_rows = [full(ret_o, HV, RET_V), (proj, HV, 2, RET_V)]
    y0, y0t = _rowwise_fwd(_fn_ret_gate, "ret_gate", gate_rows, [], [(w["ret_gn"], RET_V)], [(HV, RET_V, BF16)], 128, S,
                           transposed=(0,))
    w.update(late("ret_out", y0))
    x1 = _mm(y0, w["ret_w_out"], "nn", F32, "ret_out", residual=x)
    x2, ffn0_saved = ffn_fwd(x1, 0)

    w.update(late("mla", x2))
    (h2,) = _rowwise_fwd(_fn_rms, "mla_norm", [full(x2, D)], [], [(w["mla_norm"], D)], [(D, D, BF16)], bm, S)
    proj2 = _mm(h2, w["mla_w_in"], "nn", F32, "mla_in")
    lat_consts = [(w["mla_q_norm"], LANES), (w["mla_kv_norm"], LANES)]
    cqn, ckvn, kr = _rowwise_fwd(_fn_mla_lat, "mla_latent_norm", [full(proj2, MLA_IN_PAD, LANES)], [], lat_consts,
                                 [(MLA_Q_RANK, LANES, BF16), (MLA_KV_RANK, LANES, BF16), (LANES, LANES, F32)], bm, S)
    qf = _mm(cqn, w["mla_w_qb"], "nn", F32, "mla_qb")
    kvf = _mm(ckvn, w["mla_w_kvb"], "nn", F32, "mla_kvb")
    HP, HVm = MLA_HEADS * MLA_PAD, MLA_HEADS * MLA_V
    head_rows = [full(qf, HP, LANES), full(kvf, HP, LANES), full(kr, LANES)]
    head_consts = [(w["mla_q_head_norm"], LANES), (w["mla_k_head_norm"], LANES)]
    q_a, k_a, v_a = _rowwise_fwd(_fn_mla_heads, "mla_heads", head_rows, [cos_m, sin_m], head_consts,
                                 [(HP, LANES, BF16), (HP, LANES, BF16), (HVm, LANES, BF16)], bm, S)
    att_o, lse = _mla_attn_fwd(q_a, k_a, v_a, B, S)
    x3 = _mm(att_o, w["mla_w_out"], "nn", F32, "mla_out", residual=x2)
    x4, ffn1_saved = ffn_fwd(x3, 1)

    dy, dy_c, loss = _loss_head(x4, target)

    dx3, dx3_c, (g_n1, dw8_1) = ffn_bwd(dy, dy_c, ffn1_saved, 1)

    d_att_o = _mm(dx3_c, w["mla_w_out"], "nt", F32, "mla_out_dx")
    g_mla_out = _mm(att_o, dx3_c, "tn", BF16, "mla_out_dw")
    dq_a, dk_a, dv_a = _mla_attn_bwd(q_a, k_a, v_a, att_o, d_att_o, lse, B, S)
    (dqf, dkvf, dkr), (g["mla_q_head_norm"], g["mla_k_head_norm"]) = _rowwise_bwd(
        _fn_mla_heads, "mla_heads_bwd", head_rows, [cos_m, sin_m], head_consts,
        [(dq_a, LANES), (dk_a, LANES), (dv_a, LANES)], 128, S)
    dcqn = _mm(dqf, w["mla_w_qb"], "nt", F32, "mla_qb_dx")
    g_qb = _mm(cqn, dqf, "tn", BF16, "mla_qb_dw")
    g_qb = _to_slots(_unpad_heads(g_qb, 1), 1).reshape(N_SHARD, MLA_Q_RANK, -1)
    dckvn = _mm(dkvf, w["mla_w_kvb"], "nt", F32, "mla_kvb_dx")
    g_kvb = _mm(ckvn, dkvf, "tn", BF16, "mla_kvb_dw", bn=512, out_slots=N_SHARD)
    (dproj2,), (g["mla_q_norm"], g["mla_kv_norm"]) = _rowwise_bwd(
        _fn_mla_lat, "mla_latent_norm_bwd", [full(proj2, MLA_IN_PAD, LANES)], [], lat_consts,
        [(dcqn, LANES), (dckvn, LANES), (dkr, LANES)], bm, S)
    g_mla_in = _mm(h2, dproj2, "tn", BF16, "mla_in_dw")
    token = exchange("mla", [_slot_rows(g_mla_in[:, :MLA_IN]), g_qb, g_kvb, _slot_rows(g_mla_out)])
    dh2 = _mm(dproj2, w["mla_w_in"], "nt", F32, "mla_in_dx", after=token)
    (dx2, dx2_c), (g["mla_norm"],) = _rowwise_bwd(_fn_rms, "mla_norm_bwd", [full(x2, D)], [], [(w["mla_norm"], D)],
                                                  [(dh2, D)], bm, S, adds={0: dx3}, mxu_copies=(0,))

    dx1, dx1_c, (g_n0, dw8_0) = ffn_bwd(dx2, dx2_c, ffn0_saved, 0)

    dy0 = _mm(dx1_c, w["ret_w_out"], "nt", F32, "ret_out_dx")
    g_ret_out = _mm(y0t, dx1_c, "nn", BF16, "ret_out_dw", bm=1024, bn=512, bk=T)
    (d_ret_o, dgate), (g["ret_gn"],) = _rowwise_bwd(_fn_ret_gate, "ret_gate_bwd", gate_rows, [], [(w["ret_gn"], RET_V)],
                                                    [(dy0, RET_V)], 128, S, grad_dtypes=[F32, BF16])
    dq_r, dk_r, dv_r = _ret_attn_bwd(q_r, k_r, v_r, d_ret_o, B, S)
    (dqkv,), _ = _rowwise_bwd(_fn_ret_rope, "ret_rope_bwd", rope_rows, [cos_r, sin_r], [],
                              [(dq_r, LANES), (dk_r, LANES), (dv_r, LANES)], bm, S, grad_dtypes=[BF16])
    g_ret_in = _mm(h0t, [dqkv, dgate], "nn", BF16, "ret_in_dw", bn=512, bk=T, out_slots=N_SHARD)
    token = exchange("ret", [g_ret_in, _slot_rows(g_ret_out)])
    dh0 = _mm([dqkv, dgate], w["ret_w_in"], "nt", F32, "ret_in_dx", bk=1024, after=token)
    (dx,), (g["ret_norm"],) = _rowwise_bwd(_fn_rms, "ret_norm_bwd", [full(x, D)], [], [(w["ret_norm"], D)],
                                           [(dh0, D)], bm, S, adds={0: dx1})

    g["ffn_norm"] = jnp.concatenate([g_n0, g_n1], axis=0)
    g["ffn_conv_w"] = jnp.stack([dw8_0[0:3], dw8_1[0:3]])
    g["ffn_conv_b"] = jnp.stack([dw8_0[3], dw8_1[3]])
    return loss, dx, g


_BIG = [("ret_w_in", 2), ("ret_w_out", 1), ("mla_w_in", 1), ("mla_w_qb", 2), ("mla_w_kvb", 2), ("mla_w_out", 1),
        ("ffn_w_in", 2), ("ffn_w_out", 1)]
_SMALL_SHARDED = [("ret_gn", 2), ("mla_norm", 1), ("mla_q_norm", 1), ("mla_kv_norm", 1), ("ffn_conv_w", 2)]
_SMALL_REPLICATED = ["ret_norm", "mla_q_head_norm", "mla_k_head_norm", "ffn_norm", "ffn_conv_b"]
_SMALL_ALL = ["ret_norm", "ret_gn", "mla_norm", "mla_q_norm", "mla_kv_norm", "mla_q_head_norm", "mla_k_head_norm",
              "ffn_norm", "ffn_conv_w", "ffn_conv_b"]


def _to_slots(full, axis):
    shape = full.shape
    split = shape[:axis] + (N_SHARD, shape[axis] // N_SHARD) + shape[axis + 1:]
    return jnp.moveaxis(full.reshape(split), axis, 0).reshape(N_SHARD, -1)


def _from_slots(slots, shard_shape, axis):
    parts = jnp.moveaxis(slots.reshape((N_SHARD,) + tuple(shard_shape)), 0, axis)
    full = shard_shape[:axis] + (N_SHARD * shard_shape[axis],) + shard_shape[axis + 1:]
    return parts.reshape(full)


def _pad_rows(flat, cols, row_unit):
    n, L = flat.shape
    unit = cols * row_unit
    Lp = -(-L // unit) * unit
    if Lp != L:
        flat = jnp.concatenate([flat, jnp.zeros((n, Lp - L), flat.dtype)], axis=1)
    return flat.reshape(n, Lp // cols, cols)


def _pad_heads(a, axis):
    shape = a.shape
    a = a.reshape(shape[:axis] + (MLA_HEADS, MLA_QK) + shape[axis + 1:])
    pad = [(0, 0)] * a.ndim
    pad[axis + 1] = (0, MLA_PAD - MLA_QK)
    return jnp.pad(a, pad).reshape(shape[:axis] + (MLA_HEADS * MLA_PAD,) + shape[axis + 1:])


def _unpad_heads(a, axis):
    shape = a.shape
    a = a.reshape(shape[:axis] + (MLA_HEADS, MLA_PAD) + shape[axis + 1:])
    a = lax.slice_in_dim(a, 0, MLA_QK, axis=axis + 1)
    return a.reshape(shape[:axis] + (MLA_HEADS * MLA_QK,) + shape[axis + 1:])


def kernel(x, ret_norm, ret_w_in, ret_gn, ret_w_out, mla_norm, mla_w_in, mla_q_norm, mla_w_qb, mla_kv_norm, mla_w_kvb, mla_q_head_norm, mla_k_head_norm, mla_w_out, ffn_norm, ffn_w_in, ffn_conv_w, ffn_conv_b, ffn_w_out, loss_target, m_ret_norm, m_ret_w_in, m_ret_gn, m_ret_w_out, m_mla_norm, m_mla_w_in, m_mla_q_norm, m_mla_w_qb, m_mla_kv_norm, m_mla_w_kvb, m_mla_q_head_norm, m_mla_k_head_norm, m_mla_w_out, m_ffn_norm, m_ffn_w_in, m_ffn_conv_w, m_ffn_conv_b, m_ffn_w_out, v_ret_norm, v_ret_w_in, v_ret_gn, v_ret_w_out, v_mla_norm, v_mla_w_in, v_mla_q_norm, v_mla_w_qb, v_mla_kv_norm, v_mla_w_kvb, v_mla_q_head_norm, v_mla_k_head_norm, v_mla_w_out, v_ffn_norm, v_ffn_w_in, v_ffn_conv_w, v_ffn_conv_b, v_ffn_w_out):
    names = ["ret_norm", "ret_w_in", "ret_gn", "ret_w_out", "mla_norm", "mla_w_in", "mla_q_norm", "mla_w_qb",
             "mla_kv_norm", "mla_w_kvb", "mla_q_head_norm", "mla_k_head_norm", "mla_w_out", "ffn_norm", "ffn_w_in",
             "ffn_conv_w", "ffn_conv_b", "ffn_w_out"]
    shard = dict(zip(names, [ret_norm, ret_w_in, ret_gn, ret_w_out, mla_norm, mla_w_in, mla_q_norm, mla_w_qb,
                             mla_kv_norm, mla_w_kvb, mla_q_head_norm, mla_k_head_norm, mla_w_out, ffn_norm, ffn_w_in,
                             ffn_conv_w, ffn_conv_b, ffn_w_out]))
    mom_m = dict(zip(names, [m_ret_norm, m_ret_w_in, m_ret_gn, m_ret_w_out, m_mla_norm, m_mla_w_in, m_mla_q_norm,
                             m_mla_w_qb, m_mla_kv_norm, m_mla_w_kvb, m_mla_q_head_norm, m_mla_k_head_norm, m_mla_w_out,
                             m_ffn_norm, m_ffn_w_in, m_ffn_conv_w, m_ffn_conv_b, m_ffn_w_out]))
    mom_v = dict(zip(names, [v_ret_norm, v_ret_w_in, v_ret_gn, v_ret_w_out, v_mla_norm, v_mla_w_in, v_mla_q_norm,
                             v_mla_w_qb, v_mla_kv_norm, v_mla_w_kvb, v_mla_q_head_norm, v_mla_k_head_norm, v_mla_w_out,
                             v_ffn_norm, v_ffn_w_in, v_ffn_conv_w, v_ffn_conv_b, v_ffn_w_out]))
    B, S, D = x.shape
    T = B * S
    sx, sy = lax.axis_index("x"), lax.axis_index("y")
    me = 2 * sx + sy

    two_d = lambda a: a.reshape(-1, a.shape[-1])
    small_sizes = [int(np.prod(shard[n].shape)) for n, _ in _SMALL_SHARDED]
    small = jnp.concatenate([shard[n].reshape(1, -1) for n, _ in _SMALL_SHARDED], axis=1)
    small = _pad_rows(small, LANES, 8)[0]
    as_mxu = lambda a: two_d(a).astype(BF16)
    is_me = lax.broadcasted_iota(jnp.int32, (N_SHARD, 1, 1), 0) == me
    with_own = lambda gathered, own: jnp.where(is_me, own[None], gathered)
    by_cols = lambda a: jnp.moveaxis(a, 0, 1).reshape(a.shape[1], -1)
    by_rows = lambda a: a.reshape(-1, a.shape[-1])
    pad_in = lambda a: jnp.pad(by_rows(a), ((0, 0), (0, MLA_IN_PAD - MLA_IN)))
    pad_qb = lambda a: _pad_heads(by_cols(a), 1)
    ret_in_shard = as_mxu(shard["ret_w_in"])
    g_ret_in, gsmall = _all_gather_weights([ret_in_shard], small)
    later = [
        ("ret_out", [("ret_w_out", as_mxu(shard["ret_w_out"]), by_rows)]),
        ("ffn0", [("ffn_w_in0", as_mxu(shard["ffn_w_in"][0]), by_cols), ("ffn_w_out0", as_mxu(shard["ffn_w_out"][0]), by_rows)]),
        ("mla", [("mla_w_in", as_mxu(shard["mla_w_in"]), pad_in), ("mla_w_qb", as_mxu(shard["mla_w_qb"]), pad_qb),
                 ("mla_w_kvb", as_mxu(shard["mla_w_kvb"]), by_cols), ("mla_w_out", as_mxu(shard["mla_w_out"]), by_rows)]),
        ("ffn1", [("ffn_w_in1", as_mxu(shard["ffn_w_in"][1]), by_cols), ("ffn_w_out1", as_mxu(shard["ffn_w_out"][1]), by_rows)]),
    ]
    gathering = {}
    token = gsmall
    for group, items in later:
        shards = [s_ for _, s_, _ in items]
        lands = [lax.empty((N_SHARD,) + s_.shape, s_.dtype) for s_ in shards]
        send_sems, recv_sems, shards, lands, token = _exchange_start(
            _weight_copies, shards, lands, 3 * len(shards), f"weights_start_{group}", after=token)
        gathering[group] = (send_sems, recv_sems, shards, lands, items)

    def late(group, after):
        send_sems, recv_sems, shards, lands, items = gathering[group]
        shards, lands = _exchange_wait(_weight_copies, send_sems, recv_sems, shards, lands, after,
                                       f"weights_wait_{group}")
        return {key: full(with_own(l_, s_)) for (key, _, full), s_, l_ in zip(items, shards, lands)}

    gsmall = with_own(gsmall, small).reshape(N_SHARD, -1)
    wfull = {}
    off = 0
    for (n, ax), sz in zip(_SMALL_SHARDED, small_sizes):
        wfull[n] = _from_slots(gsmall[:, off:off + sz], shard[n].shape, ax)
        off += sz
    for n in _SMALL_REPLICATED:
        wfull[n] = shard[n]

    conv8 = jnp.concatenate([wfull["ffn_conv_w"], wfull["ffn_conv_b"][:, None, :],
                             jnp.zeros((2, 4, FFN_DIM), F32)], axis=1)
    w = {
        "started": token, "ret_norm": wfull["ret_norm"], "ret_w_in": by_cols(with_own(g_ret_in, ret_in_shard)),
        "ret_gn": wfull["ret_gn"].reshape(1, RET_HEADS * RET_V), "mla_norm": wfull["mla_norm"],
        "mla_q_norm": wfull["mla_q_norm"], "mla_kv_norm": wfull["mla_kv_norm"],
        "mla_q_head_norm": jnp.pad(wfull["mla_q_head_norm"], ((0, 0), (0, MLA_PAD - MLA_QK))),
        "mla_k_head_norm": jnp.pad(wfull["mla_k_head_norm"], ((0, 0), (0, MLA_PAD - MLA_QK))),
        "ffn_norm": wfull["ffn_norm"], "ffn_conv8": conv8,
    }

    started = {}

    def exchange(group, arrays):
        lands = [lax.empty((N_PEERS, p.shape[1] // 2, p.shape[2]), p.dtype) for p in arrays]
        send_sems, recv_sems, ps, lands, token = _exchange_start(
            _grad_copies, arrays, lands, N_PEERS * len(arrays), f"grads_start_{group}")
        started[group] = (send_sems, recv_sems, ps, lands)
        return token

    loss_part, dx, gl = _local_step(x.reshape(T, D), loss_target.reshape(T, D), w, B, S, late, exchange)
    loss = lax.psum(loss_part, ("x", "y", "c"))
    gfull = {
        "ret_norm": gl["ret_norm"], "ret_gn": gl["ret_gn"].reshape(1, RET_HEADS, RET_V),
        "mla_norm": gl["mla_norm"], "mla_q_norm": gl["mla_q_norm"], "mla_kv_norm": gl["mla_kv_norm"],
        "mla_q_head_norm": gl["mla_q_head_norm"][:, :MLA_QK], "mla_k_head_norm": gl["mla_k_head_norm"][:, :MLA_QK],
        "ffn_norm": gl["ffn_norm"], "ffn_conv_w": gl["ffn_conv_w"], "ffn_conv_b": gl["ffn_conv_b"],
    }

    red = {}
    after = dx
    for group in ("ffn1", "mla", "ffn0", "ret"):
        send_sems, recv_sems, ps, lands = started[group]
        ps, lands = _exchange_wait(_grad_copies, send_sems, recv_sems, ps, lands, after, f"grads_wait_{group}")
        halves = [_sum_partials(p_, l_, f"grads_sum_{group}_{i}") for i, (p_, l_) in enumerate(zip(ps, lands))]
        red[group] = [two_d(r) for r in _sibling_share(halves, f"grads_share_{group}")]
        after = red[group][0]
    grads = {"ret_w_in": red["ret"][0], "ret_w_out": red["ret"][1], "mla_w_in": red["mla"][0],
             "mla_w_qb": red["mla"][1], "mla_w_kvb": red["mla"][2], "mla_w_out": red["mla"][3]}
    grads = {n: a.reshape(shard[n].shape) for n, a in grads.items()}
    grads["ffn_w_in"] = jnp.stack([red["ffn0"][0], red["ffn1"][0]])
    grads["ffn_w_out"] = jnp.stack([red["ffn0"][1], red["ffn1"][1]])

    small_sizes_all = [int(np.prod(gfull[n].shape)) for n in _SMALL_ALL]
    gsm = jnp.concatenate([gfull[n].reshape(1, -1) for n in _SMALL_ALL], axis=1)
    gsm = _all_reduce_small(_pad_rows(gsm, LANES, 8)[0]).reshape(-1)

    sharded_axis = dict(_SMALL_SHARDED)
    off = 0
    for n, sz in zip(_SMALL_ALL, small_sizes_all):
        gn = gsm[off:off + sz].reshape(gfull[n].shape)
        off += sz
        if n in sharded_axis:
            ax = sharded_axis[n]
            width = shard[n].shape[ax]
            gn = lax.dynamic_slice_in_dim(gn, me * width, width, axis=ax)
        grads[n] = gn

    delta, new_m, new_v = {}, {}, {}
    for n, _ in _BIG:
        shp = shard[n].shape
        two_d = lambda a: a.reshape(-1, shp[-1])
        d_, m_, v_ = _adamw(two_d(shard[n]), two_d(grads[n]), two_d(mom_m[n]), two_d(mom_v[n]), f"adamw_{n}")
        delta[n], new_m[n], new_v[n] = d_.reshape(shp), m_.reshape(shp), v_.reshape(shp)
    pack_small = lambda d: _pad_rows(jnp.concatenate([d[n].reshape(1, -1) for n in _SMALL_ALL], axis=1), LANES, 8)[0]
    d_, m_, v_ = _adamw(pack_small(shard), pack_small(grads), pack_small(mom_m), pack_small(mom_v), "adamw_small")
    off = 0
    for n in _SMALL_ALL:
        sz = int(np.prod(shard[n].shape))
        for dst, src in ((delta, d_), (new_m, m_), (new_v, v_)):
            dst[n] = src.reshape(-1)[off:off + sz].reshape(shard[n].shape)
        off += sz

    return (loss, dx.reshape(B, S, D), *[grads[n] for n in names], *[delta[n] for n in names],
            *[new_m[n] for n in names], *[new_v[n] for n in names])
```

```python
import functools
import math

import numpy as np
import jax
import jax.numpy as jnp
from jax import lax
from jax.experimental import pallas as pl
from jax.experimental.pallas import tpu as pltpu

F32 = jnp.float32
BF16 = jnp.bfloat16
MXU_DTYPE = jnp.bfloat16

CHUNK = 64
RMS_EPS = 1e-6
ROPE_THETA = 10000.0
D_MODEL = 1024
RET_HEADS = 4
RET_QK = 256
RET_V = 512
RET_GAMMA_BASE = -5.0
MLA_HEADS = 8
MLA_Q_RANK = 384
MLA_KV_RANK = 256
MLA_NOPE = 128
MLA_ROPE = 64
MLA_V = 128
MLA_QK = MLA_NOPE + MLA_ROPE
MLA_PAD = 256
MLA_IN = MLA_Q_RANK + MLA_KV_RANK + MLA_ROPE
MLA_IN_PAD = MLA_IN + 64
MASK_VALUE = -1e30
FFN_DIM = 2816
ADAM_LR = 0.001
ADAM_B1 = 0.9
ADAM_B2 = 0.999
ADAM_EPS = 1e-08
ADAM_WD = 0.01
ADAM_STEP = 10

LANES = 128
ATT_BLOCK = 256
MLA_FWD_BLOCK = 512
VMEM_LIMIT = 56 * 2 ** 20
N_SHARD = 4
N_DEV = 8

MESH = pl.DeviceIdType.MESH


def _params(sem=None, **kw):
    return pltpu.CompilerParams(dimension_semantics=sem, vmem_limit_bytes=VMEM_LIMIT, **kw)


def _pick(dim, target):
    if dim <= target:
        return dim
    best = None
    for d in range(LANES, target + 1, LANES):
        if dim % d == 0:
            best = d
    assert best is not None, (dim, target)
    return best


def _mm(a, b, dims, out_dtype, name, residual=None, bm=512, bn=1024, bk=2048, out_slots=None, after=None):
    a_parts = list(a) if isinstance(a, (list, tuple)) else [a]
    b_parts = list(b) if isinstance(b, (list, tuple)) else [b]
    parts_on_n = dims == "tn" or len(b_parts) > 1
    if parts_on_n:
        assert len(a_parts) == 1 and dims in ("tn", "nn")
        (K, M) = a_parts[0].shape if dims == "tn" else a_parts[0].shape[::-1]
        N = sum(p.shape[1] for p in b_parts)
        part_widths = [p.shape[1] for p in b_parts]
    else:
        assert len(b_parts) == 1
        M = a_parts[0].shape[0]
        K = sum(p.shape[1] for p in a_parts)
        N = b_parts[0].shape[1 if dims == "nn" else 0]
        part_widths = [p.shape[1] for p in a_parts]
    bm, bn, bk = _pick(M, bm), _pick(N, bn), _pick(K, min(bk, 1024) if dims == "tn" else bk)
    nk = K // bk
    unit = bn if parts_on_n else bk
    assert all(wd % unit == 0 for wd in part_widths), (name, part_widths, unit)
    bounds = np.cumsum([0] + [wd // unit for wd in part_widths])
    ranges = [(int(lo), int(hi)) for lo, hi in zip(bounds[:-1], bounds[1:])]

    def part_index(idx, lo, hi):
        return jnp.clip(idx - lo, 0, hi - lo - 1)

    if parts_on_n:
        if dims == "tn":
            a_specs = [pl.BlockSpec((bk, bm), lambda i, j, k: (k, i))]
            dn = (((0,), (0,)), ((), ()))
        else:
            a_specs = [pl.BlockSpec((bm, bk), lambda i, j, k: (i, k))]
            dn = (((1,), (0,)), ((), ()))
        b_specs = [pl.BlockSpec((bk, bn), functools.partial(lambda i, j, k, lo, hi: (k, part_index(j, lo, hi)), lo=lo, hi=hi))
                   for lo, hi in ranges]
    else:
        a_specs = [pl.BlockSpec((bm, bk), functools.partial(lambda i, j, k, lo, hi: (i, part_index(k, lo, hi)), lo=lo, hi=hi))
                   for lo, hi in ranges]
        if dims == "nt":
            b_specs = [pl.BlockSpec((bn, bk), lambda i, j, k: (j, k))]
        else:
            b_specs = [pl.BlockSpec((bk, bn), lambda i, j, k: (k, j))]
        dn = (((1,), (1 if dims == "nt" else 0,)), ((), ()))
    r_spec = pl.BlockSpec((bm, bn), lambda i, j, k: (i, j))
    if out_slots is None:
        o_spec, o_shape = r_spec, (M, N)
    else:
        ns = N // out_slots
        assert ns % bn == 0, (name, ns, bn)
        nbs = ns // bn
        o_spec = pl.BlockSpec((None, bm, bn), lambda i, j, k: (j // nbs, i, j % nbs))
        o_shape = (out_slots, M, ns)
    has_res = residual is not None
    na, nb = len(a_parts), len(b_parts)

    def body(*refs):
        a_refs, b_refs = refs[:na], refs[na:na + nb]
        r_ref = refs[na + nb] if has_res else None
        n_in = na + nb + has_res + (after is not None)
        o_ref = refs[n_in]
        acc_ref = refs[n_in + 1] if nk > 1 else None
        k = pl.program_id(2)

        def finish(acc):
            if has_res:
                acc = acc + r_ref[...].astype(F32)
            o_ref[...] = acc.astype(out_dtype)

        def compute(a_ref, b_ref):
            p = lax.dot_general(a_ref[...].astype(MXU_DTYPE), b_ref[...].astype(MXU_DTYPE), dn,
                                preferred_element_type=F32)
            if nk == 1:
                finish(p)
                return

            @pl.when(k == 0)
            def _():
                acc_ref[...] = p

            @pl.when(jnp.logical_and(k > 0, k < nk - 1))
            def _():
                acc_ref[...] += p

            @pl.when(k == nk - 1)
            def _():
                finish(acc_ref[...] + p)

        if len(ranges) == 1:
            compute(a_refs[0], b_refs[0])
        else:
            idx = pl.program_id(1) if parts_on_n else k
            for p, (lo, hi) in enumerate(ranges):
                @pl.when(jnp.logical_and(idx >= lo, idx < hi))
                def _(p=p):
                    compute(a_refs[0 if parts_on_n else p], b_refs[p if parts_on_n else 0])

    after_specs = [] if after is None else [pl.BlockSpec(after.shape, lambda i, j, k: (0, 0))]
    return pl.pallas_call(
        body, name=name, grid=(M // bm, N // bn, nk),
        in_specs=a_specs + b_specs + ([r_spec] if has_res else []) + after_specs, out_specs=o_spec,
        out_shape=jax.ShapeDtypeStruct(o_shape, out_dtype),
        scratch_shapes=[pltpu.VMEM((bm, bn), F32)] if nk > 1 else [],
        compiler_params=_params(("parallel", "parallel", "arbitrary")),
    )(*a_parts, *b_parts, *((residual,) if has_res else ()), *(() if after is None else (after,)))


def _tiles(ref, width, tile):
    return [ref[:, t * tile:(t + 1) * tile].astype(F32) for t in range(width // tile)]


def _row_specs(rows, pos, consts, bm, S):
    npos_blocks = S // bm
    specs = [pl.BlockSpec((bm, w), functools.partial(lambda i, c: (i, c), c=cb)) for (_, w, cb, _) in rows]
    specs += [pl.BlockSpec((bm, p.shape[1]), lambda i: (i % npos_blocks, 0)) for p in pos]
    specs += [pl.BlockSpec(c.shape, lambda i: (0, 0)) for (c, _) in consts]
    return specs


def _rowwise_fwd(fn, name, rows, pos, consts, outs, bm, S, transposed=()):
    T = rows[0][0].shape[0]
    nr, npos, nc, no = len(rows), len(pos), len(consts), len(outs)

    def body(*refs):
        row_v = [_tiles(r, w, t) for r, (_, w, _, t) in zip(refs[:nr], rows)]
        pos_v = [r[...] for r in refs[nr:nr + npos]]
        const_v = [_tiles(r, c.shape[1], t) for r, (c, t) in zip(refs[nr + npos:nr + npos + nc], consts)]
        res = fn(row_v, pos_v, const_v)
        out_refs = refs[nr + npos + nc:]
        for o_ref, tiles, (w, t, dt) in zip(out_refs, res, outs):
            for k, v in enumerate(tiles):
                o_ref[:, k * t:(k + 1) * t] = v.astype(dt)
        for t_ref, a in zip(out_refs[no:], transposed):
            t = outs[a][1]
            for k, v in enumerate(res[a]):
                t_ref[k * t:(k + 1) * t, :] = v.T.astype(t_ref.dtype)

    return pl.pallas_call(
        body, name=name, grid=(T // bm,),
        in_specs=_row_specs(rows, pos, consts, bm, S),
        out_specs=[pl.BlockSpec((bm, w), lambda i: (i, 0)) for (w, _, _) in outs]
        + [pl.BlockSpec((outs[a][0], bm), lambda i: (0, i)) for a in transposed],
        out_shape=[jax.ShapeDtypeStruct((T, w), dt) for (w, _, dt) in outs]
        + [jax.ShapeDtypeStruct((outs[a][0], T), BF16) for a in transposed],
        compiler_params=_params(("parallel",)),
    )(*[r[0] for r in rows], *pos, *[c[0] for c in consts])


def _rowwise_bwd(fn, name, rows, pos, consts, cts, bm, S, adds=None, grad_dtypes=None, mxu_copies=()):
    adds = adds or {}
    T = rows[0][0].shape[0]
    nr, npos, nc, nct = len(rows), len(pos), len(consts), len(cts)
    add_idx = sorted(adds)
    grad_dtypes = grad_dtypes or [F32] * nr

    def body(*refs):
        it = iter(refs)
        row_refs = [next(it) for _ in range(nr)]
        pos_refs = [next(it) for _ in range(npos)]
        const_refs = [next(it) for _ in range(nc)]
        ct_refs = [next(it) for _ in range(nct)]
        add_refs = {k: next(it) for k in add_idx}
        drow_refs = [next(it) for _ in range(nr)]
        copy_refs = {a: next(it) for a in mxu_copies}
        dconst_refs = [next(it) for _ in range(nc)]
        row_v = [_tiles(r, w, t) for r, (_, w, _, t) in zip(row_refs, rows)]
        pos_v = [r[...] for r in pos_refs]
        const_v = [_tiles(r, c.shape[1], t) for r, (c, t) in zip(const_refs, consts)]
        ct_v = [_tiles(r, c.shape[1], t) for r, (c, t) in zip(ct_refs, cts)]
        _, vjp = jax.vjp(lambda rv, cv: fn(rv, pos_v, cv), row_v, const_v)
        drows, dconsts = vjp(ct_v)
        for a, (d_ref, tiles, (_, w, _, t)) in enumerate(zip(drow_refs, drows, rows)):
            for k, v in enumerate(tiles):
                if a in add_refs:
                    v = v + add_refs[a][:, k * t:(k + 1) * t].astype(F32)
                d_ref[:, k * t:(k + 1) * t] = v.astype(d_ref.dtype)
                if a in copy_refs:
                    copy_refs[a][:, k * t:(k + 1) * t] = v.astype(BF16)
        first = pl.program_id(0) == 0
        for d_ref, tiles, (_, t) in zip(dconst_refs, dconsts, consts):
            for k, v in enumerate(tiles):
                @pl.when(first)
                def _(d_ref=d_ref, k=k, t=t, v=v):
                    d_ref[:, k * t:(k + 1) * t] = v

                @pl.when(jnp.logical_not(first))
                def _(d_ref=d_ref, k=k, t=t, v=v):
                    d_ref[:, k * t:(k + 1) * t] += v

    in_specs = _row_specs(rows, pos, consts, bm, S)
    in_specs += [pl.BlockSpec((bm, c.shape[1]), lambda i: (i, 0)) for (c, _) in cts]
    in_specs += [pl.BlockSpec((bm, adds[k].shape[1]), lambda i: (i, 0)) for k in add_idx]
    out_specs = [pl.BlockSpec((bm, w), lambda i: (i, 0)) for (_, w, _, _) in rows]
    out_specs += [pl.BlockSpec((bm, rows[a][1]), lambda i: (i, 0)) for a in mxu_copies]
    out_specs += [pl.BlockSpec(c.shape, lambda i: (0, 0)) for (c, _) in consts]
    out_shape = [jax.ShapeDtypeStruct((T, w), dt) for (_, w, _, _), dt in zip(rows, grad_dtypes)]
    out_shape += [jax.ShapeDtypeStruct((T, rows[a][1]), BF16) for a in mxu_copies]
    out_shape += [jax.ShapeDtypeStruct(c.shape, F32) for (c, _) in consts]
    res = pl.pallas_call(
        body, name=name, grid=(T // bm,),
        in_specs=in_specs, out_specs=out_specs, out_shape=out_shape,
        compiler_params=_params(("arbitrary",)),
    )(*[r[0] for r in rows], *pos, *[c[0] for c in consts], *[c[0] for c in cts], *[adds[k] for k in add_idx])
    n_rows = nr + len(mxu_copies)
    return res[:n_rows], res[n_rows:]


def _ssq(tiles):
    s = jnp.sum(tiles[0] * tiles[0], axis=-1, keepdims=True)
    for t in tiles[1:]:
        s = s + jnp.sum(t * t, axis=-1, keepdims=True)
    return s


def _sigmoid(x):
    return 1.0 / (1.0 + jnp.exp(-x))


def _fn_rms(rows, pos, consts):
    (x,), (g,) = rows[0], consts[0]
    r = lax.rsqrt(jnp.mean(x * x, axis=-1, keepdims=True) + RMS_EPS)
    return [[x * r * g]]


def _fn_ret_rope(rows, pos, consts):
    (qkv,) = rows
    nq = RET_HEADS * RET_QK // LANES
    q, k, v = qkv[:nq], qkv[nq:2 * nq], qkv[2 * nq:]
    cos, sin = pos

    def rot(t, scale):
        out = []
        for h in range(RET_HEADS):
            x1, x2 = t[2 * h], t[2 * h + 1]
            o1, o2 = x1 * cos - x2 * sin, x2 * cos + x1 * sin
            out += [o1, o2] if scale is None else [o1 * scale, o2 * scale]
        return out

    return [rot(q, None), rot(k, RET_QK ** -0.5), list(v)]


def _fn_ret_gate(rows, pos, consts):
    o, g = rows
    (gn,) = consts
    out = []
    for h in range(RET_HEADS):
        r = lax.rsqrt(jnp.mean(o[h] * o[h], axis=-1, keepdims=True) + RMS_EPS)
        out.append((o[h] * r * gn[h]) * (g[h] * _sigmoid(g[h])))
    return [out]


def _fn_mla_lat(rows, pos, consts):
    (p,) = rows
    gq, gkv = consts
    nq, nkv = MLA_Q_RANK // LANES, MLA_KV_RANK // LANES
    cq, ckv, kr = p[:nq], p[nq:nq + nkv], p[nq + nkv]
    rq = lax.rsqrt(_ssq(cq) / MLA_Q_RANK + RMS_EPS)
    rkv = lax.rsqrt(_ssq(ckv) / MLA_KV_RANK + RMS_EPS)
    return [[t * rq * g for t, g in zip(cq, gq)], [t * rkv * g for t, g in zip(ckv, gkv)], [kr]]


def _swap32_impl(x):
    lane = lax.broadcasted_iota(jnp.int32, x.shape, 1)
    up, down = pltpu.roll(x, LANES - 32, 1), pltpu.roll(x, 32, 1)
    return jnp.where(lane < 32, up, jnp.where(lane < 64, down, 0.0))


@jax.custom_vjp
def _swap32(x):
    return _swap32_impl(x)


_swap32.defvjp(lambda x: (_swap32_impl(x), None), lambda _, g: (_swap32_impl(g),))


def _fn_mla_heads(rows, pos, consts):
    qf, kvf, (kr,) = rows
    cos, sin = pos
    gq, gk = consts
    q_out, k_out, v_out = [], [], []
    for h in range(MLA_HEADS):
        q0, q1 = qf[2 * h], qf[2 * h + 1]
        r = lax.rsqrt(_ssq([q0, q1]) / MLA_QK + RMS_EPS)
        a0, a1 = q0 * r * gq[0], q1 * r * gq[1]
        a1 = a1 * cos + _swap32(a1) * sin
        q_out += [a0 * (MLA_QK ** -0.5), a1 * (MLA_QK ** -0.5)]
        k0 = kvf[2 * h]
        r = lax.rsqrt(_ssq([k0, kr]) / MLA_QK + RMS_EPS)
        b0, b1 = k0 * r * gk[0], kr * r * gk[1]
        k_out += [b0, b1 * cos + _swap32(b1) * sin]
        v_out.append(kvf[2 * h + 1])
    return [q_out, k_out, v_out]


def _shift_down(x, n):
    row = lax.broadcasted_iota(jnp.int32, x.shape, 0)
    return jnp.where(row >= n, pltpu.roll(x, n, 0), 0.0)


def _shift_up(x, n):
    rows = x.shape[0]
    row = lax.broadcasted_iota(jnp.int32, x.shape, 0)
    return jnp.where(row < rows - n, pltpu.roll(x, rows - n, 0), 0.0)


def _conv_blocks(S):
    cb = 256
    return cb, FFN_DIM // cb


def _conv_fwd(ag, w8, B, S, name):
    cb, ncb = _conv_blocks(S)

    def body(a_ref, g_ref, w_ref, u_ref, ut_ref):
        g = g_ref[...]
        w = w_ref[...]
        gc = w[0:1] * _shift_down(g, 2) + w[1:2] * _shift_down(g, 1) + w[2:3] * g + w[3:4]
        u = a_ref[...] * (gc * _sigmoid(gc))
        u_ref[...] = u.astype(u_ref.dtype)
        ut_ref[...] = u.T.astype(ut_ref.dtype)

    return pl.pallas_call(
        body, name=name, grid=(ncb, B),
        in_specs=[pl.BlockSpec((S, cb), lambda j, b: (b, j)),
                  pl.BlockSpec((S, cb), lambda j, b: (b, ncb + j)),
                  pl.BlockSpec((8, cb), lambda j, b: (0, j))],
        out_specs=[pl.BlockSpec((S, cb), lambda j, b: (b, j)), pl.BlockSpec((cb, S), lambda j, b: (j, b))],
        out_shape=[jax.ShapeDtypeStruct((B * S, FFN_DIM), BF16), jax.ShapeDtypeStruct((FFN_DIM, B * S), BF16)],
        compiler_params=_params(("parallel", "parallel")),
    )(ag, ag, w8)


def _conv_bwd(ag, w8, du, B, S, name):
    cb, ncb = _conv_blocks(S)

    def body(a_ref, g_ref, w_ref, du_ref, da_ref, dg_ref, dw_ref):
        g = g_ref[...]
        w = w_ref[...]
        g1, g2 = _shift_down(g, 1), _shift_down(g, 2)
        gc = w[0:1] * g2 + w[1:2] * g1 + w[2:3] * g + w[3:4]
        sg = _sigmoid(gc)
        du_v = du_ref[...]
        da_ref[...] = (du_v * (gc * sg)).astype(da_ref.dtype)
        dgc = du_v * a_ref[...] * (sg * (1.0 + gc * (1.0 - sg)))
        dg = w[2:3] * dgc + w[1:2] * _shift_up(dgc, 1) + w[0:1] * _shift_up(dgc, 2)
        dg_ref[...] = dg.astype(dg_ref.dtype)
        part = jnp.concatenate([
            jnp.sum(dgc * g2, axis=0, keepdims=True), jnp.sum(dgc * g1, axis=0, keepdims=True),
            jnp.sum(dgc * g, axis=0, keepdims=True), jnp.sum(dgc, axis=0, keepdims=True),
            jnp.zeros((4, cb), F32)], axis=0)

        @pl.when(pl.program_id(1) == 0)
        def _():
            dw_ref[...] = part

        @pl.when(pl.program_id(1) > 0)
        def _():
            dw_ref[...] += part

    blk = lambda j, b: (b, j)
    return pl.pallas_call(
        body, name=name, grid=(ncb, B),
        in_specs=[pl.BlockSpec((S, cb), blk),
                  pl.BlockSpec((S, cb), lambda j, b: (b, ncb + j)),
                  pl.BlockSpec((8, cb), lambda j, b: (0, j)),
                  pl.BlockSpec((S, cb), blk)],
        out_specs=[pl.BlockSpec((S, cb), blk), pl.BlockSpec((S, cb), blk),
                   pl.BlockSpec((8, cb), lambda j, b: (0, j))],
        out_shape=[jax.ShapeDtypeStruct((B * S, FFN_DIM), BF16), jax.ShapeDtypeStruct((B * S, FFN_DIM), BF16),
                   jax.ShapeDtypeStruct((8, FFN_DIM), F32)],
        compiler_params=_params(("parallel", "arbitrary")),
    )(ag, ag, w8, du)


_NT = (((1,), (1,)), ((), ()))
_NN = (((1,), (0,)), ((), ()))
_TN = (((0,), (0,)), ((), ()))


def _dot(a, b, dn):
    return lax.dot_general(a.astype(MXU_DTYPE), b.astype(MXU_DTYPE), dn, preferred_element_type=F32)


def _rel_and_mask():
    il = lax.broadcasted_iota(jnp.int32, (ATT_BLOCK, ATT_BLOCK), 0)
    jl = lax.broadcasted_iota(jnp.int32, (ATT_BLOCK, ATT_BLOCK), 1)
    return (il - jl).astype(F32), (jl // CHUNK) <= (il // CHUNK)


def _rows(i):
    return pl.ds(pl.multiple_of(i * ATT_BLOCK, ATT_BLOCK), ATT_BLOCK)


def _run_bits(n):
    bits, b = [], 1
    while b < n:
        bits.append(b)
        b *= 2
    return bits[::-1]


def _key_runs(n, nq, update):
    for bit in _run_bits(nq + 1):
        @pl.when((n & bit) != 0)
        def _(bit=bit):
            update(n & ~(2 * bit - 1), bit, (n & (bit - 1)) == 0)


def _earlier_runs(n, nq, update):
    for bit in _run_bits(nq):
        @pl.when((n & bit) != 0)
        def _(bit=bit):
            update(n & ~(2 * bit - 1), bit, False)


def _chunk_visible(shape, nblk, blk):
    key = lax.broadcasted_iota(jnp.int32, shape, 0) - (nblk - 1) * blk
    query = lax.broadcasted_iota(jnp.int32, shape, 1)
    return jnp.logical_or(key < 0, (key // CHUNK) <= (query // CHUNK))


KV_UNROLL = 2


def _kv_loop(n, body, carry):
    main = n // KV_UNROLL

    def chunk(t, c):
        for u in range(KV_UNROLL):
            c = body(t * KV_UNROLL + u, c)
        return c

    carry = lax.fori_loop(0, main, chunk, carry)
    return lax.fori_loop(main * KV_UNROLL, n, body, carry)


def _mla_attn_fwd(q, k, v, B, S):
    blk = min(MLA_FWD_BLOCK, S)
    H, nq = MLA_HEADS, S // blk

    def body(q_ref, k_ref, v_ref, o_ref, lse_ref, m_ref, l_ref, acc_ref):
        def qblock(i, _):
            q_rows = pl.ds(pl.multiple_of(i * blk, blk), blk)
            qi = q_ref[q_rows, :]
            m_ref[...] = jnp.full(m_ref.shape, MASK_VALUE, F32)
            l_ref[...] = jnp.zeros(l_ref.shape, F32)
            acc_ref[...] = jnp.zeros(acc_ref.shape, F32)

            def keys(first, nblk, last):
                rows = pl.ds(pl.multiple_of(first * blk, blk), nblk * blk)
                s = _dot(k_ref[rows, :], qi, _NT)
                s = jnp.where(jnp.logical_or(_chunk_visible(s.shape, nblk, blk), jnp.logical_not(last)), s, MASK_VALUE)
                m = m_ref[...]
                m2 = jnp.maximum(m, jnp.max(s, axis=0, keepdims=True))
                alpha = jnp.exp(m - m2)
                p = jnp.exp(s - m2)
                l_ref[...] = alpha * l_ref[...] + jnp.sum(p, axis=0, keepdims=True)
                acc_ref[...] = alpha * acc_ref[...] + _dot(v_ref[rows, :], p, _TN)
                m_ref[...] = m2

            _key_runs(i + 1, nq, keys)
            l = l_ref[...]
            o_ref[q_rows, :] = (acc_ref[...] / l).T
            lse_ref[0, :, q_rows] = m_ref[...] + jnp.log(l)
            return 0

        lax.fori_loop(0, nq, qblock, 0)

    return pl.pallas_call(
        body, name="mla_attn_fwd", grid=(B, H),
        in_specs=[pl.BlockSpec((S, MLA_PAD), lambda b, h: (b, h)),
                  pl.BlockSpec((S, MLA_PAD), lambda b, h: (b, h)),
                  pl.BlockSpec((S, MLA_V), lambda b, h: (b, h))],
        out_specs=[pl.BlockSpec((S, MLA_V), lambda b, h: (b, h)),
                   pl.BlockSpec((1, 1, S), lambda b, h: (b * H + h, 0, 0))],
        out_shape=[jax.ShapeDtypeStruct((B * S, H * MLA_V), F32), jax.ShapeDtypeStruct((B * H, 1, S), F32)],
        scratch_shapes=[pltpu.VMEM((1, blk), F32), pltpu.VMEM((1, blk), F32), pltpu.VMEM((MLA_V, blk), F32)],
        compiler_params=_params(("parallel", "parallel")),
    )(q, k, v)


def _mla_attn_bwd(q, k, v, o, do, lse, B, S):
    blk = min(MLA_FWD_BLOCK, S)
    H, nq = MLA_HEADS, S // blk

    def body(q_ref, k_ref, v_ref, o_ref, do_ref, lse_ref, dq_ref, dk_ref, dv_ref, kt_ref, dqt_ref):
        dk_ref[...] = jnp.zeros(dk_ref.shape, F32)
        dv_ref[...] = jnp.zeros(dv_ref.shape, F32)
        for g in range(nq):
            kt_ref[g] = k_ref[g * blk:(g + 1) * blk, :].T

        def qblock(i, _):
            q_rows = pl.ds(pl.multiple_of(i * blk, blk), blk)
            qi = q_ref[q_rows, :]
            doi = do_ref[q_rows, :]
            delta = jnp.sum((doi * o_ref[q_rows, :]).T, axis=0, keepdims=True)
            lse_i = lse_ref[0, :, q_rows]
            doi = doi.astype(MXU_DTYPE)
            dqt_ref[...] = jnp.zeros(dqt_ref.shape, F32)

            def keys(first, nblk, last):
                rows = pl.ds(pl.multiple_of(first * blk, blk), nblk * blk)
                k_run, v_run = k_ref[rows, :], v_ref[rows, :]
                p = jnp.exp(_dot(k_run, qi, _NT) - lse_i)
                p = jnp.where(jnp.logical_or(_chunk_visible(p.shape, nblk, blk), jnp.logical_not(last)), p, 0.0)
                ds = (p * (_dot(v_run, doi, _NT) - delta)).astype(MXU_DTYPE)
                dk_ref[rows, :] += _dot(ds, qi, _NN)
                dv_ref[rows, :] += _dot(p, doi, _NN)
                for r in range(nblk):
                    dqt_ref[...] += _dot(kt_ref[first + r], ds[r * blk:(r + 1) * blk, :], _NN)

            _key_runs(i + 1, nq, keys)
            dq_ref[q_rows, :] = dqt_ref[...].T
            return 0

        lax.fori_loop(0, nq, qblock, 0)

    qk_spec = pl.BlockSpec((S, MLA_PAD), lambda b, h: (b, h))
    v_spec = pl.BlockSpec((S, MLA_V), lambda b, h: (b, h))
    return pl.pallas_call(
        body, name="mla_attn_bwd", grid=(B, H),
        in_specs=[qk_spec, qk_spec, v_spec, v_spec, v_spec,
                  pl.BlockSpec((1, 1, S), lambda b, h: (b * H + h, 0, 0))],
        out_specs=[qk_spec, qk_spec, v_spec],
        out_shape=[jax.ShapeDtypeStruct((B * S, H * MLA_PAD), F32), jax.ShapeDtypeStruct((B * S, H * MLA_PAD), F32),
                   jax.ShapeDtypeStruct((B * S, H * MLA_V), F32)],
        scratch_shapes=[pltpu.VMEM((nq, MLA_PAD, blk), q.dtype), pltpu.VMEM((MLA_PAD, blk), F32)],
        compiler_params=_params(("parallel", "parallel")),
    )(q, k, v, o, do, lse)


def _ret_log_gamma():
    lg = np.log1p(-np.exp2(RET_GAMMA_BASE - np.arange(RET_HEADS, dtype=np.float32))).astype(np.float32)
    return jnp.asarray(np.broadcast_to(lg[:, None, None], (RET_HEADS, 8, LANES)).copy())


RET_BLOCK = 512


def _ret_local_scale(lg, shape, blk, rising):
    local = lax.broadcasted_iota(jnp.int32, shape, 0) % blk
    return jnp.exp(lg * (local if rising else blk - 1 - local).astype(F32))


def _ret_pair_factor(lg, blk, steps):
    return jnp.exp(lg * (blk * (steps - 1) + 1).astype(F32))


def _ret_own_decay(lg, blk, transposed):
    a = lax.broadcasted_iota(jnp.int32, (blk, blk), 0)
    b = lax.broadcasted_iota(jnp.int32, (blk, blk), 1)
    query, key = (b, a) if transposed else (a, b)
    dec = jnp.exp(lg * jnp.abs(query - key).astype(F32))
    return jnp.where((key // CHUNK) <= (query // CHUNK), dec, 0.0)


def _ret_attn_fwd(q, k, v, B, S):
    blk = min(RET_BLOCK, S)
    H, nq = RET_HEADS, S // blk

    def body(lg_ref, q_ref, k_ref, v_ref, o_ref, ks_ref, dec_ref, acc_ref):
        lg = lg_ref[0, 0:1, 0:1]
        ks_ref[...] = (k_ref[...].astype(F32) * _ret_local_scale(lg, k_ref.shape, blk, False)).astype(ks_ref.dtype)
        dec_ref[...] = _ret_own_decay(lg, blk, False)

        def qblock(i, _):
            q_rows = pl.ds(pl.multiple_of(i * blk, blk), blk)
            qi = q_ref[q_rows, :]
            qs = (qi.astype(F32) * _ret_local_scale(lg, qi.shape, blk, True)).astype(qi.dtype)
            a = _dot(qi, k_ref[q_rows, :], _NT) * dec_ref[...]
            acc_ref[...] = _dot(a, v_ref[q_rows, :], _NN)

            def keys(first, nblk, _):
                rows = pl.ds(pl.multiple_of(first * blk, blk), nblk * blk)
                steps = i - first - lax.broadcasted_iota(jnp.int32, (1, nblk * blk), 1) // blk
                a = _dot(qs, ks_ref[rows, :], _NT) * _ret_pair_factor(lg, blk, steps)
                acc_ref[...] += _dot(a, v_ref[rows, :], _NN)

            _earlier_runs(i, nq, keys)
            o_ref[q_rows, :] = acc_ref[...]
            return 0

        lax.fori_loop(0, nq, qblock, 0)

    qk_spec = pl.BlockSpec((S, RET_QK), lambda b, h: (b, h))
    v_spec = pl.BlockSpec((S, RET_V), lambda b, h: (b, h))
    return pl.pallas_call(
        body, name="ret_attn_fwd", grid=(B, H),
        in_specs=[pl.BlockSpec((1, 8, LANES), lambda b, h: (h, 0, 0)), qk_spec, qk_spec, v_spec],
        out_specs=v_spec,
        out_shape=jax.ShapeDtypeStruct((B * S, H * RET_V), F32),
        scratch_shapes=[pltpu.VMEM((S, RET_QK), k.dtype), pltpu.VMEM((blk, blk), F32), pltpu.VMEM((blk, RET_V), F32)],
        compiler_params=_params(("parallel", "parallel")),
    )(_ret_log_gamma(), q, k, v)


def _ret_attn_bwd(q, k, v, do, B, S):
    blk = min(RET_BLOCK, S)
    H, nq = RET_HEADS, S // blk

    def body(lg_ref, q_ref, k_ref, v_ref, do_ref, dq_ref, dk_ref, dv_ref, ks_ref, kst_ref, dks_ref, dqt_ref, dec_ref):
        lg = lg_ref[0, 0:1, 0:1]
        dk_ref[...] = jnp.zeros(dk_ref.shape, F32)
        dv_ref[...] = jnp.zeros(dv_ref.shape, F32)
        dks_ref[...] = jnp.zeros(dks_ref.shape, F32)
        ks_ref[...] = (k_ref[...].astype(F32) * _ret_local_scale(lg, k_ref.shape, blk, False)).astype(ks_ref.dtype)
        for g in range(nq):
            kst_ref[g] = ks_ref[g * blk:(g + 1) * blk, :].T
        dec_ref[...] = _ret_own_decay(lg, blk, True)

        def qblock(i, _):
            q_rows = pl.ds(pl.multiple_of(i * blk, blk), blk)
            qi = q_ref[q_rows, :]
            q_scale = _ret_local_scale(lg, qi.shape, blk, True)
            qs = (qi.astype(F32) * q_scale).astype(qi.dtype)
            doi = do_ref[q_rows, :].astype(MXU_DTYPE)
            ki = k_ref[q_rows, :]
            dec = dec_ref[...]
            a = _dot(ki, qi, _NT) * dec
            da = (_dot(v_ref[q_rows, :], doi, _NT) * dec).astype(MXU_DTYPE)
            dv_ref[q_rows, :] += _dot(a, doi, _NN)
            dk_ref[q_rows, :] += _dot(da, qi, _NN)
            dq_own = _dot(da, ki, _TN)
            dqt_ref[...] = jnp.zeros(dqt_ref.shape, F32)

            def keys(first, nblk, _):
                for r in range(nblk):
                    g = first + r
                    rows = pl.ds(pl.multiple_of(g * blk, blk), blk)
                    c = _ret_pair_factor(lg, blk, i - g)
                    a = _dot(ks_ref[rows, :], qs, _NT) * c
                    da = (_dot(v_ref[rows, :], doi, _NT) * c).astype(MXU_DTYPE)
                    dv_ref[rows, :] += _dot(a, doi, _NN)
                    dks_ref[rows, :] += _dot(da, qs, _NN)
                    dqt_ref[...] += _dot(kst_ref[g], da, _NN)

            _earlier_runs(i, nq, keys)
            dq_ref[q_rows, :] = dqt_ref[...].T * q_scale + dq_own
            return 0

        lax.fori_loop(0, nq, qblock, 0)
        dk_ref[...] += dks_ref[...] * _ret_local_scale(lg, dks_ref.shape, blk, False)

    qk_spec = pl.BlockSpec((S, RET_QK), lambda b, h: (b, h))
    v_spec = pl.BlockSpec((S, RET_V), lambda b, h: (b, h))
    return pl.pallas_call(
        body, name="ret_attn_bwd", grid=(B, H),
        in_specs=[pl.BlockSpec((1, 8, LANES), lambda b, h: (h, 0, 0)), qk_spec, qk_spec, v_spec, v_spec],
        out_specs=[qk_spec, qk_spec, v_spec],
        out_shape=[jax.ShapeDtypeStruct((B * S, H * RET_QK), F32), jax.ShapeDtypeStruct((B * S, H * RET_QK), F32),
                   jax.ShapeDtypeStruct((B * S, H * RET_V), F32)],
        scratch_shapes=[pltpu.VMEM((S, RET_QK), k.dtype), pltpu.VMEM((nq, RET_QK, blk), k.dtype),
                        pltpu.VMEM((S, RET_QK), F32), pltpu.VMEM((RET_QK, blk), F32), pltpu.VMEM((blk, blk), F32)],
        compiler_params=_params(("parallel", "parallel")),
    )(_ret_log_gamma(), q, k, v, do)


def _loss_head(y, target, bm=512):
    T, D = y.shape
    bm = _pick(T, bm)

    def body(y_ref, t_ref, dy_ref, dyc_ref, l_ref):
        err = y_ref[...] - t_ref[...]
        dy_ref[...] = err / D
        dyc_ref[...] = (err / D).astype(dyc_ref.dtype)
        part = jnp.full((8, LANES), 0.5 * jnp.sum(jnp.mean(err * err, axis=-1)), F32)

        @pl.when(pl.program_id(0) == 0)
        def _():
            l_ref[...] = part

        @pl.when(pl.program_id(0) > 0)
        def _():
            l_ref[...] += part

    blk = pl.BlockSpec((bm, D), lambda i: (i, 0))
    dy, dyc, l = pl.pallas_call(
        body, name="loss_head", grid=(T // bm,),
        in_specs=[blk, blk], out_specs=[blk, blk, pl.BlockSpec((8, LANES), lambda i: (0, 0))],
        out_shape=[jax.ShapeDtypeStruct((T, D), F32), jax.ShapeDtypeStruct((T, D), BF16),
                   jax.ShapeDtypeStruct((8, LANES), F32)],
        compiler_params=_params(("arbitrary",)),
    )(y, target)
    return dy, dyc, l[0, 0]


def _adamw(w, g, m, v, name):
    R, C = w.shape
    br = R if R * C * 4 <= 2 ** 21 else _pick_rows(R, max(8, (2 ** 21) // (C * 4)))

    def body(w_ref, g_ref, m_ref, v_ref, d_ref, mo_ref, vo_ref):
        g_v = g_ref[...]
        m_v = ADAM_B1 * m_ref[...] + (1.0 - ADAM_B1) * g_v
        v_v = ADAM_B2 * v_ref[...] + (1.0 - ADAM_B2) * (g_v * g_v)
        m_hat = m_v / (1.0 - ADAM_B1 ** ADAM_STEP)
        v_hat = v_v / (1.0 - ADAM_B2 ** ADAM_STEP)
        d_ref[...] = -ADAM_LR * (m_hat / (jnp.sqrt(v_hat) + ADAM_EPS) + ADAM_WD * w_ref[...])
        mo_ref[...] = m_v
        vo_ref[...] = v_v

    blk = pl.BlockSpec((br, C), lambda i: (i, 0))
    return pl.pallas_call(
        body, name=name, grid=(R // br,),
        in_specs=[blk] * 4, out_specs=[blk] * 3,
        out_shape=[jax.ShapeDtypeStruct((R, C), F32)] * 3,
        compiler_params=_params(("parallel",)),
    )(w, g, m, v)


def _pick_rows(R, target):
    best = None
    for d in range(8, min(R, target) + 1, 8):
        if R % d == 0:
            best = d
    assert best is not None, (R, target)
    return best


def _position():
    return lax.axis_index("x"), lax.axis_index("y"), lax.axis_index("c")


HBM_SPEC = pl.BlockSpec(memory_space=pltpu.HBM)


def _other_chips(x, y):
    return [(1 - x, y), (x, 1 - y), (1 - x, 1 - y)]


def _all_gather_weights(bigs, small):
    nb = len(bigs)

    def body(*refs):
        big_refs, small_ref = refs[:nb], refs[nb]
        obig, osmall = refs[nb + 1:2 * nb + 1], refs[2 * nb + 1]
        ici_send, ici_recv, d2d_send, d2d_recv, sm_send, sm_recv = refs[2 * nb + 2:]
        x, y, c = _position()
        me = 2 * x + y
        chips = _other_chips(x, y)

        def rows(n, half):
            rh = bigs[n].shape[0] // 2
            return pl.ds(half * rh, rh)

        def over_ici(n, j, slot, from_shard):
            px, py = chips[j]
            dst = obig[n].at[slot, rows(n, c)]
            return pltpu.make_async_remote_copy(
                src_ref=big_refs[n].at[rows(n, c)] if from_shard else dst, dst_ref=dst,
                send_sem=ici_send.at[3 * n + j], recv_sem=ici_recv.at[3 * n + j],
                device_id=(px, py, c), device_id_type=MESH)

        def over_d2d(n, j, half):
            px, py = chips[j]
            part = obig[n].at[2 * px + py, rows(n, half)]
            return pltpu.make_async_remote_copy(
                src_ref=part, dst_ref=part, send_sem=d2d_send.at[3 * n + j], recv_sem=d2d_recv.at[3 * n + j],
                device_id=(x, y, 1 - c), device_id_type=MESH)

        def small_copy(j, slot):
            px, py = chips[j]
            return pltpu.make_async_remote_copy(
                src_ref=small_ref, dst_ref=osmall.at[slot], send_sem=sm_send.at[j], recv_sem=sm_recv.at[j],
                device_id=(px, py, c), device_id_type=MESH)

        sends = [over_ici(n, j, me, True) for n in range(nb) for j in range(3)]
        sends += [small_copy(j, me) for j in range(3)]
        for cp in sends:
            cp.start()
        passed = []
        for n in range(nb):
            for j, (px, py) in enumerate(chips):
                over_ici(n, j, 2 * px + py, False).wait_recv()
                fwd = over_d2d(n, j, c)
                fwd.start()
                passed.append(fwd)
        for n in range(nb):
            for j in range(3):
                over_d2d(n, j, 1 - c).wait_recv()
        for j, (px, py) in enumerate(chips):
            small_copy(j, 2 * px + py).wait_recv()
        for cp in sends + passed:
            cp.wait_send()

    dma = pltpu.SemaphoreType.DMA
    return pl.pallas_call(
        body, name="weights_all_gather",
        in_specs=[HBM_SPEC] * (nb + 1), out_specs=[HBM_SPEC] * (nb + 1),
        out_shape=[jax.ShapeDtypeStruct((N_SHARD,) + b.shape, b.dtype) for b in bigs]
        + [jax.ShapeDtypeStruct((N_SHARD,) + small.shape, small.dtype)],
        scratch_shapes=[dma((3 * nb,)), dma((3 * nb,)), dma((3 * nb,)), dma((3 * nb,)), dma((3,)), dma((3,))],
    )(*bigs, small)


SEM_SPEC = pl.BlockSpec(memory_space=pltpu.SEMAPHORE)
DATAFLOW_EFFECT = pltpu.SideEffectType.DATAFLOW_SIDE_EFFECTING
N_PEERS = N_DEV - 1


def _grad_copies(p_refs, land_refs, send_sems, recv_sems):
    x, y, c = _position()
    copies = []
    for a, (p_ref, land_ref) in enumerate(zip(p_refs, land_refs)):
        rh = p_ref.shape[1] // 2
        for k in range(1, N_DEV):
            px = 1 - x if k & 4 else x
            py = 1 - y if k & 2 else y
            pc = 1 - c if k & 1 else c
            copies.append(pltpu.make_async_remote_copy(
                src_ref=p_ref.at[2 * px + py, pl.ds(pc * rh, rh)], dst_ref=land_ref.at[k - 1],
                send_sem=send_sems.at[N_PEERS * a + k - 1], recv_sem=recv_sems.at[N_PEERS * a + k - 1],
                device_id=(px, py, pc), device_id_type=MESH))
    return copies


def _weight_copies(w_refs, land_refs, send_sems, recv_sems):
    x, y, c = _position()
    copies = []
    for a, (w_ref, land_ref) in enumerate(zip(w_refs, land_refs)):
        for j, (px, py) in enumerate(_other_chips(x, y)):
            copies.append(pltpu.make_async_remote_copy(
                src_ref=w_ref, dst_ref=land_ref.at[2 * x + y], send_sem=send_sems.at[3 * a + j],
                recv_sem=recv_sems.at[3 * a + j], device_id=(px, py, c), device_id_type=MESH))
    return copies


def _exchange_start(make_copies, srcs, lands, n_sems, name, after=None):
    n, m = len(srcs), len(lands)
    n_in = n + m + (after is not None)

    def body(*refs):
        send_sems, recv_sems, token = refs[n_in], refs[n_in + 1], refs[-1]
        for cp in make_copies(refs[:n], refs[n:n + m], send_sems, recv_sems):
            cp.start()
        token[...] = jnp.zeros(token.shape, token.dtype)

    hbm = lambda a: pltpu.with_memory_space_constraint(a, pltpu.HBM)
    dma = pltpu.SemaphoreType.DMA
    res = pl.pallas_call(
        body, name=name,
        in_specs=[HBM_SPEC] * (n + m) + ([] if after is None else [pl.BlockSpec(memory_space=pl.ANY)]),
        out_specs=[SEM_SPEC, SEM_SPEC] + [HBM_SPEC] * (n + m) + [pl.BlockSpec(memory_space=pltpu.VMEM)],
        out_shape=[dma((n_sems,)), dma((n_sems,))] + [pltpu.HBM(a.shape, a.dtype) for a in list(srcs) + list(lands)]
        + [jax.ShapeDtypeStruct((8, LANES), F32)],
        input_output_aliases={i: 2 + i for i in range(n + m)},
        compiler_params=pltpu.CompilerParams(has_side_effects=DATAFLOW_EFFECT),
    )(*[hbm(a) for a in srcs], *[hbm(a) for a in lands], *(() if after is None else (after,)))
    return res[0], res[1], list(res[2:2 + n]), list(res[2 + n:2 + n + m]), res[-1]


def _exchange_wait(make_copies, send_sems, recv_sems, srcs, lands, after, name):
    n, m = len(srcs), len(lands)

    def body(*refs):
        for cp in make_copies(refs[:n], refs[n:n + m], refs[n + m], refs[n + m + 1]):
            cp.wait_send()
            cp.wait_recv()

    res = pl.pallas_call(
        body, name=name,
        in_specs=[HBM_SPEC] * (n + m) + [SEM_SPEC, SEM_SPEC, pl.BlockSpec(memory_space=pl.ANY)],
        out_specs=[HBM_SPEC] * (n + m),
        out_shape=[pltpu.HBM(a.shape, a.dtype) for a in list(srcs) + list(lands)],
        input_output_aliases={i: i for i in range(n + m)},
        compiler_params=pltpu.CompilerParams(has_side_effects=DATAFLOW_EFFECT),
    )(*srcs, *lands, send_sems, recv_sems, after)
    return list(res[:n]), list(res[n:])


def _sum_partials(p, land, name):
    _, rh, cols = land.shape
    br = _pick_rows(rh, 256)
    nrb = rh // br
    x, y, c = _position()
    where = jnp.stack([2 * x + y, c]).astype(jnp.int32)

    def body(where_ref, p_ref, land_ref, o_ref):
        acc = p_ref[...].astype(F32)
        for k in range(N_PEERS):
            acc = acc + land_ref[k].astype(F32)
        o_ref[...] = acc

    return pl.pallas_call(
        body, name=name,
        grid_spec=pltpu.PrefetchScalarGridSpec(
            num_scalar_prefetch=1, grid=(nrb,),
            in_specs=[pl.BlockSpec((None, br, cols), lambda r, where_ref: (where_ref[0], where_ref[1] * nrb + r, 0)),
                      pl.BlockSpec((N_PEERS, br, cols), lambda r, where_ref: (0, r, 0))],
            out_specs=pl.BlockSpec((None, br, cols), lambda r, where_ref: (where_ref[1], r, 0))),
        out_shape=jax.ShapeDtypeStruct((2, rh, cols), F32),
        compiler_params=_params(("parallel",)),
    )(where, p, land)


def _sibling_share(fulls, name):
    n = len(fulls)

    def body(*refs):
        o_refs = refs[n:2 * n]
        send_sems, recv_sems = refs[2 * n:]
        x, y, c = _position()

        def copy(a, half):
            return pltpu.make_async_remote_copy(
                src_ref=o_refs[a].at[half], dst_ref=o_refs[a].at[half], send_sem=send_sems.at[a],
                recv_sem=recv_sems.at[a], device_id=(x, y, 1 - c), device_id_type=MESH)

        sends = [copy(a, c) for a in range(n)]
        for cp in sends:
            cp.start()
        for a in range(n):
            copy(a, 1 - c).wait_recv()
        for cp in sends:
            cp.wait_send()

    dma = pltpu.SemaphoreType.DMA
    return pl.pallas_call(
        body, name=name,
        in_specs=[HBM_SPEC] * n, out_specs=[HBM_SPEC] * n,
        out_shape=[jax.ShapeDtypeStruct(f.shape, f.dtype) for f in fulls],
        input_output_aliases={a: a for a in range(n)},
        scratch_shapes=[dma((n,)), dma((n,))],
    )(*fulls)


def _all_reduce_small(v):
    R, cols = v.shape

    def body(v_ref, o_ref, buf_ref, send_sems, recv_sems):
        x, y, c = _position()
        me = 4 * x + 2 * y + c
        buf_ref[me] = v_ref[...]
        sends = []
        for k in range(1, N_DEV):
            px = 1 - x if k & 4 else x
            py = 1 - y if k & 2 else y
            pc = 1 - c if k & 1 else c
            sends.append(pltpu.make_async_remote_copy(
                src_ref=v_ref, dst_ref=buf_ref.at[me], send_sem=send_sems.at[k - 1], recv_sem=recv_sems.at[k - 1],
                device_id=(px, py, pc), device_id_type=MESH))
        for cp in sends:
            cp.start()
        for k in range(1, N_DEV):
            px = 1 - x if k & 4 else x
            py = 1 - y if k & 2 else y
            pc = 1 - c if k & 1 else c
            pltpu.make_async_remote_copy(
                src_ref=v_ref, dst_ref=buf_ref.at[4 * px + 2 * py + pc], send_sem=send_sems.at[k - 1],
                recv_sem=recv_sems.at[k - 1], device_id=(px, py, pc), device_id_type=MESH).wait_recv()
        for cp in sends:
            cp.wait_send()
        acc = buf_ref[0]
        for d in range(1, N_DEV):
            acc = acc + buf_ref[d]
        o_ref[...] = acc

    return pl.pallas_call(
        body, name="small_grads_all_reduce",
        in_specs=[pl.BlockSpec(memory_space=pltpu.VMEM)], out_specs=pl.BlockSpec(memory_space=pltpu.VMEM),
        out_shape=jax.ShapeDtypeStruct((R, cols), F32),
        scratch_shapes=[pltpu.VMEM((N_DEV, R, cols), F32), pltpu.SemaphoreType.DMA((N_DEV - 1,)),
                        pltpu.SemaphoreType.DMA((N_DEV - 1,))],
    )(v)


def _rope_tables(S, half, width):
    inv_freq = ROPE_THETA ** (-jnp.arange(half, dtype=F32) / half)
    ang = jnp.arange(S).astype(F32)[:, None] * inv_freq[None, :]
    return jnp.cos(ang), jnp.sin(ang)


def _slot_rows(a):
    return a.reshape(N_SHARD, -1, a.shape[-1])


def _local_step(x, target, w, B, S, late, exchange):
    T = B * S
    D = D_MODEL
    bm = 256
    full = lambda a, wd, tile=None: (a, wd, 0, tile or wd)
    g = {}

    cos_r, sin_r = _rope_tables(S, RET_QK // 2, LANES)
    cos_m, sin_m = _rope_tables(S, MLA_ROPE // 2, LANES)
    zeros64 = jnp.zeros((S, 64), F32)
    cos_m = jnp.concatenate([cos_m, cos_m, zeros64], axis=1)
    sin_m = jnp.concatenate([-sin_m, sin_m, zeros64], axis=1)

    def ffn_fwd(xin, i):
        w.update(late(f"ffn{i}", xin))
        norm = w["ffn_norm"][i:i + 1]
        h, ht = _rowwise_fwd(_fn_rms, f"ffn{i}_norm", [full(xin, D)], [], [(norm, D)], [(D, D, BF16)], bm, S,
                             transposed=(0,))
        ag = _mm(h, w[f"ffn_w_in{i}"], "nn", F32, f"ffn{i}_in", bn=1408)
        u, ut = _conv_fwd(ag, w["ffn_conv8"][i], B, S, f"ffn{i}_conv")
        xout = _mm(u, w[f"ffn_w_out{i}"], "nn", F32, f"ffn{i}_out", residual=xin, bk=1408)
        return xout, (xin, norm, ht, ag, ut)

    def ffn_bwd(dxout, dxout_c, saved, i):
        xin, norm, ht, ag, ut = saved
        du = _mm(dxout_c, w[f"ffn_w_out{i}"], "nt", F32, f"ffn{i}_out_dx", bn=1408)
        g_w_out = _mm(ut, dxout_c, "nn", BF16, f"ffn{i}_out_dw", bm=1408, bn=512, bk=T)
        da, dg, dw8 = _conv_bwd(ag, w["ffn_conv8"][i], du, B, S, f"ffn{i}_conv_bwd")
        g_w_in = _mm(ht, [da, dg], "nn", BF16, f"ffn{i}_in_dw", bm=1024, bn=1408, bk=T // 2, out_slots=N_SHARD)
        token = exchange(f"ffn{i}", [g_w_in, _slot_rows(g_w_out)])
        dh = _mm([da, dg], w[f"ffn_w_in{i}"], "nt", F32, f"ffn{i}_in_dx", bk=1408, after=token)
        (dxin, dxin_c), (g_norm,) = _rowwise_bwd(_fn_rms, f"ffn{i}_norm_bwd", [full(xin, D)], [], [(norm, D)],
                                                 [(dh, D)], bm, S, adds={0: dxout}, mxu_copies=(0,))
        return dxin, dxin_c, (g_norm, dw8)

    h0, h0t = _rowwise_fwd(_fn_rms, "ret_norm", [full(x, D)], [], [(w["ret_norm"], D)], [(D, D, BF16)], bm, S,
                           transposed=(0,))
    proj = _mm(h0, w["ret_w_in"], "nn", F32, "ret_in", after=w["started"])
    HQ, HV = RET_HEADS * RET_QK, RET_HEADS * RET_V
    rope_rows = [(proj, 2 * HQ + HV, 0, LANES)]
    q_r, k_r, v_r = _rowwise_fwd(_fn_ret_rope, "ret_rope", rope_rows, [cos_r, sin_r], [],
                                 [(HQ, LANES, BF16), (HQ, LANES, BF16), (HV, LANES, BF16)], bm, S)
    ret_o = _ret_attn_fwd(q_r, k_r, v_r, B, S)
    gate_rows = [full(ret_o, HV, RET_V), (proj, HV, 2, RET_V)]
    y0, y0t = _rowwise_fwd(_fn_ret_gate, "ret_gate", gate_rows, [], [(w["ret_gn"], RET_V)], [(HV, RET_V, BF16)], 128, S,
                           transposed=(0,))
    w.update(late("ret_out", y0))
    x1 = _mm(y0, w["ret_w_out"], "nn", F32, "ret_out", residual=x)
    x2, ffn0_saved = ffn_fwd(x1, 0)

    w.update(late("mla", x2))
    (h2,) = _rowwise_fwd(_fn_rms, "mla_norm", [full(x2, D)], [], [(w["mla_norm"], D)], [(D, D, BF16)], bm, S)
    proj2 = _mm(h2, w["mla_w_in"], "nn", F32, "mla_in")
    lat_consts = [(w["mla_q_norm"], LANES), (w["mla_kv_norm"], LANES)]
    cqn, ckvn, kr = _rowwise_fwd(_fn_mla_lat, "mla_latent_norm", [full(proj2, MLA_IN_PAD, LANES)], [], lat_consts,
                                 [(MLA_Q_RANK, LANES, BF16), (MLA_KV_RANK, LANES, BF16), (LANES, LANES, F32)], bm, S)
    qf = _mm(cqn, w["mla_w_qb"], "nn", F32, "mla_qb")
    kvf = _mm(ckvn, w["mla_w_kvb"], "nn", F32, "mla_kvb")
    HP, HVm = MLA_HEADS * MLA_PAD, MLA_HEADS * MLA_V
    head_rows = [full(qf, HP, LANES), full(kvf, HP, LANES), full(kr, LANES)]
    head_consts = [(w["mla_q_head_norm"], LANES), (w["mla_k_head_norm"], LANES)]
    q_a, k_a, v_a = _rowwise_fwd(_fn_mla_heads, "mla_heads", head_rows, [cos_m, sin_m], head_consts,
                                 [(HP, LANES, BF16), (HP, LANES, BF16), (HVm, LANES, BF16)], bm, S)
    att_o, lse = _mla_attn_fwd(q_a, k_a, v_a, B, S)
    x3 = _mm(att_o, w["mla_w_out"], "nn", F32, "mla_out", residual=x2)
    x4, ffn1_saved = ffn_fwd(x3, 1)

    dy, dy_c, loss = _loss_head(x4, target)

    dx3, dx3_c, (g_n1, dw8_1) = ffn_bwd(dy, dy_c, ffn1_saved, 1)

    d_att_o = _mm(dx3_c, w["mla_w_out"], "nt", F32, "mla_out_dx")
    g_mla_out = _mm(att_o, dx3_c, "tn", BF16, "mla_out_dw")
    dq_a, dk_a, dv_a = _mla_attn_bwd(q_a, k_a, v_a, att_o, d_att_o, lse, B, S)
    (dqf, dkvf, dkr), (g["mla_q_head_norm"], g["mla_k_head_norm"]) = _rowwise_bwd(
        _fn_mla_heads, "mla_heads_bwd", head_rows, [cos_m, sin_m], head_consts,
        [(dq_a, LANES), (dk_a, LANES), (dv_a, LANES)], 128, S)
    dcqn = _mm(dqf, w["mla_w_qb"], "nt", F32, "mla_qb_dx")
    g_qb = _mm(cqn, dqf, "tn", BF16, "mla_qb_dw")
    g_qb = _to_slots(_unpad_heads(g_qb, 1), 1).reshape(N_SHARD, MLA_Q_RANK, -1)
    dckvn = _mm(dkvf, w["mla_w_kvb"], "nt", F32, "mla_kvb_dx")
    g_kvb = _mm(ckvn, dkvf, "tn", BF16, "mla_kvb_dw", bn=512, out_slots=N_SHARD)
    (dproj2,), (g["mla_q_norm"], g["mla_kv_norm"]) = _rowwise_bwd(
        _fn_mla_lat, "mla_latent_norm_bwd", [full(proj2, MLA_IN_PAD, LANES)], [], lat_consts,
        [(dcqn, LANES), (dckvn, LANES), (dkr, LANES)], bm, S)
    g_mla_in = _mm(h2, dproj2, "tn", BF16, "mla_in_dw")
    token = exchange("mla", [_slot_rows(g_mla_in[:, :MLA_IN]), g_qb, g_kvb, _slot_rows(g_mla_out)])
    dh2 = _mm(dproj2, w["mla_w_in"], "nt", F32, "mla_in_dx", after=token)
    (dx2, dx2_c), (g["mla_norm"],) = _rowwise_bwd(_fn_rms, "mla_norm_bwd", [full(x2, D)], [], [(w["mla_norm"], D)],
                                                  [(dh2, D)], bm, S, adds={0: dx3}, mxu_copies=(0,))

    dx1, dx1_c, (g_n0, dw8_0) = ffn_bwd(dx2, dx2_c, ffn0_saved, 0)

    dy0 = _mm(dx1_c, w["ret_w_out"], "nt", F32, "ret_out_dx")
    g_ret_out = _mm(y0t, dx1_c, "nn", BF16, "ret_out_dw", bm=1024, bn=512, bk=T)
    (d_ret_o, dgate), (g["ret_gn"],) = _rowwise_bwd(_fn_ret_gate, "ret_gate_bwd", gate_rows, [], [(w["ret_gn"], RET_V)],
                                                    [(dy0, RET_V)], 128, S, grad_dtypes=[F32, BF16])
    dq_r, dk_r, dv_r = _ret_attn_bwd(q_r, k_r, v_r, d_ret_o, B, S)
    (dqkv,), _ = _rowwise_bwd(_fn_ret_rope, "ret_rope_bwd", rope_rows, [cos_r, sin_r], [],
                              [(dq_r, LANES), (dk_r, LANES), (dv_r, LANES)], bm, S, grad_dtypes=[BF16])
    g_ret_in = _mm(h0t, [dqkv, dgate], "nn", BF16, "ret_in_dw", bn=512, bk=T, out_slots=N_SHARD)
    token = exchange("ret", [g_ret_in, _slot_rows(g_ret_out)])
    dh0 = _mm([dqkv, dgate], w["ret_w_in"], "nt", F32, "ret_in_dx", bk=1024, after=token)
    (dx,), (g["ret_norm"],) = _rowwise_bwd(_fn_rms, "ret_norm_bwd", [full(x, D)], [], [(w["ret_norm"], D)],
                                           [(dh0, D)], bm, S, adds={0: dx1})

    g["ffn_norm"] = jnp.concatenate([g_n0, g_n1], axis=0)
    g["ffn_conv_w"] = jnp.stack([dw8_0[0:3], dw8_1[0:3]])
    g["ffn_conv_b"] = jnp.stack([dw8_0[3], dw8_1[3]])
    return loss, dx, g


_BIG = [("ret_w_in", 2), ("ret_w_out", 1), ("mla_w_in", 1), ("mla_w_qb", 2), ("mla_w_kvb", 2), ("mla_w_out", 1),
        ("ffn_w_in", 2), ("ffn_w_out", 1)]
_SMALL_SHARDED = [("ret_gn", 2), ("mla_norm", 1), ("mla_q_norm", 1), ("mla_kv_norm", 1), ("ffn_conv_w", 2)]
_SMALL_REPLICATED = ["ret_norm", "mla_q_head_norm", "mla_k_head_norm", "ffn_norm", "ffn_conv_b"]
_SMALL_ALL = ["ret_norm", "ret_gn", "mla_norm", "mla_q_norm", "mla_kv_norm", "mla_q_head_norm", "mla_k_head_norm",
              "ffn_norm", "ffn_conv_w", "ffn_conv_b"]


def _to_slots(full, axis):
    shape = full.shape
    split = shape[:axis] + (N_SHARD, shape[axis] // N_SHARD) + shape[axis + 1:]
    return jnp.moveaxis(full.reshape(split), axis, 0).reshape(N_SHARD, -1)


def _from_slots(slots, shard_shape, axis):
    parts = jnp.moveaxis(slots.reshape((N_SHARD,) + tuple(shard_shape)), 0, axis)
    full = shard_shape[:axis] + (N_SHARD * shard_shape[axis],) + shard_shape[axis + 1:]
    return parts.reshape(full)


def _pad_rows(flat, cols, row_unit):
    n, L = flat.shape
    unit = cols * row_unit
    Lp = -(-L // unit) * unit
    if Lp != L:
        flat = jnp.concatenate([flat, jnp.zeros((n, Lp - L), flat.dtype)], axis=1)
    return flat.reshape(n, Lp // cols, cols)


def _pad_heads(a, axis):
    shape = a.shape
    a = a.reshape(shape[:axis] + (MLA_HEADS, MLA_QK) + shape[axis + 1:])
    pad = [(0, 0)] * a.ndim
    pad[axis + 1] = (0, MLA_PAD - MLA_QK)
    return jnp.pad(a, pad).reshape(shape[:axis] + (MLA_HEADS * MLA_PAD,) + shape[axis + 1:])


def _unpad_heads(a, axis):
    shape = a.shape
    a = a.reshape(shape[:axis] + (MLA_HEADS, MLA_PAD) + shape[axis + 1:])
    a = lax.slice_in_dim(a, 0, MLA_QK, axis=axis + 1)
    return a.reshape(shape[:axis] + (MLA_HEADS * MLA_QK,) + shape[axis + 1:])


def kernel(x, ret_norm, ret_w_in, ret_gn, ret_w_out, mla_norm, mla_w_in, mla_q_norm, mla_w_qb, mla_kv_norm, mla_w_kvb, mla_q_head_norm, mla_k_head_norm, mla_w_out, ffn_norm, ffn_w_in, ffn_conv_w, ffn_conv_b, ffn_w_out, loss_target, m_ret_norm, m_ret_w_in, m_ret_gn, m_ret_w_out, m_mla_norm, m_mla_w_in, m_mla_q_norm, m_mla_w_qb, m_mla_kv_norm, m_mla_w_kvb, m_mla_q_head_norm, m_mla_k_head_norm, m_mla_w_out, m_ffn_norm, m_ffn_w_in, m_ffn_conv_w, m_ffn_conv_b, m_ffn_w_out, v_ret_norm, v_ret_w_in, v_ret_gn, v_ret_w_out, v_mla_norm, v_mla_w_in, v_mla_q_norm, v_mla_w_qb, v_mla_kv_norm, v_mla_w_kvb, v_mla_q_head_norm, v_mla_k_head_norm, v_mla_w_out, v_ffn_norm, v_ffn_w_in, v_ffn_conv_w, v_ffn_conv_b, v_ffn_w_out):
    names = ["ret_norm", "ret_w_in", "ret_gn", "ret_w_out", "mla_norm", "mla_w_in", "mla_q_norm", "mla_w_qb",
             "mla_kv_norm", "mla_w_kvb", "mla_q_head_norm", "mla_k_head_norm", "mla_w_out", "ffn_norm", "ffn_w_in",
             "ffn_conv_w", "ffn_conv_b", "ffn_w_out"]
    shard = dict(zip(names, [ret_norm, ret_w_in, ret_gn, ret_w_out, mla_norm, mla_w_in, mla_q_norm, mla_w_qb,
                             mla_kv_norm, mla_w_kvb, mla_q_head_norm, mla_k_head_norm, mla_w_out, ffn_norm, ffn_w_in,
                             ffn_conv_w, ffn_conv_b, ffn_w_out]))
    mom_m = dict(zip(names, [m_ret_norm, m_ret_w_in, m_ret_gn, m_ret_w_out, m_mla_norm, m_mla_w_in, m_mla_q_norm,
                             m_mla_w_qb, m_mla_kv_norm, m_mla_w_kvb, m_mla_q_head_norm, m_mla_k_head_norm, m_mla_w_out,
                             m_ffn_norm, m_ffn_w_in, m_ffn_conv_w, m_ffn_conv_b, m_ffn_w_out]))
    mom_v = dict(zip(names, [v_ret_norm, v_ret_w_in, v_ret_gn, v_ret_w_out, v_mla_norm, v_mla_w_in, v_mla_q_norm,
                             v_mla_w_qb, v_mla_kv_norm, v_mla_w_kvb, v_mla_q_head_norm, v_mla_k_head_norm, v_mla_w_out,
                             v_ffn_norm, v_ffn_w_in, v_ffn_conv_w, v_ffn_conv_b, v_ffn_w_out]))
    B, S, D = x.shape
    T = B * S
    sx, sy = lax.axis_index("x"), lax.axis_index("y")
    me = 2 * sx + sy

    two_d = lambda a: a.reshape(-1, a.shape[-1])
    small_sizes = [int(np.prod(shard[n].shape)) for n, _ in _SMALL_SHARDED]
    small = jnp.concatenate([shard[n].reshape(1, -1) for n, _ in _SMALL_SHARDED], axis=1)
    small = _pad_rows(small, LANES, 8)[0]
    as_mxu = lambda a: two_d(a).astype(BF16)
    is_me = lax.broadcasted_iota(jnp.int32, (N_SHARD, 1, 1), 0) == me
    with_own = lambda gathered, own: jnp.where(is_me, own[None], gathered)
    by_cols = lambda a: jnp.moveaxis(a, 0, 1).reshape(a.shape[1], -1)
    by_rows = lambda a: a.reshape(-1, a.shape[-1])
    pad_in = lambda a: jnp.pad(by_rows(a), ((0, 0), (0, MLA_IN_PAD - MLA_IN)))
    pad_qb = lambda a: _pad_heads(by_cols(a), 1)
    ret_in_shard = as_mxu(shard["ret_w_in"])
    g_ret_in, gsmall = _all_gather_weights([ret_in_shard], small)
    later = [
        ("ret_out", [("ret_w_out", as_mxu(shard["ret_w_out"]), by_rows)]),
        ("ffn0", [("ffn_w_in0", as_mxu(shard["ffn_w_in"][0]), by_cols), ("ffn_w_out0", as_mxu(shard["ffn_w_out"][0]), by_rows)]),
        ("mla", [("mla_w_in", as_mxu(shard["mla_w_in"]), pad_in), ("mla_w_qb", as_mxu(shard["mla_w_qb"]), pad_qb),
                 ("mla_w_kvb", as_mxu(shard["mla_w_kvb"]), by_cols), ("mla_w_out", as_mxu(shard["mla_w_out"]), by_rows)]),
        ("ffn1", [("ffn_w_in1", as_mxu(shard["ffn_w_in"][1]), by_cols), ("ffn_w_out1", as_mxu(shard["ffn_w_out"][1]), by_rows)]),
    ]
    gathering = {}
    token = gsmall
    for group, items in later:
        shards = [s_ for _, s_, _ in items]
        lands = [lax.empty((N_SHARD,) + s_.shape, s_.dtype) for s_ in shards]
        send_sems, recv_sems, shards, lands, token = _exchange_start(
            _weight_copies, shards, lands, 3 * len(shards), f"weights_start_{group}", after=token)
        gathering[group] = (send_sems, recv_sems, shards, lands, items)

    def late(group, after):
        send_sems, recv_sems, shards, lands, items = gathering[group]
        shards, lands = _exchange_wait(_weight_copies, send_sems, recv_sems, shards, lands, after,
                                       f"weights_wait_{group}")
        return {key: full(with_own(l_, s_)) for (key, _, full), s_, l_ in zip(items, shards, lands)}

    gsmall = with_own(gsmall, small).reshape(N_SHARD, -1)
    wfull = {}
    off = 0
    for (n, ax), sz in zip(_SMALL_SHARDED, small_sizes):
        wfull[n] = _from_slots(gsmall[:, off:off + sz], shard[n].shape, ax)
        off += sz
    for n in _SMALL_REPLICATED:
        wfull[n] = shard[n]

    conv8 = jnp.concatenate([wfull["ffn_conv_w"], wfull["ffn_conv_b"][:, None, :],
                             jnp.zeros((2, 4, FFN_DIM), F32)], axis=1)
    w = {
        "started": token, "ret_norm": wfull["ret_norm"], "ret_w_in": by_cols(with_own(g_ret_in, ret_in_shard)),
        "ret_gn": wfull["ret_gn"].reshape(1, RET_HEADS * RET_V), "mla_norm": wfull["mla_norm"],
        "mla_q_norm": wfull["mla_q_norm"], "mla_kv_norm": wfull["mla_kv_norm"],
        "mla_q_head_norm": jnp.pad(wfull["mla_q_head_norm"], ((0, 0), (0, MLA_PAD - MLA_QK))),
        "mla_k_head_norm": jnp.pad(wfull["mla_k_head_norm"], ((0, 0), (0, MLA_PAD - MLA_QK))),
        "ffn_norm": wfull["ffn_norm"], "ffn_conv8": conv8,
    }

    started = {}

    def exchange(group, arrays):
        lands = [lax.empty((N_PEERS, p.shape[1] // 2, p.shape[2]), p.dtype) for p in arrays]
        send_sems, recv_sems, ps, lands, token = _exchange_start(
            _grad_copies, arrays, lands, N_PEERS * len(arrays), f"grads_start_{group}")
        started[group] = (send_sems, recv_sems, ps, lands)
        return token

    loss_part, dx, gl = _local_step(x.reshape(T, D), loss_target.reshape(T, D), w, B, S, late, exchange)
    loss = lax.psum(loss_part, ("x", "y", "c"))
    gfull = {
        "ret_norm": gl["ret_norm"], "ret_gn": gl["ret_gn"].reshape(1, RET_HEADS, RET_V),
        "mla_norm": gl["mla_norm"], "mla_q_norm": gl["mla_q_norm"], "mla_kv_norm": gl["mla_kv_norm"],
        "mla_q_head_norm": gl["mla_q_head_norm"][:, :MLA_QK], "mla_k_head_norm": gl["mla_k_head_norm"][:, :MLA_QK],
        "ffn_norm": gl["ffn_norm"], "ffn_conv_w": gl["ffn_conv_w"], "ffn_conv_b": gl["ffn_conv_b"],
    }

    red = {}
    after = dx
    for group in ("ffn1", "mla", "ffn0", "ret"):
        send_sems, recv_sems, ps, lands = started[group]
        ps, lands = _exchange_wait(_grad_copies, send_sems, recv_sems, ps, lands, after, f"grads_wait_{group}")
        halves = [_sum_partials(p_, l_, f"grads_sum_{group}_{i}") for i, (p_, l_) in enumerate(zip(ps, lands))]
        red[group] = [two_d(r) for r in _sibling_share(halves, f"grads_share_{group}")]
        after = red[group][0]
    grads = {"ret_w_in": red["ret"][0], "ret_w_out": red["ret"][1], "mla_w_in": red["mla"][0],
             "mla_w_qb": red["mla"][1], "mla_w_kvb": red["mla"][2], "mla_w_out": red["mla"][3]}
    grads = {n: a.reshape(shard[n].shape) for n, a in grads.items()}
    grads["ffn_w_in"] = jnp.stack([red["ffn0"][0], red["ffn1"][0]])
    grads["ffn_w_out"] = jnp.stack([red["ffn0"][1], red["ffn1"][1]])

    small_sizes_all = [int(np.prod(gfull[n].shape)) for n in _SMALL_ALL]
    gsm = jnp.concatenate([gfull[n].reshape(1, -1) for n in _SMALL_ALL], axis=1)
    gsm = _all_reduce_small(_pad_rows(gsm, LANES, 8)[0]).reshape(-1)

    sharded_axis = dict(_SMALL_SHARDED)
    off = 0
    for n, sz in zip(_SMALL_ALL, small_sizes_all):
        gn = gsm[off:off + sz].reshape(gfull[n].shape)
        off += sz
        if n in sharded_axis:
            ax = sharded_axis[n]
            width = shard[n].shape[ax]
            gn = lax.dynamic_slice_in_dim(gn, me * width, width, axis=ax)
        grads[n] = gn

    delta, new_m, new_v = {}, {}, {}
    for n, _ in _BIG:
        shp = shard[n].shape
        two_d = lambda a: a.reshape(-1, shp[-1])
        d_, m_, v_ = _adamw(two_d(shard[n]), two_d(grads[n]), two_d(mom_m[n]), two_d(mom_v[n]), f"adamw_{n}")
        delta[n], new_m[n], new_v[n] = d_.reshape(shp), m_.reshape(shp), v_.reshape(shp)
    pack_small = lambda d: _pad_rows(jnp.concatenate([d[n].reshape(1, -1) for n in _SMALL_ALL], axis=1), LANES, 8)[0]
    d_, m_, v_ = _adamw(pack_small(shard), pack_small(grads), pack_small(mom_m), pack_small(mom_v), "adamw_small")
    off = 0
    for n in _SMALL_ALL:
        sz = int(np.prod(shard[n].shape))
        for dst, src in ((delta, d_), (new_m, m_), (new_v, v_)):
            dst[n] = src.reshape(-1)[off:off + sz].reshape(shard[n].shape)
        off += sz

    return (loss, dx.reshape(B, S, D), *[grads[n] for n in names], *[delta[n] for n in names],
            *[new_m[n] for n in names], *[new_v[n] for n in names])
```

```python
import functools
import math

import numpy as np
import jax
import jax.numpy as jnp
from jax import lax
from jax.experimental import pallas as pl
from jax.experimental.pallas import tpu as pltpu

F32 = jnp.float32
BF16 = jnp.bfloat16
MXU_DTYPE = jnp.bfloat16

CHUNK = 64
RMS_EPS = 1e-6
ROPE_THETA = 10000.0
D_MODEL = 1024
RET_HEADS = 4
RET_QK = 256
RET_V = 512
RET_GAMMA_BASE = -5.0
MLA_HEADS = 8
MLA_Q_RANK = 384
MLA_KV_RANK = 256
MLA_NOPE = 128
MLA_ROPE = 64
MLA_V = 128
MLA_QK = MLA_NOPE + MLA_ROPE
MLA_PAD = 256
MLA_IN = MLA_Q_RANK + MLA_KV_RANK + MLA_ROPE
MLA_IN_PAD = MLA_IN + 64
MASK_VALUE = -1e30
FFN_DIM = 2816
ADAM_LR = 0.001
ADAM_B1 = 0.9
ADAM_B2 = 0.999
ADAM_EPS = 1e-08
ADAM_WD = 0.01
ADAM_STEP = 10

LANES = 128
ATT_BLOCK = 256
MLA_FWD_BLOCK = 512
VMEM_LIMIT = 56 * 2 ** 20
N_SHARD = 4
N_DEV = 8

MESH = pl.DeviceIdType.MESH


def _params(sem=None, **kw):
    return pltpu.CompilerParams(dimension_semantics=sem, vmem_limit_bytes=VMEM_LIMIT, **kw)


def _pick(dim, target):
    if dim <= target:
        return dim
    best = None
    for d in range(LANES, target + 1, LANES):
        if dim % d == 0:
            best = d
    assert best is not None, (dim, target)
    return best


def _mm(a, b, dims, out_dtype, name, residual=None, bm=512, bn=1024, bk=2048, out_slots=None, after=None,
        cols_outer=False):
    a_parts = list(a) if isinstance(a, (list, tuple)) else [a]
    b_parts = list(b) if isinstance(b, (list, tuple)) else [b]
    parts_on_n = dims == "tn" or len(b_parts) > 1
    if parts_on_n:
        assert len(a_parts) == 1 and dims in ("tn", "nn")
        (K, M) = a_parts[0].shape if dims == "tn" else a_parts[0].shape[::-1]
        N = sum(p.shape[1] for p in b_parts)
        part_widths = [p.shape[1] for p in b_parts]
    else:
        assert len(b_parts) == 1
        M = a_parts[0].shape[0]
        K = sum(p.shape[1] for p in a_parts)
        N = b_parts[0].shape[1 if dims == "nn" else 0]
        part_widths = [p.shape[1] for p in a_parts]
    bm, bn, bk = _pick(M, bm), _pick(N, bn), _pick(K, min(bk, 1024) if dims == "tn" else bk)
    nk = K // bk
    unit = bn if parts_on_n else bk
    assert all(wd % unit == 0 for wd in part_widths), (name, part_widths, unit)
    bounds = np.cumsum([0] + [wd // unit for wd in part_widths])
    ranges = [(int(lo), int(hi)) for lo, hi in zip(bounds[:-1], bounds[1:])]

    def part_index(idx, lo, hi):
        return jnp.clip(idx - lo, 0, hi - lo - 1)

    if parts_on_n:
        if dims == "tn":
            a_specs = [pl.BlockSpec((bk, bm), lambda i, j, k: (k, i))]
            dn = (((0,), (0,)), ((), ()))
        else:
            a_specs = [pl.BlockSpec((bm, bk), lambda i, j, k: (i, k))]
            dn = (((1,), (0,)), ((), ()))
        b_specs = [pl.BlockSpec((bk, bn), functools.partial(lambda i, j, k, lo, hi: (k, part_index(j, lo, hi)), lo=lo, hi=hi))
                   for lo, hi in ranges]
    else:
        a_specs = [pl.BlockSpec((bm, bk), functools.partial(lambda i, j, k, lo, hi: (i, part_index(k, lo, hi)), lo=lo, hi=hi))
                   for lo, hi in ranges]
        if dims == "nt":
            b_specs = [pl.BlockSpec((bn, bk), lambda i, j, k: (j, k))]
        else:
            b_specs = [pl.BlockSpec((bk, bn), lambda i, j, k: (k, j))]
        dn = (((1,), (1 if dims == "nt" else 0,)), ((), ()))
    r_spec = pl.BlockSpec((bm, bn), lambda i, j, k: (i, j))
    if out_slots is None:
        o_spec, o_shape = r_spec, (M, N)
    else:
        ns = N // out_slots
        assert ns % bn == 0, (name, ns, bn)
        nbs = ns // bn
        o_spec = pl.BlockSpec((None, bm, bn), lambda i, j, k: (j // nbs, i, j % nbs))
        o_shape = (out_slots, M, ns)
    has_res = residual is not None
    na, nb = len(a_parts), len(b_parts)

    def body(*refs):
        a_refs, b_refs = refs[:na], refs[na:na + nb]
        r_ref = refs[na + nb] if has_res else None
        n_in = na + nb + has_res + (after is not None)
        o_ref = refs[n_in]
        acc_ref = refs[n_in + 1] if nk > 1 else None
        k = pl.program_id(2)

        def finish(acc):
            if has_res:
                acc = acc + r_ref[...].astype(F32)
            o_ref[...] = acc.astype(out_dtype)

        def compute(a_ref, b_ref):
            p = lax.dot_general(a_ref[...].astype(MXU_DTYPE), b_ref[...].astype(MXU_DTYPE), dn,
                                preferred_element_type=F32)
            if nk == 1:
                finish(p)
                return

            @pl.when(k == 0)
            def _():
                acc_ref[...] = p

            @pl.when(jnp.logical_and(k > 0, k < nk - 1))
            def _():
                acc_ref[...] += p

            @pl.when(k == nk - 1)
            def _():
                finish(acc_ref[...] + p)

        if len(ranges) == 1:
            compute(a_refs[0], b_refs[0])
        else:
            idx = pl.program_id(0 if cols_outer else 1) if parts_on_n else k
            for p, (lo, hi) in enumerate(ranges):
                @pl.when(jnp.logical_and(idx >= lo, idx < hi))
                def _(p=p):
                    compute(a_refs[0 if parts_on_n else p], b_refs[p if parts_on_n else 0])

    after_specs = [] if after is None else [pl.BlockSpec(after.shape, lambda i, j, k: (0, 0))]
    in_specs = a_specs + b_specs + ([r_spec] if has_res else []) + after_specs
    grid = (M // bm, N // bn, nk)
    if cols_outer:
        swap = lambda sp: pl.BlockSpec(sp.block_shape, functools.partial(lambda j, i, k, f: f(i, j, k), f=sp.index_map))
        in_specs, o_spec, grid = [swap(sp) for sp in in_specs], swap(o_spec), (grid[1], grid[0], nk)
    return pl.pallas_call(
        body, name=name, grid=grid,
        in_specs=in_specs, out_specs=o_spec,
        out_shape=jax.ShapeDtypeStruct(o_shape, out_dtype),
        scratch_shapes=[pltpu.VMEM((bm, bn), F32)] if nk > 1 else [],
        compiler_params=_params(("parallel", "parallel", "arbitrary")),
    )(*a_parts, *b_parts, *((residual,) if has_res else ()), *(() if after is None else (after,)))


def _mm_dx(a_parts, w, name, bm=512, after=None):
    M = a_parts[0].shape[0]
    N, K = w.shape
    widths = [p.shape[1] for p in a_parts]
    assert sum(widths) == K, (name, widths, K)
    offs = [int(o) for o in np.cumsum([0] + widths[:-1])]
    bm = _pick(M, bm)
    na = len(a_parts)

    def body(*refs):
        w_ref = refs[na]
        o_ref = refs[na + 1 + (after is not None)]
        acc = None
        for a_ref, off, wd in zip(refs[:na], offs, widths):
            p = lax.dot_general(a_ref[...].astype(MXU_DTYPE), w_ref[:, off:off + wd].astype(MXU_DTYPE), _NT,
                                preferred_element_type=F32)
            acc = p if acc is None else acc + p
        o_ref[...] = acc

    in_specs = [pl.BlockSpec((bm, wd), lambda i: (i, 0)) for wd in widths] + [pl.BlockSpec((N, K), lambda i: (0, 0))]
    in_specs += [] if after is None else [pl.BlockSpec(after.shape, lambda i: (0, 0))]
    return pl.pallas_call(
        body, name=name, grid=(M // bm,),
        in_specs=in_specs, out_specs=pl.BlockSpec((bm, N), lambda i: (i, 0)),
        out_shape=jax.ShapeDtypeStruct((M, N), F32),
        compiler_params=_params(("parallel",)),
    )(*a_parts, w, *(() if after is None else (after,)))


def _tiles(ref, width, tile):
    return [ref[:, t * tile:(t + 1) * tile].astype(F32) for t in range(width // tile)]


def _row_specs(rows, pos, consts, bm, S):
    npos_blocks = S // bm
    specs = [pl.BlockSpec((bm, w), functools.partial(lambda i, c: (i, c), c=cb)) for (_, w, cb, _) in rows]
    specs += [pl.BlockSpec((bm, p.shape[1]), lambda i: (i % npos_blocks, 0)) for p in pos]
    specs += [pl.BlockSpec(c.shape, lambda i: (0, 0)) for (c, _) in consts]
    return specs


def _rowwise_fwd(fn, name, rows, pos, consts, outs, bm, S, transposed=()):
    T = rows[0][0].shape[0]
    nr, npos, nc, no = len(rows), len(pos), len(consts), len(outs)

    def body(*refs):
        row_v = [_tiles(r, w, t) for r, (_, w, _, t) in zip(refs[:nr], rows)]
        pos_v = [r[...] for r in refs[nr:nr + npos]]
        const_v = [_tiles(r, c.shape[1], t) for r, (c, t) in zip(refs[nr + npos:nr + npos + nc], consts)]
        res = fn(row_v, pos_v, const_v)
        out_refs = refs[nr + npos + nc:]
        for o_ref, tiles, (w, t, dt) in zip(out_refs, res, outs):
            for k, v in enumerate(tiles):
                o_ref[:, k * t:(k + 1) * t] = v.astype(dt)
        for t_ref, a in zip(out_refs[no:], transposed):
            t = outs[a][1]
            for k, v in enumerate(res[a]):
                t_ref[k * t:(k + 1) * t, :] = v.T.astype(t_ref.dtype)

    return pl.pallas_call(
        body, name=name, grid=(T // bm,),
        in_specs=_row_specs(rows, pos, consts, bm, S),
        out_specs=[pl.BlockSpec((bm, w), lambda i: (i, 0)) for (w, _, _) in outs]
        + [pl.BlockSpec((outs[a][0], bm), lambda i: (0, i)) for a in transposed],
        out_shape=[jax.ShapeDtypeStruct((T, w), dt) for (w, _, dt) in outs]
        + [jax.ShapeDtypeStruct((outs[a][0], T), BF16) for a in transposed],
        compiler_params=_params(("parallel",)),
    )(*[r[0] for r in rows], *pos, *[c[0] for c in consts])


def _rowwise_bwd(fn, name, rows, pos, consts, cts, bm, S, adds=None, grad_dtypes=None, mxu_copies=()):
    adds = adds or {}
    T = rows[0][0].shape[0]
    nr, npos, nc, nct = len(rows), len(pos), len(consts), len(cts)
    add_idx = sorted(adds)
    grad_dtypes = grad_dtypes or [F32] * nr

    def body(*refs):
        it = iter(refs)
        row_refs = [next(it) for _ in range(nr)]
        pos_refs = [next(it) for _ in range(npos)]
        const_refs = [next(it) for _ in range(nc)]
        ct_refs = [next(it) for _ in range(nct)]
        add_refs = {k: next(it) for k in add_idx}
        drow_refs = [next(it) for _ in range(nr)]
        copy_refs = {a: next(it) for a in mxu_copies}
        dconst_refs = [next(it) for _ in range(nc)]
        row_v = [_tiles(r, w, t) for r, (_, w, _, t) in zip(row_refs, rows)]
        pos_v = [r[...] for r in pos_refs]
        const_v = [_tiles(r, c.shape[1], t) for r, (c, t) in zip(const_refs, consts)]
        ct_v = [_tiles(r, c.shape[1], t) for r, (c, t) in zip(ct_refs, cts)]
        _, vjp = jax.vjp(lambda rv, cv: fn(rv, pos_v, cv), row_v, const_v)
        drows, dconsts = vjp(ct_v)
        for a, (d_ref, tiles, (_, w, _, t)) in enumerate(zip(drow_refs, drows, rows)):
            for k, v in enumerate(tiles):
                if a in add_refs:
                    v = v + add_refs[a][:, k * t:(k + 1) * t].astype(F32)
                d_ref[:, k * t:(k + 1) * t] = v.astype(d_ref.dtype)
                if a in copy_refs:
                    copy_refs[a][:, k * t:(k + 1) * t] = v.astype(BF16)
        first = pl.program_id(0) == 0
        for d_ref, tiles, (_, t) in zip(dconst_refs, dconsts, consts):
            for k, v in enumerate(tiles):
                @pl.when(first)
                def _(d_ref=d_ref, k=k, t=t, v=v):
                    d_ref[:, k * t:(k + 1) * t] = v

                @pl.when(jnp.logical_not(first))
                def _(d_ref=d_ref, k=k, t=t, v=v):
                    d_ref[:, k * t:(k + 1) * t] += v

    in_specs = _row_specs(rows, pos, consts, bm, S)
    in_specs += [pl.BlockSpec((bm, c.shape[1]), lambda i: (i, 0)) for (c, _) in cts]
    in_specs += [pl.BlockSpec((bm, adds[k].shape[1]), lambda i: (i, 0)) for k in add_idx]
    out_specs = [pl.BlockSpec((bm, w), lambda i: (i, 0)) for (_, w, _, _) in rows]
    out_specs += [pl.BlockSpec((bm, rows[a][1]), lambda i: (i, 0)) for a in mxu_copies]
    out_specs += [pl.BlockSpec(c.shape, lambda i: (0, 0)) for (c, _) in consts]
    out_shape = [jax.ShapeDtypeStruct((T, w), dt) for (_, w, _, _), dt in zip(rows, grad_dtypes)]
    out_shape += [jax.ShapeDtypeStruct((T, rows[a][1]), BF16) for a in mxu_copies]
    out_shape += [jax.ShapeDtypeStruct(c.shape, F32) for (c, _) in consts]
    res = pl.pallas_call(
        body, name=name, grid=(T // bm,),
        in_specs=in_specs, out_specs=out_specs, out_shape=out_shape,
        compiler_params=_params(("arbitrary",)),
    )(*[r[0] for r in rows], *pos, *[c[0] for c in consts], *[c[0] for c in cts], *[adds[k] for k in add_idx])
    n_rows = nr + len(mxu_copies)
    return res[:n_rows], res[n_rows:]


def _ssq(tiles):
    s = jnp.sum(tiles[0] * tiles[0], axis=-1, keepdims=True)
    for t in tiles[1:]:
        s = s + jnp.sum(t * t, axis=-1, keepdims=True)
    return s


def _sigmoid(x):
    return 1.0 / (1.0 + jnp.exp(-x))


def _fn_rms(rows, pos, consts):
    (x,), (g,) = rows[0], consts[0]
    r = lax.rsqrt(jnp.mean(x * x, axis=-1, keepdims=True) + RMS_EPS)
    return [[x * r * g]]


def _fn_ret_rope(rows, pos, consts):
    (qkv,) = rows
    nq = RET_HEADS * RET_QK // LANES
    q, k, v = qkv[:nq], qkv[nq:2 * nq], qkv[2 * nq:]
    cos, sin = pos

    def rot(t, scale):
        out = []
        for h in range(RET_HEADS):
            x1, x2 = t[2 * h], t[2 * h + 1]
            o1, o2 = x1 * cos - x2 * sin, x2 * cos + x1 * sin
            out += [o1, o2] if scale is None else [o1 * scale, o2 * scale]
        return out

    return [rot(q, None), rot(k, RET_QK ** -0.5), list(v)]


def _fn_ret_gate(rows, pos, consts):
    o, g = rows
    (gn,) = consts
    out = []
    for h in range(RET_HEADS):
        r = lax.rsqrt(jnp.mean(o[h] * o[h], axis=-1, keepdims=True) + RMS_EPS)
        out.append((o[h] * r * gn[h]) * (g[h] * _sigmoid(g[h])))
    return [out]


def _fn_mla_lat(rows, pos, consts):
    (p,) = rows
    gq, gkv = consts
    nq, nkv = MLA_Q_RANK // LANES, MLA_KV_RANK // LANES
    cq, ckv, kr = p[:nq], p[nq:nq + nkv], p[nq + nkv]
    rq = lax.rsqrt(_ssq(cq) / MLA_Q_RANK + RMS_EPS)
    rkv = lax.rsqrt(_ssq(ckv) / MLA_KV_RANK + RMS_EPS)
    return [[t * rq * g for t, g in zip(cq, gq)], [t * rkv * g for t, g in zip(ckv, gkv)], [kr]]


def _swap32_impl(x):
    lane = lax.broadcasted_iota(jnp.int32, x.shape, 1)
    up, down = pltpu.roll(x, LANES - 32, 1), pltpu.roll(x, 32, 1)
    return jnp.where(lane < 32, up, jnp.where(lane < 64, down, 0.0))


@jax.custom_vjp
def _swap32(x):
    return _swap32_impl(x)


_swap32.defvjp(lambda x: (_swap32_impl(x), None), lambda _, g: (_swap32_impl(g),))


def _fn_mla_heads(rows, pos, consts):
    qf, kvf, (kr,) = rows
    cos, sin = pos
    gq, gk = consts
    q_out, k_out, v_out = [], [], []
    for h in range(MLA_HEADS):
        q0, q1 = qf[2 * h], qf[2 * h + 1]
        r = lax.rsqrt(_ssq([q0, q1]) / MLA_QK + RMS_EPS)
        a0, a1 = q0 * r * gq[0], q1 * r * gq[1]
        a1 = a1 * cos + _swap32(a1) * sin
        q_out += [a0 * (MLA_QK ** -0.5), a1 * (MLA_QK ** -0.5)]
        k0 = kvf[2 * h]
        r = lax.rsqrt(_ssq([k0, kr]) / MLA_QK + RMS_EPS)
        b0, b1 = k0 * r * gk[0], kr * r * gk[1]
        k_out += [b0, b1 * cos + _swap32(b1) * sin]
        v_out.append(kvf[2 * h + 1])
    return [q_out, k_out, v_out]


def _shift_down(x, n):
    row = lax.broadcasted_iota(jnp.int32, x.shape, 0)
    return jnp.where(row >= n, pltpu.roll(x, n, 0), 0.0)


def _shift_up(x, n):
    rows = x.shape[0]
    row = lax.broadcasted_iota(jnp.int32, x.shape, 0)
    return jnp.where(row < rows - n, pltpu.roll(x, rows - n, 0), 0.0)


def _conv_blocks(S):
    cb = 256
    return cb, FFN_DIM // cb


def _conv_fwd(ag, w8, B, S, name):
    cb, ncb = _conv_blocks(S)

    def body(a_ref, g_ref, w_ref, u_ref, ut_ref):
        g = g_ref[...]
        w = w_ref[...]
        gc = w[0:1] * _shift_down(g, 2) + w[1:2] * _shift_down(g, 1) + w[2:3] * g + w[3:4]
        u = a_ref[...] * (gc * _sigmoid(gc))
        u_ref[...] = u.astype(u_ref.dtype)
        ut_ref[...] = u.T.astype(ut_ref.dtype)

    return pl.pallas_call(
        body, name=name, grid=(ncb, B),
        in_specs=[pl.BlockSpec((S, cb), lambda j, b: (b, j)),
                  pl.BlockSpec((S, cb), lambda j, b: (b, ncb + j)),
                  pl.BlockSpec((8, cb), lambda j, b: (0, j))],
        out_specs=[pl.BlockSpec((S, cb), lambda j, b: (b, j)), pl.BlockSpec((cb, S), lambda j, b: (j, b))],
        out_shape=[jax.ShapeDtypeStruct((B * S, FFN_DIM), BF16), jax.ShapeDtypeStruct((FFN_DIM, B * S), BF16)],
        compiler_params=_params(("parallel", "parallel")),
    )(ag, ag, w8)


def _conv_bwd(ag, w8, du, B, S, name):
    cb, ncb = _conv_blocks(S)

    def body(a_ref, g_ref, w_ref, du_ref, da_ref, dg_ref, dw_ref):
        g = g_ref[...]
        w = w_ref[...]
        g1, g2 = _shift_down(g, 1), _shift_down(g, 2)
        gc = w[0:1] * g2 + w[1:2] * g1 + w[2:3] * g + w[3:4]
        sg = _sigmoid(gc)
        du_v = du_ref[...]
        da_ref[...] = (du_v * (gc * sg)).astype(da_ref.dtype)
        dgc = du_v * a_ref[...] * (sg * (1.0 + gc * (1.0 - sg)))
        dg = w[2:3] * dgc + w[1:2] * _shift_up(dgc, 1) + w[0:1] * _shift_up(dgc, 2)
        dg_ref[...] = dg.astype(dg_ref.dtype)
        part = jnp.concatenate([
            jnp.sum(dgc * g2, axis=0, keepdims=True), jnp.sum(dgc * g1, axis=0, keepdims=True),
            jnp.sum(dgc * g, axis=0, keepdims=True), jnp.sum(dgc, axis=0, keepdims=True),
            jnp.zeros((4, cb), F32)], axis=0)

        @pl.when(pl.program_id(1) == 0)
        def _():
            dw_ref[...] = part

        @pl.when(pl.program_id(1) > 0)
        def _():
            dw_ref[...] += part

    blk = lambda j, b: (b, j)
    return pl.pallas_call(
        body, name=name, grid=(ncb, B),
        in_specs=[pl.BlockSpec((S, cb), blk),
                  pl.BlockSpec((S, cb), lambda j, b: (b, ncb + j)),
                  pl.BlockSpec((8, cb), lambda j, b: (0, j)),
                  pl.BlockSpec((S, cb), blk)],
        out_specs=[pl.BlockSpec((S, cb), blk), pl.BlockSpec((S, cb), blk),
                   pl.BlockSpec((8, cb), lambda j, b: (0, j))],
        out_shape=[jax.ShapeDtypeStruct((B * S, FFN_DIM), BF16), jax.ShapeDtypeStruct((B * S, FFN_DIM), BF16),
                   jax.ShapeDtypeStruct((8, FFN_DIM), F32)],
        compiler_params=_params(("parallel", "arbitrary")),
    )(ag, ag, w8, du)


_NT = (((1,), (1,)), ((), ()))
_NN = (((1,), (0,)), ((), ()))
_TN = (((0,), (0,)), ((), ()))


def _dot(a, b, dn):
    return lax.dot_general(a.astype(MXU_DTYPE), b.astype(MXU_DTYPE), dn, preferred_element_type=F32)


def _rel_and_mask():
    il = lax.broadcasted_iota(jnp.int32, (ATT_BLOCK, ATT_BLOCK), 0)
    jl = lax.broadcasted_iota(jnp.int32, (ATT_BLOCK, ATT_BLOCK), 1)
    return (il - jl).astype(F32), (jl // CHUNK) <= (il // CHUNK)


def _rows(i):
    return pl.ds(pl.multiple_of(i * ATT_BLOCK, ATT_BLOCK), ATT_BLOCK)


def _run_bits(n):
    bits, b = [], 1
    while b < n:
        bits.append(b)
        b *= 2
    return bits[::-1]


def _key_runs(n, nq, update):
    for bit in _run_bits(nq + 1):
        @pl.when((n & bit) != 0)
        def _(bit=bit):
            update(n & ~(2 * bit - 1), bit, (n & (bit - 1)) == 0)


def _earlier_runs(n, nq, update):
    for bit in _run_bits(nq):
        @pl.when((n & bit) != 0)
        def _(bit=bit):
            update(n & ~(2 * bit - 1), bit, False)


def _chunk_visible(shape, nblk, blk):
    key = lax.broadcasted_iota(jnp.int32, shape, 0) - (nblk - 1) * blk
    query = lax.broadcasted_iota(jnp.int32, shape, 1)
    return jnp.logical_or(key < 0, (key // CHUNK) <= (query // CHUNK))


KV_UNROLL = 2


def _kv_loop(n, body, carry):
    main = n // KV_UNROLL

    def chunk(t, c):
        for u in range(KV_UNROLL):
            c = body(t * KV_UNROLL + u, c)
        return c

    carry = lax.fori_loop(0, main, chunk, carry)
    return lax.fori_loop(main * KV_UNROLL, n, body, carry)


def _mla_attn_fwd(q, k, v, B, S):
    blk = min(MLA_FWD_BLOCK, S)
    H, nq = MLA_HEADS, S // blk

    def body(q_ref, k_ref, v_ref, o_ref, lse_ref, m_ref, l_ref, acc_ref):
        def qblock(i, _):
            q_rows = pl.ds(pl.multiple_of(i * blk, blk), blk)
            qi = q_ref[q_rows, :]
            m_ref[...] = jnp.full(m_ref.shape, MASK_VALUE, F32)
            l_ref[...] = jnp.zeros(l_ref.shape, F32)
            acc_ref[...] = jnp.zeros(acc_ref.shape, F32)

            def keys(first, nblk, last):
                rows = pl.ds(pl.multiple_of(first * blk, blk), nblk * blk)
                s = _dot(k_ref[rows, :], qi, _NT)
                s = jnp.where(jnp.logical_or(_chunk_visible(s.shape, nblk, blk), jnp.logical_not(last)), s, MASK_VALUE)
                m = m_ref[...]
                m2 = jnp.maximum(m, jnp.max(s, axis=0, keepdims=True))
                alpha = jnp.exp(m - m2)
                p = jnp.exp(s - m2)
                l_ref[...] = alpha * l_ref[...] + jnp.sum(p, axis=0, keepdims=True)
                acc_ref[...] = alpha * acc_ref[...] + _dot(v_ref[rows, :], p, _TN)
                m_ref[...] = m2

            _key_runs(i + 1, nq, keys)
            l = l_ref[...]
            o_ref[q_rows, :] = (acc_ref[...] / l).T
            lse_ref[0, :, q_rows] = m_ref[...] + jnp.log(l)
            return 0

        lax.fori_loop(0, nq, qblock, 0)

    return pl.pallas_call(
        body, name="mla_attn_fwd", grid=(B, H),
        in_specs=[pl.BlockSpec((S, MLA_PAD), lambda b, h: (b, h)),
                  pl.BlockSpec((S, MLA_PAD), lambda b, h: (b, h)),
                  pl.BlockSpec((S, MLA_V), lambda b, h: (b, h))],
        out_specs=[pl.BlockSpec((S, MLA_V), lambda b, h: (b, h)),
                   pl.BlockSpec((1, 1, S), lambda b, h: (b * H + h, 0, 0))],
        out_shape=[jax.ShapeDtypeStruct((B * S, H * MLA_V), F32), jax.ShapeDtypeStruct((B * H, 1, S), F32)],
        scratch_shapes=[pltpu.VMEM((1, blk), F32), pltpu.VMEM((1, blk), F32), pltpu.VMEM((MLA_V, blk), F32)],
        compiler_params=_params(("parallel", "parallel")),
    )(q, k, v)


def _mla_attn_bwd(q, k, v, o, do, lse, B, S):
    blk = min(MLA_FWD_BLOCK, S)
    H, nq = MLA_HEADS, S // blk

    def body(q_ref, k_ref, v_ref, o_ref, do_ref, lse_ref, dq_ref, dk_ref, dv_ref, kt_ref, dqt_ref):
        dk_ref[...] = jnp.zeros(dk_ref.shape, F32)
        dv_ref[...] = jnp.zeros(dv_ref.shape, F32)
        for g in range(nq):
            kt_ref[g] = k_ref[g * blk:(g + 1) * blk, :].T

        def qblock(i, _):
            q_rows = pl.ds(pl.multiple_of(i * blk, blk), blk)
            qi = q_ref[q_rows, :]
            doi = do_ref[q_rows, :]
            delta = jnp.sum((doi * o_ref[q_rows, :]).T, axis=0, keepdims=True)
            lse_i = lse_ref[0, :, q_rows]
            doi = doi.astype(MXU_DTYPE)
            dqt_ref[...] = jnp.zeros(dqt_ref.shape, F32)

            def keys(first, nblk, last):
                rows = pl.ds(pl.multiple_of(first * blk, blk), nblk * blk)
                k_run, v_run = k_ref[rows, :], v_ref[rows, :]
                p = jnp.exp(_dot(k_run, qi, _NT) - lse_i)
                p = jnp.where(jnp.logical_or(_chunk_visible(p.shape, nblk, blk), jnp.logical_not(last)), p, 0.0)
                ds = (p * (_dot(v_run, doi, _NT) - delta)).astype(MXU_DTYPE)
                dk_ref[rows, :] += _dot(ds, qi, _NN)
                dv_ref[rows, :] += _dot(p, doi, _NN)
                for r in range(nblk):
                    dqt_ref[...] += _dot(kt_ref[first + r], ds[r * blk:(r + 1) * blk, :], _NN)

            _key_runs(i + 1, nq, keys)
            dq_ref[q_rows, :] = dqt_ref[...].T
            return 0

        lax.fori_loop(0, nq, qblock, 0)

    qk_spec = pl.BlockSpec((S, MLA_PAD), lambda b, h: (b, h))
    v_spec = pl.BlockSpec((S, MLA_V), lambda b, h: (b, h))
    return pl.pallas_call(
        body, name="mla_attn_bwd", grid=(B, H),
        in_specs=[qk_spec, qk_spec, v_spec, v_spec, v_spec,
                  pl.BlockSpec((1, 1, S), lambda b, h: (b * H + h, 0, 0))],
        out_specs=[qk_spec, qk_spec, v_spec],
        out_shape=[jax.ShapeDtypeStruct((B * S, H * MLA_PAD), F32), jax.ShapeDtypeStruct((B * S, H * MLA_PAD), F32),
                   jax.ShapeDtypeStruct((B * S, H * MLA_V), F32)],
        scratch_shapes=[pltpu.VMEM((nq, MLA_PAD, blk), q.dtype), pltpu.VMEM((MLA_PAD, blk), F32)],
        compiler_params=_params(("parallel", "parallel")),
    )(q, k, v, o, do, lse)


def _ret_log_gamma():
    lg = np.log1p(-np.exp2(RET_GAMMA_BASE - np.arange(RET_HEADS, dtype=np.float32))).astype(np.float32)
    return jnp.asarray(np.broadcast_to(lg[:, None, None], (RET_HEADS, 8, LANES)).copy())


RET_BLOCK = 512


def _ret_local_scale(lg, shape, blk, rising):
    local = lax.broadcasted_iota(jnp.int32, shape, 0) % blk
    return jnp.exp(lg * (local if rising else blk - 1 - local).astype(F32))


def _ret_pair_factor(lg, blk, steps):
    return jnp.exp(lg * (blk * (steps - 1) + 1).astype(F32))


def _ret_own_decay(lg, blk, transposed):
    a = lax.broadcasted_iota(jnp.int32, (blk, blk), 0)
    b = lax.broadcasted_iota(jnp.int32, (blk, blk), 1)
    query, key = (b, a) if transposed else (a, b)
    dec = jnp.exp(lg * jnp.abs(query - key).astype(F32))
    return jnp.where((key // CHUNK) <= (query // CHUNK), dec, 0.0)


def _ret_attn_fwd(q, k, v, B, S):
    blk = min(RET_BLOCK, S)
    H, nq = RET_HEADS, S // blk

    def body(lg_ref, q_ref, k_ref, v_ref, o_ref, ks_ref, dec_ref, acc_ref):
        lg = lg_ref[0, 0:1, 0:1]
        ks_ref[...] = (k_ref[...].astype(F32) * _ret_local_scale(lg, k_ref.shape, blk, False)).astype(ks_ref.dtype)
        dec_ref[...] = _ret_own_decay(lg, blk, False)

        def qblock(i, _):
            q_rows = pl.ds(pl.multiple_of(i * blk, blk), blk)
            qi = q_ref[q_rows, :]
            qs = (qi.astype(F32) * _ret_local_scale(lg, qi.shape, blk, True)).astype(qi.dtype)
            a = _dot(qi, k_ref[q_rows, :], _NT) * dec_ref[...]
            acc_ref[...] = _dot(a, v_ref[q_rows, :], _NN)

            def keys(first, nblk, _):
                rows = pl.ds(pl.multiple_of(first * blk, blk), nblk * blk)
                steps = i - first - lax.broadcasted_iota(jnp.int32, (1, nblk * blk), 1) // blk
                a = _dot(qs, ks_ref[rows, :], _NT) * _ret_pair_factor(lg, blk, steps)
                acc_ref[...] += _dot(a, v_ref[rows, :], _NN)

            _earlier_runs(i, nq, keys)
            o_ref[q_rows, :] = acc_ref[...]
            return 0

        lax.fori_loop(0, nq, qblock, 0)

    qk_spec = pl.BlockSpec((S, RET_QK), lambda b, h: (b, h))
    v_spec = pl.BlockSpec((S, RET_V), lambda b, h: (b, h))
    return pl.pallas_call(
        body, name="ret_attn_fwd", grid=(B, H),
        in_specs=[pl.BlockSpec((1, 8, LANES), lambda b, h: (h, 0, 0)), qk_spec, qk_spec, v_spec],
        out_specs=v_spec,
        out_shape=jax.ShapeDtypeStruct((B * S, H * RET_V), F32),
        scratch_shapes=[pltpu.VMEM((S, RET_QK), k.dtype), pltpu.VMEM((blk, blk), F32), pltpu.VMEM((blk, RET_V), F32)],
        compiler_params=_params(("parallel", "parallel")),
    )(_ret_log_gamma(), q, k, v)


def _ret_attn_bwd(q, k, v, do, B, S):
    blk = min(RET_BLOCK, S)
    H, nq = RET_HEADS, S // blk

    def body(lg_ref, q_ref, k_ref, v_ref, do_ref, dq_ref, dk_ref, dv_ref, ks_ref, kst_ref, dks_ref, dqt_ref, dec_ref):
        lg = lg_ref[0, 0:1, 0:1]
        dk_ref[...] = jnp.zeros(dk_ref.shape, F32)
        dv_ref[...] = jnp.zeros(dv_ref.shape, F32)
        dks_ref[...] = jnp.zeros(dks_ref.shape, F32)
        ks_ref[...] = (k_ref[...].astype(F32) * _ret_local_scale(lg, k_ref.shape, blk, False)).astype(ks_ref.dtype)
        for g in range(nq):
            kst_ref[g] = ks_ref[g * blk:(g + 1) * blk, :].T
        dec_ref[...] = _ret_own_decay(lg, blk, True)

        def qblock(i, _):
            q_rows = pl.ds(pl.multiple_of(i * blk, blk), blk)
            qi = q_ref[q_rows, :]
            q_scale = _ret_local_scale(lg, qi.shape, blk, True)
            qs = (qi.astype(F32) * q_scale).astype(qi.dtype)
            doi = do_ref[q_rows, :].astype(MXU_DTYPE)
            ki = k_ref[q_rows, :]
            dec = dec_ref[...]
            a = _dot(ki, qi, _NT) * dec
            da = (_dot(v_ref[q_rows, :], doi, _NT) * dec).astype(MXU_DTYPE)
            dv_ref[q_rows, :] += _dot(a, doi, _NN)
            dk_ref[q_rows, :] += _dot(da, qi, _NN)
            dq_own = _dot(da, ki, _TN)
            dqt_ref[...] = jnp.zeros(dqt_ref.shape, F32)

            def keys(first, nblk, _):
                for r in range(nblk):
                    g = first + r
                    rows = pl.ds(pl.multiple_of(g * blk, blk), blk)
                    c = _ret_pair_factor(lg, blk, i - g)
                    a = _dot(ks_ref[rows, :], qs, _NT) * c
                    da = (_dot(v_ref[rows, :], doi, _NT) * c).astype(MXU_DTYPE)
                    dv_ref[rows, :] += _dot(a, doi, _NN)
                    dks_ref[rows, :] += _dot(da, qs, _NN)
                    dqt_ref[...] += _dot(kst_ref[g], da, _NN)

            _earlier_runs(i, nq, keys)
            dq_ref[q_rows, :] = dqt_ref[...].T * q_scale + dq_own
            return 0

        lax.fori_loop(0, nq, qblock, 0)
        dk_ref[...] += dks_ref[...] * _ret_local_scale(lg, dks_ref.shape, blk, False)

    qk_spec = pl.BlockSpec((S, RET_QK), lambda b, h: (b, h))
    v_spec = pl.BlockSpec((S, RET_V), lambda b, h: (b, h))
    return pl.pallas_call(
        body, name="ret_attn_bwd", grid=(B, H),
        in_specs=[pl.BlockSpec((1, 8, LANES), lambda b, h: (h, 0, 0)), qk_spec, qk_spec, v_spec, v_spec],
        out_specs=[qk_spec, qk_spec, v_spec],
        out_shape=[jax.ShapeDtypeStruct((B * S, H * RET_QK), F32), jax.ShapeDtypeStruct((B * S, H * RET_QK), F32),
                   jax.ShapeDtypeStruct((B * S, H * RET_V), F32)],
        scratch_shapes=[pltpu.VMEM((S, RET_QK), k.dtype), pltpu.VMEM((nq, RET_QK, blk), k.dtype),
                        pltpu.VMEM((S, RET_QK), F32), pltpu.VMEM((RET_QK, blk), F32), pltpu.VMEM((blk, blk), F32)],
        compiler_params=_params(("parallel", "parallel")),
    )(_ret_log_gamma(), q, k, v, do)


def _loss_head(y, target, bm=512):
    T, D = y.shape
    bm = _pick(T, bm)

    def body(y_ref, t_ref, dy_ref, dyc_ref, l_ref):
        err = y_ref[...] - t_ref[...]
        dy_ref[...] = err / D
        dyc_ref[...] = (err / D).astype(dyc_ref.dtype)
        part = jnp.full((8, LANES), 0.5 * jnp.sum(jnp.mean(err * err, axis=-1)), F32)

        @pl.when(pl.program_id(0) == 0)
        def _():
            l_ref[...] = part

        @pl.when(pl.program_id(0) > 0)
        def _():
            l_ref[...] += part

    blk = pl.BlockSpec((bm, D), lambda i: (i, 0))
    dy, dyc, l = pl.pallas_call(
        body, name="loss_head", grid=(T // bm,),
        in_specs=[blk, blk], out_specs=[blk, blk, pl.BlockSpec((8, LANES), lambda i: (0, 0))],
        out_shape=[jax.ShapeDtypeStruct((T, D), F32), jax.ShapeDtypeStruct((T, D), BF16),
                   jax.ShapeDtypeStruct((8, LANES), F32)],
        compiler_params=_params(("arbitrary",)),
    )(y, target)
    return dy, dyc, l[0, 0]


def _adamw(w, g, m, v, name):
    R, C = w.shape
    br = R if R * C * 4 <= 2 ** 21 else _pick_rows(R, max(8, (2 ** 21) // (C * 4)))

    def body(w_ref, g_ref, m_ref, v_ref, d_ref, mo_ref, vo_ref):
        g_v = g_ref[...]
        m_v = ADAM_B1 * m_ref[...] + (1.0 - ADAM_B1) * g_v
        v_v = ADAM_B2 * v_ref[...] + (1.0 - ADAM_B2) * (g_v * g_v)
        m_hat = m_v / (1.0 - ADAM_B1 ** ADAM_STEP)
        v_hat = v_v / (1.0 - ADAM_B2 ** ADAM_STEP)
        d_ref[...] = -ADAM_LR * (m_hat / (jnp.sqrt(v_hat) + ADAM_EPS) + ADAM_WD * w_ref[...])
        mo_ref[...] = m_v
        vo_ref[...] = v_v

    blk = pl.BlockSpec((br, C), lambda i: (i, 0))
    return pl.pallas_call(
        body, name=name, grid=(R // br,),
        in_specs=[blk] * 4, out_specs=[blk] * 3,
        out_shape=[jax.ShapeDtypeStruct((R, C), F32)] * 3,
        compiler_params=_params(("parallel",)),
    )(w, g, m, v)


def _pick_rows(R, target):
    best = None
    for d in range(8, min(R, target) + 1, 8):
        if R % d == 0:
            best = d
    assert best is not None, (R, target)
    return best


def _position():
    return lax.axis_index("x"), lax.axis_index("y"), lax.axis_index("c")


HBM_SPEC = pl.BlockSpec(memory_space=pltpu.HBM)


def _other_chips(x, y):
    return [(1 - x, y), (x, 1 - y), (1 - x, 1 - y)]


def _all_gather_weights(bigs, small):
    nb = len(bigs)

    def body(*refs):
        big_refs, small_ref = refs[:nb], refs[nb]
        obig, osmall = refs[nb + 1:2 * nb + 1], refs[2 * nb + 1]
        ici_send, ici_recv, d2d_send, d2d_recv, sm_send, sm_recv = refs[2 * nb + 2:]
        x, y, c = _position()
        me = 2 * x + y
        chips = _other_chips(x, y)

        def rows(n, half):
            rh = bigs[n].shape[0] // 2
            return pl.ds(half * rh, rh)

        def over_ici(n, j, slot, from_shard):
            px, py = chips[j]
            dst = obig[n].at[slot, rows(n, c)]
            return pltpu.make_async_remote_copy(
                src_ref=big_refs[n].at[rows(n, c)] if from_shard else dst, dst_ref=dst,
                send_sem=ici_send.at[3 * n + j], recv_sem=ici_recv.at[3 * n + j],
                device_id=(px, py, c), device_id_type=MESH)

        def over_d2d(n, j, half):
            px, py = chips[j]
            part = obig[n].at[2 * px + py, rows(n, half)]
            return pltpu.make_async_remote_copy(
                src_ref=part, dst_ref=part, send_sem=d2d_send.at[3 * n + j], recv_sem=d2d_recv.at[3 * n + j],
                device_id=(x, y, 1 - c), device_id_type=MESH)

        def small_copy(j, slot):
            px, py = chips[j]
            return pltpu.make_async_remote_copy(
                src_ref=small_ref, dst_ref=osmall.at[slot], send_sem=sm_send.at[j], recv_sem=sm_recv.at[j],
                device_id=(px, py, c), device_id_type=MESH)

        sends = [over_ici(n, j, me, True) for n in range(nb) for j in range(3)]
        sends += [small_copy(j, me) for j in range(3)]
        for cp in sends:
            cp.start()
        passed = []
        for n in range(nb):
            for j, (px, py) in enumerate(chips):
                over_ici(n, j, 2 * px + py, False).wait_recv()
                fwd = over_d2d(n, j, c)
                fwd.start()
                passed.append(fwd)
        for n in range(nb):
            for j in range(3):
                over_d2d(n, j, 1 - c).wait_recv()
        for j, (px, py) in enumerate(chips):
            small_copy(j, 2 * px + py).wait_recv()
        for cp in sends + passed:
            cp.wait_send()

    dma = pltpu.SemaphoreType.DMA
    return pl.pallas_call(
        body, name="weights_all_gather",
        in_specs=[HBM_SPEC] * (nb + 1), out_specs=[HBM_SPEC] * (nb + 1),
        out_shape=[jax.ShapeDtypeStruct((N_SHARD,) + b.shape, b.dtype) for b in bigs]
        + [jax.ShapeDtypeStruct((N_SHARD,) + small.shape, small.dtype)],
        scratch_shapes=[dma((3 * nb,)), dma((3 * nb,)), dma((3 * nb,)), dma((3 * nb,)), dma((3,)), dma((3,))],
    )(*bigs, small)


SEM_SPEC = pl.BlockSpec(memory_space=pltpu.SEMAPHORE)
DATAFLOW_EFFECT = pltpu.SideEffectType.DATAFLOW_SIDE_EFFECTING
N_PEERS = N_DEV - 1


def _grad_copies(p_refs, land_refs, send_sems, recv_sems):
    x, y, c = _position()
    copies = []
    for a, (p_ref, land_ref) in enumerate(zip(p_refs, land_refs)):
        rh = p_ref.shape[1] // 2
        for k in range(1, N_DEV):
            px = 1 - x if k & 4 else x
            py = 1 - y if k & 2 else y
            pc = 1 - c if k & 1 else c
            copies.append(pltpu.make_async_remote_copy(
                src_ref=p_ref.at[2 * px + py, pl.ds(pc * rh, rh)], dst_ref=land_ref.at[k - 1],
                send_sem=send_sems.at[N_PEERS * a + k - 1], recv_sem=recv_sems.at[N_PEERS * a + k - 1],
                device_id=(px, py, pc), device_id_type=MESH))
    return copies


def _weight_copies(w_refs, land_refs, send_sems, recv_sems):
    x, y, c = _position()
    copies = []
    for a, (w_ref, land_ref) in enumerate(zip(w_refs, land_refs)):
        for j, (px, py) in enumerate(_other_chips(x, y)):
            copies.append(pltpu.make_async_remote_copy(
                src_ref=w_ref, dst_ref=land_ref.at[2 * x + y], send_sem=send_sems.at[3 * a + j],
                recv_sem=recv_sems.at[3 * a + j], device_id=(px, py, c), device_id_type=MESH))
    return copies


def _exchange_start(make_copies, srcs, lands, n_sems, name, after=None):
    n, m = len(srcs), len(lands)
    n_in = n + m + (after is not None)

    def body(*refs):
        send_sems, recv_sems, token = refs[n_in], refs[n_in + 1], refs[-1]
        for cp in make_copies(refs[:n], refs[n:n + m], send_sems, recv_sems):
            cp.start()
        token[...] = jnp.zeros(token.shape, token.dtype)

    hbm = lambda a: pltpu.with_memory_space_constraint(a, pltpu.HBM)
    dma = pltpu.SemaphoreType.DMA
    res = pl.pallas_call(
        body, name=name,
        in_specs=[HBM_SPEC] * (n + m) + ([] if after is None else [pl.BlockSpec(memory_space=pl.ANY)]),
        out_specs=[SEM_SPEC, SEM_SPEC] + [HBM_SPEC] * (n + m) + [pl.BlockSpec(memory_space=pltpu.VMEM)],
        out_shape=[dma((n_sems,)), dma((n_sems,))] + [pltpu.HBM(a.shape, a.dtype) for a in list(srcs) + list(lands)]
        + [jax.ShapeDtypeStruct((8, LANES), F32)],
        input_output_aliases={i: 2 + i for i in range(n + m)},
        compiler_params=pltpu.CompilerParams(has_side_effects=DATAFLOW_EFFECT),
    )(*[hbm(a) for a in srcs], *[hbm(a) for a in lands], *(() if after is None else (after,)))
    return res[0], res[1], list(res[2:2 + n]), list(res[2 + n:2 + n + m]), res[-1]


def _exchange_wait(make_copies, send_sems, recv_sems, srcs, lands, after, name):
    n, m = len(srcs), len(lands)

    def body(*refs):
        for cp in make_copies(refs[:n], refs[n:n + m], refs[n + m], refs[n + m + 1]):
            cp.wait_send()
            cp.wait_recv()

    res = pl.pallas_call(
        body, name=name,
        in_specs=[HBM_SPEC] * (n + m) + [SEM_SPEC, SEM_SPEC, pl.BlockSpec(memory_space=pl.ANY)],
        out_specs=[HBM_SPEC] * (n + m),
        out_shape=[pltpu.HBM(a.shape, a.dtype) for a in list(srcs) + list(lands)],
        input_output_aliases={i: i for i in range(n + m)},
        compiler_params=pltpu.CompilerParams(has_side_effects=DATAFLOW_EFFECT),
    )(*srcs, *lands, send_sems, recv_sems, after)
    return list(res[:n]), list(res[n:])


def _sum_partials(p, land, name):
    _, rh, cols = land.shape
    br = _pick_rows(rh, 256)
    nrb = rh // br
    x, y, c = _position()
    where = jnp.stack([2 * x + y, c]).astype(jnp.int32)

    def body(where_ref, p_ref, land_ref, o_ref):
        acc = p_ref[...].astype(F32)
        for k in range(N_PEERS):
            acc = acc + land_ref[k].astype(F32)
        o_ref[...] = acc

    return pl.pallas_call(
        body, name=name,
        grid_spec=pltpu.PrefetchScalarGridSpec(
            num_scalar_prefetch=1, grid=(nrb,),
            in_specs=[pl.BlockSpec((None, br, cols), lambda r, where_ref: (where_ref[0], where_ref[1] * nrb + r, 0)),
                      pl.BlockSpec((N_PEERS, br, cols), lambda r, where_ref: (0, r, 0))],
            out_specs=pl.BlockSpec((None, br, cols), lambda r, where_ref: (where_ref[1], r, 0))),
        out_shape=jax.ShapeDtypeStruct((2, rh, cols), F32),
        compiler_params=_params(("parallel",)),
    )(where, p, land)


def _sibling_share(fulls, name):
    n = len(fulls)

    def body(*refs):
        o_refs = refs[n:2 * n]
        send_sems, recv_sems = refs[2 * n:]
        x, y, c = _position()

        def copy(a, half):
            return pltpu.make_async_remote_copy(
                src_ref=o_refs[a].at[half], dst_ref=o_refs[a].at[half], send_sem=send_sems.at[a],
                recv_sem=recv_sems.at[a], device_id=(x, y, 1 - c), device_id_type=MESH)

        sends = [copy(a, c) for a in range(n)]
        for cp in sends:
            cp.start()
        for a in range(n):
            copy(a, 1 - c).wait_recv()
        for cp in sends:
            cp.wait_send()

    dma = pltpu.SemaphoreType.DMA
    return pl.pallas_call(
        body, name=name,
        in_specs=[HBM_SPEC] * n, out_specs=[HBM_SPEC] * n,
        out_shape=[jax.ShapeDtypeStruct(f.shape, f.dtype) for f in fulls],
        input_output_aliases={a: a for a in range(n)},
        scratch_shapes=[dma((n,)), dma((n,))],
    )(*fulls)


def _all_reduce_small(v):
    R, cols = v.shape

    def body(v_ref, o_ref, buf_ref, send_sems, recv_sems):
        x, y, c = _position()
        me = 4 * x + 2 * y + c
        buf_ref[me] = v_ref[...]
        sends = []
        for k in range(1, N_DEV):
            px = 1 - x if k & 4 else x
            py = 1 - y if k & 2 else y
            pc = 1 - c if k & 1 else c
            sends.append(pltpu.make_async_remote_copy(
                src_ref=v_ref, dst_ref=buf_ref.at[me], send_sem=send_sems.at[k - 1], recv_sem=recv_sems.at[k - 1],
                device_id=(px, py, pc), device_id_type=MESH))
        for cp in sends:
            cp.start()
        for k in range(1, N_DEV):
            px = 1 - x if k & 4 else x
            py = 1 - y if k & 2 else y
            pc = 1 - c if k & 1 else c
            pltpu.make_async_remote_copy(
                src_ref=v_ref, dst_ref=buf_ref.at[4 * px + 2 * py + pc], send_sem=send_sems.at[k - 1],
                recv_sem=recv_sems.at[k - 1], device_id=(px, py, pc), device_id_type=MESH).wait_recv()
        for cp in sends:
            cp.wait_send()
        acc = buf_ref[0]
        for d in range(1, N_DEV):
            acc = acc + buf_ref[d]
        o_ref[...] = acc

    return pl.pallas_call(
        body, name="small_grads_all_reduce",
        in_specs=[pl.BlockSpec(memory_space=pltpu.VMEM)], out_specs=pl.BlockSpec(memory_space=pltpu.VMEM),
        out_shape=jax.ShapeDtypeStruct((R, cols), F32),
        scratch_shapes=[pltpu.VMEM((N_DEV, R, cols), F32), pltpu.SemaphoreType.DMA((N_DEV - 1,)),
                        pltpu.SemaphoreType.DMA((N_DEV - 1,))],
    )(v)


def _rope_tables(S, half, width):
    inv_freq = ROPE_THETA ** (-jnp.arange(half, dtype=F32) / half)
    ang = jnp.arange(S).astype(F32)[:, None] * inv_freq[None, :]
    return jnp.cos(ang), jnp.sin(ang)


def _slot_rows(a):
    return a.reshape(N_SHARD, -1, a.shape[-1])


def _local_step(x, target, w, B, S, late, exchange):
    T = B * S
    D = D_MODEL
    bm = 256
    full = lambda a, wd, tile=None: (a, wd, 0, tile or wd)
    g = {}

    cos_r, sin_r = _rope_tables(S, RET_QK // 2, LANES)
    cos_m, sin_m = _rope_tables(S, MLA_ROPE // 2, LANES)
    zeros64 = jnp.zeros((S, 64), F32)
    cos_m = jnp.concatenate([cos_m, cos_m, zeros64], axis=1)
    sin_m = jnp.concatenate([-sin_m, sin_m, zeros64], axis=1)

    def ffn_fwd(xin, i):
        w.update(late(f"ffn{i}", xin))
        norm = w["ffn_norm"][i:i + 1]
        h, ht = _rowwise_fwd(_fn_rms, f"ffn{i}_norm", [full(xin, D)], [], [(norm, D)], [(D, D, BF16)], bm, S,
                             transposed=(0,))
        ag = _mm(h, w[f"ffn_w_in{i}"], "nn", F32, f"ffn{i}_in", bn=1408, cols_outer=True)
        u, ut = _conv_fwd(ag, w["ffn_conv8"][i], B, S, f"ffn{i}_conv")
        xout = _mm(u, w[f"ffn_w_out{i}"], "nn", F32, f"ffn{i}_out", residual=xin, bk=1408)
        return xout, (xin, norm, ht, ag, ut)

    def ffn_bwd(dxout, dxout_c, saved, i):
        xin, norm, ht, ag, ut = saved
        du = _mm(dxout_c, w[f"ffn_w_out{i}"], "nt", F32, f"ffn{i}_out_dx", bn=1408)
        g_w_out = _mm(ut, dxout_c, "nn", BF16, f"ffn{i}_out_dw", bm=1408, bn=512, bk=T)
        da, dg, dw8 = _conv_bwd(ag, w["ffn_conv8"][i], du, B, S, f"ffn{i}_conv_bwd")
        g_w_in = _mm(ht, [da, dg], "nn", BF16, f"ffn{i}_in_dw", bm=1024, bn=1408, bk=T // 2, out_slots=N_SHARD)
        token = exchange(f"ffn{i}", [g_w_in, _slot_rows(g_w_out)])
        dh = _mm_dx([da, dg], w[f"ffn_w_in{i}"], f"ffn{i}_in_dx", after=token)
        (dxin, dxin_c), (g_norm,) = _rowwise_bwd(_fn_rms, f"ffn{i}_norm_bwd", [full(xin, D)], [], [(norm, D)],
                                                 [(dh, D)], bm, S, adds={0: dxout}, mxu_copies=(0,))
        return dxin, dxin_c, (g_norm, dw8)

    h0, h0t = _rowwise_fwd(_fn_rms, "ret_norm", [full(x, D)], [], [(w["ret_norm"], D)], [(D, D, BF16)], bm, S,
                           transposed=(0,))
    proj = _mm(h0, w["ret_w_in"], "nn", F32, "ret_in", after=w["started"], cols_outer=True)
    HQ, HV = RET_HEADS * RET_QK, RET_HEADS * RET_V
    rope_rows = [(proj, 2 * HQ + HV, 0, LANES)]
    q_r, k_r, v_r = _rowwise_fwd(_fn_ret_rope, "ret_rope", rope_rows, [cos_r, sin_r], [],
                                 [(HQ, LANES, BF16), (HQ, LANES, BF16), (HV, LANES, BF16)], bm, S)
    ret_o = _ret_attn_fwd(q_r, k_r, v_r, B, S)
    gate_rows = [full(ret_o, HV, RET_V), (proj, HV, 2, RET_V)]
    y0, y0t = _rowwise_fwd(_fn_ret_gate, "ret_gate", gate_rows, [], [(w["ret_gn"], RET_V)], [(HV, RET_V, BF16)], 128, S,
                           transposed=(0,))
    w.update(late("ret_out", y0))
    x1 = _mm(y0, w["ret_w_out"], "nn", F32, "ret_out", residual=x)
    x2, ffn0_saved = ffn_fwd(x1, 0)

    w.update(late("mla", x2))
    (h2,) = _rowwise_fwd(_fn_rms, "mla_norm", [full(x2, D)], [], [(w["mla_norm"], D)], [(D, D, BF16)], bm, S)
    proj2 = _mm(h2, w["mla_w_in"], "nn", F32, "mla_in")
    lat_consts = [(w["mla_q_norm"], LANES), (w["mla_kv_norm"], LANES)]
    cqn, ckvn, kr = _rowwise_fwd(_fn_mla_lat, "mla_latent_norm", [full(proj2, MLA_IN_PAD, LANES)], [], lat_consts,
                                 [(MLA_Q_RANK, LANES, BF16), (MLA_KV_RANK, LANES, BF16), (LANES, LANES, F32)], bm, S)
    qf = _mm(cqn, w["mla_w_qb"], "nn", F32, "mla_qb")
    kvf = _mm(ckvn, w["mla_w_kvb"], "nn", F32, "mla_kvb")
    HP, HVm = MLA_HEADS * MLA_PAD, MLA_HEADS * MLA_V
    head_rows = [full(qf, HP, LANES), full(kvf, HP, LANES), full(kr, LANES)]
    head_consts = [(w["mla_q_head_norm"], LANES), (w["mla_k_head_norm"], LANES)]
    q_a, k_a, v_a = _rowwise_fwd(_fn_mla_heads, "mla_heads", head_rows, [cos_m, sin_m], head_consts,
                                 [(HP, LANES, BF16), (HP, LANES, BF16), (HVm, LANES, BF16)], bm, S)
    att_o, lse = _mla_attn_fwd(q_a, k_a, v_a, B, S)
    x3 = _mm(att_o, w["mla_w_out"], "nn", F32, "mla_out", residual=x2)
    x4, ffn1_saved = ffn_fwd(x3, 1)

    dy, dy_c, loss = _loss_head(x4, target)

    dx3, dx3_c, (g_n1, dw8_1) = ffn_bwd(dy, dy_c, ffn1_saved, 1)

    d_att_o = _mm(dx3_c, w["mla_w_out"], "nt", F32, "mla_out_dx")
    g_mla_out = _mm(att_o, dx3_c, "tn", BF16, "mla_out_dw")
    dq_a, dk_a, dv_a = _mla_attn_bwd(q_a, k_a, v_a, att_o, d_att_o, lse, B, S)
    (dqf, dkvf, dkr), (g["mla_q_head_norm"], g["mla_k_head_norm"]) = _rowwise_bwd(
        _fn_mla_heads, "mla_heads_bwd", head_rows, [cos_m, sin_m], head_consts,
        [(dq_a, LANES), (dk_a, LANES), (dv_a, LANES)], 128, S, grad_dtypes=[BF16, BF16, F32])
    dcqn = _mm(dqf, w["mla_w_qb"], "nt", F32, "mla_qb_dx")
    g_qb = _mm(cqn, dqf, "tn", BF16, "mla_qb_dw")
    g_qb = _to_slots(_unpad_heads(g_qb, 1), 1).reshape(N_SHARD, MLA_Q_RANK, -1)
    dckvn = _mm(dkvf, w["mla_w_kvb"], "nt", F32, "mla_kvb_dx")
    g_kvb = _mm(ckvn, dkvf, "tn", BF16, "mla_kvb_dw", bn=512, out_slots=N_SHARD)
    (dproj2,), (g["mla_q_norm"], g["mla_kv_norm"]) = _rowwise_bwd(
        _fn_mla_lat, "mla_latent_norm_bwd", [full(proj2, MLA_IN_PAD, LANES)], [], lat_consts,
        [(dcqn, LANES), (dckvn, LANES), (dkr, LANES)], bm, S, grad_dtypes=[BF16])
    g_mla_in = _mm(h2, dproj2, "tn", BF16, "mla_in_dw")
    token = exchange("mla", [_slot_rows(g_mla_in[:, :MLA_IN]), g_qb, g_kvb, _slot_rows(g_mla_out)])
    dh2 = _mm(dproj2, w["mla_w_in"], "nt", F32, "mla_in_dx", after=token)
    (dx2, dx2_c), (g["mla_norm"],) = _rowwise_bwd(_fn_rms, "mla_norm_bwd", [full(x2, D)], [], [(w["mla_norm"], D)],
                                                  [(dh2, D)], bm, S, adds={0: dx3}, mxu_copies=(0,))

    dx1, dx1_c, (g_n0, dw8_0) = ffn_bwd(dx2, dx2_c, ffn0_saved, 0)

    dy0 = _mm(dx1_c, w["ret_w_out"], "nt", F32, "ret_out_dx")
    g_ret_out = _mm(y0t, dx1_c, "nn", BF16, "ret_out_dw", bm=1024, bn=512, bk=T)
    (d_ret_o, dgate), (g["ret_gn"],) = _rowwise_bwd(_fn_ret_gate, "ret_gate_bwd", gate_rows, [], [(w["ret_gn"], RET_V)],
                                                    [(dy0, RET_V)], 128, S, grad_dtypes=[F32, BF16])
    dq_r, dk_r, dv_r = _ret_attn_bwd(q_r, k_r, v_r, d_ret_o, B, S)
    (dqkv,), _ = _rowwise_bwd(_fn_ret_rope, "ret_rope_bwd", rope_rows, [cos_r, sin_r], [],
                              [(dq_r, LANES), (dk_r, LANES), (dv_r, LANES)], bm, S, grad_dtypes=[BF16])
    g_ret_in = _mm(h0t, [dqkv, dgate], "nn", BF16, "ret_in_dw", bn=512, bk=T, out_slots=N_SHARD)
    token = exchange("ret", [g_ret_in, _slot_rows(g_ret_out)])
    dh0 = _mm_dx([dqkv, dgate], w["ret_w_in"], "ret_in_dx", after=token)
    (dx,), (g["ret_norm"],) = _rowwise_bwd(_fn_rms, "ret_norm_bwd", [full(x, D)], [], [(w["ret_norm"], D)],
                                           [(dh0, D)], bm, S, adds={0: dx1})

    g["ffn_norm"] = jnp.concatenate([g_n0, g_n1], axis=0)
    g["ffn_conv_w"] = jnp.stack([dw8_0[0:3], dw8_1[0:3]])
    g["ffn_conv_b"] = jnp.stack([dw8_0[3], dw8_1[3]])
    return loss, dx, g


_BIG = [("ret_w_in", 2), ("ret_w_out", 1), ("mla_w_in", 1), ("mla_w_qb", 2), ("mla_w_kvb", 2), ("mla_w_out", 1),
        ("ffn_w_in", 2), ("ffn_w_out", 1)]
_SMALL_SHARDED = [("ret_gn", 2), ("mla_norm", 1), ("mla_q_norm", 1), ("mla_kv_norm", 1), ("ffn_conv_w", 2)]
_SMALL_REPLICATED = ["ret_norm", "mla_q_head_norm", "mla_k_head_norm", "ffn_norm", "ffn_conv_b"]
_SMALL_ALL = ["ret_norm", "ret_gn", "mla_norm", "mla_q_norm", "mla_kv_norm", "mla_q_head_norm", "mla_k_head_norm",
              "ffn_norm", "ffn_conv_w", "ffn_conv_b"]


def _to_slots(full, axis):
    shape = full.shape
    split = shape[:axis] + (N_SHARD, shape[axis] // N_SHARD) + shape[axis + 1:]
    return jnp.moveaxis(full.reshape(split), axis, 0).reshape(N_SHARD, -1)


def _from_slots(slots, shard_shape, axis):
    parts = jnp.moveaxis(slots.reshape((N_SHARD,) + tuple(shard_shape)), 0, axis)
    full = shard_shape[:axis] + (N_SHARD * shard_shape[axis],) + shard_shape[axis + 1:]
    return parts.reshape(full)


def _pad_rows(flat, cols, row_unit):
    n, L = flat.shape
    unit = cols * row_unit
    Lp = -(-L // unit) * unit
    if Lp != L:
        flat = jnp.concatenate([flat, jnp.zeros((n, Lp - L), flat.dtype)], axis=1)
    return flat.reshape(n, Lp // cols, cols)


def _pad_heads(a, axis):
    shape = a.shape
    a = a.reshape(shape[:axis] + (MLA_HEADS, MLA_QK) + shape[axis + 1:])
    pad = [(0, 0)] * a.ndim
    pad[axis + 1] = (0, MLA_PAD - MLA_QK)
    return jnp.pad(a, pad).reshape(shape[:axis] + (MLA_HEADS * MLA_PAD,) + shape[axis + 1:])


def _unpad_heads(a, axis):
    shape = a.shape
    a = a.reshape(shape[:axis] + (MLA_HEADS, MLA_PAD) + shape[axis + 1:])
    a = lax.slice_in_dim(a, 0, MLA_QK, axis=axis + 1)
    return a.reshape(shape[:axis] + (MLA_HEADS * MLA_QK,) + shape[axis + 1:])


def kernel(x, ret_norm, ret_w_in, ret_gn, ret_w_out, mla_norm, mla_w_in, mla_q_norm, mla_w_qb, mla_kv_norm, mla_w_kvb, mla_q_head_norm, mla_k_head_norm, mla_w_out, ffn_norm, ffn_w_in, ffn_conv_w, ffn_conv_b, ffn_w_out, loss_target, m_ret_norm, m_ret_w_in, m_ret_gn, m_ret_w_out, m_mla_norm, m_mla_w_in, m_mla_q_norm, m_mla_w_qb, m_mla_kv_norm, m_mla_w_kvb, m_mla_q_head_norm, m_mla_k_head_norm, m_mla_w_out, m_ffn_norm, m_ffn_w_in, m_ffn_conv_w, m_ffn_conv_b, m_ffn_w_out, v_ret_norm, v_ret_w_in, v_ret_gn, v_ret_w_out, v_mla_norm, v_mla_w_in, v_mla_q_norm, v_mla_w_qb, v_mla_kv_norm, v_mla_w_kvb, v_mla_q_head_norm, v_mla_k_head_norm, v_mla_w_out, v_ffn_norm, v_ffn_w_in, v_ffn_conv_w, v_ffn_conv_b, v_ffn_w_out):
    names = ["ret_norm", "ret_w_in", "ret_gn", "ret_w_out", "mla_norm", "mla_w_in", "mla_q_norm", "mla_w_qb",
             "mla_kv_norm", "mla_w_kvb", "mla_q_head_norm", "mla_k_head_norm", "mla_w_out", "ffn_norm", "ffn_w_in",
             "ffn_conv_w", "ffn_conv_b", "ffn_w_out"]
    shard = dict(zip(names, [ret_norm, ret_w_in, ret_gn, ret_w_out, mla_norm, mla_w_in, mla_q_norm, mla_w_qb,
                             mla_kv_norm, mla_w_kvb, mla_q_head_norm, mla_k_head_norm, mla_w_out, ffn_norm, ffn_w_in,
                             ffn_conv_w, ffn_conv_b, ffn_w_out]))
    mom_m = dict(zip(names, [m_ret_norm, m_ret_w_in, m_ret_gn, m_ret_w_out, m_mla_norm, m_mla_w_in, m_mla_q_norm,
                             m_mla_w_qb, m_mla_kv_norm, m_mla_w_kvb, m_mla_q_head_norm, m_mla_k_head_norm, m_mla_w_out,
                             m_ffn_norm, m_ffn_w_in, m_ffn_conv_w, m_ffn_conv_b, m_ffn_w_out]))
    mom_v = dict(zip(names, [v_ret_norm, v_ret_w_in, v_ret_gn, v_ret_w_out, v_mla_norm, v_mla_w_in, v_mla_q_norm,
                             v_mla_w_qb, v_mla_kv_norm, v_mla_w_kvb, v_mla_q_head_norm, v_mla_k_head_norm, v_mla_w_out,
                             v_ffn_norm, v_ffn_w_in, v_ffn_conv_w, v_ffn_conv_b, v_ffn_w_out]))
    B, S, D = x.shape
    T = B * S
    sx, sy = lax.axis_index("x"), lax.axis_index("y")
    me = 2 * sx + sy

    two_d = lambda a: a.reshape(-1, a.shape[-1])
    small_sizes = [int(np.prod(shard[n].shape)) for n, _ in _SMALL_SHARDED]
    small = jnp.concatenate([shard[n].reshape(1, -1) for n, _ in _SMALL_SHARDED], axis=1)
    small = _pad_rows(small, LANES, 8)[0]
    as_mxu = lambda a: two_d(a).astype(BF16)
    is_me = lax.broadcasted_iota(jnp.int32, (N_SHARD, 1, 1), 0) == me
    with_own = lambda gathered, own: jnp.where(is_me, own[None], gathered)
    by_cols = lambda a: jnp.moveaxis(a, 0, 1).reshape(a.shape[1], -1)
    by_rows = lambda a: a.reshape(-1, a.shape[-1])
    pad_in = lambda a: jnp.pad(by_rows(a), ((0, 0), (0, MLA_IN_PAD - MLA_IN)))
    pad_qb = lambda a: _pad_heads(by_cols(a), 1)
    ret_in_shard = as_mxu(shard["ret_w_in"])
    g_ret_in, gsmall = _all_gather_weights([ret_in_shard], small)
    later = [
        ("ret_out", [("ret_w_out", as_mxu(shard["ret_w_out"]), by_rows)]),
        ("ffn0", [("ffn_w_in0", as_mxu(shard["ffn_w_in"][0]), by_cols), ("ffn_w_out0", as_mxu(shard["ffn_w_out"][0]), by_rows)]),
        ("mla", [("mla_w_in", as_mxu(shard["mla_w_in"]), pad_in), ("mla_w_qb", as_mxu(shard["mla_w_qb"]), pad_qb),
                 ("mla_w_kvb", as_mxu(shard["mla_w_kvb"]), by_cols), ("mla_w_out", as_mxu(shard["mla_w_out"]), by_rows)]),
        ("ffn1", [("ffn_w_in1", as_mxu(shard["ffn_w_in"][1]), by_cols), ("ffn_w_out1", as_mxu(shard["ffn_w_out"][1]), by_rows)]),
    ]
    gathering = {}
    token = gsmall
    for group, items in later:
        shards = [s_ for _, s_, _ in items]
        lands = [lax.empty((N_SHARD,) + s_.shape, s_.dtype) for s_ in shards]
        send_sems, recv_sems, shards, lands, token = _exchange_start(
            _weight_copies, shards, lands, 3 * len(shards), f"weights_start_{group}", after=token)
        gathering[group] = (send_sems, recv_sems, shards, lands, items)

    def late(group, after):
        send_sems, recv_sems, shards, lands, items = gathering[group]
        shards, lands = _exchange_wait(_weight_copies, send_sems, recv_sems, shards, lands, after,
                                       f"weights_wait_{group}")
        return {key: full(with_own(l_, s_)) for (key, _, full), s_, l_ in zip(items, shards, lands)}

    gsmall = with_own(gsmall, small).reshape(N_SHARD, -1)
    wfull = {}
    off = 0
    for (n, ax), sz in zip(_SMALL_SHARDED, small_sizes):
        wfull[n] = _from_slots(gsmall[:, off:off + sz], shard[n].shape, ax)
        off += sz
    for n in _SMALL_REPLICATED:
        wfull[n] = shard[n]

    conv8 = jnp.concatenate([wfull["ffn_conv_w"], wfull["ffn_conv_b"][:, None, :],
                             jnp.zeros((2, 4, FFN_DIM), F32)], axis=1)
    w = {
        "started": token, "ret_norm": wfull["ret_norm"], "ret_w_in": by_cols(with_own(g_ret_in, ret_in_shard)),
        "ret_gn": wfull["ret_gn"].reshape(1, RET_HEADS * RET_V), "mla_norm": wfull["mla_norm"],
        "mla_q_norm": wfull["mla_q_norm"], "mla_kv_norm": wfull["mla_kv_norm"],
        "mla_q_head_norm": jnp.pad(wfull["mla_q_head_norm"], ((0, 0), (0, MLA_PAD - MLA_QK))),
        "mla_k_head_norm": jnp.pad(wfull["mla_k_head_norm"], ((0, 0), (0, MLA_PAD - MLA_QK))),
        "ffn_norm": wfull["ffn_norm"], "ffn_conv8": conv8,
    }

    started = {}

    def exchange(group, arrays):
        lands = [lax.empty((N_PEERS, p.shape[1] // 2, p.shape[2]), p.dtype) for p in arrays]
        send_sems, recv_sems, ps, lands, token = _exchange_start(
            _grad_copies, arrays, lands, N_PEERS * len(arrays), f"grads_start_{group}")
        started[group] = (send_sems, recv_sems, ps, lands)
        return token

    loss_part, dx, gl = _local_step(x.reshape(T, D), loss_target.reshape(T, D), w, B, S, late, exchange)
    loss = lax.psum(loss_part, ("x", "y", "c"))
    gfull = {
        "ret_norm": gl["ret_norm"], "ret_gn": gl["ret_gn"].reshape(1, RET_HEADS, RET_V),
        "mla_norm": gl["mla_norm"], "mla_q_norm": gl["mla_q_norm"], "mla_kv_norm": gl["mla_kv_norm"],
        "mla_q_head_norm": gl["mla_q_head_norm"][:, :MLA_QK], "mla_k_head_norm": gl["mla_k_head_norm"][:, :MLA_QK],
        "ffn_norm": gl["ffn_norm"], "ffn_conv_w": gl["ffn_conv_w"], "ffn_conv_b": gl["ffn_conv_b"],
    }

    red = {}
    after = dx
    for group in ("ffn1", "mla", "ffn0", "ret"):
        send_sems, recv_sems, ps, lands = started[group]
        ps, lands = _exchange_wait(_grad_copies, send_sems, recv_sems, ps, lands, after, f"grads_wait_{group}")
        halves = [_sum_partials(p_, l_, f"grads_sum_{group}_{i}") for i, (p_, l_) in enumerate(zip(ps, lands))]
        red[group] = [two_d(r) for r in _sibling_share(halves, f"grads_share_{group}")]
        after = red[group][0]
    grads = {"ret_w_in": red["ret"][0], "ret_w_out": red["ret"][1], "mla_w_in": red["mla"][0],
             "mla_w_qb": red["mla"][1], "mla_w_kvb": red["mla"][2], "mla_w_out": red["mla"][3]}
    grads = {n: a.reshape(shard[n].shape) for n, a in grads.items()}
    grads["ffn_w_in"] = jnp.stack([red["ffn0"][0], red["ffn1"][0]])
    grads["ffn_w_out"] = jnp.stack([red["ffn0"][1], red["ffn1"][1]])

    small_sizes_all = [int(np.prod(gfull[n].shape)) for n in _SMALL_ALL]
    gsm = jnp.concatenate([gfull[n].reshape(1, -1) for n in _SMALL_ALL], axis=1)
    gsm = _all_reduce_small(_pad_rows(gsm, LANES, 8)[0]).reshape(-1)

    sharded_axis = dict(_SMALL_SHARDED)
    off = 0
    for n, sz in zip(_SMALL_ALL, small_sizes_all):
        gn = gsm[off:off + sz].reshape(gfull[n].shape)
        off += sz
        if n in sharded_axis:
            ax = sharded_axis[n]
            width = shard[n].shape[ax]
            gn = lax.dynamic_slice_in_dim(gn, me * width, width, axis=ax)
        grads[n] = gn

    delta, new_m, new_v = {}, {}, {}
    for n, _ in _BIG:
        shp = shard[n].shape
        two_d = lambda a: a.reshape(-1, shp[-1])
        d_, m_, v_ = _adamw(two_d(shard[n]), two_d(grads[n]), two_d(mom_m[n]), two_d(mom_v[n]), f"adamw_{n}")
        delta[n], new_m[n], new_v[n] = d_.reshape(shp), m_.reshape(shp), v_.reshape(shp)
    pack_small = lambda d: _pad_rows(jnp.concatenate([d[n].reshape(1, -1) for n in _SMALL_ALL], axis=1), LANES, 8)[0]
    d_, m_, v_ = _adamw(pack_small(shard), pack_small(grads), pack_small(mom_m), pack_small(mom_v), "adamw_small")
    off = 0
    for n in _SMALL_ALL:
        sz = int(np.prod(shard[n].shape))
        for dst, src in ((delta, d_), (new_m, m_), (new_v, v_)):
            dst[n] = src.reshape(-1)[off:off + sz].reshape(shard[n].shape)
        off += sz

    return (loss, dx.reshape(B, S, D), *[grads[n] for n in names], *[delta[n] for n in names],
            *[new_m[n] for n in names], *[new_v[n] for n in names])
```

```python
import functools
import math

import numpy as np
import jax
import jax.numpy as jnp
from jax import lax
from jax.experimental import pallas as pl
from jax.experimental.pallas import tpu as pltpu

F32 = jnp.float32
BF16 = jnp.bfloat16
MXU_DTYPE = jnp.bfloat16

CHUNK = 64
RMS_EPS = 1e-6
ROPE_THETA = 10000.0
D_MODEL = 1024
RET_HEADS = 4
RET_QK = 256
RET_V = 512
RET_GAMMA_BASE = -5.0
MLA_HEADS = 8
MLA_Q_RANK = 384
MLA_KV_RANK = 256
MLA_NOPE = 128
MLA_ROPE = 64
MLA_V = 128
MLA_QK = MLA_NOPE + MLA_ROPE
MLA_PAD = 256
MLA_IN = MLA_Q_RANK + MLA_KV_RANK + MLA_ROPE
MLA_IN_PAD = MLA_IN + 64
MASK_VALUE = -1e30
FFN_DIM = 2816
ADAM_LR = 0.001
ADAM_B1 = 0.9
ADAM_B2 = 0.999
ADAM_EPS = 1e-08
ADAM_WD = 0.01
ADAM_STEP = 10

LANES = 128
ATT_BLOCK = 256
MLA_FWD_BLOCK = 512
VMEM_LIMIT = 56 * 2 ** 20
N_SHARD = 4
N_DEV = 8

MESH = pl.DeviceIdType.MESH


def _params(sem=None, **kw):
    return pltpu.CompilerParams(dimension_semantics=sem, vmem_limit_bytes=VMEM_LIMIT, **kw)


def _pick(dim, target):
    if dim <= target:
        return dim
    best = None
    for d in range(LANES, target + 1, LANES):
        if dim % d == 0:
            best = d
    assert best is not None, (dim, target)
    return best


def _mm(a, b, dims, out_dtype, name, residual=None, bm=512, bn=1024, bk=2048, out_slots=None, after=None,
        cols_outer=False):
    a_parts = list(a) if isinstance(a, (list, tuple)) else [a]
    b_parts = list(b) if isinstance(b, (list, tuple)) else [b]
    parts_on_n = dims == "tn" or len(b_parts) > 1
    if parts_on_n:
        assert len(a_parts) == 1 and dims in ("tn", "nn")
        (K, M) = a_parts[0].shape if dims == "tn" else a_parts[0].shape[::-1]
        N = sum(p.shape[1] for p in b_parts)
        part_widths = [p.shape[1] for p in b_parts]
    else:
        assert len(b_parts) == 1
        M = a_parts[0].shape[0]
        K = sum(p.shape[1] for p in a_parts)
        N = b_parts[0].shape[1 if dims == "nn" else 0]
        part_widths = [p.shape[1] for p in a_parts]
    bm, bn, bk = _pick(M, bm), _pick(N, bn), _pick(K, min(bk, 1024) if dims == "tn" else bk)
    nk = K // bk
    unit = bn if parts_on_n else bk
    assert all(wd % unit == 0 for wd in part_widths), (name, part_widths, unit)
    bounds = np.cumsum([0] + [wd // unit for wd in part_widths])
    ranges = [(int(lo), int(hi)) for lo, hi in zip(bounds[:-1], bounds[1:])]

    def part_index(idx, lo, hi):
        return jnp.clip(idx - lo, 0, hi - lo - 1)

    if parts_on_n:
        if dims == "tn":
            a_specs = [pl.BlockSpec((bk, bm), lambda i, j, k: (k, i))]
            dn = (((0,), (0,)), ((), ()))
        else:
            a_specs = [pl.BlockSpec((bm, bk), lambda i, j, k: (i, k))]
            dn = (((1,), (0,)), ((), ()))
        b_specs = [pl.BlockSpec((bk, bn), functools.partial(lambda i, j, k, lo, hi: (k, part_index(j, lo, hi)), lo=lo, hi=hi))
                   for lo, hi in ranges]
    else:
        a_specs = [pl.BlockSpec((bm, bk), functools.partial(lambda i, j, k, lo, hi: (i, part_index(k, lo, hi)), lo=lo, hi=hi))
                   for lo, hi in ranges]
        if dims == "nt":
            b_specs = [pl.BlockSpec((bn, bk), lambda i, j, k: (j, k))]
        else:
            b_specs = [pl.BlockSpec((bk, bn), lambda i, j, k: (k, j))]
        dn = (((1,), (1 if dims == "nt" else 0,)), ((), ()))
    r_spec = pl.BlockSpec((bm, bn), lambda i, j, k: (i, j))
    if out_slots is None:
        o_spec, o_shape = r_spec, (M, N)
    else:
        ns = N // out_slots
        assert ns % bn == 0, (name, ns, bn)
        nbs = ns // bn
        o_spec = pl.BlockSpec((None, bm, bn), lambda i, j, k: (j // nbs, i, j % nbs))
        o_shape = (out_slots, M, ns)
    has_res = residual is not None
    na, nb = len(a_parts), len(b_parts)

    def body(*refs):
        a_refs, b_refs = refs[:na], refs[na:na + nb]
        r_ref = refs[na + nb] if has_res else None
        n_in = na + nb + has_res + (after is not None)
        o_ref = refs[n_in]
        acc_ref = refs[n_in + 1] if nk > 1 else None
        k = pl.program_id(2)

        def finish(acc):
            if has_res:
                acc = acc + r_ref[...].astype(F32)
            o_ref[...] = acc.astype(out_dtype)

        def compute(a_ref, b_ref):
            p = lax.dot_general(a_ref[...].astype(MXU_DTYPE), b_ref[...].astype(MXU_DTYPE), dn,
                                preferred_element_type=F32)
            if nk == 1:
                finish(p)
                return

            @pl.when(k == 0)
            def _():
                acc_ref[...] = p

            @pl.when(jnp.logical_and(k > 0, k < nk - 1))
            def _():
                acc_ref[...] += p

            @pl.when(k == nk - 1)
            def _():
                finish(acc_ref[...] + p)

        if len(ranges) == 1:
            compute(a_refs[0], b_refs[0])
        else:
            idx = pl.program_id(0 if cols_outer else 1) if parts_on_n else k
            for p, (lo, hi) in enumerate(ranges):
                @pl.when(jnp.logical_and(idx >= lo, idx < hi))
                def _(p=p):
                    compute(a_refs[0 if parts_on_n else p], b_refs[p if parts_on_n else 0])

    after_specs = [] if after is None else [pl.BlockSpec(after.shape, lambda i, j, k: (0, 0))]
    in_specs = a_specs + b_specs + ([r_spec] if has_res else []) + after_specs
    grid = (M // bm, N // bn, nk)
    if cols_outer:
        swap = lambda sp: pl.BlockSpec(sp.block_shape, functools.partial(lambda j, i, k, f: f(i, j, k), f=sp.index_map))
        in_specs, o_spec, grid = [swap(sp) for sp in in_specs], swap(o_spec), (grid[1], grid[0], nk)
    return pl.pallas_call(
        body, name=name, grid=grid,
        in_specs=in_specs, out_specs=o_spec,
        out_shape=jax.ShapeDtypeStruct(o_shape, out_dtype),
        scratch_shapes=[pltpu.VMEM((bm, bn), F32)] if nk > 1 else [],
        compiler_params=_params(("parallel", "parallel", "arbitrary")),
    )(*a_parts, *b_parts, *((residual,) if has_res else ()), *(() if after is None else (after,)))


def _mm_dx(a_parts, w, name, bm=512, after=None):
    M = a_parts[0].shape[0]
    N, K = w.shape
    widths = [p.shape[1] for p in a_parts]
    assert sum(widths) == K, (name, widths, K)
    offs = [int(o) for o in np.cumsum([0] + widths[:-1])]
    bm = _pick(M, bm)
    na = len(a_parts)

    def body(*refs):
        w_ref = refs[na]
        o_ref = refs[na + 1 + (after is not None)]
        acc = None
        for a_ref, off, wd in zip(refs[:na], offs, widths):
            p = lax.dot_general(a_ref[...].astype(MXU_DTYPE), w_ref[:, off:off + wd].astype(MXU_DTYPE), _NT,
                                preferred_element_type=F32)
            acc = p if acc is None else acc + p
        o_ref[...] = acc

    in_specs = [pl.BlockSpec((bm, wd), lambda i: (i, 0)) for wd in widths] + [pl.BlockSpec((N, K), lambda i: (0, 0))]
    in_specs += [] if after is None else [pl.BlockSpec(after.shape, lambda i: (0, 0))]
    return pl.pallas_call(
        body, name=name, grid=(M // bm,),
        in_specs=in_specs, out_specs=pl.BlockSpec((bm, N), lambda i: (i, 0)),
        out_shape=jax.ShapeDtypeStruct((M, N), F32),
        compiler_params=_params(("parallel",)),
    )(*a_parts, w, *(() if after is None else (after,)))


def _tiles(ref, width, tile):
    return [ref[:, t * tile:(t + 1) * tile].astype(F32) for t in range(width // tile)]


def _row_specs(rows, pos, consts, bm, S):
    npos_blocks = S // bm
    specs = [pl.BlockSpec((bm, w), functools.partial(lambda i, c: (i, c), c=cb)) for (_, w, cb, _) in rows]
    specs += [pl.BlockSpec((bm, p.shape[1]), lambda i: (i % npos_blocks, 0)) for p in pos]
    specs += [pl.BlockSpec(c.shape, lambda i: (0, 0)) for (c, _) in consts]
    return specs


def _rowwise_fwd(fn, name, rows, pos, consts, outs, bm, S, transposed=()):
    T = rows[0][0].shape[0]
    nr, npos, nc, no = len(rows), len(pos), len(consts), len(outs)

    def body(*refs):
        row_v = [_tiles(r, w, t) for r, (_, w, _, t) in zip(refs[:nr], rows)]
        pos_v = [r[...] for r in refs[nr:nr + npos]]
        const_v = [_tiles(r, c.shape[1], t) for r, (c, t) in zip(refs[nr + npos:nr + npos + nc], consts)]
        res = fn(row_v, pos_v, const_v)
        out_refs = refs[nr + npos + nc:]
        for o_ref, tiles, (w, t, dt) in zip(out_refs, res, outs):
            for k, v in enumerate(tiles):
                o_ref[:, k * t:(k + 1) * t] = v.astype(dt)
        for t_ref, a in zip(out_refs[no:], transposed):
            t = outs[a][1]
            for k, v in enumerate(res[a]):
                t_ref[k * t:(k + 1) * t, :] = v.T.astype(t_ref.dtype)

    return pl.pallas_call(
        body, name=name, grid=(T // bm,),
        in_specs=_row_specs(rows, pos, consts, bm, S),
        out_specs=[pl.BlockSpec((bm, w), lambda i: (i, 0)) for (w, _, _) in outs]
        + [pl.BlockSpec((outs[a][0], bm), lambda i: (0, i)) for a in transposed],
        out_shape=[jax.ShapeDtypeStruct((T, w), dt) for (w, _, dt) in outs]
        + [jax.ShapeDtypeStruct((outs[a][0], T), BF16) for a in transposed],
        compiler_params=_params(("parallel",)),
    )(*[r[0] for r in rows], *pos, *[c[0] for c in consts])


def _rowwise_bwd(fn, name, rows, pos, consts, cts, bm, S, adds=None, grad_dtypes=None, mxu_copies=(), linear=False):
    adds = adds or {}
    T = rows[0][0].shape[0]
    nr, npos, nc, nct = len(rows), len(pos), len(consts), len(cts)
    add_idx = sorted(adds)
    grad_dtypes = grad_dtypes or [F32] * nr

    def body(*refs):
        it = iter(refs)
        row_refs = [None if linear else next(it) for _ in range(nr)]
        pos_refs = [next(it) for _ in range(npos)]
        const_refs = [next(it) for _ in range(nc)]
        ct_refs = [next(it) for _ in range(nct)]
        add_refs = {k: next(it) for k in add_idx}
        drow_refs = [next(it) for _ in range(nr)]
        copy_refs = {a: next(it) for a in mxu_copies}
        dconst_refs = [next(it) for _ in range(nc)]
        if linear:
            row_v = [[jnp.zeros((bm, t), F32)] * (w // t) for (_, w, _, t) in rows]
        else:
            row_v = [_tiles(r, w, t) for r, (_, w, _, t) in zip(row_refs, rows)]
        pos_v = [r[...] for r in pos_refs]
        const_v = [_tiles(r, c.shape[1], t) for r, (c, t) in zip(const_refs, consts)]
        ct_v = [_tiles(r, c.shape[1], t) for r, (c, t) in zip(ct_refs, cts)]
        _, vjp = jax.vjp(lambda rv, cv: fn(rv, pos_v, cv), row_v, const_v)
        drows, dconsts = vjp(ct_v)
        for a, (d_ref, tiles, (_, w, _, t)) in enumerate(zip(drow_refs, drows, rows)):
            for k, v in enumerate(tiles):
                if a in add_refs:
                    v = v + add_refs[a][:, k * t:(k + 1) * t].astype(F32)
                d_ref[:, k * t:(k + 1) * t] = v.astype(d_ref.dtype)
                if a in copy_refs:
                    copy_refs[a][:, k * t:(k + 1) * t] = v.astype(BF16)
        first = pl.program_id(0) == 0
        for d_ref, tiles, (_, t) in zip(dconst_refs, dconsts, consts):
            for k, v in enumerate(tiles):
                @pl.when(first)
                def _(d_ref=d_ref, k=k, t=t, v=v):
                    d_ref[:, k * t:(k + 1) * t] = v

                @pl.when(jnp.logical_not(first))
                def _(d_ref=d_ref, k=k, t=t, v=v):
                    d_ref[:, k * t:(k + 1) * t] += v

    in_specs = _row_specs([] if linear else rows, pos, consts, bm, S)
    in_specs += [pl.BlockSpec((bm, c.shape[1]), lambda i: (i, 0)) for (c, _) in cts]
    in_specs += [pl.BlockSpec((bm, adds[k].shape[1]), lambda i: (i, 0)) for k in add_idx]
    out_specs = [pl.BlockSpec((bm, w), lambda i: (i, 0)) for (_, w, _, _) in rows]
    out_specs += [pl.BlockSpec((bm, rows[a][1]), lambda i: (i, 0)) for a in mxu_copies]
    out_specs += [pl.BlockSpec(c.shape, lambda i: (0, 0)) for (c, _) in consts]
    out_shape = [jax.ShapeDtypeStruct((T, w), dt) for (_, w, _, _), dt in zip(rows, grad_dtypes)]
    out_shape += [jax.ShapeDtypeStruct((T, rows[a][1]), BF16) for a in mxu_copies]
    out_shape += [jax.ShapeDtypeStruct(c.shape, F32) for (c, _) in consts]
    res = pl.pallas_call(
        body, name=name, grid=(T // bm,),
        in_specs=in_specs, out_specs=out_specs, out_shape=out_shape,
        compiler_params=_params(("arbitrary",)),
    )(*([] if linear else [r[0] for r in rows]), *pos, *[c[0] for c in consts], *[c[0] for c in cts],
      *[adds[k] for k in add_idx])
    n_rows = nr + len(mxu_copies)
    return res[:n_rows], res[n_rows:]


def _ssq(tiles):
    s = jnp.sum(tiles[0] * tiles[0], axis=-1, keepdims=True)
    for t in tiles[1:]:
        s = s + jnp.sum(t * t, axis=-1, keepdims=True)
    return s


def _sigmoid(x):
    return 1.0 / (1.0 + jnp.exp(-x))


def _fn_rms(rows, pos, consts):
    (x,), (g,) = rows[0], consts[0]
    r = lax.rsqrt(jnp.mean(x * x, axis=-1, keepdims=True) + RMS_EPS)
    return [[x * r * g]]


def _fn_ret_rope(rows, pos, consts):
    (qkv,) = rows
    nq = RET_HEADS * RET_QK // LANES
    q, k, v = qkv[:nq], qkv[nq:2 * nq], qkv[2 * nq:]
    cos, sin = pos

    def rot(t, scale):
        out = []
        for h in range(RET_HEADS):
            x1, x2 = t[2 * h], t[2 * h + 1]
            o1, o2 = x1 * cos - x2 * sin, x2 * cos + x1 * sin
            out += [o1, o2] if scale is None else [o1 * scale, o2 * scale]
        return out

    return [rot(q, None), rot(k, RET_QK ** -0.5), list(v)]


def _fn_ret_gate(rows, pos, consts):
    o, g = rows
    (gn,) = consts
    out = []
    for h in range(RET_HEADS):
        r = lax.rsqrt(jnp.mean(o[h] * o[h], axis=-1, keepdims=True) + RMS_EPS)
        out.append((o[h] * r * gn[h]) * (g[h] * _sigmoid(g[h])))
    return [out]


def _fn_mla_lat(rows, pos, consts):
    (p,) = rows
    gq, gkv = consts
    nq, nkv = MLA_Q_RANK // LANES, MLA_KV_RANK // LANES
    cq, ckv, kr = p[:nq], p[nq:nq + nkv], p[nq + nkv]
    rq = lax.rsqrt(_ssq(cq) / MLA_Q_RANK + RMS_EPS)
    rkv = lax.rsqrt(_ssq(ckv) / MLA_KV_RANK + RMS_EPS)
    return [[t * rq * g for t, g in zip(cq, gq)], [t * rkv * g for t, g in zip(ckv, gkv)], [kr]]


def _swap32_impl(x):
    lane = lax.broadcasted_iota(jnp.int32, x.shape, 1)
    up, down = pltpu.roll(x, LANES - 32, 1), pltpu.roll(x, 32, 1)
    return jnp.where(lane < 32, up, jnp.where(lane < 64, down, 0.0))


@jax.custom_vjp
def _swap32(x):
    return _swap32_impl(x)


_swap32.defvjp(lambda x: (_swap32_impl(x), None), lambda _, g: (_swap32_impl(g),))


def _fn_mla_heads(rows, pos, consts):
    qf, kvf, (kr,) = rows
    cos, sin = pos
    gq, gk = consts
    q_out, k_out, v_out = [], [], []
    for h in range(MLA_HEADS):
        q0, q1 = qf[2 * h], qf[2 * h + 1]
        r = lax.rsqrt(_ssq([q0, q1]) / MLA_QK + RMS_EPS)
        a0, a1 = q0 * r * gq[0], q1 * r * gq[1]
        a1 = a1 * cos + _swap32(a1) * sin
        q_out += [a0 * (MLA_QK ** -0.5), a1 * (MLA_QK ** -0.5)]
        k0 = kvf[2 * h]
        r = lax.rsqrt(_ssq([k0, kr]) / MLA_QK + RMS_EPS)
        b0, b1 = k0 * r * gk[0], kr * r * gk[1]
        k_out += [b0, b1 * cos + _swap32(b1) * sin]
        v_out.append(kvf[2 * h + 1])
    return [q_out, k_out, v_out]


def _shift_down(x, n):
    row = lax.broadcasted_iota(jnp.int32, x.shape, 0)
    return jnp.where(row >= n, pltpu.roll(x, n, 0), 0.0)


def _shift_up(x, n):
    rows = x.shape[0]
    row = lax.broadcasted_iota(jnp.int32, x.shape, 0)
    return jnp.where(row < rows - n, pltpu.roll(x, rows - n, 0), 0.0)


def _conv_blocks(S):
    cb = 256
    return cb, FFN_DIM // cb


def _conv_fwd(ag, w8, B, S, name):
    cb, ncb = _conv_blocks(S)

    def body(a_ref, g_ref, w_ref, u_ref, ut_ref):
        g = g_ref[...]
        w = w_ref[...]
        gc = w[0:1] * _shift_down(g, 2) + w[1:2] * _shift_down(g, 1) + w[2:3] * g + w[3:4]
        u = a_ref[...] * (gc * _sigmoid(gc))
        u_ref[...] = u.astype(u_ref.dtype)
        ut_ref[...] = u.T.astype(ut_ref.dtype)

    return pl.pallas_call(
        body, name=name, grid=(ncb, B),
        in_specs=[pl.BlockSpec((S, cb), lambda j, b: (b, j)),
                  pl.BlockSpec((S, cb), lambda j, b: (b, ncb + j)),
                  pl.BlockSpec((8, cb), lambda j, b: (0, j))],
        out_specs=[pl.BlockSpec((S, cb), lambda j, b: (b, j)), pl.BlockSpec((cb, S), lambda j, b: (j, b))],
        out_shape=[jax.ShapeDtypeStruct((B * S, FFN_DIM), BF16), jax.ShapeDtypeStruct((FFN_DIM, B * S), BF16)],
        compiler_params=_params(("parallel", "parallel")),
    )(ag, ag, w8)


def _conv_bwd(ag, w8, du, B, S, name):
    cb, ncb = _conv_blocks(S)

    def body(a_ref, g_ref, w_ref, du_ref, da_ref, dg_ref, dw_ref):
        g = g_ref[...]
        w = w_ref[...]
        g1, g2 = _shift_down(g, 1), _shift_down(g, 2)
        gc = w[0:1] * g2 + w[1:2] * g1 + w[2:3] * g + w[3:4]
        sg = _sigmoid(gc)
        du_v = du_ref[...]
        da_ref[...] = (du_v * (gc * sg)).astype(da_ref.dtype)
        dgc = du_v * a_ref[...] * (sg * (1.0 + gc * (1.0 - sg)))
        dg = w[2:3] * dgc + w[1:2] * _shift_up(dgc, 1) + w[0:1] * _shift_up(dgc, 2)
        dg_ref[...] = dg.astype(dg_ref.dtype)
        part = jnp.concatenate([
            jnp.sum(dgc * g2, axis=0, keepdims=True), jnp.sum(dgc * g1, axis=0, keepdims=True),
            jnp.sum(dgc * g, axis=0, keepdims=True), jnp.sum(dgc, axis=0, keepdims=True),
            jnp.zeros((4, cb), F32)], axis=0)

        @pl.when(pl.program_id(1) == 0)
        def _():
            dw_ref[...] = part

        @pl.when(pl.program_id(1) > 0)
        def _():
            dw_ref[...] += part

    blk = lambda j, b: (b, j)
    return pl.pallas_call(
        body, name=name, grid=(ncb, B),
        in_specs=[pl.BlockSpec((S, cb), blk),
                  pl.BlockSpec((S, cb), lambda j, b: (b, ncb + j)),
                  pl.BlockSpec((8, cb), lambda j, b: (0, j)),
                  pl.BlockSpec((S, cb), blk)],
        out_specs=[pl.BlockSpec((S, cb), blk), pl.BlockSpec((S, cb), blk),
                   pl.BlockSpec((8, cb), lambda j, b: (0, j))],
        out_shape=[jax.ShapeDtypeStruct((B * S, FFN_DIM), BF16), jax.ShapeDtypeStruct((B * S, FFN_DIM), BF16),
                   jax.ShapeDtypeStruct((8, FFN_DIM), F32)],
        compiler_params=_params(("parallel", "arbitrary")),
    )(ag, ag, w8, du)


_NT = (((1,), (1,)), ((), ()))
_NN = (((1,), (0,)), ((), ()))
_TN = (((0,), (0,)), ((), ()))


def _dot(a, b, dn):
    return lax.dot_general(a.astype(MXU_DTYPE), b.astype(MXU_DTYPE), dn, preferred_element_type=F32)


def _rel_and_mask():
    il = lax.broadcasted_iota(jnp.int32, (ATT_BLOCK, ATT_BLOCK), 0)
    jl = lax.broadcasted_iota(jnp.int32, (ATT_BLOCK, ATT_BLOCK), 1)
    return (il - jl).astype(F32), (jl // CHUNK) <= (il // CHUNK)


def _rows(i):
    return pl.ds(pl.multiple_of(i * ATT_BLOCK, ATT_BLOCK), ATT_BLOCK)


def _run_bits(n):
    bits, b = [], 1
    while b < n:
        bits.append(b)
        b *= 2
    return bits[::-1]


def _key_runs(n, nq, update):
    for bit in _run_bits(nq + 1):
        @pl.when((n & bit) != 0)
        def _(bit=bit):
            update(n & ~(2 * bit - 1), bit, (n & (bit - 1)) == 0)


def _earlier_runs(n, nq, update):
    for bit in _run_bits(nq):
        @pl.when((n & bit) != 0)
        def _(bit=bit):
            update(n & ~(2 * bit - 1), bit, False)


def _chunk_visible(shape, nblk, blk):
    key = lax.broadcasted_iota(jnp.int32, shape, 0) - (nblk - 1) * blk
    query = lax.broadcasted_iota(jnp.int32, shape, 1)
    return jnp.logical_or(key < 0, (key // CHUNK) <= (query // CHUNK))


KV_UNROLL = 2


def _kv_loop(n, body, carry):
    main = n // KV_UNROLL

    def chunk(t, c):
        for u in range(KV_UNROLL):
            c = body(t * KV_UNROLL + u, c)
        return c

    carry = lax.fori_loop(0, main, chunk, carry)
    return lax.fori_loop(main * KV_UNROLL, n, body, carry)


def _mla_attn_fwd(q, k, v, B, S):
    blk = min(MLA_FWD_BLOCK, S)
    H, nq = MLA_HEADS, S // blk

    def body(q_ref, k_ref, v_ref, o_ref, lse_ref, m_ref, l_ref, acc_ref):
        def qblock(i, _):
            q_rows = pl.ds(pl.multiple_of(i * blk, blk), blk)
            qi = q_ref[q_rows, :]
            m_ref[...] = jnp.full(m_ref.shape, MASK_VALUE, F32)
            l_ref[...] = jnp.zeros(l_ref.shape, F32)
            acc_ref[...] = jnp.zeros(acc_ref.shape, F32)

            def keys(first, nblk, last):
                rows = pl.ds(pl.multiple_of(first * blk, blk), nblk * blk)
                s = _dot(k_ref[rows, :], qi, _NT)
                s = jnp.where(jnp.logical_or(_chunk_visible(s.shape, nblk, blk), jnp.logical_not(last)), s, MASK_VALUE)
                m = m_ref[...]
                m2 = jnp.maximum(m, jnp.max(s, axis=0, keepdims=True))
                alpha = jnp.exp(m - m2)
                p = jnp.exp(s - m2)
                l_ref[...] = alpha * l_ref[...] + jnp.sum(p, axis=0, keepdims=True)
                acc_ref[...] = alpha * acc_ref[...] + _dot(v_ref[rows, :], p, _TN)
                m_ref[...] = m2

            _key_runs(i + 1, nq, keys)
            l = l_ref[...]
            o_ref[q_rows, :] = (acc_ref[...] / l).T
            lse_ref[0, :, q_rows] = m_ref[...] + jnp.log(l)
            return 0

        lax.fori_loop(0, nq, qblock, 0)

    return pl.pallas_call(
        body, name="mla_attn_fwd", grid=(B, H),
        in_specs=[pl.BlockSpec((S, MLA_PAD), lambda b, h: (b, h)),
                  pl.BlockSpec((S, MLA_PAD), lambda b, h: (b, h)),
                  pl.BlockSpec((S, MLA_V), lambda b, h: (b, h))],
        out_specs=[pl.BlockSpec((S, MLA_V), lambda b, h: (b, h)),
                   pl.BlockSpec((1, 1, S), lambda b, h: (b * H + h, 0, 0))],
        out_shape=[jax.ShapeDtypeStruct((B * S, H * MLA_V), F32), jax.ShapeDtypeStruct((B * H, 1, S), F32)],
        scratch_shapes=[pltpu.VMEM((1, blk), F32), pltpu.VMEM((1, blk), F32), pltpu.VMEM((MLA_V, blk), F32)],
        compiler_params=_params(("parallel", "parallel")),
    )(q, k, v)


def _mla_attn_bwd(q, k, v, o, do, lse, B, S):
    blk = min(MLA_FWD_BLOCK, S)
    H, nq = MLA_HEADS, S // blk

    def body(q_ref, k_ref, v_ref, o_ref, do_ref, lse_ref, dq_ref, dk_ref, dv_ref, kt_ref, dqt_ref):
        dk_ref[...] = jnp.zeros(dk_ref.shape, F32)
        dv_ref[...] = jnp.zeros(dv_ref.shape, F32)
        for g in range(nq):
            kt_ref[g] = k_ref[g * blk:(g + 1) * blk, :].T

        def qblock(i, _):
            q_rows = pl.ds(pl.multiple_of(i * blk, blk), blk)
            qi = q_ref[q_rows, :]
            doi = do_ref[q_rows, :]
            delta = jnp.sum((doi * o_ref[q_rows, :]).T, axis=0, keepdims=True)
            lse_i = lse_ref[0, :, q_rows]
            doi = doi.astype(MXU_DTYPE)
            dqt_ref[...] = jnp.zeros(dqt_ref.shape, F32)

            def keys(first, nblk, last):
                rows = pl.ds(pl.multiple_of(first * blk, blk), nblk * blk)
                k_run, v_run = k_ref[rows, :], v_ref[rows, :]
                p = jnp.exp(_dot(k_run, qi, _NT) - lse_i)
                p = jnp.where(jnp.logical_or(_chunk_visible(p.shape, nblk, blk), jnp.logical_not(last)), p, 0.0)
                ds = (p * (_dot(v_run, doi, _NT) - delta)).astype(MXU_DTYPE)
                dk_ref[rows, :] += _dot(ds, qi, _NN)
                dv_ref[rows, :] += _dot(p, doi, _NN)
                for r in range(nblk):
                    dqt_ref[...] += _dot(kt_ref[first + r], ds[r * blk:(r + 1) * blk, :], _NN)

            _key_runs(i + 1, nq, keys)
            dq_ref[q_rows, :] = dqt_ref[...].T
            return 0

        lax.fori_loop(0, nq, qblock, 0)

    qk_spec = pl.BlockSpec((S, MLA_PAD), lambda b, h: (b, h))
    v_spec = pl.BlockSpec((S, MLA_V), lambda b, h: (b, h))
    return pl.pallas_call(
        body, name="mla_attn_bwd", grid=(B, H),
        in_specs=[qk_spec, qk_spec, v_spec, v_spec, v_spec,
                  pl.BlockSpec((1, 1, S), lambda b, h: (b * H + h, 0, 0))],
        out_specs=[qk_spec, qk_spec, v_spec],
        out_shape=[jax.ShapeDtypeStruct((B * S, H * MLA_PAD), F32), jax.ShapeDtypeStruct((B * S, H * MLA_PAD), F32),
                   jax.ShapeDtypeStruct((B * S, H * MLA_V), F32)],
        scratch_shapes=[pltpu.VMEM((nq, MLA_PAD, blk), q.dtype), pltpu.VMEM((MLA_PAD, blk), F32)],
        compiler_params=_params(("parallel", "parallel")),
    )(q, k, v, o, do, lse)


def _ret_log_gamma():
    lg = np.log1p(-np.exp2(RET_GAMMA_BASE - np.arange(RET_HEADS, dtype=np.float32))).astype(np.float32)
    return jnp.asarray(np.broadcast_to(lg[:, None, None], (RET_HEADS, 8, LANES)).copy())


RET_BLOCK = 512


def _ret_local_scale(lg, shape, blk, rising):
    local = lax.broadcasted_iota(jnp.int32, shape, 0) % blk
    return jnp.exp(lg * (local if rising else blk - 1 - local).astype(F32))


def _ret_pair_factor(lg, blk, steps):
    return jnp.exp(lg * (blk * (steps - 1) + 1).astype(F32))


def _ret_own_decay(lg, blk, transposed):
    a = lax.broadcasted_iota(jnp.int32, (blk, blk), 0)
    b = lax.broadcasted_iota(jnp.int32, (blk, blk), 1)
    query, key = (b, a) if transposed else (a, b)
    dec = jnp.exp(lg * jnp.abs(query - key).astype(F32))
    return jnp.where((key // CHUNK) <= (query // CHUNK), dec, 0.0)


def _ret_attn_fwd(q, k, v, B, S):
    blk = min(RET_BLOCK, S)
    H, nq = RET_HEADS, S // blk

    def body(lg_ref, q_ref, k_ref, v_ref, o_ref, ks_ref, dec_ref, acc_ref):
        lg = lg_ref[0, 0:1, 0:1]
        ks_ref[...] = (k_ref[...].astype(F32) * _ret_local_scale(lg, k_ref.shape, blk, False)).astype(ks_ref.dtype)
        dec_ref[...] = _ret_own_decay(lg, blk, False)

        def qblock(i, _):
            q_rows = pl.ds(pl.multiple_of(i * blk, blk), blk)
            qi = q_ref[q_rows, :]
            qs = (qi.astype(F32) * _ret_local_scale(lg, qi.shape, blk, True)).astype(qi.dtype)
            a = _dot(qi, k_ref[q_rows, :], _NT) * dec_ref[...]
            acc_ref[...] = _dot(a, v_ref[q_rows, :], _NN)

            def keys(first, nblk, _):
                rows = pl.ds(pl.multiple_of(first * blk, blk), nblk * blk)
                steps = i - first - lax.broadcasted_iota(jnp.int32, (1, nblk * blk), 1) // blk
                a = _dot(qs, ks_ref[rows, :], _NT) * _ret_pair_factor(lg, blk, steps)
                acc_ref[...] += _dot(a, v_ref[rows, :], _NN)

            _earlier_runs(i, nq, keys)
            o_ref[q_rows, :] = acc_ref[...]
            return 0

        lax.fori_loop(0, nq, qblock, 0)

    qk_spec = pl.BlockSpec((S, RET_QK), lambda b, h: (b, h))
    v_spec = pl.BlockSpec((S, RET_V), lambda b, h: (b, h))
    return pl.pallas_call(
        body, name="ret_attn_fwd", grid=(B, H),
        in_specs=[pl.BlockSpec((1, 8, LANES), lambda b, h: (h, 0, 0)), qk_spec, qk_spec, v_spec],
        out_specs=v_spec,
        out_shape=jax.ShapeDtypeStruct((B * S, H * RET_V), F32),
        scratch_shapes=[pltpu.VMEM((S, RET_QK), k.dtype), pltpu.VMEM((blk, blk), F32), pltpu.VMEM((blk, RET_V), F32)],
        compiler_params=_params(("parallel", "parallel")),
    )(_ret_log_gamma(), q, k, v)


def _ret_attn_bwd(q, k, v, do, B, S):
    blk = min(RET_BLOCK, S)
    H, nq = RET_HEADS, S // blk

    def body(lg_ref, q_ref, k_ref, v_ref, do_ref, dq_ref, dk_ref, dv_ref, ks_ref, kst_ref, dks_ref, dqt_ref, dec_ref):
        lg = lg_ref[0, 0:1, 0:1]
        dk_ref[...] = jnp.zeros(dk_ref.shape, F32)
        dv_ref[...] = jnp.zeros(dv_ref.shape, F32)
        dks_ref[...] = jnp.zeros(dks_ref.shape, F32)
        ks_ref[...] = (k_ref[...].astype(F32) * _ret_local_scale(lg, k_ref.shape, blk, False)).astype(ks_ref.dtype)
        for g in range(nq):
            kst_ref[g] = ks_ref[g * blk:(g + 1) * blk, :].T
        dec_ref[...] = _ret_own_decay(lg, blk, True)

        def qblock(i, _):
            q_rows = pl.ds(pl.multiple_of(i * blk, blk), blk)
            qi = q_ref[q_rows, :]
            q_scale = _ret_local_scale(lg, qi.shape, blk, True)
            qs = (qi.astype(F32) * q_scale).astype(qi.dtype)
            doi = do_ref[q_rows, :].astype(MXU_DTYPE)
            ki = k_ref[q_rows, :]
            dec = dec_ref[...]
            a = _dot(ki, qi, _NT) * dec
            da = (_dot(v_ref[q_rows, :], doi, _NT) * dec).astype(MXU_DTYPE)
            dv_ref[q_rows, :] += _dot(a, doi, _NN)
            dk_ref[q_rows, :] += _dot(da, qi, _NN)
            dq_own = _dot(da, ki, _TN)
            dqt_ref[...] = jnp.zeros(dqt_ref.shape, F32)

            def keys(first, nblk, _):
                for r in range(nblk):
                    g = first + r
                    rows = pl.ds(pl.multiple_of(g * blk, blk), blk)
                    c = _ret_pair_factor(lg, blk, i - g)
                    a = _dot(ks_ref[rows, :], qs, _NT) * c
                    da = (_dot(v_ref[rows, :], doi, _NT) * c).astype(MXU_DTYPE)
                    dv_ref[rows, :] += _dot(a, doi, _NN)
                    dks_ref[rows, :] += _dot(da, qs, _NN)
                    dqt_ref[...] += _dot(kst_ref[g], da, _NN)

            _earlier_runs(i, nq, keys)
            dq_ref[q_rows, :] = dqt_ref[...].T * q_scale + dq_own
            return 0

        lax.fori_loop(0, nq, qblock, 0)
        dk_ref[...] += dks_ref[...] * _ret_local_scale(lg, dks_ref.shape, blk, False)

    qk_spec = pl.BlockSpec((S, RET_QK), lambda b, h: (b, h))
    v_spec = pl.BlockSpec((S, RET_V), lambda b, h: (b, h))
    return pl.pallas_call(
        body, name="ret_attn_bwd", grid=(B, H),
        in_specs=[pl.BlockSpec((1, 8, LANES), lambda b, h: (h, 0, 0)), qk_spec, qk_spec, v_spec, v_spec],
        out_specs=[qk_spec, qk_spec, v_spec],
        out_shape=[jax.ShapeDtypeStruct((B * S, H * RET_QK), F32), jax.ShapeDtypeStruct((B * S, H * RET_QK), F32),
                   jax.ShapeDtypeStruct((B * S, H * RET_V), F32)],
        scratch_shapes=[pltpu.VMEM((S, RET_QK), k.dtype), pltpu.VMEM((nq, RET_QK, blk), k.dtype),
                        pltpu.VMEM((S, RET_QK), F32), pltpu.VMEM((RET_QK, blk), F32), pltpu.VMEM((blk, blk), F32)],
        compiler_params=_params(("parallel", "parallel")),
    )(_ret_log_gamma(), q, k, v, do)


def _loss_head(y, target, bm=512):
    T, D = y.shape
    bm = _pick(T, bm)

    def body(y_ref, t_ref, dy_ref, dyc_ref, l_ref):
        err = y_ref[...] - t_ref[...]
        dy_ref[...] = err / D
        dyc_ref[...] = (err / D).astype(dyc_ref.dtype)
        part = jnp.full((8, LANES), 0.5 * jnp.sum(jnp.mean(err * err, axis=-1)), F32)

        @pl.when(pl.program_id(0) == 0)
        def _():
            l_ref[...] = part

        @pl.when(pl.program_id(0) > 0)
        def _():
            l_ref[...] += part

    blk = pl.BlockSpec((bm, D), lambda i: (i, 0))
    dy, dyc, l = pl.pallas_call(
        body, name="loss_head", grid=(T // bm,),
        in_specs=[blk, blk], out_specs=[blk, blk, pl.BlockSpec((8, LANES), lambda i: (0, 0))],
        out_shape=[jax.ShapeDtypeStruct((T, D), F32), jax.ShapeDtypeStruct((T, D), BF16),
                   jax.ShapeDtypeStruct((8, LANES), F32)],
        compiler_params=_params(("arbitrary",)),
    )(y, target)
    return dy, dyc, l[0, 0]


def _adamw(w, g, m, v, name):
    R, C = w.shape
    br = R if R * C * 4 <= 2 ** 21 else _pick_rows(R, max(8, (2 ** 21) // (C * 4)))

    def body(w_ref, g_ref, m_ref, v_ref, d_ref, mo_ref, vo_ref):
        g_v = g_ref[...]
        m_v = ADAM_B1 * m_ref[...] + (1.0 - ADAM_B1) * g_v
        v_v = ADAM_B2 * v_ref[...] + (1.0 - ADAM_B2) * (g_v * g_v)
        m_hat = m_v / (1.0 - ADAM_B1 ** ADAM_STEP)
        v_hat = v_v / (1.0 - ADAM_B2 ** ADAM_STEP)
        d_ref[...] = -ADAM_LR * (m_hat / (jnp.sqrt(v_hat) + ADAM_EPS) + ADAM_WD * w_ref[...])
        mo_ref[...] = m_v
        vo_ref[...] = v_v

    blk = pl.BlockSpec((br, C), lambda i: (i, 0))
    return pl.pallas_call(
        body, name=name, grid=(R // br,),
        in_specs=[blk] * 4, out_specs=[blk] * 3,
        out_shape=[jax.ShapeDtypeStruct((R, C), F32)] * 3,
        compiler_params=_params(("parallel",)),
    )(w, g, m, v)


def _pick_rows(R, target):
    best = None
    for d in range(8, min(R, target) + 1, 8):
        if R % d == 0:
            best = d
    assert best is not None, (R, target)
    return best


def _position():
    return lax.axis_index("x"), lax.axis_index("y"), lax.axis_index("c")


HBM_SPEC = pl.BlockSpec(memory_space=pltpu.HBM)


def _other_chips(x, y):
    return [(1 - x, y), (x, 1 - y), (1 - x, 1 - y)]


def _all_gather_weights(bigs, small):
    nb = len(bigs)

    def body(*refs):
        big_refs, small_ref = refs[:nb], refs[nb]
        obig, osmall = refs[nb + 1:2 * nb + 1], refs[2 * nb + 1]
        ici_send, ici_recv, d2d_send, d2d_recv, sm_send, sm_recv = refs[2 * nb + 2:]
        x, y, c = _position()
        me = 2 * x + y
        chips = _other_chips(x, y)

        def rows(n, half):
            rh = bigs[n].shape[0] // 2
            return pl.ds(half * rh, rh)

        def over_ici(n, j, slot, from_shard):
            px, py = chips[j]
            dst = obig[n].at[slot, rows(n, c)]
            return pltpu.make_async_remote_copy(
                src_ref=big_refs[n].at[rows(n, c)] if from_shard else dst, dst_ref=dst,
                send_sem=ici_send.at[3 * n + j], recv_sem=ici_recv.at[3 * n + j],
                device_id=(px, py, c), device_id_type=MESH)

        def over_d2d(n, j, half):
            px, py = chips[j]
            part = obig[n].at[2 * px + py, rows(n, half)]
            return pltpu.make_async_remote_copy(
                src_ref=part, dst_ref=part, send_sem=d2d_send.at[3 * n + j], recv_sem=d2d_recv.at[3 * n + j],
                device_id=(x, y, 1 - c), device_id_type=MESH)

        def small_copy(j, slot):
            px, py = chips[j]
            return pltpu.make_async_remote_copy(
                src_ref=small_ref, dst_ref=osmall.at[slot], send_sem=sm_send.at[j], recv_sem=sm_recv.at[j],
                device_id=(px, py, c), device_id_type=MESH)

        sends = [over_ici(n, j, me, True) for n in range(nb) for j in range(3)]
        sends += [small_copy(j, me) for j in range(3)]
        for cp in sends:
            cp.start()
        passed = []
        for n in range(nb):
            for j, (px, py) in enumerate(chips):
                over_ici(n, j, 2 * px + py, False).wait_recv()
                fwd = over_d2d(n, j, c)
                fwd.start()
                passed.append(fwd)
        for n in range(nb):
            for j in range(3):
                over_d2d(n, j, 1 - c).wait_recv()
        for j, (px, py) in enumerate(chips):
            small_copy(j, 2 * px + py).wait_recv()
        for cp in sends + passed:
            cp.wait_send()

    dma = pltpu.SemaphoreType.DMA
    return pl.pallas_call(
        body, name="weights_all_gather",
        in_specs=[HBM_SPEC] * (nb + 1), out_specs=[HBM_SPEC] * (nb + 1),
        out_shape=[jax.ShapeDtypeStruct((N_SHARD,) + b.shape, b.dtype) for b in bigs]
        + [jax.ShapeDtypeStruct((N_SHARD,) + small.shape, small.dtype)],
        scratch_shapes=[dma((3 * nb,)), dma((3 * nb,)), dma((3 * nb,)), dma((3 * nb,)), dma((3,)), dma((3,))],
    )(*bigs, small)


SEM_SPEC = pl.BlockSpec(memory_space=pltpu.SEMAPHORE)
DATAFLOW_EFFECT = pltpu.SideEffectType.DATAFLOW_SIDE_EFFECTING
N_PEERS = N_DEV - 1


def _grad_copies(p_refs, land_refs, send_sems, recv_sems):
    x, y, c = _position()
    copies = []
    for a, (p_ref, land_ref) in enumerate(zip(p_refs, land_refs)):
        rh = p_ref.shape[1] // 2
        for k in range(1, N_DEV):
            px = 1 - x if k & 4 else x
            py = 1 - y if k & 2 else y
            pc = 1 - c if k & 1 else c
            copies.append(pltpu.make_async_remote_copy(
                src_ref=p_ref.at[2 * px + py, pl.ds(pc * rh, rh)], dst_ref=land_ref.at[k - 1],
                send_sem=send_sems.at[N_PEERS * a + k - 1], recv_sem=recv_sems.at[N_PEERS * a + k - 1],
                device_id=(px, py, pc), device_id_type=MESH))
    return copies


def _weight_copies(w_refs, land_refs, send_sems, recv_sems):
    x, y, c = _position()
    copies = []
    for a, (w_ref, land_ref) in enumerate(zip(w_refs, land_refs)):
        for j, (px, py) in enumerate(_other_chips(x, y)):
            copies.append(pltpu.make_async_remote_copy(
                src_ref=w_ref, dst_ref=land_ref.at[2 * x + y], send_sem=send_sems.at[3 * a + j],
                recv_sem=recv_sems.at[3 * a + j], device_id=(px, py, c), device_id_type=MESH))
    return copies


def _exchange_start(make_copies, srcs, lands, n_sems, name, after=None):
    n, m = len(srcs), len(lands)
    n_in = n + m + (after is not None)

    def body(*refs):
        send_sems, recv_sems, token = refs[n_in], refs[n_in + 1], refs[-1]
        for cp in make_copies(refs[:n], refs[n:n + m], send_sems, recv_sems):
            cp.start()
        token[...] = jnp.zeros(token.shape, token.dtype)

    hbm = lambda a: pltpu.with_memory_space_constraint(a, pltpu.HBM)
    dma = pltpu.SemaphoreType.DMA
    res = pl.pallas_call(
        body, name=name,
        in_specs=[HBM_SPEC] * (n + m) + ([] if after is None else [pl.BlockSpec(memory_space=pl.ANY)]),
        out_specs=[SEM_SPEC, SEM_SPEC] + [HBM_SPEC] * (n + m) + [pl.BlockSpec(memory_space=pltpu.VMEM)],
        out_shape=[dma((n_sems,)), dma((n_sems,))] + [pltpu.HBM(a.shape, a.dtype) for a in list(srcs) + list(lands)]
        + [jax.ShapeDtypeStruct((8, LANES), F32)],
        input_output_aliases={i: 2 + i for i in range(n + m)},
        compiler_params=pltpu.CompilerParams(has_side_effects=DATAFLOW_EFFECT),
    )(*[hbm(a) for a in srcs], *[hbm(a) for a in lands], *(() if after is None else (after,)))
    return res[0], res[1], list(res[2:2 + n]), list(res[2 + n:2 + n + m]), res[-1]


def _exchange_wait(make_copies, send_sems, recv_sems, srcs, lands, after, name):
    n, m = len(srcs), len(lands)

    def body(*refs):
        for cp in make_copies(refs[:n], refs[n:n + m], refs[n + m], refs[n + m + 1]):
            cp.wait_send()
            cp.wait_recv()

    res = pl.pallas_call(
        body, name=name,
        in_specs=[HBM_SPEC] * (n + m) + [SEM_SPEC, SEM_SPEC, pl.BlockSpec(memory_space=pl.ANY)],
        out_specs=[HBM_SPEC] * (n + m),
        out_shape=[pltpu.HBM(a.shape, a.dtype) for a in list(srcs) + list(lands)],
        input_output_aliases={i: i for i in range(n + m)},
        compiler_params=pltpu.CompilerParams(has_side_effects=DATAFLOW_EFFECT),
    )(*srcs, *lands, send_sems, recv_sems, after)
    return list(res[:n]), list(res[n:])


def _sum_partials(p, land, name):
    _, rh, cols = land.shape
    br = _pick_rows(rh, 256)
    nrb = rh // br
    x, y, c = _position()
    where = jnp.stack([2 * x + y, c]).astype(jnp.int32)

    def body(where_ref, p_ref, land_ref, o_ref):
        acc = p_ref[...].astype(F32)
        for k in range(N_PEERS):
            acc = acc + land_ref[k].astype(F32)
        o_ref[...] = acc

    return pl.pallas_call(
        body, name=name,
        grid_spec=pltpu.PrefetchScalarGridSpec(
            num_scalar_prefetch=1, grid=(nrb,),
            in_specs=[pl.BlockSpec((None, br, cols), lambda r, where_ref: (where_ref[0], where_ref[1] * nrb + r, 0)),
                      pl.BlockSpec((N_PEERS, br, cols), lambda r, where_ref: (0, r, 0))],
            out_specs=pl.BlockSpec((None, br, cols), lambda r, where_ref: (where_ref[1], r, 0))),
        out_shape=jax.ShapeDtypeStruct((2, rh, cols), F32),
        compiler_params=_params(("parallel",)),
    )(where, p, land)


def _sibling_share(fulls, name):
    n = len(fulls)

    def body(*refs):
        o_refs = refs[n:2 * n]
        send_sems, recv_sems = refs[2 * n:]
        x, y, c = _position()

        def copy(a, half):
            return pltpu.make_async_remote_copy(
                src_ref=o_refs[a].at[half], dst_ref=o_refs[a].at[half], send_sem=send_sems.at[a],
                recv_sem=recv_sems.at[a], device_id=(x, y, 1 - c), device_id_type=MESH)

        sends = [copy(a, c) for a in range(n)]
        for cp in sends:
            cp.start()
        for a in range(n):
            copy(a, 1 - c).wait_recv()
        for cp in sends:
            cp.wait_send()

    dma = pltpu.SemaphoreType.DMA
    return pl.pallas_call(
        body, name=name,
        in_specs=[HBM_SPEC] * n, out_specs=[HBM_SPEC] * n,
        out_shape=[jax.ShapeDtypeStruct(f.shape, f.dtype) for f in fulls],
        input_output_aliases={a: a for a in range(n)},
        scratch_shapes=[dma((n,)), dma((n,))],
    )(*fulls)


def _all_reduce_small(v):
    R, cols = v.shape

    def body(v_ref, o_ref, buf_ref, send_sems, recv_sems):
        x, y, c = _position()
        me = 4 * x + 2 * y + c
        buf_ref[me] = v_ref[...]
        sends = []
        for k in range(1, N_DEV):
            px = 1 - x if k & 4 else x
            py = 1 - y if k & 2 else y
            pc = 1 - c if k & 1 else c
            sends.append(pltpu.make_async_remote_copy(
                src_ref=v_ref, dst_ref=buf_ref.at[me], send_sem=send_sems.at[k - 1], recv_sem=recv_sems.at[k - 1],
                device_id=(px, py, pc), device_id_type=MESH))
        for cp in sends:
            cp.start()
        for k in range(1, N_DEV):
            px = 1 - x if k & 4 else x
            py = 1 - y if k & 2 else y
            pc = 1 - c if k & 1 else c
            pltpu.make_async_remote_copy(
                src_ref=v_ref, dst_ref=buf_ref.at[4 * px + 2 * py + pc], send_sem=send_sems.at[k - 1],
                recv_sem=recv_sems.at[k - 1], device_id=(px, py, pc), device_id_type=MESH).wait_recv()
        for cp in sends:
            cp.wait_send()
        acc = buf_ref[0]
        for d in range(1, N_DEV):
            acc = acc + buf_ref[d]
        o_ref[...] = acc

    return pl.pallas_call(
        body, name="small_grads_all_reduce",
        in_specs=[pl.BlockSpec(memory_space=pltpu.VMEM)], out_specs=pl.BlockSpec(memory_space=pltpu.VMEM),
        out_shape=jax.ShapeDtypeStruct((R, cols), F32),
        scratch_shapes=[pltpu.VMEM((N_DEV, R, cols), F32), pltpu.SemaphoreType.DMA((N_DEV - 1,)),
                        pltpu.SemaphoreType.DMA((N_DEV - 1,))],
    )(v)


def _rope_tables(S, half, width):
    inv_freq = ROPE_THETA ** (-jnp.arange(half, dtype=F32) / half)
    ang = jnp.arange(S).astype(F32)[:, None] * inv_freq[None, :]
    return jnp.cos(ang), jnp.sin(ang)


def _slot_rows(a):
    return a.reshape(N_SHARD, -1, a.shape[-1])


def _local_step(x, target, w, B, S, late, exchange, reduce_small):
    T = B * S
    D = D_MODEL
    bm = 256
    full = lambda a, wd, tile=None: (a, wd, 0, tile or wd)
    g = {}

    cos_r, sin_r = _rope_tables(S, RET_QK // 2, LANES)
    cos_m, sin_m = _rope_tables(S, MLA_ROPE // 2, LANES)
    zeros64 = jnp.zeros((S, 64), F32)
    cos_m = jnp.concatenate([cos_m, cos_m, zeros64], axis=1)
    sin_m = jnp.concatenate([-sin_m, sin_m, zeros64], axis=1)

    def ffn_fwd(xin, i):
        w.update(late(f"ffn{i}", xin))
        norm = w["ffn_norm"][i:i + 1]
        h, ht = _rowwise_fwd(_fn_rms, f"ffn{i}_norm", [full(xin, D)], [], [(norm, D)], [(D, D, BF16)], bm, S,
                             transposed=(0,))
        ag = _mm(h, w[f"ffn_w_in{i}"], "nn", F32, f"ffn{i}_in", bn=1408, cols_outer=True)
        u, ut = _conv_fwd(ag, w["ffn_conv8"][i], B, S, f"ffn{i}_conv")
        xout = _mm(u, w[f"ffn_w_out{i}"], "nn", F32, f"ffn{i}_out", residual=xin, bk=FFN_DIM)
        return xout, (xin, norm, ht, ag, ut)

    def ffn_bwd(dxout, dxout_c, saved, i):
        xin, norm, ht, ag, ut = saved
        du = _mm(dxout_c, w[f"ffn_w_out{i}"], "nt", F32, f"ffn{i}_out_dx", bn=1408, cols_outer=True)
        g_w_out = _mm(ut, dxout_c, "nn", BF16, f"ffn{i}_out_dw", bm=1408, bn=512, bk=T)
        da, dg, dw8 = _conv_bwd(ag, w["ffn_conv8"][i], du, B, S, f"ffn{i}_conv_bwd")
        g_w_in = _mm(ht, [da, dg], "nn", BF16, f"ffn{i}_in_dw", bm=1024, bn=1408, bk=T // 2, out_slots=N_SHARD)
        token = exchange(f"ffn{i}", [g_w_in, _slot_rows(g_w_out)])
        dh = _mm_dx([da, dg], w[f"ffn_w_in{i}"], f"ffn{i}_in_dx", after=token)
        (dxin, dxin_c), (g_norm,) = _rowwise_bwd(_fn_rms, f"ffn{i}_norm_bwd", [full(xin, D)], [], [(norm, D)],
                                                 [(dh, D)], bm, S, adds={0: dxout}, mxu_copies=(0,))
        return dxin, dxin_c, (g_norm, dw8)

    h0, h0t = _rowwise_fwd(_fn_rms, "ret_norm", [full(x, D)], [], [(w["ret_norm"], D)], [(D, D, BF16)], bm, S,
                           transposed=(0,))
    proj = _mm(h0, w["ret_w_in"], "nn", F32, "ret_in", after=w["started"], cols_outer=True)
    HQ, HV = RET_HEADS * RET_QK, RET_HEADS * RET_V
    rope_rows = [(proj, 2 * HQ + HV, 0, LANES)]
    q_r, k_r, v_r = _rowwise_fwd(_fn_ret_rope, "ret_rope", rope_rows, [cos_r, sin_r], [],
                                 [(HQ, LANES, BF16), (HQ, LANES, BF16), (HV, LANES, BF16)], bm, S)
    ret_o = _ret_attn_fwd(q_r, k_r, v_r, B, S)
    gate_rows = [full(ret_o, HV, RET_V), (proj, HV, 2, RET_V)]
    y0, y0t = _rowwise_fwd(_fn_ret_gate, "ret_gate", gate_rows, [], [(w["ret_gn"], RET_V)], [(HV, RET_V, BF16)], 128, S,
                           transposed=(0,))
    w.update(late("ret_out", y0))
    x1 = _mm(y0, w["ret_w_out"], "nn", F32, "ret_out", residual=x)
    x2, ffn0_saved = ffn_fwd(x1, 0)

    w.update(late("mla", x2))
    (h2,) = _rowwise_fwd(_fn_rms, "mla_norm", [full(x2, D)], [], [(w["mla_norm"], D)], [(D, D, BF16)], bm, S)
    proj2 = _mm(h2, w["mla_w_in"], "nn", F32, "mla_in")
    lat_consts = [(w["mla_q_norm"], LANES), (w["mla_kv_norm"], LANES)]
    cqn, ckvn, kr = _rowwise_fwd(_fn_mla_lat, "mla_latent_norm", [full(proj2, MLA_IN_PAD, LANES)], [], lat_consts,
                                 [(MLA_Q_RANK, LANES, BF16), (MLA_KV_RANK, LANES, BF16), (LANES, LANES, F32)], bm, S)
    qf = _mm(cqn, w["mla_w_qb"], "nn", F32, "mla_qb")
    kvf = _mm(ckvn, w["mla_w_kvb"], "nn", F32, "mla_kvb")
    HP, HVm = MLA_HEADS * MLA_PAD, MLA_HEADS * MLA_V
    head_rows = [full(qf, HP, LANES), full(kvf, HP, LANES), full(kr, LANES)]
    head_consts = [(w["mla_q_head_norm"], LANES), (w["mla_k_head_norm"], LANES)]
    q_a, k_a, v_a = _rowwise_fwd(_fn_mla_heads, "mla_heads", head_rows, [cos_m, sin_m], head_consts,
                                 [(HP, LANES, BF16), (HP, LANES, BF16), (HVm, LANES, BF16)], bm, S)
    att_o, lse = _mla_attn_fwd(q_a, k_a, v_a, B, S)
    x3 = _mm(att_o, w["mla_w_out"], "nn", F32, "mla_out", residual=x2)
    x4, ffn1_saved = ffn_fwd(x3, 1)

    dy, dy_c, loss = _loss_head(x4, target)

    dx3, dx3_c, (g_n1, dw8_1) = ffn_bwd(dy, dy_c, ffn1_saved, 1)

    d_att_o = _mm(dx3_c, w["mla_w_out"], "nt", F32, "mla_out_dx")
    g_mla_out = _mm(att_o, dx3_c, "tn", BF16, "mla_out_dw")
    dq_a, dk_a, dv_a = _mla_attn_bwd(q_a, k_a, v_a, att_o, d_att_o, lse, B, S)
    (dqf, dkvf, dkr), (g["mla_q_head_norm"], g["mla_k_head_norm"]) = _rowwise_bwd(
        _fn_mla_heads, "mla_heads_bwd", head_rows, [cos_m, sin_m], head_consts,
        [(dq_a, LANES), (dk_a, LANES), (dv_a, LANES)], 128, S, grad_dtypes=[BF16, BF16, F32])
    dcqn = _mm(dqf, w["mla_w_qb"], "nt", F32, "mla_qb_dx")
    g_qb = _mm(cqn, dqf, "tn", BF16, "mla_qb_dw")
    g_qb = _to_slots(_unpad_heads(g_qb, 1), 1).reshape(N_SHARD, MLA_Q_RANK, -1)
    dckvn = _mm(dkvf, w["mla_w_kvb"], "nt", F32, "mla_kvb_dx")
    g_kvb = _mm(ckvn, dkvf, "tn", BF16, "mla_kvb_dw", bn=512, out_slots=N_SHARD)
    (dproj2,), (g["mla_q_norm"], g["mla_kv_norm"]) = _rowwise_bwd(
        _fn_mla_lat, "mla_latent_norm_bwd", [full(proj2, MLA_IN_PAD, LANES)], [], lat_consts,
        [(dcqn, LANES), (dckvn, LANES), (dkr, LANES)], bm, S, grad_dtypes=[BF16])
    g_mla_in = _mm(h2, dproj2, "tn", BF16, "mla_in_dw")
    token = exchange("mla", [_slot_rows(g_mla_in[:, :MLA_IN]), g_qb, g_kvb, _slot_rows(g_mla_out)])
    dh2 = _mm(dproj2, w["mla_w_in"], "nt", F32, "mla_in_dx", after=token)
    (dx2, dx2_c), (g["mla_norm"],) = _rowwise_bwd(_fn_rms, "mla_norm_bwd", [full(x2, D)], [], [(w["mla_norm"], D)],
                                                  [(dh2, D)], bm, S, adds={0: dx3}, mxu_copies=(0,))

    dx1, dx1_c, (g_n0, dw8_0) = ffn_bwd(dx2, dx2_c, ffn0_saved, 0)

    dy0 = _mm(dx1_c, w["ret_w_out"], "nt", F32, "ret_out_dx")
    g_ret_out = _mm(y0t, dx1_c, "nn", BF16, "ret_out_dw", bm=1024, bn=512, bk=T)
    (d_ret_o, dgate), (g["ret_gn"],) = _rowwise_bwd(_fn_ret_gate, "ret_gate_bwd", gate_rows, [], [(w["ret_gn"], RET_V)],
                                                    [(dy0, RET_V)], 128, S, grad_dtypes=[F32, BF16])
    dq_r, dk_r, dv_r = _ret_attn_bwd(q_r, k_r, v_r, d_ret_o, B, S)
    (dqkv,), _ = _rowwise_bwd(_fn_ret_rope, "ret_rope_bwd", rope_rows, [cos_r, sin_r], [],
                              [(dq_r, LANES), (dk_r, LANES), (dv_r, LANES)], bm, S, grad_dtypes=[BF16], linear=True)
    dh0 = _mm_dx([dqkv, dgate], w["ret_w_in"], "ret_in_dx")
    (dx,), (g["ret_norm"],) = _rowwise_bwd(_fn_rms, "ret_norm_bwd", [full(x, D)], [], [(w["ret_norm"], D)],
                                           [(dh0, D)], bm, S, adds={0: dx1})
    g["ffn_norm"] = jnp.concatenate([g_n0, g_n1], axis=0)
    g["ffn_conv_w"] = jnp.stack([dw8_0[0:3], dw8_1[0:3]])
    g["ffn_conv_b"] = jnp.stack([dw8_0[3], dw8_1[3]])
    reduced_small = reduce_small(g)
    g_ret_in = _mm(h0t, [dqkv, dgate], "nn", BF16, "ret_in_dw", bn=512, bk=T, out_slots=N_SHARD, after=reduced_small)
    exchange("ret", [g_ret_in, _slot_rows(g_ret_out)])
    return loss, dx, reduced_small


_BIG = [("ret_w_in", 2), ("ret_w_out", 1), ("mla_w_in", 1), ("mla_w_qb", 2), ("mla_w_kvb", 2), ("mla_w_out", 1),
        ("ffn_w_in", 2), ("ffn_w_out", 1)]
_SMALL_SHARDED = [("ret_gn", 2), ("mla_norm", 1), ("mla_q_norm", 1), ("mla_kv_norm", 1), ("ffn_conv_w", 2)]
_SMALL_REPLICATED = ["ret_norm", "mla_q_head_norm", "mla_k_head_norm", "ffn_norm", "ffn_conv_b"]
_SMALL_ALL = ["ret_norm", "ret_gn", "mla_norm", "mla_q_norm", "mla_kv_norm", "mla_q_head_norm", "mla_k_head_norm",
              "ffn_norm", "ffn_conv_w", "ffn_conv_b"]


def _to_slots(full, axis):
    shape = full.shape
    split = shape[:axis] + (N_SHARD, shape[axis] // N_SHARD) + shape[axis + 1:]
    return jnp.moveaxis(full.reshape(split), axis, 0).reshape(N_SHARD, -1)


def _from_slots(slots, shard_shape, axis):
    parts = jnp.moveaxis(slots.reshape((N_SHARD,) + tuple(shard_shape)), 0, axis)
    full = shard_shape[:axis] + (N_SHARD * shard_shape[axis],) + shard_shape[axis + 1:]
    return parts.reshape(full)


def _pad_rows(flat, cols, row_unit):
    n, L = flat.shape
    unit = cols * row_unit
    Lp = -(-L // unit) * unit
    if Lp != L:
        flat = jnp.concatenate([flat, jnp.zeros((n, Lp - L), flat.dtype)], axis=1)
    return flat.reshape(n, Lp // cols, cols)


def _pad_heads(a, axis):
    shape = a.shape
    a = a.reshape(shape[:axis] + (MLA_HEADS, MLA_QK) + shape[axis + 1:])
    pad = [(0, 0)] * a.ndim
    pad[axis + 1] = (0, MLA_PAD - MLA_QK)
    return jnp.pad(a, pad).reshape(shape[:axis] + (MLA_HEADS * MLA_PAD,) + shape[axis + 1:])


def _unpad_heads(a, axis):
    shape = a.shape
    a = a.reshape(shape[:axis] + (MLA_HEADS, MLA_PAD) + shape[axis + 1:])
    a = lax.slice_in_dim(a, 0, MLA_QK, axis=axis + 1)
    return a.reshape(shape[:axis] + (MLA_HEADS * MLA_QK,) + shape[axis + 1:])


def kernel(x, ret_norm, ret_w_in, ret_gn, ret_w_out, mla_norm, mla_w_in, mla_q_norm, mla_w_qb, mla_kv_norm, mla_w_kvb, mla_q_head_norm, mla_k_head_norm, mla_w_out, ffn_norm, ffn_w_in, ffn_conv_w, ffn_conv_b, ffn_w_out, loss_target, m_ret_norm, m_ret_w_in, m_ret_gn, m_ret_w_out, m_mla_norm, m_mla_w_in, m_mla_q_norm, m_mla_w_qb, m_mla_kv_norm, m_mla_w_kvb, m_mla_q_head_norm, m_mla_k_head_norm, m_mla_w_out, m_ffn_norm, m_ffn_w_in, m_ffn_conv_w, m_ffn_conv_b, m_ffn_w_out, v_ret_norm, v_ret_w_in, v_ret_gn, v_ret_w_out, v_mla_norm, v_mla_w_in, v_mla_q_norm, v_mla_w_qb, v_mla_kv_norm, v_mla_w_kvb, v_mla_q_head_norm, v_mla_k_head_norm, v_mla_w_out, v_ffn_norm, v_ffn_w_in, v_ffn_conv_w, v_ffn_conv_b, v_ffn_w_out):
    names = ["ret_norm", "ret_w_in", "ret_gn", "ret_w_out", "mla_norm", "mla_w_in", "mla_q_norm", "mla_w_qb",
             "mla_kv_norm", "mla_w_kvb", "mla_q_head_norm", "mla_k_head_norm", "mla_w_out", "ffn_norm", "ffn_w_in",
             "ffn_conv_w", "ffn_conv_b", "ffn_w_out"]
    shard = dict(zip(names, [ret_norm, ret_w_in, ret_gn, ret_w_out, mla_norm, mla_w_in, mla_q_norm, mla_w_qb,
                             mla_kv_norm, mla_w_kvb, mla_q_head_norm, mla_k_head_norm, mla_w_out, ffn_norm, ffn_w_in,
                             ffn_conv_w, ffn_conv_b, ffn_w_out]))
    mom_m = dict(zip(names, [m_ret_norm, m_ret_w_in, m_ret_gn, m_ret_w_out, m_mla_norm, m_mla_w_in, m_mla_q_norm,
                             m_mla_w_qb, m_mla_kv_norm, m_mla_w_kvb, m_mla_q_head_norm, m_mla_k_head_norm, m_mla_w_out,
                             m_ffn_norm, m_ffn_w_in, m_ffn_conv_w, m_ffn_conv_b, m_ffn_w_out]))
    mom_v = dict(zip(names, [v_ret_norm, v_ret_w_in, v_ret_gn, v_ret_w_out, v_mla_norm, v_mla_w_in, v_mla_q_norm,
                             v_mla_w_qb, v_mla_kv_norm, v_mla_w_kvb, v_mla_q_head_norm, v_mla_k_head_norm, v_mla_w_out,
                             v_ffn_norm, v_ffn_w_in, v_ffn_conv_w, v_ffn_conv_b, v_ffn_w_out]))
    B, S, D = x.shape
    T = B * S
    sx, sy = lax.axis_index("x"), lax.axis_index("y")
    me = 2 * sx + sy

    two_d = lambda a: a.reshape(-1, a.shape[-1])
    small_sizes = [int(np.prod(shard[n].shape)) for n, _ in _SMALL_SHARDED]
    small = jnp.concatenate([shard[n].reshape(1, -1) for n, _ in _SMALL_SHARDED], axis=1)
    small = _pad_rows(small, LANES, 8)[0]
    as_mxu = lambda a: two_d(a).astype(BF16)
    is_me = lax.broadcasted_iota(jnp.int32, (N_SHARD, 1, 1), 0) == me
    with_own = lambda gathered, own: jnp.where(is_me, own[None], gathered)
    by_cols = lambda a: jnp.moveaxis(a, 0, 1).reshape(a.shape[1], -1)
    by_rows = lambda a: a.reshape(-1, a.shape[-1])
    pad_in = lambda a: jnp.pad(by_rows(a), ((0, 0), (0, MLA_IN_PAD - MLA_IN)))
    pad_qb = lambda a: _pad_heads(by_cols(a), 1)
    ret_in_shard = as_mxu(shard["ret_w_in"])
    g_ret_in, gsmall = _all_gather_weights([ret_in_shard], small)
    later = [
        ("ret_out", [("ret_w_out", as_mxu(shard["ret_w_out"]), by_rows)]),
        ("ffn0", [("ffn_w_in0", as_mxu(shard["ffn_w_in"][0]), by_cols), ("ffn_w_out0", as_mxu(shard["ffn_w_out"][0]), by_rows)]),
        ("mla", [("mla_w_in", as_mxu(shard["mla_w_in"]), pad_in), ("mla_w_qb", as_mxu(shard["mla_w_qb"]), pad_qb),
                 ("mla_w_kvb", as_mxu(shard["mla_w_kvb"]), by_cols), ("mla_w_out", as_mxu(shard["mla_w_out"]), by_rows)]),
        ("ffn1", [("ffn_w_in1", as_mxu(shard["ffn_w_in"][1]), by_cols), ("ffn_w_out1", as_mxu(shard["ffn_w_out"][1]), by_rows)]),
    ]
    gathering = {}
    token = gsmall
    for group, items in later:
        shards = [s_ for _, s_, _ in items]
        lands = [lax.empty((N_SHARD,) + s_.shape, s_.dtype) for s_ in shards]
        send_sems, recv_sems, shards, lands, token = _exchange_start(
            _weight_copies, shards, lands, 3 * len(shards), f"weights_start_{group}", after=token)
        gathering[group] = (send_sems, recv_sems, shards, lands, items)

    def late(group, after):
        send_sems, recv_sems, shards, lands, items = gathering[group]
        shards, lands = _exchange_wait(_weight_copies, send_sems, recv_sems, shards, lands, after,
                                       f"weights_wait_{group}")
        return {key: full(with_own(l_, s_)) for (key, _, full), s_, l_ in zip(items, shards, lands)}

    gsmall = with_own(gsmall, small).reshape(N_SHARD, -1)
    wfull = {}
    off = 0
    for (n, ax), sz in zip(_SMALL_SHARDED, small_sizes):
        wfull[n] = _from_slots(gsmall[:, off:off + sz], shard[n].shape, ax)
        off += sz
    for n in _SMALL_REPLICATED:
        wfull[n] = shard[n]

    conv8 = jnp.concatenate([wfull["ffn_conv_w"], wfull["ffn_conv_b"][:, None, :],
                             jnp.zeros((2, 4, FFN_DIM), F32)], axis=1)
    w = {
        "started": token, "ret_norm": wfull["ret_norm"], "ret_w_in": by_cols(with_own(g_ret_in, ret_in_shard)),
        "ret_gn": wfull["ret_gn"].reshape(1, RET_HEADS * RET_V), "mla_norm": wfull["mla_norm"],
        "mla_q_norm": wfull["mla_q_norm"], "mla_kv_norm": wfull["mla_kv_norm"],
        "mla_q_head_norm": jnp.pad(wfull["mla_q_head_norm"], ((0, 0), (0, MLA_PAD - MLA_QK))),
        "mla_k_head_norm": jnp.pad(wfull["mla_k_head_norm"], ((0, 0), (0, MLA_PAD - MLA_QK))),
        "ffn_norm": wfull["ffn_norm"], "ffn_conv8": conv8,
    }

    started = {}

    def exchange(group, arrays):
        lands = [lax.empty((N_PEERS, p.shape[1] // 2, p.shape[2]), p.dtype) for p in arrays]
        send_sems, recv_sems, ps, lands, token = _exchange_start(
            _grad_copies, arrays, lands, N_PEERS * len(arrays), f"grads_start_{group}")
        started[group] = (send_sems, recv_sems, ps, lands)
        return token

    small_shapes = {
        "ret_norm": (1, D_MODEL), "ret_gn": (1, RET_HEADS, RET_V), "mla_norm": (1, D_MODEL),
        "mla_q_norm": (1, MLA_Q_RANK), "mla_kv_norm": (1, MLA_KV_RANK), "mla_q_head_norm": (1, MLA_QK),
        "mla_k_head_norm": (1, MLA_QK), "ffn_norm": (2, D_MODEL), "ffn_conv_w": (2, 3, FFN_DIM),
        "ffn_conv_b": (2, FFN_DIM)}

    def reduce_small(gl):
        gl = dict(gl, mla_q_head_norm=gl["mla_q_head_norm"][:, :MLA_QK], mla_k_head_norm=gl["mla_k_head_norm"][:, :MLA_QK])
        packed = jnp.concatenate([gl[n].reshape(1, -1) for n in _SMALL_ALL], axis=1)
        return _all_reduce_small(_pad_rows(packed, LANES, 8)[0])

    loss_part, dx, gsm = _local_step(x.reshape(T, D), loss_target.reshape(T, D), w, B, S, late, exchange,
                                     reduce_small)
    loss = lax.psum(loss_part, ("x", "y", "c"))

    delta, new_m, new_v, grads = {}, {}, {}, {}

    def reduced(group, after):
        send_sems, recv_sems, ps, lands = started[group]
        ps, lands = _exchange_wait(_grad_copies, send_sems, recv_sems, ps, lands, after, f"grads_wait_{group}")
        halves = [_sum_partials(p_, l_, f"grads_sum_{group}_{i}") for i, (p_, l_) in enumerate(zip(ps, lands))]
        return [two_d(r) for r in _sibling_share(halves, f"grads_share_{group}")]

    def adamw(n, g_):
        shp = shard[n].shape
        grads[n] = g_.reshape(shp)
        flat = lambda a: a.reshape(-1, shp[-1])
        d_, m_, v_ = _adamw(flat(shard[n]), flat(grads[n]), flat(mom_m[n]), flat(mom_v[n]), f"adamw_{n}")
        delta[n], new_m[n], new_v[n] = d_.reshape(shp), m_.reshape(shp), v_.reshape(shp)
        return d_

    ffn1 = reduced("ffn1", started["ret"][2][0])
    mla = reduced("mla", ffn1[0])
    ffn0 = reduced("ffn0", mla[0])
    early = [adamw(n, g_) for n, g_ in zip(["mla_w_in", "mla_w_qb", "mla_w_kvb", "mla_w_out"], mla)]
    early.append(adamw("ffn_w_in", jnp.stack([ffn0[0], ffn1[0]])))
    early.append(adamw("ffn_w_out", jnp.stack([ffn0[1], ffn1[1]])))
    ret = reduced("ret", jnp.stack([d_[0, 0] for d_ in early]))
    adamw("ret_w_in", ret[0])
    adamw("ret_w_out", ret[1])

    gsm = gsm.reshape(-1)
    sharded_axis = dict(_SMALL_SHARDED)
    off = 0
    for n in _SMALL_ALL:
        sz = int(np.prod(small_shapes[n]))
        gn = gsm[off:off + sz].reshape(small_shapes[n])
        off += sz
        if n in sharded_axis:
            ax = sharded_axis[n]
            width = shard[n].shape[ax]
            gn = lax.dynamic_slice_in_dim(gn, me * width, width, axis=ax)
        grads[n] = gn

    pack_small = lambda d: _pad_rows(jnp.concatenate([d[n].reshape(1, -1) for n in _SMALL_ALL], axis=1), LANES, 8)[0]
    d_, m_, v_ = _adamw(pack_small(shard), pack_small(grads), pack_small(mom_m), pack_small(mom_v), "adamw_small")
    off = 0
    for n in _SMALL_ALL:
        sz = int(np.prod(shard[n].shape))
        for dst, src in ((delta, d_), (new_m, m_), (new_v, v_)):
            dst[n] = src.reshape(-1)[off:off + sz].reshape(shard[n].shape)
        off += sz

    return (loss, dx.reshape(B, S, D), *[grads[n] for n in names], *[delta[n] for n in names],
            *[new_m[n] for n in names], *[new_v[n] for n in names])
```

```python
import functools
import math

import numpy as np
import jax
import jax.numpy as jnp
from jax import lax
from jax.experimental import pallas as pl
from jax.experimental.pallas import tpu as pltpu

F32 = jnp.float32
BF16 = jnp.bfloat16
MXU_DTYPE = jnp.bfloat16

CHUNK = 64
RMS_EPS = 1e-6
ROPE_THETA = 10000.0
D_MODEL = 1024
RET_HEADS = 4
RET_QK = 256
RET_V = 512
RET_GAMMA_BASE = -5.0
MLA_HEADS = 8
MLA_Q_RANK = 384
MLA_KV_RANK = 256
MLA_NOPE = 128
MLA_ROPE = 64
MLA_V = 128
MLA_QK = MLA_NOPE + MLA_ROPE
MLA_PAD = 256
MLA_IN = MLA_Q_RANK + MLA_KV_RANK + MLA_ROPE
MLA_IN_PAD = MLA_IN + 64
MASK_VALUE = -1e30
FFN_DIM = 2816
ADAM_LR = 0.001
ADAM_B1 = 0.9
ADAM_B2 = 0.999
ADAM_EPS = 1e-08
ADAM_WD = 0.01
ADAM_STEP = 10

LANES = 128
ATT_BLOCK = 256
MLA_FWD_BLOCK = 512
VMEM_LIMIT = 56 * 2 ** 20
N_SHARD = 4
N_DEV = 8

MESH = pl.DeviceIdType.MESH


def _params(sem=None, **kw):
    return pltpu.CompilerParams(dimension_semantics=sem, vmem_limit_bytes=VMEM_LIMIT, **kw)


def _pick(dim, target):
    if dim <= target:
        return dim
    best = None
    for d in range(LANES, target + 1, LANES):
        if dim % d == 0:
            best = d
    assert best is not None, (dim, target)
    return best


def _mm(a, b, dims, out_dtype, name, residual=None, bm=512, bn=1024, bk=2048, out_slots=None, after=None,
        cols_outer=False):
    a_parts = list(a) if isinstance(a, (list, tuple)) else [a]
    b_parts = list(b) if isinstance(b, (list, tuple)) else [b]
    parts_on_n = dims == "tn" or len(b_parts) > 1
    if parts_on_n:
        assert len(a_parts) == 1 and dims in ("tn", "nn")
        (K, M) = a_parts[0].shape if dims == "tn" else a_parts[0].shape[::-1]
        N = sum(p.shape[1] for p in b_parts)
        part_widths = [p.shape[1] for p in b_parts]
    else:
        assert len(b_parts) == 1
        M = a_parts[0].shape[0]
        K = sum(p.shape[1] for p in a_parts)
        N = b_parts[0].shape[1 if dims == "nn" else 0]
        part_widths = [p.shape[1] for p in a_parts]
    bm, bn, bk = _pick(M, bm), _pick(N, bn), _pick(K, min(bk, 1024) if dims == "tn" else bk)
    nk = K // bk
    unit = bn if parts_on_n else bk
    assert all(wd % unit == 0 for wd in part_widths), (name, part_widths, unit)
    bounds = np.cumsum([0] + [wd // unit for wd in part_widths])
    ranges = [(int(lo), int(hi)) for lo, hi in zip(bounds[:-1], bounds[1:])]

    def part_index(idx, lo, hi):
        return jnp.clip(idx - lo, 0, hi - lo - 1)

    if parts_on_n:
        if dims == "tn":
            a_specs = [pl.BlockSpec((bk, bm), lambda i, j, k: (k, i))]
            dn = (((0,), (0,)), ((), ()))
        else:
            a_specs = [pl.BlockSpec((bm, bk), lambda i, j, k: (i, k))]
            dn = (((1,), (0,)), ((), ()))
        b_specs = [pl.BlockSpec((bk, bn), functools.partial(lambda i, j, k, lo, hi: (k, part_index(j, lo, hi)), lo=lo, hi=hi))
                   for lo, hi in ranges]
    else:
        a_specs = [pl.BlockSpec((bm, bk), functools.partial(lambda i, j, k, lo, hi: (i, part_index(k, lo, hi)), lo=lo, hi=hi))
                   for lo, hi in ranges]
        if dims == "nt":
            b_specs = [pl.BlockSpec((bn, bk), lambda i, j, k: (j, k))]
        else:
            b_specs = [pl.BlockSpec((bk, bn), lambda i, j, k: (k, j))]
        dn = (((1,), (1 if dims == "nt" else 0,)), ((), ()))
    r_spec = pl.BlockSpec((bm, bn), lambda i, j, k: (i, j))
    if out_slots is None:
        o_spec, o_shape = r_spec, (M, N)
    else:
        ns = N // out_slots
        assert ns % bn == 0, (name, ns, bn)
        nbs = ns // bn
        o_spec = pl.BlockSpec((None, bm, bn), lambda i, j, k: (j // nbs, i, j % nbs))
        o_shape = (out_slots, M, ns)
    has_res = residual is not None
    na, nb = len(a_parts), len(b_parts)

    def body(*refs):
        a_refs, b_refs = refs[:na], refs[na:na + nb]
        r_ref = refs[na + nb] if has_res else None
        n_in = na + nb + has_res + (after is not None)
        o_ref = refs[n_in]
        acc_ref = refs[n_in + 1] if nk > 1 else None
        k = pl.program_id(2)

        def finish(acc):
            if has_res:
                acc = acc + r_ref[...].astype(F32)
            o_ref[...] = acc.astype(out_dtype)

        def compute(a_ref, b_ref):
            p = lax.dot_general(a_ref[...].astype(MXU_DTYPE), b_ref[...].astype(MXU_DTYPE), dn,
                                preferred_element_type=F32)
            if nk == 1:
                finish(p)
                return

            @pl.when(k == 0)
            def _():
                acc_ref[...] = p

            @pl.when(jnp.logical_and(k > 0, k < nk - 1))
            def _():
                acc_ref[...] += p

            @pl.when(k == nk - 1)
            def _():
                finish(acc_ref[...] + p)

        if len(ranges) == 1:
            compute(a_refs[0], b_refs[0])
        else:
            idx = pl.program_id(0 if cols_outer else 1) if parts_on_n else k
            for p, (lo, hi) in enumerate(ranges):
                @pl.when(jnp.logical_and(idx >= lo, idx < hi))
                def _(p=p):
                    compute(a_refs[0 if parts_on_n else p], b_refs[p if parts_on_n else 0])

    after_specs = [] if after is None else [pl.BlockSpec(after.shape, lambda i, j, k: (0, 0))]
    in_specs = a_specs + b_specs + ([r_spec] if has_res else []) + after_specs
    grid = (M // bm, N // bn, nk)
    if cols_outer:
        swap = lambda sp: pl.BlockSpec(sp.block_shape, functools.partial(lambda j, i, k, f: f(i, j, k), f=sp.index_map))
        in_specs, o_spec, grid = [swap(sp) for sp in in_specs], swap(o_spec), (grid[1], grid[0], nk)
    return pl.pallas_call(
        body, name=name, grid=grid,
        in_specs=in_specs, out_specs=o_spec,
        out_shape=jax.ShapeDtypeStruct(o_shape, out_dtype),
        scratch_shapes=[pltpu.VMEM((bm, bn), F32)] if nk > 1 else [],
        compiler_params=_params(("parallel", "parallel", "arbitrary")),
    )(*a_parts, *b_parts, *((residual,) if has_res else ()), *(() if after is None else (after,)))


def _mm_dx(a_parts, w, name, bm=512, after=None):
    M = a_parts[0].shape[0]
    N, K = w.shape
    widths = [p.shape[1] for p in a_parts]
    assert sum(widths) == K, (name, widths, K)
    offs = [int(o) for o in np.cumsum([0] + widths[:-1])]
    bm = _pick(M, bm)
    na = len(a_parts)

    def body(*refs):
        w_ref = refs[na]
        o_ref = refs[na + 1 + (after is not None)]
        acc = None
        for a_ref, off, wd in zip(refs[:na], offs, widths):
            p = lax.dot_general(a_ref[...].astype(MXU_DTYPE), w_ref[:, off:off + wd].astype(MXU_DTYPE), _NT,
                                preferred_element_type=F32)
            acc = p if acc is None else acc + p
        o_ref[...] = acc

    in_specs = [pl.BlockSpec((bm, wd), lambda i: (i, 0)) for wd in widths] + [pl.BlockSpec((N, K), lambda i: (0, 0))]
    in_specs += [] if after is None else [pl.BlockSpec(after.shape, lambda i: (0, 0))]
    return pl.pallas_call(
        body, name=name, grid=(M // bm,),
        in_specs=in_specs, out_specs=pl.BlockSpec((bm, N), lambda i: (i, 0)),
        out_shape=jax.ShapeDtypeStruct((M, N), F32),
        compiler_params=_params(("parallel",)),
    )(*a_parts, w, *(() if after is None else (after,)))


def _tiles(ref, width, tile):
    return [ref[:, t * tile:(t + 1) * tile].astype(F32) for t in range(width // tile)]


def _row_specs(rows, pos, consts, bm, S):
    npos_blocks = S // bm
    specs = [pl.BlockSpec((bm, w), functools.partial(lambda i, c: (i, c), c=cb)) for (_, w, cb, _) in rows]
    specs += [pl.BlockSpec((bm, p.shape[1]), lambda i: (i % npos_blocks, 0)) for p in pos]
    specs += [pl.BlockSpec(c.shape, lambda i: (0, 0)) for (c, _) in consts]
    return specs


def _rowwise_fwd(fn, name, rows, pos, consts, outs, bm, S, transposed=()):
    T = rows[0][0].shape[0]
    nr, npos, nc, no = len(rows), len(pos), len(consts), len(outs)

    def body(*refs):
        row_v = [_tiles(r, w, t) for r, (_, w, _, t) in zip(refs[:nr], rows)]
        pos_v = [r[...] for r in refs[nr:nr + npos]]
        const_v = [_tiles(r, c.shape[1], t) for r, (c, t) in zip(refs[nr + npos:nr + npos + nc], consts)]
        res = fn(row_v, pos_v, const_v)
        out_refs = refs[nr + npos + nc:]
        for o_ref, tiles, (w, t, dt) in zip(out_refs, res, outs):
            for k, v in enumerate(tiles):
                o_ref[:, k * t:(k + 1) * t] = v.astype(dt)
        for t_ref, a in zip(out_refs[no:], transposed):
            t = outs[a][1]
            for k, v in enumerate(res[a]):
                t_ref[k * t:(k + 1) * t, :] = v.T.astype(t_ref.dtype)

    return pl.pallas_call(
        body, name=name, grid=(T // bm,),
        in_specs=_row_specs(rows, pos, consts, bm, S),
        out_specs=[pl.BlockSpec((bm, w), lambda i: (i, 0)) for (w, _, _) in outs]
        + [pl.BlockSpec((outs[a][0], bm), lambda i: (0, i)) for a in transposed],
        out_shape=[jax.ShapeDtypeStruct((T, w), dt) for (w, _, dt) in outs]
        + [jax.ShapeDtypeStruct((outs[a][0], T), BF16) for a in transposed],
        compiler_params=_params(("parallel",)),
    )(*[r[0] for r in rows], *pos, *[c[0] for c in consts])


def _rowwise_bwd(fn, name, rows, pos, consts, cts, bm, S, adds=None, grad_dtypes=None, mxu_copies=(), linear=False):
    adds = adds or {}
    T = rows[0][0].shape[0]
    nr, npos, nc, nct = len(rows), len(pos), len(consts), len(cts)
    add_idx = sorted(adds)
    grad_dtypes = grad_dtypes or [F32] * nr

    def body(*refs):
        it = iter(refs)
        row_refs = [None if linear else next(it) for _ in range(nr)]
        pos_refs = [next(it) for _ in range(npos)]
        const_refs = [next(it) for _ in range(nc)]
        ct_refs = [next(it) for _ in range(nct)]
        add_refs = {k: next(it) for k in add_idx}
        drow_refs = [next(it) for _ in range(nr)]
        copy_refs = {a: next(it) for a in mxu_copies}
        dconst_refs = [next(it) for _ in range(nc)]
        if linear:
            row_v = [[jnp.zeros((bm, t), F32)] * (w // t) for (_, w, _, t) in rows]
        else:
            row_v = [_tiles(r, w, t) for r, (_, w, _, t) in zip(row_refs, rows)]
        pos_v = [r[...] for r in pos_refs]
        const_v = [_tiles(r, c.shape[1], t) for r, (c, t) in zip(const_refs, consts)]
        ct_v = [_tiles(r, c.shape[1], t) for r, (c, t) in zip(ct_refs, cts)]
        _, vjp = jax.vjp(lambda rv, cv: fn(rv, pos_v, cv), row_v, const_v)
        drows, dconsts = vjp(ct_v)
        for a, (d_ref, tiles, (_, w, _, t)) in enumerate(zip(drow_refs, drows, rows)):
            for k, v in enumerate(tiles):
                if a in add_refs:
                    v = v + add_refs[a][:, k * t:(k + 1) * t].astype(F32)
                d_ref[:, k * t:(k + 1) * t] = v.astype(d_ref.dtype)
                if a in copy_refs:
                    copy_refs[a][:, k * t:(k + 1) * t] = v.astype(BF16)
        first = pl.program_id(0) == 0
        for d_ref, tiles, (_, t) in zip(dconst_refs, dconsts, consts):
            for k, v in enumerate(tiles):
                @pl.when(first)
                def _(d_ref=d_ref, k=k, t=t, v=v):
                    d_ref[:, k * t:(k + 1) * t] = v

                @pl.when(jnp.logical_not(first))
                def _(d_ref=d_ref, k=k, t=t, v=v):
                    d_ref[:, k * t:(k + 1) * t] += v

    in_specs = _row_specs([] if linear else rows, pos, consts, bm, S)
    in_specs += [pl.BlockSpec((bm, c.shape[1]), lambda i: (i, 0)) for (c, _) in cts]
    in_specs += [pl.BlockSpec((bm, adds[k].shape[1]), lambda i: (i, 0)) for k in add_idx]
    out_specs = [pl.BlockSpec((bm, w), lambda i: (i, 0)) for (_, w, _, _) in rows]
    out_specs += [pl.BlockSpec((bm, rows[a][1]), lambda i: (i, 0)) for a in mxu_copies]
    out_specs += [pl.BlockSpec(c.shape, lambda i: (0, 0)) for (c, _) in consts]
    out_shape = [jax.ShapeDtypeStruct((T, w), dt) for (_, w, _, _), dt in zip(rows, grad_dtypes)]
    out_shape += [jax.ShapeDtypeStruct((T, rows[a][1]), BF16) for a in mxu_copies]
    out_shape += [jax.ShapeDtypeStruct(c.shape, F32) for (c, _) in consts]
    res = pl.pallas_call(
        body, name=name, grid=(T // bm,),
        in_specs=in_specs, out_specs=out_specs, out_shape=out_shape,
        compiler_params=_params(("arbitrary",)),
    )(*([] if linear else [r[0] for r in rows]), *pos, *[c[0] for c in consts], *[c[0] for c in cts],
      *[adds[k] for k in add_idx])
    n_rows = nr + len(mxu_copies)
    return res[:n_rows], res[n_rows:]


def _ssq(tiles):
    s = jnp.sum(tiles[0] * tiles[0], axis=-1, keepdims=True)
    for t in tiles[1:]:
        s = s + jnp.sum(t * t, axis=-1, keepdims=True)
    return s


def _sigmoid(x):
    return 1.0 / (1.0 + jnp.exp(-x))


def _fn_rms(rows, pos, consts):
    (x,), (g,) = rows[0], consts[0]
    r = lax.rsqrt(jnp.mean(x * x, axis=-1, keepdims=True) + RMS_EPS)
    return [[x * r * g]]


def _fn_ret_rope(rows, pos, consts):
    (qkv,) = rows
    nq = RET_HEADS * RET_QK // LANES
    q, k, v = qkv[:nq], qkv[nq:2 * nq], qkv[2 * nq:]
    cos, sin = pos

    def rot(t, scale):
        out = []
        for h in range(RET_HEADS):
            x1, x2 = t[2 * h], t[2 * h + 1]
            o1, o2 = x1 * cos - x2 * sin, x2 * cos + x1 * sin
            out += [o1, o2] if scale is None else [o1 * scale, o2 * scale]
        return out

    return [rot(q, None), rot(k, RET_QK ** -0.5), list(v)]


def _fn_ret_gate(rows, pos, consts):
    o, g = rows
    (gn,) = consts
    out = []
    for h in range(RET_HEADS):
        r = lax.rsqrt(jnp.mean(o[h] * o[h], axis=-1, keepdims=True) + RMS_EPS)
        out.append((o[h] * r * gn[h]) * (g[h] * _sigmoid(g[h])))
    return [out]


def _fn_mla_lat(rows, pos, consts):
    (p,) = rows
    gq, gkv = consts
    nq, nkv = MLA_Q_RANK // LANES, MLA_KV_RANK // LANES
    cq, ckv, kr = p[:nq], p[nq:nq + nkv], p[nq + nkv]
    rq = lax.rsqrt(_ssq(cq) / MLA_Q_RANK + RMS_EPS)
    rkv = lax.rsqrt(_ssq(ckv) / MLA_KV_RANK + RMS_EPS)
    return [[t * rq * g for t, g in zip(cq, gq)], [t * rkv * g for t, g in zip(ckv, gkv)], [kr]]


def _swap32_impl(x):
    lane = lax.broadcasted_iota(jnp.int32, x.shape, 1)
    up, down = pltpu.roll(x, LANES - 32, 1), pltpu.roll(x, 32, 1)
    return jnp.where(lane < 32, up, jnp.where(lane < 64, down, 0.0))


@jax.custom_vjp
def _swap32(x):
    return _swap32_impl(x)


_swap32.defvjp(lambda x: (_swap32_impl(x), None), lambda _, g: (_swap32_impl(g),))


def _fn_mla_heads(rows, pos, consts):
    qf, kvf, (kr,) = rows
    cos, sin = pos
    gq, gk = consts
    q_out, k_out, v_out = [], [], []
    for h in range(MLA_HEADS):
        q0, q1 = qf[2 * h], qf[2 * h + 1]
        r = lax.rsqrt(_ssq([q0, q1]) / MLA_QK + RMS_EPS)
        a0, a1 = q0 * r * gq[0], q1 * r * gq[1]
        a1 = a1 * cos + _swap32(a1) * sin
        q_out += [a0 * (MLA_QK ** -0.5), a1 * (MLA_QK ** -0.5)]
        k0 = kvf[2 * h]
        r = lax.rsqrt(_ssq([k0, kr]) / MLA_QK + RMS_EPS)
        b0, b1 = k0 * r * gk[0], kr * r * gk[1]
        k_out += [b0, b1 * cos + _swap32(b1) * sin]
        v_out.append(kvf[2 * h + 1])
    return [q_out, k_out, v_out]


def _shift_down(x, n):
    row = lax.broadcasted_iota(jnp.int32, x.shape, 0)
    return jnp.where(row >= n, pltpu.roll(x, n, 0), 0.0)


def _shift_up(x, n):
    rows = x.shape[0]
    row = lax.broadcasted_iota(jnp.int32, x.shape, 0)
    return jnp.where(row < rows - n, pltpu.roll(x, rows - n, 0), 0.0)


def _conv_blocks(S):
    cb = 256
    return cb, FFN_DIM // cb


def _conv_fwd(ag, w8, B, S, name):
    cb, ncb = _conv_blocks(S)

    def body(a_ref, g_ref, w_ref, u_ref, ut_ref):
        g = g_ref[...].astype(F32)
        w = w_ref[...]
        gc = w[0:1] * _shift_down(g, 2) + w[1:2] * _shift_down(g, 1) + w[2:3] * g + w[3:4]
        u = a_ref[...].astype(F32) * (gc * _sigmoid(gc))
        u_ref[...] = u.astype(u_ref.dtype)
        ut_ref[...] = u.T.astype(ut_ref.dtype)

    return pl.pallas_call(
        body, name=name, grid=(ncb, B),
        in_specs=[pl.BlockSpec((S, cb), lambda j, b: (b, j)),
                  pl.BlockSpec((S, cb), lambda j, b: (b, ncb + j)),
                  pl.BlockSpec((8, cb), lambda j, b: (0, j))],
        out_specs=[pl.BlockSpec((S, cb), lambda j, b: (b, j)), pl.BlockSpec((cb, S), lambda j, b: (j, b))],
        out_shape=[jax.ShapeDtypeStruct((B * S, FFN_DIM), BF16), jax.ShapeDtypeStruct((FFN_DIM, B * S), BF16)],
        compiler_params=_params(("parallel", "parallel")),
    )(ag, ag, w8)


def _conv_bwd(ag, w8, du, B, S, name):
    cb, ncb = _conv_blocks(S)

    def body(a_ref, g_ref, w_ref, du_ref, da_ref, dg_ref, dw_ref):
        g = g_ref[...].astype(F32)
        w = w_ref[...]
        g1, g2 = _shift_down(g, 1), _shift_down(g, 2)
        gc = w[0:1] * g2 + w[1:2] * g1 + w[2:3] * g + w[3:4]
        sg = _sigmoid(gc)
        du_v = du_ref[...]
        da_ref[...] = (du_v * (gc * sg)).astype(da_ref.dtype)
        dgc = du_v * a_ref[...].astype(F32) * (sg * (1.0 + gc * (1.0 - sg)))
        dg = w[2:3] * dgc + w[1:2] * _shift_up(dgc, 1) + w[0:1] * _shift_up(dgc, 2)
        dg_ref[...] = dg.astype(dg_ref.dtype)
        part = jnp.concatenate([
            jnp.sum(dgc * g2, axis=0, keepdims=True), jnp.sum(dgc * g1, axis=0, keepdims=True),
            jnp.sum(dgc * g, axis=0, keepdims=True), jnp.sum(dgc, axis=0, keepdims=True),
            jnp.zeros((4, cb), F32)], axis=0)

        @pl.when(pl.program_id(1) == 0)
        def _():
            dw_ref[...] = part

        @pl.when(pl.program_id(1) > 0)
        def _():
            dw_ref[...] += part

    blk = lambda j, b: (b, j)
    return pl.pallas_call(
        body, name=name, grid=(ncb, B),
        in_specs=[pl.BlockSpec((S, cb), blk),
                  pl.BlockSpec((S, cb), lambda j, b: (b, ncb + j)),
                  pl.BlockSpec((8, cb), lambda j, b: (0, j)),
                  pl.BlockSpec((S, cb), blk)],
        out_specs=[pl.BlockSpec((S, cb), blk), pl.BlockSpec((S, cb), blk),
                   pl.BlockSpec((8, cb), lambda j, b: (0, j))],
        out_shape=[jax.ShapeDtypeStruct((B * S, FFN_DIM), BF16), jax.ShapeDtypeStruct((B * S, FFN_DIM), BF16),
                   jax.ShapeDtypeStruct((8, FFN_DIM), F32)],
        compiler_params=_params(("parallel", "arbitrary")),
    )(ag, ag, w8, du)


_NT = (((1,), (1,)), ((), ()))
_NN = (((1,), (0,)), ((), ()))
_TN = (((0,), (0,)), ((), ()))


def _dot(a, b, dn):
    return lax.dot_general(a.astype(MXU_DTYPE), b.astype(MXU_DTYPE), dn, preferred_element_type=F32)


def _rel_and_mask():
    il = lax.broadcasted_iota(jnp.int32, (ATT_BLOCK, ATT_BLOCK), 0)
    jl = lax.broadcasted_iota(jnp.int32, (ATT_BLOCK, ATT_BLOCK), 1)
    return (il - jl).astype(F32), (jl // CHUNK) <= (il // CHUNK)


def _rows(i):
    return pl.ds(pl.multiple_of(i * ATT_BLOCK, ATT_BLOCK), ATT_BLOCK)


def _run_bits(n):
    bits, b = [], 1
    while b < n:
        bits.append(b)
        b *= 2
    return bits[::-1]


def _key_runs(n, nq, update):
    for bit in _run_bits(nq + 1):
        @pl.when((n & bit) != 0)
        def _(bit=bit):
            update(n & ~(2 * bit - 1), bit, (n & (bit - 1)) == 0)


def _earlier_runs(n, nq, update):
    for bit in _run_bits(nq):
        @pl.when((n & bit) != 0)
        def _(bit=bit):
            update(n & ~(2 * bit - 1), bit, False)


def _chunk_visible(shape, nblk, blk):
    key = lax.broadcasted_iota(jnp.int32, shape, 0) - (nblk - 1) * blk
    query = lax.broadcasted_iota(jnp.int32, shape, 1)
    return jnp.logical_or(key < 0, (key // CHUNK) <= (query // CHUNK))


KV_UNROLL = 2


def _kv_loop(n, body, carry):
    main = n // KV_UNROLL

    def chunk(t, c):
        for u in range(KV_UNROLL):
            c = body(t * KV_UNROLL + u, c)
        return c

    carry = lax.fori_loop(0, main, chunk, carry)
    return lax.fori_loop(main * KV_UNROLL, n, body, carry)


def _mla_attn_fwd(q, k, v, B, S):
    blk = min(MLA_FWD_BLOCK, S)
    H, nq = MLA_HEADS, S // blk

    def body(q_ref, k_ref, v_ref, o_ref, lse_ref, m_ref, l_ref, acc_ref):
        def qblock(i, _):
            q_rows = pl.ds(pl.multiple_of(i * blk, blk), blk)
            qi = q_ref[q_rows, :]
            m_ref[...] = jnp.full(m_ref.shape, MASK_VALUE, F32)
            l_ref[...] = jnp.zeros(l_ref.shape, F32)
            acc_ref[...] = jnp.zeros(acc_ref.shape, F32)

            def keys(first, nblk, last):
                rows = pl.ds(pl.multiple_of(first * blk, blk), nblk * blk)
                s = _dot(k_ref[rows, :], qi, _NT)
                s = jnp.where(jnp.logical_or(_chunk_visible(s.shape, nblk, blk), jnp.logical_not(last)), s, MASK_VALUE)
                m = m_ref[...]
                m2 = jnp.maximum(m, jnp.max(s, axis=0, keepdims=True))
                alpha = jnp.exp(m - m2)
                p = jnp.exp(s - m2)
                l_ref[...] = alpha * l_ref[...] + jnp.sum(p, axis=0, keepdims=True)
                acc_ref[...] = alpha * acc_ref[...] + _dot(v_ref[rows, :], p, _TN)
                m_ref[...] = m2

            _key_runs(i + 1, nq, keys)
            l = l_ref[...]
            o_ref[q_rows, :] = (acc_ref[...] / l).T
            lse_ref[0, :, q_rows] = m_ref[...] + jnp.log(l)
            return 0

        lax.fori_loop(0, nq, qblock, 0)

    return pl.pallas_call(
        body, name="mla_attn_fwd", grid=(B, H),
        in_specs=[pl.BlockSpec((S, MLA_PAD), lambda b, h: (b, h)),
                  pl.BlockSpec((S, MLA_PAD), lambda b, h: (b, h)),
                  pl.BlockSpec((S, MLA_V), lambda b, h: (b, h))],
        out_specs=[pl.BlockSpec((S, MLA_V), lambda b, h: (b, h)),
                   pl.BlockSpec((1, 1, S), lambda b, h: (b * H + h, 0, 0))],
        out_shape=[jax.ShapeDtypeStruct((B * S, H * MLA_V), F32), jax.ShapeDtypeStruct((B * H, 1, S), F32)],
        scratch_shapes=[pltpu.VMEM((1, blk), F32), pltpu.VMEM((1, blk), F32), pltpu.VMEM((MLA_V, blk), F32)],
        compiler_params=_params(("parallel", "parallel")),
    )(q, k, v)


def _mla_attn_bwd(q, k, v, o, do, lse, B, S):
    blk = min(MLA_FWD_BLOCK, S)
    H, nq = MLA_HEADS, S // blk

    def body(q_ref, k_ref, v_ref, o_ref, do_ref, lse_ref, dq_ref, dk_ref, dv_ref, kt_ref, dqt_ref):
        dk_ref[...] = jnp.zeros(dk_ref.shape, F32)
        dv_ref[...] = jnp.zeros(dv_ref.shape, F32)
        for g in range(nq):
            kt_ref[g] = k_ref[g * blk:(g + 1) * blk, :].T

        def qblock(i, _):
            q_rows = pl.ds(pl.multiple_of(i * blk, blk), blk)
            qi = q_ref[q_rows, :]
            doi = do_ref[q_rows, :]
            delta = jnp.sum((doi * o_ref[q_rows, :]).T, axis=0, keepdims=True)
            lse_i = lse_ref[0, :, q_rows]
            doi = doi.astype(MXU_DTYPE)
            dqt_ref[...] = jnp.zeros(dqt_ref.shape, F32)

            def keys(first, nblk, last):
                rows = pl.ds(pl.multiple_of(first * blk, blk), nblk * blk)
                k_run, v_run = k_ref[rows, :], v_ref[rows, :]
                p = jnp.exp(_dot(k_run, qi, _NT) - lse_i)
                p = jnp.where(jnp.logical_or(_chunk_visible(p.shape, nblk, blk), jnp.logical_not(last)), p, 0.0)
                ds = (p * (_dot(v_run, doi, _NT) - delta)).astype(MXU_DTYPE)
                dk_ref[rows, :] += _dot(ds, qi, _NN)
                dv_ref[rows, :] += _dot(p, doi, _NN)
                for r in range(nblk):
                    dqt_ref[...] += _dot(kt_ref[first + r], ds[r * blk:(r + 1) * blk, :], _NN)

            _key_runs(i + 1, nq, keys)
            dq_ref[q_rows, :] = dqt_ref[...].T
            return 0

        lax.fori_loop(0, nq, qblock, 0)

    qk_spec = pl.BlockSpec((S, MLA_PAD), lambda b, h: (b, h))
    v_spec = pl.BlockSpec((S, MLA_V), lambda b, h: (b, h))
    return pl.pallas_call(
        body, name="mla_attn_bwd", grid=(B, H),
        in_specs=[qk_spec, qk_spec, v_spec, v_spec, v_spec,
                  pl.BlockSpec((1, 1, S), lambda b, h: (b * H + h, 0, 0))],
        out_specs=[qk_spec, qk_spec, v_spec],
        out_shape=[jax.ShapeDtypeStruct((B * S, H * MLA_PAD), F32), jax.ShapeDtypeStruct((B * S, H * MLA_PAD), F32),
                   jax.ShapeDtypeStruct((B * S, H * MLA_V), F32)],
        scratch_shapes=[pltpu.VMEM((nq, MLA_PAD, blk), q.dtype), pltpu.VMEM((MLA_PAD, blk), F32)],
        compiler_params=_params(("parallel", "parallel")),
    )(q, k, v, o, do, lse)


def _ret_log_gamma():
    lg = np.log1p(-np.exp2(RET_GAMMA_BASE - np.arange(RET_HEADS, dtype=np.float32))).astype(np.float32)
    return jnp.asarray(np.broadcast_to(lg[:, None, None], (RET_HEADS, 8, LANES)).copy())


RET_BLOCK = 512


def _ret_local_scale(lg, shape, blk, rising):
    local = lax.broadcasted_iota(jnp.int32, shape, 0) % blk
    return jnp.exp(lg * (local if rising else blk - 1 - local).astype(F32))


def _ret_pair_factor(lg, blk, steps):
    return jnp.exp(lg * (blk * (steps - 1) + 1).astype(F32))


def _ret_own_decay(lg, blk, transposed):
    a = lax.broadcasted_iota(jnp.int32, (blk, blk), 0)
    b = lax.broadcasted_iota(jnp.int32, (blk, blk), 1)
    query, key = (b, a) if transposed else (a, b)
    dec = jnp.exp(lg * jnp.abs(query - key).astype(F32))
    return jnp.where((key // CHUNK) <= (query // CHUNK), dec, 0.0)


def _ret_attn_fwd(q, k, v, B, S):
    blk = min(RET_BLOCK, S)
    H, nq = RET_HEADS, S // blk

    def body(lg_ref, q_ref, k_ref, v_ref, o_ref, ks_ref, dec_ref, acc_ref):
        lg = lg_ref[0, 0:1, 0:1]
        ks_ref[...] = (k_ref[...].astype(F32) * _ret_local_scale(lg, k_ref.shape, blk, False)).astype(ks_ref.dtype)
        dec_ref[...] = _ret_own_decay(lg, blk, False)

        def qblock(i, _):
            q_rows = pl.ds(pl.multiple_of(i * blk, blk), blk)
            qi = q_ref[q_rows, :]
            qs = (qi.astype(F32) * _ret_local_scale(lg, qi.shape, blk, True)).astype(qi.dtype)
            a = _dot(qi, k_ref[q_rows, :], _NT) * dec_ref[...]
            acc_ref[...] = _dot(a, v_ref[q_rows, :], _NN)

            def keys(first, nblk, _):
                rows = pl.ds(pl.multiple_of(first * blk, blk), nblk * blk)
                steps = i - first - lax.broadcasted_iota(jnp.int32, (1, nblk * blk), 1) // blk
                a = _dot(qs, ks_ref[rows, :], _NT) * _ret_pair_factor(lg, blk, steps)
                acc_ref[...] += _dot(a, v_ref[rows, :], _NN)

            _earlier_runs(i, nq, keys)
            o_ref[q_rows, :] = acc_ref[...]
            return 0

        lax.fori_loop(0, nq, qblock, 0)

    qk_spec = pl.BlockSpec((S, RET_QK), lambda b, h: (b, h))
    v_spec = pl.BlockSpec((S, RET_V), lambda b, h: (b, h))
    return pl.pallas_call(
        body, name="ret_attn_fwd", grid=(B, H),
        in_specs=[pl.BlockSpec((1, 8, LANES), lambda b, h: (h, 0, 0)), qk_spec, qk_spec, v_spec],
        out_specs=v_spec,
        out_shape=jax.ShapeDtypeStruct((B * S, H * RET_V), F32),
        scratch_shapes=[pltpu.VMEM((S, RET_QK), k.dtype), pltpu.VMEM((blk, blk), F32), pltpu.VMEM((blk, RET_V), F32)],
        compiler_params=_params(("parallel", "parallel")),
    )(_ret_log_gamma(), q, k, v)


def _ret_attn_bwd(q, k, v, do, B, S):
    blk = min(RET_BLOCK, S)
    H, nq = RET_HEADS, S // blk

    def body(lg_ref, q_ref, k_ref, v_ref, do_ref, dq_ref, dk_ref, dv_ref, ks_ref, kst_ref, dks_ref, dqt_ref, dec_ref):
        lg = lg_ref[0, 0:1, 0:1]
        dk_ref[...] = jnp.zeros(dk_ref.shape, F32)
        dv_ref[...] = jnp.zeros(dv_ref.shape, F32)
        dks_ref[...] = jnp.zeros(dks_ref.shape, F32)
        ks_ref[...] = (k_ref[...].astype(F32) * _ret_local_scale(lg, k_ref.shape, blk, False)).astype(ks_ref.dtype)
        for g in range(nq):
            kst_ref[g] = ks_ref[g * blk:(g + 1) * blk, :].T
        dec_ref[...] = _ret_own_decay(lg, blk, True)

        def qblock(i, _):
            q_rows = pl.ds(pl.multiple_of(i * blk, blk), blk)
            qi = q_ref[q_rows, :]
            q_scale = _ret_local_scale(lg, qi.shape, blk, True)
            qs = (qi.astype(F32) * q_scale).astype(qi.dtype)
            doi = do_ref[q_rows, :].astype(MXU_DTYPE)
            ki = k_ref[q_rows, :]
            dec = dec_ref[...]
            a = _dot(ki, qi, _NT) * dec
            da = (_dot(v_ref[q_rows, :], doi, _NT) * dec).astype(MXU_DTYPE)
            dv_ref[q_rows, :] += _dot(a, doi, _NN)
            dk_ref[q_rows, :] += _dot(da, qi, _NN)
            dq_own = _dot(da, ki, _TN)
            dqt_ref[...] = jnp.zeros(dqt_ref.shape, F32)

            def keys(first, nblk, _):
                for r in range(nblk):
                    g = first + r
                    rows = pl.ds(pl.multiple_of(g * blk, blk), blk)
                    c = _ret_pair_factor(lg, blk, i - g)
                    a = _dot(ks_ref[rows, :], qs, _NT) * c
                    da = (_dot(v_ref[rows, :], doi, _NT) * c).astype(MXU_DTYPE)
                    dv_ref[rows, :] += _dot(a, doi, _NN)
                    dks_ref[rows, :] += _dot(da, qs, _NN)
                    dqt_ref[...] += _dot(kst_ref[g], da, _NN)

            _earlier_runs(i, nq, keys)
            dq_ref[q_rows, :] = dqt_ref[...].T * q_scale + dq_own
            return 0

        lax.fori_loop(0, nq, qblock, 0)
        dk_ref[...] += dks_ref[...] * _ret_local_scale(lg, dks_ref.shape, blk, False)

    qk_spec = pl.BlockSpec((S, RET_QK), lambda b, h: (b, h))
    v_spec = pl.BlockSpec((S, RET_V), lambda b, h: (b, h))
    return pl.pallas_call(
        body, name="ret_attn_bwd", grid=(B, H),
        in_specs=[pl.BlockSpec((1, 8, LANES), lambda b, h: (h, 0, 0)), qk_spec, qk_spec, v_spec, v_spec],
        out_specs=[qk_spec, qk_spec, v_spec],
        out_shape=[jax.ShapeDtypeStruct((B * S, H * RET_QK), F32), jax.ShapeDtypeStruct((B * S, H * RET_QK), F32),
                   jax.ShapeDtypeStruct((B * S, H * RET_V), F32)],
        scratch_shapes=[pltpu.VMEM((S, RET_QK), k.dtype), pltpu.VMEM((nq, RET_QK, blk), k.dtype),
                        pltpu.VMEM((S, RET_QK), F32), pltpu.VMEM((RET_QK, blk), F32), pltpu.VMEM((blk, blk), F32)],
        compiler_params=_params(("parallel", "parallel")),
    )(_ret_log_gamma(), q, k, v, do)


def _loss_head(y, target, bm=512):
    T, D = y.shape
    bm = _pick(T, bm)

    def body(y_ref, t_ref, dy_ref, dyc_ref, l_ref):
        err = y_ref[...] - t_ref[...]
        dy_ref[...] = err / D
        dyc_ref[...] = (err / D).astype(dyc_ref.dtype)
        part = jnp.full((8, LANES), 0.5 * jnp.sum(jnp.mean(err * err, axis=-1)), F32)

        @pl.when(pl.program_id(0) == 0)
        def _():
            l_ref[...] = part

        @pl.when(pl.program_id(0) > 0)
        def _():
            l_ref[...] += part

    blk = pl.BlockSpec((bm, D), lambda i: (i, 0))
    dy, dyc, l = pl.pallas_call(
        body, name="loss_head", grid=(T // bm,),
        in_specs=[blk, blk], out_specs=[blk, blk, pl.BlockSpec((8, LANES), lambda i: (0, 0))],
        out_shape=[jax.ShapeDtypeStruct((T, D), F32), jax.ShapeDtypeStruct((T, D), BF16),
                   jax.ShapeDtypeStruct((8, LANES), F32)],
        compiler_params=_params(("arbitrary",)),
    )(y, target)
    return dy, dyc, l[0, 0]


def _adamw(w, g, m, v, name):
    R, C = w.shape
    br = R if R * C * 4 <= 2 ** 21 else _pick_rows(R, max(8, (2 ** 21) // (C * 4)))

    def body(w_ref, g_ref, m_ref, v_ref, d_ref, mo_ref, vo_ref):
        g_v = g_ref[...]
        m_v = ADAM_B1 * m_ref[...] + (1.0 - ADAM_B1) * g_v
        v_v = ADAM_B2 * v_ref[...] + (1.0 - ADAM_B2) * (g_v * g_v)
        m_hat = m_v / (1.0 - ADAM_B1 ** ADAM_STEP)
        v_hat = v_v / (1.0 - ADAM_B2 ** ADAM_STEP)
        d_ref[...] = -ADAM_LR * (m_hat / (jnp.sqrt(v_hat) + ADAM_EPS) + ADAM_WD * w_ref[...])
        mo_ref[...] = m_v
        vo_ref[...] = v_v

    blk = pl.BlockSpec((br, C), lambda i: (i, 0))
    return pl.pallas_call(
        body, name=name, grid=(R // br,),
        in_specs=[blk] * 4, out_specs=[blk] * 3,
        out_shape=[jax.ShapeDtypeStruct((R, C), F32)] * 3,
        compiler_params=_params(("parallel",)),
    )(w, g, m, v)


def _pick_rows(R, target):
    best = None
    for d in range(8, min(R, target) + 1, 8):
        if R % d == 0:
            best = d
    assert best is not None, (R, target)
    return best


def _position():
    return lax.axis_index("x"), lax.axis_index("y"), lax.axis_index("c")


HBM_SPEC = pl.BlockSpec(memory_space=pltpu.HBM)


def _other_chips(x, y):
    return [(1 - x, y), (x, 1 - y), (1 - x, 1 - y)]


def _all_gather_weights(bigs, small):
    nb = len(bigs)

    def body(*refs):
        big_refs, small_ref = refs[:nb], refs[nb]
        obig, osmall = refs[nb + 1:2 * nb + 1], refs[2 * nb + 1]
        ici_send, ici_recv, d2d_send, d2d_recv, sm_send, sm_recv = refs[2 * nb + 2:]
        x, y, c = _position()
        me = 2 * x + y
        chips = _other_chips(x, y)

        def rows(n, half):
            rh = bigs[n].shape[0] // 2
            return pl.ds(half * rh, rh)

        def over_ici(n, j, slot, from_shard):
            px, py = chips[j]
            dst = obig[n].at[slot, rows(n, c)]
            return pltpu.make_async_remote_copy(
                src_ref=big_refs[n].at[rows(n, c)] if from_shard else dst, dst_ref=dst,
                send_sem=ici_send.at[3 * n + j], recv_sem=ici_recv.at[3 * n + j],
                device_id=(px, py, c), device_id_type=MESH)

        def over_d2d(n, j, half):
            px, py = chips[j]
            part = obig[n].at[2 * px + py, rows(n, half)]
            return pltpu.make_async_remote_copy(
                src_ref=part, dst_ref=part, send_sem=d2d_send.at[3 * n + j], recv_sem=d2d_recv.at[3 * n + j],
                device_id=(x, y, 1 - c), device_id_type=MESH)

        def small_copy(j, slot):
            px, py = chips[j]
            return pltpu.make_async_remote_copy(
                src_ref=small_ref, dst_ref=osmall.at[slot], send_sem=sm_send.at[j], recv_sem=sm_recv.at[j],
                device_id=(px, py, c), device_id_type=MESH)

        sends = [over_ici(n, j, me, True) for n in range(nb) for j in range(3)]
        sends += [small_copy(j, me) for j in range(3)]
        for cp in sends:
            cp.start()
        passed = []
        for n in range(nb):
            for j, (px, py) in enumerate(chips):
                over_ici(n, j, 2 * px + py, False).wait_recv()
                fwd = over_d2d(n, j, c)
                fwd.start()
                passed.append(fwd)
        for n in range(nb):
            for j in range(3):
                over_d2d(n, j, 1 - c).wait_recv()
        for j, (px, py) in enumerate(chips):
            small_copy(j, 2 * px + py).wait_recv()
        for cp in sends + passed:
            cp.wait_send()

    dma = pltpu.SemaphoreType.DMA
    return pl.pallas_call(
        body, name="weights_all_gather",
        in_specs=[HBM_SPEC] * (nb + 1), out_specs=[HBM_SPEC] * (nb + 1),
        out_shape=[jax.ShapeDtypeStruct((N_SHARD,) + b.shape, b.dtype) for b in bigs]
        + [jax.ShapeDtypeStruct((N_SHARD,) + small.shape, small.dtype)],
        scratch_shapes=[dma((3 * nb,)), dma((3 * nb,)), dma((3 * nb,)), dma((3 * nb,)), dma((3,)), dma((3,))],
    )(*bigs, small)


SEM_SPEC = pl.BlockSpec(memory_space=pltpu.SEMAPHORE)
DATAFLOW_EFFECT = pltpu.SideEffectType.DATAFLOW_SIDE_EFFECTING
N_PEERS = N_DEV - 1


def _grad_copies(p_refs, land_refs, send_sems, recv_sems):
    x, y, c = _position()
    copies = []
    for a, (p_ref, land_ref) in enumerate(zip(p_refs, land_refs)):
        rh = p_ref.shape[1] // 2
        for k in range(1, N_DEV):
            px = 1 - x if k & 4 else x
            py = 1 - y if k & 2 else y
            pc = 1 - c if k & 1 else c
            copies.append(pltpu.make_async_remote_copy(
                src_ref=p_ref.at[2 * px + py, pl.ds(pc * rh, rh)], dst_ref=land_ref.at[k - 1],
                send_sem=send_sems.at[N_PEERS * a + k - 1], recv_sem=recv_sems.at[N_PEERS * a + k - 1],
                device_id=(px, py, pc), device_id_type=MESH))
    return copies


def _weight_copies(w_refs, land_refs, send_sems, recv_sems):
    x, y, c = _position()
    copies = []
    for a, (w_ref, land_ref) in enumerate(zip(w_refs, land_refs)):
        for j, (px, py) in enumerate(_other_chips(x, y)):
            copies.append(pltpu.make_async_remote_copy(
                src_ref=w_ref, dst_ref=land_ref.at[2 * x + y], send_sem=send_sems.at[3 * a + j],
                recv_sem=recv_sems.at[3 * a + j], device_id=(px, py, c), device_id_type=MESH))
    return copies


def _exchange_start(make_copies, srcs, lands, n_sems, name, after=None):
    n, m = len(srcs), len(lands)
    n_in = n + m + (after is not None)

    def body(*refs):
        send_sems, recv_sems, token = refs[n_in], refs[n_in + 1], refs[-1]
        for cp in make_copies(refs[:n], refs[n:n + m], send_sems, recv_sems):
            cp.start()
        token[...] = jnp.zeros(token.shape, token.dtype)

    hbm = lambda a: pltpu.with_memory_space_constraint(a, pltpu.HBM)
    dma = pltpu.SemaphoreType.DMA
    res = pl.pallas_call(
        body, name=name,
        in_specs=[HBM_SPEC] * (n + m) + ([] if after is None else [pl.BlockSpec(memory_space=pl.ANY)]),
        out_specs=[SEM_SPEC, SEM_SPEC] + [HBM_SPEC] * (n + m) + [pl.BlockSpec(memory_space=pltpu.VMEM)],
        out_shape=[dma((n_sems,)), dma((n_sems,))] + [pltpu.HBM(a.shape, a.dtype) for a in list(srcs) + list(lands)]
        + [jax.ShapeDtypeStruct((8, LANES), F32)],
        input_output_aliases={i: 2 + i for i in range(n + m)},
        compiler_params=pltpu.CompilerParams(has_side_effects=DATAFLOW_EFFECT),
    )(*[hbm(a) for a in srcs], *[hbm(a) for a in lands], *(() if after is None else (after,)))
    return res[0], res[1], list(res[2:2 + n]), list(res[2 + n:2 + n + m]), res[-1]


def _exchange_wait(make_copies, send_sems, recv_sems, srcs, lands, after, name):
    n, m = len(srcs), len(lands)

    def body(*refs):
        for cp in make_copies(refs[:n], refs[n:n + m], refs[n + m], refs[n + m + 1]):
            cp.wait_send()
            cp.wait_recv()

    res = pl.pallas_call(
        body, name=name,
        in_specs=[HBM_SPEC] * (n + m) + [SEM_SPEC, SEM_SPEC, pl.BlockSpec(memory_space=pl.ANY)],
        out_specs=[HBM_SPEC] * (n + m),
        out_shape=[pltpu.HBM(a.shape, a.dtype) for a in list(srcs) + list(lands)],
        input_output_aliases={i: i for i in range(n + m)},
        compiler_params=pltpu.CompilerParams(has_side_effects=DATAFLOW_EFFECT),
    )(*srcs, *lands, send_sems, recv_sems, after)
    return list(res[:n]), list(res[n:])


def _sum_partials(p, land, name):
    _, rh, cols = land.shape
    br = _pick_rows(rh, 256)
    nrb = rh // br
    x, y, c = _position()
    where = jnp.stack([2 * x + y, c]).astype(jnp.int32)

    def body(where_ref, p_ref, land_ref, o_ref):
        acc = p_ref[...].astype(F32)
        for k in range(N_PEERS):
            acc = acc + land_ref[k].astype(F32)
        o_ref[...] = acc

    return pl.pallas_call(
        body, name=name,
        grid_spec=pltpu.PrefetchScalarGridSpec(
            num_scalar_prefetch=1, grid=(nrb,),
            in_specs=[pl.BlockSpec((None, br, cols), lambda r, where_ref: (where_ref[0], where_ref[1] * nrb + r, 0)),
                      pl.BlockSpec((N_PEERS, br, cols), lambda r, where_ref: (0, r, 0))],
            out_specs=pl.BlockSpec((None, br, cols), lambda r, where_ref: (where_ref[1], r, 0))),
        out_shape=jax.ShapeDtypeStruct((2, rh, cols), F32),
        compiler_params=_params(("parallel",)),
    )(where, p, land)


def _sibling_share(fulls, name):
    n = len(fulls)

    def body(*refs):
        o_refs = refs[n:2 * n]
        send_sems, recv_sems = refs[2 * n:]
        x, y, c = _position()

        def copy(a, half):
            return pltpu.make_async_remote_copy(
                src_ref=o_refs[a].at[half], dst_ref=o_refs[a].at[half], send_sem=send_sems.at[a],
                recv_sem=recv_sems.at[a], device_id=(x, y, 1 - c), device_id_type=MESH)

        sends = [copy(a, c) for a in range(n)]
        for cp in sends:
            cp.start()
        for a in range(n):
            copy(a, 1 - c).wait_recv()
        for cp in sends:
            cp.wait_send()

    dma = pltpu.SemaphoreType.DMA
    return pl.pallas_call(
        body, name=name,
        in_specs=[HBM_SPEC] * n, out_specs=[HBM_SPEC] * n,
        out_shape=[jax.ShapeDtypeStruct(f.shape, f.dtype) for f in fulls],
        input_output_aliases={a: a for a in range(n)},
        scratch_shapes=[dma((n,)), dma((n,))],
    )(*fulls)


def _all_reduce_small(v):
    R, cols = v.shape

    def body(v_ref, o_ref, buf_ref, send_sems, recv_sems):
        x, y, c = _position()
        me = 4 * x + 2 * y + c
        buf_ref[me] = v_ref[...]
        sends = []
        for k in range(1, N_DEV):
            px = 1 - x if k & 4 else x
            py = 1 - y if k & 2 else y
            pc = 1 - c if k & 1 else c
            sends.append(pltpu.make_async_remote_copy(
                src_ref=v_ref, dst_ref=buf_ref.at[me], send_sem=send_sems.at[k - 1], recv_sem=recv_sems.at[k - 1],
                device_id=(px, py, pc), device_id_type=MESH))
        for cp in sends:
            cp.start()
        for k in range(1, N_DEV):
            px = 1 - x if k & 4 else x
            py = 1 - y if k & 2 else y
            pc = 1 - c if k & 1 else c
            pltpu.make_async_remote_copy(
                src_ref=v_ref, dst_ref=buf_ref.at[4 * px + 2 * py + pc], send_sem=send_sems.at[k - 1],
                recv_sem=recv_sems.at[k - 1], device_id=(px, py, pc), device_id_type=MESH).wait_recv()
        for cp in sends:
            cp.wait_send()
        acc = buf_ref[0]
        for d in range(1, N_DEV):
            acc = acc + buf_ref[d]
        o_ref[...] = acc

    return pl.pallas_call(
        body, name="small_grads_all_reduce",
        in_specs=[pl.BlockSpec(memory_space=pltpu.VMEM)], out_specs=pl.BlockSpec(memory_space=pltpu.VMEM),
        out_shape=jax.ShapeDtypeStruct((R, cols), F32),
        scratch_shapes=[pltpu.VMEM((N_DEV, R, cols), F32), pltpu.SemaphoreType.DMA((N_DEV - 1,)),
                        pltpu.SemaphoreType.DMA((N_DEV - 1,))],
    )(v)


def _rope_tables(S, half, width):
    inv_freq = ROPE_THETA ** (-jnp.arange(half, dtype=F32) / half)
    ang = jnp.arange(S).astype(F32)[:, None] * inv_freq[None, :]
    return jnp.cos(ang), jnp.sin(ang)


def _slot_rows(a):
    return a.reshape(N_SHARD, -1, a.shape[-1])


def _local_step(x, target, w, B, S, late, exchange, reduce_small):
    T = B * S
    D = D_MODEL
    bm = 256
    full = lambda a, wd, tile=None: (a, wd, 0, tile or wd)
    g = {}

    cos_r, sin_r = _rope_tables(S, RET_QK // 2, LANES)
    cos_m, sin_m = _rope_tables(S, MLA_ROPE // 2, LANES)
    zeros64 = jnp.zeros((S, 64), F32)
    cos_m = jnp.concatenate([cos_m, cos_m, zeros64], axis=1)
    sin_m = jnp.concatenate([-sin_m, sin_m, zeros64], axis=1)

    def ffn_fwd(xin, i):
        w.update(late(f"ffn{i}", xin))
        norm = w["ffn_norm"][i:i + 1]
        h, ht = _rowwise_fwd(_fn_rms, f"ffn{i}_norm", [full(xin, D)], [], [(norm, D)], [(D, D, BF16)], bm, S,
                             transposed=(0,))
        ag = _mm(h, w[f"ffn_w_in{i}"], "nn", BF16, f"ffn{i}_in", bn=1408, cols_outer=True)
        u, ut = _conv_fwd(ag, w["ffn_conv8"][i], B, S, f"ffn{i}_conv")
        xout = _mm(u, w[f"ffn_w_out{i}"], "nn", F32, f"ffn{i}_out", residual=xin, bk=FFN_DIM)
        return xout, (xin, norm, ht, ag, ut)

    def ffn_bwd(dxout, dxout_c, saved, i):
        xin, norm, ht, ag, ut = saved
        du = _mm(dxout_c, w[f"ffn_w_out{i}"], "nt", F32, f"ffn{i}_out_dx", bn=1408, cols_outer=True)
        g_w_out = _mm(ut, dxout_c, "nn", BF16, f"ffn{i}_out_dw", bm=1408, bn=512, bk=T)
        da, dg, dw8 = _conv_bwd(ag, w["ffn_conv8"][i], du, B, S, f"ffn{i}_conv_bwd")
        g_w_in = _mm(ht, [da, dg], "nn", BF16, f"ffn{i}_in_dw", bm=1024, bn=1408, bk=T // 2, out_slots=N_SHARD)
        token = exchange(f"ffn{i}", [g_w_in, _slot_rows(g_w_out)])
        dh = _mm_dx([da, dg], w[f"ffn_w_in{i}"], f"ffn{i}_in_dx", after=token)
        (dxin, dxin_c), (g_norm,) = _rowwise_bwd(_fn_rms, f"ffn{i}_norm_bwd", [full(xin, D)], [], [(norm, D)],
                                                 [(dh, D)], bm, S, adds={0: dxout}, mxu_copies=(0,))
        return dxin, dxin_c, (g_norm, dw8)

    h0, h0t = _rowwise_fwd(_fn_rms, "ret_norm", [full(x, D)], [], [(w["ret_norm"], D)], [(D, D, BF16)], bm, S,
                           transposed=(0,))
    proj = _mm(h0, w["ret_w_in"], "nn", BF16, "ret_in", after=w["started"], cols_outer=True)
    HQ, HV = RET_HEADS * RET_QK, RET_HEADS * RET_V
    rope_rows = [(proj, 2 * HQ + HV, 0, LANES)]
    q_r, k_r, v_r = _rowwise_fwd(_fn_ret_rope, "ret_rope", rope_rows, [cos_r, sin_r], [],
                                 [(HQ, LANES, BF16), (HQ, LANES, BF16), (HV, LANES, BF16)], bm, S)
    ret_o = _ret_attn_fwd(q_r, k_r, v_r, B, S)
    gate_rows = [full(ret_o, HV, RET_V), (proj, HV, 2, RET_V)]
    y0, y0t = _rowwise_fwd(_fn_ret_gate, "ret_gate", gate_rows, [], [(w["ret_gn"], RET_V)], [(HV, RET_V, BF16)], 128, S,
                           transposed=(0,))
    w.update(late("ret_out", y0))
    x1 = _mm(y0, w["ret_w_out"], "nn", F32, "ret_out", residual=x)
    x2, ffn0_saved = ffn_fwd(x1, 0)

    w.update(late("mla", x2))
    (h2,) = _rowwise_fwd(_fn_rms, "mla_norm", [full(x2, D)], [], [(w["mla_norm"], D)], [(D, D, BF16)], bm, S)
    proj2 = _mm(h2, w["mla_w_in"], "nn", F32, "mla_in")
    lat_consts = [(w["mla_q_norm"], LANES), (w["mla_kv_norm"], LANES)]
    cqn, ckvn, kr = _rowwise_fwd(_fn_mla_lat, "mla_latent_norm", [full(proj2, MLA_IN_PAD, LANES)], [], lat_consts,
                                 [(MLA_Q_RANK, LANES, BF16), (MLA_KV_RANK, LANES, BF16), (LANES, LANES, F32)], bm, S)
    qf = _mm(cqn, w["mla_w_qb"], "nn", BF16, "mla_qb")
    kvf = _mm(ckvn, w["mla_w_kvb"], "nn", BF16, "mla_kvb")
    HP, HVm = MLA_HEADS * MLA_PAD, MLA_HEADS * MLA_V
    head_rows = [full(qf, HP, LANES), full(kvf, HP, LANES), full(kr, LANES)]
    head_consts = [(w["mla_q_head_norm"], LANES), (w["mla_k_head_norm"], LANES)]
    q_a, k_a, v_a = _rowwise_fwd(_fn_mla_heads, "mla_heads", head_rows, [cos_m, sin_m], head_consts,
                                 [(HP, LANES, BF16), (HP, LANES, BF16), (HVm, LANES, BF16)], bm, S)
    att_o, lse = _mla_attn_fwd(q_a, k_a, v_a, B, S)
    x3 = _mm(att_o, w["mla_w_out"], "nn", F32, "mla_out", residual=x2)
    x4, ffn1_saved = ffn_fwd(x3, 1)

    dy, dy_c, loss = _loss_head(x4, target)

    dx3, dx3_c, (g_n1, dw8_1) = ffn_bwd(dy, dy_c, ffn1_saved, 1)

    d_att_o = _mm(dx3_c, w["mla_w_out"], "nt", F32, "mla_out_dx")
    g_mla_out = _mm(att_o, dx3_c, "tn", BF16, "mla_out_dw")
    dq_a, dk_a, dv_a = _mla_attn_bwd(q_a, k_a, v_a, att_o, d_att_o, lse, B, S)
    (dqf, dkvf, dkr), (g["mla_q_head_norm"], g["mla_k_head_norm"]) = _rowwise_bwd(
        _fn_mla_heads, "mla_heads_bwd", head_rows, [cos_m, sin_m], head_consts,
        [(dq_a, LANES), (dk_a, LANES), (dv_a, LANES)], 128, S, grad_dtypes=[BF16, BF16, F32])
    dcqn = _mm(dqf, w["mla_w_qb"], "nt", F32, "mla_qb_dx")
    g_qb = _mm(cqn, dqf, "tn", BF16, "mla_qb_dw")
    g_qb = _to_slots(_unpad_heads(g_qb, 1), 1).reshape(N_SHARD, MLA_Q_RANK, -1)
    dckvn = _mm(dkvf, w["mla_w_kvb"], "nt", F32, "mla_kvb_dx")
    g_kvb = _mm(ckvn, dkvf, "tn", BF16, "mla_kvb_dw", bn=512, out_slots=N_SHARD)
    (dproj2,), (g["mla_q_norm"], g["mla_kv_norm"]) = _rowwise_bwd(
        _fn_mla_lat, "mla_latent_norm_bwd", [full(proj2, MLA_IN_PAD, LANES)], [], lat_consts,
        [(dcqn, LANES), (dckvn, LANES), (dkr, LANES)], bm, S, grad_dtypes=[BF16])
    g_mla_in = _mm(h2, dproj2, "tn", BF16, "mla_in_dw")
    token = exchange("mla", [_slot_rows(g_mla_in[:, :MLA_IN]), g_qb, g_kvb, _slot_rows(g_mla_out)])
    dh2 = _mm(dproj2, w["mla_w_in"], "nt", F32, "mla_in_dx", after=token)
    (dx2, dx2_c), (g["mla_norm"],) = _rowwise_bwd(_fn_rms, "mla_norm_bwd", [full(x2, D)], [], [(w["mla_norm"], D)],
                                                  [(dh2, D)], bm, S, adds={0: dx3}, mxu_copies=(0,))

    dx1, dx1_c, (g_n0, dw8_0) = ffn_bwd(dx2, dx2_c, ffn0_saved, 0)

    dy0 = _mm(dx1_c, w["ret_w_out"], "nt", F32, "ret_out_dx")
    g_ret_out = _mm(y0t, dx1_c, "nn", BF16, "ret_out_dw", bm=1024, bn=512, bk=T)
    (d_ret_o, dgate), (g["ret_gn"],) = _rowwise_bwd(_fn_ret_gate, "ret_gate_bwd", gate_rows, [], [(w["ret_gn"], RET_V)],
                                                    [(dy0, RET_V)], 128, S, grad_dtypes=[F32, BF16])
    dq_r, dk_r, dv_r = _ret_attn_bwd(q_r, k_r, v_r, d_ret_o, B, S)
    (dqkv,), _ = _rowwise_bwd(_fn_ret_rope, "ret_rope_bwd", rope_rows, [cos_r, sin_r], [],
                              [(dq_r, LANES), (dk_r, LANES), (dv_r, LANES)], bm, S, grad_dtypes=[BF16], linear=True)
    dh0 = _mm_dx([dqkv, dgate], w["ret_w_in"], "ret_in_dx")
    (dx,), (g["ret_norm"],) = _rowwise_bwd(_fn_rms, "ret_norm_bwd", [full(x, D)], [], [(w["ret_norm"], D)],
                                           [(dh0, D)], bm, S, adds={0: dx1})
    g["ffn_norm"] = jnp.concatenate([g_n0, g_n1], axis=0)
    g["ffn_conv_w"] = jnp.stack([dw8_0[0:3], dw8_1[0:3]])
    g["ffn_conv_b"] = jnp.stack([dw8_0[3], dw8_1[3]])
    reduced_small = reduce_small(g)
    g_ret_in = _mm(h0t, [dqkv, dgate], "nn", BF16, "ret_in_dw", bn=512, bk=T, out_slots=N_SHARD, after=reduced_small)
    exchange("ret", [g_ret_in, _slot_rows(g_ret_out)])
    return loss, dx, reduced_small


_BIG = [("ret_w_in", 2), ("ret_w_out", 1), ("mla_w_in", 1), ("mla_w_qb", 2), ("mla_w_kvb", 2), ("mla_w_out", 1),
        ("ffn_w_in", 2), ("ffn_w_out", 1)]
_SMALL_SHARDED = [("ret_gn", 2), ("mla_norm", 1), ("mla_q_norm", 1), ("mla_kv_norm", 1), ("ffn_conv_w", 2)]
_SMALL_REPLICATED = ["ret_norm", "mla_q_head_norm", "mla_k_head_norm", "ffn_norm", "ffn_conv_b"]
_SMALL_ALL = ["ret_norm", "ret_gn", "mla_norm", "mla_q_norm", "mla_kv_norm", "mla_q_head_norm", "mla_k_head_norm",
              "ffn_norm", "ffn_conv_w", "ffn_conv_b"]


def _to_slots(full, axis):
    shape = full.shape
    split = shape[:axis] + (N_SHARD, shape[axis] // N_SHARD) + shape[axis + 1:]
    return jnp.moveaxis(full.reshape(split), axis, 0).reshape(N_SHARD, -1)


def _from_slots(slots, shard_shape, axis):
    parts = jnp.moveaxis(slots.reshape((N_SHARD,) + tuple(shard_shape)), 0, axis)
    full = shard_shape[:axis] + (N_SHARD * shard_shape[axis],) + shard_shape[axis + 1:]
    return parts.reshape(full)


def _pad_rows(flat, cols, row_unit):
    n, L = flat.shape
    unit = cols * row_unit
    Lp = -(-L // unit) * unit
    if Lp != L:
        flat = jnp.concatenate([flat, jnp.zeros((n, Lp - L), flat.dtype)], axis=1)
    return flat.reshape(n, Lp // cols, cols)


def _pad_heads(a, axis):
    shape = a.shape
    a = a.reshape(shape[:axis] + (MLA_HEADS, MLA_QK) + shape[axis + 1:])
    pad = [(0, 0)] * a.ndim
    pad[axis + 1] = (0, MLA_PAD - MLA_QK)
    return jnp.pad(a, pad).reshape(shape[:axis] + (MLA_HEADS * MLA_PAD,) + shape[axis + 1:])


def _unpad_heads(a, axis):
    shape = a.shape
    a = a.reshape(shape[:axis] + (MLA_HEADS, MLA_PAD) + shape[axis + 1:])
    a = lax.slice_in_dim(a, 0, MLA_QK, axis=axis + 1)
    return a.reshape(shape[:axis] + (MLA_HEADS * MLA_QK,) + shape[axis + 1:])


def kernel(x, ret_norm, ret_w_in, ret_gn, ret_w_out, mla_norm, mla_w_in, mla_q_norm, mla_w_qb, mla_kv_norm, mla_w_kvb, mla_q_head_norm, mla_k_head_norm, mla_w_out, ffn_norm, ffn_w_in, ffn_conv_w, ffn_conv_b, ffn_w_out, loss_target, m_ret_norm, m_ret_w_in, m_ret_gn, m_ret_w_out, m_mla_norm, m_mla_w_in, m_mla_q_norm, m_mla_w_qb, m_mla_kv_norm, m_mla_w_kvb, m_mla_q_head_norm, m_mla_k_head_norm, m_mla_w_out, m_ffn_norm, m_ffn_w_in, m_ffn_conv_w, m_ffn_conv_b, m_ffn_w_out, v_ret_norm, v_ret_w_in, v_ret_gn, v_ret_w_out, v_mla_norm, v_mla_w_in, v_mla_q_norm, v_mla_w_qb, v_mla_kv_norm, v_mla_w_kvb, v_mla_q_head_norm, v_mla_k_head_norm, v_mla_w_out, v_ffn_norm, v_ffn_w_in, v_ffn_conv_w, v_ffn_conv_b, v_ffn_w_out):
    names = ["ret_norm", "ret_w_in", "ret_gn", "ret_w_out", "mla_norm", "mla_w_in", "mla_q_norm", "mla_w_qb",
             "mla_kv_norm", "mla_w_kvb", "mla_q_head_norm", "mla_k_head_norm", "mla_w_out", "ffn_norm", "ffn_w_in",
             "ffn_conv_w", "ffn_conv_b", "ffn_w_out"]
    shard = dict(zip(names, [ret_norm, ret_w_in, ret_gn, ret_w_out, mla_norm, mla_w_in, mla_q_norm, mla_w_qb,
                             mla_kv_norm, mla_w_kvb, mla_q_head_norm, mla_k_head_norm, mla_w_out, ffn_norm, ffn_w_in,
                             ffn_conv_w, ffn_conv_b, ffn_w_out]))
    mom_m = dict(zip(names, [m_ret_norm, m_ret_w_in, m_ret_gn, m_ret_w_out, m_mla_norm, m_mla_w_in, m_mla_q_norm,
                             m_mla_w_qb, m_mla_kv_norm, m_mla_w_kvb, m_mla_q_head_norm, m_mla_k_head_norm, m_mla_w_out,
                             m_ffn_norm, m_ffn_w_in, m_ffn_conv_w, m_ffn_conv_b, m_ffn_w_out]))
    mom_v = dict(zip(names, [v_ret_norm, v_ret_w_in, v_ret_gn, v_ret_w_out, v_mla_norm, v_mla_w_in, v_mla_q_norm,
                             v_mla_w_qb, v_mla_kv_norm, v_mla_w_kvb, v_mla_q_head_norm, v_mla_k_head_norm, v_mla_w_out,
                             v_ffn_norm, v_ffn_w_in, v_ffn_conv_w, v_ffn_conv_b, v_ffn_w_out]))
    B, S, D = x.shape
    T = B * S
    sx, sy = lax.axis_index("x"), lax.axis_index("y")
    me = 2 * sx + sy

    two_d = lambda a: a.reshape(-1, a.shape[-1])
    small_sizes = [int(np.prod(shard[n].shape)) for n, _ in _SMALL_SHARDED]
    small = jnp.concatenate([shard[n].reshape(1, -1) for n, _ in _SMALL_SHARDED], axis=1)
    small = _pad_rows(small, LANES, 8)[0]
    as_mxu = lambda a: two_d(a).astype(BF16)
    is_me = lax.broadcasted_iota(jnp.int32, (N_SHARD, 1, 1), 0) == me
    with_own = lambda gathered, own: jnp.where(is_me, own[None], gathered)
    by_cols = lambda a: jnp.moveaxis(a, 0, 1).reshape(a.shape[1], -1)
    by_rows = lambda a: a.reshape(-1, a.shape[-1])
    pad_in = lambda a: jnp.pad(by_rows(a), ((0, 0), (0, MLA_IN_PAD - MLA_IN)))
    pad_qb = lambda a: _pad_heads(by_cols(a), 1)
    ret_in_shard = as_mxu(shard["ret_w_in"])
    g_ret_in, gsmall = _all_gather_weights([ret_in_shard], small)
    later = [
        ("ret_out", [("ret_w_out", as_mxu(shard["ret_w_out"]), by_rows)]),
        ("ffn0", [("ffn_w_in0", as_mxu(shard["ffn_w_in"][0]), by_cols), ("ffn_w_out0", as_mxu(shard["ffn_w_out"][0]), by_rows)]),
        ("mla", [("mla_w_in", as_mxu(shard["mla_w_in"]), pad_in), ("mla_w_qb", as_mxu(shard["mla_w_qb"]), pad_qb),
                 ("mla_w_kvb", as_mxu(shard["mla_w_kvb"]), by_cols), ("mla_w_out", as_mxu(shard["mla_w_out"]), by_rows)]),
        ("ffn1", [("ffn_w_in1", as_mxu(shard["ffn_w_in"][1]), by_cols), ("ffn_w_out1", as_mxu(shard["ffn_w_out"][1]), by_rows)]),
    ]
    gathering = {}
    token = gsmall
    for group, items in later:
        shards = [s_ for _, s_, _ in items]
        lands = [lax.empty((N_SHARD,) + s_.shape, s_.dtype) for s_ in shards]
        send_sems, recv_sems, shards, lands, token = _exchange_start(
            _weight_copies, shards, lands, 3 * len(shards), f"weights_start_{group}", after=token)
        gathering[group] = (send_sems, recv_sems, shards, lands, items)

    def late(group, after):
        send_sems, recv_sems, shards, lands, items = gathering[group]
        shards, lands = _exchange_wait(_weight_copies, send_sems, recv_sems, shards, lands, after,
                                       f"weights_wait_{group}")
        return {key: full(with_own(l_, s_)) for (key, _, full), s_, l_ in zip(items, shards, lands)}

    gsmall = with_own(gsmall, small).reshape(N_SHARD, -1)
    wfull = {}
    off = 0
    for (n, ax), sz in zip(_SMALL_SHARDED, small_sizes):
        wfull[n] = _from_slots(gsmall[:, off:off + sz], shard[n].shape, ax)
        off += sz
    for n in _SMALL_REPLICATED:
        wfull[n] = shard[n]

    conv8 = jnp.concatenate([wfull["ffn_conv_w"], wfull["ffn_conv_b"][:, None, :],
                             jnp.zeros((2, 4, FFN_DIM), F32)], axis=1)
    w = {
        "started": token, "ret_norm": wfull["ret_norm"], "ret_w_in": by_cols(with_own(g_ret_in, ret_in_shard)),
        "ret_gn": wfull["ret_gn"].reshape(1, RET_HEADS * RET_V), "mla_norm": wfull["mla_norm"],
        "mla_q_norm": wfull["mla_q_norm"], "mla_kv_norm": wfull["mla_kv_norm"],
        "mla_q_head_norm": jnp.pad(wfull["mla_q_head_norm"], ((0, 0), (0, MLA_PAD - MLA_QK))),
        "mla_k_head_norm": jnp.pad(wfull["mla_k_head_norm"], ((0, 0), (0, MLA_PAD - MLA_QK))),
        "ffn_norm": wfull["ffn_norm"], "ffn_conv8": conv8,
    }

    started = {}

    def exchange(group, arrays):
        lands = [lax.empty((N_PEERS, p.shape[1] // 2, p.shape[2]), p.dtype) for p in arrays]
        send_sems, recv_sems, ps, lands, token = _exchange_start(
            _grad_copies, arrays, lands, N_PEERS * len(arrays), f"grads_start_{group}")
        started[group] = (send_sems, recv_sems, ps, lands)
        return token

    small_shapes = {
        "ret_norm": (1, D_MODEL), "ret_gn": (1, RET_HEADS, RET_V), "mla_norm": (1, D_MODEL),
        "mla_q_norm": (1, MLA_Q_RANK), "mla_kv_norm": (1, MLA_KV_RANK), "mla_q_head_norm": (1, MLA_QK),
        "mla_k_head_norm": (1, MLA_QK), "ffn_norm": (2, D_MODEL), "ffn_conv_w": (2, 3, FFN_DIM),
        "ffn_conv_b": (2, FFN_DIM)}

    def reduce_small(gl):
        gl = dict(gl, mla_q_head_norm=gl["mla_q_head_norm"][:, :MLA_QK], mla_k_head_norm=gl["mla_k_head_norm"][:, :MLA_QK])
        packed = jnp.concatenate([gl[n].reshape(1, -1) for n in _SMALL_ALL], axis=1)
        return _all_reduce_small(_pad_rows(packed, LANES, 8)[0])

    loss_part, dx, gsm = _local_step(x.reshape(T, D), loss_target.reshape(T, D), w, B, S, late, exchange,
                                     reduce_small)
    loss = lax.psum(loss_part, ("x", "y", "c"))

    delta, new_m, new_v, grads = {}, {}, {}, {}

    def reduced(group, after):
        send_sems, recv_sems, ps, lands = started[group]
        ps, lands = _exchange_wait(_grad_copies, send_sems, recv_sems, ps, lands, after, f"grads_wait_{group}")
        halves = [_sum_partials(p_, l_, f"grads_sum_{group}_{i}") for i, (p_, l_) in enumerate(zip(ps, lands))]
        return [two_d(r) for r in _sibling_share(halves, f"grads_share_{group}")]

    def adamw(n, g_):
        shp = shard[n].shape
        grads[n] = g_.reshape(shp)
        flat = lambda a: a.reshape(-1, shp[-1])
        d_, m_, v_ = _adamw(flat(shard[n]), flat(grads[n]), flat(mom_m[n]), flat(mom_v[n]), f"adamw_{n}")
        delta[n], new_m[n], new_v[n] = d_.reshape(shp), m_.reshape(shp), v_.reshape(shp)
        return d_

    ffn1 = reduced("ffn1", started["ret"][2][0])
    mla = reduced("mla", ffn1[0])
    ffn0 = reduced("ffn0", mla[0])
    early = [adamw(n, g_) for n, g_ in zip(["mla_w_in", "mla_w_qb", "mla_w_kvb", "mla_w_out"], mla)]
    early.append(adamw("ffn_w_in", jnp.stack([ffn0[0], ffn1[0]])))
    early.append(adamw("ffn_w_out", jnp.stack([ffn0[1], ffn1[1]])))
    ret = reduced("ret", jnp.stack([d_[0, 0] for d_ in early]))
    adamw("ret_w_in", ret[0])
    adamw("ret_w_out", ret[1])

    gsm = gsm.reshape(-1)
    sharded_axis = dict(_SMALL_SHARDED)
    off = 0
    for n in _SMALL_ALL:
        sz = int(np.prod(small_shapes[n]))
        gn = gsm[off:off + sz].reshape(small_shapes[n])
        off += sz
        if n in sharded_axis:
            ax = sharded_axis[n]
            width = shard[n].shape[ax]
            gn = lax.dynamic_slice_in_dim(gn, me * width, width, axis=ax)
        grads[n] = gn

    pack_small = lambda d: _pad_rows(jnp.concatenate([d[n].reshape(1, -1) for n in _SMALL_ALL], axis=1), LANES, 8)[0]
    d_, m_, v_ = _adamw(pack_small(shard), pack_small(grads), pack_small(mom_m), pack_small(mom_v), "adamw_small")
    off = 0
    for n in _SMALL_ALL:
        sz = int(np.prod(shard[n].shape))
        for dst, src in ((delta, d_), (new_m, m_), (new_v, v_)):
            dst[n] = src.reshape(-1)[off:off + sz].reshape(shard[n].shape)
        off += sz

    return (loss, dx.reshape(B, S, D), *[grads[n] for n in names], *[delta[n] for n in names],
            *[new_m[n] for n in names], *[new_v[n] for n in names])
```

```python
import functools
import math

import numpy as np
import jax
import jax.numpy as jnp
from jax import lax
from jax.experimental import pallas as pl
from jax.experimental.pallas import tpu as pltpu

F32 = jnp.float32
BF16 = jnp.bfloat16
MXU_DTYPE = jnp.bfloat16

CHUNK = 64
RMS_EPS = 1e-6
ROPE_THETA = 10000.0
D_MODEL = 1024
RET_HEADS = 4
RET_QK = 256
RET_V = 512
RET_GAMMA_BASE = -5.0
MLA_HEADS = 8
MLA_Q_RANK = 384
MLA_KV_RANK = 256
MLA_NOPE = 128
MLA_ROPE = 64
MLA_V = 128
MLA_QK = MLA_NOPE + MLA_ROPE
MLA_PAD = 256
MLA_IN = MLA_Q_RANK + MLA_KV_RANK + MLA_ROPE
MLA_IN_PAD = MLA_IN + 64
MASK_VALUE = -1e30
FFN_DIM = 2816
ADAM_LR = 0.001
ADAM_B1 = 0.9
ADAM_B2 = 0.999
ADAM_EPS = 1e-08
ADAM_WD = 0.01
ADAM_STEP = 10

LANES = 128
ATT_BLOCK = 256
MLA_FWD_BLOCK = 512
VMEM_LIMIT = 56 * 2 ** 20
N_SHARD = 4
N_DEV = 8

MESH = pl.DeviceIdType.MESH


def _params(sem=None, **kw):
    return pltpu.CompilerParams(dimension_semantics=sem, vmem_limit_bytes=VMEM_LIMIT, **kw)


def _pick(dim, target):
    if dim <= target:
        return dim
    best = None
    for d in range(LANES, target + 1, LANES):
        if dim % d == 0:
            best = d
    assert best is not None, (dim, target)
    return best


def _mm(a, b, dims, out_dtype, name, residual=None, bm=512, bn=1024, bk=2048, out_slots=None, after=None,
        cols_outer=False):
    a_parts = list(a) if isinstance(a, (list, tuple)) else [a]
    b_parts = list(b) if isinstance(b, (list, tuple)) else [b]
    parts_on_n = dims == "tn" or len(b_parts) > 1
    if parts_on_n:
        assert len(a_parts) == 1 and dims in ("tn", "nn")
        (K, M) = a_parts[0].shape if dims == "tn" else a_parts[0].shape[::-1]
        N = sum(p.shape[1] for p in b_parts)
        part_widths = [p.shape[1] for p in b_parts]
    else:
        assert len(b_parts) == 1
        M = a_parts[0].shape[0]
        K = sum(p.shape[1] for p in a_parts)
        N = b_parts[0].shape[1 if dims == "nn" else 0]
        part_widths = [p.shape[1] for p in a_parts]
    bm, bn, bk = _pick(M, bm), _pick(N, bn), _pick(K, min(bk, 1024) if dims == "tn" else bk)
    nk = K // bk
    unit = bn if parts_on_n else bk
    assert all(wd % unit == 0 for wd in part_widths), (name, part_widths, unit)
    bounds = np.cumsum([0] + [wd // unit for wd in part_widths])
    ranges = [(int(lo), int(hi)) for lo, hi in zip(bounds[:-1], bounds[1:])]

    def part_index(idx, lo, hi):
        return jnp.clip(idx - lo, 0, hi - lo - 1)

    if parts_on_n:
        if dims == "tn":
            a_specs = [pl.BlockSpec((bk, bm), lambda i, j, k: (k, i))]
            dn = (((0,), (0,)), ((), ()))
        else:
            a_specs = [pl.BlockSpec((bm, bk), lambda i, j, k: (i, k))]
            dn = (((1,), (0,)), ((), ()))
        b_specs = [pl.BlockSpec((bk, bn), functools.partial(lambda i, j, k, lo, hi: (k, part_index(j, lo, hi)), lo=lo, hi=hi))
                   for lo, hi in ranges]
    else:
        a_specs = [pl.BlockSpec((bm, bk), functools.partial(lambda i, j, k, lo, hi: (i, part_index(k, lo, hi)), lo=lo, hi=hi))
                   for lo, hi in ranges]
        if dims == "nt":
            b_specs = [pl.BlockSpec((bn, bk), lambda i, j, k: (j, k))]
        else:
            b_specs = [pl.BlockSpec((bk, bn), lambda i, j, k: (k, j))]
        dn = (((1,), (1 if dims == "nt" else 0,)), ((), ()))
    r_spec = pl.BlockSpec((bm, bn), lambda i, j, k: (i, j))
    if out_slots is None:
        o_spec, o_shape = r_spec, (M, N)
    else:
        ns = N // out_slots
        assert ns % bn == 0, (name, ns, bn)
        nbs = ns // bn
        o_spec = pl.BlockSpec((None, bm, bn), lambda i, j, k: (j // nbs, i, j % nbs))
        o_shape = (out_slots, M, ns)
    has_res = residual is not None
    na, nb = len(a_parts), len(b_parts)

    def body(*refs):
        a_refs, b_refs = refs[:na], refs[na:na + nb]
        r_ref = refs[na + nb] if has_res else None
        n_in = na + nb + has_res + (after is not None)
        o_ref = refs[n_in]
        acc_ref = refs[n_in + 1] if nk > 1 else None
        k = pl.program_id(2)

        def finish(acc):
            if has_res:
                acc = acc + r_ref[...].astype(F32)
            o_ref[...] = acc.astype(out_dtype)

        def compute(a_ref, b_ref):
            p = lax.dot_general(a_ref[...].astype(MXU_DTYPE), b_ref[...].astype(MXU_DTYPE), dn,
                                preferred_element_type=F32)
            if nk == 1:
                finish(p)
                return

            @pl.when(k == 0)
            def _():
                acc_ref[...] = p

            @pl.when(jnp.logical_and(k > 0, k < nk - 1))
            def _():
                acc_ref[...] += p

            @pl.when(k == nk - 1)
            def _():
                finish(acc_ref[...] + p)

        if len(ranges) == 1:
            compute(a_refs[0], b_refs[0])
        else:
            idx = pl.program_id(0 if cols_outer else 1) if parts_on_n else k
            for p, (lo, hi) in enumerate(ranges):
                @pl.when(jnp.logical_and(idx >= lo, idx < hi))
                def _(p=p):
                    compute(a_refs[0 if parts_on_n else p], b_refs[p if parts_on_n else 0])

    after_specs = [] if after is None else [pl.BlockSpec(after.shape, lambda i, j, k: (0, 0))]
    in_specs = a_specs + b_specs + ([r_spec] if has_res else []) + after_specs
    grid = (M // bm, N // bn, nk)
    if cols_outer:
        swap = lambda sp: pl.BlockSpec(sp.block_shape, functools.partial(lambda j, i, k, f: f(i, j, k), f=sp.index_map))
        in_specs, o_spec, grid = [swap(sp) for sp in in_specs], swap(o_spec), (grid[1], grid[0], nk)
    return pl.pallas_call(
        body, name=name, grid=grid,
        in_specs=in_specs, out_specs=o_spec,
        out_shape=jax.ShapeDtypeStruct(o_shape, out_dtype),
        scratch_shapes=[pltpu.VMEM((bm, bn), F32)] if nk > 1 else [],
        compiler_params=_params(("parallel", "parallel", "arbitrary")),
    )(*a_parts, *b_parts, *((residual,) if has_res else ()), *(() if after is None else (after,)))


def _mm_dx_norm(a_parts, w, x, gain, add, name, bm=256, after=None):
    M = a_parts[0].shape[0]
    N, K = w.shape
    widths = [p.shape[1] for p in a_parts]
    assert sum(widths) == K, (name, widths, K)
    offs = [int(o) for o in np.cumsum([0] + widths[:-1])]
    bm = _pick(M, bm)
    na = len(a_parts)
    n_in = na + 4 + (after is not None)

    def body(*refs):
        w_ref, x_ref, g_ref, add_ref = refs[na:na + 4]
        dx_ref, dxc_ref, dg_ref = refs[n_in:n_in + 3]
        dh = None
        for a_ref, off, wd in zip(refs[:na], offs, widths):
            p = lax.dot_general(a_ref[...].astype(MXU_DTYPE), w_ref[:, off:off + wd].astype(MXU_DTYPE), _NT,
                                preferred_element_type=F32)
            dh = p if dh is None else dh + p
        _, vjp = jax.vjp(lambda xv, gv: _fn_rms([[xv]], [], [[gv]])[0][0], x_ref[...], g_ref[...])
        dxv, dgv = vjp(dh)
        dxv = dxv + add_ref[...]
        dx_ref[...] = dxv
        dxc_ref[...] = dxv.astype(dxc_ref.dtype)

        @pl.when(pl.program_id(0) == 0)
        def _():
            dg_ref[...] = dgv

        @pl.when(pl.program_id(0) > 0)
        def _():
            dg_ref[...] += dgv

    row = pl.BlockSpec((bm, N), lambda i: (i, 0))
    whole = lambda a: pl.BlockSpec(a.shape, lambda i: (0, 0))
    in_specs = [pl.BlockSpec((bm, wd), lambda i: (i, 0)) for wd in widths] + [whole(w), row, whole(gain), row]
    in_specs += [] if after is None else [whole(after)]
    return pl.pallas_call(
        body, name=name, grid=(M // bm,),
        in_specs=in_specs, out_specs=[row, row, whole(gain)],
        out_shape=[jax.ShapeDtypeStruct((M, N), F32), jax.ShapeDtypeStruct((M, N), BF16),
                   jax.ShapeDtypeStruct(gain.shape, F32)],
        compiler_params=_params(("arbitrary",)),
    )(*a_parts, w, x, gain, add, *(() if after is None else (after,)))


def _tiles(ref, width, tile):
    return [ref[:, t * tile:(t + 1) * tile].astype(F32) for t in range(width // tile)]


def _row_specs(rows, pos, consts, bm, S):
    npos_blocks = S // bm
    specs = [pl.BlockSpec((bm, w), functools.partial(lambda i, c: (i, c), c=cb)) for (_, w, cb, _) in rows]
    specs += [pl.BlockSpec((bm, p.shape[1]), lambda i: (i % npos_blocks, 0)) for p in pos]
    specs += [pl.BlockSpec(c.shape, lambda i: (0, 0)) for (c, _) in consts]
    return specs


def _rowwise_fwd(fn, name, rows, pos, consts, outs, bm, S, transposed=()):
    T = rows[0][0].shape[0]
    nr, npos, nc, no = len(rows), len(pos), len(consts), len(outs)

    def body(*refs):
        row_v = [_tiles(r, w, t) for r, (_, w, _, t) in zip(refs[:nr], rows)]
        pos_v = [r[...] for r in refs[nr:nr + npos]]
        const_v = [_tiles(r, c.shape[1], t) for r, (c, t) in zip(refs[nr + npos:nr + npos + nc], consts)]
        res = fn(row_v, pos_v, const_v)
        out_refs = refs[nr + npos + nc:]
        for o_ref, tiles, (w, t, dt) in zip(out_refs, res, outs):
            for k, v in enumerate(tiles):
                o_ref[:, k * t:(k + 1) * t] = v.astype(dt)
        for t_ref, a in zip(out_refs[no:], transposed):
            t = outs[a][1]
            for k, v in enumerate(res[a]):
                t_ref[k * t:(k + 1) * t, :] = v.T.astype(t_ref.dtype)

    return pl.pallas_call(
        body, name=name, grid=(T // bm,),
        in_specs=_row_specs(rows, pos, consts, bm, S),
        out_specs=[pl.BlockSpec((bm, w), lambda i: (i, 0)) for (w, _, _) in outs]
        + [pl.BlockSpec((outs[a][0], bm), lambda i: (0, i)) for a in transposed],
        out_shape=[jax.ShapeDtypeStruct((T, w), dt) for (w, _, dt) in outs]
        + [jax.ShapeDtypeStruct((outs[a][0], T), BF16) for a in transposed],
        compiler_params=_params(("parallel",)),
    )(*[r[0] for r in rows], *pos, *[c[0] for c in consts])


def _rowwise_bwd(fn, name, rows, pos, consts, cts, bm, S, adds=None, grad_dtypes=None, mxu_copies=(), linear=False):
    adds = adds or {}
    T = rows[0][0].shape[0]
    nr, npos, nc, nct = len(rows), len(pos), len(consts), len(cts)
    add_idx = sorted(adds)
    grad_dtypes = grad_dtypes or [F32] * nr

    def body(*refs):
        it = iter(refs)
        row_refs = [None if linear else next(it) for _ in range(nr)]
        pos_refs = [next(it) for _ in range(npos)]
        const_refs = [next(it) for _ in range(nc)]
        ct_refs = [next(it) for _ in range(nct)]
        add_refs = {k: next(it) for k in add_idx}
        drow_refs = [next(it) for _ in range(nr)]
        copy_refs = {a: next(it) for a in mxu_copies}
        dconst_refs = [next(it) for _ in range(nc)]
        if linear:
            row_v = [[jnp.zeros((bm, t), F32)] * (w // t) for (_, w, _, t) in rows]
        else:
            row_v = [_tiles(r, w, t) for r, (_, w, _, t) in zip(row_refs, rows)]
        pos_v = [r[...] for r in pos_refs]
        const_v = [_tiles(r, c.shape[1], t) for r, (c, t) in zip(const_refs, consts)]
        ct_v = [_tiles(r, c.shape[1], t) for r, (c, t) in zip(ct_refs, cts)]
        _, vjp = jax.vjp(lambda rv, cv: fn(rv, pos_v, cv), row_v, const_v)
        drows, dconsts = vjp(ct_v)
        for a, (d_ref, tiles, (_, w, _, t)) in enumerate(zip(drow_refs, drows, rows)):
            for k, v in enumerate(tiles):
                if a in add_refs:
                    v = v + add_refs[a][:, k * t:(k + 1) * t].astype(F32)
                d_ref[:, k * t:(k + 1) * t] = v.astype(d_ref.dtype)
                if a in copy_refs:
                    copy_refs[a][:, k * t:(k + 1) * t] = v.astype(BF16)
        first = pl.program_id(0) == 0
        for d_ref, tiles, (_, t) in zip(dconst_refs, dconsts, consts):
            for k, v in enumerate(tiles):
                @pl.when(first)
                def _(d_ref=d_ref, k=k, t=t, v=v):
                    d_ref[:, k * t:(k + 1) * t] = v

                @pl.when(jnp.logical_not(first))
                def _(d_ref=d_ref, k=k, t=t, v=v):
                    d_ref[:, k * t:(k + 1) * t] += v

    in_specs = _row_specs([] if linear else rows, pos, consts, bm, S)
    in_specs += [pl.BlockSpec((bm, c.shape[1]), lambda i: (i, 0)) for (c, _) in cts]
    in_specs += [pl.BlockSpec((bm, adds[k].shape[1]), lambda i: (i, 0)) for k in add_idx]
    out_specs = [pl.BlockSpec((bm, w), lambda i: (i, 0)) for (_, w, _, _) in rows]
    out_specs += [pl.BlockSpec((bm, rows[a][1]), lambda i: (i, 0)) for a in mxu_copies]
    out_specs += [pl.BlockSpec(c.shape, lambda i: (0, 0)) for (c, _) in consts]
    out_shape = [jax.ShapeDtypeStruct((T, w), dt) for (_, w, _, _), dt in zip(rows, grad_dtypes)]
    out_shape += [jax.ShapeDtypeStruct((T, rows[a][1]), BF16) for a in mxu_copies]
    out_shape += [jax.ShapeDtypeStruct(c.shape, F32) for (c, _) in consts]
    res = pl.pallas_call(
        body, name=name, grid=(T // bm,),
        in_specs=in_specs, out_specs=out_specs, out_shape=out_shape,
        compiler_params=_params(("arbitrary",)),
    )(*([] if linear else [r[0] for r in rows]), *pos, *[c[0] for c in consts], *[c[0] for c in cts],
      *[adds[k] for k in add_idx])
    n_rows = nr + len(mxu_copies)
    return res[:n_rows], res[n_rows:]


def _ssq(tiles):
    s = jnp.sum(tiles[0] * tiles[0], axis=-1, keepdims=True)
    for t in tiles[1:]:
        s = s + jnp.sum(t * t, axis=-1, keepdims=True)
    return s


def _sigmoid(x):
    return 1.0 / (1.0 + jnp.exp(-x))


def _fn_rms(rows, pos, consts):
    (x,), (g,) = rows[0], consts[0]
    r = lax.rsqrt(jnp.mean(x * x, axis=-1, keepdims=True) + RMS_EPS)
    return [[x * r * g]]


def _fn_ret_rope(rows, pos, consts):
    (qkv,) = rows
    nq = RET_HEADS * RET_QK // LANES
    q, k, v = qkv[:nq], qkv[nq:2 * nq], qkv[2 * nq:]
    cos, sin = pos

    def rot(t, scale):
        out = []
        for h in range(RET_HEADS):
            x1, x2 = t[2 * h], t[2 * h + 1]
            o1, o2 = x1 * cos - x2 * sin, x2 * cos + x1 * sin
            out += [o1, o2] if scale is None else [o1 * scale, o2 * scale]
        return out

    return [rot(q, None), rot(k, RET_QK ** -0.5), list(v)]


def _fn_ret_gate(rows, pos, consts):
    o, g = rows
    (gn,) = consts
    out = []
    for h in range(RET_HEADS):
        r = lax.rsqrt(jnp.mean(o[h] * o[h], axis=-1, keepdims=True) + RMS_EPS)
        out.append((o[h] * r * gn[h]) * (g[h] * _sigmoid(g[h])))
    return [out]


def _fn_mla_lat(rows, pos, consts):
    (p,) = rows
    gq, gkv = consts
    nq, nkv = MLA_Q_RANK // LANES, MLA_KV_RANK // LANES
    cq, ckv, kr = p[:nq], p[nq:nq + nkv], p[nq + nkv]
    rq = lax.rsqrt(_ssq(cq) / MLA_Q_RANK + RMS_EPS)
    rkv = lax.rsqrt(_ssq(ckv) / MLA_KV_RANK + RMS_EPS)
    return [[t * rq * g for t, g in zip(cq, gq)], [t * rkv * g for t, g in zip(ckv, gkv)], [kr]]


def _swap32_impl(x):
    lane = lax.broadcasted_iota(jnp.int32, x.shape, 1)
    up, down = pltpu.roll(x, LANES - 32, 1), pltpu.roll(x, 32, 1)
    return jnp.where(lane < 32, up, jnp.where(lane < 64, down, 0.0))


@jax.custom_vjp
def _swap32(x):
    return _swap32_impl(x)


_swap32.defvjp(lambda x: (_swap32_impl(x), None), lambda _, g: (_swap32_impl(g),))


def _fn_mla_heads(rows, pos, consts):
    qf, kvf, (kr,) = rows
    cos, sin = pos
    gq, gk = consts
    q_out, k_out, v_out = [], [], []
    for h in range(MLA_HEADS):
        q0, q1 = qf[2 * h], qf[2 * h + 1]
        r = lax.rsqrt(_ssq([q0, q1]) / MLA_QK + RMS_EPS)
        a0, a1 = q0 * r * gq[0], q1 * r * gq[1]
        a1 = a1 * cos + _swap32(a1) * sin
        q_out += [a0 * (MLA_QK ** -0.5), a1 * (MLA_QK ** -0.5)]
        k0 = kvf[2 * h]
        r = lax.rsqrt(_ssq([k0, kr]) / MLA_QK + RMS_EPS)
        b0, b1 = k0 * r * gk[0], kr * r * gk[1]
        k_out += [b0, b1 * cos + _swap32(b1) * sin]
        v_out.append(kvf[2 * h + 1])
    return [q_out, k_out, v_out]


def _shift_down(x, n):
    row = lax.broadcasted_iota(jnp.int32, x.shape, 0)
    return jnp.where(row >= n, pltpu.roll(x, n, 0), 0.0)


def _shift_up(x, n):
    rows = x.shape[0]
    row = lax.broadcasted_iota(jnp.int32, x.shape, 0)
    return jnp.where(row < rows - n, pltpu.roll(x, rows - n, 0), 0.0)


def _conv_blocks(S):
    cb = 256
    return cb, FFN_DIM // cb


def _conv_fwd(ag, w8, B, S, name):
    cb, ncb = _conv_blocks(S)

    def body(a_ref, g_ref, w_ref, u_ref, ut_ref):
        g = g_ref[...].astype(F32)
        w = w_ref[...]
        gc = w[0:1] * _shift_down(g, 2) + w[1:2] * _shift_down(g, 1) + w[2:3] * g + w[3:4]
        u = a_ref[...].astype(F32) * (gc * _sigmoid(gc))
        u_ref[...] = u.astype(u_ref.dtype)
        ut_ref[...] = u.T.astype(ut_ref.dtype)

    return pl.pallas_call(
        body, name=name, grid=(ncb, B),
        in_specs=[pl.BlockSpec((S, cb), lambda j, b: (b, j)),
                  pl.BlockSpec((S, cb), lambda j, b: (b, ncb + j)),
                  pl.BlockSpec((8, cb), lambda j, b: (0, j))],
        out_specs=[pl.BlockSpec((S, cb), lambda j, b: (b, j)), pl.BlockSpec((cb, S), lambda j, b: (j, b))],
        out_shape=[jax.ShapeDtypeStruct((B * S, FFN_DIM), BF16), jax.ShapeDtypeStruct((FFN_DIM, B * S), BF16)],
        compiler_params=_params(("parallel", "parallel")),
    )(ag, ag, w8)


def _conv_bwd(ag, w8, du, B, S, name):
    cb, ncb = _conv_blocks(S)

    def body(a_ref, g_ref, w_ref, du_ref, da_ref, dg_ref, dw_ref):
        g = g_ref[...].astype(F32)
        w = w_ref[...]
        g1, g2 = _shift_down(g, 1), _shift_down(g, 2)
        gc = w[0:1] * g2 + w[1:2] * g1 + w[2:3] * g + w[3:4]
        sg = _sigmoid(gc)
        du_v = du_ref[...]
        da_ref[...] = (du_v * (gc * sg)).astype(da_ref.dtype)
        dgc = du_v * a_ref[...].astype(F32) * (sg * (1.0 + gc * (1.0 - sg)))
        dg = w[2:3] * dgc + w[1:2] * _shift_up(dgc, 1) + w[0:1] * _shift_up(dgc, 2)
        dg_ref[...] = dg.astype(dg_ref.dtype)
        part = jnp.concatenate([
            jnp.sum(dgc * g2, axis=0, keepdims=True), jnp.sum(dgc * g1, axis=0, keepdims=True),
            jnp.sum(dgc * g, axis=0, keepdims=True), jnp.sum(dgc, axis=0, keepdims=True),
            jnp.zeros((4, cb), F32)], axis=0)

        @pl.when(pl.program_id(1) == 0)
        def _():
            dw_ref[...] = part

        @pl.when(pl.program_id(1) > 0)
        def _():
            dw_ref[...] += part

    blk = lambda j, b: (b, j)
    return pl.pallas_call(
        body, name=name, grid=(ncb, B),
        in_specs=[pl.BlockSpec((S, cb), blk),
                  pl.BlockSpec((S, cb), lambda j, b: (b, ncb + j)),
                  pl.BlockSpec((8, cb), lambda j, b: (0, j)),
                  pl.BlockSpec((S, cb), blk)],
        out_specs=[pl.BlockSpec((S, cb), blk), pl.BlockSpec((S, cb), blk),
                   pl.BlockSpec((8, cb), lambda j, b: (0, j))],
        out_shape=[jax.ShapeDtypeStruct((B * S, FFN_DIM), BF16), jax.ShapeDtypeStruct((B * S, FFN_DIM), BF16),
                   jax.ShapeDtypeStruct((8, FFN_DIM), F32)],
        compiler_params=_params(("parallel", "arbitrary")),
    )(ag, ag, w8, du)


_NT = (((1,), (1,)), ((), ()))
_NN = (((1,), (0,)), ((), ()))
_TN = (((0,), (0,)), ((), ()))


def _dot(a, b, dn):
    return lax.dot_general(a.astype(MXU_DTYPE), b.astype(MXU_DTYPE), dn, preferred_element_type=F32)


def _rel_and_mask():
    il = lax.broadcasted_iota(jnp.int32, (ATT_BLOCK, ATT_BLOCK), 0)
    jl = lax.broadcasted_iota(jnp.int32, (ATT_BLOCK, ATT_BLOCK), 1)
    return (il - jl).astype(F32), (jl // CHUNK) <= (il // CHUNK)


def _rows(i):
    return pl.ds(pl.multiple_of(i * ATT_BLOCK, ATT_BLOCK), ATT_BLOCK)


def _run_bits(n):
    bits, b = [], 1
    while b < n:
        bits.append(b)
        b *= 2
    return bits[::-1]


def _key_runs(n, nq, update):
    for bit in _run_bits(nq + 1):
        @pl.when((n & bit) != 0)
        def _(bit=bit):
            update(n & ~(2 * bit - 1), bit, (n & (bit - 1)) == 0)


def _earlier_runs(n, nq, update):
    for bit in _run_bits(nq):
        @pl.when((n & bit) != 0)
        def _(bit=bit):
            update(n & ~(2 * bit - 1), bit, False)


def _chunk_visible(shape, nblk, blk):
    key = lax.broadcasted_iota(jnp.int32, shape, 0) - (nblk - 1) * blk
    query = lax.broadcasted_iota(jnp.int32, shape, 1)
    return jnp.logical_or(key < 0, (key // CHUNK) <= (query // CHUNK))


KV_UNROLL = 2


def _kv_loop(n, body, carry):
    main = n // KV_UNROLL

    def chunk(t, c):
        for u in range(KV_UNROLL):
            c = body(t * KV_UNROLL + u, c)
        return c

    carry = lax.fori_loop(0, main, chunk, carry)
    return lax.fori_loop(main * KV_UNROLL, n, body, carry)


def _mla_attn_fwd(q, k, v, B, S):
    blk = min(MLA_FWD_BLOCK, S)
    H, nq = MLA_HEADS, S // blk

    def body(q_ref, k_ref, v_ref, o_ref, lse_ref, m_ref, l_ref, acc_ref):
        def qblock(i, _):
            q_rows = pl.ds(pl.multiple_of(i * blk, blk), blk)
            qi = q_ref[q_rows, :]
            m_ref[...] = jnp.full(m_ref.shape, MASK_VALUE, F32)
            l_ref[...] = jnp.zeros(l_ref.shape, F32)
            acc_ref[...] = jnp.zeros(acc_ref.shape, F32)

            def keys(first, nblk, last):
                rows = pl.ds(pl.multiple_of(first * blk, blk), nblk * blk)
                s = _dot(k_ref[rows, :], qi, _NT)
                s = jnp.where(jnp.logical_or(_chunk_visible(s.shape, nblk, blk), jnp.logical_not(last)), s, MASK_VALUE)
                m = m_ref[...]
                m2 = jnp.maximum(m, jnp.max(s, axis=0, keepdims=True))
                alpha = jnp.exp(m - m2)
                p = jnp.exp(s - m2)
                l_ref[...] = alpha * l_ref[...] + jnp.sum(p, axis=0, keepdims=True)
                acc_ref[...] = alpha * acc_ref[...] + _dot(v_ref[rows, :], p, _TN)
                m_ref[...] = m2

            _key_runs(i + 1, nq, keys)
            l = l_ref[...]
            o_ref[q_rows, :] = (acc_ref[...] / l).T
            lse_ref[0, :, q_rows] = m_ref[...] + jnp.log(l)
            return 0

        lax.fori_loop(0, nq, qblock, 0)

    return pl.pallas_call(
        body, name="mla_attn_fwd", grid=(B, H),
        in_specs=[pl.BlockSpec((S, MLA_PAD), lambda b, h: (b, h)),
                  pl.BlockSpec((S, MLA_PAD), lambda b, h: (b, h)),
                  pl.BlockSpec((S, MLA_V), lambda b, h: (b, h))],
        out_specs=[pl.BlockSpec((S, MLA_V), lambda b, h: (b, h)),
                   pl.BlockSpec((1, 1, S), lambda b, h: (b * H + h, 0, 0))],
        out_shape=[jax.ShapeDtypeStruct((B * S, H * MLA_V), F32), jax.ShapeDtypeStruct((B * H, 1, S), F32)],
        scratch_shapes=[pltpu.VMEM((1, blk), F32), pltpu.VMEM((1, blk), F32), pltpu.VMEM((MLA_V, blk), F32)],
        compiler_params=_params(("parallel", "parallel")),
    )(q, k, v)


def _mla_attn_bwd(q, k, v, o, do, lse, B, S):
    blk = min(MLA_FWD_BLOCK, S)
    H, nq = MLA_HEADS, S // blk

    def body(q_ref, k_ref, v_ref, o_ref, do_ref, lse_ref, dq_ref, dk_ref, dv_ref, kt_ref, dqt_ref):
        dk_ref[...] = jnp.zeros(dk_ref.shape, F32)
        dv_ref[...] = jnp.zeros(dv_ref.shape, F32)
        for g in range(nq):
            kt_ref[g] = k_ref[g * blk:(g + 1) * blk, :].T

        def qblock(i, _):
            q_rows = pl.ds(pl.multiple_of(i * blk, blk), blk)
            qi = q_ref[q_rows, :]
            doi = do_ref[q_rows, :]
            delta = jnp.sum((doi * o_ref[q_rows, :]).T, axis=0, keepdims=True)
            lse_i = lse_ref[0, :, q_rows]
            doi = doi.astype(MXU_DTYPE)
            dqt_ref[...] = jnp.zeros(dqt_ref.shape, F32)

            def keys(first, nblk, last):
                rows = pl.ds(pl.multiple_of(first * blk, blk), nblk * blk)
                k_run, v_run = k_ref[rows, :], v_ref[rows, :]
                p = jnp.exp(_dot(k_run, qi, _NT) - lse_i)
                p = jnp.where(jnp.logical_or(_chunk_visible(p.shape, nblk, blk), jnp.logical_not(last)), p, 0.0)
                ds = (p * (_dot(v_run, doi, _NT) - delta)).astype(MXU_DTYPE)
                dk_ref[rows, :] += _dot(ds, qi, _NN)
                dv_ref[rows, :] += _dot(p, doi, _NN)
                for r in range(nblk):
                    dqt_ref[...] += _dot(kt_ref[first + r], ds[r * blk:(r + 1) * blk, :], _NN)

            _key_runs(i + 1, nq, keys)
            dq_ref[q_rows, :] = dqt_ref[...].T
            return 0

        lax.fori_loop(0, nq, qblock, 0)

    qk_spec = pl.BlockSpec((S, MLA_PAD), lambda b, h: (b, h))
    v_spec = pl.BlockSpec((S, MLA_V), lambda b, h: (b, h))
    return pl.pallas_call(
        body, name="mla_attn_bwd", grid=(B, H),
        in_specs=[qk_spec, qk_spec, v_spec, v_spec, v_spec,
                  pl.BlockSpec((1, 1, S), lambda b, h: (b * H + h, 0, 0))],
        out_specs=[qk_spec, qk_spec, v_spec],
        out_shape=[jax.ShapeDtypeStruct((B * S, H * MLA_PAD), F32), jax.ShapeDtypeStruct((B * S, H * MLA_PAD), F32),
                   jax.ShapeDtypeStruct((B * S, H * MLA_V), F32)],
        scratch_shapes=[pltpu.VMEM((nq, MLA_PAD, blk), q.dtype), pltpu.VMEM((MLA_PAD, blk), F32)],
        compiler_params=_params(("parallel", "parallel")),
    )(q, k, v, o, do, lse)


def _ret_log_gamma():
    lg = np.log1p(-np.exp2(RET_GAMMA_BASE - np.arange(RET_HEADS, dtype=np.float32))).astype(np.float32)
    return jnp.asarray(np.broadcast_to(lg[:, None, None], (RET_HEADS, 8, LANES)).copy())


RET_BLOCK = 512


def _ret_local_scale(lg, shape, blk, rising):
    local = lax.broadcasted_iota(jnp.int32, shape, 0) % blk
    return jnp.exp(lg * (local if rising else blk - 1 - local).astype(F32))


def _ret_pair_factor(lg, blk, steps):
    return jnp.exp(lg * (blk * (steps - 1) + 1).astype(F32))


def _ret_own_decay(lg, blk, transposed):
    a = lax.broadcasted_iota(jnp.int32, (blk, blk), 0)
    b = lax.broadcasted_iota(jnp.int32, (blk, blk), 1)
    query, key = (b, a) if transposed else (a, b)
    dec = jnp.exp(lg * jnp.abs(query - key).astype(F32))
    return jnp.where((key // CHUNK) <= (query // CHUNK), dec, 0.0)


def _ret_attn_fwd(q, k, v, B, S):
    blk = min(RET_BLOCK, S)
    H, nq = RET_HEADS, S // blk

    def body(lg_ref, q_ref, k_ref, v_ref, o_ref, ks_ref, dec_ref, acc_ref):
        lg = lg_ref[0, 0:1, 0:1]
        ks_ref[...] = (k_ref[...].astype(F32) * _ret_local_scale(lg, k_ref.shape, blk, False)).astype(ks_ref.dtype)
        dec_ref[...] = _ret_own_decay(lg, blk, False)

        def qblock(i, _):
            q_rows = pl.ds(pl.multiple_of(i * blk, blk), blk)
            qi = q_ref[q_rows, :]
            qs = (qi.astype(F32) * _ret_local_scale(lg, qi.shape, blk, True)).astype(qi.dtype)
            a = _dot(qi, k_ref[q_rows, :], _NT) * dec_ref[...]
            acc_ref[...] = _dot(a, v_ref[q_rows, :], _NN)

            def keys(first, nblk, _):
                rows = pl.ds(pl.multiple_of(first * blk, blk), nblk * blk)
                steps = i - first - lax.broadcasted_iota(jnp.int32, (1, nblk * blk), 1) // blk
                a = _dot(qs, ks_ref[rows, :], _NT) * _ret_pair_factor(lg, blk, steps)
                acc_ref[...] += _dot(a, v_ref[rows, :], _NN)

            _earlier_runs(i, nq, keys)
            o_ref[q_rows, :] = acc_ref[...]
            return 0

        lax.fori_loop(0, nq, qblock, 0)

    qk_spec = pl.BlockSpec((S, RET_QK), lambda b, h: (b, h))
    v_spec = pl.BlockSpec((S, RET_V), lambda b, h: (b, h))
    return pl.pallas_call(
        body, name="ret_attn_fwd", grid=(B, H),
        in_specs=[pl.BlockSpec((1, 8, LANES), lambda b, h: (h, 0, 0)), qk_spec, qk_spec, v_spec],
        out_specs=v_spec,
        out_shape=jax.ShapeDtypeStruct((B * S, H * RET_V), F32),
        scratch_shapes=[pltpu.VMEM((S, RET_QK), k.dtype), pltpu.VMEM((blk, blk), F32), pltpu.VMEM((blk, RET_V), F32)],
        compiler_params=_params(("parallel", "parallel")),
    )(_ret_log_gamma(), q, k, v)


def _ret_attn_bwd(q, k, v, do, B, S):
    blk = min(RET_BLOCK, S)
    H, nq = RET_HEADS, S // blk

    def body(lg_ref, q_ref, k_ref, v_ref, do_ref, dq_ref, dk_ref, dv_ref, ks_ref, kst_ref, dks_ref, dqt_ref, dec_ref):
        lg = lg_ref[0, 0:1, 0:1]
        dk_ref[...] = jnp.zeros(dk_ref.shape, F32)
        dv_ref[...] = jnp.zeros(dv_ref.shape, F32)
        dks_ref[...] = jnp.zeros(dks_ref.shape, F32)
        ks_ref[...] = (k_ref[...].astype(F32) * _ret_local_scale(lg, k_ref.shape, blk, False)).astype(ks_ref.dtype)
        for g in range(nq):
            kst_ref[g] = ks_ref[g * blk:(g + 1) * blk, :].T
        dec_ref[...] = _ret_own_decay(lg, blk, True)

        def qblock(i, _):
            q_rows = pl.ds(pl.multiple_of(i * blk, blk), blk)
            qi = q_ref[q_rows, :]
            q_scale = _ret_local_scale(lg, qi.shape, blk, True)
            qs = (qi.astype(F32) * q_scale).astype(qi.dtype)
            doi = do_ref[q_rows, :].astype(MXU_DTYPE)
            ki = k_ref[q_rows, :]
            dec = dec_ref[...]
            a = _dot(ki, qi, _NT) * dec
            da = (_dot(v_ref[q_rows, :], doi, _NT) * dec).astype(MXU_DTYPE)
            dv_ref[q_rows, :] += _dot(a, doi, _NN)
            dk_ref[q_rows, :] += _dot(da, qi, _NN)
            dq_own = _dot(da, ki, _TN)
            dqt_ref[...] = jnp.zeros(dqt_ref.shape, F32)

            def keys(first, nblk, _):
                for r in range(nblk):
                    g = first + r
                    rows = pl.ds(pl.multiple_of(g * blk, blk), blk)
                    c = _ret_pair_factor(lg, blk, i - g)
                    a = _dot(ks_ref[rows, :], qs, _NT) * c
                    da = (_dot(v_ref[rows, :], doi, _NT) * c).astype(MXU_DTYPE)
                    dv_ref[rows, :] += _dot(a, doi, _NN)
                    dks_ref[rows, :] += _dot(da, qs, _NN)
                    dqt_ref[...] += _dot(kst_ref[g], da, _NN)

            _earlier_runs(i, nq, keys)
            dq_ref[q_rows, :] = dqt_ref[...].T * q_scale + dq_own
            return 0

        lax.fori_loop(0, nq, qblock, 0)
        dk_ref[...] += dks_ref[...] * _ret_local_scale(lg, dks_ref.shape, blk, False)

    qk_spec = pl.BlockSpec((S, RET_QK), lambda b, h: (b, h))
    v_spec = pl.BlockSpec((S, RET_V), lambda b, h: (b, h))
    return pl.pallas_call(
        body, name="ret_attn_bwd", grid=(B, H),
        in_specs=[pl.BlockSpec((1, 8, LANES), lambda b, h: (h, 0, 0)), qk_spec, qk_spec, v_spec, v_spec],
        out_specs=[qk_spec, qk_spec, v_spec],
        out_shape=[jax.ShapeDtypeStruct((B * S, H * RET_QK), F32), jax.ShapeDtypeStruct((B * S, H * RET_QK), F32),
                   jax.ShapeDtypeStruct((B * S, H * RET_V), F32)],
        scratch_shapes=[pltpu.VMEM((S, RET_QK), k.dtype), pltpu.VMEM((nq, RET_QK, blk), k.dtype),
                        pltpu.VMEM((S, RET_QK), F32), pltpu.VMEM((RET_QK, blk), F32), pltpu.VMEM((blk, blk), F32)],
        compiler_params=_params(("parallel", "parallel")),
    )(_ret_log_gamma(), q, k, v, do)


def _loss_head(y, target, bm=512):
    T, D = y.shape
    bm = _pick(T, bm)

    def body(y_ref, t_ref, dy_ref, dyc_ref, l_ref):
        err = y_ref[...] - t_ref[...]
        dy_ref[...] = err / D
        dyc_ref[...] = (err / D).astype(dyc_ref.dtype)
        part = jnp.full((8, LANES), 0.5 * jnp.sum(jnp.mean(err * err, axis=-1)), F32)

        @pl.when(pl.program_id(0) == 0)
        def _():
            l_ref[...] = part

        @pl.when(pl.program_id(0) > 0)
        def _():
            l_ref[...] += part

    blk = pl.BlockSpec((bm, D), lambda i: (i, 0))
    dy, dyc, l = pl.pallas_call(
        body, name="loss_head", grid=(T // bm,),
        in_specs=[blk, blk], out_specs=[blk, blk, pl.BlockSpec((8, LANES), lambda i: (0, 0))],
        out_shape=[jax.ShapeDtypeStruct((T, D), F32), jax.ShapeDtypeStruct((T, D), BF16),
                   jax.ShapeDtypeStruct((8, LANES), F32)],
        compiler_params=_params(("arbitrary",)),
    )(y, target)
    return dy, dyc, l[0, 0]


def _adamw(w, g, m, v, name):
    R, C = w.shape
    br = R if R * C * 4 <= 2 ** 21 else _pick_rows(R, max(8, (2 ** 21) // (C * 4)))

    def body(w_ref, g_ref, m_ref, v_ref, d_ref, mo_ref, vo_ref):
        g_v = g_ref[...]
        m_v = ADAM_B1 * m_ref[...] + (1.0 - ADAM_B1) * g_v
        v_v = ADAM_B2 * v_ref[...] + (1.0 - ADAM_B2) * (g_v * g_v)
        m_hat = m_v / (1.0 - ADAM_B1 ** ADAM_STEP)
        v_hat = v_v / (1.0 - ADAM_B2 ** ADAM_STEP)
        d_ref[...] = -ADAM_LR * (m_hat / (jnp.sqrt(v_hat) + ADAM_EPS) + ADAM_WD * w_ref[...])
        mo_ref[...] = m_v
        vo_ref[...] = v_v

    blk = pl.BlockSpec((br, C), lambda i: (i, 0))
    return pl.pallas_call(
        body, name=name, grid=(R // br,),
        in_specs=[blk] * 4, out_specs=[blk] * 3,
        out_shape=[jax.ShapeDtypeStruct((R, C), F32)] * 3,
        compiler_params=_params(("parallel",)),
    )(w, g, m, v)


def _pick_rows(R, target):
    best = None
    for d in range(8, min(R, target) + 1, 8):
        if R % d == 0:
            best = d
    assert best is not None, (R, target)
    return best


def _position():
    return lax.axis_index("x"), lax.axis_index("y"), lax.axis_index("c")


HBM_SPEC = pl.BlockSpec(memory_space=pltpu.HBM)


def _other_chips(x, y):
    return [(1 - x, y), (x, 1 - y), (1 - x, 1 - y)]


def _all_gather_weights(bigs, small):
    nb = len(bigs)

    def body(*refs):
        big_refs, small_ref = refs[:nb], refs[nb]
        obig, osmall = refs[nb + 1:2 * nb + 1], refs[2 * nb + 1]
        ici_send, ici_recv, d2d_send, d2d_recv, sm_send, sm_recv = refs[2 * nb + 2:]
        x, y, c = _position()
        me = 2 * x + y
        chips = _other_chips(x, y)

        def rows(n, half):
            rh = bigs[n].shape[0] // 2
            return pl.ds(half * rh, rh)

        def over_ici(n, j, slot, from_shard):
            px, py = chips[j]
            dst = obig[n].at[slot, rows(n, c)]
            return pltpu.make_async_remote_copy(
                src_ref=big_refs[n].at[rows(n, c)] if from_shard else dst, dst_ref=dst,
                send_sem=ici_send.at[3 * n + j], recv_sem=ici_recv.at[3 * n + j],
                device_id=(px, py, c), device_id_type=MESH)

        def over_d2d(n, j, half):
            px, py = chips[j]
            part = obig[n].at[2 * px + py, rows(n, half)]
            return pltpu.make_async_remote_copy(
                src_ref=part, dst_ref=part, send_sem=d2d_send.at[3 * n + j], recv_sem=d2d_recv.at[3 * n + j],
                device_id=(x, y, 1 - c), device_id_type=MESH)

        def small_copy(j, slot):
            px, py = chips[j]
            return pltpu.make_async_remote_copy(
                src_ref=small_ref, dst_ref=osmall.at[slot], send_sem=sm_send.at[j], recv_sem=sm_recv.at[j],
                device_id=(px, py, c), device_id_type=MESH)

        sends = [over_ici(n, j, me, True) for n in range(nb) for j in range(3)]
        sends += [small_copy(j, me) for j in range(3)]
        for cp in sends:
            cp.start()
        passed = []
        for n in range(nb):
            for j, (px, py) in enumerate(chips):
                over_ici(n, j, 2 * px + py, False).wait_recv()
                fwd = over_d2d(n, j, c)
                fwd.start()
                passed.append(fwd)
        for n in range(nb):
            for j in range(3):
                over_d2d(n, j, 1 - c).wait_recv()
        for j, (px, py) in enumerate(chips):
            small_copy(j, 2 * px + py).wait_recv()
        for cp in sends + passed:
            cp.wait_send()

    dma = pltpu.SemaphoreType.DMA
    return pl.pallas_call(
        body, name="weights_all_gather",
        in_specs=[HBM_SPEC] * (nb + 1), out_specs=[HBM_SPEC] * (nb + 1),
        out_shape=[jax.ShapeDtypeStruct((N_SHARD,) + b.shape, b.dtype) for b in bigs]
        + [jax.ShapeDtypeStruct((N_SHARD,) + small.shape, small.dtype)],
        scratch_shapes=[dma((3 * nb,)), dma((3 * nb,)), dma((3 * nb,)), dma((3 * nb,)), dma((3,)), dma((3,))],
    )(*bigs, small)


SEM_SPEC = pl.BlockSpec(memory_space=pltpu.SEMAPHORE)
DATAFLOW_EFFECT = pltpu.SideEffectType.DATAFLOW_SIDE_EFFECTING
N_PEERS = N_DEV - 1


def _grad_copies(p_refs, land_refs, send_sems, recv_sems):
    x, y, c = _position()
    copies = []
    for a, (p_ref, land_ref) in enumerate(zip(p_refs, land_refs)):
        rh = p_ref.shape[1] // 2
        for k in range(1, N_DEV):
            px = 1 - x if k & 4 else x
            py = 1 - y if k & 2 else y
            pc = 1 - c if k & 1 else c
            copies.append(pltpu.make_async_remote_copy(
                src_ref=p_ref.at[2 * px + py, pl.ds(pc * rh, rh)], dst_ref=land_ref.at[k - 1],
                send_sem=send_sems.at[N_PEERS * a + k - 1], recv_sem=recv_sems.at[N_PEERS * a + k - 1],
                device_id=(px, py, pc), device_id_type=MESH))
    return copies


def _weight_copies(w_refs, land_refs, send_sems, recv_sems):
    x, y, c = _position()
    copies = []
    for a, (w_ref, land_ref) in enumerate(zip(w_refs, land_refs)):
        for j, (px, py) in enumerate(_other_chips(x, y)):
            copies.append(pltpu.make_async_remote_copy(
                src_ref=w_ref, dst_ref=land_ref.at[2 * x + y], send_sem=send_sems.at[3 * a + j],
                recv_sem=recv_sems.at[3 * a + j], device_id=(px, py, c), device_id_type=MESH))
    return copies


def _exchange_start(make_copies, srcs, lands, n_sems, name, after=None):
    n, m = len(srcs), len(lands)
    n_in = n + m + (after is not None)

    def body(*refs):
        send_sems, recv_sems, token = refs[n_in], refs[n_in + 1], refs[-1]
        for cp in make_copies(refs[:n], refs[n:n + m], send_sems, recv_sems):
            cp.start()
        token[...] = jnp.zeros(token.shape, token.dtype)

    hbm = lambda a: pltpu.with_memory_space_constraint(a, pltpu.HBM)
    dma = pltpu.SemaphoreType.DMA
    res = pl.pallas_call(
        body, name=name,
        in_specs=[HBM_SPEC] * (n + m) + ([] if after is None else [pl.BlockSpec(memory_space=pl.ANY)]),
        out_specs=[SEM_SPEC, SEM_SPEC] + [HBM_SPEC] * (n + m) + [pl.BlockSpec(memory_space=pltpu.VMEM)],
        out_shape=[dma((n_sems,)), dma((n_sems,))] + [pltpu.HBM(a.shape, a.dtype) for a in list(srcs) + list(lands)]
        + [jax.ShapeDtypeStruct((8, LANES), F32)],
        input_output_aliases={i: 2 + i for i in range(n + m)},
        compiler_params=pltpu.CompilerParams(has_side_effects=DATAFLOW_EFFECT),
    )(*[hbm(a) for a in srcs], *[hbm(a) for a in lands], *(() if after is None else (after,)))
    return res[0], res[1], list(res[2:2 + n]), list(res[2 + n:2 + n + m]), res[-1]


def _exchange_wait(make_copies, send_sems, recv_sems, srcs, lands, after, name):
    n, m = len(srcs), len(lands)

    def body(*refs):
        for cp in make_copies(refs[:n], refs[n:n + m], refs[n + m], refs[n + m + 1]):
            cp.wait_send()
            cp.wait_recv()

    res = pl.pallas_call(
        body, name=name,
        in_specs=[HBM_SPEC] * (n + m) + [SEM_SPEC, SEM_SPEC, pl.BlockSpec(memory_space=pl.ANY)],
        out_specs=[HBM_SPEC] * (n + m),
        out_shape=[pltpu.HBM(a.shape, a.dtype) for a in list(srcs) + list(lands)],
        input_output_aliases={i: i for i in range(n + m)},
        compiler_params=pltpu.CompilerParams(has_side_effects=DATAFLOW_EFFECT),
    )(*srcs, *lands, send_sems, recv_sems, after)
    return list(res[:n]), list(res[n:])


def _sum_partials(p, land, name):
    _, rh, cols = land.shape
    br = _pick_rows(rh, 256)
    nrb = rh // br
    x, y, c = _position()
    where = jnp.stack([2 * x + y, c]).astype(jnp.int32)

    def body(where_ref, p_ref, land_ref, o_ref):
        acc = p_ref[...].astype(F32)
        for k in range(N_PEERS):
            acc = acc + land_ref[k].astype(F32)
        o_ref[...] = acc

    return pl.pallas_call(
        body, name=name,
        grid_spec=pltpu.PrefetchScalarGridSpec(
            num_scalar_prefetch=1, grid=(nrb,),
            in_specs=[pl.BlockSpec((None, br, cols), lambda r, where_ref: (where_ref[0], where_ref[1] * nrb + r, 0)),
                      pl.BlockSpec((N_PEERS, br, cols), lambda r, where_ref: (0, r, 0))],
            out_specs=pl.BlockSpec((None, br, cols), lambda r, where_ref: (where_ref[1], r, 0))),
        out_shape=jax.ShapeDtypeStruct((2, rh, cols), F32),
        compiler_params=_params(("parallel",)),
    )(where, p, land)


def _sibling_share(fulls, name):
    n = len(fulls)

    def body(*refs):
        o_refs = refs[n:2 * n]
        send_sems, recv_sems = refs[2 * n:]
        x, y, c = _position()

        def copy(a, half):
            return pltpu.make_async_remote_copy(
                src_ref=o_refs[a].at[half], dst_ref=o_refs[a].at[half], send_sem=send_sems.at[a],
                recv_sem=recv_sems.at[a], device_id=(x, y, 1 - c), device_id_type=MESH)

        sends = [copy(a, c) for a in range(n)]
        for cp in sends:
            cp.start()
        for a in range(n):
            copy(a, 1 - c).wait_recv()
        for cp in sends:
            cp.wait_send()

    dma = pltpu.SemaphoreType.DMA
    return pl.pallas_call(
        body, name=name,
        in_specs=[HBM_SPEC] * n, out_specs=[HBM_SPEC] * n,
        out_shape=[jax.ShapeDtypeStruct(f.shape, f.dtype) for f in fulls],
        input_output_aliases={a: a for a in range(n)},
        scratch_shapes=[dma((n,)), dma((n,))],
    )(*fulls)


def _all_reduce_small(v):
    R, cols = v.shape

    def body(v_ref, o_ref, buf_ref, send_sems, recv_sems):
        x, y, c = _position()
        me = 4 * x + 2 * y + c
        buf_ref[me] = v_ref[...]
        sends = []
        for k in range(1, N_DEV):
            px = 1 - x if k & 4 else x
            py = 1 - y if k & 2 else y
            pc = 1 - c if k & 1 else c
            sends.append(pltpu.make_async_remote_copy(
                src_ref=v_ref, dst_ref=buf_ref.at[me], send_sem=send_sems.at[k - 1], recv_sem=recv_sems.at[k - 1],
                device_id=(px, py, pc), device_id_type=MESH))
        for cp in sends:
            cp.start()
        for k in range(1, N_DEV):
            px = 1 - x if k & 4 else x
            py = 1 - y if k & 2 else y
            pc = 1 - c if k & 1 else c
            pltpu.make_async_remote_copy(
                src_ref=v_ref, dst_ref=buf_ref.at[4 * px + 2 * py + pc], send_sem=send_sems.at[k - 1],
                recv_sem=recv_sems.at[k - 1], device_id=(px, py, pc), device_id_type=MESH).wait_recv()
        for cp in sends:
            cp.wait_send()
        acc = buf_ref[0]
        for d in range(1, N_DEV):
            acc = acc + buf_ref[d]
        o_ref[...] = acc

    return pl.pallas_call(
        body, name="small_grads_all_reduce",
        in_specs=[pl.BlockSpec(memory_space=pltpu.VMEM)], out_specs=pl.BlockSpec(memory_space=pltpu.VMEM),
        out_shape=jax.ShapeDtypeStruct((R, cols), F32),
        scratch_shapes=[pltpu.VMEM((N_DEV, R, cols), F32), pltpu.SemaphoreType.DMA((N_DEV - 1,)),
                        pltpu.SemaphoreType.DMA((N_DEV - 1,))],
    )(v)


def _rope_tables(S, half, width):
    inv_freq = ROPE_THETA ** (-jnp.arange(half, dtype=F32) / half)
    ang = jnp.arange(S).astype(F32)[:, None] * inv_freq[None, :]
    return jnp.cos(ang), jnp.sin(ang)


def _slot_rows(a):
    return a.reshape(N_SHARD, -1, a.shape[-1])


def _local_step(x, target, w, B, S, late, exchange, reduce_small):
    T = B * S
    D = D_MODEL
    bm = 256
    full = lambda a, wd, tile=None: (a, wd, 0, tile or wd)
    g = {}

    cos_r, sin_r = _rope_tables(S, RET_QK // 2, LANES)
    cos_m, sin_m = _rope_tables(S, MLA_ROPE // 2, LANES)
    zeros64 = jnp.zeros((S, 64), F32)
    cos_m = jnp.concatenate([cos_m, cos_m, zeros64], axis=1)
    sin_m = jnp.concatenate([-sin_m, sin_m, zeros64], axis=1)

    def ffn_fwd(xin, i):
        w.update(late(f"ffn{i}", xin))
        norm = w["ffn_norm"][i:i + 1]
        h, ht = _rowwise_fwd(_fn_rms, f"ffn{i}_norm", [full(xin, D)], [], [(norm, D)], [(D, D, BF16)], bm, S,
                             transposed=(0,))
        ag = _mm(h, w[f"ffn_w_in{i}"], "nn", BF16, f"ffn{i}_in", bn=1408, cols_outer=True)
        u, ut = _conv_fwd(ag, w["ffn_conv8"][i], B, S, f"ffn{i}_conv")
        xout = _mm(u, w[f"ffn_w_out{i}"], "nn", F32, f"ffn{i}_out", residual=xin, bk=FFN_DIM)
        return xout, (xin, norm, ht, ag, ut)

    def ffn_bwd(dxout, dxout_c, saved, i):
        xin, norm, ht, ag, ut = saved
        du = _mm(dxout_c, w[f"ffn_w_out{i}"], "nt", F32, f"ffn{i}_out_dx", bn=1408, cols_outer=True)
        g_w_out = _mm(ut, dxout_c, "nn", BF16, f"ffn{i}_out_dw", bm=1408, bn=512, bk=T)
        da, dg, dw8 = _conv_bwd(ag, w["ffn_conv8"][i], du, B, S, f"ffn{i}_conv_bwd")
        g_w_in = _mm(ht, [da, dg], "nn", BF16, f"ffn{i}_in_dw", bm=1024, bn=1408, bk=T // 2, out_slots=N_SHARD)
        token = exchange(f"ffn{i}", [g_w_in, _slot_rows(g_w_out)])
        dxin, dxin_c, g_norm = _mm_dx_norm([da, dg], w[f"ffn_w_in{i}"], xin, norm, dxout, f"ffn{i}_in_dx", after=token)
        return dxin, dxin_c, (g_norm, dw8)

    h0, h0t = _rowwise_fwd(_fn_rms, "ret_norm", [full(x, D)], [], [(w["ret_norm"], D)], [(D, D, BF16)], bm, S,
                           transposed=(0,))
    proj = _mm(h0, w["ret_w_in"], "nn", BF16, "ret_in", after=w["started"], cols_outer=True)
    HQ, HV = RET_HEADS * RET_QK, RET_HEADS * RET_V
    rope_rows = [(proj, 2 * HQ + HV, 0, LANES)]
    q_r, k_r, v_r = _rowwise_fwd(_fn_ret_rope, "ret_rope", rope_rows, [cos_r, sin_r], [],
                                 [(HQ, LANES, BF16), (HQ, LANES, BF16), (HV, LANES, BF16)], bm, S)
    ret_o = _ret_attn_fwd(q_r, k_r, v_r, B, S)
    gate_rows = [full(ret_o, HV, RET_V), (proj, HV, 2, RET_V)]
    y0, y0t = _rowwise_fwd(_fn_ret_gate, "ret_gate", gate_rows, [], [(w["ret_gn"], RET_V)], [(HV, RET_V, BF16)], 128, S,
                           transposed=(0,))
    w.update(late("ret_out", y0))
    x1 = _mm(y0, w["ret_w_out"], "nn", F32, "ret_out", residual=x)
    x2, ffn0_saved = ffn_fwd(x1, 0)

    w.update(late("mla", x2))
    (h2,) = _rowwise_fwd(_fn_rms, "mla_norm", [full(x2, D)], [], [(w["mla_norm"], D)], [(D, D, BF16)], bm, S)
    proj2 = _mm(h2, w["mla_w_in"], "nn", F32, "mla_in", bm=2048)
    lat_consts = [(w["mla_q_norm"], LANES), (w["mla_kv_norm"], LANES)]
    cqn, ckvn, kr = _rowwise_fwd(_fn_mla_lat, "mla_latent_norm", [full(proj2, MLA_IN_PAD, LANES)], [], lat_consts,
                                 [(MLA_Q_RANK, LANES, BF16), (MLA_KV_RANK, LANES, BF16), (LANES, LANES, F32)], bm, S)
    qf = _mm(cqn, w["mla_w_qb"], "nn", BF16, "mla_qb", bm=2048, bn=2048)
    kvf = _mm(ckvn, w["mla_w_kvb"], "nn", BF16, "mla_kvb", bm=2048, bn=2048)
    HP, HVm = MLA_HEADS * MLA_PAD, MLA_HEADS * MLA_V
    head_rows = [full(qf, HP, LANES), full(kvf, HP, LANES), full(kr, LANES)]
    head_consts = [(w["mla_q_head_norm"], LANES), (w["mla_k_head_norm"], LANES)]
    q_a, k_a, v_a = _rowwise_fwd(_fn_mla_heads, "mla_heads", head_rows, [cos_m, sin_m], head_consts,
                                 [(HP, LANES, BF16), (HP, LANES, BF16), (HVm, LANES, BF16)], bm, S)
    att_o, lse = _mla_attn_fwd(q_a, k_a, v_a, B, S)
    x3 = _mm(att_o, w["mla_w_out"], "nn", F32, "mla_out", residual=x2, bm=1024)
    x4, ffn1_saved = ffn_fwd(x3, 1)

    dy, dy_c, loss = _loss_head(x4, target)

    dx3, dx3_c, (g_n1, dw8_1) = ffn_bwd(dy, dy_c, ffn1_saved, 1)

    d_att_o = _mm(dx3_c, w["mla_w_out"], "nt", F32, "mla_out_dx", bm=2048)
    g_mla_out = _mm(att_o, dx3_c, "tn", BF16, "mla_out_dw")
    dq_a, dk_a, dv_a = _mla_attn_bwd(q_a, k_a, v_a, att_o, d_att_o, lse, B, S)
    (dqf, dkvf, dkr), (g["mla_q_head_norm"], g["mla_k_head_norm"]) = _rowwise_bwd(
        _fn_mla_heads, "mla_heads_bwd", head_rows, [cos_m, sin_m], head_consts,
        [(dq_a, LANES), (dk_a, LANES), (dv_a, LANES)], 128, S, grad_dtypes=[BF16, BF16, F32])
    dcqn = _mm(dqf, w["mla_w_qb"], "nt", F32, "mla_qb_dx", bm=2048)
    g_qb = _mm(cqn, dqf, "tn", BF16, "mla_qb_dw")
    g_qb = _to_slots(_unpad_heads(g_qb, 1), 1).reshape(N_SHARD, MLA_Q_RANK, -1)
    dckvn = _mm(dkvf, w["mla_w_kvb"], "nt", F32, "mla_kvb_dx", bm=2048)
    g_kvb = _mm(ckvn, dkvf, "tn", BF16, "mla_kvb_dw", bn=512, out_slots=N_SHARD)
    (dproj2,), (g["mla_q_norm"], g["mla_kv_norm"]) = _rowwise_bwd(
        _fn_mla_lat, "mla_latent_norm_bwd", [full(proj2, MLA_IN_PAD, LANES)], [], lat_consts,
        [(dcqn, LANES), (dckvn, LANES), (dkr, LANES)], bm, S, grad_dtypes=[BF16])
    g_mla_in = _mm(h2, dproj2, "tn", BF16, "mla_in_dw")
    token = exchange("mla", [_slot_rows(g_mla_in[:, :MLA_IN]), g_qb, g_kvb, _slot_rows(g_mla_out)])
    dh2 = _mm(dproj2, w["mla_w_in"], "nt", F32, "mla_in_dx", after=token, bm=2048)
    (dx2, dx2_c), (g["mla_norm"],) = _rowwise_bwd(_fn_rms, "mla_norm_bwd", [full(x2, D)], [], [(w["mla_norm"], D)],
                                                  [(dh2, D)], bm, S, adds={0: dx3}, mxu_copies=(0,))

    dx1, dx1_c, (g_n0, dw8_0) = ffn_bwd(dx2, dx2_c, ffn0_saved, 0)

    dy0 = _mm(dx1_c, w["ret_w_out"], "nt", F32, "ret_out_dx")
    g_ret_out = _mm(y0t, dx1_c, "nn", BF16, "ret_out_dw", bm=1024, bn=512, bk=T)
    (d_ret_o, dgate), (g["ret_gn"],) = _rowwise_bwd(_fn_ret_gate, "ret_gate_bwd", gate_rows, [], [(w["ret_gn"], RET_V)],
                                                    [(dy0, RET_V)], 128, S, grad_dtypes=[F32, BF16])
    dq_r, dk_r, dv_r = _ret_attn_bwd(q_r, k_r, v_r, d_ret_o, B, S)
    (dqkv,), _ = _rowwise_bwd(_fn_ret_rope, "ret_rope_bwd", rope_rows, [cos_r, sin_r], [],
                              [(dq_r, LANES), (dk_r, LANES), (dv_r, LANES)], bm, S, grad_dtypes=[BF16], linear=True)
    dx, _, g["ret_norm"] = _mm_dx_norm([dqkv, dgate], w["ret_w_in"], x, w["ret_norm"], dx1, "ret_in_dx")
    g["ffn_norm"] = jnp.concatenate([g_n0, g_n1], axis=0)
    g["ffn_conv_w"] = jnp.stack([dw8_0[0:3], dw8_1[0:3]])
    g["ffn_conv_b"] = jnp.stack([dw8_0[3], dw8_1[3]])
    reduced_small = reduce_small(g)
    g_ret_in = _mm(h0t, [dqkv, dgate], "nn", BF16, "ret_in_dw", bn=512, bk=T, out_slots=N_SHARD, after=reduced_small)
    exchange("ret", [g_ret_in, _slot_rows(g_ret_out)])
    return loss, dx, reduced_small


_BIG = [("ret_w_in", 2), ("ret_w_out", 1), ("mla_w_in", 1), ("mla_w_qb", 2), ("mla_w_kvb", 2), ("mla_w_out", 1),
        ("ffn_w_in", 2), ("ffn_w_out", 1)]
_SMALL_SHARDED = [("ret_gn", 2), ("mla_norm", 1), ("mla_q_norm", 1), ("mla_kv_norm", 1), ("ffn_conv_w", 2)]
_SMALL_REPLICATED = ["ret_norm", "mla_q_head_norm", "mla_k_head_norm", "ffn_norm", "ffn_conv_b"]
_SMALL_ALL = ["ret_norm", "ret_gn", "mla_norm", "mla_q_norm", "mla_kv_norm", "mla_q_head_norm", "mla_k_head_norm",
              "ffn_norm", "ffn_conv_w", "ffn_conv_b"]


def _to_slots(full, axis):
    shape = full.shape
    split = shape[:axis] + (N_SHARD, shape[axis] // N_SHARD) + shape[axis + 1:]
    return jnp.moveaxis(full.reshape(split), axis, 0).reshape(N_SHARD, -1)


def _from_slots(slots, shard_shape, axis):
    parts = jnp.moveaxis(slots.reshape((N_SHARD,) + tuple(shard_shape)), 0, axis)
    full = shard_shape[:axis] + (N_SHARD * shard_shape[axis],) + shard_shape[axis + 1:]
    return parts.reshape(full)


def _pad_rows(flat, cols, row_unit):
    n, L = flat.shape
    unit = cols * row_unit
    Lp = -(-L // unit) * unit
    if Lp != L:
        flat = jnp.concatenate([flat, jnp.zeros((n, Lp - L), flat.dtype)], axis=1)
    return flat.reshape(n, Lp // cols, cols)


def _pad_heads(a, axis):
    shape = a.shape
    a = a.reshape(shape[:axis] + (MLA_HEADS, MLA_QK) + shape[axis + 1:])
    pad = [(0, 0)] * a.ndim
    pad[axis + 1] = (0, MLA_PAD - MLA_QK)
    return jnp.pad(a, pad).reshape(shape[:axis] + (MLA_HEADS * MLA_PAD,) + shape[axis + 1:])


def _unpad_heads(a, axis):
    shape = a.shape
    a = a.reshape(shape[:axis] + (MLA_HEADS, MLA_PAD) + shape[axis + 1:])
    a = lax.slice_in_dim(a, 0, MLA_QK, axis=axis + 1)
    return a.reshape(shape[:axis] + (MLA_HEADS * MLA_QK,) + shape[axis + 1:])


def kernel(x, ret_norm, ret_w_in, ret_gn, ret_w_out, mla_norm, mla_w_in, mla_q_norm, mla_w_qb, mla_kv_norm, mla_w_kvb, mla_q_head_norm, mla_k_head_norm, mla_w_out, ffn_norm, ffn_w_in, ffn_conv_w, ffn_conv_b, ffn_w_out, loss_target, m_ret_norm, m_ret_w_in, m_ret_gn, m_ret_w_out, m_mla_norm, m_mla_w_in, m_mla_q_norm, m_mla_w_qb, m_mla_kv_norm, m_mla_w_kvb, m_mla_q_head_norm, m_mla_k_head_norm, m_mla_w_out, m_ffn_norm, m_ffn_w_in, m_ffn_conv_w, m_ffn_conv_b, m_ffn_w_out, v_ret_norm, v_ret_w_in, v_ret_gn, v_ret_w_out, v_mla_norm, v_mla_w_in, v_mla_q_norm, v_mla_w_qb, v_mla_kv_norm, v_mla_w_kvb, v_mla_q_head_norm, v_mla_k_head_norm, v_mla_w_out, v_ffn_norm, v_ffn_w_in, v_ffn_conv_w, v_ffn_conv_b, v_ffn_w_out):
    names = ["ret_norm", "ret_w_in", "ret_gn", "ret_w_out", "mla_norm", "mla_w_in", "mla_q_norm", "mla_w_qb",
             "mla_kv_norm", "mla_w_kvb", "mla_q_head_norm", "mla_k_head_norm", "mla_w_out", "ffn_norm", "ffn_w_in",
             "ffn_conv_w", "ffn_conv_b", "ffn_w_out"]
    shard = dict(zip(names, [ret_norm, ret_w_in, ret_gn, ret_w_out, mla_norm, mla_w_in, mla_q_norm, mla_w_qb,
                             mla_kv_norm, mla_w_kvb, mla_q_head_norm, mla_k_head_norm, mla_w_out, ffn_norm, ffn_w_in,
                             ffn_conv_w, ffn_conv_b, ffn_w_out]))
    mom_m = dict(zip(names, [m_ret_norm, m_ret_w_in, m_ret_gn, m_ret_w_out, m_mla_norm, m_mla_w_in, m_mla_q_norm,
                             m_mla_w_qb, m_mla_kv_norm, m_mla_w_kvb, m_mla_q_head_norm, m_mla_k_head_norm, m_mla_w_out,
                             m_ffn_norm, m_ffn_w_in, m_ffn_conv_w, m_ffn_conv_b, m_ffn_w_out]))
    mom_v = dict(zip(names, [v_ret_norm, v_ret_w_in, v_ret_gn, v_ret_w_out, v_mla_norm, v_mla_w_in, v_mla_q_norm,
                             v_mla_w_qb, v_mla_kv_norm, v_mla_w_kvb, v_mla_q_head_norm, v_mla_k_head_norm, v_mla_w_out,
                             v_ffn_norm, v_ffn_w_in, v_ffn_conv_w, v_ffn_conv_b, v_ffn_w_out]))
    B, S, D = x.shape
    T = B * S
    sx, sy = lax.axis_index("x"), lax.axis_index("y")
    me = 2 * sx + sy

    two_d = lambda a: a.reshape(-1, a.shape[-1])
    small_sizes = [int(np.prod(shard[n].shape)) for n, _ in _SMALL_SHARDED]
    small = jnp.concatenate([shard[n].reshape(1, -1) for n, _ in _SMALL_SHARDED], axis=1)
    small = _pad_rows(small, LANES, 8)[0]
    as_mxu = lambda a: two_d(a).astype(BF16)
    is_me = lax.broadcasted_iota(jnp.int32, (N_SHARD, 1, 1), 0) == me
    with_own = lambda gathered, own: jnp.where(is_me, own[None], gathered)
    by_cols = lambda a: jnp.moveaxis(a, 0, 1).reshape(a.shape[1], -1)
    by_rows = lambda a: a.reshape(-1, a.shape[-1])
    pad_in = lambda a: jnp.pad(by_rows(a), ((0, 0), (0, MLA_IN_PAD - MLA_IN)))
    pad_qb = lambda a: _pad_heads(by_cols(a), 1)
    ret_in_shard = as_mxu(shard["ret_w_in"])
    g_ret_in, gsmall = _all_gather_weights([ret_in_shard], small)
    later = [
        ("ret_out", [("ret_w_out", as_mxu(shard["ret_w_out"]), by_rows)]),
        ("ffn0", [("ffn_w_in0", as_mxu(shard["ffn_w_in"][0]), by_cols), ("ffn_w_out0", as_mxu(shard["ffn_w_out"][0]), by_rows)]),
        ("mla", [("mla_w_in", as_mxu(shard["mla_w_in"]), pad_in), ("mla_w_qb", as_mxu(shard["mla_w_qb"]), pad_qb),
                 ("mla_w_kvb", as_mxu(shard["mla_w_kvb"]), by_cols), ("mla_w_out", as_mxu(shard["mla_w_out"]), by_rows)]),
        ("ffn1", [("ffn_w_in1", as_mxu(shard["ffn_w_in"][1]), by_cols), ("ffn_w_out1", as_mxu(shard["ffn_w_out"][1]), by_rows)]),
    ]
    gathering = {}
    token = gsmall
    for group, items in later:
        shards = [s_ for _, s_, _ in items]
        lands = [lax.empty((N_SHARD,) + s_.shape, s_.dtype) for s_ in shards]
        send_sems, recv_sems, shards, lands, token = _exchange_start(
            _weight_copies, shards, lands, 3 * len(shards), f"weights_start_{group}", after=token)
        gathering[group] = (send_sems, recv_sems, shards, lands, items)

    def late(group, after):
        send_sems, recv_sems, shards, lands, items = gathering[group]
        shards, lands = _exchange_wait(_weight_copies, send_sems, recv_sems, shards, lands, after,
                                       f"weights_wait_{group}")
        return {key: full(with_own(l_, s_)) for (key, _, full), s_, l_ in zip(items, shards, lands)}

    gsmall = with_own(gsmall, small).reshape(N_SHARD, -1)
    wfull = {}
    off = 0
    for (n, ax), sz in zip(_SMALL_SHARDED, small_sizes):
        wfull[n] = _from_slots(gsmall[:, off:off + sz], shard[n].shape, ax)
        off += sz
    for n in _SMALL_REPLICATED:
        wfull[n] = shard[n]

    conv8 = jnp.concatenate([wfull["ffn_conv_w"], wfull["ffn_conv_b"][:, None, :],
                             jnp.zeros((2, 4, FFN_DIM), F32)], axis=1)
    w = {
        "started": token, "ret_norm": wfull["ret_norm"], "ret_w_in": by_cols(with_own(g_ret_in, ret_in_shard)),
        "ret_gn": wfull["ret_gn"].reshape(1, RET_HEADS * RET_V), "mla_norm": wfull["mla_norm"],
        "mla_q_norm": wfull["mla_q_norm"], "mla_kv_norm": wfull["mla_kv_norm"],
        "mla_q_head_norm": jnp.pad(wfull["mla_q_head_norm"], ((0, 0), (0, MLA_PAD - MLA_QK))),
        "mla_k_head_norm": jnp.pad(wfull["mla_k_head_norm"], ((0, 0), (0, MLA_PAD - MLA_QK))),
        "ffn_norm": wfull["ffn_norm"], "ffn_conv8": conv8,
    }

    started = {}

    def exchange(group, arrays):
        lands = [lax.empty((N_PEERS, p.shape[1] // 2, p.shape[2]), p.dtype) for p in arrays]
        send_sems, recv_sems, ps, lands, token = _exchange_start(
            _grad_copies, arrays, lands, N_PEERS * len(arrays), f"grads_start_{group}")
        started[group] = (send_sems, recv_sems, ps, lands)
        return token

    small_shapes = {
        "ret_norm": (1, D_MODEL), "ret_gn": (1, RET_HEADS, RET_V), "mla_norm": (1, D_MODEL),
        "mla_q_norm": (1, MLA_Q_RANK), "mla_kv_norm": (1, MLA_KV_RANK), "mla_q_head_norm": (1, MLA_QK),
        "mla_k_head_norm": (1, MLA_QK), "ffn_norm": (2, D_MODEL), "ffn_conv_w": (2, 3, FFN_DIM),
        "ffn_conv_b": (2, FFN_DIM)}

    def reduce_small(gl):
        gl = dict(gl, mla_q_head_norm=gl["mla_q_head_norm"][:, :MLA_QK], mla_k_head_norm=gl["mla_k_head_norm"][:, :MLA_QK])
        packed = jnp.concatenate([gl[n].reshape(1, -1) for n in _SMALL_ALL], axis=1)
        return _all_reduce_small(_pad_rows(packed, LANES, 8)[0])

    loss_part, dx, gsm = _local_step(x.reshape(T, D), loss_target.reshape(T, D), w, B, S, late, exchange,
                                     reduce_small)
    loss = lax.psum(loss_part, ("x", "y", "c"))

    delta, new_m, new_v, grads = {}, {}, {}, {}

    def reduced(group, after):
        send_sems, recv_sems, ps, lands = started[group]
        ps, lands = _exchange_wait(_grad_copies, send_sems, recv_sems, ps, lands, after, f"grads_wait_{group}")
        halves = [_sum_partials(p_, l_, f"grads_sum_{group}_{i}") for i, (p_, l_) in enumerate(zip(ps, lands))]
        return [two_d(r) for r in _sibling_share(halves, f"grads_share_{group}")]

    def adamw(n, g_):
        shp = shard[n].shape
        grads[n] = g_.reshape(shp)
        flat = lambda a: a.reshape(-1, shp[-1])
        d_, m_, v_ = _adamw(flat(shard[n]), flat(grads[n]), flat(mom_m[n]), flat(mom_v[n]), f"adamw_{n}")
        delta[n], new_m[n], new_v[n] = d_.reshape(shp), m_.reshape(shp), v_.reshape(shp)
        return d_

    ffn1 = reduced("ffn1", started["ret"][2][0])
    mla = reduced("mla", ffn1[0])
    ffn0 = reduced("ffn0", mla[0])
    early = [adamw(n, g_) for n, g_ in zip(["mla_w_in", "mla_w_qb", "mla_w_kvb", "mla_w_out"], mla)]
    early.append(adamw("ffn_w_in", jnp.stack([ffn0[0], ffn1[0]])))
    early.append(adamw("ffn_w_out", jnp.stack([ffn0[1], ffn1[1]])))
    ret = reduced("ret", jnp.stack([d_[0, 0] for d_ in early]))
    adamw("ret_w_in", ret[0])
    adamw("ret_w_out", ret[1])

    gsm = gsm.reshape(-1)
    sharded_axis = dict(_SMALL_SHARDED)
    off = 0
    for n in _SMALL_ALL:
        sz = int(np.prod(small_shapes[n]))
        gn = gsm[off:off + sz].reshape(small_shapes[n])
        off += sz
        if n in sharded_axis:
            ax = sharded_axis[n]
            width = shard[n].shape[ax]
            gn = lax.dynamic_slice_in_dim(gn, me * width, width, axis=ax)
        grads[n] = gn

    pack_small = lambda d: _pad_rows(jnp.concatenate([d[n].reshape(1, -1) for n in _SMALL_ALL], axis=1), LANES, 8)[0]
    d_, m_, v_ = _adamw(pack_small(shard), pack_small(grads), pack_small(mom_m), pack_small(mom_v), "adamw_small")
    off = 0
    for n in _SMALL_ALL:
        sz = int(np.prod(shard[n].shape))
        for dst, src in ((delta, d_), (new_m, m_), (new_v, v_)):
            dst[n] = src.reshape(-1)[off:off + sz].reshape(shard[n].shape)
        off += sz

    return (loss, dx.reshape(B, S, D), *[grads[n] for n in names], *[delta[n] for n in names],
            *[new_m[n] for n in names], *[new_v[n] for n in names])
```

```python
import functools
import math

import numpy as np
import jax
import jax.numpy as jnp
from jax import lax
from jax.experimental import pallas as pl
from jax.experimental.pallas import tpu as pltpu

F32 = jnp.float32
BF16 = jnp.bfloat16
MXU_DTYPE = jnp.bfloat16

CHUNK = 64
RMS_EPS = 1e-6
ROPE_THETA = 10000.0
D_MODEL = 1024
RET_HEADS = 4
RET_QK = 256
RET_V = 512
RET_GAMMA_BASE = -5.0
MLA_HEADS = 8
MLA_Q_RANK = 384
MLA_KV_RANK = 256
MLA_NOPE = 128
MLA_ROPE = 64
MLA_V = 128
MLA_QK = MLA_NOPE + MLA_ROPE
MLA_PAD = 256
MLA_IN = MLA_Q_RANK + MLA_KV_RANK + MLA_ROPE
MLA_IN_PAD = MLA_IN + 64
MASK_VALUE = -1e30
FFN_DIM = 2816
ADAM_LR = 0.001
ADAM_B1 = 0.9
ADAM_B2 = 0.999
ADAM_EPS = 1e-08
ADAM_WD = 0.01
ADAM_STEP = 10

LANES = 128
ATT_BLOCK = 256
MLA_FWD_BLOCK = 512
VMEM_LIMIT = 56 * 2 ** 20
N_SHARD = 4
N_DEV = 8

MESH = pl.DeviceIdType.MESH


def _params(sem=None, **kw):
    return pltpu.CompilerParams(dimension_semantics=sem, vmem_limit_bytes=VMEM_LIMIT, **kw)


def _pick(dim, target):
    if dim <= target:
        return dim
    best = None
    for d in range(LANES, target + 1, LANES):
        if dim % d == 0:
            best = d
    assert best is not None, (dim, target)
    return best


def _mm(a, b, dims, out_dtype, name, residual=None, bm=512, bn=1024, bk=2048, out_slots=None, after=None,
        cols_outer=False):
    a_parts = list(a) if isinstance(a, (list, tuple)) else [a]
    b_parts = list(b) if isinstance(b, (list, tuple)) else [b]
    parts_on_n = dims == "tn" or len(b_parts) > 1
    if parts_on_n:
        assert len(a_parts) == 1 and dims in ("tn", "nn")
        (K, M) = a_parts[0].shape if dims == "tn" else a_parts[0].shape[::-1]
        N = sum(p.shape[1] for p in b_parts)
        part_widths = [p.shape[1] for p in b_parts]
    else:
        assert len(b_parts) == 1
        M = a_parts[0].shape[0]
        K = sum(p.shape[1] for p in a_parts)
        N = b_parts[0].shape[1 if dims == "nn" else 0]
        part_widths = [p.shape[1] for p in a_parts]
    bm, bn, bk = _pick(M, bm), _pick(N, bn), _pick(K, min(bk, 1024) if dims == "tn" else bk)
    nk = K // bk
    unit = bn if parts_on_n else bk
    assert all(wd % unit == 0 for wd in part_widths), (name, part_widths, unit)
    bounds = np.cumsum([0] + [wd // unit for wd in part_widths])
    ranges = [(int(lo), int(hi)) for lo, hi in zip(bounds[:-1], bounds[1:])]

    def part_index(idx, lo, hi):
        return jnp.clip(idx - lo, 0, hi - lo - 1)

    if parts_on_n:
        if dims == "tn":
            a_specs = [pl.BlockSpec((bk, bm), lambda i, j, k: (k, i))]
            dn = (((0,), (0,)), ((), ()))
        else:
            a_specs = [pl.BlockSpec((bm, bk), lambda i, j, k: (i, k))]
            dn = (((1,), (0,)), ((), ()))
        b_specs = [pl.BlockSpec((bk, bn), functools.partial(lambda i, j, k, lo, hi: (k, part_index(j, lo, hi)), lo=lo, hi=hi))
                   for lo, hi in ranges]
    else:
        a_specs = [pl.BlockSpec((bm, bk), functools.partial(lambda i, j, k, lo, hi: (i, part_index(k, lo, hi)), lo=lo, hi=hi))
                   for lo, hi in ranges]
        if dims == "nt":
            b_specs = [pl.BlockSpec((bn, bk), lambda i, j, k: (j, k))]
        else:
            b_specs = [pl.BlockSpec((bk, bn), lambda i, j, k: (k, j))]
        dn = (((1,), (1 if dims == "nt" else 0,)), ((), ()))
    r_spec = pl.BlockSpec((bm, bn), lambda i, j, k: (i, j))
    if out_slots is None:
        o_spec, o_shape = r_spec, (M, N)
    else:
        ns = N // out_slots
        assert ns % bn == 0, (name, ns, bn)
        nbs = ns // bn
        o_spec = pl.BlockSpec((None, bm, bn), lambda i, j, k: (j // nbs, i, j % nbs))
        o_shape = (out_slots, M, ns)
    has_res = residual is not None
    na, nb = len(a_parts), len(b_parts)

    def body(*refs):
        a_refs, b_refs = refs[:na], refs[na:na + nb]
        r_ref = refs[na + nb] if has_res else None
        n_in = na + nb + has_res + (after is not None)
        o_ref = refs[n_in]
        acc_ref = refs[n_in + 1] if nk > 1 else None
        k = pl.program_id(2)

        def finish(acc):
            if has_res:
                acc = acc + r_ref[...].astype(F32)
            o_ref[...] = acc.astype(out_dtype)

        def compute(a_ref, b_ref):
            p = lax.dot_general(a_ref[...].astype(MXU_DTYPE), b_ref[...].astype(MXU_DTYPE), dn,
                                preferred_element_type=F32)
            if nk == 1:
                finish(p)
                return

            @pl.when(k == 0)
            def _():
                acc_ref[...] = p

            @pl.when(jnp.logical_and(k > 0, k < nk - 1))
            def _():
                acc_ref[...] += p

            @pl.when(k == nk - 1)
            def _():
                finish(acc_ref[...] + p)

        if len(ranges) == 1:
            compute(a_refs[0], b_refs[0])
        else:
            idx = pl.program_id(0 if cols_outer else 1) if parts_on_n else k
            for p, (lo, hi) in enumerate(ranges):
                @pl.when(jnp.logical_and(idx >= lo, idx < hi))
                def _(p=p):
                    compute(a_refs[0 if parts_on_n else p], b_refs[p if parts_on_n else 0])

    after_specs = [] if after is None else [pl.BlockSpec(after.shape, lambda i, j, k: (0, 0))]
    in_specs = a_specs + b_specs + ([r_spec] if has_res else []) + after_specs
    grid = (M // bm, N // bn, nk)
    if cols_outer:
        swap = lambda sp: pl.BlockSpec(sp.block_shape, functools.partial(lambda j, i, k, f: f(i, j, k), f=sp.index_map))
        in_specs, o_spec, grid = [swap(sp) for sp in in_specs], swap(o_spec), (grid[1], grid[0], nk)
    return pl.pallas_call(
        body, name=name, grid=grid,
        in_specs=in_specs, out_specs=o_spec,
        out_shape=jax.ShapeDtypeStruct(o_shape, out_dtype),
        scratch_shapes=[pltpu.VMEM((bm, bn), F32)] if nk > 1 else [],
        compiler_params=_params(("parallel", "parallel", "arbitrary")),
    )(*a_parts, *b_parts, *((residual,) if has_res else ()), *(() if after is None else (after,)))


def _mm_out_norm(a, w, residual, gain, name, bm=512):
    (M, K), N = a.shape, w.shape[1]
    bm = _pick(M, bm)

    def body(a_ref, w_ref, r_ref, g_ref, o_ref, h_ref, ht_ref):
        acc = lax.dot_general(a_ref[...].astype(MXU_DTYPE), w_ref[...].astype(MXU_DTYPE), _NN,
                              preferred_element_type=F32) + r_ref[...]
        o_ref[...] = acc
        hv = _fn_rms([[acc]], [], [[g_ref[...]]])[0][0]
        h_ref[...] = hv.astype(h_ref.dtype)
        ht_ref[...] = hv.T.astype(ht_ref.dtype)

    row = pl.BlockSpec((bm, N), lambda i: (i, 0))
    whole = lambda arr: pl.BlockSpec(arr.shape, lambda i: (0, 0))
    return pl.pallas_call(
        body, name=name, grid=(M // bm,),
        in_specs=[pl.BlockSpec((bm, K), lambda i: (i, 0)), whole(w), row, whole(gain)],
        out_specs=[row, row, pl.BlockSpec((N, bm), lambda i: (0, i))],
        out_shape=[jax.ShapeDtypeStruct((M, N), F32), jax.ShapeDtypeStruct((M, N), BF16),
                   jax.ShapeDtypeStruct((N, M), BF16)],
        compiler_params=_params(("parallel",)),
    )(a, w, residual, gain)


def _mm_dx_norm(a_parts, w, x, gain, add, name, bm=256, after=None):
    M = a_parts[0].shape[0]
    N, K = w.shape
    widths = [p.shape[1] for p in a_parts]
    assert sum(widths) == K, (name, widths, K)
    offs = [int(o) for o in np.cumsum([0] + widths[:-1])]
    bm = _pick(M, bm)
    na = len(a_parts)
    n_in = na + 4 + (after is not None)

    def body(*refs):
        w_ref, x_ref, g_ref, add_ref = refs[na:na + 4]
        dx_ref, dxc_ref, dg_ref = refs[n_in:n_in + 3]
        dh = None
        for a_ref, off, wd in zip(refs[:na], offs, widths):
            p = lax.dot_general(a_ref[...].astype(MXU_DTYPE), w_ref[:, off:off + wd].astype(MXU_DTYPE), _NT,
                                preferred_element_type=F32)
            dh = p if dh is None else dh + p
        _, vjp = jax.vjp(lambda xv, gv: _fn_rms([[xv]], [], [[gv]])[0][0], x_ref[...], g_ref[...])
        dxv, dgv = vjp(dh)
        dxv = dxv + add_ref[...]
        dx_ref[...] = dxv
        dxc_ref[...] = dxv.astype(dxc_ref.dtype)

        @pl.when(pl.program_id(0) == 0)
        def _():
            dg_ref[...] = dgv

        @pl.when(pl.program_id(0) > 0)
        def _():
            dg_ref[...] += dgv

    row = pl.BlockSpec((bm, N), lambda i: (i, 0))
    whole = lambda a: pl.BlockSpec(a.shape, lambda i: (0, 0))
    in_specs = [pl.BlockSpec((bm, wd), lambda i: (i, 0)) for wd in widths] + [whole(w), row, whole(gain), row]
    in_specs += [] if after is None else [whole(after)]
    return pl.pallas_call(
        body, name=name, grid=(M // bm,),
        in_specs=in_specs, out_specs=[row, row, whole(gain)],
        out_shape=[jax.ShapeDtypeStruct((M, N), F32), jax.ShapeDtypeStruct((M, N), BF16),
                   jax.ShapeDtypeStruct(gain.shape, F32)],
        compiler_params=_params(("arbitrary",)),
    )(*a_parts, w, x, gain, add, *(() if after is None else (after,)))


def _tiles(ref, width, tile):
    return [ref[:, t * tile:(t + 1) * tile].astype(F32) for t in range(width // tile)]


def _row_specs(rows, pos, consts, bm, S):
    npos_blocks = S // bm
    specs = [pl.BlockSpec((bm, w), functools.partial(lambda i, c: (i, c), c=cb)) for (_, w, cb, _) in rows]
    specs += [pl.BlockSpec((bm, p.shape[1]), lambda i: (i % npos_blocks, 0)) for p in pos]
    specs += [pl.BlockSpec(c.shape, lambda i: (0, 0)) for (c, _) in consts]
    return specs


def _rowwise_fwd(fn, name, rows, pos, consts, outs, bm, S, transposed=()):
    T = rows[0][0].shape[0]
    nr, npos, nc, no = len(rows), len(pos), len(consts), len(outs)

    def body(*refs):
        row_v = [_tiles(r, w, t) for r, (_, w, _, t) in zip(refs[:nr], rows)]
        pos_v = [r[...] for r in refs[nr:nr + npos]]
        const_v = [_tiles(r, c.shape[1], t) for r, (c, t) in zip(refs[nr + npos:nr + npos + nc], consts)]
        res = fn(row_v, pos_v, const_v)
        out_refs = refs[nr + npos + nc:]
        for o_ref, tiles, (w, t, dt) in zip(out_refs, res, outs):
            for k, v in enumerate(tiles):
                o_ref[:, k * t:(k + 1) * t] = v.astype(dt)
        for t_ref, a in zip(out_refs[no:], transposed):
            t = outs[a][1]
            for k, v in enumerate(res[a]):
                t_ref[k * t:(k + 1) * t, :] = v.T.astype(t_ref.dtype)

    return pl.pallas_call(
        body, name=name, grid=(T // bm,),
        in_specs=_row_specs(rows, pos, consts, bm, S),
        out_specs=[pl.BlockSpec((bm, w), lambda i: (i, 0)) for (w, _, _) in outs]
        + [pl.BlockSpec((outs[a][0], bm), lambda i: (0, i)) for a in transposed],
        out_shape=[jax.ShapeDtypeStruct((T, w), dt) for (w, _, dt) in outs]
        + [jax.ShapeDtypeStruct((outs[a][0], T), BF16) for a in transposed],
        compiler_params=_params(("parallel",)),
    )(*[r[0] for r in rows], *pos, *[c[0] for c in consts])


def _rowwise_bwd(fn, name, rows, pos, consts, cts, bm, S, adds=None, grad_dtypes=None, mxu_copies=(), linear=False):
    adds = adds or {}
    T = rows[0][0].shape[0]
    nr, npos, nc, nct = len(rows), len(pos), len(consts), len(cts)
    add_idx = sorted(adds)
    grad_dtypes = grad_dtypes or [F32] * nr

    def body(*refs):
        it = iter(refs)
        row_refs = [None if linear else next(it) for _ in range(nr)]
        pos_refs = [next(it) for _ in range(npos)]
        const_refs = [next(it) for _ in range(nc)]
        ct_refs = [next(it) for _ in range(nct)]
        add_refs = {k: next(it) for k in add_idx}
        drow_refs = [next(it) for _ in range(nr)]
        copy_refs = {a: next(it) for a in mxu_copies}
        dconst_refs = [next(it) for _ in range(nc)]
        if linear:
            row_v = [[jnp.zeros((bm, t), F32)] * (w // t) for (_, w, _, t) in rows]
        else:
            row_v = [_tiles(r, w, t) for r, (_, w, _, t) in zip(row_refs, rows)]
        pos_v = [r[...] for r in pos_refs]
        const_v = [_tiles(r, c.shape[1], t) for r, (c, t) in zip(const_refs, consts)]
        ct_v = [_tiles(r, c.shape[1], t) for r, (c, t) in zip(ct_refs, cts)]
        _, vjp = jax.vjp(lambda rv, cv: fn(rv, pos_v, cv), row_v, const_v)
        drows, dconsts = vjp(ct_v)
        for a, (d_ref, tiles, (_, w, _, t)) in enumerate(zip(drow_refs, drows, rows)):
            for k, v in enumerate(tiles):
                if a in add_refs:
                    v = v + add_refs[a][:, k * t:(k + 1) * t].astype(F32)
                d_ref[:, k * t:(k + 1) * t] = v.astype(d_ref.dtype)
                if a in copy_refs:
                    copy_refs[a][:, k * t:(k + 1) * t] = v.astype(BF16)
        first = pl.program_id(0) == 0
        for d_ref, tiles, (_, t) in zip(dconst_refs, dconsts, consts):
            for k, v in enumerate(tiles):
                @pl.when(first)
                def _(d_ref=d_ref, k=k, t=t, v=v):
                    d_ref[:, k * t:(k + 1) * t] = v

                @pl.when(jnp.logical_not(first))
                def _(d_ref=d_ref, k=k, t=t, v=v):
                    d_ref[:, k * t:(k + 1) * t] += v

    in_specs = _row_specs([] if linear else rows, pos, consts, bm, S)
    in_specs += [pl.BlockSpec((bm, c.shape[1]), lambda i: (i, 0)) for (c, _) in cts]
    in_specs += [pl.BlockSpec((bm, adds[k].shape[1]), lambda i: (i, 0)) for k in add_idx]
    out_specs = [pl.BlockSpec((bm, w), lambda i: (i, 0)) for (_, w, _, _) in rows]
    out_specs += [pl.BlockSpec((bm, rows[a][1]), lambda i: (i, 0)) for a in mxu_copies]
    out_specs += [pl.BlockSpec(c.shape, lambda i: (0, 0)) for (c, _) in consts]
    out_shape = [jax.ShapeDtypeStruct((T, w), dt) for (_, w, _, _), dt in zip(rows, grad_dtypes)]
    out_shape += [jax.ShapeDtypeStruct((T, rows[a][1]), BF16) for a in mxu_copies]
    out_shape += [jax.ShapeDtypeStruct(c.shape, F32) for (c, _) in consts]
    res = pl.pallas_call(
        body, name=name, grid=(T // bm,),
        in_specs=in_specs, out_specs=out_specs, out_shape=out_shape,
        compiler_params=_params(("arbitrary",)),
    )(*([] if linear else [r[0] for r in rows]), *pos, *[c[0] for c in consts], *[c[0] for c in cts],
      *[adds[k] for k in add_idx])
    n_rows = nr + len(mxu_copies)
    return res[:n_rows], res[n_rows:]


def _ssq(tiles):
    s = jnp.sum(tiles[0] * tiles[0], axis=-1, keepdims=True)
    for t in tiles[1:]:
        s = s + jnp.sum(t * t, axis=-1, keepdims=True)
    return s


def _sigmoid(x):
    return 1.0 / (1.0 + jnp.exp(-x))


def _fn_rms(rows, pos, consts):
    (x,), (g,) = rows[0], consts[0]
    r = lax.rsqrt(jnp.mean(x * x, axis=-1, keepdims=True) + RMS_EPS)
    return [[x * r * g]]


def _fn_ret_rope(rows, pos, consts):
    (qkv,) = rows
    nq = RET_HEADS * RET_QK // LANES
    q, k, v = qkv[:nq], qkv[nq:2 * nq], qkv[2 * nq:]
    cos, sin = pos

    def rot(t, scale):
        out = []
        for h in range(RET_HEADS):
            x1, x2 = t[2 * h], t[2 * h + 1]
            o1, o2 = x1 * cos - x2 * sin, x2 * cos + x1 * sin
            out += [o1, o2] if scale is None else [o1 * scale, o2 * scale]
        return out

    return [rot(q, None), rot(k, RET_QK ** -0.5), list(v)]


def _fn_ret_gate(rows, pos, consts):
    o, g = rows
    (gn,) = consts
    out = []
    for h in range(RET_HEADS):
        r = lax.rsqrt(jnp.mean(o[h] * o[h], axis=-1, keepdims=True) + RMS_EPS)
        out.append((o[h] * r * gn[h]) * (g[h] * _sigmoid(g[h])))
    return [out]


def _fn_mla_lat(rows, pos, consts):
    (p,) = rows
    gq, gkv = consts
    nq, nkv = MLA_Q_RANK // LANES, MLA_KV_RANK // LANES
    cq, ckv, kr = p[:nq], p[nq:nq + nkv], p[nq + nkv]
    rq = lax.rsqrt(_ssq(cq) / MLA_Q_RANK + RMS_EPS)
    rkv = lax.rsqrt(_ssq(ckv) / MLA_KV_RANK + RMS_EPS)
    return [[t * rq * g for t, g in zip(cq, gq)], [t * rkv * g for t, g in zip(ckv, gkv)], [kr]]


def _swap32_impl(x):
    lane = lax.broadcasted_iota(jnp.int32, x.shape, 1)
    up, down = pltpu.roll(x, LANES - 32, 1), pltpu.roll(x, 32, 1)
    return jnp.where(lane < 32, up, jnp.where(lane < 64, down, 0.0))


@jax.custom_vjp
def _swap32(x):
    return _swap32_impl(x)


_swap32.defvjp(lambda x: (_swap32_impl(x), None), lambda _, g: (_swap32_impl(g),))


def _fn_mla_heads(rows, pos, consts):
    qf, kvf, (kr,) = rows
    cos, sin = pos
    gq, gk = consts
    q_out, k_out, v_out = [], [], []
    for h in range(MLA_HEADS):
        q0, q1 = qf[2 * h], qf[2 * h + 1]
        r = lax.rsqrt(_ssq([q0, q1]) / MLA_QK + RMS_EPS)
        a0, a1 = q0 * r * gq[0], q1 * r * gq[1]
        a1 = a1 * cos + _swap32(a1) * sin
        q_out += [a0 * (MLA_QK ** -0.5), a1 * (MLA_QK ** -0.5)]
        k0 = kvf[2 * h]
        r = lax.rsqrt(_ssq([k0, kr]) / MLA_QK + RMS_EPS)
        b0, b1 = k0 * r * gk[0], kr * r * gk[1]
        k_out += [b0, b1 * cos + _swap32(b1) * sin]
        v_out.append(kvf[2 * h + 1])
    return [q_out, k_out, v_out]


def _shift_down(x, n):
    row = lax.broadcasted_iota(jnp.int32, x.shape, 0)
    return jnp.where(row >= n, pltpu.roll(x, n, 0), 0.0)


def _shift_up(x, n):
    rows = x.shape[0]
    row = lax.broadcasted_iota(jnp.int32, x.shape, 0)
    return jnp.where(row < rows - n, pltpu.roll(x, rows - n, 0), 0.0)


def _conv_blocks(S):
    cb = 256
    return cb, FFN_DIM // cb


def _conv_fwd(ag, w8, B, S, name):
    cb, ncb = _conv_blocks(S)

    def body(a_ref, g_ref, w_ref, u_ref, ut_ref):
        g = g_ref[...].astype(F32)
        w = w_ref[...]
        gc = w[0:1] * _shift_down(g, 2) + w[1:2] * _shift_down(g, 1) + w[2:3] * g + w[3:4]
        u = a_ref[...].astype(F32) * (gc * _sigmoid(gc))
        u_ref[...] = u.astype(u_ref.dtype)
        ut_ref[...] = u.T.astype(ut_ref.dtype)

    return pl.pallas_call(
        body, name=name, grid=(ncb, B),
        in_specs=[pl.BlockSpec((S, cb), lambda j, b: (b, j)),
                  pl.BlockSpec((S, cb), lambda j, b: (b, ncb + j)),
                  pl.BlockSpec((8, cb), lambda j, b: (0, j))],
        out_specs=[pl.BlockSpec((S, cb), lambda j, b: (b, j)), pl.BlockSpec((cb, S), lambda j, b: (j, b))],
        out_shape=[jax.ShapeDtypeStruct((B * S, FFN_DIM), BF16), jax.ShapeDtypeStruct((FFN_DIM, B * S), BF16)],
        compiler_params=_params(("parallel", "parallel")),
    )(ag, ag, w8)


def _conv_bwd(ag, w8, du, B, S, name):
    cb, ncb = _conv_blocks(S)

    def body(a_ref, g_ref, w_ref, du_ref, da_ref, dg_ref, dw_ref):
        g = g_ref[...].astype(F32)
        w = w_ref[...]
        g1, g2 = _shift_down(g, 1), _shift_down(g, 2)
        gc = w[0:1] * g2 + w[1:2] * g1 + w[2:3] * g + w[3:4]
        sg = _sigmoid(gc)
        du_v = du_ref[...]
        da_ref[...] = (du_v * (gc * sg)).astype(da_ref.dtype)
        dgc = du_v * a_ref[...].astype(F32) * (sg * (1.0 + gc * (1.0 - sg)))
        dg = w[2:3] * dgc + w[1:2] * _shift_up(dgc, 1) + w[0:1] * _shift_up(dgc, 2)
        dg_ref[...] = dg.astype(dg_ref.dtype)
        part = jnp.concatenate([
            jnp.sum(dgc * g2, axis=0, keepdims=True), jnp.sum(dgc * g1, axis=0, keepdims=True),
            jnp.sum(dgc * g, axis=0, keepdims=True), jnp.sum(dgc, axis=0, keepdims=True),
            jnp.zeros((4, cb), F32)], axis=0)

        @pl.when(pl.program_id(1) == 0)
        def _():
            dw_ref[...] = part

        @pl.when(pl.program_id(1) > 0)
        def _():
            dw_ref[...] += part

    blk = lambda j, b: (b, j)
    return pl.pallas_call(
        body, name=name, grid=(ncb, B),
        in_specs=[pl.BlockSpec((S, cb), blk),
                  pl.BlockSpec((S, cb), lambda j, b: (b, ncb + j)),
                  pl.BlockSpec((8, cb), lambda j, b: (0, j)),
                  pl.BlockSpec((S, cb), blk)],
        out_specs=[pl.BlockSpec((S, cb), blk), pl.BlockSpec((S, cb), blk),
                   pl.BlockSpec((8, cb), lambda j, b: (0, j))],
        out_shape=[jax.ShapeDtypeStruct((B * S, FFN_DIM), BF16), jax.ShapeDtypeStruct((B * S, FFN_DIM), BF16),
                   jax.ShapeDtypeStruct((8, FFN_DIM), F32)],
        compiler_params=_params(("parallel", "arbitrary")),
    )(ag, ag, w8, du)


_NT = (((1,), (1,)), ((), ()))
_NN = (((1,), (0,)), ((), ()))
_TN = (((0,), (0,)), ((), ()))


def _dot(a, b, dn):
    return lax.dot_general(a.astype(MXU_DTYPE), b.astype(MXU_DTYPE), dn, preferred_element_type=F32)


def _rel_and_mask():
    il = lax.broadcasted_iota(jnp.int32, (ATT_BLOCK, ATT_BLOCK), 0)
    jl = lax.broadcasted_iota(jnp.int32, (ATT_BLOCK, ATT_BLOCK), 1)
    return (il - jl).astype(F32), (jl // CHUNK) <= (il // CHUNK)


def _rows(i):
    return pl.ds(pl.multiple_of(i * ATT_BLOCK, ATT_BLOCK), ATT_BLOCK)


def _run_bits(n):
    bits, b = [], 1
    while b < n:
        bits.append(b)
        b *= 2
    return bits[::-1]


def _key_runs(n, nq, update):
    for bit in _run_bits(nq + 1):
        @pl.when((n & bit) != 0)
        def _(bit=bit):
            update(n & ~(2 * bit - 1), bit, (n & (bit - 1)) == 0)


def _earlier_runs(n, nq, update):
    for bit in _run_bits(nq):
        @pl.when((n & bit) != 0)
        def _(bit=bit):
            update(n & ~(2 * bit - 1), bit, False)


def _chunk_visible(shape, nblk, blk):
    key = lax.broadcasted_iota(jnp.int32, shape, 0) - (nblk - 1) * blk
    query = lax.broadcasted_iota(jnp.int32, shape, 1)
    return jnp.logical_or(key < 0, (key // CHUNK) <= (query // CHUNK))


KV_UNROLL = 2


def _kv_loop(n, body, carry):
    main = n // KV_UNROLL

    def chunk(t, c):
        for u in range(KV_UNROLL):
            c = body(t * KV_UNROLL + u, c)
        return c

    carry = lax.fori_loop(0, main, chunk, carry)
    return lax.fori_loop(main * KV_UNROLL, n, body, carry)


def _mla_attn_fwd(q, k, v, B, S):
    blk = min(MLA_FWD_BLOCK, S)
    H, nq = MLA_HEADS, S // blk

    def body(q_ref, k_ref, v_ref, o_ref, lse_ref, m_ref, l_ref, acc_ref):
        def qblock(i, _):
            q_rows = pl.ds(pl.multiple_of(i * blk, blk), blk)
            qi = q_ref[q_rows, :]
            m_ref[...] = jnp.full(m_ref.shape, MASK_VALUE, F32)
            l_ref[...] = jnp.zeros(l_ref.shape, F32)
            acc_ref[...] = jnp.zeros(acc_ref.shape, F32)

            def keys(first, nblk, last):
                rows = pl.ds(pl.multiple_of(first * blk, blk), nblk * blk)
                s = _dot(k_ref[rows, :], qi, _NT)
                s = jnp.where(jnp.logical_or(_chunk_visible(s.shape, nblk, blk), jnp.logical_not(last)), s, MASK_VALUE)
                m = m_ref[...]
                m2 = jnp.maximum(m, jnp.max(s, axis=0, keepdims=True))
                alpha = jnp.exp(m - m2)
                p = jnp.exp(s - m2)
                l_ref[...] = alpha * l_ref[...] + jnp.sum(p, axis=0, keepdims=True)
                acc_ref[...] = alpha * acc_ref[...] + _dot(v_ref[rows, :], p, _TN)
                m_ref[...] = m2

            _key_runs(i + 1, nq, keys)
            l = l_ref[...]
            o_ref[q_rows, :] = (acc_ref[...] / l).T
            lse_ref[0, :, q_rows] = m_ref[...] + jnp.log(l)
            return 0

        lax.fori_loop(0, nq, qblock, 0)

    return pl.pallas_call(
        body, name="mla_attn_fwd", grid=(B, H),
        in_specs=[pl.BlockSpec((S, MLA_PAD), lambda b, h: (b, h)),
                  pl.BlockSpec((S, MLA_PAD), lambda b, h: (b, h)),
                  pl.BlockSpec((S, MLA_V), lambda b, h: (b, h))],
        out_specs=[pl.BlockSpec((S, MLA_V), lambda b, h: (b, h)),
                   pl.BlockSpec((1, 1, S), lambda b, h: (b * H + h, 0, 0))],
        out_shape=[jax.ShapeDtypeStruct((B * S, H * MLA_V), F32), jax.ShapeDtypeStruct((B * H, 1, S), F32)],
        scratch_shapes=[pltpu.VMEM((1, blk), F32), pltpu.VMEM((1, blk), F32), pltpu.VMEM((MLA_V, blk), F32)],
        compiler_params=_params(("parallel", "parallel")),
    )(q, k, v)


def _mla_attn_bwd(q, k, v, o, do, lse, B, S):
    blk = min(MLA_FWD_BLOCK, S)
    H, nq = MLA_HEADS, S // blk

    def body(q_ref, k_ref, v_ref, o_ref, do_ref, lse_ref, dq_ref, dk_ref, dv_ref, kt_ref, dqt_ref):
        dk_ref[...] = jnp.zeros(dk_ref.shape, F32)
        dv_ref[...] = jnp.zeros(dv_ref.shape, F32)
        for g in range(nq):
            kt_ref[g] = k_ref[g * blk:(g + 1) * blk, :].T

        def qblock(i, _):
            q_rows = pl.ds(pl.multiple_of(i * blk, blk), blk)
            qi = q_ref[q_rows, :]
            doi = do_ref[q_rows, :]
            delta = jnp.sum((doi * o_ref[q_rows, :]).T, axis=0, keepdims=True)
            lse_i = lse_ref[0, :, q_rows]
            doi = doi.astype(MXU_DTYPE)
            dqt_ref[...] = jnp.zeros(dqt_ref.shape, F32)

            def keys(first, nblk, last):
                rows = pl.ds(pl.multiple_of(first * blk, blk), nblk * blk)
                k_run, v_run = k_ref[rows, :], v_ref[rows, :]
                p = jnp.exp(_dot(k_run, qi, _NT) - lse_i)
                p = jnp.where(jnp.logical_or(_chunk_visible(p.shape, nblk, blk), jnp.logical_not(last)), p, 0.0)
                ds = (p * (_dot(v_run, doi, _NT) - delta)).astype(MXU_DTYPE)
                dk_ref[rows, :] += _dot(ds, qi, _NN)
                dv_ref[rows, :] += _dot(p, doi, _NN)
                for r in range(nblk):
                    dqt_ref[...] += _dot(kt_ref[first + r], ds[r * blk:(r + 1) * blk, :], _NN)

            _key_runs(i + 1, nq, keys)
            dq_ref[q_rows, :] = dqt_ref[...].T
            return 0

        lax.fori_loop(0, nq, qblock, 0)

    qk_spec = pl.BlockSpec((S, MLA_PAD), lambda b, h: (b, h))
    v_spec = pl.BlockSpec((S, MLA_V), lambda b, h: (b, h))
    return pl.pallas_call(
        body, name="mla_attn_bwd", grid=(B, H),
        in_specs=[qk_spec, qk_spec, v_spec, v_spec, v_spec,
                  pl.BlockSpec((1, 1, S), lambda b, h: (b * H + h, 0, 0))],
        out_specs=[qk_spec, qk_spec, v_spec],
        out_shape=[jax.ShapeDtypeStruct((B * S, H * MLA_PAD), F32), jax.ShapeDtypeStruct((B * S, H * MLA_PAD), F32),
                   jax.ShapeDtypeStruct((B * S, H * MLA_V), F32)],
        scratch_shapes=[pltpu.VMEM((nq, MLA_PAD, blk), q.dtype), pltpu.VMEM((MLA_PAD, blk), F32)],
        compiler_params=_params(("parallel", "parallel")),
    )(q, k, v, o, do, lse)


def _ret_log_gamma():
    lg = np.log1p(-np.exp2(RET_GAMMA_BASE - np.arange(RET_HEADS, dtype=np.float32))).astype(np.float32)
    return jnp.asarray(np.broadcast_to(lg[:, None, None], (RET_HEADS, 8, LANES)).copy())


RET_BLOCK = 512


def _ret_local_scale(lg, shape, blk, rising):
    local = lax.broadcasted_iota(jnp.int32, shape, 0) % blk
    return jnp.exp(lg * (local if rising else blk - 1 - local).astype(F32))


def _ret_pair_factor(lg, blk, steps):
    return jnp.exp(lg * (blk * (steps - 1) + 1).astype(F32))


def _ret_own_decay(lg, blk, transposed):
    a = lax.broadcasted_iota(jnp.int32, (blk, blk), 0)
    b = lax.broadcasted_iota(jnp.int32, (blk, blk), 1)
    query, key = (b, a) if transposed else (a, b)
    dec = jnp.exp(lg * jnp.abs(query - key).astype(F32))
    return jnp.where((key // CHUNK) <= (query // CHUNK), dec, 0.0)


def _ret_attn_fwd(q, k, v, B, S):
    blk = min(RET_BLOCK, S)
    H, nq = RET_HEADS, S // blk

    def body(lg_ref, q_ref, k_ref, v_ref, o_ref, ks_ref, dec_ref, acc_ref):
        lg = lg_ref[0, 0:1, 0:1]
        ks_ref[...] = (k_ref[...].astype(F32) * _ret_local_scale(lg, k_ref.shape, blk, False)).astype(ks_ref.dtype)
        dec_ref[...] = _ret_own_decay(lg, blk, False)

        def qblock(i, _):
            q_rows = pl.ds(pl.multiple_of(i * blk, blk), blk)
            qi = q_ref[q_rows, :]
            qs = (qi.astype(F32) * _ret_local_scale(lg, qi.shape, blk, True)).astype(qi.dtype)
            a = _dot(qi, k_ref[q_rows, :], _NT) * dec_ref[...]
            acc_ref[...] = _dot(a, v_ref[q_rows, :], _NN)

            def keys(first, nblk, _):
                rows = pl.ds(pl.multiple_of(first * blk, blk), nblk * blk)
                steps = i - first - lax.broadcasted_iota(jnp.int32, (1, nblk * blk), 1) // blk
                a = _dot(qs, ks_ref[rows, :], _NT) * _ret_pair_factor(lg, blk, steps)
                acc_ref[...] += _dot(a, v_ref[rows, :], _NN)

            _earlier_runs(i, nq, keys)
            o_ref[q_rows, :] = acc_ref[...]
            return 0

        lax.fori_loop(0, nq, qblock, 0)

    qk_spec = pl.BlockSpec((S, RET_QK), lambda b, h: (b, h))
    v_spec = pl.BlockSpec((S, RET_V), lambda b, h: (b, h))
    return pl.pallas_call(
        body, name="ret_attn_fwd", grid=(B, H),
        in_specs=[pl.BlockSpec((1, 8, LANES), lambda b, h: (h, 0, 0)), qk_spec, qk_spec, v_spec],
        out_specs=v_spec,
        out_shape=jax.ShapeDtypeStruct((B * S, H * RET_V), F32),
        scratch_shapes=[pltpu.VMEM((S, RET_QK), k.dtype), pltpu.VMEM((blk, blk), F32), pltpu.VMEM((blk, RET_V), F32)],
        compiler_params=_params(("parallel", "parallel")),
    )(_ret_log_gamma(), q, k, v)


def _ret_attn_bwd(q, k, v, do, B, S):
    blk = min(RET_BLOCK, S)
    H, nq = RET_HEADS, S // blk

    def body(lg_ref, q_ref, k_ref, v_ref, do_ref, dq_ref, dk_ref, dv_ref, ks_ref, kst_ref, dks_ref, dqt_ref, dec_ref):
        lg = lg_ref[0, 0:1, 0:1]
        dk_ref[...] = jnp.zeros(dk_ref.shape, F32)
        dv_ref[...] = jnp.zeros(dv_ref.shape, F32)
        dks_ref[...] = jnp.zeros(dks_ref.shape, F32)
        ks_ref[...] = (k_ref[...].astype(F32) * _ret_local_scale(lg, k_ref.shape, blk, False)).astype(ks_ref.dtype)
        for g in range(nq):
            kst_ref[g] = ks_ref[g * blk:(g + 1) * blk, :].T
        dec_ref[...] = _ret_own_decay(lg, blk, True)

        def qblock(i, _):
            q_rows = pl.ds(pl.multiple_of(i * blk, blk), blk)
            qi = q_ref[q_rows, :]
            q_scale = _ret_local_scale(lg, qi.shape, blk, True)
            qs = (qi.astype(F32) * q_scale).astype(qi.dtype)
            doi = do_ref[q_rows, :].astype(MXU_DTYPE)
            ki = k_ref[q_rows, :]
            dec = dec_ref[...]
            a = _dot(ki, qi, _NT) * dec
            da = (_dot(v_ref[q_rows, :], doi, _NT) * dec).astype(MXU_DTYPE)
            dv_ref[q_rows, :] += _dot(a, doi, _NN)
            dk_ref[q_rows, :] += _dot(da, qi, _NN)
            dq_own = _dot(da, ki, _TN)
            dqt_ref[...] = jnp.zeros(dqt_ref.shape, F32)

            def keys(first, nblk, _):
                for r in range(nblk):
                    g = first + r
                    rows = pl.ds(pl.multiple_of(g * blk, blk), blk)
                    c = _ret_pair_factor(lg, blk, i - g)
                    a = _dot(ks_ref[rows, :], qs, _NT) * c
                    da = (_dot(v_ref[rows, :], doi, _NT) * c).astype(MXU_DTYPE)
                    dv_ref[rows, :] += _dot(a, doi, _NN)
                    dks_ref[rows, :] += _dot(da, qs, _NN)
                    dqt_ref[...] += _dot(kst_ref[g], da, _NN)

            _earlier_runs(i, nq, keys)
            dq_ref[q_rows, :] = dqt_ref[...].T * q_scale + dq_own
            return 0

        lax.fori_loop(0, nq, qblock, 0)
        dk_ref[...] += dks_ref[...] * _ret_local_scale(lg, dks_ref.shape, blk, False)

    qk_spec = pl.BlockSpec((S, RET_QK), lambda b, h: (b, h))
    v_spec = pl.BlockSpec((S, RET_V), lambda b, h: (b, h))
    return pl.pallas_call(
        body, name="ret_attn_bwd", grid=(B, H),
        in_specs=[pl.BlockSpec((1, 8, LANES), lambda b, h: (h, 0, 0)), qk_spec, qk_spec, v_spec, v_spec],
        out_specs=[qk_spec, qk_spec, v_spec],
        out_shape=[jax.ShapeDtypeStruct((B * S, H * RET_QK), F32), jax.ShapeDtypeStruct((B * S, H * RET_QK), F32),
                   jax.ShapeDtypeStruct((B * S, H * RET_V), F32)],
        scratch_shapes=[pltpu.VMEM((S, RET_QK), k.dtype), pltpu.VMEM((nq, RET_QK, blk), k.dtype),
                        pltpu.VMEM((S, RET_QK), F32), pltpu.VMEM((RET_QK, blk), F32), pltpu.VMEM((blk, blk), F32)],
        compiler_params=_params(("parallel", "parallel")),
    )(_ret_log_gamma(), q, k, v, do)


def _loss_head(y, target, bm=512):
    T, D = y.shape
    bm = _pick(T, bm)

    def body(y_ref, t_ref, dy_ref, dyc_ref, l_ref):
        err = y_ref[...] - t_ref[...]
        dy_ref[...] = err / D
        dyc_ref[...] = (err / D).astype(dyc_ref.dtype)
        part = jnp.full((8, LANES), 0.5 * jnp.sum(jnp.mean(err * err, axis=-1)), F32)

        @pl.when(pl.program_id(0) == 0)
        def _():
            l_ref[...] = part

        @pl.when(pl.program_id(0) > 0)
        def _():
            l_ref[...] += part

    blk = pl.BlockSpec((bm, D), lambda i: (i, 0))
    dy, dyc, l = pl.pallas_call(
        body, name="loss_head", grid=(T // bm,),
        in_specs=[blk, blk], out_specs=[blk, blk, pl.BlockSpec((8, LANES), lambda i: (0, 0))],
        out_shape=[jax.ShapeDtypeStruct((T, D), F32), jax.ShapeDtypeStruct((T, D), BF16),
                   jax.ShapeDtypeStruct((8, LANES), F32)],
        compiler_params=_params(("arbitrary",)),
    )(y, target)
    return dy, dyc, l[0, 0]


def _adamw(w, g, m, v, name):
    R, C = w.shape
    br = R if R * C * 4 <= 2 ** 21 else _pick_rows(R, max(8, (2 ** 21) // (C * 4)))

    def body(w_ref, g_ref, m_ref, v_ref, d_ref, mo_ref, vo_ref):
        g_v = g_ref[...]
        m_v = ADAM_B1 * m_ref[...] + (1.0 - ADAM_B1) * g_v
        v_v = ADAM_B2 * v_ref[...] + (1.0 - ADAM_B2) * (g_v * g_v)
        m_hat = m_v / (1.0 - ADAM_B1 ** ADAM_STEP)
        v_hat = v_v / (1.0 - ADAM_B2 ** ADAM_STEP)
        d_ref[...] = -ADAM_LR * (m_hat / (jnp.sqrt(v_hat) + ADAM_EPS) + ADAM_WD * w_ref[...])
        mo_ref[...] = m_v
        vo_ref[...] = v_v

    blk = pl.BlockSpec((br, C), lambda i: (i, 0))
    return pl.pallas_call(
        body, name=name, grid=(R // br,),
        in_specs=[blk] * 4, out_specs=[blk] * 3,
        out_shape=[jax.ShapeDtypeStruct((R, C), F32)] * 3,
        compiler_params=_params(("parallel",)),
    )(w, g, m, v)


def _pick_rows(R, target):
    best = None
    for d in range(8, min(R, target) + 1, 8):
        if R % d == 0:
            best = d
    assert best is not None, (R, target)
    return best


def _position():
    return lax.axis_index("x"), lax.axis_index("y"), lax.axis_index("c")


HBM_SPEC = pl.BlockSpec(memory_space=pltpu.HBM)


def _other_chips(x, y):
    return [(1 - x, y), (x, 1 - y), (1 - x, 1 - y)]


def _all_gather_weights(bigs, small):
    nb = len(bigs)

    def body(*refs):
        big_refs, small_ref = refs[:nb], refs[nb]
        obig, osmall = refs[nb + 1:2 * nb + 1], refs[2 * nb + 1]
        ici_send, ici_recv, d2d_send, d2d_recv, sm_send, sm_recv = refs[2 * nb + 2:]
        x, y, c = _position()
        me = 2 * x + y
        chips = _other_chips(x, y)

        def rows(n, half):
            rh = bigs[n].shape[0] // 2
            return pl.ds(half * rh, rh)

        def over_ici(n, j, slot, from_shard):
            px, py = chips[j]
            dst = obig[n].at[slot, rows(n, c)]
            return pltpu.make_async_remote_copy(
                src_ref=big_refs[n].at[rows(n, c)] if from_shard else dst, dst_ref=dst,
                send_sem=ici_send.at[3 * n + j], recv_sem=ici_recv.at[3 * n + j],
                device_id=(px, py, c), device_id_type=MESH)

        def over_d2d(n, j, half):
            px, py = chips[j]
            part = obig[n].at[2 * px + py, rows(n, half)]
            return pltpu.make_async_remote_copy(
                src_ref=part, dst_ref=part, send_sem=d2d_send.at[3 * n + j], recv_sem=d2d_recv.at[3 * n + j],
                device_id=(x, y, 1 - c), device_id_type=MESH)

        def small_copy(j, slot):
            px, py = chips[j]
            return pltpu.make_async_remote_copy(
                src_ref=small_ref, dst_ref=osmall.at[slot], send_sem=sm_send.at[j], recv_sem=sm_recv.at[j],
                device_id=(px, py, c), device_id_type=MESH)

        sends = [over_ici(n, j, me, True) for n in range(nb) for j in range(3)]
        sends += [small_copy(j, me) for j in range(3)]
        for cp in sends:
            cp.start()
        passed = []
        for n in range(nb):
            for j, (px, py) in enumerate(chips):
                over_ici(n, j, 2 * px + py, False).wait_recv()
                fwd = over_d2d(n, j, c)
                fwd.start()
                passed.append(fwd)
        for n in range(nb):
            for j in range(3):
                over_d2d(n, j, 1 - c).wait_recv()
        for j, (px, py) in enumerate(chips):
            small_copy(j, 2 * px + py).wait_recv()
        for cp in sends + passed:
            cp.wait_send()

    dma = pltpu.SemaphoreType.DMA
    return pl.pallas_call(
        body, name="weights_all_gather",
        in_specs=[HBM_SPEC] * (nb + 1), out_specs=[HBM_SPEC] * (nb + 1),
        out_shape=[jax.ShapeDtypeStruct((N_SHARD,) + b.shape, b.dtype) for b in bigs]
        + [jax.ShapeDtypeStruct((N_SHARD,) + small.shape, small.dtype)],
        scratch_shapes=[dma((3 * nb,)), dma((3 * nb,)), dma((3 * nb,)), dma((3 * nb,)), dma((3,)), dma((3,))],
    )(*bigs, small)


SEM_SPEC = pl.BlockSpec(memory_space=pltpu.SEMAPHORE)
DATAFLOW_EFFECT = pltpu.SideEffectType.DATAFLOW_SIDE_EFFECTING
N_PEERS = N_DEV - 1


def _grad_copies(p_refs, land_refs, send_sems, recv_sems):
    x, y, c = _position()
    copies = []
    for a, (p_ref, land_ref) in enumerate(zip(p_refs, land_refs)):
        rh = p_ref.shape[1] // 2
        for k in range(1, N_DEV):
            px = 1 - x if k & 4 else x
            py = 1 - y if k & 2 else y
            pc = 1 - c if k & 1 else c
            copies.append(pltpu.make_async_remote_copy(
                src_ref=p_ref.at[2 * px + py, pl.ds(pc * rh, rh)], dst_ref=land_ref.at[k - 1],
                send_sem=send_sems.at[N_PEERS * a + k - 1], recv_sem=recv_sems.at[N_PEERS * a + k - 1],
                device_id=(px, py, pc), device_id_type=MESH))
    return copies


def _weight_copies(w_refs, land_refs, send_sems, recv_sems):
    x, y, c = _position()
    copies = []
    for a, (w_ref, land_ref) in enumerate(zip(w_refs, land_refs)):
        for j, (px, py) in enumerate(_other_chips(x, y)):
            copies.append(pltpu.make_async_remote_copy(
                src_ref=w_ref, dst_ref=land_ref.at[2 * x + y], send_sem=send_sems.at[3 * a + j],
                recv_sem=recv_sems.at[3 * a + j], device_id=(px, py, c), device_id_type=MESH))
    return copies


def _exchange_start(make_copies, srcs, lands, n_sems, name, after=None):
    n, m = len(srcs), len(lands)
    n_in = n + m + (after is not None)

    def body(*refs):
        send_sems, recv_sems, token = refs[n_in], refs[n_in + 1], refs[-1]
        for cp in make_copies(refs[:n], refs[n:n + m], send_sems, recv_sems):
            cp.start()
        token[...] = jnp.zeros(token.shape, token.dtype)

    hbm = lambda a: pltpu.with_memory_space_constraint(a, pltpu.HBM)
    dma = pltpu.SemaphoreType.DMA
    res = pl.pallas_call(
        body, name=name,
        in_specs=[HBM_SPEC] * (n + m) + ([] if after is None else [pl.BlockSpec(memory_space=pl.ANY)]),
        out_specs=[SEM_SPEC, SEM_SPEC] + [HBM_SPEC] * (n + m) + [pl.BlockSpec(memory_space=pltpu.VMEM)],
        out_shape=[dma((n_sems,)), dma((n_sems,))] + [pltpu.HBM(a.shape, a.dtype) for a in list(srcs) + list(lands)]
        + [jax.ShapeDtypeStruct((8, LANES), F32)],
        input_output_aliases={i: 2 + i for i in range(n + m)},
        compiler_params=pltpu.CompilerParams(has_side_effects=DATAFLOW_EFFECT),
    )(*[hbm(a) for a in srcs], *[hbm(a) for a in lands], *(() if after is None else (after,)))
    return res[0], res[1], list(res[2:2 + n]), list(res[2 + n:2 + n + m]), res[-1]


def _exchange_wait(make_copies, send_sems, recv_sems, srcs, lands, after, name):
    n, m = len(srcs), len(lands)

    def body(*refs):
        for cp in make_copies(refs[:n], refs[n:n + m], refs[n + m], refs[n + m + 1]):
            cp.wait_send()
            cp.wait_recv()

    res = pl.pallas_call(
        body, name=name,
        in_specs=[HBM_SPEC] * (n + m) + [SEM_SPEC, SEM_SPEC, pl.BlockSpec(memory_space=pl.ANY)],
        out_specs=[HBM_SPEC] * (n + m),
        out_shape=[pltpu.HBM(a.shape, a.dtype) for a in list(srcs) + list(lands)],
        input_output_aliases={i: i for i in range(n + m)},
        compiler_params=pltpu.CompilerParams(has_side_effects=DATAFLOW_EFFECT),
    )(*srcs, *lands, send_sems, recv_sems, after)
    return list(res[:n]), list(res[n:])


def _sum_partials(p, land, name):
    _, rh, cols = land.shape
    br = _pick_rows(rh, 256)
    nrb = rh // br
    x, y, c = _position()
    where = jnp.stack([2 * x + y, c]).astype(jnp.int32)

    def body(where_ref, p_ref, land_ref, o_ref):
        acc = p_ref[...].astype(F32)
        for k in range(N_PEERS):
            acc = acc + land_ref[k].astype(F32)
        o_ref[...] = acc

    return pl.pallas_call(
        body, name=name,
        grid_spec=pltpu.PrefetchScalarGridSpec(
            num_scalar_prefetch=1, grid=(nrb,),
            in_specs=[pl.BlockSpec((None, br, cols), lambda r, where_ref: (where_ref[0], where_ref[1] * nrb + r, 0)),
                      pl.BlockSpec((N_PEERS, br, cols), lambda r, where_ref: (0, r, 0))],
            out_specs=pl.BlockSpec((None, br, cols), lambda r, where_ref: (where_ref[1], r, 0))),
        out_shape=jax.ShapeDtypeStruct((2, rh, cols), F32),
        compiler_params=_params(("parallel",)),
    )(where, p, land)


def _sibling_share(fulls, name):
    n = len(fulls)

    def body(*refs):
        o_refs = refs[n:2 * n]
        send_sems, recv_sems = refs[2 * n:]
        x, y, c = _position()

        def copy(a, half):
            return pltpu.make_async_remote_copy(
                src_ref=o_refs[a].at[half], dst_ref=o_refs[a].at[half], send_sem=send_sems.at[a],
                recv_sem=recv_sems.at[a], device_id=(x, y, 1 - c), device_id_type=MESH)

        sends = [copy(a, c) for a in range(n)]
        for cp in sends:
            cp.start()
        for a in range(n):
            copy(a, 1 - c).wait_recv()
        for cp in sends:
            cp.wait_send()

    dma = pltpu.SemaphoreType.DMA
    return pl.pallas_call(
        body, name=name,
        in_specs=[HBM_SPEC] * n, out_specs=[HBM_SPEC] * n,
        out_shape=[jax.ShapeDtypeStruct(f.shape, f.dtype) for f in fulls],
        input_output_aliases={a: a for a in range(n)},
        scratch_shapes=[dma((n,)), dma((n,))],
    )(*fulls)


def _all_reduce_small(v):
    R, cols = v.shape

    def body(v_ref, o_ref, buf_ref, send_sems, recv_sems):
        x, y, c = _position()
        me = 4 * x + 2 * y + c
        buf_ref[me] = v_ref[...]
        sends = []
        for k in range(1, N_DEV):
            px = 1 - x if k & 4 else x
            py = 1 - y if k & 2 else y
            pc = 1 - c if k & 1 else c
            sends.append(pltpu.make_async_remote_copy(
                src_ref=v_ref, dst_ref=buf_ref.at[me], send_sem=send_sems.at[k - 1], recv_sem=recv_sems.at[k - 1],
                device_id=(px, py, pc), device_id_type=MESH))
        for cp in sends:
            cp.start()
        for k in range(1, N_DEV):
            px = 1 - x if k & 4 else x
            py = 1 - y if k & 2 else y
            pc = 1 - c if k & 1 else c
            pltpu.make_async_remote_copy(
                src_ref=v_ref, dst_ref=buf_ref.at[4 * px + 2 * py + pc], send_sem=send_sems.at[k - 1],
                recv_sem=recv_sems.at[k - 1], device_id=(px, py, pc), device_id_type=MESH).wait_recv()
        for cp in sends:
            cp.wait_send()
        acc = buf_ref[0]
        for d in range(1, N_DEV):
            acc = acc + buf_ref[d]
        o_ref[...] = acc

    return pl.pallas_call(
        body, name="small_grads_all_reduce",
        in_specs=[pl.BlockSpec(memory_space=pltpu.VMEM)], out_specs=pl.BlockSpec(memory_space=pltpu.VMEM),
        out_shape=jax.ShapeDtypeStruct((R, cols), F32),
        scratch_shapes=[pltpu.VMEM((N_DEV, R, cols), F32), pltpu.SemaphoreType.DMA((N_DEV - 1,)),
                        pltpu.SemaphoreType.DMA((N_DEV - 1,))],
    )(v)


def _rope_tables(S, half, width):
    inv_freq = ROPE_THETA ** (-jnp.arange(half, dtype=F32) / half)
    ang = jnp.arange(S).astype(F32)[:, None] * inv_freq[None, :]
    return jnp.cos(ang), jnp.sin(ang)


def _slot_rows(a):
    return a.reshape(N_SHARD, -1, a.shape[-1])


def _local_step(x, target, w, B, S, late, exchange, reduce_small):
    T = B * S
    D = D_MODEL
    bm = 256
    full = lambda a, wd, tile=None: (a, wd, 0, tile or wd)
    g = {}

    cos_r, sin_r = _rope_tables(S, RET_QK // 2, LANES)
    cos_m, sin_m = _rope_tables(S, MLA_ROPE // 2, LANES)
    zeros64 = jnp.zeros((S, 64), F32)
    cos_m = jnp.concatenate([cos_m, cos_m, zeros64], axis=1)
    sin_m = jnp.concatenate([-sin_m, sin_m, zeros64], axis=1)

    def ffn_fwd(xin, h, ht, i, next_gain):
        w.update(late(f"ffn{i}", xin))
        norm = w["ffn_norm"][i:i + 1]
        ag = _mm(h, w[f"ffn_w_in{i}"], "nn", BF16, f"ffn{i}_in", bn=1408, cols_outer=True)
        u, ut = _conv_fwd(ag, w["ffn_conv8"][i], B, S, f"ffn{i}_conv")
        if next_gain is None:
            out = (_mm(u, w[f"ffn_w_out{i}"], "nn", F32, f"ffn{i}_out", residual=xin, bk=FFN_DIM),)
        else:
            out = _mm_out_norm(u, w[f"ffn_w_out{i}"], xin, next_gain, f"ffn{i}_out")
        return out, (xin, norm, ht, ag, ut)

    def ffn_bwd(dxout, dxout_c, saved, i):
        xin, norm, ht, ag, ut = saved
        du = _mm(dxout_c, w[f"ffn_w_out{i}"], "nt", F32, f"ffn{i}_out_dx", bn=1408, cols_outer=True)
        g_w_out = _mm(ut, dxout_c, "nn", BF16, f"ffn{i}_out_dw", bm=1408, bn=512, bk=T)
        da, dg, dw8 = _conv_bwd(ag, w["ffn_conv8"][i], du, B, S, f"ffn{i}_conv_bwd")
        g_w_in = _mm(ht, [da, dg], "nn", BF16, f"ffn{i}_in_dw", bm=1024, bn=1408, bk=T // 2, out_slots=N_SHARD)
        token = exchange(f"ffn{i}", [g_w_in, _slot_rows(g_w_out)])
        dxin, dxin_c, g_norm = _mm_dx_norm([da, dg], w[f"ffn_w_in{i}"], xin, norm, dxout, f"ffn{i}_in_dx", after=token)
        return dxin, dxin_c, (g_norm, dw8)

    h0, h0t = _rowwise_fwd(_fn_rms, "ret_norm", [full(x, D)], [], [(w["ret_norm"], D)], [(D, D, BF16)], bm, S,
                           transposed=(0,))
    proj = _mm(h0, w["ret_w_in"], "nn", BF16, "ret_in", after=w["started"], cols_outer=True)
    HQ, HV = RET_HEADS * RET_QK, RET_HEADS * RET_V
    rope_rows = [(proj, 2 * HQ + HV, 0, LANES)]
    q_r, k_r, v_r = _rowwise_fwd(_fn_ret_rope, "ret_rope", rope_rows, [cos_r, sin_r], [],
                                 [(HQ, LANES, BF16), (HQ, LANES, BF16), (HV, LANES, BF16)], bm, S)
    ret_o = _ret_attn_fwd(q_r, k_r, v_r, B, S)
    gate_rows = [full(ret_o, HV, RET_V), (proj, HV, 2, RET_V)]
    y0, y0t = _rowwise_fwd(_fn_ret_gate, "ret_gate", gate_rows, [], [(w["ret_gn"], RET_V)], [(HV, RET_V, BF16)], 128, S,
                           transposed=(0,))
    w.update(late("ret_out", y0))
    x1, h1, h1t = _mm_out_norm(y0, w["ret_w_out"], x, w["ffn_norm"][0:1], "ret_out")
    (x2, h2, _), ffn0_saved = ffn_fwd(x1, h1, h1t, 0, w["mla_norm"])

    w.update(late("mla", x2))
    proj2 = _mm(h2, w["mla_w_in"], "nn", F32, "mla_in", bm=2048)
    lat_consts = [(w["mla_q_norm"], LANES), (w["mla_kv_norm"], LANES)]
    cqn, ckvn, kr = _rowwise_fwd(_fn_mla_lat, "mla_latent_norm", [full(proj2, MLA_IN_PAD, LANES)], [], lat_consts,
                                 [(MLA_Q_RANK, LANES, BF16), (MLA_KV_RANK, LANES, BF16), (LANES, LANES, F32)], bm, S)
    qf = _mm(cqn, w["mla_w_qb"], "nn", BF16, "mla_qb", bm=2048, bn=2048)
    kvf = _mm(ckvn, w["mla_w_kvb"], "nn", BF16, "mla_kvb", bm=2048, bn=2048)
    HP, HVm = MLA_HEADS * MLA_PAD, MLA_HEADS * MLA_V
    head_rows = [full(qf, HP, LANES), full(kvf, HP, LANES), full(kr, LANES)]
    head_consts = [(w["mla_q_head_norm"], LANES), (w["mla_k_head_norm"], LANES)]
    q_a, k_a, v_a = _rowwise_fwd(_fn_mla_heads, "mla_heads", head_rows, [cos_m, sin_m], head_consts,
                                 [(HP, LANES, BF16), (HP, LANES, BF16), (HVm, LANES, BF16)], bm, S)
    att_o, lse = _mla_attn_fwd(q_a, k_a, v_a, B, S)
    x3, h3, h3t = _mm_out_norm(att_o, w["mla_w_out"], x2, w["ffn_norm"][1:2], "mla_out")
    (x4,), ffn1_saved = ffn_fwd(x3, h3, h3t, 1, None)

    dy, dy_c, loss = _loss_head(x4, target)

    dx3, dx3_c, (g_n1, dw8_1) = ffn_bwd(dy, dy_c, ffn1_saved, 1)

    d_att_o = _mm(dx3_c, w["mla_w_out"], "nt", F32, "mla_out_dx", bm=2048)
    g_mla_out = _mm(att_o, dx3_c, "tn", BF16, "mla_out_dw")
    dq_a, dk_a, dv_a = _mla_attn_bwd(q_a, k_a, v_a, att_o, d_att_o, lse, B, S)
    (dqf, dkvf, dkr), (g["mla_q_head_norm"], g["mla_k_head_norm"]) = _rowwise_bwd(
        _fn_mla_heads, "mla_heads_bwd", head_rows, [cos_m, sin_m], head_consts,
        [(dq_a, LANES), (dk_a, LANES), (dv_a, LANES)], 128, S, grad_dtypes=[BF16, BF16, F32])
    dcqn = _mm(dqf, w["mla_w_qb"], "nt", F32, "mla_qb_dx", bm=2048)
    g_qb = _mm(cqn, dqf, "tn", BF16, "mla_qb_dw")
    g_qb = _to_slots(_unpad_heads(g_qb, 1), 1).reshape(N_SHARD, MLA_Q_RANK, -1)
    dckvn = _mm(dkvf, w["mla_w_kvb"], "nt", F32, "mla_kvb_dx", bm=2048)
    g_kvb = _mm(ckvn, dkvf, "tn", BF16, "mla_kvb_dw", bn=512, out_slots=N_SHARD)
    (dproj2,), (g["mla_q_norm"], g["mla_kv_norm"]) = _rowwise_bwd(
        _fn_mla_lat, "mla_latent_norm_bwd", [full(proj2, MLA_IN_PAD, LANES)], [], lat_consts,
        [(dcqn, LANES), (dckvn, LANES), (dkr, LANES)], bm, S, grad_dtypes=[BF16])
    g_mla_in = _mm(h2, dproj2, "tn", BF16, "mla_in_dw")
    token = exchange("mla", [_slot_rows(g_mla_in[:, :MLA_IN]), g_qb, g_kvb, _slot_rows(g_mla_out)])
    dx2, dx2_c, g["mla_norm"] = _mm_dx_norm([dproj2], w["mla_w_in"], x2, w["mla_norm"], dx3, "mla_in_dx", bm=512,
                                            after=token)

    dx1, dx1_c, (g_n0, dw8_0) = ffn_bwd(dx2, dx2_c, ffn0_saved, 0)

    dy0 = _mm(dx1_c, w["ret_w_out"], "nt", F32, "ret_out_dx")
    g_ret_out = _mm(y0t, dx1_c, "nn", BF16, "ret_out_dw", bm=1024, bn=512, bk=T)
    (d_ret_o, dgate), (g["ret_gn"],) = _rowwise_bwd(_fn_ret_gate, "ret_gate_bwd", gate_rows, [], [(w["ret_gn"], RET_V)],
                                                    [(dy0, RET_V)], 128, S, grad_dtypes=[F32, BF16])
    dq_r, dk_r, dv_r = _ret_attn_bwd(q_r, k_r, v_r, d_ret_o, B, S)
    (dqkv,), _ = _rowwise_bwd(_fn_ret_rope, "ret_rope_bwd", rope_rows, [cos_r, sin_r], [],
                              [(dq_r, LANES), (dk_r, LANES), (dv_r, LANES)], bm, S, grad_dtypes=[BF16], linear=True)
    dx, _, g["ret_norm"] = _mm_dx_norm([dqkv, dgate], w["ret_w_in"], x, w["ret_norm"], dx1, "ret_in_dx")
    g["ffn_norm"] = jnp.concatenate([g_n0, g_n1], axis=0)
    g["ffn_conv_w"] = jnp.stack([dw8_0[0:3], dw8_1[0:3]])
    g["ffn_conv_b"] = jnp.stack([dw8_0[3], dw8_1[3]])
    reduced_small = reduce_small(g)
    g_ret_in = _mm(h0t, [dqkv, dgate], "nn", BF16, "ret_in_dw", bn=512, bk=T, out_slots=N_SHARD, after=reduced_small)
    exchange("ret", [g_ret_in, _slot_rows(g_ret_out)])
    return loss, dx, reduced_small


_BIG = [("ret_w_in", 2), ("ret_w_out", 1), ("mla_w_in", 1), ("mla_w_qb", 2), ("mla_w_kvb", 2), ("mla_w_out", 1),
        ("ffn_w_in", 2), ("ffn_w_out", 1)]
_SMALL_SHARDED = [("ret_gn", 2), ("mla_norm", 1), ("mla_q_norm", 1), ("mla_kv_norm", 1), ("ffn_conv_w", 2)]
_SMALL_REPLICATED = ["ret_norm", "mla_q_head_norm", "mla_k_head_norm", "ffn_norm", "ffn_conv_b"]
_SMALL_ALL = ["ret_norm", "ret_gn", "mla_norm", "mla_q_norm", "mla_kv_norm", "mla_q_head_norm", "mla_k_head_norm",
              "ffn_norm", "ffn_conv_w", "ffn_conv_b"]


def _to_slots(full, axis):
    shape = full.shape
    split = shape[:axis] + (N_SHARD, shape[axis] // N_SHARD) + shape[axis + 1:]
    return jnp.moveaxis(full.reshape(split), axis, 0).reshape(N_SHARD, -1)


def _from_slots(slots, shard_shape, axis):
    parts = jnp.moveaxis(slots.reshape((N_SHARD,) + tuple(shard_shape)), 0, axis)
    full = shard_shape[:axis] + (N_SHARD * shard_shape[axis],) + shard_shape[axis + 1:]
    return parts.reshape(full)


def _pad_rows(flat, cols, row_unit):
    n, L = flat.shape
    unit = cols * row_unit
    Lp = -(-L // unit) * unit
    if Lp != L:
        flat = jnp.concatenate([flat, jnp.zeros((n, Lp - L), flat.dtype)], axis=1)
    return flat.reshape(n, Lp // cols, cols)


def _pad_heads(a, axis):
    shape = a.shape
    a = a.reshape(shape[:axis] + (MLA_HEADS, MLA_QK) + shape[axis + 1:])
    pad = [(0, 0)] * a.ndim
    pad[axis + 1] = (0, MLA_PAD - MLA_QK)
    return jnp.pad(a, pad).reshape(shape[:axis] + (MLA_HEADS * MLA_PAD,) + shape[axis + 1:])


def _unpad_heads(a, axis):
    shape = a.shape
    a = a.reshape(shape[:axis] + (MLA_HEADS, MLA_PAD) + shape[axis + 1:])
    a = lax.slice_in_dim(a, 0, MLA_QK, axis=axis + 1)
    return a.reshape(shape[:axis] + (MLA_HEADS * MLA_QK,) + shape[axis + 1:])


def kernel(x, ret_norm, ret_w_in, ret_gn, ret_w_out, mla_norm, mla_w_in, mla_q_norm, mla_w_qb, mla_kv_norm, mla_w_kvb, mla_q_head_norm, mla_k_head_norm, mla_w_out, ffn_norm, ffn_w_in, ffn_conv_w, ffn_conv_b, ffn_w_out, loss_target, m_ret_norm, m_ret_w_in, m_ret_gn, m_ret_w_out, m_mla_norm, m_mla_w_in, m_mla_q_norm, m_mla_w_qb, m_mla_kv_norm, m_mla_w_kvb, m_mla_q_head_norm, m_mla_k_head_norm, m_mla_w_out, m_ffn_norm, m_ffn_w_in, m_ffn_conv_w, m_ffn_conv_b, m_ffn_w_out, v_ret_norm, v_ret_w_in, v_ret_gn, v_ret_w_out, v_mla_norm, v_mla_w_in, v_mla_q_norm, v_mla_w_qb, v_mla_kv_norm, v_mla_w_kvb, v_mla_q_head_norm, v_mla_k_head_norm, v_mla_w_out, v_ffn_norm, v_ffn_w_in, v_ffn_conv_w, v_ffn_conv_b, v_ffn_w_out):
    names = ["ret_norm", "ret_w_in", "ret_gn", "ret_w_out", "mla_norm", "mla_w_in", "mla_q_norm", "mla_w_qb",
             "mla_kv_norm", "mla_w_kvb", "mla_q_head_norm", "mla_k_head_norm", "mla_w_out", "ffn_norm", "ffn_w_in",
             "ffn_conv_w", "ffn_conv_b", "ffn_w_out"]
    shard = dict(zip(names, [ret_norm, ret_w_in, ret_gn, ret_w_out, mla_norm, mla_w_in, mla_q_norm, mla_w_qb,
                             mla_kv_norm, mla_w_kvb, mla_q_head_norm, mla_k_head_norm, mla_w_out, ffn_norm, ffn_w_in,
                             ffn_conv_w, ffn_conv_b, ffn_w_out]))
    mom_m = dict(zip(names, [m_ret_norm, m_ret_w_in, m_ret_gn, m_ret_w_out, m_mla_norm, m_mla_w_in, m_mla_q_norm,
                             m_mla_w_qb, m_mla_kv_norm, m_mla_w_kvb, m_mla_q_head_norm, m_mla_k_head_norm, m_mla_w_out,
                             m_ffn_norm, m_ffn_w_in, m_ffn_conv_w, m_ffn_conv_b, m_ffn_w_out]))
    mom_v = dict(zip(names, [v_ret_norm, v_ret_w_in, v_ret_gn, v_ret_w_out, v_mla_norm, v_mla_w_in, v_mla_q_norm,
                             v_mla_w_qb, v_mla_kv_norm, v_mla_w_kvb, v_mla_q_head_norm, v_mla_k_head_norm, v_mla_w_out,
                             v_ffn_norm, v_ffn_w_in, v_ffn_conv_w, v_ffn_conv_b, v_ffn_w_out]))
    B, S, D = x.shape
    T = B * S
    sx, sy = lax.axis_index("x"), lax.axis_index("y")
    me = 2 * sx + sy

    two_d = lambda a: a.reshape(-1, a.shape[-1])
    small_sizes = [int(np.prod(shard[n].shape)) for n, _ in _SMALL_SHARDED]
    small = jnp.concatenate([shard[n].reshape(1, -1) for n, _ in _SMALL_SHARDED], axis=1)
    small = _pad_rows(small, LANES, 8)[0]
    as_mxu = lambda a: two_d(a).astype(BF16)
    is_me = lax.broadcasted_iota(jnp.int32, (N_SHARD, 1, 1), 0) == me
    with_own = lambda gathered, own: jnp.where(is_me, own[None], gathered)
    by_cols = lambda a: jnp.moveaxis(a, 0, 1).reshape(a.shape[1], -1)
    by_rows = lambda a: a.reshape(-1, a.shape[-1])
    pad_in = lambda a: jnp.pad(by_rows(a), ((0, 0), (0, MLA_IN_PAD - MLA_IN)))
    pad_qb = lambda a: _pad_heads(by_cols(a), 1)
    ret_in_shard = as_mxu(shard["ret_w_in"])
    g_ret_in, gsmall = _all_gather_weights([ret_in_shard], small)
    later = [
        ("ret_out", [("ret_w_out", as_mxu(shard["ret_w_out"]), by_rows)]),
        ("ffn0", [("ffn_w_in0", as_mxu(shard["ffn_w_in"][0]), by_cols), ("ffn_w_out0", as_mxu(shard["ffn_w_out"][0]), by_rows)]),
        ("mla", [("mla_w_in", as_mxu(shard["mla_w_in"]), pad_in), ("mla_w_qb", as_mxu(shard["mla_w_qb"]), pad_qb),
                 ("mla_w_kvb", as_mxu(shard["mla_w_kvb"]), by_cols), ("mla_w_out", as_mxu(shard["mla_w_out"]), by_rows)]),
        ("ffn1", [("ffn_w_in1", as_mxu(shard["ffn_w_in"][1]), by_cols), ("ffn_w_out1", as_mxu(shard["ffn_w_out"][1]), by_rows)]),
    ]
    gathering = {}
    token = gsmall
    for group, items in later:
        shards = [s_ for _, s_, _ in items]
        lands = [lax.empty((N_SHARD,) + s_.shape, s_.dtype) for s_ in shards]
        send_sems, recv_sems, shards, lands, token = _exchange_start(
            _weight_copies, shards, lands, 3 * len(shards), f"weights_start_{group}", after=token)
        gathering[group] = (send_sems, recv_sems, shards, lands, items)

    def late(group, after):
        send_sems, recv_sems, shards, lands, items = gathering[group]
        shards, lands = _exchange_wait(_weight_copies, send_sems, recv_sems, shards, lands, after,
                                       f"weights_wait_{group}")
        return {key: full(with_own(l_, s_)) for (key, _, full), s_, l_ in zip(items, shards, lands)}

    gsmall = with_own(gsmall, small).reshape(N_SHARD, -1)
    wfull = {}
    off = 0
    for (n, ax), sz in zip(_SMALL_SHARDED, small_sizes):
        wfull[n] = _from_slots(gsmall[:, off:off + sz], shard[n].shape, ax)
        off += sz
    for n in _SMALL_REPLICATED:
        wfull[n] = shard[n]

    conv8 = jnp.concatenate([wfull["ffn_conv_w"], wfull["ffn_conv_b"][:, None, :],
                             jnp.zeros((2, 4, FFN_DIM), F32)], axis=1)
    w = {
        "started": token, "ret_norm": wfull["ret_norm"], "ret_w_in": by_cols(with_own(g_ret_in, ret_in_shard)),
        "ret_gn": wfull["ret_gn"].reshape(1, RET_HEADS * RET_V), "mla_norm": wfull["mla_norm"],
        "mla_q_norm": wfull["mla_q_norm"], "mla_kv_norm": wfull["mla_kv_norm"],
        "mla_q_head_norm": jnp.pad(wfull["mla_q_head_norm"], ((0, 0), (0, MLA_PAD - MLA_QK))),
        "mla_k_head_norm": jnp.pad(wfull["mla_k_head_norm"], ((0, 0), (0, MLA_PAD - MLA_QK))),
        "ffn_norm": wfull["ffn_norm"], "ffn_conv8": conv8,
    }

    started = {}

    def exchange(group, arrays):
        lands = [lax.empty((N_PEERS, p.shape[1] // 2, p.shape[2]), p.dtype) for p in arrays]
        send_sems, recv_sems, ps, lands, token = _exchange_start(
            _grad_copies, arrays, lands, N_PEERS * len(arrays), f"grads_start_{group}")
        started[group] = (send_sems, recv_sems, ps, lands)
        return token

    small_shapes = {
        "ret_norm": (1, D_MODEL), "ret_gn": (1, RET_HEADS, RET_V), "mla_norm": (1, D_MODEL),
        "mla_q_norm": (1, MLA_Q_RANK), "mla_kv_norm": (1, MLA_KV_RANK), "mla_q_head_norm": (1, MLA_QK),
        "mla_k_head_norm": (1, MLA_QK), "ffn_norm": (2, D_MODEL), "ffn_conv_w": (2, 3, FFN_DIM),
        "ffn_conv_b": (2, FFN_DIM)}

    def reduce_small(gl):
        gl = dict(gl, mla_q_head_norm=gl["mla_q_head_norm"][:, :MLA_QK], mla_k_head_norm=gl["mla_k_head_norm"][:, :MLA_QK])
        packed = jnp.concatenate([gl[n].reshape(1, -1) for n in _SMALL_ALL], axis=1)
        return _all_reduce_small(_pad_rows(packed, LANES, 8)[0])

    loss_part, dx, gsm = _local_step(x.reshape(T, D), loss_target.reshape(T, D), w, B, S, late, exchange,
                                     reduce_small)
    loss = lax.psum(loss_part, ("x", "y", "c"))

    delta, new_m, new_v, grads = {}, {}, {}, {}

    def reduced(group, after):
        send_sems, recv_sems, ps, lands = started[group]
        ps, lands = _exchange_wait(_grad_copies, send_sems, recv_sems, ps, lands, after, f"grads_wait_{group}")
        halves = [_sum_partials(p_, l_, f"grads_sum_{group}_{i}") for i, (p_, l_) in enumerate(zip(ps, lands))]
        return [two_d(r) for r in _sibling_share(halves, f"grads_share_{group}")]

    def adamw(n, g_):
        shp = shard[n].shape
        grads[n] = g_.reshape(shp)
        flat = lambda a: a.reshape(-1, shp[-1])
        d_, m_, v_ = _adamw(flat(shard[n]), flat(grads[n]), flat(mom_m[n]), flat(mom_v[n]), f"adamw_{n}")
        delta[n], new_m[n], new_v[n] = d_.reshape(shp), m_.reshape(shp), v_.reshape(shp)
        return d_

    ffn1 = reduced("ffn1", started["ret"][2][0])
    mla = reduced("mla", ffn1[0])
    ffn0 = reduced("ffn0", mla[0])
    early = [adamw(n, g_) for n, g_ in zip(["mla_w_in", "mla_w_qb", "mla_w_kvb", "mla_w_out"], mla)]
    early.append(adamw("ffn_w_in", jnp.stack([ffn0[0], ffn1[0]])))
    early.append(adamw("ffn_w_out", jnp.stack([ffn0[1], ffn1[1]])))
    ret = reduced("ret", jnp.stack([d_[0, 0] for d_ in early]))
    adamw("ret_w_in", ret[0])
    adamw("ret_w_out", ret[1])

    gsm = gsm.reshape(-1)
    sharded_axis = dict(_SMALL_SHARDED)
    off = 0
    for n in _SMALL_ALL:
        sz = int(np.prod(small_shapes[n]))
        gn = gsm[off:off + sz].reshape(small_shapes[n])
        off += sz
        if n in sharded_axis:
            ax = sharded_axis[n]
            width = shard[n].shape[ax]
            gn = lax.dynamic_slice_in_dim(gn, me * width, width, axis=ax)
        grads[n] = gn

    pack_small = lambda d: _pad_rows(jnp.concatenate([d[n].reshape(1, -1) for n in _SMALL_ALL], axis=1), LANES, 8)[0]
    d_, m_, v_ = _adamw(pack_small(shard), pack_small(grads), pack_small(mom_m), pack_small(mom_v), "adamw_small")
    off = 0
    for n in _SMALL_ALL:
        sz = int(np.prod(shard[n].shape))
        for dst, src in ((delta, d_), (new_m, m_), (new_v, v_)):
            dst[n] = src.reshape(-1)[off:off + sz].reshape(shard[n].shape)
        off += sz

    return (loss, dx.reshape(B, S, D), *[grads[n] for n in names], *[delta[n] for n in names],
            *[new_m[n] for n in names], *[new_v[n] for n in names])
```

```python
import functools
import math

import numpy as np
import jax
import jax.numpy as jnp
from jax import lax
from jax.experimental import pallas as pl
from jax.experimental.pallas import tpu as pltpu

F32 = jnp.float32
BF16 = jnp.bfloat16
MXU_DTYPE = jnp.bfloat16

CHUNK = 64
RMS_EPS = 1e-6
ROPE_THETA = 10000.0
D_MODEL = 1024
RET_HEADS = 4
RET_QK = 256
RET_V = 512
RET_GAMMA_BASE = -5.0
MLA_HEADS = 8
MLA_Q_RANK = 384
MLA_KV_RANK = 256
MLA_NOPE = 128
MLA_ROPE = 64
MLA_V = 128
MLA_QK = MLA_NOPE + MLA_ROPE
MLA_PAD = 256
MLA_IN = MLA_Q_RANK + MLA_KV_RANK + MLA_ROPE
MLA_IN_PAD = MLA_IN + 64
MASK_VALUE = -1e30
FFN_DIM = 2816
ADAM_LR = 0.001
ADAM_B1 = 0.9
ADAM_B2 = 0.999
ADAM_EPS = 1e-08
ADAM_WD = 0.01
ADAM_STEP = 10

LANES = 128
ATT_BLOCK = 256
MLA_FWD_BLOCK = 512
VMEM_LIMIT = 56 * 2 ** 20
N_SHARD = 4
N_DEV = 8

MESH = pl.DeviceIdType.MESH


def _params(sem=None, **kw):
    return pltpu.CompilerParams(dimension_semantics=sem, vmem_limit_bytes=VMEM_LIMIT, **kw)


def _pick(dim, target):
    if dim <= target:
        return dim
    best = None
    for d in range(LANES, target + 1, LANES):
        if dim % d == 0:
            best = d
    assert best is not None, (dim, target)
    return best


def _mm(a, b, dims, out_dtype, name, residual=None, bm=512, bn=1024, bk=2048, out_slots=None, after=None,
        cols_outer=False):
    a_parts = list(a) if isinstance(a, (list, tuple)) else [a]
    b_parts = list(b) if isinstance(b, (list, tuple)) else [b]
    parts_on_n = dims == "tn" or len(b_parts) > 1
    if parts_on_n:
        assert len(a_parts) == 1 and dims in ("tn", "nn")
        (K, M) = a_parts[0].shape if dims == "tn" else a_parts[0].shape[::-1]
        N = sum(p.shape[1] for p in b_parts)
        part_widths = [p.shape[1] for p in b_parts]
    else:
        assert len(b_parts) == 1
        M = a_parts[0].shape[0]
        K = sum(p.shape[1] for p in a_parts)
        N = b_parts[0].shape[1 if dims == "nn" else 0]
        part_widths = [p.shape[1] for p in a_parts]
    bm, bn, bk = _pick(M, bm), _pick(N, bn), _pick(K, min(bk, 1024) if dims == "tn" else bk)
    nk = K // bk
    unit = bn if parts_on_n else bk
    assert all(wd % unit == 0 for wd in part_widths), (name, part_widths, unit)
    bounds = np.cumsum([0] + [wd // unit for wd in part_widths])
    ranges = [(int(lo), int(hi)) for lo, hi in zip(bounds[:-1], bounds[1:])]

    def part_index(idx, lo, hi):
        return jnp.clip(idx - lo, 0, hi - lo - 1)

    if parts_on_n:
        if dims == "tn":
            a_specs = [pl.BlockSpec((bk, bm), lambda i, j, k: (k, i))]
            dn = (((0,), (0,)), ((), ()))
        else:
            a_specs = [pl.BlockSpec((bm, bk), lambda i, j, k: (i, k))]
            dn = (((1,), (0,)), ((), ()))
        b_specs = [pl.BlockSpec((bk, bn), functools.partial(lambda i, j, k, lo, hi: (k, part_index(j, lo, hi)), lo=lo, hi=hi))
                   for lo, hi in ranges]
    else:
        a_specs = [pl.BlockSpec((bm, bk), functools.partial(lambda i, j, k, lo, hi: (i, part_index(k, lo, hi)), lo=lo, hi=hi))
                   for lo, hi in ranges]
        if dims == "nt":
            b_specs = [pl.BlockSpec((bn, bk), lambda i, j, k: (j, k))]
        else:
            b_specs = [pl.BlockSpec((bk, bn), lambda i, j, k: (k, j))]
        dn = (((1,), (1 if dims == "nt" else 0,)), ((), ()))
    r_spec = pl.BlockSpec((bm, bn), lambda i, j, k: (i, j))
    if out_slots is None:
        o_spec, o_shape = r_spec, (M, N)
    else:
        ns = N // out_slots
        assert ns % bn == 0, (name, ns, bn)
        nbs = ns // bn
        o_spec = pl.BlockSpec((None, bm, bn), lambda i, j, k: (j // nbs, i, j % nbs))
        o_shape = (out_slots, M, ns)
    has_res = residual is not None
    na, nb = len(a_parts), len(b_parts)

    def body(*refs):
        a_refs, b_refs = refs[:na], refs[na:na + nb]
        r_ref = refs[na + nb] if has_res else None
        n_in = na + nb + has_res + (after is not None)
        o_ref = refs[n_in]
        acc_ref = refs[n_in + 1] if nk > 1 else None
        k = pl.program_id(2)

        def finish(acc):
            if has_res:
                acc = acc + r_ref[...].astype(F32)
            o_ref[...] = acc.astype(out_dtype)

        def compute(a_ref, b_ref):
            p = lax.dot_general(a_ref[...].astype(MXU_DTYPE), b_ref[...].astype(MXU_DTYPE), dn,
                                preferred_element_type=F32)
            if nk == 1:
                finish(p)
                return

            @pl.when(k == 0)
            def _():
                acc_ref[...] = p

            @pl.when(jnp.logical_and(k > 0, k < nk - 1))
            def _():
                acc_ref[...] += p

            @pl.when(k == nk - 1)
            def _():
                finish(acc_ref[...] + p)

        if len(ranges) == 1:
            compute(a_refs[0], b_refs[0])
        else:
            idx = pl.program_id(0 if cols_outer else 1) if parts_on_n else k
            for p, (lo, hi) in enumerate(ranges):
                @pl.when(jnp.logical_and(idx >= lo, idx < hi))
                def _(p=p):
                    compute(a_refs[0 if parts_on_n else p], b_refs[p if parts_on_n else 0])

    after_specs = [] if after is None else [pl.BlockSpec(after.shape, lambda i, j, k: (0, 0))]
    in_specs = a_specs + b_specs + ([r_spec] if has_res else []) + after_specs
    grid = (M // bm, N // bn, nk)
    if cols_outer:
        swap = lambda sp: pl.BlockSpec(sp.block_shape, functools.partial(lambda j, i, k, f: f(i, j, k), f=sp.index_map))
        in_specs, o_spec, grid = [swap(sp) for sp in in_specs], swap(o_spec), (grid[1], grid[0], nk)
    return pl.pallas_call(
        body, name=name, grid=grid,
        in_specs=in_specs, out_specs=o_spec,
        out_shape=jax.ShapeDtypeStruct(o_shape, out_dtype),
        scratch_shapes=[pltpu.VMEM((bm, bn), F32)] if nk > 1 else [],
        compiler_params=_params(("parallel", "parallel", "arbitrary")),
    )(*a_parts, *b_parts, *((residual,) if has_res else ()), *(() if after is None else (after,)))


def _mm_out_norm(a, w, residual, gain, name, bm=512):
    (M, K), N = a.shape, w.shape[1]
    bm = _pick(M, bm)

    def body(a_ref, w_ref, r_ref, g_ref, o_ref, h_ref, ht_ref):
        acc = lax.dot_general(a_ref[...].astype(MXU_DTYPE), w_ref[...].astype(MXU_DTYPE), _NN,
                              preferred_element_type=F32) + r_ref[...]
        o_ref[...] = acc
        hv = _fn_rms([[acc]], [], [[g_ref[...]]])[0][0]
        h_ref[...] = hv.astype(h_ref.dtype)
        ht_ref[...] = hv.T.astype(ht_ref.dtype)

    row = pl.BlockSpec((bm, N), lambda i: (i, 0))
    whole = lambda arr: pl.BlockSpec(arr.shape, lambda i: (0, 0))
    return pl.pallas_call(
        body, name=name, grid=(M // bm,),
        in_specs=[pl.BlockSpec((bm, K), lambda i: (i, 0)), whole(w), row, whole(gain)],
        out_specs=[row, row, pl.BlockSpec((N, bm), lambda i: (0, i))],
        out_shape=[jax.ShapeDtypeStruct((M, N), F32), jax.ShapeDtypeStruct((M, N), BF16),
                   jax.ShapeDtypeStruct((N, M), BF16)],
        compiler_params=_params(("parallel",)),
    )(a, w, residual, gain)


def _mm_dx_norm(a_parts, w, x, gain, add, name, bm=256, after=None):
    M = a_parts[0].shape[0]
    N, K = w.shape
    widths = [p.shape[1] for p in a_parts]
    assert sum(widths) == K, (name, widths, K)
    offs = [int(o) for o in np.cumsum([0] + widths[:-1])]
    bm = _pick(M, bm)
    na = len(a_parts)
    n_in = na + 4 + (after is not None)

    def body(*refs):
        w_ref, x_ref, g_ref, add_ref = refs[na:na + 4]
        dx_ref, dxc_ref, dg_ref = refs[n_in:n_in + 3]
        dh = None
        for a_ref, off, wd in zip(refs[:na], offs, widths):
            p = lax.dot_general(a_ref[...].astype(MXU_DTYPE), w_ref[:, off:off + wd].astype(MXU_DTYPE), _NT,
                                preferred_element_type=F32)
            dh = p if dh is None else dh + p
        _, vjp = jax.vjp(lambda xv, gv: _fn_rms([[xv]], [], [[gv]])[0][0], x_ref[...], g_ref[...])
        dxv, dgv = vjp(dh)
        dxv = dxv + add_ref[...]
        dx_ref[...] = dxv
        dxc_ref[...] = dxv.astype(dxc_ref.dtype)

        @pl.when(pl.program_id(0) == 0)
        def _():
            dg_ref[...] = dgv

        @pl.when(pl.program_id(0) > 0)
        def _():
            dg_ref[...] += dgv

    row = pl.BlockSpec((bm, N), lambda i: (i, 0))
    whole = lambda a: pl.BlockSpec(a.shape, lambda i: (0, 0))
    in_specs = [pl.BlockSpec((bm, wd), lambda i: (i, 0)) for wd in widths] + [whole(w), row, whole(gain), row]
    in_specs += [] if after is None else [whole(after)]
    return pl.pallas_call(
        body, name=name, grid=(M // bm,),
        in_specs=in_specs, out_specs=[row, row, whole(gain)],
        out_shape=[jax.ShapeDtypeStruct((M, N), F32), jax.ShapeDtypeStruct((M, N), BF16),
                   jax.ShapeDtypeStruct(gain.shape, F32)],
        compiler_params=_params(("arbitrary",)),
    )(*a_parts, w, x, gain, add, *(() if after is None else (after,)))


def _tiles(ref, width, tile):
    return [ref[:, t * tile:(t + 1) * tile].astype(F32) for t in range(width // tile)]


def _row_specs(rows, pos, consts, bm, S):
    npos_blocks = S // bm
    specs = [pl.BlockSpec((bm, w), functools.partial(lambda i, c: (i, c), c=cb)) for (_, w, cb, _) in rows]
    specs += [pl.BlockSpec((bm, p.shape[1]), lambda i: (i % npos_blocks, 0)) for p in pos]
    specs += [pl.BlockSpec(c.shape, lambda i: (0, 0)) for (c, _) in consts]
    return specs


def _rowwise_fwd(fn, name, rows, pos, consts, outs, bm, S, transposed=()):
    T = rows[0][0].shape[0]
    nr, npos, nc, no = len(rows), len(pos), len(consts), len(outs)

    def body(*refs):
        row_v = [_tiles(r, w, t) for r, (_, w, _, t) in zip(refs[:nr], rows)]
        pos_v = [r[...] for r in refs[nr:nr + npos]]
        const_v = [_tiles(r, c.shape[1], t) for r, (c, t) in zip(refs[nr + npos:nr + npos + nc], consts)]
        res = fn(row_v, pos_v, const_v)
        out_refs = refs[nr + npos + nc:]
        for o_ref, tiles, (w, t, dt) in zip(out_refs, res, outs):
            for k, v in enumerate(tiles):
                o_ref[:, k * t:(k + 1) * t] = v.astype(dt)
        for t_ref, a in zip(out_refs[no:], transposed):
            t = outs[a][1]
            for k, v in enumerate(res[a]):
                t_ref[k * t:(k + 1) * t, :] = v.T.astype(t_ref.dtype)

    return pl.pallas_call(
        body, name=name, grid=(T // bm,),
        in_specs=_row_specs(rows, pos, consts, bm, S),
        out_specs=[pl.BlockSpec((bm, w), lambda i: (i, 0)) for (w, _, _) in outs]
        + [pl.BlockSpec((outs[a][0], bm), lambda i: (0, i)) for a in transposed],
        out_shape=[jax.ShapeDtypeStruct((T, w), dt) for (w, _, dt) in outs]
        + [jax.ShapeDtypeStruct((outs[a][0], T), BF16) for a in transposed],
        compiler_params=_params(("parallel",)),
    )(*[r[0] for r in rows], *pos, *[c[0] for c in consts])


def _rowwise_bwd(fn, name, rows, pos, consts, cts, bm, S, adds=None, grad_dtypes=None, mxu_copies=(), linear=False):
    adds = adds or {}
    T = rows[0][0].shape[0]
    nr, npos, nc, nct = len(rows), len(pos), len(consts), len(cts)
    add_idx = sorted(adds)
    grad_dtypes = grad_dtypes or [F32] * nr

    def body(*refs):
        it = iter(refs)
        row_refs = [None if linear else next(it) for _ in range(nr)]
        pos_refs = [next(it) for _ in range(npos)]
        const_refs = [next(it) for _ in range(nc)]
        ct_refs = [next(it) for _ in range(nct)]
        add_refs = {k: next(it) for k in add_idx}
        drow_refs = [next(it) for _ in range(nr)]
        copy_refs = {a: next(it) for a in mxu_copies}
        dconst_refs = [next(it) for _ in range(nc)]
        if linear:
            row_v = [[jnp.zeros((bm, t), F32)] * (w // t) for (_, w, _, t) in rows]
        else:
            row_v = [_tiles(r, w, t) for r, (_, w, _, t) in zip(row_refs, rows)]
        pos_v = [r[...] for r in pos_refs]
        const_v = [_tiles(r, c.shape[1], t) for r, (c, t) in zip(const_refs, consts)]
        ct_v = [_tiles(r, c.shape[1], t) for r, (c, t) in zip(ct_refs, cts)]
        _, vjp = jax.vjp(lambda rv, cv: fn(rv, pos_v, cv), row_v, const_v)
        drows, dconsts = vjp(ct_v)
        for a, (d_ref, tiles, (_, w, _, t)) in enumerate(zip(drow_refs, drows, rows)):
            for k, v in enumerate(tiles):
                if a in add_refs:
                    v = v + add_refs[a][:, k * t:(k + 1) * t].astype(F32)
                d_ref[:, k * t:(k + 1) * t] = v.astype(d_ref.dtype)
                if a in copy_refs:
                    copy_refs[a][:, k * t:(k + 1) * t] = v.astype(BF16)
        first = pl.program_id(0) == 0
        for d_ref, tiles, (_, t) in zip(dconst_refs, dconsts, consts):
            for k, v in enumerate(tiles):
                @pl.when(first)
                def _(d_ref=d_ref, k=k, t=t, v=v):
                    d_ref[:, k * t:(k + 1) * t] = v

                @pl.when(jnp.logical_not(first))
                def _(d_ref=d_ref, k=k, t=t, v=v):
                    d_ref[:, k * t:(k + 1) * t] += v

    in_specs = _row_specs([] if linear else rows, pos, consts, bm, S)
    in_specs += [pl.BlockSpec((bm, c.shape[1]), lambda i: (i, 0)) for (c, _) in cts]
    in_specs += [pl.BlockSpec((bm, adds[k].shape[1]), lambda i: (i, 0)) for k in add_idx]
    out_specs = [pl.BlockSpec((bm, w), lambda i: (i, 0)) for (_, w, _, _) in rows]
    out_specs += [pl.BlockSpec((bm, rows[a][1]), lambda i: (i, 0)) for a in mxu_copies]
    out_specs += [pl.BlockSpec(c.shape, lambda i: (0, 0)) for (c, _) in consts]
    out_shape = [jax.ShapeDtypeStruct((T, w), dt) for (_, w, _, _), dt in zip(rows, grad_dtypes)]
    out_shape += [jax.ShapeDtypeStruct((T, rows[a][1]), BF16) for a in mxu_copies]
    out_shape += [jax.ShapeDtypeStruct(c.shape, F32) for (c, _) in consts]
    res = pl.pallas_call(
        body, name=name, grid=(T // bm,),
        in_specs=in_specs, out_specs=out_specs, out_shape=out_shape,
        compiler_params=_params(("arbitrary",)),
    )(*([] if linear else [r[0] for r in rows]), *pos, *[c[0] for c in consts], *[c[0] for c in cts],
      *[adds[k] for k in add_idx])
    n_rows = nr + len(mxu_copies)
    return res[:n_rows], res[n_rows:]


def _ssq(tiles):
    s = jnp.sum(tiles[0] * tiles[0], axis=-1, keepdims=True)
    for t in tiles[1:]:
        s = s + jnp.sum(t * t, axis=-1, keepdims=True)
    return s


def _sigmoid(x):
    return 0.5 * jnp.tanh(0.5 * x) + 0.5


def _fn_rms(rows, pos, consts):
    (x,), (g,) = rows[0], consts[0]
    r = lax.rsqrt(jnp.mean(x * x, axis=-1, keepdims=True) + RMS_EPS)
    return [[x * r * g]]


def _fn_ret_rope(rows, pos, consts):
    (qkv,) = rows
    nq = RET_HEADS * RET_QK // LANES
    q, k, v = qkv[:nq], qkv[nq:2 * nq], qkv[2 * nq:]
    cos, sin = pos

    def rot(t, scale):
        out = []
        for h in range(RET_HEADS):
            x1, x2 = t[2 * h], t[2 * h + 1]
            o1, o2 = x1 * cos - x2 * sin, x2 * cos + x1 * sin
            out += [o1, o2] if scale is None else [o1 * scale, o2 * scale]
        return out

    return [rot(q, None), rot(k, RET_QK ** -0.5), list(v)]


def _fn_ret_gate(rows, pos, consts):
    o, g = rows
    (gn,) = consts
    out = []
    for h in range(RET_HEADS):
        r = lax.rsqrt(jnp.mean(o[h] * o[h], axis=-1, keepdims=True) + RMS_EPS)
        out.append((o[h] * r * gn[h]) * (g[h] * _sigmoid(g[h])))
    return [out]


def _fn_mla_lat(rows, pos, consts):
    (p,) = rows
    gq, gkv = consts
    nq, nkv = MLA_Q_RANK // LANES, MLA_KV_RANK // LANES
    cq, ckv, kr = p[:nq], p[nq:nq + nkv], p[nq + nkv]
    rq = lax.rsqrt(_ssq(cq) / MLA_Q_RANK + RMS_EPS)
    rkv = lax.rsqrt(_ssq(ckv) / MLA_KV_RANK + RMS_EPS)
    return [[t * rq * g for t, g in zip(cq, gq)], [t * rkv * g for t, g in zip(ckv, gkv)], [kr]]


def _swap32_impl(x):
    lane = lax.broadcasted_iota(jnp.int32, x.shape, 1)
    up, down = pltpu.roll(x, LANES - 32, 1), pltpu.roll(x, 32, 1)
    return jnp.where(lane < 32, up, jnp.where(lane < 64, down, 0.0))


@jax.custom_vjp
def _swap32(x):
    return _swap32_impl(x)


_swap32.defvjp(lambda x: (_swap32_impl(x), None), lambda _, g: (_swap32_impl(g),))


def _fn_mla_heads(rows, pos, consts):
    qf, kvf, (kr,) = rows
    cos, sin = pos
    gq, gk = consts
    q_out, k_out, v_out = [], [], []
    for h in range(MLA_HEADS):
        q0, q1 = qf[2 * h], qf[2 * h + 1]
        r = lax.rsqrt(_ssq([q0, q1]) / MLA_QK + RMS_EPS)
        a0, a1 = q0 * r * gq[0], q1 * r * gq[1]
        a1 = a1 * cos + _swap32(a1) * sin
        q_out += [a0 * (MLA_QK ** -0.5), a1 * (MLA_QK ** -0.5)]
        k0 = kvf[2 * h]
        r = lax.rsqrt(_ssq([k0, kr]) / MLA_QK + RMS_EPS)
        b0, b1 = k0 * r * gk[0], kr * r * gk[1]
        k_out += [b0, b1 * cos + _swap32(b1) * sin]
        v_out.append(kvf[2 * h + 1])
    return [q_out, k_out, v_out]


def _shift_down(x, n):
    row = lax.broadcasted_iota(jnp.int32, x.shape, 0)
    return jnp.where(row >= n, pltpu.roll(x, n, 0), 0.0)


def _shift_up(x, n):
    rows = x.shape[0]
    row = lax.broadcasted_iota(jnp.int32, x.shape, 0)
    return jnp.where(row < rows - n, pltpu.roll(x, rows - n, 0), 0.0)


def _conv_blocks(S):
    cb = 256
    return cb, FFN_DIM // cb


def _conv_fwd(ag, w8, B, S, name):
    cb, ncb = _conv_blocks(S)

    def body(a_ref, g_ref, w_ref, u_ref, ut_ref):
        g = g_ref[...].astype(F32)
        w = w_ref[...]
        gc = w[0:1] * _shift_down(g, 2) + w[1:2] * _shift_down(g, 1) + w[2:3] * g + w[3:4]
        u = a_ref[...].astype(F32) * (gc * _sigmoid(gc))
        u_ref[...] = u.astype(u_ref.dtype)
        ut_ref[...] = u.T.astype(ut_ref.dtype)

    return pl.pallas_call(
        body, name=name, grid=(ncb, B),
        in_specs=[pl.BlockSpec((S, cb), lambda j, b: (b, j)),
                  pl.BlockSpec((S, cb), lambda j, b: (b, ncb + j)),
                  pl.BlockSpec((8, cb), lambda j, b: (0, j))],
        out_specs=[pl.BlockSpec((S, cb), lambda j, b: (b, j)), pl.BlockSpec((cb, S), lambda j, b: (j, b))],
        out_shape=[jax.ShapeDtypeStruct((B * S, FFN_DIM), BF16), jax.ShapeDtypeStruct((FFN_DIM, B * S), BF16)],
        compiler_params=_params(("parallel", "parallel")),
    )(ag, ag, w8)


def _conv_bwd(ag, w8, du, B, S, name):
    cb, ncb = _conv_blocks(S)

    def body(a_ref, g_ref, w_ref, du_ref, da_ref, dg_ref, dw_ref):
        g = g_ref[...].astype(F32)
        w = w_ref[...]
        g1, g2 = _shift_down(g, 1), _shift_down(g, 2)
        gc = w[0:1] * g2 + w[1:2] * g1 + w[2:3] * g + w[3:4]
        sg = _sigmoid(gc)
        du_v = du_ref[...]
        da_ref[...] = (du_v * (gc * sg)).astype(da_ref.dtype)
        dgc = du_v * a_ref[...].astype(F32) * (sg * (1.0 + gc * (1.0 - sg)))
        dg = w[2:3] * dgc + w[1:2] * _shift_up(dgc, 1) + w[0:1] * _shift_up(dgc, 2)
        dg_ref[...] = dg.astype(dg_ref.dtype)
        part = jnp.concatenate([
            jnp.sum(dgc * g2, axis=0, keepdims=True), jnp.sum(dgc * g1, axis=0, keepdims=True),
            jnp.sum(dgc * g, axis=0, keepdims=True), jnp.sum(dgc, axis=0, keepdims=True),
            jnp.zeros((4, cb), F32)], axis=0)

        @pl.when(pl.program_id(1) == 0)
        def _():
            dw_ref[...] = part

        @pl.when(pl.program_id(1) > 0)
        def _():
            dw_ref[...] += part

    blk = lambda j, b: (b, j)
    return pl.pallas_call(
        body, name=name, grid=(ncb, B),
        in_specs=[pl.BlockSpec((S, cb), blk),
                  pl.BlockSpec((S, cb), lambda j, b: (b, ncb + j)),
                  pl.BlockSpec((8, cb), lambda j, b: (0, j)),
                  pl.BlockSpec((S, cb), blk)],
        out_specs=[pl.BlockSpec((S, cb), blk), pl.BlockSpec((S, cb), blk),
                   pl.BlockSpec((8, cb), lambda j, b: (0, j))],
        out_shape=[jax.ShapeDtypeStruct((B * S, FFN_DIM), BF16), jax.ShapeDtypeStruct((B * S, FFN_DIM), BF16),
                   jax.ShapeDtypeStruct((8, FFN_DIM), F32)],
        compiler_params=_params(("parallel", "arbitrary")),
    )(ag, ag, w8, du)


_NT = (((1,), (1,)), ((), ()))
_NN = (((1,), (0,)), ((), ()))
_TN = (((0,), (0,)), ((), ()))


def _dot(a, b, dn):
    return lax.dot_general(a.astype(MXU_DTYPE), b.astype(MXU_DTYPE), dn, preferred_element_type=F32)


def _rel_and_mask():
    il = lax.broadcasted_iota(jnp.int32, (ATT_BLOCK, ATT_BLOCK), 0)
    jl = lax.broadcasted_iota(jnp.int32, (ATT_BLOCK, ATT_BLOCK), 1)
    return (il - jl).astype(F32), (jl // CHUNK) <= (il // CHUNK)


def _rows(i):
    return pl.ds(pl.multiple_of(i * ATT_BLOCK, ATT_BLOCK), ATT_BLOCK)


def _run_bits(n):
    bits, b = [], 1
    while b < n:
        bits.append(b)
        b *= 2
    return bits[::-1]


def _key_runs(n, nq, update):
    for bit in _run_bits(nq + 1):
        @pl.when((n & bit) != 0)
        def _(bit=bit):
            update(n & ~(2 * bit - 1), bit, (n & (bit - 1)) == 0)


def _earlier_runs(n, nq, update):
    for bit in _run_bits(nq):
        @pl.when((n & bit) != 0)
        def _(bit=bit):
            update(n & ~(2 * bit - 1), bit, False)


def _chunk_visible(shape, nblk, blk):
    key = lax.broadcasted_iota(jnp.int32, shape, 0) - (nblk - 1) * blk
    query = lax.broadcasted_iota(jnp.int32, shape, 1)
    return jnp.logical_or(key < 0, (key // CHUNK) <= (query // CHUNK))


KV_UNROLL = 2


def _kv_loop(n, body, carry):
    main = n // KV_UNROLL

    def chunk(t, c):
        for u in range(KV_UNROLL):
            c = body(t * KV_UNROLL + u, c)
        return c

    carry = lax.fori_loop(0, main, chunk, carry)
    return lax.fori_loop(main * KV_UNROLL, n, body, carry)


def _mla_attn_fwd(q, k, v, B, S):
    blk = min(MLA_FWD_BLOCK, S)
    H, nq = MLA_HEADS, S // blk

    def body(q_ref, k_ref, v_ref, o_ref, lse_ref, m_ref, l_ref, acc_ref):
        def qblock(i, _):
            q_rows = pl.ds(pl.multiple_of(i * blk, blk), blk)
            qi = q_ref[q_rows, :]
            m_ref[...] = jnp.full(m_ref.shape, MASK_VALUE, F32)
            l_ref[...] = jnp.zeros(l_ref.shape, F32)
            acc_ref[...] = jnp.zeros(acc_ref.shape, F32)

            def keys(first, nblk, last):
                rows = pl.ds(pl.multiple_of(first * blk, blk), nblk * blk)
                s = _dot(k_ref[rows, :], qi, _NT)
                s = jnp.where(jnp.logical_or(_chunk_visible(s.shape, nblk, blk), jnp.logical_not(last)), s, MASK_VALUE)
                m = m_ref[...]
                m2 = jnp.maximum(m, jnp.max(s, axis=0, keepdims=True))
                alpha = jnp.exp(m - m2)
                p = jnp.exp(s - m2)
                l_ref[...] = alpha * l_ref[...] + jnp.sum(p, axis=0, keepdims=True)
                acc_ref[...] = alpha * acc_ref[...] + _dot(v_ref[rows, :], p, _TN)
                m_ref[...] = m2

            _key_runs(i + 1, nq, keys)
            l = l_ref[...]
            o_ref[q_rows, :] = (acc_ref[...] / l).T
            lse_ref[0, :, q_rows] = m_ref[...] + jnp.log(l)
            return 0

        lax.fori_loop(0, nq, qblock, 0)

    return pl.pallas_call(
        body, name="mla_attn_fwd", grid=(B, H),
        in_specs=[pl.BlockSpec((S, MLA_PAD), lambda b, h: (b, h)),
                  pl.BlockSpec((S, MLA_PAD), lambda b, h: (b, h)),
                  pl.BlockSpec((S, MLA_V), lambda b, h: (b, h))],
        out_specs=[pl.BlockSpec((S, MLA_V), lambda b, h: (b, h)),
                   pl.BlockSpec((1, 1, S), lambda b, h: (b * H + h, 0, 0))],
        out_shape=[jax.ShapeDtypeStruct((B * S, H * MLA_V), F32), jax.ShapeDtypeStruct((B * H, 1, S), F32)],
        scratch_shapes=[pltpu.VMEM((1, blk), F32), pltpu.VMEM((1, blk), F32), pltpu.VMEM((MLA_V, blk), F32)],
        compiler_params=_params(("parallel", "parallel")),
    )(q, k, v)


def _mla_attn_bwd(q, k, v, o, do, lse, B, S):
    blk = min(MLA_FWD_BLOCK, S)
    H, nq = MLA_HEADS, S // blk

    def body(q_ref, k_ref, v_ref, o_ref, do_ref, lse_ref, dq_ref, dk_ref, dv_ref, kt_ref, dqt_ref):
        dk_ref[...] = jnp.zeros(dk_ref.shape, F32)
        dv_ref[...] = jnp.zeros(dv_ref.shape, F32)
        for g in range(nq):
            kt_ref[g] = k_ref[g * blk:(g + 1) * blk, :].T

        def qblock(i, _):
            q_rows = pl.ds(pl.multiple_of(i * blk, blk), blk)
            qi = q_ref[q_rows, :]
            doi = do_ref[q_rows, :]
            delta = jnp.sum((doi * o_ref[q_rows, :]).T, axis=0, keepdims=True)
            lse_i = lse_ref[0, :, q_rows]
            doi = doi.astype(MXU_DTYPE)
            dqt_ref[...] = jnp.zeros(dqt_ref.shape, F32)

            def keys(first, nblk, last):
                rows = pl.ds(pl.multiple_of(first * blk, blk), nblk * blk)
                k_run, v_run = k_ref[rows, :], v_ref[rows, :]
                p = jnp.exp(_dot(k_run, qi, _NT) - lse_i)
                p = jnp.where(jnp.logical_or(_chunk_visible(p.shape, nblk, blk), jnp.logical_not(last)), p, 0.0)
                ds = (p * (_dot(v_run, doi, _NT) - delta)).astype(MXU_DTYPE)
                dk_ref[rows, :] += _dot(ds, qi, _NN)
                dv_ref[rows, :] += _dot(p, doi, _NN)
                for r in range(nblk):
                    dqt_ref[...] += _dot(kt_ref[first + r], ds[r * blk:(r + 1) * blk, :], _NN)

            _key_runs(i + 1, nq, keys)
            dq_ref[q_rows, :] = dqt_ref[...].T
            return 0

        lax.fori_loop(0, nq, qblock, 0)

    qk_spec = pl.BlockSpec((S, MLA_PAD), lambda b, h: (b, h))
    v_spec = pl.BlockSpec((S, MLA_V), lambda b, h: (b, h))
    return pl.pallas_call(
        body, name="mla_attn_bwd", grid=(B, H),
        in_specs=[qk_spec, qk_spec, v_spec, v_spec, v_spec,
                  pl.BlockSpec((1, 1, S), lambda b, h: (b * H + h, 0, 0))],
        out_specs=[qk_spec, qk_spec, v_spec],
        out_shape=[jax.ShapeDtypeStruct((B * S, H * MLA_PAD), F32), jax.ShapeDtypeStruct((B * S, H * MLA_PAD), F32),
                   jax.ShapeDtypeStruct((B * S, H * MLA_V), F32)],
        scratch_shapes=[pltpu.VMEM((nq, MLA_PAD, blk), q.dtype), pltpu.VMEM((MLA_PAD, blk), F32)],
        compiler_params=_params(("parallel", "parallel")),
    )(q, k, v, o, do, lse)


def _ret_log_gamma():
    lg = np.log1p(-np.exp2(RET_GAMMA_BASE - np.arange(RET_HEADS, dtype=np.float32))).astype(np.float32)
    return jnp.asarray(np.broadcast_to(lg[:, None, None], (RET_HEADS, 8, LANES)).copy())


RET_BLOCK = 512


def _ret_local_scale(lg, shape, blk, rising):
    local = lax.broadcasted_iota(jnp.int32, shape, 0) % blk
    return jnp.exp(lg * (local if rising else blk - 1 - local).astype(F32))


def _ret_pair_factor(lg, blk, steps):
    return jnp.exp(lg * (blk * (steps - 1) + 1).astype(F32))


def _ret_own_decay(lg, blk, transposed):
    a = lax.broadcasted_iota(jnp.int32, (blk, blk), 0)
    b = lax.broadcasted_iota(jnp.int32, (blk, blk), 1)
    query, key = (b, a) if transposed else (a, b)
    dec = jnp.exp(lg * jnp.abs(query - key).astype(F32))
    return jnp.where((key // CHUNK) <= (query // CHUNK), dec, 0.0)


def _ret_attn_fwd(q, k, v, B, S):
    blk = min(RET_BLOCK, S)
    H, nq = RET_HEADS, S // blk

    def body(lg_ref, q_ref, k_ref, v_ref, o_ref, ks_ref, dec_ref, acc_ref):
        lg = lg_ref[0, 0:1, 0:1]
        ks_ref[...] = (k_ref[...].astype(F32) * _ret_local_scale(lg, k_ref.shape, blk, False)).astype(ks_ref.dtype)
        dec_ref[...] = _ret_own_decay(lg, blk, False)

        def qblock(i, _):
            q_rows = pl.ds(pl.multiple_of(i * blk, blk), blk)
            qi = q_ref[q_rows, :]
            qs = (qi.astype(F32) * _ret_local_scale(lg, qi.shape, blk, True)).astype(qi.dtype)
            a = _dot(qi, k_ref[q_rows, :], _NT) * dec_ref[...]
            acc_ref[...] = _dot(a, v_ref[q_rows, :], _NN)

            def keys(first, nblk, _):
                rows = pl.ds(pl.multiple_of(first * blk, blk), nblk * blk)
                steps = i - first - lax.broadcasted_iota(jnp.int32, (1, nblk * blk), 1) // blk
                a = _dot(qs, ks_ref[rows, :], _NT) * _ret_pair_factor(lg, blk, steps)
                acc_ref[...] += _dot(a, v_ref[rows, :], _NN)

            _earlier_runs(i, nq, keys)
            o_ref[q_rows, :] = acc_ref[...]
            return 0

        lax.fori_loop(0, nq, qblock, 0)

    qk_spec = pl.BlockSpec((S, RET_QK), lambda b, h: (b, h))
    v_spec = pl.BlockSpec((S, RET_V), lambda b, h: (b, h))
    return pl.pallas_call(
        body, name="ret_attn_fwd", grid=(B, H),
        in_specs=[pl.BlockSpec((1, 8, LANES), lambda b, h: (h, 0, 0)), qk_spec, qk_spec, v_spec],
        out_specs=v_spec,
        out_shape=jax.ShapeDtypeStruct((B * S, H * RET_V), F32),
        scratch_shapes=[pltpu.VMEM((S, RET_QK), k.dtype), pltpu.VMEM((blk, blk), F32), pltpu.VMEM((blk, RET_V), F32)],
        compiler_params=_params(("parallel", "parallel")),
    )(_ret_log_gamma(), q, k, v)


def _ret_attn_bwd(q, k, v, do, B, S):
    blk = min(RET_BLOCK, S)
    H, nq = RET_HEADS, S // blk

    def body(lg_ref, q_ref, k_ref, v_ref, do_ref, dq_ref, dk_ref, dv_ref, ks_ref, kst_ref, dks_ref, dqt_ref, dec_ref):
        lg = lg_ref[0, 0:1, 0:1]
        dk_ref[...] = jnp.zeros(dk_ref.shape, F32)
        dv_ref[...] = jnp.zeros(dv_ref.shape, F32)
        dks_ref[...] = jnp.zeros(dks_ref.shape, F32)
        ks_ref[...] = (k_ref[...].astype(F32) * _ret_local_scale(lg, k_ref.shape, blk, False)).astype(ks_ref.dtype)
        for g in range(nq):
            kst_ref[g] = ks_ref[g * blk:(g + 1) * blk, :].T
        dec_ref[...] = _ret_own_decay(lg, blk, True)

        def qblock(i, _):
            q_rows = pl.ds(pl.multiple_of(i * blk, blk), blk)
            qi = q_ref[q_rows, :]
            q_scale = _ret_local_scale(lg, qi.shape, blk, True)
            qs = (qi.astype(F32) * q_scale).astype(qi.dtype)
            doi = do_ref[q_rows, :].astype(MXU_DTYPE)
            ki = k_ref[q_rows, :]
            dec = dec_ref[...]
            a = _dot(ki, qi, _NT) * dec
            da = (_dot(v_ref[q_rows, :], doi, _NT) * dec).astype(MXU_DTYPE)
            dv_ref[q_rows, :] += _dot(a, doi, _NN)
            dk_ref[q_rows, :] += _dot(da, qi, _NN)
            dq_own = _dot(da, ki, _TN)
            dqt_ref[...] = jnp.zeros(dqt_ref.shape, F32)

            def keys(first, nblk, _):
                for r in range(nblk):
                    g = first + r
                    rows = pl.ds(pl.multiple_of(g * blk, blk), blk)
                    c = _ret_pair_factor(lg, blk, i - g)
                    a = _dot(ks_ref[rows, :], qs, _NT) * c
                    da = (_dot(v_ref[rows, :], doi, _NT) * c).astype(MXU_DTYPE)
                    dv_ref[rows, :] += _dot(a, doi, _NN)
                    dks_ref[rows, :] += _dot(da, qs, _NN)
                    dqt_ref[...] += _dot(kst_ref[g], da, _NN)

            _earlier_runs(i, nq, keys)
            dq_ref[q_rows, :] = dqt_ref[...].T * q_scale + dq_own
            return 0

        lax.fori_loop(0, nq, qblock, 0)
        dk_ref[...] += dks_ref[...] * _ret_local_scale(lg, dks_ref.shape, blk, False)

    qk_spec = pl.BlockSpec((S, RET_QK), lambda b, h: (b, h))
    v_spec = pl.BlockSpec((S, RET_V), lambda b, h: (b, h))
    return pl.pallas_call(
        body, name="ret_attn_bwd", grid=(B, H),
        in_specs=[pl.BlockSpec((1, 8, LANES), lambda b, h: (h, 0, 0)), qk_spec, qk_spec, v_spec, v_spec],
        out_specs=[qk_spec, qk_spec, v_spec],
        out_shape=[jax.ShapeDtypeStruct((B * S, H * RET_QK), F32), jax.ShapeDtypeStruct((B * S, H * RET_QK), F32),
                   jax.ShapeDtypeStruct((B * S, H * RET_V), F32)],
        scratch_shapes=[pltpu.VMEM((S, RET_QK), k.dtype), pltpu.VMEM((nq, RET_QK, blk), k.dtype),
                        pltpu.VMEM((S, RET_QK), F32), pltpu.VMEM((RET_QK, blk), F32), pltpu.VMEM((blk, blk), F32)],
        compiler_params=_params(("parallel", "parallel")),
    )(_ret_log_gamma(), q, k, v, do)


def _loss_head(y, target, bm=512):
    T, D = y.shape
    bm = _pick(T, bm)

    def body(y_ref, t_ref, dy_ref, dyc_ref, l_ref):
        err = y_ref[...] - t_ref[...]
        dy_ref[...] = err / D
        dyc_ref[...] = (err / D).astype(dyc_ref.dtype)
        part = jnp.full((8, LANES), 0.5 * jnp.sum(jnp.mean(err * err, axis=-1)), F32)

        @pl.when(pl.program_id(0) == 0)
        def _():
            l_ref[...] = part

        @pl.when(pl.program_id(0) > 0)
        def _():
            l_ref[...] += part

    blk = pl.BlockSpec((bm, D), lambda i: (i, 0))
    dy, dyc, l = pl.pallas_call(
        body, name="loss_head", grid=(T // bm,),
        in_specs=[blk, blk], out_specs=[blk, blk, pl.BlockSpec((8, LANES), lambda i: (0, 0))],
        out_shape=[jax.ShapeDtypeStruct((T, D), F32), jax.ShapeDtypeStruct((T, D), BF16),
                   jax.ShapeDtypeStruct((8, LANES), F32)],
        compiler_params=_params(("arbitrary",)),
    )(y, target)
    return dy, dyc, l[0, 0]


def _adamw(w, g, m, v, name):
    R, C = w.shape
    br = R if R * C * 4 <= 2 ** 21 else _pick_rows(R, max(8, (2 ** 21) // (C * 4)))

    def body(w_ref, g_ref, m_ref, v_ref, d_ref, mo_ref, vo_ref):
        g_v = g_ref[...]
        m_v = ADAM_B1 * m_ref[...] + (1.0 - ADAM_B1) * g_v
        v_v = ADAM_B2 * v_ref[...] + (1.0 - ADAM_B2) * (g_v * g_v)
        m_hat = m_v / (1.0 - ADAM_B1 ** ADAM_STEP)
        v_hat = v_v / (1.0 - ADAM_B2 ** ADAM_STEP)
        d_ref[...] = -ADAM_LR * (m_hat / (jnp.sqrt(v_hat) + ADAM_EPS) + ADAM_WD * w_ref[...])
        mo_ref[...] = m_v
        vo_ref[...] = v_v

    blk = pl.BlockSpec((br, C), lambda i: (i, 0))
    return pl.pallas_call(
        body, name=name, grid=(R // br,),
        in_specs=[blk] * 4, out_specs=[blk] * 3,
        out_shape=[jax.ShapeDtypeStruct((R, C), F32)] * 3,
        compiler_params=_params(("parallel",)),
    )(w, g, m, v)


def _pick_rows(R, target):
    best = None
    for d in range(8, min(R, target) + 1, 8):
        if R % d == 0:
            best = d
    assert best is not None, (R, target)
    return best


def _position():
    return lax.axis_index("x"), lax.axis_index("y"), lax.axis_index("c")


HBM_SPEC = pl.BlockSpec(memory_space=pltpu.HBM)


def _other_chips(x, y):
    return [(1 - x, y), (x, 1 - y), (1 - x, 1 - y)]


def _all_gather_weights(bigs, small):
    nb = len(bigs)

    def body(*refs):
        big_refs, small_ref = refs[:nb], refs[nb]
        obig, osmall = refs[nb + 1:2 * nb + 1], refs[2 * nb + 1]
        ici_send, ici_recv, d2d_send, d2d_recv, sm_send, sm_recv = refs[2 * nb + 2:]
        x, y, c = _position()
        me = 2 * x + y
        chips = _other_chips(x, y)

        def rows(n, half):
            rh = bigs[n].shape[0] // 2
            return pl.ds(half * rh, rh)

        def over_ici(n, j, slot, from_shard):
            px, py = chips[j]
            dst = obig[n].at[slot, rows(n, c)]
            return pltpu.make_async_remote_copy(
                src_ref=big_refs[n].at[rows(n, c)] if from_shard else dst, dst_ref=dst,
                send_sem=ici_send.at[3 * n + j], recv_sem=ici_recv.at[3 * n + j],
                device_id=(px, py, c), device_id_type=MESH)

        def over_d2d(n, j, half):
            px, py = chips[j]
            part = obig[n].at[2 * px + py, rows(n, half)]
            return pltpu.make_async_remote_copy(
                src_ref=part, dst_ref=part, send_sem=d2d_send.at[3 * n + j], recv_sem=d2d_recv.at[3 * n + j],
                device_id=(x, y, 1 - c), device_id_type=MESH)

        def small_copy(j, slot):
            px, py = chips[j]
            return pltpu.make_async_remote_copy(
                src_ref=small_ref, dst_ref=osmall.at[slot], send_sem=sm_send.at[j], recv_sem=sm_recv.at[j],
                device_id=(px, py, c), device_id_type=MESH)

        sends = [over_ici(n, j, me, True) for n in range(nb) for j in range(3)]
        sends += [small_copy(j, me) for j in range(3)]
        for cp in sends:
            cp.start()
        passed = []
        for n in range(nb):
            for j, (px, py) in enumerate(chips):
                over_ici(n, j, 2 * px + py, False).wait_recv()
                fwd = over_d2d(n, j, c)
                fwd.start()
                passed.append(fwd)
        for n in range(nb):
            for j in range(3):
                over_d2d(n, j, 1 - c).wait_recv()
        for j, (px, py) in enumerate(chips):
            small_copy(j, 2 * px + py).wait_recv()
        for cp in sends + passed:
            cp.wait_send()

    dma = pltpu.SemaphoreType.DMA
    return pl.pallas_call(
        body, name="weights_all_gather",
        in_specs=[HBM_SPEC] * (nb + 1), out_specs=[HBM_SPEC] * (nb + 1),
        out_shape=[jax.ShapeDtypeStruct((N_SHARD,) + b.shape, b.dtype) for b in bigs]
        + [jax.ShapeDtypeStruct((N_SHARD,) + small.shape, small.dtype)],
        scratch_shapes=[dma((3 * nb,)), dma((3 * nb,)), dma((3 * nb,)), dma((3 * nb,)), dma((3,)), dma((3,))],
    )(*bigs, small)


SEM_SPEC = pl.BlockSpec(memory_space=pltpu.SEMAPHORE)
DATAFLOW_EFFECT = pltpu.SideEffectType.DATAFLOW_SIDE_EFFECTING
N_PEERS = N_DEV - 1


def _grad_copies(p_refs, land_refs, send_sems, recv_sems):
    x, y, c = _position()
    copies = []
    for a, (p_ref, land_ref) in enumerate(zip(p_refs, land_refs)):
        rh = p_ref.shape[1] // 2
        for k in range(1, N_DEV):
            px = 1 - x if k & 4 else x
            py = 1 - y if k & 2 else y
            pc = 1 - c if k & 1 else c
            copies.append(pltpu.make_async_remote_copy(
                src_ref=p_ref.at[2 * px + py, pl.ds(pc * rh, rh)], dst_ref=land_ref.at[k - 1],
                send_sem=send_sems.at[N_PEERS * a + k - 1], recv_sem=recv_sems.at[N_PEERS * a + k - 1],
                device_id=(px, py, pc), device_id_type=MESH))
    return copies


def _weight_copies(w_refs, land_refs, send_sems, recv_sems):
    x, y, c = _position()
    copies = []
    for a, (w_ref, land_ref) in enumerate(zip(w_refs, land_refs)):
        for j, (px, py) in enumerate(_other_chips(x, y)):
            copies.append(pltpu.make_async_remote_copy(
                src_ref=w_ref, dst_ref=land_ref.at[2 * x + y], send_sem=send_sems.at[3 * a + j],
                recv_sem=recv_sems.at[3 * a + j], device_id=(px, py, c), device_id_type=MESH))
    return copies


def _exchange_start(make_copies, srcs, lands, n_sems, name, after=None):
    n, m = len(srcs), len(lands)
    n_in = n + m + (after is not None)

    def body(*refs):
        send_sems, recv_sems, token = refs[n_in], refs[n_in + 1], refs[-1]
        for cp in make_copies(refs[:n], refs[n:n + m], send_sems, recv_sems):
            cp.start()
        token[...] = jnp.zeros(token.shape, token.dtype)

    hbm = lambda a: pltpu.with_memory_space_constraint(a, pltpu.HBM)
    dma = pltpu.SemaphoreType.DMA
    res = pl.pallas_call(
        body, name=name,
        in_specs=[HBM_SPEC] * (n + m) + ([] if after is None else [pl.BlockSpec(memory_space=pl.ANY)]),
        out_specs=[SEM_SPEC, SEM_SPEC] + [HBM_SPEC] * (n + m) + [pl.BlockSpec(memory_space=pltpu.VMEM)],
        out_shape=[dma((n_sems,)), dma((n_sems,))] + [pltpu.HBM(a.shape, a.dtype) for a in list(srcs) + list(lands)]
        + [jax.ShapeDtypeStruct((8, LANES), F32)],
        input_output_aliases={i: 2 + i for i in range(n + m)},
        compiler_params=pltpu.CompilerParams(has_side_effects=DATAFLOW_EFFECT),
    )(*[hbm(a) for a in srcs], *[hbm(a) for a in lands], *(() if after is None else (after,)))
    return res[0], res[1], list(res[2:2 + n]), list(res[2 + n:2 + n + m]), res[-1]


def _exchange_wait(make_copies, send_sems, recv_sems, srcs, lands, after, name):
    n, m = len(srcs), len(lands)

    def body(*refs):
        for cp in make_copies(refs[:n], refs[n:n + m], refs[n + m], refs[n + m + 1]):
            cp.wait_send()
            cp.wait_recv()

    res = pl.pallas_call(
        body, name=name,
        in_specs=[HBM_SPEC] * (n + m) + [SEM_SPEC, SEM_SPEC, pl.BlockSpec(memory_space=pl.ANY)],
        out_specs=[HBM_SPEC] * (n + m),
        out_shape=[pltpu.HBM(a.shape, a.dtype) for a in list(srcs) + list(lands)],
        input_output_aliases={i: i for i in range(n + m)},
        compiler_params=pltpu.CompilerParams(has_side_effects=DATAFLOW_EFFECT),
    )(*srcs, *lands, send_sems, recv_sems, after)
    return list(res[:n]), list(res[n:])


def _sum_partials(p, land, name):
    _, rh, cols = land.shape
    br = _pick_rows(rh, 256)
    nrb = rh // br
    x, y, c = _position()
    where = jnp.stack([2 * x + y, c]).astype(jnp.int32)

    def body(where_ref, p_ref, land_ref, o_ref):
        acc = p_ref[...].astype(F32)
        for k in range(N_PEERS):
            acc = acc + land_ref[k].astype(F32)
        o_ref[...] = acc

    return pl.pallas_call(
        body, name=name,
        grid_spec=pltpu.PrefetchScalarGridSpec(
            num_scalar_prefetch=1, grid=(nrb,),
            in_specs=[pl.BlockSpec((None, br, cols), lambda r, where_ref: (where_ref[0], where_ref[1] * nrb + r, 0)),
                      pl.BlockSpec((N_PEERS, br, cols), lambda r, where_ref: (0, r, 0))],
            out_specs=pl.BlockSpec((None, br, cols), lambda r, where_ref: (where_ref[1], r, 0))),
        out_shape=jax.ShapeDtypeStruct((2, rh, cols), F32),
        compiler_params=_params(("parallel",)),
    )(where, p, land)


def _sibling_share(fulls, name):
    n = len(fulls)

    def body(*refs):
        o_refs = refs[n:2 * n]
        send_sems, recv_sems = refs[2 * n:]
        x, y, c = _position()

        def copy(a, half):
            return pltpu.make_async_remote_copy(
                src_ref=o_refs[a].at[half], dst_ref=o_refs[a].at[half], send_sem=send_sems.at[a],
                recv_sem=recv_sems.at[a], device_id=(x, y, 1 - c), device_id_type=MESH)

        sends = [copy(a, c) for a in range(n)]
        for cp in sends:
            cp.start()
        for a in range(n):
            copy(a, 1 - c).wait_recv()
        for cp in sends:
            cp.wait_send()

    dma = pltpu.SemaphoreType.DMA
    return pl.pallas_call(
        body, name=name,
        in_specs=[HBM_SPEC] * n, out_specs=[HBM_SPEC] * n,
        out_shape=[jax.ShapeDtypeStruct(f.shape, f.dtype) for f in fulls],
        input_output_aliases={a: a for a in range(n)},
        scratch_shapes=[dma((n,)), dma((n,))],
    )(*fulls)


def _all_reduce_small(v):
    R, cols = v.shape

    def body(v_ref, o_ref, buf_ref, send_sems, recv_sems):
        x, y, c = _position()
        me = 4 * x + 2 * y + c
        buf_ref[me] = v_ref[...]
        sends = []
        for k in range(1, N_DEV):
            px = 1 - x if k & 4 else x
            py = 1 - y if k & 2 else y
            pc = 1 - c if k & 1 else c
            sends.append(pltpu.make_async_remote_copy(
                src_ref=v_ref, dst_ref=buf_ref.at[me], send_sem=send_sems.at[k - 1], recv_sem=recv_sems.at[k - 1],
                device_id=(px, py, pc), device_id_type=MESH))
        for cp in sends:
            cp.start()
        for k in range(1, N_DEV):
            px = 1 - x if k & 4 else x
            py = 1 - y if k & 2 else y
            pc = 1 - c if k & 1 else c
            pltpu.make_async_remote_copy(
                src_ref=v_ref, dst_ref=buf_ref.at[4 * px + 2 * py + pc], send_sem=send_sems.at[k - 1],
                recv_sem=recv_sems.at[k - 1], device_id=(px, py, pc), device_id_type=MESH).wait_recv()
        for cp in sends:
            cp.wait_send()
        acc = buf_ref[0]
        for d in range(1, N_DEV):
            acc = acc + buf_ref[d]
        o_ref[...] = acc

    return pl.pallas_call(
        body, name="small_grads_all_reduce",
        in_specs=[pl.BlockSpec(memory_space=pltpu.VMEM)], out_specs=pl.BlockSpec(memory_space=pltpu.VMEM),
        out_shape=jax.ShapeDtypeStruct((R, cols), F32),
        scratch_shapes=[pltpu.VMEM((N_DEV, R, cols), F32), pltpu.SemaphoreType.DMA((N_DEV - 1,)),
                        pltpu.SemaphoreType.DMA((N_DEV - 1,))],
    )(v)


def _rope_tables(S, half, width):
    inv_freq = ROPE_THETA ** (-jnp.arange(half, dtype=F32) / half)
    ang = jnp.arange(S).astype(F32)[:, None] * inv_freq[None, :]
    return jnp.cos(ang), jnp.sin(ang)


def _slot_rows(a):
    return a.reshape(N_SHARD, -1, a.shape[-1])


def _local_step(x, target, w, B, S, late, exchange, reduce_small):
    T = B * S
    D = D_MODEL
    bm = 256
    full = lambda a, wd, tile=None: (a, wd, 0, tile or wd)
    g = {}

    cos_r, sin_r = _rope_tables(S, RET_QK // 2, LANES)
    cos_m, sin_m = _rope_tables(S, MLA_ROPE // 2, LANES)
    zeros64 = jnp.zeros((S, 64), F32)
    cos_m = jnp.concatenate([cos_m, cos_m, zeros64], axis=1)
    sin_m = jnp.concatenate([-sin_m, sin_m, zeros64], axis=1)

    def ffn_fwd(xin, h, ht, i, next_gain):
        w.update(late(f"ffn{i}", xin))
        norm = w["ffn_norm"][i:i + 1]
        ag = _mm(h, w[f"ffn_w_in{i}"], "nn", BF16, f"ffn{i}_in", bn=1408, cols_outer=True)
        u, ut = _conv_fwd(ag, w["ffn_conv8"][i], B, S, f"ffn{i}_conv")
        if next_gain is None:
            out = (_mm(u, w[f"ffn_w_out{i}"], "nn", F32, f"ffn{i}_out", residual=xin, bk=FFN_DIM),)
        else:
            out = _mm_out_norm(u, w[f"ffn_w_out{i}"], xin, next_gain, f"ffn{i}_out")
        return out, (xin, norm, ht, ag, ut)

    def ffn_bwd(dxout, dxout_c, saved, i):
        xin, norm, ht, ag, ut = saved
        du = _mm(dxout_c, w[f"ffn_w_out{i}"], "nt", F32, f"ffn{i}_out_dx", bn=1408, cols_outer=True)
        g_w_out = _mm(ut, dxout_c, "nn", BF16, f"ffn{i}_out_dw", bm=1408, bn=512, bk=T)
        da, dg, dw8 = _conv_bwd(ag, w["ffn_conv8"][i], du, B, S, f"ffn{i}_conv_bwd")
        g_w_in = _mm(ht, [da, dg], "nn", BF16, f"ffn{i}_in_dw", bm=1024, bn=1408, bk=T // 2, out_slots=N_SHARD)
        token = exchange(f"ffn{i}", [g_w_in, _slot_rows(g_w_out)])
        dxin, dxin_c, g_norm = _mm_dx_norm([da, dg], w[f"ffn_w_in{i}"], xin, norm, dxout, f"ffn{i}_in_dx", after=token)
        return dxin, dxin_c, (g_norm, dw8)

    h0, h0t = _rowwise_fwd(_fn_rms, "ret_norm", [full(x, D)], [], [(w["ret_norm"], D)], [(D, D, BF16)], bm, S,
                           transposed=(0,))
    proj = _mm(h0, w["ret_w_in"], "nn", BF16, "ret_in", after=w["started"], cols_outer=True)
    HQ, HV = RET_HEADS * RET_QK, RET_HEADS * RET_V
    rope_rows = [(proj, 2 * HQ + HV, 0, LANES)]
    q_r, k_r, v_r = _rowwise_fwd(_fn_ret_rope, "ret_rope", rope_rows, [cos_r, sin_r], [],
                                 [(HQ, LANES, BF16), (HQ, LANES, BF16), (HV, LANES, BF16)], bm, S)
    ret_o = _ret_attn_fwd(q_r, k_r, v_r, B, S)
    gate_rows = [full(ret_o, HV, RET_V), (proj, HV, 2, RET_V)]
    y0, y0t = _rowwise_fwd(_fn_ret_gate, "ret_gate", gate_rows, [], [(w["ret_gn"], RET_V)], [(HV, RET_V, BF16)], 128, S,
                           transposed=(0,))
    w.update(late("ret_out", y0))
    x1, h1, h1t = _mm_out_norm(y0, w["ret_w_out"], x, w["ffn_norm"][0:1], "ret_out")
    (x2, h2, _), ffn0_saved = ffn_fwd(x1, h1, h1t, 0, w["mla_norm"])

    w.update(late("mla", x2))
    proj2 = _mm(h2, w["mla_w_in"], "nn", F32, "mla_in", bm=2048)
    lat_consts = [(w["mla_q_norm"], LANES), (w["mla_kv_norm"], LANES)]
    cqn, ckvn, kr = _rowwise_fwd(_fn_mla_lat, "mla_latent_norm", [full(proj2, MLA_IN_PAD, LANES)], [], lat_consts,
                                 [(MLA_Q_RANK, LANES, BF16), (MLA_KV_RANK, LANES, BF16), (LANES, LANES, F32)], bm, S)
    qf = _mm(cqn, w["mla_w_qb"], "nn", BF16, "mla_qb", bm=2048, bn=2048)
    kvf = _mm(ckvn, w["mla_w_kvb"], "nn", BF16, "mla_kvb", bm=2048, bn=2048)
    HP, HVm = MLA_HEADS * MLA_PAD, MLA_HEADS * MLA_V
    head_rows = [full(qf, HP, LANES), full(kvf, HP, LANES), full(kr, LANES)]
    head_consts = [(w["mla_q_head_norm"], LANES), (w["mla_k_head_norm"], LANES)]
    q_a, k_a, v_a = _rowwise_fwd(_fn_mla_heads, "mla_heads", head_rows, [cos_m, sin_m], head_consts,
                                 [(HP, LANES, BF16), (HP, LANES, BF16), (HVm, LANES, BF16)], bm, S)
    att_o, lse = _mla_attn_fwd(q_a, k_a, v_a, B, S)
    x3, h3, h3t = _mm_out_norm(att_o, w["mla_w_out"], x2, w["ffn_norm"][1:2], "mla_out")
    (x4,), ffn1_saved = ffn_fwd(x3, h3, h3t, 1, None)

    dy, dy_c, loss = _loss_head(x4, target)

    dx3, dx3_c, (g_n1, dw8_1) = ffn_bwd(dy, dy_c, ffn1_saved, 1)

    d_att_o = _mm(dx3_c, w["mla_w_out"], "nt", F32, "mla_out_dx", bm=2048)
    g_mla_out = _mm(att_o, dx3_c, "tn", BF16, "mla_out_dw")
    dq_a, dk_a, dv_a = _mla_attn_bwd(q_a, k_a, v_a, att_o, d_att_o, lse, B, S)
    (dqf, dkvf, dkr), (g["mla_q_head_norm"], g["mla_k_head_norm"]) = _rowwise_bwd(
        _fn_mla_heads, "mla_heads_bwd", head_rows, [cos_m, sin_m], head_consts,
        [(dq_a, LANES), (dk_a, LANES), (dv_a, LANES)], 128, S, grad_dtypes=[BF16, BF16, F32])
    dcqn = _mm(dqf, w["mla_w_qb"], "nt", F32, "mla_qb_dx", bm=2048)
    g_qb = _mm(cqn, dqf, "tn", BF16, "mla_qb_dw")
    g_qb = _to_slots(_unpad_heads(g_qb, 1), 1).reshape(N_SHARD, MLA_Q_RANK, -1)
    dckvn = _mm(dkvf, w["mla_w_kvb"], "nt", F32, "mla_kvb_dx", bm=2048)
    g_kvb = _mm(ckvn, dkvf, "tn", BF16, "mla_kvb_dw", bn=512, out_slots=N_SHARD)
    (dproj2,), (g["mla_q_norm"], g["mla_kv_norm"]) = _rowwise_bwd(
        _fn_mla_lat, "mla_latent_norm_bwd", [full(proj2, MLA_IN_PAD, LANES)], [], lat_consts,
        [(dcqn, LANES), (dckvn, LANES), (dkr, LANES)], bm, S, grad_dtypes=[BF16])
    g_mla_in = _mm(h2, dproj2, "tn", BF16, "mla_in_dw")
    token = exchange("mla", [_slot_rows(g_mla_in[:, :MLA_IN]), g_qb, g_kvb, _slot_rows(g_mla_out)])
    dx2, dx2_c, g["mla_norm"] = _mm_dx_norm([dproj2], w["mla_w_in"], x2, w["mla_norm"], dx3, "mla_in_dx", bm=512,
                                            after=token)

    dx1, dx1_c, (g_n0, dw8_0) = ffn_bwd(dx2, dx2_c, ffn0_saved, 0)

    dy0 = _mm(dx1_c, w["ret_w_out"], "nt", F32, "ret_out_dx")
    g_ret_out = _mm(y0t, dx1_c, "nn", BF16, "ret_out_dw", bm=1024, bn=512, bk=T)
    token = exchange("reto", [_slot_rows(g_ret_out)])
    gn_behind = w["ret_gn"] + token[0:1, 0:1]
    (d_ret_o, dgate), (g["ret_gn"],) = _rowwise_bwd(_fn_ret_gate, "ret_gate_bwd", gate_rows, [], [(gn_behind, RET_V)],
                                                    [(dy0, RET_V)], 128, S, grad_dtypes=[F32, BF16])
    dq_r, dk_r, dv_r = _ret_attn_bwd(q_r, k_r, v_r, d_ret_o, B, S)
    (dqkv,), _ = _rowwise_bwd(_fn_ret_rope, "ret_rope_bwd", rope_rows, [cos_r, sin_r], [],
                              [(dq_r, LANES), (dk_r, LANES), (dv_r, LANES)], bm, S, grad_dtypes=[BF16], linear=True)
    dx, _, g["ret_norm"] = _mm_dx_norm([dqkv, dgate], w["ret_w_in"], x, w["ret_norm"], dx1, "ret_in_dx")
    g["ffn_norm"] = jnp.concatenate([g_n0, g_n1], axis=0)
    g["ffn_conv_w"] = jnp.stack([dw8_0[0:3], dw8_1[0:3]])
    g["ffn_conv_b"] = jnp.stack([dw8_0[3], dw8_1[3]])
    reduced_small = reduce_small(g)
    g_ret_in = _mm(h0t, [dqkv, dgate], "nn", BF16, "ret_in_dw", bn=512, bk=T, out_slots=N_SHARD, after=reduced_small)
    exchange("ret", [g_ret_in])
    return loss, dx, reduced_small


_BIG = [("ret_w_in", 2), ("ret_w_out", 1), ("mla_w_in", 1), ("mla_w_qb", 2), ("mla_w_kvb", 2), ("mla_w_out", 1),
        ("ffn_w_in", 2), ("ffn_w_out", 1)]
_SMALL_SHARDED = [("ret_gn", 2), ("mla_norm", 1), ("mla_q_norm", 1), ("mla_kv_norm", 1), ("ffn_conv_w", 2)]
_SMALL_REPLICATED = ["ret_norm", "mla_q_head_norm", "mla_k_head_norm", "ffn_norm", "ffn_conv_b"]
_SMALL_ALL = ["ret_norm", "ret_gn", "mla_norm", "mla_q_norm", "mla_kv_norm", "mla_q_head_norm", "mla_k_head_norm",
              "ffn_norm", "ffn_conv_w", "ffn_conv_b"]


def _to_slots(full, axis):
    shape = full.shape
    split = shape[:axis] + (N_SHARD, shape[axis] // N_SHARD) + shape[axis + 1:]
    return jnp.moveaxis(full.reshape(split), axis, 0).reshape(N_SHARD, -1)


def _from_slots(slots, shard_shape, axis):
    parts = jnp.moveaxis(slots.reshape((N_SHARD,) + tuple(shard_shape)), 0, axis)
    full = shard_shape[:axis] + (N_SHARD * shard_shape[axis],) + shard_shape[axis + 1:]
    return parts.reshape(full)


def _pad_rows(flat, cols, row_unit):
    n, L = flat.shape
    unit = cols * row_unit
    Lp = -(-L // unit) * unit
    if Lp != L:
        flat = jnp.concatenate([flat, jnp.zeros((n, Lp - L), flat.dtype)], axis=1)
    return flat.reshape(n, Lp // cols, cols)


def _pad_heads(a, axis):
    shape = a.shape
    a = a.reshape(shape[:axis] + (MLA_HEADS, MLA_QK) + shape[axis + 1:])
    pad = [(0, 0)] * a.ndim
    pad[axis + 1] = (0, MLA_PAD - MLA_QK)
    return jnp.pad(a, pad).reshape(shape[:axis] + (MLA_HEADS * MLA_PAD,) + shape[axis + 1:])


def _unpad_heads(a, axis):
    shape = a.shape
    a = a.reshape(shape[:axis] + (MLA_HEADS, MLA_PAD) + shape[axis + 1:])
    a = lax.slice_in_dim(a, 0, MLA_QK, axis=axis + 1)
    return a.reshape(shape[:axis] + (MLA_HEADS * MLA_QK,) + shape[axis + 1:])


def kernel(x, ret_norm, ret_w_in, ret_gn, ret_w_out, mla_norm, mla_w_in, mla_q_norm, mla_w_qb, mla_kv_norm, mla_w_kvb, mla_q_head_norm, mla_k_head_norm, mla_w_out, ffn_norm, ffn_w_in, ffn_conv_w, ffn_conv_b, ffn_w_out, loss_target, m_ret_norm, m_ret_w_in, m_ret_gn, m_ret_w_out, m_mla_norm, m_mla_w_in, m_mla_q_norm, m_mla_w_qb, m_mla_kv_norm, m_mla_w_kvb, m_mla_q_head_norm, m_mla_k_head_norm, m_mla_w_out, m_ffn_norm, m_ffn_w_in, m_ffn_conv_w, m_ffn_conv_b, m_ffn_w_out, v_ret_norm, v_ret_w_in, v_ret_gn, v_ret_w_out, v_mla_norm, v_mla_w_in, v_mla_q_norm, v_mla_w_qb, v_mla_kv_norm, v_mla_w_kvb, v_mla_q_head_norm, v_mla_k_head_norm, v_mla_w_out, v_ffn_norm, v_ffn_w_in, v_ffn_conv_w, v_ffn_conv_b, v_ffn_w_out):
    names = ["ret_norm", "ret_w_in", "ret_gn", "ret_w_out", "mla_norm", "mla_w_in", "mla_q_norm", "mla_w_qb",
             "mla_kv_norm", "mla_w_kvb", "mla_q_head_norm", "mla_k_head_norm", "mla_w_out", "ffn_norm", "ffn_w_in",
             "ffn_conv_w", "ffn_conv_b", "ffn_w_out"]
    shard = dict(zip(names, [ret_norm, ret_w_in, ret_gn, ret_w_out, mla_norm, mla_w_in, mla_q_norm, mla_w_qb,
                             mla_kv_norm, mla_w_kvb, mla_q_head_norm, mla_k_head_norm, mla_w_out, ffn_norm, ffn_w_in,
                             ffn_conv_w, ffn_conv_b, ffn_w_out]))
    mom_m = dict(zip(names, [m_ret_norm, m_ret_w_in, m_ret_gn, m_ret_w_out, m_mla_norm, m_mla_w_in, m_mla_q_norm,
                             m_mla_w_qb, m_mla_kv_norm, m_mla_w_kvb, m_mla_q_head_norm, m_mla_k_head_norm, m_mla_w_out,
                             m_ffn_norm, m_ffn_w_in, m_ffn_conv_w, m_ffn_conv_b, m_ffn_w_out]))
    mom_v = dict(zip(names, [v_ret_norm, v_ret_w_in, v_ret_gn, v_ret_w_out, v_mla_norm, v_mla_w_in, v_mla_q_norm,
                             v_mla_w_qb, v_mla_kv_norm, v_mla_w_kvb, v_mla_q_head_norm, v_mla_k_head_norm, v_mla_w_out,
                             v_ffn_norm, v_ffn_w_in, v_ffn_conv_w, v_ffn_conv_b, v_ffn_w_out]))
    B, S, D = x.shape
    T = B * S
    sx, sy = lax.axis_index("x"), lax.axis_index("y")
    me = 2 * sx + sy

    two_d = lambda a: a.reshape(-1, a.shape[-1])
    small_sizes = [int(np.prod(shard[n].shape)) for n, _ in _SMALL_SHARDED]
    small = jnp.concatenate([shard[n].reshape(1, -1) for n, _ in _SMALL_SHARDED], axis=1)
    small = _pad_rows(small, LANES, 8)[0]
    as_mxu = lambda a: two_d(a).astype(BF16)
    is_me = lax.broadcasted_iota(jnp.int32, (N_SHARD, 1, 1), 0) == me
    with_own = lambda gathered, own: jnp.where(is_me, own[None], gathered)
    by_cols = lambda a: jnp.moveaxis(a, 0, 1).reshape(a.shape[1], -1)
    by_rows = lambda a: a.reshape(-1, a.shape[-1])
    pad_in = lambda a: jnp.pad(by_rows(a), ((0, 0), (0, MLA_IN_PAD - MLA_IN)))
    pad_qb = lambda a: _pad_heads(by_cols(a), 1)
    ret_in_shard = as_mxu(shard["ret_w_in"])
    g_ret_in, gsmall = _all_gather_weights([ret_in_shard], small)
    later = [
        ("ret_out", [("ret_w_out", as_mxu(shard["ret_w_out"]), by_rows)]),
        ("ffn0", [("ffn_w_in0", as_mxu(shard["ffn_w_in"][0]), by_cols), ("ffn_w_out0", as_mxu(shard["ffn_w_out"][0]), by_rows)]),
        ("mla", [("mla_w_in", as_mxu(shard["mla_w_in"]), pad_in), ("mla_w_qb", as_mxu(shard["mla_w_qb"]), pad_qb),
                 ("mla_w_kvb", as_mxu(shard["mla_w_kvb"]), by_cols), ("mla_w_out", as_mxu(shard["mla_w_out"]), by_rows)]),
        ("ffn1", [("ffn_w_in1", as_mxu(shard["ffn_w_in"][1]), by_cols), ("ffn_w_out1", as_mxu(shard["ffn_w_out"][1]), by_rows)]),
    ]
    gathering = {}
    token = gsmall
    for group, items in later:
        shards = [s_ for _, s_, _ in items]
        lands = [lax.empty((N_SHARD,) + s_.shape, s_.dtype) for s_ in shards]
        send_sems, recv_sems, shards, lands, token = _exchange_start(
            _weight_copies, shards, lands, 3 * len(shards), f"weights_start_{group}", after=token)
        gathering[group] = (send_sems, recv_sems, shards, lands, items)

    def late(group, after):
        send_sems, recv_sems, shards, lands, items = gathering[group]
        shards, lands = _exchange_wait(_weight_copies, send_sems, recv_sems, shards, lands, after,
                                       f"weights_wait_{group}")
        return {key: full(with_own(l_, s_)) for (key, _, full), s_, l_ in zip(items, shards, lands)}

    gsmall = with_own(gsmall, small).reshape(N_SHARD, -1)
    wfull = {}
    off = 0
    for (n, ax), sz in zip(_SMALL_SHARDED, small_sizes):
        wfull[n] = _from_slots(gsmall[:, off:off + sz], shard[n].shape, ax)
        off += sz
    for n in _SMALL_REPLICATED:
        wfull[n] = shard[n]

    conv8 = jnp.concatenate([wfull["ffn_conv_w"], wfull["ffn_conv_b"][:, None, :],
                             jnp.zeros((2, 4, FFN_DIM), F32)], axis=1)
    w = {
        "started": token, "ret_norm": wfull["ret_norm"], "ret_w_in": by_cols(with_own(g_ret_in, ret_in_shard)),
        "ret_gn": wfull["ret_gn"].reshape(1, RET_HEADS * RET_V), "mla_norm": wfull["mla_norm"],
        "mla_q_norm": wfull["mla_q_norm"], "mla_kv_norm": wfull["mla_kv_norm"],
        "mla_q_head_norm": jnp.pad(wfull["mla_q_head_norm"], ((0, 0), (0, MLA_PAD - MLA_QK))),
        "mla_k_head_norm": jnp.pad(wfull["mla_k_head_norm"], ((0, 0), (0, MLA_PAD - MLA_QK))),
        "ffn_norm": wfull["ffn_norm"], "ffn_conv8": conv8,
    }

    started = {}

    def exchange(group, arrays):
        lands = [lax.empty((N_PEERS, p.shape[1] // 2, p.shape[2]), p.dtype) for p in arrays]
        send_sems, recv_sems, ps, lands, token = _exchange_start(
            _grad_copies, arrays, lands, N_PEERS * len(arrays), f"grads_start_{group}")
        started[group] = (send_sems, recv_sems, ps, lands)
        return token

    small_shapes = {
        "ret_norm": (1, D_MODEL), "ret_gn": (1, RET_HEADS, RET_V), "mla_norm": (1, D_MODEL),
        "mla_q_norm": (1, MLA_Q_RANK), "mla_kv_norm": (1, MLA_KV_RANK), "mla_q_head_norm": (1, MLA_QK),
        "mla_k_head_norm": (1, MLA_QK), "ffn_norm": (2, D_MODEL), "ffn_conv_w": (2, 3, FFN_DIM),
        "ffn_conv_b": (2, FFN_DIM)}

    def reduce_small(gl):
        gl = dict(gl, mla_q_head_norm=gl["mla_q_head_norm"][:, :MLA_QK], mla_k_head_norm=gl["mla_k_head_norm"][:, :MLA_QK])
        packed = jnp.concatenate([gl[n].reshape(1, -1) for n in _SMALL_ALL], axis=1)
        return _all_reduce_small(_pad_rows(packed, LANES, 8)[0])

    loss_part, dx, gsm = _local_step(x.reshape(T, D), loss_target.reshape(T, D), w, B, S, late, exchange,
                                     reduce_small)
    loss = lax.psum(loss_part, ("x", "y", "c"))

    delta, new_m, new_v, grads = {}, {}, {}, {}

    def reduced(group, after):
        send_sems, recv_sems, ps, lands = started[group]
        ps, lands = _exchange_wait(_grad_copies, send_sems, recv_sems, ps, lands, after, f"grads_wait_{group}")
        halves = [_sum_partials(p_, l_, f"grads_sum_{group}_{i}") for i, (p_, l_) in enumerate(zip(ps, lands))]
        return [two_d(r) for r in _sibling_share(halves, f"grads_share_{group}")]

    def adamw(n, g_):
        shp = shard[n].shape
        grads[n] = g_.reshape(shp)
        flat = lambda a: a.reshape(-1, shp[-1])
        d_, m_, v_ = _adamw(flat(shard[n]), flat(grads[n]), flat(mom_m[n]), flat(mom_v[n]), f"adamw_{n}")
        delta[n], new_m[n], new_v[n] = d_.reshape(shp), m_.reshape(shp), v_.reshape(shp)
        return d_

    ffn1 = reduced("ffn1", started["ret"][2][0])
    mla = reduced("mla", ffn1[0])
    ffn0 = reduced("ffn0", mla[0])
    reto = reduced("reto", ffn0[0])
    early = [adamw(n, g_) for n, g_ in zip(["mla_w_in", "mla_w_qb", "mla_w_kvb", "mla_w_out"], mla)]
    early.append(adamw("ffn_w_in", jnp.stack([ffn0[0], ffn1[0]])))
    early.append(adamw("ffn_w_out", jnp.stack([ffn0[1], ffn1[1]])))
    early.append(adamw("ret_w_out", reto[0]))
    ret = reduced("ret", jnp.stack([d_[0, 0] for d_ in early]))
    adamw("ret_w_in", ret[0])

    gsm = gsm.reshape(-1)
    sharded_axis = dict(_SMALL_SHARDED)
    off = 0
    for n in _SMALL_ALL:
        sz = int(np.prod(small_shapes[n]))
        gn = gsm[off:off + sz].reshape(small_shapes[n])
        off += sz
        if n in sharded_axis:
            ax = sharded_axis[n]
            width = shard[n].shape[ax]
            gn = lax.dynamic_slice_in_dim(gn, me * width, width, axis=ax)
        grads[n] = gn

    pack_small = lambda d: _pad_rows(jnp.concatenate([d[n].reshape(1, -1) for n in _SMALL_ALL], axis=1), LANES, 8)[0]
    d_, m_, v_ = _adamw(pack_small(shard), pack_small(grads), pack_small(mom_m), pack_small(mom_v), "adamw_small")
    off = 0
    for n in _SMALL_ALL:
        sz = int(np.prod(shard[n].shape))
        for dst, src in ((delta, d_), (new_m, m_), (new_v, v_)):
            dst[n] = src.reshape(-1)[off:off + sz].reshape(shard[n].shape)
        off += sz

    return (loss, dx.reshape(B, S, D), *[grads[n] for n in names], *[delta[n] for n in names],
            *[new_m[n] for n in names], *[new_v[n] for n in names])
```

```python
import functools

import numpy as np
import jax
import jax.numpy as jnp
from jax import lax
from jax.experimental import pallas as pl
from jax.experimental.pallas import tpu as pltpu

F32 = jnp.float32
BF16 = jnp.bfloat16
MXU_DTYPE = jnp.bfloat16

CHUNK = 64
RMS_EPS = 1e-6
ROPE_THETA = 10000.0
D_MODEL = 1024
RET_HEADS = 4
RET_QK = 256
RET_V = 512
RET_GAMMA_BASE = -5.0
MLA_HEADS = 8
MLA_Q_RANK = 384
MLA_KV_RANK = 256
MLA_NOPE = 128
MLA_ROPE = 64
MLA_V = 128
MLA_QK = MLA_NOPE + MLA_ROPE
MLA_PAD = 256
MLA_IN = MLA_Q_RANK + MLA_KV_RANK + MLA_ROPE
MLA_IN_PAD = MLA_IN + 64
MASK_VALUE = -1e30
FFN_DIM = 2816
ADAM_LR = 0.001
ADAM_B1 = 0.9
ADAM_B2 = 0.999
ADAM_EPS = 1e-08
ADAM_WD = 0.01
ADAM_STEP = 10

LANES = 128
MLA_FWD_BLOCK = 512
VMEM_LIMIT = 56 * 2 ** 20
N_SHARD = 4
N_DEV = 8

MESH = pl.DeviceIdType.MESH


def _params(sem=None, **kw):
    return pltpu.CompilerParams(dimension_semantics=sem, vmem_limit_bytes=VMEM_LIMIT, **kw)


def _pick(dim, target):
    if dim <= target:
        return dim
    best = None
    for d in range(LANES, target + 1, LANES):
        if dim % d == 0:
            best = d
    assert best is not None, (dim, target)
    return best


def _mm(a, b, dims, out_dtype, name, residual=None, bm=512, bn=1024, bk=2048, out_slots=None, after=None,
        cols_outer=False):
    a_parts = list(a) if isinstance(a, (list, tuple)) else [a]
    b_parts = list(b) if isinstance(b, (list, tuple)) else [b]
    parts_on_n = dims == "tn" or len(b_parts) > 1
    if parts_on_n:
        assert len(a_parts) == 1 and dims in ("tn", "nn")
        (K, M) = a_parts[0].shape if dims == "tn" else a_parts[0].shape[::-1]
        N = sum(p.shape[1] for p in b_parts)
        part_widths = [p.shape[1] for p in b_parts]
    else:
        assert len(b_parts) == 1
        M = a_parts[0].shape[0]
        K = sum(p.shape[1] for p in a_parts)
        N = b_parts[0].shape[1 if dims == "nn" else 0]
        part_widths = [p.shape[1] for p in a_parts]
    bm, bn, bk = _pick(M, bm), _pick(N, bn), _pick(K, min(bk, 1024) if dims == "tn" else bk)
    nk = K // bk
    unit = bn if parts_on_n else bk
    assert all(wd % unit == 0 for wd in part_widths), (name, part_widths, unit)
    bounds = np.cumsum([0] + [wd // unit for wd in part_widths])
    ranges = [(int(lo), int(hi)) for lo, hi in zip(bounds[:-1], bounds[1:])]

    def part_index(idx, lo, hi):
        return jnp.clip(idx - lo, 0, hi - lo - 1)

    if parts_on_n:
        if dims == "tn":
            a_specs = [pl.BlockSpec((bk, bm), lambda i, j, k: (k, i))]
            dn = (((0,), (0,)), ((), ()))
        else:
            a_specs = [pl.BlockSpec((bm, bk), lambda i, j, k: (i, k))]
            dn = (((1,), (0,)), ((), ()))
        b_specs = [pl.BlockSpec((bk, bn), functools.partial(lambda i, j, k, lo, hi: (k, part_index(j, lo, hi)), lo=lo, hi=hi))
                   for lo, hi in ranges]
    else:
        a_specs = [pl.BlockSpec((bm, bk), functools.partial(lambda i, j, k, lo, hi: (i, part_index(k, lo, hi)), lo=lo, hi=hi))
                   for lo, hi in ranges]
        if dims == "nt":
            b_specs = [pl.BlockSpec((bn, bk), lambda i, j, k: (j, k))]
        else:
            b_specs = [pl.BlockSpec((bk, bn), lambda i, j, k: (k, j))]
        dn = (((1,), (1 if dims == "nt" else 0,)), ((), ()))
    r_spec = pl.BlockSpec((bm, bn), lambda i, j, k: (i, j))
    if out_slots is None:
        o_spec, o_shape = r_spec, (M, N)
    else:
        ns = N // out_slots
        assert ns % bn == 0, (name, ns, bn)
        nbs = ns // bn
        o_spec = pl.BlockSpec((None, bm, bn), lambda i, j, k: (j // nbs, i, j % nbs))
        o_shape = (out_slots, M, ns)
    has_res = residual is not None
    na, nb = len(a_parts), len(b_parts)

    def body(*refs):
        a_refs, b_refs = refs[:na], refs[na:na + nb]
        r_ref = refs[na + nb] if has_res else None
        n_in = na + nb + has_res + (after is not None)
        o_ref = refs[n_in]
        acc_ref = refs[n_in + 1] if nk > 1 else None
        k = pl.program_id(2)

        def finish(acc):
            if has_res:
                acc = acc + r_ref[...].astype(F32)
            o_ref[...] = acc.astype(out_dtype)

        def compute(a_ref, b_ref):
            p = lax.dot_general(a_ref[...].astype(MXU_DTYPE), b_ref[...].astype(MXU_DTYPE), dn,
                                preferred_element_type=F32)
            if nk == 1:
                finish(p)
                return

            @pl.when(k == 0)
            def _():
                acc_ref[...] = p

            @pl.when(jnp.logical_and(k > 0, k < nk - 1))
            def _():
                acc_ref[...] += p

            @pl.when(k == nk - 1)
            def _():
                finish(acc_ref[...] + p)

        if len(ranges) == 1:
            compute(a_refs[0], b_refs[0])
        else:
            idx = pl.program_id(0 if cols_outer else 1) if parts_on_n else k
            for p, (lo, hi) in enumerate(ranges):
                @pl.when(jnp.logical_and(idx >= lo, idx < hi))
                def _(p=p):
                    compute(a_refs[0 if parts_on_n else p], b_refs[p if parts_on_n else 0])

    after_specs = [] if after is None else [pl.BlockSpec(after.shape, lambda i, j, k: (0, 0))]
    in_specs = a_specs + b_specs + ([r_spec] if has_res else []) + after_specs
    grid = (M // bm, N // bn, nk)
    if cols_outer:
        swap = lambda sp: pl.BlockSpec(sp.block_shape, functools.partial(lambda j, i, k, f: f(i, j, k), f=sp.index_map))
        in_specs, o_spec, grid = [swap(sp) for sp in in_specs], swap(o_spec), (grid[1], grid[0], nk)
    return pl.pallas_call(
        body, name=name, grid=grid,
        in_specs=in_specs, out_specs=o_spec,
        out_shape=jax.ShapeDtypeStruct(o_shape, out_dtype),
        scratch_shapes=[pltpu.VMEM((bm, bn), F32)] if nk > 1 else [],
        compiler_params=_params(("parallel", "parallel", "arbitrary")),
    )(*a_parts, *b_parts, *((residual,) if has_res else ()), *(() if after is None else (after,)))


def _mm_out_norm(a, w, residual, gain, name, bm=512):
    (M, K), N = a.shape, w.shape[1]
    bm = _pick(M, bm)

    def body(a_ref, w_ref, r_ref, g_ref, o_ref, h_ref, ht_ref):
        acc = lax.dot_general(a_ref[...].astype(MXU_DTYPE), w_ref[...].astype(MXU_DTYPE), _NN,
                              preferred_element_type=F32) + r_ref[...]
        o_ref[...] = acc
        hv = _fn_rms([[acc]], [], [[g_ref[...]]])[0][0]
        h_ref[...] = hv.astype(h_ref.dtype)
        ht_ref[...] = hv.T.astype(ht_ref.dtype)

    row = pl.BlockSpec((bm, N), lambda i: (i, 0))
    whole = lambda arr: pl.BlockSpec(arr.shape, lambda i: (0, 0))
    return pl.pallas_call(
        body, name=name, grid=(M // bm,),
        in_specs=[pl.BlockSpec((bm, K), lambda i: (i, 0)), whole(w), row, whole(gain)],
        out_specs=[row, row, pl.BlockSpec((N, bm), lambda i: (0, i))],
        out_shape=[jax.ShapeDtypeStruct((M, N), F32), jax.ShapeDtypeStruct((M, N), BF16),
                   jax.ShapeDtypeStruct((N, M), BF16)],
        compiler_params=_params(("parallel",)),
    )(a, w, residual, gain)


def _mm_dx_norm(a_parts, w, x, gain, add, name, bm=256, after=None):
    M = a_parts[0].shape[0]
    N, K = w.shape
    widths = [p.shape[1] for p in a_parts]
    assert sum(widths) == K, (name, widths, K)
    offs = [int(o) for o in np.cumsum([0] + widths[:-1])]
    bm = _pick(M, bm)
    na = len(a_parts)
    n_in = na + 4 + (after is not None)

    def body(*refs):
        w_ref, x_ref, g_ref, add_ref = refs[na:na + 4]
        dx_ref, dxc_ref, dg_ref = refs[n_in:n_in + 3]
        dh = None
        for a_ref, off, wd in zip(refs[:na], offs, widths):
            p = lax.dot_general(a_ref[...].astype(MXU_DTYPE), w_ref[:, off:off + wd].astype(MXU_DTYPE), _NT,
                                preferred_element_type=F32)
            dh = p if dh is None else dh + p
        _, vjp = jax.vjp(lambda xv, gv: _fn_rms([[xv]], [], [[gv]])[0][0], x_ref[...], g_ref[...])
        dxv, dgv = vjp(dh)
        dxv = dxv + add_ref[...]
        dx_ref[...] = dxv
        dxc_ref[...] = dxv.astype(dxc_ref.dtype)

        @pl.when(pl.program_id(0) == 0)
        def _():
            dg_ref[...] = dgv

        @pl.when(pl.program_id(0) > 0)
        def _():
            dg_ref[...] += dgv

    row = pl.BlockSpec((bm, N), lambda i: (i, 0))
    whole = lambda a: pl.BlockSpec(a.shape, lambda i: (0, 0))
    in_specs = [pl.BlockSpec((bm, wd), lambda i: (i, 0)) for wd in widths] + [whole(w), row, whole(gain), row]
    in_specs += [] if after is None else [whole(after)]
    return pl.pallas_call(
        body, name=name, grid=(M // bm,),
        in_specs=in_specs, out_specs=[row, row, whole(gain)],
        out_shape=[jax.ShapeDtypeStruct((M, N), F32), jax.ShapeDtypeStruct((M, N), BF16),
                   jax.ShapeDtypeStruct(gain.shape, F32)],
        compiler_params=_params(("arbitrary",)),
    )(*a_parts, w, x, gain, add, *(() if after is None else (after,)))


def _tiles(ref, width, tile):
    return [ref[:, t * tile:(t + 1) * tile].astype(F32) for t in range(width // tile)]


def _row_specs(rows, pos, consts, bm, S):
    npos_blocks = S // bm
    specs = [pl.BlockSpec((bm, w), functools.partial(lambda i, c: (i, c), c=cb)) for (_, w, cb, _) in rows]
    specs += [pl.BlockSpec((bm, p.shape[1]), lambda i: (i % npos_blocks, 0)) for p in pos]
    specs += [pl.BlockSpec(c.shape, lambda i: (0, 0)) for (c, _) in consts]
    return specs


def _rowwise_fwd(fn, name, rows, pos, consts, outs, bm, S, transposed=()):
    T = rows[0][0].shape[0]
    nr, npos, nc, no = len(rows), len(pos), len(consts), len(outs)

    def body(*refs):
        row_v = [_tiles(r, w, t) for r, (_, w, _, t) in zip(refs[:nr], rows)]
        pos_v = [r[...] for r in refs[nr:nr + npos]]
        const_v = [_tiles(r, c.shape[1], t) for r, (c, t) in zip(refs[nr + npos:nr + npos + nc], consts)]
        res = fn(row_v, pos_v, const_v)
        out_refs = refs[nr + npos + nc:]
        for o_ref, tiles, (w, t, dt) in zip(out_refs, res, outs):
            for k, v in enumerate(tiles):
                o_ref[:, k * t:(k + 1) * t] = v.astype(dt)
        for t_ref, a in zip(out_refs[no:], transposed):
            t = outs[a][1]
            for k, v in enumerate(res[a]):
                t_ref[k * t:(k + 1) * t, :] = v.T.astype(t_ref.dtype)

    return pl.pallas_call(
        body, name=name, grid=(T // bm,),
        in_specs=_row_specs(rows, pos, consts, bm, S),
        out_specs=[pl.BlockSpec((bm, w), lambda i: (i, 0)) for (w, _, _) in outs]
        + [pl.BlockSpec((outs[a][0], bm), lambda i: (0, i)) for a in transposed],
        out_shape=[jax.ShapeDtypeStruct((T, w), dt) for (w, _, dt) in outs]
        + [jax.ShapeDtypeStruct((outs[a][0], T), BF16) for a in transposed],
        compiler_params=_params(("parallel",)),
    )(*[r[0] for r in rows], *pos, *[c[0] for c in consts])


def _rowwise_bwd(fn, name, rows, pos, consts, cts, bm, S, adds=None, grad_dtypes=None, mxu_copies=(), linear=False):
    adds = adds or {}
    T = rows[0][0].shape[0]
    nr, npos, nc, nct = len(rows), len(pos), len(consts), len(cts)
    add_idx = sorted(adds)
    grad_dtypes = grad_dtypes or [F32] * nr

    def body(*refs):
        it = iter(refs)
        row_refs = [None if linear else next(it) for _ in range(nr)]
        pos_refs = [next(it) for _ in range(npos)]
        const_refs = [next(it) for _ in range(nc)]
        ct_refs = [next(it) for _ in range(nct)]
        add_refs = {k: next(it) for k in add_idx}
        drow_refs = [next(it) for _ in range(nr)]
        copy_refs = {a: next(it) for a in mxu_copies}
        dconst_refs = [next(it) for _ in range(nc)]
        if linear:
            row_v = [[jnp.zeros((bm, t), F32)] * (w // t) for (_, w, _, t) in rows]
        else:
            row_v = [_tiles(r, w, t) for r, (_, w, _, t) in zip(row_refs, rows)]
        pos_v = [r[...] for r in pos_refs]
        const_v = [_tiles(r, c.shape[1], t) for r, (c, t) in zip(const_refs, consts)]
        ct_v = [_tiles(r, c.shape[1], t) for r, (c, t) in zip(ct_refs, cts)]
        _, vjp = jax.vjp(lambda rv, cv: fn(rv, pos_v, cv), row_v, const_v)
        drows, dconsts = vjp(ct_v)
        for a, (d_ref, tiles, (_, w, _, t)) in enumerate(zip(drow_refs, drows, rows)):
            for k, v in enumerate(tiles):
                if a in add_refs:
                    v = v + add_refs[a][:, k * t:(k + 1) * t].astype(F32)
                d_ref[:, k * t:(k + 1) * t] = v.astype(d_ref.dtype)
                if a in copy_refs:
                    copy_refs[a][:, k * t:(k + 1) * t] = v.astype(BF16)
        first = pl.program_id(0) == 0
        for d_ref, tiles, (_, t) in zip(dconst_refs, dconsts, consts):
            for k, v in enumerate(tiles):
                @pl.when(first)
                def _(d_ref=d_ref, k=k, t=t, v=v):
                    d_ref[:, k * t:(k + 1) * t] = v

                @pl.when(jnp.logical_not(first))
                def _(d_ref=d_ref, k=k, t=t, v=v):
                    d_ref[:, k * t:(k + 1) * t] += v

    in_specs = _row_specs([] if linear else rows, pos, consts, bm, S)
    in_specs += [pl.BlockSpec((bm, c.shape[1]), lambda i: (i, 0)) for (c, _) in cts]
    in_specs += [pl.BlockSpec((bm, adds[k].shape[1]), lambda i: (i, 0)) for k in add_idx]
    out_specs = [pl.BlockSpec((bm, w), lambda i: (i, 0)) for (_, w, _, _) in rows]
    out_specs += [pl.BlockSpec((bm, rows[a][1]), lambda i: (i, 0)) for a in mxu_copies]
    out_specs += [pl.BlockSpec(c.shape, lambda i: (0, 0)) for (c, _) in consts]
    out_shape = [jax.ShapeDtypeStruct((T, w), dt) for (_, w, _, _), dt in zip(rows, grad_dtypes)]
    out_shape += [jax.ShapeDtypeStruct((T, rows[a][1]), BF16) for a in mxu_copies]
    out_shape += [jax.ShapeDtypeStruct(c.shape, F32) for (c, _) in consts]
    res = pl.pallas_call(
        body, name=name, grid=(T // bm,),
        in_specs=in_specs, out_specs=out_specs, out_shape=out_shape,
        compiler_params=_params(("arbitrary",)),
    )(*([] if linear else [r[0] for r in rows]), *pos, *[c[0] for c in consts], *[c[0] for c in cts],
      *[adds[k] for k in add_idx])
    n_rows = nr + len(mxu_copies)
    return res[:n_rows], res[n_rows:]


def _ssq(tiles):
    s = jnp.sum(tiles[0] * tiles[0], axis=-1, keepdims=True)
    for t in tiles[1:]:
        s = s + jnp.sum(t * t, axis=-1, keepdims=True)
    return s


def _sigmoid(x):
    return 0.5 * jnp.tanh(0.5 * x) + 0.5


def _fn_rms(rows, pos, consts):
    (x,), (g,) = rows[0], consts[0]
    r = lax.rsqrt(jnp.mean(x * x, axis=-1, keepdims=True) + RMS_EPS)
    return [[x * r * g]]


def _fn_ret_rope(rows, pos, consts):
    (qkv,) = rows
    nq = RET_HEADS * RET_QK // LANES
    q, k, v = qkv[:nq], qkv[nq:2 * nq], qkv[2 * nq:]
    cos, sin = pos

    def rot(t, scale):
        out = []
        for h in range(RET_HEADS):
            x1, x2 = t[2 * h], t[2 * h + 1]
            o1, o2 = x1 * cos - x2 * sin, x2 * cos + x1 * sin
            out += [o1, o2] if scale is None else [o1 * scale, o2 * scale]
        return out

    return [rot(q, None), rot(k, RET_QK ** -0.5), list(v)]


def _fn_ret_gate(rows, pos, consts):
    o, g = rows
    (gn,) = consts
    out = []
    for h in range(RET_HEADS):
        r = lax.rsqrt(jnp.mean(o[h] * o[h], axis=-1, keepdims=True) + RMS_EPS)
        out.append((o[h] * r * gn[h]) * (g[h] * _sigmoid(g[h])))
    return [out]


def _fn_mla_lat(rows, pos, consts):
    (p,) = rows
    gq, gkv = consts
    nq, nkv = MLA_Q_RANK // LANES, MLA_KV_RANK // LANES
    cq, ckv, kr = p[:nq], p[nq:nq + nkv], p[nq + nkv]
    rq = lax.rsqrt(_ssq(cq) / MLA_Q_RANK + RMS_EPS)
    rkv = lax.rsqrt(_ssq(ckv) / MLA_KV_RANK + RMS_EPS)
    return [[t * rq * g for t, g in zip(cq, gq)], [t * rkv * g for t, g in zip(ckv, gkv)], [kr]]


def _swap32_impl(x):
    lane = lax.broadcasted_iota(jnp.int32, x.shape, 1)
    up, down = pltpu.roll(x, LANES - 32, 1), pltpu.roll(x, 32, 1)
    return jnp.where(lane < 32, up, jnp.where(lane < 64, down, 0.0))


@jax.custom_vjp
def _swap32(x):
    return _swap32_impl(x)


_swap32.defvjp(lambda x: (_swap32_impl(x), None), lambda _, g: (_swap32_impl(g),))


def _fn_mla_heads(rows, pos, consts):
    qf, kvf, (kr,) = rows
    cos, sin = pos
    gq, gk = consts
    q_out, k_out, v_out = [], [], []
    for h in range(MLA_HEADS):
        q0, q1 = qf[2 * h], qf[2 * h + 1]
        r = lax.rsqrt(_ssq([q0, q1]) / MLA_QK + RMS_EPS)
        a0, a1 = q0 * r * gq[0], q1 * r * gq[1]
        a1 = a1 * cos + _swap32(a1) * sin
        q_out += [a0 * (MLA_QK ** -0.5), a1 * (MLA_QK ** -0.5)]
        k0 = kvf[2 * h]
        r = lax.rsqrt(_ssq([k0, kr]) / MLA_QK + RMS_EPS)
        b0, b1 = k0 * r * gk[0], kr * r * gk[1]
        k_out += [b0, b1 * cos + _swap32(b1) * sin]
        v_out.append(kvf[2 * h + 1])
    return [q_out, k_out, v_out]


def _shift_down(x, n):
    row = lax.broadcasted_iota(jnp.int32, x.shape, 0)
    return jnp.where(row >= n, pltpu.roll(x, n, 0), 0.0)


def _shift_up(x, n):
    rows = x.shape[0]
    row = lax.broadcasted_iota(jnp.int32, x.shape, 0)
    return jnp.where(row < rows - n, pltpu.roll(x, rows - n, 0), 0.0)


def _conv_blocks(S):
    cb = 256
    return cb, FFN_DIM // cb


def _conv_fwd(ag, w8, B, S, name):
    cb, ncb = _conv_blocks(S)

    def body(a_ref, g_ref, w_ref, u_ref, ut_ref):
        g = g_ref[...].astype(F32)
        w = w_ref[...]
        gc = w[0:1] * _shift_down(g, 2) + w[1:2] * _shift_down(g, 1) + w[2:3] * g + w[3:4]
        u = a_ref[...].astype(F32) * (gc * _sigmoid(gc))
        u_ref[...] = u.astype(u_ref.dtype)
        ut_ref[...] = u.T.astype(ut_ref.dtype)

    return pl.pallas_call(
        body, name=name, grid=(ncb, B),
        in_specs=[pl.BlockSpec((S, cb), lambda j, b: (b, j)),
                  pl.BlockSpec((S, cb), lambda j, b: (b, ncb + j)),
                  pl.BlockSpec((8, cb), lambda j, b: (0, j))],
        out_specs=[pl.BlockSpec((S, cb), lambda j, b: (b, j)), pl.BlockSpec((cb, S), lambda j, b: (j, b))],
        out_shape=[jax.ShapeDtypeStruct((B * S, FFN_DIM), BF16), jax.ShapeDtypeStruct((FFN_DIM, B * S), BF16)],
        compiler_params=_params(("parallel", "parallel")),
    )(ag, ag, w8)


def _conv_bwd(ag, w8, du, B, S, name):
    cb, ncb = _conv_blocks(S)

    def body(a_ref, g_ref, w_ref, du_ref, da_ref, dg_ref, dw_ref):
        g = g_ref[...].astype(F32)
        w = w_ref[...]
        g1, g2 = _shift_down(g, 1), _shift_down(g, 2)
        gc = w[0:1] * g2 + w[1:2] * g1 + w[2:3] * g + w[3:4]
        sg = _sigmoid(gc)
        du_v = du_ref[...]
        da_ref[...] = (du_v * (gc * sg)).astype(da_ref.dtype)
        dgc = du_v * a_ref[...].astype(F32) * (sg * (1.0 + gc * (1.0 - sg)))
        dg = w[2:3] * dgc + w[1:2] * _shift_up(dgc, 1) + w[0:1] * _shift_up(dgc, 2)
        dg_ref[...] = dg.astype(dg_ref.dtype)
        part = jnp.concatenate([
            jnp.sum(dgc * g2, axis=0, keepdims=True), jnp.sum(dgc * g1, axis=0, keepdims=True),
            jnp.sum(dgc * g, axis=0, keepdims=True), jnp.sum(dgc, axis=0, keepdims=True),
            jnp.zeros((4, cb), F32)], axis=0)

        @pl.when(pl.program_id(1) == 0)
        def _():
            dw_ref[...] = part

        @pl.when(pl.program_id(1) > 0)
        def _():
            dw_ref[...] += part

    blk = lambda j, b: (b, j)
    return pl.pallas_call(
        body, name=name, grid=(ncb, B),
        in_specs=[pl.BlockSpec((S, cb), blk),
                  pl.BlockSpec((S, cb), lambda j, b: (b, ncb + j)),
                  pl.BlockSpec((8, cb), lambda j, b: (0, j)),
                  pl.BlockSpec((S, cb), blk)],
        out_specs=[pl.BlockSpec((S, cb), blk), pl.BlockSpec((S, cb), blk),
                   pl.BlockSpec((8, cb), lambda j, b: (0, j))],
        out_shape=[jax.ShapeDtypeStruct((B * S, FFN_DIM), BF16), jax.ShapeDtypeStruct((B * S, FFN_DIM), BF16),
                   jax.ShapeDtypeStruct((8, FFN_DIM), F32)],
        compiler_params=_params(("parallel", "arbitrary")),
    )(ag, ag, w8, du)


_NT = (((1,), (1,)), ((), ()))
_NN = (((1,), (0,)), ((), ()))
_TN = (((0,), (0,)), ((), ()))


def _dot(a, b, dn):
    return lax.dot_general(a.astype(MXU_DTYPE), b.astype(MXU_DTYPE), dn, preferred_element_type=F32)


def _run_bits(n):
    bits, b = [], 1
    while b < n:
        bits.append(b)
        b *= 2
    return bits[::-1]


def _key_runs(n, nq, update):
    for bit in _run_bits(nq + 1):
        @pl.when((n & bit) != 0)
        def _(bit=bit):
            update(n & ~(2 * bit - 1), bit, (n & (bit - 1)) == 0)


def _earlier_runs(n, nq, update):
    for bit in _run_bits(nq):
        @pl.when((n & bit) != 0)
        def _(bit=bit):
            update(n & ~(2 * bit - 1), bit, False)


def _chunk_visible(shape, nblk, blk):
    key = lax.broadcasted_iota(jnp.int32, shape, 0) - (nblk - 1) * blk
    query = lax.broadcasted_iota(jnp.int32, shape, 1)
    return jnp.logical_or(key < 0, (key // CHUNK) <= (query // CHUNK))


def _mla_attn_fwd(q, k, v, B, S):
    blk = min(MLA_FWD_BLOCK, S)
    H, nq = MLA_HEADS, S // blk

    def body(q_ref, k_ref, v_ref, o_ref, lse_ref, m_ref, l_ref, acc_ref):
        def qblock(i, _):
            q_rows = pl.ds(pl.multiple_of(i * blk, blk), blk)
            qi = q_ref[q_rows, :]
            m_ref[...] = jnp.full(m_ref.shape, MASK_VALUE, F32)
            l_ref[...] = jnp.zeros(l_ref.shape, F32)
            acc_ref[...] = jnp.zeros(acc_ref.shape, F32)

            def keys(first, nblk, last):
                rows = pl.ds(pl.multiple_of(first * blk, blk), nblk * blk)
                s = _dot(k_ref[rows, :], qi, _NT)
                s = jnp.where(jnp.logical_or(_chunk_visible(s.shape, nblk, blk), jnp.logical_not(last)), s, MASK_VALUE)
                m = m_ref[...]
                m2 = jnp.maximum(m, jnp.max(s, axis=0, keepdims=True))
                alpha = jnp.exp(m - m2)
                p = jnp.exp(s - m2)
                l_ref[...] = alpha * l_ref[...] + jnp.sum(p, axis=0, keepdims=True)
                acc_ref[...] = alpha * acc_ref[...] + _dot(v_ref[rows, :], p, _TN)
                m_ref[...] = m2

            _key_runs(i + 1, nq, keys)
            l = l_ref[...]
            o_ref[q_rows, :] = (acc_ref[...] / l).T
            lse_ref[0, :, q_rows] = m_ref[...] + jnp.log(l)
            return 0

        lax.fori_loop(0, nq, qblock, 0)

    return pl.pallas_call(
        body, name="mla_attn_fwd", grid=(B, H),
        in_specs=[pl.BlockSpec((S, MLA_PAD), lambda b, h: (b, h)),
                  pl.BlockSpec((S, MLA_PAD), lambda b, h: (b, h)),
                  pl.BlockSpec((S, MLA_V), lambda b, h: (b, h))],
        out_specs=[pl.BlockSpec((S, MLA_V), lambda b, h: (b, h)),
                   pl.BlockSpec((1, 1, S), lambda b, h: (b * H + h, 0, 0))],
        out_shape=[jax.ShapeDtypeStruct((B * S, H * MLA_V), F32), jax.ShapeDtypeStruct((B * H, 1, S), F32)],
        scratch_shapes=[pltpu.VMEM((1, blk), F32), pltpu.VMEM((1, blk), F32), pltpu.VMEM((MLA_V, blk), F32)],
        compiler_params=_params(("parallel", "parallel")),
    )(q, k, v)


def _mla_attn_bwd(q, k, v, o, do, lse, B, S):
    blk = min(MLA_FWD_BLOCK, S)
    H, nq = MLA_HEADS, S // blk

    def body(q_ref, k_ref, v_ref, o_ref, do_ref, lse_ref, dq_ref, dk_ref, dv_ref, kt_ref, dqt_ref):
        dk_ref[...] = jnp.zeros(dk_ref.shape, F32)
        dv_ref[...] = jnp.zeros(dv_ref.shape, F32)
        for g in range(nq):
            kt_ref[g] = k_ref[g * blk:(g + 1) * blk, :].T

        def qblock(i, _):
            q_rows = pl.ds(pl.multiple_of(i * blk, blk), blk)
            qi = q_ref[q_rows, :]
            doi = do_ref[q_rows, :]
            delta = jnp.sum((doi * o_ref[q_rows, :]).T, axis=0, keepdims=True)
            lse_i = lse_ref[0, :, q_rows]
            doi = doi.astype(MXU_DTYPE)
            dqt_ref[...] = jnp.zeros(dqt_ref.shape, F32)

            def keys(first, nblk, last):
                rows = pl.ds(pl.multiple_of(first * blk, blk), nblk * blk)
                k_run, v_run = k_ref[rows, :], v_ref[rows, :]
                p = jnp.exp(_dot(k_run, qi, _NT) - lse_i)
                p = jnp.where(jnp.logical_or(_chunk_visible(p.shape, nblk, blk), jnp.logical_not(last)), p, 0.0)
                ds = (p * (_dot(v_run, doi, _NT) - delta)).astype(MXU_DTYPE)
                dk_ref[rows, :] += _dot(ds, qi, _NN)
                dv_ref[rows, :] += _dot(p, doi, _NN)
                for r in range(nblk):
                    dqt_ref[...] += _dot(kt_ref[first + r], ds[r * blk:(r + 1) * blk, :], _NN)

            _key_runs(i + 1, nq, keys)
            dq_ref[q_rows, :] = dqt_ref[...].T
            return 0

        lax.fori_loop(0, nq, qblock, 0)

    qk_spec = pl.BlockSpec((S, MLA_PAD), lambda b, h: (b, h))
    v_spec = pl.BlockSpec((S, MLA_V), lambda b, h: (b, h))
    return pl.pallas_call(
        body, name="mla_attn_bwd", grid=(B, H),
        in_specs=[qk_spec, qk_spec, v_spec, v_spec, v_spec,
                  pl.BlockSpec((1, 1, S), lambda b, h: (b * H + h, 0, 0))],
        out_specs=[qk_spec, qk_spec, v_spec],
        out_shape=[jax.ShapeDtypeStruct((B * S, H * MLA_PAD), F32), jax.ShapeDtypeStruct((B * S, H * MLA_PAD), F32),
                   jax.ShapeDtypeStruct((B * S, H * MLA_V), F32)],
        scratch_shapes=[pltpu.VMEM((nq, MLA_PAD, blk), q.dtype), pltpu.VMEM((MLA_PAD, blk), F32)],
        compiler_params=_params(("parallel", "parallel")),
    )(q, k, v, o, do, lse)


def _ret_log_gamma():
    lg = np.log1p(-np.exp2(RET_GAMMA_BASE - np.arange(RET_HEADS, dtype=np.float32))).astype(np.float32)
    return jnp.asarray(np.broadcast_to(lg[:, None, None], (RET_HEADS, 8, LANES)).copy())


RET_BLOCK = 512


def _ret_local_scale(lg, shape, blk, rising):
    local = lax.broadcasted_iota(jnp.int32, shape, 0) % blk
    return jnp.exp(lg * (local if rising else blk - 1 - local).astype(F32))


def _ret_pair_factor(lg, blk, steps):
    return jnp.exp(lg * (blk * (steps - 1) + 1).astype(F32))


def _ret_own_decay(lg, blk, transposed):
    a = lax.broadcasted_iota(jnp.int32, (blk, blk), 0)
    b = lax.broadcasted_iota(jnp.int32, (blk, blk), 1)
    query, key = (b, a) if transposed else (a, b)
    dec = jnp.exp(lg * jnp.abs(query - key).astype(F32))
    return jnp.where((key // CHUNK) <= (query // CHUNK), dec, 0.0)


def _ret_attn_fwd(q, k, v, B, S):
    blk = min(RET_BLOCK, S)
    H, nq = RET_HEADS, S // blk

    def body(lg_ref, q_ref, k_ref, v_ref, o_ref, ks_ref, dec_ref, acc_ref):
        lg = lg_ref[0, 0:1, 0:1]
        ks_ref[...] = (k_ref[...].astype(F32) * _ret_local_scale(lg, k_ref.shape, blk, False)).astype(ks_ref.dtype)
        dec_ref[...] = _ret_own_decay(lg, blk, False)

        def qblock(i, _):
            q_rows = pl.ds(pl.multiple_of(i * blk, blk), blk)
            qi = q_ref[q_rows, :]
            qs = (qi.astype(F32) * _ret_local_scale(lg, qi.shape, blk, True)).astype(qi.dtype)
            a = _dot(qi, k_ref[q_rows, :], _NT) * dec_ref[...]
            acc_ref[...] = _dot(a, v_ref[q_rows, :], _NN)

            def keys(first, nblk, _):
                rows = pl.ds(pl.multiple_of(first * blk, blk), nblk * blk)
                steps = i - first - lax.broadcasted_iota(jnp.int32, (1, nblk * blk), 1) // blk
                a = _dot(qs, ks_ref[rows, :], _NT) * _ret_pair_factor(lg, blk, steps)
                acc_ref[...] += _dot(a, v_ref[rows, :], _NN)

            _earlier_runs(i, nq, keys)
            o_ref[q_rows, :] = acc_ref[...]
            return 0

        lax.fori_loop(0, nq, qblock, 0)

    qk_spec = pl.BlockSpec((S, RET_QK), lambda b, h: (b, h))
    v_spec = pl.BlockSpec((S, RET_V), lambda b, h: (b, h))
    return pl.pallas_call(
        body, name="ret_attn_fwd", grid=(B, H),
        in_specs=[pl.BlockSpec((1, 8, LANES), lambda b, h: (h, 0, 0)), qk_spec, qk_spec, v_spec],
        out_specs=v_spec,
        out_shape=jax.ShapeDtypeStruct((B * S, H * RET_V), F32),
        scratch_shapes=[pltpu.VMEM((S, RET_QK), k.dtype), pltpu.VMEM((blk, blk), F32), pltpu.VMEM((blk, RET_V), F32)],
        compiler_params=_params(("parallel", "parallel")),
    )(_ret_log_gamma(), q, k, v)


def _ret_attn_bwd(q, k, v, do, B, S):
    blk = min(RET_BLOCK, S)
    H, nq = RET_HEADS, S // blk

    def body(lg_ref, q_ref, k_ref, v_ref, do_ref, dq_ref, dk_ref, dv_ref, ks_ref, kst_ref, dks_ref, dqt_ref, dec_ref):
        lg = lg_ref[0, 0:1, 0:1]
        dk_ref[...] = jnp.zeros(dk_ref.shape, F32)
        dv_ref[...] = jnp.zeros(dv_ref.shape, F32)
        dks_ref[...] = jnp.zeros(dks_ref.shape, F32)
        ks_ref[...] = (k_ref[...].astype(F32) * _ret_local_scale(lg, k_ref.shape, blk, False)).astype(ks_ref.dtype)
        for g in range(nq):
            kst_ref[g] = ks_ref[g * blk:(g + 1) * blk, :].T
        dec_ref[...] = _ret_own_decay(lg, blk, True)

        def qblock(i, _):
            q_rows = pl.ds(pl.multiple_of(i * blk, blk), blk)
            qi = q_ref[q_rows, :]
            q_scale = _ret_local_scale(lg, qi.shape, blk, True)
            qs = (qi.astype(F32) * q_scale).astype(qi.dtype)
            doi = do_ref[q_rows, :].astype(MXU_DTYPE)
            ki = k_ref[q_rows, :]
            dec = dec_ref[...]
            a = _dot(ki, qi, _NT) * dec
            da = (_dot(v_ref[q_rows, :], doi, _NT) * dec).astype(MXU_DTYPE)
            dv_ref[q_rows, :] += _dot(a, doi, _NN)
            dk_ref[q_rows, :] += _dot(da, qi, _NN)
            dq_own = _dot(da, ki, _TN)
            dqt_ref[...] = jnp.zeros(dqt_ref.shape, F32)

            def keys(first, nblk, _):
                for r in range(nblk):
                    g = first + r
                    rows = pl.ds(pl.multiple_of(g * blk, blk), blk)
                    c = _ret_pair_factor(lg, blk, i - g)
                    a = _dot(ks_ref[rows, :], qs, _NT) * c
                    da = (_dot(v_ref[rows, :], doi, _NT) * c).astype(MXU_DTYPE)
                    dv_ref[rows, :] += _dot(a, doi, _NN)
                    dks_ref[rows, :] += _dot(da, qs, _NN)
                    dqt_ref[...] += _dot(kst_ref[g], da, _NN)

            _earlier_runs(i, nq, keys)
            dq_ref[q_rows, :] = dqt_ref[...].T * q_scale + dq_own
            return 0

        lax.fori_loop(0, nq, qblock, 0)
        dk_ref[...] += dks_ref[...] * _ret_local_scale(lg, dks_ref.shape, blk, False)

    qk_spec = pl.BlockSpec((S, RET_QK), lambda b, h: (b, h))
    v_spec = pl.BlockSpec((S, RET_V), lambda b, h: (b, h))
    return pl.pallas_call(
        body, name="ret_attn_bwd", grid=(B, H),
        in_specs=[pl.BlockSpec((1, 8, LANES), lambda b, h: (h, 0, 0)), qk_spec, qk_spec, v_spec, v_spec],
        out_specs=[qk_spec, qk_spec, v_spec],
        out_shape=[jax.ShapeDtypeStruct((B * S, H * RET_QK), F32), jax.ShapeDtypeStruct((B * S, H * RET_QK), F32),
                   jax.ShapeDtypeStruct((B * S, H * RET_V), F32)],
        scratch_shapes=[pltpu.VMEM((S, RET_QK), k.dtype), pltpu.VMEM((nq, RET_QK, blk), k.dtype),
                        pltpu.VMEM((S, RET_QK), F32), pltpu.VMEM((RET_QK, blk), F32), pltpu.VMEM((blk, blk), F32)],
        compiler_params=_params(("parallel", "parallel")),
    )(_ret_log_gamma(), q, k, v, do)


def _loss_head(y, target, bm=512):
    T, D = y.shape
    bm = _pick(T, bm)

    def body(y_ref, t_ref, dy_ref, dyc_ref, l_ref):
        err = y_ref[...] - t_ref[...]
        dy_ref[...] = err / D
        dyc_ref[...] = (err / D).astype(dyc_ref.dtype)
        part = jnp.full((8, LANES), 0.5 * jnp.sum(jnp.mean(err * err, axis=-1)), F32)

        @pl.when(pl.program_id(0) == 0)
        def _():
            l_ref[...] = part

        @pl.when(pl.program_id(0) > 0)
        def _():
            l_ref[...] += part

    blk = pl.BlockSpec((bm, D), lambda i: (i, 0))
    dy, dyc, l = pl.pallas_call(
        body, name="loss_head", grid=(T // bm,),
        in_specs=[blk, blk], out_specs=[blk, blk, pl.BlockSpec((8, LANES), lambda i: (0, 0))],
        out_shape=[jax.ShapeDtypeStruct((T, D), F32), jax.ShapeDtypeStruct((T, D), BF16),
                   jax.ShapeDtypeStruct((8, LANES), F32)],
        compiler_params=_params(("arbitrary",)),
    )(y, target)
    return dy, dyc, l[0, 0]


def _adamw(w, g, m, v, name):
    R, C = w.shape
    br = R if R * C * 4 <= 2 ** 21 else _pick_rows(R, max(8, (2 ** 21) // (C * 4)))

    def body(w_ref, g_ref, m_ref, v_ref, d_ref, mo_ref, vo_ref):
        g_v = g_ref[...]
        m_v = ADAM_B1 * m_ref[...] + (1.0 - ADAM_B1) * g_v
        v_v = ADAM_B2 * v_ref[...] + (1.0 - ADAM_B2) * (g_v * g_v)
        m_hat = m_v / (1.0 - ADAM_B1 ** ADAM_STEP)
        v_hat = v_v / (1.0 - ADAM_B2 ** ADAM_STEP)
        d_ref[...] = -ADAM_LR * (m_hat / (jnp.sqrt(v_hat) + ADAM_EPS) + ADAM_WD * w_ref[...])
        mo_ref[...] = m_v
        vo_ref[...] = v_v

    blk = pl.BlockSpec((br, C), lambda i: (i, 0))
    return pl.pallas_call(
        body, name=name, grid=(R // br,),
        in_specs=[blk] * 4, out_specs=[blk] * 3,
        out_shape=[jax.ShapeDtypeStruct((R, C), F32)] * 3,
        compiler_params=_params(("parallel",)),
    )(w, g, m, v)


def _pick_rows(R, target):
    best = None
    for d in range(8, min(R, target) + 1, 8):
        if R % d == 0:
            best = d
    assert best is not None, (R, target)
    return best


def _position():
    return lax.axis_index("x"), lax.axis_index("y"), lax.axis_index("c")


HBM_SPEC = pl.BlockSpec(memory_space=pltpu.HBM)


def _other_chips(x, y):
    return [(1 - x, y), (x, 1 - y), (1 - x, 1 - y)]


def _all_gather_weights(bigs, small):
    nb = len(bigs)

    def body(*refs):
        big_refs, small_ref = refs[:nb], refs[nb]
        obig, osmall = refs[nb + 1:2 * nb + 1], refs[2 * nb + 1]
        ici_send, ici_recv, d2d_send, d2d_recv, sm_send, sm_recv = refs[2 * nb + 2:]
        x, y, c = _position()
        me = 2 * x + y
        chips = _other_chips(x, y)

        def rows(n, half):
            rh = bigs[n].shape[0] // 2
            return pl.ds(half * rh, rh)

        def over_ici(n, j, slot, from_shard):
            px, py = chips[j]
            dst = obig[n].at[slot, rows(n, c)]
            return pltpu.make_async_remote_copy(
                src_ref=big_refs[n].at[rows(n, c)] if from_shard else dst, dst_ref=dst,
                send_sem=ici_send.at[3 * n + j], recv_sem=ici_recv.at[3 * n + j],
                device_id=(px, py, c), device_id_type=MESH)

        def over_d2d(n, j, half):
            px, py = chips[j]
            part = obig[n].at[2 * px + py, rows(n, half)]
            return pltpu.make_async_remote_copy(
                src_ref=part, dst_ref=part, send_sem=d2d_send.at[3 * n + j], recv_sem=d2d_recv.at[3 * n + j],
                device_id=(x, y, 1 - c), device_id_type=MESH)

        def small_copy(j, slot):
            px, py = chips[j]
            return pltpu.make_async_remote_copy(
                src_ref=small_ref, dst_ref=osmall.at[slot], send_sem=sm_send.at[j], recv_sem=sm_recv.at[j],
                device_id=(px, py, c), device_id_type=MESH)

        sends = [over_ici(n, j, me, True) for n in range(nb) for j in range(3)]
        sends += [small_copy(j, me) for j in range(3)]
        for cp in sends:
            cp.start()
        passed = []
        for n in range(nb):
            for j, (px, py) in enumerate(chips):
                over_ici(n, j, 2 * px + py, False).wait_recv()
                fwd = over_d2d(n, j, c)
                fwd.start()
                passed.append(fwd)
        for n in range(nb):
            for j in range(3):
                over_d2d(n, j, 1 - c).wait_recv()
        for j, (px, py) in enumerate(chips):
            small_copy(j, 2 * px + py).wait_recv()
        for cp in sends + passed:
            cp.wait_send()

    dma = pltpu.SemaphoreType.DMA
    return pl.pallas_call(
        body, name="weights_all_gather",
        in_specs=[HBM_SPEC] * (nb + 1), out_specs=[HBM_SPEC] * (nb + 1),
        out_shape=[jax.ShapeDtypeStruct((N_SHARD,) + b.shape, b.dtype) for b in bigs]
        + [jax.ShapeDtypeStruct((N_SHARD,) + small.shape, small.dtype)],
        scratch_shapes=[dma((3 * nb,)), dma((3 * nb,)), dma((3 * nb,)), dma((3 * nb,)), dma((3,)), dma((3,))],
    )(*bigs, small)


SEM_SPEC = pl.BlockSpec(memory_space=pltpu.SEMAPHORE)
DATAFLOW_EFFECT = pltpu.SideEffectType.DATAFLOW_SIDE_EFFECTING
N_PEERS = N_DEV - 1


def _grad_copies(p_refs, land_refs, send_sems, recv_sems):
    x, y, c = _position()
    copies = []
    for a, (p_ref, land_ref) in enumerate(zip(p_refs, land_refs)):
        rh = p_ref.shape[1] // 2
        for k in range(1, N_DEV):
            px = 1 - x if k & 4 else x
            py = 1 - y if k & 2 else y
            pc = 1 - c if k & 1 else c
            copies.append(pltpu.make_async_remote_copy(
                src_ref=p_ref.at[2 * px + py, pl.ds(pc * rh, rh)], dst_ref=land_ref.at[k - 1],
                send_sem=send_sems.at[N_PEERS * a + k - 1], recv_sem=recv_sems.at[N_PEERS * a + k - 1],
                device_id=(px, py, pc), device_id_type=MESH))
    return copies


def _weight_copies(w_refs, land_refs, send_sems, recv_sems):
    x, y, c = _position()
    copies = []
    for a, (w_ref, land_ref) in enumerate(zip(w_refs, land_refs)):
        for j, (px, py) in enumerate(_other_chips(x, y)):
            copies.append(pltpu.make_async_remote_copy(
                src_ref=w_ref, dst_ref=land_ref.at[2 * x + y], send_sem=send_sems.at[3 * a + j],
                recv_sem=recv_sems.at[3 * a + j], device_id=(px, py, c), device_id_type=MESH))
    return copies


def _exchange_start(make_copies, srcs, lands, n_sems, name, after=None):
    n, m = len(srcs), len(lands)
    n_in = n + m + (after is not None)

    def body(*refs):
        send_sems, recv_sems, token = refs[n_in], refs[n_in + 1], refs[-1]
        for cp in make_copies(refs[:n], refs[n:n + m], send_sems, recv_sems):
            cp.start()
        token[...] = jnp.zeros(token.shape, token.dtype)

    hbm = lambda a: pltpu.with_memory_space_constraint(a, pltpu.HBM)
    dma = pltpu.SemaphoreType.DMA
    res = pl.pallas_call(
        body, name=name,
        in_specs=[HBM_SPEC] * (n + m) + ([] if after is None else [pl.BlockSpec(memory_space=pl.ANY)]),
        out_specs=[SEM_SPEC, SEM_SPEC] + [HBM_SPEC] * (n + m) + [pl.BlockSpec(memory_space=pltpu.VMEM)],
        out_shape=[dma((n_sems,)), dma((n_sems,))] + [pltpu.HBM(a.shape, a.dtype) for a in list(srcs) + list(lands)]
        + [jax.ShapeDtypeStruct((8, LANES), F32)],
        input_output_aliases={i: 2 + i for i in range(n + m)},
        compiler_params=pltpu.CompilerParams(has_side_effects=DATAFLOW_EFFECT),
    )(*[hbm(a) for a in srcs], *[hbm(a) for a in lands], *(() if after is None else (after,)))
    return res[0], res[1], list(res[2:2 + n]), list(res[2 + n:2 + n + m]), res[-1]


def _exchange_wait(make_copies, send_sems, recv_sems, srcs, lands, after, name):
    n, m = len(srcs), len(lands)

    def body(*refs):
        for cp in make_copies(refs[:n], refs[n:n + m], refs[n + m], refs[n + m + 1]):
            cp.wait_send()
            cp.wait_recv()

    res = pl.pallas_call(
        body, name=name,
        in_specs=[HBM_SPEC] * (n + m) + [SEM_SPEC, SEM_SPEC, pl.BlockSpec(memory_space=pl.ANY)],
        out_specs=[HBM_SPEC] * (n + m),
        out_shape=[pltpu.HBM(a.shape, a.dtype) for a in list(srcs) + list(lands)],
        input_output_aliases={i: i for i in range(n + m)},
        compiler_params=pltpu.CompilerParams(has_side_effects=DATAFLOW_EFFECT),
    )(*srcs, *lands, send_sems, recv_sems, after)
    return list(res[:n]), list(res[n:])


def _sum_partials(p, land, name):
    _, rh, cols = land.shape
    br = _pick_rows(rh, 256)
    nrb = rh // br
    x, y, c = _position()
    where = jnp.stack([2 * x + y, c]).astype(jnp.int32)

    def body(where_ref, p_ref, land_ref, o_ref):
        acc = p_ref[...].astype(F32)
        for k in range(N_PEERS):
            acc = acc + land_ref[k].astype(F32)
        o_ref[...] = acc

    return pl.pallas_call(
        body, name=name,
        grid_spec=pltpu.PrefetchScalarGridSpec(
            num_scalar_prefetch=1, grid=(nrb,),
            in_specs=[pl.BlockSpec((None, br, cols), lambda r, where_ref: (where_ref[0], where_ref[1] * nrb + r, 0)),
                      pl.BlockSpec((N_PEERS, br, cols), lambda r, where_ref: (0, r, 0))],
            out_specs=pl.BlockSpec((None, br, cols), lambda r, where_ref: (where_ref[1], r, 0))),
        out_shape=jax.ShapeDtypeStruct((2, rh, cols), F32),
        compiler_params=_params(("parallel",)),
    )(where, p, land)


def _sibling_share(fulls, name):
    n = len(fulls)

    def body(*refs):
        o_refs = refs[n:2 * n]
        send_sems, recv_sems = refs[2 * n:]
        x, y, c = _position()

        def copy(a, half):
            return pltpu.make_async_remote_copy(
                src_ref=o_refs[a].at[half], dst_ref=o_refs[a].at[half], send_sem=send_sems.at[a],
                recv_sem=recv_sems.at[a], device_id=(x, y, 1 - c), device_id_type=MESH)

        sends = [copy(a, c) for a in range(n)]
        for cp in sends:
            cp.start()
        for a in range(n):
            copy(a, 1 - c).wait_recv()
        for cp in sends:
            cp.wait_send()

    dma = pltpu.SemaphoreType.DMA
    return pl.pallas_call(
        body, name=name,
        in_specs=[HBM_SPEC] * n, out_specs=[HBM_SPEC] * n,
        out_shape=[jax.ShapeDtypeStruct(f.shape, f.dtype) for f in fulls],
        input_output_aliases={a: a for a in range(n)},
        scratch_shapes=[dma((n,)), dma((n,))],
    )(*fulls)


def _all_reduce_small(v):
    R, cols = v.shape

    def body(v_ref, o_ref, buf_ref, send_sems, recv_sems):
        x, y, c = _position()
        me = 4 * x + 2 * y + c
        buf_ref[me] = v_ref[...]
        sends = []
        for k in range(1, N_DEV):
            px = 1 - x if k & 4 else x
            py = 1 - y if k & 2 else y
            pc = 1 - c if k & 1 else c
            sends.append(pltpu.make_async_remote_copy(
                src_ref=v_ref, dst_ref=buf_ref.at[me], send_sem=send_sems.at[k - 1], recv_sem=recv_sems.at[k - 1],
                device_id=(px, py, pc), device_id_type=MESH))
        for cp in sends:
            cp.start()
        for k in range(1, N_DEV):
            px = 1 - x if k & 4 else x
            py = 1 - y if k & 2 else y
            pc = 1 - c if k & 1 else c
            pltpu.make_async_remote_copy(
                src_ref=v_ref, dst_ref=buf_ref.at[4 * px + 2 * py + pc], send_sem=send_sems.at[k - 1],
                recv_sem=recv_sems.at[k - 1], device_id=(px, py, pc), device_id_type=MESH).wait_recv()
        for cp in sends:
            cp.wait_send()
        acc = buf_ref[0]
        for d in range(1, N_DEV):
            acc = acc + buf_ref[d]
        o_ref[...] = acc

    return pl.pallas_call(
        body, name="small_grads_all_reduce",
        in_specs=[pl.BlockSpec(memory_space=pltpu.VMEM)], out_specs=pl.BlockSpec(memory_space=pltpu.VMEM),
        out_shape=jax.ShapeDtypeStruct((R, cols), F32),
        scratch_shapes=[pltpu.VMEM((N_DEV, R, cols), F32), pltpu.SemaphoreType.DMA((N_DEV - 1,)),
                        pltpu.SemaphoreType.DMA((N_DEV - 1,))],
    )(v)


def _rope_tables(S, half, width):
    inv_freq = ROPE_THETA ** (-jnp.arange(half, dtype=F32) / half)
    ang = jnp.arange(S).astype(F32)[:, None] * inv_freq[None, :]
    return jnp.cos(ang), jnp.sin(ang)


def _slot_rows(a):
    return a.reshape(N_SHARD, -1, a.shape[-1])


def _local_step(x, target, w, B, S, late, exchange, reduce_small):
    T = B * S
    D = D_MODEL
    bm = 256
    full = lambda a, wd, tile=None: (a, wd, 0, tile or wd)
    g = {}

    cos_r, sin_r = _rope_tables(S, RET_QK // 2, LANES)
    cos_m, sin_m = _rope_tables(S, MLA_ROPE // 2, LANES)
    zeros64 = jnp.zeros((S, 64), F32)
    cos_m = jnp.concatenate([cos_m, cos_m, zeros64], axis=1)
    sin_m = jnp.concatenate([-sin_m, sin_m, zeros64], axis=1)

    def ffn_fwd(xin, h, ht, i, next_gain):
        w.update(late(f"ffn{i}", xin))
        norm = w["ffn_norm"][i:i + 1]
        ag = _mm(h, w[f"ffn_w_in{i}"], "nn", BF16, f"ffn{i}_in", bm=1024, bn=1408, cols_outer=True)
        u, ut = _conv_fwd(ag, w["ffn_conv8"][i], B, S, f"ffn{i}_conv")
        if next_gain is None:
            out = (_mm(u, w[f"ffn_w_out{i}"], "nn", F32, f"ffn{i}_out", residual=xin, bk=FFN_DIM),)
        else:
            out = _mm_out_norm(u, w[f"ffn_w_out{i}"], xin, next_gain, f"ffn{i}_out")
        return out, (xin, norm, ht, ag, ut)

    def ffn_bwd(dxout, dxout_c, saved, i):
        xin, norm, ht, ag, ut = saved
        du = _mm(dxout_c, w[f"ffn_w_out{i}"], "nt", F32, f"ffn{i}_out_dx", bm=1024, bn=1408, cols_outer=True)
        g_w_out = _mm(ut, dxout_c, "nn", BF16, f"ffn{i}_out_dw", bm=1408, bn=512, bk=T)
        da, dg, dw8 = _conv_bwd(ag, w["ffn_conv8"][i], du, B, S, f"ffn{i}_conv_bwd")
        g_w_in = _mm(ht, [da, dg], "nn", BF16, f"ffn{i}_in_dw", bm=1024, bn=1408, bk=T // 2, out_slots=N_SHARD)
        token = exchange(f"ffn{i}", [g_w_in, _slot_rows(g_w_out)])
        dxin, dxin_c, g_norm = _mm_dx_norm([da, dg], w[f"ffn_w_in{i}"], xin, norm, dxout, f"ffn{i}_in_dx", after=token)
        return dxin, dxin_c, (g_norm, dw8)

    h0, h0t = _rowwise_fwd(_fn_rms, "ret_norm", [full(x, D)], [], [(w["ret_norm"], D)], [(D, D, BF16)], bm, S,
                           transposed=(0,))
    proj = _mm(h0, w["ret_w_in"], "nn", BF16, "ret_in", bm=1024, after=w["started"], cols_outer=True)
    HQ, HV = RET_HEADS * RET_QK, RET_HEADS * RET_V
    rope_rows = [(proj, 2 * HQ + HV, 0, LANES)]
    q_r, k_r, v_r = _rowwise_fwd(_fn_ret_rope, "ret_rope", rope_rows, [cos_r, sin_r], [],
                                 [(HQ, LANES, BF16), (HQ, LANES, BF16), (HV, LANES, BF16)], bm, S)
    ret_o = _ret_attn_fwd(q_r, k_r, v_r, B, S)
    gate_rows = [full(ret_o, HV, RET_V), (proj, HV, 2, RET_V)]
    y0, y0t = _rowwise_fwd(_fn_ret_gate, "ret_gate", gate_rows, [], [(w["ret_gn"], RET_V)], [(HV, RET_V, BF16)], 128, S,
                           transposed=(0,))
    w.update(late("ret_out", y0))
    x1, h1, h1t = _mm_out_norm(y0, w["ret_w_out"], x, w["ffn_norm"][0:1], "ret_out")
    (x2, h2, _), ffn0_saved = ffn_fwd(x1, h1, h1t, 0, w["mla_norm"])

    w.update(late("mla", x2))
    proj2 = _mm(h2, w["mla_w_in"], "nn", F32, "mla_in", bm=2048)
    lat_consts = [(w["mla_q_norm"], LANES), (w["mla_kv_norm"], LANES)]
    cqn, ckvn, kr = _rowwise_fwd(_fn_mla_lat, "mla_latent_norm", [full(proj2, MLA_IN_PAD, LANES)], [], lat_consts,
                                 [(MLA_Q_RANK, LANES, BF16), (MLA_KV_RANK, LANES, BF16), (LANES, LANES, F32)], bm, S)
    qf = _mm(cqn, w["mla_w_qb"], "nn", BF16, "mla_qb", bm=2048, bn=2048)
    kvf = _mm(ckvn, w["mla_w_kvb"], "nn", BF16, "mla_kvb", bm=2048, bn=2048)
    HP, HVm = MLA_HEADS * MLA_PAD, MLA_HEADS * MLA_V
    head_rows = [full(qf, HP, LANES), full(kvf, HP, LANES), full(kr, LANES)]
    head_consts = [(w["mla_q_head_norm"], LANES), (w["mla_k_head_norm"], LANES)]
    q_a, k_a, v_a = _rowwise_fwd(_fn_mla_heads, "mla_heads", head_rows, [cos_m, sin_m], head_consts,
                                 [(HP, LANES, BF16), (HP, LANES, BF16), (HVm, LANES, BF16)], bm, S)
    att_o, lse = _mla_attn_fwd(q_a, k_a, v_a, B, S)
    x3, h3, h3t = _mm_out_norm(att_o, w["mla_w_out"], x2, w["ffn_norm"][1:2], "mla_out")
    (x4,), ffn1_saved = ffn_fwd(x3, h3, h3t, 1, None)

    dy, dy_c, loss = _loss_head(x4, target)

    dx3, dx3_c, (g_n1, dw8_1) = ffn_bwd(dy, dy_c, ffn1_saved, 1)

    d_att_o = _mm(dx3_c, w["mla_w_out"], "nt", F32, "mla_out_dx", bm=2048)
    g_mla_out = _mm(att_o, dx3_c, "tn", BF16, "mla_out_dw")
    dq_a, dk_a, dv_a = _mla_attn_bwd(q_a, k_a, v_a, att_o, d_att_o, lse, B, S)
    (dqf, dkvf, dkr), (g["mla_q_head_norm"], g["mla_k_head_norm"]) = _rowwise_bwd(
        _fn_mla_heads, "mla_heads_bwd", head_rows, [cos_m, sin_m], head_consts,
        [(dq_a, LANES), (dk_a, LANES), (dv_a, LANES)], 128, S, grad_dtypes=[BF16, BF16, F32])
    dcqn = _mm(dqf, w["mla_w_qb"], "nt", F32, "mla_qb_dx", bm=2048)
    g_qb = _mm(cqn, dqf, "tn", BF16, "mla_qb_dw")
    g_qb = _to_slots(_unpad_heads(g_qb, 1), 1).reshape(N_SHARD, MLA_Q_RANK, -1)
    dckvn = _mm(dkvf, w["mla_w_kvb"], "nt", F32, "mla_kvb_dx", bm=2048)
    g_kvb = _mm(ckvn, dkvf, "tn", BF16, "mla_kvb_dw", bn=512, out_slots=N_SHARD)
    (dproj2,), (g["mla_q_norm"], g["mla_kv_norm"]) = _rowwise_bwd(
        _fn_mla_lat, "mla_latent_norm_bwd", [full(proj2, MLA_IN_PAD, LANES)], [], lat_consts,
        [(dcqn, LANES), (dckvn, LANES), (dkr, LANES)], bm, S, grad_dtypes=[BF16])
    g_mla_in = _mm(h2, dproj2, "tn", BF16, "mla_in_dw")
    token = exchange("mla", [_slot_rows(g_mla_in[:, :MLA_IN]), g_qb, g_kvb, _slot_rows(g_mla_out)])
    dx2, dx2_c, g["mla_norm"] = _mm_dx_norm([dproj2], w["mla_w_in"], x2, w["mla_norm"], dx3, "mla_in_dx", bm=512,
                                            after=token)

    dx1, dx1_c, (g_n0, dw8_0) = ffn_bwd(dx2, dx2_c, ffn0_saved, 0)

    dy0 = _mm(dx1_c, w["ret_w_out"], "nt", F32, "ret_out_dx", bm=1024, cols_outer=True)
    g_ret_out = _mm(y0t, dx1_c, "nn", BF16, "ret_out_dw", bm=1024, bn=512, bk=T)
    token = exchange("reto", [_slot_rows(g_ret_out)])
    gn_behind = w["ret_gn"] + token[0:1, 0:1]
    (d_ret_o, dgate), (g["ret_gn"],) = _rowwise_bwd(_fn_ret_gate, "ret_gate_bwd", gate_rows, [], [(gn_behind, RET_V)],
                                                    [(dy0, RET_V)], 128, S, grad_dtypes=[F32, BF16])
    dq_r, dk_r, dv_r = _ret_attn_bwd(q_r, k_r, v_r, d_ret_o, B, S)
    (dqkv,), _ = _rowwise_bwd(_fn_ret_rope, "ret_rope_bwd", rope_rows, [cos_r, sin_r], [],
                              [(dq_r, LANES), (dk_r, LANES), (dv_r, LANES)], bm, S, grad_dtypes=[BF16], linear=True)
    dx, _, g["ret_norm"] = _mm_dx_norm([dqkv, dgate], w["ret_w_in"], x, w["ret_norm"], dx1, "ret_in_dx")
    g["ffn_norm"] = jnp.concatenate([g_n0, g_n1], axis=0)
    g["ffn_conv_w"] = jnp.stack([dw8_0[0:3], dw8_1[0:3]])
    g["ffn_conv_b"] = jnp.stack([dw8_0[3], dw8_1[3]])
    reduced_small = reduce_small(g)
    g_ret_in = _mm(h0t, [dqkv, dgate], "nn", BF16, "ret_in_dw", bn=512, bk=T, out_slots=N_SHARD, after=reduced_small)
    exchange("ret", [g_ret_in])
    return loss, dx, reduced_small


_SMALL_SHARDED = [("ret_gn", 2), ("mla_norm", 1), ("mla_q_norm", 1), ("mla_kv_norm", 1), ("ffn_conv_w", 2)]
_SMALL_REPLICATED = ["ret_norm", "mla_q_head_norm", "mla_k_head_norm", "ffn_norm", "ffn_conv_b"]
_SMALL_ALL = ["ret_norm", "ret_gn", "mla_norm", "mla_q_norm", "mla_kv_norm", "mla_q_head_norm", "mla_k_head_norm",
              "ffn_norm", "ffn_conv_w", "ffn_conv_b"]


def _to_slots(full, axis):
    shape = full.shape
    split = shape[:axis] + (N_SHARD, shape[axis] // N_SHARD) + shape[axis + 1:]
    return jnp.moveaxis(full.reshape(split), axis, 0).reshape(N_SHARD, -1)


def _from_slots(slots, shard_shape, axis):
    parts = jnp.moveaxis(slots.reshape((N_SHARD,) + tuple(shard_shape)), 0, axis)
    full = shard_shape[:axis] + (N_SHARD * shard_shape[axis],) + shard_shape[axis + 1:]
    return parts.reshape(full)


def _pad_rows(flat, cols, row_unit):
    n, L = flat.shape
    unit = cols * row_unit
    Lp = -(-L // unit) * unit
    if Lp != L:
        flat = jnp.concatenate([flat, jnp.zeros((n, Lp - L), flat.dtype)], axis=1)
    return flat.reshape(n, Lp // cols, cols)


def _pad_heads(a, axis):
    shape = a.shape
    a = a.reshape(shape[:axis] + (MLA_HEADS, MLA_QK) + shape[axis + 1:])
    pad = [(0, 0)] * a.ndim
    pad[axis + 1] = (0, MLA_PAD - MLA_QK)
    return jnp.pad(a, pad).reshape(shape[:axis] + (MLA_HEADS * MLA_PAD,) + shape[axis + 1:])


def _unpad_heads(a, axis):
    shape = a.shape
    a = a.reshape(shape[:axis] + (MLA_HEADS, MLA_PAD) + shape[axis + 1:])
    a = lax.slice_in_dim(a, 0, MLA_QK, axis=axis + 1)
    return a.reshape(shape[:axis] + (MLA_HEADS * MLA_QK,) + shape[axis + 1:])


def kernel(x, ret_norm, ret_w_in, ret_gn, ret_w_out, mla_norm, mla_w_in, mla_q_norm, mla_w_qb, mla_kv_norm, mla_w_kvb, mla_q_head_norm, mla_k_head_norm, mla_w_out, ffn_norm, ffn_w_in, ffn_conv_w, ffn_conv_b, ffn_w_out, loss_target, m_ret_norm, m_ret_w_in, m_ret_gn, m_ret_w_out, m_mla_norm, m_mla_w_in, m_mla_q_norm, m_mla_w_qb, m_mla_kv_norm, m_mla_w_kvb, m_mla_q_head_norm, m_mla_k_head_norm, m_mla_w_out, m_ffn_norm, m_ffn_w_in, m_ffn_conv_w, m_ffn_conv_b, m_ffn_w_out, v_ret_norm, v_ret_w_in, v_ret_gn, v_ret_w_out, v_mla_norm, v_mla_w_in, v_mla_q_norm, v_mla_w_qb, v_mla_kv_norm, v_mla_w_kvb, v_mla_q_head_norm, v_mla_k_head_norm, v_mla_w_out, v_ffn_norm, v_ffn_w_in, v_ffn_conv_w, v_ffn_conv_b, v_ffn_w_out):
    names = ["ret_norm", "ret_w_in", "ret_gn", "ret_w_out", "mla_norm", "mla_w_in", "mla_q_norm", "mla_w_qb",
             "mla_kv_norm", "mla_w_kvb", "mla_q_head_norm", "mla_k_head_norm", "mla_w_out", "ffn_norm", "ffn_w_in",
             "ffn_conv_w", "ffn_conv_b", "ffn_w_out"]
    shard = dict(zip(names, [ret_norm, ret_w_in, ret_gn, ret_w_out, mla_norm, mla_w_in, mla_q_norm, mla_w_qb,
                             mla_kv_norm, mla_w_kvb, mla_q_head_norm, mla_k_head_norm, mla_w_out, ffn_norm, ffn_w_in,
                             ffn_conv_w, ffn_conv_b, ffn_w_out]))
    mom_m = dict(zip(names, [m_ret_norm, m_ret_w_in, m_ret_gn, m_ret_w_out, m_mla_norm, m_mla_w_in, m_mla_q_norm,
                             m_mla_w_qb, m_mla_kv_norm, m_mla_w_kvb, m_mla_q_head_norm, m_mla_k_head_norm, m_mla_w_out,
                             m_ffn_norm, m_ffn_w_in, m_ffn_conv_w, m_ffn_conv_b, m_ffn_w_out]))
    mom_v = dict(zip(names, [v_ret_norm, v_ret_w_in, v_ret_gn, v_ret_w_out, v_mla_norm, v_mla_w_in, v_mla_q_norm,
                             v_mla_w_qb, v_mla_kv_norm, v_mla_w_kvb, v_mla_q_head_norm, v_mla_k_head_norm, v_mla_w_out,
                             v_ffn_norm, v_ffn_w_in, v_ffn_conv_w, v_ffn_conv_b, v_ffn_w_out]))
    B, S, D = x.shape
    T = B * S
    sx, sy = lax.axis_index("x"), lax.axis_index("y")
    me = 2 * sx + sy

    two_d = lambda a: a.reshape(-1, a.shape[-1])
    small_sizes = [int(np.prod(shard[n].shape)) for n, _ in _SMALL_SHARDED]
    small = jnp.concatenate([shard[n].reshape(1, -1) for n, _ in _SMALL_SHARDED], axis=1)
    small = _pad_rows(small, LANES, 8)[0]
    as_mxu = lambda a: two_d(a).astype(BF16)
    is_me = lax.broadcasted_iota(jnp.int32, (N_SHARD, 1, 1), 0) == me
    with_own = lambda gathered, own: jnp.where(is_me, own[None], gathered)
    by_cols = lambda a: jnp.moveaxis(a, 0, 1).reshape(a.shape[1], -1)
    by_rows = lambda a: a.reshape(-1, a.shape[-1])
    pad_in = lambda a: jnp.pad(by_rows(a), ((0, 0), (0, MLA_IN_PAD - MLA_IN)))
    pad_qb = lambda a: _pad_heads(by_cols(a), 1)
    ret_in_shard = as_mxu(shard["ret_w_in"])
    g_ret_in, gsmall = _all_gather_weights([ret_in_shard], small)
    later = [
        ("ret_out", [("ret_w_out", as_mxu(shard["ret_w_out"]), by_rows)]),
        ("ffn0", [("ffn_w_in0", as_mxu(shard["ffn_w_in"][0]), by_cols), ("ffn_w_out0", as_mxu(shard["ffn_w_out"][0]), by_rows)]),
        ("mla", [("mla_w_in", as_mxu(shard["mla_w_in"]), pad_in), ("mla_w_qb", as_mxu(shard["mla_w_qb"]), pad_qb),
                 ("mla_w_kvb", as_mxu(shard["mla_w_kvb"]), by_cols), ("mla_w_out", as_mxu(shard["mla_w_out"]), by_rows)]),
        ("ffn1", [("ffn_w_in1", as_mxu(shard["ffn_w_in"][1]), by_cols), ("ffn_w_out1", as_mxu(shard["ffn_w_out"][1]), by_rows)]),
    ]
    gathering = {}
    token = gsmall
    for group, items in later:
        shards = [s_ for _, s_, _ in items]
        lands = [lax.empty((N_SHARD,) + s_.shape, s_.dtype) for s_ in shards]
        send_sems, recv_sems, shards, lands, token = _exchange_start(
            _weight_copies, shards, lands, 3 * len(shards), f"weights_start_{group}", after=token)
        gathering[group] = (send_sems, recv_sems, shards, lands, items)

    def late(group, after):
        send_sems, recv_sems, shards, lands, items = gathering[group]
        shards, lands = _exchange_wait(_weight_copies, send_sems, recv_sems, shards, lands, after,
                                       f"weights_wait_{group}")
        return {key: full(with_own(l_, s_)) for (key, _, full), s_, l_ in zip(items, shards, lands)}

    gsmall = with_own(gsmall, small).reshape(N_SHARD, -1)
    wfull = {}
    off = 0
    for (n, ax), sz in zip(_SMALL_SHARDED, small_sizes):
        wfull[n] = _from_slots(gsmall[:, off:off + sz], shard[n].shape, ax)
        off += sz
    for n in _SMALL_REPLICATED:
        wfull[n] = shard[n]

    conv8 = jnp.concatenate([wfull["ffn_conv_w"], wfull["ffn_conv_b"][:, None, :],
                             jnp.zeros((2, 4, FFN_DIM), F32)], axis=1)
    w = {
        "started": token, "ret_norm": wfull["ret_norm"], "ret_w_in": by_cols(with_own(g_ret_in, ret_in_shard)),
        "ret_gn": wfull["ret_gn"].reshape(1, RET_HEADS * RET_V), "mla_norm": wfull["mla_norm"],
        "mla_q_norm": wfull["mla_q_norm"], "mla_kv_norm": wfull["mla_kv_norm"],
        "mla_q_head_norm": jnp.pad(wfull["mla_q_head_norm"], ((0, 0), (0, MLA_PAD - MLA_QK))),
        "mla_k_head_norm": jnp.pad(wfull["mla_k_head_norm"], ((0, 0), (0, MLA_PAD - MLA_QK))),
        "ffn_norm": wfull["ffn_norm"], "ffn_conv8": conv8,
    }

    started = {}

    def exchange(group, arrays):
        lands = [lax.empty((N_PEERS, p.shape[1] // 2, p.shape[2]), p.dtype) for p in arrays]
        send_sems, recv_sems, ps, lands, token = _exchange_start(
            _grad_copies, arrays, lands, N_PEERS * len(arrays), f"grads_start_{group}")
        started[group] = (send_sems, recv_sems, ps, lands)
        return token

    small_shapes = {
        "ret_norm": (1, D_MODEL), "ret_gn": (1, RET_HEADS, RET_V), "mla_norm": (1, D_MODEL),
        "mla_q_norm": (1, MLA_Q_RANK), "mla_kv_norm": (1, MLA_KV_RANK), "mla_q_head_norm": (1, MLA_QK),
        "mla_k_head_norm": (1, MLA_QK), "ffn_norm": (2, D_MODEL), "ffn_conv_w": (2, 3, FFN_DIM),
        "ffn_conv_b": (2, FFN_DIM)}

    def reduce_small(gl):
        gl = dict(gl, mla_q_head_norm=gl["mla_q_head_norm"][:, :MLA_QK], mla_k_head_norm=gl["mla_k_head_norm"][:, :MLA_QK])
        packed = jnp.concatenate([gl[n].reshape(1, -1) for n in _SMALL_ALL], axis=1)
        return _all_reduce_small(_pad_rows(packed, LANES, 8)[0])

    loss_part, dx, gsm = _local_step(x.reshape(T, D), loss_target.reshape(T, D), w, B, S, late, exchange,
                                     reduce_small)
    loss = lax.psum(loss_part, ("x", "y", "c"))

    delta, new_m, new_v, grads = {}, {}, {}, {}

    def reduced(group, after):
        send_sems, recv_sems, ps, lands = started[group]
        ps, lands = _exchange_wait(_grad_copies, send_sems, recv_sems, ps, lands, after, f"grads_wait_{group}")
        halves = [_sum_partials(p_, l_, f"grads_sum_{group}_{i}") for i, (p_, l_) in enumerate(zip(ps, lands))]
        return [two_d(r) for r in _sibling_share(halves, f"grads_share_{group}")]

    def adamw(n, g_):
        shp = shard[n].shape
        grads[n] = g_.reshape(shp)
        flat = lambda a: a.reshape(-1, shp[-1])
        d_, m_, v_ = _adamw(flat(shard[n]), flat(grads[n]), flat(mom_m[n]), flat(mom_v[n]), f"adamw_{n}")
        delta[n], new_m[n], new_v[n] = d_.reshape(shp), m_.reshape(shp), v_.reshape(shp)
        return d_

    ffn1 = reduced("ffn1", started["ret"][2][0])
    mla = reduced("mla", ffn1[0])
    ffn0 = reduced("ffn0", mla[0])
    reto = reduced("reto", ffn0[0])
    early = [adamw(n, g_) for n, g_ in zip(["mla_w_in", "mla_w_qb", "mla_w_kvb", "mla_w_out"], mla)]
    early.append(adamw("ffn_w_in", jnp.stack([ffn0[0], ffn1[0]])))
    early.append(adamw("ffn_w_out", jnp.stack([ffn0[1], ffn1[1]])))
    early.append(adamw("ret_w_out", reto[0]))
    ret = reduced("ret", jnp.stack([d_[0, 0] for d_ in early]))
    adamw("ret_w_in", ret[0])

    gsm = gsm.reshape(-1)
    sharded_axis = dict(_SMALL_SHARDED)
    off = 0
    for n in _SMALL_ALL:
        sz = int(np.prod(small_shapes[n]))
        gn = gsm[off:off + sz].reshape(small_shapes[n])
        off += sz
        if n in sharded_axis:
            ax = sharded_axis[n]
            width = shard[n].shape[ax]
            gn = lax.dynamic_slice_in_dim(gn, me * width, width, axis=ax)
        grads[n] = gn

    pack_small = lambda d: _pad_rows(jnp.concatenate([d[n].reshape(1, -1) for n in _SMALL_ALL], axis=1), LANES, 8)[0]
    d_, m_, v_ = _adamw(pack_small(shard), pack_small(grads), pack_small(mom_m), pack_small(mom_v), "adamw_small")
    off = 0
    for n in _SMALL_ALL:
        sz = int(np.prod(shard[n].shape))
        for dst, src in ((delta, d_), (new_m, m_), (new_v, v_)):
            dst[n] = src.reshape(-1)[off:off + sz].reshape(shard[n].shape)
        off += sz

    return (loss, dx.reshape(B, S, D), *[grads[n] for n in names], *[delta[n] for n in names],
            *[new_m[n] for n in names], *[new_v[n] for n in names])
```

```python
import functools

import numpy as np
import jax
import jax.numpy as jnp
from jax import lax
from jax.experimental import pallas as pl
from jax.experimental.pallas import tpu as pltpu

F32 = jnp.float32
BF16 = jnp.bfloat16
MXU_DTYPE = jnp.bfloat16

CHUNK = 64
RMS_EPS = 1e-6
ROPE_THETA = 10000.0
D_MODEL = 1024
RET_HEADS = 4
RET_QK = 256
RET_V = 512
RET_GAMMA_BASE = -5.0
MLA_HEADS = 8
MLA_Q_RANK = 384
MLA_KV_RANK = 256
MLA_NOPE = 128
MLA_ROPE = 64
MLA_V = 128
MLA_QK = MLA_NOPE + MLA_ROPE
MLA_PAD = 256
MLA_IN = MLA_Q_RANK + MLA_KV_RANK + MLA_ROPE
MLA_IN_PAD = MLA_IN + 64
MASK_VALUE = -1e30
FFN_DIM = 2816
ADAM_LR = 0.001
ADAM_B1 = 0.9
ADAM_B2 = 0.999
ADAM_EPS = 1e-08
ADAM_WD = 0.01
ADAM_STEP = 10

LANES = 128
MLA_FWD_BLOCK = 512
VMEM_LIMIT = 56 * 2 ** 20
N_SHARD = 4
N_DEV = 8

MESH = pl.DeviceIdType.MESH


def _params(sem=None, **kw):
    return pltpu.CompilerParams(dimension_semantics=sem, vmem_limit_bytes=VMEM_LIMIT, **kw)


def _pick(dim, target):
    if dim <= target:
        return dim
    best = None
    for d in range(LANES, target + 1, LANES):
        if dim % d == 0:
            best = d
    assert best is not None, (dim, target)
    return best


def _mm(a, b, dims, out_dtype, name, residual=None, bm=512, bn=1024, bk=2048, out_slots=None, after=None,
        cols_outer=False):
    a_parts = list(a) if isinstance(a, (list, tuple)) else [a]
    b_parts = list(b) if isinstance(b, (list, tuple)) else [b]
    parts_on_n = dims == "tn" or len(b_parts) > 1
    if parts_on_n:
        assert len(a_parts) == 1 and dims in ("tn", "nn")
        (K, M) = a_parts[0].shape if dims == "tn" else a_parts[0].shape[::-1]
        N = sum(p.shape[1] for p in b_parts)
        part_widths = [p.shape[1] for p in b_parts]
    else:
        assert len(b_parts) == 1
        M = a_parts[0].shape[0]
        K = sum(p.shape[1] for p in a_parts)
        N = b_parts[0].shape[1 if dims == "nn" else 0]
        part_widths = [p.shape[1] for p in a_parts]
    bm, bn, bk = _pick(M, bm), _pick(N, bn), _pick(K, min(bk, 1024) if dims == "tn" else bk)
    nk = K // bk
    unit = bn if parts_on_n else bk
    assert all(wd % unit == 0 for wd in part_widths), (name, part_widths, unit)
    bounds = np.cumsum([0] + [wd // unit for wd in part_widths])
    ranges = [(int(lo), int(hi)) for lo, hi in zip(bounds[:-1], bounds[1:])]

    def part_index(idx, lo, hi):
        return jnp.clip(idx - lo, 0, hi - lo - 1)

    if parts_on_n:
        if dims == "tn":
            a_specs = [pl.BlockSpec((bk, bm), lambda i, j, k: (k, i))]
            dn = (((0,), (0,)), ((), ()))
        else:
            a_specs = [pl.BlockSpec((bm, bk), lambda i, j, k: (i, k))]
            dn = (((1,), (0,)), ((), ()))
        b_specs = [pl.BlockSpec((bk, bn), functools.partial(lambda i, j, k, lo, hi: (k, part_index(j, lo, hi)), lo=lo, hi=hi))
                   for lo, hi in ranges]
    else:
        a_specs = [pl.BlockSpec((bm, bk), functools.partial(lambda i, j, k, lo, hi: (i, part_index(k, lo, hi)), lo=lo, hi=hi))
                   for lo, hi in ranges]
        if dims == "nt":
            b_specs = [pl.BlockSpec((bn, bk), lambda i, j, k: (j, k))]
        else:
            b_specs = [pl.BlockSpec((bk, bn), lambda i, j, k: (k, j))]
        dn = (((1,), (1 if dims == "nt" else 0,)), ((), ()))
    r_spec = pl.BlockSpec((bm, bn), lambda i, j, k: (i, j))
    if out_slots is None:
        o_spec, o_shape = r_spec, (M, N)
    else:
        ns = N // out_slots
        assert ns % bn == 0, (name, ns, bn)
        nbs = ns // bn
        o_spec = pl.BlockSpec((None, bm, bn), lambda i, j, k: (j // nbs, i, j % nbs))
        o_shape = (out_slots, M, ns)
    has_res = residual is not None
    na, nb = len(a_parts), len(b_parts)

    def body(*refs):
        a_refs, b_refs = refs[:na], refs[na:na + nb]
        r_ref = refs[na + nb] if has_res else None
        n_in = na + nb + has_res + (after is not None)
        o_ref = refs[n_in]
        acc_ref = refs[n_in + 1] if nk > 1 else None
        k = pl.program_id(2)

        def finish(acc):
            if has_res:
                acc = acc + r_ref[...].astype(F32)
            o_ref[...] = acc.astype(out_dtype)

        def compute(a_ref, b_ref):
            p = lax.dot_general(a_ref[...].astype(MXU_DTYPE), b_ref[...].astype(MXU_DTYPE), dn,
                                preferred_element_type=F32)
            if nk == 1:
                finish(p)
                return

            @pl.when(k == 0)
            def _():
                acc_ref[...] = p

            @pl.when(jnp.logical_and(k > 0, k < nk - 1))
            def _():
                acc_ref[...] += p

            @pl.when(k == nk - 1)
            def _():
                finish(acc_ref[...] + p)

        if len(ranges) == 1:
            compute(a_refs[0], b_refs[0])
        else:
            idx = pl.program_id(0 if cols_outer else 1) if parts_on_n else k
            for p, (lo, hi) in enumerate(ranges):
                @pl.when(jnp.logical_and(idx >= lo, idx < hi))
                def _(p=p):
                    compute(a_refs[0 if parts_on_n else p], b_refs[p if parts_on_n else 0])

    after_specs = [] if after is None else [pl.BlockSpec(after.shape, lambda i, j, k: (0, 0))]
    in_specs = a_specs + b_specs + ([r_spec] if has_res else []) + after_specs
    grid = (M // bm, N // bn, nk)
    if cols_outer:
        swap = lambda sp: pl.BlockSpec(sp.block_shape, functools.partial(lambda j, i, k, f: f(i, j, k), f=sp.index_map))
        in_specs, o_spec, grid = [swap(sp) for sp in in_specs], swap(o_spec), (grid[1], grid[0], nk)
    return pl.pallas_call(
        body, name=name, grid=grid,
        in_specs=in_specs, out_specs=o_spec,
        out_shape=jax.ShapeDtypeStruct(o_shape, out_dtype),
        scratch_shapes=[pltpu.VMEM((bm, bn), F32)] if nk > 1 else [],
        compiler_params=_params(("parallel", "parallel", "arbitrary")),
    )(*a_parts, *b_parts, *((residual,) if has_res else ()), *(() if after is None else (after,)))


def _mm_out_norm(a, w, residual, gain, name, bm=512):
    (M, K), N = a.shape, w.shape[1]
    bm = _pick(M, bm)

    def body(a_ref, w_ref, r_ref, g_ref, o_ref, h_ref, ht_ref):
        acc = lax.dot_general(a_ref[...].astype(MXU_DTYPE), w_ref[...].astype(MXU_DTYPE), _NN,
                              preferred_element_type=F32) + r_ref[...]
        o_ref[...] = acc
        hv = _fn_rms([[acc]], [], [[g_ref[...]]])[0][0]
        h_ref[...] = hv.astype(h_ref.dtype)
        ht_ref[...] = hv.T.astype(ht_ref.dtype)

    row = pl.BlockSpec((bm, N), lambda i: (i, 0))
    whole = lambda arr: pl.BlockSpec(arr.shape, lambda i: (0, 0))
    return pl.pallas_call(
        body, name=name, grid=(M // bm,),
        in_specs=[pl.BlockSpec((bm, K), lambda i: (i, 0)), whole(w), row, whole(gain)],
        out_specs=[row, row, pl.BlockSpec((N, bm), lambda i: (0, i))],
        out_shape=[jax.ShapeDtypeStruct((M, N), F32), jax.ShapeDtypeStruct((M, N), BF16),
                   jax.ShapeDtypeStruct((N, M), BF16)],
        compiler_params=_params(("parallel",)),
    )(a, w, residual, gain)


def _mm_dx_norm(a_parts, w, x, gain, add, name, bm=256, after=None):
    M = a_parts[0].shape[0]
    N, K = w.shape
    widths = [p.shape[1] for p in a_parts]
    assert sum(widths) == K, (name, widths, K)
    offs = [int(o) for o in np.cumsum([0] + widths[:-1])]
    bm = _pick(M, bm)
    na = len(a_parts)
    n_in = na + 4 + (after is not None)

    def body(*refs):
        w_ref, x_ref, g_ref, add_ref = refs[na:na + 4]
        dx_ref, dxc_ref, dg_ref = refs[n_in:n_in + 3]
        dh = None
        for a_ref, off, wd in zip(refs[:na], offs, widths):
            p = lax.dot_general(a_ref[...].astype(MXU_DTYPE), w_ref[:, off:off + wd].astype(MXU_DTYPE), _NT,
                                preferred_element_type=F32)
            dh = p if dh is None else dh + p
        _, vjp = jax.vjp(lambda xv, gv: _fn_rms([[xv]], [], [[gv]])[0][0], x_ref[...], g_ref[...])
        dxv, dgv = vjp(dh)
        dxv = dxv + add_ref[...]
        dx_ref[...] = dxv
        dxc_ref[...] = dxv.astype(dxc_ref.dtype)

        @pl.when(pl.program_id(0) == 0)
        def _():
            dg_ref[...] = dgv

        @pl.when(pl.program_id(0) > 0)
        def _():
            dg_ref[...] += dgv

    row = pl.BlockSpec((bm, N), lambda i: (i, 0))
    whole = lambda a: pl.BlockSpec(a.shape, lambda i: (0, 0))
    in_specs = [pl.BlockSpec((bm, wd), lambda i: (i, 0)) for wd in widths] + [whole(w), row, whole(gain), row]
    in_specs += [] if after is None else [whole(after)]
    return pl.pallas_call(
        body, name=name, grid=(M // bm,),
        in_specs=in_specs, out_specs=[row, row, whole(gain)],
        out_shape=[jax.ShapeDtypeStruct((M, N), F32), jax.ShapeDtypeStruct((M, N), BF16),
                   jax.ShapeDtypeStruct(gain.shape, F32)],
        compiler_params=_params(("arbitrary",)),
    )(*a_parts, w, x, gain, add, *(() if after is None else (after,)))


def _tiles(ref, width, tile):
    return [ref[:, t * tile:(t + 1) * tile].astype(F32) for t in range(width // tile)]


def _row_specs(rows, pos, consts, bm, S):
    npos_blocks = S // bm
    specs = [pl.BlockSpec((bm, w), functools.partial(lambda i, c: (i, c), c=cb)) for (_, w, cb, _) in rows]
    specs += [pl.BlockSpec((bm, p.shape[1]), lambda i: (i % npos_blocks, 0)) for p in pos]
    specs += [pl.BlockSpec(c.shape, lambda i: (0, 0)) for (c, _) in consts]
    return specs


def _rowwise_fwd(fn, name, rows, pos, consts, outs, bm, S, transposed=()):
    T = rows[0][0].shape[0]
    nr, npos, nc, no = len(rows), len(pos), len(consts), len(outs)

    def body(*refs):
        row_v = [_tiles(r, w, t) for r, (_, w, _, t) in zip(refs[:nr], rows)]
        pos_v = [r[...] for r in refs[nr:nr + npos]]
        const_v = [_tiles(r, c.shape[1], t) for r, (c, t) in zip(refs[nr + npos:nr + npos + nc], consts)]
        res = fn(row_v, pos_v, const_v)
        out_refs = refs[nr + npos + nc:]
        for o_ref, tiles, (w, t, dt) in zip(out_refs, res, outs):
            for k, v in enumerate(tiles):
                o_ref[:, k * t:(k + 1) * t] = v.astype(dt)
        for t_ref, a in zip(out_refs[no:], transposed):
            t = outs[a][1]
            for k, v in enumerate(res[a]):
                t_ref[k * t:(k + 1) * t, :] = v.T.astype(t_ref.dtype)

    return pl.pallas_call(
        body, name=name, grid=(T // bm,),
        in_specs=_row_specs(rows, pos, consts, bm, S),
        out_specs=[pl.BlockSpec((bm, w), lambda i: (i, 0)) for (w, _, _) in outs]
        + [pl.BlockSpec((outs[a][0], bm), lambda i: (0, i)) for a in transposed],
        out_shape=[jax.ShapeDtypeStruct((T, w), dt) for (w, _, dt) in outs]
        + [jax.ShapeDtypeStruct((outs[a][0], T), BF16) for a in transposed],
        compiler_params=_params(("parallel",)),
    )(*[r[0] for r in rows], *pos, *[c[0] for c in consts])


def _rowwise_bwd(fn, name, rows, pos, consts, cts, bm, S, adds=None, grad_dtypes=None, mxu_copies=(), linear=False):
    adds = adds or {}
    T = rows[0][0].shape[0]
    nr, npos, nc, nct = len(rows), len(pos), len(consts), len(cts)
    add_idx = sorted(adds)
    grad_dtypes = grad_dtypes or [F32] * nr

    def body(*refs):
        it = iter(refs)
        row_refs = [None if linear else next(it) for _ in range(nr)]
        pos_refs = [next(it) for _ in range(npos)]
        const_refs = [next(it) for _ in range(nc)]
        ct_refs = [next(it) for _ in range(nct)]
        add_refs = {k: next(it) for k in add_idx}
        drow_refs = [next(it) for _ in range(nr)]
        copy_refs = {a: next(it) for a in mxu_copies}
        dconst_refs = [next(it) for _ in range(nc)]
        if linear:
            row_v = [[jnp.zeros((bm, t), F32)] * (w // t) for (_, w, _, t) in rows]
        else:
            row_v = [_tiles(r, w, t) for r, (_, w, _, t) in zip(row_refs, rows)]
        pos_v = [r[...] for r in pos_refs]
        const_v = [_tiles(r, c.shape[1], t) for r, (c, t) in zip(const_refs, consts)]
        ct_v = [_tiles(r, c.shape[1], t) for r, (c, t) in zip(ct_refs, cts)]
        _, vjp = jax.vjp(lambda rv, cv: fn(rv, pos_v, cv), row_v, const_v)
        drows, dconsts = vjp(ct_v)
        for a, (d_ref, tiles, (_, w, _, t)) in enumerate(zip(drow_refs, drows, rows)):
            for k, v in enumerate(tiles):
                if a in add_refs:
                    v = v + add_refs[a][:, k * t:(k + 1) * t].astype(F32)
                d_ref[:, k * t:(k + 1) * t] = v.astype(d_ref.dtype)
                if a in copy_refs:
                    copy_refs[a][:, k * t:(k + 1) * t] = v.astype(BF16)
        first = pl.program_id(0) == 0
        for d_ref, tiles, (_, t) in zip(dconst_refs, dconsts, consts):
            for k, v in enumerate(tiles):
                @pl.when(first)
                def _(d_ref=d_ref, k=k, t=t, v=v):
                    d_ref[:, k * t:(k + 1) * t] = v

                @pl.when(jnp.logical_not(first))
                def _(d_ref=d_ref, k=k, t=t, v=v):
                    d_ref[:, k * t:(k + 1) * t] += v

    in_specs = _row_specs([] if linear else rows, pos, consts, bm, S)
    in_specs += [pl.BlockSpec((bm, c.shape[1]), lambda i: (i, 0)) for (c, _) in cts]
    in_specs += [pl.BlockSpec((bm, adds[k].shape[1]), lambda i: (i, 0)) for k in add_idx]
    out_specs = [pl.BlockSpec((bm, w), lambda i: (i, 0)) for (_, w, _, _) in rows]
    out_specs += [pl.BlockSpec((bm, rows[a][1]), lambda i: (i, 0)) for a in mxu_copies]
    out_specs += [pl.BlockSpec(c.shape, lambda i: (0, 0)) for (c, _) in consts]
    out_shape = [jax.ShapeDtypeStruct((T, w), dt) for (_, w, _, _), dt in zip(rows, grad_dtypes)]
    out_shape += [jax.ShapeDtypeStruct((T, rows[a][1]), BF16) for a in mxu_copies]
    out_shape += [jax.ShapeDtypeStruct(c.shape, F32) for (c, _) in consts]
    res = pl.pallas_call(
        body, name=name, grid=(T // bm,),
        in_specs=in_specs, out_specs=out_specs, out_shape=out_shape,
        compiler_params=_params(("arbitrary",)),
    )(*([] if linear else [r[0] for r in rows]), *pos, *[c[0] for c in consts], *[c[0] for c in cts],
      *[adds[k] for k in add_idx])
    n_rows = nr + len(mxu_copies)
    return res[:n_rows], res[n_rows:]


def _ssq(tiles):
    s = jnp.sum(tiles[0] * tiles[0], axis=-1, keepdims=True)
    for t in tiles[1:]:
        s = s + jnp.sum(t * t, axis=-1, keepdims=True)
    return s


def _sigmoid(x):
    return 0.5 * jnp.tanh(0.5 * x) + 0.5


def _fn_rms(rows, pos, consts):
    (x,), (g,) = rows[0], consts[0]
    r = lax.rsqrt(jnp.mean(x * x, axis=-1, keepdims=True) + RMS_EPS)
    return [[x * r * g]]


def _fn_ret_rope(rows, pos, consts):
    (qkv,) = rows
    nq = RET_HEADS * RET_QK // LANES
    q, k, v = qkv[:nq], qkv[nq:2 * nq], qkv[2 * nq:]
    cos, sin = pos

    def rot(t, scale):
        out = []
        for h in range(RET_HEADS):
            x1, x2 = t[2 * h], t[2 * h + 1]
            o1, o2 = x1 * cos - x2 * sin, x2 * cos + x1 * sin
            out += [o1, o2] if scale is None else [o1 * scale, o2 * scale]
        return out

    return [rot(q, None), rot(k, RET_QK ** -0.5), list(v)]


def _fn_ret_gate(rows, pos, consts):
    o, g = rows
    (gn,) = consts
    out = []
    for h in range(RET_HEADS):
        r = lax.rsqrt(jnp.mean(o[h] * o[h], axis=-1, keepdims=True) + RMS_EPS)
        out.append((o[h] * r * gn[h]) * (g[h] * _sigmoid(g[h])))
    return [out]


def _fn_mla_lat(rows, pos, consts):
    (p,) = rows
    gq, gkv = consts
    nq, nkv = MLA_Q_RANK // LANES, MLA_KV_RANK // LANES
    cq, ckv, kr = p[:nq], p[nq:nq + nkv], p[nq + nkv]
    rq = lax.rsqrt(_ssq(cq) / MLA_Q_RANK + RMS_EPS)
    rkv = lax.rsqrt(_ssq(ckv) / MLA_KV_RANK + RMS_EPS)
    return [[t * rq * g for t, g in zip(cq, gq)], [t * rkv * g for t, g in zip(ckv, gkv)], [kr]]


def _swap32_impl(x):
    lane = lax.broadcasted_iota(jnp.int32, x.shape, 1)
    up, down = pltpu.roll(x, LANES - 32, 1), pltpu.roll(x, 32, 1)
    return jnp.where(lane < 32, up, jnp.where(lane < 64, down, 0.0))


@jax.custom_vjp
def _swap32(x):
    return _swap32_impl(x)


_swap32.defvjp(lambda x: (_swap32_impl(x), None), lambda _, g: (_swap32_impl(g),))


def _fn_mla_heads(rows, pos, consts):
    qf, kvf, (kr,) = rows
    cos, sin = pos
    gq, gk = consts
    q_out, k_out, v_out = [], [], []
    for h in range(MLA_HEADS):
        q0, q1 = qf[2 * h], qf[2 * h + 1]
        r = lax.rsqrt(_ssq([q0, q1]) / MLA_QK + RMS_EPS)
        a0, a1 = q0 * r * gq[0], q1 * r * gq[1]
        a1 = a1 * cos + _swap32(a1) * sin
        q_out += [a0 * (MLA_QK ** -0.5), a1 * (MLA_QK ** -0.5)]
        k0 = kvf[2 * h]
        r = lax.rsqrt(_ssq([k0, kr]) / MLA_QK + RMS_EPS)
        b0, b1 = k0 * r * gk[0], kr * r * gk[1]
        k_out += [b0, b1 * cos + _swap32(b1) * sin]
        v_out.append(kvf[2 * h + 1])
    return [q_out, k_out, v_out]


def _shift_down(x, n):
    row = lax.broadcasted_iota(jnp.int32, x.shape, 0)
    return jnp.where(row >= n, pltpu.roll(x, n, 0), 0.0)


def _shift_up(x, n):
    rows = x.shape[0]
    row = lax.broadcasted_iota(jnp.int32, x.shape, 0)
    return jnp.where(row < rows - n, pltpu.roll(x, rows - n, 0), 0.0)


def _conv_blocks(S):
    cb = 256
    return cb, FFN_DIM // cb


def _conv_fwd(ag, w8, B, S, name):
    cb, ncb = _conv_blocks(S)

    def body(a_ref, g_ref, w_ref, u_ref, ut_ref):
        g = g_ref[...].astype(F32)
        w = w_ref[...]
        gc = w[0:1] * _shift_down(g, 2) + w[1:2] * _shift_down(g, 1) + w[2:3] * g + w[3:4]
        u = a_ref[...].astype(F32) * (gc * _sigmoid(gc))
        u_ref[...] = u.astype(u_ref.dtype)
        ut_ref[...] = u.T.astype(ut_ref.dtype)

    return pl.pallas_call(
        body, name=name, grid=(ncb, B),
        in_specs=[pl.BlockSpec((S, cb), lambda j, b: (b, j)),
                  pl.BlockSpec((S, cb), lambda j, b: (b, ncb + j)),
                  pl.BlockSpec((8, cb), lambda j, b: (0, j))],
        out_specs=[pl.BlockSpec((S, cb), lambda j, b: (b, j)), pl.BlockSpec((cb, S), lambda j, b: (j, b))],
        out_shape=[jax.ShapeDtypeStruct((B * S, FFN_DIM), BF16), jax.ShapeDtypeStruct((FFN_DIM, B * S), BF16)],
        compiler_params=_params(("parallel", "parallel")),
    )(ag, ag, w8)


def _conv_bwd(ag, w8, du, B, S, name):
    cb, ncb = _conv_blocks(S)

    def body(a_ref, g_ref, w_ref, du_ref, da_ref, dg_ref, dw_ref):
        g = g_ref[...].astype(F32)
        w = w_ref[...]
        g1, g2 = _shift_down(g, 1), _shift_down(g, 2)
        gc = w[0:1] * g2 + w[1:2] * g1 + w[2:3] * g + w[3:4]
        sg = _sigmoid(gc)
        du_v = du_ref[...]
        da_ref[...] = (du_v * (gc * sg)).astype(da_ref.dtype)
        dgc = du_v * a_ref[...].astype(F32) * (sg * (1.0 + gc * (1.0 - sg)))
        dg = w[2:3] * dgc + w[1:2] * _shift_up(dgc, 1) + w[0:1] * _shift_up(dgc, 2)
        dg_ref[...] = dg.astype(dg_ref.dtype)
        part = jnp.concatenate([
            jnp.sum(dgc * g2, axis=0, keepdims=True), jnp.sum(dgc * g1, axis=0, keepdims=True),
            jnp.sum(dgc * g, axis=0, keepdims=True), jnp.sum(dgc, axis=0, keepdims=True),
            jnp.zeros((4, cb), F32)], axis=0)

        @pl.when(pl.program_id(1) == 0)
        def _():
            dw_ref[...] = part

        @pl.when(pl.program_id(1) > 0)
        def _():
            dw_ref[...] += part

    blk = lambda j, b: (b, j)
    return pl.pallas_call(
        body, name=name, grid=(ncb, B),
        in_specs=[pl.BlockSpec((S, cb), blk),
                  pl.BlockSpec((S, cb), lambda j, b: (b, ncb + j)),
                  pl.BlockSpec((8, cb), lambda j, b: (0, j)),
                  pl.BlockSpec((S, cb), blk)],
        out_specs=[pl.BlockSpec((S, cb), blk), pl.BlockSpec((S, cb), blk),
                   pl.BlockSpec((8, cb), lambda j, b: (0, j))],
        out_shape=[jax.ShapeDtypeStruct((B * S, FFN_DIM), BF16), jax.ShapeDtypeStruct((B * S, FFN_DIM), BF16),
                   jax.ShapeDtypeStruct((8, FFN_DIM), F32)],
        compiler_params=_params(("parallel", "arbitrary")),
    )(ag, ag, w8, du)


_NT = (((1,), (1,)), ((), ()))
_NN = (((1,), (0,)), ((), ()))
_TN = (((0,), (0,)), ((), ()))


def _dot(a, b, dn):
    return lax.dot_general(a.astype(MXU_DTYPE), b.astype(MXU_DTYPE), dn, preferred_element_type=F32)


def _run_bits(n):
    bits, b = [], 1
    while b < n:
        bits.append(b)
        b *= 2
    return bits[::-1]


def _key_runs(n, nq, update):
    for bit in _run_bits(nq + 1):
        @pl.when((n & bit) != 0)
        def _(bit=bit):
            update(n & ~(2 * bit - 1), bit, (n & (bit - 1)) == 0)


def _earlier_runs(n, nq, update):
    for bit in _run_bits(nq):
        @pl.when((n & bit) != 0)
        def _(bit=bit):
            update(n & ~(2 * bit - 1), bit, False)


def _chunk_visible(shape, nblk, blk):
    key = lax.broadcasted_iota(jnp.int32, shape, 0) - (nblk - 1) * blk
    query = lax.broadcasted_iota(jnp.int32, shape, 1)
    return jnp.logical_or(key < 0, (key // CHUNK) <= (query // CHUNK))


def _mla_attn_fwd(q, k, v, B, S):
    blk = min(MLA_FWD_BLOCK, S)
    H, nq = MLA_HEADS, S // blk

    def body(q_ref, k_ref, v_ref, o_ref, lse_ref, m_ref, l_ref, acc_ref):
        def qblock(i, _):
            q_rows = pl.ds(pl.multiple_of(i * blk, blk), blk)
            qi = q_ref[q_rows, :]
            m_ref[...] = jnp.full(m_ref.shape, MASK_VALUE, F32)
            l_ref[...] = jnp.zeros(l_ref.shape, F32)
            acc_ref[...] = jnp.zeros(acc_ref.shape, F32)

            def keys(first, nblk, last):
                rows = pl.ds(pl.multiple_of(first * blk, blk), nblk * blk)
                s = _dot(k_ref[rows, :], qi, _NT)
                s = jnp.where(jnp.logical_or(_chunk_visible(s.shape, nblk, blk), jnp.logical_not(last)), s, MASK_VALUE)
                m = m_ref[...]
                m2 = jnp.maximum(m, jnp.max(s, axis=0, keepdims=True))
                alpha = jnp.exp(m - m2)
                p = jnp.exp(s - m2)
                l_ref[...] = alpha * l_ref[...] + jnp.sum(p, axis=0, keepdims=True)
                acc_ref[...] = alpha * acc_ref[...] + _dot(v_ref[rows, :], p, _TN)
                m_ref[...] = m2

            _key_runs(i + 1, nq, keys)
            l = l_ref[...]
            o_ref[q_rows, :] = (acc_ref[...] / l).T
            lse_ref[0, :, q_rows] = m_ref[...] + jnp.log(l)
            return 0

        lax.fori_loop(0, nq, qblock, 0)

    return pl.pallas_call(
        body, name="mla_attn_fwd", grid=(B, H),
        in_specs=[pl.BlockSpec((S, MLA_PAD), lambda b, h: (b, h)),
                  pl.BlockSpec((S, MLA_PAD), lambda b, h: (b, h)),
                  pl.BlockSpec((S, MLA_V), lambda b, h: (b, h))],
        out_specs=[pl.BlockSpec((S, MLA_V), lambda b, h: (b, h)),
                   pl.BlockSpec((1, 1, S), lambda b, h: (b * H + h, 0, 0))],
        out_shape=[jax.ShapeDtypeStruct((B * S, H * MLA_V), F32), jax.ShapeDtypeStruct((B * H, 1, S), F32)],
        scratch_shapes=[pltpu.VMEM((1, blk), F32), pltpu.VMEM((1, blk), F32), pltpu.VMEM((MLA_V, blk), F32)],
        compiler_params=_params(("parallel", "parallel")),
    )(q, k, v)


def _mla_attn_bwd(q, k, v, o, do, lse, B, S):
    blk = min(MLA_FWD_BLOCK, S)
    H, nq = MLA_HEADS, S // blk

    def body(q_ref, k_ref, v_ref, o_ref, do_ref, lse_ref, dq_ref, dk_ref, dv_ref, kt_ref, dqt_ref):
        dk_ref[...] = jnp.zeros(dk_ref.shape, F32)
        dv_ref[...] = jnp.zeros(dv_ref.shape, F32)
        for g in range(nq):
            kt_ref[g] = k_ref[g * blk:(g + 1) * blk, :].T

        def qblock(i, _):
            q_rows = pl.ds(pl.multiple_of(i * blk, blk), blk)
            qi = q_ref[q_rows, :]
            doi = do_ref[q_rows, :]
            delta = jnp.sum((doi * o_ref[q_rows, :]).T, axis=0, keepdims=True)
            lse_i = lse_ref[0, :, q_rows]
            doi = doi.astype(MXU_DTYPE)
            dqt_ref[...] = jnp.zeros(dqt_ref.shape, F32)

            def keys(first, nblk, last):
                rows = pl.ds(pl.multiple_of(first * blk, blk), nblk * blk)
                k_run, v_run = k_ref[rows, :], v_ref[rows, :]
                p = jnp.exp(_dot(k_run, qi, _NT) - lse_i)
                p = jnp.where(jnp.logical_or(_chunk_visible(p.shape, nblk, blk), jnp.logical_not(last)), p, 0.0)
                ds = (p * (_dot(v_run, doi, _NT) - delta)).astype(MXU_DTYPE)
                dk_ref[rows, :] += _dot(ds, qi, _NN)
                dv_ref[rows, :] += _dot(p, doi, _NN)
                for r in range(nblk):
                    dqt_ref[...] += _dot(kt_ref[first + r], ds[r * blk:(r + 1) * blk, :], _NN)

            _key_runs(i + 1, nq, keys)
            dq_ref[q_rows, :] = dqt_ref[...].T
            return 0

        lax.fori_loop(0, nq, qblock, 0)

    qk_spec = pl.BlockSpec((S, MLA_PAD), lambda b, h: (b, h))
    v_spec = pl.BlockSpec((S, MLA_V), lambda b, h: (b, h))
    return pl.pallas_call(
        body, name="mla_attn_bwd", grid=(B, H),
        in_specs=[qk_spec, qk_spec, v_spec, v_spec, v_spec,
                  pl.BlockSpec((1, 1, S), lambda b, h: (b * H + h, 0, 0))],
        out_specs=[qk_spec, qk_spec, v_spec],
        out_shape=[jax.ShapeDtypeStruct((B * S, H * MLA_PAD), F32), jax.ShapeDtypeStruct((B * S, H * MLA_PAD), F32),
                   jax.ShapeDtypeStruct((B * S, H * MLA_V), F32)],
        scratch_shapes=[pltpu.VMEM((nq, MLA_PAD, blk), q.dtype), pltpu.VMEM((MLA_PAD, blk), F32)],
        compiler_params=_params(("parallel", "parallel")),
    )(q, k, v, o, do, lse)


def _ret_log_gamma():
    lg = np.log1p(-np.exp2(RET_GAMMA_BASE - np.arange(RET_HEADS, dtype=np.float32))).astype(np.float32)
    return jnp.asarray(np.broadcast_to(lg[:, None, None], (RET_HEADS, 8, LANES)).copy())


RET_BLOCK = 512


def _ret_local_scale(lg, shape, blk, rising):
    local = lax.broadcasted_iota(jnp.int32, shape, 0) % blk
    return jnp.exp(lg * (local if rising else blk - 1 - local).astype(F32))


def _ret_pair_factor(lg, blk, steps):
    return jnp.exp(lg * (blk * (steps - 1) + 1).astype(F32))


def _ret_own_decay(lg, blk, transposed):
    a = lax.broadcasted_iota(jnp.int32, (blk, blk), 0)
    b = lax.broadcasted_iota(jnp.int32, (blk, blk), 1)
    query, key = (b, a) if transposed else (a, b)
    dec = jnp.exp(lg * jnp.abs(query - key).astype(F32))
    return jnp.where((key // CHUNK) <= (query // CHUNK), dec, 0.0)


def _ret_attn_fwd(q, k, v, B, S):
    blk = min(RET_BLOCK, S)
    H, nq = RET_HEADS, S // blk

    def body(lg_ref, q_ref, k_ref, v_ref, o_ref, ks_ref, dec_ref, acc_ref):
        lg = lg_ref[0, 0:1, 0:1]
        ks_ref[...] = (k_ref[...].astype(F32) * _ret_local_scale(lg, k_ref.shape, blk, False)).astype(ks_ref.dtype)
        dec_ref[...] = _ret_own_decay(lg, blk, False)

        def qblock(i, _):
            q_rows = pl.ds(pl.multiple_of(i * blk, blk), blk)
            qi = q_ref[q_rows, :]
            qs = (qi.astype(F32) * _ret_local_scale(lg, qi.shape, blk, True)).astype(qi.dtype)
            a = _dot(qi, k_ref[q_rows, :], _NT) * dec_ref[...]
            acc_ref[...] = _dot(a, v_ref[q_rows, :], _NN)

            def keys(first, nblk, _):
                rows = pl.ds(pl.multiple_of(first * blk, blk), nblk * blk)
                steps = i - first - lax.broadcasted_iota(jnp.int32, (1, nblk * blk), 1) // blk
                a = _dot(qs, ks_ref[rows, :], _NT) * _ret_pair_factor(lg, blk, steps)
                acc_ref[...] += _dot(a, v_ref[rows, :], _NN)

            _earlier_runs(i, nq, keys)
            o_ref[q_rows, :] = acc_ref[...]
            return 0

        lax.fori_loop(0, nq, qblock, 0)

    qk_spec = pl.BlockSpec((S, RET_QK), lambda b, h: (b, h))
    v_spec = pl.BlockSpec((S, RET_V), lambda b, h: (b, h))
    return pl.pallas_call(
        body, name="ret_attn_fwd", grid=(B, H),
        in_specs=[pl.BlockSpec((1, 8, LANES), lambda b, h: (h, 0, 0)), qk_spec, qk_spec, v_spec],
        out_specs=v_spec,
        out_shape=jax.ShapeDtypeStruct((B * S, H * RET_V), F32),
        scratch_shapes=[pltpu.VMEM((S, RET_QK), k.dtype), pltpu.VMEM((blk, blk), F32), pltpu.VMEM((blk, RET_V), F32)],
        compiler_params=_params(("parallel", "parallel")),
    )(_ret_log_gamma(), q, k, v)


def _ret_attn_bwd(q, k, v, do, B, S):
    blk = min(RET_BLOCK, S)
    H, nq = RET_HEADS, S // blk

    def body(lg_ref, q_ref, k_ref, v_ref, do_ref, dq_ref, dk_ref, dv_ref, ks_ref, kst_ref, dks_ref, dqt_ref, dec_ref):
        lg = lg_ref[0, 0:1, 0:1]
        dk_ref[...] = jnp.zeros(dk_ref.shape, F32)
        dv_ref[...] = jnp.zeros(dv_ref.shape, F32)
        dks_ref[...] = jnp.zeros(dks_ref.shape, F32)
        ks_ref[...] = (k_ref[...].astype(F32) * _ret_local_scale(lg, k_ref.shape, blk, False)).astype(ks_ref.dtype)
        for g in range(nq):
            kst_ref[g] = ks_ref[g * blk:(g + 1) * blk, :].T
        dec_ref[...] = _ret_own_decay(lg, blk, True)

        def qblock(i, _):
            q_rows = pl.ds(pl.multiple_of(i * blk, blk), blk)
            qi = q_ref[q_rows, :]
            q_scale = _ret_local_scale(lg, qi.shape, blk, True)
            qs = (qi.astype(F32) * q_scale).astype(qi.dtype)
            doi = do_ref[q_rows, :].astype(MXU_DTYPE)
            ki = k_ref[q_rows, :]
            dec = dec_ref[...]
            a = _dot(ki, qi, _NT) * dec
            da = (_dot(v_ref[q_rows, :], doi, _NT) * dec).astype(MXU_DTYPE)
            dv_ref[q_rows, :] += _dot(a, doi, _NN)
            dk_ref[q_rows, :] += _dot(da, qi, _NN)
            dq_own = _dot(da, ki, _TN)
            dqt_ref[...] = jnp.zeros(dqt_ref.shape, F32)

            def keys(first, nblk, _):
                for r in range(nblk):
                    g = first + r
                    rows = pl.ds(pl.multiple_of(g * blk, blk), blk)
                    c = _ret_pair_factor(lg, blk, i - g)
                    a = _dot(ks_ref[rows, :], qs, _NT) * c
                    da = (_dot(v_ref[rows, :], doi, _NT) * c).astype(MXU_DTYPE)
                    dv_ref[rows, :] += _dot(a, doi, _NN)
                    dks_ref[rows, :] += _dot(da, qs, _NN)
                    dqt_ref[...] += _dot(kst_ref[g], da, _NN)

            _earlier_runs(i, nq, keys)
            dq_ref[q_rows, :] = dqt_ref[...].T * q_scale + dq_own
            return 0

        lax.fori_loop(0, nq, qblock, 0)
        dk_ref[...] += dks_ref[...] * _ret_local_scale(lg, dks_ref.shape, blk, False)

    qk_spec = pl.BlockSpec((S, RET_QK), lambda b, h: (b, h))
    v_spec = pl.BlockSpec((S, RET_V), lambda b, h: (b, h))
    return pl.pallas_call(
        body, name="ret_attn_bwd", grid=(B, H),
        in_specs=[pl.BlockSpec((1, 8, LANES), lambda b, h: (h, 0, 0)), qk_spec, qk_spec, v_spec, v_spec],
        out_specs=[qk_spec, qk_spec, v_spec],
        out_shape=[jax.ShapeDtypeStruct((B * S, H * RET_QK), F32), jax.ShapeDtypeStruct((B * S, H * RET_QK), F32),
                   jax.ShapeDtypeStruct((B * S, H * RET_V), F32)],
        scratch_shapes=[pltpu.VMEM((S, RET_QK), k.dtype), pltpu.VMEM((nq, RET_QK, blk), k.dtype),
                        pltpu.VMEM((S, RET_QK), F32), pltpu.VMEM((RET_QK, blk), F32), pltpu.VMEM((blk, blk), F32)],
        compiler_params=_params(("parallel", "parallel")),
    )(_ret_log_gamma(), q, k, v, do)


def _loss_head(y, target, bm=512):
    T, D = y.shape
    bm = _pick(T, bm)

    def body(y_ref, t_ref, dy_ref, dyc_ref, l_ref):
        err = y_ref[...] - t_ref[...]
        dy_ref[...] = err / D
        dyc_ref[...] = (err / D).astype(dyc_ref.dtype)
        part = jnp.full((8, LANES), 0.5 * jnp.sum(jnp.mean(err * err, axis=-1)), F32)

        @pl.when(pl.program_id(0) == 0)
        def _():
            l_ref[...] = part

        @pl.when(pl.program_id(0) > 0)
        def _():
            l_ref[...] += part

    blk = pl.BlockSpec((bm, D), lambda i: (i, 0))
    dy, dyc, l = pl.pallas_call(
        body, name="loss_head", grid=(T // bm,),
        in_specs=[blk, blk], out_specs=[blk, blk, pl.BlockSpec((8, LANES), lambda i: (0, 0))],
        out_shape=[jax.ShapeDtypeStruct((T, D), F32), jax.ShapeDtypeStruct((T, D), BF16),
                   jax.ShapeDtypeStruct((8, LANES), F32)],
        compiler_params=_params(("arbitrary",)),
    )(y, target)
    return dy, dyc, l[0, 0]


def _adamw(w, g, m, v, name):
    R, C = w.shape
    br = R if R * C * 4 <= 2 ** 21 else _pick_rows(R, max(8, (2 ** 21) // (C * 4)))

    def body(w_ref, g_ref, m_ref, v_ref, d_ref, mo_ref, vo_ref):
        g_v = g_ref[...]
        m_v = ADAM_B1 * m_ref[...] + (1.0 - ADAM_B1) * g_v
        v_v = ADAM_B2 * v_ref[...] + (1.0 - ADAM_B2) * (g_v * g_v)
        m_hat = m_v / (1.0 - ADAM_B1 ** ADAM_STEP)
        v_hat = v_v / (1.0 - ADAM_B2 ** ADAM_STEP)
        d_ref[...] = -ADAM_LR * (m_hat / (jnp.sqrt(v_hat) + ADAM_EPS) + ADAM_WD * w_ref[...])
        mo_ref[...] = m_v
        vo_ref[...] = v_v

    blk = pl.BlockSpec((br, C), lambda i: (i, 0))
    return pl.pallas_call(
        body, name=name, grid=(R // br,),
        in_specs=[blk] * 4, out_specs=[blk] * 3,
        out_shape=[jax.ShapeDtypeStruct((R, C), F32)] * 3,
        compiler_params=_params(("parallel",)),
    )(w, g, m, v)


def _pick_rows(R, target):
    best = None
    for d in range(8, min(R, target) + 1, 8):
        if R % d == 0:
            best = d
    assert best is not None, (R, target)
    return best


def _position():
    return lax.axis_index("x"), lax.axis_index("y"), lax.axis_index("c")


HBM_SPEC = pl.BlockSpec(memory_space=pltpu.HBM)


def _other_chips(x, y):
    return [(1 - x, y), (x, 1 - y), (1 - x, 1 - y)]


def _all_gather_weights(bigs, small):
    nb = len(bigs)

    def body(*refs):
        big_refs, small_ref = refs[:nb], refs[nb]
        obig, osmall = refs[nb + 1:2 * nb + 1], refs[2 * nb + 1]
        ici_send, ici_recv, d2d_send, d2d_recv, sm_send, sm_recv = refs[2 * nb + 2:]
        x, y, c = _position()
        me = 2 * x + y
        chips = _other_chips(x, y)

        def rows(n, half):
            rh = bigs[n].shape[0] // 2
            return pl.ds(half * rh, rh)

        def over_ici(n, j, slot, from_shard):
            px, py = chips[j]
            dst = obig[n].at[slot, rows(n, c)]
            return pltpu.make_async_remote_copy(
                src_ref=big_refs[n].at[rows(n, c)] if from_shard else dst, dst_ref=dst,
                send_sem=ici_send.at[3 * n + j], recv_sem=ici_recv.at[3 * n + j],
                device_id=(px, py, c), device_id_type=MESH)

        def over_d2d(n, j, half):
            px, py = chips[j]
            part = obig[n].at[2 * px + py, rows(n, half)]
            return pltpu.make_async_remote_copy(
                src_ref=part, dst_ref=part, send_sem=d2d_send.at[3 * n + j], recv_sem=d2d_recv.at[3 * n + j],
                device_id=(x, y, 1 - c), device_id_type=MESH)

        def small_copy(j, slot):
            px, py = chips[j]
            return pltpu.make_async_remote_copy(
                src_ref=small_ref, dst_ref=osmall.at[slot], send_sem=sm_send.at[j], recv_sem=sm_recv.at[j],
                device_id=(px, py, c), device_id_type=MESH)

        sends = [over_ici(n, j, me, True) for n in range(nb) for j in range(3)]
        sends += [small_copy(j, me) for j in range(3)]
        for cp in sends:
            cp.start()
        passed = []
        for n in range(nb):
            for j, (px, py) in enumerate(chips):
                over_ici(n, j, 2 * px + py, False).wait_recv()
                fwd = over_d2d(n, j, c)
                fwd.start()
                passed.append(fwd)
        for n in range(nb):
            for j in range(3):
                over_d2d(n, j, 1 - c).wait_recv()
        for j, (px, py) in enumerate(chips):
            small_copy(j, 2 * px + py).wait_recv()
        for cp in sends + passed:
            cp.wait_send()

    dma = pltpu.SemaphoreType.DMA
    return pl.pallas_call(
        body, name="weights_all_gather",
        in_specs=[HBM_SPEC] * (nb + 1), out_specs=[HBM_SPEC] * (nb + 1),
        out_shape=[jax.ShapeDtypeStruct((N_SHARD,) + b.shape, b.dtype) for b in bigs]
        + [jax.ShapeDtypeStruct((N_SHARD,) + small.shape, small.dtype)],
        scratch_shapes=[dma((3 * nb,)), dma((3 * nb,)), dma((3 * nb,)), dma((3 * nb,)), dma((3,)), dma((3,))],
    )(*bigs, small)


SEM_SPEC = pl.BlockSpec(memory_space=pltpu.SEMAPHORE)
DATAFLOW_EFFECT = pltpu.SideEffectType.DATAFLOW_SIDE_EFFECTING
N_PEERS = N_DEV - 1


def _grad_copies(p_refs, land_refs, send_sems, recv_sems):
    x, y, c = _position()
    copies = []
    for a, (p_ref, land_ref) in enumerate(zip(p_refs, land_refs)):
        rh = p_ref.shape[1] // 2
        for k in range(1, N_DEV):
            px = 1 - x if k & 4 else x
            py = 1 - y if k & 2 else y
            pc = 1 - c if k & 1 else c
            copies.append(pltpu.make_async_remote_copy(
                src_ref=p_ref.at[2 * px + py, pl.ds(pc * rh, rh)], dst_ref=land_ref.at[k - 1],
                send_sem=send_sems.at[N_PEERS * a + k - 1], recv_sem=recv_sems.at[N_PEERS * a + k - 1],
                device_id=(px, py, pc), device_id_type=MESH))
    return copies


def _weight_copies(w_refs, land_refs, send_sems, recv_sems):
    x, y, c = _position()
    copies = []
    for a, (w_ref, land_ref) in enumerate(zip(w_refs, land_refs)):
        for j, (px, py) in enumerate(_other_chips(x, y)):
            copies.append(pltpu.make_async_remote_copy(
                src_ref=w_ref, dst_ref=land_ref.at[2 * x + y], send_sem=send_sems.at[3 * a + j],
                recv_sem=recv_sems.at[3 * a + j], device_id=(px, py, c), device_id_type=MESH))
    return copies


def _exchange_start(make_copies, srcs, lands, n_sems, name, after=None):
    n, m = len(srcs), len(lands)
    n_in = n + m + (after is not None)

    def body(*refs):
        send_sems, recv_sems, token = refs[n_in], refs[n_in + 1], refs[-1]
        for cp in make_copies(refs[:n], refs[n:n + m], send_sems, recv_sems):
            cp.start()
        token[...] = jnp.zeros(token.shape, token.dtype)

    hbm = lambda a: pltpu.with_memory_space_constraint(a, pltpu.HBM)
    dma = pltpu.SemaphoreType.DMA
    res = pl.pallas_call(
        body, name=name,
        in_specs=[HBM_SPEC] * (n + m) + ([] if after is None else [pl.BlockSpec(memory_space=pl.ANY)]),
        out_specs=[SEM_SPEC, SEM_SPEC] + [HBM_SPEC] * (n + m) + [pl.BlockSpec(memory_space=pltpu.VMEM)],
        out_shape=[dma((n_sems,)), dma((n_sems,))] + [pltpu.HBM(a.shape, a.dtype) for a in list(srcs) + list(lands)]
        + [jax.ShapeDtypeStruct((8, LANES), F32)],
        input_output_aliases={i: 2 + i for i in range(n + m)},
        compiler_params=pltpu.CompilerParams(has_side_effects=DATAFLOW_EFFECT),
    )(*[hbm(a) for a in srcs], *[hbm(a) for a in lands], *(() if after is None else (after,)))
    return res[0], res[1], list(res[2:2 + n]), list(res[2 + n:2 + n + m]), res[-1]


def _exchange_wait(make_copies, send_sems, recv_sems, srcs, lands, after, name):
    n, m = len(srcs), len(lands)

    def body(*refs):
        for cp in make_copies(refs[:n], refs[n:n + m], refs[n + m], refs[n + m + 1]):
            cp.wait_send()
            cp.wait_recv()

    res = pl.pallas_call(
        body, name=name,
        in_specs=[HBM_SPEC] * (n + m) + [SEM_SPEC, SEM_SPEC, pl.BlockSpec(memory_space=pl.ANY)],
        out_specs=[HBM_SPEC] * (n + m),
        out_shape=[pltpu.HBM(a.shape, a.dtype) for a in list(srcs) + list(lands)],
        input_output_aliases={i: i for i in range(n + m)},
        compiler_params=pltpu.CompilerParams(has_side_effects=DATAFLOW_EFFECT),
    )(*srcs, *lands, send_sems, recv_sems, after)
    return list(res[:n]), list(res[n:])


def _sum_partials(p, land, name):
    _, rh, cols = land.shape
    br = _pick_rows(rh, 256)
    nrb = rh // br
    x, y, c = _position()
    where = jnp.stack([2 * x + y, c]).astype(jnp.int32)

    def body(where_ref, p_ref, land_ref, o_ref):
        acc = p_ref[...].astype(F32)
        for k in range(N_PEERS):
            acc = acc + land_ref[k].astype(F32)
        o_ref[...] = acc

    return pl.pallas_call(
        body, name=name,
        grid_spec=pltpu.PrefetchScalarGridSpec(
            num_scalar_prefetch=1, grid=(nrb,),
            in_specs=[pl.BlockSpec((None, br, cols), lambda r, where_ref: (where_ref[0], where_ref[1] * nrb + r, 0)),
                      pl.BlockSpec((N_PEERS, br, cols), lambda r, where_ref: (0, r, 0))],
            out_specs=pl.BlockSpec((None, br, cols), lambda r, where_ref: (where_ref[1], r, 0))),
        out_shape=jax.ShapeDtypeStruct((2, rh, cols), F32),
        compiler_params=_params(("parallel",)),
    )(where, p, land)


def _sibling_share(fulls, name):
    n = len(fulls)

    def body(*refs):
        o_refs = refs[n:2 * n]
        send_sems, recv_sems = refs[2 * n:]
        x, y, c = _position()

        def copy(a, half):
            return pltpu.make_async_remote_copy(
                src_ref=o_refs[a].at[half], dst_ref=o_refs[a].at[half], send_sem=send_sems.at[a],
                recv_sem=recv_sems.at[a], device_id=(x, y, 1 - c), device_id_type=MESH)

        sends = [copy(a, c) for a in range(n)]
        for cp in sends:
            cp.start()
        for a in range(n):
            copy(a, 1 - c).wait_recv()
        for cp in sends:
            cp.wait_send()

    dma = pltpu.SemaphoreType.DMA
    return pl.pallas_call(
        body, name=name,
        in_specs=[HBM_SPEC] * n, out_specs=[HBM_SPEC] * n,
        out_shape=[jax.ShapeDtypeStruct(f.shape, f.dtype) for f in fulls],
        input_output_aliases={a: a for a in range(n)},
        scratch_shapes=[dma((n,)), dma((n,))],
    )(*fulls)


def _all_reduce_small(v):
    R, cols = v.shape

    def body(v_ref, o_ref, buf_ref, send_sems, recv_sems):
        x, y, c = _position()
        me = 4 * x + 2 * y + c
        buf_ref[me] = v_ref[...]
        sends = []
        for k in range(1, N_DEV):
            px = 1 - x if k & 4 else x
            py = 1 - y if k & 2 else y
            pc = 1 - c if k & 1 else c
            sends.append(pltpu.make_async_remote_copy(
                src_ref=v_ref, dst_ref=buf_ref.at[me], send_sem=send_sems.at[k - 1], recv_sem=recv_sems.at[k - 1],
                device_id=(px, py, pc), device_id_type=MESH))
        for cp in sends:
            cp.start()
        for k in range(1, N_DEV):
            px = 1 - x if k & 4 else x
            py = 1 - y if k & 2 else y
            pc = 1 - c if k & 1 else c
            pltpu.make_async_remote_copy(
                src_ref=v_ref, dst_ref=buf_ref.at[4 * px + 2 * py + pc], send_sem=send_sems.at[k - 1],
                recv_sem=recv_sems.at[k - 1], device_id=(px, py, pc), device_id_type=MESH).wait_recv()
        for cp in sends:
            cp.wait_send()
        acc = buf_ref[0]
        for d in range(1, N_DEV):
            acc = acc + buf_ref[d]
        o_ref[...] = acc

    return pl.pallas_call(
        body, name="small_grads_all_reduce",
        in_specs=[pl.BlockSpec(memory_space=pltpu.VMEM)], out_specs=pl.BlockSpec(memory_space=pltpu.VMEM),
        out_shape=jax.ShapeDtypeStruct((R, cols), F32),
        scratch_shapes=[pltpu.VMEM((N_DEV, R, cols), F32), pltpu.SemaphoreType.DMA((N_DEV - 1,)),
                        pltpu.SemaphoreType.DMA((N_DEV - 1,))],
    )(v)


def _rope_tables(S, half, width):
    inv_freq = ROPE_THETA ** (-jnp.arange(half, dtype=F32) / half)
    ang = jnp.arange(S).astype(F32)[:, None] * inv_freq[None, :]
    return jnp.cos(ang), jnp.sin(ang)


def _slot_rows(a):
    return a.reshape(N_SHARD, -1, a.shape[-1])


def _local_step(x, target, w, B, S, late, exchange, reduce_small):
    T = B * S
    D = D_MODEL
    bm = min(512, S)
    bm_wide = min(256, S)
    full = lambda a, wd, tile=None: (a, wd, 0, tile or wd)
    g = {}

    cos_r, sin_r = _rope_tables(S, RET_QK // 2, LANES)
    cos_m, sin_m = _rope_tables(S, MLA_ROPE // 2, LANES)
    zeros64 = jnp.zeros((S, 64), F32)
    cos_m = jnp.concatenate([cos_m, cos_m, zeros64], axis=1)
    sin_m = jnp.concatenate([-sin_m, sin_m, zeros64], axis=1)

    def ffn_fwd(xin, h, ht, i, next_gain):
        w.update(late(f"ffn{i}", xin))
        norm = w["ffn_norm"][i:i + 1]
        ag = _mm(h, w[f"ffn_w_in{i}"], "nn", BF16, f"ffn{i}_in", bm=1024, bn=1408, cols_outer=True)
        u, ut = _conv_fwd(ag, w["ffn_conv8"][i], B, S, f"ffn{i}_conv")
        if next_gain is None:
            out = (_mm(u, w[f"ffn_w_out{i}"], "nn", F32, f"ffn{i}_out", residual=xin, bk=FFN_DIM),)
        else:
            out = _mm_out_norm(u, w[f"ffn_w_out{i}"], xin, next_gain, f"ffn{i}_out")
        return out, (xin, norm, ht, ag, ut)

    def ffn_bwd(dxout, dxout_c, saved, i):
        xin, norm, ht, ag, ut = saved
        du = _mm(dxout_c, w[f"ffn_w_out{i}"], "nt", F32, f"ffn{i}_out_dx", bm=1024, bn=1408, cols_outer=True)
        g_w_out = _mm(ut, dxout_c, "nn", BF16, f"ffn{i}_out_dw", bm=1408, bn=512, bk=T)
        da, dg, dw8 = _conv_bwd(ag, w["ffn_conv8"][i], du, B, S, f"ffn{i}_conv_bwd")
        g_w_in = _mm(ht, [da, dg], "nn", BF16, f"ffn{i}_in_dw", bm=1024, bn=1408, bk=T // 2, out_slots=N_SHARD)
        token = exchange(f"ffn{i}", [g_w_in, _slot_rows(g_w_out)])
        dxin, dxin_c, g_norm = _mm_dx_norm([da, dg], w[f"ffn_w_in{i}"], xin, norm, dxout, f"ffn{i}_in_dx", after=token)
        return dxin, dxin_c, (g_norm, dw8)

    h0, h0t = _rowwise_fwd(_fn_rms, "ret_norm", [full(x, D)], [], [(w["ret_norm"], D)], [(D, D, BF16)], bm, S,
                           transposed=(0,))
    proj = _mm(h0, w["ret_w_in"], "nn", BF16, "ret_in", bm=1024, after=w["started"], cols_outer=True)
    HQ, HV = RET_HEADS * RET_QK, RET_HEADS * RET_V
    rope_rows = [(proj, 2 * HQ + HV, 0, LANES)]
    q_r, k_r, v_r = _rowwise_fwd(_fn_ret_rope, "ret_rope", rope_rows, [cos_r, sin_r], [],
                                 [(HQ, LANES, BF16), (HQ, LANES, BF16), (HV, LANES, BF16)], bm, S)
    ret_o = _ret_attn_fwd(q_r, k_r, v_r, B, S)
    gate_rows = [full(ret_o, HV, RET_V), (proj, HV, 2, RET_V)]
    y0, y0t = _rowwise_fwd(_fn_ret_gate, "ret_gate", gate_rows, [], [(w["ret_gn"], RET_V)], [(HV, RET_V, BF16)], bm, S,
                           transposed=(0,))
    w.update(late("ret_out", y0))
    x1, h1, h1t = _mm_out_norm(y0, w["ret_w_out"], x, w["ffn_norm"][0:1], "ret_out")
    (x2, h2, _), ffn0_saved = ffn_fwd(x1, h1, h1t, 0, w["mla_norm"])

    w.update(late("mla", x2))
    proj2 = _mm(h2, w["mla_w_in"], "nn", F32, "mla_in", bm=2048)
    lat_consts = [(w["mla_q_norm"], LANES), (w["mla_kv_norm"], LANES)]
    cqn, ckvn, kr = _rowwise_fwd(_fn_mla_lat, "mla_latent_norm", [full(proj2, MLA_IN_PAD, LANES)], [], lat_consts,
                                 [(MLA_Q_RANK, LANES, BF16), (MLA_KV_RANK, LANES, BF16), (LANES, LANES, F32)], bm, S)
    qf = _mm(cqn, w["mla_w_qb"], "nn", BF16, "mla_qb", bm=2048, bn=2048)
    kvf = _mm(ckvn, w["mla_w_kvb"], "nn", BF16, "mla_kvb", bm=2048, bn=2048)
    HP, HVm = MLA_HEADS * MLA_PAD, MLA_HEADS * MLA_V
    head_rows = [full(qf, HP, LANES), full(kvf, HP, LANES), full(kr, LANES)]
    head_consts = [(w["mla_q_head_norm"], LANES), (w["mla_k_head_norm"], LANES)]
    q_a, k_a, v_a = _rowwise_fwd(_fn_mla_heads, "mla_heads", head_rows, [cos_m, sin_m], head_consts,
                                 [(HP, LANES, BF16), (HP, LANES, BF16), (HVm, LANES, BF16)], bm, S)
    att_o, lse = _mla_attn_fwd(q_a, k_a, v_a, B, S)
    x3, h3, h3t = _mm_out_norm(att_o, w["mla_w_out"], x2, w["ffn_norm"][1:2], "mla_out")
    (x4,), ffn1_saved = ffn_fwd(x3, h3, h3t, 1, None)

    dy, dy_c, loss = _loss_head(x4, target)

    dx3, dx3_c, (g_n1, dw8_1) = ffn_bwd(dy, dy_c, ffn1_saved, 1)

    d_att_o = _mm(dx3_c, w["mla_w_out"], "nt", F32, "mla_out_dx", bm=2048)
    g_mla_out = _mm(att_o, dx3_c, "tn", BF16, "mla_out_dw")
    dq_a, dk_a, dv_a = _mla_attn_bwd(q_a, k_a, v_a, att_o, d_att_o, lse, B, S)
    (dqf, dkvf, dkr), (g["mla_q_head_norm"], g["mla_k_head_norm"]) = _rowwise_bwd(
        _fn_mla_heads, "mla_heads_bwd", head_rows, [cos_m, sin_m], head_consts,
        [(dq_a, LANES), (dk_a, LANES), (dv_a, LANES)], bm_wide, S, grad_dtypes=[BF16, BF16, F32])
    dcqn = _mm(dqf, w["mla_w_qb"], "nt", F32, "mla_qb_dx", bm=2048)
    g_qb = _mm(cqn, dqf, "tn", BF16, "mla_qb_dw")
    g_qb = _to_slots(_unpad_heads(g_qb, 1), 1).reshape(N_SHARD, MLA_Q_RANK, -1)
    dckvn = _mm(dkvf, w["mla_w_kvb"], "nt", F32, "mla_kvb_dx", bm=2048)
    g_kvb = _mm(ckvn, dkvf, "tn", BF16, "mla_kvb_dw", bn=512, out_slots=N_SHARD)
    (dproj2,), (g["mla_q_norm"], g["mla_kv_norm"]) = _rowwise_bwd(
        _fn_mla_lat, "mla_latent_norm_bwd", [full(proj2, MLA_IN_PAD, LANES)], [], lat_consts,
        [(dcqn, LANES), (dckvn, LANES), (dkr, LANES)], bm, S, grad_dtypes=[BF16])
    g_mla_in = _mm(h2, dproj2, "tn", BF16, "mla_in_dw")
    token = exchange("mla", [_slot_rows(g_mla_in[:, :MLA_IN]), g_qb, g_kvb, _slot_rows(g_mla_out)])
    dx2, dx2_c, g["mla_norm"] = _mm_dx_norm([dproj2], w["mla_w_in"], x2, w["mla_norm"], dx3, "mla_in_dx", bm=512,
                                            after=token)

    dx1, dx1_c, (g_n0, dw8_0) = ffn_bwd(dx2, dx2_c, ffn0_saved, 0)

    dy0 = _mm(dx1_c, w["ret_w_out"], "nt", F32, "ret_out_dx", bm=1024, cols_outer=True)
    g_ret_out = _mm(y0t, dx1_c, "nn", BF16, "ret_out_dw", bm=1024, bn=512, bk=T)
    token = exchange("reto", [_slot_rows(g_ret_out)])
    gn_behind = w["ret_gn"] + token[0:1, 0:1]
    (d_ret_o, dgate), (g["ret_gn"],) = _rowwise_bwd(_fn_ret_gate, "ret_gate_bwd", gate_rows, [], [(gn_behind, RET_V)],
                                                    [(dy0, RET_V)], bm_wide, S, grad_dtypes=[F32, BF16])
    dq_r, dk_r, dv_r = _ret_attn_bwd(q_r, k_r, v_r, d_ret_o, B, S)
    (dqkv,), _ = _rowwise_bwd(_fn_ret_rope, "ret_rope_bwd", rope_rows, [cos_r, sin_r], [],
                              [(dq_r, LANES), (dk_r, LANES), (dv_r, LANES)], bm, S, grad_dtypes=[BF16], linear=True)
    dx, _, g["ret_norm"] = _mm_dx_norm([dqkv, dgate], w["ret_w_in"], x, w["ret_norm"], dx1, "ret_in_dx")
    g["ffn_norm"] = jnp.concatenate([g_n0, g_n1], axis=0)
    g["ffn_conv_w"] = jnp.stack([dw8_0[0:3], dw8_1[0:3]])
    g["ffn_conv_b"] = jnp.stack([dw8_0[3], dw8_1[3]])
    reduced_small = reduce_small(g)
    g_ret_in = _mm(h0t, [dqkv, dgate], "nn", BF16, "ret_in_dw", bm=1024, bn=512, bk=T, out_slots=N_SHARD,
                   after=reduced_small)
    exchange("ret", [g_ret_in])
    return loss, dx, reduced_small


_SMALL_SHARDED = [("ret_gn", 2), ("mla_norm", 1), ("mla_q_norm", 1), ("mla_kv_norm", 1), ("ffn_conv_w", 2)]
_SMALL_REPLICATED = ["ret_norm", "mla_q_head_norm", "mla_k_head_norm", "ffn_norm", "ffn_conv_b"]
_SMALL_ALL = ["ret_norm", "ret_gn", "mla_norm", "mla_q_norm", "mla_kv_norm", "mla_q_head_norm", "mla_k_head_norm",
              "ffn_norm", "ffn_conv_w", "ffn_conv_b"]


def _to_slots(full, axis):
    shape = full.shape
    split = shape[:axis] + (N_SHARD, shape[axis] // N_SHARD) + shape[axis + 1:]
    return jnp.moveaxis(full.reshape(split), axis, 0).reshape(N_SHARD, -1)


def _from_slots(slots, shard_shape, axis):
    parts = jnp.moveaxis(slots.reshape((N_SHARD,) + tuple(shard_shape)), 0, axis)
    full = shard_shape[:axis] + (N_SHARD * shard_shape[axis],) + shard_shape[axis + 1:]
    return parts.reshape(full)


def _pad_rows(flat, cols, row_unit):
    n, L = flat.shape
    unit = cols * row_unit
    Lp = -(-L // unit) * unit
    if Lp != L:
        flat = jnp.concatenate([flat, jnp.zeros((n, Lp - L), flat.dtype)], axis=1)
    return flat.reshape(n, Lp // cols, cols)


def _pad_heads(a, axis):
    shape = a.shape
    a = a.reshape(shape[:axis] + (MLA_HEADS, MLA_QK) + shape[axis + 1:])
    pad = [(0, 0)] * a.ndim
    pad[axis + 1] = (0, MLA_PAD - MLA_QK)
    return jnp.pad(a, pad).reshape(shape[:axis] + (MLA_HEADS * MLA_PAD,) + shape[axis + 1:])


def _unpad_heads(a, axis):
    shape = a.shape
    a = a.reshape(shape[:axis] + (MLA_HEADS, MLA_PAD) + shape[axis + 1:])
    a = lax.slice_in_dim(a, 0, MLA_QK, axis=axis + 1)
    return a.reshape(shape[:axis] + (MLA_HEADS * MLA_QK,) + shape[axis + 1:])


def kernel(x, ret_norm, ret_w_in, ret_gn, ret_w_out, mla_norm, mla_w_in, mla_q_norm, mla_w_qb, mla_kv_norm, mla_w_kvb, mla_q_head_norm, mla_k_head_norm, mla_w_out, ffn_norm, ffn_w_in, ffn_conv_w, ffn_conv_b, ffn_w_out, loss_target, m_ret_norm, m_ret_w_in, m_ret_gn, m_ret_w_out, m_mla_norm, m_mla_w_in, m_mla_q_norm, m_mla_w_qb, m_mla_kv_norm, m_mla_w_kvb, m_mla_q_head_norm, m_mla_k_head_norm, m_mla_w_out, m_ffn_norm, m_ffn_w_in, m_ffn_conv_w, m_ffn_conv_b, m_ffn_w_out, v_ret_norm, v_ret_w_in, v_ret_gn, v_ret_w_out, v_mla_norm, v_mla_w_in, v_mla_q_norm, v_mla_w_qb, v_mla_kv_norm, v_mla_w_kvb, v_mla_q_head_norm, v_mla_k_head_norm, v_mla_w_out, v_ffn_norm, v_ffn_w_in, v_ffn_conv_w, v_ffn_conv_b, v_ffn_w_out):
    names = ["ret_norm", "ret_w_in", "ret_gn", "ret_w_out", "mla_norm", "mla_w_in", "mla_q_norm", "mla_w_qb",
             "mla_kv_norm", "mla_w_kvb", "mla_q_head_norm", "mla_k_head_norm", "mla_w_out", "ffn_norm", "ffn_w_in",
             "ffn_conv_w", "ffn_conv_b", "ffn_w_out"]
    shard = dict(zip(names, [ret_norm, ret_w_in, ret_gn, ret_w_out, mla_norm, mla_w_in, mla_q_norm, mla_w_qb,
                             mla_kv_norm, mla_w_kvb, mla_q_head_norm, mla_k_head_norm, mla_w_out, ffn_norm, ffn_w_in,
                             ffn_conv_w, ffn_conv_b, ffn_w_out]))
    mom_m = dict(zip(names, [m_ret_norm, m_ret_w_in, m_ret_gn, m_ret_w_out, m_mla_norm, m_mla_w_in, m_mla_q_norm,
                             m_mla_w_qb, m_mla_kv_norm, m_mla_w_kvb, m_mla_q_head_norm, m_mla_k_head_norm, m_mla_w_out,
                             m_ffn_norm, m_ffn_w_in, m_ffn_conv_w, m_ffn_conv_b, m_ffn_w_out]))
    mom_v = dict(zip(names, [v_ret_norm, v_ret_w_in, v_ret_gn, v_ret_w_out, v_mla_norm, v_mla_w_in, v_mla_q_norm,
                             v_mla_w_qb, v_mla_kv_norm, v_mla_w_kvb, v_mla_q_head_norm, v_mla_k_head_norm, v_mla_w_out,
                             v_ffn_norm, v_ffn_w_in, v_ffn_conv_w, v_ffn_conv_b, v_ffn_w_out]))
    B, S, D = x.shape
    T = B * S
    sx, sy = lax.axis_index("x"), lax.axis_index("y")
    me = 2 * sx + sy

    two_d = lambda a: a.reshape(-1, a.shape[-1])
    small_sizes = [int(np.prod(shard[n].shape)) for n, _ in _SMALL_SHARDED]
    small = jnp.concatenate([shard[n].reshape(1, -1) for n, _ in _SMALL_SHARDED], axis=1)
    small = _pad_rows(small, LANES, 8)[0]
    as_mxu = lambda a: two_d(a).astype(BF16)
    is_me = lax.broadcasted_iota(jnp.int32, (N_SHARD, 1, 1), 0) == me
    with_own = lambda gathered, own: jnp.where(is_me, own[None], gathered)
    by_cols = lambda a: jnp.moveaxis(a, 0, 1).reshape(a.shape[1], -1)
    by_rows = lambda a: a.reshape(-1, a.shape[-1])
    pad_in = lambda a: jnp.pad(by_rows(a), ((0, 0), (0, MLA_IN_PAD - MLA_IN)))
    pad_qb = lambda a: _pad_heads(by_cols(a), 1)
    ret_in_shard = as_mxu(shard["ret_w_in"])
    g_ret_in, gsmall = _all_gather_weights([ret_in_shard], small)
    later = [
        ("ret_out", [("ret_w_out", as_mxu(shard["ret_w_out"]), by_rows)]),
        ("ffn0", [("ffn_w_in0", as_mxu(shard["ffn_w_in"][0]), by_cols), ("ffn_w_out0", as_mxu(shard["ffn_w_out"][0]), by_rows)]),
        ("mla", [("mla_w_in", as_mxu(shard["mla_w_in"]), pad_in), ("mla_w_qb", as_mxu(shard["mla_w_qb"]), pad_qb),
                 ("mla_w_kvb", as_mxu(shard["mla_w_kvb"]), by_cols), ("mla_w_out", as_mxu(shard["mla_w_out"]), by_rows)]),
        ("ffn1", [("ffn_w_in1", as_mxu(shard["ffn_w_in"][1]), by_cols), ("ffn_w_out1", as_mxu(shard["ffn_w_out"][1]), by_rows)]),
    ]
    gathering = {}
    token = gsmall
    for group, items in later:
        shards = [s_ for _, s_, _ in items]
        lands = [lax.empty((N_SHARD,) + s_.shape, s_.dtype) for s_ in shards]
        send_sems, recv_sems, shards, lands, token = _exchange_start(
            _weight_copies, shards, lands, 3 * len(shards), f"weights_start_{group}", after=token)
        gathering[group] = (send_sems, recv_sems, shards, lands, items)

    def late(group, after):
        send_sems, recv_sems, shards, lands, items = gathering[group]
        shards, lands = _exchange_wait(_weight_copies, send_sems, recv_sems, shards, lands, after,
                                       f"weights_wait_{group}")
        return {key: full(with_own(l_, s_)) for (key, _, full), s_, l_ in zip(items, shards, lands)}

    gsmall = with_own(gsmall, small).reshape(N_SHARD, -1)
    wfull = {}
    off = 0
    for (n, ax), sz in zip(_SMALL_SHARDED, small_sizes):
        wfull[n] = _from_slots(gsmall[:, off:off + sz], shard[n].shape, ax)
        off += sz
    for n in _SMALL_REPLICATED:
        wfull[n] = shard[n]

    conv8 = jnp.concatenate([wfull["ffn_conv_w"], wfull["ffn_conv_b"][:, None, :],
                             jnp.zeros((2, 4, FFN_DIM), F32)], axis=1)
    w = {
        "started": token, "ret_norm": wfull["ret_norm"], "ret_w_in": by_cols(with_own(g_ret_in, ret_in_shard)),
        "ret_gn": wfull["ret_gn"].reshape(1, RET_HEADS * RET_V), "mla_norm": wfull["mla_norm"],
        "mla_q_norm": wfull["mla_q_norm"], "mla_kv_norm": wfull["mla_kv_norm"],
        "mla_q_head_norm": jnp.pad(wfull["mla_q_head_norm"], ((0, 0), (0, MLA_PAD - MLA_QK))),
        "mla_k_head_norm": jnp.pad(wfull["mla_k_head_norm"], ((0, 0), (0, MLA_PAD - MLA_QK))),
        "ffn_norm": wfull["ffn_norm"], "ffn_conv8": conv8,
    }

    started = {}

    def exchange(group, arrays):
        lands = [lax.empty((N_PEERS, p.shape[1] // 2, p.shape[2]), p.dtype) for p in arrays]
        send_sems, recv_sems, ps, lands, token = _exchange_start(
            _grad_copies, arrays, lands, N_PEERS * len(arrays), f"grads_start_{group}")
        started[group] = (send_sems, recv_sems, ps, lands)
        return token

    small_shapes = {
        "ret_norm": (1, D_MODEL), "ret_gn": (1, RET_HEADS, RET_V), "mla_norm": (1, D_MODEL),
        "mla_q_norm": (1, MLA_Q_RANK), "mla_kv_norm": (1, MLA_KV_RANK), "mla_q_head_norm": (1, MLA_QK),
        "mla_k_head_norm": (1, MLA_QK), "ffn_norm": (2, D_MODEL), "ffn_conv_w": (2, 3, FFN_DIM),
        "ffn_conv_b": (2, FFN_DIM)}

    def reduce_small(gl):
        gl = dict(gl, mla_q_head_norm=gl["mla_q_head_norm"][:, :MLA_QK], mla_k_head_norm=gl["mla_k_head_norm"][:, :MLA_QK])
        packed = jnp.concatenate([gl[n].reshape(1, -1) for n in _SMALL_ALL], axis=1)
        return _all_reduce_small(_pad_rows(packed, LANES, 8)[0])

    loss_part, dx, gsm = _local_step(x.reshape(T, D), loss_target.reshape(T, D), w, B, S, late, exchange,
                                     reduce_small)
    loss = lax.psum(loss_part, ("x", "y", "c"))

    delta, new_m, new_v, grads = {}, {}, {}, {}

    def reduced(group, after):
        send_sems, recv_sems, ps, lands = started[group]
        ps, lands = _exchange_wait(_grad_copies, send_sems, recv_sems, ps, lands, after, f"grads_wait_{group}")
        halves = [_sum_partials(p_, l_, f"grads_sum_{group}_{i}") for i, (p_, l_) in enumerate(zip(ps, lands))]
        return [two_d(r) for r in _sibling_share(halves, f"grads_share_{group}")]

    def adamw(n, g_):
        shp = shard[n].shape
        grads[n] = g_.reshape(shp)
        flat = lambda a: a.reshape(-1, shp[-1])
        d_, m_, v_ = _adamw(flat(shard[n]), flat(grads[n]), flat(mom_m[n]), flat(mom_v[n]), f"adamw_{n}")
        delta[n], new_m[n], new_v[n] = d_.reshape(shp), m_.reshape(shp), v_.reshape(shp)
        return d_

    ffn1 = reduced("ffn1", started["ret"][2][0])
    mla = reduced("mla", ffn1[0])
    ffn0 = reduced("ffn0", mla[0])
    reto = reduced("reto", ffn0[0])
    early = [adamw(n, g_) for n, g_ in zip(["mla_w_in", "mla_w_qb", "mla_w_kvb", "mla_w_out"], mla)]
    early.append(adamw("ffn_w_in", jnp.stack([ffn0[0], ffn1[0]])))
    early.append(adamw("ffn_w_out", jnp.stack([ffn0[1], ffn1[1]])))
    early.append(adamw("ret_w_out", reto[0]))
    ret = reduced("ret", jnp.stack([d_[0, 0] for d_ in early]))
    adamw("ret_w_in", ret[0])

    gsm = gsm.reshape(-1)
    sharded_axis = dict(_SMALL_SHARDED)
    off = 0
    for n in _SMALL_ALL:
        sz = int(np.prod(small_shapes[n]))
        gn = gsm[off:off + sz].reshape(small_shapes[n])
        off += sz
        if n in sharded_axis:
            ax = sharded_axis[n]
            width = shard[n].shape[ax]
            gn = lax.dynamic_slice_in_dim(gn, me * width, width, axis=ax)
        grads[n] = gn

    pack_small = lambda d: _pad_rows(jnp.concatenate([d[n].reshape(1, -1) for n in _SMALL_ALL], axis=1), LANES, 8)[0]
    d_, m_, v_ = _adamw(pack_small(shard), pack_small(grads), pack_small(mom_m), pack_small(mom_v), "adamw_small")
    off = 0
    for n in _SMALL_ALL:
        sz = int(np.prod(shard[n].shape))
        for dst, src in ((delta, d_), (new_m, m_), (new_v, v_)):
            dst[n] = src.reshape(-1)[off:off + sz].reshape(shard[n].shape)
        off += sz

    return (loss, dx.reshape(B, S, D), *[grads[n] for n in names], *[delta[n] for n in names],
            *[new_m[n] for n in names], *[new_v[n] for n in names])
```

```python
import functools

import numpy as np
import jax
import jax.numpy as jnp
from jax import lax
from jax.experimental import pallas as pl
from jax.experimental.pallas import tpu as pltpu

F32 = jnp.float32
BF16 = jnp.bfloat16
MXU_DTYPE = jnp.bfloat16

CHUNK = 64
RMS_EPS = 1e-6
ROPE_THETA = 10000.0
D_MODEL = 1024
RET_HEADS = 4
RET_QK = 256
RET_V = 512
RET_GAMMA_BASE = -5.0
MLA_HEADS = 8
MLA_Q_RANK = 384
MLA_KV_RANK = 256
MLA_NOPE = 128
MLA_ROPE = 64
MLA_V = 128
MLA_QK = MLA_NOPE + MLA_ROPE
MLA_PAD = 256
MLA_IN = MLA_Q_RANK + MLA_KV_RANK + MLA_ROPE
MLA_IN_PAD = MLA_IN + 64
MASK_VALUE = -1e30
FFN_DIM = 2816
ADAM_LR = 0.001
ADAM_B1 = 0.9
ADAM_B2 = 0.999
ADAM_EPS = 1e-08
ADAM_WD = 0.01
ADAM_STEP = 10

LANES = 128
MLA_FWD_BLOCK = 512
VMEM_LIMIT = 56 * 2 ** 20
N_SHARD = 4
N_DEV = 8

MESH = pl.DeviceIdType.MESH


def _params(sem=None, **kw):
    return pltpu.CompilerParams(dimension_semantics=sem, vmem_limit_bytes=VMEM_LIMIT, **kw)


def _pick(dim, target):
    if dim <= target:
        return dim
    best = None
    for d in range(LANES, target + 1, LANES):
        if dim % d == 0:
            best = d
    assert best is not None, (dim, target)
    return best


def _mm(a, b, dims, out_dtype, name, residual=None, bm=512, bn=1024, bk=2048, out_slots=None, after=None,
        cols_outer=False):
    a_parts = list(a) if isinstance(a, (list, tuple)) else [a]
    b_parts = list(b) if isinstance(b, (list, tuple)) else [b]
    parts_on_n = dims == "tn" or len(b_parts) > 1
    if parts_on_n:
        assert len(a_parts) == 1 and dims in ("tn", "nn")
        (K, M) = a_parts[0].shape if dims == "tn" else a_parts[0].shape[::-1]
        N = sum(p.shape[1] for p in b_parts)
        part_widths = [p.shape[1] for p in b_parts]
    else:
        assert len(b_parts) == 1
        M = a_parts[0].shape[0]
        K = sum(p.shape[1] for p in a_parts)
        N = b_parts[0].shape[1 if dims == "nn" else 0]
        part_widths = [p.shape[1] for p in a_parts]
    bm, bn, bk = _pick(M, bm), _pick(N, bn), _pick(K, min(bk, 1024) if dims == "tn" else bk)
    nk = K // bk
    unit = bn if parts_on_n else bk
    assert all(wd % unit == 0 for wd in part_widths), (name, part_widths, unit)
    bounds = np.cumsum([0] + [wd // unit for wd in part_widths])
    ranges = [(int(lo), int(hi)) for lo, hi in zip(bounds[:-1], bounds[1:])]

    def part_index(idx, lo, hi):
        return jnp.clip(idx - lo, 0, hi - lo - 1)

    if parts_on_n:
        if dims == "tn":
            a_specs = [pl.BlockSpec((bk, bm), lambda i, j, k: (k, i))]
            dn = (((0,), (0,)), ((), ()))
        else:
            a_specs = [pl.BlockSpec((bm, bk), lambda i, j, k: (i, k))]
            dn = (((1,), (0,)), ((), ()))
        b_specs = [pl.BlockSpec((bk, bn), functools.partial(lambda i, j, k, lo, hi: (k, part_index(j, lo, hi)), lo=lo, hi=hi))
                   for lo, hi in ranges]
    else:
        a_specs = [pl.BlockSpec((bm, bk), functools.partial(lambda i, j, k, lo, hi: (i, part_index(k, lo, hi)), lo=lo, hi=hi))
                   for lo, hi in ranges]
        if dims == "nt":
            b_specs = [pl.BlockSpec((bn, bk), lambda i, j, k: (j, k))]
        else:
            b_specs = [pl.BlockSpec((bk, bn), lambda i, j, k: (k, j))]
        dn = (((1,), (1 if dims == "nt" else 0,)), ((), ()))
    r_spec = pl.BlockSpec((bm, bn), lambda i, j, k: (i, j))
    if out_slots is None:
        o_spec, o_shape = r_spec, (M, N)
    else:
        ns = N // out_slots
        assert ns % bn == 0, (name, ns, bn)
        nbs = ns // bn
        o_spec = pl.BlockSpec((None, bm, bn), lambda i, j, k: (j // nbs, i, j % nbs))
        o_shape = (out_slots, M, ns)
    has_res = residual is not None
    na, nb = len(a_parts), len(b_parts)

    def body(*refs):
        a_refs, b_refs = refs[:na], refs[na:na + nb]
        r_ref = refs[na + nb] if has_res else None
        n_in = na + nb + has_res + (after is not None)
        o_ref = refs[n_in]
        acc_ref = refs[n_in + 1] if nk > 1 else None
        k = pl.program_id(2)

        def finish(acc):
            if has_res:
                acc = acc + r_ref[...].astype(F32)
            o_ref[...] = acc.astype(out_dtype)

        def compute(a_ref, b_ref):
            p = lax.dot_general(a_ref[...].astype(MXU_DTYPE), b_ref[...].astype(MXU_DTYPE), dn,
                                preferred_element_type=F32)
            if nk == 1:
                finish(p)
                return

            @pl.when(k == 0)
            def _():
                acc_ref[...] = p

            @pl.when(jnp.logical_and(k > 0, k < nk - 1))
            def _():
                acc_ref[...] += p

            @pl.when(k == nk - 1)
            def _():
                finish(acc_ref[...] + p)

        if len(ranges) == 1:
            compute(a_refs[0], b_refs[0])
        else:
            idx = pl.program_id(0 if cols_outer else 1) if parts_on_n else k
            for p, (lo, hi) in enumerate(ranges):
                @pl.when(jnp.logical_and(idx >= lo, idx < hi))
                def _(p=p):
                    compute(a_refs[0 if parts_on_n else p], b_refs[p if parts_on_n else 0])

    after_specs = [] if after is None else [pl.BlockSpec(after.shape, lambda i, j, k: (0, 0))]
    in_specs = a_specs + b_specs + ([r_spec] if has_res else []) + after_specs
    grid = (M // bm, N // bn, nk)
    if cols_outer:
        swap = lambda sp: pl.BlockSpec(sp.block_shape, functools.partial(lambda j, i, k, f: f(i, j, k), f=sp.index_map))
        in_specs, o_spec, grid = [swap(sp) for sp in in_specs], swap(o_spec), (grid[1], grid[0], nk)
    return pl.pallas_call(
        body, name=name, grid=grid,
        in_specs=in_specs, out_specs=o_spec,
        out_shape=jax.ShapeDtypeStruct(o_shape, out_dtype),
        scratch_shapes=[pltpu.VMEM((bm, bn), F32)] if nk > 1 else [],
        compiler_params=_params(("parallel", "parallel", "arbitrary")),
    )(*a_parts, *b_parts, *((residual,) if has_res else ()), *(() if after is None else (after,)))


def _mm_out_norm(a, w, residual, gain, name, bm=512):
    (M, K), N = a.shape, w.shape[1]
    bm = _pick(M, bm)

    def body(a_ref, w_ref, r_ref, g_ref, o_ref, h_ref, ht_ref):
        acc = lax.dot_general(a_ref[...].astype(MXU_DTYPE), w_ref[...].astype(MXU_DTYPE), _NN,
                              preferred_element_type=F32) + r_ref[...]
        o_ref[...] = acc
        hv = _fn_rms([[acc]], [], [[g_ref[...]]])[0][0]
        h_ref[...] = hv.astype(h_ref.dtype)
        ht_ref[...] = hv.T.astype(ht_ref.dtype)

    row = pl.BlockSpec((bm, N), lambda i: (i, 0))
    whole = lambda arr: pl.BlockSpec(arr.shape, lambda i: (0, 0))
    return pl.pallas_call(
        body, name=name, grid=(M // bm,),
        in_specs=[pl.BlockSpec((bm, K), lambda i: (i, 0)), whole(w), row, whole(gain)],
        out_specs=[row, row, pl.BlockSpec((N, bm), lambda i: (0, i))],
        out_shape=[jax.ShapeDtypeStruct((M, N), F32), jax.ShapeDtypeStruct((M, N), BF16),
                   jax.ShapeDtypeStruct((N, M), BF16)],
        compiler_params=_params(("parallel",)),
    )(a, w, residual, gain)


def _mm_out_loss(a, w, residual, target, name, bm=512):
    (M, K), N = a.shape, w.shape[1]
    bm = _pick(M, bm)

    def body(a_ref, w_ref, r_ref, t_ref, dy_ref, dyc_ref, l_ref):
        y = lax.dot_general(a_ref[...].astype(MXU_DTYPE), w_ref[...].astype(MXU_DTYPE), _NN,
                            preferred_element_type=F32) + r_ref[...]
        err = y - t_ref[...]
        dy_ref[...] = err / N
        dyc_ref[...] = (err / N).astype(dyc_ref.dtype)
        part = jnp.full((8, LANES), 0.5 * jnp.sum(jnp.mean(err * err, axis=-1)), F32)

        @pl.when(pl.program_id(0) == 0)
        def _():
            l_ref[...] = part

        @pl.when(pl.program_id(0) > 0)
        def _():
            l_ref[...] += part

    row = pl.BlockSpec((bm, N), lambda i: (i, 0))
    dy, dyc, l = pl.pallas_call(
        body, name=name, grid=(M // bm,),
        in_specs=[pl.BlockSpec((bm, K), lambda i: (i, 0)), pl.BlockSpec(w.shape, lambda i: (0, 0)), row, row],
        out_specs=[row, row, pl.BlockSpec((8, LANES), lambda i: (0, 0))],
        out_shape=[jax.ShapeDtypeStruct((M, N), F32), jax.ShapeDtypeStruct((M, N), BF16),
                   jax.ShapeDtypeStruct((8, LANES), F32)],
        compiler_params=_params(("arbitrary",)),
    )(a, w, residual, target)
    return dy, dyc, l[0, 0]


def _mm_dx_norm(a_parts, w, x, gain, add, name, bm=256, after=None):
    M = a_parts[0].shape[0]
    N, K = w.shape
    widths = [p.shape[1] for p in a_parts]
    assert sum(widths) == K, (name, widths, K)
    offs = [int(o) for o in np.cumsum([0] + widths[:-1])]
    bm = _pick(M, bm)
    na = len(a_parts)
    n_in = na + 4 + (after is not None)

    def body(*refs):
        w_ref, x_ref, g_ref, add_ref = refs[na:na + 4]
        dx_ref, dxc_ref, dg_ref = refs[n_in:n_in + 3]
        dh = None
        for a_ref, off, wd in zip(refs[:na], offs, widths):
            p = lax.dot_general(a_ref[...].astype(MXU_DTYPE), w_ref[:, off:off + wd].astype(MXU_DTYPE), _NT,
                                preferred_element_type=F32)
            dh = p if dh is None else dh + p
        _, vjp = jax.vjp(lambda xv, gv: _fn_rms([[xv]], [], [[gv]])[0][0], x_ref[...], g_ref[...])
        dxv, dgv = vjp(dh)
        dxv = dxv + add_ref[...]
        dx_ref[...] = dxv
        dxc_ref[...] = dxv.astype(dxc_ref.dtype)

        @pl.when(pl.program_id(0) == 0)
        def _():
            dg_ref[...] = dgv

        @pl.when(pl.program_id(0) > 0)
        def _():
            dg_ref[...] += dgv

    row = pl.BlockSpec((bm, N), lambda i: (i, 0))
    whole = lambda a: pl.BlockSpec(a.shape, lambda i: (0, 0))
    in_specs = [pl.BlockSpec((bm, wd), lambda i: (i, 0)) for wd in widths] + [whole(w), row, whole(gain), row]
    in_specs += [] if after is None else [whole(after)]
    return pl.pallas_call(
        body, name=name, grid=(M // bm,),
        in_specs=in_specs, out_specs=[row, row, whole(gain)],
        out_shape=[jax.ShapeDtypeStruct((M, N), F32), jax.ShapeDtypeStruct((M, N), BF16),
                   jax.ShapeDtypeStruct(gain.shape, F32)],
        compiler_params=_params(("arbitrary",)),
    )(*a_parts, w, x, gain, add, *(() if after is None else (after,)))


def _tiles(ref, width, tile):
    return [ref[:, t * tile:(t + 1) * tile].astype(F32) for t in range(width // tile)]


def _row_specs(rows, pos, consts, bm, S):
    npos_blocks = S // bm
    specs = [pl.BlockSpec((bm, w), functools.partial(lambda i, c: (i, c), c=cb)) for (_, w, cb, _) in rows]
    specs += [pl.BlockSpec((bm, p.shape[1]), lambda i: (i % npos_blocks, 0)) for p in pos]
    specs += [pl.BlockSpec(c.shape, lambda i: (0, 0)) for (c, _) in consts]
    return specs


def _rowwise_fwd(fn, name, rows, pos, consts, outs, bm, S, transposed=()):
    T = rows[0][0].shape[0]
    nr, npos, nc, no = len(rows), len(pos), len(consts), len(outs)

    def body(*refs):
        row_v = [_tiles(r, w, t) for r, (_, w, _, t) in zip(refs[:nr], rows)]
        pos_v = [r[...] for r in refs[nr:nr + npos]]
        const_v = [_tiles(r, c.shape[1], t) for r, (c, t) in zip(refs[nr + npos:nr + npos + nc], consts)]
        res = fn(row_v, pos_v, const_v)
        out_refs = refs[nr + npos + nc:]
        for o_ref, tiles, (w, t, dt) in zip(out_refs, res, outs):
            for k, v in enumerate(tiles):
                o_ref[:, k * t:(k + 1) * t] = v.astype(dt)
        for t_ref, a in zip(out_refs[no:], transposed):
            t = outs[a][1]
            for k, v in enumerate(res[a]):
                t_ref[k * t:(k + 1) * t, :] = v.T.astype(t_ref.dtype)

    return pl.pallas_call(
        body, name=name, grid=(T // bm,),
        in_specs=_row_specs(rows, pos, consts, bm, S),
        out_specs=[pl.BlockSpec((bm, w), lambda i: (i, 0)) for (w, _, _) in outs]
        + [pl.BlockSpec((outs[a][0], bm), lambda i: (0, i)) for a in transposed],
        out_shape=[jax.ShapeDtypeStruct((T, w), dt) for (w, _, dt) in outs]
        + [jax.ShapeDtypeStruct((outs[a][0], T), BF16) for a in transposed],
        compiler_params=_params(("parallel",)),
    )(*[r[0] for r in rows], *pos, *[c[0] for c in consts])


def _rowwise_bwd(fn, name, rows, pos, consts, cts, bm, S, adds=None, grad_dtypes=None, mxu_copies=(), linear=False):
    adds = adds or {}
    T = rows[0][0].shape[0]
    nr, npos, nc, nct = len(rows), len(pos), len(consts), len(cts)
    add_idx = sorted(adds)
    grad_dtypes = grad_dtypes or [F32] * nr

    def body(*refs):
        it = iter(refs)
        row_refs = [None if linear else next(it) for _ in range(nr)]
        pos_refs = [next(it) for _ in range(npos)]
        const_refs = [next(it) for _ in range(nc)]
        ct_refs = [next(it) for _ in range(nct)]
        add_refs = {k: next(it) for k in add_idx}
        drow_refs = [next(it) for _ in range(nr)]
        copy_refs = {a: next(it) for a in mxu_copies}
        dconst_refs = [next(it) for _ in range(nc)]
        if linear:
            row_v = [[jnp.zeros((bm, t), F32)] * (w // t) for (_, w, _, t) in rows]
        else:
            row_v = [_tiles(r, w, t) for r, (_, w, _, t) in zip(row_refs, rows)]
        pos_v = [r[...] for r in pos_refs]
        const_v = [_tiles(r, c.shape[1], t) for r, (c, t) in zip(const_refs, consts)]
        ct_v = [_tiles(r, c.shape[1], t) for r, (c, t) in zip(ct_refs, cts)]
        _, vjp = jax.vjp(lambda rv, cv: fn(rv, pos_v, cv), row_v, const_v)
        drows, dconsts = vjp(ct_v)
        for a, (d_ref, tiles, (_, w, _, t)) in enumerate(zip(drow_refs, drows, rows)):
            for k, v in enumerate(tiles):
                if a in add_refs:
                    v = v + add_refs[a][:, k * t:(k + 1) * t].astype(F32)
                d_ref[:, k * t:(k + 1) * t] = v.astype(d_ref.dtype)
                if a in copy_refs:
                    copy_refs[a][:, k * t:(k + 1) * t] = v.astype(BF16)
        first = pl.program_id(0) == 0
        for d_ref, tiles, (_, t) in zip(dconst_refs, dconsts, consts):
            for k, v in enumerate(tiles):
                @pl.when(first)
                def _(d_ref=d_ref, k=k, t=t, v=v):
                    d_ref[:, k * t:(k + 1) * t] = v

                @pl.when(jnp.logical_not(first))
                def _(d_ref=d_ref, k=k, t=t, v=v):
                    d_ref[:, k * t:(k + 1) * t] += v

    in_specs = _row_specs([] if linear else rows, pos, consts, bm, S)
    in_specs += [pl.BlockSpec((bm, c.shape[1]), lambda i: (i, 0)) for (c, _) in cts]
    in_specs += [pl.BlockSpec((bm, adds[k].shape[1]), lambda i: (i, 0)) for k in add_idx]
    out_specs = [pl.BlockSpec((bm, w), lambda i: (i, 0)) for (_, w, _, _) in rows]
    out_specs += [pl.BlockSpec((bm, rows[a][1]), lambda i: (i, 0)) for a in mxu_copies]
    out_specs += [pl.BlockSpec(c.shape, lambda i: (0, 0)) for (c, _) in consts]
    out_shape = [jax.ShapeDtypeStruct((T, w), dt) for (_, w, _, _), dt in zip(rows, grad_dtypes)]
    out_shape += [jax.ShapeDtypeStruct((T, rows[a][1]), BF16) for a in mxu_copies]
    out_shape += [jax.ShapeDtypeStruct(c.shape, F32) for (c, _) in consts]
    res = pl.pallas_call(
        body, name=name, grid=(T // bm,),
        in_specs=in_specs, out_specs=out_specs, out_shape=out_shape,
        compiler_params=_params(("arbitrary",)),
    )(*([] if linear else [r[0] for r in rows]), *pos, *[c[0] for c in consts], *[c[0] for c in cts],
      *[adds[k] for k in add_idx])
    n_rows = nr + len(mxu_copies)
    return res[:n_rows], res[n_rows:]


def _ssq(tiles):
    s = jnp.sum(tiles[0] * tiles[0], axis=-1, keepdims=True)
    for t in tiles[1:]:
        s = s + jnp.sum(t * t, axis=-1, keepdims=True)
    return s


def _sigmoid(x):
    return 0.5 * jnp.tanh(0.5 * x) + 0.5


def _fn_rms(rows, pos, consts):
    (x,), (g,) = rows[0], consts[0]
    r = lax.rsqrt(jnp.mean(x * x, axis=-1, keepdims=True) + RMS_EPS)
    return [[x * r * g]]


def _fn_ret_rope(rows, pos, consts):
    (qkv,) = rows
    nq = RET_HEADS * RET_QK // LANES
    q, k, v = qkv[:nq], qkv[nq:2 * nq], qkv[2 * nq:]
    cos, sin = pos

    def rot(t, scale):
        out = []
        for h in range(RET_HEADS):
            x1, x2 = t[2 * h], t[2 * h + 1]
            o1, o2 = x1 * cos - x2 * sin, x2 * cos + x1 * sin
            out += [o1, o2] if scale is None else [o1 * scale, o2 * scale]
        return out

    return [rot(q, None), rot(k, RET_QK ** -0.5), list(v)]


def _fn_ret_gate(rows, pos, consts):
    o, g = rows
    (gn,) = consts
    out = []
    for h in range(RET_HEADS):
        r = lax.rsqrt(jnp.mean(o[h] * o[h], axis=-1, keepdims=True) + RMS_EPS)
        out.append((o[h] * r * gn[h]) * (g[h] * _sigmoid(g[h])))
    return [out]


def _fn_mla_lat(rows, pos, consts):
    (p,) = rows
    gq, gkv = consts
    nq, nkv = MLA_Q_RANK // LANES, MLA_KV_RANK // LANES
    cq, ckv, kr = p[:nq], p[nq:nq + nkv], p[nq + nkv]
    rq = lax.rsqrt(_ssq(cq) / MLA_Q_RANK + RMS_EPS)
    rkv = lax.rsqrt(_ssq(ckv) / MLA_KV_RANK + RMS_EPS)
    return [[t * rq * g for t, g in zip(cq, gq)], [t * rkv * g for t, g in zip(ckv, gkv)], [kr]]


def _swap32_impl(x):
    lane = lax.broadcasted_iota(jnp.int32, x.shape, 1)
    up, down = pltpu.roll(x, LANES - 32, 1), pltpu.roll(x, 32, 1)
    return jnp.where(lane < 32, up, jnp.where(lane < 64, down, 0.0))


@jax.custom_vjp
def _swap32(x):
    return _swap32_impl(x)


_swap32.defvjp(lambda x: (_swap32_impl(x), None), lambda _, g: (_swap32_impl(g),))


def _fn_mla_heads(rows, pos, consts):
    qf, kvf, (kr,) = rows
    cos, sin = pos
    gq, gk = consts
    q_out, k_out, v_out = [], [], []
    for h in range(MLA_HEADS):
        q0, q1 = qf[2 * h], qf[2 * h + 1]
        r = lax.rsqrt(_ssq([q0, q1]) / MLA_QK + RMS_EPS)
        a0, a1 = q0 * r * gq[0], q1 * r * gq[1]
        a1 = a1 * cos + _swap32(a1) * sin
        q_out += [a0 * (MLA_QK ** -0.5), a1 * (MLA_QK ** -0.5)]
        k0 = kvf[2 * h]
        r = lax.rsqrt(_ssq([k0, kr]) / MLA_QK + RMS_EPS)
        b0, b1 = k0 * r * gk[0], kr * r * gk[1]
        k_out += [b0, b1 * cos + _swap32(b1) * sin]
        v_out.append(kvf[2 * h + 1])
    return [q_out, k_out, v_out]


def _shift_down(x, n):
    row = lax.broadcasted_iota(jnp.int32, x.shape, 0)
    return jnp.where(row >= n, pltpu.roll(x, n, 0), 0.0)


def _shift_up(x, n):
    rows = x.shape[0]
    row = lax.broadcasted_iota(jnp.int32, x.shape, 0)
    return jnp.where(row < rows - n, pltpu.roll(x, rows - n, 0), 0.0)


def _conv_blocks(S):
    cb = 256
    return cb, FFN_DIM // cb


def _conv_fwd(ag, w8, B, S, name):
    cb, ncb = _conv_blocks(S)

    def body(a_ref, g_ref, w_ref, u_ref, ut_ref):
        g = g_ref[...].astype(F32)
        w = w_ref[...]
        gc = w[0:1] * _shift_down(g, 2) + w[1:2] * _shift_down(g, 1) + w[2:3] * g + w[3:4]
        u = a_ref[...].astype(F32) * (gc * _sigmoid(gc))
        u_ref[...] = u.astype(u_ref.dtype)
        ut_ref[...] = u.T.astype(ut_ref.dtype)

    return pl.pallas_call(
        body, name=name, grid=(ncb, B),
        in_specs=[pl.BlockSpec((S, cb), lambda j, b: (b, j)),
                  pl.BlockSpec((S, cb), lambda j, b: (b, ncb + j)),
                  pl.BlockSpec((8, cb), lambda j, b: (0, j))],
        out_specs=[pl.BlockSpec((S, cb), lambda j, b: (b, j)), pl.BlockSpec((cb, S), lambda j, b: (j, b))],
        out_shape=[jax.ShapeDtypeStruct((B * S, FFN_DIM), BF16), jax.ShapeDtypeStruct((FFN_DIM, B * S), BF16)],
        compiler_params=_params(("parallel", "parallel")),
    )(ag, ag, w8)


def _conv_bwd(ag, w8, du, B, S, name):
    cb, ncb = _conv_blocks(S)

    def body(a_ref, g_ref, w_ref, du_ref, da_ref, dg_ref, dw_ref):
        g = g_ref[...].astype(F32)
        w = w_ref[...]
        g1, g2 = _shift_down(g, 1), _shift_down(g, 2)
        gc = w[0:1] * g2 + w[1:2] * g1 + w[2:3] * g + w[3:4]
        sg = _sigmoid(gc)
        du_v = du_ref[...]
        da_ref[...] = (du_v * (gc * sg)).astype(da_ref.dtype)
        dgc = du_v * a_ref[...].astype(F32) * (sg * (1.0 + gc * (1.0 - sg)))
        dg = w[2:3] * dgc + w[1:2] * _shift_up(dgc, 1) + w[0:1] * _shift_up(dgc, 2)
        dg_ref[...] = dg.astype(dg_ref.dtype)
        part = jnp.concatenate([
            jnp.sum(dgc * g2, axis=0, keepdims=True), jnp.sum(dgc * g1, axis=0, keepdims=True),
            jnp.sum(dgc * g, axis=0, keepdims=True), jnp.sum(dgc, axis=0, keepdims=True),
            jnp.zeros((4, cb), F32)], axis=0)

        @pl.when(pl.program_id(1) == 0)
        def _():
            dw_ref[...] = part

        @pl.when(pl.program_id(1) > 0)
        def _():
            dw_ref[...] += part

    blk = lambda j, b: (b, j)
    return pl.pallas_call(
        body, name=name, grid=(ncb, B),
        in_specs=[pl.BlockSpec((S, cb), blk),
                  pl.BlockSpec((S, cb), lambda j, b: (b, ncb + j)),
                  pl.BlockSpec((8, cb), lambda j, b: (0, j)),
                  pl.BlockSpec((S, cb), blk)],
        out_specs=[pl.BlockSpec((S, cb), blk), pl.BlockSpec((S, cb), blk),
                   pl.BlockSpec((8, cb), lambda j, b: (0, j))],
        out_shape=[jax.ShapeDtypeStruct((B * S, FFN_DIM), BF16), jax.ShapeDtypeStruct((B * S, FFN_DIM), BF16),
                   jax.ShapeDtypeStruct((8, FFN_DIM), F32)],
        compiler_params=_params(("parallel", "arbitrary")),
    )(ag, ag, w8, du)


_NT = (((1,), (1,)), ((), ()))
_NN = (((1,), (0,)), ((), ()))
_TN = (((0,), (0,)), ((), ()))


def _dot(a, b, dn):
    return lax.dot_general(a.astype(MXU_DTYPE), b.astype(MXU_DTYPE), dn, preferred_element_type=F32)


def _run_bits(n):
    bits, b = [], 1
    while b < n:
        bits.append(b)
        b *= 2
    return bits[::-1]


def _key_runs(n, nq, update):
    for bit in _run_bits(nq + 1):
        @pl.when((n & bit) != 0)
        def _(bit=bit):
            update(n & ~(2 * bit - 1), bit, (n & (bit - 1)) == 0)


def _earlier_runs(n, nq, update):
    for bit in _run_bits(nq):
        @pl.when((n & bit) != 0)
        def _(bit=bit):
            update(n & ~(2 * bit - 1), bit, False)


def _chunk_visible(shape, nblk, blk):
    key = lax.broadcasted_iota(jnp.int32, shape, 0) - (nblk - 1) * blk
    query = lax.broadcasted_iota(jnp.int32, shape, 1)
    return jnp.logical_or(key < 0, (key // CHUNK) <= (query // CHUNK))


def _mla_attn_fwd(q, k, v, B, S):
    blk = min(MLA_FWD_BLOCK, S)
    H, nq = MLA_HEADS, S // blk

    def body(q_ref, k_ref, v_ref, o_ref, lse_ref, m_ref, l_ref, acc_ref):
        def qblock(i, _):
            q_rows = pl.ds(pl.multiple_of(i * blk, blk), blk)
            qi = q_ref[q_rows, :]
            m_ref[...] = jnp.full(m_ref.shape, MASK_VALUE, F32)
            l_ref[...] = jnp.zeros(l_ref.shape, F32)
            acc_ref[...] = jnp.zeros(acc_ref.shape, F32)

            def keys(first, nblk, last):
                rows = pl.ds(pl.multiple_of(first * blk, blk), nblk * blk)
                s = _dot(k_ref[rows, :], qi, _NT)
                s = jnp.where(jnp.logical_or(_chunk_visible(s.shape, nblk, blk), jnp.logical_not(last)), s, MASK_VALUE)
                m = m_ref[...]
                m2 = jnp.maximum(m, jnp.max(s, axis=0, keepdims=True))
                alpha = jnp.exp(m - m2)
                p = jnp.exp(s - m2)
                l_ref[...] = alpha * l_ref[...] + jnp.sum(p, axis=0, keepdims=True)
                acc_ref[...] = alpha * acc_ref[...] + _dot(v_ref[rows, :], p, _TN)
                m_ref[...] = m2

            _key_runs(i + 1, nq, keys)
            l = l_ref[...]
            o_ref[q_rows, :] = (acc_ref[...] / l).T
            lse_ref[0, :, q_rows] = m_ref[...] + jnp.log(l)
            return 0

        lax.fori_loop(0, nq, qblock, 0)

    return pl.pallas_call(
        body, name="mla_attn_fwd", grid=(B, H),
        in_specs=[pl.BlockSpec((S, MLA_PAD), lambda b, h: (b, h)),
                  pl.BlockSpec((S, MLA_PAD), lambda b, h: (b, h)),
                  pl.BlockSpec((S, MLA_V), lambda b, h: (b, h))],
        out_specs=[pl.BlockSpec((S, MLA_V), lambda b, h: (b, h)),
                   pl.BlockSpec((1, 1, S), lambda b, h: (b * H + h, 0, 0))],
        out_shape=[jax.ShapeDtypeStruct((B * S, H * MLA_V), F32), jax.ShapeDtypeStruct((B * H, 1, S), F32)],
        scratch_shapes=[pltpu.VMEM((1, blk), F32), pltpu.VMEM((1, blk), F32), pltpu.VMEM((MLA_V, blk), F32)],
        compiler_params=_params(("parallel", "parallel")),
    )(q, k, v)


def _mla_attn_bwd(q, k, v, o, do, lse, B, S):
    blk = min(MLA_FWD_BLOCK, S)
    H, nq = MLA_HEADS, S // blk

    def body(q_ref, k_ref, v_ref, o_ref, do_ref, lse_ref, dq_ref, dk_ref, dv_ref, kt_ref, dqt_ref):
        dk_ref[...] = jnp.zeros(dk_ref.shape, F32)
        dv_ref[...] = jnp.zeros(dv_ref.shape, F32)
        for g in range(nq):
            kt_ref[g] = k_ref[g * blk:(g + 1) * blk, :].T

        def qblock(i, _):
            q_rows = pl.ds(pl.multiple_of(i * blk, blk), blk)
            qi = q_ref[q_rows, :]
            doi = do_ref[q_rows, :]
            delta = jnp.sum((doi * o_ref[q_rows, :]).T, axis=0, keepdims=True)
            lse_i = lse_ref[0, :, q_rows]
            doi = doi.astype(MXU_DTYPE)
            dqt_ref[...] = jnp.zeros(dqt_ref.shape, F32)

            def keys(first, nblk, last):
                rows = pl.ds(pl.multiple_of(first * blk, blk), nblk * blk)
                k_run, v_run = k_ref[rows, :], v_ref[rows, :]
                p = jnp.exp(_dot(k_run, qi, _NT) - lse_i)
                p = jnp.where(jnp.logical_or(_chunk_visible(p.shape, nblk, blk), jnp.logical_not(last)), p, 0.0)
                ds = (p * (_dot(v_run, doi, _NT) - delta)).astype(MXU_DTYPE)
                dk_ref[rows, :] += _dot(ds, qi, _NN)
                dv_ref[rows, :] += _dot(p, doi, _NN)
                for r in range(nblk):
                    dqt_ref[...] += _dot(kt_ref[first + r], ds[r * blk:(r + 1) * blk, :], _NN)

            _key_runs(i + 1, nq, keys)
            dq_ref[q_rows, :] = dqt_ref[...].T
            return 0

        lax.fori_loop(0, nq, qblock, 0)

    qk_spec = pl.BlockSpec((S, MLA_PAD), lambda b, h: (b, h))
    v_spec = pl.BlockSpec((S, MLA_V), lambda b, h: (b, h))
    return pl.pallas_call(
        body, name="mla_attn_bwd", grid=(B, H),
        in_specs=[qk_spec, qk_spec, v_spec, v_spec, v_spec,
                  pl.BlockSpec((1, 1, S), lambda b, h: (b * H + h, 0, 0))],
        out_specs=[qk_spec, qk_spec, v_spec],
        out_shape=[jax.ShapeDtypeStruct((B * S, H * MLA_PAD), F32), jax.ShapeDtypeStruct((B * S, H * MLA_PAD), F32),
                   jax.ShapeDtypeStruct((B * S, H * MLA_V), F32)],
        scratch_shapes=[pltpu.VMEM((nq, MLA_PAD, blk), q.dtype), pltpu.VMEM((MLA_PAD, blk), F32)],
        compiler_params=_params(("parallel", "parallel")),
    )(q, k, v, o, do, lse)


def _ret_log_gamma():
    lg = np.log1p(-np.exp2(RET_GAMMA_BASE - np.arange(RET_HEADS, dtype=np.float32))).astype(np.float32)
    return jnp.asarray(np.broadcast_to(lg[:, None, None], (RET_HEADS, 8, LANES)).copy())


RET_BLOCK = 512


def _ret_local_scale(lg, shape, blk, rising):
    local = lax.broadcasted_iota(jnp.int32, shape, 0) % blk
    return jnp.exp(lg * (local if rising else blk - 1 - local).astype(F32))


def _ret_pair_factor(lg, blk, steps):
    return jnp.exp(lg * (blk * (steps - 1) + 1).astype(F32))


def _ret_own_decay(lg, blk, transposed):
    a = lax.broadcasted_iota(jnp.int32, (blk, blk), 0)
    b = lax.broadcasted_iota(jnp.int32, (blk, blk), 1)
    query, key = (b, a) if transposed else (a, b)
    dec = jnp.exp(lg * jnp.abs(query - key).astype(F32))
    return jnp.where((key // CHUNK) <= (query // CHUNK), dec, 0.0)


def _ret_attn_fwd(q, k, v, B, S):
    blk = min(RET_BLOCK, S)
    H, nq = RET_HEADS, S // blk

    def body(lg_ref, q_ref, k_ref, v_ref, o_ref, ks_ref, dec_ref, acc_ref):
        lg = lg_ref[0, 0:1, 0:1]
        ks_ref[...] = (k_ref[...].astype(F32) * _ret_local_scale(lg, k_ref.shape, blk, False)).astype(ks_ref.dtype)
        dec_ref[...] = _ret_own_decay(lg, blk, False)

        def qblock(i, _):
            q_rows = pl.ds(pl.multiple_of(i * blk, blk), blk)
            qi = q_ref[q_rows, :]
            qs = (qi.astype(F32) * _ret_local_scale(lg, qi.shape, blk, True)).astype(qi.dtype)
            a = _dot(qi, k_ref[q_rows, :], _NT) * dec_ref[...]
            acc_ref[...] = _dot(a, v_ref[q_rows, :], _NN)

            def keys(first, nblk, _):
                rows = pl.ds(pl.multiple_of(first * blk, blk), nblk * blk)
                steps = i - first - lax.broadcasted_iota(jnp.int32, (1, nblk * blk), 1) // blk
                a = _dot(qs, ks_ref[rows, :], _NT) * _ret_pair_factor(lg, blk, steps)
                acc_ref[...] += _dot(a, v_ref[rows, :], _NN)

            _earlier_runs(i, nq, keys)
            o_ref[q_rows, :] = acc_ref[...]
            return 0

        lax.fori_loop(0, nq, qblock, 0)

    qk_spec = pl.BlockSpec((S, RET_QK), lambda b, h: (b, h))
    v_spec = pl.BlockSpec((S, RET_V), lambda b, h: (b, h))
    return pl.pallas_call(
        body, name="ret_attn_fwd", grid=(B, H),
        in_specs=[pl.BlockSpec((1, 8, LANES), lambda b, h: (h, 0, 0)), qk_spec, qk_spec, v_spec],
        out_specs=v_spec,
        out_shape=jax.ShapeDtypeStruct((B * S, H * RET_V), F32),
        scratch_shapes=[pltpu.VMEM((S, RET_QK), k.dtype), pltpu.VMEM((blk, blk), F32), pltpu.VMEM((blk, RET_V), F32)],
        compiler_params=_params(("parallel", "parallel")),
    )(_ret_log_gamma(), q, k, v)


def _ret_attn_bwd(q, k, v, do, B, S):
    blk = min(RET_BLOCK, S)
    H, nq = RET_HEADS, S // blk

    def body(lg_ref, q_ref, k_ref, v_ref, do_ref, dq_ref, dk_ref, dv_ref, ks_ref, kst_ref, dks_ref, dqt_ref, dec_ref):
        lg = lg_ref[0, 0:1, 0:1]
        dk_ref[...] = jnp.zeros(dk_ref.shape, F32)
        dv_ref[...] = jnp.zeros(dv_ref.shape, F32)
        dks_ref[...] = jnp.zeros(dks_ref.shape, F32)
        ks_ref[...] = (k_ref[...].astype(F32) * _ret_local_scale(lg, k_ref.shape, blk, False)).astype(ks_ref.dtype)
        for g in range(nq):
            kst_ref[g] = ks_ref[g * blk:(g + 1) * blk, :].T
        dec_ref[...] = _ret_own_decay(lg, blk, True)

        def qblock(i, _):
            q_rows = pl.ds(pl.multiple_of(i * blk, blk), blk)
            qi = q_ref[q_rows, :]
            q_scale = _ret_local_scale(lg, qi.shape, blk, True)
            qs = (qi.astype(F32) * q_scale).astype(qi.dtype)
            doi = do_ref[q_rows, :].astype(MXU_DTYPE)
            ki = k_ref[q_rows, :]
            dec = dec_ref[...]
            a = _dot(ki, qi, _NT) * dec
            da = (_dot(v_ref[q_rows, :], doi, _NT) * dec).astype(MXU_DTYPE)
            dv_ref[q_rows, :] += _dot(a, doi, _NN)
            dk_ref[q_rows, :] += _dot(da, qi, _NN)
            dq_own = _dot(da, ki, _TN)
            dqt_ref[...] = jnp.zeros(dqt_ref.shape, F32)

            def keys(first, nblk, _):
                for r in range(nblk):
                    g = first + r
                    rows = pl.ds(pl.multiple_of(g * blk, blk), blk)
                    c = _ret_pair_factor(lg, blk, i - g)
                    a = _dot(ks_ref[rows, :], qs, _NT) * c
                    da = (_dot(v_ref[rows, :], doi, _NT) * c).astype(MXU_DTYPE)
                    dv_ref[rows, :] += _dot(a, doi, _NN)
                    dks_ref[rows, :] += _dot(da, qs, _NN)
                    dqt_ref[...] += _dot(kst_ref[g], da, _NN)

            _earlier_runs(i, nq, keys)
            dq_ref[q_rows, :] = dqt_ref[...].T * q_scale + dq_own
            return 0

        lax.fori_loop(0, nq, qblock, 0)
        dk_ref[...] += dks_ref[...] * _ret_local_scale(lg, dks_ref.shape, blk, False)

    qk_spec = pl.BlockSpec((S, RET_QK), lambda b, h: (b, h))
    v_spec = pl.BlockSpec((S, RET_V), lambda b, h: (b, h))
    return pl.pallas_call(
        body, name="ret_attn_bwd", grid=(B, H),
        in_specs=[pl.BlockSpec((1, 8, LANES), lambda b, h: (h, 0, 0)), qk_spec, qk_spec, v_spec, v_spec],
        out_specs=[qk_spec, qk_spec, v_spec],
        out_shape=[jax.ShapeDtypeStruct((B * S, H * RET_QK), F32), jax.ShapeDtypeStruct((B * S, H * RET_QK), F32),
                   jax.ShapeDtypeStruct((B * S, H * RET_V), F32)],
        scratch_shapes=[pltpu.VMEM((S, RET_QK), k.dtype), pltpu.VMEM((nq, RET_QK, blk), k.dtype),
                        pltpu.VMEM((S, RET_QK), F32), pltpu.VMEM((RET_QK, blk), F32), pltpu.VMEM((blk, blk), F32)],
        compiler_params=_params(("parallel", "parallel")),
    )(_ret_log_gamma(), q, k, v, do)


def _adamw(w, g, m, v, name):
    R, C = w.shape
    br = R if R * C * 4 <= 2 ** 21 else _pick_rows(R, max(8, (2 ** 21) // (C * 4)))

    def body(w_ref, g_ref, m_ref, v_ref, d_ref, mo_ref, vo_ref):
        g_v = g_ref[...]
        m_v = ADAM_B1 * m_ref[...] + (1.0 - ADAM_B1) * g_v
        v_v = ADAM_B2 * v_ref[...] + (1.0 - ADAM_B2) * (g_v * g_v)
        m_hat = m_v / (1.0 - ADAM_B1 ** ADAM_STEP)
        v_hat = v_v / (1.0 - ADAM_B2 ** ADAM_STEP)
        d_ref[...] = -ADAM_LR * (m_hat / (jnp.sqrt(v_hat) + ADAM_EPS) + ADAM_WD * w_ref[...])
        mo_ref[...] = m_v
        vo_ref[...] = v_v

    blk = pl.BlockSpec((br, C), lambda i: (i, 0))
    return pl.pallas_call(
        body, name=name, grid=(R // br,),
        in_specs=[blk] * 4, out_specs=[blk] * 3,
        out_shape=[jax.ShapeDtypeStruct((R, C), F32)] * 3,
        compiler_params=_params(("parallel",)),
    )(w, g, m, v)


def _pick_rows(R, target):
    best = None
    for d in range(8, min(R, target) + 1, 8):
        if R % d == 0:
            best = d
    assert best is not None, (R, target)
    return best


def _position():
    return lax.axis_index("x"), lax.axis_index("y"), lax.axis_index("c")


HBM_SPEC = pl.BlockSpec(memory_space=pltpu.HBM)


def _other_chips(x, y):
    return [(1 - x, y), (x, 1 - y), (1 - x, 1 - y)]


def _all_gather_weights(bigs, small):
    nb = len(bigs)

    def body(*refs):
        big_refs, small_ref = refs[:nb], refs[nb]
        obig, osmall = refs[nb + 1:2 * nb + 1], refs[2 * nb + 1]
        ici_send, ici_recv, d2d_send, d2d_recv, sm_send, sm_recv = refs[2 * nb + 2:]
        x, y, c = _position()
        me = 2 * x + y
        chips = _other_chips(x, y)

        def rows(n, half):
            rh = bigs[n].shape[0] // 2
            return pl.ds(half * rh, rh)

        def over_ici(n, j, slot, from_shard):
            px, py = chips[j]
            dst = obig[n].at[slot, rows(n, c)]
            return pltpu.make_async_remote_copy(
                src_ref=big_refs[n].at[rows(n, c)] if from_shard else dst, dst_ref=dst,
                send_sem=ici_send.at[3 * n + j], recv_sem=ici_recv.at[3 * n + j],
                device_id=(px, py, c), device_id_type=MESH)

        def over_d2d(n, j, half):
            px, py = chips[j]
            part = obig[n].at[2 * px + py, rows(n, half)]
            return pltpu.make_async_remote_copy(
                src_ref=part, dst_ref=part, send_sem=d2d_send.at[3 * n + j], recv_sem=d2d_recv.at[3 * n + j],
                device_id=(x, y, 1 - c), device_id_type=MESH)

        def small_copy(j, slot):
            px, py = chips[j]
            return pltpu.make_async_remote_copy(
                src_ref=small_ref, dst_ref=osmall.at[slot], send_sem=sm_send.at[j], recv_sem=sm_recv.at[j],
                device_id=(px, py, c), device_id_type=MESH)

        sends = [over_ici(n, j, me, True) for n in range(nb) for j in range(3)]
        sends += [small_copy(j, me) for j in range(3)]
        for cp in sends:
            cp.start()
        passed = []
        for n in range(nb):
            for j, (px, py) in enumerate(chips):
                over_ici(n, j, 2 * px + py, False).wait_recv()
                fwd = over_d2d(n, j, c)
                fwd.start()
                passed.append(fwd)
        for n in range(nb):
            for j in range(3):
                over_d2d(n, j, 1 - c).wait_recv()
        for j, (px, py) in enumerate(chips):
            small_copy(j, 2 * px + py).wait_recv()
        for cp in sends + passed:
            cp.wait_send()

    dma = pltpu.SemaphoreType.DMA
    return pl.pallas_call(
        body, name="weights_all_gather",
        in_specs=[HBM_SPEC] * (nb + 1), out_specs=[HBM_SPEC] * (nb + 1),
        out_shape=[jax.ShapeDtypeStruct((N_SHARD,) + b.shape, b.dtype) for b in bigs]
        + [jax.ShapeDtypeStruct((N_SHARD,) + small.shape, small.dtype)],
        scratch_shapes=[dma((3 * nb,)), dma((3 * nb,)), dma((3 * nb,)), dma((3 * nb,)), dma((3,)), dma((3,))],
    )(*bigs, small)


SEM_SPEC = pl.BlockSpec(memory_space=pltpu.SEMAPHORE)
DATAFLOW_EFFECT = pltpu.SideEffectType.DATAFLOW_SIDE_EFFECTING
N_PEERS = N_DEV - 1


def _grad_copies(p_refs, land_refs, send_sems, recv_sems):
    x, y, c = _position()
    copies = []
    for a, (p_ref, land_ref) in enumerate(zip(p_refs, land_refs)):
        rh = p_ref.shape[1] // 2
        for k in range(1, N_DEV):
            px = 1 - x if k & 4 else x
            py = 1 - y if k & 2 else y
            pc = 1 - c if k & 1 else c
            copies.append(pltpu.make_async_remote_copy(
                src_ref=p_ref.at[2 * px + py, pl.ds(pc * rh, rh)], dst_ref=land_ref.at[k - 1],
                send_sem=send_sems.at[N_PEERS * a + k - 1], recv_sem=recv_sems.at[N_PEERS * a + k - 1],
                device_id=(px, py, pc), device_id_type=MESH))
    return copies


def _weight_copies(w_refs, land_refs, send_sems, recv_sems):
    x, y, c = _position()
    copies = []
    for a, (w_ref, land_ref) in enumerate(zip(w_refs, land_refs)):
        for j, (px, py) in enumerate(_other_chips(x, y)):
            copies.append(pltpu.make_async_remote_copy(
                src_ref=w_ref, dst_ref=land_ref.at[2 * x + y], send_sem=send_sems.at[3 * a + j],
                recv_sem=recv_sems.at[3 * a + j], device_id=(px, py, c), device_id_type=MESH))
    return copies


def _exchange_start(make_copies, srcs, lands, n_sems, name, after=None):
    n, m = len(srcs), len(lands)
    n_in = n + m + (after is not None)

    def body(*refs):
        send_sems, recv_sems, token = refs[n_in], refs[n_in + 1], refs[-1]
        for cp in make_copies(refs[:n], refs[n:n + m], send_sems, recv_sems):
            cp.start()
        token[...] = jnp.zeros(token.shape, token.dtype)

    hbm = lambda a: pltpu.with_memory_space_constraint(a, pltpu.HBM)
    dma = pltpu.SemaphoreType.DMA
    res = pl.pallas_call(
        body, name=name,
        in_specs=[HBM_SPEC] * (n + m) + ([] if after is None else [pl.BlockSpec(memory_space=pl.ANY)]),
        out_specs=[SEM_SPEC, SEM_SPEC] + [HBM_SPEC] * (n + m) + [pl.BlockSpec(memory_space=pltpu.VMEM)],
        out_shape=[dma((n_sems,)), dma((n_sems,))] + [pltpu.HBM(a.shape, a.dtype) for a in list(srcs) + list(lands)]
        + [jax.ShapeDtypeStruct((8, LANES), F32)],
        input_output_aliases={i: 2 + i for i in range(n + m)},
        compiler_params=pltpu.CompilerParams(has_side_effects=DATAFLOW_EFFECT),
    )(*[hbm(a) for a in srcs], *[hbm(a) for a in lands], *(() if after is None else (after,)))
    return res[0], res[1], list(res[2:2 + n]), list(res[2 + n:2 + n + m]), res[-1]


def _exchange_wait(make_copies, send_sems, recv_sems, srcs, lands, after, name):
    n, m = len(srcs), len(lands)

    def body(*refs):
        for cp in make_copies(refs[:n], refs[n:n + m], refs[n + m], refs[n + m + 1]):
            cp.wait_send()
            cp.wait_recv()

    res = pl.pallas_call(
        body, name=name,
        in_specs=[HBM_SPEC] * (n + m) + [SEM_SPEC, SEM_SPEC, pl.BlockSpec(memory_space=pl.ANY)],
        out_specs=[HBM_SPEC] * (n + m),
        out_shape=[pltpu.HBM(a.shape, a.dtype) for a in list(srcs) + list(lands)],
        input_output_aliases={i: i for i in range(n + m)},
        compiler_params=pltpu.CompilerParams(has_side_effects=DATAFLOW_EFFECT),
    )(*srcs, *lands, send_sems, recv_sems, after)
    return list(res[:n]), list(res[n:])


def _sum_partials(p, land, name):
    _, rh, cols = land.shape
    br = _pick_rows(rh, 256)
    nrb = rh // br
    x, y, c = _position()
    where = jnp.stack([2 * x + y, c]).astype(jnp.int32)

    def body(where_ref, p_ref, land_ref, o_ref):
        acc = p_ref[...].astype(F32)
        for k in range(N_PEERS):
            acc = acc + land_ref[k].astype(F32)
        o_ref[...] = acc

    return pl.pallas_call(
        body, name=name,
        grid_spec=pltpu.PrefetchScalarGridSpec(
            num_scalar_prefetch=1, grid=(nrb,),
            in_specs=[pl.BlockSpec((None, br, cols), lambda r, where_ref: (where_ref[0], where_ref[1] * nrb + r, 0)),
                      pl.BlockSpec((N_PEERS, br, cols), lambda r, where_ref: (0, r, 0))],
            out_specs=pl.BlockSpec((None, br, cols), lambda r, where_ref: (where_ref[1], r, 0))),
        out_shape=jax.ShapeDtypeStruct((2, rh, cols), F32),
        compiler_params=_params(("parallel",)),
    )(where, p, land)


def _sibling_share(fulls, name):
    n = len(fulls)

    def body(*refs):
        o_refs = refs[n:2 * n]
        send_sems, recv_sems = refs[2 * n:]
        x, y, c = _position()

        def copy(a, half):
            return pltpu.make_async_remote_copy(
                src_ref=o_refs[a].at[half], dst_ref=o_refs[a].at[half], send_sem=send_sems.at[a],
                recv_sem=recv_sems.at[a], device_id=(x, y, 1 - c), device_id_type=MESH)

        sends = [copy(a, c) for a in range(n)]
        for cp in sends:
            cp.start()
        for a in range(n):
            copy(a, 1 - c).wait_recv()
        for cp in sends:
            cp.wait_send()

    dma = pltpu.SemaphoreType.DMA
    return pl.pallas_call(
        body, name=name,
        in_specs=[HBM_SPEC] * n, out_specs=[HBM_SPEC] * n,
        out_shape=[jax.ShapeDtypeStruct(f.shape, f.dtype) for f in fulls],
        input_output_aliases={a: a for a in range(n)},
        scratch_shapes=[dma((n,)), dma((n,))],
    )(*fulls)


def _all_reduce_small(v):
    R, cols = v.shape

    def body(v_ref, o_ref, buf_ref, send_sems, recv_sems):
        x, y, c = _position()
        me = 4 * x + 2 * y + c
        buf_ref[me] = v_ref[...]
        sends = []
        for k in range(1, N_DEV):
            px = 1 - x if k & 4 else x
            py = 1 - y if k & 2 else y
            pc = 1 - c if k & 1 else c
            sends.append(pltpu.make_async_remote_copy(
                src_ref=v_ref, dst_ref=buf_ref.at[me], send_sem=send_sems.at[k - 1], recv_sem=recv_sems.at[k - 1],
                device_id=(px, py, pc), device_id_type=MESH))
        for cp in sends:
            cp.start()
        for k in range(1, N_DEV):
            px = 1 - x if k & 4 else x
            py = 1 - y if k & 2 else y
            pc = 1 - c if k & 1 else c
            pltpu.make_async_remote_copy(
                src_ref=v_ref, dst_ref=buf_ref.at[4 * px + 2 * py + pc], send_sem=send_sems.at[k - 1],
                recv_sem=recv_sems.at[k - 1], device_id=(px, py, pc), device_id_type=MESH).wait_recv()
        for cp in sends:
            cp.wait_send()
        acc = buf_ref[0]
        for d in range(1, N_DEV):
            acc = acc + buf_ref[d]
        o_ref[...] = acc

    return pl.pallas_call(
        body, name="small_grads_all_reduce",
        in_specs=[pl.BlockSpec(memory_space=pltpu.VMEM)], out_specs=pl.BlockSpec(memory_space=pltpu.VMEM),
        out_shape=jax.ShapeDtypeStruct((R, cols), F32),
        scratch_shapes=[pltpu.VMEM((N_DEV, R, cols), F32), pltpu.SemaphoreType.DMA((N_DEV - 1,)),
                        pltpu.SemaphoreType.DMA((N_DEV - 1,))],
    )(v)


def _rope_tables(S, half, width):
    inv_freq = ROPE_THETA ** (-jnp.arange(half, dtype=F32) / half)
    ang = jnp.arange(S).astype(F32)[:, None] * inv_freq[None, :]
    return jnp.cos(ang), jnp.sin(ang)


def _slot_rows(a):
    return a.reshape(N_SHARD, -1, a.shape[-1])


def _local_step(x, target, w, B, S, late, exchange, reduce_small):
    T = B * S
    D = D_MODEL
    bm = min(512, S)
    full = lambda a, wd, tile=None: (a, wd, 0, tile or wd)
    g = {}

    cos_r, sin_r = _rope_tables(S, RET_QK // 2, LANES)
    cos_m, sin_m = _rope_tables(S, MLA_ROPE // 2, LANES)
    zeros64 = jnp.zeros((S, 64), F32)
    cos_m = jnp.concatenate([cos_m, cos_m, zeros64], axis=1)
    sin_m = jnp.concatenate([-sin_m, sin_m, zeros64], axis=1)

    def ffn_fwd(xin, h, ht, i, next_gain):
        w.update(late(f"ffn{i}", xin))
        norm = w["ffn_norm"][i:i + 1]
        ag = _mm(h, w[f"ffn_w_in{i}"], "nn", BF16, f"ffn{i}_in", bm=1024, bn=1408, cols_outer=True)
        u, ut = _conv_fwd(ag, w["ffn_conv8"][i], B, S, f"ffn{i}_conv")
        if next_gain is None:
            out = _mm_out_loss(u, w[f"ffn_w_out{i}"], xin, target, f"ffn{i}_out")
        else:
            out = _mm_out_norm(u, w[f"ffn_w_out{i}"], xin, next_gain, f"ffn{i}_out")
        return out, (xin, norm, ht, ag, ut)

    def ffn_bwd(dxout, dxout_c, saved, i):
        xin, norm, ht, ag, ut = saved
        du = _mm(dxout_c, w[f"ffn_w_out{i}"], "nt", F32, f"ffn{i}_out_dx", bm=1024, bn=1408, cols_outer=True)
        g_w_out = _mm(ut, dxout_c, "nn", BF16, f"ffn{i}_out_dw", bm=1408, bn=512, bk=T)
        da, dg, dw8 = _conv_bwd(ag, w["ffn_conv8"][i], du, B, S, f"ffn{i}_conv_bwd")
        g_w_in = _mm(ht, [da, dg], "nn", BF16, f"ffn{i}_in_dw", bm=1024, bn=1408, bk=T // 2, out_slots=N_SHARD)
        token = exchange(f"ffn{i}", [g_w_in, _slot_rows(g_w_out)])
        dxin, dxin_c, g_norm = _mm_dx_norm([da, dg], w[f"ffn_w_in{i}"], xin, norm, dxout, f"ffn{i}_in_dx", after=token)
        return dxin, dxin_c, (g_norm, dw8)

    h0, h0t = _rowwise_fwd(_fn_rms, "ret_norm", [full(x, D)], [], [(w["ret_norm"], D)], [(D, D, BF16)], bm, S,
                           transposed=(0,))
    proj = _mm(h0, w["ret_w_in"], "nn", BF16, "ret_in", bm=1024, after=w["started"], cols_outer=True)
    HQ, HV = RET_HEADS * RET_QK, RET_HEADS * RET_V
    rope_rows = [(proj, 2 * HQ + HV, 0, LANES)]
    q_r, k_r, v_r = _rowwise_fwd(_fn_ret_rope, "ret_rope", rope_rows, [cos_r, sin_r], [],
                                 [(HQ, LANES, BF16), (HQ, LANES, BF16), (HV, LANES, BF16)], bm, S)
    ret_o = _ret_attn_fwd(q_r, k_r, v_r, B, S)
    gate_rows = [full(ret_o, HV, RET_V), (proj, HV, 2, RET_V)]
    y0, y0t = _rowwise_fwd(_fn_ret_gate, "ret_gate", gate_rows, [], [(w["ret_gn"], RET_V)], [(HV, RET_V, BF16)], bm, S,
                           transposed=(0,))
    w.update(late("ret_out", y0))
    x1, h1, h1t = _mm_out_norm(y0, w["ret_w_out"], x, w["ffn_norm"][0:1], "ret_out")
    (x2, h2, _), ffn0_saved = ffn_fwd(x1, h1, h1t, 0, w["mla_norm"])

    w.update(late("mla", x2))
    proj2 = _mm(h2, w["mla_w_in"], "nn", F32, "mla_in", bm=2048)
    lat_consts = [(w["mla_q_norm"], LANES), (w["mla_kv_norm"], LANES)]
    cqn, ckvn, kr = _rowwise_fwd(_fn_mla_lat, "mla_latent_norm", [full(proj2, MLA_IN_PAD, LANES)], [], lat_consts,
                                 [(MLA_Q_RANK, LANES, BF16), (MLA_KV_RANK, LANES, BF16), (LANES, LANES, F32)], bm, S)
    qf = _mm(cqn, w["mla_w_qb"], "nn", BF16, "mla_qb", bm=2048, bn=2048)
    kvf = _mm(ckvn, w["mla_w_kvb"], "nn", BF16, "mla_kvb", bm=2048, bn=2048)
    HP, HVm = MLA_HEADS * MLA_PAD, MLA_HEADS * MLA_V
    head_rows = [full(qf, HP, LANES), full(kvf, HP, LANES), full(kr, LANES)]
    head_consts = [(w["mla_q_head_norm"], LANES), (w["mla_k_head_norm"], LANES)]
    q_a, k_a, v_a = _rowwise_fwd(_fn_mla_heads, "mla_heads", head_rows, [cos_m, sin_m], head_consts,
                                 [(HP, LANES, BF16), (HP, LANES, BF16), (HVm, LANES, BF16)], bm, S)
    att_o, lse = _mla_attn_fwd(q_a, k_a, v_a, B, S)
    x3, h3, h3t = _mm_out_norm(att_o, w["mla_w_out"], x2, w["ffn_norm"][1:2], "mla_out")
    (dy, dy_c, loss), ffn1_saved = ffn_fwd(x3, h3, h3t, 1, None)

    dx3, dx3_c, (g_n1, dw8_1) = ffn_bwd(dy, dy_c, ffn1_saved, 1)

    d_att_o = _mm(dx3_c, w["mla_w_out"], "nt", F32, "mla_out_dx", bm=2048)
    g_mla_out = _mm(att_o, dx3_c, "tn", BF16, "mla_out_dw")
    dq_a, dk_a, dv_a = _mla_attn_bwd(q_a, k_a, v_a, att_o, d_att_o, lse, B, S)
    (dqf, dkvf, dkr), (g["mla_q_head_norm"], g["mla_k_head_norm"]) = _rowwise_bwd(
        _fn_mla_heads, "mla_heads_bwd", head_rows, [cos_m, sin_m], head_consts,
        [(dq_a, LANES), (dk_a, LANES), (dv_a, LANES)], bm, S, grad_dtypes=[BF16, BF16, F32])
    dcqn = _mm(dqf, w["mla_w_qb"], "nt", F32, "mla_qb_dx", bm=2048)
    g_qb = _mm(cqn, dqf, "tn", BF16, "mla_qb_dw")
    g_qb = _to_slots(_unpad_heads(g_qb, 1), 1).reshape(N_SHARD, MLA_Q_RANK, -1)
    dckvn = _mm(dkvf, w["mla_w_kvb"], "nt", F32, "mla_kvb_dx", bm=2048)
    g_kvb = _mm(ckvn, dkvf, "tn", BF16, "mla_kvb_dw", bn=512, out_slots=N_SHARD)
    (dproj2,), (g["mla_q_norm"], g["mla_kv_norm"]) = _rowwise_bwd(
        _fn_mla_lat, "mla_latent_norm_bwd", [full(proj2, MLA_IN_PAD, LANES)], [], lat_consts,
        [(dcqn, LANES), (dckvn, LANES), (dkr, LANES)], bm, S, grad_dtypes=[BF16])
    g_mla_in = _mm(h2, dproj2, "tn", BF16, "mla_in_dw")
    token = exchange("mla", [_slot_rows(g_mla_in[:, :MLA_IN]), g_qb, g_kvb, _slot_rows(g_mla_out)])
    dx2, dx2_c, g["mla_norm"] = _mm_dx_norm([dproj2], w["mla_w_in"], x2, w["mla_norm"], dx3, "mla_in_dx", bm=512,
                                            after=token)

    dx1, dx1_c, (g_n0, dw8_0) = ffn_bwd(dx2, dx2_c, ffn0_saved, 0)

    dy0 = _mm(dx1_c, w["ret_w_out"], "nt", F32, "ret_out_dx", bm=1024, cols_outer=True)
    g_ret_out = _mm(y0t, dx1_c, "nn", BF16, "ret_out_dw", bm=1024, bn=512, bk=T)
    token = exchange("reto", [_slot_rows(g_ret_out)])
    gn_behind = w["ret_gn"] + token[0:1, 0:1]
    (d_ret_o, dgate), (g["ret_gn"],) = _rowwise_bwd(_fn_ret_gate, "ret_gate_bwd", gate_rows, [], [(gn_behind, RET_V)],
                                                    [(dy0, RET_V)], bm, S, grad_dtypes=[F32, BF16])
    dq_r, dk_r, dv_r = _ret_attn_bwd(q_r, k_r, v_r, d_ret_o, B, S)
    (dqkv,), _ = _rowwise_bwd(_fn_ret_rope, "ret_rope_bwd", rope_rows, [cos_r, sin_r], [],
                              [(dq_r, LANES), (dk_r, LANES), (dv_r, LANES)], bm, S, grad_dtypes=[BF16], linear=True)
    dx, _, g["ret_norm"] = _mm_dx_norm([dqkv, dgate], w["ret_w_in"], x, w["ret_norm"], dx1, "ret_in_dx")
    g["ffn_norm"] = jnp.concatenate([g_n0, g_n1], axis=0)
    g["ffn_conv_w"] = jnp.stack([dw8_0[0:3], dw8_1[0:3]])
    g["ffn_conv_b"] = jnp.stack([dw8_0[3], dw8_1[3]])
    reduced_small = reduce_small(g)
    g_ret_in = _mm(h0t, [dqkv, dgate], "nn", BF16, "ret_in_dw", bm=1024, bn=512, bk=T, out_slots=N_SHARD,
                   after=reduced_small)
    exchange("ret", [g_ret_in])
    return loss, dx, reduced_small


_SMALL_SHARDED = [("ret_gn", 2), ("mla_norm", 1), ("mla_q_norm", 1), ("mla_kv_norm", 1), ("ffn_conv_w", 2)]
_SMALL_REPLICATED = ["ret_norm", "mla_q_head_norm", "mla_k_head_norm", "ffn_norm", "ffn_conv_b"]
_SMALL_ALL = ["ret_norm", "ret_gn", "mla_norm", "mla_q_norm", "mla_kv_norm", "mla_q_head_norm", "mla_k_head_norm",
              "ffn_norm", "ffn_conv_w", "ffn_conv_b"]


def _to_slots(full, axis):
    shape = full.shape
    split = shape[:axis] + (N_SHARD, shape[axis] // N_SHARD) + shape[axis + 1:]
    return jnp.moveaxis(full.reshape(split), axis, 0).reshape(N_SHARD, -1)


def _from_slots(slots, shard_shape, axis):
    parts = jnp.moveaxis(slots.reshape((N_SHARD,) + tuple(shard_shape)), 0, axis)
    full = shard_shape[:axis] + (N_SHARD * shard_shape[axis],) + shard_shape[axis + 1:]
    return parts.reshape(full)


def _pad_rows(flat, cols, row_unit):
    n, L = flat.shape
    unit = cols * row_unit
    Lp = -(-L // unit) * unit
    if Lp != L:
        flat = jnp.concatenate([flat, jnp.zeros((n, Lp - L), flat.dtype)], axis=1)
    return flat.reshape(n, Lp // cols, cols)


def _pad_heads(a, axis):
    shape = a.shape
    a = a.reshape(shape[:axis] + (MLA_HEADS, MLA_QK) + shape[axis + 1:])
    pad = [(0, 0)] * a.ndim
    pad[axis + 1] = (0, MLA_PAD - MLA_QK)
    return jnp.pad(a, pad).reshape(shape[:axis] + (MLA_HEADS * MLA_PAD,) + shape[axis + 1:])


def _unpad_heads(a, axis):
    shape = a.shape
    a = a.reshape(shape[:axis] + (MLA_HEADS, MLA_PAD) + shape[axis + 1:])
    a = lax.slice_in_dim(a, 0, MLA_QK, axis=axis + 1)
    return a.reshape(shape[:axis] + (MLA_HEADS * MLA_QK,) + shape[axis + 1:])


def kernel(x, ret_norm, ret_w_in, ret_gn, ret_w_out, mla_norm, mla_w_in, mla_q_norm, mla_w_qb, mla_kv_norm, mla_w_kvb, mla_q_head_norm, mla_k_head_norm, mla_w_out, ffn_norm, ffn_w_in, ffn_conv_w, ffn_conv_b, ffn_w_out, loss_target, m_ret_norm, m_ret_w_in, m_ret_gn, m_ret_w_out, m_mla_norm, m_mla_w_in, m_mla_q_norm, m_mla_w_qb, m_mla_kv_norm, m_mla_w_kvb, m_mla_q_head_norm, m_mla_k_head_norm, m_mla_w_out, m_ffn_norm, m_ffn_w_in, m_ffn_conv_w, m_ffn_conv_b, m_ffn_w_out, v_ret_norm, v_ret_w_in, v_ret_gn, v_ret_w_out, v_mla_norm, v_mla_w_in, v_mla_q_norm, v_mla_w_qb, v_mla_kv_norm, v_mla_w_kvb, v_mla_q_head_norm, v_mla_k_head_norm, v_mla_w_out, v_ffn_norm, v_ffn_w_in, v_ffn_conv_w, v_ffn_conv_b, v_ffn_w_out):
    names = ["ret_norm", "ret_w_in", "ret_gn", "ret_w_out", "mla_norm", "mla_w_in", "mla_q_norm", "mla_w_qb",
             "mla_kv_norm", "mla_w_kvb", "mla_q_head_norm", "mla_k_head_norm", "mla_w_out", "ffn_norm", "ffn_w_in",
             "ffn_conv_w", "ffn_conv_b", "ffn_w_out"]
    shard = dict(zip(names, [ret_norm, ret_w_in, ret_gn, ret_w_out, mla_norm, mla_w_in, mla_q_norm, mla_w_qb,
                             mla_kv_norm, mla_w_kvb, mla_q_head_norm, mla_k_head_norm, mla_w_out, ffn_norm, ffn_w_in,
                             ffn_conv_w, ffn_conv_b, ffn_w_out]))
    mom_m = dict(zip(names, [m_ret_norm, m_ret_w_in, m_ret_gn, m_ret_w_out, m_mla_norm, m_mla_w_in, m_mla_q_norm,
                             m_mla_w_qb, m_mla_kv_norm, m_mla_w_kvb, m_mla_q_head_norm, m_mla_k_head_norm, m_mla_w_out,
                             m_ffn_norm, m_ffn_w_in, m_ffn_conv_w, m_ffn_conv_b, m_ffn_w_out]))
    mom_v = dict(zip(names, [v_ret_norm, v_ret_w_in, v_ret_gn, v_ret_w_out, v_mla_norm, v_mla_w_in, v_mla_q_norm,
                             v_mla_w_qb, v_mla_kv_norm, v_mla_w_kvb, v_mla_q_head_norm, v_mla_k_head_norm, v_mla_w_out,
                             v_ffn_norm, v_ffn_w_in, v_ffn_conv_w, v_ffn_conv_b, v_ffn_w_out]))
    B, S, D = x.shape
    T = B * S
    sx, sy = lax.axis_index("x"), lax.axis_index("y")
    me = 2 * sx + sy

    two_d = lambda a: a.reshape(-1, a.shape[-1])
    small_sizes = [int(np.prod(shard[n].shape)) for n, _ in _SMALL_SHARDED]
    small = jnp.concatenate([shard[n].reshape(1, -1) for n, _ in _SMALL_SHARDED], axis=1)
    small = _pad_rows(small, LANES, 8)[0]
    as_mxu = lambda a: two_d(a).astype(BF16)
    is_me = lax.broadcasted_iota(jnp.int32, (N_SHARD, 1, 1), 0) == me
    with_own = lambda gathered, own: jnp.where(is_me, own[None], gathered)
    by_cols = lambda a: jnp.moveaxis(a, 0, 1).reshape(a.shape[1], -1)
    by_rows = lambda a: a.reshape(-1, a.shape[-1])
    pad_in = lambda a: jnp.pad(by_rows(a), ((0, 0), (0, MLA_IN_PAD - MLA_IN)))
    pad_qb = lambda a: _pad_heads(by_cols(a), 1)
    ret_in_shard = as_mxu(shard["ret_w_in"])
    g_ret_in, gsmall = _all_gather_weights([ret_in_shard], small)
    later = [
        ("ret_out", [("ret_w_out", as_mxu(shard["ret_w_out"]), by_rows)]),
        ("ffn0", [("ffn_w_in0", as_mxu(shard["ffn_w_in"][0]), by_cols), ("ffn_w_out0", as_mxu(shard["ffn_w_out"][0]), by_rows)]),
        ("mla", [("mla_w_in", as_mxu(shard["mla_w_in"]), pad_in), ("mla_w_qb", as_mxu(shard["mla_w_qb"]), pad_qb),
                 ("mla_w_kvb", as_mxu(shard["mla_w_kvb"]), by_cols), ("mla_w_out", as_mxu(shard["mla_w_out"]), by_rows)]),
        ("ffn1", [("ffn_w_in1", as_mxu(shard["ffn_w_in"][1]), by_cols), ("ffn_w_out1", as_mxu(shard["ffn_w_out"][1]), by_rows)]),
    ]
    gathering = {}
    token = gsmall
    for group, items in later:
        shards = [s_ for _, s_, _ in items]
        lands = [lax.empty((N_SHARD,) + s_.shape, s_.dtype) for s_ in shards]
        send_sems, recv_sems, shards, lands, token = _exchange_start(
            _weight_copies, shards, lands, 3 * len(shards), f"weights_start_{group}", after=token)
        gathering[group] = (send_sems, recv_sems, shards, lands, items)

    def late(group, after):
        send_sems, recv_sems, shards, lands, items = gathering[group]
        shards, lands = _exchange_wait(_weight_copies, send_sems, recv_sems, shards, lands, after,
                                       f"weights_wait_{group}")
        return {key: full(with_own(l_, s_)) for (key, _, full), s_, l_ in zip(items, shards, lands)}

    gsmall = with_own(gsmall, small).reshape(N_SHARD, -1)
    wfull = {}
    off = 0
    for (n, ax), sz in zip(_SMALL_SHARDED, small_sizes):
        wfull[n] = _from_slots(gsmall[:, off:off + sz], shard[n].shape, ax)
        off += sz
    for n in _SMALL_REPLICATED:
        wfull[n] = shard[n]

    conv8 = jnp.concatenate([wfull["ffn_conv_w"], wfull["ffn_conv_b"][:, None, :],
                             jnp.zeros((2, 4, FFN_DIM), F32)], axis=1)
    w = {
        "started": token, "ret_norm": wfull["ret_norm"], "ret_w_in": by_cols(with_own(g_ret_in, ret_in_shard)),
        "ret_gn": wfull["ret_gn"].reshape(1, RET_HEADS * RET_V), "mla_norm": wfull["mla_norm"],
        "mla_q_norm": wfull["mla_q_norm"], "mla_kv_norm": wfull["mla_kv_norm"],
        "mla_q_head_norm": jnp.pad(wfull["mla_q_head_norm"], ((0, 0), (0, MLA_PAD - MLA_QK))),
        "mla_k_head_norm": jnp.pad(wfull["mla_k_head_norm"], ((0, 0), (0, MLA_PAD - MLA_QK))),
        "ffn_norm": wfull["ffn_norm"], "ffn_conv8": conv8,
    }

    started = {}

    def exchange(group, arrays):
        lands = [lax.empty((N_PEERS, p.shape[1] // 2, p.shape[2]), p.dtype) for p in arrays]
        send_sems, recv_sems, ps, lands, token = _exchange_start(
            _grad_copies, arrays, lands, N_PEERS * len(arrays), f"grads_start_{group}")
        started[group] = (send_sems, recv_sems, ps, lands)
        return token

    small_shapes = {
        "ret_norm": (1, D_MODEL), "ret_gn": (1, RET_HEADS, RET_V), "mla_norm": (1, D_MODEL),
        "mla_q_norm": (1, MLA_Q_RANK), "mla_kv_norm": (1, MLA_KV_RANK), "mla_q_head_norm": (1, MLA_QK),
        "mla_k_head_norm": (1, MLA_QK), "ffn_norm": (2, D_MODEL), "ffn_conv_w": (2, 3, FFN_DIM),
        "ffn_conv_b": (2, FFN_DIM)}

    def reduce_small(gl):
        gl = dict(gl, mla_q_head_norm=gl["mla_q_head_norm"][:, :MLA_QK], mla_k_head_norm=gl["mla_k_head_norm"][:, :MLA_QK])
        packed = jnp.concatenate([gl[n].reshape(1, -1) for n in _SMALL_ALL], axis=1)
        return _all_reduce_small(_pad_rows(packed, LANES, 8)[0])

    loss_part, dx, gsm = _local_step(x.reshape(T, D), loss_target.reshape(T, D), w, B, S, late, exchange,
                                     reduce_small)
    loss = lax.psum(loss_part, ("x", "y", "c"))

    delta, new_m, new_v, grads = {}, {}, {}, {}

    def reduced(group, after):
        send_sems, recv_sems, ps, lands = started[group]
        ps, lands = _exchange_wait(_grad_copies, send_sems, recv_sems, ps, lands, after, f"grads_wait_{group}")
        halves = [_sum_partials(p_, l_, f"grads_sum_{group}_{i}") for i, (p_, l_) in enumerate(zip(ps, lands))]
        return [two_d(r) for r in _sibling_share(halves, f"grads_share_{group}")]

    def adamw(n, g_):
        shp = shard[n].shape
        grads[n] = g_.reshape(shp)
        flat = lambda a: a.reshape(-1, shp[-1])
        d_, m_, v_ = _adamw(flat(shard[n]), flat(grads[n]), flat(mom_m[n]), flat(mom_v[n]), f"adamw_{n}")
        delta[n], new_m[n], new_v[n] = d_.reshape(shp), m_.reshape(shp), v_.reshape(shp)
        return d_

    ffn1 = reduced("ffn1", started["ret"][2][0])
    mla = reduced("mla", ffn1[0])
    ffn0 = reduced("ffn0", mla[0])
    reto = reduced("reto", ffn0[0])
    early = [adamw(n, g_) for n, g_ in zip(["mla_w_in", "mla_w_qb", "mla_w_kvb", "mla_w_out"], mla)]
    early.append(adamw("ffn_w_in", jnp.stack([ffn0[0], ffn1[0]])))
    early.append(adamw("ffn_w_out", jnp.stack([ffn0[1], ffn1[1]])))
    early.append(adamw("ret_w_out", reto[0]))
    ret = reduced("ret", jnp.stack([d_[0, 0] for d_ in early]))
    adamw("ret_w_in", ret[0])

    gsm = gsm.reshape(-1)
    sharded_axis = dict(_SMALL_SHARDED)
    off = 0
    for n in _SMALL_ALL:
        sz = int(np.prod(small_shapes[n]))
        gn = gsm[off:off + sz].reshape(small_shapes[n])
        off += sz
        if n in sharded_axis:
            ax = sharded_axis[n]
            width = shard[n].shape[ax]
            gn = lax.dynamic_slice_in_dim(gn, me * width, width, axis=ax)
        grads[n] = gn

    pack_small = lambda d: _pad_rows(jnp.concatenate([d[n].reshape(1, -1) for n in _SMALL_ALL], axis=1), LANES, 8)[0]
    d_, m_, v_ = _adamw(pack_small(shard), pack_small(grads), pack_small(mom_m), pack_small(mom_v), "adamw_small")
    off = 0
    for n in _SMALL_ALL:
        sz = int(np.prod(shard[n].shape))
        for dst, src in ((delta, d_), (new_m, m_), (new_v, v_)):
            dst[n] = src.reshape(-1)[off:off + sz].reshape(shard[n].shape)
        off += sz

    return (loss, dx.reshape(B, S, D), *[grads[n] for n in names], *[delta[n] for n in names],
            *[new_m[n] for n in names], *[new_v[n] for n in names])
```

```python
import functools

import numpy as np
import jax
import jax.numpy as jnp
from jax import lax
from jax.experimental import pallas as pl
from jax.experimental.pallas import tpu as pltpu

F32 = jnp.float32
BF16 = jnp.bfloat16
MXU_DTYPE = jnp.bfloat16

CHUNK = 64
RMS_EPS = 1e-6
ROPE_THETA = 10000.0
D_MODEL = 1024
RET_HEADS = 4
RET_QK = 256
RET_V = 512
RET_GAMMA_BASE = -5.0
MLA_HEADS = 8
MLA_Q_RANK = 384
MLA_KV_RANK = 256
MLA_NOPE = 128
MLA_ROPE = 64
MLA_V = 128
MLA_QK = MLA_NOPE + MLA_ROPE
MLA_PAD = 256
MLA_IN = MLA_Q_RANK + MLA_KV_RANK + MLA_ROPE
MLA_IN_PAD = MLA_IN + 64
MASK_VALUE = -1e30
FFN_DIM = 2816
ADAM_LR = 0.001
ADAM_B1 = 0.9
ADAM_B2 = 0.999
ADAM_EPS = 1e-08
ADAM_WD = 0.01
ADAM_STEP = 10

LANES = 128
MLA_FWD_BLOCK = 512
VMEM_LIMIT = 56 * 2 ** 20
N_SHARD = 4
N_DEV = 8

MESH = pl.DeviceIdType.MESH


def _params(sem=None, **kw):
    return pltpu.CompilerParams(dimension_semantics=sem, vmem_limit_bytes=VMEM_LIMIT, **kw)


def _pick(dim, target):
    if dim <= target:
        return dim
    best = None
    for d in range(LANES, target + 1, LANES):
        if dim % d == 0:
            best = d
    assert best is not None, (dim, target)
    return best


def _mm(a, b, dims, out_dtype, name, residual=None, bm=512, bn=1024, bk=2048, out_slots=None, after=None,
        cols_outer=False):
    a_parts = list(a) if isinstance(a, (list, tuple)) else [a]
    b_parts = list(b) if isinstance(b, (list, tuple)) else [b]
    parts_on_n = dims == "tn" or len(b_parts) > 1
    if parts_on_n:
        assert len(a_parts) == 1 and dims in ("tn", "nn")
        (K, M) = a_parts[0].shape if dims == "tn" else a_parts[0].shape[::-1]
        N = sum(p.shape[1] for p in b_parts)
        part_widths = [p.shape[1] for p in b_parts]
    else:
        assert len(b_parts) == 1
        M = a_parts[0].shape[0]
        K = sum(p.shape[1] for p in a_parts)
        N = b_parts[0].shape[1 if dims == "nn" else 0]
        part_widths = [p.shape[1] for p in a_parts]
    bm, bn, bk = _pick(M, bm), _pick(N, bn), _pick(K, min(bk, 1024) if dims == "tn" else bk)
    nk = K // bk
    unit = bn if parts_on_n else bk
    assert all(wd % unit == 0 for wd in part_widths), (name, part_widths, unit)
    bounds = np.cumsum([0] + [wd // unit for wd in part_widths])
    ranges = [(int(lo), int(hi)) for lo, hi in zip(bounds[:-1], bounds[1:])]

    def part_index(idx, lo, hi):
        return jnp.clip(idx - lo, 0, hi - lo - 1)

    if parts_on_n:
        if dims == "tn":
            a_specs = [pl.BlockSpec((bk, bm), lambda i, j, k: (k, i))]
            dn = (((0,), (0,)), ((), ()))
        else:
            a_specs = [pl.BlockSpec((bm, bk), lambda i, j, k: (i, k))]
            dn = (((1,), (0,)), ((), ()))
        b_specs = [pl.BlockSpec((bk, bn), functools.partial(lambda i, j, k, lo, hi: (k, part_index(j, lo, hi)), lo=lo, hi=hi))
                   for lo, hi in ranges]
    else:
        a_specs = [pl.BlockSpec((bm, bk), functools.partial(lambda i, j, k, lo, hi: (i, part_index(k, lo, hi)), lo=lo, hi=hi))
                   for lo, hi in ranges]
        if dims == "nt":
            b_specs = [pl.BlockSpec((bn, bk), lambda i, j, k: (j, k))]
        else:
            b_specs = [pl.BlockSpec((bk, bn), lambda i, j, k: (k, j))]
        dn = (((1,), (1 if dims == "nt" else 0,)), ((), ()))
    r_spec = pl.BlockSpec((bm, bn), lambda i, j, k: (i, j))
    if out_slots is None:
        o_spec, o_shape = r_spec, (M, N)
    else:
        ns = N // out_slots
        assert ns % bn == 0, (name, ns, bn)
        nbs = ns // bn
        o_spec = pl.BlockSpec((None, bm, bn), lambda i, j, k: (j // nbs, i, j % nbs))
        o_shape = (out_slots, M, ns)
    has_res = residual is not None
    na, nb = len(a_parts), len(b_parts)

    def body(*refs):
        a_refs, b_refs = refs[:na], refs[na:na + nb]
        r_ref = refs[na + nb] if has_res else None
        n_in = na + nb + has_res + (after is not None)
        o_ref = refs[n_in]
        acc_ref = refs[n_in + 1] if nk > 1 else None
        k = pl.program_id(2)

        def finish(acc):
            if has_res:
                acc = acc + r_ref[...].astype(F32)
            o_ref[...] = acc.astype(out_dtype)

        def compute(a_ref, b_ref):
            p = lax.dot_general(a_ref[...].astype(MXU_DTYPE), b_ref[...].astype(MXU_DTYPE), dn,
                                preferred_element_type=F32)
            if nk == 1:
                finish(p)
                return

            @pl.when(k == 0)
            def _():
                acc_ref[...] = p

            @pl.when(jnp.logical_and(k > 0, k < nk - 1))
            def _():
                acc_ref[...] += p

            @pl.when(k == nk - 1)
            def _():
                finish(acc_ref[...] + p)

        if len(ranges) == 1:
            compute(a_refs[0], b_refs[0])
        else:
            idx = pl.program_id(0 if cols_outer else 1) if parts_on_n else k
            for p, (lo, hi) in enumerate(ranges):
                @pl.when(jnp.logical_and(idx >= lo, idx < hi))
                def _(p=p):
                    compute(a_refs[0 if parts_on_n else p], b_refs[p if parts_on_n else 0])

    after_specs = [] if after is None else [pl.BlockSpec(after.shape, lambda i, j, k: (0, 0))]
    in_specs = a_specs + b_specs + ([r_spec] if has_res else []) + after_specs
    grid = (M // bm, N // bn, nk)
    if cols_outer:
        swap = lambda sp: pl.BlockSpec(sp.block_shape, functools.partial(lambda j, i, k, f: f(i, j, k), f=sp.index_map))
        in_specs, o_spec, grid = [swap(sp) for sp in in_specs], swap(o_spec), (grid[1], grid[0], nk)
    return pl.pallas_call(
        body, name=name, grid=grid,
        in_specs=in_specs, out_specs=o_spec,
        out_shape=jax.ShapeDtypeStruct(o_shape, out_dtype),
        scratch_shapes=[pltpu.VMEM((bm, bn), F32)] if nk > 1 else [],
        compiler_params=_params(("parallel", "parallel", "arbitrary")),
    )(*a_parts, *b_parts, *((residual,) if has_res else ()), *(() if after is None else (after,)))


def _mm_out_norm(a, w, residual, gain, name, bm=512):
    (M, K), N = a.shape, w.shape[1]
    bm = _pick(M, bm)

    def body(a_ref, w_ref, r_ref, g_ref, o_ref, h_ref, ht_ref):
        acc = lax.dot_general(a_ref[...].astype(MXU_DTYPE), w_ref[...].astype(MXU_DTYPE), _NN,
                              preferred_element_type=F32) + r_ref[...]
        o_ref[...] = acc
        hv = _fn_rms([[acc]], [], [[g_ref[...]]])[0][0]
        h_ref[...] = hv.astype(h_ref.dtype)
        ht_ref[...] = hv.T.astype(ht_ref.dtype)

    row = pl.BlockSpec((bm, N), lambda i: (i, 0))
    whole = lambda arr: pl.BlockSpec(arr.shape, lambda i: (0, 0))
    return pl.pallas_call(
        body, name=name, grid=(M // bm,),
        in_specs=[pl.BlockSpec((bm, K), lambda i: (i, 0)), whole(w), row, whole(gain)],
        out_specs=[row, row, pl.BlockSpec((N, bm), lambda i: (0, i))],
        out_shape=[jax.ShapeDtypeStruct((M, N), F32), jax.ShapeDtypeStruct((M, N), BF16),
                   jax.ShapeDtypeStruct((N, M), BF16)],
        compiler_params=_params(("parallel",)),
    )(a, w, residual, gain)


def _mm_out_loss(a, w, residual, target, name, bm=512):
    (M, K), N = a.shape, w.shape[1]
    bm = _pick(M, bm)

    def body(a_ref, w_ref, r_ref, t_ref, dy_ref, dyc_ref, l_ref):
        y = lax.dot_general(a_ref[...].astype(MXU_DTYPE), w_ref[...].astype(MXU_DTYPE), _NN,
                            preferred_element_type=F32) + r_ref[...]
        err = y - t_ref[...]
        dy_ref[...] = err / N
        dyc_ref[...] = (err / N).astype(dyc_ref.dtype)
        part = jnp.full((8, LANES), 0.5 * jnp.sum(jnp.mean(err * err, axis=-1)), F32)

        @pl.when(pl.program_id(0) == 0)
        def _():
            l_ref[...] = part

        @pl.when(pl.program_id(0) > 0)
        def _():
            l_ref[...] += part

    row = pl.BlockSpec((bm, N), lambda i: (i, 0))
    dy, dyc, l = pl.pallas_call(
        body, name=name, grid=(M // bm,),
        in_specs=[pl.BlockSpec((bm, K), lambda i: (i, 0)), pl.BlockSpec(w.shape, lambda i: (0, 0)), row, row],
        out_specs=[row, row, pl.BlockSpec((8, LANES), lambda i: (0, 0))],
        out_shape=[jax.ShapeDtypeStruct((M, N), F32), jax.ShapeDtypeStruct((M, N), BF16),
                   jax.ShapeDtypeStruct((8, LANES), F32)],
        compiler_params=_params(("arbitrary",)),
    )(a, w, residual, target)
    return dy, dyc, l[0, 0]


def _mm_dx_norm(a_parts, w, x, gain, add, name, bm=512, after=None):
    M = a_parts[0].shape[0]
    N, K = w.shape
    widths = [p.shape[1] for p in a_parts]
    assert sum(widths) == K, (name, widths, K)
    offs = [int(o) for o in np.cumsum([0] + widths[:-1])]
    bm = _pick(M, bm)
    na = len(a_parts)
    n_in = na + 4 + (after is not None)

    def body(*refs):
        w_ref, x_ref, g_ref, add_ref = refs[na:na + 4]
        dx_ref, dxc_ref, dg_ref = refs[n_in:n_in + 3]
        dh = None
        for a_ref, off, wd in zip(refs[:na], offs, widths):
            p = lax.dot_general(a_ref[...].astype(MXU_DTYPE), w_ref[:, off:off + wd].astype(MXU_DTYPE), _NT,
                                preferred_element_type=F32)
            dh = p if dh is None else dh + p
        _, vjp = jax.vjp(lambda xv, gv: _fn_rms([[xv]], [], [[gv]])[0][0], x_ref[...], g_ref[...])
        dxv, dgv = vjp(dh)
        dxv = dxv + add_ref[...]
        dx_ref[...] = dxv
        dxc_ref[...] = dxv.astype(dxc_ref.dtype)

        @pl.when(pl.program_id(0) == 0)
        def _():
            dg_ref[...] = dgv

        @pl.when(pl.program_id(0) > 0)
        def _():
            dg_ref[...] += dgv

    row = pl.BlockSpec((bm, N), lambda i: (i, 0))
    whole = lambda a: pl.BlockSpec(a.shape, lambda i: (0, 0))
    in_specs = [pl.BlockSpec((bm, wd), lambda i: (i, 0)) for wd in widths] + [whole(w), row, whole(gain), row]
    in_specs += [] if after is None else [whole(after)]
    return pl.pallas_call(
        body, name=name, grid=(M // bm,),
        in_specs=in_specs, out_specs=[row, row, whole(gain)],
        out_shape=[jax.ShapeDtypeStruct((M, N), F32), jax.ShapeDtypeStruct((M, N), BF16),
                   jax.ShapeDtypeStruct(gain.shape, F32)],
        compiler_params=_params(("arbitrary",)),
    )(*a_parts, w, x, gain, add, *(() if after is None else (after,)))


def _tiles(ref, width, tile):
    return [ref[:, t * tile:(t + 1) * tile].astype(F32) for t in range(width // tile)]


def _row_specs(rows, pos, consts, bm, S):
    npos_blocks = S // bm
    specs = [pl.BlockSpec((bm, w), functools.partial(lambda i, c: (i, c), c=cb)) for (_, w, cb, _) in rows]
    specs += [pl.BlockSpec((bm, p.shape[1]), lambda i: (i % npos_blocks, 0)) for p in pos]
    specs += [pl.BlockSpec(c.shape, lambda i: (0, 0)) for (c, _) in consts]
    return specs


def _rowwise_fwd(fn, name, rows, pos, consts, outs, bm, S, transposed=()):
    T = rows[0][0].shape[0]
    nr, npos, nc, no = len(rows), len(pos), len(consts), len(outs)

    def body(*refs):
        row_v = [_tiles(r, w, t) for r, (_, w, _, t) in zip(refs[:nr], rows)]
        pos_v = [r[...] for r in refs[nr:nr + npos]]
        const_v = [_tiles(r, c.shape[1], t) for r, (c, t) in zip(refs[nr + npos:nr + npos + nc], consts)]
        res = fn(row_v, pos_v, const_v)
        out_refs = refs[nr + npos + nc:]
        for o_ref, tiles, (w, t, dt) in zip(out_refs, res, outs):
            for k, v in enumerate(tiles):
                o_ref[:, k * t:(k + 1) * t] = v.astype(dt)
        for t_ref, a in zip(out_refs[no:], transposed):
            t = outs[a][1]
            for k, v in enumerate(res[a]):
                t_ref[k * t:(k + 1) * t, :] = v.T.astype(t_ref.dtype)

    return pl.pallas_call(
        body, name=name, grid=(T // bm,),
        in_specs=_row_specs(rows, pos, consts, bm, S),
        out_specs=[pl.BlockSpec((bm, w), lambda i: (i, 0)) for (w, _, _) in outs]
        + [pl.BlockSpec((outs[a][0], bm), lambda i: (0, i)) for a in transposed],
        out_shape=[jax.ShapeDtypeStruct((T, w), dt) for (w, _, dt) in outs]
        + [jax.ShapeDtypeStruct((outs[a][0], T), BF16) for a in transposed],
        compiler_params=_params(("parallel",)),
    )(*[r[0] for r in rows], *pos, *[c[0] for c in consts])


def _rowwise_bwd(fn, name, rows, pos, consts, cts, bm, S, adds=None, grad_dtypes=None, mxu_copies=(), linear=False):
    adds = adds or {}
    T = rows[0][0].shape[0]
    nr, npos, nc, nct = len(rows), len(pos), len(consts), len(cts)
    add_idx = sorted(adds)
    grad_dtypes = grad_dtypes or [F32] * nr

    def body(*refs):
        it = iter(refs)
        row_refs = [None if linear else next(it) for _ in range(nr)]
        pos_refs = [next(it) for _ in range(npos)]
        const_refs = [next(it) for _ in range(nc)]
        ct_refs = [next(it) for _ in range(nct)]
        add_refs = {k: next(it) for k in add_idx}
        drow_refs = [next(it) for _ in range(nr)]
        copy_refs = {a: next(it) for a in mxu_copies}
        dconst_refs = [next(it) for _ in range(nc)]
        if linear:
            row_v = [[jnp.zeros((bm, t), F32)] * (w // t) for (_, w, _, t) in rows]
        else:
            row_v = [_tiles(r, w, t) for r, (_, w, _, t) in zip(row_refs, rows)]
        pos_v = [r[...] for r in pos_refs]
        const_v = [_tiles(r, c.shape[1], t) for r, (c, t) in zip(const_refs, consts)]
        ct_v = [_tiles(r, c.shape[1], t) for r, (c, t) in zip(ct_refs, cts)]
        _, vjp = jax.vjp(lambda rv, cv: fn(rv, pos_v, cv), row_v, const_v)
        drows, dconsts = vjp(ct_v)
        for a, (d_ref, tiles, (_, w, _, t)) in enumerate(zip(drow_refs, drows, rows)):
            for k, v in enumerate(tiles):
                if a in add_refs:
                    v = v + add_refs[a][:, k * t:(k + 1) * t].astype(F32)
                d_ref[:, k * t:(k + 1) * t] = v.astype(d_ref.dtype)
                if a in copy_refs:
                    copy_refs[a][:, k * t:(k + 1) * t] = v.astype(BF16)
        first = pl.program_id(0) == 0
        for d_ref, tiles, (_, t) in zip(dconst_refs, dconsts, consts):
            for k, v in enumerate(tiles):
                @pl.when(first)
                def _(d_ref=d_ref, k=k, t=t, v=v):
                    d_ref[:, k * t:(k + 1) * t] = v

                @pl.when(jnp.logical_not(first))
                def _(d_ref=d_ref, k=k, t=t, v=v):
                    d_ref[:, k * t:(k + 1) * t] += v

    in_specs = _row_specs([] if linear else rows, pos, consts, bm, S)
    in_specs += [pl.BlockSpec((bm, c.shape[1]), lambda i: (i, 0)) for (c, _) in cts]
    in_specs += [pl.BlockSpec((bm, adds[k].shape[1]), lambda i: (i, 0)) for k in add_idx]
    out_specs = [pl.BlockSpec((bm, w), lambda i: (i, 0)) for (_, w, _, _) in rows]
    out_specs += [pl.BlockSpec((bm, rows[a][1]), lambda i: (i, 0)) for a in mxu_copies]
    out_specs += [pl.BlockSpec(c.shape, lambda i: (0, 0)) for (c, _) in consts]
    out_shape = [jax.ShapeDtypeStruct((T, w), dt) for (_, w, _, _), dt in zip(rows, grad_dtypes)]
    out_shape += [jax.ShapeDtypeStruct((T, rows[a][1]), BF16) for a in mxu_copies]
    out_shape += [jax.ShapeDtypeStruct(c.shape, F32) for (c, _) in consts]
    res = pl.pallas_call(
        body, name=name, grid=(T // bm,),
        in_specs=in_specs, out_specs=out_specs, out_shape=out_shape,
        compiler_params=_params(("arbitrary",)),
    )(*([] if linear else [r[0] for r in rows]), *pos, *[c[0] for c in consts], *[c[0] for c in cts],
      *[adds[k] for k in add_idx])
    n_rows = nr + len(mxu_copies)
    return res[:n_rows], res[n_rows:]


def _ssq(tiles):
    s = jnp.sum(tiles[0] * tiles[0], axis=-1, keepdims=True)
    for t in tiles[1:]:
        s = s + jnp.sum(t * t, axis=-1, keepdims=True)
    return s


def _sigmoid(x):
    return 0.5 * jnp.tanh(0.5 * x) + 0.5


def _fn_rms(rows, pos, consts):
    (x,), (g,) = rows[0], consts[0]
    r = lax.rsqrt(jnp.mean(x * x, axis=-1, keepdims=True) + RMS_EPS)
    return [[x * r * g]]


def _fn_ret_rope(rows, pos, consts):
    (qkv,) = rows
    nq = RET_HEADS * RET_QK // LANES
    q, k, v = qkv[:nq], qkv[nq:2 * nq], qkv[2 * nq:]
    cos, sin = pos

    def rot(t, scale):
        out = []
        for h in range(RET_HEADS):
            x1, x2 = t[2 * h], t[2 * h + 1]
            o1, o2 = x1 * cos - x2 * sin, x2 * cos + x1 * sin
            out += [o1, o2] if scale is None else [o1 * scale, o2 * scale]
        return out

    return [rot(q, None), rot(k, RET_QK ** -0.5), list(v)]


def _fn_ret_gate(rows, pos, consts):
    o, g = rows
    (gn,) = consts
    out = []
    for h in range(RET_HEADS):
        r = lax.rsqrt(jnp.mean(o[h] * o[h], axis=-1, keepdims=True) + RMS_EPS)
        out.append((o[h] * r * gn[h]) * (g[h] * _sigmoid(g[h])))
    return [out]


def _fn_mla_lat(rows, pos, consts):
    (p,) = rows
    gq, gkv = consts
    nq, nkv = MLA_Q_RANK // LANES, MLA_KV_RANK // LANES
    cq, ckv, kr = p[:nq], p[nq:nq + nkv], p[nq + nkv]
    rq = lax.rsqrt(_ssq(cq) / MLA_Q_RANK + RMS_EPS)
    rkv = lax.rsqrt(_ssq(ckv) / MLA_KV_RANK + RMS_EPS)
    return [[t * rq * g for t, g in zip(cq, gq)], [t * rkv * g for t, g in zip(ckv, gkv)], [kr]]


def _swap32_impl(x):
    lane = lax.broadcasted_iota(jnp.int32, x.shape, 1)
    up, down = pltpu.roll(x, LANES - 32, 1), pltpu.roll(x, 32, 1)
    return jnp.where(lane < 32, up, jnp.where(lane < 64, down, 0.0))


@jax.custom_vjp
def _swap32(x):
    return _swap32_impl(x)


_swap32.defvjp(lambda x: (_swap32_impl(x), None), lambda _, g: (_swap32_impl(g),))


def _fn_mla_heads(rows, pos, consts):
    qf, kvf, (kr,) = rows
    cos, sin = pos
    gq, gk = consts
    q_out, k_out, v_out = [], [], []
    for h in range(MLA_HEADS):
        q0, q1 = qf[2 * h], qf[2 * h + 1]
        r = lax.rsqrt(_ssq([q0, q1]) / MLA_QK + RMS_EPS)
        a0, a1 = q0 * r * gq[0], q1 * r * gq[1]
        a1 = a1 * cos + _swap32(a1) * sin
        q_out += [a0 * (MLA_QK ** -0.5), a1 * (MLA_QK ** -0.5)]
        k0 = kvf[2 * h]
        r = lax.rsqrt(_ssq([k0, kr]) / MLA_QK + RMS_EPS)
        b0, b1 = k0 * r * gk[0], kr * r * gk[1]
        k_out += [b0, b1 * cos + _swap32(b1) * sin]
        v_out.append(kvf[2 * h + 1])
    return [q_out, k_out, v_out]


def _shift_down(x, n):
    row = lax.broadcasted_iota(jnp.int32, x.shape, 0)
    return jnp.where(row >= n, pltpu.roll(x, n, 0), 0.0)


def _shift_up(x, n):
    rows = x.shape[0]
    row = lax.broadcasted_iota(jnp.int32, x.shape, 0)
    return jnp.where(row < rows - n, pltpu.roll(x, rows - n, 0), 0.0)


def _conv_blocks(S):
    cb = 256
    return cb, FFN_DIM // cb


def _conv_fwd(ag, w8, B, S, name):
    cb, ncb = _conv_blocks(S)

    def body(a_ref, g_ref, w_ref, u_ref, ut_ref):
        g = g_ref[...].astype(F32)
        w = w_ref[...]
        gc = w[0:1] * _shift_down(g, 2) + w[1:2] * _shift_down(g, 1) + w[2:3] * g + w[3:4]
        u = a_ref[...].astype(F32) * (gc * _sigmoid(gc))
        u_ref[...] = u.astype(u_ref.dtype)
        ut_ref[...] = u.T.astype(ut_ref.dtype)

    return pl.pallas_call(
        body, name=name, grid=(ncb, B),
        in_specs=[pl.BlockSpec((S, cb), lambda j, b: (b, j)),
                  pl.BlockSpec((S, cb), lambda j, b: (b, ncb + j)),
                  pl.BlockSpec((8, cb), lambda j, b: (0, j))],
        out_specs=[pl.BlockSpec((S, cb), lambda j, b: (b, j)), pl.BlockSpec((cb, S), lambda j, b: (j, b))],
        out_shape=[jax.ShapeDtypeStruct((B * S, FFN_DIM), BF16), jax.ShapeDtypeStruct((FFN_DIM, B * S), BF16)],
        compiler_params=_params(("parallel", "parallel")),
    )(ag, ag, w8)


def _conv_bwd(ag, w8, du, B, S, name):
    cb, ncb = _conv_blocks(S)

    def body(a_ref, g_ref, w_ref, du_ref, da_ref, dg_ref, dw_ref):
        g = g_ref[...].astype(F32)
        w = w_ref[...]
        g1, g2 = _shift_down(g, 1), _shift_down(g, 2)
        gc = w[0:1] * g2 + w[1:2] * g1 + w[2:3] * g + w[3:4]
        sg = _sigmoid(gc)
        du_v = du_ref[...]
        da_ref[...] = (du_v * (gc * sg)).astype(da_ref.dtype)
        dgc = du_v * a_ref[...].astype(F32) * (sg * (1.0 + gc * (1.0 - sg)))
        dg = w[2:3] * dgc + w[1:2] * _shift_up(dgc, 1) + w[0:1] * _shift_up(dgc, 2)
        dg_ref[...] = dg.astype(dg_ref.dtype)
        part = jnp.concatenate([
            jnp.sum(dgc * g2, axis=0, keepdims=True), jnp.sum(dgc * g1, axis=0, keepdims=True),
            jnp.sum(dgc * g, axis=0, keepdims=True), jnp.sum(dgc, axis=0, keepdims=True),
            jnp.zeros((4, cb), F32)], axis=0)

        @pl.when(pl.program_id(1) == 0)
        def _():
            dw_ref[...] = part

        @pl.when(pl.program_id(1) > 0)
        def _():
            dw_ref[...] += part

    blk = lambda j, b: (b, j)
    return pl.pallas_call(
        body, name=name, grid=(ncb, B),
        in_specs=[pl.BlockSpec((S, cb), blk),
                  pl.BlockSpec((S, cb), lambda j, b: (b, ncb + j)),
                  pl.BlockSpec((8, cb), lambda j, b: (0, j)),
                  pl.BlockSpec((S, cb), blk)],
        out_specs=[pl.BlockSpec((S, cb), blk), pl.BlockSpec((S, cb), blk),
                   pl.BlockSpec((8, cb), lambda j, b: (0, j))],
        out_shape=[jax.ShapeDtypeStruct((B * S, FFN_DIM), BF16), jax.ShapeDtypeStruct((B * S, FFN_DIM), BF16),
                   jax.ShapeDtypeStruct((8, FFN_DIM), F32)],
        compiler_params=_params(("parallel", "arbitrary")),
    )(ag, ag, w8, du)


_NT = (((1,), (1,)), ((), ()))
_NN = (((1,), (0,)), ((), ()))
_TN = (((0,), (0,)), ((), ()))


def _dot(a, b, dn):
    return lax.dot_general(a.astype(MXU_DTYPE), b.astype(MXU_DTYPE), dn, preferred_element_type=F32)


def _run_bits(n):
    bits, b = [], 1
    while b < n:
        bits.append(b)
        b *= 2
    return bits[::-1]


def _key_runs(n, nq, update):
    for bit in _run_bits(nq + 1):
        @pl.when((n & bit) != 0)
        def _(bit=bit):
            update(n & ~(2 * bit - 1), bit, (n & (bit - 1)) == 0)


def _earlier_runs(n, nq, update):
    for bit in _run_bits(nq):
        @pl.when((n & bit) != 0)
        def _(bit=bit):
            update(n & ~(2 * bit - 1), bit, False)


def _chunk_visible(shape, nblk, blk):
    key = lax.broadcasted_iota(jnp.int32, shape, 0) - (nblk - 1) * blk
    query = lax.broadcasted_iota(jnp.int32, shape, 1)
    return jnp.logical_or(key < 0, (key // CHUNK) <= (query // CHUNK))


def _mla_attn_fwd(q, k, v, B, S):
    blk = min(MLA_FWD_BLOCK, S)
    H, nq = MLA_HEADS, S // blk

    def body(q_ref, k_ref, v_ref, o_ref, lse_ref, m_ref, l_ref, acc_ref):
        def qblock(i, _):
            q_rows = pl.ds(pl.multiple_of(i * blk, blk), blk)
            qi = q_ref[q_rows, :]
            m_ref[...] = jnp.full(m_ref.shape, MASK_VALUE, F32)
            l_ref[...] = jnp.zeros(l_ref.shape, F32)
            acc_ref[...] = jnp.zeros(acc_ref.shape, F32)

            def keys(first, nblk, last):
                rows = pl.ds(pl.multiple_of(first * blk, blk), nblk * blk)
                s = _dot(k_ref[rows, :], qi, _NT)
                s = jnp.where(jnp.logical_or(_chunk_visible(s.shape, nblk, blk), jnp.logical_not(last)), s, MASK_VALUE)
                m = m_ref[...]
                m2 = jnp.maximum(m, jnp.max(s, axis=0, keepdims=True))
                alpha = jnp.exp(m - m2)
                p = jnp.exp(s - m2)
                l_ref[...] = alpha * l_ref[...] + jnp.sum(p, axis=0, keepdims=True)
                acc_ref[...] = alpha * acc_ref[...] + _dot(v_ref[rows, :], p, _TN)
                m_ref[...] = m2

            _key_runs(i + 1, nq, keys)
            l = l_ref[...]
            o_ref[q_rows, :] = (acc_ref[...] / l).T
            lse_ref[0, :, q_rows] = m_ref[...] + jnp.log(l)
            return 0

        lax.fori_loop(0, nq, qblock, 0)

    return pl.pallas_call(
        body, name="mla_attn_fwd", grid=(B, H),
        in_specs=[pl.BlockSpec((S, MLA_PAD), lambda b, h: (b, h)),
                  pl.BlockSpec((S, MLA_PAD), lambda b, h: (b, h)),
                  pl.BlockSpec((S, MLA_V), lambda b, h: (b, h))],
        out_specs=[pl.BlockSpec((S, MLA_V), lambda b, h: (b, h)),
                   pl.BlockSpec((1, 1, S), lambda b, h: (b * H + h, 0, 0))],
        out_shape=[jax.ShapeDtypeStruct((B * S, H * MLA_V), F32), jax.ShapeDtypeStruct((B * H, 1, S), F32)],
        scratch_shapes=[pltpu.VMEM((1, blk), F32), pltpu.VMEM((1, blk), F32), pltpu.VMEM((MLA_V, blk), F32)],
        compiler_params=_params(("parallel", "parallel")),
    )(q, k, v)


def _mla_attn_bwd(q, k, v, o, do, lse, B, S):
    blk = min(MLA_FWD_BLOCK, S)
    H, nq = MLA_HEADS, S // blk

    def body(q_ref, k_ref, v_ref, o_ref, do_ref, lse_ref, dq_ref, dk_ref, dv_ref, kt_ref, dqt_ref):
        dk_ref[...] = jnp.zeros(dk_ref.shape, F32)
        dv_ref[...] = jnp.zeros(dv_ref.shape, F32)
        for g in range(nq):
            kt_ref[g] = k_ref[g * blk:(g + 1) * blk, :].T

        def qblock(i, _):
            q_rows = pl.ds(pl.multiple_of(i * blk, blk), blk)
            qi = q_ref[q_rows, :]
            doi = do_ref[q_rows, :]
            delta = jnp.sum((doi * o_ref[q_rows, :]).T, axis=0, keepdims=True)
            lse_i = lse_ref[0, :, q_rows]
            doi = doi.astype(MXU_DTYPE)
            dqt_ref[...] = jnp.zeros(dqt_ref.shape, F32)

            def keys(first, nblk, last):
                rows = pl.ds(pl.multiple_of(first * blk, blk), nblk * blk)
                k_run, v_run = k_ref[rows, :], v_ref[rows, :]
                p = jnp.exp(_dot(k_run, qi, _NT) - lse_i)
                p = jnp.where(jnp.logical_or(_chunk_visible(p.shape, nblk, blk), jnp.logical_not(last)), p, 0.0)
                ds = (p * (_dot(v_run, doi, _NT) - delta)).astype(MXU_DTYPE)
                dk_ref[rows, :] += _dot(ds, qi, _NN)
                dv_ref[rows, :] += _dot(p, doi, _NN)
                for r in range(nblk):
                    dqt_ref[...] += _dot(kt_ref[first + r], ds[r * blk:(r + 1) * blk, :], _NN)

            _key_runs(i + 1, nq, keys)
            dq_ref[q_rows, :] = dqt_ref[...].T
            return 0

        lax.fori_loop(0, nq, qblock, 0)

    qk_spec = pl.BlockSpec((S, MLA_PAD), lambda b, h: (b, h))
    v_spec = pl.BlockSpec((S, MLA_V), lambda b, h: (b, h))
    return pl.pallas_call(
        body, name="mla_attn_bwd", grid=(B, H),
        in_specs=[qk_spec, qk_spec, v_spec, v_spec, v_spec,
                  pl.BlockSpec((1, 1, S), lambda b, h: (b * H + h, 0, 0))],
        out_specs=[qk_spec, qk_spec, v_spec],
        out_shape=[jax.ShapeDtypeStruct((B * S, H * MLA_PAD), F32), jax.ShapeDtypeStruct((B * S, H * MLA_PAD), F32),
                   jax.ShapeDtypeStruct((B * S, H * MLA_V), F32)],
        scratch_shapes=[pltpu.VMEM((nq, MLA_PAD, blk), q.dtype), pltpu.VMEM((MLA_PAD, blk), F32)],
        compiler_params=_params(("parallel", "parallel")),
    )(q, k, v, o, do, lse)


def _ret_log_gamma():
    lg = np.log1p(-np.exp2(RET_GAMMA_BASE - np.arange(RET_HEADS, dtype=np.float32))).astype(np.float32)
    return jnp.asarray(np.broadcast_to(lg[:, None, None], (RET_HEADS, 8, LANES)).copy())


RET_BLOCK = 512


def _ret_local_scale(lg, shape, blk, rising):
    local = lax.broadcasted_iota(jnp.int32, shape, 0) % blk
    return jnp.exp(lg * (local if rising else blk - 1 - local).astype(F32))


def _ret_pair_factor(lg, blk, steps):
    return jnp.exp(lg * (blk * (steps - 1) + 1).astype(F32))


def _ret_own_decay(lg, blk, transposed):
    a = lax.broadcasted_iota(jnp.int32, (blk, blk), 0)
    b = lax.broadcasted_iota(jnp.int32, (blk, blk), 1)
    query, key = (b, a) if transposed else (a, b)
    dec = jnp.exp(lg * jnp.abs(query - key).astype(F32))
    return jnp.where((key // CHUNK) <= (query // CHUNK), dec, 0.0)


def _ret_attn_fwd(q, k, v, B, S):
    blk = min(RET_BLOCK, S)
    H, nq = RET_HEADS, S // blk

    def body(lg_ref, q_ref, k_ref, v_ref, o_ref, ks_ref, dec_ref, acc_ref):
        lg = lg_ref[0, 0:1, 0:1]
        ks_ref[...] = (k_ref[...].astype(F32) * _ret_local_scale(lg, k_ref.shape, blk, False)).astype(ks_ref.dtype)
        dec_ref[...] = _ret_own_decay(lg, blk, False)

        def qblock(i, _):
            q_rows = pl.ds(pl.multiple_of(i * blk, blk), blk)
            qi = q_ref[q_rows, :]
            qs = (qi.astype(F32) * _ret_local_scale(lg, qi.shape, blk, True)).astype(qi.dtype)
            a = _dot(qi, k_ref[q_rows, :], _NT) * dec_ref[...]
            acc_ref[...] = _dot(a, v_ref[q_rows, :], _NN)

            def keys(first, nblk, _):
                rows = pl.ds(pl.multiple_of(first * blk, blk), nblk * blk)
                steps = i - first - lax.broadcasted_iota(jnp.int32, (1, nblk * blk), 1) // blk
                a = _dot(qs, ks_ref[rows, :], _NT) * _ret_pair_factor(lg, blk, steps)
                acc_ref[...] += _dot(a, v_ref[rows, :], _NN)

            _earlier_runs(i, nq, keys)
            o_ref[q_rows, :] = acc_ref[...]
            return 0

        lax.fori_loop(0, nq, qblock, 0)

    qk_spec = pl.BlockSpec((S, RET_QK), lambda b, h: (b, h))
    v_spec = pl.BlockSpec((S, RET_V), lambda b, h: (b, h))
    return pl.pallas_call(
        body, name="ret_attn_fwd", grid=(B, H),
        in_specs=[pl.BlockSpec((1, 8, LANES), lambda b, h: (h, 0, 0)), qk_spec, qk_spec, v_spec],
        out_specs=v_spec,
        out_shape=jax.ShapeDtypeStruct((B * S, H * RET_V), F32),
        scratch_shapes=[pltpu.VMEM((S, RET_QK), k.dtype), pltpu.VMEM((blk, blk), F32), pltpu.VMEM((blk, RET_V), F32)],
        compiler_params=_params(("parallel", "parallel")),
    )(_ret_log_gamma(), q, k, v)


def _ret_attn_bwd(q, k, v, do, B, S):
    blk = min(RET_BLOCK, S)
    H, nq = RET_HEADS, S // blk

    def body(lg_ref, q_ref, k_ref, v_ref, do_ref, dq_ref, dk_ref, dv_ref, ks_ref, kst_ref, dks_ref, dqt_ref, dec_ref):
        lg = lg_ref[0, 0:1, 0:1]
        dk_ref[...] = jnp.zeros(dk_ref.shape, F32)
        dv_ref[...] = jnp.zeros(dv_ref.shape, F32)
        dks_ref[...] = jnp.zeros(dks_ref.shape, F32)
        ks_ref[...] = (k_ref[...].astype(F32) * _ret_local_scale(lg, k_ref.shape, blk, False)).astype(ks_ref.dtype)
        for g in range(nq):
            kst_ref[g] = ks_ref[g * blk:(g + 1) * blk, :].T
        dec_ref[...] = _ret_own_decay(lg, blk, True)

        def qblock(i, _):
            q_rows = pl.ds(pl.multiple_of(i * blk, blk), blk)
            qi = q_ref[q_rows, :]
            q_scale = _ret_local_scale(lg, qi.shape, blk, True)
            qs = (qi.astype(F32) * q_scale).astype(qi.dtype)
            doi = do_ref[q_rows, :].astype(MXU_DTYPE)
            ki = k_ref[q_rows, :]
            dec = dec_ref[...]
            a = _dot(ki, qi, _NT) * dec
            da = (_dot(v_ref[q_rows, :], doi, _NT) * dec).astype(MXU_DTYPE)
            dv_ref[q_rows, :] += _dot(a, doi, _NN)
            dk_ref[q_rows, :] += _dot(da, qi, _NN)
            dq_own = _dot(da, ki, _TN)
            dqt_ref[...] = jnp.zeros(dqt_ref.shape, F32)

            def keys(first, nblk, _):
                for r in range(nblk):
                    g = first + r
                    rows = pl.ds(pl.multiple_of(g * blk, blk), blk)
                    c = _ret_pair_factor(lg, blk, i - g)
                    a = _dot(ks_ref[rows, :], qs, _NT) * c
                    da = (_dot(v_ref[rows, :], doi, _NT) * c).astype(MXU_DTYPE)
                    dv_ref[rows, :] += _dot(a, doi, _NN)
                    dks_ref[rows, :] += _dot(da, qs, _NN)
                    dqt_ref[...] += _dot(kst_ref[g], da, _NN)

            _earlier_runs(i, nq, keys)
            dq_ref[q_rows, :] = dqt_ref[...].T * q_scale + dq_own
            return 0

        lax.fori_loop(0, nq, qblock, 0)
        dk_ref[...] += dks_ref[...] * _ret_local_scale(lg, dks_ref.shape, blk, False)

    qk_spec = pl.BlockSpec((S, RET_QK), lambda b, h: (b, h))
    v_spec = pl.BlockSpec((S, RET_V), lambda b, h: (b, h))
    return pl.pallas_call(
        body, name="ret_attn_bwd", grid=(B, H),
        in_specs=[pl.BlockSpec((1, 8, LANES), lambda b, h: (h, 0, 0)), qk_spec, qk_spec, v_spec, v_spec],
        out_specs=[qk_spec, qk_spec, v_spec],
        out_shape=[jax.ShapeDtypeStruct((B * S, H * RET_QK), F32), jax.ShapeDtypeStruct((B * S, H * RET_QK), F32),
                   jax.ShapeDtypeStruct((B * S, H * RET_V), F32)],
        scratch_shapes=[pltpu.VMEM((S, RET_QK), k.dtype), pltpu.VMEM((nq, RET_QK, blk), k.dtype),
                        pltpu.VMEM((S, RET_QK), F32), pltpu.VMEM((RET_QK, blk), F32), pltpu.VMEM((blk, blk), F32)],
        compiler_params=_params(("parallel", "parallel")),
    )(_ret_log_gamma(), q, k, v, do)


def _adamw(w, g, m, v, name):
    R, C = w.shape
    br = R if R * C * 4 <= 2 ** 21 else _pick_rows(R, max(8, (2 ** 21) // (C * 4)))

    def body(w_ref, g_ref, m_ref, v_ref, d_ref, mo_ref, vo_ref):
        g_v = g_ref[...]
        m_v = ADAM_B1 * m_ref[...] + (1.0 - ADAM_B1) * g_v
        v_v = ADAM_B2 * v_ref[...] + (1.0 - ADAM_B2) * (g_v * g_v)
        m_hat = m_v / (1.0 - ADAM_B1 ** ADAM_STEP)
        v_hat = v_v / (1.0 - ADAM_B2 ** ADAM_STEP)
        d_ref[...] = -ADAM_LR * (m_hat / (jnp.sqrt(v_hat) + ADAM_EPS) + ADAM_WD * w_ref[...])
        mo_ref[...] = m_v
        vo_ref[...] = v_v

    blk = pl.BlockSpec((br, C), lambda i: (i, 0))
    return pl.pallas_call(
        body, name=name, grid=(R // br,),
        in_specs=[blk] * 4, out_specs=[blk] * 3,
        out_shape=[jax.ShapeDtypeStruct((R, C), F32)] * 3,
        compiler_params=_params(("parallel",)),
    )(w, g, m, v)


def _pick_rows(R, target):
    best = None
    for d in range(8, min(R, target) + 1, 8):
        if R % d == 0:
            best = d
    assert best is not None, (R, target)
    return best


def _position():
    return lax.axis_index("x"), lax.axis_index("y"), lax.axis_index("c")


HBM_SPEC = pl.BlockSpec(memory_space=pltpu.HBM)


def _other_chips(x, y):
    return [(1 - x, y), (x, 1 - y), (1 - x, 1 - y)]


def _all_gather_weights(bigs, small):
    nb = len(bigs)

    def body(*refs):
        big_refs, small_ref = refs[:nb], refs[nb]
        obig, osmall = refs[nb + 1:2 * nb + 1], refs[2 * nb + 1]
        ici_send, ici_recv, d2d_send, d2d_recv, sm_send, sm_recv = refs[2 * nb + 2:]
        x, y, c = _position()
        me = 2 * x + y
        chips = _other_chips(x, y)

        def rows(n, half):
            rh = bigs[n].shape[0] // 2
            return pl.ds(half * rh, rh)

        def over_ici(n, j, slot, from_shard):
            px, py = chips[j]
            dst = obig[n].at[slot, rows(n, c)]
            return pltpu.make_async_remote_copy(
                src_ref=big_refs[n].at[rows(n, c)] if from_shard else dst, dst_ref=dst,
                send_sem=ici_send.at[3 * n + j], recv_sem=ici_recv.at[3 * n + j],
                device_id=(px, py, c), device_id_type=MESH)

        def over_d2d(n, j, half):
            px, py = chips[j]
            part = obig[n].at[2 * px + py, rows(n, half)]
            return pltpu.make_async_remote_copy(
                src_ref=part, dst_ref=part, send_sem=d2d_send.at[3 * n + j], recv_sem=d2d_recv.at[3 * n + j],
                device_id=(x, y, 1 - c), device_id_type=MESH)

        def small_copy(j, slot):
            px, py = chips[j]
            return pltpu.make_async_remote_copy(
                src_ref=small_ref, dst_ref=osmall.at[slot], send_sem=sm_send.at[j], recv_sem=sm_recv.at[j],
                device_id=(px, py, c), device_id_type=MESH)

        sends = [over_ici(n, j, me, True) for n in range(nb) for j in range(3)]
        sends += [small_copy(j, me) for j in range(3)]
        for cp in sends:
            cp.start()
        passed = []
        for n in range(nb):
            for j, (px, py) in enumerate(chips):
                over_ici(n, j, 2 * px + py, False).wait_recv()
                fwd = over_d2d(n, j, c)
                fwd.start()
                passed.append(fwd)
        for n in range(nb):
            for j in range(3):
                over_d2d(n, j, 1 - c).wait_recv()
        for j, (px, py) in enumerate(chips):
            small_copy(j, 2 * px + py).wait_recv()
        for cp in sends + passed:
            cp.wait_send()

    dma = pltpu.SemaphoreType.DMA
    return pl.pallas_call(
        body, name="weights_all_gather",
        in_specs=[HBM_SPEC] * (nb + 1), out_specs=[HBM_SPEC] * (nb + 1),
        out_shape=[jax.ShapeDtypeStruct((N_SHARD,) + b.shape, b.dtype) for b in bigs]
        + [jax.ShapeDtypeStruct((N_SHARD,) + small.shape, small.dtype)],
        scratch_shapes=[dma((3 * nb,)), dma((3 * nb,)), dma((3 * nb,)), dma((3 * nb,)), dma((3,)), dma((3,))],
    )(*bigs, small)


SEM_SPEC = pl.BlockSpec(memory_space=pltpu.SEMAPHORE)
DATAFLOW_EFFECT = pltpu.SideEffectType.DATAFLOW_SIDE_EFFECTING
N_PEERS = N_DEV - 1


def _grad_copies(p_refs, land_refs, send_sems, recv_sems):
    x, y, c = _position()
    copies = []
    for a, (p_ref, land_ref) in enumerate(zip(p_refs, land_refs)):
        rh = p_ref.shape[1] // 2
        for k in range(1, N_DEV):
            px = 1 - x if k & 4 else x
            py = 1 - y if k & 2 else y
            pc = 1 - c if k & 1 else c
            copies.append(pltpu.make_async_remote_copy(
                src_ref=p_ref.at[2 * px + py, pl.ds(pc * rh, rh)], dst_ref=land_ref.at[k - 1],
                send_sem=send_sems.at[N_PEERS * a + k - 1], recv_sem=recv_sems.at[N_PEERS * a + k - 1],
                device_id=(px, py, pc), device_id_type=MESH))
    return copies


def _weight_copies(w_refs, land_refs, send_sems, recv_sems):
    x, y, c = _position()
    copies = []
    for a, (w_ref, land_ref) in enumerate(zip(w_refs, land_refs)):
        for j, (px, py) in enumerate(_other_chips(x, y)):
            copies.append(pltpu.make_async_remote_copy(
                src_ref=w_ref, dst_ref=land_ref.at[2 * x + y], send_sem=send_sems.at[3 * a + j],
                recv_sem=recv_sems.at[3 * a + j], device_id=(px, py, c), device_id_type=MESH))
    return copies


def _exchange_start(make_copies, srcs, lands, n_sems, name, after=None):
    n, m = len(srcs), len(lands)
    n_in = n + m + (after is not None)

    def body(*refs):
        send_sems, recv_sems, token = refs[n_in], refs[n_in + 1], refs[-1]
        for cp in make_copies(refs[:n], refs[n:n + m], send_sems, recv_sems):
            cp.start()
        token[...] = jnp.zeros(token.shape, token.dtype)

    hbm = lambda a: pltpu.with_memory_space_constraint(a, pltpu.HBM)
    dma = pltpu.SemaphoreType.DMA
    res = pl.pallas_call(
        body, name=name,
        in_specs=[HBM_SPEC] * (n + m) + ([] if after is None else [pl.BlockSpec(memory_space=pl.ANY)]),
        out_specs=[SEM_SPEC, SEM_SPEC] + [HBM_SPEC] * (n + m) + [pl.BlockSpec(memory_space=pltpu.VMEM)],
        out_shape=[dma((n_sems,)), dma((n_sems,))] + [pltpu.HBM(a.shape, a.dtype) for a in list(srcs) + list(lands)]
        + [jax.ShapeDtypeStruct((8, LANES), F32)],
        input_output_aliases={i: 2 + i for i in range(n + m)},
        compiler_params=pltpu.CompilerParams(has_side_effects=DATAFLOW_EFFECT),
    )(*[hbm(a) for a in srcs], *[hbm(a) for a in lands], *(() if after is None else (after,)))
    return res[0], res[1], list(res[2:2 + n]), list(res[2 + n:2 + n + m]), res[-1]


def _exchange_wait(make_copies, send_sems, recv_sems, srcs, lands, after, name):
    n, m = len(srcs), len(lands)

    def body(*refs):
        for cp in make_copies(refs[:n], refs[n:n + m], refs[n + m], refs[n + m + 1]):
            cp.wait_send()
            cp.wait_recv()

    res = pl.pallas_call(
        body, name=name,
        in_specs=[HBM_SPEC] * (n + m) + [SEM_SPEC, SEM_SPEC, pl.BlockSpec(memory_space=pl.ANY)],
        out_specs=[HBM_SPEC] * (n + m),
        out_shape=[pltpu.HBM(a.shape, a.dtype) for a in list(srcs) + list(lands)],
        input_output_aliases={i: i for i in range(n + m)},
        compiler_params=pltpu.CompilerParams(has_side_effects=DATAFLOW_EFFECT),
    )(*srcs, *lands, send_sems, recv_sems, after)
    return list(res[:n]), list(res[n:])


def _sum_partials(p, land, name):
    _, rh, cols = land.shape
    br = _pick_rows(rh, 256)
    nrb = rh // br
    x, y, c = _position()
    where = jnp.stack([2 * x + y, c]).astype(jnp.int32)

    def body(where_ref, p_ref, land_ref, o_ref):
        acc = p_ref[...].astype(F32)
        for k in range(N_PEERS):
            acc = acc + land_ref[k].astype(F32)
        o_ref[...] = acc

    return pl.pallas_call(
        body, name=name,
        grid_spec=pltpu.PrefetchScalarGridSpec(
            num_scalar_prefetch=1, grid=(nrb,),
            in_specs=[pl.BlockSpec((None, br, cols), lambda r, where_ref: (where_ref[0], where_ref[1] * nrb + r, 0)),
                      pl.BlockSpec((N_PEERS, br, cols), lambda r, where_ref: (0, r, 0))],
            out_specs=pl.BlockSpec((None, br, cols), lambda r, where_ref: (where_ref[1], r, 0))),
        out_shape=jax.ShapeDtypeStruct((2, rh, cols), F32),
        compiler_params=_params(("parallel",)),
    )(where, p, land)


def _sibling_share(fulls, name):
    n = len(fulls)

    def body(*refs):
        o_refs = refs[n:2 * n]
        send_sems, recv_sems = refs[2 * n:]
        x, y, c = _position()

        def copy(a, half):
            return pltpu.make_async_remote_copy(
                src_ref=o_refs[a].at[half], dst_ref=o_refs[a].at[half], send_sem=send_sems.at[a],
                recv_sem=recv_sems.at[a], device_id=(x, y, 1 - c), device_id_type=MESH)

        sends = [copy(a, c) for a in range(n)]
        for cp in sends:
            cp.start()
        for a in range(n):
            copy(a, 1 - c).wait_recv()
        for cp in sends:
            cp.wait_send()

    dma = pltpu.SemaphoreType.DMA
    return pl.pallas_call(
        body, name=name,
        in_specs=[HBM_SPEC] * n, out_specs=[HBM_SPEC] * n,
        out_shape=[jax.ShapeDtypeStruct(f.shape, f.dtype) for f in fulls],
        input_output_aliases={a: a for a in range(n)},
        scratch_shapes=[dma((n,)), dma((n,))],
    )(*fulls)


def _all_reduce_small(v):
    R, cols = v.shape

    def body(v_ref, o_ref, buf_ref, send_sems, recv_sems):
        x, y, c = _position()
        me = 4 * x + 2 * y + c
        buf_ref[me] = v_ref[...]
        sends = []
        for k in range(1, N_DEV):
            px = 1 - x if k & 4 else x
            py = 1 - y if k & 2 else y
            pc = 1 - c if k & 1 else c
            sends.append(pltpu.make_async_remote_copy(
                src_ref=v_ref, dst_ref=buf_ref.at[me], send_sem=send_sems.at[k - 1], recv_sem=recv_sems.at[k - 1],
                device_id=(px, py, pc), device_id_type=MESH))
        for cp in sends:
            cp.start()
        for k in range(1, N_DEV):
            px = 1 - x if k & 4 else x
            py = 1 - y if k & 2 else y
            pc = 1 - c if k & 1 else c
            pltpu.make_async_remote_copy(
                src_ref=v_ref, dst_ref=buf_ref.at[4 * px + 2 * py + pc], send_sem=send_sems.at[k - 1],
                recv_sem=recv_sems.at[k - 1], device_id=(px, py, pc), device_id_type=MESH).wait_recv()
        for cp in sends:
            cp.wait_send()
        acc = buf_ref[0]
        for d in range(1, N_DEV):
            acc = acc + buf_ref[d]
        o_ref[...] = acc

    return pl.pallas_call(
        body, name="small_grads_all_reduce",
        in_specs=[pl.BlockSpec(memory_space=pltpu.VMEM)], out_specs=pl.BlockSpec(memory_space=pltpu.VMEM),
        out_shape=jax.ShapeDtypeStruct((R, cols), F32),
        scratch_shapes=[pltpu.VMEM((N_DEV, R, cols), F32), pltpu.SemaphoreType.DMA((N_DEV - 1,)),
                        pltpu.SemaphoreType.DMA((N_DEV - 1,))],
    )(v)


def _rope_tables(S, half, width):
    inv_freq = ROPE_THETA ** (-jnp.arange(half, dtype=F32) / half)
    ang = jnp.arange(S).astype(F32)[:, None] * inv_freq[None, :]
    return jnp.cos(ang), jnp.sin(ang)


def _slot_rows(a):
    return a.reshape(N_SHARD, -1, a.shape[-1])


def _local_step(x, target, w, B, S, late, exchange, reduce_small):
    T = B * S
    D = D_MODEL
    bm = min(512, S)
    full = lambda a, wd, tile=None: (a, wd, 0, tile or wd)
    g = {}

    cos_r, sin_r = _rope_tables(S, RET_QK // 2, LANES)
    cos_m, sin_m = _rope_tables(S, MLA_ROPE // 2, LANES)
    zeros64 = jnp.zeros((S, 64), F32)
    cos_m = jnp.concatenate([cos_m, cos_m, zeros64], axis=1)
    sin_m = jnp.concatenate([-sin_m, sin_m, zeros64], axis=1)

    def ffn_fwd(xin, h, ht, i, next_gain):
        w.update(late(f"ffn{i}", xin))
        norm = w["ffn_norm"][i:i + 1]
        ag = _mm(h, w[f"ffn_w_in{i}"], "nn", BF16, f"ffn{i}_in", bm=1024, bn=1408, cols_outer=True)
        u, ut = _conv_fwd(ag, w["ffn_conv8"][i], B, S, f"ffn{i}_conv")
        if next_gain is None:
            out = _mm_out_loss(u, w[f"ffn_w_out{i}"], xin, target, f"ffn{i}_out")
        else:
            out = _mm_out_norm(u, w[f"ffn_w_out{i}"], xin, next_gain, f"ffn{i}_out")
        return out, (xin, norm, ht, ag, ut)

    def ffn_bwd(dxout, dxout_c, saved, i):
        xin, norm, ht, ag, ut = saved
        du = _mm(dxout_c, w[f"ffn_w_out{i}"], "nt", F32, f"ffn{i}_out_dx", bm=1024, bn=1408, cols_outer=True)
        g_w_out = _mm(ut, dxout_c, "nn", BF16, f"ffn{i}_out_dw", bm=1408, bn=512, bk=T)
        da, dg, dw8 = _conv_bwd(ag, w["ffn_conv8"][i], du, B, S, f"ffn{i}_conv_bwd")
        g_w_in = _mm(ht, [da, dg], "nn", BF16, f"ffn{i}_in_dw", bm=1024, bn=1408, bk=T // 2, out_slots=N_SHARD)
        token = exchange(f"ffn{i}", [g_w_in, _slot_rows(g_w_out)])
        dxin, dxin_c, g_norm = _mm_dx_norm([da, dg], w[f"ffn_w_in{i}"], xin, norm, dxout, f"ffn{i}_in_dx", after=token)
        return dxin, dxin_c, (g_norm, dw8)

    h0, h0t = _rowwise_fwd(_fn_rms, "ret_norm", [full(x, D)], [], [(w["ret_norm"], D)], [(D, D, BF16)], bm, S,
                           transposed=(0,))
    proj = _mm(h0, w["ret_w_in"], "nn", BF16, "ret_in", bm=1024, after=w["started"], cols_outer=True)
    HQ, HV = RET_HEADS * RET_QK, RET_HEADS * RET_V
    rope_rows = [(proj, 2 * HQ + HV, 0, LANES)]
    q_r, k_r, v_r = _rowwise_fwd(_fn_ret_rope, "ret_rope", rope_rows, [cos_r, sin_r], [],
                                 [(HQ, LANES, BF16), (HQ, LANES, BF16), (HV, LANES, BF16)], bm, S)
    ret_o = _ret_attn_fwd(q_r, k_r, v_r, B, S)
    gate_rows = [full(ret_o, HV, RET_V), (proj, HV, 2, RET_V)]
    y0, y0t = _rowwise_fwd(_fn_ret_gate, "ret_gate", gate_rows, [], [(w["ret_gn"], RET_V)], [(HV, RET_V, BF16)], bm, S,
                           transposed=(0,))
    w.update(late("ret_out", y0))
    x1, h1, h1t = _mm_out_norm(y0, w["ret_w_out"], x, w["ffn_norm"][0:1], "ret_out")
    (x2, h2, _), ffn0_saved = ffn_fwd(x1, h1, h1t, 0, w["mla_norm"])

    w.update(late("mla", x2))
    proj2 = _mm(h2, w["mla_w_in"], "nn", F32, "mla_in", bm=2048)
    lat_consts = [(w["mla_q_norm"], LANES), (w["mla_kv_norm"], LANES)]
    cqn, ckvn, kr = _rowwise_fwd(_fn_mla_lat, "mla_latent_norm", [full(proj2, MLA_IN_PAD, LANES)], [], lat_consts,
                                 [(MLA_Q_RANK, LANES, BF16), (MLA_KV_RANK, LANES, BF16), (LANES, LANES, F32)], bm, S)
    qf = _mm(cqn, w["mla_w_qb"], "nn", BF16, "mla_qb", bm=2048, bn=2048)
    kvf = _mm(ckvn, w["mla_w_kvb"], "nn", BF16, "mla_kvb", bm=2048, bn=2048)
    HP, HVm = MLA_HEADS * MLA_PAD, MLA_HEADS * MLA_V
    head_rows = [full(qf, HP, LANES), full(kvf, HP, LANES), full(kr, LANES)]
    head_consts = [(w["mla_q_head_norm"], LANES), (w["mla_k_head_norm"], LANES)]
    q_a, k_a, v_a = _rowwise_fwd(_fn_mla_heads, "mla_heads", head_rows, [cos_m, sin_m], head_consts,
                                 [(HP, LANES, BF16), (HP, LANES, BF16), (HVm, LANES, BF16)], bm, S)
    att_o, lse = _mla_attn_fwd(q_a, k_a, v_a, B, S)
    x3, h3, h3t = _mm_out_norm(att_o, w["mla_w_out"], x2, w["ffn_norm"][1:2], "mla_out")
    (dy, dy_c, loss), ffn1_saved = ffn_fwd(x3, h3, h3t, 1, None)

    dx3, dx3_c, (g_n1, dw8_1) = ffn_bwd(dy, dy_c, ffn1_saved, 1)

    d_att_o = _mm(dx3_c, w["mla_w_out"], "nt", F32, "mla_out_dx", bm=2048)
    g_mla_out = _mm(att_o, dx3_c, "tn", BF16, "mla_out_dw")
    dq_a, dk_a, dv_a = _mla_attn_bwd(q_a, k_a, v_a, att_o, d_att_o, lse, B, S)
    (dqf, dkvf, dkr), (g["mla_q_head_norm"], g["mla_k_head_norm"]) = _rowwise_bwd(
        _fn_mla_heads, "mla_heads_bwd", head_rows, [cos_m, sin_m], head_consts,
        [(dq_a, LANES), (dk_a, LANES), (dv_a, LANES)], bm, S, grad_dtypes=[BF16, BF16, F32])
    dcqn = _mm(dqf, w["mla_w_qb"], "nt", F32, "mla_qb_dx", bm=2048)
    g_qb = _mm(cqn, dqf, "tn", BF16, "mla_qb_dw")
    g_qb = _to_slots(_unpad_heads(g_qb, 1), 1).reshape(N_SHARD, MLA_Q_RANK, -1)
    dckvn = _mm(dkvf, w["mla_w_kvb"], "nt", F32, "mla_kvb_dx", bm=2048)
    g_kvb = _mm(ckvn, dkvf, "tn", BF16, "mla_kvb_dw", bn=512, out_slots=N_SHARD)
    (dproj2,), (g["mla_q_norm"], g["mla_kv_norm"]) = _rowwise_bwd(
        _fn_mla_lat, "mla_latent_norm_bwd", [full(proj2, MLA_IN_PAD, LANES)], [], lat_consts,
        [(dcqn, LANES), (dckvn, LANES), (dkr, LANES)], bm, S, grad_dtypes=[BF16])
    g_mla_in = _mm(h2, dproj2, "tn", BF16, "mla_in_dw")
    token = exchange("mla", [_slot_rows(g_mla_in[:, :MLA_IN]), g_qb, g_kvb, _slot_rows(g_mla_out)])
    dx2, dx2_c, g["mla_norm"] = _mm_dx_norm([dproj2], w["mla_w_in"], x2, w["mla_norm"], dx3, "mla_in_dx", bm=512,
                                            after=token)

    dx1, dx1_c, (g_n0, dw8_0) = ffn_bwd(dx2, dx2_c, ffn0_saved, 0)

    dy0 = _mm(dx1_c, w["ret_w_out"], "nt", F32, "ret_out_dx", bm=1024, cols_outer=True)
    g_ret_out = _mm(y0t, dx1_c, "nn", BF16, "ret_out_dw", bm=1024, bn=512, bk=T)
    token = exchange("reto", [_slot_rows(g_ret_out)])
    gn_behind = w["ret_gn"] + token[0:1, 0:1]
    (d_ret_o, dgate), (g["ret_gn"],) = _rowwise_bwd(_fn_ret_gate, "ret_gate_bwd", gate_rows, [], [(gn_behind, RET_V)],
                                                    [(dy0, RET_V)], bm, S, grad_dtypes=[F32, BF16])
    dq_r, dk_r, dv_r = _ret_attn_bwd(q_r, k_r, v_r, d_ret_o, B, S)
    (dqkv,), _ = _rowwise_bwd(_fn_ret_rope, "ret_rope_bwd", rope_rows, [cos_r, sin_r], [],
                              [(dq_r, LANES), (dk_r, LANES), (dv_r, LANES)], bm, S, grad_dtypes=[BF16], linear=True)
    dx, _, g["ret_norm"] = _mm_dx_norm([dqkv, dgate], w["ret_w_in"], x, w["ret_norm"], dx1, "ret_in_dx")
    g["ffn_norm"] = jnp.concatenate([g_n0, g_n1], axis=0)
    g["ffn_conv_w"] = jnp.stack([dw8_0[0:3], dw8_1[0:3]])
    g["ffn_conv_b"] = jnp.stack([dw8_0[3], dw8_1[3]])
    reduced_small = reduce_small(g)
    g_ret_in = _mm(h0t, [dqkv, dgate], "nn", BF16, "ret_in_dw", bm=1024, bn=512, bk=T, out_slots=N_SHARD,
                   after=reduced_small)
    exchange("ret", [g_ret_in])
    return loss, dx, reduced_small


_SMALL_SHARDED = [("ret_gn", 2), ("mla_norm", 1), ("mla_q_norm", 1), ("mla_kv_norm", 1), ("ffn_conv_w", 2)]
_SMALL_REPLICATED = ["ret_norm", "mla_q_head_norm", "mla_k_head_norm", "ffn_norm", "ffn_conv_b"]
_SMALL_ALL = ["ret_norm", "ret_gn", "mla_norm", "mla_q_norm", "mla_kv_norm", "mla_q_head_norm", "mla_k_head_norm",
              "ffn_norm", "ffn_conv_w", "ffn_conv_b"]


def _to_slots(full, axis):
    shape = full.shape
    split = shape[:axis] + (N_SHARD, shape[axis] // N_SHARD) + shape[axis + 1:]
    return jnp.moveaxis(full.reshape(split), axis, 0).reshape(N_SHARD, -1)


def _from_slots(slots, shard_shape, axis):
    parts = jnp.moveaxis(slots.reshape((N_SHARD,) + tuple(shard_shape)), 0, axis)
    full = shard_shape[:axis] + (N_SHARD * shard_shape[axis],) + shard_shape[axis + 1:]
    return parts.reshape(full)


def _pad_rows(flat, cols, row_unit):
    n, L = flat.shape
    unit = cols * row_unit
    Lp = -(-L // unit) * unit
    if Lp != L:
        flat = jnp.concatenate([flat, jnp.zeros((n, Lp - L), flat.dtype)], axis=1)
    return flat.reshape(n, Lp // cols, cols)


def _pad_heads(a, axis):
    shape = a.shape
    a = a.reshape(shape[:axis] + (MLA_HEADS, MLA_QK) + shape[axis + 1:])
    pad = [(0, 0)] * a.ndim
    pad[axis + 1] = (0, MLA_PAD - MLA_QK)
    return jnp.pad(a, pad).reshape(shape[:axis] + (MLA_HEADS * MLA_PAD,) + shape[axis + 1:])


def _unpad_heads(a, axis):
    shape = a.shape
    a = a.reshape(shape[:axis] + (MLA_HEADS, MLA_PAD) + shape[axis + 1:])
    a = lax.slice_in_dim(a, 0, MLA_QK, axis=axis + 1)
    return a.reshape(shape[:axis] + (MLA_HEADS * MLA_QK,) + shape[axis + 1:])


def kernel(x, ret_norm, ret_w_in, ret_gn, ret_w_out, mla_norm, mla_w_in, mla_q_norm, mla_w_qb, mla_kv_norm, mla_w_kvb, mla_q_head_norm, mla_k_head_norm, mla_w_out, ffn_norm, ffn_w_in, ffn_conv_w, ffn_conv_b, ffn_w_out, loss_target, m_ret_norm, m_ret_w_in, m_ret_gn, m_ret_w_out, m_mla_norm, m_mla_w_in, m_mla_q_norm, m_mla_w_qb, m_mla_kv_norm, m_mla_w_kvb, m_mla_q_head_norm, m_mla_k_head_norm, m_mla_w_out, m_ffn_norm, m_ffn_w_in, m_ffn_conv_w, m_ffn_conv_b, m_ffn_w_out, v_ret_norm, v_ret_w_in, v_ret_gn, v_ret_w_out, v_mla_norm, v_mla_w_in, v_mla_q_norm, v_mla_w_qb, v_mla_kv_norm, v_mla_w_kvb, v_mla_q_head_norm, v_mla_k_head_norm, v_mla_w_out, v_ffn_norm, v_ffn_w_in, v_ffn_conv_w, v_ffn_conv_b, v_ffn_w_out):
    names = ["ret_norm", "ret_w_in", "ret_gn", "ret_w_out", "mla_norm", "mla_w_in", "mla_q_norm", "mla_w_qb",
             "mla_kv_norm", "mla_w_kvb", "mla_q_head_norm", "mla_k_head_norm", "mla_w_out", "ffn_norm", "ffn_w_in",
             "ffn_conv_w", "ffn_conv_b", "ffn_w_out"]
    shard = dict(zip(names, [ret_norm, ret_w_in, ret_gn, ret_w_out, mla_norm, mla_w_in, mla_q_norm, mla_w_qb,
                             mla_kv_norm, mla_w_kvb, mla_q_head_norm, mla_k_head_norm, mla_w_out, ffn_norm, ffn_w_in,
                             ffn_conv_w, ffn_conv_b, ffn_w_out]))
    mom_m = dict(zip(names, [m_ret_norm, m_ret_w_in, m_ret_gn, m_ret_w_out, m_mla_norm, m_mla_w_in, m_mla_q_norm,
                             m_mla_w_qb, m_mla_kv_norm, m_mla_w_kvb, m_mla_q_head_norm, m_mla_k_head_norm, m_mla_w_out,
                             m_ffn_norm, m_ffn_w_in, m_ffn_conv_w, m_ffn_conv_b, m_ffn_w_out]))
    mom_v = dict(zip(names, [v_ret_norm, v_ret_w_in, v_ret_gn, v_ret_w_out, v_mla_norm, v_mla_w_in, v_mla_q_norm,
                             v_mla_w_qb, v_mla_kv_norm, v_mla_w_kvb, v_mla_q_head_norm, v_mla_k_head_norm, v_mla_w_out,
                             v_ffn_norm, v_ffn_w_in, v_ffn_conv_w, v_ffn_conv_b, v_ffn_w_out]))
    B, S, D = x.shape
    T = B * S
    sx, sy = lax.axis_index("x"), lax.axis_index("y")
    me = 2 * sx + sy

    two_d = lambda a: a.reshape(-1, a.shape[-1])
    small_sizes = [int(np.prod(shard[n].shape)) for n, _ in _SMALL_SHARDED]
    small = jnp.concatenate([shard[n].reshape(1, -1) for n, _ in _SMALL_SHARDED], axis=1)
    small = _pad_rows(small, LANES, 8)[0]
    as_mxu = lambda a: two_d(a).astype(BF16)
    is_me = lax.broadcasted_iota(jnp.int32, (N_SHARD, 1, 1), 0) == me
    with_own = lambda gathered, own: jnp.where(is_me, own[None], gathered)
    by_cols = lambda a: jnp.moveaxis(a, 0, 1).reshape(a.shape[1], -1)
    by_rows = lambda a: a.reshape(-1, a.shape[-1])
    pad_in = lambda a: jnp.pad(by_rows(a), ((0, 0), (0, MLA_IN_PAD - MLA_IN)))
    pad_qb = lambda a: _pad_heads(by_cols(a), 1)
    ret_in_shard = as_mxu(shard["ret_w_in"])
    g_ret_in, gsmall = _all_gather_weights([ret_in_shard], small)
    later = [
        ("ret_out", [("ret_w_out", as_mxu(shard["ret_w_out"]), by_rows)]),
        ("ffn0", [("ffn_w_in0", as_mxu(shard["ffn_w_in"][0]), by_cols), ("ffn_w_out0", as_mxu(shard["ffn_w_out"][0]), by_rows)]),
        ("mla", [("mla_w_in", as_mxu(shard["mla_w_in"]), pad_in), ("mla_w_qb", as_mxu(shard["mla_w_qb"]), pad_qb),
                 ("mla_w_kvb", as_mxu(shard["mla_w_kvb"]), by_cols), ("mla_w_out", as_mxu(shard["mla_w_out"]), by_rows)]),
        ("ffn1", [("ffn_w_in1", as_mxu(shard["ffn_w_in"][1]), by_cols), ("ffn_w_out1", as_mxu(shard["ffn_w_out"][1]), by_rows)]),
    ]
    gathering = {}
    token = gsmall
    for group, items in later:
        shards = [s_ for _, s_, _ in items]
        lands = [lax.empty((N_SHARD,) + s_.shape, s_.dtype) for s_ in shards]
        send_sems, recv_sems, shards, lands, token = _exchange_start(
            _weight_copies, shards, lands, 3 * len(shards), f"weights_start_{group}", after=token)
        gathering[group] = (send_sems, recv_sems, shards, lands, items)

    def late(group, after):
        send_sems, recv_sems, shards, lands, items = gathering[group]
        shards, lands = _exchange_wait(_weight_copies, send_sems, recv_sems, shards, lands, after,
                                       f"weights_wait_{group}")
        return {key: full(with_own(l_, s_)) for (key, _, full), s_, l_ in zip(items, shards, lands)}

    gsmall = with_own(gsmall, small).reshape(N_SHARD, -1)
    wfull = {}
    off = 0
    for (n, ax), sz in zip(_SMALL_SHARDED, small_sizes):
        wfull[n] = _from_slots(gsmall[:, off:off + sz], shard[n].shape, ax)
        off += sz
    for n in _SMALL_REPLICATED:
        wfull[n] = shard[n]

    conv8 = jnp.concatenate([wfull["ffn_conv_w"], wfull["ffn_conv_b"][:, None, :],
                             jnp.zeros((2, 4, FFN_DIM), F32)], axis=1)
    w = {
        "started": token, "ret_norm": wfull["ret_norm"], "ret_w_in": by_cols(with_own(g_ret_in, ret_in_shard)),
        "ret_gn": wfull["ret_gn"].reshape(1, RET_HEADS * RET_V), "mla_norm": wfull["mla_norm"],
        "mla_q_norm": wfull["mla_q_norm"], "mla_kv_norm": wfull["mla_kv_norm"],
        "mla_q_head_norm": jnp.pad(wfull["mla_q_head_norm"], ((0, 0), (0, MLA_PAD - MLA_QK))),
        "mla_k_head_norm": jnp.pad(wfull["mla_k_head_norm"], ((0, 0), (0, MLA_PAD - MLA_QK))),
        "ffn_norm": wfull["ffn_norm"], "ffn_conv8": conv8,
    }

    started = {}

    def exchange(group, arrays):
        lands = [lax.empty((N_PEERS, p.shape[1] // 2, p.shape[2]), p.dtype) for p in arrays]
        send_sems, recv_sems, ps, lands, token = _exchange_start(
            _grad_copies, arrays, lands, N_PEERS * len(arrays), f"grads_start_{group}")
        started[group] = (send_sems, recv_sems, ps, lands)
        return token

    small_shapes = {
        "ret_norm": (1, D_MODEL), "ret_gn": (1, RET_HEADS, RET_V), "mla_norm": (1, D_MODEL),
        "mla_q_norm": (1, MLA_Q_RANK), "mla_kv_norm": (1, MLA_KV_RANK), "mla_q_head_norm": (1, MLA_QK),
        "mla_k_head_norm": (1, MLA_QK), "ffn_norm": (2, D_MODEL), "ffn_conv_w": (2, 3, FFN_DIM),
        "ffn_conv_b": (2, FFN_DIM)}

    def reduce_small(gl):
        gl = dict(gl, mla_q_head_norm=gl["mla_q_head_norm"][:, :MLA_QK], mla_k_head_norm=gl["mla_k_head_norm"][:, :MLA_QK])
        packed = jnp.concatenate([gl[n].reshape(1, -1) for n in _SMALL_ALL], axis=1)
        return _all_reduce_small(_pad_rows(packed, LANES, 8)[0])

    loss_part, dx, gsm = _local_step(x.reshape(T, D), loss_target.reshape(T, D), w, B, S, late, exchange,
                                     reduce_small)
    loss = lax.psum(loss_part, ("x", "y", "c"))

    delta, new_m, new_v, grads = {}, {}, {}, {}

    def reduced(group, after):
        send_sems, recv_sems, ps, lands = started[group]
        ps, lands = _exchange_wait(_grad_copies, send_sems, recv_sems, ps, lands, after, f"grads_wait_{group}")
        halves = [_sum_partials(p_, l_, f"grads_sum_{group}_{i}") for i, (p_, l_) in enumerate(zip(ps, lands))]
        return [two_d(r) for r in _sibling_share(halves, f"grads_share_{group}")]

    def adamw(n, g_):
        shp = shard[n].shape
        grads[n] = g_.reshape(shp)
        flat = lambda a: a.reshape(-1, shp[-1])
        d_, m_, v_ = _adamw(flat(shard[n]), flat(grads[n]), flat(mom_m[n]), flat(mom_v[n]), f"adamw_{n}")
        delta[n], new_m[n], new_v[n] = d_.reshape(shp), m_.reshape(shp), v_.reshape(shp)
        return d_

    ffn1 = reduced("ffn1", started["ret"][2][0])
    mla = reduced("mla", ffn1[0])
    ffn0 = reduced("ffn0", mla[0])
    reto = reduced("reto", ffn0[0])
    early = [adamw(n, g_) for n, g_ in zip(["mla_w_in", "mla_w_qb", "mla_w_kvb", "mla_w_out"], mla)]
    early.append(adamw("ffn_w_in", jnp.stack([ffn0[0], ffn1[0]])))
    early.append(adamw("ffn_w_out", jnp.stack([ffn0[1], ffn1[1]])))
    early.append(adamw("ret_w_out", reto[0]))
    ret = reduced("ret", jnp.stack([d_[0, 0] for d_ in early]))
    adamw("ret_w_in", ret[0])

    gsm = gsm.reshape(-1)
    sharded_axis = dict(_SMALL_SHARDED)
    off = 0
    for n in _SMALL_ALL:
        sz = int(np.prod(small_shapes[n]))
        gn = gsm[off:off + sz].reshape(small_shapes[n])
        off += sz
        if n in sharded_axis:
            ax = sharded_axis[n]
            width = shard[n].shape[ax]
            gn = lax.dynamic_slice_in_dim(gn, me * width, width, axis=ax)
        grads[n] = gn

    pack_small = lambda d: _pad_rows(jnp.concatenate([d[n].reshape(1, -1) for n in _SMALL_ALL], axis=1), LANES, 8)[0]
    d_, m_, v_ = _adamw(pack_small(shard), pack_small(grads), pack_small(mom_m), pack_small(mom_v), "adamw_small")
    off = 0
    for n in _SMALL_ALL:
        sz = int(np.prod(shard[n].shape))
        for dst, src in ((delta, d_), (new_m, m_), (new_v, v_)):
            dst[n] = src.reshape(-1)[off:off + sz].reshape(shard[n].shape)
        off += sz

    return (loss, dx.reshape(B, S, D), *[grads[n] for n in names], *[delta[n] for n in names],
            *[new_m[n] for n in names], *[new_v[n] for n in names])
```

```python
import functools

import numpy as np
import jax
import jax.numpy as jnp
from jax import lax
from jax.experimental import pallas as pl
from jax.experimental.pallas import tpu as pltpu

F32 = jnp.float32
BF16 = jnp.bfloat16
MXU_DTYPE = jnp.bfloat16

CHUNK = 64
RMS_EPS = 1e-6
ROPE_THETA = 10000.0
D_MODEL = 1024
RET_HEADS = 4
RET_QK = 256
RET_V = 512
RET_GAMMA_BASE = -5.0
MLA_HEADS = 8
MLA_Q_RANK = 384
MLA_KV_RANK = 256
MLA_NOPE = 128
MLA_ROPE = 64
MLA_V = 128
MLA_QK = MLA_NOPE + MLA_ROPE
MLA_PAD = 256
MLA_IN = MLA_Q_RANK + MLA_KV_RANK + MLA_ROPE
MLA_IN_PAD = MLA_IN + 64
MASK_VALUE = -1e30
FFN_DIM = 2816
ADAM_LR = 0.001
ADAM_B1 = 0.9
ADAM_B2 = 0.999
ADAM_EPS = 1e-08
ADAM_WD = 0.01
ADAM_STEP = 10

LANES = 128
MLA_FWD_BLOCK = 512
VMEM_LIMIT = 56 * 2 ** 20
N_SHARD = 4
N_DEV = 8

MESH = pl.DeviceIdType.MESH


def _params(sem=None, **kw):
    return pltpu.CompilerParams(dimension_semantics=sem, vmem_limit_bytes=VMEM_LIMIT, **kw)


def _pick(dim, target):
    if dim <= target:
        return dim
    best = None
    for d in range(LANES, target + 1, LANES):
        if dim % d == 0:
            best = d
    assert best is not None, (dim, target)
    return best


def _mm(a, b, dims, out_dtype, name, residual=None, bm=512, bn=1024, bk=2048, out_slots=None, after=None,
        cols_outer=False):
    a_parts = list(a) if isinstance(a, (list, tuple)) else [a]
    b_parts = list(b) if isinstance(b, (list, tuple)) else [b]
    parts_on_n = dims == "tn" or len(b_parts) > 1
    if parts_on_n:
        assert len(a_parts) == 1 and dims in ("tn", "nn")
        (K, M) = a_parts[0].shape if dims == "tn" else a_parts[0].shape[::-1]
        N = sum(p.shape[1] for p in b_parts)
        part_widths = [p.shape[1] for p in b_parts]
    else:
        assert len(b_parts) == 1
        M = a_parts[0].shape[0]
        K = sum(p.shape[1] for p in a_parts)
        N = b_parts[0].shape[1 if dims == "nn" else 0]
        part_widths = [p.shape[1] for p in a_parts]
    bm, bn, bk = _pick(M, bm), _pick(N, bn), _pick(K, min(bk, 1024) if dims == "tn" else bk)
    nk = K // bk
    unit = bn if parts_on_n else bk
    assert all(wd % unit == 0 for wd in part_widths), (name, part_widths, unit)
    bounds = np.cumsum([0] + [wd // unit for wd in part_widths])
    ranges = [(int(lo), int(hi)) for lo, hi in zip(bounds[:-1], bounds[1:])]

    def part_index(idx, lo, hi):
        return jnp.clip(idx - lo, 0, hi - lo - 1)

    if parts_on_n:
        if dims == "tn":
            a_specs = [pl.BlockSpec((bk, bm), lambda i, j, k: (k, i))]
            dn = (((0,), (0,)), ((), ()))
        else:
            a_specs = [pl.BlockSpec((bm, bk), lambda i, j, k: (i, k))]
            dn = (((1,), (0,)), ((), ()))
        b_specs = [pl.BlockSpec((bk, bn), functools.partial(lambda i, j, k, lo, hi: (k, part_index(j, lo, hi)), lo=lo, hi=hi))
                   for lo, hi in ranges]
    else:
        a_specs = [pl.BlockSpec((bm, bk), functools.partial(lambda i, j, k, lo, hi: (i, part_index(k, lo, hi)), lo=lo, hi=hi))
                   for lo, hi in ranges]
        if dims == "nt":
            b_specs = [pl.BlockSpec((bn, bk), lambda i, j, k: (j, k))]
        else:
            b_specs = [pl.BlockSpec((bk, bn), lambda i, j, k: (k, j))]
        dn = (((1,), (1 if dims == "nt" else 0,)), ((), ()))
    r_spec = pl.BlockSpec((bm, bn), lambda i, j, k: (i, j))
    if out_slots is None:
        o_spec, o_shape = r_spec, (M, N)
    else:
        ns = N // out_slots
        assert ns % bn == 0, (name, ns, bn)
        nbs = ns // bn
        o_spec = pl.BlockSpec((None, bm, bn), lambda i, j, k: (j // nbs, i, j % nbs))
        o_shape = (out_slots, M, ns)
    has_res = residual is not None
    na, nb = len(a_parts), len(b_parts)

    def body(*refs):
        a_refs, b_refs = refs[:na], refs[na:na + nb]
        r_ref = refs[na + nb] if has_res else None
        n_in = na + nb + has_res + (after is not None)
        o_ref = refs[n_in]
        acc_ref = refs[n_in + 1] if nk > 1 else None
        k = pl.program_id(2)

        def finish(acc):
            if has_res:
                acc = acc + r_ref[...].astype(F32)
            o_ref[...] = acc.astype(out_dtype)

        def compute(a_ref, b_ref):
            p = lax.dot_general(a_ref[...].astype(MXU_DTYPE), b_ref[...].astype(MXU_DTYPE), dn,
                                preferred_element_type=F32)
            if nk == 1:
                finish(p)
                return

            @pl.when(k == 0)
            def _():
                acc_ref[...] = p

            @pl.when(jnp.logical_and(k > 0, k < nk - 1))
            def _():
                acc_ref[...] += p

            @pl.when(k == nk - 1)
            def _():
                finish(acc_ref[...] + p)

        if len(ranges) == 1:
            compute(a_refs[0], b_refs[0])
        else:
            idx = pl.program_id(0 if cols_outer else 1) if parts_on_n else k
            for p, (lo, hi) in enumerate(ranges):
                @pl.when(jnp.logical_and(idx >= lo, idx < hi))
                def _(p=p):
                    compute(a_refs[0 if parts_on_n else p], b_refs[p if parts_on_n else 0])

    after_specs = [] if after is None else [pl.BlockSpec(after.shape, lambda i, j, k: (0, 0))]
    in_specs = a_specs + b_specs + ([r_spec] if has_res else []) + after_specs
    grid = (M // bm, N // bn, nk)
    if cols_outer:
        swap = lambda sp: pl.BlockSpec(sp.block_shape, functools.partial(lambda j, i, k, f: f(i, j, k), f=sp.index_map))
        in_specs, o_spec, grid = [swap(sp) for sp in in_specs], swap(o_spec), (grid[1], grid[0], nk)
    return pl.pallas_call(
        body, name=name, grid=grid,
        in_specs=in_specs, out_specs=o_spec,
        out_shape=jax.ShapeDtypeStruct(o_shape, out_dtype),
        scratch_shapes=[pltpu.VMEM((bm, bn), F32)] if nk > 1 else [],
        compiler_params=_params(("parallel", "parallel", "arbitrary")),
    )(*a_parts, *b_parts, *((residual,) if has_res else ()), *(() if after is None else (after,)))


def _mm_out_norm(a, w, residual, gain, name, bm=512):
    (M, K), N = a.shape, w.shape[1]
    bm = _pick(M, bm)

    def body(a_ref, w_ref, r_ref, g_ref, o_ref, h_ref, ht_ref):
        acc = lax.dot_general(a_ref[...].astype(MXU_DTYPE), w_ref[...].astype(MXU_DTYPE), _NN,
                              preferred_element_type=F32) + r_ref[...]
        o_ref[...] = acc
        hv = _fn_rms([[acc]], [], [[g_ref[...]]])[0][0]
        h_ref[...] = hv.astype(h_ref.dtype)
        ht_ref[...] = hv.T.astype(ht_ref.dtype)

    row = pl.BlockSpec((bm, N), lambda i: (i, 0))
    whole = lambda arr: pl.BlockSpec(arr.shape, lambda i: (0, 0))
    return pl.pallas_call(
        body, name=name, grid=(M // bm,),
        in_specs=[pl.BlockSpec((bm, K), lambda i: (i, 0)), whole(w), row, whole(gain)],
        out_specs=[row, row, pl.BlockSpec((N, bm), lambda i: (0, i))],
        out_shape=[jax.ShapeDtypeStruct((M, N), F32), jax.ShapeDtypeStruct((M, N), BF16),
                   jax.ShapeDtypeStruct((N, M), BF16)],
        compiler_params=_params(("parallel",)),
    )(a, w, residual, gain)


def _mm_out_loss(a, w, residual, target, name, bm=512):
    (M, K), N = a.shape, w.shape[1]
    bm = _pick(M, bm)

    def body(a_ref, w_ref, r_ref, t_ref, dy_ref, dyc_ref, l_ref):
        y = lax.dot_general(a_ref[...].astype(MXU_DTYPE), w_ref[...].astype(MXU_DTYPE), _NN,
                            preferred_element_type=F32) + r_ref[...]
        err = y - t_ref[...]
        dy_ref[...] = err / N
        dyc_ref[...] = (err / N).astype(dyc_ref.dtype)
        part = jnp.full((8, LANES), 0.5 * jnp.sum(jnp.mean(err * err, axis=-1)), F32)

        @pl.when(pl.program_id(0) == 0)
        def _():
            l_ref[...] = part

        @pl.when(pl.program_id(0) > 0)
        def _():
            l_ref[...] += part

    row = pl.BlockSpec((bm, N), lambda i: (i, 0))
    dy, dyc, l = pl.pallas_call(
        body, name=name, grid=(M // bm,),
        in_specs=[pl.BlockSpec((bm, K), lambda i: (i, 0)), pl.BlockSpec(w.shape, lambda i: (0, 0)), row, row],
        out_specs=[row, row, pl.BlockSpec((8, LANES), lambda i: (0, 0))],
        out_shape=[jax.ShapeDtypeStruct((M, N), F32), jax.ShapeDtypeStruct((M, N), BF16),
                   jax.ShapeDtypeStruct((8, LANES), F32)],
        compiler_params=_params(("arbitrary",)),
    )(a, w, residual, target)
    return dy, dyc, l[0, 0]


def _mm_dx_norm(a_parts, w, x, gain, add, name, bm=256, after=None):
    M = a_parts[0].shape[0]
    N, K = w.shape
    widths = [p.shape[1] for p in a_parts]
    assert sum(widths) == K, (name, widths, K)
    offs = [int(o) for o in np.cumsum([0] + widths[:-1])]
    bm = _pick(M, bm)
    na = len(a_parts)
    n_in = na + 4 + (after is not None)

    def body(*refs):
        w_ref, x_ref, g_ref, add_ref = refs[na:na + 4]
        dx_ref, dxc_ref, dg_ref = refs[n_in:n_in + 3]
        dh = None
        for a_ref, off, wd in zip(refs[:na], offs, widths):
            p = lax.dot_general(a_ref[...].astype(MXU_DTYPE), w_ref[:, off:off + wd].astype(MXU_DTYPE), _NT,
                                preferred_element_type=F32)
            dh = p if dh is None else dh + p
        _, vjp = jax.vjp(lambda xv, gv: _fn_rms([[xv]], [], [[gv]])[0][0], x_ref[...], g_ref[...])
        dxv, dgv = vjp(dh)
        dxv = dxv + add_ref[...]
        dx_ref[...] = dxv
        dxc_ref[...] = dxv.astype(dxc_ref.dtype)

        @pl.when(pl.program_id(0) == 0)
        def _():
            dg_ref[...] = dgv

        @pl.when(pl.program_id(0) > 0)
        def _():
            dg_ref[...] += dgv

    row = pl.BlockSpec((bm, N), lambda i: (i, 0))
    whole = lambda a: pl.BlockSpec(a.shape, lambda i: (0, 0))
    in_specs = [pl.BlockSpec((bm, wd), lambda i: (i, 0)) for wd in widths] + [whole(w), row, whole(gain), row]
    in_specs += [] if after is None else [whole(after)]
    return pl.pallas_call(
        body, name=name, grid=(M // bm,),
        in_specs=in_specs, out_specs=[row, row, whole(gain)],
        out_shape=[jax.ShapeDtypeStruct((M, N), F32), jax.ShapeDtypeStruct((M, N), BF16),
                   jax.ShapeDtypeStruct(gain.shape, F32)],
        compiler_params=_params(("arbitrary",)),
    )(*a_parts, w, x, gain, add, *(() if after is None else (after,)))


def _tiles(ref, width, tile):
    return [ref[:, t * tile:(t + 1) * tile].astype(F32) for t in range(width // tile)]


def _row_specs(rows, pos, consts, bm, S):
    npos_blocks = S // bm
    specs = [pl.BlockSpec((bm, w), functools.partial(lambda i, c: (i, c), c=cb)) for (_, w, cb, _) in rows]
    specs += [pl.BlockSpec((bm, p.shape[1]), lambda i: (i % npos_blocks, 0)) for p in pos]
    specs += [pl.BlockSpec(c.shape, lambda i: (0, 0)) for (c, _) in consts]
    return specs


def _rowwise_fwd(fn, name, rows, pos, consts, outs, bm, S, transposed=()):
    T = rows[0][0].shape[0]
    nr, npos, nc, no = len(rows), len(pos), len(consts), len(outs)

    def body(*refs):
        row_v = [_tiles(r, w, t) for r, (_, w, _, t) in zip(refs[:nr], rows)]
        pos_v = [r[...] for r in refs[nr:nr + npos]]
        const_v = [_tiles(r, c.shape[1], t) for r, (c, t) in zip(refs[nr + npos:nr + npos + nc], consts)]
        res = fn(row_v, pos_v, const_v)
        out_refs = refs[nr + npos + nc:]
        for o_ref, tiles, (w, t, dt) in zip(out_refs, res, outs):
            for k, v in enumerate(tiles):
                o_ref[:, k * t:(k + 1) * t] = v.astype(dt)
        for t_ref, a in zip(out_refs[no:], transposed):
            t = outs[a][1]
            for k, v in enumerate(res[a]):
                t_ref[k * t:(k + 1) * t, :] = v.T.astype(t_ref.dtype)

    return pl.pallas_call(
        body, name=name, grid=(T // bm,),
        in_specs=_row_specs(rows, pos, consts, bm, S),
        out_specs=[pl.BlockSpec((bm, w), lambda i: (i, 0)) for (w, _, _) in outs]
        + [pl.BlockSpec((outs[a][0], bm), lambda i: (0, i)) for a in transposed],
        out_shape=[jax.ShapeDtypeStruct((T, w), dt) for (w, _, dt) in outs]
        + [jax.ShapeDtypeStruct((outs[a][0], T), BF16) for a in transposed],
        compiler_params=_params(("parallel",)),
    )(*[r[0] for r in rows], *pos, *[c[0] for c in consts])


def _rowwise_bwd(fn, name, rows, pos, consts, cts, bm, S, adds=None, grad_dtypes=None, mxu_copies=(), linear=False):
    adds = adds or {}
    T = rows[0][0].shape[0]
    nr, npos, nc, nct = len(rows), len(pos), len(consts), len(cts)
    add_idx = sorted(adds)
    grad_dtypes = grad_dtypes or [F32] * nr

    def body(*refs):
        it = iter(refs)
        row_refs = [None if linear else next(it) for _ in range(nr)]
        pos_refs = [next(it) for _ in range(npos)]
        const_refs = [next(it) for _ in range(nc)]
        ct_refs = [next(it) for _ in range(nct)]
        add_refs = {k: next(it) for k in add_idx}
        drow_refs = [next(it) for _ in range(nr)]
        copy_refs = {a: next(it) for a in mxu_copies}
        dconst_refs = [next(it) for _ in range(nc)]
        if linear:
            row_v = [[jnp.zeros((bm, t), F32)] * (w // t) for (_, w, _, t) in rows]
        else:
            row_v = [_tiles(r, w, t) for r, (_, w, _, t) in zip(row_refs, rows)]
        pos_v = [r[...] for r in pos_refs]
        const_v = [_tiles(r, c.shape[1], t) for r, (c, t) in zip(const_refs, consts)]
        ct_v = [_tiles(r, c.shape[1], t) for r, (c, t) in zip(ct_refs, cts)]
        _, vjp = jax.vjp(lambda rv, cv: fn(rv, pos_v, cv), row_v, const_v)
        drows, dconsts = vjp(ct_v)
        for a, (d_ref, tiles, (_, w, _, t)) in enumerate(zip(drow_refs, drows, rows)):
            for k, v in enumerate(tiles):
                if a in add_refs:
                    v = v + add_refs[a][:, k * t:(k + 1) * t].astype(F32)
                d_ref[:, k * t:(k + 1) * t] = v.astype(d_ref.dtype)
                if a in copy_refs:
                    copy_refs[a][:, k * t:(k + 1) * t] = v.astype(BF16)
        first = pl.program_id(0) == 0
        for d_ref, tiles, (_, t) in zip(dconst_refs, dconsts, consts):
            for k, v in enumerate(tiles):
                @pl.when(first)
                def _(d_ref=d_ref, k=k, t=t, v=v):
                    d_ref[:, k * t:(k + 1) * t] = v

                @pl.when(jnp.logical_not(first))
                def _(d_ref=d_ref, k=k, t=t, v=v):
                    d_ref[:, k * t:(k + 1) * t] += v

    in_specs = _row_specs([] if linear else rows, pos, consts, bm, S)
    in_specs += [pl.BlockSpec((bm, c.shape[1]), lambda i: (i, 0)) for (c, _) in cts]
    in_specs += [pl.BlockSpec((bm, adds[k].shape[1]), lambda i: (i, 0)) for k in add_idx]
    out_specs = [pl.BlockSpec((bm, w), lambda i: (i, 0)) for (_, w, _, _) in rows]
    out_specs += [pl.BlockSpec((bm, rows[a][1]), lambda i: (i, 0)) for a in mxu_copies]
    out_specs += [pl.BlockSpec(c.shape, lambda i: (0, 0)) for (c, _) in consts]
    out_shape = [jax.ShapeDtypeStruct((T, w), dt) for (_, w, _, _), dt in zip(rows, grad_dtypes)]
    out_shape += [jax.ShapeDtypeStruct((T, rows[a][1]), BF16) for a in mxu_copies]
    out_shape += [jax.ShapeDtypeStruct(c.shape, F32) for (c, _) in consts]
    res = pl.pallas_call(
        body, name=name, grid=(T // bm,),
        in_specs=in_specs, out_specs=out_specs, out_shape=out_shape,
        compiler_params=_params(("arbitrary",)),
    )(*([] if linear else [r[0] for r in rows]), *pos, *[c[0] for c in consts], *[c[0] for c in cts],
      *[adds[k] for k in add_idx])
    n_rows = nr + len(mxu_copies)
    return res[:n_rows], res[n_rows:]


def _ssq(tiles):
    s = jnp.sum(tiles[0] * tiles[0], axis=-1, keepdims=True)
    for t in tiles[1:]:
        s = s + jnp.sum(t * t, axis=-1, keepdims=True)
    return s


def _sigmoid(x):
    return 0.5 * jnp.tanh(0.5 * x) + 0.5


def _fn_rms(rows, pos, consts):
    (x,), (g,) = rows[0], consts[0]
    r = lax.rsqrt(jnp.mean(x * x, axis=-1, keepdims=True) + RMS_EPS)
    return [[x * r * g]]


def _fn_ret_rope(rows, pos, consts):
    (qkv,) = rows
    nq = RET_HEADS * RET_QK // LANES
    q, k, v = qkv[:nq], qkv[nq:2 * nq], qkv[2 * nq:]
    cos, sin = pos

    def rot(t, scale):
        out = []
        for h in range(RET_HEADS):
            x1, x2 = t[2 * h], t[2 * h + 1]
            o1, o2 = x1 * cos - x2 * sin, x2 * cos + x1 * sin
            out += [o1, o2] if scale is None else [o1 * scale, o2 * scale]
        return out

    return [rot(q, None), rot(k, RET_QK ** -0.5), list(v)]


def _fn_ret_gate(rows, pos, consts):
    o, g = rows
    (gn,) = consts
    out = []
    for h in range(RET_HEADS):
        r = lax.rsqrt(jnp.mean(o[h] * o[h], axis=-1, keepdims=True) + RMS_EPS)
        out.append((o[h] * r * gn[h]) * (g[h] * _sigmoid(g[h])))
    return [out]


def _fn_mla_lat(rows, pos, consts):
    (p,) = rows
    gq, gkv = consts
    nq, nkv = MLA_Q_RANK // LANES, MLA_KV_RANK // LANES
    cq, ckv, kr = p[:nq], p[nq:nq + nkv], p[nq + nkv]
    rq = lax.rsqrt(_ssq(cq) / MLA_Q_RANK + RMS_EPS)
    rkv = lax.rsqrt(_ssq(ckv) / MLA_KV_RANK + RMS_EPS)
    return [[t * rq * g for t, g in zip(cq, gq)], [t * rkv * g for t, g in zip(ckv, gkv)], [kr]]


def _swap32_impl(x):
    lane = lax.broadcasted_iota(jnp.int32, x.shape, 1)
    up, down = pltpu.roll(x, LANES - 32, 1), pltpu.roll(x, 32, 1)
    return jnp.where(lane < 32, up, jnp.where(lane < 64, down, 0.0))


@jax.custom_vjp
def _swap32(x):
    return _swap32_impl(x)


_swap32.defvjp(lambda x: (_swap32_impl(x), None), lambda _, g: (_swap32_impl(g),))


def _fn_mla_heads(rows, pos, consts):
    qf, kvf, (kr,) = rows
    cos, sin = pos
    gq, gk = consts
    q_out, k_out, v_out = [], [], []
    for h in range(MLA_HEADS):
        q0, q1 = qf[2 * h], qf[2 * h + 1]
        r = lax.rsqrt(_ssq([q0, q1]) / MLA_QK + RMS_EPS)
        a0, a1 = q0 * r * gq[0], q1 * r * gq[1]
        a1 = a1 * cos + _swap32(a1) * sin
        q_out += [a0 * (MLA_QK ** -0.5), a1 * (MLA_QK ** -0.5)]
        k0 = kvf[2 * h]
        r = lax.rsqrt(_ssq([k0, kr]) / MLA_QK + RMS_EPS)
        b0, b1 = k0 * r * gk[0], kr * r * gk[1]
        k_out += [b0, b1 * cos + _swap32(b1) * sin]
        v_out.append(kvf[2 * h + 1])
    return [q_out, k_out, v_out]


def _shift_down(x, n):
    row = lax.broadcasted_iota(jnp.int32, x.shape, 0)
    return jnp.where(row >= n, pltpu.roll(x, n, 0), 0.0)


def _shift_up(x, n):
    rows = x.shape[0]
    row = lax.broadcasted_iota(jnp.int32, x.shape, 0)
    return jnp.where(row < rows - n, pltpu.roll(x, rows - n, 0), 0.0)


def _conv_blocks(S):
    cb = 256
    return cb, FFN_DIM // cb


def _conv_fwd(ag, w8, B, S, name):
    cb, ncb = _conv_blocks(S)

    def body(a_ref, g_ref, w_ref, u_ref, ut_ref):
        g = g_ref[...].astype(F32)
        w = w_ref[...]
        gc = w[0:1] * _shift_down(g, 2) + w[1:2] * _shift_down(g, 1) + w[2:3] * g + w[3:4]
        u = a_ref[...].astype(F32) * (gc * _sigmoid(gc))
        u_ref[...] = u.astype(u_ref.dtype)
        ut_ref[...] = u.T.astype(ut_ref.dtype)

    return pl.pallas_call(
        body, name=name, grid=(ncb, B),
        in_specs=[pl.BlockSpec((S, cb), lambda j, b: (b, j)),
                  pl.BlockSpec((S, cb), lambda j, b: (b, ncb + j)),
                  pl.BlockSpec((8, cb), lambda j, b: (0, j))],
        out_specs=[pl.BlockSpec((S, cb), lambda j, b: (b, j)), pl.BlockSpec((cb, S), lambda j, b: (j, b))],
        out_shape=[jax.ShapeDtypeStruct((B * S, FFN_DIM), BF16), jax.ShapeDtypeStruct((FFN_DIM, B * S), BF16)],
        compiler_params=_params(("parallel", "parallel")),
    )(ag, ag, w8)


def _conv_bwd(ag, w8, du, B, S, name):
    cb, ncb = _conv_blocks(S)

    def body(a_ref, g_ref, w_ref, du_ref, da_ref, dg_ref, dw_ref):
        g = g_ref[...].astype(F32)
        w = w_ref[...]
        g1, g2 = _shift_down(g, 1), _shift_down(g, 2)
        gc = w[0:1] * g2 + w[1:2] * g1 + w[2:3] * g + w[3:4]
        sg = _sigmoid(gc)
        du_v = du_ref[...]
        da_ref[...] = (du_v * (gc * sg)).astype(da_ref.dtype)
        dgc = du_v * a_ref[...].astype(F32) * (sg * (1.0 + gc * (1.0 - sg)))
        dg = w[2:3] * dgc + w[1:2] * _shift_up(dgc, 1) + w[0:1] * _shift_up(dgc, 2)
        dg_ref[...] = dg.astype(dg_ref.dtype)
        part = jnp.concatenate([
            jnp.sum(dgc * g2, axis=0, keepdims=True), jnp.sum(dgc * g1, axis=0, keepdims=True),
            jnp.sum(dgc * g, axis=0, keepdims=True), jnp.sum(dgc, axis=0, keepdims=True),
            jnp.zeros((4, cb), F32)], axis=0)

        @pl.when(pl.program_id(1) == 0)
        def _():
            dw_ref[...] = part

        @pl.when(pl.program_id(1) > 0)
        def _():
            dw_ref[...] += part

    blk = lambda j, b: (b, j)
    return pl.pallas_call(
        body, name=name, grid=(ncb, B),
        in_specs=[pl.BlockSpec((S, cb), blk),
                  pl.BlockSpec((S, cb), lambda j, b: (b, ncb + j)),
                  pl.BlockSpec((8, cb), lambda j, b: (0, j)),
                  pl.BlockSpec((S, cb), blk)],
        out_specs=[pl.BlockSpec((S, cb), blk), pl.BlockSpec((S, cb), blk),
                   pl.BlockSpec((8, cb), lambda j, b: (0, j))],
        out_shape=[jax.ShapeDtypeStruct((B * S, FFN_DIM), BF16), jax.ShapeDtypeStruct((B * S, FFN_DIM), BF16),
                   jax.ShapeDtypeStruct((8, FFN_DIM), F32)],
        compiler_params=_params(("parallel", "arbitrary")),
    )(ag, ag, w8, du)


_NT = (((1,), (1,)), ((), ()))
_NN = (((1,), (0,)), ((), ()))
_TN = (((0,), (0,)), ((), ()))


def _dot(a, b, dn):
    return lax.dot_general(a.astype(MXU_DTYPE), b.astype(MXU_DTYPE), dn, preferred_element_type=F32)


def _run_bits(n):
    bits, b = [], 1
    while b < n:
        bits.append(b)
        b *= 2
    return bits[::-1]


def _key_runs(n, nq, update):
    for bit in _run_bits(nq + 1):
        @pl.when((n & bit) != 0)
        def _(bit=bit):
            update(n & ~(2 * bit - 1), bit, (n & (bit - 1)) == 0)


def _earlier_runs(n, nq, update):
    for bit in _run_bits(nq):
        @pl.when((n & bit) != 0)
        def _(bit=bit):
            update(n & ~(2 * bit - 1), bit, False)


def _chunk_visible(shape, nblk, blk):
    key = lax.broadcasted_iota(jnp.int32, shape, 0) - (nblk - 1) * blk
    query = lax.broadcasted_iota(jnp.int32, shape, 1)
    return jnp.logical_or(key < 0, (key // CHUNK) <= (query // CHUNK))


def _mla_attn_fwd(q, k, v, B, S):
    blk = min(MLA_FWD_BLOCK, S)
    H, nq = MLA_HEADS, S // blk

    def body(q_ref, k_ref, v_ref, o_ref, lse_ref, m_ref, l_ref, acc_ref):
        def qblock(i, _):
            q_rows = pl.ds(pl.multiple_of(i * blk, blk), blk)
            qi = q_ref[q_rows, :]
            m_ref[...] = jnp.full(m_ref.shape, MASK_VALUE, F32)
            l_ref[...] = jnp.zeros(l_ref.shape, F32)
            acc_ref[...] = jnp.zeros(acc_ref.shape, F32)

            def keys(first, nblk, last):
                rows = pl.ds(pl.multiple_of(first * blk, blk), nblk * blk)
                s = _dot(k_ref[rows, :], qi, _NT)
                s = jnp.where(jnp.logical_or(_chunk_visible(s.shape, nblk, blk), jnp.logical_not(last)), s, MASK_VALUE)
                m = m_ref[...]
                m2 = jnp.maximum(m, jnp.max(s, axis=0, keepdims=True))
                alpha = jnp.exp(m - m2)
                p = jnp.exp(s - m2)
                l_ref[...] = alpha * l_ref[...] + jnp.sum(p, axis=0, keepdims=True)
                acc_ref[...] = alpha * acc_ref[...] + _dot(v_ref[rows, :], p, _TN)
                m_ref[...] = m2

            _key_runs(i + 1, nq, keys)
            l = l_ref[...]
            o_ref[q_rows, :] = (acc_ref[...] / l).T
            lse_ref[0, :, q_rows] = m_ref[...] + jnp.log(l)
            return 0

        lax.fori_loop(0, nq, qblock, 0)

    return pl.pallas_call(
        body, name="mla_attn_fwd", grid=(B, H),
        in_specs=[pl.BlockSpec((S, MLA_PAD), lambda b, h: (b, h)),
                  pl.BlockSpec((S, MLA_PAD), lambda b, h: (b, h)),
                  pl.BlockSpec((S, MLA_V), lambda b, h: (b, h))],
        out_specs=[pl.BlockSpec((S, MLA_V), lambda b, h: (b, h)),
                   pl.BlockSpec((1, 1, S), lambda b, h: (b * H + h, 0, 0))],
        out_shape=[jax.ShapeDtypeStruct((B * S, H * MLA_V), F32), jax.ShapeDtypeStruct((B * H, 1, S), F32)],
        scratch_shapes=[pltpu.VMEM((1, blk), F32), pltpu.VMEM((1, blk), F32), pltpu.VMEM((MLA_V, blk), F32)],
        compiler_params=_params(("parallel", "parallel")),
    )(q, k, v)


def _mla_attn_bwd(q, k, v, o, do, lse, B, S):
    blk = min(MLA_FWD_BLOCK, S)
    H, nq = MLA_HEADS, S // blk

    def body(q_ref, k_ref, v_ref, o_ref, do_ref, lse_ref, dq_ref, dk_ref, dv_ref, kt_ref, dqt_ref):
        dk_ref[...] = jnp.zeros(dk_ref.shape, F32)
        dv_ref[...] = jnp.zeros(dv_ref.shape, F32)
        for g in range(nq):
            kt_ref[g] = k_ref[g * blk:(g + 1) * blk, :].T

        def qblock(i, _):
            q_rows = pl.ds(pl.multiple_of(i * blk, blk), blk)
            qi = q_ref[q_rows, :]
            doi = do_ref[q_rows, :]
            delta = jnp.sum((doi * o_ref[q_rows, :]).T, axis=0, keepdims=True)
            lse_i = lse_ref[0, :, q_rows]
            doi = doi.astype(MXU_DTYPE)
            dqt_ref[...] = jnp.zeros(dqt_ref.shape, F32)

            def keys(first, nblk, last):
                rows = pl.ds(pl.multiple_of(first * blk, blk), nblk * blk)
                k_run, v_run = k_ref[rows, :], v_ref[rows, :]
                p = jnp.exp(_dot(k_run, qi, _NT) - lse_i)
                p = jnp.where(jnp.logical_or(_chunk_visible(p.shape, nblk, blk), jnp.logical_not(last)), p, 0.0)
                ds = (p * (_dot(v_run, doi, _NT) - delta)).astype(MXU_DTYPE)
                dk_ref[rows, :] += _dot(ds, qi, _NN)
                dv_ref[rows, :] += _dot(p, doi, _NN)
                for r in range(nblk):
                    dqt_ref[...] += _dot(kt_ref[first + r], ds[r * blk:(r + 1) * blk, :], _NN)

            _key_runs(i + 1, nq, keys)
            dq_ref[q_rows, :] = dqt_ref[...].T
            return 0

        lax.fori_loop(0, nq, qblock, 0)

    qk_spec = pl.BlockSpec((S, MLA_PAD), lambda b, h: (b, h))
    v_spec = pl.BlockSpec((S, MLA_V), lambda b, h: (b, h))
    return pl.pallas_call(
        body, name="mla_attn_bwd", grid=(B, H),
        in_specs=[qk_spec, qk_spec, v_spec, v_spec, v_spec,
                  pl.BlockSpec((1, 1, S), lambda b, h: (b * H + h, 0, 0))],
        out_specs=[qk_spec, qk_spec, v_spec],
        out_shape=[jax.ShapeDtypeStruct((B * S, H * MLA_PAD), F32), jax.ShapeDtypeStruct((B * S, H * MLA_PAD), F32),
                   jax.ShapeDtypeStruct((B * S, H * MLA_V), F32)],
        scratch_shapes=[pltpu.VMEM((nq, MLA_PAD, blk), q.dtype), pltpu.VMEM((MLA_PAD, blk), F32)],
        compiler_params=_params(("parallel", "parallel")),
    )(q, k, v, o, do, lse)


def _ret_log_gamma():
    lg = np.log1p(-np.exp2(RET_GAMMA_BASE - np.arange(RET_HEADS, dtype=np.float32))).astype(np.float32)
    return jnp.asarray(np.broadcast_to(lg[:, None, None], (RET_HEADS, 8, LANES)).copy())


RET_BLOCK = 512


def _ret_local_scale(lg, shape, blk, rising):
    local = lax.broadcasted_iota(jnp.int32, shape, 0) % blk
    return jnp.exp(lg * (local if rising else blk - 1 - local).astype(F32))


def _ret_pair_factor(lg, blk, steps):
    return jnp.exp(lg * (blk * (steps - 1) + 1).astype(F32))


def _ret_own_decay(lg, blk, transposed):
    a = lax.broadcasted_iota(jnp.int32, (blk, blk), 0)
    b = lax.broadcasted_iota(jnp.int32, (blk, blk), 1)
    query, key = (b, a) if transposed else (a, b)
    dec = jnp.exp(lg * jnp.abs(query - key).astype(F32))
    return jnp.where((key // CHUNK) <= (query // CHUNK), dec, 0.0)


def _ret_attn_fwd(q, k, v, B, S):
    blk = min(RET_BLOCK, S)
    H, nq = RET_HEADS, S // blk

    def body(lg_ref, q_ref, k_ref, v_ref, o_ref, ks_ref, dec_ref, acc_ref):
        lg = lg_ref[0, 0:1, 0:1]
        ks_ref[...] = (k_ref[...].astype(F32) * _ret_local_scale(lg, k_ref.shape, blk, False)).astype(ks_ref.dtype)
        dec_ref[...] = _ret_own_decay(lg, blk, False)

        def qblock(i, _):
            q_rows = pl.ds(pl.multiple_of(i * blk, blk), blk)
            qi = q_ref[q_rows, :]
            qs = (qi.astype(F32) * _ret_local_scale(lg, qi.shape, blk, True)).astype(qi.dtype)
            a = _dot(qi, k_ref[q_rows, :], _NT) * dec_ref[...]
            acc_ref[...] = _dot(a, v_ref[q_rows, :], _NN)

            def keys(first, nblk, _):
                rows = pl.ds(pl.multiple_of(first * blk, blk), nblk * blk)
                steps = i - first - lax.broadcasted_iota(jnp.int32, (1, nblk * blk), 1) // blk
                a = _dot(qs, ks_ref[rows, :], _NT) * _ret_pair_factor(lg, blk, steps)
                acc_ref[...] += _dot(a, v_ref[rows, :], _NN)

            _earlier_runs(i, nq, keys)
            o_ref[q_rows, :] = acc_ref[...]
            return 0

        lax.fori_loop(0, nq, qblock, 0)

    qk_spec = pl.BlockSpec((S, RET_QK), lambda b, h: (b, h))
    v_spec = pl.BlockSpec((S, RET_V), lambda b, h: (b, h))
    return pl.pallas_call(
        body, name="ret_attn_fwd", grid=(B, H),
        in_specs=[pl.BlockSpec((1, 8, LANES), lambda b, h: (h, 0, 0)), qk_spec, qk_spec, v_spec],
        out_specs=v_spec,
        out_shape=jax.ShapeDtypeStruct((B * S, H * RET_V), F32),
        scratch_shapes=[pltpu.VMEM((S, RET_QK), k.dtype), pltpu.VMEM((blk, blk), F32), pltpu.VMEM((blk, RET_V), F32)],
        compiler_params=_params(("parallel", "parallel")),
    )(_ret_log_gamma(), q, k, v)


def _ret_attn_bwd(q, k, v, do, B, S):
    blk = min(RET_BLOCK, S)
    H, nq = RET_HEADS, S // blk

    def body(lg_ref, q_ref, k_ref, v_ref, do_ref, dq_ref, dk_ref, dv_ref, ks_ref, kst_ref, dks_ref, dqt_ref, dec_ref):
        lg = lg_ref[0, 0:1, 0:1]
        dk_ref[...] = jnp.zeros(dk_ref.shape, F32)
        dv_ref[...] = jnp.zeros(dv_ref.shape, F32)
        dks_ref[...] = jnp.zeros(dks_ref.shape, F32)
        ks_ref[...] = (k_ref[...].astype(F32) * _ret_local_scale(lg, k_ref.shape, blk, False)).astype(ks_ref.dtype)
        for g in range(nq):
            kst_ref[g] = ks_ref[g * blk:(g + 1) * blk, :].T
        dec_ref[...] = _ret_own_decay(lg, blk, True)

        def qblock(i, _):
            q_rows = pl.ds(pl.multiple_of(i * blk, blk), blk)
            qi = q_ref[q_rows, :]
            q_scale = _ret_local_scale(lg, qi.shape, blk, True)
            qs = (qi.astype(F32) * q_scale).astype(qi.dtype)
            doi = do_ref[q_rows, :].astype(MXU_DTYPE)
            ki = k_ref[q_rows, :]
            dec = dec_ref[...]
            a = _dot(ki, qi, _NT) * dec
            da = (_dot(v_ref[q_rows, :], doi, _NT) * dec).astype(MXU_DTYPE)
            dv_ref[q_rows, :] += _dot(a, doi, _NN)
            dk_ref[q_rows, :] += _dot(da, qi, _NN)
            dq_own = _dot(da, ki, _TN)
            dqt_ref[...] = jnp.zeros(dqt_ref.shape, F32)

            def keys(first, nblk, _):
                for r in range(nblk):
                    g = first + r
                    rows = pl.ds(pl.multiple_of(g * blk, blk), blk)
                    c = _ret_pair_factor(lg, blk, i - g)
                    a = _dot(ks_ref[rows, :], qs, _NT) * c
                    da = (_dot(v_ref[rows, :], doi, _NT) * c).astype(MXU_DTYPE)
                    dv_ref[rows, :] += _dot(a, doi, _NN)
                    dks_ref[rows, :] += _dot(da, qs, _NN)
                    dqt_ref[...] += _dot(kst_ref[g], da, _NN)

            _earlier_runs(i, nq, keys)
            dq_ref[q_rows, :] = dqt_ref[...].T * q_scale + dq_own
            return 0

        lax.fori_loop(0, nq, qblock, 0)
        dk_ref[...] += dks_ref[...] * _ret_local_scale(lg, dks_ref.shape, blk, False)

    qk_spec = pl.BlockSpec((S, RET_QK), lambda b, h: (b, h))
    v_spec = pl.BlockSpec((S, RET_V), lambda b, h: (b, h))
    return pl.pallas_call(
        body, name="ret_attn_bwd", grid=(B, H),
        in_specs=[pl.BlockSpec((1, 8, LANES), lambda b, h: (h, 0, 0)), qk_spec, qk_spec, v_spec, v_spec],
        out_specs=[qk_spec, qk_spec, v_spec],
        out_shape=[jax.ShapeDtypeStruct((B * S, H * RET_QK), F32), jax.ShapeDtypeStruct((B * S, H * RET_QK), F32),
                   jax.ShapeDtypeStruct((B * S, H * RET_V), F32)],
        scratch_shapes=[pltpu.VMEM((S, RET_QK), k.dtype), pltpu.VMEM((nq, RET_QK, blk), k.dtype),
                        pltpu.VMEM((S, RET_QK), F32), pltpu.VMEM((RET_QK, blk), F32), pltpu.VMEM((blk, blk), F32)],
        compiler_params=_params(("parallel", "parallel")),
    )(_ret_log_gamma(), q, k, v, do)


def _adamw(w, g, m, v, name):
    R, C = w.shape
    br = R if R * C * 4 <= 2 ** 21 else _pick_rows(R, max(8, (2 ** 21) // (C * 4)))

    def body(w_ref, g_ref, m_ref, v_ref, d_ref, mo_ref, vo_ref):
        g_v = g_ref[...]
        m_v = ADAM_B1 * m_ref[...] + (1.0 - ADAM_B1) * g_v
        v_v = ADAM_B2 * v_ref[...] + (1.0 - ADAM_B2) * (g_v * g_v)
        m_hat = m_v / (1.0 - ADAM_B1 ** ADAM_STEP)
        v_hat = v_v / (1.0 - ADAM_B2 ** ADAM_STEP)
        d_ref[...] = -ADAM_LR * (m_hat / (jnp.sqrt(v_hat) + ADAM_EPS) + ADAM_WD * w_ref[...])
        mo_ref[...] = m_v
        vo_ref[...] = v_v

    blk = pl.BlockSpec((br, C), lambda i: (i, 0))
    return pl.pallas_call(
        body, name=name, grid=(R // br,),
        in_specs=[blk] * 4, out_specs=[blk] * 3,
        out_shape=[jax.ShapeDtypeStruct((R, C), F32)] * 3,
        compiler_params=_params(("parallel",)),
    )(w, g, m, v)


def _pick_rows(R, target):
    best = None
    for d in range(8, min(R, target) + 1, 8):
        if R % d == 0:
            best = d
    assert best is not None, (R, target)
    return best


def _position():
    return lax.axis_index("x"), lax.axis_index("y"), lax.axis_index("c")


HBM_SPEC = pl.BlockSpec(memory_space=pltpu.HBM)


def _other_chips(x, y):
    return [(1 - x, y), (x, 1 - y), (1 - x, 1 - y)]


def _all_gather_weights(bigs, small):
    nb = len(bigs)

    def body(*refs):
        big_refs, small_ref = refs[:nb], refs[nb]
        obig, osmall = refs[nb + 1:2 * nb + 1], refs[2 * nb + 1]
        ici_send, ici_recv, d2d_send, d2d_recv, sm_send, sm_recv = refs[2 * nb + 2:]
        x, y, c = _position()
        me = 2 * x + y
        chips = _other_chips(x, y)

        def rows(n, half):
            rh = bigs[n].shape[0] // 2
            return pl.ds(half * rh, rh)

        def over_ici(n, j, slot, from_shard):
            px, py = chips[j]
            dst = obig[n].at[slot, rows(n, c)]
            return pltpu.make_async_remote_copy(
                src_ref=big_refs[n].at[rows(n, c)] if from_shard else dst, dst_ref=dst,
                send_sem=ici_send.at[3 * n + j], recv_sem=ici_recv.at[3 * n + j],
                device_id=(px, py, c), device_id_type=MESH)

        def over_d2d(n, j, half):
            px, py = chips[j]
            part = obig[n].at[2 * px + py, rows(n, half)]
            return pltpu.make_async_remote_copy(
                src_ref=part, dst_ref=part, send_sem=d2d_send.at[3 * n + j], recv_sem=d2d_recv.at[3 * n + j],
                device_id=(x, y, 1 - c), device_id_type=MESH)

        def small_copy(j, slot):
            px, py = chips[j]
            return pltpu.make_async_remote_copy(
                src_ref=small_ref, dst_ref=osmall.at[slot], send_sem=sm_send.at[j], recv_sem=sm_recv.at[j],
                device_id=(px, py, c), device_id_type=MESH)

        sends = [over_ici(n, j, me, True) for n in range(nb) for j in range(3)]
        sends += [small_copy(j, me) for j in range(3)]
        for cp in sends:
            cp.start()
        passed = []
        for n in range(nb):
            for j, (px, py) in enumerate(chips):
                over_ici(n, j, 2 * px + py, False).wait_recv()
                fwd = over_d2d(n, j, c)
                fwd.start()
                passed.append(fwd)
        for n in range(nb):
            for j in range(3):
                over_d2d(n, j, 1 - c).wait_recv()
        for j, (px, py) in enumerate(chips):
            small_copy(j, 2 * px + py).wait_recv()
        for cp in sends + passed:
            cp.wait_send()

    dma = pltpu.SemaphoreType.DMA
    return pl.pallas_call(
        body, name="weights_all_gather",
        in_specs=[HBM_SPEC] * (nb + 1), out_specs=[HBM_SPEC] * (nb + 1),
        out_shape=[jax.ShapeDtypeStruct((N_SHARD,) + b.shape, b.dtype) for b in bigs]
        + [jax.ShapeDtypeStruct((N_SHARD,) + small.shape, small.dtype)],
        scratch_shapes=[dma((3 * nb,)), dma((3 * nb,)), dma((3 * nb,)), dma((3 * nb,)), dma((3,)), dma((3,))],
    )(*bigs, small)


SEM_SPEC = pl.BlockSpec(memory_space=pltpu.SEMAPHORE)
DATAFLOW_EFFECT = pltpu.SideEffectType.DATAFLOW_SIDE_EFFECTING
N_PEERS = N_DEV - 1


def _grad_copies(p_refs, land_refs, send_sems, recv_sems):
    x, y, c = _position()
    copies = []
    for a, (p_ref, land_ref) in enumerate(zip(p_refs, land_refs)):
        rh = p_ref.shape[1] // 2
        for k in range(1, N_DEV):
            px = 1 - x if k & 4 else x
            py = 1 - y if k & 2 else y
            pc = 1 - c if k & 1 else c
            copies.append(pltpu.make_async_remote_copy(
                src_ref=p_ref.at[2 * px + py, pl.ds(pc * rh, rh)], dst_ref=land_ref.at[k - 1],
                send_sem=send_sems.at[N_PEERS * a + k - 1], recv_sem=recv_sems.at[N_PEERS * a + k - 1],
                device_id=(px, py, pc), device_id_type=MESH))
    return copies


def _weight_copies(w_refs, land_refs, send_sems, recv_sems):
    x, y, c = _position()
    copies = []
    for a, (w_ref, land_ref) in enumerate(zip(w_refs, land_refs)):
        for j, (px, py) in enumerate(_other_chips(x, y)):
            copies.append(pltpu.make_async_remote_copy(
                src_ref=w_ref, dst_ref=land_ref.at[2 * x + y], send_sem=send_sems.at[3 * a + j],
                recv_sem=recv_sems.at[3 * a + j], device_id=(px, py, c), device_id_type=MESH))
    return copies


def _exchange_start(make_copies, srcs, lands, n_sems, name, after=None):
    n, m = len(srcs), len(lands)
    n_in = n + m + (after is not None)

    def body(*refs):
        send_sems, recv_sems, token = refs[n_in], refs[n_in + 1], refs[-1]
        for cp in make_copies(refs[:n], refs[n:n + m], send_sems, recv_sems):
            cp.start()
        token[...] = jnp.zeros(token.shape, token.dtype)

    hbm = lambda a: pltpu.with_memory_space_constraint(a, pltpu.HBM)
    dma = pltpu.SemaphoreType.DMA
    res = pl.pallas_call(
        body, name=name,
        in_specs=[HBM_SPEC] * (n + m) + ([] if after is None else [pl.BlockSpec(memory_space=pl.ANY)]),
        out_specs=[SEM_SPEC, SEM_SPEC] + [HBM_SPEC] * (n + m) + [pl.BlockSpec(memory_space=pltpu.VMEM)],
        out_shape=[dma((n_sems,)), dma((n_sems,))] + [pltpu.HBM(a.shape, a.dtype) for a in list(srcs) + list(lands)]
        + [jax.ShapeDtypeStruct((8, LANES), F32)],
        input_output_aliases={i: 2 + i for i in range(n + m)},
        compiler_params=pltpu.CompilerParams(has_side_effects=DATAFLOW_EFFECT),
    )(*[hbm(a) for a in srcs], *[hbm(a) for a in lands], *(() if after is None else (after,)))
    return res[0], res[1], list(res[2:2 + n]), list(res[2 + n:2 + n + m]), res[-1]


def _exchange_wait(make_copies, send_sems, recv_sems, srcs, lands, after, name):
    n, m = len(srcs), len(lands)

    def body(*refs):
        for cp in make_copies(refs[:n], refs[n:n + m], refs[n + m], refs[n + m + 1]):
            cp.wait_send()
            cp.wait_recv()

    res = pl.pallas_call(
        body, name=name,
        in_specs=[HBM_SPEC] * (n + m) + [SEM_SPEC, SEM_SPEC, pl.BlockSpec(memory_space=pl.ANY)],
        out_specs=[HBM_SPEC] * (n + m),
        out_shape=[pltpu.HBM(a.shape, a.dtype) for a in list(srcs) + list(lands)],
        input_output_aliases={i: i for i in range(n + m)},
        compiler_params=pltpu.CompilerParams(has_side_effects=DATAFLOW_EFFECT),
    )(*srcs, *lands, send_sems, recv_sems, after)
    return list(res[:n]), list(res[n:])


def _sum_partials(p, land, name):
    _, rh, cols = land.shape
    br = _pick_rows(rh, 256)
    nrb = rh // br
    x, y, c = _position()
    where = jnp.stack([2 * x + y, c]).astype(jnp.int32)

    def body(where_ref, p_ref, land_ref, o_ref):
        acc = p_ref[...].astype(F32)
        for k in range(N_PEERS):
            acc = acc + land_ref[k].astype(F32)
        o_ref[...] = acc

    return pl.pallas_call(
        body, name=name,
        grid_spec=pltpu.PrefetchScalarGridSpec(
            num_scalar_prefetch=1, grid=(nrb,),
            in_specs=[pl.BlockSpec((None, br, cols), lambda r, where_ref: (where_ref[0], where_ref[1] * nrb + r, 0)),
                      pl.BlockSpec((N_PEERS, br, cols), lambda r, where_ref: (0, r, 0))],
            out_specs=pl.BlockSpec((None, br, cols), lambda r, where_ref: (where_ref[1], r, 0))),
        out_shape=jax.ShapeDtypeStruct((2, rh, cols), F32),
        compiler_params=_params(("parallel",)),
    )(where, p, land)


def _sibling_share(fulls, name):
    n = len(fulls)

    def body(*refs):
        o_refs = refs[n:2 * n]
        send_sems, recv_sems = refs[2 * n:]
        x, y, c = _position()

        def copy(a, half):
            return pltpu.make_async_remote_copy(
                src_ref=o_refs[a].at[half], dst_ref=o_refs[a].at[half], send_sem=send_sems.at[a],
                recv_sem=recv_sems.at[a], device_id=(x, y, 1 - c), device_id_type=MESH)

        sends = [copy(a, c) for a in range(n)]
        for cp in sends:
            cp.start()
        for a in range(n):
            copy(a, 1 - c).wait_recv()
        for cp in sends:
            cp.wait_send()

    dma = pltpu.SemaphoreType.DMA
    return pl.pallas_call(
        body, name=name,
        in_specs=[HBM_SPEC] * n, out_specs=[HBM_SPEC] * n,
        out_shape=[jax.ShapeDtypeStruct(f.shape, f.dtype) for f in fulls],
        input_output_aliases={a: a for a in range(n)},
        scratch_shapes=[dma((n,)), dma((n,))],
    )(*fulls)


def _all_reduce_small(v):
    R, cols = v.shape

    def body(v_ref, o_ref, buf_ref, send_sems, recv_sems):
        x, y, c = _position()
        me = 4 * x + 2 * y + c
        buf_ref[me] = v_ref[...]
        sends = []
        for k in range(1, N_DEV):
            px = 1 - x if k & 4 else x
            py = 1 - y if k & 2 else y
            pc = 1 - c if k & 1 else c
            sends.append(pltpu.make_async_remote_copy(
                src_ref=v_ref, dst_ref=buf_ref.at[me], send_sem=send_sems.at[k - 1], recv_sem=recv_sems.at[k - 1],
                device_id=(px, py, pc), device_id_type=MESH))
        for cp in sends:
            cp.start()
        for k in range(1, N_DEV):
            px = 1 - x if k & 4 else x
            py = 1 - y if k & 2 else y
            pc = 1 - c if k & 1 else c
            pltpu.make_async_remote_copy(
                src_ref=v_ref, dst_ref=buf_ref.at[4 * px + 2 * py + pc], send_sem=send_sems.at[k - 1],
                recv_sem=recv_sems.at[k - 1], device_id=(px, py, pc), device_id_type=MESH).wait_recv()
        for cp in sends:
            cp.wait_send()
        acc = buf_ref[0]
        for d in range(1, N_DEV):
            acc = acc + buf_ref[d]
        o_ref[...] = acc

    return pl.pallas_call(
        body, name="small_grads_all_reduce",
        in_specs=[pl.BlockSpec(memory_space=pltpu.VMEM)], out_specs=pl.BlockSpec(memory_space=pltpu.VMEM),
        out_shape=jax.ShapeDtypeStruct((R, cols), F32),
        scratch_shapes=[pltpu.VMEM((N_DEV, R, cols), F32), pltpu.SemaphoreType.DMA((N_DEV - 1,)),
                        pltpu.SemaphoreType.DMA((N_DEV - 1,))],
    )(v)


def _rope_tables(S, half, width):
    inv_freq = ROPE_THETA ** (-jnp.arange(half, dtype=F32) / half)
    ang = jnp.arange(S).astype(F32)[:, None] * inv_freq[None, :]
    return jnp.cos(ang), jnp.sin(ang)


def _slot_rows(a):
    return a.reshape(N_SHARD, -1, a.shape[-1])


def _local_step(x, target, w, B, S, late, exchange, reduce_small):
    T = B * S
    D = D_MODEL
    bm = min(512, S)
    full = lambda a, wd, tile=None: (a, wd, 0, tile or wd)
    g = {}

    cos_r, sin_r = _rope_tables(S, RET_QK // 2, LANES)
    cos_m, sin_m = _rope_tables(S, MLA_ROPE // 2, LANES)
    zeros64 = jnp.zeros((S, 64), F32)
    cos_m = jnp.concatenate([cos_m, cos_m, zeros64], axis=1)
    sin_m = jnp.concatenate([-sin_m, sin_m, zeros64], axis=1)

    def ffn_fwd(xin, h, ht, i, next_gain):
        w.update(late(f"ffn{i}", xin))
        norm = w["ffn_norm"][i:i + 1]
        ag = _mm(h, w[f"ffn_w_in{i}"], "nn", BF16, f"ffn{i}_in", bm=1024, bn=1408, cols_outer=True)
        u, ut = _conv_fwd(ag, w["ffn_conv8"][i], B, S, f"ffn{i}_conv")
        if next_gain is None:
            out = _mm_out_loss(u, w[f"ffn_w_out{i}"], xin, target, f"ffn{i}_out")
        else:
            out = _mm_out_norm(u, w[f"ffn_w_out{i}"], xin, next_gain, f"ffn{i}_out")
        return out, (xin, norm, ht, ag, ut)

    def ffn_bwd(dxout, dxout_c, saved, i):
        xin, norm, ht, ag, ut = saved
        du = _mm(dxout_c, w[f"ffn_w_out{i}"], "nt", F32, f"ffn{i}_out_dx", bm=1024, bn=1408, cols_outer=True)
        g_w_out = _mm(ut, dxout_c, "nn", BF16, f"ffn{i}_out_dw", bm=1408, bn=512, bk=T)
        da, dg, dw8 = _conv_bwd(ag, w["ffn_conv8"][i], du, B, S, f"ffn{i}_conv_bwd")
        g_w_in = _mm(ht, [da, dg], "nn", BF16, f"ffn{i}_in_dw", bm=1024, bn=1408, bk=T // 2, out_slots=N_SHARD)
        token = exchange(f"ffn{i}", [g_w_in, _slot_rows(g_w_out)])
        dxin, dxin_c, g_norm = _mm_dx_norm([da, dg], w[f"ffn_w_in{i}"], xin, norm, dxout, f"ffn{i}_in_dx", after=token)
        return dxin, dxin_c, (g_norm, dw8)

    h0, h0t = _rowwise_fwd(_fn_rms, "ret_norm", [full(x, D)], [], [(w["ret_norm"], D)], [(D, D, BF16)], bm, S,
                           transposed=(0,))
    proj = _mm(h0, w["ret_w_in"], "nn", BF16, "ret_in", bm=1024, after=w["started"], cols_outer=True)
    HQ, HV = RET_HEADS * RET_QK, RET_HEADS * RET_V
    rope_rows = [(proj, 2 * HQ + HV, 0, LANES)]
    q_r, k_r, v_r = _rowwise_fwd(_fn_ret_rope, "ret_rope", rope_rows, [cos_r, sin_r], [],
                                 [(HQ, LANES, BF16), (HQ, LANES, BF16), (HV, LANES, BF16)], bm, S)
    ret_o = _ret_attn_fwd(q_r, k_r, v_r, B, S)
    gate_rows = [full(ret_o, HV, RET_V), (proj, HV, 2, RET_V)]
    y0, y0t = _rowwise_fwd(_fn_ret_gate, "ret_gate", gate_rows, [], [(w["ret_gn"], RET_V)], [(HV, RET_V, BF16)], bm, S,
                           transposed=(0,))
    w.update(late("ret_out", y0))
    x1, h1, h1t = _mm_out_norm(y0, w["ret_w_out"], x, w["ffn_norm"][0:1], "ret_out")
    (x2, h2, _), ffn0_saved = ffn_fwd(x1, h1, h1t, 0, w["mla_norm"])

    w.update(late("mla", x2))
    proj2 = _mm(h2, w["mla_w_in"], "nn", F32, "mla_in", bm=2048)
    lat_consts = [(w["mla_q_norm"], LANES), (w["mla_kv_norm"], LANES)]
    cqn, ckvn, kr = _rowwise_fwd(_fn_mla_lat, "mla_latent_norm", [full(proj2, MLA_IN_PAD, LANES)], [], lat_consts,
                                 [(MLA_Q_RANK, LANES, BF16), (MLA_KV_RANK, LANES, BF16), (LANES, LANES, F32)], bm, S)
    qf = _mm(cqn, w["mla_w_qb"], "nn", BF16, "mla_qb", bm=2048, bn=2048)
    kvf = _mm(ckvn, w["mla_w_kvb"], "nn", BF16, "mla_kvb", bm=2048, bn=2048)
    HP, HVm = MLA_HEADS * MLA_PAD, MLA_HEADS * MLA_V
    head_rows = [full(qf, HP, LANES), full(kvf, HP, LANES), full(kr, LANES)]
    head_consts = [(w["mla_q_head_norm"], LANES), (w["mla_k_head_norm"], LANES)]
    q_a, k_a, v_a = _rowwise_fwd(_fn_mla_heads, "mla_heads", head_rows, [cos_m, sin_m], head_consts,
                                 [(HP, LANES, BF16), (HP, LANES, BF16), (HVm, LANES, BF16)], bm, S)
    att_o, lse = _mla_attn_fwd(q_a, k_a, v_a, B, S)
    x3, h3, h3t = _mm_out_norm(att_o, w["mla_w_out"], x2, w["ffn_norm"][1:2], "mla_out")
    (dy, dy_c, loss), ffn1_saved = ffn_fwd(x3, h3, h3t, 1, None)

    dx3, dx3_c, (g_n1, dw8_1) = ffn_bwd(dy, dy_c, ffn1_saved, 1)

    d_att_o = _mm(dx3_c, w["mla_w_out"], "nt", F32, "mla_out_dx", bm=2048)
    g_mla_out = _mm(att_o, dx3_c, "tn", BF16, "mla_out_dw")
    dq_a, dk_a, dv_a = _mla_attn_bwd(q_a, k_a, v_a, att_o, d_att_o, lse, B, S)
    (dqf, dkvf, dkr), (g["mla_q_head_norm"], g["mla_k_head_norm"]) = _rowwise_bwd(
        _fn_mla_heads, "mla_heads_bwd", head_rows, [cos_m, sin_m], head_consts,
        [(dq_a, LANES), (dk_a, LANES), (dv_a, LANES)], bm, S, grad_dtypes=[BF16, BF16, F32])
    dcqn = _mm(dqf, w["mla_w_qb"], "nt", F32, "mla_qb_dx", bm=2048)
    g_qb = _mm(cqn, dqf, "tn", BF16, "mla_qb_dw")
    g_qb = _to_slots(_unpad_heads(g_qb, 1), 1).reshape(N_SHARD, MLA_Q_RANK, -1)
    dckvn = _mm(dkvf, w["mla_w_kvb"], "nt", F32, "mla_kvb_dx", bm=2048)
    g_kvb = _mm(ckvn, dkvf, "tn", BF16, "mla_kvb_dw", bn=512, out_slots=N_SHARD)
    (dproj2,), (g["mla_q_norm"], g["mla_kv_norm"]) = _rowwise_bwd(
        _fn_mla_lat, "mla_latent_norm_bwd", [full(proj2, MLA_IN_PAD, LANES)], [], lat_consts,
        [(dcqn, LANES), (dckvn, LANES), (dkr, LANES)], bm, S, grad_dtypes=[BF16])
    g_mla_in = _mm(h2, dproj2, "tn", BF16, "mla_in_dw")
    token = exchange("mla", [_slot_rows(g_mla_in[:, :MLA_IN]), g_qb, g_kvb, _slot_rows(g_mla_out)])
    dx2, dx2_c, g["mla_norm"] = _mm_dx_norm([dproj2], w["mla_w_in"], x2, w["mla_norm"], dx3, "mla_in_dx", bm=512,
                                            after=token)

    dx1, dx1_c, (g_n0, dw8_0) = ffn_bwd(dx2, dx2_c, ffn0_saved, 0)

    dy0 = _mm(dx1_c, w["ret_w_out"], "nt", F32, "ret_out_dx", bm=1024, cols_outer=True)
    g_ret_out = _mm(y0t, dx1_c, "nn", BF16, "ret_out_dw", bm=1024, bn=512, bk=T)
    token = exchange("reto", [_slot_rows(g_ret_out)])
    gn_behind = w["ret_gn"] + token[0:1, 0:1]
    (d_ret_o, dgate), (g["ret_gn"],) = _rowwise_bwd(_fn_ret_gate, "ret_gate_bwd", gate_rows, [], [(gn_behind, RET_V)],
                                                    [(dy0, RET_V)], bm, S, grad_dtypes=[F32, BF16])
    dq_r, dk_r, dv_r = _ret_attn_bwd(q_r, k_r, v_r, d_ret_o, B, S)
    (dqkv,), _ = _rowwise_bwd(_fn_ret_rope, "ret_rope_bwd", rope_rows, [cos_r, sin_r], [],
                              [(dq_r, LANES), (dk_r, LANES), (dv_r, LANES)], bm, S, grad_dtypes=[BF16], linear=True)
    dx, _, g["ret_norm"] = _mm_dx_norm([dqkv, dgate], w["ret_w_in"], x, w["ret_norm"], dx1, "ret_in_dx")
    g["ffn_norm"] = jnp.concatenate([g_n0, g_n1], axis=0)
    g["ffn_conv_w"] = jnp.stack([dw8_0[0:3], dw8_1[0:3]])
    g["ffn_conv_b"] = jnp.stack([dw8_0[3], dw8_1[3]])
    reduced_small = reduce_small(g, loss)
    g_ret_in = _mm(h0t, [dqkv, dgate], "nn", BF16, "ret_in_dw", bm=1024, bn=512, bk=T, out_slots=N_SHARD,
                   after=reduced_small)
    exchange("ret", [g_ret_in])
    return loss, dx, reduced_small


_SMALL_SHARDED = [("ret_gn", 2), ("mla_norm", 1), ("mla_q_norm", 1), ("mla_kv_norm", 1), ("ffn_conv_w", 2)]
_SMALL_REPLICATED = ["ret_norm", "mla_q_head_norm", "mla_k_head_norm", "ffn_norm", "ffn_conv_b"]
_SMALL_ALL = ["ret_norm", "ret_gn", "mla_norm", "mla_q_norm", "mla_kv_norm", "mla_q_head_norm", "mla_k_head_norm",
              "ffn_norm", "ffn_conv_w", "ffn_conv_b"]


def _to_slots(full, axis):
    shape = full.shape
    split = shape[:axis] + (N_SHARD, shape[axis] // N_SHARD) + shape[axis + 1:]
    return jnp.moveaxis(full.reshape(split), axis, 0).reshape(N_SHARD, -1)


def _from_slots(slots, shard_shape, axis):
    parts = jnp.moveaxis(slots.reshape((N_SHARD,) + tuple(shard_shape)), 0, axis)
    full = shard_shape[:axis] + (N_SHARD * shard_shape[axis],) + shard_shape[axis + 1:]
    return parts.reshape(full)


def _pad_rows(flat, cols, row_unit):
    n, L = flat.shape
    unit = cols * row_unit
    Lp = -(-L // unit) * unit
    if Lp != L:
        flat = jnp.concatenate([flat, jnp.zeros((n, Lp - L), flat.dtype)], axis=1)
    return flat.reshape(n, Lp // cols, cols)


def _pad_heads(a, axis):
    shape = a.shape
    a = a.reshape(shape[:axis] + (MLA_HEADS, MLA_QK) + shape[axis + 1:])
    pad = [(0, 0)] * a.ndim
    pad[axis + 1] = (0, MLA_PAD - MLA_QK)
    return jnp.pad(a, pad).reshape(shape[:axis] + (MLA_HEADS * MLA_PAD,) + shape[axis + 1:])


def _unpad_heads(a, axis):
    shape = a.shape
    a = a.reshape(shape[:axis] + (MLA_HEADS, MLA_PAD) + shape[axis + 1:])
    a = lax.slice_in_dim(a, 0, MLA_QK, axis=axis + 1)
    return a.reshape(shape[:axis] + (MLA_HEADS * MLA_QK,) + shape[axis + 1:])


def kernel(x, ret_norm, ret_w_in, ret_gn, ret_w_out, mla_norm, mla_w_in, mla_q_norm, mla_w_qb, mla_kv_norm, mla_w_kvb, mla_q_head_norm, mla_k_head_norm, mla_w_out, ffn_norm, ffn_w_in, ffn_conv_w, ffn_conv_b, ffn_w_out, loss_target, m_ret_norm, m_ret_w_in, m_ret_gn, m_ret_w_out, m_mla_norm, m_mla_w_in, m_mla_q_norm, m_mla_w_qb, m_mla_kv_norm, m_mla_w_kvb, m_mla_q_head_norm, m_mla_k_head_norm, m_mla_w_out, m_ffn_norm, m_ffn_w_in, m_ffn_conv_w, m_ffn_conv_b, m_ffn_w_out, v_ret_norm, v_ret_w_in, v_ret_gn, v_ret_w_out, v_mla_norm, v_mla_w_in, v_mla_q_norm, v_mla_w_qb, v_mla_kv_norm, v_mla_w_kvb, v_mla_q_head_norm, v_mla_k_head_norm, v_mla_w_out, v_ffn_norm, v_ffn_w_in, v_ffn_conv_w, v_ffn_conv_b, v_ffn_w_out):
    names = ["ret_norm", "ret_w_in", "ret_gn", "ret_w_out", "mla_norm", "mla_w_in", "mla_q_norm", "mla_w_qb",
             "mla_kv_norm", "mla_w_kvb", "mla_q_head_norm", "mla_k_head_norm", "mla_w_out", "ffn_norm", "ffn_w_in",
             "ffn_conv_w", "ffn_conv_b", "ffn_w_out"]
    shard = dict(zip(names, [ret_norm, ret_w_in, ret_gn, ret_w_out, mla_norm, mla_w_in, mla_q_norm, mla_w_qb,
                             mla_kv_norm, mla_w_kvb, mla_q_head_norm, mla_k_head_norm, mla_w_out, ffn_norm, ffn_w_in,
                             ffn_conv_w, ffn_conv_b, ffn_w_out]))
    mom_m = dict(zip(names, [m_ret_norm, m_ret_w_in, m_ret_gn, m_ret_w_out, m_mla_norm, m_mla_w_in, m_mla_q_norm,
                             m_mla_w_qb, m_mla_kv_norm, m_mla_w_kvb, m_mla_q_head_norm, m_mla_k_head_norm, m_mla_w_out,
                             m_ffn_norm, m_ffn_w_in, m_ffn_conv_w, m_ffn_conv_b, m_ffn_w_out]))
    mom_v = dict(zip(names, [v_ret_norm, v_ret_w_in, v_ret_gn, v_ret_w_out, v_mla_norm, v_mla_w_in, v_mla_q_norm,
                             v_mla_w_qb, v_mla_kv_norm, v_mla_w_kvb, v_mla_q_head_norm, v_mla_k_head_norm, v_mla_w_out,
                             v_ffn_norm, v_ffn_w_in, v_ffn_conv_w, v_ffn_conv_b, v_ffn_w_out]))
    B, S, D = x.shape
    T = B * S
    sx, sy = lax.axis_index("x"), lax.axis_index("y")
    me = 2 * sx + sy

    two_d = lambda a: a.reshape(-1, a.shape[-1])
    small_sizes = [int(np.prod(shard[n].shape)) for n, _ in _SMALL_SHARDED]
    small = jnp.concatenate([shard[n].reshape(1, -1) for n, _ in _SMALL_SHARDED], axis=1)
    small = _pad_rows(small, LANES, 8)[0]
    as_mxu = lambda a: two_d(a).astype(BF16)
    zero = jnp.zeros((), jnp.int32)
    with_own = lambda gathered, own: lax.dynamic_update_slice(gathered, own[None], (me.astype(jnp.int32), zero, zero))
    by_cols = lambda a: jnp.moveaxis(a, 0, 1).reshape(a.shape[1], -1)
    by_rows = lambda a: a.reshape(-1, a.shape[-1])
    pad_in = lambda a: jnp.pad(by_rows(a), ((0, 0), (0, MLA_IN_PAD - MLA_IN)))
    pad_qb = lambda a: _pad_heads(by_cols(a), 1)
    ret_in_shard = as_mxu(shard["ret_w_in"])
    g_ret_in, gsmall = _all_gather_weights([ret_in_shard], small)
    later = [
        ("ret_out", [("ret_w_out", as_mxu(shard["ret_w_out"]), by_rows)]),
        ("ffn0", [("ffn_w_in0", as_mxu(shard["ffn_w_in"][0]), by_cols), ("ffn_w_out0", as_mxu(shard["ffn_w_out"][0]), by_rows)]),
        ("mla", [("mla_w_in", as_mxu(shard["mla_w_in"]), pad_in), ("mla_w_qb", as_mxu(shard["mla_w_qb"]), pad_qb),
                 ("mla_w_kvb", as_mxu(shard["mla_w_kvb"]), by_cols), ("mla_w_out", as_mxu(shard["mla_w_out"]), by_rows)]),
        ("ffn1", [("ffn_w_in1", as_mxu(shard["ffn_w_in"][1]), by_cols), ("ffn_w_out1", as_mxu(shard["ffn_w_out"][1]), by_rows)]),
    ]
    gathering = {}
    token = gsmall
    for group, items in later:
        shards = [s_ for _, s_, _ in items]
        lands = [lax.empty((N_SHARD,) + s_.shape, s_.dtype) for s_ in shards]
        send_sems, recv_sems, shards, lands, token = _exchange_start(
            _weight_copies, shards, lands, 3 * len(shards), f"weights_start_{group}", after=token)
        gathering[group] = (send_sems, recv_sems, shards, lands, items)

    def late(group, after):
        send_sems, recv_sems, shards, lands, items = gathering[group]
        shards, lands = _exchange_wait(_weight_copies, send_sems, recv_sems, shards, lands, after,
                                       f"weights_wait_{group}")
        return {key: full(with_own(l_, s_)) for (key, _, full), s_, l_ in zip(items, shards, lands)}

    gsmall = with_own(gsmall, small).reshape(N_SHARD, -1)
    wfull = {}
    off = 0
    for (n, ax), sz in zip(_SMALL_SHARDED, small_sizes):
        wfull[n] = _from_slots(gsmall[:, off:off + sz], shard[n].shape, ax)
        off += sz
    for n in _SMALL_REPLICATED:
        wfull[n] = shard[n]

    conv8 = jnp.concatenate([wfull["ffn_conv_w"], wfull["ffn_conv_b"][:, None, :],
                             jnp.zeros((2, 4, FFN_DIM), F32)], axis=1)
    w = {
        "started": token, "ret_norm": wfull["ret_norm"], "ret_w_in": by_cols(with_own(g_ret_in, ret_in_shard)),
        "ret_gn": wfull["ret_gn"].reshape(1, RET_HEADS * RET_V), "mla_norm": wfull["mla_norm"],
        "mla_q_norm": wfull["mla_q_norm"], "mla_kv_norm": wfull["mla_kv_norm"],
        "mla_q_head_norm": jnp.pad(wfull["mla_q_head_norm"], ((0, 0), (0, MLA_PAD - MLA_QK))),
        "mla_k_head_norm": jnp.pad(wfull["mla_k_head_norm"], ((0, 0), (0, MLA_PAD - MLA_QK))),
        "ffn_norm": wfull["ffn_norm"], "ffn_conv8": conv8,
    }

    started = {}

    def exchange(group, arrays):
        lands = [lax.empty((N_PEERS, p.shape[1] // 2, p.shape[2]), p.dtype) for p in arrays]
        send_sems, recv_sems, ps, lands, token = _exchange_start(
            _grad_copies, arrays, lands, N_PEERS * len(arrays), f"grads_start_{group}")
        started[group] = (send_sems, recv_sems, ps, lands)
        return token

    small_shapes = {
        "ret_norm": (1, D_MODEL), "ret_gn": (1, RET_HEADS, RET_V), "mla_norm": (1, D_MODEL),
        "mla_q_norm": (1, MLA_Q_RANK), "mla_kv_norm": (1, MLA_KV_RANK), "mla_q_head_norm": (1, MLA_QK),
        "mla_k_head_norm": (1, MLA_QK), "ffn_norm": (2, D_MODEL), "ffn_conv_w": (2, 3, FFN_DIM),
        "ffn_conv_b": (2, FFN_DIM)}

    def reduce_small(gl, loss_part):
        gl = dict(gl, mla_q_head_norm=gl["mla_q_head_norm"][:, :MLA_QK], mla_k_head_norm=gl["mla_k_head_norm"][:, :MLA_QK])
        packed = jnp.concatenate([gl[n].reshape(1, -1) for n in _SMALL_ALL] + [loss_part.reshape(1, 1)], axis=1)
        return _all_reduce_small(_pad_rows(packed, LANES, 8)[0])

    _, dx, gsm = _local_step(x.reshape(T, D), loss_target.reshape(T, D), w, B, S, late, exchange, reduce_small)

    delta, new_m, new_v, grads = {}, {}, {}, {}

    def reduced(group, after):
        send_sems, recv_sems, ps, lands = started[group]
        ps, lands = _exchange_wait(_grad_copies, send_sems, recv_sems, ps, lands, after, f"grads_wait_{group}")
        halves = [_sum_partials(p_, l_, f"grads_sum_{group}_{i}") for i, (p_, l_) in enumerate(zip(ps, lands))]
        return [two_d(r) for r in _sibling_share(halves, f"grads_share_{group}")]

    def adamw(n, g_):
        shp = shard[n].shape
        grads[n] = g_.reshape(shp)
        flat = lambda a: a.reshape(-1, shp[-1])
        d_, m_, v_ = _adamw(flat(shard[n]), flat(grads[n]), flat(mom_m[n]), flat(mom_v[n]), f"adamw_{n}")
        delta[n], new_m[n], new_v[n] = d_.reshape(shp), m_.reshape(shp), v_.reshape(shp)
        return d_

    ffn1 = reduced("ffn1", started["ret"][2][0])
    mla = reduced("mla", ffn1[0])
    ffn0 = reduced("ffn0", mla[0])
    reto = reduced("reto", ffn0[0])
    early = [adamw(n, g_) for n, g_ in zip(["mla_w_in", "mla_w_qb", "mla_w_kvb", "mla_w_out"], mla)]
    early.append(adamw("ffn_w_in", jnp.stack([ffn0[0], ffn1[0]])))
    early.append(adamw("ffn_w_out", jnp.stack([ffn0[1], ffn1[1]])))
    early.append(adamw("ret_w_out", reto[0]))
    ret = reduced("ret", jnp.stack([d_[0, 0] for d_ in early]))
    adamw("ret_w_in", ret[0])

    gsm = gsm.reshape(-1)
    sharded_axis = dict(_SMALL_SHARDED)
    off = 0
    for n in _SMALL_ALL:
        sz = int(np.prod(small_shapes[n]))
        gn = gsm[off:off + sz].reshape(small_shapes[n])
        off += sz
        if n in sharded_axis:
            ax = sharded_axis[n]
            width = shard[n].shape[ax]
            gn = lax.dynamic_slice_in_dim(gn, me * width, width, axis=ax)
        grads[n] = gn
    loss = gsm[off]

    pack_small = lambda d: _pad_rows(jnp.concatenate([d[n].reshape(1, -1) for n in _SMALL_ALL], axis=1), LANES, 8)[0]
    d_, m_, v_ = _adamw(pack_small(shard), pack_small(grads), pack_small(mom_m), pack_small(mom_v), "adamw_small")
    off = 0
    for n in _SMALL_ALL:
        sz = int(np.prod(shard[n].shape))
        for dst, src in ((delta, d_), (new_m, m_), (new_v, v_)):
            dst[n] = src.reshape(-1)[off:off + sz].reshape(shard[n].shape)
        off += sz

    return (loss, dx.reshape(B, S, D), *[grads[n] for n in names], *[delta[n] for n in names],
            *[new_m[n] for n in names], *[new_v[n] for n in names])
```

```python
import functools

import numpy as np
import jax
import jax.numpy as jnp
from jax import lax
from jax.experimental import pallas as pl
from jax.experimental.pallas import tpu as pltpu

F32 = jnp.float32
BF16 = jnp.bfloat16
MXU_DTYPE = jnp.bfloat16

CHUNK = 64
RMS_EPS = 1e-6
ROPE_THETA = 10000.0
D_MODEL = 1024
RET_HEADS = 4
RET_QK = 256
RET_V = 512
RET_GAMMA_BASE = -5.0
MLA_HEADS = 8
MLA_Q_RANK = 384
MLA_KV_RANK = 256
MLA_NOPE = 128
MLA_ROPE = 64
MLA_V = 128
MLA_QK = MLA_NOPE + MLA_ROPE
MLA_PAD = 256
MLA_IN = MLA_Q_RANK + MLA_KV_RANK + MLA_ROPE
MLA_IN_PAD = MLA_IN + 64
MASK_VALUE = -1e30
FFN_DIM = 2816
ADAM_LR = 0.001
ADAM_B1 = 0.9
ADAM_B2 = 0.999
ADAM_EPS = 1e-08
ADAM_WD = 0.01
ADAM_STEP = 10

LANES = 128
MLA_FWD_BLOCK = 512
VMEM_LIMIT = 56 * 2 ** 20
N_SHARD = 4
N_DEV = 8

MESH = pl.DeviceIdType.MESH


def _params(sem=None, **kw):
    return pltpu.CompilerParams(dimension_semantics=sem, vmem_limit_bytes=VMEM_LIMIT, **kw)


def _pick(dim, target):
    if dim <= target:
        return dim
    best = None
    for d in range(LANES, target + 1, LANES):
        if dim % d == 0:
            best = d
    assert best is not None, (dim, target)
    return best


def _mm(a, b, dims, out_dtype, name, residual=None, bm=512, bn=1024, bk=2048, out_slots=None, after=None,
        cols_outer=False):
    a_parts = list(a) if isinstance(a, (list, tuple)) else [a]
    b_parts = list(b) if isinstance(b, (list, tuple)) else [b]
    parts_on_n = dims == "tn" or len(b_parts) > 1
    if parts_on_n:
        assert len(a_parts) == 1 and dims in ("tn", "nn")
        (K, M) = a_parts[0].shape if dims == "tn" else a_parts[0].shape[::-1]
        N = sum(p.shape[1] for p in b_parts)
        part_widths = [p.shape[1] for p in b_parts]
    else:
        assert len(b_parts) == 1
        M = a_parts[0].shape[0]
        K = sum(p.shape[1] for p in a_parts)
        N = b_parts[0].shape[1 if dims == "nn" else 0]
        part_widths = [p.shape[1] for p in a_parts]
    bm, bn, bk = _pick(M, bm), _pick(N, bn), _pick(K, min(bk, 1024) if dims == "tn" else bk)
    nk = K // bk
    unit = bn if parts_on_n else bk
    assert all(wd % unit == 0 for wd in part_widths), (name, part_widths, unit)
    bounds = np.cumsum([0] + [wd // unit for wd in part_widths])
    ranges = [(int(lo), int(hi)) for lo, hi in zip(bounds[:-1], bounds[1:])]

    def part_index(idx, lo, hi):
        return jnp.clip(idx - lo, 0, hi - lo - 1)

    if parts_on_n:
        if dims == "tn":
            a_specs = [pl.BlockSpec((bk, bm), lambda i, j, k: (k, i))]
            dn = (((0,), (0,)), ((), ()))
        else:
            a_specs = [pl.BlockSpec((bm, bk), lambda i, j, k: (i, k))]
            dn = (((1,), (0,)), ((), ()))
        b_specs = [pl.BlockSpec((bk, bn), functools.partial(lambda i, j, k, lo, hi: (k, part_index(j, lo, hi)), lo=lo, hi=hi))
                   for lo, hi in ranges]
    else:
        a_specs = [pl.BlockSpec((bm, bk), functools.partial(lambda i, j, k, lo, hi: (i, part_index(k, lo, hi)), lo=lo, hi=hi))
                   for lo, hi in ranges]
        if dims == "nt":
            b_specs = [pl.BlockSpec((bn, bk), lambda i, j, k: (j, k))]
        else:
            b_specs = [pl.BlockSpec((bk, bn), lambda i, j, k: (k, j))]
        dn = (((1,), (1 if dims == "nt" else 0,)), ((), ()))
    r_spec = pl.BlockSpec((bm, bn), lambda i, j, k: (i, j))
    if out_slots is None:
        o_spec, o_shape = r_spec, (M, N)
    else:
        ns = N // out_slots
        assert ns % bn == 0, (name, ns, bn)
        nbs = ns // bn
        o_spec = pl.BlockSpec((None, bm, bn), lambda i, j, k: (j // nbs, i, j % nbs))
        o_shape = (out_slots, M, ns)
    has_res = residual is not None
    na, nb = len(a_parts), len(b_parts)

    def body(*refs):
        a_refs, b_refs = refs[:na], refs[na:na + nb]
        r_ref = refs[na + nb] if has_res else None
        n_in = na + nb + has_res + (after is not None)
        o_ref = refs[n_in]
        acc_ref = refs[n_in + 1] if nk > 1 else None
        k = pl.program_id(2)

        def finish(acc):
            if has_res:
                acc = acc + r_ref[...].astype(F32)
            o_ref[...] = acc.astype(out_dtype)

        def compute(a_ref, b_ref):
            p = lax.dot_general(a_ref[...].astype(MXU_DTYPE), b_ref[...].astype(MXU_DTYPE), dn,
                                preferred_element_type=F32)
            if nk == 1:
                finish(p)
                return

            @pl.when(k == 0)
            def _():
                acc_ref[...] = p

            @pl.when(jnp.logical_and(k > 0, k < nk - 1))
            def _():
                acc_ref[...] += p

            @pl.when(k == nk - 1)
            def _():
                finish(acc_ref[...] + p)

        if len(ranges) == 1:
            compute(a_refs[0], b_refs[0])
        else:
            idx = pl.program_id(0 if cols_outer else 1) if parts_on_n else k
            for p, (lo, hi) in enumerate(ranges):
                @pl.when(jnp.logical_and(idx >= lo, idx < hi))
                def _(p=p):
                    compute(a_refs[0 if parts_on_n else p], b_refs[p if parts_on_n else 0])

    after_specs = [] if after is None else [pl.BlockSpec(after.shape, lambda i, j, k: (0, 0))]
    in_specs = a_specs + b_specs + ([r_spec] if has_res else []) + after_specs
    grid = (M // bm, N // bn, nk)
    if cols_outer:
        swap = lambda sp: pl.BlockSpec(sp.block_shape, functools.partial(lambda j, i, k, f: f(i, j, k), f=sp.index_map))
        in_specs, o_spec, grid = [swap(sp) for sp in in_specs], swap(o_spec), (grid[1], grid[0], nk)
    return pl.pallas_call(
        body, name=name, grid=grid,
        in_specs=in_specs, out_specs=o_spec,
        out_shape=jax.ShapeDtypeStruct(o_shape, out_dtype),
        scratch_shapes=[pltpu.VMEM((bm, bn), F32)] if nk > 1 else [],
        compiler_params=_params(("parallel", "parallel", "arbitrary")),
    )(*a_parts, *b_parts, *((residual,) if has_res else ()), *(() if after is None else (after,)))


def _mm_out_norm(a, w, residual, gain, name, bm=512):
    (M, K), N = a.shape, w.shape[1]
    bm = _pick(M, bm)

    def body(a_ref, w_ref, r_ref, g_ref, o_ref, h_ref, ht_ref):
        acc = lax.dot_general(a_ref[...].astype(MXU_DTYPE), w_ref[...].astype(MXU_DTYPE), _NN,
                              preferred_element_type=F32) + r_ref[...]
        o_ref[...] = acc
        hv = _fn_rms([[acc]], [], [[g_ref[...]]])[0][0]
        h_ref[...] = hv.astype(h_ref.dtype)
        ht_ref[...] = hv.T.astype(ht_ref.dtype)

    row = pl.BlockSpec((bm, N), lambda i: (i, 0))
    whole = lambda arr: pl.BlockSpec(arr.shape, lambda i: (0, 0))
    return pl.pallas_call(
        body, name=name, grid=(M // bm,),
        in_specs=[pl.BlockSpec((bm, K), lambda i: (i, 0)), whole(w), row, whole(gain)],
        out_specs=[row, row, pl.BlockSpec((N, bm), lambda i: (0, i))],
        out_shape=[jax.ShapeDtypeStruct((M, N), F32), jax.ShapeDtypeStruct((M, N), BF16),
                   jax.ShapeDtypeStruct((N, M), BF16)],
        compiler_params=_params(("parallel",)),
    )(a, w, residual, gain)


def _mm_out_loss(a, w, residual, target, name, bm=512):
    (M, K), N = a.shape, w.shape[1]
    bm = _pick(M, bm)

    def body(a_ref, w_ref, r_ref, t_ref, dy_ref, dyc_ref, l_ref):
        y = lax.dot_general(a_ref[...].astype(MXU_DTYPE), w_ref[...].astype(MXU_DTYPE), _NN,
                            preferred_element_type=F32) + r_ref[...]
        err = y - t_ref[...]
        dy_ref[...] = err / N
        dyc_ref[...] = (err / N).astype(dyc_ref.dtype)
        part = jnp.full((8, LANES), 0.5 * jnp.sum(jnp.mean(err * err, axis=-1)), F32)

        @pl.when(pl.program_id(0) == 0)
        def _():
            l_ref[...] = part

        @pl.when(pl.program_id(0) > 0)
        def _():
            l_ref[...] += part

    row = pl.BlockSpec((bm, N), lambda i: (i, 0))
    dy, dyc, l = pl.pallas_call(
        body, name=name, grid=(M // bm,),
        in_specs=[pl.BlockSpec((bm, K), lambda i: (i, 0)), pl.BlockSpec(w.shape, lambda i: (0, 0)), row, row],
        out_specs=[row, row, pl.BlockSpec((8, LANES), lambda i: (0, 0))],
        out_shape=[jax.ShapeDtypeStruct((M, N), F32), jax.ShapeDtypeStruct((M, N), BF16),
                   jax.ShapeDtypeStruct((8, LANES), F32)],
        compiler_params=_params(("arbitrary",)),
    )(a, w, residual, target)
    return dy, dyc, l[0, 0]


def _mm_dx_norm(a_parts, w, x, gain, add, name, bm=256, after=None):
    M = a_parts[0].shape[0]
    N, K = w.shape
    widths = [p.shape[1] for p in a_parts]
    assert sum(widths) == K, (name, widths, K)
    offs = [int(o) for o in np.cumsum([0] + widths[:-1])]
    bm = _pick(M, bm)
    na = len(a_parts)
    n_in = na + 4 + (after is not None)

    def body(*refs):
        w_ref, x_ref, g_ref, add_ref = refs[na:na + 4]
        dx_ref, dxc_ref, dg_ref = refs[n_in:n_in + 3]
        dh = None
        for a_ref, off, wd in zip(refs[:na], offs, widths):
            p = lax.dot_general(a_ref[...].astype(MXU_DTYPE), w_ref[:, off:off + wd].astype(MXU_DTYPE), _NT,
                                preferred_element_type=F32)
            dh = p if dh is None else dh + p
        _, vjp = jax.vjp(lambda xv, gv: _fn_rms([[xv]], [], [[gv]])[0][0], x_ref[...], g_ref[...])
        dxv, dgv = vjp(dh)
        dxv = dxv + add_ref[...]
        dx_ref[...] = dxv
        dxc_ref[...] = dxv.astype(dxc_ref.dtype)

        @pl.when(pl.program_id(0) == 0)
        def _():
            dg_ref[...] = dgv

        @pl.when(pl.program_id(0) > 0)
        def _():
            dg_ref[...] += dgv

    row = pl.BlockSpec((bm, N), lambda i: (i, 0))
    whole = lambda a: pl.BlockSpec(a.shape, lambda i: (0, 0))
    in_specs = [pl.BlockSpec((bm, wd), lambda i: (i, 0)) for wd in widths] + [whole(w), row, whole(gain), row]
    in_specs += [] if after is None else [whole(after)]
    return pl.pallas_call(
        body, name=name, grid=(M // bm,),
        in_specs=in_specs, out_specs=[row, row, whole(gain)],
        out_shape=[jax.ShapeDtypeStruct((M, N), F32), jax.ShapeDtypeStruct((M, N), BF16),
                   jax.ShapeDtypeStruct(gain.shape, F32)],
        compiler_params=_params(("arbitrary",)),
    )(*a_parts, w, x, gain, add, *(() if after is None else (after,)))


def _tiles(ref, width, tile):
    return [ref[:, t * tile:(t + 1) * tile].astype(F32) for t in range(width // tile)]


def _row_specs(rows, pos, consts, bm, S):
    npos_blocks = S // bm
    specs = [pl.BlockSpec((bm, w), functools.partial(lambda i, c: (i, c), c=cb)) for (_, w, cb, _) in rows]
    specs += [pl.BlockSpec((bm, p.shape[1]), lambda i: (i % npos_blocks, 0)) for p in pos]
    specs += [pl.BlockSpec(c.shape, lambda i: (0, 0)) for (c, _) in consts]
    return specs


def _rowwise_fwd(fn, name, rows, pos, consts, outs, bm, S, transposed=()):
    T = rows[0][0].shape[0]
    nr, npos, nc, no = len(rows), len(pos), len(consts), len(outs)

    def body(*refs):
        row_v = [_tiles(r, w, t) for r, (_, w, _, t) in zip(refs[:nr], rows)]
        pos_v = [r[...] for r in refs[nr:nr + npos]]
        const_v = [_tiles(r, c.shape[1], t) for r, (c, t) in zip(refs[nr + npos:nr + npos + nc], consts)]
        res = fn(row_v, pos_v, const_v)
        out_refs = refs[nr + npos + nc:]
        for o_ref, tiles, (w, t, dt) in zip(out_refs, res, outs):
            for k, v in enumerate(tiles):
                o_ref[:, k * t:(k + 1) * t] = v.astype(dt)
        for t_ref, a in zip(out_refs[no:], transposed):
            t = outs[a][1]
            for k, v in enumerate(res[a]):
                t_ref[k * t:(k + 1) * t, :] = v.T.astype(t_ref.dtype)

    return pl.pallas_call(
        body, name=name, grid=(T // bm,),
        in_specs=_row_specs(rows, pos, consts, bm, S),
        out_specs=[pl.BlockSpec((bm, w), lambda i: (i, 0)) for (w, _, _) in outs]
        + [pl.BlockSpec((outs[a][0], bm), lambda i: (0, i)) for a in transposed],
        out_shape=[jax.ShapeDtypeStruct((T, w), dt) for (w, _, dt) in outs]
        + [jax.ShapeDtypeStruct((outs[a][0], T), BF16) for a in transposed],
        compiler_params=_params(("parallel",)),
    )(*[r[0] for r in rows], *pos, *[c[0] for c in consts])


def _rowwise_bwd(fn, name, rows, pos, consts, cts, bm, S, adds=None, grad_dtypes=None, mxu_copies=(), linear=False):
    adds = adds or {}
    T = rows[0][0].shape[0]
    nr, npos, nc, nct = len(rows), len(pos), len(consts), len(cts)
    add_idx = sorted(adds)
    grad_dtypes = grad_dtypes or [F32] * nr

    def body(*refs):
        it = iter(refs)
        row_refs = [None if linear else next(it) for _ in range(nr)]
        pos_refs = [next(it) for _ in range(npos)]
        const_refs = [next(it) for _ in range(nc)]
        ct_refs = [next(it) for _ in range(nct)]
        add_refs = {k: next(it) for k in add_idx}
        drow_refs = [next(it) for _ in range(nr)]
        copy_refs = {a: next(it) for a in mxu_copies}
        dconst_refs = [next(it) for _ in range(nc)]
        if linear:
            row_v = [[jnp.zeros((bm, t), F32)] * (w // t) for (_, w, _, t) in rows]
        else:
            row_v = [_tiles(r, w, t) for r, (_, w, _, t) in zip(row_refs, rows)]
        pos_v = [r[...] for r in pos_refs]
        const_v = [_tiles(r, c.shape[1], t) for r, (c, t) in zip(const_refs, consts)]
        ct_v = [_tiles(r, c.shape[1], t) for r, (c, t) in zip(ct_refs, cts)]
        _, vjp = jax.vjp(lambda rv, cv: fn(rv, pos_v, cv), row_v, const_v)
        drows, dconsts = vjp(ct_v)
        for a, (d_ref, tiles, (_, w, _, t)) in enumerate(zip(drow_refs, drows, rows)):
            for k, v in enumerate(tiles):
                if a in add_refs:
                    v = v + add_refs[a][:, k * t:(k + 1) * t].astype(F32)
                d_ref[:, k * t:(k + 1) * t] = v.astype(d_ref.dtype)
                if a in copy_refs:
                    copy_refs[a][:, k * t:(k + 1) * t] = v.astype(BF16)
        first = pl.program_id(0) == 0
        for d_ref, tiles, (_, t) in zip(dconst_refs, dconsts, consts):
            for k, v in enumerate(tiles):
                @pl.when(first)
                def _(d_ref=d_ref, k=k, t=t, v=v):
                    d_ref[:, k * t:(k + 1) * t] = v

                @pl.when(jnp.logical_not(first))
                def _(d_ref=d_ref, k=k, t=t, v=v):
                    d_ref[:, k * t:(k + 1) * t] += v

    in_specs = _row_specs([] if linear else rows, pos, consts, bm, S)
    in_specs += [pl.BlockSpec((bm, c.shape[1]), lambda i: (i, 0)) for (c, _) in cts]
    in_specs += [pl.BlockSpec((bm, adds[k].shape[1]), lambda i: (i, 0)) for k in add_idx]
    out_specs = [pl.BlockSpec((bm, w), lambda i: (i, 0)) for (_, w, _, _) in rows]
    out_specs += [pl.BlockSpec((bm, rows[a][1]), lambda i: (i, 0)) for a in mxu_copies]
    out_specs += [pl.BlockSpec(c.shape, lambda i: (0, 0)) for (c, _) in consts]
    out_shape = [jax.ShapeDtypeStruct((T, w), dt) for (_, w, _, _), dt in zip(rows, grad_dtypes)]
    out_shape += [jax.ShapeDtypeStruct((T, rows[a][1]), BF16) for a in mxu_copies]
    out_shape += [jax.ShapeDtypeStruct(c.shape, F32) for (c, _) in consts]
    res = pl.pallas_call(
        body, name=name, grid=(T // bm,),
        in_specs=in_specs, out_specs=out_specs, out_shape=out_shape,
        compiler_params=_params(("arbitrary",)),
    )(*([] if linear else [r[0] for r in rows]), *pos, *[c[0] for c in consts], *[c[0] for c in cts],
      *[adds[k] for k in add_idx])
    n_rows = nr + len(mxu_copies)
    return res[:n_rows], res[n_rows:]


def _ssq(tiles):
    s = jnp.sum(tiles[0] * tiles[0], axis=-1, keepdims=True)
    for t in tiles[1:]:
        s = s + jnp.sum(t * t, axis=-1, keepdims=True)
    return s


def _sigmoid(x):
    return 0.5 * jnp.tanh(0.5 * x) + 0.5


def _fn_rms(rows, pos, consts):
    (x,), (g,) = rows[0], consts[0]
    r = lax.rsqrt(jnp.mean(x * x, axis=-1, keepdims=True) + RMS_EPS)
    return [[x * r * g]]


def _fn_ret_rope(rows, pos, consts):
    (qkv,) = rows
    nq = RET_HEADS * RET_QK // LANES
    q, k, v = qkv[:nq], qkv[nq:2 * nq], qkv[2 * nq:]
    cos, sin = pos

    def rot(t, scale):
        out = []
        for h in range(RET_HEADS):
            x1, x2 = t[2 * h], t[2 * h + 1]
            o1, o2 = x1 * cos - x2 * sin, x2 * cos + x1 * sin
            out += [o1, o2] if scale is None else [o1 * scale, o2 * scale]
        return out

    return [rot(q, None), rot(k, RET_QK ** -0.5), list(v)]


def _fn_ret_gate(rows, pos, consts):
    o, g = rows
    (gn,) = consts
    out = []
    for h in range(RET_HEADS):
        r = lax.rsqrt(jnp.mean(o[h] * o[h], axis=-1, keepdims=True) + RMS_EPS)
        out.append((o[h] * r * gn[h]) * (g[h] * _sigmoid(g[h])))
    return [out]


def _fn_mla_lat(rows, pos, consts):
    (p,) = rows
    gq, gkv = consts
    nq, nkv = MLA_Q_RANK // LANES, MLA_KV_RANK // LANES
    cq, ckv, kr = p[:nq], p[nq:nq + nkv], p[nq + nkv]
    rq = lax.rsqrt(_ssq(cq) / MLA_Q_RANK + RMS_EPS)
    rkv = lax.rsqrt(_ssq(ckv) / MLA_KV_RANK + RMS_EPS)
    return [[t * rq * g for t, g in zip(cq, gq)], [t * rkv * g for t, g in zip(ckv, gkv)], [kr]]


def _swap32_impl(x):
    lane = lax.broadcasted_iota(jnp.int32, x.shape, 1)
    up, down = pltpu.roll(x, LANES - 32, 1), pltpu.roll(x, 32, 1)
    return jnp.where(lane < 32, up, jnp.where(lane < 64, down, 0.0))


@jax.custom_vjp
def _swap32(x):
    return _swap32_impl(x)


_swap32.defvjp(lambda x: (_swap32_impl(x), None), lambda _, g: (_swap32_impl(g),))


def _fn_mla_heads(rows, pos, consts):
    qf, kvf, (kr,) = rows
    cos, sin = pos
    gq, gk = consts
    q_out, k_out, v_out = [], [], []
    for h in range(MLA_HEADS):
        q0, q1 = qf[2 * h], qf[2 * h + 1]
        r = lax.rsqrt(_ssq([q0, q1]) / MLA_QK + RMS_EPS)
        a0, a1 = q0 * r * gq[0], q1 * r * gq[1]
        a1 = a1 * cos + _swap32(a1) * sin
        q_out += [a0 * (MLA_QK ** -0.5), a1 * (MLA_QK ** -0.5)]
        k0 = kvf[2 * h]
        r = lax.rsqrt(_ssq([k0, kr]) / MLA_QK + RMS_EPS)
        b0, b1 = k0 * r * gk[0], kr * r * gk[1]
        k_out += [b0, b1 * cos + _swap32(b1) * sin]
        v_out.append(kvf[2 * h + 1])
    return [q_out, k_out, v_out]


def _shift_down(x, n):
    row = lax.broadcasted_iota(jnp.int32, x.shape, 0)
    return jnp.where(row >= n, pltpu.roll(x, n, 0), 0.0)


def _shift_up(x, n):
    rows = x.shape[0]
    row = lax.broadcasted_iota(jnp.int32, x.shape, 0)
    return jnp.where(row < rows - n, pltpu.roll(x, rows - n, 0), 0.0)


def _conv_blocks(S):
    cb = 256
    return cb, FFN_DIM // cb


def _conv_fwd(ag, w8, B, S, name):
    cb, ncb = _conv_blocks(S)

    def body(a_ref, g_ref, w_ref, u_ref, ut_ref):
        g = g_ref[...].astype(F32)
        w = w_ref[...]
        gc = w[0:1] * _shift_down(g, 2) + w[1:2] * _shift_down(g, 1) + w[2:3] * g + w[3:4]
        u = a_ref[...].astype(F32) * (gc * _sigmoid(gc))
        u_ref[...] = u.astype(u_ref.dtype)
        ut_ref[...] = u.T.astype(ut_ref.dtype)

    return pl.pallas_call(
        body, name=name, grid=(ncb, B),
        in_specs=[pl.BlockSpec((S, cb), lambda j, b: (b, j)),
                  pl.BlockSpec((S, cb), lambda j, b: (b, ncb + j)),
                  pl.BlockSpec((8, cb), lambda j, b: (0, j))],
        out_specs=[pl.BlockSpec((S, cb), lambda j, b: (b, j)), pl.BlockSpec((cb, S), lambda j, b: (j, b))],
        out_shape=[jax.ShapeDtypeStruct((B * S, FFN_DIM), BF16), jax.ShapeDtypeStruct((FFN_DIM, B * S), BF16)],
        compiler_params=_params(("parallel", "parallel")),
    )(ag, ag, w8)


def _conv_bwd(ag, w8, du, B, S, name):
    cb, ncb = _conv_blocks(S)

    def body(a_ref, g_ref, w_ref, du_ref, da_ref, dg_ref, dw_ref):
        g = g_ref[...].astype(F32)
        w = w_ref[...]
        g1, g2 = _shift_down(g, 1), _shift_down(g, 2)
        gc = w[0:1] * g2 + w[1:2] * g1 + w[2:3] * g + w[3:4]
        sg = _sigmoid(gc)
        du_v = du_ref[...]
        da_ref[...] = (du_v * (gc * sg)).astype(da_ref.dtype)
        dgc = du_v * a_ref[...].astype(F32) * (sg * (1.0 + gc * (1.0 - sg)))
        dg = w[2:3] * dgc + w[1:2] * _shift_up(dgc, 1) + w[0:1] * _shift_up(dgc, 2)
        dg_ref[...] = dg.astype(dg_ref.dtype)
        part = jnp.concatenate([
            jnp.sum(dgc * g2, axis=0, keepdims=True), jnp.sum(dgc * g1, axis=0, keepdims=True),
            jnp.sum(dgc * g, axis=0, keepdims=True), jnp.sum(dgc, axis=0, keepdims=True),
            jnp.zeros((4, cb), F32)], axis=0)

        @pl.when(pl.program_id(1) == 0)
        def _():
            dw_ref[...] = part

        @pl.when(pl.program_id(1) > 0)
        def _():
            dw_ref[...] += part

    blk = lambda j, b: (b, j)
    return pl.pallas_call(
        body, name=name, grid=(ncb, B),
        in_specs=[pl.BlockSpec((S, cb), blk),
                  pl.BlockSpec((S, cb), lambda j, b: (b, ncb + j)),
                  pl.BlockSpec((8, cb), lambda j, b: (0, j)),
                  pl.BlockSpec((S, cb), blk)],
        out_specs=[pl.BlockSpec((S, cb), blk), pl.BlockSpec((S, cb), blk),
                   pl.BlockSpec((8, cb), lambda j, b: (0, j))],
        out_shape=[jax.ShapeDtypeStruct((B * S, FFN_DIM), BF16), jax.ShapeDtypeStruct((B * S, FFN_DIM), BF16),
                   jax.ShapeDtypeStruct((8, FFN_DIM), F32)],
        compiler_params=_params(("parallel", "arbitrary")),
    )(ag, ag, w8, du)


_NT = (((1,), (1,)), ((), ()))
_NN = (((1,), (0,)), ((), ()))
_TN = (((0,), (0,)), ((), ()))


def _dot(a, b, dn):
    return lax.dot_general(a.astype(MXU_DTYPE), b.astype(MXU_DTYPE), dn, preferred_element_type=F32)


def _run_bits(n):
    bits, b = [], 1
    while b < n:
        bits.append(b)
        b *= 2
    return bits[::-1]


def _key_runs(n, nq, update):
    for bit in _run_bits(nq + 1):
        @pl.when((n & bit) != 0)
        def _(bit=bit):
            update(n & ~(2 * bit - 1), bit, (n & (bit - 1)) == 0)


def _earlier_runs(n, nq, update):
    for bit in _run_bits(nq):
        @pl.when((n & bit) != 0)
        def _(bit=bit):
            update(n & ~(2 * bit - 1), bit, False)


def _chunk_visible(shape, nblk, blk):
    key = lax.broadcasted_iota(jnp.int32, shape, 0) - (nblk - 1) * blk
    query = lax.broadcasted_iota(jnp.int32, shape, 1)
    return jnp.logical_or(key < 0, (key // CHUNK) <= (query // CHUNK))


def _mla_attn_fwd(q, k, v, B, S):
    blk = min(MLA_FWD_BLOCK, S)
    H, nq = MLA_HEADS, S // blk

    def body(q_ref, k_ref, v_ref, o_ref, lse_ref, m_ref, l_ref, acc_ref):
        def qblock(i, _):
            q_rows = pl.ds(pl.multiple_of(i * blk, blk), blk)
            qi = q_ref[q_rows, :]
            m_ref[...] = jnp.full(m_ref.shape, MASK_VALUE, F32)
            l_ref[...] = jnp.zeros(l_ref.shape, F32)
            acc_ref[...] = jnp.zeros(acc_ref.shape, F32)

            def keys(first, nblk, last):
                rows = pl.ds(pl.multiple_of(first * blk, blk), nblk * blk)
                s = _dot(k_ref[rows, :], qi, _NT)
                s = jnp.where(jnp.logical_or(_chunk_visible(s.shape, nblk, blk), jnp.logical_not(last)), s, MASK_VALUE)
                m = m_ref[...]
                m2 = jnp.maximum(m, jnp.max(s, axis=0, keepdims=True))
                alpha = jnp.exp(m - m2)
                p = jnp.exp(s - m2)
                l_ref[...] = alpha * l_ref[...] + jnp.sum(p, axis=0, keepdims=True)
                acc_ref[...] = alpha * acc_ref[...] + _dot(v_ref[rows, :], p, _TN)
                m_ref[...] = m2

            _key_runs(i + 1, nq, keys)
            l = l_ref[...]
            o_ref[q_rows, :] = (acc_ref[...] / l).T
            lse_ref[0, :, q_rows] = m_ref[...] + jnp.log(l)
            return 0

        lax.fori_loop(0, nq, qblock, 0)

    return pl.pallas_call(
        body, name="mla_attn_fwd", grid=(B, H),
        in_specs=[pl.BlockSpec((S, MLA_PAD), lambda b, h: (b, h)),
                  pl.BlockSpec((S, MLA_PAD), lambda b, h: (b, h)),
                  pl.BlockSpec((S, MLA_V), lambda b, h: (b, h))],
        out_specs=[pl.BlockSpec((S, MLA_V), lambda b, h: (b, h)),
                   pl.BlockSpec((1, 1, S), lambda b, h: (b * H + h, 0, 0))],
        out_shape=[jax.ShapeDtypeStruct((B * S, H * MLA_V), F32), jax.ShapeDtypeStruct((B * H, 1, S), F32)],
        scratch_shapes=[pltpu.VMEM((1, blk), F32), pltpu.VMEM((1, blk), F32), pltpu.VMEM((MLA_V, blk), F32)],
        compiler_params=_params(("parallel", "parallel")),
    )(q, k, v)


def _mla_attn_bwd(q, k, v, o, do, lse, B, S):
    blk = min(MLA_FWD_BLOCK, S)
    H, nq = MLA_HEADS, S // blk

    def body(q_ref, k_ref, v_ref, o_ref, do_ref, lse_ref, dq_ref, dk_ref, dv_ref, kt_ref, dqt_ref):
        dk_ref[...] = jnp.zeros(dk_ref.shape, F32)
        dv_ref[...] = jnp.zeros(dv_ref.shape, F32)
        for g in range(nq):
            kt_ref[g] = k_ref[g * blk:(g + 1) * blk, :].T

        def qblock(i, _):
            q_rows = pl.ds(pl.multiple_of(i * blk, blk), blk)
            qi = q_ref[q_rows, :]
            doi = do_ref[q_rows, :]
            delta = jnp.sum((doi * o_ref[q_rows, :]).T, axis=0, keepdims=True)
            lse_i = lse_ref[0, :, q_rows]
            doi = doi.astype(MXU_DTYPE)
            dqt_ref[...] = jnp.zeros(dqt_ref.shape, F32)

            def keys(first, nblk, last):
                rows = pl.ds(pl.multiple_of(first * blk, blk), nblk * blk)
                k_run, v_run = k_ref[rows, :], v_ref[rows, :]
                p = jnp.exp(_dot(k_run, qi, _NT) - lse_i)
                p = jnp.where(jnp.logical_or(_chunk_visible(p.shape, nblk, blk), jnp.logical_not(last)), p, 0.0)
                ds = (p * (_dot(v_run, doi, _NT) - delta)).astype(MXU_DTYPE)
                dk_ref[rows, :] += _dot(ds, qi, _NN)
                dv_ref[rows, :] += _dot(p, doi, _NN)
                for r in range(nblk):
                    dqt_ref[...] += _dot(kt_ref[first + r], ds[r * blk:(r + 1) * blk, :], _NN)

            _key_runs(i + 1, nq, keys)
            dq_ref[q_rows, :] = dqt_ref[...].T
            return 0

        lax.fori_loop(0, nq, qblock, 0)

    qk_spec = pl.BlockSpec((S, MLA_PAD), lambda b, h: (b, h))
    v_spec = pl.BlockSpec((S, MLA_V), lambda b, h: (b, h))
    return pl.pallas_call(
        body, name="mla_attn_bwd", grid=(B, H),
        in_specs=[qk_spec, qk_spec, v_spec, v_spec, v_spec,
                  pl.BlockSpec((1, 1, S), lambda b, h: (b * H + h, 0, 0))],
        out_specs=[qk_spec, qk_spec, v_spec],
        out_shape=[jax.ShapeDtypeStruct((B * S, H * MLA_PAD), F32), jax.ShapeDtypeStruct((B * S, H * MLA_PAD), F32),
                   jax.ShapeDtypeStruct((B * S, H * MLA_V), F32)],
        scratch_shapes=[pltpu.VMEM((nq, MLA_PAD, blk), q.dtype), pltpu.VMEM((MLA_PAD, blk), F32)],
        compiler_params=_params(("parallel", "parallel")),
    )(q, k, v, o, do, lse)


def _ret_log_gamma():
    lg = np.log1p(-np.exp2(RET_GAMMA_BASE - np.arange(RET_HEADS, dtype=np.float32))).astype(np.float32)
    return jnp.asarray(np.broadcast_to(lg[:, None, None], (RET_HEADS, 8, LANES)).copy())


RET_BLOCK = 512


def _ret_local_scale(lg, shape, blk, rising):
    local = lax.broadcasted_iota(jnp.int32, shape, 0) % blk
    return jnp.exp(lg * (local if rising else blk - 1 - local).astype(F32))


def _ret_pair_factor(lg, blk, steps):
    return jnp.exp(lg * (blk * (steps - 1) + 1).astype(F32))


def _ret_own_decay(lg, blk, transposed):
    a = lax.broadcasted_iota(jnp.int32, (blk, blk), 0)
    b = lax.broadcasted_iota(jnp.int32, (blk, blk), 1)
    query, key = (b, a) if transposed else (a, b)
    dec = jnp.exp(lg * jnp.abs(query - key).astype(F32))
    return jnp.where((key // CHUNK) <= (query // CHUNK), dec, 0.0)


def _ret_attn_fwd(q, k, v, B, S):
    blk = min(RET_BLOCK, S)
    H, nq = RET_HEADS, S // blk

    def body(lg_ref, q_ref, k_ref, v_ref, o_ref, ks_ref, dec_ref, acc_ref):
        lg = lg_ref[0, 0:1, 0:1]
        ks_ref[...] = (k_ref[...].astype(F32) * _ret_local_scale(lg, k_ref.shape, blk, False)).astype(ks_ref.dtype)
        dec_ref[...] = _ret_own_decay(lg, blk, False)

        def qblock(i, _):
            q_rows = pl.ds(pl.multiple_of(i * blk, blk), blk)
            qi = q_ref[q_rows, :]
            qs = (qi.astype(F32) * _ret_local_scale(lg, qi.shape, blk, True)).astype(qi.dtype)
            a = _dot(qi, k_ref[q_rows, :], _NT) * dec_ref[...]
            acc_ref[...] = _dot(a, v_ref[q_rows, :], _NN)

            def keys(first, nblk, _):
                rows = pl.ds(pl.multiple_of(first * blk, blk), nblk * blk)
                steps = i - first - lax.broadcasted_iota(jnp.int32, (1, nblk * blk), 1) // blk
                a = _dot(qs, ks_ref[rows, :], _NT) * _ret_pair_factor(lg, blk, steps)
                acc_ref[...] += _dot(a, v_ref[rows, :], _NN)

            _earlier_runs(i, nq, keys)
            o_ref[q_rows, :] = acc_ref[...]
            return 0

        lax.fori_loop(0, nq, qblock, 0)

    qk_spec = pl.BlockSpec((S, RET_QK), lambda b, h: (b, h))
    v_spec = pl.BlockSpec((S, RET_V), lambda b, h: (b, h))
    return pl.pallas_call(
        body, name="ret_attn_fwd", grid=(B, H),
        in_specs=[pl.BlockSpec((1, 8, LANES), lambda b, h: (h, 0, 0)), qk_spec, qk_spec, v_spec],
        out_specs=v_spec,
        out_shape=jax.ShapeDtypeStruct((B * S, H * RET_V), F32),
        scratch_shapes=[pltpu.VMEM((S, RET_QK), k.dtype), pltpu.VMEM((blk, blk), F32), pltpu.VMEM((blk, RET_V), F32)],
        compiler_params=_params(("parallel", "parallel")),
    )(_ret_log_gamma(), q, k, v)


def _ret_attn_bwd(q, k, v, do, B, S):
    blk = min(RET_BLOCK, S)
    H, nq = RET_HEADS, S // blk

    def body(lg_ref, q_ref, k_ref, v_ref, do_ref, dq_ref, dk_ref, dv_ref, ks_ref, kst_ref, dks_ref, dqt_ref, dec_ref):
        lg = lg_ref[0, 0:1, 0:1]
        dk_ref[...] = jnp.zeros(dk_ref.shape, F32)
        dv_ref[...] = jnp.zeros(dv_ref.shape, F32)
        dks_ref[...] = jnp.zeros(dks_ref.shape, F32)
        ks_ref[...] = (k_ref[...].astype(F32) * _ret_local_scale(lg, k_ref.shape, blk, False)).astype(ks_ref.dtype)
        for g in range(nq):
            kst_ref[g] = ks_ref[g * blk:(g + 1) * blk, :].T
        dec_ref[...] = _ret_own_decay(lg, blk, True)

        def qblock(i, _):
            q_rows = pl.ds(pl.multiple_of(i * blk, blk), blk)
            qi = q_ref[q_rows, :]
            q_scale = _ret_local_scale(lg, qi.shape, blk, True)
            qs = (qi.astype(F32) * q_scale).astype(qi.dtype)
            doi = do_ref[q_rows, :].astype(MXU_DTYPE)
            ki = k_ref[q_rows, :]
            dec = dec_ref[...]
            a = _dot(ki, qi, _NT) * dec
            da = (_dot(v_ref[q_rows, :], doi, _NT) * dec).astype(MXU_DTYPE)
            dv_ref[q_rows, :] += _dot(a, doi, _NN)
            dk_ref[q_rows, :] += _dot(da, qi, _NN)
            dq_own = _dot(da, ki, _TN)
            dqt_ref[...] = jnp.zeros(dqt_ref.shape, F32)

            def keys(first, nblk, _):
                for r in range(nblk):
                    g = first + r
                    rows = pl.ds(pl.multiple_of(g * blk, blk), blk)
                    c = _ret_pair_factor(lg, blk, i - g)
                    a = _dot(ks_ref[rows, :], qs, _NT) * c
                    da = (_dot(v_ref[rows, :], doi, _NT) * c).astype(MXU_DTYPE)
                    dv_ref[rows, :] += _dot(a, doi, _NN)
                    dks_ref[rows, :] += _dot(da, qs, _NN)
                    dqt_ref[...] += _dot(kst_ref[g], da, _NN)

            _earlier_runs(i, nq, keys)
            dq_ref[q_rows, :] = dqt_ref[...].T * q_scale + dq_own
            return 0

        lax.fori_loop(0, nq, qblock, 0)
        dk_ref[...] += dks_ref[...] * _ret_local_scale(lg, dks_ref.shape, blk, False)

    qk_spec = pl.BlockSpec((S, RET_QK), lambda b, h: (b, h))
    v_spec = pl.BlockSpec((S, RET_V), lambda b, h: (b, h))
    return pl.pallas_call(
        body, name="ret_attn_bwd", grid=(B, H),
        in_specs=[pl.BlockSpec((1, 8, LANES), lambda b, h: (h, 0, 0)), qk_spec, qk_spec, v_spec, v_spec],
        out_specs=[qk_spec, qk_spec, v_spec],
        out_shape=[jax.ShapeDtypeStruct((B * S, H * RET_QK), F32), jax.ShapeDtypeStruct((B * S, H * RET_QK), F32),
                   jax.ShapeDtypeStruct((B * S, H * RET_V), F32)],
        scratch_shapes=[pltpu.VMEM((S, RET_QK), k.dtype), pltpu.VMEM((nq, RET_QK, blk), k.dtype),
                        pltpu.VMEM((S, RET_QK), F32), pltpu.VMEM((RET_QK, blk), F32), pltpu.VMEM((blk, blk), F32)],
        compiler_params=_params(("parallel", "parallel")),
    )(_ret_log_gamma(), q, k, v, do)


def _adamw(w, g, m, v, name):
    R, C = w.shape
    br = R if R * C * 4 <= 2 ** 21 else _pick_rows(R, max(8, (2 ** 21) // (C * 4)))

    def body(w_ref, g_ref, m_ref, v_ref, d_ref, mo_ref, vo_ref):
        g_v = g_ref[...]
        m_v = ADAM_B1 * m_ref[...] + (1.0 - ADAM_B1) * g_v
        v_v = ADAM_B2 * v_ref[...] + (1.0 - ADAM_B2) * (g_v * g_v)
        m_hat = m_v / (1.0 - ADAM_B1 ** ADAM_STEP)
        v_hat = v_v / (1.0 - ADAM_B2 ** ADAM_STEP)
        d_ref[...] = -ADAM_LR * (m_hat / (jnp.sqrt(v_hat) + ADAM_EPS) + ADAM_WD * w_ref[...])
        mo_ref[...] = m_v
        vo_ref[...] = v_v

    blk = pl.BlockSpec((br, C), lambda i: (i, 0))
    return pl.pallas_call(
        body, name=name, grid=(R // br,),
        in_specs=[blk] * 4, out_specs=[blk] * 3,
        out_shape=[jax.ShapeDtypeStruct((R, C), F32)] * 3,
        compiler_params=_params(("parallel",)),
    )(w, g, m, v)


def _pick_rows(R, target):
    best = None
    for d in range(8, min(R, target) + 1, 8):
        if R % d == 0:
            best = d
    assert best is not None, (R, target)
    return best


def _position():
    return lax.axis_index("x"), lax.axis_index("y"), lax.axis_index("c")


HBM_SPEC = pl.BlockSpec(memory_space=pltpu.HBM)


def _other_chips(x, y):
    return [(1 - x, y), (x, 1 - y), (1 - x, 1 - y)]


def _all_gather_weights(bigs, small):
    nb = len(bigs)

    def body(*refs):
        big_refs, small_ref = refs[:nb], refs[nb]
        obig, osmall = refs[nb + 1:2 * nb + 1], refs[2 * nb + 1]
        ici_send, ici_recv, d2d_send, d2d_recv, sm_send, sm_recv = refs[2 * nb + 2:]
        x, y, c = _position()
        me = 2 * x + y
        chips = _other_chips(x, y)

        def rows(n, half):
            rh = bigs[n].shape[0] // 2
            return pl.ds(half * rh, rh)

        def over_ici(n, j, slot, from_shard):
            px, py = chips[j]
            dst = obig[n].at[slot, rows(n, c)]
            return pltpu.make_async_remote_copy(
                src_ref=big_refs[n].at[rows(n, c)] if from_shard else dst, dst_ref=dst,
                send_sem=ici_send.at[3 * n + j], recv_sem=ici_recv.at[3 * n + j],
                device_id=(px, py, c), device_id_type=MESH)

        def over_d2d(n, j, half):
            px, py = chips[j]
            part = obig[n].at[2 * px + py, rows(n, half)]
            return pltpu.make_async_remote_copy(
                src_ref=part, dst_ref=part, send_sem=d2d_send.at[3 * n + j], recv_sem=d2d_recv.at[3 * n + j],
                device_id=(x, y, 1 - c), device_id_type=MESH)

        def small_copy(j, slot):
            px, py = chips[j]
            return pltpu.make_async_remote_copy(
                src_ref=small_ref, dst_ref=osmall.at[slot], send_sem=sm_send.at[j], recv_sem=sm_recv.at[j],
                device_id=(px, py, c), device_id_type=MESH)

        sends = [over_ici(n, j, me, True) for n in range(nb) for j in range(3)]
        sends += [small_copy(j, me) for j in range(3)]
        for cp in sends:
            cp.start()
        passed = []
        for n in range(nb):
            for j, (px, py) in enumerate(chips):
                over_ici(n, j, 2 * px + py, False).wait_recv()
                fwd = over_d2d(n, j, c)
                fwd.start()
                passed.append(fwd)
        for n in range(nb):
            for j in range(3):
                over_d2d(n, j, 1 - c).wait_recv()
        for j, (px, py) in enumerate(chips):
            small_copy(j, 2 * px + py).wait_recv()
        for cp in sends + passed:
            cp.wait_send()

    dma = pltpu.SemaphoreType.DMA
    return pl.pallas_call(
        body, name="weights_all_gather",
        in_specs=[HBM_SPEC] * (nb + 1), out_specs=[HBM_SPEC] * (nb + 1),
        out_shape=[jax.ShapeDtypeStruct((N_SHARD,) + b.shape, b.dtype) for b in bigs]
        + [jax.ShapeDtypeStruct((N_SHARD,) + small.shape, small.dtype)],
        scratch_shapes=[dma((3 * nb,)), dma((3 * nb,)), dma((3 * nb,)), dma((3 * nb,)), dma((3,)), dma((3,))],
    )(*bigs, small)


SEM_SPEC = pl.BlockSpec(memory_space=pltpu.SEMAPHORE)
DATAFLOW_EFFECT = pltpu.SideEffectType.DATAFLOW_SIDE_EFFECTING
N_PEERS = N_DEV - 1


def _grad_copies(p_refs, land_refs, send_sems, recv_sems):
    x, y, c = _position()
    copies = []
    for a, (p_ref, land_ref) in enumerate(zip(p_refs, land_refs)):
        rh = p_ref.shape[1] // 2
        for k in range(1, N_DEV):
            px = 1 - x if k & 4 else x
            py = 1 - y if k & 2 else y
            pc = 1 - c if k & 1 else c
            copies.append(pltpu.make_async_remote_copy(
                src_ref=p_ref.at[2 * px + py, pl.ds(pc * rh, rh)], dst_ref=land_ref.at[k - 1],
                send_sem=send_sems.at[N_PEERS * a + k - 1], recv_sem=recv_sems.at[N_PEERS * a + k - 1],
                device_id=(px, py, pc), device_id_type=MESH))
    return copies


def _weight_copies(w_refs, land_refs, send_sems, recv_sems):
    x, y, c = _position()
    copies = []
    for a, (w_ref, land_ref) in enumerate(zip(w_refs, land_refs)):
        for j, (px, py) in enumerate(_other_chips(x, y)):
            copies.append(pltpu.make_async_remote_copy(
                src_ref=w_ref, dst_ref=land_ref.at[2 * x + y], send_sem=send_sems.at[3 * a + j],
                recv_sem=recv_sems.at[3 * a + j], device_id=(px, py, c), device_id_type=MESH))
    return copies


def _exchange_start(make_copies, srcs, lands, n_sems, name, after=None):
    n, m = len(srcs), len(lands)
    n_in = n + m + (after is not None)

    def body(*refs):
        send_sems, recv_sems, token = refs[n_in], refs[n_in + 1], refs[-1]
        for cp in make_copies(refs[:n], refs[n:n + m], send_sems, recv_sems):
            cp.start()
        token[...] = jnp.zeros(token.shape, token.dtype)

    hbm = lambda a: pltpu.with_memory_space_constraint(a, pltpu.HBM)
    dma = pltpu.SemaphoreType.DMA
    res = pl.pallas_call(
        body, name=name,
        in_specs=[HBM_SPEC] * (n + m) + ([] if after is None else [pl.BlockSpec(memory_space=pl.ANY)]),
        out_specs=[SEM_SPEC, SEM_SPEC] + [HBM_SPEC] * (n + m) + [pl.BlockSpec(memory_space=pltpu.VMEM)],
        out_shape=[dma((n_sems,)), dma((n_sems,))] + [pltpu.HBM(a.shape, a.dtype) for a in list(srcs) + list(lands)]
        + [jax.ShapeDtypeStruct((8, LANES), F32)],
        input_output_aliases={i: 2 + i for i in range(n + m)},
        compiler_params=pltpu.CompilerParams(has_side_effects=DATAFLOW_EFFECT),
    )(*[hbm(a) for a in srcs], *[hbm(a) for a in lands], *(() if after is None else (after,)))
    return res[0], res[1], list(res[2:2 + n]), list(res[2 + n:2 + n + m]), res[-1]


def _exchange_wait(make_copies, send_sems, recv_sems, srcs, lands, after, name):
    n, m = len(srcs), len(lands)

    def body(*refs):
        for cp in make_copies(refs[:n], refs[n:n + m], refs[n + m], refs[n + m + 1]):
            cp.wait_send()
            cp.wait_recv()

    res = pl.pallas_call(
        body, name=name,
        in_specs=[HBM_SPEC] * (n + m) + [SEM_SPEC, SEM_SPEC, pl.BlockSpec(memory_space=pl.ANY)],
        out_specs=[HBM_SPEC] * (n + m),
        out_shape=[pltpu.HBM(a.shape, a.dtype) for a in list(srcs) + list(lands)],
        input_output_aliases={i: i for i in range(n + m)},
        compiler_params=pltpu.CompilerParams(has_side_effects=DATAFLOW_EFFECT),
    )(*srcs, *lands, send_sems, recv_sems, after)
    return list(res[:n]), list(res[n:])


def _sum_partials(p, land, name):
    _, rh, cols = land.shape
    br = _pick_rows(rh, 256)
    nrb = rh // br
    x, y, c = _position()
    where = jnp.stack([2 * x + y, c]).astype(jnp.int32)

    def body(where_ref, p_ref, land_ref, o_ref):
        acc = p_ref[...].astype(F32)
        for k in range(N_PEERS):
            acc = acc + land_ref[k].astype(F32)
        o_ref[...] = acc

    return pl.pallas_call(
        body, name=name,
        grid_spec=pltpu.PrefetchScalarGridSpec(
            num_scalar_prefetch=1, grid=(nrb,),
            in_specs=[pl.BlockSpec((None, br, cols), lambda r, where_ref: (where_ref[0], where_ref[1] * nrb + r, 0)),
                      pl.BlockSpec((N_PEERS, br, cols), lambda r, where_ref: (0, r, 0))],
            out_specs=pl.BlockSpec((None, br, cols), lambda r, where_ref: (where_ref[1], r, 0))),
        out_shape=jax.ShapeDtypeStruct((2, rh, cols), F32),
        compiler_params=_params(("parallel",)),
    )(where, p, land)


def _sibling_share(fulls, name):
    n = len(fulls)

    def body(*refs):
        o_refs = refs[n:2 * n]
        send_sems, recv_sems = refs[2 * n:]
        x, y, c = _position()

        def copy(a, half):
            return pltpu.make_async_remote_copy(
                src_ref=o_refs[a].at[half], dst_ref=o_refs[a].at[half], send_sem=send_sems.at[a],
                recv_sem=recv_sems.at[a], device_id=(x, y, 1 - c), device_id_type=MESH)

        sends = [copy(a, c) for a in range(n)]
        for cp in sends:
            cp.start()
        for a in range(n):
            copy(a, 1 - c).wait_recv()
        for cp in sends:
            cp.wait_send()

    dma = pltpu.SemaphoreType.DMA
    return pl.pallas_call(
        body, name=name,
        in_specs=[HBM_SPEC] * n, out_specs=[HBM_SPEC] * n,
        out_shape=[jax.ShapeDtypeStruct(f.shape, f.dtype) for f in fulls],
        input_output_aliases={a: a for a in range(n)},
        scratch_shapes=[dma((n,)), dma((n,))],
    )(*fulls)


def _all_reduce_small(v):
    R, cols = v.shape

    def body(v_ref, o_ref, buf_ref, send_sems, recv_sems):
        x, y, c = _position()
        me = 4 * x + 2 * y + c
        buf_ref[me] = v_ref[...]
        sends = []
        for k in range(1, N_DEV):
            px = 1 - x if k & 4 else x
            py = 1 - y if k & 2 else y
            pc = 1 - c if k & 1 else c
            sends.append(pltpu.make_async_remote_copy(
                src_ref=v_ref, dst_ref=buf_ref.at[me], send_sem=send_sems.at[k - 1], recv_sem=recv_sems.at[k - 1],
                device_id=(px, py, pc), device_id_type=MESH))
        for cp in sends:
            cp.start()
        for k in range(1, N_DEV):
            px = 1 - x if k & 4 else x
            py = 1 - y if k & 2 else y
            pc = 1 - c if k & 1 else c
            pltpu.make_async_remote_copy(
                src_ref=v_ref, dst_ref=buf_ref.at[4 * px + 2 * py + pc], send_sem=send_sems.at[k - 1],
                recv_sem=recv_sems.at[k - 1], device_id=(px, py, pc), device_id_type=MESH).wait_recv()
        for cp in sends:
            cp.wait_send()
        acc = buf_ref[0]
        for d in range(1, N_DEV):
            acc = acc + buf_ref[d]
        o_ref[...] = acc

    return pl.pallas_call(
        body, name="small_grads_all_reduce",
        in_specs=[pl.BlockSpec(memory_space=pltpu.VMEM)], out_specs=pl.BlockSpec(memory_space=pltpu.VMEM),
        out_shape=jax.ShapeDtypeStruct((R, cols), F32),
        scratch_shapes=[pltpu.VMEM((N_DEV, R, cols), F32), pltpu.SemaphoreType.DMA((N_DEV - 1,)),
                        pltpu.SemaphoreType.DMA((N_DEV - 1,))],
    )(v)


def _rope_tables(S, half, width):
    inv_freq = ROPE_THETA ** (-jnp.arange(half, dtype=F32) / half)
    ang = jnp.arange(S).astype(F32)[:, None] * inv_freq[None, :]
    return jnp.cos(ang), jnp.sin(ang)


def _slot_rows(a):
    return a.reshape(N_SHARD, -1, a.shape[-1])


def _local_step(x, target, w, B, S, late, exchange, reduce_small):
    T = B * S
    D = D_MODEL
    bm = min(512, S)
    full = lambda a, wd, tile=None: (a, wd, 0, tile or wd)
    g = {}

    cos_r, sin_r = _rope_tables(S, RET_QK // 2, LANES)
    cos_m, sin_m = _rope_tables(S, MLA_ROPE // 2, LANES)
    zeros64 = jnp.zeros((S, 64), F32)
    cos_m = jnp.concatenate([cos_m, cos_m, zeros64], axis=1)
    sin_m = jnp.concatenate([-sin_m, sin_m, zeros64], axis=1)

    def ffn_fwd(xin, h, ht, i, next_gain):
        w.update(late(f"ffn{i}", xin))
        norm = w["ffn_norm"][i:i + 1]
        ag = _mm(h, w[f"ffn_w_in{i}"], "nn", BF16, f"ffn{i}_in", bm=1024, bn=1408, cols_outer=True)
        u, ut = _conv_fwd(ag, w["ffn_conv8"][i], B, S, f"ffn{i}_conv")
        if next_gain is None:
            out = _mm_out_loss(u, w[f"ffn_w_out{i}"], xin, target, f"ffn{i}_out")
        else:
            out = _mm_out_norm(u, w[f"ffn_w_out{i}"], xin, next_gain, f"ffn{i}_out")
        return out, (xin, norm, ht, ag, ut)

    def ffn_bwd(dxout, dxout_c, saved, i):
        xin, norm, ht, ag, ut = saved
        du = _mm(dxout_c, w[f"ffn_w_out{i}"], "nt", F32, f"ffn{i}_out_dx", bm=1024, bn=1408, cols_outer=True)
        g_w_out = _mm(ut, dxout_c, "nn", BF16, f"ffn{i}_out_dw", bm=1408, bn=512, bk=T)
        da, dg, dw8 = _conv_bwd(ag, w["ffn_conv8"][i], du, B, S, f"ffn{i}_conv_bwd")
        g_w_in = _mm(ht, [da, dg], "nn", BF16, f"ffn{i}_in_dw", bm=1024, bn=1408, bk=T // 2, out_slots=N_SHARD)
        token = exchange(f"ffn{i}", [g_w_in, _slot_rows(g_w_out)])
        dxin, dxin_c, g_norm = _mm_dx_norm([da, dg], w[f"ffn_w_in{i}"], xin, norm, dxout, f"ffn{i}_in_dx", after=token)
        return dxin, dxin_c, (g_norm, dw8)

    h0, h0t = _rowwise_fwd(_fn_rms, "ret_norm", [full(x, D)], [], [(w["ret_norm"], D)], [(D, D, BF16)], bm, S,
                           transposed=(0,))
    proj = _mm(h0, w["ret_w_in"], "nn", BF16, "ret_in", bm=1024, after=w["started"], cols_outer=True)
    HQ, HV = RET_HEADS * RET_QK, RET_HEADS * RET_V
    rope_rows = [(proj, 2 * HQ + HV, 0, LANES)]
    q_r, k_r, v_r = _rowwise_fwd(_fn_ret_rope, "ret_rope", rope_rows, [cos_r, sin_r], [],
                                 [(HQ, LANES, BF16), (HQ, LANES, BF16), (HV, LANES, BF16)], bm, S)
    ret_o = _ret_attn_fwd(q_r, k_r, v_r, B, S)
    gate_rows = [full(ret_o, HV, RET_V), (proj, HV, 2, RET_V)]
    y0, y0t = _rowwise_fwd(_fn_ret_gate, "ret_gate", gate_rows, [], [(w["ret_gn"], RET_V)], [(HV, RET_V, BF16)], bm, S,
                           transposed=(0,))
    w.update(late("ret_out", y0))
    x1, h1, h1t = _mm_out_norm(y0, w["ret_w_out"], x, w["ffn_norm"][0:1], "ret_out")
    (x2, h2, _), ffn0_saved = ffn_fwd(x1, h1, h1t, 0, w["mla_norm"])

    w.update(late("mla", x2))
    proj2 = _mm(h2, w["mla_w_in"], "nn", F32, "mla_in", bm=2048)
    lat_consts = [(w["mla_q_norm"], LANES), (w["mla_kv_norm"], LANES)]
    cqn, ckvn, kr = _rowwise_fwd(_fn_mla_lat, "mla_latent_norm", [full(proj2, MLA_IN_PAD, LANES)], [], lat_consts,
                                 [(MLA_Q_RANK, LANES, BF16), (MLA_KV_RANK, LANES, BF16), (LANES, LANES, F32)], bm, S)
    qf = _mm(cqn, w["mla_w_qb"], "nn", BF16, "mla_qb", bm=2048, bn=2048)
    kvf = _mm(ckvn, w["mla_w_kvb"], "nn", BF16, "mla_kvb", bm=2048, bn=2048)
    HP, HVm = MLA_HEADS * MLA_PAD, MLA_HEADS * MLA_V
    head_rows = [full(qf, HP, LANES), full(kvf, HP, LANES), full(kr, LANES)]
    head_consts = [(w["mla_q_head_norm"], LANES), (w["mla_k_head_norm"], LANES)]
    q_a, k_a, v_a = _rowwise_fwd(_fn_mla_heads, "mla_heads", head_rows, [cos_m, sin_m], head_consts,
                                 [(HP, LANES, BF16), (HP, LANES, BF16), (HVm, LANES, BF16)], bm, S)
    att_o, lse = _mla_attn_fwd(q_a, k_a, v_a, B, S)
    x3, h3, h3t = _mm_out_norm(att_o, w["mla_w_out"], x2, w["ffn_norm"][1:2], "mla_out")
    (dy, dy_c, loss), ffn1_saved = ffn_fwd(x3, h3, h3t, 1, None)

    dx3, dx3_c, (g_n1, dw8_1) = ffn_bwd(dy, dy_c, ffn1_saved, 1)

    d_att_o = _mm(dx3_c, w["mla_w_out"], "nt", F32, "mla_out_dx", bm=2048)
    g_mla_out = _mm(att_o, dx3_c, "tn", BF16, "mla_out_dw")
    dq_a, dk_a, dv_a = _mla_attn_bwd(q_a, k_a, v_a, att_o, d_att_o, lse, B, S)
    (dqf, dkvf, dkr), (g["mla_q_head_norm"], g["mla_k_head_norm"]) = _rowwise_bwd(
        _fn_mla_heads, "mla_heads_bwd", head_rows, [cos_m, sin_m], head_consts,
        [(dq_a, LANES), (dk_a, LANES), (dv_a, LANES)], bm, S, grad_dtypes=[BF16, BF16, F32])
    dcqn = _mm(dqf, w["mla_w_qb"], "nt", F32, "mla_qb_dx", bm=2048)
    g_qb = _mm(cqn, dqf, "tn", BF16, "mla_qb_dw")
    g_qb = _to_slots(_unpad_heads(g_qb, 1), 1).reshape(N_SHARD, MLA_Q_RANK, -1)
    dckvn = _mm(dkvf, w["mla_w_kvb"], "nt", F32, "mla_kvb_dx", bm=2048)
    g_kvb = _mm(ckvn, dkvf, "tn", BF16, "mla_kvb_dw", bn=512, out_slots=N_SHARD)
    (dproj2,), (g["mla_q_norm"], g["mla_kv_norm"]) = _rowwise_bwd(
        _fn_mla_lat, "mla_latent_norm_bwd", [full(proj2, MLA_IN_PAD, LANES)], [], lat_consts,
        [(dcqn, LANES), (dckvn, LANES), (dkr, LANES)], bm, S, grad_dtypes=[BF16])
    g_mla_in = _mm(h2, dproj2, "tn", BF16, "mla_in_dw")
    token = exchange("mla", [_slot_rows(g_mla_in[:, :MLA_IN]), g_qb, g_kvb, _slot_rows(g_mla_out)])
    dx2, dx2_c, g["mla_norm"] = _mm_dx_norm([dproj2], w["mla_w_in"], x2, w["mla_norm"], dx3, "mla_in_dx", bm=512,
                                            after=token)

    dx1, dx1_c, (g_n0, dw8_0) = ffn_bwd(dx2, dx2_c, ffn0_saved, 0)

    dy0 = _mm(dx1_c, w["ret_w_out"], "nt", F32, "ret_out_dx", bm=1024, cols_outer=True)
    g_ret_out = _mm(y0t, dx1_c, "nn", BF16, "ret_out_dw", bm=1024, bn=512, bk=T)
    token = exchange("reto", [_slot_rows(g_ret_out)])
    gn_behind = w["ret_gn"] + token[0:1, 0:1]
    (d_ret_o, dgate), (g["ret_gn"],) = _rowwise_bwd(_fn_ret_gate, "ret_gate_bwd", gate_rows, [], [(gn_behind, RET_V)],
                                                    [(dy0, RET_V)], bm, S, grad_dtypes=[F32, BF16])
    dq_r, dk_r, dv_r = _ret_attn_bwd(q_r, k_r, v_r, d_ret_o, B, S)
    (dqkv,), _ = _rowwise_bwd(_fn_ret_rope, "ret_rope_bwd", rope_rows, [cos_r, sin_r], [],
                              [(dq_r, LANES), (dk_r, LANES), (dv_r, LANES)], bm, S, grad_dtypes=[BF16], linear=True)
    dx, _, g["ret_norm"] = _mm_dx_norm([dqkv, dgate], w["ret_w_in"], x, w["ret_norm"], dx1, "ret_in_dx")
    g["ffn_norm"] = jnp.concatenate([g_n0, g_n1], axis=0)
    g["ffn_conv_w"] = jnp.stack([dw8_0[0:3], dw8_1[0:3]])
    g["ffn_conv_b"] = jnp.stack([dw8_0[3], dw8_1[3]])
    reduced_small = reduce_small(g, loss)
    g_ret_in = _mm(h0t, [dqkv, dgate], "nn", BF16, "ret_in_dw", bm=1024, bn=512, bk=T, out_slots=N_SHARD,
                   after=reduced_small)
    exchange("ret", [g_ret_in])
    return loss, dx, reduced_small


_SMALL_SHARDED = [("ret_gn", 2), ("mla_norm", 1), ("mla_q_norm", 1), ("mla_kv_norm", 1), ("ffn_conv_w", 2)]
_SMALL_REPLICATED = ["ret_norm", "mla_q_head_norm", "mla_k_head_norm", "ffn_norm", "ffn_conv_b"]
_SMALL_ALL = ["ret_norm", "ret_gn", "mla_norm", "mla_q_norm", "mla_kv_norm", "mla_q_head_norm", "mla_k_head_norm",
              "ffn_norm", "ffn_conv_w", "ffn_conv_b"]


def _to_slots(full, axis):
    shape = full.shape
    split = shape[:axis] + (N_SHARD, shape[axis] // N_SHARD) + shape[axis + 1:]
    return jnp.moveaxis(full.reshape(split), axis, 0).reshape(N_SHARD, -1)


def _from_slots(slots, shard_shape, axis):
    parts = jnp.moveaxis(slots.reshape((N_SHARD,) + tuple(shard_shape)), 0, axis)
    full = shard_shape[:axis] + (N_SHARD * shard_shape[axis],) + shard_shape[axis + 1:]
    return parts.reshape(full)


def _pad_rows(flat, cols, row_unit):
    n, L = flat.shape
    unit = cols * row_unit
    Lp = -(-L // unit) * unit
    if Lp != L:
        flat = jnp.concatenate([flat, jnp.zeros((n, Lp - L), flat.dtype)], axis=1)
    return flat.reshape(n, Lp // cols, cols)


def _pad_heads(a, axis):
    shape = a.shape
    heads = shape[axis] // MLA_QK
    a = a.reshape(shape[:axis] + (heads, MLA_QK) + shape[axis + 1:])
    pad = [(0, 0)] * a.ndim
    pad[axis + 1] = (0, MLA_PAD - MLA_QK)
    return jnp.pad(a, pad).reshape(shape[:axis] + (heads * MLA_PAD,) + shape[axis + 1:])


def _unpad_heads(a, axis):
    shape = a.shape
    a = a.reshape(shape[:axis] + (MLA_HEADS, MLA_PAD) + shape[axis + 1:])
    a = lax.slice_in_dim(a, 0, MLA_QK, axis=axis + 1)
    return a.reshape(shape[:axis] + (MLA_HEADS * MLA_QK,) + shape[axis + 1:])


def kernel(x, ret_norm, ret_w_in, ret_gn, ret_w_out, mla_norm, mla_w_in, mla_q_norm, mla_w_qb, mla_kv_norm, mla_w_kvb, mla_q_head_norm, mla_k_head_norm, mla_w_out, ffn_norm, ffn_w_in, ffn_conv_w, ffn_conv_b, ffn_w_out, loss_target, m_ret_norm, m_ret_w_in, m_ret_gn, m_ret_w_out, m_mla_norm, m_mla_w_in, m_mla_q_norm, m_mla_w_qb, m_mla_kv_norm, m_mla_w_kvb, m_mla_q_head_norm, m_mla_k_head_norm, m_mla_w_out, m_ffn_norm, m_ffn_w_in, m_ffn_conv_w, m_ffn_conv_b, m_ffn_w_out, v_ret_norm, v_ret_w_in, v_ret_gn, v_ret_w_out, v_mla_norm, v_mla_w_in, v_mla_q_norm, v_mla_w_qb, v_mla_kv_norm, v_mla_w_kvb, v_mla_q_head_norm, v_mla_k_head_norm, v_mla_w_out, v_ffn_norm, v_ffn_w_in, v_ffn_conv_w, v_ffn_conv_b, v_ffn_w_out):
    names = ["ret_norm", "ret_w_in", "ret_gn", "ret_w_out", "mla_norm", "mla_w_in", "mla_q_norm", "mla_w_qb",
             "mla_kv_norm", "mla_w_kvb", "mla_q_head_norm", "mla_k_head_norm", "mla_w_out", "ffn_norm", "ffn_w_in",
             "ffn_conv_w", "ffn_conv_b", "ffn_w_out"]
    shard = dict(zip(names, [ret_norm, ret_w_in, ret_gn, ret_w_out, mla_norm, mla_w_in, mla_q_norm, mla_w_qb,
                             mla_kv_norm, mla_w_kvb, mla_q_head_norm, mla_k_head_norm, mla_w_out, ffn_norm, ffn_w_in,
                             ffn_conv_w, ffn_conv_b, ffn_w_out]))
    mom_m = dict(zip(names, [m_ret_norm, m_ret_w_in, m_ret_gn, m_ret_w_out, m_mla_norm, m_mla_w_in, m_mla_q_norm,
                             m_mla_w_qb, m_mla_kv_norm, m_mla_w_kvb, m_mla_q_head_norm, m_mla_k_head_norm, m_mla_w_out,
                             m_ffn_norm, m_ffn_w_in, m_ffn_conv_w, m_ffn_conv_b, m_ffn_w_out]))
    mom_v = dict(zip(names, [v_ret_norm, v_ret_w_in, v_ret_gn, v_ret_w_out, v_mla_norm, v_mla_w_in, v_mla_q_norm,
                             v_mla_w_qb, v_mla_kv_norm, v_mla_w_kvb, v_mla_q_head_norm, v_mla_k_head_norm, v_mla_w_out,
                             v_ffn_norm, v_ffn_w_in, v_ffn_conv_w, v_ffn_conv_b, v_ffn_w_out]))
    B, S, D = x.shape
    T = B * S
    sx, sy = lax.axis_index("x"), lax.axis_index("y")
    me = 2 * sx + sy

    two_d = lambda a: a.reshape(-1, a.shape[-1])
    small_sizes = [int(np.prod(shard[n].shape)) for n, _ in _SMALL_SHARDED]
    small = jnp.concatenate([shard[n].reshape(1, -1) for n, _ in _SMALL_SHARDED], axis=1)
    small = _pad_rows(small, LANES, 8)[0]
    as_mxu = lambda a: two_d(a).astype(BF16)
    zero = jnp.zeros((), jnp.int32)
    with_own = lambda gathered, own: lax.dynamic_update_slice(gathered, own[None], (me.astype(jnp.int32), zero, zero))
    by_cols = lambda a: jnp.moveaxis(a, 0, 1).reshape(a.shape[1], -1)
    by_rows = lambda a: a.reshape(-1, a.shape[-1])
    mla_in_shard = jnp.pad(as_mxu(shard["mla_w_in"]), ((0, 0), (0, MLA_IN_PAD - MLA_IN)))
    mla_qb_shard = _pad_heads(as_mxu(shard["mla_w_qb"]), 1)
    ret_in_shard = as_mxu(shard["ret_w_in"])
    g_ret_in, gsmall = _all_gather_weights([ret_in_shard], small)
    later = [
        ("ret_out", [("ret_w_out", as_mxu(shard["ret_w_out"]), by_rows)]),
        ("ffn0", [("ffn_w_in0", as_mxu(shard["ffn_w_in"][0]), by_cols), ("ffn_w_out0", as_mxu(shard["ffn_w_out"][0]), by_rows)]),
        ("mla", [("mla_w_in", mla_in_shard, by_rows), ("mla_w_qb", mla_qb_shard, by_cols),
                 ("mla_w_kvb", as_mxu(shard["mla_w_kvb"]), by_cols), ("mla_w_out", as_mxu(shard["mla_w_out"]), by_rows)]),
        ("ffn1", [("ffn_w_in1", as_mxu(shard["ffn_w_in"][1]), by_cols), ("ffn_w_out1", as_mxu(shard["ffn_w_out"][1]), by_rows)]),
    ]
    gathering = {}
    token = gsmall
    for group, items in later:
        shards = [s_ for _, s_, _ in items]
        lands = [lax.empty((N_SHARD,) + s_.shape, s_.dtype) for s_ in shards]
        send_sems, recv_sems, shards, lands, token = _exchange_start(
            _weight_copies, shards, lands, 3 * len(shards), f"weights_start_{group}", after=token)
        gathering[group] = (send_sems, recv_sems, shards, lands, items)

    def late(group, after):
        send_sems, recv_sems, shards, lands, items = gathering[group]
        shards, lands = _exchange_wait(_weight_copies, send_sems, recv_sems, shards, lands, after,
                                       f"weights_wait_{group}")
        return {key: full(with_own(l_, s_)) for (key, _, full), s_, l_ in zip(items, shards, lands)}

    gsmall = with_own(gsmall, small).reshape(N_SHARD, -1)
    wfull = {}
    off = 0
    for (n, ax), sz in zip(_SMALL_SHARDED, small_sizes):
        wfull[n] = _from_slots(gsmall[:, off:off + sz], shard[n].shape, ax)
        off += sz
    for n in _SMALL_REPLICATED:
        wfull[n] = shard[n]

    conv8 = jnp.concatenate([wfull["ffn_conv_w"], wfull["ffn_conv_b"][:, None, :],
                             jnp.zeros((2, 4, FFN_DIM), F32)], axis=1)
    w = {
        "started": token, "ret_norm": wfull["ret_norm"], "ret_w_in": by_cols(with_own(g_ret_in, ret_in_shard)),
        "ret_gn": wfull["ret_gn"].reshape(1, RET_HEADS * RET_V), "mla_norm": wfull["mla_norm"],
        "mla_q_norm": wfull["mla_q_norm"], "mla_kv_norm": wfull["mla_kv_norm"],
        "mla_q_head_norm": jnp.pad(wfull["mla_q_head_norm"], ((0, 0), (0, MLA_PAD - MLA_QK))),
        "mla_k_head_norm": jnp.pad(wfull["mla_k_head_norm"], ((0, 0), (0, MLA_PAD - MLA_QK))),
        "ffn_norm": wfull["ffn_norm"], "ffn_conv8": conv8,
    }

    started = {}

    def exchange(group, arrays):
        lands = [lax.empty((N_PEERS, p.shape[1] // 2, p.shape[2]), p.dtype) for p in arrays]
        send_sems, recv_sems, ps, lands, token = _exchange_start(
            _grad_copies, arrays, lands, N_PEERS * len(arrays), f"grads_start_{group}")
        started[group] = (send_sems, recv_sems, ps, lands)
        return token

    small_shapes = {
        "ret_norm": (1, D_MODEL), "ret_gn": (1, RET_HEADS, RET_V), "mla_norm": (1, D_MODEL),
        "mla_q_norm": (1, MLA_Q_RANK), "mla_kv_norm": (1, MLA_KV_RANK), "mla_q_head_norm": (1, MLA_QK),
        "mla_k_head_norm": (1, MLA_QK), "ffn_norm": (2, D_MODEL), "ffn_conv_w": (2, 3, FFN_DIM),
        "ffn_conv_b": (2, FFN_DIM)}

    def reduce_small(gl, loss_part):
        gl = dict(gl, mla_q_head_norm=gl["mla_q_head_norm"][:, :MLA_QK], mla_k_head_norm=gl["mla_k_head_norm"][:, :MLA_QK])
        packed = jnp.concatenate([gl[n].reshape(1, -1) for n in _SMALL_ALL] + [loss_part.reshape(1, 1)], axis=1)
        return _all_reduce_small(_pad_rows(packed, LANES, 8)[0])

    _, dx, gsm = _local_step(x.reshape(T, D), loss_target.reshape(T, D), w, B, S, late, exchange, reduce_small)

    delta, new_m, new_v, grads = {}, {}, {}, {}

    def reduced(group, after):
        send_sems, recv_sems, ps, lands = started[group]
        ps, lands = _exchange_wait(_grad_copies, send_sems, recv_sems, ps, lands, after, f"grads_wait_{group}")
        halves = [_sum_partials(p_, l_, f"grads_sum_{group}_{i}") for i, (p_, l_) in enumerate(zip(ps, lands))]
        return [two_d(r) for r in _sibling_share(halves, f"grads_share_{group}")]

    def adamw(n, g_):
        shp = shard[n].shape
        grads[n] = g_.reshape(shp)
        flat = lambda a: a.reshape(-1, shp[-1])
        d_, m_, v_ = _adamw(flat(shard[n]), flat(grads[n]), flat(mom_m[n]), flat(mom_v[n]), f"adamw_{n}")
        delta[n], new_m[n], new_v[n] = d_.reshape(shp), m_.reshape(shp), v_.reshape(shp)
        return d_

    ffn1 = reduced("ffn1", started["ret"][2][0])
    mla = reduced("mla", ffn1[0])
    ffn0 = reduced("ffn0", mla[0])
    reto = reduced("reto", ffn0[0])
    early = [adamw(n, g_) for n, g_ in zip(["mla_w_in", "mla_w_qb", "mla_w_kvb", "mla_w_out"], mla)]
    early.append(adamw("ffn_w_in", jnp.stack([ffn0[0], ffn1[0]])))
    early.append(adamw("ffn_w_out", jnp.stack([ffn0[1], ffn1[1]])))
    early.append(adamw("ret_w_out", reto[0]))
    ret = reduced("ret", jnp.stack([d_[0, 0] for d_ in early]))
    adamw("ret_w_in", ret[0])

    gsm = gsm.reshape(-1)
    sharded_axis = dict(_SMALL_SHARDED)
    off = 0
    for n in _SMALL_ALL:
        sz = int(np.prod(small_shapes[n]))
        gn = gsm[off:off + sz].reshape(small_shapes[n])
        off += sz
        if n in sharded_axis:
            ax = sharded_axis[n]
            width = shard[n].shape[ax]
            gn = lax.dynamic_slice_in_dim(gn, me * width, width, axis=ax)
        grads[n] = gn
    loss = gsm[off]

    pack_small = lambda d: _pad_rows(jnp.concatenate([d[n].reshape(1, -1) for n in _SMALL_ALL], axis=1), LANES, 8)[0]
    d_, m_, v_ = _adamw(pack_small(shard), pack_small(grads), pack_small(mom_m), pack_small(mom_v), "adamw_small")
    off = 0
    for n in _SMALL_ALL:
        sz = int(np.prod(shard[n].shape))
        for dst, src in ((delta, d_), (new_m, m_), (new_v, v_)):
            dst[n] = src.reshape(-1)[off:off + sz].reshape(shard[n].shape)
        off += sz

    return (loss, dx.reshape(B, S, D), *[grads[n] for n in names], *[delta[n] for n in names],
            *[new_m[n] for n in names], *[new_v[n] for n in names])
```

```python
import functools

import numpy as np
import jax
import jax.numpy as jnp
from jax import lax
from jax.experimental import pallas as pl
from jax.experimental.pallas import tpu as pltpu

F32 = jnp.float32
BF16 = jnp.bfloat16
MXU_DTYPE = jnp.bfloat16

CHUNK = 64
RMS_EPS = 1e-6
ROPE_THETA = 10000.0
D_MODEL = 1024
RET_HEADS = 4
RET_QK = 256
RET_V = 512
RET_GAMMA_BASE = -5.0
MLA_HEADS = 8
MLA_Q_RANK = 384
MLA_KV_RANK = 256
MLA_NOPE = 128
MLA_ROPE = 64
MLA_V = 128
MLA_QK = MLA_NOPE + MLA_ROPE
MLA_PAD = 256
MLA_IN = MLA_Q_RANK + MLA_KV_RANK + MLA_ROPE
MLA_IN_PAD = MLA_IN + 64
MASK_VALUE = -1e30
FFN_DIM = 2816
ADAM_LR = 0.001
ADAM_B1 = 0.9
ADAM_B2 = 0.999
ADAM_EPS = 1e-08
ADAM_WD = 0.01
ADAM_STEP = 10

LANES = 128
MLA_FWD_BLOCK = 512
VMEM_LIMIT = 56 * 2 ** 20
N_SHARD = 4
N_DEV = 8

MESH = pl.DeviceIdType.MESH


def _params(sem=None, **kw):
    return pltpu.CompilerParams(dimension_semantics=sem, vmem_limit_bytes=VMEM_LIMIT, **kw)


def _pick(dim, target):
    if dim <= target:
        return dim
    best = None
    for d in range(LANES, target + 1, LANES):
        if dim % d == 0:
            best = d
    assert best is not None, (dim, target)
    return best


def _mm(a, b, dims, out_dtype, name, residual=None, bm=512, bn=1024, bk=2048, out_slots=None, after=None,
        cols_outer=False):
    a_parts = list(a) if isinstance(a, (list, tuple)) else [a]
    b_parts = list(b) if isinstance(b, (list, tuple)) else [b]
    parts_on_n = dims == "tn" or len(b_parts) > 1
    if parts_on_n:
        assert len(a_parts) == 1 and dims in ("tn", "nn")
        (K, M) = a_parts[0].shape if dims == "tn" else a_parts[0].shape[::-1]
        N = sum(p.shape[1] for p in b_parts)
        part_widths = [p.shape[1] for p in b_parts]
    else:
        assert len(b_parts) == 1
        M = a_parts[0].shape[0]
        K = sum(p.shape[1] for p in a_parts)
        N = b_parts[0].shape[1 if dims == "nn" else 0]
        part_widths = [p.shape[1] for p in a_parts]
    bm, bn, bk = _pick(M, bm), _pick(N, bn), _pick(K, min(bk, 1024) if dims == "tn" else bk)
    nk = K // bk
    unit = bn if parts_on_n else bk
    assert all(wd % unit == 0 for wd in part_widths), (name, part_widths, unit)
    bounds = np.cumsum([0] + [wd // unit for wd in part_widths])
    ranges = [(int(lo), int(hi)) for lo, hi in zip(bounds[:-1], bounds[1:])]

    def part_index(idx, lo, hi):
        return jnp.clip(idx - lo, 0, hi - lo - 1)

    if parts_on_n:
        if dims == "tn":
            a_specs = [pl.BlockSpec((bk, bm), lambda i, j, k: (k, i))]
            dn = (((0,), (0,)), ((), ()))
        else:
            a_specs = [pl.BlockSpec((bm, bk), lambda i, j, k: (i, k))]
            dn = (((1,), (0,)), ((), ()))
        b_specs = [pl.BlockSpec((bk, bn), functools.partial(lambda i, j, k, lo, hi: (k, part_index(j, lo, hi)), lo=lo, hi=hi))
                   for lo, hi in ranges]
    else:
        a_specs = [pl.BlockSpec((bm, bk), functools.partial(lambda i, j, k, lo, hi: (i, part_index(k, lo, hi)), lo=lo, hi=hi))
                   for lo, hi in ranges]
        if dims == "nt":
            b_specs = [pl.BlockSpec((bn, bk), lambda i, j, k: (j, k))]
        else:
            b_specs = [pl.BlockSpec((bk, bn), lambda i, j, k: (k, j))]
        dn = (((1,), (1 if dims == "nt" else 0,)), ((), ()))
    r_spec = pl.BlockSpec((bm, bn), lambda i, j, k: (i, j))
    if out_slots is None:
        o_spec, o_shape = r_spec, (M, N)
    else:
        ns = N // out_slots
        assert ns % bn == 0, (name, ns, bn)
        nbs = ns // bn
        o_spec = pl.BlockSpec((None, bm, bn), lambda i, j, k: (j // nbs, i, j % nbs))
        o_shape = (out_slots, M, ns)
    has_res = residual is not None
    na, nb = len(a_parts), len(b_parts)

    def body(*refs):
        a_refs, b_refs = refs[:na], refs[na:na + nb]
        r_ref = refs[na + nb] if has_res else None
        n_in = na + nb + has_res + (after is not None)
        o_ref = refs[n_in]
        acc_ref = refs[n_in + 1] if nk > 1 else None
        k = pl.program_id(2)

        def finish(acc):
            if has_res:
                acc = acc + r_ref[...].astype(F32)
            o_ref[...] = acc.astype(out_dtype)

        def compute(a_ref, b_ref):
            p = lax.dot_general(a_ref[...].astype(MXU_DTYPE), b_ref[...].astype(MXU_DTYPE), dn,
                                preferred_element_type=F32)
            if nk == 1:
                finish(p)
                return

            @pl.when(k == 0)
            def _():
                acc_ref[...] = p

            @pl.when(jnp.logical_and(k > 0, k < nk - 1))
            def _():
                acc_ref[...] += p

            @pl.when(k == nk - 1)
            def _():
                finish(acc_ref[...] + p)

        if len(ranges) == 1:
            compute(a_refs[0], b_refs[0])
        else:
            idx = pl.program_id(0 if cols_outer else 1) if parts_on_n else k
            for p, (lo, hi) in enumerate(ranges):
                @pl.when(jnp.logical_and(idx >= lo, idx < hi))
                def _(p=p):
                    compute(a_refs[0 if parts_on_n else p], b_refs[p if parts_on_n else 0])

    after_specs = [] if after is None else [pl.BlockSpec(after.shape, lambda i, j, k: (0, 0))]
    in_specs = a_specs + b_specs + ([r_spec] if has_res else []) + after_specs
    grid = (M // bm, N // bn, nk)
    if cols_outer:
        swap = lambda sp: pl.BlockSpec(sp.block_shape, functools.partial(lambda j, i, k, f: f(i, j, k), f=sp.index_map))
        in_specs, o_spec, grid = [swap(sp) for sp in in_specs], swap(o_spec), (grid[1], grid[0], nk)
    return pl.pallas_call(
        body, name=name, grid=grid,
        in_specs=in_specs, out_specs=o_spec,
        out_shape=jax.ShapeDtypeStruct(o_shape, out_dtype),
        scratch_shapes=[pltpu.VMEM((bm, bn), F32)] if nk > 1 else [],
        compiler_params=_params(("parallel", "parallel", "arbitrary")),
    )(*a_parts, *b_parts, *((residual,) if has_res else ()), *(() if after is None else (after,)))


def _mm_out_norm(a, w, residual, gain, name, bm=512):
    (M, K), N = a.shape, w.shape[1]
    bm = _pick(M, bm)

    def body(a_ref, w_ref, r_ref, g_ref, o_ref, h_ref, ht_ref):
        acc = lax.dot_general(a_ref[...].astype(MXU_DTYPE), w_ref[...].astype(MXU_DTYPE), _NN,
                              preferred_element_type=F32) + r_ref[...]
        o_ref[...] = acc
        hv = _fn_rms([[acc]], [], [[g_ref[...]]])[0][0]
        h_ref[...] = hv.astype(h_ref.dtype)
        ht_ref[...] = hv.T.astype(ht_ref.dtype)

    row = pl.BlockSpec((bm, N), lambda i: (i, 0))
    whole = lambda arr: pl.BlockSpec(arr.shape, lambda i: (0, 0))
    return pl.pallas_call(
        body, name=name, grid=(M // bm,),
        in_specs=[pl.BlockSpec((bm, K), lambda i: (i, 0)), whole(w), row, whole(gain)],
        out_specs=[row, row, pl.BlockSpec((N, bm), lambda i: (0, i))],
        out_shape=[jax.ShapeDtypeStruct((M, N), F32), jax.ShapeDtypeStruct((M, N), BF16),
                   jax.ShapeDtypeStruct((N, M), BF16)],
        compiler_params=_params(("parallel",)),
    )(a, w, residual, gain)


def _mm_out_loss(a, w, residual, target, name, bm=512):
    (M, K), N = a.shape, w.shape[1]
    bm = _pick(M, bm)

    def body(a_ref, w_ref, r_ref, t_ref, dy_ref, dyc_ref, l_ref):
        y = lax.dot_general(a_ref[...].astype(MXU_DTYPE), w_ref[...].astype(MXU_DTYPE), _NN,
                            preferred_element_type=F32) + r_ref[...]
        err = y - t_ref[...]
        dy_ref[...] = err / N
        dyc_ref[...] = (err / N).astype(dyc_ref.dtype)
        part = jnp.full((8, LANES), 0.5 * jnp.sum(jnp.mean(err * err, axis=-1)), F32)

        @pl.when(pl.program_id(0) == 0)
        def _():
            l_ref[...] = part

        @pl.when(pl.program_id(0) > 0)
        def _():
            l_ref[...] += part

    row = pl.BlockSpec((bm, N), lambda i: (i, 0))
    dy, dyc, l = pl.pallas_call(
        body, name=name, grid=(M // bm,),
        in_specs=[pl.BlockSpec((bm, K), lambda i: (i, 0)), pl.BlockSpec(w.shape, lambda i: (0, 0)), row, row],
        out_specs=[row, row, pl.BlockSpec((8, LANES), lambda i: (0, 0))],
        out_shape=[jax.ShapeDtypeStruct((M, N), F32), jax.ShapeDtypeStruct((M, N), BF16),
                   jax.ShapeDtypeStruct((8, LANES), F32)],
        compiler_params=_params(("arbitrary",)),
    )(a, w, residual, target)
    return dy, dyc, l[0, 0]


def _mm_dx_norm(a_parts, w, x, gain, add, name, bm=256, after=None):
    M = a_parts[0].shape[0]
    N, K = w.shape
    widths = [p.shape[1] for p in a_parts]
    assert sum(widths) == K, (name, widths, K)
    offs = [int(o) for o in np.cumsum([0] + widths[:-1])]
    bm = _pick(M, bm)
    na = len(a_parts)
    n_in = na + 4 + (after is not None)

    def body(*refs):
        w_ref, x_ref, g_ref, add_ref = refs[na:na + 4]
        dx_ref, dxc_ref, dg_ref = refs[n_in:n_in + 3]
        dh = None
        for a_ref, off, wd in zip(refs[:na], offs, widths):
            p = lax.dot_general(a_ref[...].astype(MXU_DTYPE), w_ref[:, off:off + wd].astype(MXU_DTYPE), _NT,
                                preferred_element_type=F32)
            dh = p if dh is None else dh + p
        _, vjp = jax.vjp(lambda xv, gv: _fn_rms([[xv]], [], [[gv]])[0][0], x_ref[...], g_ref[...])
        dxv, dgv = vjp(dh)
        dxv = dxv + add_ref[...]
        dx_ref[...] = dxv
        dxc_ref[...] = dxv.astype(dxc_ref.dtype)

        @pl.when(pl.program_id(0) == 0)
        def _():
            dg_ref[...] = dgv

        @pl.when(pl.program_id(0) > 0)
        def _():
            dg_ref[...] += dgv

    row = pl.BlockSpec((bm, N), lambda i: (i, 0))
    whole = lambda a: pl.BlockSpec(a.shape, lambda i: (0, 0))
    in_specs = [pl.BlockSpec((bm, wd), lambda i: (i, 0)) for wd in widths] + [whole(w), row, whole(gain), row]
    in_specs += [] if after is None else [whole(after)]
    return pl.pallas_call(
        body, name=name, grid=(M // bm,),
        in_specs=in_specs, out_specs=[row, row, whole(gain)],
        out_shape=[jax.ShapeDtypeStruct((M, N), F32), jax.ShapeDtypeStruct((M, N), BF16),
                   jax.ShapeDtypeStruct(gain.shape, F32)],
        compiler_params=_params(("arbitrary",)),
    )(*a_parts, w, x, gain, add, *(() if after is None else (after,)))


def _tiles(ref, width, tile):
    return [ref[:, t * tile:(t + 1) * tile].astype(F32) for t in range(width // tile)]


def _row_specs(rows, pos, consts, bm, S):
    npos_blocks = S // bm
    specs = [pl.BlockSpec((bm, w), functools.partial(lambda i, c: (i, c), c=cb)) for (_, w, cb, _) in rows]
    specs += [pl.BlockSpec((bm, p.shape[1]), lambda i: (i % npos_blocks, 0)) for p in pos]
    specs += [pl.BlockSpec(c.shape, lambda i: (0, 0)) for (c, _) in consts]
    return specs


def _rowwise_fwd(fn, name, rows, pos, consts, outs, bm, S, transposed=()):
    T = rows[0][0].shape[0]
    nr, npos, nc, no = len(rows), len(pos), len(consts), len(outs)

    def body(*refs):
        row_v = [_tiles(r, w, t) for r, (_, w, _, t) in zip(refs[:nr], rows)]
        pos_v = [r[...] for r in refs[nr:nr + npos]]
        const_v = [_tiles(r, c.shape[1], t) for r, (c, t) in zip(refs[nr + npos:nr + npos + nc], consts)]
        res = fn(row_v, pos_v, const_v)
        out_refs = refs[nr + npos + nc:]
        for o_ref, tiles, (w, t, dt) in zip(out_refs, res, outs):
            for k, v in enumerate(tiles):
                o_ref[:, k * t:(k + 1) * t] = v.astype(dt)
        for t_ref, a in zip(out_refs[no:], transposed):
            t = outs[a][1]
            for k, v in enumerate(res[a]):
                t_ref[k * t:(k + 1) * t, :] = v.T.astype(t_ref.dtype)

    return pl.pallas_call(
        body, name=name, grid=(T // bm,),
        in_specs=_row_specs(rows, pos, consts, bm, S),
        out_specs=[pl.BlockSpec((bm, w), lambda i: (i, 0)) for (w, _, _) in outs]
        + [pl.BlockSpec((outs[a][0], bm), lambda i: (0, i)) for a in transposed],
        out_shape=[jax.ShapeDtypeStruct((T, w), dt) for (w, _, dt) in outs]
        + [jax.ShapeDtypeStruct((outs[a][0], T), BF16) for a in transposed],
        compiler_params=_params(("parallel",)),
    )(*[r[0] for r in rows], *pos, *[c[0] for c in consts])


def _rowwise_bwd(fn, name, rows, pos, consts, cts, bm, S, adds=None, grad_dtypes=None, mxu_copies=(), linear=False):
    adds = adds or {}
    T = rows[0][0].shape[0]
    nr, npos, nc, nct = len(rows), len(pos), len(consts), len(cts)
    add_idx = sorted(adds)
    grad_dtypes = grad_dtypes or [F32] * nr

    def body(*refs):
        it = iter(refs)
        row_refs = [None if linear else next(it) for _ in range(nr)]
        pos_refs = [next(it) for _ in range(npos)]
        const_refs = [next(it) for _ in range(nc)]
        ct_refs = [next(it) for _ in range(nct)]
        add_refs = {k: next(it) for k in add_idx}
        drow_refs = [next(it) for _ in range(nr)]
        copy_refs = {a: next(it) for a in mxu_copies}
        dconst_refs = [next(it) for _ in range(nc)]
        if linear:
            row_v = [[jnp.zeros((bm, t), F32)] * (w // t) for (_, w, _, t) in rows]
        else:
            row_v = [_tiles(r, w, t) for r, (_, w, _, t) in zip(row_refs, rows)]
        pos_v = [r[...] for r in pos_refs]
        const_v = [_tiles(r, c.shape[1], t) for r, (c, t) in zip(const_refs, consts)]
        ct_v = [_tiles(r, c.shape[1], t) for r, (c, t) in zip(ct_refs, cts)]
        _, vjp = jax.vjp(lambda rv, cv: fn(rv, pos_v, cv), row_v, const_v)
        drows, dconsts = vjp(ct_v)
        for a, (d_ref, tiles, (_, w, _, t)) in enumerate(zip(drow_refs, drows, rows)):
            for k, v in enumerate(tiles):
                if a in add_refs:
                    v = v + add_refs[a][:, k * t:(k + 1) * t].astype(F32)
                d_ref[:, k * t:(k + 1) * t] = v.astype(d_ref.dtype)
                if a in copy_refs:
                    copy_refs[a][:, k * t:(k + 1) * t] = v.astype(BF16)
        first = pl.program_id(0) == 0
        for d_ref, tiles, (_, t) in zip(dconst_refs, dconsts, consts):
            for k, v in enumerate(tiles):
                @pl.when(first)
                def _(d_ref=d_ref, k=k, t=t, v=v):
                    d_ref[:, k * t:(k + 1) * t] = v

                @pl.when(jnp.logical_not(first))
                def _(d_ref=d_ref, k=k, t=t, v=v):
                    d_ref[:, k * t:(k + 1) * t] += v

    in_specs = _row_specs([] if linear else rows, pos, consts, bm, S)
    in_specs += [pl.BlockSpec((bm, c.shape[1]), lambda i: (i, 0)) for (c, _) in cts]
    in_specs += [pl.BlockSpec((bm, adds[k].shape[1]), lambda i: (i, 0)) for k in add_idx]
    out_specs = [pl.BlockSpec((bm, w), lambda i: (i, 0)) for (_, w, _, _) in rows]
    out_specs += [pl.BlockSpec((bm, rows[a][1]), lambda i: (i, 0)) for a in mxu_copies]
    out_specs += [pl.BlockSpec(c.shape, lambda i: (0, 0)) for (c, _) in consts]
    out_shape = [jax.ShapeDtypeStruct((T, w), dt) for (_, w, _, _), dt in zip(rows, grad_dtypes)]
    out_shape += [jax.ShapeDtypeStruct((T, rows[a][1]), BF16) for a in mxu_copies]
    out_shape += [jax.ShapeDtypeStruct(c.shape, F32) for (c, _) in consts]
    res = pl.pallas_call(
        body, name=name, grid=(T // bm,),
        in_specs=in_specs, out_specs=out_specs, out_shape=out_shape,
        compiler_params=_params(("arbitrary",)),
    )(*([] if linear else [r[0] for r in rows]), *pos, *[c[0] for c in consts], *[c[0] for c in cts],
      *[adds[k] for k in add_idx])
    n_rows = nr + len(mxu_copies)
    return res[:n_rows], res[n_rows:]


def _ssq(tiles):
    s = jnp.sum(tiles[0] * tiles[0], axis=-1, keepdims=True)
    for t in tiles[1:]:
        s = s + jnp.sum(t * t, axis=-1, keepdims=True)
    return s


def _sigmoid(x):
    return 0.5 * jnp.tanh(0.5 * x) + 0.5


def _fn_rms(rows, pos, consts):
    (x,), (g,) = rows[0], consts[0]
    r = lax.rsqrt(jnp.mean(x * x, axis=-1, keepdims=True) + RMS_EPS)
    return [[x * r * g]]


def _fn_ret_rope(rows, pos, consts):
    (qkv,) = rows
    nq = RET_HEADS * RET_QK // LANES
    q, k, v = qkv[:nq], qkv[nq:2 * nq], qkv[2 * nq:]
    cos, sin = pos

    def rot(t, scale):
        out = []
        for h in range(RET_HEADS):
            x1, x2 = t[2 * h], t[2 * h + 1]
            o1, o2 = x1 * cos - x2 * sin, x2 * cos + x1 * sin
            out += [o1, o2] if scale is None else [o1 * scale, o2 * scale]
        return out

    return [rot(q, None), rot(k, RET_QK ** -0.5), list(v)]


def _fn_ret_gate(rows, pos, consts):
    o, g = rows
    (gn,) = consts
    out = []
    for h in range(RET_HEADS):
        r = lax.rsqrt(jnp.mean(o[h] * o[h], axis=-1, keepdims=True) + RMS_EPS)
        out.append((o[h] * r * gn[h]) * (g[h] * _sigmoid(g[h])))
    return [out]


def _fn_mla_lat(rows, pos, consts):
    (p,) = rows
    gq, gkv = consts
    nq, nkv = MLA_Q_RANK // LANES, MLA_KV_RANK // LANES
    cq, ckv, kr = p[:nq], p[nq:nq + nkv], p[nq + nkv]
    rq = lax.rsqrt(_ssq(cq) / MLA_Q_RANK + RMS_EPS)
    rkv = lax.rsqrt(_ssq(ckv) / MLA_KV_RANK + RMS_EPS)
    return [[t * rq * g for t, g in zip(cq, gq)], [t * rkv * g for t, g in zip(ckv, gkv)], [kr]]


def _swap32_impl(x):
    lane = lax.broadcasted_iota(jnp.int32, x.shape, 1)
    up, down = pltpu.roll(x, LANES - 32, 1), pltpu.roll(x, 32, 1)
    return jnp.where(lane < 32, up, jnp.where(lane < 64, down, 0.0))


@jax.custom_vjp
def _swap32(x):
    return _swap32_impl(x)


_swap32.defvjp(lambda x: (_swap32_impl(x), None), lambda _, g: (_swap32_impl(g),))


def _fn_mla_heads(rows, pos, consts):
    qf, kvf, (kr,) = rows
    cos, sin = pos
    gq, gk = consts
    q_out, k_out, v_out = [], [], []
    for h in range(MLA_HEADS):
        q0, q1 = qf[2 * h], qf[2 * h + 1]
        r = lax.rsqrt(_ssq([q0, q1]) / MLA_QK + RMS_EPS)
        a0, a1 = q0 * r * gq[0], q1 * r * gq[1]
        a1 = a1 * cos + _swap32(a1) * sin
        q_out += [a0 * (MLA_QK ** -0.5), a1 * (MLA_QK ** -0.5)]
        k0 = kvf[2 * h]
        r = lax.rsqrt(_ssq([k0, kr]) / MLA_QK + RMS_EPS)
        b0, b1 = k0 * r * gk[0], kr * r * gk[1]
        k_out += [b0, b1 * cos + _swap32(b1) * sin]
        v_out.append(kvf[2 * h + 1])
    return [q_out, k_out, v_out]


def _shift_down(x, n):
    row = lax.broadcasted_iota(jnp.int32, x.shape, 0)
    return jnp.where(row >= n, pltpu.roll(x, n, 0), 0.0)


def _shift_up(x, n):
    rows = x.shape[0]
    row = lax.broadcasted_iota(jnp.int32, x.shape, 0)
    return jnp.where(row < rows - n, pltpu.roll(x, rows - n, 0), 0.0)


def _conv_blocks(S):
    cb = 256
    return cb, FFN_DIM // cb


def _conv_fwd(ag, w8, B, S, name):
    cb, ncb = _conv_blocks(S)

    def body(a_ref, g_ref, w_ref, u_ref, ut_ref):
        g = g_ref[...].astype(F32)
        w = w_ref[...]
        gc = w[0:1] * _shift_down(g, 2) + w[1:2] * _shift_down(g, 1) + w[2:3] * g + w[3:4]
        u = a_ref[...].astype(F32) * (gc * _sigmoid(gc))
        u_ref[...] = u.astype(u_ref.dtype)
        ut_ref[...] = u.T.astype(ut_ref.dtype)

    return pl.pallas_call(
        body, name=name, grid=(ncb, B),
        in_specs=[pl.BlockSpec((S, cb), lambda j, b: (b, j)),
                  pl.BlockSpec((S, cb), lambda j, b: (b, ncb + j)),
                  pl.BlockSpec((8, cb), lambda j, b: (0, j))],
        out_specs=[pl.BlockSpec((S, cb), lambda j, b: (b, j)), pl.BlockSpec((cb, S), lambda j, b: (j, b))],
        out_shape=[jax.ShapeDtypeStruct((B * S, FFN_DIM), BF16), jax.ShapeDtypeStruct((FFN_DIM, B * S), BF16)],
        compiler_params=_params(("parallel", "parallel")),
    )(ag, ag, w8)


def _conv_bwd(ag, w8, du, B, S, name):
    cb, ncb = _conv_blocks(S)

    def body(a_ref, g_ref, w_ref, du_ref, da_ref, dg_ref, dw_ref):
        g = g_ref[...].astype(F32)
        w = w_ref[...]
        g1, g2 = _shift_down(g, 1), _shift_down(g, 2)
        gc = w[0:1] * g2 + w[1:2] * g1 + w[2:3] * g + w[3:4]
        sg = _sigmoid(gc)
        du_v = du_ref[...]
        da_ref[...] = (du_v * (gc * sg)).astype(da_ref.dtype)
        dgc = du_v * a_ref[...].astype(F32) * (sg * (1.0 + gc * (1.0 - sg)))
        dg = w[2:3] * dgc + w[1:2] * _shift_up(dgc, 1) + w[0:1] * _shift_up(dgc, 2)
        dg_ref[...] = dg.astype(dg_ref.dtype)
        part = jnp.concatenate([
            jnp.sum(dgc * g2, axis=0, keepdims=True), jnp.sum(dgc * g1, axis=0, keepdims=True),
            jnp.sum(dgc * g, axis=0, keepdims=True), jnp.sum(dgc, axis=0, keepdims=True),
            jnp.zeros((4, cb), F32)], axis=0)

        @pl.when(pl.program_id(1) == 0)
        def _():
            dw_ref[...] = part

        @pl.when(pl.program_id(1) > 0)
        def _():
            dw_ref[...] += part

    blk = lambda j, b: (b, j)
    return pl.pallas_call(
        body, name=name, grid=(ncb, B),
        in_specs=[pl.BlockSpec((S, cb), blk),
                  pl.BlockSpec((S, cb), lambda j, b: (b, ncb + j)),
                  pl.BlockSpec((8, cb), lambda j, b: (0, j)),
                  pl.BlockSpec((S, cb), blk)],
        out_specs=[pl.BlockSpec((S, cb), blk), pl.BlockSpec((S, cb), blk),
                   pl.BlockSpec((8, cb), lambda j, b: (0, j))],
        out_shape=[jax.ShapeDtypeStruct((B * S, FFN_DIM), BF16), jax.ShapeDtypeStruct((B * S, FFN_DIM), BF16),
                   jax.ShapeDtypeStruct((8, FFN_DIM), F32)],
        compiler_params=_params(("parallel", "arbitrary")),
    )(ag, ag, w8, du)


_NT = (((1,), (1,)), ((), ()))
_NN = (((1,), (0,)), ((), ()))
_TN = (((0,), (0,)), ((), ()))


def _dot(a, b, dn):
    return lax.dot_general(a.astype(MXU_DTYPE), b.astype(MXU_DTYPE), dn, preferred_element_type=F32)


def _run_bits(n):
    bits, b = [], 1
    while b < n:
        bits.append(b)
        b *= 2
    return bits[::-1]


def _key_runs(n, nq, update):
    for bit in _run_bits(nq + 1):
        @pl.when((n & bit) != 0)
        def _(bit=bit):
            update(n & ~(2 * bit - 1), bit, (n & (bit - 1)) == 0)


def _earlier_runs(n, nq, update):
    for bit in _run_bits(nq):
        @pl.when((n & bit) != 0)
        def _(bit=bit):
            update(n & ~(2 * bit - 1), bit, False)


def _chunk_visible(shape, nblk, blk):
    key = lax.broadcasted_iota(jnp.int32, shape, 0) - (nblk - 1) * blk
    query = lax.broadcasted_iota(jnp.int32, shape, 1)
    return jnp.logical_or(key < 0, (key // CHUNK) <= (query // CHUNK))


def _mla_attn_fwd(q, k, v, B, S):
    blk = min(MLA_FWD_BLOCK, S)
    H, nq = MLA_HEADS, S // blk

    def body(q_ref, k_ref, v_ref, o_ref, lse_ref, m_ref, l_ref, acc_ref):
        def qblock(i, _):
            q_rows = pl.ds(pl.multiple_of(i * blk, blk), blk)
            qi = q_ref[q_rows, :]
            m_ref[...] = jnp.full(m_ref.shape, MASK_VALUE, F32)
            l_ref[...] = jnp.zeros(l_ref.shape, F32)
            acc_ref[...] = jnp.zeros(acc_ref.shape, F32)

            def keys(first, nblk, last):
                rows = pl.ds(pl.multiple_of(first * blk, blk), nblk * blk)
                s = _dot(k_ref[rows, :], qi, _NT)
                s = jnp.where(jnp.logical_or(_chunk_visible(s.shape, nblk, blk), jnp.logical_not(last)), s, MASK_VALUE)
                m = m_ref[...]
                m2 = jnp.maximum(m, jnp.max(s, axis=0, keepdims=True))
                alpha = jnp.exp(m - m2)
                p = jnp.exp(s - m2)
                l_ref[...] = alpha * l_ref[...] + jnp.sum(p, axis=0, keepdims=True)
                acc_ref[...] = alpha * acc_ref[...] + _dot(v_ref[rows, :], p, _TN)
                m_ref[...] = m2

            _key_runs(i + 1, nq, keys)
            l = l_ref[...]
            o_ref[q_rows, :] = (acc_ref[...] / l).T
            lse_ref[0, :, q_rows] = m_ref[...] + jnp.log(l)
            return 0

        lax.fori_loop(0, nq, qblock, 0)

    return pl.pallas_call(
        body, name="mla_attn_fwd", grid=(B, H),
        in_specs=[pl.BlockSpec((S, MLA_PAD), lambda b, h: (b, h)),
                  pl.BlockSpec((S, MLA_PAD), lambda b, h: (b, h)),
                  pl.BlockSpec((S, MLA_V), lambda b, h: (b, h))],
        out_specs=[pl.BlockSpec((S, MLA_V), lambda b, h: (b, h)),
                   pl.BlockSpec((1, 1, S), lambda b, h: (b * H + h, 0, 0))],
        out_shape=[jax.ShapeDtypeStruct((B * S, H * MLA_V), F32), jax.ShapeDtypeStruct((B * H, 1, S), F32)],
        scratch_shapes=[pltpu.VMEM((1, blk), F32), pltpu.VMEM((1, blk), F32), pltpu.VMEM((MLA_V, blk), F32)],
        compiler_params=_params(("parallel", "parallel")),
    )(q, k, v)


def _mla_attn_bwd(q, k, v, o, do, lse, B, S):
    blk = min(MLA_FWD_BLOCK, S)
    H, nq = MLA_HEADS, S // blk

    def body(q_ref, k_ref, v_ref, o_ref, do_ref, lse_ref, dq_ref, dk_ref, dv_ref, kt_ref, dqt_ref):
        dk_ref[...] = jnp.zeros(dk_ref.shape, F32)
        dv_ref[...] = jnp.zeros(dv_ref.shape, F32)
        for g in range(nq):
            kt_ref[g] = k_ref[g * blk:(g + 1) * blk, :].T

        def qblock(i, _):
            q_rows = pl.ds(pl.multiple_of(i * blk, blk), blk)
            qi = q_ref[q_rows, :]
            doi = do_ref[q_rows, :]
            delta = jnp.sum((doi * o_ref[q_rows, :]).T, axis=0, keepdims=True)
            lse_i = lse_ref[0, :, q_rows]
            doi = doi.astype(MXU_DTYPE)
            dqt_ref[...] = jnp.zeros(dqt_ref.shape, F32)

            def keys(first, nblk, last):
                rows = pl.ds(pl.multiple_of(first * blk, blk), nblk * blk)
                k_run, v_run = k_ref[rows, :], v_ref[rows, :]
                p = jnp.exp(_dot(k_run, qi, _NT) - lse_i)
                p = jnp.where(jnp.logical_or(_chunk_visible(p.shape, nblk, blk), jnp.logical_not(last)), p, 0.0)
                ds = (p * (_dot(v_run, doi, _NT) - delta)).astype(MXU_DTYPE)
                dk_ref[rows, :] += _dot(ds, qi, _NN)
                dv_ref[rows, :] += _dot(p, doi, _NN)
                for r in range(nblk):
                    dqt_ref[...] += _dot(kt_ref[first + r], ds[r * blk:(r + 1) * blk, :], _NN)

            _key_runs(i + 1, nq, keys)
            dq_ref[q_rows, :] = dqt_ref[...].T
            return 0

        lax.fori_loop(0, nq, qblock, 0)

    qk_spec = pl.BlockSpec((S, MLA_PAD), lambda b, h: (b, h))
    v_spec = pl.BlockSpec((S, MLA_V), lambda b, h: (b, h))
    return pl.pallas_call(
        body, name="mla_attn_bwd", grid=(B, H),
        in_specs=[qk_spec, qk_spec, v_spec, v_spec, v_spec,
                  pl.BlockSpec((1, 1, S), lambda b, h: (b * H + h, 0, 0))],
        out_specs=[qk_spec, qk_spec, v_spec],
        out_shape=[jax.ShapeDtypeStruct((B * S, H * MLA_PAD), F32), jax.ShapeDtypeStruct((B * S, H * MLA_PAD), F32),
                   jax.ShapeDtypeStruct((B * S, H * MLA_V), F32)],
        scratch_shapes=[pltpu.VMEM((nq, MLA_PAD, blk), q.dtype), pltpu.VMEM((MLA_PAD, blk), F32)],
        compiler_params=_params(("parallel", "parallel")),
    )(q, k, v, o, do, lse)


def _ret_log_gamma():
    lg = np.log1p(-np.exp2(RET_GAMMA_BASE - np.arange(RET_HEADS, dtype=np.float32))).astype(np.float32)
    return jnp.asarray(np.broadcast_to(lg[:, None, None], (RET_HEADS, 8, LANES)).copy())


RET_BLOCK = 512


def _ret_local_scale(lg, shape, blk, rising):
    local = lax.broadcasted_iota(jnp.int32, shape, 0) % blk
    return jnp.exp(lg * (local if rising else blk - 1 - local).astype(F32))


def _ret_pair_factor(lg, blk, steps):
    return jnp.exp(lg * (blk * (steps - 1) + 1).astype(F32))


def _ret_own_decay(lg, blk, transposed):
    a = lax.broadcasted_iota(jnp.int32, (blk, blk), 0)
    b = lax.broadcasted_iota(jnp.int32, (blk, blk), 1)
    query, key = (b, a) if transposed else (a, b)
    dec = jnp.exp(lg * jnp.abs(query - key).astype(F32))
    return jnp.where((key // CHUNK) <= (query // CHUNK), dec, 0.0)


def _ret_attn_fwd(q, k, v, B, S):
    blk = min(RET_BLOCK, S)
    H, nq = RET_HEADS, S // blk

    def body(lg_ref, q_ref, k_ref, v_ref, o_ref, ks_ref, dec_ref, acc_ref):
        lg = lg_ref[0, 0:1, 0:1]
        ks_ref[...] = (k_ref[...].astype(F32) * _ret_local_scale(lg, k_ref.shape, blk, False)).astype(ks_ref.dtype)
        dec_ref[...] = _ret_own_decay(lg, blk, False)

        def qblock(i, _):
            q_rows = pl.ds(pl.multiple_of(i * blk, blk), blk)
            qi = q_ref[q_rows, :]
            qs = (qi.astype(F32) * _ret_local_scale(lg, qi.shape, blk, True)).astype(qi.dtype)
            a = _dot(qi, k_ref[q_rows, :], _NT) * dec_ref[...]
            acc_ref[...] = _dot(a, v_ref[q_rows, :], _NN)

            def keys(first, nblk, _):
                rows = pl.ds(pl.multiple_of(first * blk, blk), nblk * blk)
                steps = i - first - lax.broadcasted_iota(jnp.int32, (1, nblk * blk), 1) // blk
                a = _dot(qs, ks_ref[rows, :], _NT) * _ret_pair_factor(lg, blk, steps)
                acc_ref[...] += _dot(a, v_ref[rows, :], _NN)

            _earlier_runs(i, nq, keys)
            o_ref[q_rows, :] = acc_ref[...]
            return 0

        lax.fori_loop(0, nq, qblock, 0)

    qk_spec = pl.BlockSpec((S, RET_QK), lambda b, h: (b, h))
    v_spec = pl.BlockSpec((S, RET_V), lambda b, h: (b, h))
    return pl.pallas_call(
        body, name="ret_attn_fwd", grid=(B, H),
        in_specs=[pl.BlockSpec((1, 8, LANES), lambda b, h: (h, 0, 0)), qk_spec, qk_spec, v_spec],
        out_specs=v_spec,
        out_shape=jax.ShapeDtypeStruct((B * S, H * RET_V), F32),
        scratch_shapes=[pltpu.VMEM((S, RET_QK), k.dtype), pltpu.VMEM((blk, blk), F32), pltpu.VMEM((blk, RET_V), F32)],
        compiler_params=_params(("parallel", "parallel")),
    )(_ret_log_gamma(), q, k, v)


def _ret_attn_bwd(q, k, v, do, B, S):
    blk = min(RET_BLOCK, S)
    H, nq = RET_HEADS, S // blk

    def body(lg_ref, q_ref, k_ref, v_ref, do_ref, dq_ref, dk_ref, dv_ref, ks_ref, kst_ref, dks_ref, dqt_ref, dec_ref):
        lg = lg_ref[0, 0:1, 0:1]
        dk_ref[...] = jnp.zeros(dk_ref.shape, F32)
        dv_ref[...] = jnp.zeros(dv_ref.shape, F32)
        dks_ref[...] = jnp.zeros(dks_ref.shape, F32)
        ks_ref[...] = (k_ref[...].astype(F32) * _ret_local_scale(lg, k_ref.shape, blk, False)).astype(ks_ref.dtype)
        for g in range(nq):
            kst_ref[g] = ks_ref[g * blk:(g + 1) * blk, :].T
        dec_ref[...] = _ret_own_decay(lg, blk, True)

        def qblock(i, _):
            q_rows = pl.ds(pl.multiple_of(i * blk, blk), blk)
            qi = q_ref[q_rows, :]
            q_scale = _ret_local_scale(lg, qi.shape, blk, True)
            qs = (qi.astype(F32) * q_scale).astype(qi.dtype)
            doi = do_ref[q_rows, :].astype(MXU_DTYPE)
            ki = k_ref[q_rows, :]
            dec = dec_ref[...]
            a = _dot(ki, qi, _NT) * dec
            da = (_dot(v_ref[q_rows, :], doi, _NT) * dec).astype(MXU_DTYPE)
            dv_ref[q_rows, :] += _dot(a, doi, _NN)
            dk_ref[q_rows, :] += _dot(da, qi, _NN)
            dq_own = _dot(da, ki, _TN)
            dqt_ref[...] = jnp.zeros(dqt_ref.shape, F32)

            def keys(first, nblk, _):
                for r in range(nblk):
                    g = first + r
                    rows = pl.ds(pl.multiple_of(g * blk, blk), blk)
                    c = _ret_pair_factor(lg, blk, i - g)
                    a = _dot(ks_ref[rows, :], qs, _NT) * c
                    da = (_dot(v_ref[rows, :], doi, _NT) * c).astype(MXU_DTYPE)
                    dv_ref[rows, :] += _dot(a, doi, _NN)
                    dks_ref[rows, :] += _dot(da, qs, _NN)
                    dqt_ref[...] += _dot(kst_ref[g], da, _NN)

            _earlier_runs(i, nq, keys)
            dq_ref[q_rows, :] = dqt_ref[...].T * q_scale + dq_own
            return 0

        lax.fori_loop(0, nq, qblock, 0)
        dk_ref[...] += dks_ref[...] * _ret_local_scale(lg, dks_ref.shape, blk, False)

    qk_spec = pl.BlockSpec((S, RET_QK), lambda b, h: (b, h))
    v_spec = pl.BlockSpec((S, RET_V), lambda b, h: (b, h))
    return pl.pallas_call(
        body, name="ret_attn_bwd", grid=(B, H),
        in_specs=[pl.BlockSpec((1, 8, LANES), lambda b, h: (h, 0, 0)), qk_spec, qk_spec, v_spec, v_spec],
        out_specs=[qk_spec, qk_spec, v_spec],
        out_shape=[jax.ShapeDtypeStruct((B * S, H * RET_QK), F32), jax.ShapeDtypeStruct((B * S, H * RET_QK), F32),
                   jax.ShapeDtypeStruct((B * S, H * RET_V), F32)],
        scratch_shapes=[pltpu.VMEM((S, RET_QK), k.dtype), pltpu.VMEM((nq, RET_QK, blk), k.dtype),
                        pltpu.VMEM((S, RET_QK), F32), pltpu.VMEM((RET_QK, blk), F32), pltpu.VMEM((blk, blk), F32)],
        compiler_params=_params(("parallel", "parallel")),
    )(_ret_log_gamma(), q, k, v, do)


def _adamw(w, g, m, v, name):
    R, C = w.shape
    br = R if R * C * 4 <= 2 ** 21 else _pick_rows(R, max(8, (2 ** 21) // (C * 4)))

    def body(w_ref, g_ref, m_ref, v_ref, d_ref, mo_ref, vo_ref):
        g_v = g_ref[...]
        m_v = ADAM_B1 * m_ref[...] + (1.0 - ADAM_B1) * g_v
        v_v = ADAM_B2 * v_ref[...] + (1.0 - ADAM_B2) * (g_v * g_v)
        m_hat = m_v / (1.0 - ADAM_B1 ** ADAM_STEP)
        v_hat = v_v / (1.0 - ADAM_B2 ** ADAM_STEP)
        d_ref[...] = -ADAM_LR * (m_hat / (jnp.sqrt(v_hat) + ADAM_EPS) + ADAM_WD * w_ref[...])
        mo_ref[...] = m_v
        vo_ref[...] = v_v

    blk = pl.BlockSpec((br, C), lambda i: (i, 0))
    return pl.pallas_call(
        body, name=name, grid=(R // br,),
        in_specs=[blk] * 4, out_specs=[blk] * 3,
        out_shape=[jax.ShapeDtypeStruct((R, C), F32)] * 3,
        compiler_params=_params(("parallel",)),
    )(w, g, m, v)


def _pick_rows(R, target):
    best = None
    for d in range(8, min(R, target) + 1, 8):
        if R % d == 0:
            best = d
    assert best is not None, (R, target)
    return best


def _position():
    return lax.axis_index("x"), lax.axis_index("y"), lax.axis_index("c")


HBM_SPEC = pl.BlockSpec(memory_space=pltpu.HBM)


def _other_chips(x, y):
    return [(1 - x, y), (x, 1 - y), (1 - x, 1 - y)]


def _all_gather_weights(bigs, small):
    nb = len(bigs)

    def body(*refs):
        big_refs, small_ref = refs[:nb], refs[nb]
        obig, osmall = refs[nb + 1:2 * nb + 1], refs[2 * nb + 1]
        ici_send, ici_recv, d2d_send, d2d_recv, sm_send, sm_recv = refs[2 * nb + 2:]
        x, y, c = _position()
        me = 2 * x + y
        chips = _other_chips(x, y)

        def rows(n, half):
            rh = bigs[n].shape[0] // 2
            return pl.ds(half * rh, rh)

        def over_ici(n, j, slot, from_shard):
            px, py = chips[j]
            dst = obig[n].at[slot, rows(n, c)]
            return pltpu.make_async_remote_copy(
                src_ref=big_refs[n].at[rows(n, c)] if from_shard else dst, dst_ref=dst,
                send_sem=ici_send.at[3 * n + j], recv_sem=ici_recv.at[3 * n + j],
                device_id=(px, py, c), device_id_type=MESH)

        def over_d2d(n, j, half):
            px, py = chips[j]
            part = obig[n].at[2 * px + py, rows(n, half)]
            return pltpu.make_async_remote_copy(
                src_ref=part, dst_ref=part, send_sem=d2d_send.at[3 * n + j], recv_sem=d2d_recv.at[3 * n + j],
                device_id=(x, y, 1 - c), device_id_type=MESH)

        def small_copy(j, slot):
            px, py = chips[j]
            return pltpu.make_async_remote_copy(
                src_ref=small_ref, dst_ref=osmall.at[slot], send_sem=sm_send.at[j], recv_sem=sm_recv.at[j],
                device_id=(px, py, c), device_id_type=MESH)

        sends = [over_ici(n, j, me, True) for n in range(nb) for j in range(3)]
        sends += [small_copy(j, me) for j in range(3)]
        for cp in sends:
            cp.start()
        passed = []
        for n in range(nb):
            for j, (px, py) in enumerate(chips):
                over_ici(n, j, 2 * px + py, False).wait_recv()
                fwd = over_d2d(n, j, c)
                fwd.start()
                passed.append(fwd)
        for n in range(nb):
            for j in range(3):
                over_d2d(n, j, 1 - c).wait_recv()
        for j, (px, py) in enumerate(chips):
            small_copy(j, 2 * px + py).wait_recv()
        for cp in sends + passed:
            cp.wait_send()

    dma = pltpu.SemaphoreType.DMA
    return pl.pallas_call(
        body, name="weights_all_gather",
        in_specs=[HBM_SPEC] * (nb + 1), out_specs=[HBM_SPEC] * (nb + 1),
        out_shape=[jax.ShapeDtypeStruct((N_SHARD,) + b.shape, b.dtype) for b in bigs]
        + [jax.ShapeDtypeStruct((N_SHARD,) + small.shape, small.dtype)],
        scratch_shapes=[dma((3 * nb,)), dma((3 * nb,)), dma((3 * nb,)), dma((3 * nb,)), dma((3,)), dma((3,))],
    )(*bigs, small)


SEM_SPEC = pl.BlockSpec(memory_space=pltpu.SEMAPHORE)
DATAFLOW_EFFECT = pltpu.SideEffectType.DATAFLOW_SIDE_EFFECTING
N_PEERS = N_DEV - 1


def _grad_copies(p_refs, land_refs, send_sems, recv_sems):
    x, y, c = _position()
    copies = []
    for a, (p_ref, land_ref) in enumerate(zip(p_refs, land_refs)):
        rh = p_ref.shape[1] // 2
        for k in range(1, N_DEV):
            px = 1 - x if k & 4 else x
            py = 1 - y if k & 2 else y
            pc = 1 - c if k & 1 else c
            copies.append(pltpu.make_async_remote_copy(
                src_ref=p_ref.at[2 * px + py, pl.ds(pc * rh, rh)], dst_ref=land_ref.at[k - 1],
                send_sem=send_sems.at[N_PEERS * a + k - 1], recv_sem=recv_sems.at[N_PEERS * a + k - 1],
                device_id=(px, py, pc), device_id_type=MESH))
    return copies


def _weight_copies(w_refs, land_refs, send_sems, recv_sems):
    x, y, c = _position()
    copies = []
    for a, (w_ref, land_ref) in enumerate(zip(w_refs, land_refs)):
        for j, (px, py) in enumerate(_other_chips(x, y)):
            copies.append(pltpu.make_async_remote_copy(
                src_ref=w_ref, dst_ref=land_ref.at[2 * x + y], send_sem=send_sems.at[3 * a + j],
                recv_sem=recv_sems.at[3 * a + j], device_id=(px, py, c), device_id_type=MESH))
    return copies


def _exchange_start(make_copies, srcs, lands, n_sems, name, after=None):
    n, m = len(srcs), len(lands)
    n_in = n + m + (after is not None)

    def body(*refs):
        send_sems, recv_sems, token = refs[n_in], refs[n_in + 1], refs[-1]
        for cp in make_copies(refs[:n], refs[n:n + m], send_sems, recv_sems):
            cp.start()
        token[...] = jnp.zeros(token.shape, token.dtype)

    hbm = lambda a: pltpu.with_memory_space_constraint(a, pltpu.HBM)
    dma = pltpu.SemaphoreType.DMA
    res = pl.pallas_call(
        body, name=name,
        in_specs=[HBM_SPEC] * (n + m) + ([] if after is None else [pl.BlockSpec(memory_space=pl.ANY)]),
        out_specs=[SEM_SPEC, SEM_SPEC] + [HBM_SPEC] * (n + m) + [pl.BlockSpec(memory_space=pltpu.VMEM)],
        out_shape=[dma((n_sems,)), dma((n_sems,))] + [pltpu.HBM(a.shape, a.dtype) for a in list(srcs) + list(lands)]
        + [jax.ShapeDtypeStruct((8, LANES), F32)],
        input_output_aliases={i: 2 + i for i in range(n + m)},
        compiler_params=pltpu.CompilerParams(has_side_effects=DATAFLOW_EFFECT),
    )(*[hbm(a) for a in srcs], *[hbm(a) for a in lands], *(() if after is None else (after,)))
    return res[0], res[1], list(res[2:2 + n]), list(res[2 + n:2 + n + m]), res[-1]


def _exchange_wait(make_copies, send_sems, recv_sems, srcs, lands, after, name):
    n, m = len(srcs), len(lands)

    def body(*refs):
        for cp in make_copies(refs[:n], refs[n:n + m], refs[n + m], refs[n + m + 1]):
            cp.wait_send()
            cp.wait_recv()

    res = pl.pallas_call(
        body, name=name,
        in_specs=[HBM_SPEC] * (n + m) + [SEM_SPEC, SEM_SPEC, pl.BlockSpec(memory_space=pl.ANY)],
        out_specs=[HBM_SPEC] * (n + m),
        out_shape=[pltpu.HBM(a.shape, a.dtype) for a in list(srcs) + list(lands)],
        input_output_aliases={i: i for i in range(n + m)},
        compiler_params=pltpu.CompilerParams(has_side_effects=DATAFLOW_EFFECT),
    )(*srcs, *lands, send_sems, recv_sems, after)
    return list(res[:n]), list(res[n:])


def _sum_partials(p, land, name):
    _, rh, cols = land.shape
    br = _pick_rows(rh, 256)
    nrb = rh // br
    x, y, c = _position()
    where = jnp.stack([2 * x + y, c]).astype(jnp.int32)

    def body(where_ref, p_ref, land_ref, o_ref):
        acc = p_ref[...].astype(F32)
        for k in range(N_PEERS):
            acc = acc + land_ref[k].astype(F32)
        o_ref[...] = acc

    return pl.pallas_call(
        body, name=name,
        grid_spec=pltpu.PrefetchScalarGridSpec(
            num_scalar_prefetch=1, grid=(nrb,),
            in_specs=[pl.BlockSpec((None, br, cols), lambda r, where_ref: (where_ref[0], where_ref[1] * nrb + r, 0)),
                      pl.BlockSpec((N_PEERS, br, cols), lambda r, where_ref: (0, r, 0))],
            out_specs=pl.BlockSpec((None, br, cols), lambda r, where_ref: (where_ref[1], r, 0))),
        out_shape=jax.ShapeDtypeStruct((2, rh, cols), F32),
        compiler_params=_params(("parallel",)),
    )(where, p, land)


def _sibling_share(fulls, name):
    n = len(fulls)

    def body(*refs):
        o_refs = refs[n:2 * n]
        send_sems, recv_sems = refs[2 * n:]
        x, y, c = _position()

        def copy(a, half):
            return pltpu.make_async_remote_copy(
                src_ref=o_refs[a].at[half], dst_ref=o_refs[a].at[half], send_sem=send_sems.at[a],
                recv_sem=recv_sems.at[a], device_id=(x, y, 1 - c), device_id_type=MESH)

        sends = [copy(a, c) for a in range(n)]
        for cp in sends:
            cp.start()
        for a in range(n):
            copy(a, 1 - c).wait_recv()
        for cp in sends:
            cp.wait_send()

    dma = pltpu.SemaphoreType.DMA
    return pl.pallas_call(
        body, name=name,
        in_specs=[HBM_SPEC] * n, out_specs=[HBM_SPEC] * n,
        out_shape=[jax.ShapeDtypeStruct(f.shape, f.dtype) for f in fulls],
        input_output_aliases={a: a for a in range(n)},
        scratch_shapes=[dma((n,)), dma((n,))],
    )(*fulls)


def _all_reduce_small(v):
    R, cols = v.shape

    def body(v_ref, o_ref, buf_ref, send_sems, recv_sems):
        x, y, c = _position()
        me = 4 * x + 2 * y + c
        buf_ref[me] = v_ref[...]
        sends = []
        for k in range(1, N_DEV):
            px = 1 - x if k & 4 else x
            py = 1 - y if k & 2 else y
            pc = 1 - c if k & 1 else c
            sends.append(pltpu.make_async_remote_copy(
                src_ref=v_ref, dst_ref=buf_ref.at[me], send_sem=send_sems.at[k - 1], recv_sem=recv_sems.at[k - 1],
                device_id=(px, py, pc), device_id_type=MESH))
        for cp in sends:
            cp.start()
        for k in range(1, N_DEV):
            px = 1 - x if k & 4 else x
            py = 1 - y if k & 2 else y
            pc = 1 - c if k & 1 else c
            pltpu.make_async_remote_copy(
                src_ref=v_ref, dst_ref=buf_ref.at[4 * px + 2 * py + pc], send_sem=send_sems.at[k - 1],
                recv_sem=recv_sems.at[k - 1], device_id=(px, py, pc), device_id_type=MESH).wait_recv()
        for cp in sends:
            cp.wait_send()
        acc = buf_ref[0]
        for d in range(1, N_DEV):
            acc = acc + buf_ref[d]
        o_ref[...] = acc

    return pl.pallas_call(
        body, name="small_grads_all_reduce",
        in_specs=[pl.BlockSpec(memory_space=pltpu.VMEM)], out_specs=pl.BlockSpec(memory_space=pltpu.VMEM),
        out_shape=jax.ShapeDtypeStruct((R, cols), F32),
        scratch_shapes=[pltpu.VMEM((N_DEV, R, cols), F32), pltpu.SemaphoreType.DMA((N_DEV - 1,)),
                        pltpu.SemaphoreType.DMA((N_DEV - 1,))],
    )(v)


def _rope_tables(S, half, width):
    inv_freq = ROPE_THETA ** (-jnp.arange(half, dtype=F32) / half)
    ang = jnp.arange(S).astype(F32)[:, None] * inv_freq[None, :]
    return jnp.cos(ang), jnp.sin(ang)


def _slot_rows(a):
    return a.reshape(N_SHARD, -1, a.shape[-1])


def _local_step(x, target, w, B, S, late, exchange, reduce_small):
    T = B * S
    D = D_MODEL
    bm = min(512, S)
    full = lambda a, wd, tile=None: (a, wd, 0, tile or wd)
    g = {}

    cos_r, sin_r = _rope_tables(S, RET_QK // 2, LANES)
    cos_m, sin_m = _rope_tables(S, MLA_ROPE // 2, LANES)
    zeros64 = jnp.zeros((S, 64), F32)
    cos_m = jnp.concatenate([cos_m, cos_m, zeros64], axis=1)
    sin_m = jnp.concatenate([-sin_m, sin_m, zeros64], axis=1)

    def ffn_fwd(xin, h, ht, i, next_gain):
        w.update(late(f"ffn{i}", xin))
        norm = w["ffn_norm"][i:i + 1]
        ag = _mm(h, w[f"ffn_w_in{i}"], "nn", BF16, f"ffn{i}_in", bm=1024, bn=2816, cols_outer=True)
        u, ut = _conv_fwd(ag, w["ffn_conv8"][i], B, S, f"ffn{i}_conv")
        if next_gain is None:
            out = _mm_out_loss(u, w[f"ffn_w_out{i}"], xin, target, f"ffn{i}_out")
        else:
            out = _mm_out_norm(u, w[f"ffn_w_out{i}"], xin, next_gain, f"ffn{i}_out")
        return out, (xin, norm, ht, ag, ut)

    def ffn_bwd(dxout, dxout_c, saved, i):
        xin, norm, ht, ag, ut = saved
        du = _mm(dxout_c, w[f"ffn_w_out{i}"], "nt", F32, f"ffn{i}_out_dx", bm=1024, bn=1408, cols_outer=True)
        g_w_out = _mm(ut, dxout_c, "nn", BF16, f"ffn{i}_out_dw", bm=1408, bn=512, bk=T)
        da, dg, dw8 = _conv_bwd(ag, w["ffn_conv8"][i], du, B, S, f"ffn{i}_conv_bwd")
        g_w_in = _mm(ht, [da, dg], "nn", BF16, f"ffn{i}_in_dw", bm=1024, bn=1408, bk=T // 2, out_slots=N_SHARD)
        token = exchange(f"ffn{i}", [g_w_in, _slot_rows(g_w_out)])
        dxin, dxin_c, g_norm = _mm_dx_norm([da, dg], w[f"ffn_w_in{i}"], xin, norm, dxout, f"ffn{i}_in_dx", after=token)
        return dxin, dxin_c, (g_norm, dw8)

    h0, h0t = _rowwise_fwd(_fn_rms, "ret_norm", [full(x, D)], [], [(w["ret_norm"], D)], [(D, D, BF16)], bm, S,
                           transposed=(0,))
    proj = _mm(h0, w["ret_w_in"], "nn", BF16, "ret_in", bm=1024, bn=2048, after=w["started"], cols_outer=True)
    HQ, HV = RET_HEADS * RET_QK, RET_HEADS * RET_V
    rope_rows = [(proj, 2 * HQ + HV, 0, LANES)]
    q_r, k_r, v_r = _rowwise_fwd(_fn_ret_rope, "ret_rope", rope_rows, [cos_r, sin_r], [],
                                 [(HQ, LANES, BF16), (HQ, LANES, BF16), (HV, LANES, BF16)], bm, S)
    ret_o = _ret_attn_fwd(q_r, k_r, v_r, B, S)
    gate_rows = [full(ret_o, HV, RET_V), (proj, HV, 2, RET_V)]
    y0, y0t = _rowwise_fwd(_fn_ret_gate, "ret_gate", gate_rows, [], [(w["ret_gn"], RET_V)], [(HV, RET_V, BF16)], bm, S,
                           transposed=(0,))
    w.update(late("ret_out", y0))
    x1, h1, h1t = _mm_out_norm(y0, w["ret_w_out"], x, w["ffn_norm"][0:1], "ret_out")
    (x2, h2, _), ffn0_saved = ffn_fwd(x1, h1, h1t, 0, w["mla_norm"])

    w.update(late("mla", x2))
    proj2 = _mm(h2, w["mla_w_in"], "nn", F32, "mla_in", bm=2048)
    lat_consts = [(w["mla_q_norm"], LANES), (w["mla_kv_norm"], LANES)]
    cqn, ckvn, kr = _rowwise_fwd(_fn_mla_lat, "mla_latent_norm", [full(proj2, MLA_IN_PAD, LANES)], [], lat_consts,
                                 [(MLA_Q_RANK, LANES, BF16), (MLA_KV_RANK, LANES, BF16), (LANES, LANES, F32)], bm, S)
    qf = _mm(cqn, w["mla_w_qb"], "nn", BF16, "mla_qb", bm=2048, bn=2048)
    kvf = _mm(ckvn, w["mla_w_kvb"], "nn", BF16, "mla_kvb", bm=2048, bn=2048)
    HP, HVm = MLA_HEADS * MLA_PAD, MLA_HEADS * MLA_V
    head_rows = [full(qf, HP, LANES), full(kvf, HP, LANES), full(kr, LANES)]
    head_consts = [(w["mla_q_head_norm"], LANES), (w["mla_k_head_norm"], LANES)]
    q_a, k_a, v_a = _rowwise_fwd(_fn_mla_heads, "mla_heads", head_rows, [cos_m, sin_m], head_consts,
                                 [(HP, LANES, BF16), (HP, LANES, BF16), (HVm, LANES, BF16)], bm, S)
    att_o, lse = _mla_attn_fwd(q_a, k_a, v_a, B, S)
    x3, h3, h3t = _mm_out_norm(att_o, w["mla_w_out"], x2, w["ffn_norm"][1:2], "mla_out")
    (dy, dy_c, loss), ffn1_saved = ffn_fwd(x3, h3, h3t, 1, None)

    dx3, dx3_c, (g_n1, dw8_1) = ffn_bwd(dy, dy_c, ffn1_saved, 1)

    d_att_o = _mm(dx3_c, w["mla_w_out"], "nt", F32, "mla_out_dx", bm=2048)
    g_mla_out = _mm(att_o, dx3_c, "tn", BF16, "mla_out_dw")
    dq_a, dk_a, dv_a = _mla_attn_bwd(q_a, k_a, v_a, att_o, d_att_o, lse, B, S)
    (dqf, dkvf, dkr), (g["mla_q_head_norm"], g["mla_k_head_norm"]) = _rowwise_bwd(
        _fn_mla_heads, "mla_heads_bwd", head_rows, [cos_m, sin_m], head_consts,
        [(dq_a, LANES), (dk_a, LANES), (dv_a, LANES)], bm, S, grad_dtypes=[BF16, BF16, F32])
    dcqn = _mm(dqf, w["mla_w_qb"], "nt", F32, "mla_qb_dx", bm=2048)
    g_qb = _mm(cqn, dqf, "tn", BF16, "mla_qb_dw")
    g_qb = _to_slots(_unpad_heads(g_qb, 1), 1).reshape(N_SHARD, MLA_Q_RANK, -1)
    dckvn = _mm(dkvf, w["mla_w_kvb"], "nt", F32, "mla_kvb_dx", bm=2048)
    g_kvb = _mm(ckvn, dkvf, "tn", BF16, "mla_kvb_dw", bn=512, out_slots=N_SHARD)
    (dproj2,), (g["mla_q_norm"], g["mla_kv_norm"]) = _rowwise_bwd(
        _fn_mla_lat, "mla_latent_norm_bwd", [full(proj2, MLA_IN_PAD, LANES)], [], lat_consts,
        [(dcqn, LANES), (dckvn, LANES), (dkr, LANES)], bm, S, grad_dtypes=[BF16])
    g_mla_in = _mm(h2, dproj2, "tn", BF16, "mla_in_dw")
    token = exchange("mla", [_slot_rows(g_mla_in[:, :MLA_IN]), g_qb, g_kvb, _slot_rows(g_mla_out)])
    dx2, dx2_c, g["mla_norm"] = _mm_dx_norm([dproj2], w["mla_w_in"], x2, w["mla_norm"], dx3, "mla_in_dx", bm=512,
                                            after=token)

    dx1, dx1_c, (g_n0, dw8_0) = ffn_bwd(dx2, dx2_c, ffn0_saved, 0)

    dy0 = _mm(dx1_c, w["ret_w_out"], "nt", F32, "ret_out_dx", bm=1024, cols_outer=True)
    g_ret_out = _mm(y0t, dx1_c, "nn", BF16, "ret_out_dw", bm=1024, bn=512, bk=T)
    token = exchange("reto", [_slot_rows(g_ret_out)])
    gn_behind = w["ret_gn"] + token[0:1, 0:1]
    (d_ret_o, dgate), (g["ret_gn"],) = _rowwise_bwd(_fn_ret_gate, "ret_gate_bwd", gate_rows, [], [(gn_behind, RET_V)],
                                                    [(dy0, RET_V)], bm, S, grad_dtypes=[F32, BF16])
    dq_r, dk_r, dv_r = _ret_attn_bwd(q_r, k_r, v_r, d_ret_o, B, S)
    (dqkv,), _ = _rowwise_bwd(_fn_ret_rope, "ret_rope_bwd", rope_rows, [cos_r, sin_r], [],
                              [(dq_r, LANES), (dk_r, LANES), (dv_r, LANES)], bm, S, grad_dtypes=[BF16], linear=True)
    dx, _, g["ret_norm"] = _mm_dx_norm([dqkv, dgate], w["ret_w_in"], x, w["ret_norm"], dx1, "ret_in_dx")
    g["ffn_norm"] = jnp.concatenate([g_n0, g_n1], axis=0)
    g["ffn_conv_w"] = jnp.stack([dw8_0[0:3], dw8_1[0:3]])
    g["ffn_conv_b"] = jnp.stack([dw8_0[3], dw8_1[3]])
    reduced_small = reduce_small(g, loss)
    g_ret_in = _mm(h0t, [dqkv, dgate], "nn", BF16, "ret_in_dw", bm=1024, bn=512, bk=T, out_slots=N_SHARD,
                   after=reduced_small)
    exchange("ret", [g_ret_in])
    return loss, dx, reduced_small


_SMALL_SHARDED = [("ret_gn", 2), ("mla_norm", 1), ("mla_q_norm", 1), ("mla_kv_norm", 1), ("ffn_conv_w", 2)]
_SMALL_REPLICATED = ["ret_norm", "mla_q_head_norm", "mla_k_head_norm", "ffn_norm", "ffn_conv_b"]
_SMALL_ALL = ["ret_norm", "ret_gn", "mla_norm", "mla_q_norm", "mla_kv_norm", "mla_q_head_norm", "mla_k_head_norm",
              "ffn_norm", "ffn_conv_w", "ffn_conv_b"]


def _to_slots(full, axis):
    shape = full.shape
    split = shape[:axis] + (N_SHARD, shape[axis] // N_SHARD) + shape[axis + 1:]
    return jnp.moveaxis(full.reshape(split), axis, 0).reshape(N_SHARD, -1)


def _from_slots(slots, shard_shape, axis):
    parts = jnp.moveaxis(slots.reshape((N_SHARD,) + tuple(shard_shape)), 0, axis)
    full = shard_shape[:axis] + (N_SHARD * shard_shape[axis],) + shard_shape[axis + 1:]
    return parts.reshape(full)


def _pad_rows(flat, cols, row_unit):
    n, L = flat.shape
    unit = cols * row_unit
    Lp = -(-L // unit) * unit
    if Lp != L:
        flat = jnp.concatenate([flat, jnp.zeros((n, Lp - L), flat.dtype)], axis=1)
    return flat.reshape(n, Lp // cols, cols)


def _pad_heads(a, axis):
    shape = a.shape
    heads = shape[axis] // MLA_QK
    a = a.reshape(shape[:axis] + (heads, MLA_QK) + shape[axis + 1:])
    pad = [(0, 0)] * a.ndim
    pad[axis + 1] = (0, MLA_PAD - MLA_QK)
    return jnp.pad(a, pad).reshape(shape[:axis] + (heads * MLA_PAD,) + shape[axis + 1:])


def _unpad_heads(a, axis):
    shape = a.shape
    a = a.reshape(shape[:axis] + (MLA_HEADS, MLA_PAD) + shape[axis + 1:])
    a = lax.slice_in_dim(a, 0, MLA_QK, axis=axis + 1)
    return a.reshape(shape[:axis] + (MLA_HEADS * MLA_QK,) + shape[axis + 1:])


def kernel(x, ret_norm, ret_w_in, ret_gn, ret_w_out, mla_norm, mla_w_in, mla_q_norm, mla_w_qb, mla_kv_norm, mla_w_kvb, mla_q_head_norm, mla_k_head_norm, mla_w_out, ffn_norm, ffn_w_in, ffn_conv_w, ffn_conv_b, ffn_w_out, loss_target, m_ret_norm, m_ret_w_in, m_ret_gn, m_ret_w_out, m_mla_norm, m_mla_w_in, m_mla_q_norm, m_mla_w_qb, m_mla_kv_norm, m_mla_w_kvb, m_mla_q_head_norm, m_mla_k_head_norm, m_mla_w_out, m_ffn_norm, m_ffn_w_in, m_ffn_conv_w, m_ffn_conv_b, m_ffn_w_out, v_ret_norm, v_ret_w_in, v_ret_gn, v_ret_w_out, v_mla_norm, v_mla_w_in, v_mla_q_norm, v_mla_w_qb, v_mla_kv_norm, v_mla_w_kvb, v_mla_q_head_norm, v_mla_k_head_norm, v_mla_w_out, v_ffn_norm, v_ffn_w_in, v_ffn_conv_w, v_ffn_conv_b, v_ffn_w_out):
    names = ["ret_norm", "ret_w_in", "ret_gn", "ret_w_out", "mla_norm", "mla_w_in", "mla_q_norm", "mla_w_qb",
             "mla_kv_norm", "mla_w_kvb", "mla_q_head_norm", "mla_k_head_norm", "mla_w_out", "ffn_norm", "ffn_w_in",
             "ffn_conv_w", "ffn_conv_b", "ffn_w_out"]
    shard = dict(zip(names, [ret_norm, ret_w_in, ret_gn, ret_w_out, mla_norm, mla_w_in, mla_q_norm, mla_w_qb,
                             mla_kv_norm, mla_w_kvb, mla_q_head_norm, mla_k_head_norm, mla_w_out, ffn_norm, ffn_w_in,
                             ffn_conv_w, ffn_conv_b, ffn_w_out]))
    mom_m = dict(zip(names, [m_ret_norm, m_ret_w_in, m_ret_gn, m_ret_w_out, m_mla_norm, m_mla_w_in, m_mla_q_norm,
                             m_mla_w_qb, m_mla_kv_norm, m_mla_w_kvb, m_mla_q_head_norm, m_mla_k_head_norm, m_mla_w_out,
                             m_ffn_norm, m_ffn_w_in, m_ffn_conv_w, m_ffn_conv_b, m_ffn_w_out]))
    mom_v = dict(zip(names, [v_ret_norm, v_ret_w_in, v_ret_gn, v_ret_w_out, v_mla_norm, v_mla_w_in, v_mla_q_norm,
                             v_mla_w_qb, v_mla_kv_norm, v_mla_w_kvb, v_mla_q_head_norm, v_mla_k_head_norm, v_mla_w_out,
                             v_ffn_norm, v_ffn_w_in, v_ffn_conv_w, v_ffn_conv_b, v_ffn_w_out]))
    B, S, D = x.shape
    T = B * S
    sx, sy = lax.axis_index("x"), lax.axis_index("y")
    me = 2 * sx + sy

    two_d = lambda a: a.reshape(-1, a.shape[-1])
    small_sizes = [int(np.prod(shard[n].shape)) for n, _ in _SMALL_SHARDED]
    small = jnp.concatenate([shard[n].reshape(1, -1) for n, _ in _SMALL_SHARDED], axis=1)
    small = _pad_rows(small, LANES, 8)[0]
    as_mxu = lambda a: two_d(a).astype(BF16)
    zero = jnp.zeros((), jnp.int32)
    with_own = lambda gathered, own: lax.dynamic_update_slice(gathered, own[None], (me.astype(jnp.int32), zero, zero))
    by_cols = lambda a: jnp.moveaxis(a, 0, 1).reshape(a.shape[1], -1)
    by_rows = lambda a: a.reshape(-1, a.shape[-1])
    mla_in_shard = jnp.pad(as_mxu(shard["mla_w_in"]), ((0, 0), (0, MLA_IN_PAD - MLA_IN)))
    mla_qb_shard = _pad_heads(as_mxu(shard["mla_w_qb"]), 1)
    ret_in_shard = as_mxu(shard["ret_w_in"])
    g_ret_in, gsmall = _all_gather_weights([ret_in_shard], small)
    later = [
        ("ret_out", [("ret_w_out", as_mxu(shard["ret_w_out"]), by_rows)]),
        ("ffn0", [("ffn_w_in0", as_mxu(shard["ffn_w_in"][0]), by_cols), ("ffn_w_out0", as_mxu(shard["ffn_w_out"][0]), by_rows)]),
        ("mla", [("mla_w_in", mla_in_shard, by_rows), ("mla_w_qb", mla_qb_shard, by_cols),
                 ("mla_w_kvb", as_mxu(shard["mla_w_kvb"]), by_cols), ("mla_w_out", as_mxu(shard["mla_w_out"]), by_rows)]),
        ("ffn1", [("ffn_w_in1", as_mxu(shard["ffn_w_in"][1]), by_cols), ("ffn_w_out1", as_mxu(shard["ffn_w_out"][1]), by_rows)]),
    ]
    gathering = {}
    token = gsmall
    for group, items in later:
        shards = [s_ for _, s_, _ in items]
        lands = [lax.empty((N_SHARD,) + s_.shape, s_.dtype) for s_ in shards]
        send_sems, recv_sems, shards, lands, token = _exchange_start(
            _weight_copies, shards, lands, 3 * len(shards), f"weights_start_{group}", after=token)
        gathering[group] = (send_sems, recv_sems, shards, lands, items)

    def late(group, after):
        send_sems, recv_sems, shards, lands, items = gathering[group]
        shards, lands = _exchange_wait(_weight_copies, send_sems, recv_sems, shards, lands, after,
                                       f"weights_wait_{group}")
        return {key: full(with_own(l_, s_)) for (key, _, full), s_, l_ in zip(items, shards, lands)}

    gsmall = with_own(gsmall, small).reshape(N_SHARD, -1)
    wfull = {}
    off = 0
    for (n, ax), sz in zip(_SMALL_SHARDED, small_sizes):
        wfull[n] = _from_slots(gsmall[:, off:off + sz], shard[n].shape, ax)
        off += sz
    for n in _SMALL_REPLICATED:
        wfull[n] = shard[n]

    conv8 = jnp.concatenate([wfull["ffn_conv_w"], wfull["ffn_conv_b"][:, None, :],
                             jnp.zeros((2, 4, FFN_DIM), F32)], axis=1)
    w = {
        "started": token, "ret_norm": wfull["ret_norm"], "ret_w_in": by_cols(with_own(g_ret_in, ret_in_shard)),
        "ret_gn": wfull["ret_gn"].reshape(1, RET_HEADS * RET_V), "mla_norm": wfull["mla_norm"],
        "mla_q_norm": wfull["mla_q_norm"], "mla_kv_norm": wfull["mla_kv_norm"],
        "mla_q_head_norm": jnp.pad(wfull["mla_q_head_norm"], ((0, 0), (0, MLA_PAD - MLA_QK))),
        "mla_k_head_norm": jnp.pad(wfull["mla_k_head_norm"], ((0, 0), (0, MLA_PAD - MLA_QK))),
        "ffn_norm": wfull["ffn_norm"], "ffn_conv8": conv8,
    }

    started = {}

    def exchange(group, arrays):
        lands = [lax.empty((N_PEERS, p.shape[1] // 2, p.shape[2]), p.dtype) for p in arrays]
        send_sems, recv_sems, ps, lands, token = _exchange_start(
            _grad_copies, arrays, lands, N_PEERS * len(arrays), f"grads_start_{group}")
        started[group] = (send_sems, recv_sems, ps, lands)
        return token

    small_shapes = {
        "ret_norm": (1, D_MODEL), "ret_gn": (1, RET_HEADS, RET_V), "mla_norm": (1, D_MODEL),
        "mla_q_norm": (1, MLA_Q_RANK), "mla_kv_norm": (1, MLA_KV_RANK), "mla_q_head_norm": (1, MLA_QK),
        "mla_k_head_norm": (1, MLA_QK), "ffn_norm": (2, D_MODEL), "ffn_conv_w": (2, 3, FFN_DIM),
        "ffn_conv_b": (2, FFN_DIM)}

    def reduce_small(gl, loss_part):
        gl = dict(gl, mla_q_head_norm=gl["mla_q_head_norm"][:, :MLA_QK], mla_k_head_norm=gl["mla_k_head_norm"][:, :MLA_QK])
        packed = jnp.concatenate([gl[n].reshape(1, -1) for n in _SMALL_ALL] + [loss_part.reshape(1, 1)], axis=1)
        return _all_reduce_small(_pad_rows(packed, LANES, 8)[0])

    _, dx, gsm = _local_step(x.reshape(T, D), loss_target.reshape(T, D), w, B, S, late, exchange, reduce_small)

    delta, new_m, new_v, grads = {}, {}, {}, {}

    def reduced(group, after):
        send_sems, recv_sems, ps, lands = started[group]
        ps, lands = _exchange_wait(_grad_copies, send_sems, recv_sems, ps, lands, after, f"grads_wait_{group}")
        halves = [_sum_partials(p_, l_, f"grads_sum_{group}_{i}") for i, (p_, l_) in enumerate(zip(ps, lands))]
        return [two_d(r) for r in _sibling_share(halves, f"grads_share_{group}")]

    def adamw(n, g_):
        shp = shard[n].shape
        grads[n] = g_.reshape(shp)
        flat = lambda a: a.reshape(-1, shp[-1])
        d_, m_, v_ = _adamw(flat(shard[n]), flat(grads[n]), flat(mom_m[n]), flat(mom_v[n]), f"adamw_{n}")
        delta[n], new_m[n], new_v[n] = d_.reshape(shp), m_.reshape(shp), v_.reshape(shp)
        return d_

    ffn1 = reduced("ffn1", started["ret"][2][0])
    mla = reduced("mla", ffn1[0])
    ffn0 = reduced("ffn0", mla[0])
    reto = reduced("reto", ffn0[0])
    early = [adamw(n, g_) for n, g_ in zip(["mla_w_in", "mla_w_qb", "mla_w_kvb", "mla_w_out"], mla)]
    early.append(adamw("ffn_w_in", jnp.stack([ffn0[0], ffn1[0]])))
    early.append(adamw("ffn_w_out", jnp.stack([ffn0[1], ffn1[1]])))
    early.append(adamw("ret_w_out", reto[0]))
    ret = reduced("ret", jnp.stack([d_[0, 0] for d_ in early]))
    adamw("ret_w_in", ret[0])

    gsm = gsm.reshape(-1)
    sharded_axis = dict(_SMALL_SHARDED)
    off = 0
    for n in _SMALL_ALL:
        sz = int(np.prod(small_shapes[n]))
        gn = gsm[off:off + sz].reshape(small_shapes[n])
        off += sz
        if n in sharded_axis:
            ax = sharded_axis[n]
            width = shard[n].shape[ax]
            gn = lax.dynamic_slice_in_dim(gn, me * width, width, axis=ax)
        grads[n] = gn
    loss = gsm[off]

    pack_small = lambda d: _pad_rows(jnp.concatenate([d[n].reshape(1, -1) for n in _SMALL_ALL], axis=1), LANES, 8)[0]
    d_, m_, v_ = _adamw(pack_small(shard), pack_small(grads), pack_small(mom_m), pack_small(mom_v), "adamw_small")
    off = 0
    for n in _SMALL_ALL:
        sz = int(np.prod(shard[n].shape))
        for dst, src in ((delta, d_), (new_m, m_), (new_v, v_)):
            dst[n] = src.reshape(-1)[off:off + sz].reshape(shard[n].shape)
        off += sz

    return (loss, dx.reshape(B, S, D), *[grads[n] for n in names], *[delta[n] for n in names],
            *[new_m[n] for n in names], *[new_v[n] for n in names])
```

```python
import functools

import numpy as np
import jax
import jax.numpy as jnp
from jax import lax
from jax.experimental import pallas as pl
from jax.experimental.pallas import tpu as pltpu

F32 = jnp.float32
BF16 = jnp.bfloat16
MXU_DTYPE = jnp.bfloat16

CHUNK = 64
RMS_EPS = 1e-6
ROPE_THETA = 10000.0
D_MODEL = 1024
RET_HEADS = 4
RET_QK = 256
RET_V = 512
RET_GAMMA_BASE = -5.0
MLA_HEADS = 8
MLA_Q_RANK = 384
MLA_KV_RANK = 256
MLA_NOPE = 128
MLA_ROPE = 64
MLA_V = 128
MLA_QK = MLA_NOPE + MLA_ROPE
MLA_PAD = 256
MLA_IN = MLA_Q_RANK + MLA_KV_RANK + MLA_ROPE
MLA_IN_PAD = MLA_IN + 64
MASK_VALUE = -1e30
FFN_DIM = 2816
ADAM_LR = 0.001
ADAM_B1 = 0.9
ADAM_B2 = 0.999
ADAM_EPS = 1e-08
ADAM_WD = 0.01
ADAM_STEP = 10

LANES = 128
MLA_FWD_BLOCK = 512
VMEM_LIMIT = 56 * 2 ** 20
N_SHARD = 4
N_DEV = 8

MESH = pl.DeviceIdType.MESH


def _params(sem=None, **kw):
    return pltpu.CompilerParams(dimension_semantics=sem, vmem_limit_bytes=VMEM_LIMIT, **kw)


def _pick(dim, target):
    if dim <= target:
        return dim
    best = None
    for d in range(LANES, target + 1, LANES):
        if dim % d == 0:
            best = d
    assert best is not None, (dim, target)
    return best


def _mm(a, b, dims, out_dtype, name, residual=None, bm=512, bn=1024, bk=2048, out_slots=None, after=None,
        cols_outer=False):
    a_parts = list(a) if isinstance(a, (list, tuple)) else [a]
    b_parts = list(b) if isinstance(b, (list, tuple)) else [b]
    parts_on_n = dims == "tn" or len(b_parts) > 1
    if parts_on_n:
        assert len(a_parts) == 1 and dims in ("tn", "nn")
        (K, M) = a_parts[0].shape if dims == "tn" else a_parts[0].shape[::-1]
        N = sum(p.shape[1] for p in b_parts)
        part_widths = [p.shape[1] for p in b_parts]
    else:
        assert len(b_parts) == 1
        M = a_parts[0].shape[0]
        K = sum(p.shape[1] for p in a_parts)
        N = b_parts[0].shape[1 if dims == "nn" else 0]
        part_widths = [p.shape[1] for p in a_parts]
    bm, bn, bk = _pick(M, bm), _pick(N, bn), _pick(K, min(bk, 1024) if dims == "tn" else bk)
    nk = K // bk
    unit = bn if parts_on_n else bk
    assert all(wd % unit == 0 for wd in part_widths), (name, part_widths, unit)
    bounds = np.cumsum([0] + [wd // unit for wd in part_widths])
    ranges = [(int(lo), int(hi)) for lo, hi in zip(bounds[:-1], bounds[1:])]

    def part_index(idx, lo, hi):
        return jnp.clip(idx - lo, 0, hi - lo - 1)

    if parts_on_n:
        if dims == "tn":
            a_specs = [pl.BlockSpec((bk, bm), lambda i, j, k: (k, i))]
            dn = (((0,), (0,)), ((), ()))
        else:
            a_specs = [pl.BlockSpec((bm, bk), lambda i, j, k: (i, k))]
            dn = (((1,), (0,)), ((), ()))
        b_specs = [pl.BlockSpec((bk, bn), functools.partial(lambda i, j, k, lo, hi: (k, part_index(j, lo, hi)), lo=lo, hi=hi))
                   for lo, hi in ranges]
    else:
        a_specs = [pl.BlockSpec((bm, bk), functools.partial(lambda i, j, k, lo, hi: (i, part_index(k, lo, hi)), lo=lo, hi=hi))
                   for lo, hi in ranges]
        if dims == "nt":
            b_specs = [pl.BlockSpec((bn, bk), lambda i, j, k: (j, k))]
        else:
            b_specs = [pl.BlockSpec((bk, bn), lambda i, j, k: (k, j))]
        dn = (((1,), (1 if dims == "nt" else 0,)), ((), ()))
    r_spec = pl.BlockSpec((bm, bn), lambda i, j, k: (i, j))
    if out_slots is None:
        o_spec, o_shape = r_spec, (M, N)
    else:
        ns = N // out_slots
        assert ns % bn == 0, (name, ns, bn)
        nbs = ns // bn
        o_spec = pl.BlockSpec((None, bm, bn), lambda i, j, k: (j // nbs, i, j % nbs))
        o_shape = (out_slots, M, ns)
    has_res = residual is not None
    na, nb = len(a_parts), len(b_parts)

    def body(*refs):
        a_refs, b_refs = refs[:na], refs[na:na + nb]
        r_ref = refs[na + nb] if has_res else None
        n_in = na + nb + has_res + (after is not None)
        o_ref = refs[n_in]
        acc_ref = refs[n_in + 1] if nk > 1 else None
        k = pl.program_id(2)

        def finish(acc):
            if has_res:
                acc = acc + r_ref[...].astype(F32)
            o_ref[...] = acc.astype(out_dtype)

        def compute(a_ref, b_ref):
            p = lax.dot_general(a_ref[...].astype(MXU_DTYPE), b_ref[...].astype(MXU_DTYPE), dn,
                                preferred_element_type=F32)
            if nk == 1:
                finish(p)
                return

            @pl.when(k == 0)
            def _():
                acc_ref[...] = p

            @pl.when(jnp.logical_and(k > 0, k < nk - 1))
            def _():
                acc_ref[...] += p

            @pl.when(k == nk - 1)
            def _():
                finish(acc_ref[...] + p)

        if len(ranges) == 1:
            compute(a_refs[0], b_refs[0])
        else:
            idx = pl.program_id(0 if cols_outer else 1) if parts_on_n else k
            for p, (lo, hi) in enumerate(ranges):
                @pl.when(jnp.logical_and(idx >= lo, idx < hi))
                def _(p=p):
                    compute(a_refs[0 if parts_on_n else p], b_refs[p if parts_on_n else 0])

    after_specs = [] if after is None else [pl.BlockSpec(after.shape, lambda i, j, k: (0, 0))]
    in_specs = a_specs + b_specs + ([r_spec] if has_res else []) + after_specs
    grid = (M // bm, N // bn, nk)
    if cols_outer:
        swap = lambda sp: pl.BlockSpec(sp.block_shape, functools.partial(lambda j, i, k, f: f(i, j, k), f=sp.index_map))
        in_specs, o_spec, grid = [swap(sp) for sp in in_specs], swap(o_spec), (grid[1], grid[0], nk)
    return pl.pallas_call(
        body, name=name, grid=grid,
        in_specs=in_specs, out_specs=o_spec,
        out_shape=jax.ShapeDtypeStruct(o_shape, out_dtype),
        scratch_shapes=[pltpu.VMEM((bm, bn), F32)] if nk > 1 else [],
        compiler_params=_params(("parallel", "parallel", "arbitrary")),
    )(*a_parts, *b_parts, *((residual,) if has_res else ()), *(() if after is None else (after,)))


def _mm_out_norm(a, w, residual, gain, name, bm=512):
    (M, K), N = a.shape, w.shape[1]
    bm = _pick(M, bm)

    def body(a_ref, w_ref, r_ref, g_ref, o_ref, h_ref, ht_ref):
        acc = lax.dot_general(a_ref[...].astype(MXU_DTYPE), w_ref[...].astype(MXU_DTYPE), _NN,
                              preferred_element_type=F32) + r_ref[...]
        o_ref[...] = acc
        hv = _fn_rms([[acc]], [], [[g_ref[...]]])[0][0]
        h_ref[...] = hv.astype(h_ref.dtype)
        ht_ref[...] = hv.T.astype(ht_ref.dtype)

    row = pl.BlockSpec((bm, N), lambda i: (i, 0))
    whole = lambda arr: pl.BlockSpec(arr.shape, lambda i: (0, 0))
    return pl.pallas_call(
        body, name=name, grid=(M // bm,),
        in_specs=[pl.BlockSpec((bm, K), lambda i: (i, 0)), whole(w), row, whole(gain)],
        out_specs=[row, row, pl.BlockSpec((N, bm), lambda i: (0, i))],
        out_shape=[jax.ShapeDtypeStruct((M, N), F32), jax.ShapeDtypeStruct((M, N), BF16),
                   jax.ShapeDtypeStruct((N, M), BF16)],
        compiler_params=_params(("parallel",)),
    )(a, w, residual, gain)


def _mm_out_loss(a, w, residual, target, name, bm=512):
    (M, K), N = a.shape, w.shape[1]
    bm = _pick(M, bm)

    def body(a_ref, w_ref, r_ref, t_ref, dy_ref, dyc_ref, l_ref):
        y = lax.dot_general(a_ref[...].astype(MXU_DTYPE), w_ref[...].astype(MXU_DTYPE), _NN,
                            preferred_element_type=F32) + r_ref[...]
        err = y - t_ref[...]
        dy_ref[...] = err / N
        dyc_ref[...] = (err / N).astype(dyc_ref.dtype)
        part = jnp.full((8, LANES), 0.5 * jnp.sum(jnp.mean(err * err, axis=-1)), F32)

        @pl.when(pl.program_id(0) == 0)
        def _():
            l_ref[...] = part

        @pl.when(pl.program_id(0) > 0)
        def _():
            l_ref[...] += part

    row = pl.BlockSpec((bm, N), lambda i: (i, 0))
    dy, dyc, l = pl.pallas_call(
        body, name=name, grid=(M // bm,),
        in_specs=[pl.BlockSpec((bm, K), lambda i: (i, 0)), pl.BlockSpec(w.shape, lambda i: (0, 0)), row, row],
        out_specs=[row, row, pl.BlockSpec((8, LANES), lambda i: (0, 0))],
        out_shape=[jax.ShapeDtypeStruct((M, N), F32), jax.ShapeDtypeStruct((M, N), BF16),
                   jax.ShapeDtypeStruct((8, LANES), F32)],
        compiler_params=_params(("arbitrary",)),
    )(a, w, residual, target)
    return dy, dyc, l[0, 0]


def _mm_dx_norm(a_parts, w, x, gain, add, name, bm=256, after=None):
    M = a_parts[0].shape[0]
    N, K = w.shape
    widths = [p.shape[1] for p in a_parts]
    assert sum(widths) == K, (name, widths, K)
    offs = [int(o) for o in np.cumsum([0] + widths[:-1])]
    bm = _pick(M, bm)
    na = len(a_parts)
    n_in = na + 4 + (after is not None)

    def body(*refs):
        w_ref, x_ref, g_ref, add_ref = refs[na:na + 4]
        dx_ref, dxc_ref, dg_ref = refs[n_in:n_in + 3]
        dh = None
        for a_ref, off, wd in zip(refs[:na], offs, widths):
            p = lax.dot_general(a_ref[...].astype(MXU_DTYPE), w_ref[:, off:off + wd].astype(MXU_DTYPE), _NT,
                                preferred_element_type=F32)
            dh = p if dh is None else dh + p
        _, vjp = jax.vjp(lambda xv, gv: _fn_rms([[xv]], [], [[gv]])[0][0], x_ref[...], g_ref[...])
        dxv, dgv = vjp(dh)
        dxv = dxv + add_ref[...]
        dx_ref[...] = dxv
        dxc_ref[...] = dxv.astype(dxc_ref.dtype)

        @pl.when(pl.program_id(0) == 0)
        def _():
            dg_ref[...] = dgv

        @pl.when(pl.program_id(0) > 0)
        def _():
            dg_ref[...] += dgv

    row = pl.BlockSpec((bm, N), lambda i: (i, 0))
    whole = lambda a: pl.BlockSpec(a.shape, lambda i: (0, 0))
    in_specs = [pl.BlockSpec((bm, wd), lambda i: (i, 0)) for wd in widths] + [whole(w), row, whole(gain), row]
    in_specs += [] if after is None else [whole(after)]
    return pl.pallas_call(
        body, name=name, grid=(M // bm,),
        in_specs=in_specs, out_specs=[row, row, whole(gain)],
        out_shape=[jax.ShapeDtypeStruct((M, N), F32), jax.ShapeDtypeStruct((M, N), BF16),
                   jax.ShapeDtypeStruct(gain.shape, F32)],
        compiler_params=_params(("arbitrary",)),
    )(*a_parts, w, x, gain, add, *(() if after is None else (after,)))


def _tiles(ref, width, tile):
    return [ref[:, t * tile:(t + 1) * tile].astype(F32) for t in range(width // tile)]


def _row_specs(rows, pos, consts, bm, S):
    npos_blocks = S // bm
    specs = [pl.BlockSpec((bm, w), functools.partial(lambda i, c: (i, c), c=cb)) for (_, w, cb, _) in rows]
    specs += [pl.BlockSpec((bm, p.shape[1]), lambda i: (i % npos_blocks, 0)) for p in pos]
    specs += [pl.BlockSpec(c.shape, lambda i: (0, 0)) for (c, _) in consts]
    return specs


def _rowwise_fwd(fn, name, rows, pos, consts, outs, bm, S, transposed=()):
    T = rows[0][0].shape[0]
    nr, npos, nc, no = len(rows), len(pos), len(consts), len(outs)

    def body(*refs):
        row_v = [_tiles(r, w, t) for r, (_, w, _, t) in zip(refs[:nr], rows)]
        pos_v = [r[...] for r in refs[nr:nr + npos]]
        const_v = [_tiles(r, c.shape[1], t) for r, (c, t) in zip(refs[nr + npos:nr + npos + nc], consts)]
        res = fn(row_v, pos_v, const_v)
        out_refs = refs[nr + npos + nc:]
        for o_ref, tiles, (w, t, dt) in zip(out_refs, res, outs):
            for k, v in enumerate(tiles):
                o_ref[:, k * t:(k + 1) * t] = v.astype(dt)
        for t_ref, a in zip(out_refs[no:], transposed):
            t = outs[a][1]
            for k, v in enumerate(res[a]):
                t_ref[k * t:(k + 1) * t, :] = v.T.astype(t_ref.dtype)

    return pl.pallas_call(
        body, name=name, grid=(T // bm,),
        in_specs=_row_specs(rows, pos, consts, bm, S),
        out_specs=[pl.BlockSpec((bm, w), lambda i: (i, 0)) for (w, _, _) in outs]
        + [pl.BlockSpec((outs[a][0], bm), lambda i: (0, i)) for a in transposed],
        out_shape=[jax.ShapeDtypeStruct((T, w), dt) for (w, _, dt) in outs]
        + [jax.ShapeDtypeStruct((outs[a][0], T), BF16) for a in transposed],
        compiler_params=_params(("parallel",)),
    )(*[r[0] for r in rows], *pos, *[c[0] for c in consts])


def _rowwise_bwd(fn, name, rows, pos, consts, cts, bm, S, adds=None, grad_dtypes=None, mxu_copies=(), linear=False):
    adds = adds or {}
    T = rows[0][0].shape[0]
    nr, npos, nc, nct = len(rows), len(pos), len(consts), len(cts)
    add_idx = sorted(adds)
    grad_dtypes = grad_dtypes or [F32] * nr

    def body(*refs):
        it = iter(refs)
        row_refs = [None if linear else next(it) for _ in range(nr)]
        pos_refs = [next(it) for _ in range(npos)]
        const_refs = [next(it) for _ in range(nc)]
        ct_refs = [next(it) for _ in range(nct)]
        add_refs = {k: next(it) for k in add_idx}
        drow_refs = [next(it) for _ in range(nr)]
        copy_refs = {a: next(it) for a in mxu_copies}
        dconst_refs = [next(it) for _ in range(nc)]
        if linear:
            row_v = [[jnp.zeros((bm, t), F32)] * (w // t) for (_, w, _, t) in rows]
        else:
            row_v = [_tiles(r, w, t) for r, (_, w, _, t) in zip(row_refs, rows)]
        pos_v = [r[...] for r in pos_refs]
        const_v = [_tiles(r, c.shape[1], t) for r, (c, t) in zip(const_refs, consts)]
        ct_v = [_tiles(r, c.shape[1], t) for r, (c, t) in zip(ct_refs, cts)]
        _, vjp = jax.vjp(lambda rv, cv: fn(rv, pos_v, cv), row_v, const_v)
        drows, dconsts = vjp(ct_v)
        for a, (d_ref, tiles, (_, w, _, t)) in enumerate(zip(drow_refs, drows, rows)):
            for k, v in enumerate(tiles):
                if a in add_refs:
                    v = v + add_refs[a][:, k * t:(k + 1) * t].astype(F32)
                d_ref[:, k * t:(k + 1) * t] = v.astype(d_ref.dtype)
                if a in copy_refs:
                    copy_refs[a][:, k * t:(k + 1) * t] = v.astype(BF16)
        first = pl.program_id(0) == 0
        for d_ref, tiles, (_, t) in zip(dconst_refs, dconsts, consts):
            for k, v in enumerate(tiles):
                @pl.when(first)
                def _(d_ref=d_ref, k=k, t=t, v=v):
                    d_ref[:, k * t:(k + 1) * t] = v

                @pl.when(jnp.logical_not(first))
                def _(d_ref=d_ref, k=k, t=t, v=v):
                    d_ref[:, k * t:(k + 1) * t] += v

    in_specs = _row_specs([] if linear else rows, pos, consts, bm, S)
    in_specs += [pl.BlockSpec((bm, c.shape[1]), lambda i: (i, 0)) for (c, _) in cts]
    in_specs += [pl.BlockSpec((bm, adds[k].shape[1]), lambda i: (i, 0)) for k in add_idx]
    out_specs = [pl.BlockSpec((bm, w), lambda i: (i, 0)) for (_, w, _, _) in rows]
    out_specs += [pl.BlockSpec((bm, rows[a][1]), lambda i: (i, 0)) for a in mxu_copies]
    out_specs += [pl.BlockSpec(c.shape, lambda i: (0, 0)) for (c, _) in consts]
    out_shape = [jax.ShapeDtypeStruct((T, w), dt) for (_, w, _, _), dt in zip(rows, grad_dtypes)]
    out_shape += [jax.ShapeDtypeStruct((T, rows[a][1]), BF16) for a in mxu_copies]
    out_shape += [jax.ShapeDtypeStruct(c.shape, F32) for (c, _) in consts]
    res = pl.pallas_call(
        body, name=name, grid=(T // bm,),
        in_specs=in_specs, out_specs=out_specs, out_shape=out_shape,
        compiler_params=_params(("arbitrary",)),
    )(*([] if linear else [r[0] for r in rows]), *pos, *[c[0] for c in consts], *[c[0] for c in cts],
      *[adds[k] for k in add_idx])
    n_rows = nr + len(mxu_copies)
    return res[:n_rows], res[n_rows:]


def _ssq(tiles):
    s = jnp.sum(tiles[0] * tiles[0], axis=-1, keepdims=True)
    for t in tiles[1:]:
        s = s + jnp.sum(t * t, axis=-1, keepdims=True)
    return s


def _sigmoid(x):
    return 0.5 * jnp.tanh(0.5 * x) + 0.5


def _fn_rms(rows, pos, consts):
    (x,), (g,) = rows[0], consts[0]
    r = lax.rsqrt(jnp.mean(x * x, axis=-1, keepdims=True) + RMS_EPS)
    return [[x * r * g]]


def _fn_ret_rope(rows, pos, consts):
    (qkv,) = rows
    nq = RET_HEADS * RET_QK // LANES
    q, k, v = qkv[:nq], qkv[nq:2 * nq], qkv[2 * nq:]
    cos, sin = pos

    def rot(t, scale):
        out = []
        for h in range(RET_HEADS):
            x1, x2 = t[2 * h], t[2 * h + 1]
            o1, o2 = x1 * cos - x2 * sin, x2 * cos + x1 * sin
            out += [o1, o2] if scale is None else [o1 * scale, o2 * scale]
        return out

    return [rot(q, None), rot(k, RET_QK ** -0.5), list(v)]


def _fn_ret_gate(rows, pos, consts):
    o, g = rows
    (gn,) = consts
    out = []
    for h in range(RET_HEADS):
        r = lax.rsqrt(jnp.mean(o[h] * o[h], axis=-1, keepdims=True) + RMS_EPS)
        out.append((o[h] * r * gn[h]) * (g[h] * _sigmoid(g[h])))
    return [out]


def _fn_mla_lat(rows, pos, consts):
    (p,) = rows
    gq, gkv = consts
    nq, nkv = MLA_Q_RANK // LANES, MLA_KV_RANK // LANES
    cq, ckv, kr = p[:nq], p[nq:nq + nkv], p[nq + nkv]
    rq = lax.rsqrt(_ssq(cq) / MLA_Q_RANK + RMS_EPS)
    rkv = lax.rsqrt(_ssq(ckv) / MLA_KV_RANK + RMS_EPS)
    return [[t * rq * g for t, g in zip(cq, gq)], [t * rkv * g for t, g in zip(ckv, gkv)], [kr]]


def _swap32_impl(x):
    lane = lax.broadcasted_iota(jnp.int32, x.shape, 1)
    up, down = pltpu.roll(x, LANES - 32, 1), pltpu.roll(x, 32, 1)
    return jnp.where(lane < 32, up, jnp.where(lane < 64, down, 0.0))


@jax.custom_vjp
def _swap32(x):
    return _swap32_impl(x)


_swap32.defvjp(lambda x: (_swap32_impl(x), None), lambda _, g: (_swap32_impl(g),))


def _fn_mla_heads(rows, pos, consts):
    qf, kvf, (kr,) = rows
    cos, sin = pos
    gq, gk = consts
    q_out, k_out, v_out = [], [], []
    for h in range(MLA_HEADS):
        q0, q1 = qf[2 * h], qf[2 * h + 1]
        r = lax.rsqrt(_ssq([q0, q1]) / MLA_QK + RMS_EPS)
        a0, a1 = q0 * r * gq[0], q1 * r * gq[1]
        a1 = a1 * cos + _swap32(a1) * sin
        q_out += [a0 * (MLA_QK ** -0.5), a1 * (MLA_QK ** -0.5)]
        k0 = kvf[2 * h]
        r = lax.rsqrt(_ssq([k0, kr]) / MLA_QK + RMS_EPS)
        b0, b1 = k0 * r * gk[0], kr * r * gk[1]
        k_out += [b0, b1 * cos + _swap32(b1) * sin]
        v_out.append(kvf[2 * h + 1])
    return [q_out, k_out, v_out]


def _shift_down(x, n):
    row = lax.broadcasted_iota(jnp.int32, x.shape, 0)
    return jnp.where(row >= n, pltpu.roll(x, n, 0), 0.0)


def _shift_up(x, n):
    rows = x.shape[0]
    row = lax.broadcasted_iota(jnp.int32, x.shape, 0)
    return jnp.where(row < rows - n, pltpu.roll(x, rows - n, 0), 0.0)


def _conv_blocks(S):
    cb = 256
    return cb, FFN_DIM // cb


def _conv_fwd(ag, w8, B, S, name):
    cb, ncb = _conv_blocks(S)

    def body(a_ref, g_ref, w_ref, u_ref, ut_ref):
        g = g_ref[...].astype(F32)
        w = w_ref[...]
        gc = w[0:1] * _shift_down(g, 2) + w[1:2] * _shift_down(g, 1) + w[2:3] * g + w[3:4]
        u = a_ref[...].astype(F32) * (gc * _sigmoid(gc))
        u_ref[...] = u.astype(u_ref.dtype)
        ut_ref[...] = u.T.astype(ut_ref.dtype)

    return pl.pallas_call(
        body, name=name, grid=(ncb, B),
        in_specs=[pl.BlockSpec((S, cb), lambda j, b: (b, j)),
                  pl.BlockSpec((S, cb), lambda j, b: (b, ncb + j)),
                  pl.BlockSpec((8, cb), lambda j, b: (0, j))],
        out_specs=[pl.BlockSpec((S, cb), lambda j, b: (b, j)), pl.BlockSpec((cb, S), lambda j, b: (j, b))],
        out_shape=[jax.ShapeDtypeStruct((B * S, FFN_DIM), BF16), jax.ShapeDtypeStruct((FFN_DIM, B * S), BF16)],
        compiler_params=_params(("parallel", "parallel")),
    )(ag, ag, w8)


def _conv_bwd(ag, w8, du, B, S, name):
    cb, ncb = _conv_blocks(S)

    def body(a_ref, g_ref, w_ref, du_ref, da_ref, dg_ref, dw_ref):
        g = g_ref[...].astype(F32)
        w = w_ref[...]
        g1, g2 = _shift_down(g, 1), _shift_down(g, 2)
        gc = w[0:1] * g2 + w[1:2] * g1 + w[2:3] * g + w[3:4]
        sg = _sigmoid(gc)
        du_v = du_ref[...]
        da_ref[...] = (du_v * (gc * sg)).astype(da_ref.dtype)
        dgc = du_v * a_ref[...].astype(F32) * (sg * (1.0 + gc * (1.0 - sg)))
        dg = w[2:3] * dgc + w[1:2] * _shift_up(dgc, 1) + w[0:1] * _shift_up(dgc, 2)
        dg_ref[...] = dg.astype(dg_ref.dtype)
        part = jnp.concatenate([
            jnp.sum(dgc * g2, axis=0, keepdims=True), jnp.sum(dgc * g1, axis=0, keepdims=True),
            jnp.sum(dgc * g, axis=0, keepdims=True), jnp.sum(dgc, axis=0, keepdims=True),
            jnp.zeros((4, cb), F32)], axis=0)

        @pl.when(pl.program_id(1) == 0)
        def _():
            dw_ref[...] = part

        @pl.when(pl.program_id(1) > 0)
        def _():
            dw_ref[...] += part

    blk = lambda j, b: (b, j)
    return pl.pallas_call(
        body, name=name, grid=(ncb, B),
        in_specs=[pl.BlockSpec((S, cb), blk),
                  pl.BlockSpec((S, cb), lambda j, b: (b, ncb + j)),
                  pl.BlockSpec((8, cb), lambda j, b: (0, j)),
                  pl.BlockSpec((S, cb), blk)],
        out_specs=[pl.BlockSpec((S, cb), blk), pl.BlockSpec((S, cb), blk),
                   pl.BlockSpec((8, cb), lambda j, b: (0, j))],
        out_shape=[jax.ShapeDtypeStruct((B * S, FFN_DIM), BF16), jax.ShapeDtypeStruct((B * S, FFN_DIM), BF16),
                   jax.ShapeDtypeStruct((8, FFN_DIM), F32)],
        compiler_params=_params(("parallel", "arbitrary")),
    )(ag, ag, w8, du)


_NT = (((1,), (1,)), ((), ()))
_NN = (((1,), (0,)), ((), ()))
_TN = (((0,), (0,)), ((), ()))


def _dot(a, b, dn):
    return lax.dot_general(a.astype(MXU_DTYPE), b.astype(MXU_DTYPE), dn, preferred_element_type=F32)


def _run_bits(n):
    bits, b = [], 1
    while b < n:
        bits.append(b)
        b *= 2
    return bits[::-1]


def _key_runs(n, nq, update):
    for bit in _run_bits(nq + 1):
        @pl.when((n & bit) != 0)
        def _(bit=bit):
            update(n & ~(2 * bit - 1), bit, (n & (bit - 1)) == 0)


def _earlier_runs(n, nq, update):
    for bit in _run_bits(nq):
        @pl.when((n & bit) != 0)
        def _(bit=bit):
            update(n & ~(2 * bit - 1), bit, False)


def _chunk_visible(shape, nblk, blk):
    key = lax.broadcasted_iota(jnp.int32, shape, 0) - (nblk - 1) * blk
    query = lax.broadcasted_iota(jnp.int32, shape, 1)
    return jnp.logical_or(key < 0, (key // CHUNK) <= (query // CHUNK))


def _mla_attn_fwd(q, k, v, B, S):
    blk = min(MLA_FWD_BLOCK, S)
    H, nq = MLA_HEADS, S // blk

    def body(q_ref, k_ref, v_ref, o_ref, lse_ref, m_ref, l_ref, acc_ref):
        def qblock(i, _):
            q_rows = pl.ds(pl.multiple_of(i * blk, blk), blk)
            qi = q_ref[q_rows, :]
            m_ref[...] = jnp.full(m_ref.shape, MASK_VALUE, F32)
            l_ref[...] = jnp.zeros(l_ref.shape, F32)
            acc_ref[...] = jnp.zeros(acc_ref.shape, F32)

            def keys(first, nblk, last):
                rows = pl.ds(pl.multiple_of(first * blk, blk), nblk * blk)
                s = _dot(k_ref[rows, :], qi, _NT)
                s = jnp.where(jnp.logical_or(_chunk_visible(s.shape, nblk, blk), jnp.logical_not(last)), s, MASK_VALUE)
                m = m_ref[...]
                m2 = jnp.maximum(m, jnp.max(s, axis=0, keepdims=True))
                alpha = jnp.exp(m - m2)
                p = jnp.exp(s - m2)
                l_ref[...] = alpha * l_ref[...] + jnp.sum(p, axis=0, keepdims=True)
                acc_ref[...] = alpha * acc_ref[...] + _dot(v_ref[rows, :], p, _TN)
                m_ref[...] = m2

            _key_runs(i + 1, nq, keys)
            l = l_ref[...]
            o_ref[q_rows, :] = (acc_ref[...] / l).T
            lse_ref[0, :, q_rows] = m_ref[...] + jnp.log(l)
            return 0

        lax.fori_loop(0, nq, qblock, 0)

    return pl.pallas_call(
        body, name="mla_attn_fwd", grid=(B, H),
        in_specs=[pl.BlockSpec((S, MLA_PAD), lambda b, h: (b, h)),
                  pl.BlockSpec((S, MLA_PAD), lambda b, h: (b, h)),
                  pl.BlockSpec((S, MLA_V), lambda b, h: (b, h))],
        out_specs=[pl.BlockSpec((S, MLA_V), lambda b, h: (b, h)),
                   pl.BlockSpec((1, 1, S), lambda b, h: (b * H + h, 0, 0))],
        out_shape=[jax.ShapeDtypeStruct((B * S, H * MLA_V), F32), jax.ShapeDtypeStruct((B * H, 1, S), F32)],
        scratch_shapes=[pltpu.VMEM((1, blk), F32), pltpu.VMEM((1, blk), F32), pltpu.VMEM((MLA_V, blk), F32)],
        compiler_params=_params(("parallel", "parallel")),
    )(q, k, v)


def _mla_attn_bwd(q, k, v, o, do, lse, B, S):
    blk = min(MLA_FWD_BLOCK, S)
    H, nq = MLA_HEADS, S // blk

    def body(q_ref, k_ref, v_ref, o_ref, do_ref, lse_ref, dq_ref, dk_ref, dv_ref, kt_ref, dqt_ref):
        dk_ref[...] = jnp.zeros(dk_ref.shape, F32)
        dv_ref[...] = jnp.zeros(dv_ref.shape, F32)
        for g in range(nq):
            kt_ref[g] = k_ref[g * blk:(g + 1) * blk, :].T

        def qblock(i, _):
            q_rows = pl.ds(pl.multiple_of(i * blk, blk), blk)
            qi = q_ref[q_rows, :]
            doi = do_ref[q_rows, :]
            delta = jnp.sum((doi * o_ref[q_rows, :]).T, axis=0, keepdims=True)
            lse_i = lse_ref[0, :, q_rows]
            doi = doi.astype(MXU_DTYPE)
            dqt_ref[...] = jnp.zeros(dqt_ref.shape, F32)

            def keys(first, nblk, last):
                rows = pl.ds(pl.multiple_of(first * blk, blk), nblk * blk)
                k_run, v_run = k_ref[rows, :], v_ref[rows, :]
                p = jnp.exp(_dot(k_run, qi, _NT) - lse_i)
                p = jnp.where(jnp.logical_or(_chunk_visible(p.shape, nblk, blk), jnp.logical_not(last)), p, 0.0)
                ds = (p * (_dot(v_run, doi, _NT) - delta)).astype(MXU_DTYPE)
                dk_ref[rows, :] += _dot(ds, qi, _NN)
                dv_ref[rows, :] += _dot(p, doi, _NN)
                for r in range(nblk):
                    dqt_ref[...] += _dot(kt_ref[first + r], ds[r * blk:(r + 1) * blk, :], _NN)

            _key_runs(i + 1, nq, keys)
            dq_ref[q_rows, :] = dqt_ref[...].T
            return 0

        lax.fori_loop(0, nq, qblock, 0)

    qk_spec = pl.BlockSpec((S, MLA_PAD), lambda b, h: (b, h))
    v_spec = pl.BlockSpec((S, MLA_V), lambda b, h: (b, h))
    return pl.pallas_call(
        body, name="mla_attn_bwd", grid=(B, H),
        in_specs=[qk_spec, qk_spec, v_spec, v_spec, v_spec,
                  pl.BlockSpec((1, 1, S), lambda b, h: (b * H + h, 0, 0))],
        out_specs=[qk_spec, qk_spec, v_spec],
        out_shape=[jax.ShapeDtypeStruct((B * S, H * MLA_PAD), F32), jax.ShapeDtypeStruct((B * S, H * MLA_PAD), F32),
                   jax.ShapeDtypeStruct((B * S, H * MLA_V), F32)],
        scratch_shapes=[pltpu.VMEM((nq, MLA_PAD, blk), q.dtype), pltpu.VMEM((MLA_PAD, blk), F32)],
        compiler_params=_params(("parallel", "parallel")),
    )(q, k, v, o, do, lse)


def _ret_log_gamma():
    lg = np.log1p(-np.exp2(RET_GAMMA_BASE - np.arange(RET_HEADS, dtype=np.float32))).astype(np.float32)
    return jnp.asarray(np.broadcast_to(lg[:, None, None], (RET_HEADS, 8, LANES)).copy())


RET_BLOCK = 512


def _ret_local_scale(lg, shape, blk, rising):
    local = lax.broadcasted_iota(jnp.int32, shape, 0) % blk
    return jnp.exp(lg * (local if rising else blk - 1 - local).astype(F32))


def _ret_pair_factor(lg, blk, steps):
    return jnp.exp(lg * (blk * (steps - 1) + 1).astype(F32))


def _ret_own_decay(lg, blk, transposed):
    a = lax.broadcasted_iota(jnp.int32, (blk, blk), 0)
    b = lax.broadcasted_iota(jnp.int32, (blk, blk), 1)
    query, key = (b, a) if transposed else (a, b)
    dec = jnp.exp(lg * jnp.abs(query - key).astype(F32))
    return jnp.where((key // CHUNK) <= (query // CHUNK), dec, 0.0)


def _ret_attn_fwd(q, k, v, B, S):
    blk = min(RET_BLOCK, S)
    H, nq = RET_HEADS, S // blk

    def body(lg_ref, q_ref, k_ref, v_ref, o_ref, ks_ref, dec_ref, acc_ref):
        lg = lg_ref[0, 0:1, 0:1]
        ks_ref[...] = (k_ref[...].astype(F32) * _ret_local_scale(lg, k_ref.shape, blk, False)).astype(ks_ref.dtype)
        dec_ref[...] = _ret_own_decay(lg, blk, False)

        def qblock(i, _):
            q_rows = pl.ds(pl.multiple_of(i * blk, blk), blk)
            qi = q_ref[q_rows, :]
            qs = (qi.astype(F32) * _ret_local_scale(lg, qi.shape, blk, True)).astype(qi.dtype)
            a = _dot(qi, k_ref[q_rows, :], _NT) * dec_ref[...]
            acc_ref[...] = _dot(a, v_ref[q_rows, :], _NN)

            def keys(first, nblk, _):
                rows = pl.ds(pl.multiple_of(first * blk, blk), nblk * blk)
                steps = i - first - lax.broadcasted_iota(jnp.int32, (1, nblk * blk), 1) // blk
                a = _dot(qs, ks_ref[rows, :], _NT) * _ret_pair_factor(lg, blk, steps)
                acc_ref[...] += _dot(a, v_ref[rows, :], _NN)

            _earlier_runs(i, nq, keys)
            o_ref[q_rows, :] = acc_ref[...]
            return 0

        lax.fori_loop(0, nq, qblock, 0)

    qk_spec = pl.BlockSpec((S, RET_QK), lambda b, h: (b, h))
    v_spec = pl.BlockSpec((S, RET_V), lambda b, h: (b, h))
    return pl.pallas_call(
        body, name="ret_attn_fwd", grid=(B, H),
        in_specs=[pl.BlockSpec((1, 8, LANES), lambda b, h: (h, 0, 0)), qk_spec, qk_spec, v_spec],
        out_specs=v_spec,
        out_shape=jax.ShapeDtypeStruct((B * S, H * RET_V), F32),
        scratch_shapes=[pltpu.VMEM((S, RET_QK), k.dtype), pltpu.VMEM((blk, blk), F32), pltpu.VMEM((blk, RET_V), F32)],
        compiler_params=_params(("parallel", "parallel")),
    )(_ret_log_gamma(), q, k, v)


def _ret_attn_bwd(q, k, v, do, B, S):
    blk = min(RET_BLOCK, S)
    H, nq = RET_HEADS, S // blk

    def body(lg_ref, q_ref, k_ref, v_ref, do_ref, dq_ref, dk_ref, dv_ref, ks_ref, kst_ref, dks_ref, dqt_ref, dec_ref):
        lg = lg_ref[0, 0:1, 0:1]
        dk_ref[...] = jnp.zeros(dk_ref.shape, F32)
        dv_ref[...] = jnp.zeros(dv_ref.shape, F32)
        dks_ref[...] = jnp.zeros(dks_ref.shape, F32)
        ks_ref[...] = (k_ref[...].astype(F32) * _ret_local_scale(lg, k_ref.shape, blk, False)).astype(ks_ref.dtype)
        for g in range(nq):
            kst_ref[g] = ks_ref[g * blk:(g + 1) * blk, :].T
        dec_ref[...] = _ret_own_decay(lg, blk, True)

        def qblock(i, _):
            q_rows = pl.ds(pl.multiple_of(i * blk, blk), blk)
            qi = q_ref[q_rows, :]
            q_scale = _ret_local_scale(lg, qi.shape, blk, True)
            qs = (qi.astype(F32) * q_scale).astype(qi.dtype)
            doi = do_ref[q_rows, :].astype(MXU_DTYPE)
            ki = k_ref[q_rows, :]
            dec = dec_ref[...]
            a = _dot(ki, qi, _NT) * dec
            da = (_dot(v_ref[q_rows, :], doi, _NT) * dec).astype(MXU_DTYPE)
            dv_ref[q_rows, :] += _dot(a, doi, _NN)
            dk_ref[q_rows, :] += _dot(da, qi, _NN)
            dq_own = _dot(da, ki, _TN)
            dqt_ref[...] = jnp.zeros(dqt_ref.shape, F32)

            def keys(first, nblk, _):
                for r in range(nblk):
                    g = first + r
                    rows = pl.ds(pl.multiple_of(g * blk, blk), blk)
                    c = _ret_pair_factor(lg, blk, i - g)
                    a = _dot(ks_ref[rows, :], qs, _NT) * c
                    da = (_dot(v_ref[rows, :], doi, _NT) * c).astype(MXU_DTYPE)
                    dv_ref[rows, :] += _dot(a, doi, _NN)
                    dks_ref[rows, :] += _dot(da, qs, _NN)
                    dqt_ref[...] += _dot(kst_ref[g], da, _NN)

            _earlier_runs(i, nq, keys)
            dq_ref[q_rows, :] = dqt_ref[...].T * q_scale + dq_own
            return 0

        lax.fori_loop(0, nq, qblock, 0)
        dk_ref[...] += dks_ref[...] * _ret_local_scale(lg, dks_ref.shape, blk, False)

    qk_spec = pl.BlockSpec((S, RET_QK), lambda b, h: (b, h))
    v_spec = pl.BlockSpec((S, RET_V), lambda b, h: (b, h))
    return pl.pallas_call(
        body, name="ret_attn_bwd", grid=(B, H),
        in_specs=[pl.BlockSpec((1, 8, LANES), lambda b, h: (h, 0, 0)), qk_spec, qk_spec, v_spec, v_spec],
        out_specs=[qk_spec, qk_spec, v_spec],
        out_shape=[jax.ShapeDtypeStruct((B * S, H * RET_QK), F32), jax.ShapeDtypeStruct((B * S, H * RET_QK), F32),
                   jax.ShapeDtypeStruct((B * S, H * RET_V), F32)],
        scratch_shapes=[pltpu.VMEM((S, RET_QK), k.dtype), pltpu.VMEM((nq, RET_QK, blk), k.dtype),
                        pltpu.VMEM((S, RET_QK), F32), pltpu.VMEM((RET_QK, blk), F32), pltpu.VMEM((blk, blk), F32)],
        compiler_params=_params(("parallel", "parallel")),
    )(_ret_log_gamma(), q, k, v, do)


def _adamw(w, g, m, v, name):
    R, C = w.shape
    br = R if R * C * 4 <= 2 ** 21 else _pick_rows(R, max(8, (2 ** 21) // (C * 4)))

    def body(w_ref, g_ref, m_ref, v_ref, d_ref, mo_ref, vo_ref):
        g_v = g_ref[...]
        m_v = ADAM_B1 * m_ref[...] + (1.0 - ADAM_B1) * g_v
        v_v = ADAM_B2 * v_ref[...] + (1.0 - ADAM_B2) * (g_v * g_v)
        m_hat = m_v / (1.0 - ADAM_B1 ** ADAM_STEP)
        v_hat = v_v / (1.0 - ADAM_B2 ** ADAM_STEP)
        d_ref[...] = -ADAM_LR * (m_hat / (jnp.sqrt(v_hat) + ADAM_EPS) + ADAM_WD * w_ref[...])
        mo_ref[...] = m_v
        vo_ref[...] = v_v

    blk = pl.BlockSpec((br, C), lambda i: (i, 0))
    return pl.pallas_call(
        body, name=name, grid=(R // br,),
        in_specs=[blk] * 4, out_specs=[blk] * 3,
        out_shape=[jax.ShapeDtypeStruct((R, C), F32)] * 3,
        compiler_params=_params(("parallel",)),
    )(w, g, m, v)


def _pick_rows(R, target):
    best = None
    for d in range(8, min(R, target) + 1, 8):
        if R % d == 0:
            best = d
    assert best is not None, (R, target)
    return best


def _position():
    return lax.axis_index("x"), lax.axis_index("y"), lax.axis_index("c")


HBM_SPEC = pl.BlockSpec(memory_space=pltpu.HBM)


def _other_chips(x, y):
    return [(1 - x, y), (x, 1 - y), (1 - x, 1 - y)]


def _all_gather_weights(bigs, small):
    nb = len(bigs)

    def body(*refs):
        big_refs, small_ref = refs[:nb], refs[nb]
        obig, osmall = refs[nb + 1:2 * nb + 1], refs[2 * nb + 1]
        ici_send, ici_recv, d2d_send, d2d_recv, sm_send, sm_recv = refs[2 * nb + 2:]
        x, y, c = _position()
        me = 2 * x + y
        chips = _other_chips(x, y)

        def rows(n, half):
            rh = bigs[n].shape[0] // 2
            return pl.ds(half * rh, rh)

        def over_ici(n, j, slot, from_shard):
            px, py = chips[j]
            dst = obig[n].at[slot, rows(n, c)]
            return pltpu.make_async_remote_copy(
                src_ref=big_refs[n].at[rows(n, c)] if from_shard else dst, dst_ref=dst,
                send_sem=ici_send.at[3 * n + j], recv_sem=ici_recv.at[3 * n + j],
                device_id=(px, py, c), device_id_type=MESH)

        def over_d2d(n, j, half):
            px, py = chips[j]
            part = obig[n].at[2 * px + py, rows(n, half)]
            return pltpu.make_async_remote_copy(
                src_ref=part, dst_ref=part, send_sem=d2d_send.at[3 * n + j], recv_sem=d2d_recv.at[3 * n + j],
                device_id=(x, y, 1 - c), device_id_type=MESH)

        def small_copy(j, slot):
            px, py = chips[j]
            return pltpu.make_async_remote_copy(
                src_ref=small_ref, dst_ref=osmall.at[slot], send_sem=sm_send.at[j], recv_sem=sm_recv.at[j],
                device_id=(px, py, c), device_id_type=MESH)

        sends = [over_ici(n, j, me, True) for n in range(nb) for j in range(3)]
        sends += [small_copy(j, me) for j in range(3)]
        for cp in sends:
            cp.start()
        passed = []
        for n in range(nb):
            for j, (px, py) in enumerate(chips):
                over_ici(n, j, 2 * px + py, False).wait_recv()
                fwd = over_d2d(n, j, c)
                fwd.start()
                passed.append(fwd)
        for n in range(nb):
            for j in range(3):
                over_d2d(n, j, 1 - c).wait_recv()
        for j, (px, py) in enumerate(chips):
            small_copy(j, 2 * px + py).wait_recv()
        for cp in sends + passed:
            cp.wait_send()

    dma = pltpu.SemaphoreType.DMA
    return pl.pallas_call(
        body, name="weights_all_gather",
        in_specs=[HBM_SPEC] * (nb + 1), out_specs=[HBM_SPEC] * (nb + 1),
        out_shape=[jax.ShapeDtypeStruct((N_SHARD,) + b.shape, b.dtype) for b in bigs]
        + [jax.ShapeDtypeStruct((N_SHARD,) + small.shape, small.dtype)],
        scratch_shapes=[dma((3 * nb,)), dma((3 * nb,)), dma((3 * nb,)), dma((3 * nb,)), dma((3,)), dma((3,))],
    )(*bigs, small)


SEM_SPEC = pl.BlockSpec(memory_space=pltpu.SEMAPHORE)
DATAFLOW_EFFECT = pltpu.SideEffectType.DATAFLOW_SIDE_EFFECTING
N_PEERS = N_DEV - 1


def _grad_copies(p_refs, land_refs, send_sems, recv_sems):
    x, y, c = _position()
    copies = []
    for a, (p_ref, land_ref) in enumerate(zip(p_refs, land_refs)):
        rh = p_ref.shape[1] // 2
        for k in range(1, N_DEV):
            px = 1 - x if k & 4 else x
            py = 1 - y if k & 2 else y
            pc = 1 - c if k & 1 else c
            copies.append(pltpu.make_async_remote_copy(
                src_ref=p_ref.at[2 * px + py, pl.ds(pc * rh, rh)], dst_ref=land_ref.at[k - 1],
                send_sem=send_sems.at[N_PEERS * a + k - 1], recv_sem=recv_sems.at[N_PEERS * a + k - 1],
                device_id=(px, py, pc), device_id_type=MESH))
    return copies


def _weight_copies(w_refs, land_refs, send_sems, recv_sems):
    x, y, c = _position()
    copies = []
    for a, (w_ref, land_ref) in enumerate(zip(w_refs, land_refs)):
        for j, (px, py) in enumerate(_other_chips(x, y)):
            copies.append(pltpu.make_async_remote_copy(
                src_ref=w_ref, dst_ref=land_ref.at[2 * x + y], send_sem=send_sems.at[3 * a + j],
                recv_sem=recv_sems.at[3 * a + j], device_id=(px, py, c), device_id_type=MESH))
    return copies


def _exchange_start(make_copies, srcs, lands, n_sems, name, after=None):
    n, m = len(srcs), len(lands)
    n_in = n + m + (after is not None)

    def body(*refs):
        send_sems, recv_sems, token = refs[n_in], refs[n_in + 1], refs[-1]
        for cp in make_copies(refs[:n], refs[n:n + m], send_sems, recv_sems):
            cp.start()
        token[...] = jnp.zeros(token.shape, token.dtype)

    hbm = lambda a: pltpu.with_memory_space_constraint(a, pltpu.HBM)
    dma = pltpu.SemaphoreType.DMA
    res = pl.pallas_call(
        body, name=name,
        in_specs=[HBM_SPEC] * (n + m) + ([] if after is None else [pl.BlockSpec(memory_space=pl.ANY)]),
        out_specs=[SEM_SPEC, SEM_SPEC] + [HBM_SPEC] * (n + m) + [pl.BlockSpec(memory_space=pltpu.VMEM)],
        out_shape=[dma((n_sems,)), dma((n_sems,))] + [pltpu.HBM(a.shape, a.dtype) for a in list(srcs) + list(lands)]
        + [jax.ShapeDtypeStruct((8, LANES), F32)],
        input_output_aliases={i: 2 + i for i in range(n + m)},
        compiler_params=pltpu.CompilerParams(has_side_effects=DATAFLOW_EFFECT),
    )(*[hbm(a) for a in srcs], *[hbm(a) for a in lands], *(() if after is None else (after,)))
    return res[0], res[1], list(res[2:2 + n]), list(res[2 + n:2 + n + m]), res[-1]


def _exchange_wait(make_copies, send_sems, recv_sems, srcs, lands, after, name):
    n, m = len(srcs), len(lands)

    def body(*refs):
        for cp in make_copies(refs[:n], refs[n:n + m], refs[n + m], refs[n + m + 1]):
            cp.wait_send()
            cp.wait_recv()

    res = pl.pallas_call(
        body, name=name,
        in_specs=[HBM_SPEC] * (n + m) + [SEM_SPEC, SEM_SPEC, pl.BlockSpec(memory_space=pl.ANY)],
        out_specs=[HBM_SPEC] * (n + m),
        out_shape=[pltpu.HBM(a.shape, a.dtype) for a in list(srcs) + list(lands)],
        input_output_aliases={i: i for i in range(n + m)},
        compiler_params=pltpu.CompilerParams(has_side_effects=DATAFLOW_EFFECT),
    )(*srcs, *lands, send_sems, recv_sems, after)
    return list(res[:n]), list(res[n:])


def _sum_partials(p, land, name):
    _, rh, cols = land.shape
    br = _pick_rows(rh, 256)
    nrb = rh // br
    x, y, c = _position()
    where = jnp.stack([2 * x + y, c]).astype(jnp.int32)

    def body(where_ref, p_ref, land_ref, o_ref):
        acc = p_ref[...].astype(F32)
        for k in range(N_PEERS):
            acc = acc + land_ref[k].astype(F32)
        o_ref[...] = acc

    return pl.pallas_call(
        body, name=name,
        grid_spec=pltpu.PrefetchScalarGridSpec(
            num_scalar_prefetch=1, grid=(nrb,),
            in_specs=[pl.BlockSpec((None, br, cols), lambda r, where_ref: (where_ref[0], where_ref[1] * nrb + r, 0)),
                      pl.BlockSpec((N_PEERS, br, cols), lambda r, where_ref: (0, r, 0))],
            out_specs=pl.BlockSpec((None, br, cols), lambda r, where_ref: (where_ref[1], r, 0))),
        out_shape=jax.ShapeDtypeStruct((2, rh, cols), F32),
        compiler_params=_params(("parallel",)),
    )(where, p, land)


def _sibling_share(fulls, name):
    n = len(fulls)

    def body(*refs):
        o_refs = refs[n:2 * n]
        send_sems, recv_sems = refs[2 * n:]
        x, y, c = _position()

        def copy(a, half):
            return pltpu.make_async_remote_copy(
                src_ref=o_refs[a].at[half], dst_ref=o_refs[a].at[half], send_sem=send_sems.at[a],
                recv_sem=recv_sems.at[a], device_id=(x, y, 1 - c), device_id_type=MESH)

        sends = [copy(a, c) for a in range(n)]
        for cp in sends:
            cp.start()
        for a in range(n):
            copy(a, 1 - c).wait_recv()
        for cp in sends:
            cp.wait_send()

    dma = pltpu.SemaphoreType.DMA
    return pl.pallas_call(
        body, name=name,
        in_specs=[HBM_SPEC] * n, out_specs=[HBM_SPEC] * n,
        out_shape=[jax.ShapeDtypeStruct(f.shape, f.dtype) for f in fulls],
        input_output_aliases={a: a for a in range(n)},
        scratch_shapes=[dma((n,)), dma((n,))],
    )(*fulls)


def _all_reduce_small(v):
    R, cols = v.shape

    def body(v_ref, o_ref, buf_ref, send_sems, recv_sems):
        x, y, c = _position()
        me = 4 * x + 2 * y + c
        buf_ref[me] = v_ref[...]
        sends = []
        for k in range(1, N_DEV):
            px = 1 - x if k & 4 else x
            py = 1 - y if k & 2 else y
            pc = 1 - c if k & 1 else c
            sends.append(pltpu.make_async_remote_copy(
                src_ref=v_ref, dst_ref=buf_ref.at[me], send_sem=send_sems.at[k - 1], recv_sem=recv_sems.at[k - 1],
                device_id=(px, py, pc), device_id_type=MESH))
        for cp in sends:
            cp.start()
        for k in range(1, N_DEV):
            px = 1 - x if k & 4 else x
            py = 1 - y if k & 2 else y
            pc = 1 - c if k & 1 else c
            pltpu.make_async_remote_copy(
                src_ref=v_ref, dst_ref=buf_ref.at[4 * px + 2 * py + pc], send_sem=send_sems.at[k - 1],
                recv_sem=recv_sems.at[k - 1], device_id=(px, py, pc), device_id_type=MESH).wait_recv()
        for cp in sends:
            cp.wait_send()
        acc = buf_ref[0]
        for d in range(1, N_DEV):
            acc = acc + buf_ref[d]
        o_ref[...] = acc

    return pl.pallas_call(
        body, name="small_grads_all_reduce",
        in_specs=[pl.BlockSpec(memory_space=pltpu.VMEM)], out_specs=pl.BlockSpec(memory_space=pltpu.VMEM),
        out_shape=jax.ShapeDtypeStruct((R, cols), F32),
        scratch_shapes=[pltpu.VMEM((N_DEV, R, cols), F32), pltpu.SemaphoreType.DMA((N_DEV - 1,)),
                        pltpu.SemaphoreType.DMA((N_DEV - 1,))],
    )(v)


def _rope_tables(S, half, width):
    inv_freq = ROPE_THETA ** (-jnp.arange(half, dtype=F32) / half)
    ang = jnp.arange(S).astype(F32)[:, None] * inv_freq[None, :]
    return jnp.cos(ang), jnp.sin(ang)


def _slot_rows(a):
    return a.reshape(N_SHARD, -1, a.shape[-1])


def _local_step(x, target, w, B, S, late, exchange, reduce_small):
    T = B * S
    D = D_MODEL
    bm = min(512, S)
    full = lambda a, wd, tile=None: (a, wd, 0, tile or wd)
    g = {}

    cos_r, sin_r = _rope_tables(S, RET_QK // 2, LANES)
    cos_m, sin_m = _rope_tables(S, MLA_ROPE // 2, LANES)
    zeros64 = jnp.zeros((S, 64), F32)
    cos_m = jnp.concatenate([cos_m, cos_m, zeros64], axis=1)
    sin_m = jnp.concatenate([-sin_m, sin_m, zeros64], axis=1)

    def ffn_fwd(xin, h, ht, i, next_gain):
        w.update(late(f"ffn{i}", xin))
        norm = w["ffn_norm"][i:i + 1]
        ag = _mm(h, w[f"ffn_w_in{i}"], "nn", BF16, f"ffn{i}_in", bm=1024, bn=2816, cols_outer=True)
        u, ut = _conv_fwd(ag, w["ffn_conv8"][i], B, S, f"ffn{i}_conv")
        if next_gain is None:
            out = _mm_out_loss(u, w[f"ffn_w_out{i}"], xin, target, f"ffn{i}_out")
        else:
            out = _mm_out_norm(u, w[f"ffn_w_out{i}"], xin, next_gain, f"ffn{i}_out")
        return out, (xin, norm, ht, ag, ut)

    def ffn_bwd(dxout, dxout_c, saved, i):
        xin, norm, ht, ag, ut = saved
        du = _mm(dxout_c, w[f"ffn_w_out{i}"], "nt", F32, f"ffn{i}_out_dx", bm=1024, bn=2816, cols_outer=True)
        g_w_out = _mm(ut, dxout_c, "nn", BF16, f"ffn{i}_out_dw", bm=1408, bn=512, bk=T)
        da, dg, dw8 = _conv_bwd(ag, w["ffn_conv8"][i], du, B, S, f"ffn{i}_conv_bwd")
        g_w_in = _mm(ht, [da, dg], "nn", BF16, f"ffn{i}_in_dw", bm=1024, bn=1408, bk=T // 2, out_slots=N_SHARD)
        token = exchange(f"ffn{i}", [g_w_in, _slot_rows(g_w_out)])
        dxin, dxin_c, g_norm = _mm_dx_norm([da, dg], w[f"ffn_w_in{i}"], xin, norm, dxout, f"ffn{i}_in_dx", after=token)
        return dxin, dxin_c, (g_norm, dw8)

    h0, h0t = _rowwise_fwd(_fn_rms, "ret_norm", [full(x, D)], [], [(w["ret_norm"], D)], [(D, D, BF16)], bm, S,
                           transposed=(0,))
    proj = _mm(h0, w["ret_w_in"], "nn", BF16, "ret_in", bm=1024, bn=2048, after=w["started"], cols_outer=True)
    HQ, HV = RET_HEADS * RET_QK, RET_HEADS * RET_V
    rope_rows = [(proj, 2 * HQ + HV, 0, LANES)]
    q_r, k_r, v_r = _rowwise_fwd(_fn_ret_rope, "ret_rope", rope_rows, [cos_r, sin_r], [],
                                 [(HQ, LANES, BF16), (HQ, LANES, BF16), (HV, LANES, BF16)], bm, S)
    ret_o = _ret_attn_fwd(q_r, k_r, v_r, B, S)
    gate_rows = [full(ret_o, HV, RET_V), (proj, HV, 2, RET_V)]
    y0, y0t = _rowwise_fwd(_fn_ret_gate, "ret_gate", gate_rows, [], [(w["ret_gn"], RET_V)], [(HV, RET_V, BF16)], bm, S,
                           transposed=(0,))
    w.update(late("ret_out", y0))
    x1, h1, h1t = _mm_out_norm(y0, w["ret_w_out"], x, w["ffn_norm"][0:1], "ret_out")
    (x2, h2, _), ffn0_saved = ffn_fwd(x1, h1, h1t, 0, w["mla_norm"])

    w.update(late("mla", x2))
    proj2 = _mm(h2, w["mla_w_in"], "nn", F32, "mla_in", bm=2048)
    lat_consts = [(w["mla_q_norm"], LANES), (w["mla_kv_norm"], LANES)]
    cqn, ckvn, kr = _rowwise_fwd(_fn_mla_lat, "mla_latent_norm", [full(proj2, MLA_IN_PAD, LANES)], [], lat_consts,
                                 [(MLA_Q_RANK, LANES, BF16), (MLA_KV_RANK, LANES, BF16), (LANES, LANES, F32)], bm, S)
    qf = _mm(cqn, w["mla_w_qb"], "nn", BF16, "mla_qb", bm=2048, bn=2048)
    kvf = _mm(ckvn, w["mla_w_kvb"], "nn", BF16, "mla_kvb", bm=2048, bn=2048)
    HP, HVm = MLA_HEADS * MLA_PAD, MLA_HEADS * MLA_V
    head_rows = [full(qf, HP, LANES), full(kvf, HP, LANES), full(kr, LANES)]
    head_consts = [(w["mla_q_head_norm"], LANES), (w["mla_k_head_norm"], LANES)]
    q_a, k_a, v_a = _rowwise_fwd(_fn_mla_heads, "mla_heads", head_rows, [cos_m, sin_m], head_consts,
                                 [(HP, LANES, BF16), (HP, LANES, BF16), (HVm, LANES, BF16)], bm, S)
    att_o, lse = _mla_attn_fwd(q_a, k_a, v_a, B, S)
    x3, h3, h3t = _mm_out_norm(att_o, w["mla_w_out"], x2, w["ffn_norm"][1:2], "mla_out")
    (dy, dy_c, loss), ffn1_saved = ffn_fwd(x3, h3, h3t, 1, None)

    dx3, dx3_c, (g_n1, dw8_1) = ffn_bwd(dy, dy_c, ffn1_saved, 1)

    d_att_o = _mm(dx3_c, w["mla_w_out"], "nt", F32, "mla_out_dx", bm=2048)
    g_mla_out = _mm(att_o, dx3_c, "tn", BF16, "mla_out_dw")
    dq_a, dk_a, dv_a = _mla_attn_bwd(q_a, k_a, v_a, att_o, d_att_o, lse, B, S)
    (dqf, dkvf, dkr), (g["mla_q_head_norm"], g["mla_k_head_norm"]) = _rowwise_bwd(
        _fn_mla_heads, "mla_heads_bwd", head_rows, [cos_m, sin_m], head_consts,
        [(dq_a, LANES), (dk_a, LANES), (dv_a, LANES)], bm, S, grad_dtypes=[BF16, BF16, F32])
    dcqn = _mm(dqf, w["mla_w_qb"], "nt", F32, "mla_qb_dx", bm=2048)
    g_qb = _mm(cqn, dqf, "tn", BF16, "mla_qb_dw")
    g_qb = _to_slots(_unpad_heads(g_qb, 1), 1).reshape(N_SHARD, MLA_Q_RANK, -1)
    dckvn = _mm(dkvf, w["mla_w_kvb"], "nt", F32, "mla_kvb_dx", bm=2048)
    g_kvb = _mm(ckvn, dkvf, "tn", BF16, "mla_kvb_dw", bn=512, out_slots=N_SHARD)
    (dproj2,), (g["mla_q_norm"], g["mla_kv_norm"]) = _rowwise_bwd(
        _fn_mla_lat, "mla_latent_norm_bwd", [full(proj2, MLA_IN_PAD, LANES)], [], lat_consts,
        [(dcqn, LANES), (dckvn, LANES), (dkr, LANES)], bm, S, grad_dtypes=[BF16])
    g_mla_in = _mm(h2, dproj2, "tn", BF16, "mla_in_dw")
    token = exchange("mla", [_slot_rows(g_mla_in[:, :MLA_IN]), g_qb, g_kvb, _slot_rows(g_mla_out)])
    dx2, dx2_c, g["mla_norm"] = _mm_dx_norm([dproj2], w["mla_w_in"], x2, w["mla_norm"], dx3, "mla_in_dx", bm=512,
                                            after=token)

    dx1, dx1_c, (g_n0, dw8_0) = ffn_bwd(dx2, dx2_c, ffn0_saved, 0)

    dy0 = _mm(dx1_c, w["ret_w_out"], "nt", F32, "ret_out_dx", bm=1024, bn=2048, cols_outer=True)
    g_ret_out = _mm(y0t, dx1_c, "nn", BF16, "ret_out_dw", bm=1024, bn=512, bk=T)
    token = exchange("reto", [_slot_rows(g_ret_out)])
    gn_behind = w["ret_gn"] + token[0:1, 0:1]
    (d_ret_o, dgate), (g["ret_gn"],) = _rowwise_bwd(_fn_ret_gate, "ret_gate_bwd", gate_rows, [], [(gn_behind, RET_V)],
                                                    [(dy0, RET_V)], bm, S, grad_dtypes=[F32, BF16])
    dq_r, dk_r, dv_r = _ret_attn_bwd(q_r, k_r, v_r, d_ret_o, B, S)
    (dqkv,), _ = _rowwise_bwd(_fn_ret_rope, "ret_rope_bwd", rope_rows, [cos_r, sin_r], [],
                              [(dq_r, LANES), (dk_r, LANES), (dv_r, LANES)], bm, S, grad_dtypes=[BF16], linear=True)
    dx, _, g["ret_norm"] = _mm_dx_norm([dqkv, dgate], w["ret_w_in"], x, w["ret_norm"], dx1, "ret_in_dx")
    g["ffn_norm"] = jnp.concatenate([g_n0, g_n1], axis=0)
    g["ffn_conv_w"] = jnp.stack([dw8_0[0:3], dw8_1[0:3]])
    g["ffn_conv_b"] = jnp.stack([dw8_0[3], dw8_1[3]])
    reduced_small = reduce_small(g, loss)
    g_ret_in = _mm(h0t, [dqkv, dgate], "nn", BF16, "ret_in_dw", bm=1024, bn=512, bk=T, out_slots=N_SHARD,
                   after=reduced_small)
    exchange("ret", [g_ret_in])
    return loss, dx, reduced_small


_SMALL_SHARDED = [("ret_gn", 2), ("mla_norm", 1), ("mla_q_norm", 1), ("mla_kv_norm", 1), ("ffn_conv_w", 2)]
_SMALL_REPLICATED = ["ret_norm", "mla_q_head_norm", "mla_k_head_norm", "ffn_norm", "ffn_conv_b"]
_SMALL_ALL = ["ret_norm", "ret_gn", "mla_norm", "mla_q_norm", "mla_kv_norm", "mla_q_head_norm", "mla_k_head_norm",
              "ffn_norm", "ffn_conv_w", "ffn_conv_b"]


def _to_slots(full, axis):
    shape = full.shape
    split = shape[:axis] + (N_SHARD, shape[axis] // N_SHARD) + shape[axis + 1:]
    return jnp.moveaxis(full.reshape(split), axis, 0).reshape(N_SHARD, -1)


def _from_slots(slots, shard_shape, axis):
    parts = jnp.moveaxis(slots.reshape((N_SHARD,) + tuple(shard_shape)), 0, axis)
    full = shard_shape[:axis] + (N_SHARD * shard_shape[axis],) + shard_shape[axis + 1:]
    return parts.reshape(full)


def _pad_rows(flat, cols, row_unit):
    n, L = flat.shape
    unit = cols * row_unit
    Lp = -(-L // unit) * unit
    if Lp != L:
        flat = jnp.concatenate([flat, jnp.zeros((n, Lp - L), flat.dtype)], axis=1)
    return flat.reshape(n, Lp // cols, cols)


def _pad_heads(a, axis):
    shape = a.shape
    heads = shape[axis] // MLA_QK
    a = a.reshape(shape[:axis] + (heads, MLA_QK) + shape[axis + 1:])
    pad = [(0, 0)] * a.ndim
    pad[axis + 1] = (0, MLA_PAD - MLA_QK)
    return jnp.pad(a, pad).reshape(shape[:axis] + (heads * MLA_PAD,) + shape[axis + 1:])


def _unpad_heads(a, axis):
    shape = a.shape
    a = a.reshape(shape[:axis] + (MLA_HEADS, MLA_PAD) + shape[axis + 1:])
    a = lax.slice_in_dim(a, 0, MLA_QK, axis=axis + 1)
    return a.reshape(shape[:axis] + (MLA_HEADS * MLA_QK,) + shape[axis + 1:])


def kernel(x, ret_norm, ret_w_in, ret_gn, ret_w_out, mla_norm, mla_w_in, mla_q_norm, mla_w_qb, mla_kv_norm, mla_w_kvb, mla_q_head_norm, mla_k_head_norm, mla_w_out, ffn_norm, ffn_w_in, ffn_conv_w, ffn_conv_b, ffn_w_out, loss_target, m_ret_norm, m_ret_w_in, m_ret_gn, m_ret_w_out, m_mla_norm, m_mla_w_in, m_mla_q_norm, m_mla_w_qb, m_mla_kv_norm, m_mla_w_kvb, m_mla_q_head_norm, m_mla_k_head_norm, m_mla_w_out, m_ffn_norm, m_ffn_w_in, m_ffn_conv_w, m_ffn_conv_b, m_ffn_w_out, v_ret_norm, v_ret_w_in, v_ret_gn, v_ret_w_out, v_mla_norm, v_mla_w_in, v_mla_q_norm, v_mla_w_qb, v_mla_kv_norm, v_mla_w_kvb, v_mla_q_head_norm, v_mla_k_head_norm, v_mla_w_out, v_ffn_norm, v_ffn_w_in, v_ffn_conv_w, v_ffn_conv_b, v_ffn_w_out):
    names = ["ret_norm", "ret_w_in", "ret_gn", "ret_w_out", "mla_norm", "mla_w_in", "mla_q_norm", "mla_w_qb",
             "mla_kv_norm", "mla_w_kvb", "mla_q_head_norm", "mla_k_head_norm", "mla_w_out", "ffn_norm", "ffn_w_in",
             "ffn_conv_w", "ffn_conv_b", "ffn_w_out"]
    shard = dict(zip(names, [ret_norm, ret_w_in, ret_gn, ret_w_out, mla_norm, mla_w_in, mla_q_norm, mla_w_qb,
                             mla_kv_norm, mla_w_kvb, mla_q_head_norm, mla_k_head_norm, mla_w_out, ffn_norm, ffn_w_in,
                             ffn_conv_w, ffn_conv_b, ffn_w_out]))
    mom_m = dict(zip(names, [m_ret_norm, m_ret_w_in, m_ret_gn, m_ret_w_out, m_mla_norm, m_mla_w_in, m_mla_q_norm,
                             m_mla_w_qb, m_mla_kv_norm, m_mla_w_kvb, m_mla_q_head_norm, m_mla_k_head_norm, m_mla_w_out,
                             m_ffn_norm, m_ffn_w_in, m_ffn_conv_w, m_ffn_conv_b, m_ffn_w_out]))
    mom_v = dict(zip(names, [v_ret_norm, v_ret_w_in, v_ret_gn, v_ret_w_out, v_mla_norm, v_mla_w_in, v_mla_q_norm,
                             v_mla_w_qb, v_mla_kv_norm, v_mla_w_kvb, v_mla_q_head_norm, v_mla_k_head_norm, v_mla_w_out,
                             v_ffn_norm, v_ffn_w_in, v_ffn_conv_w, v_ffn_conv_b, v_ffn_w_out]))
    B, S, D = x.shape
    T = B * S
    sx, sy = lax.axis_index("x"), lax.axis_index("y")
    me = 2 * sx + sy

    two_d = lambda a: a.reshape(-1, a.shape[-1])
    small_sizes = [int(np.prod(shard[n].shape)) for n, _ in _SMALL_SHARDED]
    small = jnp.concatenate([shard[n].reshape(1, -1) for n, _ in _SMALL_SHARDED], axis=1)
    small = _pad_rows(small, LANES, 8)[0]
    as_mxu = lambda a: two_d(a).astype(BF16)
    zero = jnp.zeros((), jnp.int32)
    with_own = lambda gathered, own: lax.dynamic_update_slice(gathered, own[None], (me.astype(jnp.int32), zero, zero))
    by_cols = lambda a: jnp.moveaxis(a, 0, 1).reshape(a.shape[1], -1)
    by_rows = lambda a: a.reshape(-1, a.shape[-1])
    mla_in_shard = jnp.pad(as_mxu(shard["mla_w_in"]), ((0, 0), (0, MLA_IN_PAD - MLA_IN)))
    mla_qb_shard = _pad_heads(as_mxu(shard["mla_w_qb"]), 1)
    ret_in_shard = as_mxu(shard["ret_w_in"])
    g_ret_in, gsmall = _all_gather_weights([ret_in_shard], small)
    later = [
        ("ret_out", [("ret_w_out", as_mxu(shard["ret_w_out"]), by_rows)]),
        ("ffn0", [("ffn_w_in0", as_mxu(shard["ffn_w_in"][0]), by_cols), ("ffn_w_out0", as_mxu(shard["ffn_w_out"][0]), by_rows)]),
        ("mla", [("mla_w_in", mla_in_shard, by_rows), ("mla_w_qb", mla_qb_shard, by_cols),
                 ("mla_w_kvb", as_mxu(shard["mla_w_kvb"]), by_cols), ("mla_w_out", as_mxu(shard["mla_w_out"]), by_rows)]),
        ("ffn1", [("ffn_w_in1", as_mxu(shard["ffn_w_in"][1]), by_cols), ("ffn_w_out1", as_mxu(shard["ffn_w_out"][1]), by_rows)]),
    ]
    gathering = {}
    token = gsmall
    for group, items in later:
        shards = [s_ for _, s_, _ in items]
        lands = [lax.empty((N_SHARD,) + s_.shape, s_.dtype) for s_ in shards]
        send_sems, recv_sems, shards, lands, token = _exchange_start(
            _weight_copies, shards, lands, 3 * len(shards), f"weights_start_{group}", after=token)
        gathering[group] = (send_sems, recv_sems, shards, lands, items)

    def late(group, after):
        send_sems, recv_sems, shards, lands, items = gathering[group]
        shards, lands = _exchange_wait(_weight_copies, send_sems, recv_sems, shards, lands, after,
                                       f"weights_wait_{group}")
        return {key: full(with_own(l_, s_)) for (key, _, full), s_, l_ in zip(items, shards, lands)}

    gsmall = with_own(gsmall, small).reshape(N_SHARD, -1)
    wfull = {}
    off = 0
    for (n, ax), sz in zip(_SMALL_SHARDED, small_sizes):
        wfull[n] = _from_slots(gsmall[:, off:off + sz], shard[n].shape, ax)
        off += sz
    for n in _SMALL_REPLICATED:
        wfull[n] = shard[n]

    conv8 = jnp.concatenate([wfull["ffn_conv_w"], wfull["ffn_conv_b"][:, None, :],
                             jnp.zeros((2, 4, FFN_DIM), F32)], axis=1)
    w = {
        "started": token, "ret_norm": wfull["ret_norm"], "ret_w_in": by_cols(with_own(g_ret_in, ret_in_shard)),
        "ret_gn": wfull["ret_gn"].reshape(1, RET_HEADS * RET_V), "mla_norm": wfull["mla_norm"],
        "mla_q_norm": wfull["mla_q_norm"], "mla_kv_norm": wfull["mla_kv_norm"],
        "mla_q_head_norm": jnp.pad(wfull["mla_q_head_norm"], ((0, 0), (0, MLA_PAD - MLA_QK))),
        "mla_k_head_norm": jnp.pad(wfull["mla_k_head_norm"], ((0, 0), (0, MLA_PAD - MLA_QK))),
        "ffn_norm": wfull["ffn_norm"], "ffn_conv8": conv8,
    }

    started = {}

    def exchange(group, arrays):
        lands = [lax.empty((N_PEERS, p.shape[1] // 2, p.shape[2]), p.dtype) for p in arrays]
        send_sems, recv_sems, ps, lands, token = _exchange_start(
            _grad_copies, arrays, lands, N_PEERS * len(arrays), f"grads_start_{group}")
        started[group] = (send_sems, recv_sems, ps, lands)
        return token

    small_shapes = {
        "ret_norm": (1, D_MODEL), "ret_gn": (1, RET_HEADS, RET_V), "mla_norm": (1, D_MODEL),
        "mla_q_norm": (1, MLA_Q_RANK), "mla_kv_norm": (1, MLA_KV_RANK), "mla_q_head_norm": (1, MLA_QK),
        "mla_k_head_norm": (1, MLA_QK), "ffn_norm": (2, D_MODEL), "ffn_conv_w": (2, 3, FFN_DIM),
        "ffn_conv_b": (2, FFN_DIM)}

    def reduce_small(gl, loss_part):
        gl = dict(gl, mla_q_head_norm=gl["mla_q_head_norm"][:, :MLA_QK], mla_k_head_norm=gl["mla_k_head_norm"][:, :MLA_QK])
        packed = jnp.concatenate([gl[n].reshape(1, -1) for n in _SMALL_ALL] + [loss_part.reshape(1, 1)], axis=1)
        return _all_reduce_small(_pad_rows(packed, LANES, 8)[0])

    _, dx, gsm = _local_step(x.reshape(T, D), loss_target.reshape(T, D), w, B, S, late, exchange, reduce_small)

    delta, new_m, new_v, grads = {}, {}, {}, {}

    def reduced(group, after):
        send_sems, recv_sems, ps, lands = started[group]
        ps, lands = _exchange_wait(_grad_copies, send_sems, recv_sems, ps, lands, after, f"grads_wait_{group}")
        halves = [_sum_partials(p_, l_, f"grads_sum_{group}_{i}") for i, (p_, l_) in enumerate(zip(ps, lands))]
        return [two_d(r) for r in _sibling_share(halves, f"grads_share_{group}")]

    def adamw(n, g_):
        shp = shard[n].shape
        grads[n] = g_.reshape(shp)
        flat = lambda a: a.reshape(-1, shp[-1])
        d_, m_, v_ = _adamw(flat(shard[n]), flat(grads[n]), flat(mom_m[n]), flat(mom_v[n]), f"adamw_{n}")
        delta[n], new_m[n], new_v[n] = d_.reshape(shp), m_.reshape(shp), v_.reshape(shp)
        return d_

    ffn1 = reduced("ffn1", started["ret"][2][0])
    mla = reduced("mla", ffn1[0])
    ffn0 = reduced("ffn0", mla[0])
    reto = reduced("reto", ffn0[0])
    early = [adamw(n, g_) for n, g_ in zip(["mla_w_in", "mla_w_qb", "mla_w_kvb", "mla_w_out"], mla)]
    early.append(adamw("ffn_w_in", jnp.stack([ffn0[0], ffn1[0]])))
    early.append(adamw("ffn_w_out", jnp.stack([ffn0[1], ffn1[1]])))
    early.append(adamw("ret_w_out", reto[0]))
    ret = reduced("ret", jnp.stack([d_[0, 0] for d_ in early]))
    adamw("ret_w_in", ret[0])

    gsm = gsm.reshape(-1)
    sharded_axis = dict(_SMALL_SHARDED)
    off = 0
    for n in _SMALL_ALL:
        sz = int(np.prod(small_shapes[n]))
        gn = gsm[off:off + sz].reshape(small_shapes[n])
        off += sz
        if n in sharded_axis:
            ax = sharded_axis[n]
            width = shard[n].shape[ax]
            gn = lax.dynamic_slice_in_dim(gn, me * width, width, axis=ax)
        grads[n] = gn
    loss = gsm[off]

    pack_small = lambda d: _pad_rows(jnp.concatenate([d[n].reshape(1, -1) for n in _SMALL_ALL], axis=1), LANES, 8)[0]
    d_, m_, v_ = _adamw(pack_small(shard), pack_small(grads), pack_small(mom_m), pack_small(mom_v), "adamw_small")
    off = 0
    for n in _SMALL_ALL:
        sz = int(np.prod(shard[n].shape))
        for dst, src in ((delta, d_), (new_m, m_), (new_v, v_)):
            dst[n] = src.reshape(-1)[off:off + sz].reshape(shard[n].shape)
        off += sz

    return (loss, dx.reshape(B, S, D), *[grads[n] for n in names], *[delta[n] for n in names],
            *[new_m[n] for n in names], *[new_v[n] for n in names])
```

```python
import functools

import numpy as np
import jax
import jax.numpy as jnp
from jax import lax
from jax.experimental import pallas as pl
from jax.experimental.pallas import tpu as pltpu

F32 = jnp.float32
BF16 = jnp.bfloat16
MXU_DTYPE = jnp.bfloat16

CHUNK = 64
RMS_EPS = 1e-6
ROPE_THETA = 10000.0
D_MODEL = 1024
RET_HEADS = 4
RET_QK = 256
RET_V = 512
RET_GAMMA_BASE = -5.0
MLA_HEADS = 8
MLA_Q_RANK = 384
MLA_KV_RANK = 256
MLA_NOPE = 128
MLA_ROPE = 64
MLA_V = 128
MLA_QK = MLA_NOPE + MLA_ROPE
MLA_PAD = 256
MLA_IN = MLA_Q_RANK + MLA_KV_RANK + MLA_ROPE
MLA_IN_PAD = MLA_IN + 64
MASK_VALUE = -1e30
FFN_DIM = 2816
ADAM_LR = 0.001
ADAM_B1 = 0.9
ADAM_B2 = 0.999
ADAM_EPS = 1e-08
ADAM_WD = 0.01
ADAM_STEP = 10

LANES = 128
MLA_FWD_BLOCK = 512
VMEM_LIMIT = 56 * 2 ** 20
N_SHARD = 4
N_DEV = 8

MESH = pl.DeviceIdType.MESH


def _params(sem=None, **kw):
    return pltpu.CompilerParams(dimension_semantics=sem, vmem_limit_bytes=VMEM_LIMIT, **kw)


def _pick(dim, target):
    if dim <= target:
        return dim
    best = None
    for d in range(LANES, target + 1, LANES):
        if dim % d == 0:
            best = d
    assert best is not None, (dim, target)
    return best


def _mm(a, b, dims, out_dtype, name, residual=None, bm=512, bn=1024, bk=2048, out_slots=None, after=None,
        cols_outer=False):
    a_parts = list(a) if isinstance(a, (list, tuple)) else [a]
    b_parts = list(b) if isinstance(b, (list, tuple)) else [b]
    parts_on_n = dims == "tn" or len(b_parts) > 1
    if parts_on_n:
        assert len(a_parts) == 1 and dims in ("tn", "nn")
        (K, M) = a_parts[0].shape if dims == "tn" else a_parts[0].shape[::-1]
        N = sum(p.shape[1] for p in b_parts)
        part_widths = [p.shape[1] for p in b_parts]
    else:
        assert len(b_parts) == 1
        M = a_parts[0].shape[0]
        K = sum(p.shape[1] for p in a_parts)
        N = b_parts[0].shape[1 if dims == "nn" else 0]
        part_widths = [p.shape[1] for p in a_parts]
    bm, bn, bk = _pick(M, bm), _pick(N, bn), _pick(K, min(bk, 1024) if dims == "tn" else bk)
    nk = K // bk
    unit = bn if parts_on_n else bk
    assert all(wd % unit == 0 for wd in part_widths), (name, part_widths, unit)
    bounds = np.cumsum([0] + [wd // unit for wd in part_widths])
    ranges = [(int(lo), int(hi)) for lo, hi in zip(bounds[:-1], bounds[1:])]

    def part_index(idx, lo, hi):
        return jnp.clip(idx - lo, 0, hi - lo - 1)

    if parts_on_n:
        if dims == "tn":
            a_specs = [pl.BlockSpec((bk, bm), lambda i, j, k: (k, i))]
            dn = (((0,), (0,)), ((), ()))
        else:
            a_specs = [pl.BlockSpec((bm, bk), lambda i, j, k: (i, k))]
            dn = (((1,), (0,)), ((), ()))
        b_specs = [pl.BlockSpec((bk, bn), functools.partial(lambda i, j, k, lo, hi: (k, part_index(j, lo, hi)), lo=lo, hi=hi))
                   for lo, hi in ranges]
    else:
        a_specs = [pl.BlockSpec((bm, bk), functools.partial(lambda i, j, k, lo, hi: (i, part_index(k, lo, hi)), lo=lo, hi=hi))
                   for lo, hi in ranges]
        if dims == "nt":
            b_specs = [pl.BlockSpec((bn, bk), lambda i, j, k: (j, k))]
        else:
            b_specs = [pl.BlockSpec((bk, bn), lambda i, j, k: (k, j))]
        dn = (((1,), (1 if dims == "nt" else 0,)), ((), ()))
    r_spec = pl.BlockSpec((bm, bn), lambda i, j, k: (i, j))
    if out_slots is None:
        o_spec, o_shape = r_spec, (M, N)
    else:
        ns = N // out_slots
        assert ns % bn == 0, (name, ns, bn)
        nbs = ns // bn
        o_spec = pl.BlockSpec((None, bm, bn), lambda i, j, k: (j // nbs, i, j % nbs))
        o_shape = (out_slots, M, ns)
    has_res = residual is not None
    na, nb = len(a_parts), len(b_parts)

    def body(*refs):
        a_refs, b_refs = refs[:na], refs[na:na + nb]
        r_ref = refs[na + nb] if has_res else None
        n_in = na + nb + has_res + (after is not None)
        o_ref = refs[n_in]
        acc_ref = refs[n_in + 1] if nk > 1 else None
        k = pl.program_id(2)

        def finish(acc):
            if has_res:
                acc = acc + r_ref[...].astype(F32)
            o_ref[...] = acc.astype(out_dtype)

        def compute(a_ref, b_ref):
            p = lax.dot_general(a_ref[...].astype(MXU_DTYPE), b_ref[...].astype(MXU_DTYPE), dn,
                                preferred_element_type=F32)
            if nk == 1:
                finish(p)
                return

            @pl.when(k == 0)
            def _():
                acc_ref[...] = p

            @pl.when(jnp.logical_and(k > 0, k < nk - 1))
            def _():
                acc_ref[...] += p

            @pl.when(k == nk - 1)
            def _():
                finish(acc_ref[...] + p)

        if len(ranges) == 1:
            compute(a_refs[0], b_refs[0])
        else:
            idx = pl.program_id(0 if cols_outer else 1) if parts_on_n else k
            for p, (lo, hi) in enumerate(ranges):
                @pl.when(jnp.logical_and(idx >= lo, idx < hi))
                def _(p=p):
                    compute(a_refs[0 if parts_on_n else p], b_refs[p if parts_on_n else 0])

    after_specs = [] if after is None else [pl.BlockSpec(after.shape, lambda i, j, k: (0, 0))]
    in_specs = a_specs + b_specs + ([r_spec] if has_res else []) + after_specs
    grid = (M // bm, N // bn, nk)
    if cols_outer:
        swap = lambda sp: pl.BlockSpec(sp.block_shape, functools.partial(lambda j, i, k, f: f(i, j, k), f=sp.index_map))
        in_specs, o_spec, grid = [swap(sp) for sp in in_specs], swap(o_spec), (grid[1], grid[0], nk)
    return pl.pallas_call(
        body, name=name, grid=grid,
        in_specs=in_specs, out_specs=o_spec,
        out_shape=jax.ShapeDtypeStruct(o_shape, out_dtype),
        scratch_shapes=[pltpu.VMEM((bm, bn), F32)] if nk > 1 else [],
        compiler_params=_params(("parallel", "parallel", "arbitrary")),
    )(*a_parts, *b_parts, *((residual,) if has_res else ()), *(() if after is None else (after,)))


def _mm_out_norm(a, w, residual, gain, name, bm=512):
    (M, K), N = a.shape, w.shape[1]
    bm = _pick(M, bm)

    def body(a_ref, w_ref, r_ref, g_ref, o_ref, h_ref, ht_ref):
        acc = lax.dot_general(a_ref[...].astype(MXU_DTYPE), w_ref[...].astype(MXU_DTYPE), _NN,
                              preferred_element_type=F32) + r_ref[...]
        o_ref[...] = acc
        hv = _fn_rms([[acc]], [], [[g_ref[...]]])[0][0]
        h_ref[...] = hv.astype(h_ref.dtype)
        ht_ref[...] = hv.T.astype(ht_ref.dtype)

    row = pl.BlockSpec((bm, N), lambda i: (i, 0))
    whole = lambda arr: pl.BlockSpec(arr.shape, lambda i: (0, 0))
    return pl.pallas_call(
        body, name=name, grid=(M // bm,),
        in_specs=[pl.BlockSpec((bm, K), lambda i: (i, 0)), whole(w), row, whole(gain)],
        out_specs=[row, row, pl.BlockSpec((N, bm), lambda i: (0, i))],
        out_shape=[jax.ShapeDtypeStruct((M, N), F32), jax.ShapeDtypeStruct((M, N), BF16),
                   jax.ShapeDtypeStruct((N, M), BF16)],
        compiler_params=_params(("parallel",)),
    )(a, w, residual, gain)


def _mm_out_loss(a, w, residual, target, name, bm=512):
    (M, K), N = a.shape, w.shape[1]
    bm = _pick(M, bm)

    def body(a_ref, w_ref, r_ref, t_ref, dy_ref, dyc_ref, l_ref):
        y = lax.dot_general(a_ref[...].astype(MXU_DTYPE), w_ref[...].astype(MXU_DTYPE), _NN,
                            preferred_element_type=F32) + r_ref[...]
        err = y - t_ref[...]
        dy_ref[...] = err / N
        dyc_ref[...] = (err / N).astype(dyc_ref.dtype)
        part = jnp.full((8, LANES), 0.5 * jnp.sum(jnp.mean(err * err, axis=-1)), F32)

        @pl.when(pl.program_id(0) == 0)
        def _():
            l_ref[...] = part

        @pl.when(pl.program_id(0) > 0)
        def _():
            l_ref[...] += part

    row = pl.BlockSpec((bm, N), lambda i: (i, 0))
    dy, dyc, l = pl.pallas_call(
        body, name=name, grid=(M // bm,),
        in_specs=[pl.BlockSpec((bm, K), lambda i: (i, 0)), pl.BlockSpec(w.shape, lambda i: (0, 0)), row, row],
        out_specs=[row, row, pl.BlockSpec((8, LANES), lambda i: (0, 0))],
        out_shape=[jax.ShapeDtypeStruct((M, N), F32), jax.ShapeDtypeStruct((M, N), BF16),
                   jax.ShapeDtypeStruct((8, LANES), F32)],
        compiler_params=_params(("arbitrary",)),
    )(a, w, residual, target)
    return dy, dyc, l[0, 0]


def _mm_dx_norm(a_parts, w, x, gain, add, name, bm=256, after=None):
    M = a_parts[0].shape[0]
    N, K = w.shape
    widths = [p.shape[1] for p in a_parts]
    assert sum(widths) == K, (name, widths, K)
    offs = [int(o) for o in np.cumsum([0] + widths[:-1])]
    bm = _pick(M, bm)
    na = len(a_parts)
    n_in = na + 4 + (after is not None)

    def body(*refs):
        w_ref, x_ref, g_ref, add_ref = refs[na:na + 4]
        dx_ref, dxc_ref, dg_ref = refs[n_in:n_in + 3]
        dh = None
        for a_ref, off, wd in zip(refs[:na], offs, widths):
            p = lax.dot_general(a_ref[...].astype(MXU_DTYPE), w_ref[:, off:off + wd].astype(MXU_DTYPE), _NT,
                                preferred_element_type=F32)
            dh = p if dh is None else dh + p
        _, vjp = jax.vjp(lambda xv, gv: _fn_rms([[xv]], [], [[gv]])[0][0], x_ref[...], g_ref[...])
        dxv, dgv = vjp(dh)
        dxv = dxv + add_ref[...]
        dx_ref[...] = dxv
        dxc_ref[...] = dxv.astype(dxc_ref.dtype)

        @pl.when(pl.program_id(0) == 0)
        def _():
            dg_ref[...] = dgv

        @pl.when(pl.program_id(0) > 0)
        def _():
            dg_ref[...] += dgv

    row = pl.BlockSpec((bm, N), lambda i: (i, 0))
    whole = lambda a: pl.BlockSpec(a.shape, lambda i: (0, 0))
    in_specs = [pl.BlockSpec((bm, wd), lambda i: (i, 0)) for wd in widths] + [whole(w), row, whole(gain), row]
    in_specs += [] if after is None else [whole(after)]
    return pl.pallas_call(
        body, name=name, grid=(M // bm,),
        in_specs=in_specs, out_specs=[row, row, whole(gain)],
        out_shape=[jax.ShapeDtypeStruct((M, N), F32), jax.ShapeDtypeStruct((M, N), BF16),
                   jax.ShapeDtypeStruct(gain.shape, F32)],
        compiler_params=_params(("arbitrary",)),
    )(*a_parts, w, x, gain, add, *(() if after is None else (after,)))


def _tiles(ref, width, tile):
    return [ref[:, t * tile:(t + 1) * tile].astype(F32) for t in range(width // tile)]


def _row_specs(rows, pos, consts, bm, S):
    npos_blocks = S // bm
    specs = [pl.BlockSpec((bm, w), functools.partial(lambda i, c: (i, c), c=cb)) for (_, w, cb, _) in rows]
    specs += [pl.BlockSpec((bm, p.shape[1]), lambda i: (i % npos_blocks, 0)) for p in pos]
    specs += [pl.BlockSpec(c.shape, lambda i: (0, 0)) for (c, _) in consts]
    return specs


def _rowwise_fwd(fn, name, rows, pos, consts, outs, bm, S, transposed=()):
    T = rows[0][0].shape[0]
    nr, npos, nc, no = len(rows), len(pos), len(consts), len(outs)

    def body(*refs):
        row_v = [_tiles(r, w, t) for r, (_, w, _, t) in zip(refs[:nr], rows)]
        pos_v = [r[...] for r in refs[nr:nr + npos]]
        const_v = [_tiles(r, c.shape[1], t) for r, (c, t) in zip(refs[nr + npos:nr + npos + nc], consts)]
        res = fn(row_v, pos_v, const_v)
        out_refs = refs[nr + npos + nc:]
        for o_ref, tiles, (w, t, dt) in zip(out_refs, res, outs):
            for k, v in enumerate(tiles):
                o_ref[:, k * t:(k + 1) * t] = v.astype(dt)
        for t_ref, a in zip(out_refs[no:], transposed):
            t = outs[a][1]
            for k, v in enumerate(res[a]):
                t_ref[k * t:(k + 1) * t, :] = v.T.astype(t_ref.dtype)

    return pl.pallas_call(
        body, name=name, grid=(T // bm,),
        in_specs=_row_specs(rows, pos, consts, bm, S),
        out_specs=[pl.BlockSpec((bm, w), lambda i: (i, 0)) for (w, _, _) in outs]
        + [pl.BlockSpec((outs[a][0], bm), lambda i: (0, i)) for a in transposed],
        out_shape=[jax.ShapeDtypeStruct((T, w), dt) for (w, _, dt) in outs]
        + [jax.ShapeDtypeStruct((outs[a][0], T), BF16) for a in transposed],
        compiler_params=_params(("parallel",)),
    )(*[r[0] for r in rows], *pos, *[c[0] for c in consts])


def _rowwise_bwd(fn, name, rows, pos, consts, cts, bm, S, adds=None, grad_dtypes=None, mxu_copies=(), linear=False):
    adds = adds or {}
    T = rows[0][0].shape[0]
    nr, npos, nc, nct = len(rows), len(pos), len(consts), len(cts)
    add_idx = sorted(adds)
    grad_dtypes = grad_dtypes or [F32] * nr

    def body(*refs):
        it = iter(refs)
        row_refs = [None if linear else next(it) for _ in range(nr)]
        pos_refs = [next(it) for _ in range(npos)]
        const_refs = [next(it) for _ in range(nc)]
        ct_refs = [next(it) for _ in range(nct)]
        add_refs = {k: next(it) for k in add_idx}
        drow_refs = [next(it) for _ in range(nr)]
        copy_refs = {a: next(it) for a in mxu_copies}
        dconst_refs = [next(it) for _ in range(nc)]
        if linear:
            row_v = [[jnp.zeros((bm, t), F32)] * (w // t) for (_, w, _, t) in rows]
        else:
            row_v = [_tiles(r, w, t) for r, (_, w, _, t) in zip(row_refs, rows)]
        pos_v = [r[...] for r in pos_refs]
        const_v = [_tiles(r, c.shape[1], t) for r, (c, t) in zip(const_refs, consts)]
        ct_v = [_tiles(r, c.shape[1], t) for r, (c, t) in zip(ct_refs, cts)]
        _, vjp = jax.vjp(lambda rv, cv: fn(rv, pos_v, cv), row_v, const_v)
        drows, dconsts = vjp(ct_v)
        for a, (d_ref, tiles, (_, w, _, t)) in enumerate(zip(drow_refs, drows, rows)):
            for k, v in enumerate(tiles):
                if a in add_refs:
                    v = v + add_refs[a][:, k * t:(k + 1) * t].astype(F32)
                d_ref[:, k * t:(k + 1) * t] = v.astype(d_ref.dtype)
                if a in copy_refs:
                    copy_refs[a][:, k * t:(k + 1) * t] = v.astype(BF16)
        first = pl.program_id(0) == 0
        for d_ref, tiles, (_, t) in zip(dconst_refs, dconsts, consts):
            for k, v in enumerate(tiles):
                @pl.when(first)
                def _(d_ref=d_ref, k=k, t=t, v=v):
                    d_ref[:, k * t:(k + 1) * t] = v

                @pl.when(jnp.logical_not(first))
                def _(d_ref=d_ref, k=k, t=t, v=v):
                    d_ref[:, k * t:(k + 1) * t] += v

    in_specs = _row_specs([] if linear else rows, pos, consts, bm, S)
    in_specs += [pl.BlockSpec((bm, c.shape[1]), lambda i: (i, 0)) for (c, _) in cts]
    in_specs += [pl.BlockSpec((bm, adds[k].shape[1]), lambda i: (i, 0)) for k in add_idx]
    out_specs = [pl.BlockSpec((bm, w), lambda i: (i, 0)) for (_, w, _, _) in rows]
    out_specs += [pl.BlockSpec((bm, rows[a][1]), lambda i: (i, 0)) for a in mxu_copies]
    out_specs += [pl.BlockSpec(c.shape, lambda i: (0, 0)) for (c, _) in consts]
    out_shape = [jax.ShapeDtypeStruct((T, w), dt) for (_, w, _, _), dt in zip(rows, grad_dtypes)]
    out_shape += [jax.ShapeDtypeStruct((T, rows[a][1]), BF16) for a in mxu_copies]
    out_shape += [jax.ShapeDtypeStruct(c.shape, F32) for (c, _) in consts]
    res = pl.pallas_call(
        body, name=name, grid=(T // bm,),
        in_specs=in_specs, out_specs=out_specs, out_shape=out_shape,
        compiler_params=_params(("arbitrary",)),
    )(*([] if linear else [r[0] for r in rows]), *pos, *[c[0] for c in consts], *[c[0] for c in cts],
      *[adds[k] for k in add_idx])
    n_rows = nr + len(mxu_copies)
    return res[:n_rows], res[n_rows:]


def _ssq(tiles):
    s = jnp.sum(tiles[0] * tiles[0], axis=-1, keepdims=True)
    for t in tiles[1:]:
        s = s + jnp.sum(t * t, axis=-1, keepdims=True)
    return s


def _sigmoid(x):
    return 0.5 * jnp.tanh(0.5 * x) + 0.5


def _fn_rms(rows, pos, consts):
    (x,), (g,) = rows[0], consts[0]
    r = lax.rsqrt(jnp.mean(x * x, axis=-1, keepdims=True) + RMS_EPS)
    return [[x * r * g]]


def _fn_ret_rope(rows, pos, consts):
    (qkv,) = rows
    nq = RET_HEADS * RET_QK // LANES
    q, k, v = qkv[:nq], qkv[nq:2 * nq], qkv[2 * nq:]
    cos, sin = pos

    def rot(t, scale):
        out = []
        for h in range(RET_HEADS):
            x1, x2 = t[2 * h], t[2 * h + 1]
            o1, o2 = x1 * cos - x2 * sin, x2 * cos + x1 * sin
            out += [o1, o2] if scale is None else [o1 * scale, o2 * scale]
        return out

    return [rot(q, None), rot(k, RET_QK ** -0.5), list(v)]


def _fn_ret_gate(rows, pos, consts):
    o, g = rows
    (gn,) = consts
    out = []
    for h in range(RET_HEADS):
        r = lax.rsqrt(jnp.mean(o[h] * o[h], axis=-1, keepdims=True) + RMS_EPS)
        out.append((o[h] * r * gn[h]) * (g[h] * _sigmoid(g[h])))
    return [out]


def _fn_mla_lat(rows, pos, consts):
    (p,) = rows
    gq, gkv = consts
    nq, nkv = MLA_Q_RANK // LANES, MLA_KV_RANK // LANES
    cq, ckv, kr = p[:nq], p[nq:nq + nkv], p[nq + nkv]
    rq = lax.rsqrt(_ssq(cq) / MLA_Q_RANK + RMS_EPS)
    rkv = lax.rsqrt(_ssq(ckv) / MLA_KV_RANK + RMS_EPS)
    return [[t * rq * g for t, g in zip(cq, gq)], [t * rkv * g for t, g in zip(ckv, gkv)], [kr]]


def _swap32_impl(x):
    lane = lax.broadcasted_iota(jnp.int32, x.shape, 1)
    up, down = pltpu.roll(x, LANES - 32, 1), pltpu.roll(x, 32, 1)
    return jnp.where(lane < 32, up, jnp.where(lane < 64, down, 0.0))


@jax.custom_vjp
def _swap32(x):
    return _swap32_impl(x)


_swap32.defvjp(lambda x: (_swap32_impl(x), None), lambda _, g: (_swap32_impl(g),))


def _fn_mla_heads(rows, pos, consts):
    qf, kvf, (kr,) = rows
    cos, sin = pos
    gq, gk = consts
    q_out, k_out, v_out = [], [], []
    for h in range(MLA_HEADS):
        q0, q1 = qf[2 * h], qf[2 * h + 1]
        r = lax.rsqrt(_ssq([q0, q1]) / MLA_QK + RMS_EPS)
        a0, a1 = q0 * r * gq[0], q1 * r * gq[1]
        a1 = a1 * cos + _swap32(a1) * sin
        q_out += [a0 * (MLA_QK ** -0.5), a1 * (MLA_QK ** -0.5)]
        k0 = kvf[2 * h]
        r = lax.rsqrt(_ssq([k0, kr]) / MLA_QK + RMS_EPS)
        b0, b1 = k0 * r * gk[0], kr * r * gk[1]
        k_out += [b0, b1 * cos + _swap32(b1) * sin]
        v_out.append(kvf[2 * h + 1])
    return [q_out, k_out, v_out]


def _shift_down(x, n):
    row = lax.broadcasted_iota(jnp.int32, x.shape, 0)
    return jnp.where(row >= n, pltpu.roll(x, n, 0), 0.0)


def _shift_up(x, n):
    rows = x.shape[0]
    row = lax.broadcasted_iota(jnp.int32, x.shape, 0)
    return jnp.where(row < rows - n, pltpu.roll(x, rows - n, 0), 0.0)


def _conv_blocks(S):
    cb = 256
    return cb, FFN_DIM // cb


def _conv_fwd(ag, w8, B, S, name):
    cb, ncb = _conv_blocks(S)

    def body(a_ref, g_ref, w_ref, u_ref, ut_ref):
        g = g_ref[...].astype(F32)
        w = w_ref[...]
        gc = w[0:1] * _shift_down(g, 2) + w[1:2] * _shift_down(g, 1) + w[2:3] * g + w[3:4]
        u = a_ref[...].astype(F32) * (gc * _sigmoid(gc))
        u_ref[...] = u.astype(u_ref.dtype)
        ut_ref[...] = u.T.astype(ut_ref.dtype)

    return pl.pallas_call(
        body, name=name, grid=(ncb, B),
        in_specs=[pl.BlockSpec((S, cb), lambda j, b: (b, j)),
                  pl.BlockSpec((S, cb), lambda j, b: (b, ncb + j)),
                  pl.BlockSpec((8, cb), lambda j, b: (0, j))],
        out_specs=[pl.BlockSpec((S, cb), lambda j, b: (b, j)), pl.BlockSpec((cb, S), lambda j, b: (j, b))],
        out_shape=[jax.ShapeDtypeStruct((B * S, FFN_DIM), BF16), jax.ShapeDtypeStruct((FFN_DIM, B * S), BF16)],
        compiler_params=_params(("parallel", "parallel")),
    )(ag, ag, w8)


def _conv_bwd(ag, w8, du, B, S, name):
    cb, ncb = _conv_blocks(S)

    def body(a_ref, g_ref, w_ref, du_ref, da_ref, dg_ref, dw_ref):
        g = g_ref[...].astype(F32)
        w = w_ref[...]
        g1, g2 = _shift_down(g, 1), _shift_down(g, 2)
        gc = w[0:1] * g2 + w[1:2] * g1 + w[2:3] * g + w[3:4]
        sg = _sigmoid(gc)
        du_v = du_ref[...]
        da_ref[...] = (du_v * (gc * sg)).astype(da_ref.dtype)
        dgc = du_v * a_ref[...].astype(F32) * (sg * (1.0 + gc * (1.0 - sg)))
        dg = w[2:3] * dgc + w[1:2] * _shift_up(dgc, 1) + w[0:1] * _shift_up(dgc, 2)
        dg_ref[...] = dg.astype(dg_ref.dtype)
        ones = jnp.ones((8, g.shape[0]), MXU_DTYPE)
        col_sum = lambda p: _dot(ones, p, _NN)[0:1]
        part = jnp.concatenate([col_sum(dgc * g2), col_sum(dgc * g1), col_sum(dgc * g), col_sum(dgc),
                                jnp.zeros((4, cb), F32)], axis=0)

        @pl.when(pl.program_id(1) == 0)
        def _():
            dw_ref[...] = part

        @pl.when(pl.program_id(1) > 0)
        def _():
            dw_ref[...] += part

    blk = lambda j, b: (b, j)
    return pl.pallas_call(
        body, name=name, grid=(ncb, B),
        in_specs=[pl.BlockSpec((S, cb), blk),
                  pl.BlockSpec((S, cb), lambda j, b: (b, ncb + j)),
                  pl.BlockSpec((8, cb), lambda j, b: (0, j)),
                  pl.BlockSpec((S, cb), blk)],
        out_specs=[pl.BlockSpec((S, cb), blk), pl.BlockSpec((S, cb), blk),
                   pl.BlockSpec((8, cb), lambda j, b: (0, j))],
        out_shape=[jax.ShapeDtypeStruct((B * S, FFN_DIM), BF16), jax.ShapeDtypeStruct((B * S, FFN_DIM), BF16),
                   jax.ShapeDtypeStruct((8, FFN_DIM), F32)],
        compiler_params=_params(("parallel", "arbitrary")),
    )(ag, ag, w8, du)


_NT = (((1,), (1,)), ((), ()))
_NN = (((1,), (0,)), ((), ()))
_TN = (((0,), (0,)), ((), ()))


def _dot(a, b, dn):
    return lax.dot_general(a.astype(MXU_DTYPE), b.astype(MXU_DTYPE), dn, preferred_element_type=F32)


def _run_bits(n):
    bits, b = [], 1
    while b < n:
        bits.append(b)
        b *= 2
    return bits[::-1]


def _key_runs(n, nq, update):
    for bit in _run_bits(nq + 1):
        @pl.when((n & bit) != 0)
        def _(bit=bit):
            update(n & ~(2 * bit - 1), bit, (n & (bit - 1)) == 0)


def _earlier_runs(n, nq, update):
    for bit in _run_bits(nq):
        @pl.when((n & bit) != 0)
        def _(bit=bit):
            update(n & ~(2 * bit - 1), bit, False)


def _chunk_visible(shape, nblk, blk):
    key = lax.broadcasted_iota(jnp.int32, shape, 0) - (nblk - 1) * blk
    query = lax.broadcasted_iota(jnp.int32, shape, 1)
    return jnp.logical_or(key < 0, (key // CHUNK) <= (query // CHUNK))


def _mla_attn_fwd(q, k, v, B, S):
    blk = min(MLA_FWD_BLOCK, S)
    H, nq = MLA_HEADS, S // blk

    def body(q_ref, k_ref, v_ref, o_ref, lse_ref, m_ref, l_ref, acc_ref):
        def qblock(i, _):
            q_rows = pl.ds(pl.multiple_of(i * blk, blk), blk)
            qi = q_ref[q_rows, :]
            m_ref[...] = jnp.full(m_ref.shape, MASK_VALUE, F32)
            l_ref[...] = jnp.zeros(l_ref.shape, F32)
            acc_ref[...] = jnp.zeros(acc_ref.shape, F32)

            def keys(first, nblk, last):
                rows = pl.ds(pl.multiple_of(first * blk, blk), nblk * blk)
                s = _dot(k_ref[rows, :], qi, _NT)
                s = jnp.where(jnp.logical_or(_chunk_visible(s.shape, nblk, blk), jnp.logical_not(last)), s, MASK_VALUE)
                m = m_ref[...]
                m2 = jnp.maximum(m, jnp.max(s, axis=0, keepdims=True))
                alpha = jnp.exp(m - m2)
                p = jnp.exp(s - m2)
                l_ref[...] = alpha * l_ref[...] + jnp.sum(p, axis=0, keepdims=True)
                acc_ref[...] = alpha * acc_ref[...] + _dot(v_ref[rows, :], p, _TN)
                m_ref[...] = m2

            _key_runs(i + 1, nq, keys)
            l = l_ref[...]
            o_ref[q_rows, :] = (acc_ref[...] / l).T
            lse_ref[0, :, q_rows] = m_ref[...] + jnp.log(l)
            return 0

        lax.fori_loop(0, nq, qblock, 0)

    return pl.pallas_call(
        body, name="mla_attn_fwd", grid=(B, H),
        in_specs=[pl.BlockSpec((S, MLA_PAD), lambda b, h: (b, h)),
                  pl.BlockSpec((S, MLA_PAD), lambda b, h: (b, h)),
                  pl.BlockSpec((S, MLA_V), lambda b, h: (b, h))],
        out_specs=[pl.BlockSpec((S, MLA_V), lambda b, h: (b, h)),
                   pl.BlockSpec((1, 1, S), lambda b, h: (b * H + h, 0, 0))],
        out_shape=[jax.ShapeDtypeStruct((B * S, H * MLA_V), F32), jax.ShapeDtypeStruct((B * H, 1, S), F32)],
        scratch_shapes=[pltpu.VMEM((1, blk), F32), pltpu.VMEM((1, blk), F32), pltpu.VMEM((MLA_V, blk), F32)],
        compiler_params=_params(("parallel", "parallel")),
    )(q, k, v)


def _mla_attn_bwd(q, k, v, o, do, lse, B, S):
    blk = min(MLA_FWD_BLOCK, S)
    H, nq = MLA_HEADS, S // blk

    def body(q_ref, k_ref, v_ref, o_ref, do_ref, lse_ref, dq_ref, dk_ref, dv_ref, kt_ref, dqt_ref):
        dk_ref[...] = jnp.zeros(dk_ref.shape, F32)
        dv_ref[...] = jnp.zeros(dv_ref.shape, F32)
        for g in range(nq):
            kt_ref[g] = k_ref[g * blk:(g + 1) * blk, :].T

        def qblock(i, _):
            q_rows = pl.ds(pl.multiple_of(i * blk, blk), blk)
            qi = q_ref[q_rows, :]
            doi = do_ref[q_rows, :]
            delta = jnp.sum((doi * o_ref[q_rows, :]).T, axis=0, keepdims=True)
            lse_i = lse_ref[0, :, q_rows]
            doi = doi.astype(MXU_DTYPE)
            dqt_ref[...] = jnp.zeros(dqt_ref.shape, F32)

            def keys(first, nblk, last):
                rows = pl.ds(pl.multiple_of(first * blk, blk), nblk * blk)
                k_run, v_run = k_ref[rows, :], v_ref[rows, :]
                p = jnp.exp(_dot(k_run, qi, _NT) - lse_i)
                p = jnp.where(jnp.logical_or(_chunk_visible(p.shape, nblk, blk), jnp.logical_not(last)), p, 0.0)
                ds = (p * (_dot(v_run, doi, _NT) - delta)).astype(MXU_DTYPE)
                dk_ref[rows, :] += _dot(ds, qi, _NN)
                dv_ref[rows, :] += _dot(p, doi, _NN)
                for r in range(nblk):
                    dqt_ref[...] += _dot(kt_ref[first + r], ds[r * blk:(r + 1) * blk, :], _NN)

            _key_runs(i + 1, nq, keys)
            dq_ref[q_rows, :] = dqt_ref[...].T
            return 0

        lax.fori_loop(0, nq, qblock, 0)

    qk_spec = pl.BlockSpec((S, MLA_PAD), lambda b, h: (b, h))
    v_spec = pl.BlockSpec((S, MLA_V), lambda b, h: (b, h))
    return pl.pallas_call(
        body, name="mla_attn_bwd", grid=(B, H),
        in_specs=[qk_spec, qk_spec, v_spec, v_spec, v_spec,
                  pl.BlockSpec((1, 1, S), lambda b, h: (b * H + h, 0, 0))],
        out_specs=[qk_spec, qk_spec, v_spec],
        out_shape=[jax.ShapeDtypeStruct((B * S, H * MLA_PAD), F32), jax.ShapeDtypeStruct((B * S, H * MLA_PAD), F32),
                   jax.ShapeDtypeStruct((B * S, H * MLA_V), F32)],
        scratch_shapes=[pltpu.VMEM((nq, MLA_PAD, blk), q.dtype), pltpu.VMEM((MLA_PAD, blk), F32)],
        compiler_params=_params(("parallel", "parallel")),
    )(q, k, v, o, do, lse)


def _ret_log_gamma():
    lg = np.log1p(-np.exp2(RET_GAMMA_BASE - np.arange(RET_HEADS, dtype=np.float32))).astype(np.float32)
    return jnp.asarray(np.broadcast_to(lg[:, None, None], (RET_HEADS, 8, LANES)).copy())


RET_BLOCK = 512


def _ret_local_scale(lg, shape, blk, rising):
    local = lax.broadcasted_iota(jnp.int32, shape, 0) % blk
    return jnp.exp(lg * (local if rising else blk - 1 - local).astype(F32))


def _ret_pair_factor(lg, blk, steps):
    return jnp.exp(lg * (blk * (steps - 1) + 1).astype(F32))


def _ret_own_decay(lg, blk, transposed):
    a = lax.broadcasted_iota(jnp.int32, (blk, blk), 0)
    b = lax.broadcasted_iota(jnp.int32, (blk, blk), 1)
    query, key = (b, a) if transposed else (a, b)
    dec = jnp.exp(lg * jnp.abs(query - key).astype(F32))
    return jnp.where((key // CHUNK) <= (query // CHUNK), dec, 0.0)


def _ret_attn_fwd(q, k, v, B, S):
    blk = min(RET_BLOCK, S)
    H, nq = RET_HEADS, S // blk

    def body(lg_ref, q_ref, k_ref, v_ref, o_ref, ks_ref, dec_ref, acc_ref):
        lg = lg_ref[0, 0:1, 0:1]
        ks_ref[...] = (k_ref[...].astype(F32) * _ret_local_scale(lg, k_ref.shape, blk, False)).astype(ks_ref.dtype)
        dec_ref[...] = _ret_own_decay(lg, blk, False)

        def qblock(i, _):
            q_rows = pl.ds(pl.multiple_of(i * blk, blk), blk)
            qi = q_ref[q_rows, :]
            qs = (qi.astype(F32) * _ret_local_scale(lg, qi.shape, blk, True)).astype(qi.dtype)
            a = _dot(qi, k_ref[q_rows, :], _NT) * dec_ref[...]
            acc_ref[...] = _dot(a, v_ref[q_rows, :], _NN)

            def keys(first, nblk, _):
                rows = pl.ds(pl.multiple_of(first * blk, blk), nblk * blk)
                steps = i - first - lax.broadcasted_iota(jnp.int32, (1, nblk * blk), 1) // blk
                a = _dot(qs, ks_ref[rows, :], _NT) * _ret_pair_factor(lg, blk, steps)
                acc_ref[...] += _dot(a, v_ref[rows, :], _NN)

            _earlier_runs(i, nq, keys)
            o_ref[q_rows, :] = acc_ref[...]
            return 0

        lax.fori_loop(0, nq, qblock, 0)

    qk_spec = pl.BlockSpec((S, RET_QK), lambda b, h: (b, h))
    v_spec = pl.BlockSpec((S, RET_V), lambda b, h: (b, h))
    return pl.pallas_call(
        body, name="ret_attn_fwd", grid=(B, H),
        in_specs=[pl.BlockSpec((1, 8, LANES), lambda b, h: (h, 0, 0)), qk_spec, qk_spec, v_spec],
        out_specs=v_spec,
        out_shape=jax.ShapeDtypeStruct((B * S, H * RET_V), F32),
        scratch_shapes=[pltpu.VMEM((S, RET_QK), k.dtype), pltpu.VMEM((blk, blk), F32), pltpu.VMEM((blk, RET_V), F32)],
        compiler_params=_params(("parallel", "parallel")),
    )(_ret_log_gamma(), q, k, v)


def _ret_attn_bwd(q, k, v, do, B, S):
    blk = min(RET_BLOCK, S)
    H, nq = RET_HEADS, S // blk

    def body(lg_ref, q_ref, k_ref, v_ref, do_ref, dq_ref, dk_ref, dv_ref, ks_ref, kst_ref, dks_ref, dqt_ref, dec_ref):
        lg = lg_ref[0, 0:1, 0:1]
        dk_ref[...] = jnp.zeros(dk_ref.shape, F32)
        dv_ref[...] = jnp.zeros(dv_ref.shape, F32)
        dks_ref[...] = jnp.zeros(dks_ref.shape, F32)
        ks_ref[...] = (k_ref[...].astype(F32) * _ret_local_scale(lg, k_ref.shape, blk, False)).astype(ks_ref.dtype)
        for g in range(nq):
            kst_ref[g] = ks_ref[g * blk:(g + 1) * blk, :].T
        dec_ref[...] = _ret_own_decay(lg, blk, True)

        def qblock(i, _):
            q_rows = pl.ds(pl.multiple_of(i * blk, blk), blk)
            qi = q_ref[q_rows, :]
            q_scale = _ret_local_scale(lg, qi.shape, blk, True)
            qs = (qi.astype(F32) * q_scale).astype(qi.dtype)
            doi = do_ref[q_rows, :].astype(MXU_DTYPE)
            ki = k_ref[q_rows, :]
            dec = dec_ref[...]
            a = _dot(ki, qi, _NT) * dec
            da = (_dot(v_ref[q_rows, :], doi, _NT) * dec).astype(MXU_DTYPE)
            dv_ref[q_rows, :] += _dot(a, doi, _NN)
            dk_ref[q_rows, :] += _dot(da, qi, _NN)
            dq_own = _dot(da, ki, _TN)
            dqt_ref[...] = jnp.zeros(dqt_ref.shape, F32)

            def keys(first, nblk, _):
                for r in range(nblk):
                    g = first + r
                    rows = pl.ds(pl.multiple_of(g * blk, blk), blk)
                    c = _ret_pair_factor(lg, blk, i - g)
                    a = _dot(ks_ref[rows, :], qs, _NT) * c
                    da = (_dot(v_ref[rows, :], doi, _NT) * c).astype(MXU_DTYPE)
                    dv_ref[rows, :] += _dot(a, doi, _NN)
                    dks_ref[rows, :] += _dot(da, qs, _NN)
                    dqt_ref[...] += _dot(kst_ref[g], da, _NN)

            _earlier_runs(i, nq, keys)
            dq_ref[q_rows, :] = dqt_ref[...].T * q_scale + dq_own
            return 0

        lax.fori_loop(0, nq, qblock, 0)
        dk_ref[...] += dks_ref[...] * _ret_local_scale(lg, dks_ref.shape, blk, False)

    qk_spec = pl.BlockSpec((S, RET_QK), lambda b, h: (b, h))
    v_spec = pl.BlockSpec((S, RET_V), lambda b, h: (b, h))
    return pl.pallas_call(
        body, name="ret_attn_bwd", grid=(B, H),
        in_specs=[pl.BlockSpec((1, 8, LANES), lambda b, h: (h, 0, 0)), qk_spec, qk_spec, v_spec, v_spec],
        out_specs=[qk_spec, qk_spec, v_spec],
        out_shape=[jax.ShapeDtypeStruct((B * S, H * RET_QK), F32), jax.ShapeDtypeStruct((B * S, H * RET_QK), F32),
                   jax.ShapeDtypeStruct((B * S, H * RET_V), F32)],
        scratch_shapes=[pltpu.VMEM((S, RET_QK), k.dtype), pltpu.VMEM((nq, RET_QK, blk), k.dtype),
                        pltpu.VMEM((S, RET_QK), F32), pltpu.VMEM((RET_QK, blk), F32), pltpu.VMEM((blk, blk), F32)],
        compiler_params=_params(("parallel", "parallel")),
    )(_ret_log_gamma(), q, k, v, do)


def _adamw(w, g, m, v, name):
    R, C = w.shape
    br = R if R * C * 4 <= 2 ** 21 else _pick_rows(R, max(8, (2 ** 21) // (C * 4)))

    def body(w_ref, g_ref, m_ref, v_ref, d_ref, mo_ref, vo_ref):
        g_v = g_ref[...]
        m_v = ADAM_B1 * m_ref[...] + (1.0 - ADAM_B1) * g_v
        v_v = ADAM_B2 * v_ref[...] + (1.0 - ADAM_B2) * (g_v * g_v)
        m_hat = m_v / (1.0 - ADAM_B1 ** ADAM_STEP)
        v_hat = v_v / (1.0 - ADAM_B2 ** ADAM_STEP)
        d_ref[...] = -ADAM_LR * (m_hat / (jnp.sqrt(v_hat) + ADAM_EPS) + ADAM_WD * w_ref[...])
        mo_ref[...] = m_v
        vo_ref[...] = v_v

    blk = pl.BlockSpec((br, C), lambda i: (i, 0))
    return pl.pallas_call(
        body, name=name, grid=(R // br,),
        in_specs=[blk] * 4, out_specs=[blk] * 3,
        out_shape=[jax.ShapeDtypeStruct((R, C), F32)] * 3,
        compiler_params=_params(("parallel",)),
    )(w, g, m, v)


def _pick_rows(R, target):
    best = None
    for d in range(8, min(R, target) + 1, 8):
        if R % d == 0:
            best = d
    assert best is not None, (R, target)
    return best


def _position():
    return lax.axis_index("x"), lax.axis_index("y"), lax.axis_index("c")


HBM_SPEC = pl.BlockSpec(memory_space=pltpu.HBM)


def _other_chips(x, y):
    return [(1 - x, y), (x, 1 - y), (1 - x, 1 - y)]


def _all_gather_weights(bigs, small):
    nb = len(bigs)

    def body(*refs):
        big_refs, small_ref = refs[:nb], refs[nb]
        obig, osmall = refs[nb + 1:2 * nb + 1], refs[2 * nb + 1]
        ici_send, ici_recv, d2d_send, d2d_recv, sm_send, sm_recv = refs[2 * nb + 2:]
        x, y, c = _position()
        me = 2 * x + y
        chips = _other_chips(x, y)

        def rows(n, half):
            rh = bigs[n].shape[0] // 2
            return pl.ds(half * rh, rh)

        def over_ici(n, j, slot, from_shard):
            px, py = chips[j]
            dst = obig[n].at[slot, rows(n, c)]
            return pltpu.make_async_remote_copy(
                src_ref=big_refs[n].at[rows(n, c)] if from_shard else dst, dst_ref=dst,
                send_sem=ici_send.at[3 * n + j], recv_sem=ici_recv.at[3 * n + j],
                device_id=(px, py, c), device_id_type=MESH)

        def over_d2d(n, j, half):
            px, py = chips[j]
            part = obig[n].at[2 * px + py, rows(n, half)]
            return pltpu.make_async_remote_copy(
                src_ref=part, dst_ref=part, send_sem=d2d_send.at[3 * n + j], recv_sem=d2d_recv.at[3 * n + j],
                device_id=(x, y, 1 - c), device_id_type=MESH)

        def small_copy(j, slot):
            px, py = chips[j]
            return pltpu.make_async_remote_copy(
                src_ref=small_ref, dst_ref=osmall.at[slot], send_sem=sm_send.at[j], recv_sem=sm_recv.at[j],
                device_id=(px, py, c), device_id_type=MESH)

        sends = [over_ici(n, j, me, True) for n in range(nb) for j in range(3)]
        sends += [small_copy(j, me) for j in range(3)]
        for cp in sends:
            cp.start()
        passed = []
        for n in range(nb):
            for j, (px, py) in enumerate(chips):
                over_ici(n, j, 2 * px + py, False).wait_recv()
                fwd = over_d2d(n, j, c)
                fwd.start()
                passed.append(fwd)
        for n in range(nb):
            for j in range(3):
                over_d2d(n, j, 1 - c).wait_recv()
        for j, (px, py) in enumerate(chips):
            small_copy(j, 2 * px + py).wait_recv()
        for cp in sends + passed:
            cp.wait_send()

    dma = pltpu.SemaphoreType.DMA
    return pl.pallas_call(
        body, name="weights_all_gather",
        in_specs=[HBM_SPEC] * (nb + 1), out_specs=[HBM_SPEC] * (nb + 1),
        out_shape=[jax.ShapeDtypeStruct((N_SHARD,) + b.shape, b.dtype) for b in bigs]
        + [jax.ShapeDtypeStruct((N_SHARD,) + small.shape, small.dtype)],
        scratch_shapes=[dma((3 * nb,)), dma((3 * nb,)), dma((3 * nb,)), dma((3 * nb,)), dma((3,)), dma((3,))],
    )(*bigs, small)


SEM_SPEC = pl.BlockSpec(memory_space=pltpu.SEMAPHORE)
DATAFLOW_EFFECT = pltpu.SideEffectType.DATAFLOW_SIDE_EFFECTING
N_PEERS = N_DEV - 1


def _grad_copies(p_refs, land_refs, send_sems, recv_sems):
    x, y, c = _position()
    copies = []
    for a, (p_ref, land_ref) in enumerate(zip(p_refs, land_refs)):
        rh = p_ref.shape[1] // 2
        for k in range(1, N_DEV):
            px = 1 - x if k & 4 else x
            py = 1 - y if k & 2 else y
            pc = 1 - c if k & 1 else c
            copies.append(pltpu.make_async_remote_copy(
                src_ref=p_ref.at[2 * px + py, pl.ds(pc * rh, rh)], dst_ref=land_ref.at[k - 1],
                send_sem=send_sems.at[N_PEERS * a + k - 1], recv_sem=recv_sems.at[N_PEERS * a + k - 1],
                device_id=(px, py, pc), device_id_type=MESH))
    return copies


def _weight_copies(w_refs, land_refs, send_sems, recv_sems):
    x, y, c = _position()
    copies = []
    for a, (w_ref, land_ref) in enumerate(zip(w_refs, land_refs)):
        for j, (px, py) in enumerate(_other_chips(x, y)):
            copies.append(pltpu.make_async_remote_copy(
                src_ref=w_ref, dst_ref=land_ref.at[2 * x + y], send_sem=send_sems.at[3 * a + j],
                recv_sem=recv_sems.at[3 * a + j], device_id=(px, py, c), device_id_type=MESH))
    return copies


def _exchange_start(make_copies, srcs, lands, n_sems, name, after=None):
    n, m = len(srcs), len(lands)
    n_in = n + m + (after is not None)

    def body(*refs):
        send_sems, recv_sems, token = refs[n_in], refs[n_in + 1], refs[-1]
        for cp in make_copies(refs[:n], refs[n:n + m], send_sems, recv_sems):
            cp.start()
        token[...] = jnp.zeros(token.shape, token.dtype)

    hbm = lambda a: pltpu.with_memory_space_constraint(a, pltpu.HBM)
    dma = pltpu.SemaphoreType.DMA
    res = pl.pallas_call(
        body, name=name,
        in_specs=[HBM_SPEC] * (n + m) + ([] if after is None else [pl.BlockSpec(memory_space=pl.ANY)]),
        out_specs=[SEM_SPEC, SEM_SPEC] + [HBM_SPEC] * (n + m) + [pl.BlockSpec(memory_space=pltpu.VMEM)],
        out_shape=[dma((n_sems,)), dma((n_sems,))] + [pltpu.HBM(a.shape, a.dtype) for a in list(srcs) + list(lands)]
        + [jax.ShapeDtypeStruct((8, LANES), F32)],
        input_output_aliases={i: 2 + i for i in range(n + m)},
        compiler_params=pltpu.CompilerParams(has_side_effects=DATAFLOW_EFFECT),
    )(*[hbm(a) for a in srcs], *[hbm(a) for a in lands], *(() if after is None else (after,)))
    return res[0], res[1], list(res[2:2 + n]), list(res[2 + n:2 + n + m]), res[-1]


def _exchange_wait(make_copies, send_sems, recv_sems, srcs, lands, after, name):
    n, m = len(srcs), len(lands)

    def body(*refs):
        for cp in make_copies(refs[:n], refs[n:n + m], refs[n + m], refs[n + m + 1]):
            cp.wait_send()
            cp.wait_recv()

    res = pl.pallas_call(
        body, name=name,
        in_specs=[HBM_SPEC] * (n + m) + [SEM_SPEC, SEM_SPEC, pl.BlockSpec(memory_space=pl.ANY)],
        out_specs=[HBM_SPEC] * (n + m),
        out_shape=[pltpu.HBM(a.shape, a.dtype) for a in list(srcs) + list(lands)],
        input_output_aliases={i: i for i in range(n + m)},
        compiler_params=pltpu.CompilerParams(has_side_effects=DATAFLOW_EFFECT),
    )(*srcs, *lands, send_sems, recv_sems, after)
    return list(res[:n]), list(res[n:])


def _sum_partials(p, land, name):
    _, rh, cols = land.shape
    br = _pick_rows(rh, 256)
    nrb = rh // br
    x, y, c = _position()
    where = jnp.stack([2 * x + y, c]).astype(jnp.int32)

    def body(where_ref, p_ref, land_ref, o_ref):
        acc = p_ref[...].astype(F32)
        for k in range(N_PEERS):
            acc = acc + land_ref[k].astype(F32)
        o_ref[...] = acc

    return pl.pallas_call(
        body, name=name,
        grid_spec=pltpu.PrefetchScalarGridSpec(
            num_scalar_prefetch=1, grid=(nrb,),
            in_specs=[pl.BlockSpec((None, br, cols), lambda r, where_ref: (where_ref[0], where_ref[1] * nrb + r, 0)),
                      pl.BlockSpec((N_PEERS, br, cols), lambda r, where_ref: (0, r, 0))],
            out_specs=pl.BlockSpec((None, br, cols), lambda r, where_ref: (where_ref[1], r, 0))),
        out_shape=jax.ShapeDtypeStruct((2, rh, cols), F32),
        compiler_params=_params(("parallel",)),
    )(where, p, land)


def _sibling_share(fulls, name):
    n = len(fulls)

    def body(*refs):
        o_refs = refs[n:2 * n]
        send_sems, recv_sems = refs[2 * n:]
        x, y, c = _position()

        def copy(a, half):
            return pltpu.make_async_remote_copy(
                src_ref=o_refs[a].at[half], dst_ref=o_refs[a].at[half], send_sem=send_sems.at[a],
                recv_sem=recv_sems.at[a], device_id=(x, y, 1 - c), device_id_type=MESH)

        sends = [copy(a, c) for a in range(n)]
        for cp in sends:
            cp.start()
        for a in range(n):
            copy(a, 1 - c).wait_recv()
        for cp in sends:
            cp.wait_send()

    dma = pltpu.SemaphoreType.DMA
    return pl.pallas_call(
        body, name=name,
        in_specs=[HBM_SPEC] * n, out_specs=[HBM_SPEC] * n,
        out_shape=[jax.ShapeDtypeStruct(f.shape, f.dtype) for f in fulls],
        input_output_aliases={a: a for a in range(n)},
        scratch_shapes=[dma((n,)), dma((n,))],
    )(*fulls)


def _all_reduce_small(v):
    R, cols = v.shape

    def body(v_ref, o_ref, buf_ref, send_sems, recv_sems):
        x, y, c = _position()
        me = 4 * x + 2 * y + c
        buf_ref[me] = v_ref[...]
        sends = []
        for k in range(1, N_DEV):
            px = 1 - x if k & 4 else x
            py = 1 - y if k & 2 else y
            pc = 1 - c if k & 1 else c
            sends.append(pltpu.make_async_remote_copy(
                src_ref=v_ref, dst_ref=buf_ref.at[me], send_sem=send_sems.at[k - 1], recv_sem=recv_sems.at[k - 1],
                device_id=(px, py, pc), device_id_type=MESH))
        for cp in sends:
            cp.start()
        for k in range(1, N_DEV):
            px = 1 - x if k & 4 else x
            py = 1 - y if k & 2 else y
            pc = 1 - c if k & 1 else c
            pltpu.make_async_remote_copy(
                src_ref=v_ref, dst_ref=buf_ref.at[4 * px + 2 * py + pc], send_sem=send_sems.at[k - 1],
                recv_sem=recv_sems.at[k - 1], device_id=(px, py, pc), device_id_type=MESH).wait_recv()
        for cp in sends:
            cp.wait_send()
        acc = buf_ref[0]
        for d in range(1, N_DEV):
            acc = acc + buf_ref[d]
        o_ref[...] = acc

    return pl.pallas_call(
        body, name="small_grads_all_reduce",
        in_specs=[pl.BlockSpec(memory_space=pltpu.VMEM)], out_specs=pl.BlockSpec(memory_space=pltpu.VMEM),
        out_shape=jax.ShapeDtypeStruct((R, cols), F32),
        scratch_shapes=[pltpu.VMEM((N_DEV, R, cols), F32), pltpu.SemaphoreType.DMA((N_DEV - 1,)),
                        pltpu.SemaphoreType.DMA((N_DEV - 1,))],
    )(v)


def _rope_tables(S, half, width):
    inv_freq = ROPE_THETA ** (-jnp.arange(half, dtype=F32) / half)
    ang = jnp.arange(S).astype(F32)[:, None] * inv_freq[None, :]
    return jnp.cos(ang), jnp.sin(ang)


def _slot_rows(a):
    return a.reshape(N_SHARD, -1, a.shape[-1])


def _local_step(x, target, w, B, S, late, exchange, reduce_small):
    T = B * S
    D = D_MODEL
    bm = min(512, S)
    full = lambda a, wd, tile=None: (a, wd, 0, tile or wd)
    g = {}

    cos_r, sin_r = _rope_tables(S, RET_QK // 2, LANES)
    cos_m, sin_m = _rope_tables(S, MLA_ROPE // 2, LANES)
    zeros64 = jnp.zeros((S, 64), F32)
    cos_m = jnp.concatenate([cos_m, cos_m, zeros64], axis=1)
    sin_m = jnp.concatenate([-sin_m, sin_m, zeros64], axis=1)

    def ffn_fwd(xin, h, ht, i, next_gain):
        w.update(late(f"ffn{i}", xin))
        norm = w["ffn_norm"][i:i + 1]
        ag = _mm(h, w[f"ffn_w_in{i}"], "nn", BF16, f"ffn{i}_in", bm=1024, bn=2816, cols_outer=True)
        u, ut = _conv_fwd(ag, w["ffn_conv8"][i], B, S, f"ffn{i}_conv")
        if next_gain is None:
            out = _mm_out_loss(u, w[f"ffn_w_out{i}"], xin, target, f"ffn{i}_out")
        else:
            out = _mm_out_norm(u, w[f"ffn_w_out{i}"], xin, next_gain, f"ffn{i}_out")
        return out, (xin, norm, ht, ag, ut)

    def ffn_bwd(dxout, dxout_c, saved, i):
        xin, norm, ht, ag, ut = saved
        du = _mm(dxout_c, w[f"ffn_w_out{i}"], "nt", F32, f"ffn{i}_out_dx", bm=1024, bn=2816, cols_outer=True)
        g_w_out = _mm(ut, dxout_c, "nn", BF16, f"ffn{i}_out_dw", bm=1408, bn=512, bk=T)
        da, dg, dw8 = _conv_bwd(ag, w["ffn_conv8"][i], du, B, S, f"ffn{i}_conv_bwd")
        g_w_in = _mm(ht, [da, dg], "nn", BF16, f"ffn{i}_in_dw", bm=1024, bn=1408, bk=T // 2, out_slots=N_SHARD)
        token = exchange(f"ffn{i}", [g_w_in, _slot_rows(g_w_out)])
        dxin, dxin_c, g_norm = _mm_dx_norm([da, dg], w[f"ffn_w_in{i}"], xin, norm, dxout, f"ffn{i}_in_dx", after=token)
        return dxin, dxin_c, (g_norm, dw8)

    h0, h0t = _rowwise_fwd(_fn_rms, "ret_norm", [full(x, D)], [], [(w["ret_norm"], D)], [(D, D, BF16)], bm, S,
                           transposed=(0,))
    proj = _mm(h0, w["ret_w_in"], "nn", BF16, "ret_in", bm=1024, bn=2048, after=w["started"], cols_outer=True)
    HQ, HV = RET_HEADS * RET_QK, RET_HEADS * RET_V
    rope_rows = [(proj, 2 * HQ + HV, 0, LANES)]
    q_r, k_r, v_r = _rowwise_fwd(_fn_ret_rope, "ret_rope", rope_rows, [cos_r, sin_r], [],
                                 [(HQ, LANES, BF16), (HQ, LANES, BF16), (HV, LANES, BF16)], bm, S)
    ret_o = _ret_attn_fwd(q_r, k_r, v_r, B, S)
    gate_rows = [full(ret_o, HV, RET_V), (proj, HV, 2, RET_V)]
    y0, y0t = _rowwise_fwd(_fn_ret_gate, "ret_gate", gate_rows, [], [(w["ret_gn"], RET_V)], [(HV, RET_V, BF16)], bm, S,
                           transposed=(0,))
    w.update(late("ret_out", y0))
    x1, h1, h1t = _mm_out_norm(y0, w["ret_w_out"], x, w["ffn_norm"][0:1], "ret_out")
    (x2, h2, _), ffn0_saved = ffn_fwd(x1, h1, h1t, 0, w["mla_norm"])

    w.update(late("mla", x2))
    proj2 = _mm(h2, w["mla_w_in"], "nn", F32, "mla_in", bm=2048)
    lat_consts = [(w["mla_q_norm"], LANES), (w["mla_kv_norm"], LANES)]
    cqn, ckvn, kr = _rowwise_fwd(_fn_mla_lat, "mla_latent_norm", [full(proj2, MLA_IN_PAD, LANES)], [], lat_consts,
                                 [(MLA_Q_RANK, LANES, BF16), (MLA_KV_RANK, LANES, BF16), (LANES, LANES, F32)], bm, S)
    qf = _mm(cqn, w["mla_w_qb"], "nn", BF16, "mla_qb", bm=2048, bn=2048)
    kvf = _mm(ckvn, w["mla_w_kvb"], "nn", BF16, "mla_kvb", bm=2048, bn=2048)
    HP, HVm = MLA_HEADS * MLA_PAD, MLA_HEADS * MLA_V
    head_rows = [full(qf, HP, LANES), full(kvf, HP, LANES), full(kr, LANES)]
    head_consts = [(w["mla_q_head_norm"], LANES), (w["mla_k_head_norm"], LANES)]
    q_a, k_a, v_a = _rowwise_fwd(_fn_mla_heads, "mla_heads", head_rows, [cos_m, sin_m], head_consts,
                                 [(HP, LANES, BF16), (HP, LANES, BF16), (HVm, LANES, BF16)], bm, S)
    att_o, lse = _mla_attn_fwd(q_a, k_a, v_a, B, S)
    x3, h3, h3t = _mm_out_norm(att_o, w["mla_w_out"], x2, w["ffn_norm"][1:2], "mla_out")
    (dy, dy_c, loss), ffn1_saved = ffn_fwd(x3, h3, h3t, 1, None)

    dx3, dx3_c, (g_n1, dw8_1) = ffn_bwd(dy, dy_c, ffn1_saved, 1)

    d_att_o = _mm(dx3_c, w["mla_w_out"], "nt", F32, "mla_out_dx", bm=2048)
    g_mla_out = _mm(att_o, dx3_c, "tn", BF16, "mla_out_dw")
    dq_a, dk_a, dv_a = _mla_attn_bwd(q_a, k_a, v_a, att_o, d_att_o, lse, B, S)
    (dqf, dkvf, dkr), (g["mla_q_head_norm"], g["mla_k_head_norm"]) = _rowwise_bwd(
        _fn_mla_heads, "mla_heads_bwd", head_rows, [cos_m, sin_m], head_consts,
        [(dq_a, LANES), (dk_a, LANES), (dv_a, LANES)], bm, S, grad_dtypes=[BF16, BF16, F32])
    dcqn = _mm(dqf, w["mla_w_qb"], "nt", F32, "mla_qb_dx", bm=2048)
    g_qb = _mm(cqn, dqf, "tn", BF16, "mla_qb_dw")
    g_qb = _to_slots(_unpad_heads(g_qb, 1), 1).reshape(N_SHARD, MLA_Q_RANK, -1)
    dckvn = _mm(dkvf, w["mla_w_kvb"], "nt", F32, "mla_kvb_dx", bm=2048)
    g_kvb = _mm(ckvn, dkvf, "tn", BF16, "mla_kvb_dw", bn=512, out_slots=N_SHARD)
    (dproj2,), (g["mla_q_norm"], g["mla_kv_norm"]) = _rowwise_bwd(
        _fn_mla_lat, "mla_latent_norm_bwd", [full(proj2, MLA_IN_PAD, LANES)], [], lat_consts,
        [(dcqn, LANES), (dckvn, LANES), (dkr, LANES)], bm, S, grad_dtypes=[BF16])
    g_mla_in = _mm(h2, dproj2, "tn", BF16, "mla_in_dw")
    token = exchange("mla", [_slot_rows(g_mla_in[:, :MLA_IN]), g_qb, g_kvb, _slot_rows(g_mla_out)])
    dx2, dx2_c, g["mla_norm"] = _mm_dx_norm([dproj2], w["mla_w_in"], x2, w["mla_norm"], dx3, "mla_in_dx", bm=512,
                                            after=token)

    dx1, dx1_c, (g_n0, dw8_0) = ffn_bwd(dx2, dx2_c, ffn0_saved, 0)

    dy0 = _mm(dx1_c, w["ret_w_out"], "nt", F32, "ret_out_dx", bm=1024, bn=2048, cols_outer=True)
    g_ret_out = _mm(y0t, dx1_c, "nn", BF16, "ret_out_dw", bm=1024, bn=512, bk=T)
    token = exchange("reto", [_slot_rows(g_ret_out)])
    gn_behind = w["ret_gn"] + token[0:1, 0:1]
    (d_ret_o, dgate), (g["ret_gn"],) = _rowwise_bwd(_fn_ret_gate, "ret_gate_bwd", gate_rows, [], [(gn_behind, RET_V)],
                                                    [(dy0, RET_V)], bm, S, grad_dtypes=[F32, BF16])
    dq_r, dk_r, dv_r = _ret_attn_bwd(q_r, k_r, v_r, d_ret_o, B, S)
    (dqkv,), _ = _rowwise_bwd(_fn_ret_rope, "ret_rope_bwd", rope_rows, [cos_r, sin_r], [],
                              [(dq_r, LANES), (dk_r, LANES), (dv_r, LANES)], bm, S, grad_dtypes=[BF16], linear=True)
    dx, _, g["ret_norm"] = _mm_dx_norm([dqkv, dgate], w["ret_w_in"], x, w["ret_norm"], dx1, "ret_in_dx")
    g["ffn_norm"] = jnp.concatenate([g_n0, g_n1], axis=0)
    g["ffn_conv_w"] = jnp.stack([dw8_0[0:3], dw8_1[0:3]])
    g["ffn_conv_b"] = jnp.stack([dw8_0[3], dw8_1[3]])
    reduced_small = reduce_small(g, loss)
    g_ret_in = _mm(h0t, [dqkv, dgate], "nn", BF16, "ret_in_dw", bm=1024, bn=512, bk=T, out_slots=N_SHARD,
                   after=reduced_small)
    exchange("ret", [g_ret_in])
    return loss, dx, reduced_small


_SMALL_SHARDED = [("ret_gn", 2), ("mla_norm", 1), ("mla_q_norm", 1), ("mla_kv_norm", 1), ("ffn_conv_w", 2)]
_SMALL_REPLICATED = ["ret_norm", "mla_q_head_norm", "mla_k_head_norm", "ffn_norm", "ffn_conv_b"]
_SMALL_ALL = ["ret_norm", "ret_gn", "mla_norm", "mla_q_norm", "mla_kv_norm", "mla_q_head_norm", "mla_k_head_norm",
              "ffn_norm", "ffn_conv_w", "ffn_conv_b"]


def _to_slots(full, axis):
    shape = full.shape
    split = shape[:axis] + (N_SHARD, shape[axis] // N_SHARD) + shape[axis + 1:]
    return jnp.moveaxis(full.reshape(split), axis, 0).reshape(N_SHARD, -1)


def _from_slots(slots, shard_shape, axis):
    parts = jnp.moveaxis(slots.reshape((N_SHARD,) + tuple(shard_shape)), 0, axis)
    full = shard_shape[:axis] + (N_SHARD * shard_shape[axis],) + shard_shape[axis + 1:]
    return parts.reshape(full)


def _pad_rows(flat, cols, row_unit):
    n, L = flat.shape
    unit = cols * row_unit
    Lp = -(-L // unit) * unit
    if Lp != L:
        flat = jnp.concatenate([flat, jnp.zeros((n, Lp - L), flat.dtype)], axis=1)
    return flat.reshape(n, Lp // cols, cols)


def _pad_heads(a, axis):
    shape = a.shape
    heads = shape[axis] // MLA_QK
    a = a.reshape(shape[:axis] + (heads, MLA_QK) + shape[axis + 1:])
    pad = [(0, 0)] * a.ndim
    pad[axis + 1] = (0, MLA_PAD - MLA_QK)
    return jnp.pad(a, pad).reshape(shape[:axis] + (heads * MLA_PAD,) + shape[axis + 1:])


def _unpad_heads(a, axis):
    shape = a.shape
    a = a.reshape(shape[:axis] + (MLA_HEADS, MLA_PAD) + shape[axis + 1:])
    a = lax.slice_in_dim(a, 0, MLA_QK, axis=axis + 1)
    return a.reshape(shape[:axis] + (MLA_HEADS * MLA_QK,) + shape[axis + 1:])


def kernel(x, ret_norm, ret_w_in, ret_gn, ret_w_out, mla_norm, mla_w_in, mla_q_norm, mla_w_qb, mla_kv_norm, mla_w_kvb, mla_q_head_norm, mla_k_head_norm, mla_w_out, ffn_norm, ffn_w_in, ffn_conv_w, ffn_conv_b, ffn_w_out, loss_target, m_ret_norm, m_ret_w_in, m_ret_gn, m_ret_w_out, m_mla_norm, m_mla_w_in, m_mla_q_norm, m_mla_w_qb, m_mla_kv_norm, m_mla_w_kvb, m_mla_q_head_norm, m_mla_k_head_norm, m_mla_w_out, m_ffn_norm, m_ffn_w_in, m_ffn_conv_w, m_ffn_conv_b, m_ffn_w_out, v_ret_norm, v_ret_w_in, v_ret_gn, v_ret_w_out, v_mla_norm, v_mla_w_in, v_mla_q_norm, v_mla_w_qb, v_mla_kv_norm, v_mla_w_kvb, v_mla_q_head_norm, v_mla_k_head_norm, v_mla_w_out, v_ffn_norm, v_ffn_w_in, v_ffn_conv_w, v_ffn_conv_b, v_ffn_w_out):
    names = ["ret_norm", "ret_w_in", "ret_gn", "ret_w_out", "mla_norm", "mla_w_in", "mla_q_norm", "mla_w_qb",
             "mla_kv_norm", "mla_w_kvb", "mla_q_head_norm", "mla_k_head_norm", "mla_w_out", "ffn_norm", "ffn_w_in",
             "ffn_conv_w", "ffn_conv_b", "ffn_w_out"]
    shard = dict(zip(names, [ret_norm, ret_w_in, ret_gn, ret_w_out, mla_norm, mla_w_in, mla_q_norm, mla_w_qb,
                             mla_kv_norm, mla_w_kvb, mla_q_head_norm, mla_k_head_norm, mla_w_out, ffn_norm, ffn_w_in,
                             ffn_conv_w, ffn_conv_b, ffn_w_out]))
    mom_m = dict(zip(names, [m_ret_norm, m_ret_w_in, m_ret_gn, m_ret_w_out, m_mla_norm, m_mla_w_in, m_mla_q_norm,
                             m_mla_w_qb, m_mla_kv_norm, m_mla_w_kvb, m_mla_q_head_norm, m_mla_k_head_norm, m_mla_w_out,
                             m_ffn_norm, m_ffn_w_in, m_ffn_conv_w, m_ffn_conv_b, m_ffn_w_out]))
    mom_v = dict(zip(names, [v_ret_norm, v_ret_w_in, v_ret_gn, v_ret_w_out, v_mla_norm, v_mla_w_in, v_mla_q_norm,
                             v_mla_w_qb, v_mla_kv_norm, v_mla_w_kvb, v_mla_q_head_norm, v_mla_k_head_norm, v_mla_w_out,
                             v_ffn_norm, v_ffn_w_in, v_ffn_conv_w, v_ffn_conv_b, v_ffn_w_out]))
    B, S, D = x.shape
    T = B * S
    sx, sy = lax.axis_index("x"), lax.axis_index("y")
    me = 2 * sx + sy

    two_d = lambda a: a.reshape(-1, a.shape[-1])
    small_sizes = [int(np.prod(shard[n].shape)) for n, _ in _SMALL_SHARDED]
    small = jnp.concatenate([shard[n].reshape(1, -1) for n, _ in _SMALL_SHARDED], axis=1)
    small = _pad_rows(small, LANES, 8)[0]
    as_mxu = lambda a: two_d(a).astype(BF16)
    zero = jnp.zeros((), jnp.int32)
    with_own = lambda gathered, own: lax.dynamic_update_slice(gathered, own[None], (me.astype(jnp.int32), zero, zero))
    by_cols = lambda a: jnp.moveaxis(a, 0, 1).reshape(a.shape[1], -1)
    by_rows = lambda a: a.reshape(-1, a.shape[-1])
    mla_in_shard = jnp.pad(as_mxu(shard["mla_w_in"]), ((0, 0), (0, MLA_IN_PAD - MLA_IN)))
    mla_qb_shard = _pad_heads(as_mxu(shard["mla_w_qb"]), 1)
    ret_in_shard = as_mxu(shard["ret_w_in"])
    g_ret_in, gsmall = _all_gather_weights([ret_in_shard], small)
    later = [
        ("ret_out", [("ret_w_out", as_mxu(shard["ret_w_out"]), by_rows)]),
        ("ffn0", [("ffn_w_in0", as_mxu(shard["ffn_w_in"][0]), by_cols), ("ffn_w_out0", as_mxu(shard["ffn_w_out"][0]), by_rows)]),
        ("mla", [("mla_w_in", mla_in_shard, by_rows), ("mla_w_qb", mla_qb_shard, by_cols),
                 ("mla_w_kvb", as_mxu(shard["mla_w_kvb"]), by_cols), ("mla_w_out", as_mxu(shard["mla_w_out"]), by_rows)]),
        ("ffn1", [("ffn_w_in1", as_mxu(shard["ffn_w_in"][1]), by_cols), ("ffn_w_out1", as_mxu(shard["ffn_w_out"][1]), by_rows)]),
    ]
    gathering = {}
    token = gsmall
    for group, items in later:
        shards = [s_ for _, s_, _ in items]
        lands = [lax.empty((N_SHARD,) + s_.shape, s_.dtype) for s_ in shards]
        send_sems, recv_sems, shards, lands, token = _exchange_start(
            _weight_copies, shards, lands, 3 * len(shards), f"weights_start_{group}", after=token)
        gathering[group] = (send_sems, recv_sems, shards, lands, items)

    def late(group, after):
        send_sems, recv_sems, shards, lands, items = gathering[group]
        shards, lands = _exchange_wait(_weight_copies, send_sems, recv_sems, shards, lands, after,
                                       f"weights_wait_{group}")
        return {key: full(with_own(l_, s_)) for (key, _, full), s_, l_ in zip(items, shards, lands)}

    gsmall = with_own(gsmall, small).reshape(N_SHARD, -1)
    wfull = {}
    off = 0
    for (n, ax), sz in zip(_SMALL_SHARDED, small_sizes):
        wfull[n] = _from_slots(gsmall[:, off:off + sz], shard[n].shape, ax)
        off += sz
    for n in _SMALL_REPLICATED:
        wfull[n] = shard[n]

    conv8 = jnp.concatenate([wfull["ffn_conv_w"], wfull["ffn_conv_b"][:, None, :],
                             jnp.zeros((2, 4, FFN_DIM), F32)], axis=1)
    w = {
        "started": token, "ret_norm": wfull["ret_norm"], "ret_w_in": by_cols(with_own(g_ret_in, ret_in_shard)),
        "ret_gn": wfull["ret_gn"].reshape(1, RET_HEADS * RET_V), "mla_norm": wfull["mla_norm"],
        "mla_q_norm": wfull["mla_q_norm"], "mla_kv_norm": wfull["mla_kv_norm"],
        "mla_q_head_norm": jnp.pad(wfull["mla_q_head_norm"], ((0, 0), (0, MLA_PAD - MLA_QK))),
        "mla_k_head_norm": jnp.pad(wfull["mla_k_head_norm"], ((0, 0), (0, MLA_PAD - MLA_QK))),
        "ffn_norm": wfull["ffn_norm"], "ffn_conv8": conv8,
    }

    started = {}

    def exchange(group, arrays):
        lands = [lax.empty((N_PEERS, p.shape[1] // 2, p.shape[2]), p.dtype) for p in arrays]
        send_sems, recv_sems, ps, lands, token = _exchange_start(
            _grad_copies, arrays, lands, N_PEERS * len(arrays), f"grads_start_{group}")
        started[group] = (send_sems, recv_sems, ps, lands)
        return token

    small_shapes = {
        "ret_norm": (1, D_MODEL), "ret_gn": (1, RET_HEADS, RET_V), "mla_norm": (1, D_MODEL),
        "mla_q_norm": (1, MLA_Q_RANK), "mla_kv_norm": (1, MLA_KV_RANK), "mla_q_head_norm": (1, MLA_QK),
        "mla_k_head_norm": (1, MLA_QK), "ffn_norm": (2, D_MODEL), "ffn_conv_w": (2, 3, FFN_DIM),
        "ffn_conv_b": (2, FFN_DIM)}

    def reduce_small(gl, loss_part):
        gl = dict(gl, mla_q_head_norm=gl["mla_q_head_norm"][:, :MLA_QK], mla_k_head_norm=gl["mla_k_head_norm"][:, :MLA_QK])
        packed = jnp.concatenate([gl[n].reshape(1, -1) for n in _SMALL_ALL] + [loss_part.reshape(1, 1)], axis=1)
        return _all_reduce_small(_pad_rows(packed, LANES, 8)[0])

    _, dx, gsm = _local_step(x.reshape(T, D), loss_target.reshape(T, D), w, B, S, late, exchange, reduce_small)

    delta, new_m, new_v, grads = {}, {}, {}, {}

    def reduced(group, after):
        send_sems, recv_sems, ps, lands = started[group]
        ps, lands = _exchange_wait(_grad_copies, send_sems, recv_sems, ps, lands, after, f"grads_wait_{group}")
        halves = [_sum_partials(p_, l_, f"grads_sum_{group}_{i}") for i, (p_, l_) in enumerate(zip(ps, lands))]
        return [two_d(r) for r in _sibling_share(halves, f"grads_share_{group}")]

    def adamw(n, g_):
        shp = shard[n].shape
        grads[n] = g_.reshape(shp)
        flat = lambda a: a.reshape(-1, shp[-1])
        d_, m_, v_ = _adamw(flat(shard[n]), flat(grads[n]), flat(mom_m[n]), flat(mom_v[n]), f"adamw_{n}")
        delta[n], new_m[n], new_v[n] = d_.reshape(shp), m_.reshape(shp), v_.reshape(shp)
        return d_

    ffn1 = reduced("ffn1", started["ret"][2][0])
    mla = reduced("mla", ffn1[0])
    ffn0 = reduced("ffn0", mla[0])
    reto = reduced("reto", ffn0[0])
    early = [adamw(n, g_) for n, g_ in zip(["mla_w_in", "mla_w_qb", "mla_w_kvb", "mla_w_out"], mla)]
    early.append(adamw("ffn_w_in", jnp.stack([ffn0[0], ffn1[0]])))
    early.append(adamw("ffn_w_out", jnp.stack([ffn0[1], ffn1[1]])))
    early.append(adamw("ret_w_out", reto[0]))
    ret = reduced("ret", jnp.stack([d_[0, 0] for d_ in early]))
    adamw("ret_w_in", ret[0])

    gsm = gsm.reshape(-1)
    sharded_axis = dict(_SMALL_SHARDED)
    off = 0
    for n in _SMALL_ALL:
        sz = int(np.prod(small_shapes[n]))
        gn = gsm[off:off + sz].reshape(small_shapes[n])
        off += sz
        if n in sharded_axis:
            ax = sharded_axis[n]
            width = shard[n].shape[ax]
            gn = lax.dynamic_slice_in_dim(gn, me * width, width, axis=ax)
        grads[n] = gn
    loss = gsm[off]

    pack_small = lambda d: _pad_rows(jnp.concatenate([d[n].reshape(1, -1) for n in _SMALL_ALL], axis=1), LANES, 8)[0]
    d_, m_, v_ = _adamw(pack_small(shard), pack_small(grads), pack_small(mom_m), pack_small(mom_v), "adamw_small")
    off = 0
    for n in _SMALL_ALL:
        sz = int(np.prod(shard[n].shape))
        for dst, src in ((delta, d_), (new_m, m_), (new_v, v_)):
            dst[n] = src.reshape(-1)[off:off + sz].reshape(shard[n].shape)
        off += sz

    return (loss, dx.reshape(B, S, D), *[grads[n] for n in names], *[delta[n] for n in names],
            *[new_m[n] for n in names], *[new_v[n] for n in names])
```
